```python
import math
import jax, jax.numpy as jnp
from jax import lax
import numpy as np

D_MODEL = 1024
BATCH = 8
SEQ = 2048
DEPTH = 1

EPS = 1e-6
PLE_DIM = 256
S5_GROUP_CH = 16
S5_GROUPS = D_MODEL // 32
S5_WIDTH = S5_GROUPS * S5_GROUP_CH
S5_STATE = 64
LRU_HEAD_DIM = 64
LRU_WIDTH = D_MODEL
LRU_HEADS = LRU_WIDTH // LRU_HEAD_DIM
LRU_C = 8.0
CONV_WIDTH = 4
FFN_HIDDEN = -(-8 * D_MODEL // (3 * 256)) * 256
IN_COLS = S5_WIDTH + LRU_WIDTH + 2 * D_MODEL

kernel_name = "hybrid_s5_rglru_gated_block"


def rms_norm(x, g):
    xf = x.astype(jnp.float32)
    y = xf * lax.rsqrt(jnp.mean(xf * xf, axis=-1, keepdims=True) + EPS)
    return (y * g.astype(jnp.float32)).astype(x.dtype)


def s5_mixer(u, lam_re, lam_im, log_dt, b_re, b_im, c_re, c_im, d_skip, w_glu, b_glu):
    f32 = jnp.float32
    bsz, L, _ = u.shape
    uf = u.astype(f32).reshape(bsz, L, S5_GROUPS, S5_GROUP_CH)
    lr = lam_re.astype(f32)
    li = lam_im.astype(f32)
    dt = jnp.exp(log_dt.astype(f32))[:, None]
    mag = jnp.exp(lr * dt)
    ar = mag * jnp.cos(li * dt)
    ai = mag * jnp.sin(li * dt)
    den = lr * lr + li * li
    nr = ar - 1.0
    fr = (nr * lr + ai * li) / den
    fi = (ai * lr - nr * li) / den
    br = b_re.astype(f32)
    bi = b_im.astype(f32)
    bbr = fr[..., None] * br - fi[..., None] * bi
    bbi = fr[..., None] * bi + fi[..., None] * br
    xr = jnp.einsum('blgp,gnp->blgn', uf, bbr)
    xi = jnp.einsum('blgp,gnp->blgn', uf, bbi)
    a_r = jnp.broadcast_to(ar, (1, L) + ar.shape)
    a_i = jnp.broadcast_to(ai, (1, L) + ai.shape)

    def combine(e1, e2):
        a1r, a1i, b1r, b1i = e1
        a2r, a2i, b2r, b2i = e2
        return (a2r * a1r - a2i * a1i,
                a2r * a1i + a2i * a1r,
                a2r * b1r - a2i * b1i + b2r,
                a2r * b1i + a2i * b1r + b2i)

    _, _, sr, si = lax.associative_scan(combine, (a_r, a_i, xr, xi), axis=1)
    y = (jnp.einsum('blgn,gpn->blgp', sr, c_re.astype(f32))
         - jnp.einsum('blgn,gpn->blgp', si, c_im.astype(f32))
         + d_skip.astype(f32) * uf)
    y = y.reshape(bsz, L, S5_WIDTH)
    z = jax.nn.gelu(y)
    out = z * jax.nn.sigmoid(z @ w_glu.astype(f32) + b_glu.astype(f32))
    return out.astype(u.dtype)


def rglru_mixer(u, conv_w, conv_b, w_r, b_r, w_i, b_i, lru_lambda):
    f32 = jnp.float32
    xc = lax.conv_general_dilated(
        u, conv_w[:, None, :].astype(u.dtype), window_strides=(1,),
        padding=[(CONV_WIDTH - 1, 0)], dimension_numbers=('NWC', 'WIO', 'NWC'),
        feature_group_count=LRU_WIDTH) + conv_b
    bsz, L, _ = xc.shape
    xh = xc.astype(f32).reshape(bsz, L, LRU_HEADS, LRU_HEAD_DIM)
    r = jax.nn.sigmoid(jnp.einsum('blhi,hij->blhj', xh, w_r.astype(f32)) + b_r.astype(f32))
    ig = jax.nn.sigmoid(jnp.einsum('blhi,hij->blhj', xh, w_i.astype(f32)) + b_i.astype(f32))
    log_a = -LRU_C * r * jax.nn.softplus(-lru_lambda.astype(f32).reshape(LRU_HEADS, LRU_HEAD_DIM))
    a = jnp.exp(log_a)
    mult = jnp.sqrt(-jnp.expm1(2.0 * log_a))
    bx = mult * ig * xh

    def combine(e1, e2):
        a1, b1 = e1
        a2, b2 = e2
        return a2 * a1, a2 * b1 + b2

    _, h = lax.associative_scan(combine, (a, bx), axis=1)
    return h.reshape(bsz, L, LRU_WIDTH).astype(u.dtype)


def _fwd_setup_inputs(seed: int = 0) -> dict:
    key = jax.random.key(seed)
    ks = jax.random.split(key, 40)
    f32 = jnp.float32

    def nrm(k, shape, scale):
        return jax.random.normal(k, shape, f32) * scale

    def gain(k, shape):
        return 1.0 + 0.01 * jax.random.normal(k, shape, f32)

    G, N, P = S5_GROUPS, S5_STATE, S5_GROUP_CH
    lam_re = -0.5 + 0.01 * jax.random.normal(ks[3], (DEPTH, G, N), f32)
    lam_im = jnp.pi * jnp.arange(N, dtype=f32)[None, None, :] + 0.01 * jax.random.normal(ks[4], (DEPTH, G, N), f32)
    log_dt = jax.random.uniform(ks[5], (DEPTH, G), f32, math.log(1e-3), math.log(1e-1))
    u_a = jax.random.uniform(ks[16], (DEPTH, LRU_WIDTH), f32, 0.9, 0.999)
    a_base = u_a ** (1.0 / LRU_C)
    lru_lambda = jnp.log(a_base) - jnp.log1p(-a_base)

    return {
        "x": nrm(ks[0], (BATCH, SEQ, D_MODEL), 1.0),
        "p": nrm(ks[1], (DEPTH, BATCH, SEQ, PLE_DIM), 1.0),
        "g_mix": gain(ks[2], (DEPTH, D_MODEL)),
        "w_in": nrm(ks[6], (DEPTH, D_MODEL, IN_COLS), D_MODEL ** -0.5),
        "b_in": nrm(ks[7], (DEPTH, IN_COLS), 0.01),
        "lam_re": lam_re,
        "lam_im": lam_im,
        "log_dt": log_dt,
        "s5_b_re": nrm(ks[8], (DEPTH, G, N, P), (2 * P) ** -0.5),
        "s5_b_im": nrm(ks[9], (DEPTH, G, N, P), (2 * P) ** -0.5),
        "s5_c_re": nrm(ks[10], (DEPTH, G, P, N), N ** -0.5),
        "s5_c_im": nrm(ks[11], (DEPTH, G, P, N), N ** -0.5),
        "s5_d": nrm(ks[12], (DEPTH, G, P), 1.0),
        "w_glu": nrm(ks[13], (DEPTH, S5_WIDTH, S5_WIDTH), S5_WIDTH ** -0.5),
        "b_glu": nrm(ks[14], (DEPTH, S5_WIDTH), 0.01),
        "conv_w": nrm(ks[15], (DEPTH, CONV_WIDTH, LRU_WIDTH), CONV_WIDTH ** -0.5),
        "conv_b": nrm(ks[17], (DEPTH, LRU_WIDTH), 0.01),
        "w_r": nrm(ks[18], (DEPTH, LRU_HEADS, LRU_HEAD_DIM, LRU_HEAD_DIM), LRU_HEAD_DIM ** -0.5),
        "b_r": nrm(ks[19], (DEPTH, LRU_HEADS, LRU_HEAD_DIM), 0.01),
        "w_i": nrm(ks[20], (DEPTH, LRU_HEADS, LRU_HEAD_DIM, LRU_HEAD_DIM), LRU_HEAD_DIM ** -0.5),
        "b_i": nrm(ks[21], (DEPTH, LRU_HEADS, LRU_HEAD_DIM), 0.01),
        "lru_lambda": lru_lambda,
        "w_a_out": nrm(ks[22], (DEPTH, S5_WIDTH, D_MODEL), S5_WIDTH ** -0.5),
        "w_b_out": nrm(ks[23], (DEPTH, LRU_WIDTH, D_MODEL), LRU_WIDTH ** -0.5),
        "w_o": nrm(ks[24], (DEPTH, D_MODEL, D_MODEL), D_MODEL ** -0.5),
        "g_ffn": gain(ks[25], (DEPTH, D_MODEL)),
        "w_ffn_gate": nrm(ks[26], (DEPTH, D_MODEL, FFN_HIDDEN), D_MODEL ** -0.5),
        "w_ffn_up": nrm(ks[27], (DEPTH, D_MODEL, FFN_HIDDEN), D_MODEL ** -0.5),
        "w_ffn_down": nrm(ks[28], (DEPTH, FFN_HIDDEN, D_MODEL), FFN_HIDDEN ** -0.5),
        "g_ple_gate": gain(ks[29], (DEPTH, D_MODEL)),
        "w_ple_gate": nrm(ks[30], (DEPTH, D_MODEL, D_MODEL), D_MODEL ** -0.5),
        "b_ple_gate": nrm(ks[31], (DEPTH, D_MODEL), 0.01),
        "w_ple": nrm(ks[32], (DEPTH, PLE_DIM, D_MODEL), PLE_DIM ** -0.5),
        "g_ple": gain(ks[33], (DEPTH, D_MODEL)),
        "g_final": gain(ks[34], (D_MODEL,)),
    }


def _fwd_reference(x, p, g_mix, w_in, b_in, lam_re, lam_im, log_dt, s5_b_re, s5_b_im,
              s5_c_re, s5_c_im, s5_d, w_glu, b_glu, conv_w, conv_b, w_r, b_r, w_i, b_i,
              lru_lambda, w_a_out, w_b_out, w_o, g_ffn, w_ffn_gate, w_ffn_up, w_ffn_down,
              g_ple_gate, w_ple_gate, b_ple_gate, w_ple, g_ple, g_final):
    s_a = S5_WIDTH
    s_b = S5_WIDTH + LRU_WIDTH
    s_g = s_b + D_MODEL
    for i in range(DEPTH):
        h = rms_norm(x, g_mix[i])
        z = h @ w_in[i] + b_in[i]
        u_a = z[..., :s_a]
        u_b = z[..., s_a:s_b]
        gate_a = jax.nn.sigmoid(z[..., s_b:s_g])
        gate_b = jax.nn.sigmoid(z[..., s_g:])
        y_a = s5_mixer(u_a, lam_re[i], lam_im[i], log_dt[i], s5_b_re[i], s5_b_im[i],
                       s5_c_re[i], s5_c_im[i], s5_d[i], w_glu[i], b_glu[i])
        y_b = rglru_mixer(u_b, conv_w[i], conv_b[i], w_r[i], b_r[i], w_i[i], b_i[i],
                          lru_lambda[i])
        merged = gate_a * (y_a @ w_a_out[i]) + gate_b * (y_b @ w_b_out[i])
        x = x + merged @ w_o[i]
        h2 = rms_norm(x, g_ffn[i])
        x = x + (jax.nn.silu(h2 @ w_ffn_gate[i]) * (h2 @ w_ffn_up[i])) @ w_ffn_down[i]
        gate_p = jax.nn.sigmoid(rms_norm(x, g_ple_gate[i]) @ w_ple_gate[i] + b_ple_gate[i])
        e = rms_norm(p[i] @ w_ple[i], g_ple[i])
        x = x + gate_p * e
    return rms_norm(x, g_final)


import jax as _jax
import jax.numpy as _jnp

TWIN_FORMAT = 'train_step'
FWD_PARAMS = ['x', 'p', 'g_mix', 'w_in', 'b_in', 'lam_re', 'lam_im', 'log_dt', 's5_b_re', 's5_b_im', 's5_c_re', 's5_c_im', 's5_d', 'w_glu', 'b_glu', 'conv_w', 'conv_b', 'w_r', 'b_r', 'w_i', 'b_i', 'lru_lambda', 'w_a_out', 'w_b_out', 'w_o', 'g_ffn', 'w_ffn_gate', 'w_ffn_up', 'w_ffn_down', 'g_ple_gate', 'w_ple_gate', 'b_ple_gate', 'w_ple', 'g_ple', 'g_final']
TWIN_WEIGHTS = ['g_mix', 'w_in', 'b_in', 'lam_re', 'lam_im', 'log_dt', 's5_b_re', 's5_b_im', 's5_c_re', 's5_c_im', 's5_d', 'w_glu', 'b_glu', 'conv_w', 'conv_b', 'w_r', 'b_r', 'w_i', 'b_i', 'lru_lambda', 'w_a_out', 'w_b_out', 'w_o', 'g_ffn', 'w_ffn_gate', 'w_ffn_up', 'w_ffn_down', 'g_ple_gate', 'w_ple_gate', 'b_ple_gate', 'w_ple', 'g_ple', 'g_final']
TWIN_DIFF_INPUT = 'x'
TWIN_INPUTS = ['x', 'p', 'g_mix', 'w_in', 'b_in', 'lam_re', 'lam_im', 'log_dt', 's5_b_re', 's5_b_im', 's5_c_re', 's5_c_im', 's5_d', 'w_glu', 'b_glu', 'conv_w', 'conv_b', 'w_r', 'b_r', 'w_i', 'b_i', 'lru_lambda', 'w_a_out', 'w_b_out', 'w_o', 'g_ffn', 'w_ffn_gate', 'w_ffn_up', 'w_ffn_down', 'g_ple_gate', 'w_ple_gate', 'b_ple_gate', 'w_ple', 'g_ple', 'g_final', 'loss_target', 'm_g_mix', 'm_w_in', 'm_b_in', 'm_lam_re', 'm_lam_im', 'm_log_dt', 'm_s5_b_re', 'm_s5_b_im', 'm_s5_c_re', 'm_s5_c_im', 'm_s5_d', 'm_w_glu', 'm_b_glu', 'm_conv_w', 'm_conv_b', 'm_w_r', 'm_b_r', 'm_w_i', 'm_b_i', 'm_lru_lambda', 'm_w_a_out', 'm_w_b_out', 'm_w_o', 'm_g_ffn', 'm_w_ffn_gate', 'm_w_ffn_up', 'm_w_ffn_down', 'm_g_ple_gate', 'm_w_ple_gate', 'm_b_ple_gate', 'm_w_ple', 'm_g_ple', 'm_g_final', 'v_g_mix', 'v_w_in', 'v_b_in', 'v_lam_re', 'v_lam_im', 'v_log_dt', 'v_s5_b_re', 'v_s5_b_im', 'v_s5_c_re', 'v_s5_c_im', 'v_s5_d', 'v_w_glu', 'v_b_glu', 'v_conv_w', 'v_conv_b', 'v_w_r', 'v_b_r', 'v_w_i', 'v_b_i', 'v_lru_lambda', 'v_w_a_out', 'v_w_b_out', 'v_w_o', 'v_g_ffn', 'v_w_ffn_gate', 'v_w_ffn_up', 'v_w_ffn_down', 'v_g_ple_gate', 'v_w_ple_gate', 'v_b_ple_gate', 'v_w_ple', 'v_g_ple', 'v_g_final']
TWIN_OUTPUTS = ['loss', 'grad_x', 'grad_g_mix', 'grad_w_in', 'grad_b_in', 'grad_lam_re', 'grad_lam_im', 'grad_log_dt', 'grad_s5_b_re', 'grad_s5_b_im', 'grad_s5_c_re', 'grad_s5_c_im', 'grad_s5_d', 'grad_w_glu', 'grad_b_glu', 'grad_conv_w', 'grad_conv_b', 'grad_w_r', 'grad_b_r', 'grad_w_i', 'grad_b_i', 'grad_lru_lambda', 'grad_w_a_out', 'grad_w_b_out', 'grad_w_o', 'grad_g_ffn', 'grad_w_ffn_gate', 'grad_w_ffn_up', 'grad_w_ffn_down', 'grad_g_ple_gate', 'grad_w_ple_gate', 'grad_b_ple_gate', 'grad_w_ple', 'grad_g_ple', 'grad_g_final', 'delta_g_mix', 'delta_w_in', 'delta_b_in', 'delta_lam_re', 'delta_lam_im', 'delta_log_dt', 'delta_s5_b_re', 'delta_s5_b_im', 'delta_s5_c_re', 'delta_s5_c_im', 'delta_s5_d', 'delta_w_glu', 'delta_b_glu', 'delta_conv_w', 'delta_conv_b', 'delta_w_r', 'delta_b_r', 'delta_w_i', 'delta_b_i', 'delta_lru_lambda', 'delta_w_a_out', 'delta_w_b_out', 'delta_w_o', 'delta_g_ffn', 'delta_w_ffn_gate', 'delta_w_ffn_up', 'delta_w_ffn_down', 'delta_g_ple_gate', 'delta_w_ple_gate', 'delta_b_ple_gate', 'delta_w_ple', 'delta_g_ple', 'delta_g_final', 'new_m_g_mix', 'new_m_w_in', 'new_m_b_in', 'new_m_lam_re', 'new_m_lam_im', 'new_m_log_dt', 'new_m_s5_b_re', 'new_m_s5_b_im', 'new_m_s5_c_re', 'new_m_s5_c_im', 'new_m_s5_d', 'new_m_w_glu', 'new_m_b_glu', 'new_m_conv_w', 'new_m_conv_b', 'new_m_w_r', 'new_m_b_r', 'new_m_w_i', 'new_m_b_i', 'new_m_lru_lambda', 'new_m_w_a_out', 'new_m_w_b_out', 'new_m_w_o', 'new_m_g_ffn', 'new_m_w_ffn_gate', 'new_m_w_ffn_up', 'new_m_w_ffn_down', 'new_m_g_ple_gate', 'new_m_w_ple_gate', 'new_m_b_ple_gate', 'new_m_w_ple', 'new_m_g_ple', 'new_m_g_final', 'new_v_g_mix', 'new_v_w_in', 'new_v_b_in', 'new_v_lam_re', 'new_v_lam_im', 'new_v_log_dt', 'new_v_s5_b_re', 'new_v_s5_b_im', 'new_v_s5_c_re', 'new_v_s5_c_im', 'new_v_s5_d', 'new_v_w_glu', 'new_v_b_glu', 'new_v_conv_w', 'new_v_conv_b', 'new_v_w_r', 'new_v_b_r', 'new_v_w_i', 'new_v_b_i', 'new_v_lru_lambda', 'new_v_w_a_out', 'new_v_w_b_out', 'new_v_w_o', 'new_v_g_ffn', 'new_v_w_ffn_gate', 'new_v_w_ffn_up', 'new_v_w_ffn_down', 'new_v_g_ple_gate', 'new_v_w_ple_gate', 'new_v_b_ple_gate', 'new_v_w_ple', 'new_v_g_ple', 'new_v_g_final']
TWIN_LEAF_KINDS = {'loss': 'loss', 'grad_x': 'grad_x', 'grad_g_mix': 'grad_w', 'grad_w_in': 'grad_w', 'grad_b_in': 'grad_w', 'grad_lam_re': 'grad_w', 'grad_lam_im': 'grad_w', 'grad_log_dt': 'grad_w', 'grad_s5_b_re': 'grad_w', 'grad_s5_b_im': 'grad_w', 'grad_s5_c_re': 'grad_w', 'grad_s5_c_im': 'grad_w', 'grad_s5_d': 'grad_w', 'grad_w_glu': 'grad_w', 'grad_b_glu': 'grad_w', 'grad_conv_w': 'grad_w', 'grad_conv_b': 'grad_w', 'grad_w_r': 'grad_w', 'grad_b_r': 'grad_w', 'grad_w_i': 'grad_w', 'grad_b_i': 'grad_w', 'grad_lru_lambda': 'grad_w', 'grad_w_a_out': 'grad_w', 'grad_w_b_out': 'grad_w', 'grad_w_o': 'grad_w', 'grad_g_ffn': 'grad_w', 'grad_w_ffn_gate': 'grad_w', 'grad_w_ffn_up': 'grad_w', 'grad_w_ffn_down': 'grad_w', 'grad_g_ple_gate': 'grad_w', 'grad_w_ple_gate': 'grad_w', 'grad_b_ple_gate': 'grad_w', 'grad_w_ple': 'grad_w', 'grad_g_ple': 'grad_w', 'grad_g_final': 'grad_w', 'delta_g_mix': 'delta_w', 'delta_w_in': 'delta_w', 'delta_b_in': 'delta_w', 'delta_lam_re': 'delta_w', 'delta_lam_im': 'delta_w', 'delta_log_dt': 'delta_w', 'delta_s5_b_re': 'delta_w', 'delta_s5_b_im': 'delta_w', 'delta_s5_c_re': 'delta_w', 'delta_s5_c_im': 'delta_w', 'delta_s5_d': 'delta_w', 'delta_w_glu': 'delta_w', 'delta_b_glu': 'delta_w', 'delta_conv_w': 'delta_w', 'delta_conv_b': 'delta_w', 'delta_w_r': 'delta_w', 'delta_b_r': 'delta_w', 'delta_w_i': 'delta_w', 'delta_b_i': 'delta_w', 'delta_lru_lambda': 'delta_w', 'delta_w_a_out': 'delta_w', 'delta_w_b_out': 'delta_w', 'delta_w_o': 'delta_w', 'delta_g_ffn': 'delta_w', 'delta_w_ffn_gate': 'delta_w', 'delta_w_ffn_up': 'delta_w', 'delta_w_ffn_down': 'delta_w', 'delta_g_ple_gate': 'delta_w', 'delta_w_ple_gate': 'delta_w', 'delta_b_ple_gate': 'delta_w', 'delta_w_ple': 'delta_w', 'delta_g_ple': 'delta_w', 'delta_g_final': 'delta_w', 'new_m_g_mix': 'new_m', 'new_m_w_in': 'new_m', 'new_m_b_in': 'new_m', 'new_m_lam_re': 'new_m', 'new_m_lam_im': 'new_m', 'new_m_log_dt': 'new_m', 'new_m_s5_b_re': 'new_m', 'new_m_s5_b_im': 'new_m', 'new_m_s5_c_re': 'new_m', 'new_m_s5_c_im': 'new_m', 'new_m_s5_d': 'new_m', 'new_m_w_glu': 'new_m', 'new_m_b_glu': 'new_m', 'new_m_conv_w': 'new_m', 'new_m_conv_b': 'new_m', 'new_m_w_r': 'new_m', 'new_m_b_r': 'new_m', 'new_m_w_i': 'new_m', 'new_m_b_i': 'new_m', 'new_m_lru_lambda': 'new_m', 'new_m_w_a_out': 'new_m', 'new_m_w_b_out': 'new_m', 'new_m_w_o': 'new_m', 'new_m_g_ffn': 'new_m', 'new_m_w_ffn_gate': 'new_m', 'new_m_w_ffn_up': 'new_m', 'new_m_w_ffn_down': 'new_m', 'new_m_g_ple_gate': 'new_m', 'new_m_w_ple_gate': 'new_m', 'new_m_b_ple_gate': 'new_m', 'new_m_w_ple': 'new_m', 'new_m_g_ple': 'new_m', 'new_m_g_final': 'new_m', 'new_v_g_mix': 'new_v', 'new_v_w_in': 'new_v', 'new_v_b_in': 'new_v', 'new_v_lam_re': 'new_v', 'new_v_lam_im': 'new_v', 'new_v_log_dt': 'new_v', 'new_v_s5_b_re': 'new_v', 'new_v_s5_b_im': 'new_v', 'new_v_s5_c_re': 'new_v', 'new_v_s5_c_im': 'new_v', 'new_v_s5_d': 'new_v', 'new_v_w_glu': 'new_v', 'new_v_b_glu': 'new_v', 'new_v_conv_w': 'new_v', 'new_v_conv_b': 'new_v', 'new_v_w_r': 'new_v', 'new_v_b_r': 'new_v', 'new_v_w_i': 'new_v', 'new_v_b_i': 'new_v', 'new_v_lru_lambda': 'new_v', 'new_v_w_a_out': 'new_v', 'new_v_w_b_out': 'new_v', 'new_v_w_o': 'new_v', 'new_v_g_ffn': 'new_v', 'new_v_w_ffn_gate': 'new_v', 'new_v_w_ffn_up': 'new_v', 'new_v_w_ffn_down': 'new_v', 'new_v_g_ple_gate': 'new_v', 'new_v_w_ple_gate': 'new_v', 'new_v_b_ple_gate': 'new_v', 'new_v_w_ple': 'new_v', 'new_v_g_ple': 'new_v', 'new_v_g_final': 'new_v'}


def _forward(args):
    return _fwd_reference(*[args[k] for k in FWD_PARAMS])


def _output_shape():
    out = _jax.eval_shape(lambda: _forward(_fwd_setup_inputs(0)))
    return out.shape, out.dtype

N_MICROBATCH = 1
ADAM_LR = 0.001
ADAM_B1 = 0.9
ADAM_B2 = 0.999
ADAM_EPS = 1e-08
ADAM_WD = 0.01
ADAM_STEP = 10
PER_EXAMPLE_BATCH_AXIS = {'x': 0, 'p': 1, 'loss_target': 0}
SHARED_INPUTS = []
_WEIGHT_DTYPES = {'g_mix': _jnp.float32, 'w_in': _jnp.float32, 'b_in': _jnp.float32, 'lam_re': _jnp.float32, 'lam_im': _jnp.float32, 'log_dt': _jnp.float32, 's5_b_re': _jnp.float32, 's5_b_im': _jnp.float32, 's5_c_re': _jnp.float32, 's5_c_im': _jnp.float32, 's5_d': _jnp.float32, 'w_glu': _jnp.float32, 'b_glu': _jnp.float32, 'conv_w': _jnp.float32, 'conv_b': _jnp.float32, 'w_r': _jnp.float32, 'b_r': _jnp.float32, 'w_i': _jnp.float32, 'b_i': _jnp.float32, 'lru_lambda': _jnp.float32, 'w_a_out': _jnp.float32, 'w_b_out': _jnp.float32, 'w_o': _jnp.float32, 'g_ffn': _jnp.float32, 'w_ffn_gate': _jnp.float32, 'w_ffn_up': _jnp.float32, 'w_ffn_down': _jnp.float32, 'g_ple_gate': _jnp.float32, 'w_ple_gate': _jnp.float32, 'b_ple_gate': _jnp.float32, 'w_ple': _jnp.float32, 'g_ple': _jnp.float32, 'g_final': _jnp.float32}
MOMENT_SCALE = {'g_mix': 5.309134e-02, 'w_in': 2.724340e-02, 'b_in': 2.482677e-01, 'lam_re': 2.563027e-03, 'lam_im': 2.499794e-03, 'log_dt': 2.376857e+00, 's5_b_re': 1.549400e-03, 's5_b_im': 1.578940e-03, 's5_c_re': 2.326306e-03, 's5_c_im': 2.311685e-03, 's5_d': 3.731297e-02, 'w_glu': 9.447616e-03, 'b_glu': 1.460212e-02, 'conv_w': 4.497382e-02, 'conv_b': 4.483759e-01, 'w_r': 1.706180e-02, 'b_r': 1.144853e-02, 'w_i': 3.096564e-02, 'b_i': 1.463651e-02, 'lru_lambda': 2.043537e-02, 'w_a_out': 2.292843e-02, 'w_b_out': 4.959520e-02, 'w_o': 5.424451e-02, 'g_ffn': 8.363985e-02, 'w_ffn_gate': 3.531220e-02, 'w_ffn_up': 3.425550e-02, 'w_ffn_down': 5.686946e-02, 'g_ple_gate': 1.986730e-02, 'w_ple_gate': 1.986352e-02, 'b_ple_gate': 2.136963e-02, 'w_ple': 5.105287e-02, 'g_ple': 5.540657e-02, 'g_final': 1.600078e+01}


def _to_microbatches(a, axis):
    t = _jnp.moveaxis(a, axis, 0)
    t = t.reshape((N_MICROBATCH, t.shape[0] // N_MICROBATCH) + t.shape[1:])
    return _jnp.moveaxis(t, 1, axis + 1)


def setup_inputs(seed: int = 0) -> dict:
    inp = _fwd_setup_inputs(seed)
    key = _jax.random.fold_in(_jax.random.key(seed), 7919)
    shape, _ = _output_shape()
    out = dict(inp)
    out["loss_target"] = _jax.random.normal(_jax.random.fold_in(key, 0), shape, _jnp.float32)
    for i, name in enumerate(TWIN_WEIGHTS):
        w = inp[name].astype(_jnp.float32)
        if MOMENT_SCALE is None:
            s = _jnp.sqrt(_jnp.mean(_jnp.square(w)) + 1e-30)
        else:
            s = MOMENT_SCALE[name]
        km, kv = _jax.random.split(_jax.random.fold_in(key, i + 1))
        out[name] = w
        out["m_" + name] = s * _jax.random.normal(km, w.shape, _jnp.float32)
        out["v_" + name] = (s * s) * _jax.random.uniform(kv, w.shape, _jnp.float32, 0.5, 1.5)
    if N_MICROBATCH > 1:
        for name, axis in PER_EXAMPLE_BATCH_AXIS.items():
            out[name] = _to_microbatches(out[name], axis)
    return {'x': out['x'], 'p': out['p'], 'g_mix': out['g_mix'], 'w_in': out['w_in'], 'b_in': out['b_in'], 'lam_re': out['lam_re'], 'lam_im': out['lam_im'], 'log_dt': out['log_dt'], 's5_b_re': out['s5_b_re'], 's5_b_im': out['s5_b_im'], 's5_c_re': out['s5_c_re'], 's5_c_im': out['s5_c_im'], 's5_d': out['s5_d'], 'w_glu': out['w_glu'], 'b_glu': out['b_glu'], 'conv_w': out['conv_w'], 'conv_b': out['conv_b'], 'w_r': out['w_r'], 'b_r': out['b_r'], 'w_i': out['w_i'], 'b_i': out['b_i'], 'lru_lambda': out['lru_lambda'], 'w_a_out': out['w_a_out'], 'w_b_out': out['w_b_out'], 'w_o': out['w_o'], 'g_ffn': out['g_ffn'], 'w_ffn_gate': out['w_ffn_gate'], 'w_ffn_up': out['w_ffn_up'], 'w_ffn_down': out['w_ffn_down'], 'g_ple_gate': out['g_ple_gate'], 'w_ple_gate': out['w_ple_gate'], 'b_ple_gate': out['b_ple_gate'], 'w_ple': out['w_ple'], 'g_ple': out['g_ple'], 'g_final': out['g_final'], 'loss_target': out['loss_target'], 'm_g_mix': out['m_g_mix'], 'm_w_in': out['m_w_in'], 'm_b_in': out['m_b_in'], 'm_lam_re': out['m_lam_re'], 'm_lam_im': out['m_lam_im'], 'm_log_dt': out['m_log_dt'], 'm_s5_b_re': out['m_s5_b_re'], 'm_s5_b_im': out['m_s5_b_im'], 'm_s5_c_re': out['m_s5_c_re'], 'm_s5_c_im': out['m_s5_c_im'], 'm_s5_d': out['m_s5_d'], 'm_w_glu': out['m_w_glu'], 'm_b_glu': out['m_b_glu'], 'm_conv_w': out['m_conv_w'], 'm_conv_b': out['m_conv_b'], 'm_w_r': out['m_w_r'], 'm_b_r': out['m_b_r'], 'm_w_i': out['m_w_i'], 'm_b_i': out['m_b_i'], 'm_lru_lambda': out['m_lru_lambda'], 'm_w_a_out': out['m_w_a_out'], 'm_w_b_out': out['m_w_b_out'], 'm_w_o': out['m_w_o'], 'm_g_ffn': out['m_g_ffn'], 'm_w_ffn_gate': out['m_w_ffn_gate'], 'm_w_ffn_up': out['m_w_ffn_up'], 'm_w_ffn_down': out['m_w_ffn_down'], 'm_g_ple_gate': out['m_g_ple_gate'], 'm_w_ple_gate': out['m_w_ple_gate'], 'm_b_ple_gate': out['m_b_ple_gate'], 'm_w_ple': out['m_w_ple'], 'm_g_ple': out['m_g_ple'], 'm_g_final': out['m_g_final'], 'v_g_mix': out['v_g_mix'], 'v_w_in': out['v_w_in'], 'v_b_in': out['v_b_in'], 'v_lam_re': out['v_lam_re'], 'v_lam_im': out['v_lam_im'], 'v_log_dt': out['v_log_dt'], 'v_s5_b_re': out['v_s5_b_re'], 'v_s5_b_im': out['v_s5_b_im'], 'v_s5_c_re': out['v_s5_c_re'], 'v_s5_c_im': out['v_s5_c_im'], 'v_s5_d': out['v_s5_d'], 'v_w_glu': out['v_w_glu'], 'v_b_glu': out['v_b_glu'], 'v_conv_w': out['v_conv_w'], 'v_conv_b': out['v_conv_b'], 'v_w_r': out['v_w_r'], 'v_b_r': out['v_b_r'], 'v_w_i': out['v_w_i'], 'v_b_i': out['v_b_i'], 'v_lru_lambda': out['v_lru_lambda'], 'v_w_a_out': out['v_w_a_out'], 'v_w_b_out': out['v_w_b_out'], 'v_w_o': out['v_w_o'], 'v_g_ffn': out['v_g_ffn'], 'v_w_ffn_gate': out['v_w_ffn_gate'], 'v_w_ffn_up': out['v_w_ffn_up'], 'v_w_ffn_down': out['v_w_ffn_down'], 'v_g_ple_gate': out['v_g_ple_gate'], 'v_w_ple_gate': out['v_w_ple_gate'], 'v_b_ple_gate': out['v_b_ple_gate'], 'v_w_ple': out['v_w_ple'], 'v_g_ple': out['v_g_ple'], 'v_g_final': out['v_g_final']}


def _loss(weights, diff, rest, loss_target):
    with _jax.named_scope("forward"):
        args = {**rest, TWIN_DIFF_INPUT: diff, **{k: w.astype(_WEIGHT_DTYPES[k]) for k, w in weights.items()}}
        y = _forward(args)
    with _jax.named_scope("loss_head"):
        err = _jnp.square(y.astype(_jnp.float32) - loss_target)
        return 0.5 * _jnp.sum(_jnp.mean(err, axis=-1)) if err.ndim else 0.5 * err


def _adamw(w, g, m, v):
    m = ADAM_B1 * m + (1.0 - ADAM_B1) * g
    v = ADAM_B2 * v + (1.0 - ADAM_B2) * _jnp.square(g)
    m_hat = m / (1.0 - ADAM_B1 ** ADAM_STEP)
    v_hat = v / (1.0 - ADAM_B2 ** ADAM_STEP)
    delta = -ADAM_LR * (m_hat / (_jnp.sqrt(v_hat) + ADAM_EPS) + ADAM_WD * w)
    return delta, m, v


def reference(x, p, g_mix, w_in, b_in, lam_re, lam_im, log_dt, s5_b_re, s5_b_im, s5_c_re, s5_c_im, s5_d, w_glu, b_glu, conv_w, conv_b, w_r, b_r, w_i, b_i, lru_lambda, w_a_out, w_b_out, w_o, g_ffn, w_ffn_gate, w_ffn_up, w_ffn_down, g_ple_gate, w_ple_gate, b_ple_gate, w_ple, g_ple, g_final, loss_target, m_g_mix, m_w_in, m_b_in, m_lam_re, m_lam_im, m_log_dt, m_s5_b_re, m_s5_b_im, m_s5_c_re, m_s5_c_im, m_s5_d, m_w_glu, m_b_glu, m_conv_w, m_conv_b, m_w_r, m_b_r, m_w_i, m_b_i, m_lru_lambda, m_w_a_out, m_w_b_out, m_w_o, m_g_ffn, m_w_ffn_gate, m_w_ffn_up, m_w_ffn_down, m_g_ple_gate, m_w_ple_gate, m_b_ple_gate, m_w_ple, m_g_ple, m_g_final, v_g_mix, v_w_in, v_b_in, v_lam_re, v_lam_im, v_log_dt, v_s5_b_re, v_s5_b_im, v_s5_c_re, v_s5_c_im, v_s5_d, v_w_glu, v_b_glu, v_conv_w, v_conv_b, v_w_r, v_b_r, v_w_i, v_b_i, v_lru_lambda, v_w_a_out, v_w_b_out, v_w_o, v_g_ffn, v_w_ffn_gate, v_w_ffn_up, v_w_ffn_down, v_g_ple_gate, v_w_ple_gate, v_b_ple_gate, v_w_ple, v_g_ple, v_g_final):
    given = dict(x=x, p=p, g_mix=g_mix, w_in=w_in, b_in=b_in, lam_re=lam_re, lam_im=lam_im, log_dt=log_dt, s5_b_re=s5_b_re, s5_b_im=s5_b_im, s5_c_re=s5_c_re, s5_c_im=s5_c_im, s5_d=s5_d, w_glu=w_glu, b_glu=b_glu, conv_w=conv_w, conv_b=conv_b, w_r=w_r, b_r=b_r, w_i=w_i, b_i=b_i, lru_lambda=lru_lambda, w_a_out=w_a_out, w_b_out=w_b_out, w_o=w_o, g_ffn=g_ffn, w_ffn_gate=w_ffn_gate, w_ffn_up=w_ffn_up, w_ffn_down=w_ffn_down, g_ple_gate=g_ple_gate, w_ple_gate=w_ple_gate, b_ple_gate=b_ple_gate, w_ple=w_ple, g_ple=g_ple, g_final=g_final, loss_target=loss_target, m_g_mix=m_g_mix, m_w_in=m_w_in, m_b_in=m_b_in, m_lam_re=m_lam_re, m_lam_im=m_lam_im, m_log_dt=m_log_dt, m_s5_b_re=m_s5_b_re, m_s5_b_im=m_s5_b_im, m_s5_c_re=m_s5_c_re, m_s5_c_im=m_s5_c_im, m_s5_d=m_s5_d, m_w_glu=m_w_glu, m_b_glu=m_b_glu, m_conv_w=m_conv_w, m_conv_b=m_conv_b, m_w_r=m_w_r, m_b_r=m_b_r, m_w_i=m_w_i, m_b_i=m_b_i, m_lru_lambda=m_lru_lambda, m_w_a_out=m_w_a_out, m_w_b_out=m_w_b_out, m_w_o=m_w_o, m_g_ffn=m_g_ffn, m_w_ffn_gate=m_w_ffn_gate, m_w_ffn_up=m_w_ffn_up, m_w_ffn_down=m_w_ffn_down, m_g_ple_gate=m_g_ple_gate, m_w_ple_gate=m_w_ple_gate, m_b_ple_gate=m_b_ple_gate, m_w_ple=m_w_ple, m_g_ple=m_g_ple, m_g_final=m_g_final, v_g_mix=v_g_mix, v_w_in=v_w_in, v_b_in=v_b_in, v_lam_re=v_lam_re, v_lam_im=v_lam_im, v_log_dt=v_log_dt, v_s5_b_re=v_s5_b_re, v_s5_b_im=v_s5_b_im, v_s5_c_re=v_s5_c_re, v_s5_c_im=v_s5_c_im, v_s5_d=v_s5_d, v_w_glu=v_w_glu, v_b_glu=v_b_glu, v_conv_w=v_conv_w, v_conv_b=v_conv_b, v_w_r=v_w_r, v_b_r=v_b_r, v_w_i=v_w_i, v_b_i=v_b_i, v_lru_lambda=v_lru_lambda, v_w_a_out=v_w_a_out, v_w_b_out=v_w_b_out, v_w_o=v_w_o, v_g_ffn=v_g_ffn, v_w_ffn_gate=v_w_ffn_gate, v_w_ffn_up=v_w_ffn_up, v_w_ffn_down=v_w_ffn_down, v_g_ple_gate=v_g_ple_gate, v_w_ple_gate=v_w_ple_gate, v_b_ple_gate=v_b_ple_gate, v_w_ple=v_w_ple, v_g_ple=v_g_ple, v_g_final=v_g_final)
    weights = {n: given[n] for n in TWIN_WEIGHTS}
    shared = {n: given[n] for n in SHARED_INPUTS}
    per_example = {n: given[n] for n in ['x', 'p']}
    grad_fn = _jax.value_and_grad(_loss, argnums=(0, 1))

    def one_microbatch(ex, loss_target):
        ex = dict(ex)
        diff = ex.pop(TWIN_DIFF_INPUT)
        return grad_fn(weights, diff, {**shared, **ex}, loss_target)

    if N_MICROBATCH == 1:
        loss, (grad_w, grad_x) = one_microbatch(per_example, given["loss_target"])
    else:
        def body(carry, xs):
            loss_sum, grad_sum = carry
            l_k, (gw_k, gx_k) = one_microbatch(xs[0], xs[1])
            with _jax.named_scope("update"):
                return (loss_sum + l_k, _jax.tree.map(_jnp.add, grad_sum, gw_k)), gx_k

        init = (_jnp.zeros((), _jnp.float32), _jax.tree.map(_jnp.zeros_like, weights))
        (loss, grad_w), grad_x = _jax.lax.scan(body, init, (per_example, given["loss_target"]))
    with _jax.named_scope("update"):
        delta_w, new_m, new_v = {}, {}, {}
        for n in TWIN_WEIGHTS:
            delta_w[n], new_m[n], new_v[n] = _adamw(weights[n], grad_w[n], given["m_" + n], given["v_" + n])
    return (loss, grad_x, *[grad_w[n] for n in TWIN_WEIGHTS], *[delta_w[n] for n in TWIN_WEIGHTS],
            *[new_m[n] for n in TWIN_WEIGHTS], *[new_v[n] for n in TWIN_WEIGHTS])
```

```python
import functools
import math

import jax
import jax.numpy as jnp
from jax import lax
from jax.experimental import pallas as pl
from jax.experimental.pallas import tpu as pltpu

F32 = jnp.float32
BF = jnp.bfloat16

D = 1024
S5W = 512
NG, NS, NP = 32, 64, 16
GN = NG * NS
LW = 1024
NH, HD = 16, 64
LRU_C = 8.0
FH = 2816
NCHIP = 4
FC = FH // NCHIP
PLE = 256
INC = S5W + LW + 2 * D
EPS = 1e-6
ADAM_LR, ADAM_B1, ADAM_B2, ADAM_EPS, ADAM_WD, ADAM_STEP = 0.001, 0.9, 0.999, 1e-08, 0.01, 10

TM = 256
TK = 512
LC = 512
SUB = 8
VMEM_MB = 1024 * 1024
MESH = pl.DeviceIdType.MESH
ANY = pl.BlockSpec(memory_space=pl.ANY)


def _mm(a, b):
    return jnp.dot(a.astype(BF), b.astype(BF), preferred_element_type=F32)


def _mm_nt(a, b):
    return lax.dot_general(a.astype(BF), b.astype(BF), (((1,), (1,)), ((), ())), preferred_element_type=F32)


def _mm_tn(a, b):
    return lax.dot_general(a.astype(BF), b.astype(BF), (((0,), (0,)), ((), ())), preferred_element_type=F32)


def _rms(x):
    r = lax.rsqrt(jnp.mean(x * x, axis=-1, keepdims=True) + EPS)
    return x * r, r


def _rms_bwd(dy, xh, r, g):
    dxh = dy * g
    return r * (dxh - xh * jnp.mean(dxh * xh, axis=-1, keepdims=True))


def _colsum(x):
    return jnp.sum(x, axis=0, keepdims=True)


def _sig(x):
    return jax.nn.sigmoid(x)


def _gelu_grad(x):
    c = math.sqrt(2.0 / math.pi)
    t = jnp.tanh(c * (x + 0.044715 * x * x * x))
    return 0.5 * (1.0 + t) + 0.5 * x * (1.0 - t * t) * c * (1.0 + 3.0 * 0.044715 * x * x)


def _neg_expm1(x):
    series = -x * (1.0 + x * (0.5 + x * (1.0 / 6.0 + x * (1.0 / 24.0))))
    return jnp.where(x > -0.03, series, 1.0 - jnp.exp(x))


def _tok(width):
    return pl.BlockSpec((TM, width), lambda i: (i, 0))


def _tok_rev(width, nt):
    return pl.BlockSpec((TM, width), lambda i: (nt - 1 - i, 0))


def _full(shape):
    return pl.BlockSpec(shape, lambda i: (0,) * len(shape))


def _params(vmem_mb, **kw):
    return pltpu.CompilerParams(dimension_semantics=("arbitrary",), vmem_limit_bytes=vmem_mb * VMEM_MB, **kw)


def _sds(shape, dtype=F32):
    return jax.ShapeDtypeStruct(shape, dtype)


def _row_iota(width):
    return lax.broadcasted_iota(jnp.int32, (SUB, width), 0)


def _bcast_row(x, row):
    return jnp.broadcast_to(x[row:row + 1, :], x.shape)


def _slab(k):
    return pl.ds(pl.multiple_of(k * SUB, SUB), SUB)


def _inproj_fwd(x, g_mix, w_in, b_in):
    L = x.shape[0]

    def body(x_ref, g_ref, w_hbm, b_ref, h_ref, ua_ref, ub_ref, gp_ref, w_vm):
        @pl.when(pl.program_id(0) == 0)
        def _():
            pltpu.sync_copy(w_hbm, w_vm)

        xh, _ = _rms(x_ref[...])
        h = (xh * g_ref[...]).astype(BF)
        h_ref[...] = h
        z = jnp.dot(h, w_vm[...], preferred_element_type=F32) + b_ref[...]
        ua_ref[...] = z[:, :S5W]
        ub_ref[...] = z[:, S5W:S5W + LW]
        gp_ref[...] = z[:, S5W + LW:]

    return pl.pallas_call(
        body, name="inproj_fwd", grid=(L // TM,),
        in_specs=[_tok(D), _full((1, D)), ANY, _full((1, INC))],
        out_specs=[_tok(D), _tok(S5W), _tok(LW), _tok(2 * D)],
        out_shape=[_sds((L, D), BF), _sds((L, S5W)), _sds((L, LW)), _sds((L, 2 * D))],
        scratch_shapes=[pltpu.VMEM((D, INC), BF)],
        compiler_params=_params(40),
    )(x, g_mix, w_in, b_in)


def _inproj_bwd(x, dx1, dua, dub, dgp, g_mix, w_in):
    L = x.shape[0]

    def body(x_ref, dx1_ref, dua_ref, dub_ref, dgp_ref, g_ref, w_hbm, gx_ref, dz_ref, dg_ref, db_ref, w_vm):
        @pl.when(pl.program_id(0) == 0)
        def _():
            pltpu.sync_copy(w_hbm, w_vm)
            dg_ref[...] = jnp.zeros_like(dg_ref)
            db_ref[...] = jnp.zeros_like(db_ref)

        dua, dub, dgp = dua_ref[...], dub_ref[...], dgp_ref[...]
        dz_ref[:, :S5W] = dua.astype(BF)
        dz_ref[:, S5W:S5W + LW] = dub.astype(BF)
        dz_ref[:, S5W + LW:] = dgp.astype(BF)
        db_ref[0:1, :S5W] += _colsum(dua)
        db_ref[0:1, S5W:S5W + LW] += _colsum(dub)
        db_ref[0:1, S5W + LW:] += _colsum(dgp)
        dh = lax.dot_general(dz_ref[...], w_vm[...], (((1,), (1,)), ((), ())), preferred_element_type=F32)
        xh, r = _rms(x_ref[...])
        dg_ref[0:1, :] += _colsum(dh * xh)
        gx_ref[...] = dx1_ref[...] + _rms_bwd(dh, xh, r, g_ref[...])

    return pl.pallas_call(
        body, name="inproj_bwd", grid=(L // TM,),
        in_specs=[_tok(D), _tok(D), _tok(S5W), _tok(LW), _tok(2 * D), _full((1, D)), ANY],
        out_specs=[_tok(D), _tok(INC), _full((SUB, D)), _full((SUB, INC))],
        out_shape=[_sds((L, D)), _sds((L, INC), BF), _sds((SUB, D)), _sds((SUB, INC))],
        scratch_shapes=[pltpu.VMEM((D, INC), BF)],
        compiler_params=_params(40),
    )(x, dx1, dua, dub, dgp, g_mix, w_in)


def _cscan(xr_ref, xi_ref, con_ref, cr_ref, ci_ref, reverse):
    n_slab = xr_ref.shape[0] // SUB
    width = xr_ref.shape[1]
    for lc in range(width // LC):
        cols = slice(lc * LC, (lc + 1) * LC)
        con = [con_ref[SUB * j:SUB * (j + 1), cols] for j in range(8)]

        def step(k, carry, cols=cols, con=con):
            cr, ci = carry
            rows = _slab(n_slab - 1 - k if reverse else k)
            xr, xi = xr_ref[rows, cols], xi_ref[rows, cols]
            for j, sh in enumerate((1, 2, 4)):
                mr, mi = con[2 * j], con[2 * j + 1]
                pr = pltpu.roll(xr, SUB - sh if reverse else sh, 0)
                pi = pltpu.roll(xi, SUB - sh if reverse else sh, 0)
                xr, xi = xr + mr * pr - mi * pi, xi + mr * pi + mi * pr
            xr, xi = xr + con[6] * cr - con[7] * ci, xi + con[6] * ci + con[7] * cr
            xr_ref[rows, cols] = xr
            xi_ref[rows, cols] = xi
            row = 0 if reverse else SUB - 1
            return _bcast_row(xr, row), _bcast_row(xi, row)

        cr, ci = lax.fori_loop(0, n_slab, step, (cr_ref[:, cols], ci_ref[:, cols]))
        cr_ref[:, cols] = cr
        ci_ref[:, cols] = ci


def _s5_fwd(ua, bbr, bbi, ccr, cci, dsk, con, w_glu, b_glu):
    L = ua.shape[0]

    def body(ua_ref, bbr_hbm, bbi_hbm, ccr_hbm, cci_hbm, dsk_ref, con_ref, wg_ref, bg_ref,
             sr_ref, si_ref, y_ref, zg_ref, ya_ref, bbr_vm, bbi_vm, ccr_vm, cci_vm, cr_ref, ci_ref):
        @pl.when(pl.program_id(0) == 0)
        def _():
            pltpu.sync_copy(bbr_hbm, bbr_vm)
            pltpu.sync_copy(bbi_hbm, bbi_vm)
            pltpu.sync_copy(ccr_hbm, ccr_vm)
            pltpu.sync_copy(cci_hbm, cci_vm)
            cr_ref[...] = jnp.zeros_like(cr_ref)
            ci_ref[...] = jnp.zeros_like(ci_ref)

        u = ua_ref[...]
        ub = u.astype(BF)
        sr_ref[...] = jnp.dot(ub, bbr_vm[...], preferred_element_type=F32)
        si_ref[...] = jnp.dot(ub, bbi_vm[...], preferred_element_type=F32)
        _cscan(sr_ref, si_ref, con_ref, cr_ref, ci_ref, reverse=False)
        y = _mm(sr_ref[...], ccr_vm[...]) - _mm(si_ref[...], cci_vm[...]) + dsk_ref[...] * u
        y_ref[...] = y
        zg = jax.nn.gelu(y)
        zg_ref[...] = zg.astype(BF)
        q = _mm(zg, wg_ref[...]) + bg_ref[...]
        ya_ref[...] = (zg * _sig(q)).astype(BF)

    return pl.pallas_call(
        body, name="s5_fwd", grid=(L // TM,),
        in_specs=[_tok(S5W), ANY, ANY, ANY, ANY, _full((1, S5W)), _full((8 * SUB, GN)),
                  _full((S5W, S5W)), _full((1, S5W))],
        out_specs=[_tok(GN), _tok(GN), _tok(S5W), _tok(S5W), _tok(S5W)],
        out_shape=[_sds((L, GN)), _sds((L, GN)), _sds((L, S5W)), _sds((L, S5W), BF), _sds((L, S5W), BF)],
        scratch_shapes=[pltpu.VMEM((S5W, GN), BF), pltpu.VMEM((S5W, GN), BF), pltpu.VMEM((GN, S5W), BF),
                        pltpu.VMEM((GN, S5W), BF), pltpu.VMEM((SUB, GN), F32), pltpu.VMEM((SUB, GN), F32)],
        compiler_params=_params(44),
    )(ua, bbr, bbi, ccr, cci, dsk, con, w_glu, b_glu)


def _s5_bwd(dya, y, ua, sr, si, bbr, bbi, ccr, cci, dsk, con_rev, w_glu, b_glu):
    L = ua.shape[0]
    nt = L // TM
    spt = TM // SUB
    n_slab = spt

    def halo_map(i):
        return (jnp.maximum((nt - 1 - i) * spt - 1, 0), 0)

    def body(dya_ref, y_ref, ua_ref, sr_ref, si_ref, hr_ref, hi_ref, bbr_hbm, bbi_hbm, ccr_hbm, cci_hbm,
             dsk_ref, con_ref, wg_ref, bg_ref,
             dua_ref, dq_ref, dy_ref, lr_ref, li_ref, da_ref, dsm_ref,
             bbr_vm, bbi_vm, ccr_vm, cci_vm, cr_ref, ci_ref):
        i = pl.program_id(0)

        @pl.when(i == 0)
        def _():
            pltpu.sync_copy(bbr_hbm, bbr_vm)
            pltpu.sync_copy(bbi_hbm, bbi_vm)
            pltpu.sync_copy(ccr_hbm, ccr_vm)
            pltpu.sync_copy(cci_hbm, cci_vm)
            cr_ref[...] = jnp.zeros_like(cr_ref)
            ci_ref[...] = jnp.zeros_like(ci_ref)
            da_ref[...] = jnp.zeros_like(da_ref)
            dsm_ref[...] = jnp.zeros_like(dsm_ref)

        u = ua_ref[...]
        yv = y_ref[...]
        dya = dya_ref[...]
        zg = jax.nn.gelu(yv)
        sg = _sig(_mm(zg, wg_ref[...]) + bg_ref[...])
        dq = dya * zg * sg * (1.0 - sg)
        dq_ref[...] = dq.astype(BF)
        dzg = dya * sg + _mm_nt(dq, wg_ref[...])
        dy = dzg * _gelu_grad(yv)
        dyb = dy.astype(BF)
        dy_ref[...] = dyb
        dsm_ref[0:1, :] += _colsum(dy * u)
        dsm_ref[1:2, :] += _colsum(dq)
        lr_ref[...] = lax.dot_general(dyb, ccr_vm[...], (((1,), (1,)), ((), ())), preferred_element_type=F32)
        li_ref[...] = -lax.dot_general(dyb, cci_vm[...], (((1,), (1,)), ((), ())), preferred_element_type=F32)
        _cscan(lr_ref, li_ref, con_ref, cr_ref, ci_ref, reverse=True)

        first_tile = (i == nt - 1)
        row = _row_iota(LC)
        for lc in range(GN // LC):
            cols = slice(lc * LC, (lc + 1) * LC)
            h_r = jnp.where(first_tile, 0.0, hr_ref[:, cols])
            h_i = jnp.where(first_tile, 0.0, hi_ref[:, cols])

            def step(k, acc, cols=cols, h_r=h_r, h_i=h_i):
                ar, ai = acc
                rows = _slab(k)
                prev = _slab(jnp.maximum(k - 1, 0))
                pr = jnp.where(k == 0, h_r, sr_ref[prev, cols])
                pi = jnp.where(k == 0, h_i, si_ref[prev, cols])
                spr = pltpu.roll(jnp.where(row == SUB - 1, pr, sr_ref[rows, cols]), 1, 0)
                spi = pltpu.roll(jnp.where(row == SUB - 1, pi, si_ref[rows, cols]), 1, 0)
                lr, li = lr_ref[rows, cols], li_ref[rows, cols]
                return ar + lr * spr + li * spi, ai + li * spr - lr * spi

            zero = jnp.zeros((SUB, LC), F32)
            ar, ai = lax.fori_loop(0, n_slab, step, (zero, zero))
            da_ref[0:1, cols] += _colsum(ar)
            da_ref[1:2, cols] += _colsum(ai)

        dua_ref[...] = (dy * dsk_ref[...] + _mm_nt(lr_ref[...], bbr_vm[...]) + _mm_nt(li_ref[...], bbi_vm[...]))

    return pl.pallas_call(
        body, name="s5_bwd", grid=(nt,),
        in_specs=[_tok_rev(S5W, nt), _tok_rev(S5W, nt), _tok_rev(S5W, nt), _tok_rev(GN, nt), _tok_rev(GN, nt),
                  pl.BlockSpec((SUB, GN), halo_map), pl.BlockSpec((SUB, GN), halo_map),
                  ANY, ANY, ANY, ANY, _full((1, S5W)), _full((8 * SUB, GN)), _full((S5W, S5W)), _full((1, S5W))],
        out_specs=[_tok_rev(S5W, nt), _tok_rev(S5W, nt), _tok_rev(S5W, nt), _tok_rev(GN, nt), _tok_rev(GN, nt),
                   _full((SUB, GN)), _full((SUB, S5W))],
        out_shape=[_sds((L, S5W)), _sds((L, S5W), BF), _sds((L, S5W), BF), _sds((L, GN)), _sds((L, GN)),
                   _sds((SUB, GN)), _sds((SUB, S5W))],
        scratch_shapes=[pltpu.VMEM((S5W, GN), BF), pltpu.VMEM((S5W, GN), BF), pltpu.VMEM((GN, S5W), BF),
                        pltpu.VMEM((GN, S5W), BF), pltpu.VMEM((SUB, GN), F32), pltpu.VMEM((SUB, GN), F32)],
        compiler_params=_params(52),
    )(dya, y, ua, sr, si, sr, si, bbr, bbi, ccr, cci, dsk, con_rev, w_glu, b_glu)


def _lru_gate_terms(rg, sp):
    log_a = -LRU_C * rg * sp
    a = jnp.exp(log_a)
    mult = jnp.sqrt(_neg_expm1(2.0 * log_a))
    return a, mult


def _lru_fwd(ub, conv_w, conv_b, wr, wi, b_r, b_i, sp):
    L = ub.shape[0]
    n_slab = TM // SUB

    def body(ub_ref, cw_ref, cb_ref, wr_ref, wi_ref, br_ref, bi_ref, sp_ref,
             xc_ref, rg_ref, ig_ref, h_ref, hp_ref, a_ref, halo_ref, carry_ref):
        @pl.when(pl.program_id(0) == 0)
        def _():
            halo_ref[...] = jnp.zeros_like(halo_ref)
            carry_ref[...] = jnp.zeros_like(carry_ref)

        row = _row_iota(LW)
        taps = [cw_ref[k:k + 1, :] for k in range(4)]
        cb = cb_ref[...]

        def conv_step(k, prev):
            rows = _slab(k)
            cur = ub_ref[rows, :]
            acc = taps[3] * cur + cb
            for j in (1, 2, 3):
                acc = acc + taps[3 - j] * pltpu.roll(jnp.where(row >= SUB - j, prev, cur), j, 0)
            xc_ref[rows, :] = acc
            return cur

        halo_ref[...] = lax.fori_loop(0, n_slab, conv_step, halo_ref[...])

        xc = xc_ref[...]
        xcb = xc.astype(BF)
        rg = _sig(jnp.dot(xcb, wr_ref[...], preferred_element_type=F32) + br_ref[...])
        ig = _sig(jnp.dot(xcb, wi_ref[...], preferred_element_type=F32) + bi_ref[...])
        rg_ref[...] = rg
        ig_ref[...] = ig
        a, mult = _lru_gate_terms(rg, sp_ref[...])
        a_ref[...] = a
        h_ref[...] = mult * ig * xc

        rowc = _row_iota(LC)
        for lc in range(LW // LC):
            cols = slice(lc * LC, (lc + 1) * LC)

            def step(k, c, cols=cols):
                rows = _slab(k)
                av, b = a_ref[rows, cols], h_ref[rows, cols]
                for sh in (1, 2, 4):
                    keep = rowc >= sh
                    b = b + av * jnp.where(keep, pltpu.roll(b, sh, 0), 0.0)
                    av = av * jnp.where(keep, pltpu.roll(av, sh, 0), 1.0)
                h = b + av * c
                h_ref[rows, cols] = h
                hp_ref[rows, cols] = jnp.where(rowc == 0, c, pltpu.roll(h, 1, 0))
                return _bcast_row(h, SUB - 1)

            carry_ref[:, cols] = lax.fori_loop(0, n_slab, step, carry_ref[:, cols])

    return pl.pallas_call(
        body, name="lru_fwd", grid=(L // TM,),
        in_specs=[_tok(LW), _full((4, LW)), _full((1, LW)), _full((LW, LW)), _full((LW, LW)),
                  _full((1, LW)), _full((1, LW)), _full((1, LW))],
        out_specs=[_tok(LW)] * 5,
        out_shape=[_sds((L, LW))] * 5,
        scratch_shapes=[pltpu.VMEM((TM, LW), F32), pltpu.VMEM((SUB, LW), F32), pltpu.VMEM((SUB, LW), F32)],
        compiler_params=_params(40),
    )(ub, conv_w, conv_b, wr, wi, b_r, b_i, sp)


def _lru_bwd(dyb, xc, rg, ig, hp, ub, conv_w, wr, wi, sp):
    L = ub.shape[0]
    nt = L // TM
    spt = TM // SUB
    n_slab = spt

    def halo_map(i):
        return (jnp.maximum((nt - 1 - i) * spt - 1, 0), 0)

    def body(dh_ref, xc_ref, rg_ref, ig_ref, hp_ref, ub_ref, uh_ref, cw_ref, wr_ref, wi_ref, sp_ref,
             dub_ref, dpr_ref, dpi_ref, acc_ref, a_ref, lam_ref, dxc_ref, carry_ref, next_ref):
        i = pl.program_id(0)

        @pl.when(i == 0)
        def _():
            carry_ref[...] = jnp.zeros_like(carry_ref)
            next_ref[...] = jnp.zeros_like(next_ref)
            acc_ref[...] = jnp.zeros_like(acc_ref)

        sp = sp_ref[...]
        rg, ig, xc = rg_ref[...], ig_ref[...], xc_ref[...]
        a, mult = _lru_gate_terms(rg, sp)
        a_ref[...] = a

        rowc = _row_iota(LC)
        for lc in range(LW // LC):
            cols = slice(lc * LC, (lc + 1) * LC)

            def step(k, c, cols=cols):
                rows = _slab(n_slab - 1 - k)
                av, dh = a_ref[rows, cols], dh_ref[rows, cols]
                b = av * dh
                for sh in (1, 2, 4):
                    keep = rowc < SUB - sh
                    b = b + av * jnp.where(keep, pltpu.roll(b, SUB - sh, 0), 0.0)
                    av = av * jnp.where(keep, pltpu.roll(av, SUB - sh, 0), 1.0)
                mu = b + av * c
                lam_ref[rows, cols] = dh + jnp.where(rowc == SUB - 1, c, pltpu.roll(mu, SUB - 1, 0))
                return _bcast_row(mu, 0)

            carry_ref[:, cols] = lax.fori_loop(0, n_slab, step, carry_ref[:, cols])

        lam = lam_ref[...]
        d_a = lam * hp_ref[...]
        d_mult = lam * ig * xc
        d_ig = lam * mult * xc
        dxc = lam * mult * ig
        d_log_a = d_a * a - d_mult * a * a / mult
        d_rg = (-LRU_C) * sp * d_log_a
        acc_ref[0:1, :] += _colsum((-LRU_C) * rg * d_log_a)
        dpr = d_rg * rg * (1.0 - rg)
        dpi = d_ig * ig * (1.0 - ig)
        acc_ref[1:2, :] += _colsum(dpr)
        acc_ref[2:3, :] += _colsum(dpi)
        dprb, dpib = dpr.astype(BF), dpi.astype(BF)
        dpr_ref[...] = dprb
        dpi_ref[...] = dpib
        dxc = dxc + _mm_nt(dprb, wr_ref[...]) + _mm_nt(dpib, wi_ref[...])
        dxc_ref[...] = dxc
        acc_ref[3:4, :] += _colsum(dxc)

        row = _row_iota(LW)
        taps = [cw_ref[k:k + 1, :] for k in range(4)]
        u_halo = jnp.where(i == nt - 1, 0.0, uh_ref[...])
        nxt_tile = next_ref[...]

        def conv_step(k, accs):
            rows = _slab(k)
            cur = dxc_ref[rows, :]
            nxt = jnp.where(k == n_slab - 1, nxt_tile, dxc_ref[_slab(jnp.minimum(k + 1, n_slab - 1)), :])
            ucur = ub_ref[rows, :]
            uprev = jnp.where(k == 0, u_halo, ub_ref[_slab(jnp.maximum(k - 1, 0)), :])
            du = taps[3] * cur
            new = [accs[3] + cur * ucur]
            for j in (1, 2, 3):
                du = du + taps[3 - j] * pltpu.roll(jnp.where(row < j, nxt, cur), SUB - j, 0)
                new.append(accs[3 - j] + cur * pltpu.roll(jnp.where(row >= SUB - j, uprev, ucur), j, 0))
            dub_ref[rows, :] = du
            return tuple(new[::-1])

        zero = jnp.zeros((SUB, LW), F32)
        accs = lax.fori_loop(0, n_slab, conv_step, (zero, zero, zero, zero))
        for k in range(4):
            acc_ref[4 + k:5 + k, :] += _colsum(accs[k])
        next_ref[...] = dxc_ref[0:SUB, :]

    return pl.pallas_call(
        body, name="lru_bwd", grid=(nt,),
        in_specs=[_tok_rev(LW, nt)] * 6 + [pl.BlockSpec((SUB, LW), halo_map), _full((4, LW)),
                                           _full((LW, LW)), _full((LW, LW)), _full((1, LW))],
        out_specs=[_tok_rev(LW, nt), _tok_rev(LW, nt), _tok_rev(LW, nt), _full((SUB, LW))],
        out_shape=[_sds((L, LW)), _sds((L, LW), BF), _sds((L, LW), BF), _sds((SUB, LW))],
        scratch_shapes=[pltpu.VMEM((TM, LW), F32), pltpu.VMEM((TM, LW), F32), pltpu.VMEM((TM, LW), F32),
                        pltpu.VMEM((SUB, LW), F32), pltpu.VMEM((SUB, LW), F32)],
        compiler_params=_params(48),
    )(dyb, xc, rg, ig, hp, ub, ub, conv_w, wr, wi, sp)


def _merge_fwd(x, ya, yb, gp, w_a, w_b, w_o):
    L = x.shape[0]

    def body(x_ref, ya_ref, yb_ref, gp_ref, wa_ref, wb_ref, wo_ref, x1_ref, pa_ref, pb_ref, mg_ref):
        pa = jnp.dot(ya_ref[...], wa_ref[...], preferred_element_type=F32)
        pb = _mm(yb_ref[...], wb_ref[...])
        pa_ref[...] = pa
        pb_ref[...] = pb
        gp = gp_ref[...]
        merged = (_sig(gp[:, :D]) * pa + _sig(gp[:, D:]) * pb).astype(BF)
        mg_ref[...] = merged
        x1_ref[...] = x_ref[...] + jnp.dot(merged, wo_ref[...], preferred_element_type=F32)

    return pl.pallas_call(
        body, name="merge_fwd", grid=(L // TM,),
        in_specs=[_tok(D), _tok(S5W), _tok(LW), _tok(2 * D), _full((S5W, D)), _full((LW, D)), _full((D, D))],
        out_specs=[_tok(D), _tok(D), _tok(D), _tok(D)],
        out_shape=[_sds((L, D)), _sds((L, D)), _sds((L, D)), _sds((L, D), BF)],
        compiler_params=_params(40),
    )(x, ya, yb, gp, w_a, w_b, w_o)


def _merge_bwd(dx1, gp, pa, pb, w_a, w_b, w_o):
    L = dx1.shape[0]

    def body(dx1_ref, gp_ref, pa_ref, pb_ref, wa_ref, wb_ref, wo_ref, dya_ref, dyb_ref, dgp_ref, dpa_ref, dpb_ref):
        dm = _mm_nt(dx1_ref[...], wo_ref[...])
        gp = gp_ref[...]
        sa, sb = _sig(gp[:, :D]), _sig(gp[:, D:])
        dpa = (dm * sa).astype(BF)
        dpb = (dm * sb).astype(BF)
        dpa_ref[...] = dpa
        dpb_ref[...] = dpb
        dgp_ref[:, :D] = dm * pa_ref[...] * sa * (1.0 - sa)
        dgp_ref[:, D:] = dm * pb_ref[...] * sb * (1.0 - sb)
        dya_ref[...] = _mm_nt(dpa, wa_ref[...])
        dyb_ref[...] = _mm_nt(dpb, wb_ref[...])

    return pl.pallas_call(
        body, name="merge_bwd", grid=(L // TM,),
        in_specs=[_tok(D), _tok(2 * D), _tok(D), _tok(D), _full((S5W, D)), _full((LW, D)), _full((D, D))],
        out_specs=[_tok(S5W), _tok(LW), _tok(2 * D), _tok(D), _tok(D)],
        out_shape=[_sds((L, S5W)), _sds((L, LW)), _sds((L, 2 * D)), _sds((L, D), BF), _sds((L, D), BF)],
        compiler_params=_params(40),
    )(dx1, gp, pa, pb, w_a, w_b, w_o)


def _chunk_tok(width):
    return pl.BlockSpec((NCHIP, TM, width), lambda i: (0, i, 0))


def _ffn_fwd(x1, g_ffn, wg, wu, wd):
    L = x1.shape[0]

    def body(x_ref, g_ref, wg_hbm, wu_hbm, wd_hbm, x2_ref, h2_ref, gg_ref, uu_ref, wg_vm, wu_vm, wd_vm):
        @pl.when(pl.program_id(0) == 0)
        def _():
            pltpu.sync_copy(wg_hbm, wg_vm)
            pltpu.sync_copy(wu_hbm, wu_vm)
            pltpu.sync_copy(wd_hbm, wd_vm)

        x = x_ref[...]
        xh, _ = _rms(x)
        h2 = (xh * g_ref[...]).astype(BF)
        h2_ref[...] = h2
        out = x
        for c in range(NCHIP):
            gg = jnp.dot(h2, wg_vm[c], preferred_element_type=F32)
            uu = jnp.dot(h2, wu_vm[c], preferred_element_type=F32)
            gg_ref[c] = gg.astype(BF)
            uu_ref[c] = uu.astype(BF)
            act = (gg * _sig(gg) * uu).astype(BF)
            out = out + jnp.dot(act, wd_vm[c], preferred_element_type=F32)
        x2_ref[...] = out

    return pl.pallas_call(
        body, name="ffn_fwd", grid=(L // TM,),
        in_specs=[_tok(D), _full((1, D)), ANY, ANY, ANY],
        out_specs=[_tok(D), _tok(D), _chunk_tok(FC), _chunk_tok(FC)],
        out_shape=[_sds((L, D)), _sds((L, D), BF), _sds((NCHIP, L, FC), BF), _sds((NCHIP, L, FC), BF)],
        scratch_shapes=[pltpu.VMEM((NCHIP, D, FC), BF), pltpu.VMEM((NCHIP, D, FC), BF), pltpu.VMEM((NCHIP, FC, D), BF)],
        compiler_params=_params(52),
    )(x1, g_ffn, wg, wu, wd)


def _ffn_bwd(x1, dx2, gg, uu, g_ffn, wg, wu, wd):
    L = x1.shape[0]

    def body(x_ref, dx2_ref, gg_ref, uu_ref, g_ref, wg_hbm, wu_hbm, wd_hbm,
             dx1_ref, act_ref, dgg_ref, duu_ref, dg_ref, wg_vm, wu_vm, wd_vm):
        @pl.when(pl.program_id(0) == 0)
        def _():
            pltpu.sync_copy(wg_hbm, wg_vm)
            pltpu.sync_copy(wu_hbm, wu_vm)
            pltpu.sync_copy(wd_hbm, wd_vm)
            dg_ref[...] = jnp.zeros_like(dg_ref)

        dx2 = dx2_ref[...]
        dx2b = dx2.astype(BF)
        dh2 = jnp.zeros((TM, D), F32)
        for c in range(NCHIP):
            g = gg_ref[c].astype(F32)
            u = uu_ref[c].astype(F32)
            s = _sig(g)
            silu = g * s
            act_ref[c] = (silu * u).astype(BF)
            dact = lax.dot_general(dx2b, wd_vm[c], (((1,), (1,)), ((), ())), preferred_element_type=F32)
            dg = (dact * u * s * (1.0 + g * (1.0 - s))).astype(BF)
            du = (dact * silu).astype(BF)
            dgg_ref[c] = dg
            duu_ref[c] = du
            dh2 = dh2 + lax.dot_general(dg, wg_vm[c], (((1,), (1,)), ((), ())), preferred_element_type=F32)
            dh2 = dh2 + lax.dot_general(du, wu_vm[c], (((1,), (1,)), ((), ())), preferred_element_type=F32)
        xh, r = _rms(x_ref[...])
        dg_ref[0:1, :] += _colsum(dh2 * xh)
        dx1_ref[...] = dx2 + _rms_bwd(dh2, xh, r, g_ref[...])

    return pl.pallas_call(
        body, name="ffn_bwd", grid=(L // TM,),
        in_specs=[_tok(D), _tok(D), _chunk_tok(FC), _chunk_tok(FC), _full((1, D)), ANY, ANY, ANY],
        out_specs=[_tok(D), _chunk_tok(FC), _chunk_tok(FC), _chunk_tok(FC), _full((SUB, D))],
        out_shape=[_sds((L, D)), _sds((NCHIP, L, FC), BF), _sds((NCHIP, L, FC), BF), _sds((NCHIP, L, FC), BF),
                   _sds((SUB, D))],
        scratch_shapes=[pltpu.VMEM((NCHIP, D, FC), BF), pltpu.VMEM((NCHIP, D, FC), BF), pltpu.VMEM((NCHIP, FC, D), BF)],
        compiler_params=_params(56),
    )(x1, dx2, gg, uu, g_ffn, wg, wu, wd)


def _ple_loss(x2, p, tgt, g_pg, w_pg, b_pg, w_ple, g_ple, g_final):
    L = x2.shape[0]

    def body(x2_ref, p_ref, t_ref, gpg_ref, wpg_ref, bpg_ref, wple_ref, gple_ref, gf_ref,
             dx2_ref, n2_ref, dpre_ref, de0_ref, acc_ref):
        @pl.when(pl.program_id(0) == 0)
        def _():
            acc_ref[...] = jnp.zeros_like(acc_ref)

        x2 = x2_ref[...]
        x2h, r2 = _rms(x2)
        n2 = (x2h * gpg_ref[...]).astype(BF)
        n2_ref[...] = n2
        gate = _sig(jnp.dot(n2, wpg_ref[...], preferred_element_type=F32) + bpg_ref[...])
        e0 = _mm(p_ref[...], wple_ref[...])
        e0h, re = _rms(e0)
        e = e0h * gple_ref[...]
        x3 = x2 + gate * e
        x3h, r3 = _rms(x3)
        diff = x3h * gf_ref[...] - t_ref[...]
        acc_ref[4:5, :] += _colsum(diff * diff) * (0.5 / D)
        dy = diff * (1.0 / D)
        acc_ref[3:4, :] += _colsum(dy * x3h)
        dx3 = _rms_bwd(dy, x3h, r3, gf_ref[...])
        de = dx3 * gate
        acc_ref[2:3, :] += _colsum(de * e0h)
        de0_ref[...] = _rms_bwd(de, e0h, re, gple_ref[...]).astype(BF)
        dpre = dx3 * e * gate * (1.0 - gate)
        acc_ref[1:2, :] += _colsum(dpre)
        dpreb = dpre.astype(BF)
        dpre_ref[...] = dpreb
        dn2 = lax.dot_general(dpreb, wpg_ref[...], (((1,), (1,)), ((), ())), preferred_element_type=F32)
        acc_ref[0:1, :] += _colsum(dn2 * x2h)
        dx2_ref[...] = dx3 + _rms_bwd(dn2, x2h, r2, gpg_ref[...])

    return pl.pallas_call(
        body, name="ple_loss", grid=(L // TM,),
        in_specs=[_tok(D), _tok(PLE), _tok(D), _full((1, D)), _full((D, D)), _full((1, D)), _full((PLE, D)),
                  _full((1, D)), _full((1, D))],
        out_specs=[_tok(D), _tok(D), _tok(D), _tok(D), _full((SUB, D))],
        out_shape=[_sds((L, D)), _sds((L, D), BF), _sds((L, D), BF), _sds((L, D), BF), _sds((SUB, D))],
        compiler_params=_params(40),
    )(x2, p, tgt, g_pg, w_pg, b_pg, w_ple, g_ple, g_final)


def _tn(name, a, b, a_chunk=None, b_chunk=None):
    L = a.shape[-2]
    m, n = a.shape[-1], b.shape[-1]
    bn = n
    for cand in (896, 1024, 512):
        if n > cand and n % cand == 0:
            bn = cand
            break

    def spec(arr, chunk, width, col_of):
        if chunk is None:
            return pl.BlockSpec((TK, width), lambda j, t: (t, col_of(j)))
        return pl.BlockSpec((None, TK, width), lambda j, t: (chunk, t, col_of(j)))

    def body(a_ref, b_ref, o_ref):
        @pl.when(pl.program_id(1) == 0)
        def _():
            o_ref[...] = jnp.zeros_like(o_ref)

        o_ref[...] += _mm_tn(a_ref[...], b_ref[...])

    return pl.pallas_call(
        body, name=name, grid=(n // bn, L // TK),
        in_specs=[spec(a, a_chunk, m, lambda j: 0), spec(b, b_chunk, bn, lambda j: j)],
        out_specs=pl.BlockSpec((m, bn), lambda j, t: (0, j)),
        out_shape=_sds((m, n)),
        compiler_params=pltpu.CompilerParams(dimension_semantics=("arbitrary", "arbitrary"),
                                             vmem_limit_bytes=40 * VMEM_MB),
    )(a, b)


def _s5_discretize(lam_re, lam_im, log_dt, b_re, b_im):
    dt = jnp.exp(log_dt)[:, None]
    mag = jnp.exp(lam_re * dt)
    ar = mag * jnp.cos(lam_im * dt)
    ai = mag * jnp.sin(lam_im * dt)
    den = lam_re * lam_re + lam_im * lam_im
    nr = ar - 1.0
    fr = (nr * lam_re + ai * lam_im) / den
    fi = (ai * lam_re - nr * lam_im) / den
    bbr = fr[..., None] * b_re - fi[..., None] * b_im
    bbi = fr[..., None] * b_im + fi[..., None] * b_re
    return ar, ai, bbr, bbi


def _scan_constants(ar, ai):
    ar, ai = ar.reshape(1, GN), ai.reshape(1, GN)
    pw = [(jnp.ones_like(ar), jnp.zeros_like(ai))]
    for _ in range(SUB):
        pr, pi = pw[-1]
        pw.append((pr * ar - pi * ai, pr * ai + pi * ar))
    row = lax.broadcasted_iota(jnp.int32, (SUB, GN), 0)

    def build(reverse):
        sign = -1.0 if reverse else 1.0
        blocks = []
        for sh in (1, 2, 4):
            keep = (row < SUB - sh) if reverse else (row >= sh)
            blocks += [jnp.where(keep, pw[sh][0], 0.0), jnp.where(keep, sign * pw[sh][1], 0.0)]
        order = [SUB - i for i in range(SUB)] if reverse else [i + 1 for i in range(SUB)]
        blocks += [jnp.concatenate([pw[k][0] for k in order], 0), jnp.concatenate([sign * pw[k][1] for k in order], 0)]
        return jnp.concatenate(blocks, 0)

    return build(False), build(True)


def _blockdiag(blocks):
    g, r, c = blocks.shape
    eye = jnp.eye(g, dtype=blocks.dtype)
    return (blocks[:, :, None, :] * eye[:, None, :, None]).reshape(g * r, g * c)


def _diag_blocks(dense, g):
    r, c = dense.shape[0] // g, dense.shape[1] // g
    d4 = dense.reshape(g, r, g, c)
    idx = jnp.arange(g)
    return d4[idx, :, idx, :]


def _local_step(x, p, tgt, w):
    f = lambda a: a.astype(F32)
    ar, ai, bbr, bbi = _s5_discretize(w["lam_re"], w["lam_im"], w["log_dt"], w["s5_b_re"], w["s5_b_im"])
    con, con_rev = _scan_constants(ar, ai)
    bbr_d = _blockdiag(jnp.swapaxes(bbr, 1, 2)).astype(BF)
    bbi_d = _blockdiag(jnp.swapaxes(bbi, 1, 2)).astype(BF)
    ccr_d = _blockdiag(jnp.swapaxes(w["s5_c_re"], 1, 2)).astype(BF)
    cci_d = _blockdiag(jnp.swapaxes(w["s5_c_im"], 1, 2)).astype(BF)
    dsk = w["s5_d"].reshape(1, S5W)
    wr_d = _blockdiag(w["w_r"]).astype(BF)
    wi_d = _blockdiag(w["w_i"]).astype(BF)
    lam = w["lru_lambda"].reshape(1, LW)
    sp = jax.nn.softplus(-lam)
    b_r, b_i = w["b_r"].reshape(1, LW), w["b_i"].reshape(1, LW)
    row = lambda name: w[name].reshape(1, -1)

    h, ua, ub, gp = _inproj_fwd(x, row("g_mix"), w["w_in"], row("b_in"))
    sr, si, y, zg, ya = _s5_fwd(ua, bbr_d, bbi_d, ccr_d, cci_d, dsk, con, w["w_glu"], row("b_glu"))
    xc, rg, ig, yb, hp = _lru_fwd(ub, w["conv_w"], row("conv_b"), wr_d, wi_d, b_r, b_i, sp)
    x1, pa, pb, merged = _merge_fwd(x, ya, yb, gp, w["w_a_out"], w["w_b_out"], w["w_o"])
    x2, h2, gg, uu = _ffn_fwd(x1, row("g_ffn"), w["w_ffn_gate"], w["w_ffn_up"], w["w_ffn_down"])
    dx2, n2, dpre, de0, acc_p = _ple_loss(x2, p, tgt, row("g_ple_gate"), w["w_ple_gate"], row("b_ple_gate"),
                                          w["w_ple"], row("g_ple"), row("g_final"))
    dx1, act, dgg, duu, acc_f = _ffn_bwd(x1, dx2, gg, uu, row("g_ffn"), w["w_ffn_gate"], w["w_ffn_up"], w["w_ffn_down"])
    dya, dyb, dgp, dpa, dpb = _merge_bwd(dx1, gp, pa, pb, w["w_a_out"], w["w_b_out"], w["w_o"])
    dua, dq, dy, lr, li, acc_a, acc_s = _s5_bwd(dya, y, ua, sr, si, bbr_d, bbi_d, ccr_d, cci_d, dsk, con_rev,
                                                w["w_glu"], row("b_glu"))
    dub, dpr, dpi, acc_l = _lru_bwd(dyb, xc, rg, ig, hp, ub, w["conv_w"], wr_d, wi_d, sp)
    gx, dz, acc_g, acc_b = _inproj_bwd(x, dx1, dua, dub, dgp, row("g_mix"), w["w_in"])

    g = {}
    g["w_in"] = _tn("dw_in", h, dz)
    g["w_glu"] = _tn("dw_glu", zg, dq)
    d_bbr = _diag_blocks(_tn("d_bbr", ua, lr), NG)
    d_bbi = _diag_blocks(_tn("d_bbi", ua, li), NG)
    g["s5_c_re"] = jnp.swapaxes(_diag_blocks(_tn("d_ccr", sr, dy), NG), 1, 2)
    g["s5_c_im"] = -jnp.swapaxes(_diag_blocks(_tn("d_cci", si, dy), NG), 1, 2)
    g["w_r"] = _diag_blocks(_tn("dw_r", xc, dpr), NH)
    g["w_i"] = _diag_blocks(_tn("dw_i", xc, dpi), NH)
    g["w_a_out"] = _tn("dw_a_out", ya, dpa)
    g["w_b_out"] = _tn("dw_b_out", yb, dpb)
    g["w_o"] = _tn("dw_o", merged, dx1)
    g["w_ffn_gate"] = jnp.stack([_tn("dw_ffn_gate%d" % c, h2, dgg, None, c) for c in range(NCHIP)])
    g["w_ffn_up"] = jnp.stack([_tn("dw_ffn_up%d" % c, h2, duu, None, c) for c in range(NCHIP)])
    g["w_ffn_down"] = jnp.stack([_tn("dw_ffn_down%d" % c, act, dx2, c, None) for c in range(NCHIP)])
    g["w_ple_gate"] = _tn("dw_ple_gate", n2, dpre)
    g["w_ple"] = _tn("dw_ple", p, de0)

    d_ar, d_ai = acc_a[0].reshape(NG, NS), acc_a[1].reshape(NG, NS)
    _, vjp = jax.vjp(_s5_discretize, w["lam_re"], w["lam_im"], w["log_dt"], w["s5_b_re"], w["s5_b_im"])
    g["lam_re"], g["lam_im"], g["log_dt"], g["s5_b_re"], g["s5_b_im"] = vjp(
        (d_ar, d_ai, jnp.swapaxes(d_bbr, 1, 2), jnp.swapaxes(d_bbi, 1, 2)))
    g["s5_d"] = acc_s[0].reshape(NG, NP)
    g["b_glu"] = acc_s[1]
    g["lru_lambda"] = (acc_l[0:1] * (-_sig(-lam))).reshape(LW)
    g["b_r"] = acc_l[1].reshape(NH, HD)
    g["b_i"] = acc_l[2].reshape(NH, HD)
    g["conv_b"] = acc_l[3]
    g["conv_w"] = acc_l[4:8]
    g["g_mix"] = acc_g[0]
    g["b_in"] = acc_b[0]
    g["g_ffn"] = acc_f[0]
    g["g_ple_gate"] = acc_p[0]
    g["b_ple_gate"] = acc_p[1]
    g["g_ple"] = acc_p[2]
    g["g_final"] = acc_p[3]
    loss = jnp.sum(acc_p[4])
    return loss, gx, g


LANES = 1024
SHARDED = [("w_in", (D, INC // NCHIP), 1), ("w_glu", (S5W // NCHIP, S5W), 0), ("w_a_out", (S5W, D // NCHIP), 1),
           ("w_b_out", (LW // NCHIP, D), 0), ("w_o", (D // NCHIP, D), 0), ("w_ffn_gate", (D, FC), None),
           ("w_ffn_up", (D, FC), None), ("w_ffn_down", (FC, D), None), ("w_ple_gate", (D // NCHIP, D), 0),
           ("w_ple", (PLE, D // NCHIP), 1)]
QROWS = sum(s[0] * s[1] for _, s, _ in SHARDED) // LANES
HROWS = QROWS // 2
CONV_ROWS = 2
GROWS = 4064
GHALF = GROWS // 2
REPLICATED = [("g_mix", (D,)), ("b_in", (INC,)), ("lam_re", (NG, NS)), ("lam_im", (NG, NS)), ("log_dt", (NG,)),
              ("s5_b_re", (NG, NS, NP)), ("s5_b_im", (NG, NS, NP)), ("s5_c_re", (NG, NP, NS)), ("s5_c_im", (NG, NP, NS)),
              ("s5_d", (NG, NP)), ("b_glu", (S5W,)), ("conv_b", (LW,)), ("w_r", (NH, HD, HD)), ("b_r", (NH, HD)),
              ("w_i", (NH, HD, HD)), ("b_i", (NH, HD)), ("lru_lambda", (LW,)), ("g_ffn", (D,)), ("g_ple_gate", (D,)),
              ("b_ple_gate", (D,)), ("g_ple", (D,)), ("g_final", (D,))]
SMALL_HEAD = 8
SMALL_ROWS = 288
ROW_BLOCK = 672


def _mesh_pos():
    return lax.axis_index("x"), lax.axis_index("y"), lax.axis_index("c")


def _other_chips(x, y):
    return [(1 - x, y), (x, 1 - y), (1 - x, 1 - y)]


def _gather_weights(send):
    def body(send_ref, out_ref, send_sems, recv_sems, local_sem):
        x, y, c = _mesh_pos()
        k0 = 2 * x + y
        sib = (x, y, 1 - c)
        chips = _other_chips(x, y)
        mine = pl.ds(pl.multiple_of(c * GHALF, 16), GHALF)
        other = pl.ds(pl.multiple_of((1 - c) * GHALF, 16), GHALF)

        def ici(j, src_chip, to):
            return pltpu.make_async_remote_copy(
                src_ref=send_ref.at[mine], dst_ref=out_ref.at[src_chip, mine],
                send_sem=send_sems.at[j], recv_sem=recv_sems.at[j], device_id=to, device_id_type=MESH)

        def d2d(j, src_chip, half):
            return pltpu.make_async_remote_copy(
                src_ref=out_ref.at[src_chip, half], dst_ref=out_ref.at[src_chip, half],
                send_sem=send_sems.at[3 + j], recv_sem=recv_sems.at[3 + j], device_id=sib, device_id_type=MESH)

        local = pltpu.make_async_copy(send_ref, out_ref.at[k0], local_sem)
        local.start()
        first = [ici(j, k0, (*chip, c)) for j, chip in enumerate(chips)]
        for cp in first:
            cp.start()
        passed = []
        for j, chip in enumerate(chips):
            kj = 2 * chip[0] + chip[1]
            ici(j, kj, (*chip, c)).wait_recv()
            fwd = d2d(j, kj, mine)
            fwd.start()
            passed.append(fwd)
        for j, chip in enumerate(chips):
            d2d(j, 2 * chip[0] + chip[1], other).wait_recv()
        for cp in first + passed:
            cp.wait_send()
        local.wait()

    return pl.pallas_call(
        body, name="gather_weights", in_specs=[ANY], out_specs=ANY,
        out_shape=_sds((NCHIP, GROWS, LANES), BF),
        scratch_shapes=[pltpu.SemaphoreType.DMA((6,)), pltpu.SemaphoreType.DMA((6,)), pltpu.SemaphoreType.DMA],
    )(send)


def _swap_sibling_halves(g):
    def body(g_ref, out_ref, send_sem, recv_sem):
        x, y, c = _mesh_pos()
        theirs = pl.ds(pl.multiple_of((1 - c) * HROWS, 8), HROWS)
        cp = pltpu.make_async_remote_copy(src_ref=g_ref.at[:, theirs], dst_ref=out_ref, send_sem=send_sem,
                                          recv_sem=recv_sem, device_id=(x, y, 1 - c), device_id_type=MESH)
        cp.start()
        cp.wait()

    return pl.pallas_call(
        body, name="swap_sibling_halves", in_specs=[ANY], out_specs=ANY,
        out_shape=_sds((NCHIP, HROWS, LANES)),
        scratch_shapes=[pltpu.SemaphoreType.DMA, pltpu.SemaphoreType.DMA],
    )(g)


def _add_sibling(c_idx, g, got):
    nb = HROWS // ROW_BLOCK

    def body(c_ref, g_ref, got_ref, p_ref, pb_ref):
        s = g_ref[...] + got_ref[...]
        p_ref[...] = s
        pb_ref[...] = s.astype(BF)

    spec = pl.BlockSpec((None, ROW_BLOCK, LANES), lambda k, r, c_ref: (k, r, 0))
    return pl.pallas_call(
        body, name="add_sibling",
        grid_spec=pltpu.PrefetchScalarGridSpec(
            num_scalar_prefetch=1, grid=(NCHIP, nb),
            in_specs=[pl.BlockSpec((None, ROW_BLOCK, LANES), lambda k, r, c_ref: (k, c_ref[0] * nb + r, 0)), spec],
            out_specs=[spec, spec]),
        out_shape=[_sds((NCHIP, HROWS, LANES)), _sds((NCHIP, HROWS, LANES), BF)],
        compiler_params=pltpu.CompilerParams(dimension_semantics=("arbitrary", "arbitrary"),
                                             vmem_limit_bytes=40 * VMEM_MB),
    )(c_idx, g, got)


def _exchange_chips(pb):
    def body(pb_ref, out_ref, send_sems, recv_sems):
        x, y, c = _mesh_pos()
        cps = []
        for j, chip in enumerate(_other_chips(x, y)):
            cp = pltpu.make_async_remote_copy(
                src_ref=pb_ref.at[2 * chip[0] + chip[1]], dst_ref=out_ref.at[j], send_sem=send_sems.at[j],
                recv_sem=recv_sems.at[j], device_id=(*chip, c), device_id_type=MESH)
            cp.start()
            cps.append(cp)
        for cp in cps:
            cp.wait()

    return pl.pallas_call(
        body, name="exchange_chips", in_specs=[ANY], out_specs=ANY,
        out_shape=_sds((3, HROWS, LANES), BF),
        scratch_shapes=[pltpu.SemaphoreType.DMA((3,)), pltpu.SemaphoreType.DMA((3,))],
    )(pb)


def _add_chips(k_idx, p, got):
    def body(k_ref, p_ref, got_ref, t_ref):
        t_ref[...] = ((p_ref[...] + got_ref[0].astype(F32)) + got_ref[1].astype(F32)) + got_ref[2].astype(F32)

    return pl.pallas_call(
        body, name="add_chips",
        grid_spec=pltpu.PrefetchScalarGridSpec(
            num_scalar_prefetch=1, grid=(HROWS // ROW_BLOCK,),
            in_specs=[pl.BlockSpec((None, ROW_BLOCK, LANES), lambda r, k_ref: (k_ref[0], r, 0)),
                      pl.BlockSpec((3, ROW_BLOCK, LANES), lambda r, k_ref: (0, r, 0))],
            out_specs=pl.BlockSpec((ROW_BLOCK, LANES), lambda r, k_ref: (r, 0))),
        out_shape=_sds((HROWS, LANES)),
        compiler_params=_params(40),
    )(k_idx, p, got)


def _join_sibling(t_half):
    def body(t_ref, out_ref, send_sem, recv_sem, local_sem):
        x, y, c = _mesh_pos()
        local = pltpu.make_async_copy(t_ref, out_ref.at[c], local_sem)
        local.start()
        cp = pltpu.make_async_remote_copy(src_ref=t_ref, dst_ref=out_ref.at[c], send_sem=send_sem, recv_sem=recv_sem,
                                          device_id=(x, y, 1 - c), device_id_type=MESH)
        cp.start()
        cp.wait_send()
        pltpu.make_async_remote_copy(src_ref=t_ref, dst_ref=out_ref.at[1 - c], send_sem=send_sem, recv_sem=recv_sem,
                                     device_id=(x, y, 1 - c), device_id_type=MESH).wait_recv()
        local.wait()

    return pl.pallas_call(
        body, name="join_sibling", in_specs=[ANY], out_specs=ANY,
        out_shape=_sds((2, HROWS, LANES)),
        scratch_shapes=[pltpu.SemaphoreType.DMA, pltpu.SemaphoreType.DMA, pltpu.SemaphoreType.DMA],
    )(t_half)


def _allreduce_small(vec):
    def body(v_ref, out_ref, sib_ref, chip_ref, send_sems, recv_sems):
        x, y, c = _mesh_pos()
        k0 = 2 * x + y
        sib = (x, y, 1 - c)
        to_sib = pltpu.make_async_remote_copy(src_ref=v_ref, dst_ref=sib_ref, send_sem=send_sems.at[0],
                                              recv_sem=recv_sems.at[0], device_id=sib, device_id_type=MESH)
        to_sib.start()
        to_sib.wait()
        chip_ref[k0] = v_ref[...] + sib_ref[...]
        cps = []
        for j, chip in enumerate(_other_chips(x, y)):
            cp = pltpu.make_async_remote_copy(src_ref=chip_ref.at[k0], dst_ref=chip_ref.at[k0],
                                              send_sem=send_sems.at[1 + j], recv_sem=recv_sems.at[1 + j],
                                              device_id=(*chip, c), device_id_type=MESH)
            cp.start()
            cps.append(cp)
        for j, chip in enumerate(_other_chips(x, y)):
            kj = 2 * chip[0] + chip[1]
            pltpu.make_async_remote_copy(src_ref=chip_ref.at[k0], dst_ref=chip_ref.at[kj],
                                         send_sem=send_sems.at[1 + j], recv_sem=recv_sems.at[1 + j],
                                         device_id=(*chip, c), device_id_type=MESH).wait_recv()
        for cp in cps:
            cp.wait_send()
        out_ref[...] = ((chip_ref[0] + chip_ref[1]) + chip_ref[2]) + chip_ref[3]

    vm = pl.BlockSpec(memory_space=pltpu.VMEM)
    return pl.pallas_call(
        body, name="allreduce_small", in_specs=[vm], out_specs=vm,
        out_shape=_sds((SMALL_ROWS, LANES)),
        scratch_shapes=[pltpu.VMEM((SMALL_ROWS, LANES), F32), pltpu.VMEM((NCHIP, SMALL_ROWS, LANES), F32),
                        pltpu.SemaphoreType.DMA((4,)), pltpu.SemaphoreType.DMA((4,))],
        compiler_params=pltpu.CompilerParams(vmem_limit_bytes=32 * VMEM_MB),
    )(vec)


def _adamw(name, w, g, m, v, rows):
    r_total = w.shape[0]

    def body(w_ref, g_ref, m_ref, v_ref, d_ref, nm_ref, nv_ref):
        g = g_ref[...]
        m = ADAM_B1 * m_ref[...] + (1.0 - ADAM_B1) * g
        v = ADAM_B2 * v_ref[...] + (1.0 - ADAM_B2) * jnp.square(g)
        nm_ref[...] = m
        nv_ref[...] = v
        m_hat = m / (1.0 - ADAM_B1 ** ADAM_STEP)
        v_hat = v / (1.0 - ADAM_B2 ** ADAM_STEP)
        d_ref[...] = -ADAM_LR * (m_hat / (jnp.sqrt(v_hat) + ADAM_EPS) + ADAM_WD * w_ref[...])

    spec = pl.BlockSpec((rows, LANES), lambda i: (i, 0))
    return pl.pallas_call(
        body, name=name, grid=(r_total // rows,), in_specs=[spec] * 4, out_specs=[spec] * 3,
        out_shape=[_sds((r_total, LANES))] * 3, compiler_params=_params(40),
    )(w, g, m, v)


def _pack_rows(arrays, rows):
    flat = jnp.concatenate([a.reshape(-1) for a in arrays])
    return jnp.pad(flat, (0, rows * LANES - flat.shape[0])).reshape(rows, LANES)


def _unpack_rows(packed, shapes):
    flat, out, off = packed.reshape(-1), [], 0
    for shape in shapes:
        n = math.prod(shape)
        out.append(flat[off:off + n].reshape(shape))
        off += n
    return out


def _quarter(name, shape, axis, full, k):
    if axis is None:
        return full[k]
    if axis == 0:
        return full[k * shape[0]:(k + 1) * shape[0]]
    return full[:, k * shape[1]:(k + 1) * shape[1]]


def _whole(name, shape, axis, parts):
    if axis is None:
        return parts
    if axis == 0:
        return parts.reshape(NCHIP * shape[0], shape[1])
    return jnp.transpose(parts, (1, 0, 2)).reshape(shape[0], NCHIP * shape[1])


INPUT_NAMES = (["x", "p"] + [n for n in
               ["g_mix", "w_in", "b_in", "lam_re", "lam_im", "log_dt", "s5_b_re", "s5_b_im", "s5_c_re", "s5_c_im", "s5_d",
                "w_glu", "b_glu", "conv_w", "conv_b", "w_r", "b_r", "w_i", "b_i", "lru_lambda", "w_a_out", "w_b_out", "w_o",
                "g_ffn", "w_ffn_gate", "w_ffn_up", "w_ffn_down", "g_ple_gate", "w_ple_gate", "b_ple_gate", "w_ple", "g_ple",
                "g_final"]])
WEIGHT_NAMES = INPUT_NAMES[2:]


def kernel(*args):
    names = INPUT_NAMES + ["loss_target"] + ["m_" + n for n in WEIGHT_NAMES] + ["v_" + n for n in WEIGHT_NAMES]
    assert len(args) == len(names)
    given = dict(zip(names, args))

    def local(name):
        a = given[name]
        return a if name.endswith("g_final") else a[0]

    xi, yi, ci = _mesh_pos()
    k0 = 2 * xi + yi
    x, p, tgt = given["x"][0], given["p"][0, 0], given["loss_target"][0]

    conv_bits = lax.bitcast_convert_type(local("conv_w"), BF).reshape(-1)
    send = jnp.concatenate([local(n).astype(BF).reshape(-1) for n, _, _ in SHARDED] + [conv_bits])
    send = jnp.pad(send, (0, GROWS * LANES - send.shape[0])).reshape(GROWS, LANES)
    got = _gather_weights(send)
    w, off = {}, 0
    for n, shape, axis in SHARDED:
        rows = shape[0] * shape[1] // LANES
        w[n] = _whole(n, shape, axis, got[:, off:off + rows].reshape((NCHIP,) + shape))
        off += rows
    conv_all = lax.bitcast_convert_type(got[:, QROWS:QROWS + CONV_ROWS].reshape(NCHIP, 4, LW // NCHIP, 2), F32)
    w["conv_w"] = jnp.transpose(conv_all, (1, 0, 2)).reshape(4, LW)
    for n, _ in REPLICATED:
        w[n] = local(n)

    loss, gx, g = _local_step(x, p, tgt, w)

    packed = jnp.stack([_pack_rows([_quarter(n, s, ax, g[n], k) for n, s, ax in SHARDED], QROWS) for k in range(NCHIP)])
    c_idx = jnp.reshape(ci, (1,)).astype(jnp.int32)
    k_idx = jnp.reshape(k0, (1,)).astype(jnp.int32)
    part, part_bf = _add_sibling(c_idx, packed, _swap_sibling_halves(packed))
    total = _join_sibling(_add_chips(k_idx, part, _exchange_chips(part_bf))).reshape(QROWS, LANES)
    shard_names = [n for n, _, _ in SHARDED]
    big = [_pack_rows([local(pre + n) for n in shard_names], QROWS) for pre in ("", "m_", "v_")]
    big_out = [total] + list(_adamw("adamw_sharded", big[0], total, big[1], big[2], QROWS // 12))

    head = jnp.concatenate([g["conv_w"], jnp.full((1, LW), loss / LW, F32), jnp.zeros((SMALL_HEAD - 5, LW), F32)])
    vec = jnp.concatenate([head, _pack_rows([g[n] for n, _ in REPLICATED], SMALL_ROWS - SMALL_HEAD)])
    red = _allreduce_small(vec)
    loss_all = jnp.sum(red[4])
    conv_g = lax.dynamic_slice(red[0:4], (0, k0 * (LW // NCHIP)), (4, LW // NCHIP))

    def small_pack(pre, conv):
        head_rows = jnp.pad(conv.reshape(1, LW), ((0, SMALL_HEAD - 1), (0, 0)))
        return jnp.concatenate([head_rows, _pack_rows([local(pre + n) for n, _ in REPLICATED], SMALL_ROWS - SMALL_HEAD)])

    small_g = jnp.concatenate([jnp.pad(conv_g.reshape(1, LW), ((0, SMALL_HEAD - 1), (0, 0))), red[SMALL_HEAD:]])
    small = [small_pack(pre, local(pre + "conv_w")) for pre in ("", "m_", "v_")]
    small_out = [small_g] + list(_adamw("adamw_replicated", small[0], small_g, small[1], small[2], SMALL_ROWS))

    results = {}
    for kind, packed_big, packed_small in zip(("grad", "delta", "new_m", "new_v"), big_out, small_out):
        for n, arr in zip(shard_names, _unpack_rows(packed_big, [s for _, s, _ in SHARDED])):
            results[kind, n] = arr.reshape(given[n].shape)
        results[kind, "conv_w"] = packed_small[0].reshape(given["conv_w"].shape)
        for (n, _), arr in zip(REPLICATED, _unpack_rows(packed_small[SMALL_HEAD:], [s for _, s in REPLICATED])):
            results[kind, n] = arr.reshape(given[n].shape)
    out = [loss_all, gx[None]]
    for kind in ("grad", "delta", "new_m", "new_v"):
        out += [results[kind, n] for n in WEIGHT_NAMES]
    return tuple(out)
```

```python
import functools
import math

import jax
import jax.numpy as jnp
from jax import lax
from jax.experimental import pallas as pl
from jax.experimental.pallas import tpu as pltpu

F32 = jnp.float32
BF = jnp.bfloat16

D = 1024
S5W = 512
NG, NS, NP = 32, 64, 16
GN = NG * NS
LW = 1024
NH, HD = 16, 64
LRU_C = 8.0
FH = 2816
NCHIP = 4
FC = FH // NCHIP
PLE = 256
INC = S5W + LW + 2 * D
EPS = 1e-6
ADAM_LR, ADAM_B1, ADAM_B2, ADAM_EPS, ADAM_WD, ADAM_STEP = 0.001, 0.9, 0.999, 1e-08, 0.01, 10

TM = 256
TK = 512
LC = 512
SUB = 8
VMEM_MB = 1024 * 1024
MESH = pl.DeviceIdType.MESH
ANY = pl.BlockSpec(memory_space=pl.ANY)


def _mm(a, b):
    return jnp.dot(a.astype(BF), b.astype(BF), preferred_element_type=F32)


def _mm_nt(a, b):
    return lax.dot_general(a.astype(BF), b.astype(BF), (((1,), (1,)), ((), ())), preferred_element_type=F32)


def _mm_tn(a, b):
    return lax.dot_general(a.astype(BF), b.astype(BF), (((0,), (0,)), ((), ())), preferred_element_type=F32)


def _rms(x):
    r = lax.rsqrt(jnp.mean(x * x, axis=-1, keepdims=True) + EPS)
    return x * r, r


def _rms_bwd(dy, xh, r, g):
    dxh = dy * g
    return r * (dxh - xh * jnp.mean(dxh * xh, axis=-1, keepdims=True))


def _colsum(x):
    return jnp.sum(x, axis=0, keepdims=True)


def _sig(x):
    return jax.nn.sigmoid(x)


def _gelu_grad(x):
    c = math.sqrt(2.0 / math.pi)
    t = jnp.tanh(c * (x + 0.044715 * x * x * x))
    return 0.5 * (1.0 + t) + 0.5 * x * (1.0 - t * t) * c * (1.0 + 3.0 * 0.044715 * x * x)


def _neg_expm1(x):
    series = -x * (1.0 + x * (0.5 + x * (1.0 / 6.0 + x * (1.0 / 24.0))))
    return jnp.where(x > -0.03, series, 1.0 - jnp.exp(x))


def _tok(width):
    return pl.BlockSpec((TM, width), lambda i: (i, 0))


def _tok_rev(width, nt):
    return pl.BlockSpec((TM, width), lambda i: (nt - 1 - i, 0))


def _full(shape):
    return pl.BlockSpec(shape, lambda i: (0,) * len(shape))


def _params(vmem_mb, **kw):
    return pltpu.CompilerParams(dimension_semantics=("arbitrary",), vmem_limit_bytes=vmem_mb * VMEM_MB, **kw)


def _sds(shape, dtype=F32):
    return jax.ShapeDtypeStruct(shape, dtype)


def _row_iota(width):
    return lax.broadcasted_iota(jnp.int32, (SUB, width), 0)


def _bcast_row(x, row):
    return jnp.broadcast_to(x[row:row + 1, :], x.shape)


def _slab(k):
    return pl.ds(pl.multiple_of(k * SUB, SUB), SUB)


QC = INC // NCHIP
Z_PARTS = ((0, S5W), (S5W, S5W + LW), (S5W + LW, INC))


def _inproj_fwd(x, g_mix, w_in, b_in):
    L = x.shape[0]

    def body(x_ref, g_ref, w_hbm, b_ref, h_ref, ua_ref, ub_ref, gp_ref, w_vm):
        @pl.when(pl.program_id(0) == 0)
        def _():
            pltpu.sync_copy(w_hbm, w_vm)

        xh, _ = _rms(x_ref[...])
        h = (xh * g_ref[...]).astype(BF)
        h_ref[...] = h
        for k in range(NCHIP):
            lo, hi = k * QC, (k + 1) * QC
            z = jnp.dot(h, w_vm[k], preferred_element_type=F32) + b_ref[:, lo:hi]
            for ref, (a, b) in zip((ua_ref, ub_ref, gp_ref), Z_PARTS):
                s, e = max(lo, a), min(hi, b)
                if s < e:
                    ref[:, s - a:e - a] = z[:, s - lo:e - lo]

    return pl.pallas_call(
        body, name="inproj_fwd", grid=(L // TM,),
        in_specs=[_tok(D), _full((1, D)), ANY, _full((1, INC))],
        out_specs=[_tok(D), _tok(S5W), _tok(LW), _tok(2 * D)],
        out_shape=[_sds((L, D), BF), _sds((L, S5W)), _sds((L, LW)), _sds((L, 2 * D))],
        scratch_shapes=[pltpu.VMEM((NCHIP, D, QC), BF)],
        compiler_params=_params(40),
    )(x, g_mix, w_in, b_in)


def _inproj_bwd(x, dx1, dua, dub, dgp, g_mix, w_in):
    L = x.shape[0]

    def body(x_ref, dx1_ref, dua_ref, dub_ref, dgp_ref, g_ref, w_hbm, gx_ref, dz_ref, dg_ref, db_ref, w_vm):
        @pl.when(pl.program_id(0) == 0)
        def _():
            pltpu.sync_copy(w_hbm, w_vm)
            dg_ref[...] = jnp.zeros_like(dg_ref)
            db_ref[...] = jnp.zeros_like(db_ref)

        for src, (a, b) in zip((dua_ref, dub_ref, dgp_ref), Z_PARTS):
            d = src[...]
            dz_ref[:, a:b] = d.astype(BF)
            db_ref[0:1, a:b] += _colsum(d)
        dh = jnp.zeros((TM, D), F32)
        for k in range(NCHIP):
            dh = dh + lax.dot_general(dz_ref[:, k * QC:(k + 1) * QC], w_vm[k], (((1,), (1,)), ((), ())),
                                      preferred_element_type=F32)
        xh, r = _rms(x_ref[...])
        dg_ref[0:1, :] += _colsum(dh * xh)
        gx_ref[...] = dx1_ref[...] + _rms_bwd(dh, xh, r, g_ref[...])

    return pl.pallas_call(
        body, name="inproj_bwd", grid=(L // TM,),
        in_specs=[_tok(D), _tok(D), _tok(S5W), _tok(LW), _tok(2 * D), _full((1, D)), ANY],
        out_specs=[_tok(D), _tok(INC), _full((SUB, D)), _full((SUB, INC))],
        out_shape=[_sds((L, D)), _sds((L, INC), BF), _sds((SUB, D)), _sds((SUB, INC))],
        scratch_shapes=[pltpu.VMEM((NCHIP, D, QC), BF)],
        compiler_params=_params(40),
    )(x, dx1, dua, dub, dgp, g_mix, w_in)


def _cscan(xr_ref, xi_ref, con_ref, cr_ref, ci_ref, reverse):
    n_slab = xr_ref.shape[0] // SUB
    width = xr_ref.shape[1]
    for lc in range(width // LC):
        cols = slice(lc * LC, (lc + 1) * LC)
        con = [con_ref[SUB * j:SUB * (j + 1), cols] for j in range(8)]

        def step(k, carry, cols=cols, con=con):
            cr, ci = carry
            rows = _slab(n_slab - 1 - k if reverse else k)
            xr, xi = xr_ref[rows, cols], xi_ref[rows, cols]
            for j, sh in enumerate((1, 2, 4)):
                mr, mi = con[2 * j], con[2 * j + 1]
                pr = pltpu.roll(xr, SUB - sh if reverse else sh, 0)
                pi = pltpu.roll(xi, SUB - sh if reverse else sh, 0)
                xr, xi = xr + mr * pr - mi * pi, xi + mr * pi + mi * pr
            xr, xi = xr + con[6] * cr - con[7] * ci, xi + con[6] * ci + con[7] * cr
            xr_ref[rows, cols] = xr
            xi_ref[rows, cols] = xi
            row = 0 if reverse else SUB - 1
            return _bcast_row(xr, row), _bcast_row(xi, row)

        cr, ci = lax.fori_loop(0, n_slab, step, (cr_ref[:, cols], ci_ref[:, cols]))
        cr_ref[:, cols] = cr
        ci_ref[:, cols] = ci


def _s5_fwd(ua, bbr, bbi, ccr, cci, dsk, con, w_glu, b_glu):
    L = ua.shape[0]

    def body(ua_ref, bbr_hbm, bbi_hbm, ccr_hbm, cci_hbm, dsk_ref, con_ref, wg_ref, bg_ref,
             sr_ref, si_ref, y_ref, zg_ref, ya_ref, bbr_vm, bbi_vm, ccr_vm, cci_vm, cr_ref, ci_ref):
        @pl.when(pl.program_id(0) == 0)
        def _():
            pltpu.sync_copy(bbr_hbm, bbr_vm)
            pltpu.sync_copy(bbi_hbm, bbi_vm)
            pltpu.sync_copy(ccr_hbm, ccr_vm)
            pltpu.sync_copy(cci_hbm, cci_vm)
            cr_ref[...] = jnp.zeros_like(cr_ref)
            ci_ref[...] = jnp.zeros_like(ci_ref)

        u = ua_ref[...]
        ub = u.astype(BF)
        sr_ref[...] = jnp.dot(ub, bbr_vm[...], preferred_element_type=F32)
        si_ref[...] = jnp.dot(ub, bbi_vm[...], preferred_element_type=F32)
        _cscan(sr_ref, si_ref, con_ref, cr_ref, ci_ref, reverse=False)
        y = _mm(sr_ref[...], ccr_vm[...]) - _mm(si_ref[...], cci_vm[...]) + dsk_ref[...] * u
        y_ref[...] = y
        zg = jax.nn.gelu(y)
        zg_ref[...] = zg.astype(BF)
        q = _mm(zg, wg_ref[...]) + bg_ref[...]
        ya_ref[...] = (zg * _sig(q)).astype(BF)

    return pl.pallas_call(
        body, name="s5_fwd", grid=(L // TM,),
        in_specs=[_tok(S5W), ANY, ANY, ANY, ANY, _full((1, S5W)), _full((8 * SUB, GN)),
                  _full((S5W, S5W)), _full((1, S5W))],
        out_specs=[_tok(GN), _tok(GN), _tok(S5W), _tok(S5W), _tok(S5W)],
        out_shape=[_sds((L, GN)), _sds((L, GN)), _sds((L, S5W)), _sds((L, S5W), BF), _sds((L, S5W), BF)],
        scratch_shapes=[pltpu.VMEM((S5W, GN), BF), pltpu.VMEM((S5W, GN), BF), pltpu.VMEM((GN, S5W), BF),
                        pltpu.VMEM((GN, S5W), BF), pltpu.VMEM((SUB, GN), F32), pltpu.VMEM((SUB, GN), F32)],
        compiler_params=_params(44),
    )(ua, bbr, bbi, ccr, cci, dsk, con, w_glu, b_glu)


def _s5_bwd(dya, y, ua, sr, si, bbr, bbi, ccr, cci, dsk, con_rev, w_glu, b_glu):
    L = ua.shape[0]
    nt = L // TM
    spt = TM // SUB
    n_slab = spt

    def halo_map(i):
        return (jnp.maximum((nt - 1 - i) * spt - 1, 0), 0)

    def body(dya_ref, y_ref, ua_ref, sr_ref, si_ref, hr_ref, hi_ref, bbr_hbm, bbi_hbm, ccr_hbm, cci_hbm,
             dsk_ref, con_ref, wg_ref, bg_ref,
             dua_ref, dq_ref, dy_ref, lr_ref, li_ref, da_ref, dsm_ref,
             bbr_vm, bbi_vm, ccr_vm, cci_vm, cr_ref, ci_ref):
        i = pl.program_id(0)

        @pl.when(i == 0)
        def _():
            pltpu.sync_copy(bbr_hbm, bbr_vm)
            pltpu.sync_copy(bbi_hbm, bbi_vm)
            pltpu.sync_copy(ccr_hbm, ccr_vm)
            pltpu.sync_copy(cci_hbm, cci_vm)
            cr_ref[...] = jnp.zeros_like(cr_ref)
            ci_ref[...] = jnp.zeros_like(ci_ref)
            da_ref[...] = jnp.zeros_like(da_ref)
            dsm_ref[...] = jnp.zeros_like(dsm_ref)

        u = ua_ref[...]
        yv = y_ref[...]
        dya = dya_ref[...]
        zg = jax.nn.gelu(yv)
        sg = _sig(_mm(zg, wg_ref[...]) + bg_ref[...])
        dq = dya * zg * sg * (1.0 - sg)
        dq_ref[...] = dq.astype(BF)
        dzg = dya * sg + _mm_nt(dq, wg_ref[...])
        dy = dzg * _gelu_grad(yv)
        dyb = dy.astype(BF)
        dy_ref[...] = dyb
        dsm_ref[0:1, :] += _colsum(dy * u)
        dsm_ref[1:2, :] += _colsum(dq)
        lr_ref[...] = lax.dot_general(dyb, ccr_vm[...], (((1,), (1,)), ((), ())), preferred_element_type=F32)
        li_ref[...] = -lax.dot_general(dyb, cci_vm[...], (((1,), (1,)), ((), ())), preferred_element_type=F32)
        _cscan(lr_ref, li_ref, con_ref, cr_ref, ci_ref, reverse=True)

        first_tile = (i == nt - 1)
        row = _row_iota(LC)
        for lc in range(GN // LC):
            cols = slice(lc * LC, (lc + 1) * LC)
            h_r = jnp.where(first_tile, 0.0, hr_ref[:, cols])
            h_i = jnp.where(first_tile, 0.0, hi_ref[:, cols])

            def step(k, acc, cols=cols, h_r=h_r, h_i=h_i):
                ar, ai = acc
                rows = _slab(k)
                prev = _slab(jnp.maximum(k - 1, 0))
                pr = jnp.where(k == 0, h_r, sr_ref[prev, cols])
                pi = jnp.where(k == 0, h_i, si_ref[prev, cols])
                spr = pltpu.roll(jnp.where(row == SUB - 1, pr, sr_ref[rows, cols]), 1, 0)
                spi = pltpu.roll(jnp.where(row == SUB - 1, pi, si_ref[rows, cols]), 1, 0)
                lr, li = lr_ref[rows, cols], li_ref[rows, cols]
                return ar + lr * spr + li * spi, ai + li * spr - lr * spi

            zero = jnp.zeros((SUB, LC), F32)
            ar, ai = lax.fori_loop(0, n_slab, step, (zero, zero))
            da_ref[0:1, cols] += _colsum(ar)
            da_ref[1:2, cols] += _colsum(ai)

        dua_ref[...] = (dy * dsk_ref[...] + _mm_nt(lr_ref[...], bbr_vm[...]) + _mm_nt(li_ref[...], bbi_vm[...]))

    return pl.pallas_call(
        body, name="s5_bwd", grid=(nt,),
        in_specs=[_tok_rev(S5W, nt), _tok_rev(S5W, nt), _tok_rev(S5W, nt), _tok_rev(GN, nt), _tok_rev(GN, nt),
                  pl.BlockSpec((SUB, GN), halo_map), pl.BlockSpec((SUB, GN), halo_map),
                  ANY, ANY, ANY, ANY, _full((1, S5W)), _full((8 * SUB, GN)), _full((S5W, S5W)), _full((1, S5W))],
        out_specs=[_tok_rev(S5W, nt), _tok_rev(S5W, nt), _tok_rev(S5W, nt), _tok_rev(GN, nt), _tok_rev(GN, nt),
                   _full((SUB, GN)), _full((SUB, S5W))],
        out_shape=[_sds((L, S5W)), _sds((L, S5W), BF), _sds((L, S5W), BF), _sds((L, GN)), _sds((L, GN)),
                   _sds((SUB, GN)), _sds((SUB, S5W))],
        scratch_shapes=[pltpu.VMEM((S5W, GN), BF), pltpu.VMEM((S5W, GN), BF), pltpu.VMEM((GN, S5W), BF),
                        pltpu.VMEM((GN, S5W), BF), pltpu.VMEM((SUB, GN), F32), pltpu.VMEM((SUB, GN), F32)],
        compiler_params=_params(52),
    )(dya, y, ua, sr, si, sr, si, bbr, bbi, ccr, cci, dsk, con_rev, w_glu, b_glu)


def _lru_gate_terms(rg, sp):
    log_a = -LRU_C * rg * sp
    a = jnp.exp(log_a)
    mult = jnp.sqrt(_neg_expm1(2.0 * log_a))
    return a, mult


def _lru_fwd(ub, conv_w, conv_b, wr, wi, b_r, b_i, sp):
    L = ub.shape[0]
    n_slab = TM // SUB

    def body(ub_ref, cw_ref, cb_ref, wr_ref, wi_ref, br_ref, bi_ref, sp_ref,
             xc_ref, rg_ref, ig_ref, h_ref, hp_ref, a_ref, halo_ref, carry_ref):
        @pl.when(pl.program_id(0) == 0)
        def _():
            halo_ref[...] = jnp.zeros_like(halo_ref)
            carry_ref[...] = jnp.zeros_like(carry_ref)

        row = _row_iota(LW)
        taps = [cw_ref[k:k + 1, :] for k in range(4)]
        cb = cb_ref[...]

        def conv_step(k, prev):
            rows = _slab(k)
            cur = ub_ref[rows, :]
            acc = taps[3] * cur + cb
            for j in (1, 2, 3):
                acc = acc + taps[3 - j] * pltpu.roll(jnp.where(row >= SUB - j, prev, cur), j, 0)
            xc_ref[rows, :] = acc
            return cur

        halo_ref[...] = lax.fori_loop(0, n_slab, conv_step, halo_ref[...])

        xc = xc_ref[...]
        xcb = xc.astype(BF)
        rg = _sig(jnp.dot(xcb, wr_ref[...], preferred_element_type=F32) + br_ref[...])
        ig = _sig(jnp.dot(xcb, wi_ref[...], preferred_element_type=F32) + bi_ref[...])
        rg_ref[...] = rg
        ig_ref[...] = ig
        a, mult = _lru_gate_terms(rg, sp_ref[...])
        a_ref[...] = a
        h_ref[...] = mult * ig * xc

        rowc = _row_iota(LC)
        for lc in range(LW // LC):
            cols = slice(lc * LC, (lc + 1) * LC)

            def step(k, c, cols=cols):
                rows = _slab(k)
                av, b = a_ref[rows, cols], h_ref[rows, cols]
                for sh in (1, 2, 4):
                    keep = rowc >= sh
                    b = b + av * jnp.where(keep, pltpu.roll(b, sh, 0), 0.0)
                    av = av * jnp.where(keep, pltpu.roll(av, sh, 0), 1.0)
                h = b + av * c
                h_ref[rows, cols] = h
                hp_ref[rows, cols] = jnp.where(rowc == 0, c, pltpu.roll(h, 1, 0))
                return _bcast_row(h, SUB - 1)

            carry_ref[:, cols] = lax.fori_loop(0, n_slab, step, carry_ref[:, cols])

    return pl.pallas_call(
        body, name="lru_fwd", grid=(L // TM,),
        in_specs=[_tok(LW), _full((4, LW)), _full((1, LW)), _full((LW, LW)), _full((LW, LW)),
                  _full((1, LW)), _full((1, LW)), _full((1, LW))],
        out_specs=[_tok(LW)] * 5,
        out_shape=[_sds((L, LW))] * 5,
        scratch_shapes=[pltpu.VMEM((TM, LW), F32), pltpu.VMEM((SUB, LW), F32), pltpu.VMEM((SUB, LW), F32)],
        compiler_params=_params(40),
    )(ub, conv_w, conv_b, wr, wi, b_r, b_i, sp)


def _lru_bwd(dyb, xc, rg, ig, hp, ub, conv_w, wr, wi, sp, dsp):
    L = ub.shape[0]
    nt = L // TM
    spt = TM // SUB
    n_slab = spt

    def halo_map(i):
        return (jnp.maximum((nt - 1 - i) * spt - 1, 0), 0)

    def body(dh_ref, xc_ref, rg_ref, ig_ref, hp_ref, ub_ref, uh_ref, cw_ref, wr_ref, wi_ref, sp_ref, dsp_ref,
             dub_ref, dpr_ref, dpi_ref, acc_ref, a_ref, lam_ref, dxc_ref, carry_ref, next_ref):
        i = pl.program_id(0)

        @pl.when(i == 0)
        def _():
            carry_ref[...] = jnp.zeros_like(carry_ref)
            next_ref[...] = jnp.zeros_like(next_ref)
            acc_ref[...] = jnp.zeros_like(acc_ref)

        sp = sp_ref[...]
        rg, ig, xc = rg_ref[...], ig_ref[...], xc_ref[...]
        a, mult = _lru_gate_terms(rg, sp)
        a_ref[...] = a

        rowc = _row_iota(LC)
        for lc in range(LW // LC):
            cols = slice(lc * LC, (lc + 1) * LC)

            def step(k, c, cols=cols):
                rows = _slab(n_slab - 1 - k)
                av, dh = a_ref[rows, cols], dh_ref[rows, cols]
                b = av * dh
                for sh in (1, 2, 4):
                    keep = rowc < SUB - sh
                    b = b + av * jnp.where(keep, pltpu.roll(b, SUB - sh, 0), 0.0)
                    av = av * jnp.where(keep, pltpu.roll(av, SUB - sh, 0), 1.0)
                mu = b + av * c
                lam_ref[rows, cols] = dh + jnp.where(rowc == SUB - 1, c, pltpu.roll(mu, SUB - 1, 0))
                return _bcast_row(mu, 0)

            carry_ref[:, cols] = lax.fori_loop(0, n_slab, step, carry_ref[:, cols])

        lam = lam_ref[...]
        d_a = lam * hp_ref[...]
        d_mult = lam * ig * xc
        d_ig = lam * mult * xc
        dxc = lam * mult * ig
        d_log_a = d_a * a - d_mult * a * a / mult
        d_rg = (-LRU_C) * sp * d_log_a
        acc_ref[0:1, :] += _colsum((-LRU_C) * rg * d_log_a) * dsp_ref[...]
        dpr = d_rg * rg * (1.0 - rg)
        dpi = d_ig * ig * (1.0 - ig)
        acc_ref[1:2, :] += _colsum(dpr)
        acc_ref[2:3, :] += _colsum(dpi)
        dprb, dpib = dpr.astype(BF), dpi.astype(BF)
        dpr_ref[...] = dprb
        dpi_ref[...] = dpib
        dxc = dxc + _mm_nt(dprb, wr_ref[...]) + _mm_nt(dpib, wi_ref[...])
        dxc_ref[...] = dxc
        acc_ref[3:4, :] += _colsum(dxc)

        row = _row_iota(LW)
        taps = [cw_ref[k:k + 1, :] for k in range(4)]
        u_halo = jnp.where(i == nt - 1, 0.0, uh_ref[...])
        nxt_tile = next_ref[...]

        def conv_step(k, accs):
            rows = _slab(k)
            cur = dxc_ref[rows, :]
            nxt = jnp.where(k == n_slab - 1, nxt_tile, dxc_ref[_slab(jnp.minimum(k + 1, n_slab - 1)), :])
            ucur = ub_ref[rows, :]
            uprev = jnp.where(k == 0, u_halo, ub_ref[_slab(jnp.maximum(k - 1, 0)), :])
            du = taps[3] * cur
            new = [accs[3] + cur * ucur]
            for j in (1, 2, 3):
                du = du + taps[3 - j] * pltpu.roll(jnp.where(row < j, nxt, cur), SUB - j, 0)
                new.append(accs[3 - j] + cur * pltpu.roll(jnp.where(row >= SUB - j, uprev, ucur), j, 0))
            dub_ref[rows, :] = du
            return tuple(new[::-1])

        zero = jnp.zeros((SUB, LW), F32)
        accs = lax.fori_loop(0, n_slab, conv_step, (zero, zero, zero, zero))
        for k in range(4):
            acc_ref[4 + k:5 + k, :] += _colsum(accs[k])
        next_ref[...] = dxc_ref[0:SUB, :]

    return pl.pallas_call(
        body, name="lru_bwd", grid=(nt,),
        in_specs=[_tok_rev(LW, nt)] * 6 + [pl.BlockSpec((SUB, LW), halo_map), _full((4, LW)),
                                           _full((LW, LW)), _full((LW, LW)), _full((1, LW)), _full((1, LW))],
        out_specs=[_tok_rev(LW, nt), _tok_rev(LW, nt), _tok_rev(LW, nt), _full((SUB, LW))],
        out_shape=[_sds((L, LW)), _sds((L, LW), BF), _sds((L, LW), BF), _sds((SUB, LW))],
        scratch_shapes=[pltpu.VMEM((TM, LW), F32), pltpu.VMEM((TM, LW), F32), pltpu.VMEM((TM, LW), F32),
                        pltpu.VMEM((SUB, LW), F32), pltpu.VMEM((SUB, LW), F32)],
        compiler_params=_params(48),
    )(dyb, xc, rg, ig, hp, ub, ub, conv_w, wr, wi, sp, dsp)


AC = D // NCHIP


def _merge_fwd(x, ya, yb, gp, w_a, w_b, w_o):
    L = x.shape[0]

    def body(x_ref, ya_ref, yb_ref, gp_ref, wa_ref, wb_ref, wo_ref, x1_ref, pa_ref, pb_ref, mg_ref):
        ya = ya_ref[...]
        for k in range(NCHIP):
            pa_ref[:, k * AC:(k + 1) * AC] = jnp.dot(ya, wa_ref[k], preferred_element_type=F32)
        pb = _mm(yb_ref[...], wb_ref[...])
        pb_ref[...] = pb
        gp = gp_ref[...]
        merged = (_sig(gp[:, :D]) * pa_ref[...] + _sig(gp[:, D:]) * pb).astype(BF)
        mg_ref[...] = merged
        x1_ref[...] = x_ref[...] + jnp.dot(merged, wo_ref[...], preferred_element_type=F32)

    return pl.pallas_call(
        body, name="merge_fwd", grid=(L // TM,),
        in_specs=[_tok(D), _tok(S5W), _tok(LW), _tok(2 * D), _full((NCHIP, S5W, AC)), _full((LW, D)), _full((D, D))],
        out_specs=[_tok(D), _tok(D), _tok(D), _tok(D)],
        out_shape=[_sds((L, D)), _sds((L, D)), _sds((L, D)), _sds((L, D), BF)],
        compiler_params=_params(40),
    )(x, ya, yb, gp, w_a, w_b, w_o)


def _merge_bwd(dx1, gp, pa, pb, w_a, w_b, w_o):
    L = dx1.shape[0]

    def body(dx1_ref, gp_ref, pa_ref, pb_ref, wa_ref, wb_ref, wo_ref, dya_ref, dyb_ref, dgp_ref, dpa_ref, dpb_ref):
        dm = _mm_nt(dx1_ref[...], wo_ref[...])
        gp = gp_ref[...]
        sa, sb = _sig(gp[:, :D]), _sig(gp[:, D:])
        dpa = (dm * sa).astype(BF)
        dpb = (dm * sb).astype(BF)
        dpa_ref[...] = dpa
        dpb_ref[...] = dpb
        dgp_ref[:, :D] = dm * pa_ref[...] * sa * (1.0 - sa)
        dgp_ref[:, D:] = dm * pb_ref[...] * sb * (1.0 - sb)
        dya = jnp.zeros((TM, S5W), F32)
        for k in range(NCHIP):
            dya = dya + _mm_nt(dpa[:, k * AC:(k + 1) * AC], wa_ref[k])
        dya_ref[...] = dya
        dyb_ref[...] = _mm_nt(dpb, wb_ref[...])

    return pl.pallas_call(
        body, name="merge_bwd", grid=(L // TM,),
        in_specs=[_tok(D), _tok(2 * D), _tok(D), _tok(D), _full((NCHIP, S5W, AC)), _full((LW, D)), _full((D, D))],
        out_specs=[_tok(S5W), _tok(LW), _tok(2 * D), _tok(D), _tok(D)],
        out_shape=[_sds((L, S5W)), _sds((L, LW)), _sds((L, 2 * D)), _sds((L, D), BF), _sds((L, D), BF)],
        compiler_params=_params(40),
    )(dx1, gp, pa, pb, w_a, w_b, w_o)


def _chunk_tok(width):
    return pl.BlockSpec((NCHIP, TM, width), lambda i: (0, i, 0))


def _ffn_fwd(x1, g_ffn, wg, wu, wd):
    L = x1.shape[0]

    def body(x_ref, g_ref, wg_hbm, wu_hbm, wd_hbm, x2_ref, h2_ref, gg_ref, uu_ref, wg_vm, wu_vm, wd_vm):
        @pl.when(pl.program_id(0) == 0)
        def _():
            pltpu.sync_copy(wg_hbm, wg_vm)
            pltpu.sync_copy(wu_hbm, wu_vm)
            pltpu.sync_copy(wd_hbm, wd_vm)

        x = x_ref[...]
        xh, _ = _rms(x)
        h2 = (xh * g_ref[...]).astype(BF)
        h2_ref[...] = h2
        out = x
        for c in range(NCHIP):
            gg = jnp.dot(h2, wg_vm[c], preferred_element_type=F32)
            uu = jnp.dot(h2, wu_vm[c], preferred_element_type=F32)
            gg_ref[c] = gg.astype(BF)
            uu_ref[c] = uu.astype(BF)
            act = (gg * _sig(gg) * uu).astype(BF)
            out = out + jnp.dot(act, wd_vm[c], preferred_element_type=F32)
        x2_ref[...] = out

    return pl.pallas_call(
        body, name="ffn_fwd", grid=(L // TM,),
        in_specs=[_tok(D), _full((1, D)), ANY, ANY, ANY],
        out_specs=[_tok(D), _tok(D), _chunk_tok(FC), _chunk_tok(FC)],
        out_shape=[_sds((L, D)), _sds((L, D), BF), _sds((NCHIP, L, FC), BF), _sds((NCHIP, L, FC), BF)],
        scratch_shapes=[pltpu.VMEM((NCHIP, D, FC), BF), pltpu.VMEM((NCHIP, D, FC), BF), pltpu.VMEM((NCHIP, FC, D), BF)],
        compiler_params=_params(52),
    )(x1, g_ffn, wg, wu, wd)


def _ffn_bwd(x1, dx2, gg, uu, g_ffn, wg, wu, wd):
    L = x1.shape[0]

    def body(x_ref, dx2_ref, gg_ref, uu_ref, g_ref, wg_hbm, wu_hbm, wd_hbm,
             dx1_ref, act_ref, dgg_ref, duu_ref, dg_ref, wg_vm, wu_vm, wd_vm):
        @pl.when(pl.program_id(0) == 0)
        def _():
            pltpu.sync_copy(wg_hbm, wg_vm)
            pltpu.sync_copy(wu_hbm, wu_vm)
            pltpu.sync_copy(wd_hbm, wd_vm)
            dg_ref[...] = jnp.zeros_like(dg_ref)

        dx2 = dx2_ref[...]
        dx2b = dx2.astype(BF)
        dh2 = jnp.zeros((TM, D), F32)
        for c in range(NCHIP):
            g = gg_ref[c].astype(F32)
            u = uu_ref[c].astype(F32)
            s = _sig(g)
            silu = g * s
            act_ref[c] = (silu * u).astype(BF)
            dact = lax.dot_general(dx2b, wd_vm[c], (((1,), (1,)), ((), ())), preferred_element_type=F32)
            dg = (dact * u * s * (1.0 + g * (1.0 - s))).astype(BF)
            du = (dact * silu).astype(BF)
            dgg_ref[c] = dg
            duu_ref[c] = du
            dh2 = dh2 + lax.dot_general(dg, wg_vm[c], (((1,), (1,)), ((), ())), preferred_element_type=F32)
            dh2 = dh2 + lax.dot_general(du, wu_vm[c], (((1,), (1,)), ((), ())), preferred_element_type=F32)
        xh, r = _rms(x_ref[...])
        dg_ref[0:1, :] += _colsum(dh2 * xh)
        dx1_ref[...] = dx2 + _rms_bwd(dh2, xh, r, g_ref[...])

    return pl.pallas_call(
        body, name="ffn_bwd", grid=(L // TM,),
        in_specs=[_tok(D), _tok(D), _chunk_tok(FC), _chunk_tok(FC), _full((1, D)), ANY, ANY, ANY],
        out_specs=[_tok(D), _chunk_tok(FC), _chunk_tok(FC), _chunk_tok(FC), _full((SUB, D))],
        out_shape=[_sds((L, D)), _sds((NCHIP, L, FC), BF), _sds((NCHIP, L, FC), BF), _sds((NCHIP, L, FC), BF),
                   _sds((SUB, D))],
        scratch_shapes=[pltpu.VMEM((NCHIP, D, FC), BF), pltpu.VMEM((NCHIP, D, FC), BF), pltpu.VMEM((NCHIP, FC, D), BF)],
        compiler_params=_params(56),
    )(x1, dx2, gg, uu, g_ffn, wg, wu, wd)


def _ple_loss(x2, p, tgt, g_pg, w_pg, b_pg, w_ple, g_ple, g_final):
    L = x2.shape[0]

    def body(x2_ref, p_ref, t_ref, gpg_ref, wpg_ref, bpg_ref, wple_ref, gple_ref, gf_ref,
             dx2_ref, n2_ref, dpre_ref, de0_ref, acc_ref):
        @pl.when(pl.program_id(0) == 0)
        def _():
            acc_ref[...] = jnp.zeros_like(acc_ref)

        x2 = x2_ref[...]
        x2h, r2 = _rms(x2)
        n2 = (x2h * gpg_ref[...]).astype(BF)
        n2_ref[...] = n2
        gate = _sig(jnp.dot(n2, wpg_ref[...], preferred_element_type=F32) + bpg_ref[...])
        pb = p_ref[...].astype(BF)
        e0 = jnp.concatenate([jnp.dot(pb, wple_ref[k], preferred_element_type=F32) for k in range(NCHIP)], axis=1)
        e0h, re = _rms(e0)
        e = e0h * gple_ref[...]
        x3 = x2 + gate * e
        x3h, r3 = _rms(x3)
        diff = x3h * gf_ref[...] - t_ref[...]
        acc_ref[4:5, :] += _colsum(diff * diff) * (0.5 / D)
        dy = diff * (1.0 / D)
        acc_ref[3:4, :] += _colsum(dy * x3h)
        dx3 = _rms_bwd(dy, x3h, r3, gf_ref[...])
        de = dx3 * gate
        acc_ref[2:3, :] += _colsum(de * e0h)
        de0_ref[...] = _rms_bwd(de, e0h, re, gple_ref[...]).astype(BF)
        dpre = dx3 * e * gate * (1.0 - gate)
        acc_ref[1:2, :] += _colsum(dpre)
        dpreb = dpre.astype(BF)
        dpre_ref[...] = dpreb
        dn2 = lax.dot_general(dpreb, wpg_ref[...], (((1,), (1,)), ((), ())), preferred_element_type=F32)
        acc_ref[0:1, :] += _colsum(dn2 * x2h)
        dx2_ref[...] = dx3 + _rms_bwd(dn2, x2h, r2, gpg_ref[...])

    return pl.pallas_call(
        body, name="ple_loss", grid=(L // TM,),
        in_specs=[_tok(D), _tok(PLE), _tok(D), _full((1, D)), _full((D, D)), _full((1, D)), _full((NCHIP, PLE, AC)),
                  _full((1, D)), _full((1, D))],
        out_specs=[_tok(D), _tok(D), _tok(D), _tok(D), _full((SUB, D))],
        out_shape=[_sds((L, D)), _sds((L, D), BF), _sds((L, D), BF), _sds((L, D), BF), _sds((SUB, D))],
        compiler_params=_params(40),
    )(x2, p, tgt, g_pg, w_pg, b_pg, w_ple, g_ple, g_final)


def _tn(name, a, b, col_chunk=None):
    L = a.shape[-2]
    m, n = a.shape[-1], b.shape[-1]
    if a.ndim == 3 or b.ndim == 3:
        nj, bn = (a if a.ndim == 3 else b).shape[0], n
        a_spec = (pl.BlockSpec((None, TK, m), lambda j, t: (j, t, 0)) if a.ndim == 3
                  else pl.BlockSpec((TK, m), lambda j, t: (t, 0)))
        b_spec = (pl.BlockSpec((None, TK, n), lambda j, t: (j, t, 0)) if b.ndim == 3
                  else pl.BlockSpec((TK, n), lambda j, t: (t, 0)))
        out_spec, out_shape = pl.BlockSpec((None, m, n), lambda j, t: (j, 0, 0)), _sds((nj, m, n))
    else:
        bn = col_chunk
        if bn is None:
            bn = next((cand for cand in (1024, 512) if n > cand and n % cand == 0), n)
        nj = n // bn
        a_spec = pl.BlockSpec((TK, m), lambda j, t: (t, 0))
        b_spec = pl.BlockSpec((TK, bn), lambda j, t: (t, j))
        if col_chunk is None:
            out_spec, out_shape = pl.BlockSpec((m, bn), lambda j, t: (0, j)), _sds((m, n))
        else:
            out_spec, out_shape = pl.BlockSpec((None, m, bn), lambda j, t: (j, 0, 0)), _sds((nj, m, bn))

    def body(a_ref, b_ref, o_ref):
        @pl.when(pl.program_id(1) == 0)
        def _():
            o_ref[...] = jnp.zeros_like(o_ref)

        o_ref[...] += _mm_tn(a_ref[...], b_ref[...])

    return pl.pallas_call(
        body, name=name, grid=(nj, L // TK), in_specs=[a_spec, b_spec], out_specs=out_spec, out_shape=out_shape,
        compiler_params=pltpu.CompilerParams(dimension_semantics=("arbitrary", "arbitrary"),
                                             vmem_limit_bytes=40 * VMEM_MB),
    )(a, b)


def _s5_discretize(lam_re, lam_im, log_dt, b_re, b_im):
    dt = jnp.exp(log_dt)[:, None]
    mag = jnp.exp(lam_re * dt)
    ar = mag * jnp.cos(lam_im * dt)
    ai = mag * jnp.sin(lam_im * dt)
    den = lam_re * lam_re + lam_im * lam_im
    nr = ar - 1.0
    fr = (nr * lam_re + ai * lam_im) / den
    fi = (ai * lam_re - nr * lam_im) / den
    bbr = fr[..., None] * b_re - fi[..., None] * b_im
    bbi = fr[..., None] * b_im + fi[..., None] * b_re
    return ar, ai, bbr, bbi


def _scan_constants(ar, ai):
    ar, ai = ar.reshape(1, GN), ai.reshape(1, GN)
    pw = [(jnp.ones_like(ar), jnp.zeros_like(ai))]
    for _ in range(SUB):
        pr, pi = pw[-1]
        pw.append((pr * ar - pi * ai, pr * ai + pi * ar))
    row = lax.broadcasted_iota(jnp.int32, (SUB, GN), 0)

    def build(reverse):
        sign = -1.0 if reverse else 1.0
        blocks = []
        for sh in (1, 2, 4):
            keep = (row < SUB - sh) if reverse else (row >= sh)
            blocks += [jnp.where(keep, pw[sh][0], 0.0), jnp.where(keep, sign * pw[sh][1], 0.0)]
        order = [SUB - i for i in range(SUB)] if reverse else [i + 1 for i in range(SUB)]
        blocks += [jnp.concatenate([pw[k][0] for k in order], 0), jnp.concatenate([sign * pw[k][1] for k in order], 0)]
        return jnp.concatenate(blocks, 0)

    return build(False), build(True)


def _blockdiag(blocks):
    g, r, c = blocks.shape
    eye = jnp.eye(g, dtype=blocks.dtype)
    return (blocks[:, :, None, :] * eye[:, None, :, None]).reshape(g * r, g * c)


def _diag_blocks(dense, g):
    r, c = dense.shape[0] // g, dense.shape[1] // g
    d4 = dense.reshape(g, r, g, c)
    idx = jnp.arange(g)
    return d4[idx, :, idx, :]


def _local_step(x, p, tgt, w):
    rows_of = lambda a: a.reshape(NCHIP * a.shape[1], a.shape[2])
    w_glu, w_b_out, w_o, w_pg = (rows_of(w[n]) for n in ("w_glu", "w_b_out", "w_o", "w_ple_gate"))
    ar, ai, bbr, bbi = _s5_discretize(w["lam_re"], w["lam_im"], w["log_dt"], w["s5_b_re"], w["s5_b_im"])
    con, con_rev = _scan_constants(ar, ai)
    bbr_d = _blockdiag(jnp.swapaxes(bbr, 1, 2)).astype(BF)
    bbi_d = _blockdiag(jnp.swapaxes(bbi, 1, 2)).astype(BF)
    ccr_d = _blockdiag(jnp.swapaxes(w["s5_c_re"], 1, 2)).astype(BF)
    cci_d = _blockdiag(jnp.swapaxes(w["s5_c_im"], 1, 2)).astype(BF)
    dsk = w["s5_d"].reshape(1, S5W)
    wr_d = _blockdiag(w["w_r"]).astype(BF)
    wi_d = _blockdiag(w["w_i"]).astype(BF)
    lam = w["lru_lambda"].reshape(1, LW)
    sp = jax.nn.softplus(-lam)
    b_r, b_i = w["b_r"].reshape(1, LW), w["b_i"].reshape(1, LW)
    row = lambda name: w[name].reshape(1, -1)

    h, ua, ub, gp = _inproj_fwd(x, row("g_mix"), w["w_in"], row("b_in"))
    sr, si, y, zg, ya = _s5_fwd(ua, bbr_d, bbi_d, ccr_d, cci_d, dsk, con, w_glu, row("b_glu"))
    xc, rg, ig, yb, hp = _lru_fwd(ub, w["conv_w"], row("conv_b"), wr_d, wi_d, b_r, b_i, sp)
    x1, pa, pb, merged = _merge_fwd(x, ya, yb, gp, w["w_a_out"], w_b_out, w_o)
    x2, h2, gg, uu = _ffn_fwd(x1, row("g_ffn"), w["w_ffn_gate"], w["w_ffn_up"], w["w_ffn_down"])
    dx2, n2, dpre, de0, acc_p = _ple_loss(x2, p, tgt, row("g_ple_gate"), w_pg, row("b_ple_gate"),
                                          w["w_ple"], row("g_ple"), row("g_final"))
    dx1, act, dgg, duu, acc_f = _ffn_bwd(x1, dx2, gg, uu, row("g_ffn"), w["w_ffn_gate"], w["w_ffn_up"], w["w_ffn_down"])
    dya, dyb, dgp, dpa, dpb = _merge_bwd(dx1, gp, pa, pb, w["w_a_out"], w_b_out, w_o)
    dua, dq, dy, lr, li, acc_a, acc_s = _s5_bwd(dya, y, ua, sr, si, bbr_d, bbi_d, ccr_d, cci_d, dsk, con_rev,
                                                w_glu, row("b_glu"))
    dub, dpr, dpi, acc_l = _lru_bwd(dyb, xc, rg, ig, hp, ub, w["conv_w"], wr_d, wi_d, sp, -_sig(-lam))
    gx, dz, acc_g, acc_b = _inproj_bwd(x, dx1, dua, dub, dgp, row("g_mix"), w["w_in"])

    quarters = lambda a: a.reshape(NCHIP, a.shape[0] // NCHIP, a.shape[1])
    g = {
        "w_in": _tn("dw_in", h, dz, col_chunk=QC),
        "w_glu": quarters(_tn("dw_glu", zg, dq)),
        "w_a_out": _tn("dw_a_out", ya, dpa, col_chunk=AC),
        "w_b_out": quarters(_tn("dw_b_out", yb, dpb)),
        "w_o": quarters(_tn("dw_o", merged, dx1)),
        "w_ffn_gate": _tn("dw_ffn_gate", h2, dgg),
        "w_ffn_up": _tn("dw_ffn_up", h2, duu),
        "w_ffn_down": _tn("dw_ffn_down", act, dx2),
        "w_ple_gate": quarters(_tn("dw_ple_gate", n2, dpre)),
        "w_ple": _tn("dw_ple", p, de0, col_chunk=AC),
    }
    sums = {"ple": acc_p, "ffn": acc_f, "mix": acc_g, "b_in": acc_b, "lru": acc_l, "s5": acc_s, "s5_a": acc_a}
    blocks = {
        "bb_re": _diag_blocks(_tn("d_bbr", ua, lr), NG),
        "bb_im": _diag_blocks(_tn("d_bbi", ua, li), NG),
        "cc_re": _diag_blocks(_tn("d_ccr", sr, dy), NG),
        "cc_im": _diag_blocks(_tn("d_cci", si, dy), NG),
        "w_r": _diag_blocks(_tn("dw_r", xc, dpr), NH),
        "w_i": _diag_blocks(_tn("dw_i", xc, dpi), NH),
    }
    return gx, g, sums, blocks


def _replicated_grads(w, sums, blocks):
    d_ar, d_ai = sums["s5_a"][0].reshape(NG, NS), sums["s5_a"][1].reshape(NG, NS)
    _, vjp = jax.vjp(_s5_discretize, w["lam_re"], w["lam_im"], w["log_dt"], w["s5_b_re"], w["s5_b_im"])
    g = dict(zip(("lam_re", "lam_im", "log_dt", "s5_b_re", "s5_b_im"),
                 vjp((d_ar, d_ai, jnp.swapaxes(blocks["bb_re"], 1, 2), jnp.swapaxes(blocks["bb_im"], 1, 2)))))
    g["s5_c_re"] = jnp.swapaxes(blocks["cc_re"], 1, 2)
    g["s5_c_im"] = -jnp.swapaxes(blocks["cc_im"], 1, 2)
    g["w_r"], g["w_i"] = blocks["w_r"], blocks["w_i"]
    g["s5_d"] = sums["s5"][0].reshape(NG, NP)
    g["b_r"] = sums["lru"][1].reshape(NH, HD)
    g["b_i"] = sums["lru"][2].reshape(NH, HD)
    return g


ACC_ROWS = {"g_mix": ("mix", 0), "b_in": ("b_in", 0), "g_ffn": ("ffn", 0), "g_ple_gate": ("ple", 0),
            "b_ple_gate": ("ple", 1), "g_ple": ("ple", 2), "g_final": ("ple", 3), "b_glu": ("s5", 1),
            "lru_lambda": ("lru", 0), "conv_b": ("lru", 3)}
LOSS_ROW = ("ple", 4)
CONV_W_ROWS = ("lru", 4)


SHARDED = [("w_in", (D, QC)), ("w_glu", (S5W // NCHIP, S5W)), ("w_a_out", (S5W, AC)), ("w_b_out", (LW // NCHIP, D)),
           ("w_o", (D // NCHIP, D)), ("w_ffn_gate", (D, FC)), ("w_ffn_up", (D, FC)), ("w_ffn_down", (FC, D)),
           ("w_ple_gate", (D // NCHIP, D)), ("w_ple", (PLE, AC))]
NSH = len(SHARDED)
CONV_SHARD = (4, LW // NCHIP)


def _mesh_pos():
    return lax.axis_index("x"), lax.axis_index("y"), lax.axis_index("c")


def _other_chips(x, y):
    return [(1 - x, y), (x, 1 - y), (1 - x, 1 - y)]


def _half_rows(c, rows, align):
    return pl.ds(pl.multiple_of(c * (rows // 2), align), rows // 2)


def _gather_weights(shards, conv_w):
    def body(*refs):
        srcs, conv_src = refs[:NSH], refs[NSH]
        outs, conv_out = refs[NSH + 1:2 * NSH + 1], refs[2 * NSH + 1]
        send_sems, recv_sems, local_sems = refs[2 * NSH + 2:]
        x, y, c = _mesh_pos()
        k0 = 2 * x + y
        sib = (x, y, 1 - c)
        chips = _other_chips(x, y)
        chip_ids = [2 * chip[0] + chip[1] for chip in chips]

        def remote(src, dst, j, i, to):
            return pltpu.make_async_remote_copy(src_ref=src, dst_ref=dst, send_sem=send_sems.at[j, i],
                                                recv_sem=recv_sems.at[j, i], device_id=to, device_id_type=MESH)

        local = [pltpu.make_async_copy(s, o.at[k0], local_sems.at[i]) for i, (s, o) in enumerate(zip(srcs, outs))]
        local.append(pltpu.make_async_copy(conv_src, conv_out.at[k0], local_sems.at[NSH]))
        for cp in local:
            cp.start()
        sends = []
        for j, chip in enumerate(chips):
            for i, (s, o) in enumerate(zip(srcs, outs)):
                mine = _half_rows(c, s.shape[0], 16)
                sends.append(remote(s.at[mine], o.at[k0, mine], j, i, (*chip, c)))
            sends.append(remote(conv_src, conv_out.at[k0], j, NSH, (*chip, c)))
        for cp in sends:
            cp.start()
        for j, chip in enumerate(chips):
            kj = chip_ids[j]
            remote(conv_src, conv_out.at[kj], j, NSH, (*chip, c)).wait_recv()
            for i, (s, o) in enumerate(zip(srcs, outs)):
                mine = _half_rows(c, s.shape[0], 16)
                remote(s.at[mine], o.at[kj, mine], j, i, (*chip, c)).wait_recv()
                fwd = remote(o.at[kj, mine], o.at[kj, mine], 3 + j, i, sib)
                fwd.start()
                sends.append(fwd)
        for j in range(3):
            kj = chip_ids[j]
            for i, (s, o) in enumerate(zip(srcs, outs)):
                other = _half_rows(1 - c, s.shape[0], 16)
                remote(o.at[kj, other], o.at[kj, other], 3 + j, i, sib).wait_recv()
        for cp in sends:
            cp.wait_send()
        for cp in local:
            cp.wait()

    out_shape = [_sds((NCHIP,) + s.shape, BF) for s in shards] + [_sds((NCHIP,) + CONV_SHARD)]
    return pl.pallas_call(
        body, name="gather_weights", in_specs=[ANY] * (NSH + 1), out_specs=[ANY] * (NSH + 1), out_shape=out_shape,
        scratch_shapes=[pltpu.SemaphoreType.DMA((6, NSH + 1)), pltpu.SemaphoreType.DMA((6, NSH + 1)),
                        pltpu.SemaphoreType.DMA((NSH + 1,))],
    )(*shards, conv_w)


def _swap_sibling_halves(grads):
    def body(*refs):
        srcs, outs, (send_sems, recv_sems) = refs[:NSH], refs[NSH:2 * NSH], refs[2 * NSH:]
        x, y, c = _mesh_pos()
        cps = [pltpu.make_async_remote_copy(src_ref=s.at[:, _half_rows(1 - c, s.shape[1], 8)], dst_ref=o,
                                            send_sem=send_sems.at[i], recv_sem=recv_sems.at[i], device_id=(x, y, 1 - c),
                                            device_id_type=MESH) for i, (s, o) in enumerate(zip(srcs, outs))]
        for cp in cps:
            cp.start()
        for cp in cps:
            cp.wait()

    return pl.pallas_call(
        body, name="swap_sibling_halves", in_specs=[ANY] * NSH, out_specs=[ANY] * NSH,
        out_shape=[_sds((NCHIP, g.shape[1] // 2, g.shape[2])) for g in grads],
        scratch_shapes=[pltpu.SemaphoreType.DMA((NSH,)), pltpu.SemaphoreType.DMA((NSH,))],
    )(*grads)


def _add_sibling(name, c_idx, g, got):
    hr, cols = got.shape[1:]

    def body(c_ref, g_ref, got_ref, p_ref, pb_ref):
        s = g_ref[...] + got_ref[...]
        p_ref[...] = s
        pb_ref[...] = s.astype(BF)

    spec = pl.BlockSpec((None, hr, cols), lambda k, c_ref: (k, 0, 0))
    return pl.pallas_call(
        body, name="add_sibling_" + name,
        grid_spec=pltpu.PrefetchScalarGridSpec(
            num_scalar_prefetch=1, grid=(NCHIP,),
            in_specs=[pl.BlockSpec((None, hr, cols), lambda k, c_ref: (k, c_ref[0], 0)), spec],
            out_specs=[spec, spec]),
        out_shape=[_sds((NCHIP, hr, cols)), _sds((NCHIP, hr, cols), BF)],
        compiler_params=_params(32),
    )(c_idx, g, got)


def _exchange_chips(parts):
    def body(*refs):
        srcs, outs, (send_sems, recv_sems) = refs[:NSH], refs[NSH:2 * NSH], refs[2 * NSH:]
        x, y, c = _mesh_pos()
        cps = []
        for j, chip in enumerate(_other_chips(x, y)):
            for i, (s, o) in enumerate(zip(srcs, outs)):
                cps.append(pltpu.make_async_remote_copy(
                    src_ref=s.at[2 * chip[0] + chip[1]], dst_ref=o.at[j], send_sem=send_sems.at[j, i],
                    recv_sem=recv_sems.at[j, i], device_id=(*chip, c), device_id_type=MESH))
        for cp in cps:
            cp.start()
        for cp in cps:
            cp.wait()

    return pl.pallas_call(
        body, name="exchange_chips", in_specs=[ANY] * NSH, out_specs=[ANY] * NSH,
        out_shape=[_sds((3,) + p.shape[1:], BF) for p in parts],
        scratch_shapes=[pltpu.SemaphoreType.DMA((3, NSH)), pltpu.SemaphoreType.DMA((3, NSH))],
    )(*parts)


def _add_chips(name, k_idx, p, got):
    hr, cols = got.shape[1:]

    def body(k_ref, p_ref, got_ref, t_ref):
        t_ref[...] = ((p_ref[...] + got_ref[0].astype(F32)) + got_ref[1].astype(F32)) + got_ref[2].astype(F32)

    return pl.pallas_call(
        body, name="add_chips_" + name,
        grid_spec=pltpu.PrefetchScalarGridSpec(
            num_scalar_prefetch=1, grid=(1,),
            in_specs=[pl.BlockSpec((None, hr, cols), lambda i, k_ref: (k_ref[0], 0, 0)),
                      pl.BlockSpec((3, hr, cols), lambda i, k_ref: (0, 0, 0))],
            out_specs=pl.BlockSpec((hr, cols), lambda i, k_ref: (0, 0))),
        out_shape=_sds((hr, cols)),
        compiler_params=_params(32),
    )(k_idx, p, got)


def _join_sibling(halves):
    def body(*refs):
        srcs, outs, (send_sems, recv_sems, local_sems) = refs[:NSH], refs[NSH:2 * NSH], refs[2 * NSH:]
        x, y, c = _mesh_pos()
        sib = (x, y, 1 - c)
        pairs = list(enumerate(zip(srcs, outs)))
        local = [pltpu.make_async_copy(s, o.at[c], local_sems.at[i]) for i, (s, o) in pairs]
        sends = [pltpu.make_async_remote_copy(src_ref=s, dst_ref=o.at[c], send_sem=send_sems.at[i],
                                              recv_sem=recv_sems.at[i], device_id=sib, device_id_type=MESH)
                 for i, (s, o) in pairs]
        for cp in local + sends:
            cp.start()
        for i, (s, o) in pairs:
            pltpu.make_async_remote_copy(src_ref=s, dst_ref=o.at[1 - c], send_sem=send_sems.at[i],
                                         recv_sem=recv_sems.at[i], device_id=sib, device_id_type=MESH).wait_recv()
        for cp in sends:
            cp.wait_send()
        for cp in local:
            cp.wait()

    return pl.pallas_call(
        body, name="join_sibling", in_specs=[ANY] * NSH, out_specs=[ANY] * NSH,
        out_shape=[_sds((2,) + h.shape) for h in halves],
        scratch_shapes=[pltpu.SemaphoreType.DMA((NSH,)), pltpu.SemaphoreType.DMA((NSH,)), pltpu.SemaphoreType.DMA((NSH,))],
    )(*halves)


def _allreduce_small(arrays):
    n = len(arrays)

    def body(*refs):
        srcs, outs = refs[:n], refs[n:2 * n]
        sibs, chip_bufs = refs[2 * n:3 * n], refs[3 * n:4 * n]
        send_sems, recv_sems = refs[4 * n:]
        x, y, c = _mesh_pos()
        k0 = 2 * x + y

        def remote(src, dst, j, i, to):
            return pltpu.make_async_remote_copy(src_ref=src, dst_ref=dst, send_sem=send_sems.at[j, i],
                                                recv_sem=recv_sems.at[j, i], device_id=to, device_id_type=MESH)

        swaps = [remote(s, b, 0, i, (x, y, 1 - c)) for i, (s, b) in enumerate(zip(srcs, sibs))]
        for cp in swaps:
            cp.start()
        for cp in swaps:
            cp.wait()
        for s, b, buf in zip(srcs, sibs, chip_bufs):
            buf[k0] = s[...] + b[...]
        chips = _other_chips(x, y)
        sends = [remote(buf.at[k0], buf.at[k0], 1 + j, i, (*chip, c))
                 for j, chip in enumerate(chips) for i, buf in enumerate(chip_bufs)]
        for cp in sends:
            cp.start()
        for j, chip in enumerate(chips):
            for i, buf in enumerate(chip_bufs):
                remote(buf.at[k0], buf.at[2 * chip[0] + chip[1]], 1 + j, i, (*chip, c)).wait_recv()
        for cp in sends:
            cp.wait_send()
        for o, buf in zip(outs, chip_bufs):
            o[...] = ((buf[0] + buf[1]) + buf[2]) + buf[3]

    vm = pl.BlockSpec(memory_space=pltpu.VMEM)
    return pl.pallas_call(
        body, name="allreduce_small", in_specs=[vm] * n, out_specs=[vm] * n,
        out_shape=[_sds(a.shape) for a in arrays],
        scratch_shapes=([pltpu.VMEM(a.shape, F32) for a in arrays] + [pltpu.VMEM((NCHIP,) + a.shape, F32) for a in arrays]
                        + [pltpu.SemaphoreType.DMA((4, n)), pltpu.SemaphoreType.DMA((4, n))]),
        compiler_params=pltpu.CompilerParams(vmem_limit_bytes=32 * VMEM_MB),
    )(*arrays)


def _adamw_terms(w, g, m, v):
    m = ADAM_B1 * m + (1.0 - ADAM_B1) * g
    v = ADAM_B2 * v + (1.0 - ADAM_B2) * jnp.square(g)
    m_hat = m / (1.0 - ADAM_B1 ** ADAM_STEP)
    v_hat = v / (1.0 - ADAM_B2 ** ADAM_STEP)
    return -ADAM_LR * (m_hat / (jnp.sqrt(v_hat) + ADAM_EPS) + ADAM_WD * w), m, v


def _adamw(name, w, g, m, v):
    r, c = w.shape
    rows = max(b for b in range(SUB, r + 1, SUB) if r % b == 0 and b * c * 4 <= 3 * VMEM_MB // 2)

    def body(w_ref, g_ref, m_ref, v_ref, d_ref, nm_ref, nv_ref):
        d_ref[...], nm_ref[...], nv_ref[...] = _adamw_terms(w_ref[...], g_ref[...], m_ref[...], v_ref[...])

    spec = pl.BlockSpec((rows, c), lambda i: (i, 0))
    return pl.pallas_call(
        body, name=name, grid=(r // rows,), in_specs=[spec] * 4, out_specs=[spec] * 3,
        out_shape=[_sds((r, c))] * 3, compiler_params=_params(40),
    )(w, g, m, v)


def _adamw_replicated(sums, row_of, direct):
    ns, nr, nd = len(sums), len(row_of), len(direct)

    def body(*refs):
        sum_refs = refs[:ns]
        ins = refs[ns:ns + 3 * nr + 4 * nd]
        outs = refs[ns + 3 * nr + 4 * nd:]
        for i, (_, _, _, si, row) in enumerate(row_of):
            w_ref, m_ref, v_ref = ins[3 * i:3 * i + 3]
            g = sum_refs[si][row:row + 1, :]
            outs[4 * i][...] = g
            outs[4 * i + 1][...], outs[4 * i + 2][...], outs[4 * i + 3][...] = _adamw_terms(w_ref[...], g, m_ref[...], v_ref[...])
        for i in range(nd):
            w_ref, m_ref, v_ref, g_ref = ins[3 * nr + 4 * i:3 * nr + 4 * i + 4]
            o = outs[4 * (nr + i):4 * (nr + i) + 4]
            g = g_ref[...]
            o[0][...] = g
            o[1][...], o[2][...], o[3][...] = _adamw_terms(w_ref[...], g, m_ref[...], v_ref[...])

    operands = list(sums)
    shapes = []
    for w, m, v, _, _ in row_of:
        operands += [w, m, v]
        shapes += [w.shape] * 4
    for w, m, v, g in direct:
        operands += [w, m, v, g]
        shapes += [w.shape] * 4
    vm = pl.BlockSpec(memory_space=pltpu.VMEM)
    flat = pl.pallas_call(
        body, name="adamw_replicated", in_specs=[vm] * len(operands), out_specs=[vm] * len(shapes),
        out_shape=[_sds(s) for s in shapes],
        compiler_params=pltpu.CompilerParams(vmem_limit_bytes=48 * VMEM_MB),
    )(*operands)
    return [flat[4 * i:4 * i + 4] for i in range(nr + nd)]


INPUT_NAMES = (["x", "p"] + [n for n in
               ["g_mix", "w_in", "b_in", "lam_re", "lam_im", "log_dt", "s5_b_re", "s5_b_im", "s5_c_re", "s5_c_im", "s5_d",
                "w_glu", "b_glu", "conv_w", "conv_b", "w_r", "b_r", "w_i", "b_i", "lru_lambda", "w_a_out", "w_b_out", "w_o",
                "g_ffn", "w_ffn_gate", "w_ffn_up", "w_ffn_down", "g_ple_gate", "w_ple_gate", "b_ple_gate", "w_ple", "g_ple",
                "g_final"]])
WEIGHT_NAMES = INPUT_NAMES[2:]


def kernel(*args):
    names = INPUT_NAMES + ["loss_target"] + ["m_" + n for n in WEIGHT_NAMES] + ["v_" + n for n in WEIGHT_NAMES]
    assert len(args) == len(names)
    given = dict(zip(names, args))

    def local(name):
        a = given[name]
        return a if name.endswith("g_final") else a[0]

    xi, yi, ci = _mesh_pos()
    k0 = 2 * xi + yi
    x, p, tgt = given["x"][0], given["p"][0, 0], given["loss_target"][0]

    shard_names = [n for n, _ in SHARDED]
    got = _gather_weights([local(n).astype(BF) for n in shard_names], local("conv_w"))
    w = dict(zip(shard_names, got[:NSH]))
    w["conv_w"] = jnp.transpose(got[NSH], (1, 0, 2)).reshape(4, LW)
    replicated = [n for n in WEIGHT_NAMES if n not in w]
    for n in replicated:
        w[n] = local(n)

    gx, g, sums, blocks = _local_step(x, p, tgt, w)

    c_idx = jnp.reshape(ci, (1,)).astype(jnp.int32)
    k_idx = jnp.reshape(k0, (1,)).astype(jnp.int32)
    grads = [g[n] for n in shard_names]
    parts = [_add_sibling(n, c_idx, gr, rx) for n, gr, rx in zip(shard_names, grads, _swap_sibling_halves(grads))]
    arrived = _exchange_chips([pb for _, pb in parts])
    halves = [_add_chips(n, k_idx, pf, rx) for n, (pf, _), rx in zip(shard_names, parts, arrived)]
    results = {}
    for (n, shape), both in zip(SHARDED, _join_sibling(halves)):
        total = both.reshape(shape)
        delta, new_m, new_v = _adamw("adamw_" + n, local(n), total, local("m_" + n), local("v_" + n))
        for kind, arr in zip(("grad", "delta", "new_m", "new_v"), (total, delta, new_m, new_v)):
            results[kind, n] = arr[None]

    sum_names, block_names = list(sums), list(blocks)
    lanes = lambda a: a.reshape(-1, LW)
    red = _allreduce_small([sums[n] for n in sum_names] + [lanes(blocks[n]) for n in block_names])
    sums = dict(zip(sum_names, red[:len(sum_names)]))
    blocks = {n: a.reshape(blocks[n].shape) for n, a in zip(block_names, red[len(sum_names):])}
    loss = jnp.sum(sums[LOSS_ROW[0]][LOSS_ROW[1]])
    direct_g = _replicated_grads(w, sums, blocks)
    conv_rows = sums[CONV_W_ROWS[0]][CONV_W_ROWS[1]:CONV_W_ROWS[1] + 4]
    direct_g["conv_w"] = lax.dynamic_slice(conv_rows, (0, k0 * CONV_SHARD[1]), CONV_SHARD)
    as_row = lambda a: a.reshape(1, -1)
    row_names = list(ACC_ROWS)
    row_of = [(as_row(given[n]), as_row(given["m_" + n]), as_row(given["v_" + n]),
               sum_names.index(ACC_ROWS[n][0]), ACC_ROWS[n][1]) for n in row_names]
    direct_names = list(direct_g)
    direct = [(given[n], given["m_" + n], given["v_" + n], direct_g[n].reshape(given[n].shape)) for n in direct_names]
    done = _adamw_replicated([sums[n] for n in sum_names], row_of, direct)
    for n, four in zip(row_names + direct_names, done):
        for kind, arr in zip(("grad", "delta", "new_m", "new_v"), four):
            results[kind, n] = arr.reshape(given[n].shape)

    out = [loss, gx[None]]
    for kind in ("grad", "delta", "new_m", "new_v"):
        out += [results[kind, n] for n in WEIGHT_NAMES]
    return tuple(out)
```

```python
import functools
import math

import jax
import jax.numpy as jnp
from jax import lax
from jax.experimental import pallas as pl
from jax.experimental.pallas import tpu as pltpu

F32 = jnp.float32
BF = jnp.bfloat16

D = 1024
S5W = 512
NG, NS, NP = 32, 64, 16
GN = NG * NS
LW = 1024
NH, HD = 16, 64
LRU_C = 8.0
FH = 2816
NCHIP = 4
FC = FH // NCHIP
PLE = 256
INC = S5W + LW + 2 * D
EPS = 1e-6
ADAM_LR, ADAM_B1, ADAM_B2, ADAM_EPS, ADAM_WD, ADAM_STEP = 0.001, 0.9, 0.999, 1e-08, 0.01, 10

TM = 256
TK = 512
LC = 512
SUB = 8
VMEM_MB = 1024 * 1024
MESH = pl.DeviceIdType.MESH
ANY = pl.BlockSpec(memory_space=pl.ANY)


def _mm(a, b):
    return jnp.dot(a.astype(BF), b.astype(BF), preferred_element_type=F32)


def _mm_nt(a, b):
    return lax.dot_general(a.astype(BF), b.astype(BF), (((1,), (1,)), ((), ())), preferred_element_type=F32)


def _mm_tn(a, b):
    return lax.dot_general(a.astype(BF), b.astype(BF), (((0,), (0,)), ((), ())), preferred_element_type=F32)


def _rms(x):
    r = lax.rsqrt(jnp.mean(x * x, axis=-1, keepdims=True) + EPS)
    return x * r, r


def _rms_bwd(dy, xh, r, g):
    dxh = dy * g
    return r * (dxh - xh * jnp.mean(dxh * xh, axis=-1, keepdims=True))


def _colsum(x):
    return jnp.sum(x, axis=0, keepdims=True)


def _sig(x):
    return jax.nn.sigmoid(x)


def _gelu_grad(x):
    c = math.sqrt(2.0 / math.pi)
    t = jnp.tanh(c * (x + 0.044715 * x * x * x))
    return 0.5 * (1.0 + t) + 0.5 * x * (1.0 - t * t) * c * (1.0 + 3.0 * 0.044715 * x * x)


def _neg_expm1(x):
    series = -x * (1.0 + x * (0.5 + x * (1.0 / 6.0 + x * (1.0 / 24.0))))
    return jnp.where(x > -0.03, series, 1.0 - jnp.exp(x))


def _tok(width):
    return pl.BlockSpec((TM, width), lambda i: (i, 0))


def _tok_rev(width, nt):
    return pl.BlockSpec((TM, width), lambda i: (nt - 1 - i, 0))


def _full(shape):
    return pl.BlockSpec(shape, lambda i: (0,) * len(shape))


def _params(vmem_mb, **kw):
    return pltpu.CompilerParams(dimension_semantics=("arbitrary",), vmem_limit_bytes=vmem_mb * VMEM_MB, **kw)


def _sds(shape, dtype=F32):
    return jax.ShapeDtypeStruct(shape, dtype)


def _row_iota(width):
    return lax.broadcasted_iota(jnp.int32, (SUB, width), 0)


def _bcast_row(x, row):
    return jnp.broadcast_to(x[row:row + 1, :], x.shape)


def _slab(k):
    return pl.ds(pl.multiple_of(k * SUB, SUB), SUB)


QC = INC // NCHIP
Z_PARTS = ((0, S5W), (S5W, S5W + LW), (S5W + LW, INC))


def _inproj_fwd(x, g_mix, w_in, b_in):
    L = x.shape[0]

    def body(x_ref, g_ref, w_hbm, b_ref, h_ref, ua_ref, ub_ref, gp_ref, w_vm):
        @pl.when(pl.program_id(0) == 0)
        def _():
            pltpu.sync_copy(w_hbm, w_vm)

        xh, _ = _rms(x_ref[...])
        h = (xh * g_ref[...]).astype(BF)
        h_ref[...] = h
        for k in range(NCHIP):
            lo, hi = k * QC, (k + 1) * QC
            z = jnp.dot(h, w_vm[k], preferred_element_type=F32) + b_ref[:, lo:hi]
            for ref, (a, b) in zip((ua_ref, ub_ref, gp_ref), Z_PARTS):
                s, e = max(lo, a), min(hi, b)
                if s < e:
                    ref[:, s - a:e - a] = z[:, s - lo:e - lo]

    return pl.pallas_call(
        body, name="inproj_fwd", grid=(L // TM,),
        in_specs=[_tok(D), _full((1, D)), ANY, _full((1, INC))],
        out_specs=[_tok(D), _tok(S5W), _tok(LW), _tok(2 * D)],
        out_shape=[_sds((L, D), BF), _sds((L, S5W)), _sds((L, LW)), _sds((L, 2 * D))],
        scratch_shapes=[pltpu.VMEM((NCHIP, D, QC), BF)],
        compiler_params=_params(40),
    )(x, g_mix, w_in, b_in)


def _inproj_bwd(x, dx1, dua, dub, dgp, g_mix, w_in):
    L = x.shape[0]

    def body(x_ref, dx1_ref, dua_ref, dub_ref, dgp_ref, g_ref, w_hbm, gx_ref, dz_ref, dg_ref, db_ref, w_vm):
        @pl.when(pl.program_id(0) == 0)
        def _():
            pltpu.sync_copy(w_hbm, w_vm)
            dg_ref[...] = jnp.zeros_like(dg_ref)
            db_ref[...] = jnp.zeros_like(db_ref)

        for src, (a, b) in zip((dua_ref, dub_ref, dgp_ref), Z_PARTS):
            d = src[...]
            dz_ref[:, a:b] = d.astype(BF)
            db_ref[0:1, a:b] += _colsum(d)
        dh = jnp.zeros((TM, D), F32)
        for k in range(NCHIP):
            dh = dh + lax.dot_general(dz_ref[:, k * QC:(k + 1) * QC], w_vm[k], (((1,), (1,)), ((), ())),
                                      preferred_element_type=F32)
        xh, r = _rms(x_ref[...])
        dg_ref[0:1, :] += _colsum(dh * xh)
        gx_ref[...] = dx1_ref[...] + _rms_bwd(dh, xh, r, g_ref[...])

    return pl.pallas_call(
        body, name="inproj_bwd", grid=(L // TM,),
        in_specs=[_tok(D), _tok(D), _tok(S5W), _tok(LW), _tok(2 * D), _full((1, D)), ANY],
        out_specs=[_tok(D), _tok(INC), _full((SUB, D)), _full((SUB, INC))],
        out_shape=[_sds((L, D)), _sds((L, INC), BF), _sds((SUB, D)), _sds((SUB, INC))],
        scratch_shapes=[pltpu.VMEM((NCHIP, D, QC), BF)],
        compiler_params=_params(40),
    )(x, dx1, dua, dub, dgp, g_mix, w_in)


def _cscan(xr_ref, xi_ref, con_ref, cr_ref, ci_ref, reverse):
    n_slab = xr_ref.shape[0] // SUB
    width = xr_ref.shape[1]
    for lc in range(width // LC):
        cols = slice(lc * LC, (lc + 1) * LC)
        con = [con_ref[SUB * j:SUB * (j + 1), cols] for j in range(8)]

        def step(k, carry, cols=cols, con=con):
            cr, ci = carry
            rows = _slab(n_slab - 1 - k if reverse else k)
            xr, xi = xr_ref[rows, cols], xi_ref[rows, cols]
            for j, sh in enumerate((1, 2, 4)):
                mr, mi = con[2 * j], con[2 * j + 1]
                pr = pltpu.roll(xr, SUB - sh if reverse else sh, 0)
                pi = pltpu.roll(xi, SUB - sh if reverse else sh, 0)
                xr, xi = xr + mr * pr - mi * pi, xi + mr * pi + mi * pr
            xr, xi = xr + con[6] * cr - con[7] * ci, xi + con[6] * ci + con[7] * cr
            xr_ref[rows, cols] = xr
            xi_ref[rows, cols] = xi
            row = 0 if reverse else SUB - 1
            return _bcast_row(xr, row), _bcast_row(xi, row)

        cr, ci = lax.fori_loop(0, n_slab, step, (cr_ref[:, cols], ci_ref[:, cols]))
        cr_ref[:, cols] = cr
        ci_ref[:, cols] = ci


def _s5_fwd(ua, bbr, bbi, ccr, cci, dsk, con, w_glu, b_glu):
    L = ua.shape[0]

    def body(ua_ref, bbr_hbm, bbi_hbm, ccr_hbm, cci_hbm, dsk_ref, con_ref, wg_ref, bg_ref,
             sr_ref, si_ref, y_ref, zg_ref, ya_ref, bbr_vm, bbi_vm, ccr_vm, cci_vm, cr_ref, ci_ref):
        @pl.when(pl.program_id(0) == 0)
        def _():
            pltpu.sync_copy(bbr_hbm, bbr_vm)
            pltpu.sync_copy(bbi_hbm, bbi_vm)
            pltpu.sync_copy(ccr_hbm, ccr_vm)
            pltpu.sync_copy(cci_hbm, cci_vm)
            cr_ref[...] = jnp.zeros_like(cr_ref)
            ci_ref[...] = jnp.zeros_like(ci_ref)

        u = ua_ref[...]
        ub = u.astype(BF)
        sr_ref[...] = jnp.dot(ub, bbr_vm[...], preferred_element_type=F32)
        si_ref[...] = jnp.dot(ub, bbi_vm[...], preferred_element_type=F32)
        _cscan(sr_ref, si_ref, con_ref, cr_ref, ci_ref, reverse=False)
        y = _mm(sr_ref[...], ccr_vm[...]) - _mm(si_ref[...], cci_vm[...]) + dsk_ref[...] * u
        y_ref[...] = y
        zg = jax.nn.gelu(y)
        zg_ref[...] = zg.astype(BF)
        q = _mm(zg, wg_ref[...]) + bg_ref[...]
        ya_ref[...] = (zg * _sig(q)).astype(BF)

    return pl.pallas_call(
        body, name="s5_fwd", grid=(L // TM,),
        in_specs=[_tok(S5W), ANY, ANY, ANY, ANY, _full((1, S5W)), _full((8 * SUB, GN)),
                  _full((S5W, S5W)), _full((1, S5W))],
        out_specs=[_tok(GN), _tok(GN), _tok(S5W), _tok(S5W), _tok(S5W)],
        out_shape=[_sds((L, GN)), _sds((L, GN)), _sds((L, S5W)), _sds((L, S5W), BF), _sds((L, S5W), BF)],
        scratch_shapes=[pltpu.VMEM((S5W, GN), BF), pltpu.VMEM((S5W, GN), BF), pltpu.VMEM((GN, S5W), BF),
                        pltpu.VMEM((GN, S5W), BF), pltpu.VMEM((SUB, GN), F32), pltpu.VMEM((SUB, GN), F32)],
        compiler_params=_params(44),
    )(ua, bbr, bbi, ccr, cci, dsk, con, w_glu, b_glu)


def _s5_bwd(dya, y, ua, sr, si, bbr, bbi, ccr, cci, dsk, con_rev, w_glu, b_glu):
    L = ua.shape[0]
    nt = L // TM
    spt = TM // SUB
    n_slab = spt

    def halo_map(i):
        return (jnp.maximum((nt - 1 - i) * spt - 1, 0), 0)

    def body(dya_ref, y_ref, ua_ref, sr_ref, si_ref, hr_ref, hi_ref, bbr_hbm, bbi_hbm, ccr_hbm, cci_hbm,
             dsk_ref, con_ref, wg_ref, bg_ref,
             dua_ref, dq_ref, dy_ref, lr_ref, li_ref, da_ref, dsm_ref,
             bbr_vm, bbi_vm, ccr_vm, cci_vm, cr_ref, ci_ref):
        i = pl.program_id(0)

        @pl.when(i == 0)
        def _():
            pltpu.sync_copy(bbr_hbm, bbr_vm)
            pltpu.sync_copy(bbi_hbm, bbi_vm)
            pltpu.sync_copy(ccr_hbm, ccr_vm)
            pltpu.sync_copy(cci_hbm, cci_vm)
            cr_ref[...] = jnp.zeros_like(cr_ref)
            ci_ref[...] = jnp.zeros_like(ci_ref)
            da_ref[...] = jnp.zeros_like(da_ref)
            dsm_ref[...] = jnp.zeros_like(dsm_ref)

        u = ua_ref[...]
        yv = y_ref[...]
        dya = dya_ref[...]
        zg = jax.nn.gelu(yv)
        sg = _sig(_mm(zg, wg_ref[...]) + bg_ref[...])
        dq = dya * zg * sg * (1.0 - sg)
        dq_ref[...] = dq.astype(BF)
        dzg = dya * sg + _mm_nt(dq, wg_ref[...])
        dy = dzg * _gelu_grad(yv)
        dyb = dy.astype(BF)
        dy_ref[...] = dyb
        dsm_ref[0:1, :] += _colsum(dy * u)
        dsm_ref[1:2, :] += _colsum(dq)
        lr_ref[...] = lax.dot_general(dyb, ccr_vm[...], (((1,), (1,)), ((), ())), preferred_element_type=F32)
        li_ref[...] = -lax.dot_general(dyb, cci_vm[...], (((1,), (1,)), ((), ())), preferred_element_type=F32)
        _cscan(lr_ref, li_ref, con_ref, cr_ref, ci_ref, reverse=True)

        first_tile = (i == nt - 1)
        row = _row_iota(LC)
        for lc in range(GN // LC):
            cols = slice(lc * LC, (lc + 1) * LC)
            h_r = jnp.where(first_tile, 0.0, hr_ref[:, cols])
            h_i = jnp.where(first_tile, 0.0, hi_ref[:, cols])

            def step(k, acc, cols=cols, h_r=h_r, h_i=h_i):
                ar, ai = acc
                rows = _slab(k)
                prev = _slab(jnp.maximum(k - 1, 0))
                pr = jnp.where(k == 0, h_r, sr_ref[prev, cols])
                pi = jnp.where(k == 0, h_i, si_ref[prev, cols])
                spr = pltpu.roll(jnp.where(row == SUB - 1, pr, sr_ref[rows, cols]), 1, 0)
                spi = pltpu.roll(jnp.where(row == SUB - 1, pi, si_ref[rows, cols]), 1, 0)
                lr, li = lr_ref[rows, cols], li_ref[rows, cols]
                return ar + lr * spr + li * spi, ai + li * spr - lr * spi

            zero = jnp.zeros((SUB, LC), F32)
            ar, ai = lax.fori_loop(0, n_slab, step, (zero, zero))
            da_ref[0:1, cols] += _colsum(ar)
            da_ref[1:2, cols] += _colsum(ai)

        dua_ref[...] = (dy * dsk_ref[...] + _mm_nt(lr_ref[...], bbr_vm[...]) + _mm_nt(li_ref[...], bbi_vm[...]))

    return pl.pallas_call(
        body, name="s5_bwd", grid=(nt,),
        in_specs=[_tok_rev(S5W, nt), _tok_rev(S5W, nt), _tok_rev(S5W, nt), _tok_rev(GN, nt), _tok_rev(GN, nt),
                  pl.BlockSpec((SUB, GN), halo_map), pl.BlockSpec((SUB, GN), halo_map),
                  ANY, ANY, ANY, ANY, _full((1, S5W)), _full((8 * SUB, GN)), _full((S5W, S5W)), _full((1, S5W))],
        out_specs=[_tok_rev(S5W, nt), _tok_rev(S5W, nt), _tok_rev(S5W, nt), _tok_rev(GN, nt), _tok_rev(GN, nt),
                   _full((SUB, GN)), _full((SUB, S5W))],
        out_shape=[_sds((L, S5W)), _sds((L, S5W), BF), _sds((L, S5W), BF), _sds((L, GN)), _sds((L, GN)),
                   _sds((SUB, GN)), _sds((SUB, S5W))],
        scratch_shapes=[pltpu.VMEM((S5W, GN), BF), pltpu.VMEM((S5W, GN), BF), pltpu.VMEM((GN, S5W), BF),
                        pltpu.VMEM((GN, S5W), BF), pltpu.VMEM((SUB, GN), F32), pltpu.VMEM((SUB, GN), F32)],
        compiler_params=_params(52),
    )(dya, y, ua, sr, si, sr, si, bbr, bbi, ccr, cci, dsk, con_rev, w_glu, b_glu)


def _lru_gate_terms(rg, sp):
    log_a = -LRU_C * rg * sp
    a = jnp.exp(log_a)
    mult = jnp.sqrt(_neg_expm1(2.0 * log_a))
    return a, mult


def _lru_fwd(ub, conv_w, conv_b, wr, wi, b_r, b_i, sp):
    L = ub.shape[0]
    n_slab = TM // SUB

    def body(ub_ref, cw_ref, cb_ref, wr_ref, wi_ref, br_ref, bi_ref, sp_ref,
             xc_ref, rg_ref, ig_ref, h_ref, hp_ref, a_ref, halo_ref, carry_ref):
        @pl.when(pl.program_id(0) == 0)
        def _():
            halo_ref[...] = jnp.zeros_like(halo_ref)
            carry_ref[...] = jnp.zeros_like(carry_ref)

        row = _row_iota(LW)
        taps = [cw_ref[k:k + 1, :] for k in range(4)]
        cb = cb_ref[...]

        def conv_step(k, prev):
            rows = _slab(k)
            cur = ub_ref[rows, :]
            acc = taps[3] * cur + cb
            for j in (1, 2, 3):
                acc = acc + taps[3 - j] * pltpu.roll(jnp.where(row >= SUB - j, prev, cur), j, 0)
            xc_ref[rows, :] = acc
            return cur

        halo_ref[...] = lax.fori_loop(0, n_slab, conv_step, halo_ref[...])

        xc = xc_ref[...]
        xcb = xc.astype(BF)
        rg = _sig(jnp.dot(xcb, wr_ref[...], preferred_element_type=F32) + br_ref[...])
        ig = _sig(jnp.dot(xcb, wi_ref[...], preferred_element_type=F32) + bi_ref[...])
        rg_ref[...] = rg
        ig_ref[...] = ig
        a, mult = _lru_gate_terms(rg, sp_ref[...])
        a_ref[...] = a
        h_ref[...] = mult * ig * xc

        rowc = _row_iota(LC)
        for lc in range(LW // LC):
            cols = slice(lc * LC, (lc + 1) * LC)

            def step(k, c, cols=cols):
                rows = _slab(k)
                av, b = a_ref[rows, cols], h_ref[rows, cols]
                for sh in (1, 2, 4):
                    keep = rowc >= sh
                    b = b + av * jnp.where(keep, pltpu.roll(b, sh, 0), 0.0)
                    av = av * jnp.where(keep, pltpu.roll(av, sh, 0), 1.0)
                h = b + av * c
                h_ref[rows, cols] = h
                hp_ref[rows, cols] = jnp.where(rowc == 0, c, pltpu.roll(h, 1, 0))
                return _bcast_row(h, SUB - 1)

            carry_ref[:, cols] = lax.fori_loop(0, n_slab, step, carry_ref[:, cols])

    return pl.pallas_call(
        body, name="lru_fwd", grid=(L // TM,),
        in_specs=[_tok(LW), _full((4, LW)), _full((1, LW)), _full((LW, LW)), _full((LW, LW)),
                  _full((1, LW)), _full((1, LW)), _full((1, LW))],
        out_specs=[_tok(LW)] * 5,
        out_shape=[_sds((L, LW))] * 5,
        scratch_shapes=[pltpu.VMEM((TM, LW), F32), pltpu.VMEM((SUB, LW), F32), pltpu.VMEM((SUB, LW), F32)],
        compiler_params=_params(40),
    )(ub, conv_w, conv_b, wr, wi, b_r, b_i, sp)


def _lru_bwd(dyb, xc, rg, ig, hp, ub, conv_w, wr, wi, sp, dsp):
    L = ub.shape[0]
    nt = L // TM
    spt = TM // SUB
    n_slab = spt

    def halo_map(i):
        return (jnp.maximum((nt - 1 - i) * spt - 1, 0), 0)

    def body(dh_ref, xc_ref, rg_ref, ig_ref, hp_ref, ub_ref, uh_ref, cw_ref, wr_ref, wi_ref, sp_ref, dsp_ref,
             dub_ref, dpr_ref, dpi_ref, acc_ref, a_ref, lam_ref, dxc_ref, carry_ref, next_ref):
        i = pl.program_id(0)

        @pl.when(i == 0)
        def _():
            carry_ref[...] = jnp.zeros_like(carry_ref)
            next_ref[...] = jnp.zeros_like(next_ref)
            acc_ref[...] = jnp.zeros_like(acc_ref)

        sp = sp_ref[...]
        rg, ig, xc = rg_ref[...], ig_ref[...], xc_ref[...]
        a, mult = _lru_gate_terms(rg, sp)
        a_ref[...] = a

        rowc = _row_iota(LC)
        for lc in range(LW // LC):
            cols = slice(lc * LC, (lc + 1) * LC)

            def step(k, c, cols=cols):
                rows = _slab(n_slab - 1 - k)
                av, dh = a_ref[rows, cols], dh_ref[rows, cols]
                b = av * dh
                for sh in (1, 2, 4):
                    keep = rowc < SUB - sh
                    b = b + av * jnp.where(keep, pltpu.roll(b, SUB - sh, 0), 0.0)
                    av = av * jnp.where(keep, pltpu.roll(av, SUB - sh, 0), 1.0)
                mu = b + av * c
                lam_ref[rows, cols] = dh + jnp.where(rowc == SUB - 1, c, pltpu.roll(mu, SUB - 1, 0))
                return _bcast_row(mu, 0)

            carry_ref[:, cols] = lax.fori_loop(0, n_slab, step, carry_ref[:, cols])

        lam = lam_ref[...]
        d_a = lam * hp_ref[...]
        d_mult = lam * ig * xc
        d_ig = lam * mult * xc
        dxc = lam * mult * ig
        d_log_a = d_a * a - d_mult * a * a / mult
        d_rg = (-LRU_C) * sp * d_log_a
        acc_ref[0:1, :] += _colsum((-LRU_C) * rg * d_log_a) * dsp_ref[...]
        dpr = d_rg * rg * (1.0 - rg)
        dpi = d_ig * ig * (1.0 - ig)
        acc_ref[1:2, :] += _colsum(dpr)
        acc_ref[2:3, :] += _colsum(dpi)
        dprb, dpib = dpr.astype(BF), dpi.astype(BF)
        dpr_ref[...] = dprb
        dpi_ref[...] = dpib
        dxc = dxc + _mm_nt(dprb, wr_ref[...]) + _mm_nt(dpib, wi_ref[...])
        dxc_ref[...] = dxc
        acc_ref[3:4, :] += _colsum(dxc)

        row = _row_iota(LW)
        taps = [cw_ref[k:k + 1, :] for k in range(4)]
        u_halo = jnp.where(i == nt - 1, 0.0, uh_ref[...])
        nxt_tile = next_ref[...]

        def conv_step(k, accs):
            rows = _slab(k)
            cur = dxc_ref[rows, :]
            nxt = jnp.where(k == n_slab - 1, nxt_tile, dxc_ref[_slab(jnp.minimum(k + 1, n_slab - 1)), :])
            ucur = ub_ref[rows, :]
            uprev = jnp.where(k == 0, u_halo, ub_ref[_slab(jnp.maximum(k - 1, 0)), :])
            du = taps[3] * cur
            new = [accs[3] + cur * ucur]
            for j in (1, 2, 3):
                du = du + taps[3 - j] * pltpu.roll(jnp.where(row < j, nxt, cur), SUB - j, 0)
                new.append(accs[3 - j] + cur * pltpu.roll(jnp.where(row >= SUB - j, uprev, ucur), j, 0))
            dub_ref[rows, :] = du
            return tuple(new[::-1])

        zero = jnp.zeros((SUB, LW), F32)
        accs = lax.fori_loop(0, n_slab, conv_step, (zero, zero, zero, zero))
        for k in range(4):
            acc_ref[4 + k:5 + k, :] += _colsum(accs[k])
        next_ref[...] = dxc_ref[0:SUB, :]

    return pl.pallas_call(
        body, name="lru_bwd", grid=(nt,),
        in_specs=[_tok_rev(LW, nt)] * 6 + [pl.BlockSpec((SUB, LW), halo_map), _full((4, LW)),
                                           _full((LW, LW)), _full((LW, LW)), _full((1, LW)), _full((1, LW))],
        out_specs=[_tok_rev(LW, nt), _tok_rev(LW, nt), _tok_rev(LW, nt), _full((SUB, LW))],
        out_shape=[_sds((L, LW)), _sds((L, LW), BF), _sds((L, LW), BF), _sds((SUB, LW))],
        scratch_shapes=[pltpu.VMEM((TM, LW), F32), pltpu.VMEM((TM, LW), F32), pltpu.VMEM((TM, LW), F32),
                        pltpu.VMEM((SUB, LW), F32), pltpu.VMEM((SUB, LW), F32)],
        compiler_params=_params(48),
    )(dyb, xc, rg, ig, hp, ub, ub, conv_w, wr, wi, sp, dsp)


AC = D // NCHIP


def _merge_fwd(x, ya, yb, gp, w_a, w_b, w_o):
    L = x.shape[0]

    def body(x_ref, ya_ref, yb_ref, gp_ref, wa_ref, wb_ref, wo_ref, x1_ref, pa_ref, pb_ref, mg_ref):
        ya = ya_ref[...]
        for k in range(NCHIP):
            pa_ref[:, k * AC:(k + 1) * AC] = jnp.dot(ya, wa_ref[k], preferred_element_type=F32)
        pb = _mm(yb_ref[...], wb_ref[...])
        pb_ref[...] = pb
        gp = gp_ref[...]
        merged = (_sig(gp[:, :D]) * pa_ref[...] + _sig(gp[:, D:]) * pb).astype(BF)
        mg_ref[...] = merged
        x1_ref[...] = x_ref[...] + jnp.dot(merged, wo_ref[...], preferred_element_type=F32)

    return pl.pallas_call(
        body, name="merge_fwd", grid=(L // TM,),
        in_specs=[_tok(D), _tok(S5W), _tok(LW), _tok(2 * D), _full((NCHIP, S5W, AC)), _full((LW, D)), _full((D, D))],
        out_specs=[_tok(D), _tok(D), _tok(D), _tok(D)],
        out_shape=[_sds((L, D)), _sds((L, D)), _sds((L, D)), _sds((L, D), BF)],
        compiler_params=_params(40),
    )(x, ya, yb, gp, w_a, w_b, w_o)


def _merge_bwd(dx1, gp, pa, pb, w_a, w_b, w_o):
    L = dx1.shape[0]

    def body(dx1_ref, gp_ref, pa_ref, pb_ref, wa_ref, wb_ref, wo_ref, dya_ref, dyb_ref, dgp_ref, dpa_ref, dpb_ref):
        dm = _mm_nt(dx1_ref[...], wo_ref[...])
        gp = gp_ref[...]
        sa, sb = _sig(gp[:, :D]), _sig(gp[:, D:])
        dpa = (dm * sa).astype(BF)
        dpb = (dm * sb).astype(BF)
        dpa_ref[...] = dpa
        dpb_ref[...] = dpb
        dgp_ref[:, :D] = dm * pa_ref[...] * sa * (1.0 - sa)
        dgp_ref[:, D:] = dm * pb_ref[...] * sb * (1.0 - sb)
        dya = jnp.zeros((TM, S5W), F32)
        for k in range(NCHIP):
            dya = dya + _mm_nt(dpa[:, k * AC:(k + 1) * AC], wa_ref[k])
        dya_ref[...] = dya
        dyb_ref[...] = _mm_nt(dpb, wb_ref[...])

    return pl.pallas_call(
        body, name="merge_bwd", grid=(L // TM,),
        in_specs=[_tok(D), _tok(2 * D), _tok(D), _tok(D), _full((NCHIP, S5W, AC)), _full((LW, D)), _full((D, D))],
        out_specs=[_tok(S5W), _tok(LW), _tok(2 * D), _tok(D), _tok(D)],
        out_shape=[_sds((L, S5W)), _sds((L, LW)), _sds((L, 2 * D)), _sds((L, D), BF), _sds((L, D), BF)],
        compiler_params=_params(40),
    )(dx1, gp, pa, pb, w_a, w_b, w_o)


def _chunk_tok(width):
    return pl.BlockSpec((NCHIP, TM, width), lambda i: (0, i, 0))


def _ffn_fwd(x1, g_ffn, wg, wu, wd):
    L = x1.shape[0]

    def body(x_ref, g_ref, wg_hbm, wu_hbm, wd_hbm, x2_ref, h2_ref, gg_ref, uu_ref, wg_vm, wu_vm, wd_vm):
        @pl.when(pl.program_id(0) == 0)
        def _():
            pltpu.sync_copy(wg_hbm, wg_vm)
            pltpu.sync_copy(wu_hbm, wu_vm)
            pltpu.sync_copy(wd_hbm, wd_vm)

        x = x_ref[...]
        xh, _ = _rms(x)
        h2 = (xh * g_ref[...]).astype(BF)
        h2_ref[...] = h2
        out = x
        for c in range(NCHIP):
            gg = lax.dot_general(h2, wg_vm[c], (((1,), (1,)), ((), ())), preferred_element_type=F32)
            uu = lax.dot_general(h2, wu_vm[c], (((1,), (1,)), ((), ())), preferred_element_type=F32)
            gg_ref[c] = gg.astype(BF)
            uu_ref[c] = uu.astype(BF)
            act = (gg * _sig(gg) * uu).astype(BF)
            out = out + jnp.dot(act, wd_vm[c], preferred_element_type=F32)
        x2_ref[...] = out

    return pl.pallas_call(
        body, name="ffn_fwd", grid=(L // TM,),
        in_specs=[_tok(D), _full((1, D)), ANY, ANY, ANY],
        out_specs=[_tok(D), _tok(D), _chunk_tok(FC), _chunk_tok(FC)],
        out_shape=[_sds((L, D)), _sds((L, D), BF), _sds((NCHIP, L, FC), BF), _sds((NCHIP, L, FC), BF)],
        scratch_shapes=[pltpu.VMEM((NCHIP, FC, D), BF)] * 3,
        compiler_params=_params(52),
    )(x1, g_ffn, wg, wu, wd)


def _ffn_bwd(x1, dx2, gg, uu, g_ffn, wg, wu, wd):
    L = x1.shape[0]

    def body(x_ref, dx2_ref, gg_ref, uu_ref, g_ref, wg_hbm, wu_hbm, wd_hbm,
             dx1_ref, act_ref, dgg_ref, duu_ref, dg_ref, wg_vm, wu_vm, wd_vm):
        @pl.when(pl.program_id(0) == 0)
        def _():
            pltpu.sync_copy(wg_hbm, wg_vm)
            pltpu.sync_copy(wu_hbm, wu_vm)
            pltpu.sync_copy(wd_hbm, wd_vm)
            dg_ref[...] = jnp.zeros_like(dg_ref)

        dx2 = dx2_ref[...]
        dx2b = dx2.astype(BF)
        dh2 = jnp.zeros((TM, D), F32)
        for c in range(NCHIP):
            g = gg_ref[c].astype(F32)
            u = uu_ref[c].astype(F32)
            s = _sig(g)
            silu = g * s
            act_ref[c] = (silu * u).astype(BF)
            dact = lax.dot_general(dx2b, wd_vm[c], (((1,), (1,)), ((), ())), preferred_element_type=F32)
            dg = (dact * u * s * (1.0 + g * (1.0 - s))).astype(BF)
            du = (dact * silu).astype(BF)
            dgg_ref[c] = dg
            duu_ref[c] = du
            dh2 = dh2 + jnp.dot(dg, wg_vm[c], preferred_element_type=F32)
            dh2 = dh2 + jnp.dot(du, wu_vm[c], preferred_element_type=F32)
        xh, r = _rms(x_ref[...])
        dg_ref[0:1, :] += _colsum(dh2 * xh)
        dx1_ref[...] = dx2 + _rms_bwd(dh2, xh, r, g_ref[...])

    return pl.pallas_call(
        body, name="ffn_bwd", grid=(L // TM,),
        in_specs=[_tok(D), _tok(D), _chunk_tok(FC), _chunk_tok(FC), _full((1, D)), ANY, ANY, ANY],
        out_specs=[_tok(D), _chunk_tok(FC), _chunk_tok(FC), _chunk_tok(FC), _full((SUB, D))],
        out_shape=[_sds((L, D)), _sds((NCHIP, L, FC), BF), _sds((NCHIP, L, FC), BF), _sds((NCHIP, L, FC), BF),
                   _sds((SUB, D))],
        scratch_shapes=[pltpu.VMEM((NCHIP, FC, D), BF)] * 3,
        compiler_params=_params(56),
    )(x1, dx2, gg, uu, g_ffn, wg, wu, wd)


def _ple_loss(x2, p, tgt, g_pg, w_pg, b_pg, w_ple, g_ple, g_final):
    L = x2.shape[0]

    def body(x2_ref, p_ref, t_ref, gpg_ref, wpg_ref, bpg_ref, wple_ref, gple_ref, gf_ref,
             dx2_ref, n2_ref, dpre_ref, de0_ref, acc_ref):
        @pl.when(pl.program_id(0) == 0)
        def _():
            acc_ref[...] = jnp.zeros_like(acc_ref)

        x2 = x2_ref[...]
        x2h, r2 = _rms(x2)
        n2 = (x2h * gpg_ref[...]).astype(BF)
        n2_ref[...] = n2
        gate = _sig(jnp.dot(n2, wpg_ref[...], preferred_element_type=F32) + bpg_ref[...])
        pb = p_ref[...].astype(BF)
        e0 = jnp.concatenate([jnp.dot(pb, wple_ref[k], preferred_element_type=F32) for k in range(NCHIP)], axis=1)
        e0h, re = _rms(e0)
        e = e0h * gple_ref[...]
        x3 = x2 + gate * e
        x3h, r3 = _rms(x3)
        diff = x3h * gf_ref[...] - t_ref[...]
        acc_ref[4:5, :] += _colsum(diff * diff) * (0.5 / D)
        dy = diff * (1.0 / D)
        acc_ref[3:4, :] += _colsum(dy * x3h)
        dx3 = _rms_bwd(dy, x3h, r3, gf_ref[...])
        de = dx3 * gate
        acc_ref[2:3, :] += _colsum(de * e0h)
        de0_ref[...] = _rms_bwd(de, e0h, re, gple_ref[...]).astype(BF)
        dpre = dx3 * e * gate * (1.0 - gate)
        acc_ref[1:2, :] += _colsum(dpre)
        dpreb = dpre.astype(BF)
        dpre_ref[...] = dpreb
        dn2 = lax.dot_general(dpreb, wpg_ref[...], (((1,), (1,)), ((), ())), preferred_element_type=F32)
        acc_ref[0:1, :] += _colsum(dn2 * x2h)
        dx2_ref[...] = dx3 + _rms_bwd(dn2, x2h, r2, gpg_ref[...])

    return pl.pallas_call(
        body, name="ple_loss", grid=(L // TM,),
        in_specs=[_tok(D), _tok(PLE), _tok(D), _full((1, D)), _full((D, D)), _full((1, D)), _full((NCHIP, PLE, AC)),
                  _full((1, D)), _full((1, D))],
        out_specs=[_tok(D), _tok(D), _tok(D), _tok(D), _full((SUB, D))],
        out_shape=[_sds((L, D)), _sds((L, D), BF), _sds((L, D), BF), _sds((L, D), BF), _sds((SUB, D))],
        compiler_params=_params(40),
    )(x2, p, tgt, g_pg, w_pg, b_pg, w_ple, g_ple, g_final)


def _tn(name, a, b, col_chunk=None):
    L = a.shape[-2]
    m, n = a.shape[-1], b.shape[-1]
    if a.ndim == 3 or b.ndim == 3:
        nj, bn = (a if a.ndim == 3 else b).shape[0], n
        a_spec = (pl.BlockSpec((None, TK, m), lambda j, t: (j, t, 0)) if a.ndim == 3
                  else pl.BlockSpec((TK, m), lambda j, t: (t, 0)))
        b_spec = (pl.BlockSpec((None, TK, n), lambda j, t: (j, t, 0)) if b.ndim == 3
                  else pl.BlockSpec((TK, n), lambda j, t: (t, 0)))
        out_spec, out_shape = pl.BlockSpec((None, m, n), lambda j, t: (j, 0, 0)), _sds((nj, m, n))
    else:
        bn = col_chunk
        if bn is None:
            bn = next((cand for cand in (1024, 512) if n > cand and n % cand == 0), n)
        nj = n // bn
        a_spec = pl.BlockSpec((TK, m), lambda j, t: (t, 0))
        b_spec = pl.BlockSpec((TK, bn), lambda j, t: (t, j))
        if col_chunk is None:
            out_spec, out_shape = pl.BlockSpec((m, bn), lambda j, t: (0, j)), _sds((m, n))
        else:
            out_spec, out_shape = pl.BlockSpec((None, m, bn), lambda j, t: (j, 0, 0)), _sds((nj, m, bn))

    def body(a_ref, b_ref, o_ref):
        @pl.when(pl.program_id(1) == 0)
        def _():
            o_ref[...] = jnp.zeros_like(o_ref)

        o_ref[...] += _mm_tn(a_ref[...], b_ref[...])

    return pl.pallas_call(
        body, name=name, grid=(nj, L // TK), in_specs=[a_spec, b_spec], out_specs=out_spec, out_shape=out_shape,
        compiler_params=pltpu.CompilerParams(dimension_semantics=("arbitrary", "arbitrary"),
                                             vmem_limit_bytes=40 * VMEM_MB),
    )(a, b)


LANE = 128


def _tn_blocks(name, a, b, ga, gb):
    L, n = a.shape[0], b.shape[1]
    per = LANE // ga
    wb = per * gb

    def body(a_ref, b_ref, o_ref, acc_ref):
        t = pl.program_id(1)

        @pl.when(t == 0)
        def _():
            acc_ref[...] = jnp.zeros_like(acc_ref)

        acc_ref[...] += _mm_tn(a_ref[...], b_ref[...])

        @pl.when(t == L // TK - 1)
        def _():
            rows = lax.broadcasted_iota(jnp.int32, (LANE, wb), 0) // ga
            cols = lax.broadcasted_iota(jnp.int32, (LANE, wb), 1) // gb
            kept = jnp.where(rows == cols, acc_ref[...], 0.0)
            o_ref[...] = jnp.sum(kept.reshape(per, ga, wb), axis=0)

    return pl.pallas_call(
        body, name=name, grid=(n // wb, L // TK),
        in_specs=[pl.BlockSpec((TK, LANE), lambda j, t: (t, j)), pl.BlockSpec((TK, wb), lambda j, t: (t, j))],
        out_specs=pl.BlockSpec((ga, wb), lambda j, t: (0, j)), out_shape=_sds((ga, n)),
        scratch_shapes=[pltpu.VMEM((LANE, wb), F32)],
        compiler_params=pltpu.CompilerParams(dimension_semantics=("arbitrary", "arbitrary"),
                                             vmem_limit_bytes=32 * VMEM_MB),
    )(a, b)


def _s5_discretize(lam_re, lam_im, log_dt, b_re, b_im):
    dt = jnp.exp(log_dt)[:, None]
    mag = jnp.exp(lam_re * dt)
    ar = mag * jnp.cos(lam_im * dt)
    ai = mag * jnp.sin(lam_im * dt)
    den = lam_re * lam_re + lam_im * lam_im
    nr = ar - 1.0
    fr = (nr * lam_re + ai * lam_im) / den
    fi = (ai * lam_re - nr * lam_im) / den
    bbr = fr[:, None, :] * b_re - fi[:, None, :] * b_im
    bbi = fr[:, None, :] * b_im + fi[:, None, :] * b_re
    return ar, ai, bbr, bbi


def _scan_constants(ar, ai):
    ar, ai = ar.reshape(1, GN), ai.reshape(1, GN)
    pw = [(jnp.ones_like(ar), jnp.zeros_like(ai))]
    for _ in range(SUB):
        pr, pi = pw[-1]
        pw.append((pr * ar - pi * ai, pr * ai + pi * ar))
    row = lax.broadcasted_iota(jnp.int32, (SUB, GN), 0)

    def build(reverse):
        sign = -1.0 if reverse else 1.0
        blocks = []
        for sh in (1, 2, 4):
            keep = (row < SUB - sh) if reverse else (row >= sh)
            blocks += [jnp.where(keep, pw[sh][0], 0.0), jnp.where(keep, sign * pw[sh][1], 0.0)]
        order = [SUB - i for i in range(SUB)] if reverse else [i + 1 for i in range(SUB)]
        blocks += [jnp.concatenate([pw[k][0] for k in order], 0), jnp.concatenate([sign * pw[k][1] for k in order], 0)]
        return jnp.concatenate(blocks, 0)

    return build(False), build(True)


def _blockdiag(blocks):
    g, r, c = blocks.shape
    eye = jnp.eye(g, dtype=blocks.dtype)
    return (blocks[:, :, None, :] * eye[:, None, :, None]).reshape(g * r, g * c)


def _local_step(x, p, tgt, w):
    rows_of = lambda a: a.reshape(NCHIP * a.shape[1], a.shape[2])
    w_glu, w_b_out, w_o, w_pg = (rows_of(w[n]) for n in ("w_glu", "w_b_out", "w_o", "w_ple_gate"))
    ar, ai, bbr, bbi = _s5_discretize(w["lam_re"], w["lam_im"], w["log_dt"], w["s5_b_re"], w["s5_b_im"])
    con, con_rev = _scan_constants(ar, ai)
    bbr_d = _blockdiag(bbr).astype(BF)
    bbi_d = _blockdiag(bbi).astype(BF)
    ccr_d = _blockdiag(jnp.swapaxes(w["s5_c_re"], 1, 2)).astype(BF)
    cci_d = _blockdiag(jnp.swapaxes(w["s5_c_im"], 1, 2)).astype(BF)
    dsk = w["s5_d"].reshape(1, S5W)
    wr_d = _blockdiag(w["w_r"]).astype(BF)
    wi_d = _blockdiag(w["w_i"]).astype(BF)
    lam = w["lru_lambda"].reshape(1, LW)
    sp = jax.nn.softplus(-lam)
    b_r, b_i = w["b_r"].reshape(1, LW), w["b_i"].reshape(1, LW)
    row = lambda name: w[name].reshape(1, -1)

    h, ua, ub, gp = _inproj_fwd(x, row("g_mix"), w["w_in"], row("b_in"))
    sr, si, y, zg, ya = _s5_fwd(ua, bbr_d, bbi_d, ccr_d, cci_d, dsk, con, w_glu, row("b_glu"))
    xc, rg, ig, yb, hp = _lru_fwd(ub, w["conv_w"], row("conv_b"), wr_d, wi_d, b_r, b_i, sp)
    x1, pa, pb, merged = _merge_fwd(x, ya, yb, gp, w["w_a_out"], w_b_out, w_o)
    x2, h2, gg, uu = _ffn_fwd(x1, row("g_ffn"), w["w_ffn_gate"], w["w_ffn_up"], w["w_ffn_down"])
    dx2, n2, dpre, de0, acc_p = _ple_loss(x2, p, tgt, row("g_ple_gate"), w_pg, row("b_ple_gate"),
                                          w["w_ple"], row("g_ple"), row("g_final"))
    dx1, act, dgg, duu, acc_f = _ffn_bwd(x1, dx2, gg, uu, row("g_ffn"), w["w_ffn_gate"], w["w_ffn_up"], w["w_ffn_down"])
    dya, dyb, dgp, dpa, dpb = _merge_bwd(dx1, gp, pa, pb, w["w_a_out"], w_b_out, w_o)
    dua, dq, dy, lr, li, acc_a, acc_s = _s5_bwd(dya, y, ua, sr, si, bbr_d, bbi_d, ccr_d, cci_d, dsk, con_rev,
                                                w_glu, row("b_glu"))
    dub, dpr, dpi, acc_l = _lru_bwd(dyb, xc, rg, ig, hp, ub, w["conv_w"], wr_d, wi_d, sp, -_sig(-lam))
    gx, dz, acc_g, acc_b = _inproj_bwd(x, dx1, dua, dub, dgp, row("g_mix"), w["w_in"])

    quarters = lambda a: a.reshape(NCHIP, a.shape[0] // NCHIP, a.shape[1])
    g = {
        "w_in": _tn("dw_in", h, dz, col_chunk=QC),
        "w_glu": quarters(_tn("dw_glu", zg, dq)),
        "w_a_out": _tn("dw_a_out", ya, dpa, col_chunk=AC),
        "w_b_out": quarters(_tn("dw_b_out", yb, dpb)),
        "w_o": quarters(_tn("dw_o", merged, dx1)),
        "w_ffn_gate": _tn("dw_ffn_gate", dgg, h2),
        "w_ffn_up": _tn("dw_ffn_up", duu, h2),
        "w_ffn_down": _tn("dw_ffn_down", act, dx2),
        "w_ple_gate": quarters(_tn("dw_ple_gate", n2, dpre)),
        "w_ple": _tn("dw_ple", p, de0, col_chunk=AC),
    }
    sums = {"ple": acc_p, "ffn": acc_f, "mix": acc_g, "b_in": acc_b, "lru": acc_l, "s5": acc_s, "s5_a": acc_a}
    blocks = {
        "bb_re": _tn_blocks("d_bbr", ua, lr, NP, NS),
        "bb_im": _tn_blocks("d_bbi", ua, li, NP, NS),
        "cc_re": _tn_blocks("d_ccr", dy, sr, NP, NS),
        "cc_im": _tn_blocks("d_cci", dy, si, NP, NS),
        "w_r": _tn_blocks("dw_r", xc, dpr, HD, HD),
        "w_i": _tn_blocks("dw_i", xc, dpi, HD, HD),
    }
    return gx, g, sums, blocks


def _replicated_grads(w, sums, blocks):
    grouped = lambda e, groups: jnp.transpose(e.reshape(e.shape[0], groups, -1), (1, 0, 2))
    d_ar, d_ai = sums["s5_a"][0].reshape(NG, NS), sums["s5_a"][1].reshape(NG, NS)
    d_bbr, d_bbi = grouped(blocks["bb_re"], NG), grouped(blocks["bb_im"], NG)
    _, vjp = jax.vjp(_s5_discretize, w["lam_re"], w["lam_im"], w["log_dt"], w["s5_b_re"], w["s5_b_im"])
    g = dict(zip(("lam_re", "lam_im", "log_dt", "s5_b_re", "s5_b_im"), vjp((d_ar, d_ai, d_bbr, d_bbi))))
    g["s5_c_re"] = grouped(blocks["cc_re"], NG)
    g["s5_c_im"] = -grouped(blocks["cc_im"], NG)
    g["w_r"], g["w_i"] = grouped(blocks["w_r"], NH), grouped(blocks["w_i"], NH)
    g["s5_d"] = sums["s5"][0].reshape(NG, NP)
    g["b_r"] = sums["lru"][1].reshape(NH, HD)
    g["b_i"] = sums["lru"][2].reshape(NH, HD)
    return g


ACC_ROWS = {"g_mix": ("mix", 0), "b_in": ("b_in", 0), "g_ffn": ("ffn", 0), "g_ple_gate": ("ple", 0),
            "b_ple_gate": ("ple", 1), "g_ple": ("ple", 2), "g_final": ("ple", 3), "b_glu": ("s5", 1),
            "lru_lambda": ("lru", 0), "conv_b": ("lru", 3)}
LOSS_ROW = ("ple", 4)
CONV_W_ROWS = ("lru", 4)


SHARDED = [("w_in", (D, QC)), ("w_glu", (S5W // NCHIP, S5W)), ("w_a_out", (S5W, AC)), ("w_b_out", (LW // NCHIP, D)),
           ("w_o", (D // NCHIP, D)), ("w_ffn_gate", (FC, D)), ("w_ffn_up", (FC, D)), ("w_ffn_down", (FC, D)),
           ("w_ple_gate", (D // NCHIP, D)), ("w_ple", (PLE, AC))]
NSH = len(SHARDED)
TRANSPOSED = ("w_ffn_gate", "w_ffn_up", "s5_b_re", "s5_b_im")
CONV_SHARD = (4, LW // NCHIP)


def _mesh_pos():
    return lax.axis_index("x"), lax.axis_index("y"), lax.axis_index("c")


def _other_chips(x, y):
    return [(1 - x, y), (x, 1 - y), (1 - x, 1 - y)]


def _half_rows(c, rows, align):
    return pl.ds(pl.multiple_of(c * (rows // 2), align), rows // 2)


def _gather_weights(shards, conv_w):
    def body(*refs):
        srcs, conv_src = refs[:NSH], refs[NSH]
        outs, conv_out = refs[NSH + 1:2 * NSH + 1], refs[2 * NSH + 1]
        send_sems, recv_sems = refs[2 * NSH + 2:]
        x, y, c = _mesh_pos()
        k0 = 2 * x + y
        sib = (x, y, 1 - c)
        chips = _other_chips(x, y)
        chip_ids = [2 * chip[0] + chip[1] for chip in chips]

        def remote(src, dst, j, i, to):
            return pltpu.make_async_remote_copy(src_ref=src, dst_ref=dst, send_sem=send_sems.at[j, i],
                                                recv_sem=recv_sems.at[j, i], device_id=to, device_id_type=MESH)

        own = [remote(s, o.at[k0], 6, i, sib) for i, (s, o) in enumerate(zip(srcs, outs))]
        own.append(remote(conv_src, conv_out.at[k0], 6, NSH, sib))
        sends = list(own)
        for j, chip in enumerate(chips):
            for i, (s, o) in enumerate(zip(srcs, outs)):
                mine = _half_rows(c, s.shape[0], 16)
                sends.append(remote(s.at[mine], o.at[k0, mine], j, i, (*chip, c)))
            sends.append(remote(conv_src, conv_out.at[k0], j, NSH, (*chip, c)))
        for cp in sends:
            cp.start()
        for j, chip in enumerate(chips):
            kj = chip_ids[j]
            remote(conv_src, conv_out.at[kj], j, NSH, (*chip, c)).wait_recv()
            for i, (s, o) in enumerate(zip(srcs, outs)):
                mine = _half_rows(c, s.shape[0], 16)
                remote(s.at[mine], o.at[kj, mine], j, i, (*chip, c)).wait_recv()
                fwd = remote(o.at[kj, mine], o.at[kj, mine], 3 + j, i, sib)
                fwd.start()
                sends.append(fwd)
        for j in range(3):
            kj = chip_ids[j]
            for i, (s, o) in enumerate(zip(srcs, outs)):
                other = _half_rows(1 - c, s.shape[0], 16)
                remote(o.at[kj, other], o.at[kj, other], 3 + j, i, sib).wait_recv()
        for cp in own:
            cp.wait_recv()
        for cp in sends:
            cp.wait_send()

    out_shape = [_sds((NCHIP,) + s.shape, BF) for s in shards] + [_sds((NCHIP,) + CONV_SHARD)]
    return pl.pallas_call(
        body, name="gather_weights", in_specs=[ANY] * (NSH + 1), out_specs=[ANY] * (NSH + 1), out_shape=out_shape,
        scratch_shapes=[pltpu.SemaphoreType.DMA((7, NSH + 1)), pltpu.SemaphoreType.DMA((7, NSH + 1))],
    )(*shards, conv_w)


def _swap_sibling_halves(grads):
    def body(*refs):
        srcs, outs, (send_sems, recv_sems) = refs[:NSH], refs[NSH:2 * NSH], refs[2 * NSH:]
        x, y, c = _mesh_pos()
        cps = [pltpu.make_async_remote_copy(src_ref=s.at[:, _half_rows(1 - c, s.shape[1], 8)], dst_ref=o,
                                            send_sem=send_sems.at[i], recv_sem=recv_sems.at[i], device_id=(x, y, 1 - c),
                                            device_id_type=MESH) for i, (s, o) in enumerate(zip(srcs, outs))]
        for cp in cps:
            cp.start()
        for cp in cps:
            cp.wait()

    return pl.pallas_call(
        body, name="swap_sibling_halves", in_specs=[ANY] * NSH, out_specs=[ANY] * NSH,
        out_shape=[_sds((NCHIP, g.shape[1] // 2, g.shape[2])) for g in grads],
        scratch_shapes=[pltpu.SemaphoreType.DMA((NSH,)), pltpu.SemaphoreType.DMA((NSH,))],
    )(*grads)


def _add_sibling(name, c_idx, g, got):
    hr, cols = got.shape[1:]

    def body(c_ref, g_ref, got_ref, p_ref, pb_ref):
        s = g_ref[...] + got_ref[...]
        p_ref[...] = s
        pb_ref[...] = s.astype(BF)

    spec = pl.BlockSpec((None, hr, cols), lambda k, c_ref: (k, 0, 0))
    return pl.pallas_call(
        body, name="add_sibling_" + name,
        grid_spec=pltpu.PrefetchScalarGridSpec(
            num_scalar_prefetch=1, grid=(NCHIP,),
            in_specs=[pl.BlockSpec((None, hr, cols), lambda k, c_ref: (k, c_ref[0], 0)), spec],
            out_specs=[spec, spec]),
        out_shape=[_sds((NCHIP, hr, cols)), _sds((NCHIP, hr, cols), BF)],
        compiler_params=_params(32),
    )(c_idx, g, got)


def _exchange_chips(parts):
    def body(*refs):
        srcs, outs, (send_sems, recv_sems) = refs[:NSH], refs[NSH:2 * NSH], refs[2 * NSH:]
        x, y, c = _mesh_pos()
        cps = []
        for j, chip in enumerate(_other_chips(x, y)):
            for i, (s, o) in enumerate(zip(srcs, outs)):
                cps.append(pltpu.make_async_remote_copy(
                    src_ref=s.at[2 * chip[0] + chip[1]], dst_ref=o.at[j], send_sem=send_sems.at[j, i],
                    recv_sem=recv_sems.at[j, i], device_id=(*chip, c), device_id_type=MESH))
        for cp in cps:
            cp.start()
        for cp in cps:
            cp.wait()

    return pl.pallas_call(
        body, name="exchange_chips", in_specs=[ANY] * NSH, out_specs=[ANY] * NSH,
        out_shape=[_sds((3,) + p.shape[1:], BF) for p in parts],
        scratch_shapes=[pltpu.SemaphoreType.DMA((3, NSH)), pltpu.SemaphoreType.DMA((3, NSH))],
    )(*parts)


def _add_chips(name, kc_idx, p, got):
    hr, cols = got.shape[1:]

    def body(kc_ref, p_ref, got_ref, t_ref):
        t_ref[...] = ((p_ref[...] + got_ref[0].astype(F32)) + got_ref[1].astype(F32)) + got_ref[2].astype(F32)

    return pl.pallas_call(
        body, name="add_chips_" + name,
        grid_spec=pltpu.PrefetchScalarGridSpec(
            num_scalar_prefetch=1, grid=(1,),
            in_specs=[pl.BlockSpec((None, hr, cols), lambda i, kc_ref: (kc_ref[0], 0, 0)),
                      pl.BlockSpec((3, hr, cols), lambda i, kc_ref: (0, 0, 0))],
            out_specs=pl.BlockSpec((None, hr, cols), lambda i, kc_ref: (kc_ref[1], 0, 0))),
        out_shape=_sds((2, hr, cols)),
        compiler_params=_params(32),
    )(kc_idx, p, got)


def _join_sibling(halves):
    def body(*refs):
        bufs, (send_sems, recv_sems) = refs[NSH:2 * NSH], refs[2 * NSH:]
        x, y, c = _mesh_pos()
        sib = (x, y, 1 - c)
        sends = [pltpu.make_async_remote_copy(src_ref=b.at[c], dst_ref=b.at[c], send_sem=send_sems.at[i],
                                              recv_sem=recv_sems.at[i], device_id=sib, device_id_type=MESH)
                 for i, b in enumerate(bufs)]
        for cp in sends:
            cp.start()
        for i, b in enumerate(bufs):
            pltpu.make_async_remote_copy(src_ref=b.at[c], dst_ref=b.at[1 - c], send_sem=send_sems.at[i],
                                         recv_sem=recv_sems.at[i], device_id=sib, device_id_type=MESH).wait_recv()
        for cp in sends:
            cp.wait_send()

    return pl.pallas_call(
        body, name="join_sibling", in_specs=[ANY] * NSH, out_specs=[ANY] * NSH,
        out_shape=[_sds(h.shape) for h in halves], input_output_aliases={i: i for i in range(NSH)},
        scratch_shapes=[pltpu.SemaphoreType.DMA((NSH,)), pltpu.SemaphoreType.DMA((NSH,))],
    )(*halves)


def _allreduce_small(arrays):
    n = len(arrays)

    def body(*refs):
        srcs, outs = refs[:n], refs[n:2 * n]
        sibs, chip_bufs = refs[2 * n:3 * n], refs[3 * n:4 * n]
        send_sems, recv_sems = refs[4 * n:]
        x, y, c = _mesh_pos()
        k0 = 2 * x + y

        def remote(src, dst, j, i, to):
            return pltpu.make_async_remote_copy(src_ref=src, dst_ref=dst, send_sem=send_sems.at[j, i],
                                                recv_sem=recv_sems.at[j, i], device_id=to, device_id_type=MESH)

        swaps = [remote(s, b, 0, i, (x, y, 1 - c)) for i, (s, b) in enumerate(zip(srcs, sibs))]
        for cp in swaps:
            cp.start()
        for cp in swaps:
            cp.wait()
        for s, b, buf in zip(srcs, sibs, chip_bufs):
            buf[k0] = s[...] + b[...]
        chips = _other_chips(x, y)
        sends = [remote(buf.at[k0], buf.at[k0], 1 + j, i, (*chip, c))
                 for j, chip in enumerate(chips) for i, buf in enumerate(chip_bufs)]
        for cp in sends:
            cp.start()
        for j, chip in enumerate(chips):
            for i, buf in enumerate(chip_bufs):
                remote(buf.at[k0], buf.at[2 * chip[0] + chip[1]], 1 + j, i, (*chip, c)).wait_recv()
        for cp in sends:
            cp.wait_send()
        for o, buf in zip(outs, chip_bufs):
            o[...] = ((buf[0] + buf[1]) + buf[2]) + buf[3]

    specs = [_full(a.shape) for a in arrays]
    return pl.pallas_call(
        body, name="allreduce_small", grid=(1,), in_specs=specs, out_specs=specs,
        out_shape=[_sds(a.shape) for a in arrays],
        scratch_shapes=([pltpu.VMEM(a.shape, F32) for a in arrays] + [pltpu.VMEM((NCHIP,) + a.shape, F32) for a in arrays]
                        + [pltpu.SemaphoreType.DMA((4, n)), pltpu.SemaphoreType.DMA((4, n))]),
        compiler_params=_params(32),
    )(*arrays)


def _adamw_terms(w, g, m, v):
    m = ADAM_B1 * m + (1.0 - ADAM_B1) * g
    v = ADAM_B2 * v + (1.0 - ADAM_B2) * jnp.square(g)
    m_hat = m / (1.0 - ADAM_B1 ** ADAM_STEP)
    v_hat = v / (1.0 - ADAM_B2 ** ADAM_STEP)
    return -ADAM_LR * (m_hat / (jnp.sqrt(v_hat) + ADAM_EPS) + ADAM_WD * w), m, v


def _adamw(name, w, g, m, v):
    r, c = w.shape
    rows = max(b for b in range(SUB, r + 1, SUB) if r % b == 0 and b * c * 4 <= 3 * VMEM_MB // 2)

    def body(w_ref, g_ref, m_ref, v_ref, d_ref, nm_ref, nv_ref):
        d_ref[...], nm_ref[...], nv_ref[...] = _adamw_terms(w_ref[...], g_ref[...], m_ref[...], v_ref[...])

    spec = pl.BlockSpec((rows, c), lambda i: (i, 0))
    return pl.pallas_call(
        body, name=name, grid=(r // rows,), in_specs=[spec] * 4, out_specs=[spec] * 3,
        out_shape=[_sds((r, c))] * 3, compiler_params=_params(40),
    )(w, g, m, v)


def _adamw_replicated(sums, row_of, direct):
    ns, nr, nd = len(sums), len(row_of), len(direct)

    def body(*refs):
        sum_refs = refs[:ns]
        ins = refs[ns:ns + 3 * nr + 4 * nd]
        outs = refs[ns + 3 * nr + 4 * nd:]
        for i, (_, _, _, si, row) in enumerate(row_of):
            w_ref, m_ref, v_ref = ins[3 * i:3 * i + 3]
            g = sum_refs[si][row:row + 1, :]
            outs[4 * i][...] = g
            outs[4 * i + 1][...], outs[4 * i + 2][...], outs[4 * i + 3][...] = _adamw_terms(w_ref[...], g, m_ref[...], v_ref[...])
        for i in range(nd):
            w_ref, m_ref, v_ref, g_ref = ins[3 * nr + 4 * i:3 * nr + 4 * i + 4]
            o = outs[4 * (nr + i):4 * (nr + i) + 4]
            g = g_ref[...]
            o[0][...] = g
            o[1][...], o[2][...], o[3][...] = _adamw_terms(w_ref[...], g, m_ref[...], v_ref[...])

    operands = list(sums)
    shapes = []
    for w, m, v, _, _ in row_of:
        operands += [w, m, v]
        shapes += [w.shape] * 4
    for w, m, v, g in direct:
        operands += [w, m, v, g]
        shapes += [w.shape] * 4
    flat = pl.pallas_call(
        body, name="adamw_replicated", grid=(1,), in_specs=[_full(a.shape) for a in operands],
        out_specs=[_full(s) for s in shapes], out_shape=[_sds(s) for s in shapes],
        compiler_params=_params(56),
    )(*operands)
    return [flat[4 * i:4 * i + 4] for i in range(nr + nd)]


INPUT_NAMES = (["x", "p"] + [n for n in
               ["g_mix", "w_in", "b_in", "lam_re", "lam_im", "log_dt", "s5_b_re", "s5_b_im", "s5_c_re", "s5_c_im", "s5_d",
                "w_glu", "b_glu", "conv_w", "conv_b", "w_r", "b_r", "w_i", "b_i", "lru_lambda", "w_a_out", "w_b_out", "w_o",
                "g_ffn", "w_ffn_gate", "w_ffn_up", "w_ffn_down", "g_ple_gate", "w_ple_gate", "b_ple_gate", "w_ple", "g_ple",
                "g_final"]])
WEIGHT_NAMES = INPUT_NAMES[2:]


def kernel(*args):
    names = INPUT_NAMES + ["loss_target"] + ["m_" + n for n in WEIGHT_NAMES] + ["v_" + n for n in WEIGHT_NAMES]
    assert len(args) == len(names)
    given = dict(zip(names, args))

    def view(name):
        a = given[name]
        return jnp.swapaxes(a, -1, -2) if name.endswith(TRANSPOSED) else a

    def unview(name, a):
        return jnp.swapaxes(a, -1, -2) if name in TRANSPOSED else a

    def local(name):
        return view(name) if name.endswith("g_final") else view(name)[0]

    xi, yi, ci = _mesh_pos()
    k0 = 2 * xi + yi
    x, p, tgt = given["x"][0], given["p"][0, 0], given["loss_target"][0]

    shard_names = [n for n, _ in SHARDED]
    got = _gather_weights([local(n).astype(BF) for n in shard_names], local("conv_w"))
    w = dict(zip(shard_names, got[:NSH]))
    w["conv_w"] = jnp.transpose(got[NSH], (1, 0, 2)).reshape(4, LW)
    replicated = [n for n in WEIGHT_NAMES if n not in w]
    for n in replicated:
        w[n] = local(n)

    gx, g, sums, blocks = _local_step(x, p, tgt, w)

    c_idx = jnp.reshape(ci, (1,)).astype(jnp.int32)
    k_idx = jnp.stack([k0, ci]).astype(jnp.int32)
    grads = [g[n] for n in shard_names]
    parts = [_add_sibling(n, c_idx, gr, rx) for n, gr, rx in zip(shard_names, grads, _swap_sibling_halves(grads))]
    arrived = _exchange_chips([pb for _, pb in parts])
    halves = [_add_chips(n, k_idx, pf, rx) for n, (pf, _), rx in zip(shard_names, parts, arrived)]
    results = {}
    for (n, shape), both in zip(SHARDED, _join_sibling(halves)):
        total = both.reshape(shape)
        delta, new_m, new_v = _adamw("adamw_" + n, local(n), total, local("m_" + n), local("v_" + n))
        for kind, arr in zip(("grad", "delta", "new_m", "new_v"), (total, delta, new_m, new_v)):
            results[kind, n] = unview(n, arr[None])

    sum_names, block_names = list(sums), list(blocks)
    red = _allreduce_small([sums[n] for n in sum_names] + [blocks[n] for n in block_names])
    sums = dict(zip(sum_names, red[:len(sum_names)]))
    blocks = dict(zip(block_names, red[len(sum_names):]))
    loss = jnp.sum(sums[LOSS_ROW[0]][LOSS_ROW[1]])
    direct_g = _replicated_grads(w, sums, blocks)
    conv_rows = sums[CONV_W_ROWS[0]][CONV_W_ROWS[1]:CONV_W_ROWS[1] + 4]
    direct_g["conv_w"] = lax.dynamic_slice(conv_rows, (0, k0 * CONV_SHARD[1]), CONV_SHARD)
    as_row = lambda a: a.reshape(1, -1)
    row_names = list(ACC_ROWS)
    row_of = [(as_row(given[n]), as_row(given["m_" + n]), as_row(given["v_" + n]),
               sum_names.index(ACC_ROWS[n][0]), ACC_ROWS[n][1]) for n in row_names]
    direct_names = list(direct_g)
    direct = [(view(n), view("m_" + n), view("v_" + n), direct_g[n].reshape(view(n).shape)) for n in direct_names]
    done = _adamw_replicated([sums[n] for n in sum_names], row_of, direct)
    for n, four in zip(row_names + direct_names, done):
        for kind, arr in zip(("grad", "delta", "new_m", "new_v"), four):
            results[kind, n] = unview(n, arr).reshape(given[n].shape)

    out = [loss, gx[None]]
    for kind in ("grad", "delta", "new_m", "new_v"):
        out += [results[kind, n] for n in WEIGHT_NAMES]
    return tuple(out)
```

```python
import functools
import math

import jax
import jax.numpy as jnp
from jax import lax
from jax.experimental import pallas as pl
from jax.experimental.pallas import tpu as pltpu

F32 = jnp.float32
BF = jnp.bfloat16

D = 1024
S5W = 512
NG, NS, NP = 32, 64, 16
GN = NG * NS
LW = 1024
NH, HD = 16, 64
LRU_C = 8.0
FH = 2816
NCHIP = 4
FC = FH // NCHIP
PLE = 256
INC = S5W + LW + 2 * D
EPS = 1e-6
ADAM_LR, ADAM_B1, ADAM_B2, ADAM_EPS, ADAM_WD, ADAM_STEP = 0.001, 0.9, 0.999, 1e-08, 0.01, 10

TM = 256
TK = 512
LC = 512
SUB = 8
VMEM_MB = 1024 * 1024
MESH = pl.DeviceIdType.MESH
ANY = pl.BlockSpec(memory_space=pl.ANY)


def _mm(a, b):
    return jnp.dot(a.astype(BF), b.astype(BF), preferred_element_type=F32)


def _mm_nt(a, b):
    return lax.dot_general(a.astype(BF), b.astype(BF), (((1,), (1,)), ((), ())), preferred_element_type=F32)


def _mm_tn(a, b):
    return lax.dot_general(a.astype(BF), b.astype(BF), (((0,), (0,)), ((), ())), preferred_element_type=F32)


def _rms(x):
    r = lax.rsqrt(jnp.mean(x * x, axis=-1, keepdims=True) + EPS)
    return x * r, r


def _rms_bwd(dy, xh, r, g):
    dxh = dy * g
    return r * (dxh - xh * jnp.mean(dxh * xh, axis=-1, keepdims=True))


def _colsum(x):
    return jnp.sum(x, axis=0, keepdims=True)


def _sig(x):
    return jax.nn.sigmoid(x)


def _gelu_grad(x):
    c = math.sqrt(2.0 / math.pi)
    t = jnp.tanh(c * (x + 0.044715 * x * x * x))
    return 0.5 * (1.0 + t) + 0.5 * x * (1.0 - t * t) * c * (1.0 + 3.0 * 0.044715 * x * x)


def _neg_expm1(x):
    series = -x * (1.0 + x * (0.5 + x * (1.0 / 6.0 + x * (1.0 / 24.0))))
    return jnp.where(x > -0.03, series, 1.0 - jnp.exp(x))


def _tok(width):
    return pl.BlockSpec((TM, width), lambda i: (i, 0))


def _tok_rev(width, nt):
    return pl.BlockSpec((TM, width), lambda i: (nt - 1 - i, 0))


def _full(shape):
    return pl.BlockSpec(shape, lambda i: (0,) * len(shape))


def _params(vmem_mb, **kw):
    return pltpu.CompilerParams(dimension_semantics=("arbitrary",), vmem_limit_bytes=vmem_mb * VMEM_MB, **kw)


def _sds(shape, dtype=F32):
    return jax.ShapeDtypeStruct(shape, dtype)


def _pallas_call(body, **kw):
    call = pl.pallas_call(body, **kw)

    def run(*operands):
        return call(*[pltpu.with_memory_space_constraint(o, pltpu.HBM) if jnp.issubdtype(o.dtype, jnp.floating) else o
                      for o in operands])

    return run


def _row_iota(width):
    return lax.broadcasted_iota(jnp.int32, (SUB, width), 0)


def _bcast_row(x, row):
    return jnp.broadcast_to(x[row:row + 1, :], x.shape)


def _slab(k):
    return pl.ds(pl.multiple_of(k * SUB, SUB), SUB)


QC = INC // NCHIP
Z_PARTS = ((0, S5W), (S5W, S5W + LW), (S5W + LW, INC))


def _inproj_fwd(x, g_mix, w_in, b_in):
    L = x.shape[0]

    def body(x_ref, g_ref, w_hbm, b_ref, h_ref, ua_ref, ub_ref, gp_ref, w_vm):
        @pl.when(pl.program_id(0) == 0)
        def _():
            pltpu.sync_copy(w_hbm, w_vm)

        xh, _ = _rms(x_ref[...])
        h = (xh * g_ref[...]).astype(BF)
        h_ref[...] = h
        for k in range(NCHIP):
            lo, hi = k * QC, (k + 1) * QC
            z = jnp.dot(h, w_vm[k], preferred_element_type=F32) + b_ref[:, lo:hi]
            for ref, (a, b) in zip((ua_ref, ub_ref, gp_ref), Z_PARTS):
                s, e = max(lo, a), min(hi, b)
                if s < e:
                    ref[:, s - a:e - a] = z[:, s - lo:e - lo]

    return _pallas_call(
        body, name="inproj_fwd", grid=(L // TM,),
        in_specs=[_tok(D), _full((1, D)), ANY, _full((1, INC))],
        out_specs=[_tok(D), _tok(S5W), _tok(LW), _tok(2 * D)],
        out_shape=[_sds((L, D), BF), _sds((L, S5W)), _sds((L, LW)), _sds((L, 2 * D))],
        scratch_shapes=[pltpu.VMEM((NCHIP, D, QC), BF)],
        compiler_params=_params(40),
    )(x, g_mix, w_in, b_in)


def _inproj_bwd(x, dx1, dua, dub, dgp, g_mix, w_in):
    L = x.shape[0]

    def body(x_ref, dx1_ref, dua_ref, dub_ref, dgp_ref, g_ref, w_hbm, gx_ref, dz_ref, dg_ref, db_ref, w_vm):
        @pl.when(pl.program_id(0) == 0)
        def _():
            pltpu.sync_copy(w_hbm, w_vm)
            dg_ref[...] = jnp.zeros_like(dg_ref)
            db_ref[...] = jnp.zeros_like(db_ref)

        for src, (a, b) in zip((dua_ref, dub_ref, dgp_ref), Z_PARTS):
            d = src[...]
            dz_ref[:, a:b] = d.astype(BF)
            db_ref[0:1, a:b] += _colsum(d)
        dh = jnp.zeros((TM, D), F32)
        for k in range(NCHIP):
            dh = dh + lax.dot_general(dz_ref[:, k * QC:(k + 1) * QC], w_vm[k], (((1,), (1,)), ((), ())),
                                      preferred_element_type=F32)
        xh, r = _rms(x_ref[...])
        dg_ref[0:1, :] += _colsum(dh * xh)
        gx_ref[...] = dx1_ref[...] + _rms_bwd(dh, xh, r, g_ref[...])

    return _pallas_call(
        body, name="inproj_bwd", grid=(L // TM,),
        in_specs=[_tok(D), _tok(D), _tok(S5W), _tok(LW), _tok(2 * D), _full((1, D)), ANY],
        out_specs=[_tok(D), _tok(INC), _full((SUB, D)), _full((SUB, INC))],
        out_shape=[_sds((L, D)), _sds((L, INC), BF), _sds((SUB, D)), _sds((SUB, INC))],
        scratch_shapes=[pltpu.VMEM((NCHIP, D, QC), BF)],
        compiler_params=_params(40),
    )(x, dx1, dua, dub, dgp, g_mix, w_in)


def _cscan(xr_ref, xi_ref, con_ref, cr_ref, ci_ref, reverse):
    n_slab = xr_ref.shape[0] // SUB
    width = xr_ref.shape[1]
    for lc in range(width // LC):
        cols = slice(lc * LC, (lc + 1) * LC)
        con = [con_ref[SUB * j:SUB * (j + 1), cols] for j in range(8)]

        def step(k, carry, cols=cols, con=con):
            cr, ci = carry
            rows = _slab(n_slab - 1 - k if reverse else k)
            xr, xi = xr_ref[rows, cols], xi_ref[rows, cols]
            for j, sh in enumerate((1, 2, 4)):
                mr, mi = con[2 * j], con[2 * j + 1]
                pr = pltpu.roll(xr, SUB - sh if reverse else sh, 0)
                pi = pltpu.roll(xi, SUB - sh if reverse else sh, 0)
                xr, xi = xr + mr * pr - mi * pi, xi + mr * pi + mi * pr
            xr, xi = xr + con[6] * cr - con[7] * ci, xi + con[6] * ci + con[7] * cr
            xr_ref[rows, cols] = xr
            xi_ref[rows, cols] = xi
            row = 0 if reverse else SUB - 1
            return _bcast_row(xr, row), _bcast_row(xi, row)

        cr, ci = lax.fori_loop(0, n_slab, step, (cr_ref[:, cols], ci_ref[:, cols]))
        cr_ref[:, cols] = cr
        ci_ref[:, cols] = ci


def _s5_fwd(ua, bbr, bbi, ccr, cci, dsk, con, w_glu, b_glu):
    L = ua.shape[0]

    def body(ua_ref, bbr_hbm, bbi_hbm, ccr_hbm, cci_hbm, dsk_ref, con_ref, wg_ref, bg_ref,
             sr_ref, si_ref, y_ref, zg_ref, ya_ref, bbr_vm, bbi_vm, ccr_vm, cci_vm, cr_ref, ci_ref):
        @pl.when(pl.program_id(0) == 0)
        def _():
            pltpu.sync_copy(bbr_hbm, bbr_vm)
            pltpu.sync_copy(bbi_hbm, bbi_vm)
            pltpu.sync_copy(ccr_hbm, ccr_vm)
            pltpu.sync_copy(cci_hbm, cci_vm)
            cr_ref[...] = jnp.zeros_like(cr_ref)
            ci_ref[...] = jnp.zeros_like(ci_ref)

        u = ua_ref[...]
        ub = u.astype(BF)
        sr_ref[...] = jnp.dot(ub, bbr_vm[...], preferred_element_type=F32)
        si_ref[...] = jnp.dot(ub, bbi_vm[...], preferred_element_type=F32)
        _cscan(sr_ref, si_ref, con_ref, cr_ref, ci_ref, reverse=False)
        y = _mm(sr_ref[...], ccr_vm[...]) - _mm(si_ref[...], cci_vm[...]) + dsk_ref[...] * u
        y_ref[...] = y
        zg = jax.nn.gelu(y)
        zg_ref[...] = zg.astype(BF)
        q = _mm(zg, wg_ref[...]) + bg_ref[...]
        ya_ref[...] = (zg * _sig(q)).astype(BF)

    return _pallas_call(
        body, name="s5_fwd", grid=(L // TM,),
        in_specs=[_tok(S5W), ANY, ANY, ANY, ANY, _full((1, S5W)), _full((8 * SUB, GN)),
                  _full((S5W, S5W)), _full((1, S5W))],
        out_specs=[_tok(GN), _tok(GN), _tok(S5W), _tok(S5W), _tok(S5W)],
        out_shape=[_sds((L, GN)), _sds((L, GN)), _sds((L, S5W)), _sds((L, S5W), BF), _sds((L, S5W), BF)],
        scratch_shapes=[pltpu.VMEM((S5W, GN), BF), pltpu.VMEM((S5W, GN), BF), pltpu.VMEM((GN, S5W), BF),
                        pltpu.VMEM((GN, S5W), BF), pltpu.VMEM((SUB, GN), F32), pltpu.VMEM((SUB, GN), F32)],
        compiler_params=_params(44),
    )(ua, bbr, bbi, ccr, cci, dsk, con, w_glu, b_glu)


def _s5_bwd(dya, y, ua, sr, si, bbr, bbi, ccr, cci, dsk, con_rev, w_glu, b_glu):
    L = ua.shape[0]
    nt = L // TM
    spt = TM // SUB
    n_slab = spt

    def halo_map(i):
        return (jnp.maximum((nt - 1 - i) * spt - 1, 0), 0)

    def body(dya_ref, y_ref, ua_ref, sr_ref, si_ref, hr_ref, hi_ref, bbr_hbm, bbi_hbm, ccr_hbm, cci_hbm,
             dsk_ref, con_ref, wg_ref, bg_ref,
             dua_ref, dq_ref, dy_ref, lr_ref, li_ref, da_ref, dsm_ref,
             bbr_vm, bbi_vm, ccr_vm, cci_vm, cr_ref, ci_ref):
        i = pl.program_id(0)

        @pl.when(i == 0)
        def _():
            pltpu.sync_copy(bbr_hbm, bbr_vm)
            pltpu.sync_copy(bbi_hbm, bbi_vm)
            pltpu.sync_copy(ccr_hbm, ccr_vm)
            pltpu.sync_copy(cci_hbm, cci_vm)
            cr_ref[...] = jnp.zeros_like(cr_ref)
            ci_ref[...] = jnp.zeros_like(ci_ref)
            da_ref[...] = jnp.zeros_like(da_ref)
            dsm_ref[...] = jnp.zeros_like(dsm_ref)

        u = ua_ref[...]
        yv = y_ref[...]
        dya = dya_ref[...]
        zg = jax.nn.gelu(yv)
        sg = _sig(_mm(zg, wg_ref[...]) + bg_ref[...])
        dq = dya * zg * sg * (1.0 - sg)
        dq_ref[...] = dq.astype(BF)
        dzg = dya * sg + _mm_nt(dq, wg_ref[...])
        dy = dzg * _gelu_grad(yv)
        dyb = dy.astype(BF)
        dy_ref[...] = dyb
        dsm_ref[0:1, :] += _colsum(dy * u)
        dsm_ref[1:2, :] += _colsum(dq)
        lr_ref[...] = lax.dot_general(dyb, ccr_vm[...], (((1,), (1,)), ((), ())), preferred_element_type=F32)
        li_ref[...] = -lax.dot_general(dyb, cci_vm[...], (((1,), (1,)), ((), ())), preferred_element_type=F32)
        _cscan(lr_ref, li_ref, con_ref, cr_ref, ci_ref, reverse=True)

        first_tile = (i == nt - 1)
        row = _row_iota(LC)
        for lc in range(GN // LC):
            cols = slice(lc * LC, (lc + 1) * LC)
            h_r = jnp.where(first_tile, 0.0, hr_ref[:, cols])
            h_i = jnp.where(first_tile, 0.0, hi_ref[:, cols])

            def step(k, acc, cols=cols, h_r=h_r, h_i=h_i):
                ar, ai = acc
                rows = _slab(k)
                prev = _slab(jnp.maximum(k - 1, 0))
                pr = jnp.where(k == 0, h_r, sr_ref[prev, cols])
                pi = jnp.where(k == 0, h_i, si_ref[prev, cols])
                spr = pltpu.roll(jnp.where(row == SUB - 1, pr, sr_ref[rows, cols]), 1, 0)
                spi = pltpu.roll(jnp.where(row == SUB - 1, pi, si_ref[rows, cols]), 1, 0)
                lr, li = lr_ref[rows, cols], li_ref[rows, cols]
                return ar + lr * spr + li * spi, ai + li * spr - lr * spi

            zero = jnp.zeros((SUB, LC), F32)
            ar, ai = lax.fori_loop(0, n_slab, step, (zero, zero))
            da_ref[0:1, cols] += _colsum(ar)
            da_ref[1:2, cols] += _colsum(ai)

        dua_ref[...] = (dy * dsk_ref[...] + _mm_nt(lr_ref[...], bbr_vm[...]) + _mm_nt(li_ref[...], bbi_vm[...]))

    return _pallas_call(
        body, name="s5_bwd", grid=(nt,),
        in_specs=[_tok_rev(S5W, nt), _tok_rev(S5W, nt), _tok_rev(S5W, nt), _tok_rev(GN, nt), _tok_rev(GN, nt),
                  pl.BlockSpec((SUB, GN), halo_map), pl.BlockSpec((SUB, GN), halo_map),
                  ANY, ANY, ANY, ANY, _full((1, S5W)), _full((8 * SUB, GN)), _full((S5W, S5W)), _full((1, S5W))],
        out_specs=[_tok_rev(S5W, nt), _tok_rev(S5W, nt), _tok_rev(S5W, nt), _tok_rev(GN, nt), _tok_rev(GN, nt),
                   _full((SUB, GN)), _full((SUB, S5W))],
        out_shape=[_sds((L, S5W)), _sds((L, S5W), BF), _sds((L, S5W), BF), _sds((L, GN)), _sds((L, GN)),
                   _sds((SUB, GN)), _sds((SUB, S5W))],
        scratch_shapes=[pltpu.VMEM((S5W, GN), BF), pltpu.VMEM((S5W, GN), BF), pltpu.VMEM((GN, S5W), BF),
                        pltpu.VMEM((GN, S5W), BF), pltpu.VMEM((SUB, GN), F32), pltpu.VMEM((SUB, GN), F32)],
        compiler_params=_params(52),
    )(dya, y, ua, sr, si, sr, si, bbr, bbi, ccr, cci, dsk, con_rev, w_glu, b_glu)


def _lru_gate_terms(rg, sp):
    log_a = -LRU_C * rg * sp
    a = jnp.exp(log_a)
    mult = jnp.sqrt(_neg_expm1(2.0 * log_a))
    return a, mult


def _lru_fwd(ub, conv_w, conv_b, wr, wi, b_r, b_i, sp):
    L = ub.shape[0]
    n_slab = TM // SUB

    def body(ub_ref, cw_ref, cb_ref, wr_ref, wi_ref, br_ref, bi_ref, sp_ref,
             xc_ref, rg_ref, ig_ref, h_ref, hp_ref, a_ref, halo_ref, carry_ref):
        @pl.when(pl.program_id(0) == 0)
        def _():
            halo_ref[...] = jnp.zeros_like(halo_ref)
            carry_ref[...] = jnp.zeros_like(carry_ref)

        row = _row_iota(LW)
        taps = [cw_ref[k:k + 1, :] for k in range(4)]
        cb = cb_ref[...]

        def conv_step(k, prev):
            rows = _slab(k)
            cur = ub_ref[rows, :]
            acc = taps[3] * cur + cb
            for j in (1, 2, 3):
                acc = acc + taps[3 - j] * pltpu.roll(jnp.where(row >= SUB - j, prev, cur), j, 0)
            xc_ref[rows, :] = acc
            return cur

        halo_ref[...] = lax.fori_loop(0, n_slab, conv_step, halo_ref[...])

        xc = xc_ref[...]
        xcb = xc.astype(BF)
        rg = _sig(jnp.dot(xcb, wr_ref[...], preferred_element_type=F32) + br_ref[...])
        ig = _sig(jnp.dot(xcb, wi_ref[...], preferred_element_type=F32) + bi_ref[...])
        rg_ref[...] = rg
        ig_ref[...] = ig
        a, mult = _lru_gate_terms(rg, sp_ref[...])
        a_ref[...] = a
        h_ref[...] = mult * ig * xc

        rowc = _row_iota(LC)
        for lc in range(LW // LC):
            cols = slice(lc * LC, (lc + 1) * LC)

            def step(k, c, cols=cols):
                rows = _slab(k)
                av, b = a_ref[rows, cols], h_ref[rows, cols]
                for sh in (1, 2, 4):
                    keep = rowc >= sh
                    b = b + av * jnp.where(keep, pltpu.roll(b, sh, 0), 0.0)
                    av = av * jnp.where(keep, pltpu.roll(av, sh, 0), 1.0)
                h = b + av * c
                h_ref[rows, cols] = h
                hp_ref[rows, cols] = jnp.where(rowc == 0, c, pltpu.roll(h, 1, 0))
                return _bcast_row(h, SUB - 1)

            carry_ref[:, cols] = lax.fori_loop(0, n_slab, step, carry_ref[:, cols])

    return _pallas_call(
        body, name="lru_fwd", grid=(L // TM,),
        in_specs=[_tok(LW), _full((4, LW)), _full((1, LW)), _full((LW, LW)), _full((LW, LW)),
                  _full((1, LW)), _full((1, LW)), _full((1, LW))],
        out_specs=[_tok(LW)] * 5,
        out_shape=[_sds((L, LW))] * 5,
        scratch_shapes=[pltpu.VMEM((TM, LW), F32), pltpu.VMEM((SUB, LW), F32), pltpu.VMEM((SUB, LW), F32)],
        compiler_params=_params(40),
    )(ub, conv_w, conv_b, wr, wi, b_r, b_i, sp)


def _lru_bwd(dyb, xc, rg, ig, hp, ub, conv_w, wr, wi, sp, dsp):
    L = ub.shape[0]
    nt = L // TM
    spt = TM // SUB
    n_slab = spt

    def halo_map(i):
        return (jnp.maximum((nt - 1 - i) * spt - 1, 0), 0)

    def body(dh_ref, xc_ref, rg_ref, ig_ref, hp_ref, ub_ref, uh_ref, cw_ref, wr_ref, wi_ref, sp_ref, dsp_ref,
             dub_ref, dpr_ref, dpi_ref, acc_ref, a_ref, lam_ref, dxc_ref, carry_ref, next_ref):
        i = pl.program_id(0)

        @pl.when(i == 0)
        def _():
            carry_ref[...] = jnp.zeros_like(carry_ref)
            next_ref[...] = jnp.zeros_like(next_ref)
            acc_ref[...] = jnp.zeros_like(acc_ref)

        sp = sp_ref[...]
        rg, ig, xc = rg_ref[...], ig_ref[...], xc_ref[...]
        a, mult = _lru_gate_terms(rg, sp)
        a_ref[...] = a

        rowc = _row_iota(LC)
        for lc in range(LW // LC):
            cols = slice(lc * LC, (lc + 1) * LC)

            def step(k, c, cols=cols):
                rows = _slab(n_slab - 1 - k)
                av, dh = a_ref[rows, cols], dh_ref[rows, cols]
                b = av * dh
                for sh in (1, 2, 4):
                    keep = rowc < SUB - sh
                    b = b + av * jnp.where(keep, pltpu.roll(b, SUB - sh, 0), 0.0)
                    av = av * jnp.where(keep, pltpu.roll(av, SUB - sh, 0), 1.0)
                mu = b + av * c
                lam_ref[rows, cols] = dh + jnp.where(rowc == SUB - 1, c, pltpu.roll(mu, SUB - 1, 0))
                return _bcast_row(mu, 0)

            carry_ref[:, cols] = lax.fori_loop(0, n_slab, step, carry_ref[:, cols])

        lam = lam_ref[...]
        d_a = lam * hp_ref[...]
        d_mult = lam * ig * xc
        d_ig = lam * mult * xc
        dxc = lam * mult * ig
        d_log_a = d_a * a - d_mult * a * a / mult
        d_rg = (-LRU_C) * sp * d_log_a
        acc_ref[0:1, :] += _colsum((-LRU_C) * rg * d_log_a) * dsp_ref[...]
        dpr = d_rg * rg * (1.0 - rg)
        dpi = d_ig * ig * (1.0 - ig)
        acc_ref[1:2, :] += _colsum(dpr)
        acc_ref[2:3, :] += _colsum(dpi)
        dprb, dpib = dpr.astype(BF), dpi.astype(BF)
        dpr_ref[...] = dprb
        dpi_ref[...] = dpib
        dxc = dxc + _mm_nt(dprb, wr_ref[...]) + _mm_nt(dpib, wi_ref[...])
        dxc_ref[...] = dxc
        acc_ref[3:4, :] += _colsum(dxc)

        row = _row_iota(LW)
        taps = [cw_ref[k:k + 1, :] for k in range(4)]
        u_halo = jnp.where(i == nt - 1, 0.0, uh_ref[...])
        nxt_tile = next_ref[...]

        def conv_step(k, accs):
            rows = _slab(k)
            cur = dxc_ref[rows, :]
            nxt = jnp.where(k == n_slab - 1, nxt_tile, dxc_ref[_slab(jnp.minimum(k + 1, n_slab - 1)), :])
            ucur = ub_ref[rows, :]
            uprev = jnp.where(k == 0, u_halo, ub_ref[_slab(jnp.maximum(k - 1, 0)), :])
            du = taps[3] * cur
            new = [accs[3] + cur * ucur]
            for j in (1, 2, 3):
                du = du + taps[3 - j] * pltpu.roll(jnp.where(row < j, nxt, cur), SUB - j, 0)
                new.append(accs[3 - j] + cur * pltpu.roll(jnp.where(row >= SUB - j, uprev, ucur), j, 0))
            dub_ref[rows, :] = du
            return tuple(new[::-1])

        zero = jnp.zeros((SUB, LW), F32)
        accs = lax.fori_loop(0, n_slab, conv_step, (zero, zero, zero, zero))
        for k in range(4):
            acc_ref[4 + k:5 + k, :] += _colsum(accs[k])
        next_ref[...] = dxc_ref[0:SUB, :]

    return _pallas_call(
        body, name="lru_bwd", grid=(nt,),
        in_specs=[_tok_rev(LW, nt)] * 6 + [pl.BlockSpec((SUB, LW), halo_map), _full((4, LW)),
                                           _full((LW, LW)), _full((LW, LW)), _full((1, LW)), _full((1, LW))],
        out_specs=[_tok_rev(LW, nt), _tok_rev(LW, nt), _tok_rev(LW, nt), _full((SUB, LW))],
        out_shape=[_sds((L, LW)), _sds((L, LW), BF), _sds((L, LW), BF), _sds((SUB, LW))],
        scratch_shapes=[pltpu.VMEM((TM, LW), F32), pltpu.VMEM((TM, LW), F32), pltpu.VMEM((TM, LW), F32),
                        pltpu.VMEM((SUB, LW), F32), pltpu.VMEM((SUB, LW), F32)],
        compiler_params=_params(48),
    )(dyb, xc, rg, ig, hp, ub, ub, conv_w, wr, wi, sp, dsp)


AC = D // NCHIP


def _merge_fwd(x, ya, yb, gp, w_a, w_b, w_o):
    L = x.shape[0]

    def body(x_ref, ya_ref, yb_ref, gp_ref, wa_ref, wb_ref, wo_ref, x1_ref, pa_ref, pb_ref, mg_ref):
        ya = ya_ref[...]
        for k in range(NCHIP):
            pa_ref[:, k * AC:(k + 1) * AC] = jnp.dot(ya, wa_ref[k], preferred_element_type=F32)
        pb = _mm(yb_ref[...], wb_ref[...])
        pb_ref[...] = pb
        gp = gp_ref[...]
        merged = (_sig(gp[:, :D]) * pa_ref[...] + _sig(gp[:, D:]) * pb).astype(BF)
        mg_ref[...] = merged
        x1_ref[...] = x_ref[...] + jnp.dot(merged, wo_ref[...], preferred_element_type=F32)

    return _pallas_call(
        body, name="merge_fwd", grid=(L // TM,),
        in_specs=[_tok(D), _tok(S5W), _tok(LW), _tok(2 * D), _full((NCHIP, S5W, AC)), _full((LW, D)), _full((D, D))],
        out_specs=[_tok(D), _tok(D), _tok(D), _tok(D)],
        out_shape=[_sds((L, D)), _sds((L, D)), _sds((L, D)), _sds((L, D), BF)],
        compiler_params=_params(40),
    )(x, ya, yb, gp, w_a, w_b, w_o)


def _merge_bwd(dx1, gp, pa, pb, w_a, w_b, w_o):
    L = dx1.shape[0]

    def body(dx1_ref, gp_ref, pa_ref, pb_ref, wa_ref, wb_ref, wo_ref, dya_ref, dyb_ref, dgp_ref, dpa_ref, dpb_ref):
        dm = _mm_nt(dx1_ref[...], wo_ref[...])
        gp = gp_ref[...]
        sa, sb = _sig(gp[:, :D]), _sig(gp[:, D:])
        dpa = (dm * sa).astype(BF)
        dpb = (dm * sb).astype(BF)
        dpa_ref[...] = dpa
        dpb_ref[...] = dpb
        dgp_ref[:, :D] = dm * pa_ref[...] * sa * (1.0 - sa)
        dgp_ref[:, D:] = dm * pb_ref[...] * sb * (1.0 - sb)
        dya = jnp.zeros((TM, S5W), F32)
        for k in range(NCHIP):
            dya = dya + _mm_nt(dpa[:, k * AC:(k + 1) * AC], wa_ref[k])
        dya_ref[...] = dya
        dyb_ref[...] = _mm_nt(dpb, wb_ref[...])

    return _pallas_call(
        body, name="merge_bwd", grid=(L // TM,),
        in_specs=[_tok(D), _tok(2 * D), _tok(D), _tok(D), _full((NCHIP, S5W, AC)), _full((LW, D)), _full((D, D))],
        out_specs=[_tok(S5W), _tok(LW), _tok(2 * D), _tok(D), _tok(D)],
        out_shape=[_sds((L, S5W)), _sds((L, LW)), _sds((L, 2 * D)), _sds((L, D), BF), _sds((L, D), BF)],
        compiler_params=_params(40),
    )(dx1, gp, pa, pb, w_a, w_b, w_o)


def _chunk_tok(width):
    return pl.BlockSpec((NCHIP, TM, width), lambda i: (0, i, 0))


def _ffn_fwd(x1, g_ffn, wg, wu, wd):
    L = x1.shape[0]

    def body(x_ref, g_ref, wg_hbm, wu_hbm, wd_hbm, x2_ref, h2_ref, gg_ref, uu_ref, wg_vm, wu_vm, wd_vm):
        @pl.when(pl.program_id(0) == 0)
        def _():
            pltpu.sync_copy(wg_hbm, wg_vm)
            pltpu.sync_copy(wu_hbm, wu_vm)
            pltpu.sync_copy(wd_hbm, wd_vm)

        x = x_ref[...]
        xh, _ = _rms(x)
        h2 = (xh * g_ref[...]).astype(BF)
        h2_ref[...] = h2
        out = x
        for c in range(NCHIP):
            gg = lax.dot_general(h2, wg_vm[c], (((1,), (1,)), ((), ())), preferred_element_type=F32)
            uu = lax.dot_general(h2, wu_vm[c], (((1,), (1,)), ((), ())), preferred_element_type=F32)
            gg_ref[c] = gg.astype(BF)
            uu_ref[c] = uu.astype(BF)
            act = (gg * _sig(gg) * uu).astype(BF)
            out = out + jnp.dot(act, wd_vm[c], preferred_element_type=F32)
        x2_ref[...] = out

    return _pallas_call(
        body, name="ffn_fwd", grid=(L // TM,),
        in_specs=[_tok(D), _full((1, D)), ANY, ANY, ANY],
        out_specs=[_tok(D), _tok(D), _chunk_tok(FC), _chunk_tok(FC)],
        out_shape=[_sds((L, D)), _sds((L, D), BF), _sds((NCHIP, L, FC), BF), _sds((NCHIP, L, FC), BF)],
        scratch_shapes=[pltpu.VMEM((NCHIP, FC, D), BF)] * 3,
        compiler_params=_params(52),
    )(x1, g_ffn, wg, wu, wd)


def _ffn_bwd(x1, dx2, gg, uu, g_ffn, wg, wu, wd):
    L = x1.shape[0]

    def body(x_ref, dx2_ref, gg_ref, uu_ref, g_ref, wg_hbm, wu_hbm, wd_hbm,
             dx1_ref, act_ref, dgg_ref, duu_ref, dg_ref, wg_vm, wu_vm, wd_vm):
        @pl.when(pl.program_id(0) == 0)
        def _():
            pltpu.sync_copy(wg_hbm, wg_vm)
            pltpu.sync_copy(wu_hbm, wu_vm)
            pltpu.sync_copy(wd_hbm, wd_vm)
            dg_ref[...] = jnp.zeros_like(dg_ref)

        dx2 = dx2_ref[...]
        dx2b = dx2.astype(BF)
        dh2 = jnp.zeros((TM, D), F32)
        for c in range(NCHIP):
            g = gg_ref[c].astype(F32)
            u = uu_ref[c].astype(F32)
            s = _sig(g)
            silu = g * s
            act_ref[c] = (silu * u).astype(BF)
            dact = lax.dot_general(dx2b, wd_vm[c], (((1,), (1,)), ((), ())), preferred_element_type=F32)
            dg = (dact * u * s * (1.0 + g * (1.0 - s))).astype(BF)
            du = (dact * silu).astype(BF)
            dgg_ref[c] = dg
            duu_ref[c] = du
            dh2 = dh2 + jnp.dot(dg, wg_vm[c], preferred_element_type=F32)
            dh2 = dh2 + jnp.dot(du, wu_vm[c], preferred_element_type=F32)
        xh, r = _rms(x_ref[...])
        dg_ref[0:1, :] += _colsum(dh2 * xh)
        dx1_ref[...] = dx2 + _rms_bwd(dh2, xh, r, g_ref[...])

    return _pallas_call(
        body, name="ffn_bwd", grid=(L // TM,),
        in_specs=[_tok(D), _tok(D), _chunk_tok(FC), _chunk_tok(FC), _full((1, D)), ANY, ANY, ANY],
        out_specs=[_tok(D), _chunk_tok(FC), _chunk_tok(FC), _chunk_tok(FC), _full((SUB, D))],
        out_shape=[_sds((L, D)), _sds((NCHIP, L, FC), BF), _sds((NCHIP, L, FC), BF), _sds((NCHIP, L, FC), BF),
                   _sds((SUB, D))],
        scratch_shapes=[pltpu.VMEM((NCHIP, FC, D), BF)] * 3,
        compiler_params=_params(56),
    )(x1, dx2, gg, uu, g_ffn, wg, wu, wd)


def _ple_loss(x2, p, tgt, g_pg, w_pg, b_pg, w_ple, g_ple, g_final):
    L = x2.shape[0]

    def body(x2_ref, p_ref, t_ref, gpg_ref, wpg_ref, bpg_ref, wple_ref, gple_ref, gf_ref,
             dx2_ref, n2_ref, dpre_ref, de0_ref, acc_ref):
        @pl.when(pl.program_id(0) == 0)
        def _():
            acc_ref[...] = jnp.zeros_like(acc_ref)

        x2 = x2_ref[...]
        x2h, r2 = _rms(x2)
        n2 = (x2h * gpg_ref[...]).astype(BF)
        n2_ref[...] = n2
        gate = _sig(jnp.dot(n2, wpg_ref[...], preferred_element_type=F32) + bpg_ref[...])
        pb = p_ref[...].astype(BF)
        e0 = jnp.concatenate([jnp.dot(pb, wple_ref[k], preferred_element_type=F32) for k in range(NCHIP)], axis=1)
        e0h, re = _rms(e0)
        e = e0h * gple_ref[...]
        x3 = x2 + gate * e
        x3h, r3 = _rms(x3)
        diff = x3h * gf_ref[...] - t_ref[...]
        acc_ref[4:5, :] += _colsum(diff * diff) * (0.5 / D)
        dy = diff * (1.0 / D)
        acc_ref[3:4, :] += _colsum(dy * x3h)
        dx3 = _rms_bwd(dy, x3h, r3, gf_ref[...])
        de = dx3 * gate
        acc_ref[2:3, :] += _colsum(de * e0h)
        de0_ref[...] = _rms_bwd(de, e0h, re, gple_ref[...]).astype(BF)
        dpre = dx3 * e * gate * (1.0 - gate)
        acc_ref[1:2, :] += _colsum(dpre)
        dpreb = dpre.astype(BF)
        dpre_ref[...] = dpreb
        dn2 = lax.dot_general(dpreb, wpg_ref[...], (((1,), (1,)), ((), ())), preferred_element_type=F32)
        acc_ref[0:1, :] += _colsum(dn2 * x2h)
        dx2_ref[...] = dx3 + _rms_bwd(dn2, x2h, r2, gpg_ref[...])

    return _pallas_call(
        body, name="ple_loss", grid=(L // TM,),
        in_specs=[_tok(D), _tok(PLE), _tok(D), _full((1, D)), _full((D, D)), _full((1, D)), _full((NCHIP, PLE, AC)),
                  _full((1, D)), _full((1, D))],
        out_specs=[_tok(D), _tok(D), _tok(D), _tok(D), _full((SUB, D))],
        out_shape=[_sds((L, D)), _sds((L, D), BF), _sds((L, D), BF), _sds((L, D), BF), _sds((SUB, D))],
        compiler_params=_params(40),
    )(x2, p, tgt, g_pg, w_pg, b_pg, w_ple, g_ple, g_final)


def _tn(name, a, b, col_chunk=None):
    L = a.shape[-2]
    m, n = a.shape[-1], b.shape[-1]
    if a.ndim == 3 or b.ndim == 3:
        nj, bn = (a if a.ndim == 3 else b).shape[0], n
        a_spec = (pl.BlockSpec((None, TK, m), lambda j, t: (j, t, 0)) if a.ndim == 3
                  else pl.BlockSpec((TK, m), lambda j, t: (t, 0)))
        b_spec = (pl.BlockSpec((None, TK, n), lambda j, t: (j, t, 0)) if b.ndim == 3
                  else pl.BlockSpec((TK, n), lambda j, t: (t, 0)))
        out_spec, out_shape = pl.BlockSpec((None, m, n), lambda j, t: (j, 0, 0)), _sds((nj, m, n))
    else:
        bn = col_chunk
        if bn is None:
            bn = next((cand for cand in (1024, 512) if n > cand and n % cand == 0), n)
        nj = n // bn
        a_spec = pl.BlockSpec((TK, m), lambda j, t: (t, 0))
        b_spec = pl.BlockSpec((TK, bn), lambda j, t: (t, j))
        if col_chunk is None:
            out_spec, out_shape = pl.BlockSpec((m, bn), lambda j, t: (0, j)), _sds((m, n))
        else:
            out_spec, out_shape = pl.BlockSpec((None, m, bn), lambda j, t: (j, 0, 0)), _sds((nj, m, bn))

    def body(a_ref, b_ref, o_ref):
        @pl.when(pl.program_id(1) == 0)
        def _():
            o_ref[...] = jnp.zeros_like(o_ref)

        o_ref[...] += _mm_tn(a_ref[...], b_ref[...])

    return _pallas_call(
        body, name=name, grid=(nj, L // TK), in_specs=[a_spec, b_spec], out_specs=out_spec, out_shape=out_shape,
        compiler_params=pltpu.CompilerParams(dimension_semantics=("arbitrary", "arbitrary"),
                                             vmem_limit_bytes=40 * VMEM_MB),
    )(a, b)


LANE = 128


def _tn_blocks(name, a, b, ga, gb):
    L, n = a.shape[0], b.shape[1]
    per = LANE // ga
    wb = per * gb

    def body(a_ref, b_ref, o_ref, acc_ref):
        t = pl.program_id(1)

        @pl.when(t == 0)
        def _():
            acc_ref[...] = jnp.zeros_like(acc_ref)

        acc_ref[...] += _mm_tn(a_ref[...], b_ref[...])

        @pl.when(t == L // TK - 1)
        def _():
            rows = lax.broadcasted_iota(jnp.int32, (LANE, wb), 0) // ga
            cols = lax.broadcasted_iota(jnp.int32, (LANE, wb), 1) // gb
            kept = jnp.where(rows == cols, acc_ref[...], 0.0)
            o_ref[...] = jnp.sum(kept.reshape(per, ga, wb), axis=0)

    return _pallas_call(
        body, name=name, grid=(n // wb, L // TK),
        in_specs=[pl.BlockSpec((TK, LANE), lambda j, t: (t, j)), pl.BlockSpec((TK, wb), lambda j, t: (t, j))],
        out_specs=pl.BlockSpec((ga, wb), lambda j, t: (0, j)), out_shape=_sds((ga, n)),
        scratch_shapes=[pltpu.VMEM((LANE, wb), F32)],
        compiler_params=pltpu.CompilerParams(dimension_semantics=("arbitrary", "arbitrary"),
                                             vmem_limit_bytes=32 * VMEM_MB),
    )(a, b)


def _s5_discretize(lam_re, lam_im, log_dt, b_re, b_im):
    dt = jnp.exp(log_dt)[:, None]
    mag = jnp.exp(lam_re * dt)
    ar = mag * jnp.cos(lam_im * dt)
    ai = mag * jnp.sin(lam_im * dt)
    den = lam_re * lam_re + lam_im * lam_im
    nr = ar - 1.0
    fr = (nr * lam_re + ai * lam_im) / den
    fi = (ai * lam_re - nr * lam_im) / den
    bbr = fr[:, None, :] * b_re - fi[:, None, :] * b_im
    bbi = fr[:, None, :] * b_im + fi[:, None, :] * b_re
    return ar, ai, bbr, bbi


def _scan_constants(ar, ai):
    ar, ai = ar.reshape(1, GN), ai.reshape(1, GN)
    pw = [(jnp.ones_like(ar), jnp.zeros_like(ai))]
    for _ in range(SUB):
        pr, pi = pw[-1]
        pw.append((pr * ar - pi * ai, pr * ai + pi * ar))
    row = lax.broadcasted_iota(jnp.int32, (SUB, GN), 0)

    def build(reverse):
        sign = -1.0 if reverse else 1.0
        blocks = []
        for sh in (1, 2, 4):
            keep = (row < SUB - sh) if reverse else (row >= sh)
            blocks += [jnp.where(keep, pw[sh][0], 0.0), jnp.where(keep, sign * pw[sh][1], 0.0)]
        order = [SUB - i for i in range(SUB)] if reverse else [i + 1 for i in range(SUB)]
        blocks += [jnp.concatenate([pw[k][0] for k in order], 0), jnp.concatenate([sign * pw[k][1] for k in order], 0)]
        return jnp.concatenate(blocks, 0)

    return build(False), build(True)


def _blockdiag(blocks):
    g, r, c = blocks.shape
    eye = jnp.eye(g, dtype=blocks.dtype)
    return (blocks[:, :, None, :] * eye[:, None, :, None]).reshape(g * r, g * c)


def _local_step(x, p, tgt, w):
    rows_of = lambda a: a.reshape(NCHIP * a.shape[1], a.shape[2])
    w_glu, w_b_out, w_o, w_pg = (rows_of(w[n]) for n in ("w_glu", "w_b_out", "w_o", "w_ple_gate"))
    ar, ai, bbr, bbi = _s5_discretize(w["lam_re"], w["lam_im"], w["log_dt"], w["s5_b_re"], w["s5_b_im"])
    con, con_rev = _scan_constants(ar, ai)
    bbr_d = _blockdiag(bbr).astype(BF)
    bbi_d = _blockdiag(bbi).astype(BF)
    ccr_d = _blockdiag(jnp.swapaxes(w["s5_c_re"], 1, 2)).astype(BF)
    cci_d = _blockdiag(jnp.swapaxes(w["s5_c_im"], 1, 2)).astype(BF)
    dsk = w["s5_d"].reshape(1, S5W)
    wr_d = _blockdiag(w["w_r"]).astype(BF)
    wi_d = _blockdiag(w["w_i"]).astype(BF)
    lam = w["lru_lambda"].reshape(1, LW)
    sp = jax.nn.softplus(-lam)
    b_r, b_i = w["b_r"].reshape(1, LW), w["b_i"].reshape(1, LW)
    row = lambda name: w[name].reshape(1, -1)

    h, ua, ub, gp = _inproj_fwd(x, row("g_mix"), w["w_in"], row("b_in"))
    sr, si, y, zg, ya = _s5_fwd(ua, bbr_d, bbi_d, ccr_d, cci_d, dsk, con, w_glu, row("b_glu"))
    xc, rg, ig, yb, hp = _lru_fwd(ub, w["conv_w"], row("conv_b"), wr_d, wi_d, b_r, b_i, sp)
    x1, pa, pb, merged = _merge_fwd(x, ya, yb, gp, w["w_a_out"], w_b_out, w_o)
    x2, h2, gg, uu = _ffn_fwd(x1, row("g_ffn"), w["w_ffn_gate"], w["w_ffn_up"], w["w_ffn_down"])
    dx2, n2, dpre, de0, acc_p = _ple_loss(x2, p, tgt, row("g_ple_gate"), w_pg, row("b_ple_gate"),
                                          w["w_ple"], row("g_ple"), row("g_final"))
    dx1, act, dgg, duu, acc_f = _ffn_bwd(x1, dx2, gg, uu, row("g_ffn"), w["w_ffn_gate"], w["w_ffn_up"], w["w_ffn_down"])
    dya, dyb, dgp, dpa, dpb = _merge_bwd(dx1, gp, pa, pb, w["w_a_out"], w_b_out, w_o)
    dua, dq, dy, lr, li, acc_a, acc_s = _s5_bwd(dya, y, ua, sr, si, bbr_d, bbi_d, ccr_d, cci_d, dsk, con_rev,
                                                w_glu, row("b_glu"))
    dub, dpr, dpi, acc_l = _lru_bwd(dyb, xc, rg, ig, hp, ub, w["conv_w"], wr_d, wi_d, sp, -_sig(-lam))
    gx, dz, acc_g, acc_b = _inproj_bwd(x, dx1, dua, dub, dgp, row("g_mix"), w["w_in"])

    quarters = lambda a: a.reshape(NCHIP, a.shape[0] // NCHIP, a.shape[1])
    g = {
        "w_in": _tn("dw_in", h, dz, col_chunk=QC),
        "w_glu": quarters(_tn("dw_glu", zg, dq)),
        "w_a_out": _tn("dw_a_out", ya, dpa, col_chunk=AC),
        "w_b_out": quarters(_tn("dw_b_out", yb, dpb)),
        "w_o": quarters(_tn("dw_o", merged, dx1)),
        "w_ffn_gate": _tn("dw_ffn_gate", dgg, h2),
        "w_ffn_up": _tn("dw_ffn_up", duu, h2),
        "w_ffn_down": _tn("dw_ffn_down", act, dx2),
        "w_ple_gate": quarters(_tn("dw_ple_gate", n2, dpre)),
        "w_ple": _tn("dw_ple", p, de0, col_chunk=AC),
    }
    sums = {"ple": acc_p, "ffn": acc_f, "mix": acc_g, "b_in": acc_b, "lru": acc_l, "s5": acc_s, "s5_a": acc_a}
    blocks = {
        "bb_re": _tn_blocks("d_bbr", ua, lr, NP, NS),
        "bb_im": _tn_blocks("d_bbi", ua, li, NP, NS),
        "cc_re": _tn_blocks("d_ccr", dy, sr, NP, NS),
        "cc_im": _tn_blocks("d_cci", dy, si, NP, NS),
        "w_r": _tn_blocks("dw_r", xc, dpr, HD, HD),
        "w_i": _tn_blocks("dw_i", xc, dpi, HD, HD),
    }
    return gx, g, sums, blocks


def _replicated_grads(w, sums, blocks):
    grouped = lambda e, groups: jnp.transpose(e.reshape(e.shape[0], groups, -1), (1, 0, 2))
    d_ar, d_ai = sums["s5_a"][0].reshape(NG, NS), sums["s5_a"][1].reshape(NG, NS)
    d_bbr, d_bbi = grouped(blocks["bb_re"], NG), grouped(blocks["bb_im"], NG)
    _, vjp = jax.vjp(_s5_discretize, w["lam_re"], w["lam_im"], w["log_dt"], w["s5_b_re"], w["s5_b_im"])
    g = dict(zip(("lam_re", "lam_im", "log_dt", "s5_b_re", "s5_b_im"), vjp((d_ar, d_ai, d_bbr, d_bbi))))
    g["s5_c_re"] = grouped(blocks["cc_re"], NG)
    g["s5_c_im"] = -grouped(blocks["cc_im"], NG)
    g["w_r"], g["w_i"] = grouped(blocks["w_r"], NH), grouped(blocks["w_i"], NH)
    g["s5_d"] = sums["s5"][0].reshape(NG, NP)
    g["b_r"] = sums["lru"][1].reshape(NH, HD)
    g["b_i"] = sums["lru"][2].reshape(NH, HD)
    return g


ACC_ROWS = {"g_mix": ("mix", 0), "b_in": ("b_in", 0), "g_ffn": ("ffn", 0), "g_ple_gate": ("ple", 0),
            "b_ple_gate": ("ple", 1), "g_ple": ("ple", 2), "g_final": ("ple", 3), "b_glu": ("s5", 1),
            "lru_lambda": ("lru", 0), "conv_b": ("lru", 3)}
LOSS_ROW = ("ple", 4)
CONV_W_ROWS = ("lru", 4)


SHARDED = [("w_in", (D, QC)), ("w_glu", (S5W // NCHIP, S5W)), ("w_a_out", (S5W, AC)), ("w_b_out", (LW // NCHIP, D)),
           ("w_o", (D // NCHIP, D)), ("w_ffn_gate", (FC, D)), ("w_ffn_up", (FC, D)), ("w_ffn_down", (FC, D)),
           ("w_ple_gate", (D // NCHIP, D)), ("w_ple", (PLE, AC))]
NSH = len(SHARDED)
TRANSPOSED = ("w_ffn_gate", "w_ffn_up", "s5_b_re", "s5_b_im")
CONV_SHARD = (4, LW // NCHIP)


def _mesh_pos():
    return lax.axis_index("x"), lax.axis_index("y"), lax.axis_index("c")


def _other_chips(x, y):
    return [(1 - x, y), (x, 1 - y), (1 - x, 1 - y)]


def _half_rows(c, rows, align):
    return pl.ds(pl.multiple_of(c * (rows // 2), align), rows // 2)


def _gather_weights(shards, conv_w):
    def body(*refs):
        srcs, conv_src = refs[:NSH], refs[NSH]
        outs, conv_out = refs[NSH + 1:2 * NSH + 1], refs[2 * NSH + 1]
        send_sems, recv_sems = refs[2 * NSH + 2:]
        x, y, c = _mesh_pos()
        k0 = 2 * x + y
        sib = (x, y, 1 - c)
        chips = _other_chips(x, y)
        chip_ids = [2 * chip[0] + chip[1] for chip in chips]

        def remote(src, dst, j, i, to):
            return pltpu.make_async_remote_copy(src_ref=src, dst_ref=dst, send_sem=send_sems.at[j, i],
                                                recv_sem=recv_sems.at[j, i], device_id=to, device_id_type=MESH)

        own = [remote(s, o.at[k0], 6, i, sib) for i, (s, o) in enumerate(zip(srcs, outs))]
        own.append(remote(conv_src, conv_out.at[k0], 6, NSH, sib))
        sends = list(own)
        for j, chip in enumerate(chips):
            for i, (s, o) in enumerate(zip(srcs, outs)):
                mine = _half_rows(c, s.shape[0], 16)
                sends.append(remote(s.at[mine], o.at[k0, mine], j, i, (*chip, c)))
            sends.append(remote(conv_src, conv_out.at[k0], j, NSH, (*chip, c)))
        for cp in sends:
            cp.start()
        for j, chip in enumerate(chips):
            kj = chip_ids[j]
            remote(conv_src, conv_out.at[kj], j, NSH, (*chip, c)).wait_recv()
            for i, (s, o) in enumerate(zip(srcs, outs)):
                mine = _half_rows(c, s.shape[0], 16)
                remote(s.at[mine], o.at[kj, mine], j, i, (*chip, c)).wait_recv()
                fwd = remote(o.at[kj, mine], o.at[kj, mine], 3 + j, i, sib)
                fwd.start()
                sends.append(fwd)
        for j in range(3):
            kj = chip_ids[j]
            for i, (s, o) in enumerate(zip(srcs, outs)):
                other = _half_rows(1 - c, s.shape[0], 16)
                remote(o.at[kj, other], o.at[kj, other], 3 + j, i, sib).wait_recv()
        for cp in own:
            cp.wait_recv()
        for cp in sends:
            cp.wait_send()

    out_shape = [_sds((NCHIP,) + s.shape, BF) for s in shards] + [_sds((NCHIP,) + CONV_SHARD)]
    return _pallas_call(
        body, name="gather_weights", in_specs=[ANY] * (NSH + 1), out_specs=[ANY] * (NSH + 1), out_shape=out_shape,
        scratch_shapes=[pltpu.SemaphoreType.DMA((7, NSH + 1)), pltpu.SemaphoreType.DMA((7, NSH + 1))],
    )(*shards, conv_w)


def _swap_sibling_halves(grads):
    def body(*refs):
        srcs, outs, (send_sems, recv_sems) = refs[:NSH], refs[NSH:2 * NSH], refs[2 * NSH:]
        x, y, c = _mesh_pos()
        cps = [pltpu.make_async_remote_copy(src_ref=s.at[:, _half_rows(1 - c, s.shape[1], 8)], dst_ref=o,
                                            send_sem=send_sems.at[i], recv_sem=recv_sems.at[i], device_id=(x, y, 1 - c),
                                            device_id_type=MESH) for i, (s, o) in enumerate(zip(srcs, outs))]
        for cp in cps:
            cp.start()
        for cp in cps:
            cp.wait()

    return _pallas_call(
        body, name="swap_sibling_halves", in_specs=[ANY] * NSH, out_specs=[ANY] * NSH,
        out_shape=[_sds((NCHIP, g.shape[1] // 2, g.shape[2])) for g in grads],
        scratch_shapes=[pltpu.SemaphoreType.DMA((NSH,)), pltpu.SemaphoreType.DMA((NSH,))],
    )(*grads)


def _add_sibling(name, c_idx, g, got):
    hr, cols = got.shape[1:]

    def body(c_ref, g_ref, got_ref, p_ref, pb_ref):
        s = g_ref[...] + got_ref[...]
        p_ref[...] = s
        pb_ref[...] = s.astype(BF)

    spec = pl.BlockSpec((None, hr, cols), lambda k, c_ref: (k, 0, 0))
    return _pallas_call(
        body, name="add_sibling_" + name,
        grid_spec=pltpu.PrefetchScalarGridSpec(
            num_scalar_prefetch=1, grid=(NCHIP,),
            in_specs=[pl.BlockSpec((None, hr, cols), lambda k, c_ref: (k, c_ref[0], 0)), spec],
            out_specs=[spec, spec]),
        out_shape=[_sds((NCHIP, hr, cols)), _sds((NCHIP, hr, cols), BF)],
        compiler_params=_params(32),
    )(c_idx, g, got)


def _exchange_chips(parts):
    def body(*refs):
        srcs, outs, (send_sems, recv_sems) = refs[:NSH], refs[NSH:2 * NSH], refs[2 * NSH:]
        x, y, c = _mesh_pos()
        cps = []
        for j, chip in enumerate(_other_chips(x, y)):
            for i, (s, o) in enumerate(zip(srcs, outs)):
                cps.append(pltpu.make_async_remote_copy(
                    src_ref=s.at[2 * chip[0] + chip[1]], dst_ref=o.at[j], send_sem=send_sems.at[j, i],
                    recv_sem=recv_sems.at[j, i], device_id=(*chip, c), device_id_type=MESH))
        for cp in cps:
            cp.start()
        for cp in cps:
            cp.wait()

    return _pallas_call(
        body, name="exchange_chips", in_specs=[ANY] * NSH, out_specs=[ANY] * NSH,
        out_shape=[_sds((3,) + p.shape[1:], BF) for p in parts],
        scratch_shapes=[pltpu.SemaphoreType.DMA((3, NSH)), pltpu.SemaphoreType.DMA((3, NSH))],
    )(*parts)


def _add_chips(name, kc_idx, p, got):
    hr, cols = got.shape[1:]

    def body(kc_ref, p_ref, got_ref, t_ref):
        t_ref[...] = ((p_ref[...] + got_ref[0].astype(F32)) + got_ref[1].astype(F32)) + got_ref[2].astype(F32)

    return _pallas_call(
        body, name="add_chips_" + name,
        grid_spec=pltpu.PrefetchScalarGridSpec(
            num_scalar_prefetch=1, grid=(1,),
            in_specs=[pl.BlockSpec((None, hr, cols), lambda i, kc_ref: (kc_ref[0], 0, 0)),
                      pl.BlockSpec((3, hr, cols), lambda i, kc_ref: (0, 0, 0))],
            out_specs=pl.BlockSpec((None, hr, cols), lambda i, kc_ref: (kc_ref[1], 0, 0))),
        out_shape=_sds((2, hr, cols)),
        compiler_params=_params(32),
    )(kc_idx, p, got)


def _join_sibling(halves):
    def body(*refs):
        bufs, (send_sems, recv_sems) = refs[NSH:2 * NSH], refs[2 * NSH:]
        x, y, c = _mesh_pos()
        sib = (x, y, 1 - c)
        sends = [pltpu.make_async_remote_copy(src_ref=b.at[c], dst_ref=b.at[c], send_sem=send_sems.at[i],
                                              recv_sem=recv_sems.at[i], device_id=sib, device_id_type=MESH)
                 for i, b in enumerate(bufs)]
        for cp in sends:
            cp.start()
        for i, b in enumerate(bufs):
            pltpu.make_async_remote_copy(src_ref=b.at[c], dst_ref=b.at[1 - c], send_sem=send_sems.at[i],
                                         recv_sem=recv_sems.at[i], device_id=sib, device_id_type=MESH).wait_recv()
        for cp in sends:
            cp.wait_send()

    return _pallas_call(
        body, name="join_sibling", in_specs=[ANY] * NSH, out_specs=[ANY] * NSH,
        out_shape=[_sds(h.shape) for h in halves], input_output_aliases={i: i for i in range(NSH)},
        scratch_shapes=[pltpu.SemaphoreType.DMA((NSH,)), pltpu.SemaphoreType.DMA((NSH,))],
    )(*halves)


def _allreduce_small(arrays):
    n = len(arrays)

    def body(*refs):
        srcs, outs = refs[:n], refs[n:2 * n]
        sibs, chip_bufs = refs[2 * n:3 * n], refs[3 * n:4 * n]
        send_sems, recv_sems = refs[4 * n:]
        x, y, c = _mesh_pos()
        k0 = 2 * x + y

        def remote(src, dst, j, i, to):
            return pltpu.make_async_remote_copy(src_ref=src, dst_ref=dst, send_sem=send_sems.at[j, i],
                                                recv_sem=recv_sems.at[j, i], device_id=to, device_id_type=MESH)

        swaps = [remote(s, b, 0, i, (x, y, 1 - c)) for i, (s, b) in enumerate(zip(srcs, sibs))]
        for cp in swaps:
            cp.start()
        for cp in swaps:
            cp.wait()
        for s, b, buf in zip(srcs, sibs, chip_bufs):
            buf[k0] = s[...] + b[...]
        chips = _other_chips(x, y)
        sends = [remote(buf.at[k0], buf.at[k0], 1 + j, i, (*chip, c))
                 for j, chip in enumerate(chips) for i, buf in enumerate(chip_bufs)]
        for cp in sends:
            cp.start()
        for j, chip in enumerate(chips):
            for i, buf in enumerate(chip_bufs):
                remote(buf.at[k0], buf.at[2 * chip[0] + chip[1]], 1 + j, i, (*chip, c)).wait_recv()
        for cp in sends:
            cp.wait_send()
        for o, buf in zip(outs, chip_bufs):
            o[...] = ((buf[0] + buf[1]) + buf[2]) + buf[3]

    specs = [_full(a.shape) for a in arrays]
    return _pallas_call(
        body, name="allreduce_small", grid=(1,), in_specs=specs, out_specs=specs,
        out_shape=[_sds(a.shape) for a in arrays],
        scratch_shapes=([pltpu.VMEM(a.shape, F32) for a in arrays] + [pltpu.VMEM((NCHIP,) + a.shape, F32) for a in arrays]
                        + [pltpu.SemaphoreType.DMA((4, n)), pltpu.SemaphoreType.DMA((4, n))]),
        compiler_params=_params(32),
    )(*arrays)


def _adamw_terms(w, g, m, v):
    m = ADAM_B1 * m + (1.0 - ADAM_B1) * g
    v = ADAM_B2 * v + (1.0 - ADAM_B2) * jnp.square(g)
    m_hat = m / (1.0 - ADAM_B1 ** ADAM_STEP)
    v_hat = v / (1.0 - ADAM_B2 ** ADAM_STEP)
    return -ADAM_LR * (m_hat / (jnp.sqrt(v_hat) + ADAM_EPS) + ADAM_WD * w), m, v


def _adamw(name, w, g, m, v):
    r, c = w.shape
    rows = max(b for b in range(SUB, r + 1, SUB) if r % b == 0 and b * c * 4 <= 3 * VMEM_MB // 2)

    def body(w_ref, g_ref, m_ref, v_ref, d_ref, nm_ref, nv_ref):
        d_ref[...], nm_ref[...], nv_ref[...] = _adamw_terms(w_ref[...], g_ref[...], m_ref[...], v_ref[...])

    spec = pl.BlockSpec((rows, c), lambda i: (i, 0))
    return _pallas_call(
        body, name=name, grid=(r // rows,), in_specs=[spec] * 4, out_specs=[spec] * 3,
        out_shape=[_sds((r, c))] * 3, compiler_params=_params(40),
    )(w, g, m, v)


def _adamw_replicated(sums, row_of, direct):
    ns, nr, nd = len(sums), len(row_of), len(direct)

    def body(*refs):
        sum_refs = refs[:ns]
        ins = refs[ns:ns + 3 * nr + 4 * nd]
        outs = refs[ns + 3 * nr + 4 * nd:]
        for i, (_, _, _, si, row) in enumerate(row_of):
            w_ref, m_ref, v_ref = ins[3 * i:3 * i + 3]
            g = sum_refs[si][row:row + 1, :]
            outs[4 * i][...] = g
            outs[4 * i + 1][...], outs[4 * i + 2][...], outs[4 * i + 3][...] = _adamw_terms(w_ref[...], g, m_ref[...], v_ref[...])
        for i in range(nd):
            w_ref, m_ref, v_ref, g_ref = ins[3 * nr + 4 * i:3 * nr + 4 * i + 4]
            o = outs[4 * (nr + i):4 * (nr + i) + 4]
            g = g_ref[...]
            o[0][...] = g
            o[1][...], o[2][...], o[3][...] = _adamw_terms(w_ref[...], g, m_ref[...], v_ref[...])

    operands = list(sums)
    shapes = []
    for w, m, v, _, _ in row_of:
        operands += [w, m, v]
        shapes += [w.shape] * 4
    for w, m, v, g in direct:
        operands += [w, m, v, g]
        shapes += [w.shape] * 4
    flat = _pallas_call(
        body, name="adamw_replicated", grid=(1,), in_specs=[_full(a.shape) for a in operands],
        out_specs=[_full(s) for s in shapes], out_shape=[_sds(s) for s in shapes],
        compiler_params=_params(56),
    )(*operands)
    return [flat[4 * i:4 * i + 4] for i in range(nr + nd)]


INPUT_NAMES = (["x", "p"] + [n for n in
               ["g_mix", "w_in", "b_in", "lam_re", "lam_im", "log_dt", "s5_b_re", "s5_b_im", "s5_c_re", "s5_c_im", "s5_d",
                "w_glu", "b_glu", "conv_w", "conv_b", "w_r", "b_r", "w_i", "b_i", "lru_lambda", "w_a_out", "w_b_out", "w_o",
                "g_ffn", "w_ffn_gate", "w_ffn_up", "w_ffn_down", "g_ple_gate", "w_ple_gate", "b_ple_gate", "w_ple", "g_ple",
                "g_final"]])
WEIGHT_NAMES = INPUT_NAMES[2:]


def kernel(*args):
    names = INPUT_NAMES + ["loss_target"] + ["m_" + n for n in WEIGHT_NAMES] + ["v_" + n for n in WEIGHT_NAMES]
    assert len(args) == len(names)
    given = dict(zip(names, args))

    def view(name):
        a = given[name]
        return jnp.swapaxes(a, -1, -2) if name.endswith(TRANSPOSED) else a

    def unview(name, a):
        return jnp.swapaxes(a, -1, -2) if name in TRANSPOSED else a

    def local(name):
        return view(name) if name.endswith("g_final") else view(name)[0]

    xi, yi, ci = _mesh_pos()
    k0 = 2 * xi + yi
    x, p, tgt = given["x"][0], given["p"][0, 0], given["loss_target"][0]

    shard_names = [n for n, _ in SHARDED]
    got = _gather_weights([local(n).astype(BF) for n in shard_names], local("conv_w"))
    w = dict(zip(shard_names, got[:NSH]))
    w["conv_w"] = jnp.transpose(got[NSH], (1, 0, 2)).reshape(4, LW)
    replicated = [n for n in WEIGHT_NAMES if n not in w]
    for n in replicated:
        w[n] = local(n)

    gx, g, sums, blocks = _local_step(x, p, tgt, w)

    c_idx = jnp.reshape(ci, (1,)).astype(jnp.int32)
    k_idx = jnp.stack([k0, ci]).astype(jnp.int32)
    grads = [g[n] for n in shard_names]
    parts = [_add_sibling(n, c_idx, gr, rx) for n, gr, rx in zip(shard_names, grads, _swap_sibling_halves(grads))]
    arrived = _exchange_chips([pb for _, pb in parts])
    halves = [_add_chips(n, k_idx, pf, rx) for n, (pf, _), rx in zip(shard_names, parts, arrived)]
    results = {}
    for (n, shape), both in zip(SHARDED, _join_sibling(halves)):
        total = both.reshape(shape)
        delta, new_m, new_v = _adamw("adamw_" + n, local(n), total, local("m_" + n), local("v_" + n))
        for kind, arr in zip(("grad", "delta", "new_m", "new_v"), (total, delta, new_m, new_v)):
            results[kind, n] = unview(n, arr[None])

    sum_names, block_names = list(sums), list(blocks)
    red = _allreduce_small([sums[n] for n in sum_names] + [blocks[n] for n in block_names])
    sums = dict(zip(sum_names, red[:len(sum_names)]))
    blocks = dict(zip(block_names, red[len(sum_names):]))
    loss = jnp.sum(sums[LOSS_ROW[0]][LOSS_ROW[1]])
    direct_g = _replicated_grads(w, sums, blocks)
    conv_rows = sums[CONV_W_ROWS[0]][CONV_W_ROWS[1]:CONV_W_ROWS[1] + 4]
    direct_g["conv_w"] = lax.dynamic_slice(conv_rows, (0, k0 * CONV_SHARD[1]), CONV_SHARD)
    as_row = lambda a: a.reshape(1, -1)
    row_names = list(ACC_ROWS)
    row_of = [(as_row(given[n]), as_row(given["m_" + n]), as_row(given["v_" + n]),
               sum_names.index(ACC_ROWS[n][0]), ACC_ROWS[n][1]) for n in row_names]
    direct_names = list(direct_g)
    direct = [(view(n), view("m_" + n), view("v_" + n), direct_g[n].reshape(view(n).shape)) for n in direct_names]
    done = _adamw_replicated([sums[n] for n in sum_names], row_of, direct)
    for n, four in zip(row_names + direct_names, done):
        for kind, arr in zip(("grad", "delta", "new_m", "new_v"), four):
            results[kind, n] = unview(n, arr).reshape(given[n].shape)

    out = [loss, gx[None]]
    for kind in ("grad", "delta", "new_m", "new_v"):
        out += [results[kind, n] for n in WEIGHT_NAMES]
    return tuple(out)
```

```python
import functools
import math

import jax
import jax.numpy as jnp
from jax import lax
from jax.experimental import pallas as pl
from jax.experimental.pallas import tpu as pltpu

F32 = jnp.float32
BF = jnp.bfloat16

D = 1024
S5W = 512
NG, NS, NP = 32, 64, 16
GN = NG * NS
LW = 1024
NH, HD = 16, 64
LRU_C = 8.0
FH = 2816
NCHIP = 4
FC = FH // NCHIP
PLE = 256
INC = S5W + LW + 2 * D
EPS = 1e-6
ADAM_LR, ADAM_B1, ADAM_B2, ADAM_EPS, ADAM_WD, ADAM_STEP = 0.001, 0.9, 0.999, 1e-08, 0.01, 10

TM = 256
TK = 512
LC = 512
SUB = 8
VMEM_MB = 1024 * 1024
MESH = pl.DeviceIdType.MESH
ANY = pl.BlockSpec(memory_space=pl.ANY)


def _mm(a, b):
    return jnp.dot(a.astype(BF), b.astype(BF), preferred_element_type=F32)


def _mm_nt(a, b):
    return lax.dot_general(a.astype(BF), b.astype(BF), (((1,), (1,)), ((), ())), preferred_element_type=F32)


def _mm_tn(a, b):
    return lax.dot_general(a.astype(BF), b.astype(BF), (((0,), (0,)), ((), ())), preferred_element_type=F32)


def _rms(x):
    r = lax.rsqrt(jnp.mean(x * x, axis=-1, keepdims=True) + EPS)
    return x * r, r


def _rms_bwd(dy, xh, r, g):
    dxh = dy * g
    return r * (dxh - xh * jnp.mean(dxh * xh, axis=-1, keepdims=True))


def _colsum(x):
    return jnp.sum(x, axis=0, keepdims=True)


def _sig(x):
    return jax.nn.sigmoid(x)


def _gelu_grad(x):
    c = math.sqrt(2.0 / math.pi)
    t = jnp.tanh(c * (x + 0.044715 * x * x * x))
    return 0.5 * (1.0 + t) + 0.5 * x * (1.0 - t * t) * c * (1.0 + 3.0 * 0.044715 * x * x)


def _neg_expm1(x):
    series = -x * (1.0 + x * (0.5 + x * (1.0 / 6.0 + x * (1.0 / 24.0))))
    return jnp.where(x > -0.03, series, 1.0 - jnp.exp(x))


def _tok(width):
    return pl.BlockSpec((TM, width), lambda i: (i, 0))


def _tok_rev(width, nt):
    return pl.BlockSpec((TM, width), lambda i: (nt - 1 - i, 0))


def _full(shape):
    return pl.BlockSpec(shape, lambda i: (0,) * len(shape))


def _params(vmem_mb, **kw):
    return pltpu.CompilerParams(dimension_semantics=("arbitrary",), vmem_limit_bytes=vmem_mb * VMEM_MB, **kw)


def _sds(shape, dtype=F32):
    return jax.ShapeDtypeStruct(shape, dtype)


class _Carried:
    def __init__(self, operands, out_shapes, sems, start, finish):
        self.operands, self.out_shapes, self.sems, self.start, self.finish = operands, out_shapes, sems, start, finish


def _pallas_call(body, carry=None, **kw):
    if carry is None:
        return pl.pallas_call(body, **kw)
    name, grid, compiler_params = kw["name"], kw["grid"], kw["compiler_params"]
    in_specs, out_specs, out_shape = list(kw["in_specs"]), list(kw["out_specs"]), list(kw["out_shape"])
    scratch_shapes = list(kw.get("scratch_shapes", ()))
    n_in, n_out, n_scr = len(in_specs), len(out_specs), len(scratch_shapes)
    c_in, c_out = len(carry.operands), len(carry.out_shapes)

    def full_body(*refs):
        ins, refs = refs[:n_in], refs[n_in:]
        c_ins, refs = refs[:c_in], refs[c_in:]
        outs, refs = refs[:n_out], refs[n_out:]
        c_outs, refs = refs[:c_out], refs[c_out:]
        scratch, c_sems = refs[:n_scr], refs[n_scr:]

        @pl.when(pl.program_id(0) == 0)
        def _():
            carry.start(c_ins, c_outs, c_sems)

        body(*ins, *outs, *scratch)

        @pl.when(pl.program_id(0) == grid[0] - 1)
        def _():
            carry.finish(c_ins, c_outs, c_sems)

    call = pl.pallas_call(
        full_body, name=name, grid=grid, in_specs=in_specs + [ANY] * c_in, out_specs=out_specs + [ANY] * c_out,
        out_shape=out_shape + list(carry.out_shapes), scratch_shapes=scratch_shapes + list(carry.sems),
        compiler_params=compiler_params)
    return lambda *operands: call(*operands, *carry.operands)


def _row_iota(width):
    return lax.broadcasted_iota(jnp.int32, (SUB, width), 0)


def _bcast_row(x, row):
    return jnp.broadcast_to(x[row:row + 1, :], x.shape)


def _slab(k):
    return pl.ds(pl.multiple_of(k * SUB, SUB), SUB)


QC = INC // NCHIP
Z_PARTS = ((0, S5W), (S5W, S5W + LW), (S5W + LW, INC))


def _inproj_fwd(x, g_mix, w_in, b_in, carry=None):
    L = x.shape[0]

    def body(x_ref, g_ref, w_hbm, b_ref, h_ref, ua_ref, ub_ref, gp_ref, w_vm):
        @pl.when(pl.program_id(0) == 0)
        def _():
            pltpu.sync_copy(w_hbm, w_vm)

        xh, _ = _rms(x_ref[...])
        h = (xh * g_ref[...]).astype(BF)
        h_ref[...] = h
        for k in range(NCHIP):
            lo, hi = k * QC, (k + 1) * QC
            z = jnp.dot(h, w_vm[k], preferred_element_type=F32) + b_ref[:, lo:hi]
            for ref, (a, b) in zip((ua_ref, ub_ref, gp_ref), Z_PARTS):
                s, e = max(lo, a), min(hi, b)
                if s < e:
                    ref[:, s - a:e - a] = z[:, s - lo:e - lo]

    return _pallas_call(
        body, carry, name="inproj_fwd", grid=(L // TM,),
        in_specs=[_tok(D), _full((1, D)), ANY, _full((1, INC))],
        out_specs=[_tok(D), _tok(S5W), _tok(LW), _tok(2 * D)],
        out_shape=[_sds((L, D), BF), _sds((L, S5W)), _sds((L, LW)), _sds((L, 2 * D))],
        scratch_shapes=[pltpu.VMEM((NCHIP, D, QC), BF)],
        compiler_params=_params(40),
    )(x, g_mix, w_in, b_in)


def _inproj_bwd(x, dx1, dua, dub, dgp, g_mix, w_in):
    L = x.shape[0]

    def body(x_ref, dx1_ref, dua_ref, dub_ref, dgp_ref, g_ref, w_hbm, gx_ref, dz_ref, dg_ref, db_ref, w_vm):
        @pl.when(pl.program_id(0) == 0)
        def _():
            pltpu.sync_copy(w_hbm, w_vm)
            dg_ref[...] = jnp.zeros_like(dg_ref)
            db_ref[...] = jnp.zeros_like(db_ref)

        for src, (a, b) in zip((dua_ref, dub_ref, dgp_ref), Z_PARTS):
            d = src[...]
            dz_ref[:, a:b] = d.astype(BF)
            db_ref[0:1, a:b] += _colsum(d)
        dh = jnp.zeros((TM, D), F32)
        for k in range(NCHIP):
            dh = dh + lax.dot_general(dz_ref[:, k * QC:(k + 1) * QC], w_vm[k], (((1,), (1,)), ((), ())),
                                      preferred_element_type=F32)
        xh, r = _rms(x_ref[...])
        dg_ref[0:1, :] += _colsum(dh * xh)
        gx_ref[...] = dx1_ref[...] + _rms_bwd(dh, xh, r, g_ref[...])

    return _pallas_call(
        body, name="inproj_bwd", grid=(L // TM,),
        in_specs=[_tok(D), _tok(D), _tok(S5W), _tok(LW), _tok(2 * D), _full((1, D)), ANY],
        out_specs=[_tok(D), _tok(INC), _full((SUB, D)), _full((SUB, INC))],
        out_shape=[_sds((L, D)), _sds((L, INC), BF), _sds((SUB, D)), _sds((SUB, INC))],
        scratch_shapes=[pltpu.VMEM((NCHIP, D, QC), BF)],
        compiler_params=_params(40),
    )(x, dx1, dua, dub, dgp, g_mix, w_in)


def _cscan(xr_ref, xi_ref, con_ref, cr_ref, ci_ref, reverse):
    n_slab = xr_ref.shape[0] // SUB
    width = xr_ref.shape[1]
    for lc in range(width // LC):
        cols = slice(lc * LC, (lc + 1) * LC)
        con = [con_ref[SUB * j:SUB * (j + 1), cols] for j in range(8)]

        def step(k, carry, cols=cols, con=con):
            cr, ci = carry
            rows = _slab(n_slab - 1 - k if reverse else k)
            xr, xi = xr_ref[rows, cols], xi_ref[rows, cols]
            for j, sh in enumerate((1, 2, 4)):
                mr, mi = con[2 * j], con[2 * j + 1]
                pr = pltpu.roll(xr, SUB - sh if reverse else sh, 0)
                pi = pltpu.roll(xi, SUB - sh if reverse else sh, 0)
                xr, xi = xr + mr * pr - mi * pi, xi + mr * pi + mi * pr
            xr, xi = xr + con[6] * cr - con[7] * ci, xi + con[6] * ci + con[7] * cr
            xr_ref[rows, cols] = xr
            xi_ref[rows, cols] = xi
            row = 0 if reverse else SUB - 1
            return _bcast_row(xr, row), _bcast_row(xi, row)

        cr, ci = lax.fori_loop(0, n_slab, step, (cr_ref[:, cols], ci_ref[:, cols]))
        cr_ref[:, cols] = cr
        ci_ref[:, cols] = ci


def _s5_fwd(ua, bbr, bbi, ccr, cci, dsk, con, w_glu, b_glu, carry=None):
    L = ua.shape[0]

    def body(ua_ref, bbr_hbm, bbi_hbm, ccr_hbm, cci_hbm, dsk_ref, con_ref, wg_ref, bg_ref,
             sr_ref, si_ref, y_ref, zg_ref, ya_ref, bbr_vm, bbi_vm, ccr_vm, cci_vm, cr_ref, ci_ref):
        @pl.when(pl.program_id(0) == 0)
        def _():
            pltpu.sync_copy(bbr_hbm, bbr_vm)
            pltpu.sync_copy(bbi_hbm, bbi_vm)
            pltpu.sync_copy(ccr_hbm, ccr_vm)
            pltpu.sync_copy(cci_hbm, cci_vm)
            cr_ref[...] = jnp.zeros_like(cr_ref)
            ci_ref[...] = jnp.zeros_like(ci_ref)

        u = ua_ref[...]
        ub = u.astype(BF)
        sr_ref[...] = jnp.dot(ub, bbr_vm[...], preferred_element_type=F32)
        si_ref[...] = jnp.dot(ub, bbi_vm[...], preferred_element_type=F32)
        _cscan(sr_ref, si_ref, con_ref, cr_ref, ci_ref, reverse=False)
        y = _mm(sr_ref[...], ccr_vm[...]) - _mm(si_ref[...], cci_vm[...]) + dsk_ref[...] * u
        y_ref[...] = y
        zg = jax.nn.gelu(y)
        zg_ref[...] = zg.astype(BF)
        q = _mm(zg, wg_ref[...]) + bg_ref[...]
        ya_ref[...] = (zg * _sig(q)).astype(BF)

    return _pallas_call(
        body, carry, name="s5_fwd", grid=(L // TM,),
        in_specs=[_tok(S5W), ANY, ANY, ANY, ANY, _full((1, S5W)), _full((8 * SUB, GN)),
                  _full((S5W, S5W)), _full((1, S5W))],
        out_specs=[_tok(GN), _tok(GN), _tok(S5W), _tok(S5W), _tok(S5W)],
        out_shape=[_sds((L, GN)), _sds((L, GN)), _sds((L, S5W)), _sds((L, S5W), BF), _sds((L, S5W), BF)],
        scratch_shapes=[pltpu.VMEM((S5W, GN), BF), pltpu.VMEM((S5W, GN), BF), pltpu.VMEM((GN, S5W), BF),
                        pltpu.VMEM((GN, S5W), BF), pltpu.VMEM((SUB, GN), F32), pltpu.VMEM((SUB, GN), F32)],
        compiler_params=_params(44),
    )(ua, bbr, bbi, ccr, cci, dsk, con, w_glu, b_glu)


def _s5_bwd(dya, y, ua, sr, si, bbr, bbi, ccr, cci, dsk, con_rev, w_glu, b_glu, carry=None):
    L = ua.shape[0]
    nt = L // TM
    spt = TM // SUB
    n_slab = spt

    def halo_map(i):
        return (jnp.maximum((nt - 1 - i) * spt - 1, 0), 0)

    def body(dya_ref, y_ref, ua_ref, sr_ref, si_ref, hr_ref, hi_ref, bbr_hbm, bbi_hbm, ccr_hbm, cci_hbm,
             dsk_ref, con_ref, wg_ref, bg_ref,
             dua_ref, dq_ref, dy_ref, lr_ref, li_ref, da_ref, dsm_ref,
             bbr_vm, bbi_vm, ccr_vm, cci_vm, cr_ref, ci_ref):
        i = pl.program_id(0)

        @pl.when(i == 0)
        def _():
            pltpu.sync_copy(bbr_hbm, bbr_vm)
            pltpu.sync_copy(bbi_hbm, bbi_vm)
            pltpu.sync_copy(ccr_hbm, ccr_vm)
            pltpu.sync_copy(cci_hbm, cci_vm)
            cr_ref[...] = jnp.zeros_like(cr_ref)
            ci_ref[...] = jnp.zeros_like(ci_ref)
            da_ref[...] = jnp.zeros_like(da_ref)
            dsm_ref[...] = jnp.zeros_like(dsm_ref)

        u = ua_ref[...]
        yv = y_ref[...]
        dya = dya_ref[...]
        zg = jax.nn.gelu(yv)
        sg = _sig(_mm(zg, wg_ref[...]) + bg_ref[...])
        dq = dya * zg * sg * (1.0 - sg)
        dq_ref[...] = dq.astype(BF)
        dzg = dya * sg + _mm_nt(dq, wg_ref[...])
        dy = dzg * _gelu_grad(yv)
        dyb = dy.astype(BF)
        dy_ref[...] = dyb
        dsm_ref[0:1, :] += _colsum(dy * u)
        dsm_ref[1:2, :] += _colsum(dq)
        lr_ref[...] = lax.dot_general(dyb, ccr_vm[...], (((1,), (1,)), ((), ())), preferred_element_type=F32)
        li_ref[...] = -lax.dot_general(dyb, cci_vm[...], (((1,), (1,)), ((), ())), preferred_element_type=F32)
        _cscan(lr_ref, li_ref, con_ref, cr_ref, ci_ref, reverse=True)

        first_tile = (i == nt - 1)
        row = _row_iota(LC)
        for lc in range(GN // LC):
            cols = slice(lc * LC, (lc + 1) * LC)
            h_r = jnp.where(first_tile, 0.0, hr_ref[:, cols])
            h_i = jnp.where(first_tile, 0.0, hi_ref[:, cols])

            def step(k, acc, cols=cols, h_r=h_r, h_i=h_i):
                ar, ai = acc
                rows = _slab(k)
                prev = _slab(jnp.maximum(k - 1, 0))
                pr = jnp.where(k == 0, h_r, sr_ref[prev, cols])
                pi = jnp.where(k == 0, h_i, si_ref[prev, cols])
                spr = pltpu.roll(jnp.where(row == SUB - 1, pr, sr_ref[rows, cols]), 1, 0)
                spi = pltpu.roll(jnp.where(row == SUB - 1, pi, si_ref[rows, cols]), 1, 0)
                lr, li = lr_ref[rows, cols], li_ref[rows, cols]
                return ar + lr * spr + li * spi, ai + li * spr - lr * spi

            zero = jnp.zeros((SUB, LC), F32)
            ar, ai = lax.fori_loop(0, n_slab, step, (zero, zero))
            da_ref[0:1, cols] += _colsum(ar)
            da_ref[1:2, cols] += _colsum(ai)

        dua_ref[...] = (dy * dsk_ref[...] + _mm_nt(lr_ref[...], bbr_vm[...]) + _mm_nt(li_ref[...], bbi_vm[...]))

    return _pallas_call(
        body, carry, name="s5_bwd", grid=(nt,),
        in_specs=[_tok_rev(S5W, nt), _tok_rev(S5W, nt), _tok_rev(S5W, nt), _tok_rev(GN, nt), _tok_rev(GN, nt),
                  pl.BlockSpec((SUB, GN), halo_map), pl.BlockSpec((SUB, GN), halo_map),
                  ANY, ANY, ANY, ANY, _full((1, S5W)), _full((8 * SUB, GN)), _full((S5W, S5W)), _full((1, S5W))],
        out_specs=[_tok_rev(S5W, nt), _tok_rev(S5W, nt), _tok_rev(S5W, nt), _tok_rev(GN, nt), _tok_rev(GN, nt),
                   _full((SUB, GN)), _full((SUB, S5W))],
        out_shape=[_sds((L, S5W)), _sds((L, S5W), BF), _sds((L, S5W), BF), _sds((L, GN)), _sds((L, GN)),
                   _sds((SUB, GN)), _sds((SUB, S5W))],
        scratch_shapes=[pltpu.VMEM((S5W, GN), BF), pltpu.VMEM((S5W, GN), BF), pltpu.VMEM((GN, S5W), BF),
                        pltpu.VMEM((GN, S5W), BF), pltpu.VMEM((SUB, GN), F32), pltpu.VMEM((SUB, GN), F32)],
        compiler_params=_params(52),
    )(dya, y, ua, sr, si, sr, si, bbr, bbi, ccr, cci, dsk, con_rev, w_glu, b_glu)


def _lru_gate_terms(rg, sp):
    log_a = -LRU_C * rg * sp
    a = jnp.exp(log_a)
    mult = jnp.sqrt(_neg_expm1(2.0 * log_a))
    return a, mult


def _lru_fwd(ub, conv_w, conv_b, wr, wi, b_r, b_i, sp, carry=None):
    L = ub.shape[0]
    n_slab = TM // SUB

    def body(ub_ref, cw_ref, cb_ref, wr_ref, wi_ref, br_ref, bi_ref, sp_ref,
             xc_ref, rg_ref, ig_ref, h_ref, hp_ref, a_ref, halo_ref, carry_ref):
        @pl.when(pl.program_id(0) == 0)
        def _():
            halo_ref[...] = jnp.zeros_like(halo_ref)
            carry_ref[...] = jnp.zeros_like(carry_ref)

        row = _row_iota(LW)
        taps = [cw_ref[k:k + 1, :] for k in range(4)]
        cb = cb_ref[...]

        def conv_step(k, prev):
            rows = _slab(k)
            cur = ub_ref[rows, :]
            acc = taps[3] * cur + cb
            for j in (1, 2, 3):
                acc = acc + taps[3 - j] * pltpu.roll(jnp.where(row >= SUB - j, prev, cur), j, 0)
            xc_ref[rows, :] = acc
            return cur

        halo_ref[...] = lax.fori_loop(0, n_slab, conv_step, halo_ref[...])

        xc = xc_ref[...]
        xcb = xc.astype(BF)
        rg = _sig(jnp.dot(xcb, wr_ref[...], preferred_element_type=F32) + br_ref[...])
        ig = _sig(jnp.dot(xcb, wi_ref[...], preferred_element_type=F32) + bi_ref[...])
        rg_ref[...] = rg
        ig_ref[...] = ig
        a, mult = _lru_gate_terms(rg, sp_ref[...])
        a_ref[...] = a
        h_ref[...] = mult * ig * xc

        rowc = _row_iota(LC)
        for lc in range(LW // LC):
            cols = slice(lc * LC, (lc + 1) * LC)

            def step(k, c, cols=cols):
                rows = _slab(k)
                av, b = a_ref[rows, cols], h_ref[rows, cols]
                for sh in (1, 2, 4):
                    keep = rowc >= sh
                    b = b + av * jnp.where(keep, pltpu.roll(b, sh, 0), 0.0)
                    av = av * jnp.where(keep, pltpu.roll(av, sh, 0), 1.0)
                h = b + av * c
                h_ref[rows, cols] = h
                hp_ref[rows, cols] = jnp.where(rowc == 0, c, pltpu.roll(h, 1, 0))
                return _bcast_row(h, SUB - 1)

            carry_ref[:, cols] = lax.fori_loop(0, n_slab, step, carry_ref[:, cols])

    return _pallas_call(
        body, carry, name="lru_fwd", grid=(L // TM,),
        in_specs=[_tok(LW), _full((4, LW)), _full((1, LW)), _full((LW, LW)), _full((LW, LW)),
                  _full((1, LW)), _full((1, LW)), _full((1, LW))],
        out_specs=[_tok(LW)] * 5,
        out_shape=[_sds((L, LW))] * 5,
        scratch_shapes=[pltpu.VMEM((TM, LW), F32), pltpu.VMEM((SUB, LW), F32), pltpu.VMEM((SUB, LW), F32)],
        compiler_params=_params(40),
    )(ub, conv_w, conv_b, wr, wi, b_r, b_i, sp)


def _lru_bwd(dyb, xc, rg, ig, hp, ub, conv_w, wr, wi, sp, dsp, carry=None):
    L = ub.shape[0]
    nt = L // TM
    spt = TM // SUB
    n_slab = spt

    def halo_map(i):
        return (jnp.maximum((nt - 1 - i) * spt - 1, 0), 0)

    def body(dh_ref, xc_ref, rg_ref, ig_ref, hp_ref, ub_ref, uh_ref, cw_ref, wr_ref, wi_ref, sp_ref, dsp_ref,
             dub_ref, dpr_ref, dpi_ref, acc_ref, a_ref, lam_ref, dxc_ref, carry_ref, next_ref):
        i = pl.program_id(0)

        @pl.when(i == 0)
        def _():
            carry_ref[...] = jnp.zeros_like(carry_ref)
            next_ref[...] = jnp.zeros_like(next_ref)
            acc_ref[...] = jnp.zeros_like(acc_ref)

        sp = sp_ref[...]
        rg, ig, xc = rg_ref[...], ig_ref[...], xc_ref[...]
        a, mult = _lru_gate_terms(rg, sp)
        a_ref[...] = a

        rowc = _row_iota(LC)
        for lc in range(LW // LC):
            cols = slice(lc * LC, (lc + 1) * LC)

            def step(k, c, cols=cols):
                rows = _slab(n_slab - 1 - k)
                av, dh = a_ref[rows, cols], dh_ref[rows, cols]
                b = av * dh
                for sh in (1, 2, 4):
                    keep = rowc < SUB - sh
                    b = b + av * jnp.where(keep, pltpu.roll(b, SUB - sh, 0), 0.0)
                    av = av * jnp.where(keep, pltpu.roll(av, SUB - sh, 0), 1.0)
                mu = b + av * c
                lam_ref[rows, cols] = dh + jnp.where(rowc == SUB - 1, c, pltpu.roll(mu, SUB - 1, 0))
                return _bcast_row(mu, 0)

            carry_ref[:, cols] = lax.fori_loop(0, n_slab, step, carry_ref[:, cols])

        lam = lam_ref[...]
        d_a = lam * hp_ref[...]
        d_mult = lam * ig * xc
        d_ig = lam * mult * xc
        dxc = lam * mult * ig
        d_log_a = d_a * a - d_mult * a * a / mult
        d_rg = (-LRU_C) * sp * d_log_a
        acc_ref[0:1, :] += _colsum((-LRU_C) * rg * d_log_a) * dsp_ref[...]
        dpr = d_rg * rg * (1.0 - rg)
        dpi = d_ig * ig * (1.0 - ig)
        acc_ref[1:2, :] += _colsum(dpr)
        acc_ref[2:3, :] += _colsum(dpi)
        dprb, dpib = dpr.astype(BF), dpi.astype(BF)
        dpr_ref[...] = dprb
        dpi_ref[...] = dpib
        dxc = dxc + _mm_nt(dprb, wr_ref[...]) + _mm_nt(dpib, wi_ref[...])
        dxc_ref[...] = dxc
        acc_ref[3:4, :] += _colsum(dxc)

        row = _row_iota(LW)
        taps = [cw_ref[k:k + 1, :] for k in range(4)]
        u_halo = jnp.where(i == nt - 1, 0.0, uh_ref[...])
        nxt_tile = next_ref[...]

        def conv_step(k, accs):
            rows = _slab(k)
            cur = dxc_ref[rows, :]
            nxt = jnp.where(k == n_slab - 1, nxt_tile, dxc_ref[_slab(jnp.minimum(k + 1, n_slab - 1)), :])
            ucur = ub_ref[rows, :]
            uprev = jnp.where(k == 0, u_halo, ub_ref[_slab(jnp.maximum(k - 1, 0)), :])
            du = taps[3] * cur
            new = [accs[3] + cur * ucur]
            for j in (1, 2, 3):
                du = du + taps[3 - j] * pltpu.roll(jnp.where(row < j, nxt, cur), SUB - j, 0)
                new.append(accs[3 - j] + cur * pltpu.roll(jnp.where(row >= SUB - j, uprev, ucur), j, 0))
            dub_ref[rows, :] = du
            return tuple(new[::-1])

        zero = jnp.zeros((SUB, LW), F32)
        accs = lax.fori_loop(0, n_slab, conv_step, (zero, zero, zero, zero))
        for k in range(4):
            acc_ref[4 + k:5 + k, :] += _colsum(accs[k])
        next_ref[...] = dxc_ref[0:SUB, :]

    return _pallas_call(
        body, carry, name="lru_bwd", grid=(nt,),
        in_specs=[_tok_rev(LW, nt)] * 6 + [pl.BlockSpec((SUB, LW), halo_map), _full((4, LW)),
                                           _full((LW, LW)), _full((LW, LW)), _full((1, LW)), _full((1, LW))],
        out_specs=[_tok_rev(LW, nt), _tok_rev(LW, nt), _tok_rev(LW, nt), _full((SUB, LW))],
        out_shape=[_sds((L, LW)), _sds((L, LW), BF), _sds((L, LW), BF), _sds((SUB, LW))],
        scratch_shapes=[pltpu.VMEM((TM, LW), F32), pltpu.VMEM((TM, LW), F32), pltpu.VMEM((TM, LW), F32),
                        pltpu.VMEM((SUB, LW), F32), pltpu.VMEM((SUB, LW), F32)],
        compiler_params=_params(48),
    )(dyb, xc, rg, ig, hp, ub, ub, conv_w, wr, wi, sp, dsp)


AC = D // NCHIP


def _merge_fwd(x, ya, yb, gp, w_a, w_b, w_o, carry=None):
    L = x.shape[0]

    def body(x_ref, ya_ref, yb_ref, gp_ref, wa_ref, wb_ref, wo_ref, x1_ref, pa_ref, pb_ref, mg_ref):
        ya = ya_ref[...]
        for k in range(NCHIP):
            pa_ref[:, k * AC:(k + 1) * AC] = jnp.dot(ya, wa_ref[k], preferred_element_type=F32)
        pb = _mm(yb_ref[...], wb_ref[...])
        pb_ref[...] = pb
        gp = gp_ref[...]
        merged = (_sig(gp[:, :D]) * pa_ref[...] + _sig(gp[:, D:]) * pb).astype(BF)
        mg_ref[...] = merged
        x1_ref[...] = x_ref[...] + jnp.dot(merged, wo_ref[...], preferred_element_type=F32)

    return _pallas_call(
        body, carry, name="merge_fwd", grid=(L // TM,),
        in_specs=[_tok(D), _tok(S5W), _tok(LW), _tok(2 * D), _full((NCHIP, S5W, AC)), _full((LW, D)), _full((D, D))],
        out_specs=[_tok(D), _tok(D), _tok(D), _tok(D)],
        out_shape=[_sds((L, D)), _sds((L, D)), _sds((L, D)), _sds((L, D), BF)],
        compiler_params=_params(40),
    )(x, ya, yb, gp, w_a, w_b, w_o)


def _merge_bwd(dx1, gp, pa, pb, w_a, w_b, w_o, carry=None):
    L = dx1.shape[0]

    def body(dx1_ref, gp_ref, pa_ref, pb_ref, wa_ref, wb_ref, wo_ref, dya_ref, dyb_ref, dgp_ref, dpa_ref, dpb_ref):
        dm = _mm_nt(dx1_ref[...], wo_ref[...])
        gp = gp_ref[...]
        sa, sb = _sig(gp[:, :D]), _sig(gp[:, D:])
        dpa = (dm * sa).astype(BF)
        dpb = (dm * sb).astype(BF)
        dpa_ref[...] = dpa
        dpb_ref[...] = dpb
        dgp_ref[:, :D] = dm * pa_ref[...] * sa * (1.0 - sa)
        dgp_ref[:, D:] = dm * pb_ref[...] * sb * (1.0 - sb)
        dya = jnp.zeros((TM, S5W), F32)
        for k in range(NCHIP):
            dya = dya + _mm_nt(dpa[:, k * AC:(k + 1) * AC], wa_ref[k])
        dya_ref[...] = dya
        dyb_ref[...] = _mm_nt(dpb, wb_ref[...])

    return _pallas_call(
        body, carry, name="merge_bwd", grid=(L // TM,),
        in_specs=[_tok(D), _tok(2 * D), _tok(D), _tok(D), _full((NCHIP, S5W, AC)), _full((LW, D)), _full((D, D))],
        out_specs=[_tok(S5W), _tok(LW), _tok(2 * D), _tok(D), _tok(D)],
        out_shape=[_sds((L, S5W)), _sds((L, LW)), _sds((L, 2 * D)), _sds((L, D), BF), _sds((L, D), BF)],
        compiler_params=_params(40),
    )(dx1, gp, pa, pb, w_a, w_b, w_o)


def _chunk_tok(width):
    return pl.BlockSpec((NCHIP, TM, width), lambda i: (0, i, 0))


def _ffn_fwd(x1, g_ffn, wg, wu, wd, carry=None):
    L = x1.shape[0]

    def body(x_ref, g_ref, wg_hbm, wu_hbm, wd_hbm, x2_ref, h2_ref, gg_ref, uu_ref, wg_vm, wu_vm, wd_vm):
        @pl.when(pl.program_id(0) == 0)
        def _():
            pltpu.sync_copy(wg_hbm, wg_vm)
            pltpu.sync_copy(wu_hbm, wu_vm)
            pltpu.sync_copy(wd_hbm, wd_vm)

        x = x_ref[...]
        xh, _ = _rms(x)
        h2 = (xh * g_ref[...]).astype(BF)
        h2_ref[...] = h2
        out = x
        for c in range(NCHIP):
            gg = lax.dot_general(h2, wg_vm[c], (((1,), (1,)), ((), ())), preferred_element_type=F32)
            uu = lax.dot_general(h2, wu_vm[c], (((1,), (1,)), ((), ())), preferred_element_type=F32)
            gg_ref[c] = gg.astype(BF)
            uu_ref[c] = uu.astype(BF)
            act = (gg * _sig(gg) * uu).astype(BF)
            out = out + jnp.dot(act, wd_vm[c], preferred_element_type=F32)
        x2_ref[...] = out

    return _pallas_call(
        body, carry, name="ffn_fwd", grid=(L // TM,),
        in_specs=[_tok(D), _full((1, D)), ANY, ANY, ANY],
        out_specs=[_tok(D), _tok(D), _chunk_tok(FC), _chunk_tok(FC)],
        out_shape=[_sds((L, D)), _sds((L, D), BF), _sds((NCHIP, L, FC), BF), _sds((NCHIP, L, FC), BF)],
        scratch_shapes=[pltpu.VMEM((NCHIP, FC, D), BF)] * 3,
        compiler_params=_params(52),
    )(x1, g_ffn, wg, wu, wd)


def _ffn_bwd(x1, dx2, gg, uu, g_ffn, wg, wu, wd, carry=None):
    L = x1.shape[0]

    def body(x_ref, dx2_ref, gg_ref, uu_ref, g_ref, wg_hbm, wu_hbm, wd_hbm,
             dx1_ref, act_ref, dgg_ref, duu_ref, dg_ref, wg_vm, wu_vm, wd_vm):
        @pl.when(pl.program_id(0) == 0)
        def _():
            pltpu.sync_copy(wg_hbm, wg_vm)
            pltpu.sync_copy(wu_hbm, wu_vm)
            pltpu.sync_copy(wd_hbm, wd_vm)
            dg_ref[...] = jnp.zeros_like(dg_ref)

        dx2 = dx2_ref[...]
        dx2b = dx2.astype(BF)
        dh2 = jnp.zeros((TM, D), F32)
        for c in range(NCHIP):
            g = gg_ref[c].astype(F32)
            u = uu_ref[c].astype(F32)
            s = _sig(g)
            silu = g * s
            act_ref[c] = (silu * u).astype(BF)
            dact = lax.dot_general(dx2b, wd_vm[c], (((1,), (1,)), ((), ())), preferred_element_type=F32)
            dg = (dact * u * s * (1.0 + g * (1.0 - s))).astype(BF)
            du = (dact * silu).astype(BF)
            dgg_ref[c] = dg
            duu_ref[c] = du
            dh2 = dh2 + jnp.dot(dg, wg_vm[c], preferred_element_type=F32)
            dh2 = dh2 + jnp.dot(du, wu_vm[c], preferred_element_type=F32)
        xh, r = _rms(x_ref[...])
        dg_ref[0:1, :] += _colsum(dh2 * xh)
        dx1_ref[...] = dx2 + _rms_bwd(dh2, xh, r, g_ref[...])

    return _pallas_call(
        body, carry, name="ffn_bwd", grid=(L // TM,),
        in_specs=[_tok(D), _tok(D), _chunk_tok(FC), _chunk_tok(FC), _full((1, D)), ANY, ANY, ANY],
        out_specs=[_tok(D), _chunk_tok(FC), _chunk_tok(FC), _chunk_tok(FC), _full((SUB, D))],
        out_shape=[_sds((L, D)), _sds((NCHIP, L, FC), BF), _sds((NCHIP, L, FC), BF), _sds((NCHIP, L, FC), BF),
                   _sds((SUB, D))],
        scratch_shapes=[pltpu.VMEM((NCHIP, FC, D), BF)] * 3,
        compiler_params=_params(56),
    )(x1, dx2, gg, uu, g_ffn, wg, wu, wd)


def _ple_loss(x2, p, tgt, g_pg, w_pg, b_pg, w_ple, g_ple, g_final):
    L = x2.shape[0]

    def body(x2_ref, p_ref, t_ref, gpg_ref, wpg_ref, bpg_ref, wple_ref, gple_ref, gf_ref,
             dx2_ref, n2_ref, dpre_ref, de0_ref, acc_ref):
        @pl.when(pl.program_id(0) == 0)
        def _():
            acc_ref[...] = jnp.zeros_like(acc_ref)

        x2 = x2_ref[...]
        x2h, r2 = _rms(x2)
        n2 = (x2h * gpg_ref[...]).astype(BF)
        n2_ref[...] = n2
        gate = _sig(jnp.dot(n2, wpg_ref[...], preferred_element_type=F32) + bpg_ref[...])
        pb = p_ref[...].astype(BF)
        e0 = jnp.concatenate([jnp.dot(pb, wple_ref[k], preferred_element_type=F32) for k in range(NCHIP)], axis=1)
        e0h, re = _rms(e0)
        e = e0h * gple_ref[...]
        x3 = x2 + gate * e
        x3h, r3 = _rms(x3)
        diff = x3h * gf_ref[...] - t_ref[...]
        acc_ref[4:5, :] += _colsum(diff * diff) * (0.5 / D)
        dy = diff * (1.0 / D)
        acc_ref[3:4, :] += _colsum(dy * x3h)
        dx3 = _rms_bwd(dy, x3h, r3, gf_ref[...])
        de = dx3 * gate
        acc_ref[2:3, :] += _colsum(de * e0h)
        de0_ref[...] = _rms_bwd(de, e0h, re, gple_ref[...]).astype(BF)
        dpre = dx3 * e * gate * (1.0 - gate)
        acc_ref[1:2, :] += _colsum(dpre)
        dpreb = dpre.astype(BF)
        dpre_ref[...] = dpreb
        dn2 = lax.dot_general(dpreb, wpg_ref[...], (((1,), (1,)), ((), ())), preferred_element_type=F32)
        acc_ref[0:1, :] += _colsum(dn2 * x2h)
        dx2_ref[...] = dx3 + _rms_bwd(dn2, x2h, r2, gpg_ref[...])

    return _pallas_call(
        body, name="ple_loss", grid=(L // TM,),
        in_specs=[_tok(D), _tok(PLE), _tok(D), _full((1, D)), _full((D, D)), _full((1, D)), _full((NCHIP, PLE, AC)),
                  _full((1, D)), _full((1, D))],
        out_specs=[_tok(D), _tok(D), _tok(D), _tok(D), _full((SUB, D))],
        out_shape=[_sds((L, D)), _sds((L, D), BF), _sds((L, D), BF), _sds((L, D), BF), _sds((SUB, D))],
        compiler_params=_params(40),
    )(x2, p, tgt, g_pg, w_pg, b_pg, w_ple, g_ple, g_final)


def _tn(name, a, b, col_chunk=None):
    L = a.shape[-2]
    m, n = a.shape[-1], b.shape[-1]
    if a.ndim == 3 or b.ndim == 3:
        nj, bn = (a if a.ndim == 3 else b).shape[0], n
        a_spec = (pl.BlockSpec((None, TK, m), lambda j, t: (j, t, 0)) if a.ndim == 3
                  else pl.BlockSpec((TK, m), lambda j, t: (t, 0)))
        b_spec = (pl.BlockSpec((None, TK, n), lambda j, t: (j, t, 0)) if b.ndim == 3
                  else pl.BlockSpec((TK, n), lambda j, t: (t, 0)))
        out_spec, out_shape = pl.BlockSpec((None, m, n), lambda j, t: (j, 0, 0)), _sds((nj, m, n))
    else:
        bn = col_chunk
        if bn is None:
            bn = next((cand for cand in (1024, 512) if n > cand and n % cand == 0), n)
        nj = n // bn
        a_spec = pl.BlockSpec((TK, m), lambda j, t: (t, 0))
        b_spec = pl.BlockSpec((TK, bn), lambda j, t: (t, j))
        if col_chunk is None:
            out_spec, out_shape = pl.BlockSpec((m, bn), lambda j, t: (0, j)), _sds((m, n))
        else:
            out_spec, out_shape = pl.BlockSpec((None, m, bn), lambda j, t: (j, 0, 0)), _sds((nj, m, bn))

    def body(a_ref, b_ref, o_ref):
        @pl.when(pl.program_id(1) == 0)
        def _():
            o_ref[...] = jnp.zeros_like(o_ref)

        o_ref[...] += _mm_tn(a_ref[...], b_ref[...])

    return _pallas_call(
        body, name=name, grid=(nj, L // TK), in_specs=[a_spec, b_spec], out_specs=out_spec, out_shape=out_shape,
        compiler_params=pltpu.CompilerParams(dimension_semantics=("arbitrary", "arbitrary"),
                                             vmem_limit_bytes=40 * VMEM_MB),
    )(a, b)


LANE = 128


def _tn_blocks(name, a, b, ga, gb):
    L, n = a.shape[0], b.shape[1]
    per = LANE // ga
    wb = per * gb

    def body(a_ref, b_ref, o_ref, acc_ref):
        t = pl.program_id(1)

        @pl.when(t == 0)
        def _():
            acc_ref[...] = jnp.zeros_like(acc_ref)

        acc_ref[...] += _mm_tn(a_ref[...], b_ref[...])

        @pl.when(t == L // TK - 1)
        def _():
            rows = lax.broadcasted_iota(jnp.int32, (LANE, wb), 0) // ga
            cols = lax.broadcasted_iota(jnp.int32, (LANE, wb), 1) // gb
            kept = jnp.where(rows == cols, acc_ref[...], 0.0)
            o_ref[...] = jnp.sum(kept.reshape(per, ga, wb), axis=0)

    return _pallas_call(
        body, name=name, grid=(n // wb, L // TK),
        in_specs=[pl.BlockSpec((TK, LANE), lambda j, t: (t, j)), pl.BlockSpec((TK, wb), lambda j, t: (t, j))],
        out_specs=pl.BlockSpec((ga, wb), lambda j, t: (0, j)), out_shape=_sds((ga, n)),
        scratch_shapes=[pltpu.VMEM((LANE, wb), F32)],
        compiler_params=pltpu.CompilerParams(dimension_semantics=("arbitrary", "arbitrary"),
                                             vmem_limit_bytes=32 * VMEM_MB),
    )(a, b)


def _s5_discretize(lam_re, lam_im, log_dt, b_re, b_im):
    dt = jnp.exp(log_dt)[:, None]
    mag = jnp.exp(lam_re * dt)
    ar = mag * jnp.cos(lam_im * dt)
    ai = mag * jnp.sin(lam_im * dt)
    den = lam_re * lam_re + lam_im * lam_im
    nr = ar - 1.0
    fr = (nr * lam_re + ai * lam_im) / den
    fi = (ai * lam_re - nr * lam_im) / den
    bbr = fr[:, None, :] * b_re - fi[:, None, :] * b_im
    bbi = fr[:, None, :] * b_im + fi[:, None, :] * b_re
    return ar, ai, bbr, bbi


def _scan_constants(ar, ai):
    ar, ai = ar.reshape(1, GN), ai.reshape(1, GN)
    pw = [(jnp.ones_like(ar), jnp.zeros_like(ai))]
    for _ in range(SUB):
        pr, pi = pw[-1]
        pw.append((pr * ar - pi * ai, pr * ai + pi * ar))
    row = lax.broadcasted_iota(jnp.int32, (SUB, GN), 0)

    def build(reverse):
        sign = -1.0 if reverse else 1.0
        blocks = []
        for sh in (1, 2, 4):
            keep = (row < SUB - sh) if reverse else (row >= sh)
            blocks += [jnp.where(keep, pw[sh][0], 0.0), jnp.where(keep, sign * pw[sh][1], 0.0)]
        order = [SUB - i for i in range(SUB)] if reverse else [i + 1 for i in range(SUB)]
        blocks += [jnp.concatenate([pw[k][0] for k in order], 0), jnp.concatenate([sign * pw[k][1] for k in order], 0)]
        return jnp.concatenate(blocks, 0)

    return build(False), build(True)


def _blockdiag(blocks):
    g, r, c = blocks.shape
    eye = jnp.eye(g, dtype=blocks.dtype)
    return (blocks[:, :, None, :] * eye[:, None, :, None]).reshape(g * r, g * c)


def _local_step(x, p, tgt, w, comm):
    rows_of = lambda a: a.reshape(NCHIP * a.shape[1], a.shape[2])
    quarters = lambda a: a.reshape(NCHIP, a.shape[0] // NCHIP, a.shape[1])

    def gathering(names, call):
        carry = comm.gather(names)
        outs = list(call(carry))
        own = len(outs) - len(carry.out_shapes)
        w.update(zip(names, outs[own:]))
        return outs[:own]

    def reducing(tag, grads, call):
        parts, carry = comm.reduce_begin(tag, grads)
        outs = list(call(carry))
        own = len(outs) - len(carry.out_shapes)
        comm.reduce_end(tag, grads, parts, outs[own:])
        return outs[:own]

    w.update(comm.first())
    w_glu = rows_of(w["w_glu"])
    ar, ai, bbr, bbi = _s5_discretize(w["lam_re"], w["lam_im"], w["log_dt"], w["s5_b_re"], w["s5_b_im"])
    con, con_rev = _scan_constants(ar, ai)
    bbr_d = _blockdiag(bbr).astype(BF)
    bbi_d = _blockdiag(bbi).astype(BF)
    ccr_d = _blockdiag(jnp.swapaxes(w["s5_c_re"], 1, 2)).astype(BF)
    cci_d = _blockdiag(jnp.swapaxes(w["s5_c_im"], 1, 2)).astype(BF)
    dsk = w["s5_d"].reshape(1, S5W)
    wr_d = _blockdiag(w["w_r"]).astype(BF)
    wi_d = _blockdiag(w["w_i"]).astype(BF)
    lam = w["lru_lambda"].reshape(1, LW)
    sp = jax.nn.softplus(-lam)
    b_r, b_i = w["b_r"].reshape(1, LW), w["b_i"].reshape(1, LW)
    row = lambda name: w[name].reshape(1, -1)

    h, ua, ub, gp = gathering(["w_a_out", "w_b_out", "w_o"], lambda carry: _inproj_fwd(
        x, row("g_mix"), w["w_in"], row("b_in"), carry))
    sr, si, y, zg, ya = gathering(["w_ffn_gate"], lambda carry: _s5_fwd(
        ua, bbr_d, bbi_d, ccr_d, cci_d, dsk, con, w_glu, row("b_glu"), carry))
    xc, rg, ig, yb, hp = gathering(["w_ffn_up"], lambda carry: _lru_fwd(
        ub, w["conv_w"], row("conv_b"), wr_d, wi_d, b_r, b_i, sp, carry))
    w_b_out, w_o = rows_of(w["w_b_out"]), rows_of(w["w_o"])
    x1, pa, pb, merged = gathering(["w_ffn_down"], lambda carry: _merge_fwd(
        x, ya, yb, gp, w["w_a_out"], w_b_out, w_o, carry))
    x2, h2, gg, uu = gathering(["w_ple_gate", "w_ple"], lambda carry: _ffn_fwd(
        x1, row("g_ffn"), w["w_ffn_gate"], w["w_ffn_up"], w["w_ffn_down"], carry))
    w_pg = rows_of(w["w_ple_gate"])
    dx2, n2, dpre, de0, acc_p = _ple_loss(x2, p, tgt, row("g_ple_gate"), w_pg, row("b_ple_gate"),
                                          w["w_ple"], row("g_ple"), row("g_final"))
    grads = {"w_ple_gate": quarters(_tn("dw_ple_gate", n2, dpre)), "w_ple": _tn("dw_ple", p, de0, col_chunk=AC)}
    dx1, act, dgg, duu, acc_f = reducing("ple", grads, lambda carry: _ffn_bwd(
        x1, dx2, gg, uu, row("g_ffn"), w["w_ffn_gate"], w["w_ffn_up"], w["w_ffn_down"], carry))
    grads = {"w_ffn_gate": _tn("dw_ffn_gate", dgg, h2)}
    dya, dyb, dgp, dpa, dpb = reducing("ffn_gate", grads, lambda carry: _merge_bwd(
        dx1, gp, pa, pb, w["w_a_out"], w_b_out, w_o, carry))
    grads = {"w_ffn_up": _tn("dw_ffn_up", duu, h2), "w_ffn_down": _tn("dw_ffn_down", act, dx2)}
    dua, dq, dy, lr, li, acc_a, acc_s = reducing("ffn_rest", grads, lambda carry: _s5_bwd(
        dya, y, ua, sr, si, bbr_d, bbi_d, ccr_d, cci_d, dsk, con_rev, w_glu, row("b_glu"), carry))
    grads = {"w_o": quarters(_tn("dw_o", merged, dx1)), "w_a_out": _tn("dw_a_out", ya, dpa, col_chunk=AC),
             "w_b_out": quarters(_tn("dw_b_out", yb, dpb))}
    dub, dpr, dpi, acc_l = reducing("merge", grads, lambda carry: _lru_bwd(
        dyb, xc, rg, ig, hp, ub, w["conv_w"], wr_d, wi_d, sp, -_sig(-lam), carry))
    gx, dz, acc_g, acc_b = _inproj_bwd(x, dx1, dua, dub, dgp, row("g_mix"), w["w_in"])
    grads = {"w_in": _tn("dw_in", h, dz, col_chunk=QC), "w_glu": quarters(_tn("dw_glu", zg, dq))}
    reducing("in", grads, lambda carry: _run_now("exchange_in", carry))
    sums = {"ple": acc_p, "ffn": acc_f, "mix": acc_g, "b_in": acc_b, "lru": acc_l, "s5": acc_s, "s5_a": acc_a}
    blocks = {
        "bb_re": _tn_blocks("d_bbr", ua, lr, NP, NS),
        "bb_im": _tn_blocks("d_bbi", ua, li, NP, NS),
        "cc_re": _tn_blocks("d_ccr", dy, sr, NP, NS),
        "cc_im": _tn_blocks("d_cci", dy, si, NP, NS),
        "w_r": _tn_blocks("dw_r", xc, dpr, HD, HD),
        "w_i": _tn_blocks("dw_i", xc, dpi, HD, HD),
    }
    return gx, sums, blocks


def _replicated_grads(w, sums, blocks):
    grouped = lambda e, groups: jnp.transpose(e.reshape(e.shape[0], groups, -1), (1, 0, 2))
    d_ar, d_ai = sums["s5_a"][0].reshape(NG, NS), sums["s5_a"][1].reshape(NG, NS)
    d_bbr, d_bbi = grouped(blocks["bb_re"], NG), grouped(blocks["bb_im"], NG)
    _, vjp = jax.vjp(_s5_discretize, w["lam_re"], w["lam_im"], w["log_dt"], w["s5_b_re"], w["s5_b_im"])
    g = dict(zip(("lam_re", "lam_im", "log_dt", "s5_b_re", "s5_b_im"), vjp((d_ar, d_ai, d_bbr, d_bbi))))
    g["s5_c_re"] = grouped(blocks["cc_re"], NG)
    g["s5_c_im"] = -grouped(blocks["cc_im"], NG)
    g["w_r"], g["w_i"] = grouped(blocks["w_r"], NH), grouped(blocks["w_i"], NH)
    g["s5_d"] = sums["s5"][0].reshape(NG, NP)
    g["b_r"] = sums["lru"][1].reshape(NH, HD)
    g["b_i"] = sums["lru"][2].reshape(NH, HD)
    return g


ACC_ROWS = {"g_mix": ("mix", 0), "b_in": ("b_in", 0), "g_ffn": ("ffn", 0), "g_ple_gate": ("ple", 0),
            "b_ple_gate": ("ple", 1), "g_ple": ("ple", 2), "g_final": ("ple", 3), "b_glu": ("s5", 1),
            "lru_lambda": ("lru", 0), "conv_b": ("lru", 3)}
LOSS_ROW = ("ple", 4)
CONV_W_ROWS = ("lru", 4)


SHARDED = [("w_in", (D, QC)), ("w_glu", (S5W // NCHIP, S5W)), ("w_a_out", (S5W, AC)), ("w_b_out", (LW // NCHIP, D)),
           ("w_o", (D // NCHIP, D)), ("w_ffn_gate", (FC, D)), ("w_ffn_up", (FC, D)), ("w_ffn_down", (FC, D)),
           ("w_ple_gate", (D // NCHIP, D)), ("w_ple", (PLE, AC))]
NSH = len(SHARDED)
TRANSPOSED = ("w_ffn_gate", "w_ffn_up", "s5_b_re", "s5_b_im")
CONV_SHARD = (4, LW // NCHIP)


def _mesh_pos():
    return lax.axis_index("x"), lax.axis_index("y"), lax.axis_index("c")


def _other_chips(x, y):
    return [(1 - x, y), (x, 1 - y), (1 - x, 1 - y)]


def _half_rows(c, rows, align):
    return pl.ds(pl.multiple_of(c * (rows // 2), align), rows // 2)


def _run_now(name, carry):
    c_in, c_out = len(carry.operands), len(carry.out_shapes)

    def body(*refs):
        ins, outs, sems = refs[:c_in], refs[c_in:c_in + c_out], refs[c_in + c_out:]
        carry.start(ins, outs, sems)
        carry.finish(ins, outs, sems)

    return pl.pallas_call(body, name=name, in_specs=[ANY] * c_in, out_specs=[ANY] * c_out,
                          out_shape=list(carry.out_shapes), scratch_shapes=list(carry.sems))(*carry.operands)


def _gather_group(shards, split):
    n = len(shards)

    def copies(srcs, outs, sems):
        send_sems, recv_sems = sems
        x, y, c = _mesh_pos()
        k0 = 2 * x + y
        sib = (x, y, 1 - c)
        chips = _other_chips(x, y)

        def remote(src, dst, j, i, to):
            return pltpu.make_async_remote_copy(src_ref=src, dst_ref=dst, send_sem=send_sems.at[j, i],
                                                recv_sem=recv_sems.at[j, i], device_id=to, device_id_type=MESH)

        def rows(ref, i, core, *lead):
            if not split[i]:
                return ref.at[lead] if lead else ref
            return ref.at[(*lead, _half_rows(core, shards[i].shape[0], 16))]

        own = [remote(s, o.at[k0], 6, i, sib) for i, (s, o) in enumerate(zip(srcs, outs))]
        ici, landed, fwd, fwd_landed = [], [], [], []
        for j, chip in enumerate(chips):
            kj = 2 * chip[0] + chip[1]
            pairs = list(enumerate(zip(srcs, outs)))
            ici.append([remote(rows(s, i, c), rows(o, i, c, k0), j, i, (*chip, c)) for i, (s, o) in pairs])
            landed.append([remote(rows(s, i, c), rows(o, i, c, kj), j, i, (*chip, c)) for i, (s, o) in pairs])
            fwd.append([remote(rows(o, i, c, kj), rows(o, i, c, kj), 3 + j, i, sib) for i, (s, o) in pairs if split[i]])
            fwd_landed.append([remote(rows(o, i, 1 - c, kj), rows(o, i, 1 - c, kj), 3 + j, i, sib)
                               for i, (s, o) in pairs if split[i]])
        return own, ici, landed, fwd, fwd_landed

    def start(srcs, outs, sems):
        own, ici, _, _, _ = copies(srcs, outs, sems)
        for cp in own + [cp for per_chip in ici for cp in per_chip]:
            cp.start()

    def finish(srcs, outs, sems):
        own, ici, landed, fwd, fwd_landed = copies(srcs, outs, sems)
        passed = [i for i in range(n) if split[i]]
        for j in range(3):
            for i, cp in enumerate(landed[j]):
                cp.wait_recv()
                if split[i]:
                    fwd[j][passed.index(i)].start()
        for j in range(3):
            for cp in fwd_landed[j]:
                cp.wait_recv()
        for cp in own:
            cp.wait_recv()
        for cp in own + [cp for per_chip in ici + fwd for cp in per_chip]:
            cp.wait_send()

    return _Carried(shards, [_sds((NCHIP,) + s.shape, s.dtype) for s in shards],
                    [pltpu.SemaphoreType.DMA((7, n)), pltpu.SemaphoreType.DMA((7, n))], start, finish)


def _swap_sibling_halves(name, grads):
    n = len(grads)

    def body(*refs):
        srcs, outs, (send_sems, recv_sems) = refs[:n], refs[n:2 * n], refs[2 * n:]
        x, y, c = _mesh_pos()
        cps = [pltpu.make_async_remote_copy(src_ref=s.at[:, _half_rows(1 - c, s.shape[1], 8)], dst_ref=o,
                                            send_sem=send_sems.at[i], recv_sem=recv_sems.at[i], device_id=(x, y, 1 - c),
                                            device_id_type=MESH) for i, (s, o) in enumerate(zip(srcs, outs))]
        for cp in cps:
            cp.start()
        for cp in cps:
            cp.wait()

    return _pallas_call(
        body, name="swap_sibling_halves_" + name, in_specs=[ANY] * n, out_specs=[ANY] * n,
        out_shape=[_sds((NCHIP, g.shape[1] // 2, g.shape[2])) for g in grads],
        scratch_shapes=[pltpu.SemaphoreType.DMA((n,)), pltpu.SemaphoreType.DMA((n,))],
    )(*grads)


def _add_sibling(name, c_idx, g, got):
    hr, cols = got.shape[1:]

    def body(c_ref, g_ref, got_ref, p_ref, pb_ref):
        s = g_ref[...] + got_ref[...]
        p_ref[...] = s
        pb_ref[...] = s.astype(BF)

    spec = pl.BlockSpec((None, hr, cols), lambda k, c_ref: (k, 0, 0))
    return _pallas_call(
        body, name="add_sibling_" + name,
        grid_spec=pltpu.PrefetchScalarGridSpec(
            num_scalar_prefetch=1, grid=(NCHIP,),
            in_specs=[pl.BlockSpec((None, hr, cols), lambda k, c_ref: (k, c_ref[0], 0)), spec],
            out_specs=[spec, spec]),
        out_shape=[_sds((NCHIP, hr, cols)), _sds((NCHIP, hr, cols), BF)],
        compiler_params=_params(32),
    )(c_idx, g, got)


def _exchange_group(parts):
    n = len(parts)

    def copies(srcs, outs, sems):
        send_sems, recv_sems = sems
        x, y, c = _mesh_pos()
        return [pltpu.make_async_remote_copy(
            src_ref=s.at[2 * chip[0] + chip[1]], dst_ref=o.at[j], send_sem=send_sems.at[j, i],
            recv_sem=recv_sems.at[j, i], device_id=(*chip, c), device_id_type=MESH)
            for j, chip in enumerate(_other_chips(x, y)) for i, (s, o) in enumerate(zip(srcs, outs))]

    def start(srcs, outs, sems):
        for cp in copies(srcs, outs, sems):
            cp.start()

    def finish(srcs, outs, sems):
        for cp in copies(srcs, outs, sems):
            cp.wait()

    return _Carried(parts, [_sds((3,) + p.shape[1:], BF) for p in parts],
                    [pltpu.SemaphoreType.DMA((3, n)), pltpu.SemaphoreType.DMA((3, n))], start, finish)


def _add_chips(name, kc_idx, p, got):
    hr, cols = got.shape[1:]

    def body(kc_ref, p_ref, got_ref, t_ref):
        t_ref[...] = ((p_ref[...] + got_ref[0].astype(F32)) + got_ref[1].astype(F32)) + got_ref[2].astype(F32)

    return _pallas_call(
        body, name="add_chips_" + name,
        grid_spec=pltpu.PrefetchScalarGridSpec(
            num_scalar_prefetch=1, grid=(1,),
            in_specs=[pl.BlockSpec((None, hr, cols), lambda i, kc_ref: (kc_ref[0], 0, 0)),
                      pl.BlockSpec((3, hr, cols), lambda i, kc_ref: (0, 0, 0))],
            out_specs=pl.BlockSpec((None, hr, cols), lambda i, kc_ref: (kc_ref[1], 0, 0))),
        out_shape=_sds((2, hr, cols)),
        compiler_params=_params(32),
    )(kc_idx, p, got)


def _join_sibling(name, halves):
    n = len(halves)

    def body(*refs):
        bufs, (send_sems, recv_sems) = refs[n:2 * n], refs[2 * n:]
        x, y, c = _mesh_pos()
        sib = (x, y, 1 - c)
        sends = [pltpu.make_async_remote_copy(src_ref=b.at[c], dst_ref=b.at[c], send_sem=send_sems.at[i],
                                              recv_sem=recv_sems.at[i], device_id=sib, device_id_type=MESH)
                 for i, b in enumerate(bufs)]
        for cp in sends:
            cp.start()
        for i, b in enumerate(bufs):
            pltpu.make_async_remote_copy(src_ref=b.at[c], dst_ref=b.at[1 - c], send_sem=send_sems.at[i],
                                         recv_sem=recv_sems.at[i], device_id=sib, device_id_type=MESH).wait_recv()
        for cp in sends:
            cp.wait_send()

    return _pallas_call(
        body, name="join_sibling_" + name, in_specs=[ANY] * n, out_specs=[ANY] * n,
        out_shape=[_sds(h.shape) for h in halves], input_output_aliases={i: i for i in range(n)},
        scratch_shapes=[pltpu.SemaphoreType.DMA((n,)), pltpu.SemaphoreType.DMA((n,))],
    )(*halves)


def _allreduce_small(arrays):
    n = len(arrays)

    def body(*refs):
        srcs, outs = refs[:n], refs[n:2 * n]
        sibs, chip_bufs = refs[2 * n:3 * n], refs[3 * n:4 * n]
        send_sems, recv_sems = refs[4 * n:]
        x, y, c = _mesh_pos()
        k0 = 2 * x + y

        def remote(src, dst, j, i, to):
            return pltpu.make_async_remote_copy(src_ref=src, dst_ref=dst, send_sem=send_sems.at[j, i],
                                                recv_sem=recv_sems.at[j, i], device_id=to, device_id_type=MESH)

        swaps = [remote(s, b, 0, i, (x, y, 1 - c)) for i, (s, b) in enumerate(zip(srcs, sibs))]
        for cp in swaps:
            cp.start()
        for cp in swaps:
            cp.wait()
        for s, b, buf in zip(srcs, sibs, chip_bufs):
            buf[k0] = s[...] + b[...]
        chips = _other_chips(x, y)
        sends = [remote(buf.at[k0], buf.at[k0], 1 + j, i, (*chip, c))
                 for j, chip in enumerate(chips) for i, buf in enumerate(chip_bufs)]
        for cp in sends:
            cp.start()
        for j, chip in enumerate(chips):
            for i, buf in enumerate(chip_bufs):
                remote(buf.at[k0], buf.at[2 * chip[0] + chip[1]], 1 + j, i, (*chip, c)).wait_recv()
        for cp in sends:
            cp.wait_send()
        for o, buf in zip(outs, chip_bufs):
            o[...] = ((buf[0] + buf[1]) + buf[2]) + buf[3]

    specs = [_full(a.shape) for a in arrays]
    return _pallas_call(
        body, name="allreduce_small", grid=(1,), in_specs=specs, out_specs=specs,
        out_shape=[_sds(a.shape) for a in arrays],
        scratch_shapes=([pltpu.VMEM(a.shape, F32) for a in arrays] + [pltpu.VMEM((NCHIP,) + a.shape, F32) for a in arrays]
                        + [pltpu.SemaphoreType.DMA((4, n)), pltpu.SemaphoreType.DMA((4, n))]),
        compiler_params=_params(32),
    )(*arrays)


def _adamw_terms(w, g, m, v):
    m = ADAM_B1 * m + (1.0 - ADAM_B1) * g
    v = ADAM_B2 * v + (1.0 - ADAM_B2) * jnp.square(g)
    m_hat = m / (1.0 - ADAM_B1 ** ADAM_STEP)
    v_hat = v / (1.0 - ADAM_B2 ** ADAM_STEP)
    return -ADAM_LR * (m_hat / (jnp.sqrt(v_hat) + ADAM_EPS) + ADAM_WD * w), m, v


def _adamw(name, w, g, m, v):
    r, c = w.shape
    rows = max(b for b in range(SUB, r + 1, SUB) if r % b == 0 and b * c * 4 <= 3 * VMEM_MB // 2)

    def body(w_ref, g_ref, m_ref, v_ref, d_ref, nm_ref, nv_ref):
        d_ref[...], nm_ref[...], nv_ref[...] = _adamw_terms(w_ref[...], g_ref[...], m_ref[...], v_ref[...])

    spec = pl.BlockSpec((rows, c), lambda i: (i, 0))
    return _pallas_call(
        body, name=name, grid=(r // rows,), in_specs=[spec] * 4, out_specs=[spec] * 3,
        out_shape=[_sds((r, c))] * 3, compiler_params=_params(40),
    )(w, g, m, v)


def _adamw_replicated(sums, row_of, direct):
    ns, nr, nd = len(sums), len(row_of), len(direct)

    def body(*refs):
        sum_refs = refs[:ns]
        ins = refs[ns:ns + 3 * nr + 4 * nd]
        outs = refs[ns + 3 * nr + 4 * nd:]
        for i, (_, _, _, si, row) in enumerate(row_of):
            w_ref, m_ref, v_ref = ins[3 * i:3 * i + 3]
            g = sum_refs[si][row:row + 1, :]
            outs[4 * i][...] = g
            outs[4 * i + 1][...], outs[4 * i + 2][...], outs[4 * i + 3][...] = _adamw_terms(w_ref[...], g, m_ref[...], v_ref[...])
        for i in range(nd):
            w_ref, m_ref, v_ref, g_ref = ins[3 * nr + 4 * i:3 * nr + 4 * i + 4]
            o = outs[4 * (nr + i):4 * (nr + i) + 4]
            g = g_ref[...]
            o[0][...] = g
            o[1][...], o[2][...], o[3][...] = _adamw_terms(w_ref[...], g, m_ref[...], v_ref[...])

    operands = list(sums)
    shapes = []
    for w, m, v, _, _ in row_of:
        operands += [w, m, v]
        shapes += [w.shape] * 4
    for w, m, v, g in direct:
        operands += [w, m, v, g]
        shapes += [w.shape] * 4
    flat = _pallas_call(
        body, name="adamw_replicated", grid=(1,), in_specs=[_full(a.shape) for a in operands],
        out_specs=[_full(s) for s in shapes], out_shape=[_sds(s) for s in shapes],
        compiler_params=_params(56),
    )(*operands)
    return [flat[4 * i:4 * i + 4] for i in range(nr + nd)]


class _Exchanges:
    def __init__(self, shards, conv_w, chip, core, apply):
        self.shards, self.conv_w, self.apply = shards, conv_w, apply
        self.core_idx = jnp.reshape(core, (1,)).astype(jnp.int32)
        self.chip_core_idx = jnp.stack([chip, core]).astype(jnp.int32)

    def first(self):
        names = ["w_in", "w_glu"]
        got = _run_now("gather_first", _gather_group([self.shards[n] for n in names] + [self.conv_w],
                                                     [True, True, False]))
        out = dict(zip(names, got))
        out["conv_w"] = jnp.transpose(got[2], (1, 0, 2)).reshape(4, LW)
        return out

    def gather(self, names):
        return _gather_group([self.shards[n] for n in names], [True] * len(names))

    def reduce_begin(self, tag, grads):
        names = list(grads)
        arrived = _swap_sibling_halves(tag, [grads[n] for n in names])
        parts = [_add_sibling(n, self.core_idx, grads[n], rx) for n, rx in zip(names, arrived)]
        return parts, _exchange_group([bf for _, bf in parts])

    def reduce_end(self, tag, grads, parts, arrived):
        names = list(grads)
        halves = [_add_chips(n, self.chip_core_idx, f32, rx) for n, (f32, _), rx in zip(names, parts, arrived)]
        for n, both in zip(names, _join_sibling(tag, halves)):
            self.apply(n, both.reshape(dict(SHARDED)[n]))


INPUT_NAMES = (["x", "p"] + [n for n in
               ["g_mix", "w_in", "b_in", "lam_re", "lam_im", "log_dt", "s5_b_re", "s5_b_im", "s5_c_re", "s5_c_im", "s5_d",
                "w_glu", "b_glu", "conv_w", "conv_b", "w_r", "b_r", "w_i", "b_i", "lru_lambda", "w_a_out", "w_b_out", "w_o",
                "g_ffn", "w_ffn_gate", "w_ffn_up", "w_ffn_down", "g_ple_gate", "w_ple_gate", "b_ple_gate", "w_ple", "g_ple",
                "g_final"]])
WEIGHT_NAMES = INPUT_NAMES[2:]


def kernel(*args):
    names = INPUT_NAMES + ["loss_target"] + ["m_" + n for n in WEIGHT_NAMES] + ["v_" + n for n in WEIGHT_NAMES]
    assert len(args) == len(names)
    given = dict(zip(names, args))

    def view(name):
        a = given[name]
        return jnp.swapaxes(a, -1, -2) if name.endswith(TRANSPOSED) else a

    def unview(name, a):
        return jnp.swapaxes(a, -1, -2) if name in TRANSPOSED else a

    def local(name):
        return view(name) if name.endswith("g_final") else view(name)[0]

    xi, yi, ci = _mesh_pos()
    k0 = 2 * xi + yi
    x, p, tgt = given["x"][0], given["p"][0, 0], given["loss_target"][0]

    results = {}

    def apply(n, total):
        delta, new_m, new_v = _adamw("adamw_" + n, local(n), total, local("m_" + n), local("v_" + n))
        for kind, arr in zip(("grad", "delta", "new_m", "new_v"), (total, delta, new_m, new_v)):
            results[kind, n] = unview(n, arr[None])

    comm = _Exchanges({n: local(n).astype(BF) for n, _ in SHARDED}, local("conv_w"), k0, ci, apply)
    w = {n: local(n) for n in WEIGHT_NAMES if n != "conv_w" and n not in dict(SHARDED)}
    gx, sums, blocks = _local_step(x, p, tgt, w, comm)

    sum_names, block_names = list(sums), list(blocks)
    red = _allreduce_small([sums[n] for n in sum_names] + [blocks[n] for n in block_names])
    sums = dict(zip(sum_names, red[:len(sum_names)]))
    blocks = dict(zip(block_names, red[len(sum_names):]))
    loss = jnp.sum(sums[LOSS_ROW[0]][LOSS_ROW[1]])
    direct_g = _replicated_grads(w, sums, blocks)
    conv_rows = sums[CONV_W_ROWS[0]][CONV_W_ROWS[1]:CONV_W_ROWS[1] + 4]
    direct_g["conv_w"] = lax.dynamic_slice(conv_rows, (0, k0 * CONV_SHARD[1]), CONV_SHARD)
    as_row = lambda a: a.reshape(1, -1)
    row_names = list(ACC_ROWS)
    row_of = [(as_row(given[n]), as_row(given["m_" + n]), as_row(given["v_" + n]),
               sum_names.index(ACC_ROWS[n][0]), ACC_ROWS[n][1]) for n in row_names]
    direct_names = list(direct_g)
    direct = [(view(n), view("m_" + n), view("v_" + n), direct_g[n].reshape(view(n).shape)) for n in direct_names]
    done = _adamw_replicated([sums[n] for n in sum_names], row_of, direct)
    for n, four in zip(row_names + direct_names, done):
        for kind, arr in zip(("grad", "delta", "new_m", "new_v"), four):
            results[kind, n] = unview(n, arr).reshape(given[n].shape)

    out = [loss, gx[None]]
    for kind in ("grad", "delta", "new_m", "new_v"):
        out += [results[kind, n] for n in WEIGHT_NAMES]
    return tuple(out)
```

```python
import functools
import math

import jax
import jax.numpy as jnp
from jax import lax
from jax.experimental import pallas as pl
from jax.experimental.pallas import tpu as pltpu

F32 = jnp.float32
BF = jnp.bfloat16

D = 1024
S5W = 512
NG, NS, NP = 32, 64, 16
GN = NG * NS
LW = 1024
NH, HD = 16, 64
LRU_C = 8.0
FH = 2816
NCHIP = 4
FC = FH // NCHIP
PLE = 256
INC = S5W + LW + 2 * D
EPS = 1e-6
ADAM_LR, ADAM_B1, ADAM_B2, ADAM_EPS, ADAM_WD, ADAM_STEP = 0.001, 0.9, 0.999, 1e-08, 0.01, 10

TM = 256
TK = 512
LC = 512
SUB = 8
VMEM_MB = 1024 * 1024
MESH = pl.DeviceIdType.MESH
ANY = pl.BlockSpec(memory_space=pl.ANY)


def _mm(a, b):
    return jnp.dot(a.astype(BF), b.astype(BF), preferred_element_type=F32)


def _mm_nt(a, b):
    return lax.dot_general(a.astype(BF), b.astype(BF), (((1,), (1,)), ((), ())), preferred_element_type=F32)


def _mm_tn(a, b):
    return lax.dot_general(a.astype(BF), b.astype(BF), (((0,), (0,)), ((), ())), preferred_element_type=F32)


def _rms(x):
    r = lax.rsqrt(jnp.mean(x * x, axis=-1, keepdims=True) + EPS)
    return x * r, r


def _rms_bwd(dy, xh, r, g):
    dxh = dy * g
    return r * (dxh - xh * jnp.mean(dxh * xh, axis=-1, keepdims=True))


def _colsum(x):
    return jnp.sum(x, axis=0, keepdims=True)


def _sig(x):
    return jax.nn.sigmoid(x)


def _gelu_grad(x):
    c = math.sqrt(2.0 / math.pi)
    t = jnp.tanh(c * (x + 0.044715 * x * x * x))
    return 0.5 * (1.0 + t) + 0.5 * x * (1.0 - t * t) * c * (1.0 + 3.0 * 0.044715 * x * x)


def _neg_expm1(x):
    series = -x * (1.0 + x * (0.5 + x * (1.0 / 6.0 + x * (1.0 / 24.0))))
    return jnp.where(x > -0.03, series, 1.0 - jnp.exp(x))


def _tok(width):
    return pl.BlockSpec((TM, width), lambda i: (i, 0))


def _tok_rev(width, nt):
    return pl.BlockSpec((TM, width), lambda i: (nt - 1 - i, 0))


def _full(shape):
    return pl.BlockSpec(shape, lambda i: (0,) * len(shape))


def _params(vmem_mb, **kw):
    return pltpu.CompilerParams(dimension_semantics=("arbitrary",), vmem_limit_bytes=vmem_mb * VMEM_MB, **kw)


def _sds(shape, dtype=F32):
    return jax.ShapeDtypeStruct(shape, dtype)


class _Carried:
    def __init__(self, operands, out_shapes, sems, start, finish):
        self.operands, self.out_shapes, self.sems, self.start, self.finish = operands, out_shapes, sems, start, finish


def _pallas_call(body, carry=None, **kw):
    if carry is None:
        return pl.pallas_call(body, **kw)
    name, grid, compiler_params = kw["name"], kw["grid"], kw["compiler_params"]
    in_specs, out_specs, out_shape = list(kw["in_specs"]), list(kw["out_specs"]), list(kw["out_shape"])
    scratch_shapes = list(kw.get("scratch_shapes", ()))
    n_in, n_out, n_scr = len(in_specs), len(out_specs), len(scratch_shapes)
    c_in, c_out = len(carry.operands), len(carry.out_shapes)

    def full_body(*refs):
        ins, refs = refs[:n_in], refs[n_in:]
        c_ins, refs = refs[:c_in], refs[c_in:]
        outs, refs = refs[:n_out], refs[n_out:]
        c_outs, refs = refs[:c_out], refs[c_out:]
        scratch, c_sems = refs[:n_scr], refs[n_scr:]

        @pl.when(pl.program_id(0) == 0)
        def _():
            carry.start(c_ins, c_outs, c_sems)

        body(*ins, *outs, *scratch)

        @pl.when(pl.program_id(0) == grid[0] - 1)
        def _():
            carry.finish(c_ins, c_outs, c_sems)

    call = pl.pallas_call(
        full_body, name=name, grid=grid, in_specs=in_specs + [ANY] * c_in, out_specs=out_specs + [ANY] * c_out,
        out_shape=out_shape + list(carry.out_shapes), scratch_shapes=scratch_shapes + list(carry.sems),
        compiler_params=compiler_params)
    return lambda *operands: call(*operands, *carry.operands)


def _row_iota(width):
    return lax.broadcasted_iota(jnp.int32, (SUB, width), 0)


def _bcast_row(x, row):
    return jnp.broadcast_to(x[row:row + 1, :], x.shape)


def _slab(k):
    return pl.ds(pl.multiple_of(k * SUB, SUB), SUB)


QC = INC // NCHIP
Z_PARTS = ((0, S5W), (S5W, S5W + LW), (S5W + LW, INC))


def _inproj_fwd(x, g_mix, w_in, b_in, carry=None):
    L = x.shape[0]

    def body(x_ref, g_ref, w_hbm, b_ref, h_ref, ua_ref, ub_ref, gp_ref, w_vm):
        @pl.when(pl.program_id(0) == 0)
        def _():
            pltpu.sync_copy(w_hbm, w_vm)

        xh, _ = _rms(x_ref[...])
        h = (xh * g_ref[...]).astype(BF)
        h_ref[...] = h
        for k in range(NCHIP):
            lo, hi = k * QC, (k + 1) * QC
            z = jnp.dot(h, w_vm[k], preferred_element_type=F32) + b_ref[:, lo:hi]
            for ref, (a, b) in zip((ua_ref, ub_ref, gp_ref), Z_PARTS):
                s, e = max(lo, a), min(hi, b)
                if s < e:
                    ref[:, s - a:e - a] = z[:, s - lo:e - lo]

    return _pallas_call(
        body, carry, name="inproj_fwd", grid=(L // TM,),
        in_specs=[_tok(D), _full((1, D)), ANY, _full((1, INC))],
        out_specs=[_tok(D), _tok(S5W), _tok(LW), _tok(2 * D)],
        out_shape=[_sds((L, D), BF), _sds((L, S5W)), _sds((L, LW)), _sds((L, 2 * D))],
        scratch_shapes=[pltpu.VMEM((NCHIP, D, QC), BF)],
        compiler_params=_params(40),
    )(x, g_mix, w_in, b_in)


def _inproj_bwd(x, dx1, dua, dub, dgp, g_mix, w_in):
    L = x.shape[0]

    def body(x_ref, dx1_ref, dua_ref, dub_ref, dgp_ref, g_ref, w_hbm, gx_ref, dz_ref, dg_ref, db_ref, w_vm):
        @pl.when(pl.program_id(0) == 0)
        def _():
            pltpu.sync_copy(w_hbm, w_vm)
            dg_ref[...] = jnp.zeros_like(dg_ref)
            db_ref[...] = jnp.zeros_like(db_ref)

        for src, (a, b) in zip((dua_ref, dub_ref, dgp_ref), Z_PARTS):
            d = src[...]
            dz_ref[:, a:b] = d.astype(BF)
            db_ref[0:1, a:b] += _colsum(d)
        dh = jnp.zeros((TM, D), F32)
        for k in range(NCHIP):
            dh = dh + lax.dot_general(dz_ref[:, k * QC:(k + 1) * QC], w_vm[k], (((1,), (1,)), ((), ())),
                                      preferred_element_type=F32)
        xh, r = _rms(x_ref[...])
        dg_ref[0:1, :] += _colsum(dh * xh)
        gx_ref[...] = dx1_ref[...] + _rms_bwd(dh, xh, r, g_ref[...])

    return _pallas_call(
        body, name="inproj_bwd", grid=(L // TM,),
        in_specs=[_tok(D), _tok(D), _tok(S5W), _tok(LW), _tok(2 * D), _full((1, D)), ANY],
        out_specs=[_tok(D), _tok(INC), _full((SUB, D)), _full((SUB, INC))],
        out_shape=[_sds((L, D)), _sds((L, INC), BF), _sds((SUB, D)), _sds((SUB, INC))],
        scratch_shapes=[pltpu.VMEM((NCHIP, D, QC), BF)],
        compiler_params=_params(40),
    )(x, dx1, dua, dub, dgp, g_mix, w_in)


def _cscan(xr_ref, xi_ref, con_ref, cr_ref, ci_ref, reverse):
    n_slab = xr_ref.shape[0] // SUB
    width = xr_ref.shape[1]
    for lc in range(width // LC):
        cols = slice(lc * LC, (lc + 1) * LC)
        con = [con_ref[SUB * j:SUB * (j + 1), cols] for j in range(8)]

        def step(k, carry, cols=cols, con=con):
            cr, ci = carry
            rows = _slab(n_slab - 1 - k if reverse else k)
            xr, xi = xr_ref[rows, cols], xi_ref[rows, cols]
            for j, sh in enumerate((1, 2, 4)):
                mr, mi = con[2 * j], con[2 * j + 1]
                pr = pltpu.roll(xr, SUB - sh if reverse else sh, 0)
                pi = pltpu.roll(xi, SUB - sh if reverse else sh, 0)
                xr, xi = xr + mr * pr - mi * pi, xi + mr * pi + mi * pr
            xr, xi = xr + con[6] * cr - con[7] * ci, xi + con[6] * ci + con[7] * cr
            xr_ref[rows, cols] = xr
            xi_ref[rows, cols] = xi
            row = 0 if reverse else SUB - 1
            return _bcast_row(xr, row), _bcast_row(xi, row)

        cr, ci = lax.fori_loop(0, n_slab, step, (cr_ref[:, cols], ci_ref[:, cols]))
        cr_ref[:, cols] = cr
        ci_ref[:, cols] = ci


def _s5_fwd(ua, bbr, bbi, ccr, cci, dsk, con, w_glu, b_glu, carry=None):
    L = ua.shape[0]

    def body(ua_ref, bbr_hbm, bbi_hbm, ccr_hbm, cci_hbm, dsk_ref, con_ref, wg_ref, bg_ref,
             sr_ref, si_ref, y_ref, zg_ref, ya_ref, bbr_vm, bbi_vm, ccr_vm, cci_vm, cr_ref, ci_ref):
        @pl.when(pl.program_id(0) == 0)
        def _():
            pltpu.sync_copy(bbr_hbm, bbr_vm)
            pltpu.sync_copy(bbi_hbm, bbi_vm)
            pltpu.sync_copy(ccr_hbm, ccr_vm)
            pltpu.sync_copy(cci_hbm, cci_vm)
            cr_ref[...] = jnp.zeros_like(cr_ref)
            ci_ref[...] = jnp.zeros_like(ci_ref)

        u = ua_ref[...]
        ub = u.astype(BF)
        sr_ref[...] = jnp.dot(ub, bbr_vm[...], preferred_element_type=F32)
        si_ref[...] = jnp.dot(ub, bbi_vm[...], preferred_element_type=F32)
        _cscan(sr_ref, si_ref, con_ref, cr_ref, ci_ref, reverse=False)
        y = _mm_nt(sr_ref[...], ccr_vm[...]) - _mm_nt(si_ref[...], cci_vm[...]) + dsk_ref[...] * u
        y_ref[...] = y
        zg = jax.nn.gelu(y)
        zg_ref[...] = zg.astype(BF)
        q = _mm(zg, wg_ref[...]) + bg_ref[...]
        ya_ref[...] = (zg * _sig(q)).astype(BF)

    return _pallas_call(
        body, carry, name="s5_fwd", grid=(L // TM,),
        in_specs=[_tok(S5W), ANY, ANY, ANY, ANY, _full((1, S5W)), _full((8 * SUB, GN)),
                  _full((S5W, S5W)), _full((1, S5W))],
        out_specs=[_tok(GN), _tok(GN), _tok(S5W), _tok(S5W), _tok(S5W)],
        out_shape=[_sds((L, GN)), _sds((L, GN)), _sds((L, S5W)), _sds((L, S5W), BF), _sds((L, S5W), BF)],
        scratch_shapes=[pltpu.VMEM((S5W, GN), BF), pltpu.VMEM((S5W, GN), BF), pltpu.VMEM((S5W, GN), BF),
                        pltpu.VMEM((S5W, GN), BF),pltpu.VMEM((SUB, GN), F32), pltpu.VMEM((SUB, GN), F32)],
        compiler_params=_params(44),
    )(ua, bbr, bbi, ccr, cci, dsk, con, w_glu, b_glu)


def _s5_bwd(dya, y, ua, sr, si, bbr, bbi, ccr, cci, dsk, con_rev, w_glu, b_glu, carry=None):
    L = ua.shape[0]
    nt = L // TM
    spt = TM // SUB
    n_slab = spt

    def halo_map(i):
        return (jnp.maximum((nt - 1 - i) * spt - 1, 0), 0)

    def body(dya_ref, y_ref, ua_ref, sr_ref, si_ref, hr_ref, hi_ref, bbr_hbm, bbi_hbm, ccr_hbm, cci_hbm,
             dsk_ref, con_ref, wg_ref, bg_ref,
             dua_ref, dq_ref, dy_ref, lr_ref, li_ref, da_ref, dsm_ref,
             bbr_vm, bbi_vm, ccr_vm, cci_vm, cr_ref, ci_ref):
        i = pl.program_id(0)

        @pl.when(i == 0)
        def _():
            pltpu.sync_copy(bbr_hbm, bbr_vm)
            pltpu.sync_copy(bbi_hbm, bbi_vm)
            pltpu.sync_copy(ccr_hbm, ccr_vm)
            pltpu.sync_copy(cci_hbm, cci_vm)
            cr_ref[...] = jnp.zeros_like(cr_ref)
            ci_ref[...] = jnp.zeros_like(ci_ref)
            da_ref[...] = jnp.zeros_like(da_ref)
            dsm_ref[...] = jnp.zeros_like(dsm_ref)

        u = ua_ref[...]
        yv = y_ref[...]
        dya = dya_ref[...]
        zg = jax.nn.gelu(yv)
        sg = _sig(_mm(zg, wg_ref[...]) + bg_ref[...])
        dq = dya * zg * sg * (1.0 - sg)
        dq_ref[...] = dq.astype(BF)
        dzg = dya * sg + _mm_nt(dq, wg_ref[...])
        dy = dzg * _gelu_grad(yv)
        dyb = dy.astype(BF)
        dy_ref[...] = dyb
        dsm_ref[0:1, :] += _colsum(dy * u)
        dsm_ref[1:2, :] += _colsum(dq)
        lr_ref[...] = jnp.dot(dyb, ccr_vm[...], preferred_element_type=F32)
        li_ref[...] = -jnp.dot(dyb, cci_vm[...], preferred_element_type=F32)
        _cscan(lr_ref, li_ref, con_ref, cr_ref, ci_ref, reverse=True)

        first_tile = (i == nt - 1)
        row = _row_iota(LC)
        for lc in range(GN // LC):
            cols = slice(lc * LC, (lc + 1) * LC)
            h_r = jnp.where(first_tile, 0.0, hr_ref[:, cols])
            h_i = jnp.where(first_tile, 0.0, hi_ref[:, cols])

            def step(k, acc, cols=cols, h_r=h_r, h_i=h_i):
                ar, ai = acc
                rows = _slab(k)
                prev = _slab(jnp.maximum(k - 1, 0))
                pr = jnp.where(k == 0, h_r, sr_ref[prev, cols])
                pi = jnp.where(k == 0, h_i, si_ref[prev, cols])
                spr = pltpu.roll(jnp.where(row == SUB - 1, pr, sr_ref[rows, cols]), 1, 0)
                spi = pltpu.roll(jnp.where(row == SUB - 1, pi, si_ref[rows, cols]), 1, 0)
                lr, li = lr_ref[rows, cols], li_ref[rows, cols]
                return ar + lr * spr + li * spi, ai + li * spr - lr * spi

            zero = jnp.zeros((SUB, LC), F32)
            ar, ai = lax.fori_loop(0, n_slab, step, (zero, zero))
            da_ref[0:1, cols] += _colsum(ar)
            da_ref[1:2, cols] += _colsum(ai)

        dua_ref[...] = (dy * dsk_ref[...] + _mm_nt(lr_ref[...], bbr_vm[...]) + _mm_nt(li_ref[...], bbi_vm[...]))

    return _pallas_call(
        body, carry, name="s5_bwd", grid=(nt,),
        in_specs=[_tok_rev(S5W, nt), _tok_rev(S5W, nt), _tok_rev(S5W, nt), _tok_rev(GN, nt), _tok_rev(GN, nt),
                  pl.BlockSpec((SUB, GN), halo_map), pl.BlockSpec((SUB, GN), halo_map),
                  ANY, ANY, ANY, ANY, _full((1, S5W)), _full((8 * SUB, GN)), _full((S5W, S5W)), _full((1, S5W))],
        out_specs=[_tok_rev(S5W, nt), _tok_rev(S5W, nt), _tok_rev(S5W, nt), _tok_rev(GN, nt), _tok_rev(GN, nt),
                   _full((SUB, GN)), _full((SUB, S5W))],
        out_shape=[_sds((L, S5W)), _sds((L, S5W), BF), _sds((L, S5W), BF), _sds((L, GN)), _sds((L, GN)),
                   _sds((SUB, GN)), _sds((SUB, S5W))],
        scratch_shapes=[pltpu.VMEM((S5W, GN), BF), pltpu.VMEM((S5W, GN), BF), pltpu.VMEM((S5W, GN), BF),
                        pltpu.VMEM((S5W, GN), BF),pltpu.VMEM((SUB, GN), F32), pltpu.VMEM((SUB, GN), F32)],
        compiler_params=_params(52),
    )(dya, y, ua, sr, si, sr, si, bbr, bbi, ccr, cci, dsk, con_rev, w_glu, b_glu)


def _lru_gate_terms(rg, sp):
    log_a = -LRU_C * rg * sp
    a = jnp.exp(log_a)
    mult = jnp.sqrt(_neg_expm1(2.0 * log_a))
    return a, mult


def _lru_fwd(ub, conv_w, conv_b, wr, wi, b_r, b_i, sp, carry=None):
    L = ub.shape[0]
    n_slab = TM // SUB

    def body(ub_ref, cw_ref, cb_ref, wr_ref, wi_ref, br_ref, bi_ref, sp_ref,
             xc_ref, rg_ref, ig_ref, h_ref, hp_ref, a_ref, halo_ref, carry_ref):
        @pl.when(pl.program_id(0) == 0)
        def _():
            halo_ref[...] = jnp.zeros_like(halo_ref)
            carry_ref[...] = jnp.zeros_like(carry_ref)

        row = _row_iota(LW)
        taps = [cw_ref[k:k + 1, :] for k in range(4)]
        cb = cb_ref[...]

        def conv_step(k, prev):
            rows = _slab(k)
            cur = ub_ref[rows, :]
            acc = taps[3] * cur + cb
            for j in (1, 2, 3):
                acc = acc + taps[3 - j] * pltpu.roll(jnp.where(row >= SUB - j, prev, cur), j, 0)
            xc_ref[rows, :] = acc
            return cur

        halo_ref[...] = lax.fori_loop(0, n_slab, conv_step, halo_ref[...])

        xc = xc_ref[...]
        xcb = xc.astype(BF)
        rg = _sig(jnp.dot(xcb, wr_ref[...], preferred_element_type=F32) + br_ref[...])
        ig = _sig(jnp.dot(xcb, wi_ref[...], preferred_element_type=F32) + bi_ref[...])
        rg_ref[...] = rg
        ig_ref[...] = ig
        a, mult = _lru_gate_terms(rg, sp_ref[...])
        a_ref[...] = a
        h_ref[...] = mult * ig * xc

        rowc = _row_iota(LC)
        for lc in range(LW // LC):
            cols = slice(lc * LC, (lc + 1) * LC)

            def step(k, c, cols=cols):
                rows = _slab(k)
                av, b = a_ref[rows, cols], h_ref[rows, cols]
                for sh in (1, 2, 4):
                    keep = rowc >= sh
                    b = b + av * jnp.where(keep, pltpu.roll(b, sh, 0), 0.0)
                    av = av * jnp.where(keep, pltpu.roll(av, sh, 0), 1.0)
                h = b + av * c
                h_ref[rows, cols] = h
                hp_ref[rows, cols] = jnp.where(rowc == 0, c, pltpu.roll(h, 1, 0))
                return _bcast_row(h, SUB - 1)

            carry_ref[:, cols] = lax.fori_loop(0, n_slab, step, carry_ref[:, cols])

    return _pallas_call(
        body, carry, name="lru_fwd", grid=(L // TM,),
        in_specs=[_tok(LW), _full((4, LW)), _full((1, LW)), _full((LW, LW)), _full((LW, LW)),
                  _full((1, LW)), _full((1, LW)), _full((1, LW))],
        out_specs=[_tok(LW)] * 5,
        out_shape=[_sds((L, LW))] * 5,
        scratch_shapes=[pltpu.VMEM((TM, LW), F32), pltpu.VMEM((SUB, LW), F32), pltpu.VMEM((SUB, LW), F32)],
        compiler_params=_params(40),
    )(ub, conv_w, conv_b, wr, wi, b_r, b_i, sp)


def _lru_bwd(dyb, xc, rg, ig, hp, ub, conv_w, wr, wi, sp, dsp, carry=None):
    L = ub.shape[0]
    nt = L // TM
    spt = TM // SUB
    n_slab = spt

    def halo_map(i):
        return (jnp.maximum((nt - 1 - i) * spt - 1, 0), 0)

    def body(dh_ref, xc_ref, rg_ref, ig_ref, hp_ref, ub_ref, uh_ref, cw_ref, wr_ref, wi_ref, sp_ref, dsp_ref,
             dub_ref, dpr_ref, dpi_ref, acc_ref, a_ref, lam_ref, dxc_ref, carry_ref, next_ref):
        i = pl.program_id(0)

        @pl.when(i == 0)
        def _():
            carry_ref[...] = jnp.zeros_like(carry_ref)
            next_ref[...] = jnp.zeros_like(next_ref)
            acc_ref[...] = jnp.zeros_like(acc_ref)

        sp = sp_ref[...]
        rg, ig, xc = rg_ref[...], ig_ref[...], xc_ref[...]
        a, mult = _lru_gate_terms(rg, sp)
        a_ref[...] = a

        rowc = _row_iota(LC)
        for lc in range(LW // LC):
            cols = slice(lc * LC, (lc + 1) * LC)

            def step(k, c, cols=cols):
                rows = _slab(n_slab - 1 - k)
                av, dh = a_ref[rows, cols], dh_ref[rows, cols]
                b = av * dh
                for sh in (1, 2, 4):
                    keep = rowc < SUB - sh
                    b = b + av * jnp.where(keep, pltpu.roll(b, SUB - sh, 0), 0.0)
                    av = av * jnp.where(keep, pltpu.roll(av, SUB - sh, 0), 1.0)
                mu = b + av * c
                lam_ref[rows, cols] = dh + jnp.where(rowc == SUB - 1, c, pltpu.roll(mu, SUB - 1, 0))
                return _bcast_row(mu, 0)

            carry_ref[:, cols] = lax.fori_loop(0, n_slab, step, carry_ref[:, cols])

        lam = lam_ref[...]
        d_a = lam * hp_ref[...]
        d_mult = lam * ig * xc
        d_ig = lam * mult * xc
        dxc = lam * mult * ig
        d_log_a = d_a * a - d_mult * a * a / mult
        d_rg = (-LRU_C) * sp * d_log_a
        acc_ref[0:1, :] += _colsum((-LRU_C) * rg * d_log_a) * dsp_ref[...]
        dpr = d_rg * rg * (1.0 - rg)
        dpi = d_ig * ig * (1.0 - ig)
        acc_ref[1:2, :] += _colsum(dpr)
        acc_ref[2:3, :] += _colsum(dpi)
        dprb, dpib = dpr.astype(BF), dpi.astype(BF)
        dpr_ref[...] = dprb
        dpi_ref[...] = dpib
        dxc = dxc + _mm_nt(dprb, wr_ref[...]) + _mm_nt(dpib, wi_ref[...])
        dxc_ref[...] = dxc
        acc_ref[3:4, :] += _colsum(dxc)

        row = _row_iota(LW)
        taps = [cw_ref[k:k + 1, :] for k in range(4)]
        u_halo = jnp.where(i == nt - 1, 0.0, uh_ref[...])
        nxt_tile = next_ref[...]

        def conv_step(k, accs):
            rows = _slab(k)
            cur = dxc_ref[rows, :]
            nxt = jnp.where(k == n_slab - 1, nxt_tile, dxc_ref[_slab(jnp.minimum(k + 1, n_slab - 1)), :])
            ucur = ub_ref[rows, :]
            uprev = jnp.where(k == 0, u_halo, ub_ref[_slab(jnp.maximum(k - 1, 0)), :])
            du = taps[3] * cur
            new = [accs[3] + cur * ucur]
            for j in (1, 2, 3):
                du = du + taps[3 - j] * pltpu.roll(jnp.where(row < j, nxt, cur), SUB - j, 0)
                new.append(accs[3 - j] + cur * pltpu.roll(jnp.where(row >= SUB - j, uprev, ucur), j, 0))
            dub_ref[rows, :] = du
            return tuple(new[::-1])

        zero = jnp.zeros((SUB, LW), F32)
        accs = lax.fori_loop(0, n_slab, conv_step, (zero, zero, zero, zero))
        for k in range(4):
            acc_ref[4 + k:5 + k, :] += _colsum(accs[k])
        next_ref[...] = dxc_ref[0:SUB, :]

    return _pallas_call(
        body, carry, name="lru_bwd", grid=(nt,),
        in_specs=[_tok_rev(LW, nt)] * 6 + [pl.BlockSpec((SUB, LW), halo_map), _full((4, LW)),
                                           _full((LW, LW)), _full((LW, LW)), _full((1, LW)), _full((1, LW))],
        out_specs=[_tok_rev(LW, nt), _tok_rev(LW, nt), _tok_rev(LW, nt), _full((SUB, LW))],
        out_shape=[_sds((L, LW)), _sds((L, LW), BF), _sds((L, LW), BF), _sds((SUB, LW))],
        scratch_shapes=[pltpu.VMEM((TM, LW), F32), pltpu.VMEM((TM, LW), F32), pltpu.VMEM((TM, LW), F32),
                        pltpu.VMEM((SUB, LW), F32), pltpu.VMEM((SUB, LW), F32)],
        compiler_params=_params(48),
    )(dyb, xc, rg, ig, hp, ub, ub, conv_w, wr, wi, sp, dsp)


AC = D // NCHIP


def _merge_fwd(x, ya, yb, gp, w_a, w_b, w_o, carry=None):
    L = x.shape[0]

    def body(x_ref, ya_ref, yb_ref, gp_ref, wa_ref, wb_ref, wo_ref, x1_ref, pa_ref, pb_ref, mg_ref):
        ya = ya_ref[...]
        for k in range(NCHIP):
            pa_ref[:, k * AC:(k + 1) * AC] = jnp.dot(ya, wa_ref[k], preferred_element_type=F32)
        pb = _mm(yb_ref[...], wb_ref[...])
        pb_ref[...] = pb
        gp = gp_ref[...]
        merged = (_sig(gp[:, :D]) * pa_ref[...] + _sig(gp[:, D:]) * pb).astype(BF)
        mg_ref[...] = merged
        x1_ref[...] = x_ref[...] + jnp.dot(merged, wo_ref[...], preferred_element_type=F32)

    return _pallas_call(
        body, carry, name="merge_fwd", grid=(L // TM,),
        in_specs=[_tok(D), _tok(S5W), _tok(LW), _tok(2 * D), _full((NCHIP, S5W, AC)), _full((LW, D)), _full((D, D))],
        out_specs=[_tok(D), _tok(D), _tok(D), _tok(D)],
        out_shape=[_sds((L, D)), _sds((L, D)), _sds((L, D)), _sds((L, D), BF)],
        compiler_params=_params(40),
    )(x, ya, yb, gp, w_a, w_b, w_o)


def _merge_bwd(dx1, gp, pa, pb, w_a, w_b, w_o, carry=None):
    L = dx1.shape[0]

    def body(dx1_ref, gp_ref, pa_ref, pb_ref, wa_ref, wb_ref, wo_ref, dya_ref, dyb_ref, dgp_ref, dpa_ref, dpb_ref):
        dm = _mm_nt(dx1_ref[...], wo_ref[...])
        gp = gp_ref[...]
        sa, sb = _sig(gp[:, :D]), _sig(gp[:, D:])
        dpa = (dm * sa).astype(BF)
        dpb = (dm * sb).astype(BF)
        dpa_ref[...] = dpa
        dpb_ref[...] = dpb
        dgp_ref[:, :D] = dm * pa_ref[...] * sa * (1.0 - sa)
        dgp_ref[:, D:] = dm * pb_ref[...] * sb * (1.0 - sb)
        dya = jnp.zeros((TM, S5W), F32)
        for k in range(NCHIP):
            dya = dya + _mm_nt(dpa[:, k * AC:(k + 1) * AC], wa_ref[k])
        dya_ref[...] = dya
        dyb_ref[...] = _mm_nt(dpb, wb_ref[...])

    return _pallas_call(
        body, carry, name="merge_bwd", grid=(L // TM,),
        in_specs=[_tok(D), _tok(2 * D), _tok(D), _tok(D), _full((NCHIP, S5W, AC)), _full((LW, D)), _full((D, D))],
        out_specs=[_tok(S5W), _tok(LW), _tok(2 * D), _tok(D), _tok(D)],
        out_shape=[_sds((L, S5W)), _sds((L, LW)), _sds((L, 2 * D)), _sds((L, D), BF), _sds((L, D), BF)],
        compiler_params=_params(40),
    )(dx1, gp, pa, pb, w_a, w_b, w_o)


def _chunk_tok(width):
    return pl.BlockSpec((NCHIP, TM, width), lambda i: (0, i, 0))


def _ffn_fwd(x1, g_ffn, wg, wu, wd, carry=None):
    L = x1.shape[0]

    def body(x_ref, g_ref, wg_hbm, wu_hbm, wd_hbm, x2_ref, h2_ref, gg_ref, uu_ref, wg_vm, wu_vm, wd_vm):
        @pl.when(pl.program_id(0) == 0)
        def _():
            pltpu.sync_copy(wg_hbm, wg_vm)
            pltpu.sync_copy(wu_hbm, wu_vm)
            pltpu.sync_copy(wd_hbm, wd_vm)

        x = x_ref[...]
        xh, _ = _rms(x)
        h2 = (xh * g_ref[...]).astype(BF)
        h2_ref[...] = h2
        out = x
        for c in range(NCHIP):
            gg = lax.dot_general(h2, wg_vm[c], (((1,), (1,)), ((), ())), preferred_element_type=F32)
            uu = lax.dot_general(h2, wu_vm[c], (((1,), (1,)), ((), ())), preferred_element_type=F32)
            gg_ref[c] = gg.astype(BF)
            uu_ref[c] = uu.astype(BF)
            act = (gg * _sig(gg) * uu).astype(BF)
            out = out + jnp.dot(act, wd_vm[c], preferred_element_type=F32)
        x2_ref[...] = out

    return _pallas_call(
        body, carry, name="ffn_fwd", grid=(L // TM,),
        in_specs=[_tok(D), _full((1, D)), ANY, ANY, ANY],
        out_specs=[_tok(D), _tok(D), _chunk_tok(FC), _chunk_tok(FC)],
        out_shape=[_sds((L, D)), _sds((L, D), BF), _sds((NCHIP, L, FC), BF), _sds((NCHIP, L, FC), BF)],
        scratch_shapes=[pltpu.VMEM((NCHIP, FC, D), BF)] * 3,
        compiler_params=_params(52),
    )(x1, g_ffn, wg, wu, wd)


def _ffn_bwd(x1, dx2, gg, uu, g_ffn, wg, wu, wd, carry=None):
    L = x1.shape[0]

    def body(x_ref, dx2_ref, gg_ref, uu_ref, g_ref, wg_hbm, wu_hbm, wd_hbm,
             dx1_ref, act_ref, dgg_ref, duu_ref, dg_ref, wg_vm, wu_vm, wd_vm):
        @pl.when(pl.program_id(0) == 0)
        def _():
            pltpu.sync_copy(wg_hbm, wg_vm)
            pltpu.sync_copy(wu_hbm, wu_vm)
            pltpu.sync_copy(wd_hbm, wd_vm)
            dg_ref[...] = jnp.zeros_like(dg_ref)

        dx2 = dx2_ref[...]
        dx2b = dx2.astype(BF)
        dh2 = jnp.zeros((TM, D), F32)
        for c in range(NCHIP):
            g = gg_ref[c].astype(F32)
            u = uu_ref[c].astype(F32)
            s = _sig(g)
            silu = g * s
            act_ref[c] = (silu * u).astype(BF)
            dact = lax.dot_general(dx2b, wd_vm[c], (((1,), (1,)), ((), ())), preferred_element_type=F32)
            dg = (dact * u * s * (1.0 + g * (1.0 - s))).astype(BF)
            du = (dact * silu).astype(BF)
            dgg_ref[c] = dg
            duu_ref[c] = du
            dh2 = dh2 + jnp.dot(dg, wg_vm[c], preferred_element_type=F32)
            dh2 = dh2 + jnp.dot(du, wu_vm[c], preferred_element_type=F32)
        xh, r = _rms(x_ref[...])
        dg_ref[0:1, :] += _colsum(dh2 * xh)
        dx1_ref[...] = dx2 + _rms_bwd(dh2, xh, r, g_ref[...])

    return _pallas_call(
        body, carry, name="ffn_bwd", grid=(L // TM,),
        in_specs=[_tok(D), _tok(D), _chunk_tok(FC), _chunk_tok(FC), _full((1, D)), ANY, ANY, ANY],
        out_specs=[_tok(D), _chunk_tok(FC), _chunk_tok(FC), _chunk_tok(FC), _full((SUB, D))],
        out_shape=[_sds((L, D)), _sds((NCHIP, L, FC), BF), _sds((NCHIP, L, FC), BF), _sds((NCHIP, L, FC), BF),
                   _sds((SUB, D))],
        scratch_shapes=[pltpu.VMEM((NCHIP, FC, D), BF)] * 3,
        compiler_params=_params(56),
    )(x1, dx2, gg, uu, g_ffn, wg, wu, wd)


def _ple_loss(x2, p, tgt, g_pg, w_pg, b_pg, w_ple, g_ple, g_final):
    L = x2.shape[0]

    def body(x2_ref, p_ref, t_ref, gpg_ref, wpg_ref, bpg_ref, wple_ref, gple_ref, gf_ref,
             dx2_ref, n2_ref, dpre_ref, de0_ref, acc_ref):
        @pl.when(pl.program_id(0) == 0)
        def _():
            acc_ref[...] = jnp.zeros_like(acc_ref)

        x2 = x2_ref[...]
        x2h, r2 = _rms(x2)
        n2 = (x2h * gpg_ref[...]).astype(BF)
        n2_ref[...] = n2
        gate = _sig(jnp.dot(n2, wpg_ref[...], preferred_element_type=F32) + bpg_ref[...])
        pb = p_ref[...].astype(BF)
        e0 = jnp.concatenate([jnp.dot(pb, wple_ref[k], preferred_element_type=F32) for k in range(NCHIP)], axis=1)
        e0h, re = _rms(e0)
        e = e0h * gple_ref[...]
        x3 = x2 + gate * e
        x3h, r3 = _rms(x3)
        diff = x3h * gf_ref[...] - t_ref[...]
        acc_ref[4:5, :] += _colsum(diff * diff) * (0.5 / D)
        dy = diff * (1.0 / D)
        acc_ref[3:4, :] += _colsum(dy * x3h)
        dx3 = _rms_bwd(dy, x3h, r3, gf_ref[...])
        de = dx3 * gate
        acc_ref[2:3, :] += _colsum(de * e0h)
        de0_ref[...] = _rms_bwd(de, e0h, re, gple_ref[...]).astype(BF)
        dpre = dx3 * e * gate * (1.0 - gate)
        acc_ref[1:2, :] += _colsum(dpre)
        dpreb = dpre.astype(BF)
        dpre_ref[...] = dpreb
        dn2 = lax.dot_general(dpreb, wpg_ref[...], (((1,), (1,)), ((), ())), preferred_element_type=F32)
        acc_ref[0:1, :] += _colsum(dn2 * x2h)
        dx2_ref[...] = dx3 + _rms_bwd(dn2, x2h, r2, gpg_ref[...])

    return _pallas_call(
        body, name="ple_loss", grid=(L // TM,),
        in_specs=[_tok(D), _tok(PLE), _tok(D), _full((1, D)), _full((D, D)), _full((1, D)), _full((NCHIP, PLE, AC)),
                  _full((1, D)), _full((1, D))],
        out_specs=[_tok(D), _tok(D), _tok(D), _tok(D), _full((SUB, D))],
        out_shape=[_sds((L, D)), _sds((L, D), BF), _sds((L, D), BF), _sds((L, D), BF), _sds((SUB, D))],
        compiler_params=_params(40),
    )(x2, p, tgt, g_pg, w_pg, b_pg, w_ple, g_ple, g_final)


def _tn(name, a, b, col_chunk=None):
    L = a.shape[-2]
    m, n = a.shape[-1], b.shape[-1]
    if a.ndim == 3 or b.ndim == 3:
        nj, bn = (a if a.ndim == 3 else b).shape[0], n
        a_spec = (pl.BlockSpec((None, TK, m), lambda j, t: (j, t, 0)) if a.ndim == 3
                  else pl.BlockSpec((TK, m), lambda j, t: (t, 0)))
        b_spec = (pl.BlockSpec((None, TK, n), lambda j, t: (j, t, 0)) if b.ndim == 3
                  else pl.BlockSpec((TK, n), lambda j, t: (t, 0)))
        out_spec, out_shape = pl.BlockSpec((None, m, n), lambda j, t: (j, 0, 0)), _sds((nj, m, n))
    else:
        bn = col_chunk
        if bn is None:
            bn = next((cand for cand in (1024, 512) if n > cand and n % cand == 0), n)
        nj = n // bn
        a_spec = pl.BlockSpec((TK, m), lambda j, t: (t, 0))
        b_spec = pl.BlockSpec((TK, bn), lambda j, t: (t, j))
        if col_chunk is None:
            out_spec, out_shape = pl.BlockSpec((m, bn), lambda j, t: (0, j)), _sds((m, n))
        else:
            out_spec, out_shape = pl.BlockSpec((None, m, bn), lambda j, t: (j, 0, 0)), _sds((nj, m, bn))

    def body(a_ref, b_ref, o_ref):
        @pl.when(pl.program_id(1) == 0)
        def _():
            o_ref[...] = jnp.zeros_like(o_ref)

        o_ref[...] += _mm_tn(a_ref[...], b_ref[...])

    return _pallas_call(
        body, name=name, grid=(nj, L // TK), in_specs=[a_spec, b_spec], out_specs=out_spec, out_shape=out_shape,
        compiler_params=pltpu.CompilerParams(dimension_semantics=("arbitrary", "arbitrary"),
                                             vmem_limit_bytes=40 * VMEM_MB),
    )(a, b)


LANE = 128


def _tn_blocks(name, a, b, ga, gb):
    L, n = a.shape[0], b.shape[1]
    per = LANE // ga
    wb = per * gb

    def body(a_ref, b_ref, o_ref, acc_ref):
        t = pl.program_id(1)

        @pl.when(t == 0)
        def _():
            acc_ref[...] = jnp.zeros_like(acc_ref)

        acc_ref[...] += _mm_tn(a_ref[...], b_ref[...])

        @pl.when(t == L // TK - 1)
        def _():
            rows = lax.broadcasted_iota(jnp.int32, (LANE, wb), 0) // ga
            cols = lax.broadcasted_iota(jnp.int32, (LANE, wb), 1) // gb
            kept = jnp.where(rows == cols, acc_ref[...], 0.0)
            o_ref[...] = jnp.sum(kept.reshape(per, ga, wb), axis=0)

    return _pallas_call(
        body, name=name, grid=(n // wb, L // TK),
        in_specs=[pl.BlockSpec((TK, LANE), lambda j, t: (t, j)), pl.BlockSpec((TK, wb), lambda j, t: (t, j))],
        out_specs=pl.BlockSpec((ga, wb), lambda j, t: (0, j)), out_shape=_sds((ga, n)),
        scratch_shapes=[pltpu.VMEM((LANE, wb), F32)],
        compiler_params=pltpu.CompilerParams(dimension_semantics=("arbitrary", "arbitrary"),
                                             vmem_limit_bytes=32 * VMEM_MB),
    )(a, b)


def _s5_discretize(lam_re, lam_im, log_dt, b_re, b_im):
    dt = jnp.exp(log_dt)[:, None]
    mag = jnp.exp(lam_re * dt)
    ar = mag * jnp.cos(lam_im * dt)
    ai = mag * jnp.sin(lam_im * dt)
    den = lam_re * lam_re + lam_im * lam_im
    nr = ar - 1.0
    fr = (nr * lam_re + ai * lam_im) / den
    fi = (ai * lam_re - nr * lam_im) / den
    bbr = fr[:, None, :] * b_re - fi[:, None, :] * b_im
    bbi = fr[:, None, :] * b_im + fi[:, None, :] * b_re
    return ar, ai, bbr, bbi


def _prepare(by_rows, block_cols, ar, ai):
    n = len(by_rows)

    def body(*refs):
        srcs, (ar_ref, ai_ref), dense, (con_ref, rev_ref) = refs[:n], refs[n:n + 2], refs[n + 2:2 * n + 2], refs[2 * n + 2:]
        for src, out, c in zip(srcs, dense, block_cols):
            r, width = src.shape
            groups = width // c
            tiled = jnp.broadcast_to(src[...][None], (groups, r, width)).reshape(groups * r, width)
            own = (lax.broadcasted_iota(jnp.int32, tiled.shape, 0) // r) == (lax.broadcasted_iota(jnp.int32, tiled.shape, 1) // c)
            out[...] = jnp.where(own, tiled, 0.0).astype(BF)
        a_r, a_i = ar_ref[...], ai_ref[...]
        pw = [(jnp.ones_like(a_r), jnp.zeros_like(a_i))]
        for _ in range(SUB):
            pr, pi = pw[-1]
            pw.append((pr * a_r - pi * a_i, pr * a_i + pi * a_r))
        row = _row_iota(GN)
        for ref, reverse in ((con_ref, False), (rev_ref, True)):
            sign = -1.0 if reverse else 1.0
            for j, sh in enumerate((1, 2, 4)):
                keep = (row < SUB - sh) if reverse else (row >= sh)
                ref[2 * j * SUB:(2 * j + 1) * SUB, :] = jnp.where(keep, pw[sh][0], 0.0)
                ref[(2 * j + 1) * SUB:(2 * j + 2) * SUB, :] = jnp.where(keep, sign * pw[sh][1], 0.0)
            p_r, p_i = jnp.zeros((SUB, GN), F32), jnp.zeros((SUB, GN), F32)
            for i in range(SUB):
                k = SUB - i if reverse else i + 1
                p_r = jnp.where(row == i, pw[k][0], p_r)
                p_i = jnp.where(row == i, sign * pw[k][1], p_i)
            ref[6 * SUB:7 * SUB, :] = p_r
            ref[7 * SUB:8 * SUB, :] = p_i

    dense_shapes = [(b.shape[1] // c * b.shape[0], b.shape[1]) for b, c in zip(by_rows, block_cols)]
    outs = _pallas_call(
        body, name="prepare", grid=(1,), in_specs=[_full(b.shape) for b in by_rows] + [_full((1, GN))] * 2,
        out_specs=[_full(s) for s in dense_shapes] + [_full((8 * SUB, GN))] * 2,
        out_shape=[_sds(s, BF) for s in dense_shapes] + [_sds((8 * SUB, GN))] * 2,
        compiler_params=_params(48),
    )(*by_rows, ar, ai)
    return outs[:n], outs[n], outs[n + 1]


def _local_step(x, p, tgt, w, comm):
    rows_of = lambda a: a.reshape(NCHIP * a.shape[1], a.shape[2])
    quarters = lambda a: a.reshape(NCHIP, a.shape[0] // NCHIP, a.shape[1])

    def gathering(names, call):
        carry = comm.gather(names)
        outs = list(call(carry))
        own = len(outs) - len(carry.out_shapes)
        w.update(zip(names, outs[own:]))
        return outs[:own]

    def reducing(tag, grads, call):
        parts, carry = comm.reduce_begin(tag, grads)
        outs = list(call(carry))
        own = len(outs) - len(carry.out_shapes)
        comm.reduce_end(tag, grads, parts, outs[own:])
        return outs[:own]

    w.update(comm.first())
    w_glu = rows_of(w["w_glu"])
    ar, ai, bbr, bbi = _s5_discretize(w["lam_re"], w["lam_im"], w["log_dt"], w["s5_b_re"], w["s5_b_im"])
    by_row = lambda b: jnp.transpose(b, (1, 0, 2)).reshape(b.shape[1], -1)
    (bbr_d, bbi_d, ccr_d, cci_d, wr_d, wi_d), con, con_rev = _prepare(
        [by_row(b) for b in (bbr, bbi, w["s5_c_re"], w["s5_c_im"], w["w_r"], w["w_i"])], [NS] * 4 + [HD] * 2,
        ar.reshape(1, GN), ai.reshape(1, GN))
    dsk = w["s5_d"].reshape(1, S5W)
    lam = w["lru_lambda"].reshape(1, LW)
    sp = jax.nn.softplus(-lam)
    b_r, b_i = w["b_r"].reshape(1, LW), w["b_i"].reshape(1, LW)
    row = lambda name: w[name].reshape(1, -1)

    h, ua, ub, gp = gathering(["w_a_out", "w_b_out", "w_o"], lambda carry: _inproj_fwd(
        x, row("g_mix"), w["w_in"], row("b_in"), carry))
    sr, si, y, zg, ya = gathering(["w_ffn_gate"], lambda carry: _s5_fwd(
        ua, bbr_d, bbi_d, ccr_d, cci_d, dsk, con, w_glu, row("b_glu"), carry))
    xc, rg, ig, yb, hp = gathering(["w_ffn_up"], lambda carry: _lru_fwd(
        ub, w["conv_w"], row("conv_b"), wr_d, wi_d, b_r, b_i, sp, carry))
    w_b_out, w_o = rows_of(w["w_b_out"]), rows_of(w["w_o"])
    x1, pa, pb, merged = gathering(["w_ffn_down"], lambda carry: _merge_fwd(
        x, ya, yb, gp, w["w_a_out"], w_b_out, w_o, carry))
    x2, h2, gg, uu = gathering(["w_ple_gate", "w_ple"], lambda carry: _ffn_fwd(
        x1, row("g_ffn"), w["w_ffn_gate"], w["w_ffn_up"], w["w_ffn_down"], carry))
    w_pg = rows_of(w["w_ple_gate"])
    dx2, n2, dpre, de0, acc_p = _ple_loss(x2, p, tgt, row("g_ple_gate"), w_pg, row("b_ple_gate"),
                                          w["w_ple"], row("g_ple"), row("g_final"))
    grads = {"w_ple_gate": quarters(_tn("dw_ple_gate", n2, dpre)), "w_ple": _tn("dw_ple", p, de0, col_chunk=AC)}
    dx1, act, dgg, duu, acc_f = reducing("ple", grads, lambda carry: _ffn_bwd(
        x1, dx2, gg, uu, row("g_ffn"), w["w_ffn_gate"], w["w_ffn_up"], w["w_ffn_down"], carry))
    grads = {"w_ffn_gate": _tn("dw_ffn_gate", dgg, h2)}
    dya, dyb, dgp, dpa, dpb = reducing("ffn_gate", grads, lambda carry: _merge_bwd(
        dx1, gp, pa, pb, w["w_a_out"], w_b_out, w_o, carry))
    grads = {"w_ffn_up": _tn("dw_ffn_up", duu, h2), "w_ffn_down": _tn("dw_ffn_down", act, dx2)}
    dua, dq, dy, lr, li, acc_a, acc_s = reducing("ffn_rest", grads, lambda carry: _s5_bwd(
        dya, y, ua, sr, si, bbr_d, bbi_d, ccr_d, cci_d, dsk, con_rev, w_glu, row("b_glu"), carry))
    grads = {"w_o": quarters(_tn("dw_o", merged, dx1)), "w_a_out": _tn("dw_a_out", ya, dpa, col_chunk=AC),
             "w_b_out": quarters(_tn("dw_b_out", yb, dpb))}
    dub, dpr, dpi, acc_l = reducing("merge", grads, lambda carry: _lru_bwd(
        dyb, xc, rg, ig, hp, ub, w["conv_w"], wr_d, wi_d, sp, -_sig(-lam), carry))
    gx, dz, acc_g, acc_b = _inproj_bwd(x, dx1, dua, dub, dgp, row("g_mix"), w["w_in"])
    grads = {"w_in": _tn("dw_in", h, dz, col_chunk=QC), "w_glu": quarters(_tn("dw_glu", zg, dq))}
    reducing("in", grads, lambda carry: _run_now("exchange_in", carry))
    sums = {"ple": acc_p, "ffn": acc_f, "mix": acc_g, "b_in": acc_b, "lru": acc_l, "s5": acc_s, "s5_a": acc_a}
    blocks = {
        "bb_re": _tn_blocks("d_bbr", ua, lr, NP, NS),
        "bb_im": _tn_blocks("d_bbi", ua, li, NP, NS),
        "cc_re": _tn_blocks("d_ccr", dy, sr, NP, NS),
        "cc_im": _tn_blocks("d_cci", dy, si, NP, NS),
        "w_r": _tn_blocks("dw_r", xc, dpr, HD, HD),
        "w_i": _tn_blocks("dw_i", xc, dpi, HD, HD),
    }
    return gx, sums, blocks


def _replicated_grads(w, sums, blocks):
    grouped = lambda e, groups: jnp.transpose(e.reshape(e.shape[0], groups, -1), (1, 0, 2))
    d_ar, d_ai = sums["s5_a"][0].reshape(NG, NS), sums["s5_a"][1].reshape(NG, NS)
    d_bbr, d_bbi = grouped(blocks["bb_re"], NG), grouped(blocks["bb_im"], NG)
    _, vjp = jax.vjp(_s5_discretize, w["lam_re"], w["lam_im"], w["log_dt"], w["s5_b_re"], w["s5_b_im"])
    g = dict(zip(("lam_re", "lam_im", "log_dt", "s5_b_re", "s5_b_im"), vjp((d_ar, d_ai, d_bbr, d_bbi))))
    g["s5_c_re"] = grouped(blocks["cc_re"], NG)
    g["s5_c_im"] = -grouped(blocks["cc_im"], NG)
    g["w_r"], g["w_i"] = grouped(blocks["w_r"], NH), grouped(blocks["w_i"], NH)
    g["s5_d"] = sums["s5"][0].reshape(NG, NP)
    g["b_r"] = sums["lru"][1].reshape(NH, HD)
    g["b_i"] = sums["lru"][2].reshape(NH, HD)
    return g


ACC_ROWS = {"g_mix": ("mix", 0), "b_in": ("b_in", 0), "g_ffn": ("ffn", 0), "g_ple_gate": ("ple", 0),
            "b_ple_gate": ("ple", 1), "g_ple": ("ple", 2), "g_final": ("ple", 3), "b_glu": ("s5", 1),
            "lru_lambda": ("lru", 0), "conv_b": ("lru", 3)}
LOSS_ROW = ("ple", 4)
CONV_W_ROWS = ("lru", 4)


SHARDED = [("w_in", (D, QC)), ("w_glu", (S5W // NCHIP, S5W)), ("w_a_out", (S5W, AC)), ("w_b_out", (LW // NCHIP, D)),
           ("w_o", (D // NCHIP, D)), ("w_ffn_gate", (FC, D)), ("w_ffn_up", (FC, D)), ("w_ffn_down", (FC, D)),
           ("w_ple_gate", (D // NCHIP, D)), ("w_ple", (PLE, AC))]
NSH = len(SHARDED)
TRANSPOSED = ("w_ffn_gate", "w_ffn_up", "s5_b_re", "s5_b_im")
CONV_SHARD = (4, LW // NCHIP)


def _mesh_pos():
    return lax.axis_index("x"), lax.axis_index("y"), lax.axis_index("c")


def _other_chips(x, y):
    return [(1 - x, y), (x, 1 - y), (1 - x, 1 - y)]


def _half_rows(c, rows, align):
    return pl.ds(pl.multiple_of(c * (rows // 2), align), rows // 2)


def _run_now(name, carry):
    c_in, c_out = len(carry.operands), len(carry.out_shapes)

    def body(*refs):
        ins, outs, sems = refs[:c_in], refs[c_in:c_in + c_out], refs[c_in + c_out:]
        carry.start(ins, outs, sems)
        carry.finish(ins, outs, sems)

    return pl.pallas_call(body, name=name, in_specs=[ANY] * c_in, out_specs=[ANY] * c_out,
                          out_shape=list(carry.out_shapes), scratch_shapes=list(carry.sems))(*carry.operands)


def _gather_group(shards, split):
    n = len(shards)

    def copies(srcs, outs, sems):
        send_sems, recv_sems = sems
        x, y, c = _mesh_pos()
        k0 = 2 * x + y
        sib = (x, y, 1 - c)
        chips = _other_chips(x, y)

        def remote(src, dst, j, i, to):
            return pltpu.make_async_remote_copy(src_ref=src, dst_ref=dst, send_sem=send_sems.at[j, i],
                                                recv_sem=recv_sems.at[j, i], device_id=to, device_id_type=MESH)

        def rows(ref, i, core, *lead):
            if not split[i]:
                return ref.at[lead] if lead else ref
            return ref.at[(*lead, _half_rows(core, shards[i].shape[0], 16))]

        own = [remote(s, o.at[k0], 6, i, sib) for i, (s, o) in enumerate(zip(srcs, outs))]
        ici, landed, fwd, fwd_landed = [], [], [], []
        for j, chip in enumerate(chips):
            kj = 2 * chip[0] + chip[1]
            pairs = list(enumerate(zip(srcs, outs)))
            ici.append([remote(rows(s, i, c), rows(o, i, c, k0), j, i, (*chip, c)) for i, (s, o) in pairs])
            landed.append([remote(rows(s, i, c), rows(o, i, c, kj), j, i, (*chip, c)) for i, (s, o) in pairs])
            fwd.append([remote(rows(o, i, c, kj), rows(o, i, c, kj), 3 + j, i, sib) for i, (s, o) in pairs if split[i]])
            fwd_landed.append([remote(rows(o, i, 1 - c, kj), rows(o, i, 1 - c, kj), 3 + j, i, sib)
                               for i, (s, o) in pairs if split[i]])
        return own, ici, landed, fwd, fwd_landed

    def start(srcs, outs, sems):
        own, ici, _, _, _ = copies(srcs, outs, sems)
        for cp in own + [cp for per_chip in ici for cp in per_chip]:
            cp.start()

    def finish(srcs, outs, sems):
        own, ici, landed, fwd, fwd_landed = copies(srcs, outs, sems)
        passed = [i for i in range(n) if split[i]]
        for j in range(3):
            for i, cp in enumerate(landed[j]):
                cp.wait_recv()
                if split[i]:
                    fwd[j][passed.index(i)].start()
        for j in range(3):
            for cp in fwd_landed[j]:
                cp.wait_recv()
        for cp in own:
            cp.wait_recv()
        for cp in own + [cp for per_chip in ici + fwd for cp in per_chip]:
            cp.wait_send()

    return _Carried(shards, [_sds((NCHIP,) + s.shape, s.dtype) for s in shards],
                    [pltpu.SemaphoreType.DMA((7, n)), pltpu.SemaphoreType.DMA((7, n))], start, finish)


def _swap_sibling_halves(name, grads):
    n = len(grads)

    def body(*refs):
        srcs, outs, (send_sems, recv_sems) = refs[:n], refs[n:2 * n], refs[2 * n:]
        x, y, c = _mesh_pos()
        cps = [pltpu.make_async_remote_copy(src_ref=s.at[:, _half_rows(1 - c, s.shape[1], 8)], dst_ref=o,
                                            send_sem=send_sems.at[i], recv_sem=recv_sems.at[i], device_id=(x, y, 1 - c),
                                            device_id_type=MESH) for i, (s, o) in enumerate(zip(srcs, outs))]
        for cp in cps:
            cp.start()
        for cp in cps:
            cp.wait()

    return _pallas_call(
        body, name="swap_sibling_halves_" + name, in_specs=[ANY] * n, out_specs=[ANY] * n,
        out_shape=[_sds((NCHIP, g.shape[1] // 2, g.shape[2])) for g in grads],
        scratch_shapes=[pltpu.SemaphoreType.DMA((n,)), pltpu.SemaphoreType.DMA((n,))],
    )(*grads)


def _add_sibling(name, c_idx, g, got):
    hr, cols = got.shape[1:]

    def body(c_ref, g_ref, got_ref, p_ref, pb_ref):
        s = g_ref[...] + got_ref[...]
        p_ref[...] = s
        pb_ref[...] = s.astype(BF)

    spec = pl.BlockSpec((None, hr, cols), lambda k, c_ref: (k, 0, 0))
    return _pallas_call(
        body, name="add_sibling_" + name,
        grid_spec=pltpu.PrefetchScalarGridSpec(
            num_scalar_prefetch=1, grid=(NCHIP,),
            in_specs=[pl.BlockSpec((None, hr, cols), lambda k, c_ref: (k, c_ref[0], 0)), spec],
            out_specs=[spec, spec]),
        out_shape=[_sds((NCHIP, hr, cols)), _sds((NCHIP, hr, cols), BF)],
        compiler_params=_params(32),
    )(c_idx, g, got)


def _exchange_group(parts):
    n = len(parts)

    def copies(srcs, outs, sems):
        send_sems, recv_sems = sems
        x, y, c = _mesh_pos()
        return [pltpu.make_async_remote_copy(
            src_ref=s.at[2 * chip[0] + chip[1]], dst_ref=o.at[j], send_sem=send_sems.at[j, i],
            recv_sem=recv_sems.at[j, i], device_id=(*chip, c), device_id_type=MESH)
            for j, chip in enumerate(_other_chips(x, y)) for i, (s, o) in enumerate(zip(srcs, outs))]

    def start(srcs, outs, sems):
        for cp in copies(srcs, outs, sems):
            cp.start()

    def finish(srcs, outs, sems):
        for cp in copies(srcs, outs, sems):
            cp.wait()

    return _Carried(parts, [_sds((3,) + p.shape[1:], BF) for p in parts],
                    [pltpu.SemaphoreType.DMA((3, n)), pltpu.SemaphoreType.DMA((3, n))], start, finish)


def _add_chips(name, kc_idx, p, got):
    hr, cols = got.shape[1:]

    def body(kc_ref, p_ref, got_ref, t_ref):
        t_ref[...] = ((p_ref[...] + got_ref[0].astype(F32)) + got_ref[1].astype(F32)) + got_ref[2].astype(F32)

    return _pallas_call(
        body, name="add_chips_" + name,
        grid_spec=pltpu.PrefetchScalarGridSpec(
            num_scalar_prefetch=1, grid=(1,),
            in_specs=[pl.BlockSpec((None, hr, cols), lambda i, kc_ref: (kc_ref[0], 0, 0)),
                      pl.BlockSpec((3, hr, cols), lambda i, kc_ref: (0, 0, 0))],
            out_specs=pl.BlockSpec((None, hr, cols), lambda i, kc_ref: (kc_ref[1], 0, 0))),
        out_shape=_sds((2, hr, cols)),
        compiler_params=_params(32),
    )(kc_idx, p, got)


def _join_sibling(name, halves):
    n = len(halves)

    def body(*refs):
        bufs, (send_sems, recv_sems) = refs[n:2 * n], refs[2 * n:]
        x, y, c = _mesh_pos()
        sib = (x, y, 1 - c)
        sends = [pltpu.make_async_remote_copy(src_ref=b.at[c], dst_ref=b.at[c], send_sem=send_sems.at[i],
                                              recv_sem=recv_sems.at[i], device_id=sib, device_id_type=MESH)
                 for i, b in enumerate(bufs)]
        for cp in sends:
            cp.start()
        for i, b in enumerate(bufs):
            pltpu.make_async_remote_copy(src_ref=b.at[c], dst_ref=b.at[1 - c], send_sem=send_sems.at[i],
                                         recv_sem=recv_sems.at[i], device_id=sib, device_id_type=MESH).wait_recv()
        for cp in sends:
            cp.wait_send()

    return _pallas_call(
        body, name="join_sibling_" + name, in_specs=[ANY] * n, out_specs=[ANY] * n,
        out_shape=[_sds(h.shape) for h in halves], input_output_aliases={i: i for i in range(n)},
        scratch_shapes=[pltpu.SemaphoreType.DMA((n,)), pltpu.SemaphoreType.DMA((n,))],
    )(*halves)


def _allreduce_small(arrays, wire):
    n = len(arrays)
    halves = [(a.shape[0], a.shape[1] // 2) for a in arrays]

    def body(*refs):
        srcs, outs = refs[:n], refs[n:2 * n]
        mine_bufs, sib_bufs, chip_bufs, total_bufs = (refs[k * n:(k + 1) * n] for k in range(2, 6))
        send_sems, recv_sems, local_sems = refs[6 * n:]
        x, y, c = _mesh_pos()
        k0 = 2 * x + y
        sib = (x, y, 1 - c)

        def remote(src, dst, j, i, to):
            return pltpu.make_async_remote_copy(src_ref=src, dst_ref=dst, send_sem=send_sems.at[j, i],
                                                recv_sem=recv_sems.at[j, i], device_id=to, device_id_type=MESH)

        def cols(ref, i, core):
            return ref.at[:, pl.ds(pl.multiple_of(core * halves[i][1], LANE), halves[i][1])]

        swaps = [remote(cols(s, i, 1 - c), b, 0, i, sib) for i, (s, b) in enumerate(zip(srcs, sib_bufs))]
        own = [pltpu.make_async_copy(cols(s, i, c), m, local_sems.at[i]) for i, (s, m) in enumerate(zip(srcs, mine_bufs))]
        for cp in swaps + own:
            cp.start()
        for cp in swaps + own:
            cp.wait()
        for m, b, buf in zip(mine_bufs, sib_bufs, chip_bufs):
            buf[k0] = (m[...] + b[...]).astype(buf.dtype)
        chips = _other_chips(x, y)
        sends = [remote(buf.at[k0], buf.at[k0], 1 + j, i, (*chip, c))
                 for j, chip in enumerate(chips) for i, buf in enumerate(chip_bufs)]
        for cp in sends:
            cp.start()
        for j, chip in enumerate(chips):
            for i, buf in enumerate(chip_bufs):
                remote(buf.at[k0], buf.at[2 * chip[0] + chip[1]], 1 + j, i, (*chip, c)).wait_recv()
        for cp in sends:
            cp.wait_send()
        for t, buf in zip(total_bufs, chip_bufs):
            t[...] = ((buf[0].astype(F32) + buf[1].astype(F32)) + buf[2].astype(F32)) + buf[3].astype(F32)
        joins = [remote(t, cols(o, i, c), 4, i, sib) for i, (t, o) in enumerate(zip(total_bufs, outs))]
        keep = [pltpu.make_async_copy(t, cols(o, i, c), local_sems.at[i]) for i, (t, o) in enumerate(zip(total_bufs, outs))]
        for cp in joins + keep:
            cp.start()
        for i, (t, o) in enumerate(zip(total_bufs, outs)):
            remote(t, cols(o, i, 1 - c), 4, i, sib).wait_recv()
        for cp in joins:
            cp.wait_send()
        for cp in keep:
            cp.wait()

    specs = [_full(a.shape) for a in arrays]
    return _pallas_call(
        body, name="allreduce_small", grid=(1,), in_specs=specs, out_specs=specs,
        out_shape=[_sds(a.shape) for a in arrays],
        scratch_shapes=([pltpu.VMEM(h, F32) for h in halves] + [pltpu.VMEM(h, F32) for h in halves]
                        + [pltpu.VMEM((NCHIP,) + h, dt) for h, dt in zip(halves, wire)] + [pltpu.VMEM(h, F32) for h in halves]
                        + [pltpu.SemaphoreType.DMA((5, n)), pltpu.SemaphoreType.DMA((5, n)), pltpu.SemaphoreType.DMA((n,))]),
        compiler_params=_params(32),
    )(*arrays)


def _adamw_terms(w, g, m, v):
    m = ADAM_B1 * m + (1.0 - ADAM_B1) * g
    v = ADAM_B2 * v + (1.0 - ADAM_B2) * jnp.square(g)
    m_hat = m / (1.0 - ADAM_B1 ** ADAM_STEP)
    v_hat = v / (1.0 - ADAM_B2 ** ADAM_STEP)
    return -ADAM_LR * (m_hat / (jnp.sqrt(v_hat) + ADAM_EPS) + ADAM_WD * w), m, v


def _adamw(name, w, g, m, v):
    r, c = w.shape
    rows = max(b for b in range(SUB, r + 1, SUB) if r % b == 0 and b * c * 4 <= 3 * VMEM_MB // 2)

    def body(w_ref, g_ref, m_ref, v_ref, d_ref, nm_ref, nv_ref):
        d_ref[...], nm_ref[...], nv_ref[...] = _adamw_terms(w_ref[...], g_ref[...], m_ref[...], v_ref[...])

    spec = pl.BlockSpec((rows, c), lambda i: (i, 0))
    return _pallas_call(
        body, name=name, grid=(r // rows,), in_specs=[spec] * 4, out_specs=[spec] * 3,
        out_shape=[_sds((r, c))] * 3, compiler_params=_params(40),
    )(w, g, m, v)


def _adamw_replicated(sums, row_of, direct):
    ns, nr, nd = len(sums), len(row_of), len(direct)

    def body(*refs):
        sum_refs = refs[:ns]
        ins = refs[ns:ns + 3 * nr + 4 * nd]
        outs = refs[ns + 3 * nr + 4 * nd:]
        for i, (_, _, _, si, row) in enumerate(row_of):
            w_ref, m_ref, v_ref = ins[3 * i:3 * i + 3]
            g = sum_refs[si][row:row + 1, :]
            outs[4 * i][...] = g
            outs[4 * i + 1][...], outs[4 * i + 2][...], outs[4 * i + 3][...] = _adamw_terms(w_ref[...], g, m_ref[...], v_ref[...])
        for i in range(nd):
            w_ref, m_ref, v_ref, g_ref = ins[3 * nr + 4 * i:3 * nr + 4 * i + 4]
            o = outs[4 * (nr + i):4 * (nr + i) + 4]
            g = g_ref[...]
            o[0][...] = g
            o[1][...], o[2][...], o[3][...] = _adamw_terms(w_ref[...], g, m_ref[...], v_ref[...])

    operands = list(sums)
    shapes = []
    for w, m, v, _, _ in row_of:
        operands += [w, m, v]
        shapes += [w.shape] * 4
    for w, m, v, g in direct:
        operands += [w, m, v, g]
        shapes += [w.shape] * 4
    flat = _pallas_call(
        body, name="adamw_replicated", grid=(1,), in_specs=[_full(a.shape) for a in operands],
        out_specs=[_full(s) for s in shapes], out_shape=[_sds(s) for s in shapes],
        compiler_params=_params(56),
    )(*operands)
    return [flat[4 * i:4 * i + 4] for i in range(nr + nd)]


class _Exchanges:
    def __init__(self, shards, conv_w, chip, core, apply):
        self.shards, self.conv_w, self.apply = shards, conv_w, apply
        self.core_idx = jnp.reshape(core, (1,)).astype(jnp.int32)
        self.chip_core_idx = jnp.stack([chip, core]).astype(jnp.int32)

    def first(self):
        names = ["w_in", "w_glu"]
        got = _run_now("gather_first", _gather_group([self.shards[n] for n in names] + [self.conv_w],
                                                     [True, True, False]))
        out = dict(zip(names, got))
        out["conv_w"] = jnp.transpose(got[2], (1, 0, 2)).reshape(4, LW)
        return out

    def gather(self, names):
        return _gather_group([self.shards[n] for n in names], [True] * len(names))

    def reduce_begin(self, tag, grads):
        names = list(grads)
        arrived = _swap_sibling_halves(tag, [grads[n] for n in names])
        parts = [_add_sibling(n, self.core_idx, grads[n], rx) for n, rx in zip(names, arrived)]
        return parts, _exchange_group([bf for _, bf in parts])

    def reduce_end(self, tag, grads, parts, arrived):
        names = list(grads)
        halves = [_add_chips(n, self.chip_core_idx, f32, rx) for n, (f32, _), rx in zip(names, parts, arrived)]
        for n, both in zip(names, _join_sibling(tag, halves)):
            self.apply(n, both.reshape(dict(SHARDED)[n]))


INPUT_NAMES = (["x", "p"] + [n for n in
               ["g_mix", "w_in", "b_in", "lam_re", "lam_im", "log_dt", "s5_b_re", "s5_b_im", "s5_c_re", "s5_c_im", "s5_d",
                "w_glu", "b_glu", "conv_w", "conv_b", "w_r", "b_r", "w_i", "b_i", "lru_lambda", "w_a_out", "w_b_out", "w_o",
                "g_ffn", "w_ffn_gate", "w_ffn_up", "w_ffn_down", "g_ple_gate", "w_ple_gate", "b_ple_gate", "w_ple", "g_ple",
                "g_final"]])
WEIGHT_NAMES = INPUT_NAMES[2:]


def kernel(*args):
    names = INPUT_NAMES + ["loss_target"] + ["m_" + n for n in WEIGHT_NAMES] + ["v_" + n for n in WEIGHT_NAMES]
    assert len(args) == len(names)
    given = dict(zip(names, args))

    def view(name):
        a = given[name]
        return jnp.swapaxes(a, -1, -2) if name.endswith(TRANSPOSED) else a

    def unview(name, a):
        return jnp.swapaxes(a, -1, -2) if name in TRANSPOSED else a

    def local(name):
        return view(name) if name.endswith("g_final") else view(name)[0]

    xi, yi, ci = _mesh_pos()
    k0 = 2 * xi + yi
    x, p, tgt = given["x"][0], given["p"][0, 0], given["loss_target"][0]

    results = {}

    def apply(n, total):
        delta, new_m, new_v = _adamw("adamw_" + n, local(n), total, local("m_" + n), local("v_" + n))
        for kind, arr in zip(("grad", "delta", "new_m", "new_v"), (total, delta, new_m, new_v)):
            results[kind, n] = unview(n, arr[None])

    comm = _Exchanges({n: local(n).astype(BF) for n, _ in SHARDED}, local("conv_w"), k0, ci, apply)
    w = {n: local(n) for n in WEIGHT_NAMES if n != "conv_w" and n not in dict(SHARDED)}
    gx, sums, blocks = _local_step(x, p, tgt, w, comm)

    sum_names, block_names = list(sums), list(blocks)
    red = _allreduce_small([sums[n] for n in sum_names] + [blocks[n] for n in block_names],
                           [F32] * len(sum_names) + [BF] * len(block_names))
    sums = dict(zip(sum_names, red[:len(sum_names)]))
    blocks = dict(zip(block_names, red[len(sum_names):]))
    loss = jnp.sum(sums[LOSS_ROW[0]][LOSS_ROW[1]])
    direct_g = _replicated_grads(w, sums, blocks)
    conv_rows = sums[CONV_W_ROWS[0]][CONV_W_ROWS[1]:CONV_W_ROWS[1] + 4]
    direct_g["conv_w"] = lax.dynamic_slice(conv_rows, (0, k0 * CONV_SHARD[1]), CONV_SHARD)
    as_row = lambda a: a.reshape(1, -1)
    row_names = list(ACC_ROWS)
    row_of = [(as_row(given[n]), as_row(given["m_" + n]), as_row(given["v_" + n]),
               sum_names.index(ACC_ROWS[n][0]), ACC_ROWS[n][1]) for n in row_names]
    direct_names = list(direct_g)
    direct = [(view(n), view("m_" + n), view("v_" + n), direct_g[n].reshape(view(n).shape)) for n in direct_names]
    done = _adamw_replicated([sums[n] for n in sum_names], row_of, direct)
    for n, four in zip(row_names + direct_names, done):
        for kind, arr in zip(("grad", "delta", "new_m", "new_v"), four):
            results[kind, n] = unview(n, arr).reshape(given[n].shape)

    out = [loss, gx[None]]
    for kind in ("grad", "delta", "new_m", "new_v"):
        out += [results[kind, n] for n in WEIGHT_NAMES]
    return tuple(out)
```

```python
import functools
import math

import jax
import jax.numpy as jnp
from jax import lax
from jax.experimental import pallas as pl
from jax.experimental.pallas import tpu as pltpu

F32 = jnp.float32
BF = jnp.bfloat16

D = 1024
S5W = 512
NG, NS, NP = 32, 64, 16
GN = NG * NS
LW = 1024
NH, HD = 16, 64
LRU_C = 8.0
FH = 2816
NCHIP = 4
FC = FH // NCHIP
PLE = 256
INC = S5W + LW + 2 * D
EPS = 1e-6
ADAM_LR, ADAM_B1, ADAM_B2, ADAM_EPS, ADAM_WD, ADAM_STEP = 0.001, 0.9, 0.999, 1e-08, 0.01, 10

TM = 256
TK = 512
LC = 512
SUB = 8
VMEM_MB = 1024 * 1024
MESH = pl.DeviceIdType.MESH
ANY = pl.BlockSpec(memory_space=pl.ANY)


def _mm(a, b):
    return jnp.dot(a.astype(BF), b.astype(BF), preferred_element_type=F32)


def _mm_nt(a, b):
    return lax.dot_general(a.astype(BF), b.astype(BF), (((1,), (1,)), ((), ())), preferred_element_type=F32)


def _mm_tn(a, b):
    return lax.dot_general(a.astype(BF), b.astype(BF), (((0,), (0,)), ((), ())), preferred_element_type=F32)


def _rms(x):
    r = lax.rsqrt(jnp.mean(x * x, axis=-1, keepdims=True) + EPS)
    return x * r, r


def _rms_bwd(dy, xh, r, g):
    dxh = dy * g
    return r * (dxh - xh * jnp.mean(dxh * xh, axis=-1, keepdims=True))


def _colsum(x):
    return jnp.sum(x, axis=0, keepdims=True)


def _sig(x):
    return jax.nn.sigmoid(x)


def _gelu_grad(x):
    c = math.sqrt(2.0 / math.pi)
    t = jnp.tanh(c * (x + 0.044715 * x * x * x))
    return 0.5 * (1.0 + t) + 0.5 * x * (1.0 - t * t) * c * (1.0 + 3.0 * 0.044715 * x * x)


def _neg_expm1(x):
    series = -x * (1.0 + x * (0.5 + x * (1.0 / 6.0 + x * (1.0 / 24.0))))
    return jnp.where(x > -0.03, series, 1.0 - jnp.exp(x))


def _tok(width):
    return pl.BlockSpec((TM, width), lambda i: (i, 0))


def _tok_rev(width, nt):
    return pl.BlockSpec((TM, width), lambda i: (nt - 1 - i, 0))


def _full(shape):
    return pl.BlockSpec(shape, lambda i: (0,) * len(shape))


def _params(vmem_mb, **kw):
    return pltpu.CompilerParams(dimension_semantics=("arbitrary",), vmem_limit_bytes=vmem_mb * VMEM_MB, **kw)


def _sds(shape, dtype=F32):
    return jax.ShapeDtypeStruct(shape, dtype)


class _Carried:
    def __init__(self, operands, out_shapes, sems, start, finish):
        self.operands, self.out_shapes, self.sems, self.start, self.finish = operands, out_shapes, sems, start, finish


def _in_hbm(arrays):
    return [pltpu.with_memory_space_constraint(a, pltpu.HBM) for a in arrays]


def _pallas_call(body, carry=None, **kw):
    if carry is None:
        return pl.pallas_call(body, **kw)
    name, grid, compiler_params = kw["name"], kw["grid"], kw["compiler_params"]
    in_specs, out_specs, out_shape = list(kw["in_specs"]), list(kw["out_specs"]), list(kw["out_shape"])
    scratch_shapes = list(kw.get("scratch_shapes", ()))
    n_in, n_out, n_scr = len(in_specs), len(out_specs), len(scratch_shapes)
    c_in, c_out = len(carry.operands), len(carry.out_shapes)

    def full_body(*refs):
        ins, refs = refs[:n_in], refs[n_in:]
        c_ins, refs = refs[:c_in], refs[c_in:]
        outs, refs = refs[:n_out], refs[n_out:]
        c_outs, refs = refs[:c_out], refs[c_out:]
        scratch, c_sems = refs[:n_scr], refs[n_scr:]

        @pl.when(pl.program_id(0) == 0)
        def _():
            carry.start(c_ins, c_outs, c_sems)

        body(*ins, *outs, *scratch)

        @pl.when(pl.program_id(0) == grid[0] - 1)
        def _():
            carry.finish(c_ins, c_outs, c_sems)

    call = pl.pallas_call(
        full_body, name=name, grid=grid, in_specs=in_specs + [ANY] * c_in, out_specs=out_specs + [ANY] * c_out,
        out_shape=out_shape + list(carry.out_shapes), scratch_shapes=scratch_shapes + list(carry.sems),
        compiler_params=compiler_params)
    return lambda *operands: call(*operands, *_in_hbm(carry.operands))


def _row_iota(width):
    return lax.broadcasted_iota(jnp.int32, (SUB, width), 0)


def _bcast_row(x, row):
    return jnp.broadcast_to(x[row:row + 1, :], x.shape)


def _slab(k):
    return pl.ds(pl.multiple_of(k * SUB, SUB), SUB)


QC = INC // NCHIP
Z_PARTS = ((0, S5W), (S5W, S5W + LW), (S5W + LW, INC))


def _inproj_fwd(x, g_mix, w_in, b_in, carry=None):
    L = x.shape[0]

    def body(x_ref, g_ref, w_hbm, b_ref, h_ref, ua_ref, ub_ref, gp_ref, w_vm):
        @pl.when(pl.program_id(0) == 0)
        def _():
            pltpu.sync_copy(w_hbm, w_vm)

        xh, _ = _rms(x_ref[...])
        h = (xh * g_ref[...]).astype(BF)
        h_ref[...] = h
        for k in range(NCHIP):
            lo, hi = k * QC, (k + 1) * QC
            z = jnp.dot(h, w_vm[k], preferred_element_type=F32) + b_ref[:, lo:hi]
            for ref, (a, b) in zip((ua_ref, ub_ref, gp_ref), Z_PARTS):
                s, e = max(lo, a), min(hi, b)
                if s < e:
                    ref[:, s - a:e - a] = z[:, s - lo:e - lo]

    return _pallas_call(
        body, carry, name="inproj_fwd", grid=(L // TM,),
        in_specs=[_tok(D), _full((1, D)), ANY, _full((1, INC))],
        out_specs=[_tok(D), _tok(S5W), _tok(LW), _tok(2 * D)],
        out_shape=[_sds((L, D), BF), _sds((L, S5W)), _sds((L, LW)), _sds((L, 2 * D))],
        scratch_shapes=[pltpu.VMEM((NCHIP, D, QC), BF)],
        compiler_params=_params(40),
    )(x, g_mix, w_in, b_in)


def _inproj_bwd(x, dx1, dua, dub, dgp, g_mix, w_in):
    L = x.shape[0]

    def body(x_ref, dx1_ref, dua_ref, dub_ref, dgp_ref, g_ref, w_hbm, gx_ref, dz_ref, dg_ref, db_ref, w_vm):
        @pl.when(pl.program_id(0) == 0)
        def _():
            pltpu.sync_copy(w_hbm, w_vm)
            dg_ref[...] = jnp.zeros_like(dg_ref)
            db_ref[...] = jnp.zeros_like(db_ref)

        for src, (a, b) in zip((dua_ref, dub_ref, dgp_ref), Z_PARTS):
            d = src[...]
            dz_ref[:, a:b] = d.astype(BF)
            db_ref[0:1, a:b] += _colsum(d)
        dh = jnp.zeros((TM, D), F32)
        for k in range(NCHIP):
            dh = dh + lax.dot_general(dz_ref[:, k * QC:(k + 1) * QC], w_vm[k], (((1,), (1,)), ((), ())),
                                      preferred_element_type=F32)
        xh, r = _rms(x_ref[...])
        dg_ref[0:1, :] += _colsum(dh * xh)
        gx_ref[...] = dx1_ref[...] + _rms_bwd(dh, xh, r, g_ref[...])

    return _pallas_call(
        body, name="inproj_bwd", grid=(L // TM,),
        in_specs=[_tok(D), _tok(D), _tok(S5W), _tok(LW), _tok(2 * D), _full((1, D)), ANY],
        out_specs=[_tok(D), _tok(INC), _full((SUB, D)), _full((SUB, INC))],
        out_shape=[_sds((L, D)), _sds((L, INC), BF), _sds((SUB, D)), _sds((SUB, INC))],
        scratch_shapes=[pltpu.VMEM((NCHIP, D, QC), BF)],
        compiler_params=_params(40),
    )(x, dx1, dua, dub, dgp, g_mix, w_in)


def _cscan(xr_ref, xi_ref, con_ref, cr_ref, ci_ref, reverse):
    n_slab = xr_ref.shape[0] // SUB
    width = xr_ref.shape[1]
    for lc in range(width // LC):
        cols = slice(lc * LC, (lc + 1) * LC)
        con = [con_ref[SUB * j:SUB * (j + 1), cols] for j in range(8)]

        def step(k, carry, cols=cols, con=con):
            cr, ci = carry
            rows = _slab(n_slab - 1 - k if reverse else k)
            xr, xi = xr_ref[rows, cols], xi_ref[rows, cols]
            for j, sh in enumerate((1, 2, 4)):
                mr, mi = con[2 * j], con[2 * j + 1]
                pr = pltpu.roll(xr, SUB - sh if reverse else sh, 0)
                pi = pltpu.roll(xi, SUB - sh if reverse else sh, 0)
                xr, xi = xr + mr * pr - mi * pi, xi + mr * pi + mi * pr
            xr, xi = xr + con[6] * cr - con[7] * ci, xi + con[6] * ci + con[7] * cr
            xr_ref[rows, cols] = xr
            xi_ref[rows, cols] = xi
            row = 0 if reverse else SUB - 1
            return _bcast_row(xr, row), _bcast_row(xi, row)

        cr, ci = lax.fori_loop(0, n_slab, step, (cr_ref[:, cols], ci_ref[:, cols]))
        cr_ref[:, cols] = cr
        ci_ref[:, cols] = ci


def _s5_fwd(ua, bbr, bbi, ccr, cci, dsk, con, w_glu, b_glu, carry=None):
    L = ua.shape[0]

    def body(ua_ref, bbr_hbm, bbi_hbm, ccr_hbm, cci_hbm, dsk_ref, con_ref, wg_ref, bg_ref,
             sr_ref, si_ref, y_ref, zg_ref, ya_ref, bbr_vm, bbi_vm, ccr_vm, cci_vm, cr_ref, ci_ref):
        @pl.when(pl.program_id(0) == 0)
        def _():
            pltpu.sync_copy(bbr_hbm, bbr_vm)
            pltpu.sync_copy(bbi_hbm, bbi_vm)
            pltpu.sync_copy(ccr_hbm, ccr_vm)
            pltpu.sync_copy(cci_hbm, cci_vm)
            cr_ref[...] = jnp.zeros_like(cr_ref)
            ci_ref[...] = jnp.zeros_like(ci_ref)

        u = ua_ref[...]
        ub = u.astype(BF)
        sr_ref[...] = jnp.dot(ub, bbr_vm[...], preferred_element_type=F32)
        si_ref[...] = jnp.dot(ub, bbi_vm[...], preferred_element_type=F32)
        _cscan(sr_ref, si_ref, con_ref, cr_ref, ci_ref, reverse=False)
        y = _mm_nt(sr_ref[...], ccr_vm[...]) - _mm_nt(si_ref[...], cci_vm[...]) + dsk_ref[...] * u
        y_ref[...] = y
        zg = jax.nn.gelu(y)
        zg_ref[...] = zg.astype(BF)
        q = _mm(zg, wg_ref[...]) + bg_ref[...]
        ya_ref[...] = (zg * _sig(q)).astype(BF)

    return _pallas_call(
        body, carry, name="s5_fwd", grid=(L // TM,),
        in_specs=[_tok(S5W), ANY, ANY, ANY, ANY, _full((1, S5W)), _full((8 * SUB, GN)),
                  _full((S5W, S5W)), _full((1, S5W))],
        out_specs=[_tok(GN), _tok(GN), _tok(S5W), _tok(S5W), _tok(S5W)],
        out_shape=[_sds((L, GN)), _sds((L, GN)), _sds((L, S5W)), _sds((L, S5W), BF), _sds((L, S5W), BF)],
        scratch_shapes=[pltpu.VMEM((S5W, GN), BF), pltpu.VMEM((S5W, GN), BF), pltpu.VMEM((S5W, GN), BF),
                        pltpu.VMEM((S5W, GN), BF),pltpu.VMEM((SUB, GN), F32), pltpu.VMEM((SUB, GN), F32)],
        compiler_params=_params(44),
    )(ua, bbr, bbi, ccr, cci, dsk, con, w_glu, b_glu)


def _s5_bwd(dya, y, ua, sr, si, bbr, bbi, ccr, cci, dsk, con_rev, w_glu, b_glu, carry=None):
    L = ua.shape[0]
    nt = L // TM
    spt = TM // SUB
    n_slab = spt

    def halo_map(i):
        return (jnp.maximum((nt - 1 - i) * spt - 1, 0), 0)

    def body(dya_ref, y_ref, ua_ref, sr_ref, si_ref, hr_ref, hi_ref, bbr_hbm, bbi_hbm, ccr_hbm, cci_hbm,
             dsk_ref, con_ref, wg_ref, bg_ref,
             dua_ref, dq_ref, dy_ref, lr_ref, li_ref, da_ref, dsm_ref,
             bbr_vm, bbi_vm, ccr_vm, cci_vm, cr_ref, ci_ref):
        i = pl.program_id(0)

        @pl.when(i == 0)
        def _():
            pltpu.sync_copy(bbr_hbm, bbr_vm)
            pltpu.sync_copy(bbi_hbm, bbi_vm)
            pltpu.sync_copy(ccr_hbm, ccr_vm)
            pltpu.sync_copy(cci_hbm, cci_vm)
            cr_ref[...] = jnp.zeros_like(cr_ref)
            ci_ref[...] = jnp.zeros_like(ci_ref)
            da_ref[...] = jnp.zeros_like(da_ref)
            dsm_ref[...] = jnp.zeros_like(dsm_ref)

        u = ua_ref[...]
        yv = y_ref[...]
        dya = dya_ref[...]
        zg = jax.nn.gelu(yv)
        sg = _sig(_mm(zg, wg_ref[...]) + bg_ref[...])
        dq = dya * zg * sg * (1.0 - sg)
        dq_ref[...] = dq.astype(BF)
        dzg = dya * sg + _mm_nt(dq, wg_ref[...])
        dy = dzg * _gelu_grad(yv)
        dyb = dy.astype(BF)
        dy_ref[...] = dyb
        dsm_ref[0:1, :] += _colsum(dy * u)
        dsm_ref[1:2, :] += _colsum(dq)
        lr_ref[...] = jnp.dot(dyb, ccr_vm[...], preferred_element_type=F32)
        li_ref[...] = -jnp.dot(dyb, cci_vm[...], preferred_element_type=F32)
        _cscan(lr_ref, li_ref, con_ref, cr_ref, ci_ref, reverse=True)

        first_tile = (i == nt - 1)
        row = _row_iota(LC)
        for lc in range(GN // LC):
            cols = slice(lc * LC, (lc + 1) * LC)
            h_r = jnp.where(first_tile, 0.0, hr_ref[:, cols])
            h_i = jnp.where(first_tile, 0.0, hi_ref[:, cols])

            def step(k, acc, cols=cols, h_r=h_r, h_i=h_i):
                ar, ai = acc
                rows = _slab(k)
                prev = _slab(jnp.maximum(k - 1, 0))
                pr = jnp.where(k == 0, h_r, sr_ref[prev, cols])
                pi = jnp.where(k == 0, h_i, si_ref[prev, cols])
                spr = pltpu.roll(jnp.where(row == SUB - 1, pr, sr_ref[rows, cols]), 1, 0)
                spi = pltpu.roll(jnp.where(row == SUB - 1, pi, si_ref[rows, cols]), 1, 0)
                lr, li = lr_ref[rows, cols], li_ref[rows, cols]
                return ar + lr * spr + li * spi, ai + li * spr - lr * spi

            zero = jnp.zeros((SUB, LC), F32)
            ar, ai = lax.fori_loop(0, n_slab, step, (zero, zero))
            da_ref[0:1, cols] += _colsum(ar)
            da_ref[1:2, cols] += _colsum(ai)

        dua_ref[...] = (dy * dsk_ref[...] + _mm_nt(lr_ref[...], bbr_vm[...]) + _mm_nt(li_ref[...], bbi_vm[...]))

    return _pallas_call(
        body, carry, name="s5_bwd", grid=(nt,),
        in_specs=[_tok_rev(S5W, nt), _tok_rev(S5W, nt), _tok_rev(S5W, nt), _tok_rev(GN, nt), _tok_rev(GN, nt),
                  pl.BlockSpec((SUB, GN), halo_map), pl.BlockSpec((SUB, GN), halo_map),
                  ANY, ANY, ANY, ANY, _full((1, S5W)), _full((8 * SUB, GN)), _full((S5W, S5W)), _full((1, S5W))],
        out_specs=[_tok_rev(S5W, nt), _tok_rev(S5W, nt), _tok_rev(S5W, nt), _tok_rev(GN, nt), _tok_rev(GN, nt),
                   _full((SUB, GN)), _full((SUB, S5W))],
        out_shape=[_sds((L, S5W)), _sds((L, S5W), BF), _sds((L, S5W), BF), _sds((L, GN)), _sds((L, GN)),
                   _sds((SUB, GN)), _sds((SUB, S5W))],
        scratch_shapes=[pltpu.VMEM((S5W, GN), BF), pltpu.VMEM((S5W, GN), BF), pltpu.VMEM((S5W, GN), BF),
                        pltpu.VMEM((S5W, GN), BF),pltpu.VMEM((SUB, GN), F32), pltpu.VMEM((SUB, GN), F32)],
        compiler_params=_params(52),
    )(dya, y, ua, sr, si, sr, si, bbr, bbi, ccr, cci, dsk, con_rev, w_glu, b_glu)


def _lru_gate_terms(rg, sp):
    log_a = -LRU_C * rg * sp
    a = jnp.exp(log_a)
    mult = jnp.sqrt(_neg_expm1(2.0 * log_a))
    return a, mult


def _lru_fwd(ub, conv_w, conv_b, wr, wi, b_r, b_i, sp, carry=None):
    L = ub.shape[0]
    n_slab = TM // SUB

    def body(ub_ref, cw_ref, cb_ref, wr_ref, wi_ref, br_ref, bi_ref, sp_ref,
             xc_ref, rg_ref, ig_ref, h_ref, hp_ref, a_ref, halo_ref, carry_ref):
        @pl.when(pl.program_id(0) == 0)
        def _():
            halo_ref[...] = jnp.zeros_like(halo_ref)
            carry_ref[...] = jnp.zeros_like(carry_ref)

        row = _row_iota(LW)
        taps = [cw_ref[k:k + 1, :] for k in range(4)]
        cb = cb_ref[...]

        def conv_step(k, prev):
            rows = _slab(k)
            cur = ub_ref[rows, :]
            acc = taps[3] * cur + cb
            for j in (1, 2, 3):
                acc = acc + taps[3 - j] * pltpu.roll(jnp.where(row >= SUB - j, prev, cur), j, 0)
            xc_ref[rows, :] = acc
            return cur

        halo_ref[...] = lax.fori_loop(0, n_slab, conv_step, halo_ref[...])

        xc = xc_ref[...]
        xcb = xc.astype(BF)
        rg = _sig(jnp.dot(xcb, wr_ref[...], preferred_element_type=F32) + br_ref[...])
        ig = _sig(jnp.dot(xcb, wi_ref[...], preferred_element_type=F32) + bi_ref[...])
        rg_ref[...] = rg
        ig_ref[...] = ig
        a, mult = _lru_gate_terms(rg, sp_ref[...])
        a_ref[...] = a
        h_ref[...] = mult * ig * xc

        rowc = _row_iota(LC)
        for lc in range(LW // LC):
            cols = slice(lc * LC, (lc + 1) * LC)

            def step(k, c, cols=cols):
                rows = _slab(k)
                av, b = a_ref[rows, cols], h_ref[rows, cols]
                for sh in (1, 2, 4):
                    keep = rowc >= sh
                    b = b + av * jnp.where(keep, pltpu.roll(b, sh, 0), 0.0)
                    av = av * jnp.where(keep, pltpu.roll(av, sh, 0), 1.0)
                h = b + av * c
                h_ref[rows, cols] = h
                hp_ref[rows, cols] = jnp.where(rowc == 0, c, pltpu.roll(h, 1, 0))
                return _bcast_row(h, SUB - 1)

            carry_ref[:, cols] = lax.fori_loop(0, n_slab, step, carry_ref[:, cols])

    return _pallas_call(
        body, carry, name="lru_fwd", grid=(L // TM,),
        in_specs=[_tok(LW), _full((4, LW)), _full((1, LW)), _full((LW, LW)), _full((LW, LW)),
                  _full((1, LW)), _full((1, LW)), _full((1, LW))],
        out_specs=[_tok(LW)] * 5,
        out_shape=[_sds((L, LW))] * 5,
        scratch_shapes=[pltpu.VMEM((TM, LW), F32), pltpu.VMEM((SUB, LW), F32), pltpu.VMEM((SUB, LW), F32)],
        compiler_params=_params(40),
    )(ub, conv_w, conv_b, wr, wi, b_r, b_i, sp)


def _lru_bwd(dyb, xc, rg, ig, hp, ub, conv_w, wr, wi, sp, dsp, carry=None):
    L = ub.shape[0]
    nt = L // TM
    spt = TM // SUB
    n_slab = spt

    def halo_map(i):
        return (jnp.maximum((nt - 1 - i) * spt - 1, 0), 0)

    def body(dh_ref, xc_ref, rg_ref, ig_ref, hp_ref, ub_ref, uh_ref, cw_ref, wr_ref, wi_ref, sp_ref, dsp_ref,
             dub_ref, dpr_ref, dpi_ref, acc_ref, a_ref, lam_ref, dxc_ref, carry_ref, next_ref):
        i = pl.program_id(0)

        @pl.when(i == 0)
        def _():
            carry_ref[...] = jnp.zeros_like(carry_ref)
            next_ref[...] = jnp.zeros_like(next_ref)
            acc_ref[...] = jnp.zeros_like(acc_ref)

        sp = sp_ref[...]
        rg, ig, xc = rg_ref[...], ig_ref[...], xc_ref[...]
        a, mult = _lru_gate_terms(rg, sp)
        a_ref[...] = a

        rowc = _row_iota(LC)
        for lc in range(LW // LC):
            cols = slice(lc * LC, (lc + 1) * LC)

            def step(k, c, cols=cols):
                rows = _slab(n_slab - 1 - k)
                av, dh = a_ref[rows, cols], dh_ref[rows, cols]
                b = av * dh
                for sh in (1, 2, 4):
                    keep = rowc < SUB - sh
                    b = b + av * jnp.where(keep, pltpu.roll(b, SUB - sh, 0), 0.0)
                    av = av * jnp.where(keep, pltpu.roll(av, SUB - sh, 0), 1.0)
                mu = b + av * c
                lam_ref[rows, cols] = dh + jnp.where(rowc == SUB - 1, c, pltpu.roll(mu, SUB - 1, 0))
                return _bcast_row(mu, 0)

            carry_ref[:, cols] = lax.fori_loop(0, n_slab, step, carry_ref[:, cols])

        lam = lam_ref[...]
        d_a = lam * hp_ref[...]
        d_mult = lam * ig * xc
        d_ig = lam * mult * xc
        dxc = lam * mult * ig
        d_log_a = d_a * a - d_mult * a * a / mult
        d_rg = (-LRU_C) * sp * d_log_a
        acc_ref[0:1, :] += _colsum((-LRU_C) * rg * d_log_a) * dsp_ref[...]
        dpr = d_rg * rg * (1.0 - rg)
        dpi = d_ig * ig * (1.0 - ig)
        acc_ref[1:2, :] += _colsum(dpr)
        acc_ref[2:3, :] += _colsum(dpi)
        dprb, dpib = dpr.astype(BF), dpi.astype(BF)
        dpr_ref[...] = dprb
        dpi_ref[...] = dpib
        dxc = dxc + _mm_nt(dprb, wr_ref[...]) + _mm_nt(dpib, wi_ref[...])
        dxc_ref[...] = dxc
        acc_ref[3:4, :] += _colsum(dxc)

        row = _row_iota(LW)
        taps = [cw_ref[k:k + 1, :] for k in range(4)]
        u_halo = jnp.where(i == nt - 1, 0.0, uh_ref[...])
        nxt_tile = next_ref[...]

        def conv_step(k, accs):
            rows = _slab(k)
            cur = dxc_ref[rows, :]
            nxt = jnp.where(k == n_slab - 1, nxt_tile, dxc_ref[_slab(jnp.minimum(k + 1, n_slab - 1)), :])
            ucur = ub_ref[rows, :]
            uprev = jnp.where(k == 0, u_halo, ub_ref[_slab(jnp.maximum(k - 1, 0)), :])
            du = taps[3] * cur
            new = [accs[3] + cur * ucur]
            for j in (1, 2, 3):
                du = du + taps[3 - j] * pltpu.roll(jnp.where(row < j, nxt, cur), SUB - j, 0)
                new.append(accs[3 - j] + cur * pltpu.roll(jnp.where(row >= SUB - j, uprev, ucur), j, 0))
            dub_ref[rows, :] = du
            return tuple(new[::-1])

        zero = jnp.zeros((SUB, LW), F32)
        accs = lax.fori_loop(0, n_slab, conv_step, (zero, zero, zero, zero))
        for k in range(4):
            acc_ref[4 + k:5 + k, :] += _colsum(accs[k])
        next_ref[...] = dxc_ref[0:SUB, :]

    return _pallas_call(
        body, carry, name="lru_bwd", grid=(nt,),
        in_specs=[_tok_rev(LW, nt)] * 6 + [pl.BlockSpec((SUB, LW), halo_map), _full((4, LW)),
                                           _full((LW, LW)), _full((LW, LW)), _full((1, LW)), _full((1, LW))],
        out_specs=[_tok_rev(LW, nt), _tok_rev(LW, nt), _tok_rev(LW, nt), _full((SUB, LW))],
        out_shape=[_sds((L, LW)), _sds((L, LW), BF), _sds((L, LW), BF), _sds((SUB, LW))],
        scratch_shapes=[pltpu.VMEM((TM, LW), F32), pltpu.VMEM((TM, LW), F32), pltpu.VMEM((TM, LW), F32),
                        pltpu.VMEM((SUB, LW), F32), pltpu.VMEM((SUB, LW), F32)],
        compiler_params=_params(48),
    )(dyb, xc, rg, ig, hp, ub, ub, conv_w, wr, wi, sp, dsp)


AC = D // NCHIP


def _merge_fwd(x, ya, yb, gp, w_a, w_b, w_o, carry=None):
    L = x.shape[0]

    def body(x_ref, ya_ref, yb_ref, gp_ref, wa_ref, wb_ref, wo_ref, x1_ref, pa_ref, pb_ref, mg_ref):
        ya = ya_ref[...]
        for k in range(NCHIP):
            pa_ref[:, k * AC:(k + 1) * AC] = jnp.dot(ya, wa_ref[k], preferred_element_type=F32)
        pb = _mm(yb_ref[...], wb_ref[...])
        pb_ref[...] = pb
        gp = gp_ref[...]
        merged = (_sig(gp[:, :D]) * pa_ref[...] + _sig(gp[:, D:]) * pb).astype(BF)
        mg_ref[...] = merged
        x1_ref[...] = x_ref[...] + jnp.dot(merged, wo_ref[...], preferred_element_type=F32)

    return _pallas_call(
        body, carry, name="merge_fwd", grid=(L // TM,),
        in_specs=[_tok(D), _tok(S5W), _tok(LW), _tok(2 * D), _full((NCHIP, S5W, AC)), _full((LW, D)), _full((D, D))],
        out_specs=[_tok(D), _tok(D), _tok(D), _tok(D)],
        out_shape=[_sds((L, D)), _sds((L, D)), _sds((L, D)), _sds((L, D), BF)],
        compiler_params=_params(40),
    )(x, ya, yb, gp, w_a, w_b, w_o)


def _merge_bwd(dx1, gp, pa, pb, w_a, w_b, w_o, carry=None):
    L = dx1.shape[0]

    def body(dx1_ref, gp_ref, pa_ref, pb_ref, wa_ref, wb_ref, wo_ref, dya_ref, dyb_ref, dgp_ref, dpa_ref, dpb_ref):
        dm = _mm_nt(dx1_ref[...], wo_ref[...])
        gp = gp_ref[...]
        sa, sb = _sig(gp[:, :D]), _sig(gp[:, D:])
        dpa = (dm * sa).astype(BF)
        dpb = (dm * sb).astype(BF)
        dpa_ref[...] = dpa
        dpb_ref[...] = dpb
        dgp_ref[:, :D] = dm * pa_ref[...] * sa * (1.0 - sa)
        dgp_ref[:, D:] = dm * pb_ref[...] * sb * (1.0 - sb)
        dya = jnp.zeros((TM, S5W), F32)
        for k in range(NCHIP):
            dya = dya + _mm_nt(dpa[:, k * AC:(k + 1) * AC], wa_ref[k])
        dya_ref[...] = dya
        dyb_ref[...] = _mm_nt(dpb, wb_ref[...])

    return _pallas_call(
        body, carry, name="merge_bwd", grid=(L // TM,),
        in_specs=[_tok(D), _tok(2 * D), _tok(D), _tok(D), _full((NCHIP, S5W, AC)), _full((LW, D)), _full((D, D))],
        out_specs=[_tok(S5W), _tok(LW), _tok(2 * D), _tok(D), _tok(D)],
        out_shape=[_sds((L, S5W)), _sds((L, LW)), _sds((L, 2 * D)), _sds((L, D), BF), _sds((L, D), BF)],
        compiler_params=_params(40),
    )(dx1, gp, pa, pb, w_a, w_b, w_o)


def _chunk_tok(width):
    return pl.BlockSpec((NCHIP, TM, width), lambda i: (0, i, 0))


def _ffn_fwd(x1, g_ffn, wg, wu, wd, carry=None):
    L = x1.shape[0]

    def body(x_ref, g_ref, wg_hbm, wu_hbm, wd_hbm, x2_ref, h2_ref, gg_ref, uu_ref, wg_vm, wu_vm, wd_vm):
        @pl.when(pl.program_id(0) == 0)
        def _():
            pltpu.sync_copy(wg_hbm, wg_vm)
            pltpu.sync_copy(wu_hbm, wu_vm)
            pltpu.sync_copy(wd_hbm, wd_vm)

        x = x_ref[...]
        xh, _ = _rms(x)
        h2 = (xh * g_ref[...]).astype(BF)
        h2_ref[...] = h2
        out = x
        for c in range(NCHIP):
            gg = lax.dot_general(h2, wg_vm[c], (((1,), (1,)), ((), ())), preferred_element_type=F32)
            uu = lax.dot_general(h2, wu_vm[c], (((1,), (1,)), ((), ())), preferred_element_type=F32)
            gg_ref[c] = gg.astype(BF)
            uu_ref[c] = uu.astype(BF)
            act = (gg * _sig(gg) * uu).astype(BF)
            out = out + jnp.dot(act, wd_vm[c], preferred_element_type=F32)
        x2_ref[...] = out

    return _pallas_call(
        body, carry, name="ffn_fwd", grid=(L // TM,),
        in_specs=[_tok(D), _full((1, D)), ANY, ANY, ANY],
        out_specs=[_tok(D), _tok(D), _chunk_tok(FC), _chunk_tok(FC)],
        out_shape=[_sds((L, D)), _sds((L, D), BF), _sds((NCHIP, L, FC), BF), _sds((NCHIP, L, FC), BF)],
        scratch_shapes=[pltpu.VMEM((NCHIP, FC, D), BF)] * 3,
        compiler_params=_params(52),
    )(x1, g_ffn, wg, wu, wd)


def _ffn_bwd(x1, dx2, gg, uu, g_ffn, wg, wu, wd, carry=None):
    L = x1.shape[0]

    def body(x_ref, dx2_ref, gg_ref, uu_ref, g_ref, wg_hbm, wu_hbm, wd_hbm,
             dx1_ref, act_ref, dgg_ref, duu_ref, dg_ref, wg_vm, wu_vm, wd_vm):
        @pl.when(pl.program_id(0) == 0)
        def _():
            pltpu.sync_copy(wg_hbm, wg_vm)
            pltpu.sync_copy(wu_hbm, wu_vm)
            pltpu.sync_copy(wd_hbm, wd_vm)
            dg_ref[...] = jnp.zeros_like(dg_ref)

        dx2 = dx2_ref[...]
        dx2b = dx2.astype(BF)
        dh2 = jnp.zeros((TM, D), F32)
        for c in range(NCHIP):
            g = gg_ref[c].astype(F32)
            u = uu_ref[c].astype(F32)
            s = _sig(g)
            silu = g * s
            act_ref[c] = (silu * u).astype(BF)
            dact = lax.dot_general(dx2b, wd_vm[c], (((1,), (1,)), ((), ())), preferred_element_type=F32)
            dg = (dact * u * s * (1.0 + g * (1.0 - s))).astype(BF)
            du = (dact * silu).astype(BF)
            dgg_ref[c] = dg
            duu_ref[c] = du
            dh2 = dh2 + jnp.dot(dg, wg_vm[c], preferred_element_type=F32)
            dh2 = dh2 + jnp.dot(du, wu_vm[c], preferred_element_type=F32)
        xh, r = _rms(x_ref[...])
        dg_ref[0:1, :] += _colsum(dh2 * xh)
        dx1_ref[...] = dx2 + _rms_bwd(dh2, xh, r, g_ref[...])

    return _pallas_call(
        body, carry, name="ffn_bwd", grid=(L // TM,),
        in_specs=[_tok(D), _tok(D), _chunk_tok(FC), _chunk_tok(FC), _full((1, D)), ANY, ANY, ANY],
        out_specs=[_tok(D), _chunk_tok(FC), _chunk_tok(FC), _chunk_tok(FC), _full((SUB, D))],
        out_shape=[_sds((L, D)), _sds((NCHIP, L, FC), BF), _sds((NCHIP, L, FC), BF), _sds((NCHIP, L, FC), BF),
                   _sds((SUB, D))],
        scratch_shapes=[pltpu.VMEM((NCHIP, FC, D), BF)] * 3,
        compiler_params=_params(56),
    )(x1, dx2, gg, uu, g_ffn, wg, wu, wd)


def _ple_loss(x2, p, tgt, g_pg, w_pg, b_pg, w_ple, g_ple, g_final):
    L = x2.shape[0]

    def body(x2_ref, p_ref, t_ref, gpg_ref, wpg_ref, bpg_ref, wple_ref, gple_ref, gf_ref,
             dx2_ref, n2_ref, dpre_ref, de0_ref, acc_ref):
        @pl.when(pl.program_id(0) == 0)
        def _():
            acc_ref[...] = jnp.zeros_like(acc_ref)

        x2 = x2_ref[...]
        x2h, r2 = _rms(x2)
        n2 = (x2h * gpg_ref[...]).astype(BF)
        n2_ref[...] = n2
        gate = _sig(jnp.dot(n2, wpg_ref[...], preferred_element_type=F32) + bpg_ref[...])
        pb = p_ref[...].astype(BF)
        e0 = jnp.concatenate([jnp.dot(pb, wple_ref[k], preferred_element_type=F32) for k in range(NCHIP)], axis=1)
        e0h, re = _rms(e0)
        e = e0h * gple_ref[...]
        x3 = x2 + gate * e
        x3h, r3 = _rms(x3)
        diff = x3h * gf_ref[...] - t_ref[...]
        acc_ref[4:5, :] += _colsum(diff * diff) * (0.5 / D)
        dy = diff * (1.0 / D)
        acc_ref[3:4, :] += _colsum(dy * x3h)
        dx3 = _rms_bwd(dy, x3h, r3, gf_ref[...])
        de = dx3 * gate
        acc_ref[2:3, :] += _colsum(de * e0h)
        de0_ref[...] = _rms_bwd(de, e0h, re, gple_ref[...]).astype(BF)
        dpre = dx3 * e * gate * (1.0 - gate)
        acc_ref[1:2, :] += _colsum(dpre)
        dpreb = dpre.astype(BF)
        dpre_ref[...] = dpreb
        dn2 = lax.dot_general(dpreb, wpg_ref[...], (((1,), (1,)), ((), ())), preferred_element_type=F32)
        acc_ref[0:1, :] += _colsum(dn2 * x2h)
        dx2_ref[...] = dx3 + _rms_bwd(dn2, x2h, r2, gpg_ref[...])

    return _pallas_call(
        body, name="ple_loss", grid=(L // TM,),
        in_specs=[_tok(D), _tok(PLE), _tok(D), _full((1, D)), _full((D, D)), _full((1, D)), _full((NCHIP, PLE, AC)),
                  _full((1, D)), _full((1, D))],
        out_specs=[_tok(D), _tok(D), _tok(D), _tok(D), _full((SUB, D))],
        out_shape=[_sds((L, D)), _sds((L, D), BF), _sds((L, D), BF), _sds((L, D), BF), _sds((SUB, D))],
        compiler_params=_params(40),
    )(x2, p, tgt, g_pg, w_pg, b_pg, w_ple, g_ple, g_final)


def _tn(name, a, b, col_chunk=None):
    L = a.shape[-2]
    m, n = a.shape[-1], b.shape[-1]
    if a.ndim == 3 or b.ndim == 3:
        nj, bn = (a if a.ndim == 3 else b).shape[0], n
        a_spec = (pl.BlockSpec((None, TK, m), lambda j, t: (j, t, 0)) if a.ndim == 3
                  else pl.BlockSpec((TK, m), lambda j, t: (t, 0)))
        b_spec = (pl.BlockSpec((None, TK, n), lambda j, t: (j, t, 0)) if b.ndim == 3
                  else pl.BlockSpec((TK, n), lambda j, t: (t, 0)))
        out_spec, out_shape = pl.BlockSpec((None, m, n), lambda j, t: (j, 0, 0)), _sds((nj, m, n))
    else:
        bn = col_chunk
        if bn is None:
            bn = next((cand for cand in (1024, 512) if n > cand and n % cand == 0), n)
        nj = n // bn
        a_spec = pl.BlockSpec((TK, m), lambda j, t: (t, 0))
        b_spec = pl.BlockSpec((TK, bn), lambda j, t: (t, j))
        if col_chunk is None:
            out_spec, out_shape = pl.BlockSpec((m, bn), lambda j, t: (0, j)), _sds((m, n))
        else:
            out_spec, out_shape = pl.BlockSpec((None, m, bn), lambda j, t: (j, 0, 0)), _sds((nj, m, bn))

    def body(a_ref, b_ref, o_ref):
        @pl.when(pl.program_id(1) == 0)
        def _():
            o_ref[...] = jnp.zeros_like(o_ref)

        o_ref[...] += _mm_tn(a_ref[...], b_ref[...])

    return _pallas_call(
        body, name=name, grid=(nj, L // TK), in_specs=[a_spec, b_spec], out_specs=out_spec, out_shape=out_shape,
        compiler_params=pltpu.CompilerParams(dimension_semantics=("arbitrary", "arbitrary"),
                                             vmem_limit_bytes=40 * VMEM_MB),
    )(a, b)


LANE = 128


def _tn_blocks(name, a, b, ga, gb):
    L, n = a.shape[0], b.shape[1]
    per = LANE // ga
    wb = per * gb

    def body(a_ref, b_ref, o_ref, acc_ref):
        t = pl.program_id(1)

        @pl.when(t == 0)
        def _():
            acc_ref[...] = jnp.zeros_like(acc_ref)

        acc_ref[...] += _mm_tn(a_ref[...], b_ref[...])

        @pl.when(t == L // TK - 1)
        def _():
            rows = lax.broadcasted_iota(jnp.int32, (LANE, wb), 0) // ga
            cols = lax.broadcasted_iota(jnp.int32, (LANE, wb), 1) // gb
            kept = jnp.where(rows == cols, acc_ref[...], 0.0)
            o_ref[...] = jnp.sum(kept.reshape(per, ga, wb), axis=0)

    return _pallas_call(
        body, name=name, grid=(n // wb, L // TK),
        in_specs=[pl.BlockSpec((TK, LANE), lambda j, t: (t, j)), pl.BlockSpec((TK, wb), lambda j, t: (t, j))],
        out_specs=pl.BlockSpec((ga, wb), lambda j, t: (0, j)), out_shape=_sds((ga, n)),
        scratch_shapes=[pltpu.VMEM((LANE, wb), F32)],
        compiler_params=pltpu.CompilerParams(dimension_semantics=("arbitrary", "arbitrary"),
                                             vmem_limit_bytes=32 * VMEM_MB),
    )(a, b)


def _s5_discretize(lam_re, lam_im, log_dt, b_re, b_im):
    dt = jnp.exp(log_dt)[:, None]
    mag = jnp.exp(lam_re * dt)
    ar = mag * jnp.cos(lam_im * dt)
    ai = mag * jnp.sin(lam_im * dt)
    den = lam_re * lam_re + lam_im * lam_im
    nr = ar - 1.0
    fr = (nr * lam_re + ai * lam_im) / den
    fi = (ai * lam_re - nr * lam_im) / den
    bbr = fr[:, None, :] * b_re - fi[:, None, :] * b_im
    bbi = fr[:, None, :] * b_im + fi[:, None, :] * b_re
    return ar, ai, bbr, bbi


def _prepare(by_rows, block_cols, ar, ai):
    n = len(by_rows)

    def body(*refs):
        srcs, (ar_ref, ai_ref), dense, (con_ref, rev_ref) = refs[:n], refs[n:n + 2], refs[n + 2:2 * n + 2], refs[2 * n + 2:]
        for src, out, c in zip(srcs, dense, block_cols):
            r, width = src.shape
            groups = width // c
            tiled = jnp.broadcast_to(src[...][None], (groups, r, width)).reshape(groups * r, width)
            own = (lax.broadcasted_iota(jnp.int32, tiled.shape, 0) // r) == (lax.broadcasted_iota(jnp.int32, tiled.shape, 1) // c)
            out[...] = jnp.where(own, tiled, 0.0).astype(BF)
        a_r, a_i = ar_ref[...], ai_ref[...]
        pw = [(jnp.ones_like(a_r), jnp.zeros_like(a_i))]
        for _ in range(SUB):
            pr, pi = pw[-1]
            pw.append((pr * a_r - pi * a_i, pr * a_i + pi * a_r))
        row = _row_iota(GN)
        for ref, reverse in ((con_ref, False), (rev_ref, True)):
            sign = -1.0 if reverse else 1.0
            for j, sh in enumerate((1, 2, 4)):
                keep = (row < SUB - sh) if reverse else (row >= sh)
                ref[2 * j * SUB:(2 * j + 1) * SUB, :] = jnp.where(keep, pw[sh][0], 0.0)
                ref[(2 * j + 1) * SUB:(2 * j + 2) * SUB, :] = jnp.where(keep, sign * pw[sh][1], 0.0)
            p_r, p_i = jnp.zeros((SUB, GN), F32), jnp.zeros((SUB, GN), F32)
            for i in range(SUB):
                k = SUB - i if reverse else i + 1
                p_r = jnp.where(row == i, pw[k][0], p_r)
                p_i = jnp.where(row == i, sign * pw[k][1], p_i)
            ref[6 * SUB:7 * SUB, :] = p_r
            ref[7 * SUB:8 * SUB, :] = p_i

    dense_shapes = [(b.shape[1] // c * b.shape[0], b.shape[1]) for b, c in zip(by_rows, block_cols)]
    outs = _pallas_call(
        body, name="prepare", grid=(1,), in_specs=[_full(b.shape) for b in by_rows] + [_full((1, GN))] * 2,
        out_specs=[_full(s) for s in dense_shapes] + [_full((8 * SUB, GN))] * 2,
        out_shape=[_sds(s, BF) for s in dense_shapes] + [_sds((8 * SUB, GN))] * 2,
        compiler_params=_params(48),
    )(*by_rows, ar, ai)
    return outs[:n], outs[n], outs[n + 1]


def _local_step(x, p, tgt, w, comm):
    rows_of = lambda a: a.reshape(NCHIP * a.shape[1], a.shape[2])
    quarters = lambda a: a.reshape(NCHIP, a.shape[0] // NCHIP, a.shape[1])

    def gathering(names, call):
        carry = comm.gather(names)
        outs = list(call(carry))
        own = len(outs) - len(carry.out_shapes)
        w.update(zip(names, outs[own:]))
        return outs[:own]

    def reducing(tag, grads, call):
        parts, carry = comm.reduce_begin(tag, grads)
        outs = list(call(carry))
        own = len(outs) - len(carry.out_shapes)
        comm.reduce_end(tag, grads, parts, outs[own:])
        return outs[:own]

    w.update(comm.first())
    w_glu = rows_of(w["w_glu"])
    ar, ai, bbr, bbi = _s5_discretize(w["lam_re"], w["lam_im"], w["log_dt"], w["s5_b_re"], w["s5_b_im"])
    by_row = lambda b: jnp.transpose(b, (1, 0, 2)).reshape(b.shape[1], -1)
    (bbr_d, bbi_d, ccr_d, cci_d, wr_d, wi_d), con, con_rev = _prepare(
        [by_row(b) for b in (bbr, bbi, w["s5_c_re"], w["s5_c_im"], w["w_r"], w["w_i"])], [NS] * 4 + [HD] * 2,
        ar.reshape(1, GN), ai.reshape(1, GN))
    dsk = w["s5_d"].reshape(1, S5W)
    lam = w["lru_lambda"].reshape(1, LW)
    sp = jax.nn.softplus(-lam)
    b_r, b_i = w["b_r"].reshape(1, LW), w["b_i"].reshape(1, LW)
    row = lambda name: w[name].reshape(1, -1)

    h, ua, ub, gp = gathering(["w_a_out", "w_b_out", "w_o"], lambda carry: _inproj_fwd(
        x, row("g_mix"), w["w_in"], row("b_in"), carry))
    sr, si, y, zg, ya = gathering(["w_ffn_gate"], lambda carry: _s5_fwd(
        ua, bbr_d, bbi_d, ccr_d, cci_d, dsk, con, w_glu, row("b_glu"), carry))
    xc, rg, ig, yb, hp = gathering(["w_ffn_up"], lambda carry: _lru_fwd(
        ub, w["conv_w"], row("conv_b"), wr_d, wi_d, b_r, b_i, sp, carry))
    w_b_out, w_o = rows_of(w["w_b_out"]), rows_of(w["w_o"])
    x1, pa, pb, merged = gathering(["w_ffn_down"], lambda carry: _merge_fwd(
        x, ya, yb, gp, w["w_a_out"], w_b_out, w_o, carry))
    x2, h2, gg, uu = gathering(["w_ple_gate", "w_ple"], lambda carry: _ffn_fwd(
        x1, row("g_ffn"), w["w_ffn_gate"], w["w_ffn_up"], w["w_ffn_down"], carry))
    w_pg = rows_of(w["w_ple_gate"])
    dx2, n2, dpre, de0, acc_p = _ple_loss(x2, p, tgt, row("g_ple_gate"), w_pg, row("b_ple_gate"),
                                          w["w_ple"], row("g_ple"), row("g_final"))
    grads = {"w_ple_gate": quarters(_tn("dw_ple_gate", n2, dpre)), "w_ple": _tn("dw_ple", p, de0, col_chunk=AC)}
    dx1, act, dgg, duu, acc_f = reducing("ple", grads, lambda carry: _ffn_bwd(
        x1, dx2, gg, uu, row("g_ffn"), w["w_ffn_gate"], w["w_ffn_up"], w["w_ffn_down"], carry))
    grads = {"w_ffn_gate": _tn("dw_ffn_gate", dgg, h2)}
    dya, dyb, dgp, dpa, dpb = reducing("ffn_gate", grads, lambda carry: _merge_bwd(
        dx1, gp, pa, pb, w["w_a_out"], w_b_out, w_o, carry))
    grads = {"w_ffn_up": _tn("dw_ffn_up", duu, h2), "w_ffn_down": _tn("dw_ffn_down", act, dx2)}
    dua, dq, dy, lr, li, acc_a, acc_s = reducing("ffn_rest", grads, lambda carry: _s5_bwd(
        dya, y, ua, sr, si, bbr_d, bbi_d, ccr_d, cci_d, dsk, con_rev, w_glu, row("b_glu"), carry))
    grads = {"w_o": quarters(_tn("dw_o", merged, dx1)), "w_a_out": _tn("dw_a_out", ya, dpa, col_chunk=AC),
             "w_b_out": quarters(_tn("dw_b_out", yb, dpb))}
    dub, dpr, dpi, acc_l = reducing("merge", grads, lambda carry: _lru_bwd(
        dyb, xc, rg, ig, hp, ub, w["conv_w"], wr_d, wi_d, sp, -_sig(-lam), carry))
    gx, dz, acc_g, acc_b = _inproj_bwd(x, dx1, dua, dub, dgp, row("g_mix"), w["w_in"])
    grads = {"w_in": _tn("dw_in", h, dz, col_chunk=QC), "w_glu": quarters(_tn("dw_glu", zg, dq))}
    reducing("in", grads, lambda carry: _run_now("exchange_in", carry))
    sums = {"ple": acc_p, "ffn": acc_f, "mix": acc_g, "b_in": acc_b, "lru": acc_l, "s5": acc_s, "s5_a": acc_a}
    blocks = {
        "bb_re": _tn_blocks("d_bbr", ua, lr, NP, NS),
        "bb_im": _tn_blocks("d_bbi", ua, li, NP, NS),
        "cc_re": _tn_blocks("d_ccr", dy, sr, NP, NS),
        "cc_im": _tn_blocks("d_cci", dy, si, NP, NS),
        "w_r": _tn_blocks("dw_r", xc, dpr, HD, HD),
        "w_i": _tn_blocks("dw_i", xc, dpi, HD, HD),
    }
    return gx, sums, blocks


def _replicated_grads(w, sums, blocks):
    grouped = lambda e, groups: jnp.transpose(e.reshape(e.shape[0], groups, -1), (1, 0, 2))
    d_ar, d_ai = sums["s5_a"][0].reshape(NG, NS), sums["s5_a"][1].reshape(NG, NS)
    d_bbr, d_bbi = grouped(blocks["bb_re"], NG), grouped(blocks["bb_im"], NG)
    _, vjp = jax.vjp(_s5_discretize, w["lam_re"], w["lam_im"], w["log_dt"], w["s5_b_re"], w["s5_b_im"])
    g = dict(zip(("lam_re", "lam_im", "log_dt", "s5_b_re", "s5_b_im"), vjp((d_ar, d_ai, d_bbr, d_bbi))))
    g["s5_c_re"] = grouped(blocks["cc_re"], NG)
    g["s5_c_im"] = -grouped(blocks["cc_im"], NG)
    g["w_r"], g["w_i"] = grouped(blocks["w_r"], NH), grouped(blocks["w_i"], NH)
    g["s5_d"] = sums["s5"][0].reshape(NG, NP)
    g["b_r"] = sums["lru"][1].reshape(NH, HD)
    g["b_i"] = sums["lru"][2].reshape(NH, HD)
    return g


ACC_ROWS = {"g_mix": ("mix", 0), "b_in": ("b_in", 0), "g_ffn": ("ffn", 0), "g_ple_gate": ("ple", 0),
            "b_ple_gate": ("ple", 1), "g_ple": ("ple", 2), "g_final": ("ple", 3), "b_glu": ("s5", 1),
            "lru_lambda": ("lru", 0), "conv_b": ("lru", 3)}
LOSS_ROW = ("ple", 4)
CONV_W_ROWS = ("lru", 4)


SHARDED = [("w_in", (D, QC)), ("w_glu", (S5W // NCHIP, S5W)), ("w_a_out", (S5W, AC)), ("w_b_out", (LW // NCHIP, D)),
           ("w_o", (D // NCHIP, D)), ("w_ffn_gate", (FC, D)), ("w_ffn_up", (FC, D)), ("w_ffn_down", (FC, D)),
           ("w_ple_gate", (D // NCHIP, D)), ("w_ple", (PLE, AC))]
NSH = len(SHARDED)
TRANSPOSED = ("w_ffn_gate", "w_ffn_up", "s5_b_re", "s5_b_im")
CONV_SHARD = (4, LW // NCHIP)


def _mesh_pos():
    return lax.axis_index("x"), lax.axis_index("y"), lax.axis_index("c")


def _other_chips(x, y):
    return [(1 - x, y), (x, 1 - y), (1 - x, 1 - y)]


def _half_rows(c, rows, align):
    return pl.ds(pl.multiple_of(c * (rows // 2), align), rows // 2)


def _run_now(name, carry):
    c_in, c_out = len(carry.operands), len(carry.out_shapes)

    def body(*refs):
        ins, outs, sems = refs[:c_in], refs[c_in:c_in + c_out], refs[c_in + c_out:]
        carry.start(ins, outs, sems)
        carry.finish(ins, outs, sems)

    return pl.pallas_call(body, name=name, in_specs=[ANY] * c_in, out_specs=[ANY] * c_out,
                          out_shape=list(carry.out_shapes), scratch_shapes=list(carry.sems))(*_in_hbm(carry.operands))


def _gather_group(shards, split):
    n = len(shards)

    def copies(srcs, outs, sems):
        send_sems, recv_sems = sems
        x, y, c = _mesh_pos()
        k0 = 2 * x + y
        sib = (x, y, 1 - c)
        chips = _other_chips(x, y)

        def remote(src, dst, j, i, to):
            return pltpu.make_async_remote_copy(src_ref=src, dst_ref=dst, send_sem=send_sems.at[j, i],
                                                recv_sem=recv_sems.at[j, i], device_id=to, device_id_type=MESH)

        def rows(ref, i, core, *lead):
            if not split[i]:
                return ref.at[lead] if lead else ref
            return ref.at[(*lead, _half_rows(core, shards[i].shape[0], 16))]

        own = [remote(s, o.at[k0], 6, i, sib) for i, (s, o) in enumerate(zip(srcs, outs))]
        ici, landed, fwd, fwd_landed = [], [], [], []
        for j, chip in enumerate(chips):
            kj = 2 * chip[0] + chip[1]
            pairs = list(enumerate(zip(srcs, outs)))
            ici.append([remote(rows(s, i, c), rows(o, i, c, k0), j, i, (*chip, c)) for i, (s, o) in pairs])
            landed.append([remote(rows(s, i, c), rows(o, i, c, kj), j, i, (*chip, c)) for i, (s, o) in pairs])
            fwd.append([remote(rows(o, i, c, kj), rows(o, i, c, kj), 3 + j, i, sib) for i, (s, o) in pairs if split[i]])
            fwd_landed.append([remote(rows(o, i, 1 - c, kj), rows(o, i, 1 - c, kj), 3 + j, i, sib)
                               for i, (s, o) in pairs if split[i]])
        return own, ici, landed, fwd, fwd_landed

    def start(srcs, outs, sems):
        own, ici, _, _, _ = copies(srcs, outs, sems)
        for cp in own + [cp for per_chip in ici for cp in per_chip]:
            cp.start()

    def finish(srcs, outs, sems):
        own, ici, landed, fwd, fwd_landed = copies(srcs, outs, sems)
        passed = [i for i in range(n) if split[i]]
        for j in range(3):
            for i, cp in enumerate(landed[j]):
                cp.wait_recv()
                if split[i]:
                    fwd[j][passed.index(i)].start()
        for j in range(3):
            for cp in fwd_landed[j]:
                cp.wait_recv()
        for cp in own:
            cp.wait_recv()
        for cp in own + [cp for per_chip in ici + fwd for cp in per_chip]:
            cp.wait_send()

    return _Carried(shards, [_sds((NCHIP,) + s.shape, s.dtype) for s in shards],
                    [pltpu.SemaphoreType.DMA((7, n)), pltpu.SemaphoreType.DMA((7, n))], start, finish)


def _swap_sibling_halves(name, grads):
    n = len(grads)

    def body(*refs):
        srcs, outs, (send_sems, recv_sems) = refs[:n], refs[n:2 * n], refs[2 * n:]
        x, y, c = _mesh_pos()
        cps = [pltpu.make_async_remote_copy(src_ref=s.at[:, _half_rows(1 - c, s.shape[1], 8)], dst_ref=o,
                                            send_sem=send_sems.at[i], recv_sem=recv_sems.at[i], device_id=(x, y, 1 - c),
                                            device_id_type=MESH) for i, (s, o) in enumerate(zip(srcs, outs))]
        for cp in cps:
            cp.start()
        for cp in cps:
            cp.wait()

    return _pallas_call(
        body, name="swap_sibling_halves_" + name, in_specs=[ANY] * n, out_specs=[ANY] * n,
        out_shape=[_sds((NCHIP, g.shape[1] // 2, g.shape[2])) for g in grads],
        scratch_shapes=[pltpu.SemaphoreType.DMA((n,)), pltpu.SemaphoreType.DMA((n,))],
    )(*_in_hbm(grads))


def _add_sibling(name, c_idx, g, got):
    hr, cols = got.shape[1:]

    def body(c_ref, g_ref, got_ref, p_ref, pb_ref):
        s = g_ref[...] + got_ref[...]
        p_ref[...] = s
        pb_ref[...] = s.astype(BF)

    spec = pl.BlockSpec((None, hr, cols), lambda k, c_ref: (k, 0, 0))
    return _pallas_call(
        body, name="add_sibling_" + name,
        grid_spec=pltpu.PrefetchScalarGridSpec(
            num_scalar_prefetch=1, grid=(NCHIP,),
            in_specs=[pl.BlockSpec((None, hr, cols), lambda k, c_ref: (k, c_ref[0], 0)), spec],
            out_specs=[spec, spec]),
        out_shape=[_sds((NCHIP, hr, cols)), _sds((NCHIP, hr, cols), BF)],
        compiler_params=_params(32),
    )(c_idx, *_in_hbm([g, got]))


def _exchange_group(parts):
    n = len(parts)

    def copies(srcs, outs, sems):
        send_sems, recv_sems = sems
        x, y, c = _mesh_pos()
        return [pltpu.make_async_remote_copy(
            src_ref=s.at[2 * chip[0] + chip[1]], dst_ref=o.at[j], send_sem=send_sems.at[j, i],
            recv_sem=recv_sems.at[j, i], device_id=(*chip, c), device_id_type=MESH)
            for j, chip in enumerate(_other_chips(x, y)) for i, (s, o) in enumerate(zip(srcs, outs))]

    def start(srcs, outs, sems):
        for cp in copies(srcs, outs, sems):
            cp.start()

    def finish(srcs, outs, sems):
        for cp in copies(srcs, outs, sems):
            cp.wait()

    return _Carried(parts, [_sds((3,) + p.shape[1:], BF) for p in parts],
                    [pltpu.SemaphoreType.DMA((3, n)), pltpu.SemaphoreType.DMA((3, n))], start, finish)


def _add_chips(name, kc_idx, p, got):
    hr, cols = got.shape[1:]

    def body(kc_ref, p_ref, got_ref, t_ref):
        t_ref[...] = ((p_ref[...] + got_ref[0].astype(F32)) + got_ref[1].astype(F32)) + got_ref[2].astype(F32)

    return _pallas_call(
        body, name="add_chips_" + name,
        grid_spec=pltpu.PrefetchScalarGridSpec(
            num_scalar_prefetch=1, grid=(1,),
            in_specs=[pl.BlockSpec((None, hr, cols), lambda i, kc_ref: (kc_ref[0], 0, 0)),
                      pl.BlockSpec((3, hr, cols), lambda i, kc_ref: (0, 0, 0))],
            out_specs=pl.BlockSpec((None, hr, cols), lambda i, kc_ref: (kc_ref[1], 0, 0))),
        out_shape=_sds((2, hr, cols)),
        compiler_params=_params(32),
    )(kc_idx, *_in_hbm([p, got]))


def _join_sibling(name, halves):
    n = len(halves)

    def body(*refs):
        bufs, (send_sems, recv_sems) = refs[n:2 * n], refs[2 * n:]
        x, y, c = _mesh_pos()
        sib = (x, y, 1 - c)
        sends = [pltpu.make_async_remote_copy(src_ref=b.at[c], dst_ref=b.at[c], send_sem=send_sems.at[i],
                                              recv_sem=recv_sems.at[i], device_id=sib, device_id_type=MESH)
                 for i, b in enumerate(bufs)]
        for cp in sends:
            cp.start()
        for i, b in enumerate(bufs):
            pltpu.make_async_remote_copy(src_ref=b.at[c], dst_ref=b.at[1 - c], send_sem=send_sems.at[i],
                                         recv_sem=recv_sems.at[i], device_id=sib, device_id_type=MESH).wait_recv()
        for cp in sends:
            cp.wait_send()

    return _pallas_call(
        body, name="join_sibling_" + name, in_specs=[ANY] * n, out_specs=[ANY] * n,
        out_shape=[_sds(h.shape) for h in halves], input_output_aliases={i: i for i in range(n)},
        scratch_shapes=[pltpu.SemaphoreType.DMA((n,)), pltpu.SemaphoreType.DMA((n,))],
    )(*_in_hbm(halves))


def _allreduce_small(arrays, wire):
    n = len(arrays)
    halves = [(a.shape[0], a.shape[1] // 2) for a in arrays]

    def body(*refs):
        srcs, outs = refs[:n], refs[n:2 * n]
        mine_bufs, sib_bufs, chip_bufs, total_bufs = (refs[k * n:(k + 1) * n] for k in range(2, 6))
        send_sems, recv_sems, local_sems = refs[6 * n:]
        x, y, c = _mesh_pos()
        k0 = 2 * x + y
        sib = (x, y, 1 - c)

        def remote(src, dst, j, i, to):
            return pltpu.make_async_remote_copy(src_ref=src, dst_ref=dst, send_sem=send_sems.at[j, i],
                                                recv_sem=recv_sems.at[j, i], device_id=to, device_id_type=MESH)

        def cols(ref, i, core):
            return ref.at[:, pl.ds(pl.multiple_of(core * halves[i][1], LANE), halves[i][1])]

        swaps = [remote(cols(s, i, 1 - c), b, 0, i, sib) for i, (s, b) in enumerate(zip(srcs, sib_bufs))]
        own = [pltpu.make_async_copy(cols(s, i, c), m, local_sems.at[i]) for i, (s, m) in enumerate(zip(srcs, mine_bufs))]
        for cp in swaps + own:
            cp.start()
        for cp in swaps + own:
            cp.wait()
        for m, b, buf in zip(mine_bufs, sib_bufs, chip_bufs):
            buf[k0] = (m[...] + b[...]).astype(buf.dtype)
        chips = _other_chips(x, y)
        sends = [remote(buf.at[k0], buf.at[k0], 1 + j, i, (*chip, c))
                 for j, chip in enumerate(chips) for i, buf in enumerate(chip_bufs)]
        for cp in sends:
            cp.start()
        for j, chip in enumerate(chips):
            for i, buf in enumerate(chip_bufs):
                remote(buf.at[k0], buf.at[2 * chip[0] + chip[1]], 1 + j, i, (*chip, c)).wait_recv()
        for cp in sends:
            cp.wait_send()
        for t, buf in zip(total_bufs, chip_bufs):
            t[...] = ((buf[0].astype(F32) + buf[1].astype(F32)) + buf[2].astype(F32)) + buf[3].astype(F32)
        joins = [remote(t, cols(o, i, c), 4, i, sib) for i, (t, o) in enumerate(zip(total_bufs, outs))]
        keep = [pltpu.make_async_copy(t, cols(o, i, c), local_sems.at[i]) for i, (t, o) in enumerate(zip(total_bufs, outs))]
        for cp in joins + keep:
            cp.start()
        for i, (t, o) in enumerate(zip(total_bufs, outs)):
            remote(t, cols(o, i, 1 - c), 4, i, sib).wait_recv()
        for cp in joins:
            cp.wait_send()
        for cp in keep:
            cp.wait()

    specs = [_full(a.shape) for a in arrays]
    return _pallas_call(
        body, name="allreduce_small", grid=(1,), in_specs=specs, out_specs=specs,
        out_shape=[_sds(a.shape) for a in arrays],
        scratch_shapes=([pltpu.VMEM(h, F32) for h in halves] + [pltpu.VMEM(h, F32) for h in halves]
                        + [pltpu.VMEM((NCHIP,) + h, dt) for h, dt in zip(halves, wire)] + [pltpu.VMEM(h, F32) for h in halves]
                        + [pltpu.SemaphoreType.DMA((5, n)), pltpu.SemaphoreType.DMA((5, n)), pltpu.SemaphoreType.DMA((n,))]),
        compiler_params=_params(32),
    )(*arrays)


def _adamw_terms(w, g, m, v):
    m = ADAM_B1 * m + (1.0 - ADAM_B1) * g
    v = ADAM_B2 * v + (1.0 - ADAM_B2) * jnp.square(g)
    m_hat = m / (1.0 - ADAM_B1 ** ADAM_STEP)
    v_hat = v / (1.0 - ADAM_B2 ** ADAM_STEP)
    return -ADAM_LR * (m_hat / (jnp.sqrt(v_hat) + ADAM_EPS) + ADAM_WD * w), m, v


def _adamw(name, w, g, m, v):
    r, c = w.shape
    rows = max(b for b in range(SUB, r + 1, SUB) if r % b == 0 and b * c * 4 <= 3 * VMEM_MB // 2)

    def body(w_ref, g_ref, m_ref, v_ref, d_ref, nm_ref, nv_ref):
        d_ref[...], nm_ref[...], nv_ref[...] = _adamw_terms(w_ref[...], g_ref[...], m_ref[...], v_ref[...])

    spec = pl.BlockSpec((rows, c), lambda i: (i, 0))
    return _pallas_call(
        body, name=name, grid=(r // rows,), in_specs=[spec] * 4, out_specs=[spec] * 3,
        out_shape=[_sds((r, c))] * 3, compiler_params=_params(40),
    )(*_in_hbm([w, g, m, v]))


def _adamw_replicated(sums, row_of, direct):
    ns, nr, nd = len(sums), len(row_of), len(direct)

    def body(*refs):
        sum_refs = refs[:ns]
        ins = refs[ns:ns + 3 * nr + 4 * nd]
        outs = refs[ns + 3 * nr + 4 * nd:]
        for i, (_, _, _, si, row) in enumerate(row_of):
            w_ref, m_ref, v_ref = ins[3 * i:3 * i + 3]
            g = sum_refs[si][row:row + 1, :]
            outs[4 * i][...] = g
            outs[4 * i + 1][...], outs[4 * i + 2][...], outs[4 * i + 3][...] = _adamw_terms(w_ref[...], g, m_ref[...], v_ref[...])
        for i in range(nd):
            w_ref, m_ref, v_ref, g_ref = ins[3 * nr + 4 * i:3 * nr + 4 * i + 4]
            o = outs[4 * (nr + i):4 * (nr + i) + 4]
            g = g_ref[...]
            o[0][...] = g
            o[1][...], o[2][...], o[3][...] = _adamw_terms(w_ref[...], g, m_ref[...], v_ref[...])

    operands = list(sums)
    shapes = []
    for w, m, v, _, _ in row_of:
        operands += [w, m, v]
        shapes += [w.shape] * 4
    for w, m, v, g in direct:
        operands += [w, m, v, g]
        shapes += [w.shape] * 4
    flat = _pallas_call(
        body, name="adamw_replicated", grid=(1,), in_specs=[_full(a.shape) for a in operands],
        out_specs=[_full(s) for s in shapes], out_shape=[_sds(s) for s in shapes],
        compiler_params=_params(56),
    )(*operands)
    return [flat[4 * i:4 * i + 4] for i in range(nr + nd)]


class _Exchanges:
    def __init__(self, shards, conv_w, chip, core, apply):
        self.shards, self.conv_w, self.apply = shards, conv_w, apply
        self.core_idx = jnp.reshape(core, (1,)).astype(jnp.int32)
        self.chip_core_idx = jnp.stack([chip, core]).astype(jnp.int32)

    def first(self):
        names = ["w_in", "w_glu"]
        got = _run_now("gather_first", _gather_group([self.shards[n] for n in names] + [self.conv_w],
                                                     [True, True, False]))
        out = dict(zip(names, got))
        out["conv_w"] = jnp.transpose(got[2], (1, 0, 2)).reshape(4, LW)
        return out

    def gather(self, names):
        return _gather_group([self.shards[n] for n in names], [True] * len(names))

    def reduce_begin(self, tag, grads):
        names = list(grads)
        arrived = _swap_sibling_halves(tag, [grads[n] for n in names])
        parts = [_add_sibling(n, self.core_idx, grads[n], rx) for n, rx in zip(names, arrived)]
        return parts, _exchange_group([bf for _, bf in parts])

    def reduce_end(self, tag, grads, parts, arrived):
        names = list(grads)
        halves = [_add_chips(n, self.chip_core_idx, f32, rx) for n, (f32, _), rx in zip(names, parts, arrived)]
        for n, both in zip(names, _join_sibling(tag, halves)):
            self.apply(n, both.reshape(dict(SHARDED)[n]))


INPUT_NAMES = (["x", "p"] + [n for n in
               ["g_mix", "w_in", "b_in", "lam_re", "lam_im", "log_dt", "s5_b_re", "s5_b_im", "s5_c_re", "s5_c_im", "s5_d",
                "w_glu", "b_glu", "conv_w", "conv_b", "w_r", "b_r", "w_i", "b_i", "lru_lambda", "w_a_out", "w_b_out", "w_o",
                "g_ffn", "w_ffn_gate", "w_ffn_up", "w_ffn_down", "g_ple_gate", "w_ple_gate", "b_ple_gate", "w_ple", "g_ple",
                "g_final"]])
WEIGHT_NAMES = INPUT_NAMES[2:]


def kernel(*args):
    names = INPUT_NAMES + ["loss_target"] + ["m_" + n for n in WEIGHT_NAMES] + ["v_" + n for n in WEIGHT_NAMES]
    assert len(args) == len(names)
    given = dict(zip(names, args))

    def view(name):
        a = given[name]
        return jnp.swapaxes(a, -1, -2) if name.endswith(TRANSPOSED) else a

    def unview(name, a):
        return jnp.swapaxes(a, -1, -2) if name in TRANSPOSED else a

    def local(name):
        return view(name) if name.endswith("g_final") else view(name)[0]

    xi, yi, ci = _mesh_pos()
    k0 = 2 * xi + yi
    x, p, tgt = given["x"][0], given["p"][0, 0], given["loss_target"][0]

    results = {}

    def apply(n, total):
        delta, new_m, new_v = _adamw("adamw_" + n, local(n), total, local("m_" + n), local("v_" + n))
        for kind, arr in zip(("grad", "delta", "new_m", "new_v"), (total, delta, new_m, new_v)):
            results[kind, n] = unview(n, arr[None])

    comm = _Exchanges({n: local(n).astype(BF) for n, _ in SHARDED}, local("conv_w"), k0, ci, apply)
    w = {n: local(n) for n in WEIGHT_NAMES if n != "conv_w" and n not in dict(SHARDED)}
    gx, sums, blocks = _local_step(x, p, tgt, w, comm)

    sum_names, block_names = list(sums), list(blocks)
    red = _allreduce_small([sums[n] for n in sum_names] + [blocks[n] for n in block_names],
                           [F32] * len(sum_names) + [BF] * len(block_names))
    sums = dict(zip(sum_names, red[:len(sum_names)]))
    blocks = dict(zip(block_names, red[len(sum_names):]))
    loss = jnp.sum(sums[LOSS_ROW[0]][LOSS_ROW[1]])
    direct_g = _replicated_grads(w, sums, blocks)
    conv_rows = sums[CONV_W_ROWS[0]][CONV_W_ROWS[1]:CONV_W_ROWS[1] + 4]
    direct_g["conv_w"] = lax.dynamic_slice(conv_rows, (0, k0 * CONV_SHARD[1]), CONV_SHARD)
    as_row = lambda a: a.reshape(1, -1)
    row_names = list(ACC_ROWS)
    row_of = [(as_row(given[n]), as_row(given["m_" + n]), as_row(given["v_" + n]),
               sum_names.index(ACC_ROWS[n][0]), ACC_ROWS[n][1]) for n in row_names]
    direct_names = list(direct_g)
    direct = [(view(n), view("m_" + n), view("v_" + n), direct_g[n].reshape(view(n).shape)) for n in direct_names]
    done = _adamw_replicated([sums[n] for n in sum_names], row_of, direct)
    for n, four in zip(row_names + direct_names, done):
        for kind, arr in zip(("grad", "delta", "new_m", "new_v"), four):
            results[kind, n] = unview(n, arr).reshape(given[n].shape)

    out = [loss, gx[None]]
    for kind in ("grad", "delta", "new_m", "new_v"):
        out += [results[kind, n] for n in WEIGHT_NAMES]
    return tuple(out)
```

```python
import functools
import math

import jax
import jax.numpy as jnp
from jax import lax
from jax.experimental import pallas as pl
from jax.experimental.pallas import tpu as pltpu

F32 = jnp.float32
BF = jnp.bfloat16

D = 1024
S5W = 512
NG, NS, NP = 32, 64, 16
GN = NG * NS
LW = 1024
NH, HD = 16, 64
LRU_C = 8.0
FH = 2816
NCHIP = 4
FC = FH // NCHIP
PLE = 256
INC = S5W + LW + 2 * D
EPS = 1e-6
ADAM_LR, ADAM_B1, ADAM_B2, ADAM_EPS, ADAM_WD, ADAM_STEP = 0.001, 0.9, 0.999, 1e-08, 0.01, 10

TM = 256
TK = 512
LC = 512
SUB = 8
VMEM_MB = 1024 * 1024
MESH = pl.DeviceIdType.MESH
ANY = pl.BlockSpec(memory_space=pl.ANY)


def _mm(a, b):
    return jnp.dot(a.astype(BF), b.astype(BF), preferred_element_type=F32)


def _mm_nt(a, b):
    return lax.dot_general(a.astype(BF), b.astype(BF), (((1,), (1,)), ((), ())), preferred_element_type=F32)


def _mm_tn(a, b):
    return lax.dot_general(a.astype(BF), b.astype(BF), (((0,), (0,)), ((), ())), preferred_element_type=F32)


def _rms(x):
    r = lax.rsqrt(jnp.mean(x * x, axis=-1, keepdims=True) + EPS)
    return x * r, r


def _rms_bwd(dy, xh, r, g):
    dxh = dy * g
    return r * (dxh - xh * jnp.mean(dxh * xh, axis=-1, keepdims=True))


def _colsum(x):
    return jnp.sum(x, axis=0, keepdims=True)


def _sig(x):
    return jax.nn.sigmoid(x)


def _gelu_grad(x):
    c = math.sqrt(2.0 / math.pi)
    t = jnp.tanh(c * (x + 0.044715 * x * x * x))
    return 0.5 * (1.0 + t) + 0.5 * x * (1.0 - t * t) * c * (1.0 + 3.0 * 0.044715 * x * x)


def _neg_expm1(x):
    series = -x * (1.0 + x * (0.5 + x * (1.0 / 6.0 + x * (1.0 / 24.0))))
    return jnp.where(x > -0.03, series, 1.0 - jnp.exp(x))


def _tok(width):
    return pl.BlockSpec((TM, width), lambda i: (i, 0))


def _tok_rev(width, nt):
    return pl.BlockSpec((TM, width), lambda i: (nt - 1 - i, 0))


def _full(shape):
    return pl.BlockSpec(shape, lambda i: (0,) * len(shape))


def _params(vmem_mb, **kw):
    return pltpu.CompilerParams(dimension_semantics=("arbitrary",), vmem_limit_bytes=vmem_mb * VMEM_MB, **kw)


def _sds(shape, dtype=F32):
    return jax.ShapeDtypeStruct(shape, dtype)


class _Carried:
    def __init__(self, operands, out_shapes, sems, start, finish, aliases=None):
        self.operands, self.out_shapes, self.sems = list(operands), list(out_shapes), list(sems)
        self.start, self.finish, self.aliases = start, finish, dict(aliases or {})


def _in_hbm(arrays):
    return [pltpu.with_memory_space_constraint(a, pltpu.HBM) for a in arrays]


def _pallas_call(body, carry=None, **kw):
    if carry is None:
        return pl.pallas_call(body, **kw)

    def at_step(corner):
        hit = [pl.program_id(d) == (size - 1 if corner else 0) for d, size in enumerate(kw["grid"])]
        return functools.reduce(jnp.logical_and, hit)

    name, grid, compiler_params = kw["name"], kw["grid"], kw["compiler_params"]
    in_specs, out_specs, out_shape = list(kw["in_specs"]), list(kw["out_specs"]), list(kw["out_shape"])
    scratch_shapes = list(kw.get("scratch_shapes", ()))
    n_in, n_out, n_scr = len(in_specs), len(out_specs), len(scratch_shapes)
    c_in, c_out = len(carry.operands), len(carry.out_shapes)

    def full_body(*refs):
        ins, refs = refs[:n_in], refs[n_in:]
        c_ins, refs = refs[:c_in], refs[c_in:]
        outs, refs = refs[:n_out], refs[n_out:]
        c_outs, refs = refs[:c_out], refs[c_out:]
        scratch, c_sems = refs[:n_scr], refs[n_scr:]

        @pl.when(at_step(0))
        def _():
            carry.start(c_ins, c_outs, c_sems)

        body(*ins, *outs, *scratch)

        @pl.when(at_step(1))
        def _():
            carry.finish(c_ins, c_outs, c_sems)

    call = pl.pallas_call(
        full_body, name=name, grid=grid, in_specs=in_specs + [ANY] * c_in, out_specs=out_specs + [ANY] * c_out,
        out_shape=out_shape + list(carry.out_shapes), scratch_shapes=scratch_shapes + list(carry.sems),
        input_output_aliases={n_in + i: n_out + o for i, o in carry.aliases.items()},
        compiler_params=compiler_params)
    return lambda *operands: call(*operands, *_in_hbm(carry.operands))


def _row_iota(width):
    return lax.broadcasted_iota(jnp.int32, (SUB, width), 0)


def _bcast_row(x, row):
    return jnp.broadcast_to(x[row:row + 1, :], x.shape)


def _slab(k):
    return pl.ds(pl.multiple_of(k * SUB, SUB), SUB)


QC = INC // NCHIP
Z_PARTS = ((0, S5W), (S5W, S5W + LW), (S5W + LW, INC))


def _inproj_fwd(x, g_mix, w_in, b_in, carry=None):
    L = x.shape[0]

    def body(x_ref, g_ref, w_hbm, b_ref, h_ref, ua_ref, ub_ref, gp_ref, w_vm):
        @pl.when(pl.program_id(0) == 0)
        def _():
            pltpu.sync_copy(w_hbm, w_vm)

        xh, _ = _rms(x_ref[...])
        h = (xh * g_ref[...]).astype(BF)
        h_ref[...] = h
        for k in range(NCHIP):
            lo, hi = k * QC, (k + 1) * QC
            z = jnp.dot(h, w_vm[k], preferred_element_type=F32) + b_ref[:, lo:hi]
            for ref, (a, b) in zip((ua_ref, ub_ref, gp_ref), Z_PARTS):
                s, e = max(lo, a), min(hi, b)
                if s < e:
                    ref[:, s - a:e - a] = z[:, s - lo:e - lo]

    return _pallas_call(
        body, carry, name="inproj_fwd", grid=(L // TM,),
        in_specs=[_tok(D), _full((1, D)), ANY, _full((1, INC))],
        out_specs=[_tok(D), _tok(S5W), _tok(LW), _tok(2 * D)],
        out_shape=[_sds((L, D), BF), _sds((L, S5W)), _sds((L, LW)), _sds((L, 2 * D))],
        scratch_shapes=[pltpu.VMEM((NCHIP, D, QC), BF)],
        compiler_params=_params(40),
    )(x, g_mix, w_in, b_in)


def _inproj_bwd(x, dx1, dua, dub, dgp, g_mix, w_in, carry=None):
    L = x.shape[0]

    def body(x_ref, dx1_ref, dua_ref, dub_ref, dgp_ref, g_ref, w_hbm, gx_ref, dz_ref, dg_ref, db_ref, w_vm):
        @pl.when(pl.program_id(0) == 0)
        def _():
            pltpu.sync_copy(w_hbm, w_vm)
            dg_ref[...] = jnp.zeros_like(dg_ref)
            db_ref[...] = jnp.zeros_like(db_ref)

        for src, (a, b) in zip((dua_ref, dub_ref, dgp_ref), Z_PARTS):
            d = src[...]
            dz_ref[:, a:b] = d.astype(BF)
            db_ref[0:1, a:b] += _colsum(d)
        dh = jnp.zeros((TM, D), F32)
        for k in range(NCHIP):
            dh = dh + lax.dot_general(dz_ref[:, k * QC:(k + 1) * QC], w_vm[k], (((1,), (1,)), ((), ())),
                                      preferred_element_type=F32)
        xh, r = _rms(x_ref[...])
        dg_ref[0:1, :] += _colsum(dh * xh)
        gx_ref[...] = dx1_ref[...] + _rms_bwd(dh, xh, r, g_ref[...])

    return _pallas_call(
        body, carry, name="inproj_bwd", grid=(L // TM,),
        in_specs=[_tok(D), _tok(D), _tok(S5W), _tok(LW), _tok(2 * D), _full((1, D)), ANY],
        out_specs=[_tok(D), _tok(INC), _full((SUB, D)), _full((SUB, INC))],
        out_shape=[_sds((L, D)), _sds((L, INC), BF), _sds((SUB, D)), _sds((SUB, INC))],
        scratch_shapes=[pltpu.VMEM((NCHIP, D, QC), BF)],
        compiler_params=_params(40),
    )(x, dx1, dua, dub, dgp, g_mix, w_in)


def _cscan(xr_ref, xi_ref, con_ref, cr_ref, ci_ref, reverse):
    n_slab = xr_ref.shape[0] // SUB
    width = xr_ref.shape[1]
    for lc in range(width // LC):
        cols = slice(lc * LC, (lc + 1) * LC)
        con = [con_ref[SUB * j:SUB * (j + 1), cols] for j in range(8)]

        def step(k, carry, cols=cols, con=con):
            cr, ci = carry
            rows = _slab(n_slab - 1 - k if reverse else k)
            xr, xi = xr_ref[rows, cols], xi_ref[rows, cols]
            for j, sh in enumerate((1, 2, 4)):
                mr, mi = con[2 * j], con[2 * j + 1]
                pr = pltpu.roll(xr, SUB - sh if reverse else sh, 0)
                pi = pltpu.roll(xi, SUB - sh if reverse else sh, 0)
                xr, xi = xr + mr * pr - mi * pi, xi + mr * pi + mi * pr
            xr, xi = xr + con[6] * cr - con[7] * ci, xi + con[6] * ci + con[7] * cr
            xr_ref[rows, cols] = xr
            xi_ref[rows, cols] = xi
            row = 0 if reverse else SUB - 1
            return _bcast_row(xr, row), _bcast_row(xi, row)

        cr, ci = lax.fori_loop(0, n_slab, step, (cr_ref[:, cols], ci_ref[:, cols]))
        cr_ref[:, cols] = cr
        ci_ref[:, cols] = ci


def _s5_fwd(ua, bbr, bbi, ccr, cci, dsk, con, w_glu, b_glu, carry=None):
    L = ua.shape[0]

    def body(ua_ref, bbr_hbm, bbi_hbm, ccr_hbm, cci_hbm, dsk_ref, con_ref, wg_ref, bg_ref,
             sr_ref, si_ref, y_ref, zg_ref, ya_ref, bbr_vm, bbi_vm, ccr_vm, cci_vm, cr_ref, ci_ref):
        @pl.when(pl.program_id(0) == 0)
        def _():
            pltpu.sync_copy(bbr_hbm, bbr_vm)
            pltpu.sync_copy(bbi_hbm, bbi_vm)
            pltpu.sync_copy(ccr_hbm, ccr_vm)
            pltpu.sync_copy(cci_hbm, cci_vm)
            cr_ref[...] = jnp.zeros_like(cr_ref)
            ci_ref[...] = jnp.zeros_like(ci_ref)

        u = ua_ref[...]
        ub = u.astype(BF)
        sr_ref[...] = jnp.dot(ub, bbr_vm[...], preferred_element_type=F32)
        si_ref[...] = jnp.dot(ub, bbi_vm[...], preferred_element_type=F32)
        _cscan(sr_ref, si_ref, con_ref, cr_ref, ci_ref, reverse=False)
        y = _mm_nt(sr_ref[...], ccr_vm[...]) - _mm_nt(si_ref[...], cci_vm[...]) + dsk_ref[...] * u
        y_ref[...] = y
        zg = jax.nn.gelu(y)
        zg_ref[...] = zg.astype(BF)
        q = _mm(zg, wg_ref[...]) + bg_ref[...]
        ya_ref[...] = (zg * _sig(q)).astype(BF)

    return _pallas_call(
        body, carry, name="s5_fwd", grid=(L // TM,),
        in_specs=[_tok(S5W), ANY, ANY, ANY, ANY, _full((1, S5W)), _full((8 * SUB, GN)),
                  _full((S5W, S5W)), _full((1, S5W))],
        out_specs=[_tok(GN), _tok(GN), _tok(S5W), _tok(S5W), _tok(S5W)],
        out_shape=[_sds((L, GN)), _sds((L, GN)), _sds((L, S5W)), _sds((L, S5W), BF), _sds((L, S5W), BF)],
        scratch_shapes=[pltpu.VMEM((S5W, GN), BF), pltpu.VMEM((S5W, GN), BF), pltpu.VMEM((S5W, GN), BF),
                        pltpu.VMEM((S5W, GN), BF),pltpu.VMEM((SUB, GN), F32), pltpu.VMEM((SUB, GN), F32)],
        compiler_params=_params(44),
    )(ua, bbr, bbi, ccr, cci, dsk, con, w_glu, b_glu)


def _s5_bwd(dya, y, ua, sr, si, bbr, bbi, ccr, cci, dsk, con_rev, w_glu, b_glu, carry=None):
    L = ua.shape[0]
    nt = L // TM
    spt = TM // SUB
    n_slab = spt

    def halo_map(i):
        return (jnp.maximum((nt - 1 - i) * spt - 1, 0), 0)

    def body(dya_ref, y_ref, ua_ref, sr_ref, si_ref, hr_ref, hi_ref, bbr_hbm, bbi_hbm, ccr_hbm, cci_hbm,
             dsk_ref, con_ref, wg_ref, bg_ref,
             dua_ref, dq_ref, dy_ref, lr_ref, li_ref, da_ref, dsm_ref,
             bbr_vm, bbi_vm, ccr_vm, cci_vm, cr_ref, ci_ref):
        i = pl.program_id(0)

        @pl.when(i == 0)
        def _():
            pltpu.sync_copy(bbr_hbm, bbr_vm)
            pltpu.sync_copy(bbi_hbm, bbi_vm)
            pltpu.sync_copy(ccr_hbm, ccr_vm)
            pltpu.sync_copy(cci_hbm, cci_vm)
            cr_ref[...] = jnp.zeros_like(cr_ref)
            ci_ref[...] = jnp.zeros_like(ci_ref)
            da_ref[...] = jnp.zeros_like(da_ref)
            dsm_ref[...] = jnp.zeros_like(dsm_ref)

        u = ua_ref[...]
        yv = y_ref[...]
        dya = dya_ref[...]
        zg = jax.nn.gelu(yv)
        sg = _sig(_mm(zg, wg_ref[...]) + bg_ref[...])
        dq = dya * zg * sg * (1.0 - sg)
        dq_ref[...] = dq.astype(BF)
        dzg = dya * sg + _mm_nt(dq, wg_ref[...])
        dy = dzg * _gelu_grad(yv)
        dyb = dy.astype(BF)
        dy_ref[...] = dyb
        dsm_ref[0:1, :] += _colsum(dy * u)
        dsm_ref[1:2, :] += _colsum(dq)
        lr_ref[...] = jnp.dot(dyb, ccr_vm[...], preferred_element_type=F32)
        li_ref[...] = -jnp.dot(dyb, cci_vm[...], preferred_element_type=F32)
        _cscan(lr_ref, li_ref, con_ref, cr_ref, ci_ref, reverse=True)

        first_tile = (i == nt - 1)
        row = _row_iota(LC)
        for lc in range(GN // LC):
            cols = slice(lc * LC, (lc + 1) * LC)
            h_r = jnp.where(first_tile, 0.0, hr_ref[:, cols])
            h_i = jnp.where(first_tile, 0.0, hi_ref[:, cols])

            def step(k, acc, cols=cols, h_r=h_r, h_i=h_i):
                ar, ai = acc
                rows = _slab(k)
                prev = _slab(jnp.maximum(k - 1, 0))
                pr = jnp.where(k == 0, h_r, sr_ref[prev, cols])
                pi = jnp.where(k == 0, h_i, si_ref[prev, cols])
                spr = pltpu.roll(jnp.where(row == SUB - 1, pr, sr_ref[rows, cols]), 1, 0)
                spi = pltpu.roll(jnp.where(row == SUB - 1, pi, si_ref[rows, cols]), 1, 0)
                lr, li = lr_ref[rows, cols], li_ref[rows, cols]
                return ar + lr * spr + li * spi, ai + li * spr - lr * spi

            zero = jnp.zeros((SUB, LC), F32)
            ar, ai = lax.fori_loop(0, n_slab, step, (zero, zero))
            da_ref[0:1, cols] += _colsum(ar)
            da_ref[1:2, cols] += _colsum(ai)

        dua_ref[...] = (dy * dsk_ref[...] + _mm_nt(lr_ref[...], bbr_vm[...]) + _mm_nt(li_ref[...], bbi_vm[...]))

    return _pallas_call(
        body, carry, name="s5_bwd", grid=(nt,),
        in_specs=[_tok_rev(S5W, nt), _tok_rev(S5W, nt), _tok_rev(S5W, nt), _tok_rev(GN, nt), _tok_rev(GN, nt),
                  pl.BlockSpec((SUB, GN), halo_map), pl.BlockSpec((SUB, GN), halo_map),
                  ANY, ANY, ANY, ANY, _full((1, S5W)), _full((8 * SUB, GN)), _full((S5W, S5W)), _full((1, S5W))],
        out_specs=[_tok_rev(S5W, nt), _tok_rev(S5W, nt), _tok_rev(S5W, nt), _tok_rev(GN, nt), _tok_rev(GN, nt),
                   _full((SUB, GN)), _full((SUB, S5W))],
        out_shape=[_sds((L, S5W)), _sds((L, S5W), BF), _sds((L, S5W), BF), _sds((L, GN)), _sds((L, GN)),
                   _sds((SUB, GN)), _sds((SUB, S5W))],
        scratch_shapes=[pltpu.VMEM((S5W, GN), BF), pltpu.VMEM((S5W, GN), BF), pltpu.VMEM((S5W, GN), BF),
                        pltpu.VMEM((S5W, GN), BF),pltpu.VMEM((SUB, GN), F32), pltpu.VMEM((SUB, GN), F32)],
        compiler_params=_params(52),
    )(dya, y, ua, sr, si, sr, si, bbr, bbi, ccr, cci, dsk, con_rev, w_glu, b_glu)


def _lru_gate_terms(rg, sp):
    log_a = -LRU_C * rg * sp
    a = jnp.exp(log_a)
    mult = jnp.sqrt(_neg_expm1(2.0 * log_a))
    return a, mult


def _lru_fwd(ub, conv_w, conv_b, wr, wi, b_r, b_i, sp, carry=None):
    L = ub.shape[0]
    n_slab = TM // SUB

    def body(ub_ref, cw_ref, cb_ref, wr_ref, wi_ref, br_ref, bi_ref, sp_ref,
             xc_ref, rg_ref, ig_ref, h_ref, hp_ref, a_ref, halo_ref, carry_ref):
        @pl.when(pl.program_id(0) == 0)
        def _():
            halo_ref[...] = jnp.zeros_like(halo_ref)
            carry_ref[...] = jnp.zeros_like(carry_ref)

        row = _row_iota(LW)
        taps = [cw_ref[k:k + 1, :] for k in range(4)]
        cb = cb_ref[...]

        def conv_step(k, prev):
            rows = _slab(k)
            cur = ub_ref[rows, :]
            acc = taps[3] * cur + cb
            for j in (1, 2, 3):
                acc = acc + taps[3 - j] * pltpu.roll(jnp.where(row >= SUB - j, prev, cur), j, 0)
            xc_ref[rows, :] = acc
            return cur

        halo_ref[...] = lax.fori_loop(0, n_slab, conv_step, halo_ref[...])

        xc = xc_ref[...]
        xcb = xc.astype(BF)
        rg = _sig(jnp.dot(xcb, wr_ref[...], preferred_element_type=F32) + br_ref[...])
        ig = _sig(jnp.dot(xcb, wi_ref[...], preferred_element_type=F32) + bi_ref[...])
        rg_ref[...] = rg
        ig_ref[...] = ig
        a, mult = _lru_gate_terms(rg, sp_ref[...])
        a_ref[...] = a
        h_ref[...] = mult * ig * xc

        rowc = _row_iota(LC)
        for lc in range(LW // LC):
            cols = slice(lc * LC, (lc + 1) * LC)

            def step(k, c, cols=cols):
                rows = _slab(k)
                av, b = a_ref[rows, cols], h_ref[rows, cols]
                for sh in (1, 2, 4):
                    keep = rowc >= sh
                    b = b + av * jnp.where(keep, pltpu.roll(b, sh, 0), 0.0)
                    av = av * jnp.where(keep, pltpu.roll(av, sh, 0), 1.0)
                h = b + av * c
                h_ref[rows, cols] = h
                hp_ref[rows, cols] = jnp.where(rowc == 0, c, pltpu.roll(h, 1, 0))
                return _bcast_row(h, SUB - 1)

            carry_ref[:, cols] = lax.fori_loop(0, n_slab, step, carry_ref[:, cols])

    return _pallas_call(
        body, carry, name="lru_fwd", grid=(L // TM,),
        in_specs=[_tok(LW), _full((4, LW)), _full((1, LW)), _full((LW, LW)), _full((LW, LW)),
                  _full((1, LW)), _full((1, LW)), _full((1, LW))],
        out_specs=[_tok(LW)] * 5,
        out_shape=[_sds((L, LW))] * 5,
        scratch_shapes=[pltpu.VMEM((TM, LW), F32), pltpu.VMEM((SUB, LW), F32), pltpu.VMEM((SUB, LW), F32)],
        compiler_params=_params(40),
    )(ub, conv_w, conv_b, wr, wi, b_r, b_i, sp)


def _lru_bwd(dyb, xc, rg, ig, hp, ub, conv_w, wr, wi, sp, dsp, carry=None):
    L = ub.shape[0]
    nt = L // TM
    spt = TM // SUB
    n_slab = spt

    def halo_map(i):
        return (jnp.maximum((nt - 1 - i) * spt - 1, 0), 0)

    def body(dh_ref, xc_ref, rg_ref, ig_ref, hp_ref, ub_ref, uh_ref, cw_ref, wr_ref, wi_ref, sp_ref, dsp_ref,
             dub_ref, dpr_ref, dpi_ref, acc_ref, a_ref, lam_ref, dxc_ref, carry_ref, next_ref):
        i = pl.program_id(0)

        @pl.when(i == 0)
        def _():
            carry_ref[...] = jnp.zeros_like(carry_ref)
            next_ref[...] = jnp.zeros_like(next_ref)
            acc_ref[...] = jnp.zeros_like(acc_ref)

        sp = sp_ref[...]
        rg, ig, xc = rg_ref[...], ig_ref[...], xc_ref[...]
        a, mult = _lru_gate_terms(rg, sp)
        a_ref[...] = a

        rowc = _row_iota(LC)
        for lc in range(LW // LC):
            cols = slice(lc * LC, (lc + 1) * LC)

            def step(k, c, cols=cols):
                rows = _slab(n_slab - 1 - k)
                av, dh = a_ref[rows, cols], dh_ref[rows, cols]
                b = av * dh
                for sh in (1, 2, 4):
                    keep = rowc < SUB - sh
                    b = b + av * jnp.where(keep, pltpu.roll(b, SUB - sh, 0), 0.0)
                    av = av * jnp.where(keep, pltpu.roll(av, SUB - sh, 0), 1.0)
                mu = b + av * c
                lam_ref[rows, cols] = dh + jnp.where(rowc == SUB - 1, c, pltpu.roll(mu, SUB - 1, 0))
                return _bcast_row(mu, 0)

            carry_ref[:, cols] = lax.fori_loop(0, n_slab, step, carry_ref[:, cols])

        lam = lam_ref[...]
        d_a = lam * hp_ref[...]
        d_mult = lam * ig * xc
        d_ig = lam * mult * xc
        dxc = lam * mult * ig
        d_log_a = d_a * a - d_mult * a * a / mult
        d_rg = (-LRU_C) * sp * d_log_a
        acc_ref[0:1, :] += _colsum((-LRU_C) * rg * d_log_a) * dsp_ref[...]
        dpr = d_rg * rg * (1.0 - rg)
        dpi = d_ig * ig * (1.0 - ig)
        acc_ref[1:2, :] += _colsum(dpr)
        acc_ref[2:3, :] += _colsum(dpi)
        dprb, dpib = dpr.astype(BF), dpi.astype(BF)
        dpr_ref[...] = dprb
        dpi_ref[...] = dpib
        dxc = dxc + _mm_nt(dprb, wr_ref[...]) + _mm_nt(dpib, wi_ref[...])
        dxc_ref[...] = dxc
        acc_ref[3:4, :] += _colsum(dxc)

        row = _row_iota(LW)
        taps = [cw_ref[k:k + 1, :] for k in range(4)]
        u_halo = jnp.where(i == nt - 1, 0.0, uh_ref[...])
        nxt_tile = next_ref[...]

        def conv_step(k, accs):
            rows = _slab(k)
            cur = dxc_ref[rows, :]
            nxt = jnp.where(k == n_slab - 1, nxt_tile, dxc_ref[_slab(jnp.minimum(k + 1, n_slab - 1)), :])
            ucur = ub_ref[rows, :]
            uprev = jnp.where(k == 0, u_halo, ub_ref[_slab(jnp.maximum(k - 1, 0)), :])
            du = taps[3] * cur
            new = [accs[3] + cur * ucur]
            for j in (1, 2, 3):
                du = du + taps[3 - j] * pltpu.roll(jnp.where(row < j, nxt, cur), SUB - j, 0)
                new.append(accs[3 - j] + cur * pltpu.roll(jnp.where(row >= SUB - j, uprev, ucur), j, 0))
            dub_ref[rows, :] = du
            return tuple(new[::-1])

        zero = jnp.zeros((SUB, LW), F32)
        accs = lax.fori_loop(0, n_slab, conv_step, (zero, zero, zero, zero))
        for k in range(4):
            acc_ref[4 + k:5 + k, :] += _colsum(accs[k])
        next_ref[...] = dxc_ref[0:SUB, :]

    return _pallas_call(
        body, carry, name="lru_bwd", grid=(nt,),
        in_specs=[_tok_rev(LW, nt)] * 6 + [pl.BlockSpec((SUB, LW), halo_map), _full((4, LW)),
                                           _full((LW, LW)), _full((LW, LW)), _full((1, LW)), _full((1, LW))],
        out_specs=[_tok_rev(LW, nt), _tok_rev(LW, nt), _tok_rev(LW, nt), _full((SUB, LW))],
        out_shape=[_sds((L, LW)), _sds((L, LW), BF), _sds((L, LW), BF), _sds((SUB, LW))],
        scratch_shapes=[pltpu.VMEM((TM, LW), F32), pltpu.VMEM((TM, LW), F32), pltpu.VMEM((TM, LW), F32),
                        pltpu.VMEM((SUB, LW), F32), pltpu.VMEM((SUB, LW), F32)],
        compiler_params=_params(48),
    )(dyb, xc, rg, ig, hp, ub, ub, conv_w, wr, wi, sp, dsp)


AC = D // NCHIP


def _merge_fwd(x, ya, yb, gp, w_a, w_b, w_o, carry=None):
    L = x.shape[0]

    def body(x_ref, ya_ref, yb_ref, gp_ref, wa_ref, wb_ref, wo_ref, x1_ref, pa_ref, pb_ref, mg_ref):
        ya = ya_ref[...]
        for k in range(NCHIP):
            pa_ref[:, k * AC:(k + 1) * AC] = jnp.dot(ya, wa_ref[k], preferred_element_type=F32)
        pb = _mm(yb_ref[...], wb_ref[...])
        pb_ref[...] = pb
        gp = gp_ref[...]
        merged = (_sig(gp[:, :D]) * pa_ref[...] + _sig(gp[:, D:]) * pb).astype(BF)
        mg_ref[...] = merged
        x1_ref[...] = x_ref[...] + jnp.dot(merged, wo_ref[...], preferred_element_type=F32)

    return _pallas_call(
        body, carry, name="merge_fwd", grid=(L // TM,),
        in_specs=[_tok(D), _tok(S5W), _tok(LW), _tok(2 * D), _full((NCHIP, S5W, AC)), _full((LW, D)), _full((D, D))],
        out_specs=[_tok(D), _tok(D), _tok(D), _tok(D)],
        out_shape=[_sds((L, D)), _sds((L, D)), _sds((L, D)), _sds((L, D), BF)],
        compiler_params=_params(40),
    )(x, ya, yb, gp, w_a, w_b, w_o)


def _merge_bwd(dx1, gp, pa, pb, w_a, w_b, w_o, carry=None):
    L = dx1.shape[0]

    def body(dx1_ref, gp_ref, pa_ref, pb_ref, wa_ref, wb_ref, wo_ref, dya_ref, dyb_ref, dgp_ref, dpa_ref, dpb_ref):
        dm = _mm_nt(dx1_ref[...], wo_ref[...])
        gp = gp_ref[...]
        sa, sb = _sig(gp[:, :D]), _sig(gp[:, D:])
        dpa = (dm * sa).astype(BF)
        dpb = (dm * sb).astype(BF)
        dpa_ref[...] = dpa
        dpb_ref[...] = dpb
        dgp_ref[:, :D] = dm * pa_ref[...] * sa * (1.0 - sa)
        dgp_ref[:, D:] = dm * pb_ref[...] * sb * (1.0 - sb)
        dya = jnp.zeros((TM, S5W), F32)
        for k in range(NCHIP):
            dya = dya + _mm_nt(dpa[:, k * AC:(k + 1) * AC], wa_ref[k])
        dya_ref[...] = dya
        dyb_ref[...] = _mm_nt(dpb, wb_ref[...])

    return _pallas_call(
        body, carry, name="merge_bwd", grid=(L // TM,),
        in_specs=[_tok(D), _tok(2 * D), _tok(D), _tok(D), _full((NCHIP, S5W, AC)), _full((LW, D)), _full((D, D))],
        out_specs=[_tok(S5W), _tok(LW), _tok(2 * D), _tok(D), _tok(D)],
        out_shape=[_sds((L, S5W)), _sds((L, LW)), _sds((L, 2 * D)), _sds((L, D), BF), _sds((L, D), BF)],
        compiler_params=_params(40),
    )(dx1, gp, pa, pb, w_a, w_b, w_o)


def _chunk_tok(width):
    return pl.BlockSpec((NCHIP, TM, width), lambda i: (0, i, 0))


def _ffn_fwd(x1, g_ffn, wg, wu, wd, carry=None):
    L = x1.shape[0]

    def body(x_ref, g_ref, wg_hbm, wu_hbm, wd_hbm, x2_ref, h2_ref, gg_ref, uu_ref, wg_vm, wu_vm, wd_vm):
        @pl.when(pl.program_id(0) == 0)
        def _():
            pltpu.sync_copy(wg_hbm, wg_vm)
            pltpu.sync_copy(wu_hbm, wu_vm)
            pltpu.sync_copy(wd_hbm, wd_vm)

        x = x_ref[...]
        xh, _ = _rms(x)
        h2 = (xh * g_ref[...]).astype(BF)
        h2_ref[...] = h2
        out = x
        for c in range(NCHIP):
            gg = lax.dot_general(h2, wg_vm[c], (((1,), (1,)), ((), ())), preferred_element_type=F32)
            uu = lax.dot_general(h2, wu_vm[c], (((1,), (1,)), ((), ())), preferred_element_type=F32)
            gg_ref[c] = gg.astype(BF)
            uu_ref[c] = uu.astype(BF)
            act = (gg * _sig(gg) * uu).astype(BF)
            out = out + jnp.dot(act, wd_vm[c], preferred_element_type=F32)
        x2_ref[...] = out

    return _pallas_call(
        body, carry, name="ffn_fwd", grid=(L // TM,),
        in_specs=[_tok(D), _full((1, D)), ANY, ANY, ANY],
        out_specs=[_tok(D), _tok(D), _chunk_tok(FC), _chunk_tok(FC)],
        out_shape=[_sds((L, D)), _sds((L, D), BF), _sds((NCHIP, L, FC), BF), _sds((NCHIP, L, FC), BF)],
        scratch_shapes=[pltpu.VMEM((NCHIP, FC, D), BF)] * 3,
        compiler_params=_params(52),
    )(x1, g_ffn, wg, wu, wd)


def _ffn_bwd(x1, dx2, gg, uu, g_ffn, wg, wu, wd, carry=None):
    L = x1.shape[0]

    def body(x_ref, dx2_ref, gg_ref, uu_ref, g_ref, wg_hbm, wu_hbm, wd_hbm,
             dx1_ref, act_ref, dgg_ref, duu_ref, dg_ref, wg_vm, wu_vm, wd_vm):
        @pl.when(pl.program_id(0) == 0)
        def _():
            pltpu.sync_copy(wg_hbm, wg_vm)
            pltpu.sync_copy(wu_hbm, wu_vm)
            pltpu.sync_copy(wd_hbm, wd_vm)
            dg_ref[...] = jnp.zeros_like(dg_ref)

        dx2 = dx2_ref[...]
        dx2b = dx2.astype(BF)
        dh2 = jnp.zeros((TM, D), F32)
        for c in range(NCHIP):
            g = gg_ref[c].astype(F32)
            u = uu_ref[c].astype(F32)
            s = _sig(g)
            silu = g * s
            act_ref[c] = (silu * u).astype(BF)
            dact = lax.dot_general(dx2b, wd_vm[c], (((1,), (1,)), ((), ())), preferred_element_type=F32)
            dg = (dact * u * s * (1.0 + g * (1.0 - s))).astype(BF)
            du = (dact * silu).astype(BF)
            dgg_ref[c] = dg
            duu_ref[c] = du
            dh2 = dh2 + jnp.dot(dg, wg_vm[c], preferred_element_type=F32)
            dh2 = dh2 + jnp.dot(du, wu_vm[c], preferred_element_type=F32)
        xh, r = _rms(x_ref[...])
        dg_ref[0:1, :] += _colsum(dh2 * xh)
        dx1_ref[...] = dx2 + _rms_bwd(dh2, xh, r, g_ref[...])

    return _pallas_call(
        body, carry, name="ffn_bwd", grid=(L // TM,),
        in_specs=[_tok(D), _tok(D), _chunk_tok(FC), _chunk_tok(FC), _full((1, D)), ANY, ANY, ANY],
        out_specs=[_tok(D), _chunk_tok(FC), _chunk_tok(FC), _chunk_tok(FC), _full((SUB, D))],
        out_shape=[_sds((L, D)), _sds((NCHIP, L, FC), BF), _sds((NCHIP, L, FC), BF), _sds((NCHIP, L, FC), BF),
                   _sds((SUB, D))],
        scratch_shapes=[pltpu.VMEM((NCHIP, FC, D), BF)] * 3,
        compiler_params=_params(56),
    )(x1, dx2, gg, uu, g_ffn, wg, wu, wd)


def _ple_loss(x2, p, tgt, g_pg, w_pg, b_pg, w_ple, g_ple, g_final):
    L = x2.shape[0]

    def body(x2_ref, p_ref, t_ref, gpg_ref, wpg_ref, bpg_ref, wple_ref, gple_ref, gf_ref,
             dx2_ref, n2_ref, dpre_ref, de0_ref, acc_ref):
        @pl.when(pl.program_id(0) == 0)
        def _():
            acc_ref[...] = jnp.zeros_like(acc_ref)

        x2 = x2_ref[...]
        x2h, r2 = _rms(x2)
        n2 = (x2h * gpg_ref[...]).astype(BF)
        n2_ref[...] = n2
        gate = _sig(jnp.dot(n2, wpg_ref[...], preferred_element_type=F32) + bpg_ref[...])
        pb = p_ref[...].astype(BF)
        e0 = jnp.concatenate([jnp.dot(pb, wple_ref[k], preferred_element_type=F32) for k in range(NCHIP)], axis=1)
        e0h, re = _rms(e0)
        e = e0h * gple_ref[...]
        x3 = x2 + gate * e
        x3h, r3 = _rms(x3)
        diff = x3h * gf_ref[...] - t_ref[...]
        acc_ref[4:5, :] += _colsum(diff * diff) * (0.5 / D)
        dy = diff * (1.0 / D)
        acc_ref[3:4, :] += _colsum(dy * x3h)
        dx3 = _rms_bwd(dy, x3h, r3, gf_ref[...])
        de = dx3 * gate
        acc_ref[2:3, :] += _colsum(de * e0h)
        de0_ref[...] = _rms_bwd(de, e0h, re, gple_ref[...]).astype(BF)
        dpre = dx3 * e * gate * (1.0 - gate)
        acc_ref[1:2, :] += _colsum(dpre)
        dpreb = dpre.astype(BF)
        dpre_ref[...] = dpreb
        dn2 = lax.dot_general(dpreb, wpg_ref[...], (((1,), (1,)), ((), ())), preferred_element_type=F32)
        acc_ref[0:1, :] += _colsum(dn2 * x2h)
        dx2_ref[...] = dx3 + _rms_bwd(dn2, x2h, r2, gpg_ref[...])

    return _pallas_call(
        body, name="ple_loss", grid=(L // TM,),
        in_specs=[_tok(D), _tok(PLE), _tok(D), _full((1, D)), _full((D, D)), _full((1, D)), _full((NCHIP, PLE, AC)),
                  _full((1, D)), _full((1, D))],
        out_specs=[_tok(D), _tok(D), _tok(D), _tok(D), _full((SUB, D))],
        out_shape=[_sds((L, D)), _sds((L, D), BF), _sds((L, D), BF), _sds((L, D), BF), _sds((SUB, D))],
        compiler_params=_params(40),
    )(x2, p, tgt, g_pg, w_pg, b_pg, w_ple, g_ple, g_final)


def _tn(name, a, b, col_chunk=None, carry=None):
    L = a.shape[-2]
    m, n = a.shape[-1], b.shape[-1]
    if a.ndim == 3 or b.ndim == 3:
        nj, bn = (a if a.ndim == 3 else b).shape[0], n
        a_spec = (pl.BlockSpec((None, TK, m), lambda j, t: (j, t, 0)) if a.ndim == 3
                  else pl.BlockSpec((TK, m), lambda j, t: (t, 0)))
        b_spec = (pl.BlockSpec((None, TK, n), lambda j, t: (j, t, 0)) if b.ndim == 3
                  else pl.BlockSpec((TK, n), lambda j, t: (t, 0)))
        out_spec, out_shape = pl.BlockSpec((None, m, n), lambda j, t: (j, 0, 0)), _sds((nj, m, n))
    else:
        bn = col_chunk
        if bn is None:
            bn = next((cand for cand in (1024, 512) if n > cand and n % cand == 0), n)
        nj = n // bn
        a_spec = pl.BlockSpec((TK, m), lambda j, t: (t, 0))
        b_spec = pl.BlockSpec((TK, bn), lambda j, t: (t, j))
        if col_chunk is None:
            out_spec, out_shape = pl.BlockSpec((m, bn), lambda j, t: (0, j)), _sds((m, n))
        else:
            out_spec, out_shape = pl.BlockSpec((None, m, bn), lambda j, t: (j, 0, 0)), _sds((nj, m, bn))

    def body(a_ref, b_ref, o_ref):
        @pl.when(pl.program_id(1) == 0)
        def _():
            o_ref[...] = jnp.zeros_like(o_ref)

        o_ref[...] += _mm_tn(a_ref[...], b_ref[...])

    outs = _pallas_call(
        body, carry, name=name, grid=(nj, L // TK), in_specs=[a_spec, b_spec], out_specs=[out_spec],
        out_shape=[out_shape],
        compiler_params=pltpu.CompilerParams(dimension_semantics=("arbitrary", "arbitrary"),
                                             vmem_limit_bytes=40 * VMEM_MB),
    )(a, b)
    return outs[0] if carry is None else outs


LANE = 128


def _tn_blocks(name, a, b, ga, gb):
    L, n = a.shape[0], b.shape[1]
    per = LANE // ga
    wb = per * gb

    def body(a_ref, b_ref, o_ref, acc_ref):
        t = pl.program_id(1)

        @pl.when(t == 0)
        def _():
            acc_ref[...] = jnp.zeros_like(acc_ref)

        acc_ref[...] += _mm_tn(a_ref[...], b_ref[...])

        @pl.when(t == L // TK - 1)
        def _():
            rows = lax.broadcasted_iota(jnp.int32, (LANE, wb), 0) // ga
            cols = lax.broadcasted_iota(jnp.int32, (LANE, wb), 1) // gb
            kept = jnp.where(rows == cols, acc_ref[...], 0.0)
            o_ref[...] = jnp.sum(kept.reshape(per, ga, wb), axis=0)

    return _pallas_call(
        body, name=name, grid=(n // wb, L // TK),
        in_specs=[pl.BlockSpec((TK, LANE), lambda j, t: (t, j)), pl.BlockSpec((TK, wb), lambda j, t: (t, j))],
        out_specs=pl.BlockSpec((ga, wb), lambda j, t: (0, j)), out_shape=_sds((ga, n)),
        scratch_shapes=[pltpu.VMEM((LANE, wb), F32)],
        compiler_params=pltpu.CompilerParams(dimension_semantics=("arbitrary", "arbitrary"),
                                             vmem_limit_bytes=32 * VMEM_MB),
    )(a, b)


def _s5_discretize(lam_re, lam_im, log_dt, b_re, b_im):
    dt = jnp.exp(log_dt)[:, None]
    mag = jnp.exp(lam_re * dt)
    ar = mag * jnp.cos(lam_im * dt)
    ai = mag * jnp.sin(lam_im * dt)
    den = lam_re * lam_re + lam_im * lam_im
    nr = ar - 1.0
    fr = (nr * lam_re + ai * lam_im) / den
    fi = (ai * lam_re - nr * lam_im) / den
    bbr = fr[:, None, :] * b_re - fi[:, None, :] * b_im
    bbi = fr[:, None, :] * b_im + fi[:, None, :] * b_re
    return ar, ai, bbr, bbi


def _prepare(by_rows, block_cols, ar, ai):
    n = len(by_rows)

    def body(*refs):
        srcs, (ar_ref, ai_ref), dense, (con_ref, rev_ref) = refs[:n], refs[n:n + 2], refs[n + 2:2 * n + 2], refs[2 * n + 2:]
        for src, out, c in zip(srcs, dense, block_cols):
            r, width = src.shape
            groups = width // c
            tiled = jnp.broadcast_to(src[...][None], (groups, r, width)).reshape(groups * r, width)
            own = (lax.broadcasted_iota(jnp.int32, tiled.shape, 0) // r) == (lax.broadcasted_iota(jnp.int32, tiled.shape, 1) // c)
            out[...] = jnp.where(own, tiled, 0.0).astype(BF)
        a_r, a_i = ar_ref[...], ai_ref[...]
        pw = [(jnp.ones_like(a_r), jnp.zeros_like(a_i))]
        for _ in range(SUB):
            pr, pi = pw[-1]
            pw.append((pr * a_r - pi * a_i, pr * a_i + pi * a_r))
        row = _row_iota(GN)
        for ref, reverse in ((con_ref, False), (rev_ref, True)):
            sign = -1.0 if reverse else 1.0
            for j, sh in enumerate((1, 2, 4)):
                keep = (row < SUB - sh) if reverse else (row >= sh)
                ref[2 * j * SUB:(2 * j + 1) * SUB, :] = jnp.where(keep, pw[sh][0], 0.0)
                ref[(2 * j + 1) * SUB:(2 * j + 2) * SUB, :] = jnp.where(keep, sign * pw[sh][1], 0.0)
            p_r, p_i = jnp.zeros((SUB, GN), F32), jnp.zeros((SUB, GN), F32)
            for i in range(SUB):
                k = SUB - i if reverse else i + 1
                p_r = jnp.where(row == i, pw[k][0], p_r)
                p_i = jnp.where(row == i, sign * pw[k][1], p_i)
            ref[6 * SUB:7 * SUB, :] = p_r
            ref[7 * SUB:8 * SUB, :] = p_i

    dense_shapes = [(b.shape[1] // c * b.shape[0], b.shape[1]) for b, c in zip(by_rows, block_cols)]
    outs = _pallas_call(
        body, name="prepare", grid=(1,), in_specs=[_full(b.shape) for b in by_rows] + [_full((1, GN))] * 2,
        out_specs=[_full(s) for s in dense_shapes] + [_full((8 * SUB, GN))] * 2,
        out_shape=[_sds(s, BF) for s in dense_shapes] + [_sds((8 * SUB, GN))] * 2,
        compiler_params=_params(48),
    )(*by_rows, ar, ai)
    return outs[:n], outs[n], outs[n + 1]


def _local_step(x, p, tgt, w, comm):
    rows_of = lambda a: a.reshape(NCHIP * a.shape[1], a.shape[2])
    quarters = lambda a: a.reshape(NCHIP, a.shape[0] // NCHIP, a.shape[1])

    def gathering(names, call):
        carry = comm.gather(names)
        outs = list(call(carry))
        own = len(outs) - len(carry.out_shapes)
        w.update(zip(names, outs[own:]))
        return outs[:own]

    w.update(comm.first())
    w_glu = rows_of(w["w_glu"])
    ar, ai, bbr, bbi = _s5_discretize(w["lam_re"], w["lam_im"], w["log_dt"], w["s5_b_re"], w["s5_b_im"])
    by_row = lambda b: jnp.transpose(b, (1, 0, 2)).reshape(b.shape[1], -1)
    (bbr_d, bbi_d, ccr_d, cci_d, wr_d, wi_d), con, con_rev = _prepare(
        [by_row(b) for b in (bbr, bbi, w["s5_c_re"], w["s5_c_im"], w["w_r"], w["w_i"])], [NS] * 4 + [HD] * 2,
        ar.reshape(1, GN), ai.reshape(1, GN))
    dsk = w["s5_d"].reshape(1, S5W)
    lam = w["lru_lambda"].reshape(1, LW)
    sp = jax.nn.softplus(-lam)
    b_r, b_i = w["b_r"].reshape(1, LW), w["b_i"].reshape(1, LW)
    row = lambda name: w[name].reshape(1, -1)

    h, ua, ub, gp = gathering(["w_a_out", "w_b_out", "w_o"], lambda carry: _inproj_fwd(
        x, row("g_mix"), w["w_in"], row("b_in"), carry))
    sr, si, y, zg, ya = gathering(["w_ffn_gate"], lambda carry: _s5_fwd(
        ua, bbr_d, bbi_d, ccr_d, cci_d, dsk, con, w_glu, row("b_glu"), carry))
    xc, rg, ig, yb, hp = gathering(["w_ffn_up"], lambda carry: _lru_fwd(
        ub, w["conv_w"], row("conv_b"), wr_d, wi_d, b_r, b_i, sp, carry))
    w_b_out, w_o = rows_of(w["w_b_out"]), rows_of(w["w_o"])
    x1, pa, pb, merged = gathering(["w_ffn_down"], lambda carry: _merge_fwd(
        x, ya, yb, gp, w["w_a_out"], w_b_out, w_o, carry))
    x2, h2, gg, uu = gathering(["w_ple_gate", "w_ple"], lambda carry: _ffn_fwd(
        x1, row("g_ffn"), w["w_ffn_gate"], w["w_ffn_up"], w["w_ffn_down"], carry))
    w_pg = rows_of(w["w_ple_gate"])
    dx2, n2, dpre, de0, acc_p = _ple_loss(x2, p, tgt, row("g_ple_gate"), w_pg, row("b_ple_gate"),
                                          w["w_ple"], row("g_ple"), row("g_final"))
    comm.reduce("ple", {"w_ple_gate": quarters(_tn("dw_ple_gate", n2, dpre)),
                        "w_ple": _tn("dw_ple", p, de0, col_chunk=AC)})
    dx1, act, dgg, duu, acc_f = comm.run(lambda carry: _ffn_bwd(
        x1, dx2, gg, uu, row("g_ffn"), w["w_ffn_gate"], w["w_ffn_up"], w["w_ffn_down"], carry))
    comm.reduce("ffn_gate", {"w_ffn_gate": _tn("dw_ffn_gate", dgg, h2)})
    comm.reduce("ffn_up", {"w_ffn_up": comm.run(lambda carry: _tn("dw_ffn_up", duu, h2, carry=carry))[0]})
    comm.reduce("ffn_down", {"w_ffn_down": comm.run(lambda carry: _tn("dw_ffn_down", act, dx2, carry=carry))[0]})
    dya, dyb, dgp, dpa, dpb = comm.run(lambda carry: _merge_bwd(
        dx1, gp, pa, pb, w["w_a_out"], w_b_out, w_o, carry))
    comm.reduce("merge", {"w_o": quarters(_tn("dw_o", merged, dx1)), "w_a_out": _tn("dw_a_out", ya, dpa, col_chunk=AC),
                          "w_b_out": quarters(_tn("dw_b_out", yb, dpb))})
    dua, dq, dy, lr, li, acc_a, acc_s = comm.run(lambda carry: _s5_bwd(
        dya, y, ua, sr, si, bbr_d, bbi_d, ccr_d, cci_d, dsk, con_rev, w_glu, row("b_glu"), carry))
    dub, dpr, dpi, acc_l = comm.run(lambda carry: _lru_bwd(
        dyb, xc, rg, ig, hp, ub, w["conv_w"], wr_d, wi_d, sp, -_sig(-lam), carry))
    gx, dz, acc_g, acc_b = comm.run(lambda carry: _inproj_bwd(
        x, dx1, dua, dub, dgp, row("g_mix"), w["w_in"], carry))
    comm.reduce("in", {"w_in": _tn("dw_in", h, dz, col_chunk=QC), "w_glu": quarters(_tn("dw_glu", zg, dq))})
    comm.drain()
    sums = {"ple": acc_p, "ffn": acc_f, "mix": acc_g, "b_in": acc_b, "lru": acc_l, "s5": acc_s, "s5_a": acc_a}
    blocks = {
        "bb_re": _tn_blocks("d_bbr", ua, lr, NP, NS),
        "bb_im": _tn_blocks("d_bbi", ua, li, NP, NS),
        "cc_re": _tn_blocks("d_ccr", dy, sr, NP, NS),
        "cc_im": _tn_blocks("d_cci", dy, si, NP, NS),
        "w_r": _tn_blocks("dw_r", xc, dpr, HD, HD),
        "w_i": _tn_blocks("dw_i", xc, dpi, HD, HD),
    }
    return gx, sums, blocks


def _replicated_grads(w, sums, blocks):
    grouped = lambda e, groups: jnp.transpose(e.reshape(e.shape[0], groups, -1), (1, 0, 2))
    d_ar, d_ai = sums["s5_a"][0].reshape(NG, NS), sums["s5_a"][1].reshape(NG, NS)
    d_bbr, d_bbi = grouped(blocks["bb_re"], NG), grouped(blocks["bb_im"], NG)
    _, vjp = jax.vjp(_s5_discretize, w["lam_re"], w["lam_im"], w["log_dt"], w["s5_b_re"], w["s5_b_im"])
    g = dict(zip(("lam_re", "lam_im", "log_dt", "s5_b_re", "s5_b_im"), vjp((d_ar, d_ai, d_bbr, d_bbi))))
    g["s5_c_re"] = grouped(blocks["cc_re"], NG)
    g["s5_c_im"] = -grouped(blocks["cc_im"], NG)
    g["w_r"], g["w_i"] = grouped(blocks["w_r"], NH), grouped(blocks["w_i"], NH)
    g["s5_d"] = sums["s5"][0].reshape(NG, NP)
    g["b_r"] = sums["lru"][1].reshape(NH, HD)
    g["b_i"] = sums["lru"][2].reshape(NH, HD)
    return g


ACC_ROWS = {"g_mix": ("mix", 0), "b_in": ("b_in", 0), "g_ffn": ("ffn", 0), "g_ple_gate": ("ple", 0),
            "b_ple_gate": ("ple", 1), "g_ple": ("ple", 2), "g_final": ("ple", 3), "b_glu": ("s5", 1),
            "lru_lambda": ("lru", 0), "conv_b": ("lru", 3)}
LOSS_ROW = ("ple", 4)
CONV_W_ROWS = ("lru", 4)


SHARDED = [("w_in", (D, QC)), ("w_glu", (S5W // NCHIP, S5W)), ("w_a_out", (S5W, AC)), ("w_b_out", (LW // NCHIP, D)),
           ("w_o", (D // NCHIP, D)), ("w_ffn_gate", (FC, D)), ("w_ffn_up", (FC, D)), ("w_ffn_down", (FC, D)),
           ("w_ple_gate", (D // NCHIP, D)), ("w_ple", (PLE, AC))]
NSH = len(SHARDED)
TRANSPOSED = ("w_ffn_gate", "w_ffn_up", "s5_b_re", "s5_b_im")
CONV_SHARD = (4, LW // NCHIP)


def _mesh_pos():
    return lax.axis_index("x"), lax.axis_index("y"), lax.axis_index("c")


def _other_chips(x, y):
    return [(1 - x, y), (x, 1 - y), (1 - x, 1 - y)]


def _half_rows(c, rows, align):
    return pl.ds(pl.multiple_of(c * (rows // 2), align), rows // 2)


def _run_now(name, carry):
    c_in, c_out = len(carry.operands), len(carry.out_shapes)

    def body(*refs):
        ins, outs, sems = refs[:c_in], refs[c_in:c_in + c_out], refs[c_in + c_out:]
        carry.start(ins, outs, sems)
        carry.finish(ins, outs, sems)

    return pl.pallas_call(body, name=name, in_specs=[ANY] * c_in, out_specs=[ANY] * c_out,
                          out_shape=list(carry.out_shapes), scratch_shapes=list(carry.sems),
                          input_output_aliases=dict(carry.aliases))(*_in_hbm(carry.operands))


def _gather_group(shards, split):
    n = len(shards)

    def copies(srcs, outs, sems):
        send_sems, recv_sems = sems
        x, y, c = _mesh_pos()
        k0 = 2 * x + y
        sib = (x, y, 1 - c)
        chips = _other_chips(x, y)

        def remote(src, dst, j, i, to):
            return pltpu.make_async_remote_copy(src_ref=src, dst_ref=dst, send_sem=send_sems.at[j, i],
                                                recv_sem=recv_sems.at[j, i], device_id=to, device_id_type=MESH)

        def rows(ref, i, core, *lead):
            if not split[i]:
                return ref.at[lead] if lead else ref
            return ref.at[(*lead, _half_rows(core, shards[i].shape[0], 16))]

        own = [remote(s, o.at[k0], 6, i, sib) for i, (s, o) in enumerate(zip(srcs, outs))]
        ici, landed, fwd, fwd_landed = [], [], [], []
        for j, chip in enumerate(chips):
            kj = 2 * chip[0] + chip[1]
            pairs = list(enumerate(zip(srcs, outs)))
            ici.append([remote(rows(s, i, c), rows(o, i, c, k0), j, i, (*chip, c)) for i, (s, o) in pairs])
            landed.append([remote(rows(s, i, c), rows(o, i, c, kj), j, i, (*chip, c)) for i, (s, o) in pairs])
            fwd.append([remote(rows(o, i, c, kj), rows(o, i, c, kj), 3 + j, i, sib) for i, (s, o) in pairs if split[i]])
            fwd_landed.append([remote(rows(o, i, 1 - c, kj), rows(o, i, 1 - c, kj), 3 + j, i, sib)
                               for i, (s, o) in pairs if split[i]])
        return own, ici, landed, fwd, fwd_landed

    def start(srcs, outs, sems):
        own, ici, _, _, _ = copies(srcs, outs, sems)
        for cp in own + [cp for per_chip in ici for cp in per_chip]:
            cp.start()

    def finish(srcs, outs, sems):
        own, ici, landed, fwd, fwd_landed = copies(srcs, outs, sems)
        passed = [i for i in range(n) if split[i]]
        for j in range(3):
            for i, cp in enumerate(landed[j]):
                cp.wait_recv()
                if split[i]:
                    fwd[j][passed.index(i)].start()
        for j in range(3):
            for cp in fwd_landed[j]:
                cp.wait_recv()
        for cp in own:
            cp.wait_recv()
        for cp in own + [cp for per_chip in ici + fwd for cp in per_chip]:
            cp.wait_send()

    return _Carried(shards, [_sds((NCHIP,) + s.shape, s.dtype) for s in shards],
                    [pltpu.SemaphoreType.DMA((7, n)), pltpu.SemaphoreType.DMA((7, n))], start, finish)


def _each_copy(copies, carried, out_shapes, sems, aliases=None):
    def start(ins, outs, sem_refs):
        for cp in copies(ins, outs, sem_refs):
            cp.start()

    def finish(ins, outs, sem_refs):
        for cp in copies(ins, outs, sem_refs):
            cp.wait()

    return _Carried(carried, out_shapes, sems, start, finish, aliases)


def _swap_group(grads):
    n = len(grads)

    def copies(srcs, outs, sems):
        send_sems, recv_sems = sems
        x, y, c = _mesh_pos()
        return [pltpu.make_async_remote_copy(src_ref=s.at[:, _half_rows(1 - c, s.shape[1], 8)], dst_ref=o,
                                             send_sem=send_sems.at[i], recv_sem=recv_sems.at[i], device_id=(x, y, 1 - c),
                                             device_id_type=MESH) for i, (s, o) in enumerate(zip(srcs, outs))]

    return _each_copy(copies, grads, [_sds((NCHIP, g.shape[1] // 2, g.shape[2])) for g in grads],
                      [pltpu.SemaphoreType.DMA((n,)), pltpu.SemaphoreType.DMA((n,))])


def _add_sibling_group(tag, c_idx, grads, gots):
    n = len(grads)

    def body(c_ref, *refs):
        for g, rx, p, pb in zip(refs[:n], refs[n:2 * n], refs[2 * n:3 * n], refs[3 * n:]):
            s = g[...] + rx[...]
            p[...] = s
            pb[...] = s.astype(BF)

    halves = [pl.BlockSpec((None,) + rx.shape[1:], lambda k, c_ref: (k, 0, 0)) for rx in gots]
    mine = [pl.BlockSpec((None,) + rx.shape[1:], lambda k, c_ref: (k, c_ref[0], 0)) for rx in gots]
    outs = _pallas_call(
        body, name="add_sibling_" + tag,
        grid_spec=pltpu.PrefetchScalarGridSpec(num_scalar_prefetch=1, grid=(NCHIP,), in_specs=mine + halves,
                                               out_specs=halves + halves),
        out_shape=[_sds(rx.shape) for rx in gots] + [_sds(rx.shape, BF) for rx in gots],
        compiler_params=_params(48),
    )(c_idx, *_in_hbm(list(grads) + list(gots)))
    return outs[:n], outs[n:]


def _exchange_group(parts):
    n = len(parts)

    def copies(srcs, outs, sems):
        send_sems, recv_sems = sems
        x, y, c = _mesh_pos()
        return [pltpu.make_async_remote_copy(
            src_ref=s.at[2 * chip[0] + chip[1]], dst_ref=o.at[j], send_sem=send_sems.at[j, i],
            recv_sem=recv_sems.at[j, i], device_id=(*chip, c), device_id_type=MESH)
            for j, chip in enumerate(_other_chips(x, y)) for i, (s, o) in enumerate(zip(srcs, outs))]

    return _each_copy(copies, parts, [_sds((3,) + p.shape[1:], BF) for p in parts],
                      [pltpu.SemaphoreType.DMA((3, n)), pltpu.SemaphoreType.DMA((3, n))])


def _add_chips_group(tag, kc_idx, parts, arrived):
    n = len(parts)

    def body(kc_ref, *refs):
        for p, rx, t in zip(refs[:n], refs[n:2 * n], refs[2 * n:]):
            t[...] = ((p[...] + rx[0].astype(F32)) + rx[1].astype(F32)) + rx[2].astype(F32)

    outs = _pallas_call(
        body, name="add_chips_" + tag,
        grid_spec=pltpu.PrefetchScalarGridSpec(
            num_scalar_prefetch=1, grid=(1,),
            in_specs=([pl.BlockSpec((None,) + rx.shape[1:], lambda i, kc_ref: (kc_ref[0], 0, 0)) for rx in arrived]
                      + [pl.BlockSpec(rx.shape, lambda i, kc_ref: (0, 0, 0)) for rx in arrived]),
            out_specs=[pl.BlockSpec((None,) + rx.shape[1:], lambda i, kc_ref: (kc_ref[1], 0, 0)) for rx in arrived]),
        out_shape=[_sds((2,) + rx.shape[1:]) for rx in arrived],
        compiler_params=_params(48),
    )(kc_idx, *_in_hbm(list(parts) + list(arrived)))
    return list(outs)


def _join_group(halves):
    n = len(halves)

    def copies(bufs, sems):
        send_sems, recv_sems = sems
        x, y, c = _mesh_pos()
        sib = (x, y, 1 - c)
        sends = [pltpu.make_async_remote_copy(src_ref=b.at[c], dst_ref=b.at[c], send_sem=send_sems.at[i],
                                              recv_sem=recv_sems.at[i], device_id=sib, device_id_type=MESH)
                 for i, b in enumerate(bufs)]
        landed = [pltpu.make_async_remote_copy(src_ref=b.at[c], dst_ref=b.at[1 - c], send_sem=send_sems.at[i],
                                               recv_sem=recv_sems.at[i], device_id=sib, device_id_type=MESH)
                  for i, b in enumerate(bufs)]
        return sends, landed

    def start(_, bufs, sems):
        for cp in copies(bufs, sems)[0]:
            cp.start()

    def finish(_, bufs, sems):
        sends, landed = copies(bufs, sems)
        for cp in landed:
            cp.wait_recv()
        for cp in sends:
            cp.wait_send()

    return _Carried(halves, [_sds(h.shape) for h in halves],
                    [pltpu.SemaphoreType.DMA((n,)), pltpu.SemaphoreType.DMA((n,))], start, finish,
                    {i: i for i in range(n)})


def _combine(carries):
    operands, out_shapes, sems, aliases, spans = [], [], [], {}, []
    for c in carries:
        aliases.update({len(operands) + i: len(out_shapes) + o for i, o in c.aliases.items()})
        spans.append((len(operands), len(out_shapes), len(sems)))
        operands += list(c.operands)
        out_shapes += list(c.out_shapes)
        sems += list(c.sems)

    def each(phase):
        def run(ins, outs, sem_refs):
            for c, (a, b, s) in zip(carries, spans):
                getattr(c, phase)(ins[a:a + len(c.operands)], outs[b:b + len(c.out_shapes)], sem_refs[s:s + len(c.sems)])
        return run

    return _Carried(operands, out_shapes, sems, each("start"), each("finish"), aliases)


def _allreduce_small(arrays, wire):
    n = len(arrays)
    halves = [(a.shape[0], a.shape[1] // 2) for a in arrays]

    def body(*refs):
        srcs, outs = refs[:n], refs[n:2 * n]
        mine_bufs, sib_bufs, chip_bufs, total_bufs = (refs[k * n:(k + 1) * n] for k in range(2, 6))
        send_sems, recv_sems, local_sems = refs[6 * n:]
        x, y, c = _mesh_pos()
        k0 = 2 * x + y
        sib = (x, y, 1 - c)

        def remote(src, dst, j, i, to):
            return pltpu.make_async_remote_copy(src_ref=src, dst_ref=dst, send_sem=send_sems.at[j, i],
                                                recv_sem=recv_sems.at[j, i], device_id=to, device_id_type=MESH)

        def cols(ref, i, core):
            return ref.at[:, pl.ds(pl.multiple_of(core * halves[i][1], LANE), halves[i][1])]

        swaps = [remote(cols(s, i, 1 - c), b, 0, i, sib) for i, (s, b) in enumerate(zip(srcs, sib_bufs))]
        own = [pltpu.make_async_copy(cols(s, i, c), m, local_sems.at[i]) for i, (s, m) in enumerate(zip(srcs, mine_bufs))]
        for cp in swaps + own:
            cp.start()
        for cp in swaps + own:
            cp.wait()
        for m, b, buf in zip(mine_bufs, sib_bufs, chip_bufs):
            buf[k0] = (m[...] + b[...]).astype(buf.dtype)
        chips = _other_chips(x, y)
        sends = [remote(buf.at[k0], buf.at[k0], 1 + j, i, (*chip, c))
                 for j, chip in enumerate(chips) for i, buf in enumerate(chip_bufs)]
        for cp in sends:
            cp.start()
        for j, chip in enumerate(chips):
            for i, buf in enumerate(chip_bufs):
                remote(buf.at[k0], buf.at[2 * chip[0] + chip[1]], 1 + j, i, (*chip, c)).wait_recv()
        for cp in sends:
            cp.wait_send()
        for t, buf in zip(total_bufs, chip_bufs):
            t[...] = ((buf[0].astype(F32) + buf[1].astype(F32)) + buf[2].astype(F32)) + buf[3].astype(F32)
        joins = [remote(t, cols(o, i, c), 4, i, sib) for i, (t, o) in enumerate(zip(total_bufs, outs))]
        keep = [pltpu.make_async_copy(t, cols(o, i, c), local_sems.at[i]) for i, (t, o) in enumerate(zip(total_bufs, outs))]
        for cp in joins + keep:
            cp.start()
        for i, (t, o) in enumerate(zip(total_bufs, outs)):
            remote(t, cols(o, i, 1 - c), 4, i, sib).wait_recv()
        for cp in joins:
            cp.wait_send()
        for cp in keep:
            cp.wait()

    specs = [_full(a.shape) for a in arrays]
    return _pallas_call(
        body, name="allreduce_small", grid=(1,), in_specs=specs, out_specs=specs,
        out_shape=[_sds(a.shape) for a in arrays],
        scratch_shapes=([pltpu.VMEM(h, F32) for h in halves] + [pltpu.VMEM(h, F32) for h in halves]
                        + [pltpu.VMEM((NCHIP,) + h, dt) for h, dt in zip(halves, wire)] + [pltpu.VMEM(h, F32) for h in halves]
                        + [pltpu.SemaphoreType.DMA((5, n)), pltpu.SemaphoreType.DMA((5, n)), pltpu.SemaphoreType.DMA((n,))]),
        compiler_params=_params(32),
    )(*arrays)


def _adamw_terms(w, g, m, v):
    m = ADAM_B1 * m + (1.0 - ADAM_B1) * g
    v = ADAM_B2 * v + (1.0 - ADAM_B2) * jnp.square(g)
    m_hat = m / (1.0 - ADAM_B1 ** ADAM_STEP)
    v_hat = v / (1.0 - ADAM_B2 ** ADAM_STEP)
    return -ADAM_LR * (m_hat / (jnp.sqrt(v_hat) + ADAM_EPS) + ADAM_WD * w), m, v


ADAM_STEPS = 4


def _adamw_group(tag, ws, gs, ms, vs):
    n = len(ws)

    def body(*refs):
        ins, outs = refs[:4 * n], refs[4 * n:]
        for i in range(n):
            w, g, m, v = (ins[k * n + i][...] for k in range(4))
            outs[i][...], outs[n + i][...], outs[2 * n + i][...] = _adamw_terms(w, g, m, v)

    specs = [pl.BlockSpec((w.shape[0] // ADAM_STEPS, w.shape[1]), lambda i: (i, 0)) for w in ws]
    outs = _pallas_call(
        body, name="adamw_" + tag, grid=(ADAM_STEPS,), in_specs=specs * 4, out_specs=specs * 3,
        out_shape=[_sds(w.shape) for w in ws] * 3, compiler_params=_params(48),
    )(*_in_hbm(list(ws) + list(gs) + list(ms) + list(vs)))
    return outs[:n], outs[n:2 * n], outs[2 * n:]


def _adamw_replicated(sums, row_of, direct):
    ns, nr, nd = len(sums), len(row_of), len(direct)

    def body(*refs):
        sum_refs = refs[:ns]
        ins = refs[ns:ns + 3 * nr + 4 * nd]
        outs = refs[ns + 3 * nr + 4 * nd:]
        for i, (_, _, _, si, row) in enumerate(row_of):
            w_ref, m_ref, v_ref = ins[3 * i:3 * i + 3]
            g = sum_refs[si][row:row + 1, :]
            outs[4 * i][...] = g
            outs[4 * i + 1][...], outs[4 * i + 2][...], outs[4 * i + 3][...] = _adamw_terms(w_ref[...], g, m_ref[...], v_ref[...])
        for i in range(nd):
            w_ref, m_ref, v_ref, g_ref = ins[3 * nr + 4 * i:3 * nr + 4 * i + 4]
            o = outs[4 * (nr + i):4 * (nr + i) + 4]
            g = g_ref[...]
            o[0][...] = g
            o[1][...], o[2][...], o[3][...] = _adamw_terms(w_ref[...], g, m_ref[...], v_ref[...])

    operands = list(sums)
    shapes = []
    for w, m, v, _, _ in row_of:
        operands += [w, m, v]
        shapes += [w.shape] * 4
    for w, m, v, g in direct:
        operands += [w, m, v, g]
        shapes += [w.shape] * 4
    flat = _pallas_call(
        body, name="adamw_replicated", grid=(1,), in_specs=[_full(a.shape) for a in operands],
        out_specs=[_full(s) for s in shapes], out_shape=[_sds(s) for s in shapes],
        compiler_params=_params(56),
    )(*operands)
    return [flat[4 * i:4 * i + 4] for i in range(nr + nd)]


class _Exchanges:
    def __init__(self, shards, conv_w, chip, core, apply):
        self.shards, self.conv_w, self.apply = shards, conv_w, apply
        self.active, self.calls = [], 0
        self.core_idx = jnp.reshape(core, (1,)).astype(jnp.int32)
        self.chip_core_idx = jnp.stack([chip, core]).astype(jnp.int32)

    def first(self):
        names = ["w_in", "w_glu"]
        got = _run_now("gather_first", _gather_group([self.shards[n] for n in names] + [self.conv_w],
                                                     [True, True, False]))
        out = dict(zip(names, got))
        out["conv_w"] = jnp.transpose(got[2], (1, 0, 2)).reshape(4, LW)
        return out

    def gather(self, names):
        return _gather_group([self.shards[n] for n in names], [True] * len(names))

    def reduce(self, tag, grads):
        self.active.append({"tag": tag, "names": list(grads), "stage": 0, "grads": list(grads.values())})

    def run(self, call):
        groups = self.active
        carries = [self._exchange_of(g) for g in groups]
        carry = _combine(carries)
        outs = list(call(carry))
        own = len(outs) - len(carry.out_shapes)
        landed = outs[own:]
        for g, c in zip(groups, carries):
            self._sum_after(g, landed[:len(c.out_shapes)])
            landed = landed[len(c.out_shapes):]
        self.active = [g for g in groups if g["stage"] < 3]
        return outs[:own]

    def _exchange_of(self, g):
        if g["stage"] == 0:
            return _swap_group(g["grads"])
        if g["stage"] == 1:
            return _exchange_group(g["bf16"])
        return _join_group(g["halves"])

    def _sum_after(self, g, landed):
        if g["stage"] == 0:
            g["f32"], g["bf16"] = _add_sibling_group(g["tag"], self.core_idx, g["grads"], landed)
        elif g["stage"] == 1:
            g["halves"] = _add_chips_group(g["tag"], self.chip_core_idx, g["f32"], landed)
        else:
            self.apply(g["tag"], g["names"], [t.reshape(dict(SHARDED)[n]) for n, t in zip(g["names"], landed)])
        g["stage"] += 1

    def drain(self):
        while self.active:
            self.calls += 1
            self.run(lambda carry: _run_now("reduce_%d" % self.calls, carry))


INPUT_NAMES = (["x", "p"] + [n for n in
               ["g_mix", "w_in", "b_in", "lam_re", "lam_im", "log_dt", "s5_b_re", "s5_b_im", "s5_c_re", "s5_c_im", "s5_d",
                "w_glu", "b_glu", "conv_w", "conv_b", "w_r", "b_r", "w_i", "b_i", "lru_lambda", "w_a_out", "w_b_out", "w_o",
                "g_ffn", "w_ffn_gate", "w_ffn_up", "w_ffn_down", "g_ple_gate", "w_ple_gate", "b_ple_gate", "w_ple", "g_ple",
                "g_final"]])
WEIGHT_NAMES = INPUT_NAMES[2:]


def kernel(*args):
    names = INPUT_NAMES + ["loss_target"] + ["m_" + n for n in WEIGHT_NAMES] + ["v_" + n for n in WEIGHT_NAMES]
    assert len(args) == len(names)
    given = dict(zip(names, args))

    def view(name):
        a = given[name]
        return jnp.swapaxes(a, -1, -2) if name.endswith(TRANSPOSED) else a

    def unview(name, a):
        return jnp.swapaxes(a, -1, -2) if name in TRANSPOSED else a

    def local(name):
        return view(name) if name.endswith("g_final") else view(name)[0]

    xi, yi, ci = _mesh_pos()
    k0 = 2 * xi + yi
    x, p, tgt = given["x"][0], given["p"][0, 0], given["loss_target"][0]

    results = {}

    def apply(tag, names, totals):
        new = _adamw_group(tag, [local(n) for n in names], totals, [local("m_" + n) for n in names],
                           [local("v_" + n) for n in names])
        for kind, arrays in zip(("grad", "delta", "new_m", "new_v"), (totals, *new)):
            for n, arr in zip(names, arrays):
                results[kind, n] = unview(n, arr[None])

    comm = _Exchanges({n: local(n).astype(BF) for n, _ in SHARDED}, local("conv_w"), k0, ci, apply)
    w = {n: local(n) for n in WEIGHT_NAMES if n != "conv_w" and n not in dict(SHARDED)}
    gx, sums, blocks = _local_step(x, p, tgt, w, comm)

    sum_names, block_names = list(sums), list(blocks)
    red = _allreduce_small([sums[n] for n in sum_names] + [blocks[n] for n in block_names],
                           [F32] * len(sum_names) + [BF] * len(block_names))
    sums = dict(zip(sum_names, red[:len(sum_names)]))
    blocks = dict(zip(block_names, red[len(sum_names):]))
    loss = jnp.sum(sums[LOSS_ROW[0]][LOSS_ROW[1]])
    direct_g = _replicated_grads(w, sums, blocks)
    conv_rows = sums[CONV_W_ROWS[0]][CONV_W_ROWS[1]:CONV_W_ROWS[1] + 4]
    direct_g["conv_w"] = lax.dynamic_slice(conv_rows, (0, k0 * CONV_SHARD[1]), CONV_SHARD)
    as_row = lambda a: a.reshape(1, -1)
    row_names = list(ACC_ROWS)
    row_of = [(as_row(given[n]), as_row(given["m_" + n]), as_row(given["v_" + n]),
               sum_names.index(ACC_ROWS[n][0]), ACC_ROWS[n][1]) for n in row_names]
    direct_names = list(direct_g)
    direct = [(view(n), view("m_" + n), view("v_" + n), direct_g[n].reshape(view(n).shape)) for n in direct_names]
    done = _adamw_replicated([sums[n] for n in sum_names], row_of, direct)
    for n, four in zip(row_names + direct_names, done):
        for kind, arr in zip(("grad", "delta", "new_m", "new_v"), four):
            results[kind, n] = unview(n, arr).reshape(given[n].shape)

    out = [loss, gx[None]]
    for kind in ("grad", "delta", "new_m", "new_v"):
        out += [results[kind, n] for n in WEIGHT_NAMES]
    return tuple(out)
```

```python
import functools
import math

import jax
import jax.numpy as jnp
from jax import lax
from jax.experimental import pallas as pl
from jax.experimental.pallas import tpu as pltpu

F32 = jnp.float32
BF = jnp.bfloat16

D = 1024
S5W = 512
NG, NS, NP = 32, 64, 16
GN = NG * NS
LW = 1024
NH, HD = 16, 64
LRU_C = 8.0
FH = 2816
NCHIP = 4
FC = FH // NCHIP
PLE = 256
INC = S5W + LW + 2 * D
EPS = 1e-6
ADAM_LR, ADAM_B1, ADAM_B2, ADAM_EPS, ADAM_WD, ADAM_STEP = 0.001, 0.9, 0.999, 1e-08, 0.01, 10

TM = 256
TK = 512
LC = 512
SUB = 8
VMEM_MB = 1024 * 1024
MESH = pl.DeviceIdType.MESH
ANY = pl.BlockSpec(memory_space=pl.ANY)


def _mm(a, b):
    return jnp.dot(a.astype(BF), b.astype(BF), preferred_element_type=F32)


def _mm_nt(a, b):
    return lax.dot_general(a.astype(BF), b.astype(BF), (((1,), (1,)), ((), ())), preferred_element_type=F32)


def _mm_tn(a, b):
    return lax.dot_general(a.astype(BF), b.astype(BF), (((0,), (0,)), ((), ())), preferred_element_type=F32)


def _rms(x):
    r = lax.rsqrt(jnp.mean(x * x, axis=-1, keepdims=True) + EPS)
    return x * r, r


def _rms_bwd(dy, xh, r, g):
    dxh = dy * g
    return r * (dxh - xh * jnp.mean(dxh * xh, axis=-1, keepdims=True))


def _colsum(x):
    return jnp.sum(x, axis=0, keepdims=True)


def _sig(x):
    return jax.nn.sigmoid(x)


def _gelu_grad(x):
    c = math.sqrt(2.0 / math.pi)
    t = jnp.tanh(c * (x + 0.044715 * x * x * x))
    return 0.5 * (1.0 + t) + 0.5 * x * (1.0 - t * t) * c * (1.0 + 3.0 * 0.044715 * x * x)


def _neg_expm1(x):
    series = -x * (1.0 + x * (0.5 + x * (1.0 / 6.0 + x * (1.0 / 24.0))))
    return jnp.where(x > -0.03, series, 1.0 - jnp.exp(x))


def _tok(width):
    return pl.BlockSpec((TM, width), lambda i: (i, 0))


def _tok_rev(width, nt):
    return pl.BlockSpec((TM, width), lambda i: (nt - 1 - i, 0))


def _full(shape):
    return pl.BlockSpec(shape, lambda i: (0,) * len(shape))


def _params(vmem_mb, **kw):
    return pltpu.CompilerParams(dimension_semantics=("arbitrary",), vmem_limit_bytes=vmem_mb * VMEM_MB, **kw)


def _sds(shape, dtype=F32):
    return jax.ShapeDtypeStruct(shape, dtype)


class _Carried:
    def __init__(self, operands, out_shapes, sems, start, finish, aliases=None):
        self.operands, self.out_shapes, self.sems = list(operands), list(out_shapes), list(sems)
        self.start, self.finish, self.aliases = start, finish, dict(aliases or {})


def _in_hbm(arrays):
    return [pltpu.with_memory_space_constraint(a, pltpu.HBM) for a in arrays]


def _pallas_call(body, carry=None, **kw):
    if carry is None:
        return pl.pallas_call(body, **kw)

    def at_step(corner):
        hit = [pl.program_id(d) == (size - 1 if corner else 0) for d, size in enumerate(kw["grid"])]
        return functools.reduce(jnp.logical_and, hit)

    name, grid, compiler_params = kw["name"], kw["grid"], kw["compiler_params"]
    in_specs, out_specs, out_shape = list(kw["in_specs"]), list(kw["out_specs"]), list(kw["out_shape"])
    scratch_shapes = list(kw.get("scratch_shapes", ()))
    n_in, n_out, n_scr = len(in_specs), len(out_specs), len(scratch_shapes)
    c_in, c_out = len(carry.operands), len(carry.out_shapes)

    def full_body(*refs):
        ins, refs = refs[:n_in], refs[n_in:]
        c_ins, refs = refs[:c_in], refs[c_in:]
        outs, refs = refs[:n_out], refs[n_out:]
        c_outs, refs = refs[:c_out], refs[c_out:]
        scratch, c_sems = refs[:n_scr], refs[n_scr:]

        @pl.when(at_step(0))
        def _():
            carry.start(c_ins, c_outs, c_sems)

        body(*ins, *outs, *scratch)

        @pl.when(at_step(1))
        def _():
            carry.finish(c_ins, c_outs, c_sems)

    call = pl.pallas_call(
        full_body, name=name, grid=grid, in_specs=in_specs + [ANY] * c_in, out_specs=out_specs + [ANY] * c_out,
        out_shape=out_shape + list(carry.out_shapes), scratch_shapes=scratch_shapes + list(carry.sems),
        input_output_aliases={n_in + i: n_out + o for i, o in carry.aliases.items()},
        compiler_params=compiler_params)
    return lambda *operands: call(*operands, *_in_hbm(carry.operands))


def _row_iota(width):
    return lax.broadcasted_iota(jnp.int32, (SUB, width), 0)


def _bcast_row(x, row):
    return jnp.broadcast_to(x[row:row + 1, :], x.shape)


def _slab(k):
    return pl.ds(pl.multiple_of(k * SUB, SUB), SUB)


QC = INC // NCHIP
Z_PARTS = ((0, S5W), (S5W, S5W + LW), (S5W + LW, INC))


def _inproj_fwd(x, g_mix, w_in, b_in, carry=None):
    L = x.shape[0]

    def body(x_ref, g_ref, w_hbm, b_ref, h_ref, ua_ref, ub_ref, gp_ref, w_vm):
        @pl.when(pl.program_id(0) == 0)
        def _():
            pltpu.sync_copy(w_hbm, w_vm)

        xh, _ = _rms(x_ref[...])
        h = (xh * g_ref[...]).astype(BF)
        h_ref[...] = h
        for k in range(NCHIP):
            lo, hi = k * QC, (k + 1) * QC
            z = jnp.dot(h, w_vm[k], preferred_element_type=F32) + b_ref[:, lo:hi]
            for ref, (a, b) in zip((ua_ref, ub_ref, gp_ref), Z_PARTS):
                s, e = max(lo, a), min(hi, b)
                if s < e:
                    ref[:, s - a:e - a] = z[:, s - lo:e - lo]

    return _pallas_call(
        body, carry, name="inproj_fwd", grid=(L // TM,),
        in_specs=[_tok(D), _full((1, D)), ANY, _full((1, INC))],
        out_specs=[_tok(D), _tok(S5W), _tok(LW), _tok(2 * D)],
        out_shape=[_sds((L, D), BF), _sds((L, S5W)), _sds((L, LW)), _sds((L, 2 * D))],
        scratch_shapes=[pltpu.VMEM((NCHIP, D, QC), BF)],
        compiler_params=_params(40),
    )(x, g_mix, w_in, b_in)


def _inproj_bwd(x, dx1, dua, dub, dgp, g_mix, w_in, carry=None):
    L = x.shape[0]

    def body(x_ref, dx1_ref, dua_ref, dub_ref, dgp_ref, g_ref, w_hbm, gx_ref, dz_ref, dg_ref, db_ref, w_vm):
        @pl.when(pl.program_id(0) == 0)
        def _():
            pltpu.sync_copy(w_hbm, w_vm)
            dg_ref[...] = jnp.zeros_like(dg_ref)
            db_ref[...] = jnp.zeros_like(db_ref)

        for src, (a, b) in zip((dua_ref, dub_ref, dgp_ref), Z_PARTS):
            d = src[...]
            dz_ref[:, a:b] = d.astype(BF)
            db_ref[0:1, a:b] += _colsum(d)
        dh = jnp.zeros((TM, D), F32)
        for k in range(NCHIP):
            dh = dh + lax.dot_general(dz_ref[:, k * QC:(k + 1) * QC], w_vm[k], (((1,), (1,)), ((), ())),
                                      preferred_element_type=F32)
        xh, r = _rms(x_ref[...])
        dg_ref[0:1, :] += _colsum(dh * xh)
        gx_ref[...] = dx1_ref[...] + _rms_bwd(dh, xh, r, g_ref[...])

    return _pallas_call(
        body, carry, name="inproj_bwd", grid=(L // TM,),
        in_specs=[_tok(D), _tok(D), _tok(S5W), _tok(LW), _tok(2 * D), _full((1, D)), ANY],
        out_specs=[_tok(D), _tok(INC), _full((SUB, D)), _full((SUB, INC))],
        out_shape=[_sds((L, D)), _sds((L, INC), BF), _sds((SUB, D)), _sds((SUB, INC))],
        scratch_shapes=[pltpu.VMEM((NCHIP, D, QC), BF)],
        compiler_params=_params(40),
    )(x, dx1, dua, dub, dgp, g_mix, w_in)


def _cscan(xr_ref, xi_ref, con_ref, cr_ref, ci_ref, reverse):
    n_slab = xr_ref.shape[0] // SUB
    width = xr_ref.shape[1]
    for lc in range(width // LC):
        cols = slice(lc * LC, (lc + 1) * LC)
        con = [con_ref[SUB * j:SUB * (j + 1), cols] for j in range(8)]

        def step(k, carry, cols=cols, con=con):
            cr, ci = carry
            rows = _slab(n_slab - 1 - k if reverse else k)
            xr, xi = xr_ref[rows, cols], xi_ref[rows, cols]
            for j, sh in enumerate((1, 2, 4)):
                mr, mi = con[2 * j], con[2 * j + 1]
                pr = pltpu.roll(xr, SUB - sh if reverse else sh, 0)
                pi = pltpu.roll(xi, SUB - sh if reverse else sh, 0)
                xr, xi = xr + mr * pr - mi * pi, xi + mr * pi + mi * pr
            xr, xi = xr + con[6] * cr - con[7] * ci, xi + con[6] * ci + con[7] * cr
            xr_ref[rows, cols] = xr
            xi_ref[rows, cols] = xi
            row = 0 if reverse else SUB - 1
            return _bcast_row(xr, row), _bcast_row(xi, row)

        cr, ci = lax.fori_loop(0, n_slab, step, (cr_ref[:, cols], ci_ref[:, cols]))
        cr_ref[:, cols] = cr
        ci_ref[:, cols] = ci


def _s5_fwd(ua, bbr, bbi, ccr, cci, dsk, con, w_glu, b_glu, carry=None):
    L = ua.shape[0]

    def body(ua_ref, bbr_hbm, bbi_hbm, ccr_hbm, cci_hbm, dsk_ref, con_ref, wg_ref, bg_ref,
             sr_ref, si_ref, y_ref, zg_ref, ya_ref, bbr_vm, bbi_vm, ccr_vm, cci_vm, cr_ref, ci_ref):
        @pl.when(pl.program_id(0) == 0)
        def _():
            pltpu.sync_copy(bbr_hbm, bbr_vm)
            pltpu.sync_copy(bbi_hbm, bbi_vm)
            pltpu.sync_copy(ccr_hbm, ccr_vm)
            pltpu.sync_copy(cci_hbm, cci_vm)
            cr_ref[...] = jnp.zeros_like(cr_ref)
            ci_ref[...] = jnp.zeros_like(ci_ref)

        u = ua_ref[...]
        ub = u.astype(BF)
        sr_ref[...] = jnp.dot(ub, bbr_vm[...], preferred_element_type=F32)
        si_ref[...] = jnp.dot(ub, bbi_vm[...], preferred_element_type=F32)
        _cscan(sr_ref, si_ref, con_ref, cr_ref, ci_ref, reverse=False)
        y = _mm_nt(sr_ref[...], ccr_vm[...]) - _mm_nt(si_ref[...], cci_vm[...]) + dsk_ref[...] * u
        y_ref[...] = y
        zg = jax.nn.gelu(y)
        zg_ref[...] = zg.astype(BF)
        q = _mm(zg, wg_ref[...]) + bg_ref[...]
        ya_ref[...] = (zg * _sig(q)).astype(BF)

    return _pallas_call(
        body, carry, name="s5_fwd", grid=(L // TM,),
        in_specs=[_tok(S5W), ANY, ANY, ANY, ANY, _full((1, S5W)), _full((8 * SUB, GN)),
                  _full((S5W, S5W)), _full((1, S5W))],
        out_specs=[_tok(GN), _tok(GN), _tok(S5W), _tok(S5W), _tok(S5W)],
        out_shape=[_sds((L, GN)), _sds((L, GN)), _sds((L, S5W)), _sds((L, S5W), BF), _sds((L, S5W), BF)],
        scratch_shapes=[pltpu.VMEM((S5W, GN), BF), pltpu.VMEM((S5W, GN), BF), pltpu.VMEM((S5W, GN), BF),
                        pltpu.VMEM((S5W, GN), BF),pltpu.VMEM((SUB, GN), F32), pltpu.VMEM((SUB, GN), F32)],
        compiler_params=_params(44),
    )(ua, bbr, bbi, ccr, cci, dsk, con, w_glu, b_glu)


def _s5_bwd(dya, y, ua, sr, si, bbr, bbi, ccr, cci, dsk, con_rev, w_glu, b_glu, carry=None):
    L = ua.shape[0]
    nt = L // TM
    spt = TM // SUB
    n_slab = spt

    def halo_map(i):
        return (jnp.maximum((nt - 1 - i) * spt - 1, 0), 0)

    def body(dya_ref, y_ref, ua_ref, sr_ref, si_ref, hr_ref, hi_ref, bbr_hbm, bbi_hbm, ccr_hbm, cci_hbm,
             dsk_ref, con_ref, wg_ref, bg_ref,
             dua_ref, dq_ref, dy_ref, lr_ref, li_ref, da_ref, dsm_ref,
             bbr_vm, bbi_vm, ccr_vm, cci_vm, cr_ref, ci_ref):
        i = pl.program_id(0)

        @pl.when(i == 0)
        def _():
            pltpu.sync_copy(bbr_hbm, bbr_vm)
            pltpu.sync_copy(bbi_hbm, bbi_vm)
            pltpu.sync_copy(ccr_hbm, ccr_vm)
            pltpu.sync_copy(cci_hbm, cci_vm)
            cr_ref[...] = jnp.zeros_like(cr_ref)
            ci_ref[...] = jnp.zeros_like(ci_ref)
            da_ref[...] = jnp.zeros_like(da_ref)
            dsm_ref[...] = jnp.zeros_like(dsm_ref)

        u = ua_ref[...]
        yv = y_ref[...]
        dya = dya_ref[...]
        zg = jax.nn.gelu(yv)
        sg = _sig(_mm(zg, wg_ref[...]) + bg_ref[...])
        dq = dya * zg * sg * (1.0 - sg)
        dq_ref[...] = dq.astype(BF)
        dzg = dya * sg + _mm_nt(dq, wg_ref[...])
        dy = dzg * _gelu_grad(yv)
        dyb = dy.astype(BF)
        dy_ref[...] = dyb
        dsm_ref[0:1, :] += _colsum(dy * u)
        dsm_ref[1:2, :] += _colsum(dq)
        lr_ref[...] = jnp.dot(dyb, ccr_vm[...], preferred_element_type=F32)
        li_ref[...] = -jnp.dot(dyb, cci_vm[...], preferred_element_type=F32)
        _cscan(lr_ref, li_ref, con_ref, cr_ref, ci_ref, reverse=True)

        first_tile = (i == nt - 1)
        row = _row_iota(LC)
        for lc in range(GN // LC):
            cols = slice(lc * LC, (lc + 1) * LC)
            h_r = jnp.where(first_tile, 0.0, hr_ref[:, cols])
            h_i = jnp.where(first_tile, 0.0, hi_ref[:, cols])

            def step(k, acc, cols=cols, h_r=h_r, h_i=h_i):
                ar, ai = acc
                rows = _slab(k)
                prev = _slab(jnp.maximum(k - 1, 0))
                pr = jnp.where(k == 0, h_r, sr_ref[prev, cols])
                pi = jnp.where(k == 0, h_i, si_ref[prev, cols])
                spr = pltpu.roll(jnp.where(row == SUB - 1, pr, sr_ref[rows, cols]), 1, 0)
                spi = pltpu.roll(jnp.where(row == SUB - 1, pi, si_ref[rows, cols]), 1, 0)
                lr, li = lr_ref[rows, cols], li_ref[rows, cols]
                return ar + lr * spr + li * spi, ai + li * spr - lr * spi

            zero = jnp.zeros((SUB, LC), F32)
            ar, ai = lax.fori_loop(0, n_slab, step, (zero, zero))
            da_ref[0:1, cols] += _colsum(ar)
            da_ref[1:2, cols] += _colsum(ai)

        dua_ref[...] = (dy * dsk_ref[...] + _mm_nt(lr_ref[...], bbr_vm[...]) + _mm_nt(li_ref[...], bbi_vm[...]))

    return _pallas_call(
        body, carry, name="s5_bwd", grid=(nt,),
        in_specs=[_tok_rev(S5W, nt), _tok_rev(S5W, nt), _tok_rev(S5W, nt), _tok_rev(GN, nt), _tok_rev(GN, nt),
                  pl.BlockSpec((SUB, GN), halo_map), pl.BlockSpec((SUB, GN), halo_map),
                  ANY, ANY, ANY, ANY, _full((1, S5W)), _full((8 * SUB, GN)), _full((S5W, S5W)), _full((1, S5W))],
        out_specs=[_tok_rev(S5W, nt), _tok_rev(S5W, nt), _tok_rev(S5W, nt), _tok_rev(GN, nt), _tok_rev(GN, nt),
                   _full((SUB, GN)), _full((SUB, S5W))],
        out_shape=[_sds((L, S5W)), _sds((L, S5W), BF), _sds((L, S5W), BF), _sds((L, GN)), _sds((L, GN)),
                   _sds((SUB, GN)), _sds((SUB, S5W))],
        scratch_shapes=[pltpu.VMEM((S5W, GN), BF), pltpu.VMEM((S5W, GN), BF), pltpu.VMEM((S5W, GN), BF),
                        pltpu.VMEM((S5W, GN), BF),pltpu.VMEM((SUB, GN), F32), pltpu.VMEM((SUB, GN), F32)],
        compiler_params=_params(52),
    )(dya, y, ua, sr, si, sr, si, bbr, bbi, ccr, cci, dsk, con_rev, w_glu, b_glu)


def _lru_gate_terms(rg, sp):
    log_a = -LRU_C * rg * sp
    a = jnp.exp(log_a)
    mult = jnp.sqrt(_neg_expm1(2.0 * log_a))
    return a, mult


def _lru_fwd(ub, conv_w, conv_b, wr, wi, b_r, b_i, sp, carry=None):
    L = ub.shape[0]
    n_slab = TM // SUB

    def body(ub_ref, cw_ref, cb_ref, wr_ref, wi_ref, br_ref, bi_ref, sp_ref,
             xc_ref, rg_ref, ig_ref, h_ref, hp_ref, a_ref, halo_ref, carry_ref):
        @pl.when(pl.program_id(0) == 0)
        def _():
            halo_ref[...] = jnp.zeros_like(halo_ref)
            carry_ref[...] = jnp.zeros_like(carry_ref)

        row = _row_iota(LW)
        taps = [cw_ref[k:k + 1, :] for k in range(4)]
        cb = cb_ref[...]

        def conv_step(k, prev):
            rows = _slab(k)
            cur = ub_ref[rows, :]
            acc = taps[3] * cur + cb
            for j in (1, 2, 3):
                acc = acc + taps[3 - j] * pltpu.roll(jnp.where(row >= SUB - j, prev, cur), j, 0)
            xc_ref[rows, :] = acc
            return cur

        halo_ref[...] = lax.fori_loop(0, n_slab, conv_step, halo_ref[...])

        xc = xc_ref[...]
        xcb = xc.astype(BF)
        rg = _sig(jnp.dot(xcb, wr_ref[...], preferred_element_type=F32) + br_ref[...])
        ig = _sig(jnp.dot(xcb, wi_ref[...], preferred_element_type=F32) + bi_ref[...])
        rg_ref[...] = rg
        ig_ref[...] = ig
        a, mult = _lru_gate_terms(rg, sp_ref[...])
        a_ref[...] = a
        h_ref[...] = mult * ig * xc

        rowc = _row_iota(LC)
        for lc in range(LW // LC):
            cols = slice(lc * LC, (lc + 1) * LC)

            def step(k, c, cols=cols):
                rows = _slab(k)
                av, b = a_ref[rows, cols], h_ref[rows, cols]
                for sh in (1, 2, 4):
                    keep = rowc >= sh
                    b = b + av * jnp.where(keep, pltpu.roll(b, sh, 0), 0.0)
                    av = av * jnp.where(keep, pltpu.roll(av, sh, 0), 1.0)
                h = b + av * c
                h_ref[rows, cols] = h
                hp_ref[rows, cols] = jnp.where(rowc == 0, c, pltpu.roll(h, 1, 0))
                return _bcast_row(h, SUB - 1)

            carry_ref[:, cols] = lax.fori_loop(0, n_slab, step, carry_ref[:, cols])

    return _pallas_call(
        body, carry, name="lru_fwd", grid=(L // TM,),
        in_specs=[_tok(LW), _full((4, LW)), _full((1, LW)), _full((LW, LW)), _full((LW, LW)),
                  _full((1, LW)), _full((1, LW)), _full((1, LW))],
        out_specs=[_tok(LW)] * 5,
        out_shape=[_sds((L, LW))] * 5,
        scratch_shapes=[pltpu.VMEM((TM, LW), F32), pltpu.VMEM((SUB, LW), F32), pltpu.VMEM((SUB, LW), F32)],
        compiler_params=_params(40),
    )(ub, conv_w, conv_b, wr, wi, b_r, b_i, sp)


def _lru_bwd(dyb, xc, rg, ig, hp, ub, conv_w, wr, wi, sp, dsp, carry=None):
    L = ub.shape[0]
    nt = L // TM
    spt = TM // SUB
    n_slab = spt

    def halo_map(i):
        return (jnp.maximum((nt - 1 - i) * spt - 1, 0), 0)

    def body(dh_ref, xc_ref, rg_ref, ig_ref, hp_ref, ub_ref, uh_ref, cw_ref, wr_ref, wi_ref, sp_ref, dsp_ref,
             dub_ref, dpr_ref, dpi_ref, acc_ref, a_ref, lam_ref, dxc_ref, carry_ref, next_ref):
        i = pl.program_id(0)

        @pl.when(i == 0)
        def _():
            carry_ref[...] = jnp.zeros_like(carry_ref)
            next_ref[...] = jnp.zeros_like(next_ref)
            acc_ref[...] = jnp.zeros_like(acc_ref)

        sp = sp_ref[...]
        rg, ig, xc = rg_ref[...], ig_ref[...], xc_ref[...]
        a, mult = _lru_gate_terms(rg, sp)
        a_ref[...] = a

        rowc = _row_iota(LC)
        for lc in range(LW // LC):
            cols = slice(lc * LC, (lc + 1) * LC)

            def step(k, c, cols=cols):
                rows = _slab(n_slab - 1 - k)
                av, dh = a_ref[rows, cols], dh_ref[rows, cols]
                b = av * dh
                for sh in (1, 2, 4):
                    keep = rowc < SUB - sh
                    b = b + av * jnp.where(keep, pltpu.roll(b, SUB - sh, 0), 0.0)
                    av = av * jnp.where(keep, pltpu.roll(av, SUB - sh, 0), 1.0)
                mu = b + av * c
                lam_ref[rows, cols] = dh + jnp.where(rowc == SUB - 1, c, pltpu.roll(mu, SUB - 1, 0))
                return _bcast_row(mu, 0)

            carry_ref[:, cols] = lax.fori_loop(0, n_slab, step, carry_ref[:, cols])

        lam = lam_ref[...]
        d_a = lam * hp_ref[...]
        d_mult = lam * ig * xc
        d_ig = lam * mult * xc
        dxc = lam * mult * ig
        d_log_a = d_a * a - d_mult * a * a / mult
        d_rg = (-LRU_C) * sp * d_log_a
        acc_ref[0:1, :] += _colsum((-LRU_C) * rg * d_log_a) * dsp_ref[...]
        dpr = d_rg * rg * (1.0 - rg)
        dpi = d_ig * ig * (1.0 - ig)
        acc_ref[1:2, :] += _colsum(dpr)
        acc_ref[2:3, :] += _colsum(dpi)
        dprb, dpib = dpr.astype(BF), dpi.astype(BF)
        dpr_ref[...] = dprb
        dpi_ref[...] = dpib
        dxc = dxc + _mm_nt(dprb, wr_ref[...]) + _mm_nt(dpib, wi_ref[...])
        dxc_ref[...] = dxc
        acc_ref[3:4, :] += _colsum(dxc)

        row = _row_iota(LW)
        taps = [cw_ref[k:k + 1, :] for k in range(4)]
        u_halo = jnp.where(i == nt - 1, 0.0, uh_ref[...])
        nxt_tile = next_ref[...]

        def conv_step(k, accs):
            rows = _slab(k)
            cur = dxc_ref[rows, :]
            nxt = jnp.where(k == n_slab - 1, nxt_tile, dxc_ref[_slab(jnp.minimum(k + 1, n_slab - 1)), :])
            ucur = ub_ref[rows, :]
            uprev = jnp.where(k == 0, u_halo, ub_ref[_slab(jnp.maximum(k - 1, 0)), :])
            du = taps[3] * cur
            new = [accs[3] + cur * ucur]
            for j in (1, 2, 3):
                du = du + taps[3 - j] * pltpu.roll(jnp.where(row < j, nxt, cur), SUB - j, 0)
                new.append(accs[3 - j] + cur * pltpu.roll(jnp.where(row >= SUB - j, uprev, ucur), j, 0))
            dub_ref[rows, :] = du
            return tuple(new[::-1])

        zero = jnp.zeros((SUB, LW), F32)
        accs = lax.fori_loop(0, n_slab, conv_step, (zero, zero, zero, zero))
        for k in range(4):
            acc_ref[4 + k:5 + k, :] += _colsum(accs[k])
        next_ref[...] = dxc_ref[0:SUB, :]

    return _pallas_call(
        body, carry, name="lru_bwd", grid=(nt,),
        in_specs=[_tok_rev(LW, nt)] * 6 + [pl.BlockSpec((SUB, LW), halo_map), _full((4, LW)),
                                           _full((LW, LW)), _full((LW, LW)), _full((1, LW)), _full((1, LW))],
        out_specs=[_tok_rev(LW, nt), _tok_rev(LW, nt), _tok_rev(LW, nt), _full((SUB, LW))],
        out_shape=[_sds((L, LW)), _sds((L, LW), BF), _sds((L, LW), BF), _sds((SUB, LW))],
        scratch_shapes=[pltpu.VMEM((TM, LW), F32), pltpu.VMEM((TM, LW), F32), pltpu.VMEM((TM, LW), F32),
                        pltpu.VMEM((SUB, LW), F32), pltpu.VMEM((SUB, LW), F32)],
        compiler_params=_params(48),
    )(dyb, xc, rg, ig, hp, ub, ub, conv_w, wr, wi, sp, dsp)


AC = D // NCHIP


def _merge_fwd(x, ya, yb, gp, w_a, w_b, w_o, carry=None):
    L = x.shape[0]

    def body(x_ref, ya_ref, yb_ref, gp_ref, wa_ref, wb_ref, wo_ref, x1_ref, pa_ref, pb_ref, mg_ref):
        ya = ya_ref[...]
        for k in range(NCHIP):
            pa_ref[:, k * AC:(k + 1) * AC] = jnp.dot(ya, wa_ref[k], preferred_element_type=F32)
        pb = _mm(yb_ref[...], wb_ref[...])
        pb_ref[...] = pb
        gp = gp_ref[...]
        merged = (_sig(gp[:, :D]) * pa_ref[...] + _sig(gp[:, D:]) * pb).astype(BF)
        mg_ref[...] = merged
        x1_ref[...] = x_ref[...] + jnp.dot(merged, wo_ref[...], preferred_element_type=F32)

    return _pallas_call(
        body, carry, name="merge_fwd", grid=(L // TM,),
        in_specs=[_tok(D), _tok(S5W), _tok(LW), _tok(2 * D), _full((NCHIP, S5W, AC)), _full((LW, D)), _full((D, D))],
        out_specs=[_tok(D), _tok(D), _tok(D), _tok(D)],
        out_shape=[_sds((L, D)), _sds((L, D)), _sds((L, D)), _sds((L, D), BF)],
        compiler_params=_params(40),
    )(x, ya, yb, gp, w_a, w_b, w_o)


def _merge_bwd(dx1, gp, pa, pb, w_a, w_b, w_o, carry=None):
    L = dx1.shape[0]

    def body(dx1_ref, gp_ref, pa_ref, pb_ref, wa_ref, wb_ref, wo_ref, dya_ref, dyb_ref, dgp_ref, dpa_ref, dpb_ref):
        dm = _mm_nt(dx1_ref[...], wo_ref[...])
        gp = gp_ref[...]
        sa, sb = _sig(gp[:, :D]), _sig(gp[:, D:])
        dpa = (dm * sa).astype(BF)
        dpb = (dm * sb).astype(BF)
        dpa_ref[...] = dpa
        dpb_ref[...] = dpb
        dgp_ref[:, :D] = dm * pa_ref[...] * sa * (1.0 - sa)
        dgp_ref[:, D:] = dm * pb_ref[...] * sb * (1.0 - sb)
        dya = jnp.zeros((TM, S5W), F32)
        for k in range(NCHIP):
            dya = dya + _mm_nt(dpa[:, k * AC:(k + 1) * AC], wa_ref[k])
        dya_ref[...] = dya
        dyb_ref[...] = _mm_nt(dpb, wb_ref[...])

    return _pallas_call(
        body, carry, name="merge_bwd", grid=(L // TM,),
        in_specs=[_tok(D), _tok(2 * D), _tok(D), _tok(D), _full((NCHIP, S5W, AC)), _full((LW, D)), _full((D, D))],
        out_specs=[_tok(S5W), _tok(LW), _tok(2 * D), _tok(D), _tok(D)],
        out_shape=[_sds((L, S5W)), _sds((L, LW)), _sds((L, 2 * D)), _sds((L, D), BF), _sds((L, D), BF)],
        compiler_params=_params(40),
    )(dx1, gp, pa, pb, w_a, w_b, w_o)


def _chunk_tok(width):
    return pl.BlockSpec((NCHIP, TM, width), lambda i: (0, i, 0))


def _ffn_fwd(x1, g_ffn, wg, wu, wd, carry=None):
    L = x1.shape[0]

    def body(x_ref, g_ref, wg_hbm, wu_hbm, wd_hbm, x2_ref, h2_ref, gg_ref, uu_ref, wg_vm, wu_vm, wd_vm):
        @pl.when(pl.program_id(0) == 0)
        def _():
            pltpu.sync_copy(wg_hbm, wg_vm)
            pltpu.sync_copy(wu_hbm, wu_vm)
            pltpu.sync_copy(wd_hbm, wd_vm)

        x = x_ref[...]
        xh, _ = _rms(x)
        h2 = (xh * g_ref[...]).astype(BF)
        h2_ref[...] = h2
        out = x
        for c in range(NCHIP):
            gg = lax.dot_general(h2, wg_vm[c], (((1,), (1,)), ((), ())), preferred_element_type=F32)
            uu = lax.dot_general(h2, wu_vm[c], (((1,), (1,)), ((), ())), preferred_element_type=F32)
            gg_ref[c] = gg.astype(BF)
            uu_ref[c] = uu.astype(BF)
            act = (gg * _sig(gg) * uu).astype(BF)
            out = out + jnp.dot(act, wd_vm[c], preferred_element_type=F32)
        x2_ref[...] = out

    return _pallas_call(
        body, carry, name="ffn_fwd", grid=(L // TM,),
        in_specs=[_tok(D), _full((1, D)), ANY, ANY, ANY],
        out_specs=[_tok(D), _tok(D), _chunk_tok(FC), _chunk_tok(FC)],
        out_shape=[_sds((L, D)), _sds((L, D), BF), _sds((NCHIP, L, FC), BF), _sds((NCHIP, L, FC), BF)],
        scratch_shapes=[pltpu.VMEM((NCHIP, FC, D), BF)] * 3,
        compiler_params=_params(52),
    )(x1, g_ffn, wg, wu, wd)


def _ffn_bwd(x1, dx2, gg, uu, g_ffn, wg, wu, wd, carry=None):
    L = x1.shape[0]

    def body(x_ref, dx2_ref, gg_ref, uu_ref, g_ref, wg_hbm, wu_hbm, wd_hbm,
             dx1_ref, act_ref, dgg_ref, duu_ref, dg_ref, wg_vm, wu_vm, wd_vm):
        @pl.when(pl.program_id(0) == 0)
        def _():
            pltpu.sync_copy(wg_hbm, wg_vm)
            pltpu.sync_copy(wu_hbm, wu_vm)
            pltpu.sync_copy(wd_hbm, wd_vm)
            dg_ref[...] = jnp.zeros_like(dg_ref)

        dx2 = dx2_ref[...]
        dx2b = dx2.astype(BF)
        dh2 = jnp.zeros((TM, D), F32)
        for c in range(NCHIP):
            g = gg_ref[c].astype(F32)
            u = uu_ref[c].astype(F32)
            s = _sig(g)
            silu = g * s
            act_ref[c] = (silu * u).astype(BF)
            dact = lax.dot_general(dx2b, wd_vm[c], (((1,), (1,)), ((), ())), preferred_element_type=F32)
            dg = (dact * u * s * (1.0 + g * (1.0 - s))).astype(BF)
            du = (dact * silu).astype(BF)
            dgg_ref[c] = dg
            duu_ref[c] = du
            dh2 = dh2 + jnp.dot(dg, wg_vm[c], preferred_element_type=F32)
            dh2 = dh2 + jnp.dot(du, wu_vm[c], preferred_element_type=F32)
        xh, r = _rms(x_ref[...])
        dg_ref[0:1, :] += _colsum(dh2 * xh)
        dx1_ref[...] = dx2 + _rms_bwd(dh2, xh, r, g_ref[...])

    return _pallas_call(
        body, carry, name="ffn_bwd", grid=(L // TM,),
        in_specs=[_tok(D), _tok(D), _chunk_tok(FC), _chunk_tok(FC), _full((1, D)), ANY, ANY, ANY],
        out_specs=[_tok(D), _chunk_tok(FC), _chunk_tok(FC), _chunk_tok(FC), _full((SUB, D))],
        out_shape=[_sds((L, D)), _sds((NCHIP, L, FC), BF), _sds((NCHIP, L, FC), BF), _sds((NCHIP, L, FC), BF),
                   _sds((SUB, D))],
        scratch_shapes=[pltpu.VMEM((NCHIP, FC, D), BF)] * 3,
        compiler_params=_params(56),
    )(x1, dx2, gg, uu, g_ffn, wg, wu, wd)


def _ple_loss(x2, p, tgt, g_pg, w_pg, b_pg, w_ple, g_ple, g_final):
    L = x2.shape[0]

    def body(x2_ref, p_ref, t_ref, gpg_ref, wpg_ref, bpg_ref, wple_ref, gple_ref, gf_ref,
             dx2_ref, n2_ref, dpre_ref, de0_ref, acc_ref):
        @pl.when(pl.program_id(0) == 0)
        def _():
            acc_ref[...] = jnp.zeros_like(acc_ref)

        x2 = x2_ref[...]
        x2h, r2 = _rms(x2)
        n2 = (x2h * gpg_ref[...]).astype(BF)
        n2_ref[...] = n2
        gate = _sig(jnp.dot(n2, wpg_ref[...], preferred_element_type=F32) + bpg_ref[...])
        pb = p_ref[...].astype(BF)
        e0 = jnp.concatenate([jnp.dot(pb, wple_ref[k], preferred_element_type=F32) for k in range(NCHIP)], axis=1)
        e0h, re = _rms(e0)
        e = e0h * gple_ref[...]
        x3 = x2 + gate * e
        x3h, r3 = _rms(x3)
        diff = x3h * gf_ref[...] - t_ref[...]
        acc_ref[4:5, :] += _colsum(diff * diff) * (0.5 / D)
        dy = diff * (1.0 / D)
        acc_ref[3:4, :] += _colsum(dy * x3h)
        dx3 = _rms_bwd(dy, x3h, r3, gf_ref[...])
        de = dx3 * gate
        acc_ref[2:3, :] += _colsum(de * e0h)
        de0_ref[...] = _rms_bwd(de, e0h, re, gple_ref[...]).astype(BF)
        dpre = dx3 * e * gate * (1.0 - gate)
        acc_ref[1:2, :] += _colsum(dpre)
        dpreb = dpre.astype(BF)
        dpre_ref[...] = dpreb
        dn2 = lax.dot_general(dpreb, wpg_ref[...], (((1,), (1,)), ((), ())), preferred_element_type=F32)
        acc_ref[0:1, :] += _colsum(dn2 * x2h)
        dx2_ref[...] = dx3 + _rms_bwd(dn2, x2h, r2, gpg_ref[...])

    return _pallas_call(
        body, name="ple_loss", grid=(L // TM,),
        in_specs=[_tok(D), _tok(PLE), _tok(D), _full((1, D)), _full((D, D)), _full((1, D)), _full((NCHIP, PLE, AC)),
                  _full((1, D)), _full((1, D))],
        out_specs=[_tok(D), _tok(D), _tok(D), _tok(D), _full((SUB, D))],
        out_shape=[_sds((L, D)), _sds((L, D), BF), _sds((L, D), BF), _sds((L, D), BF), _sds((SUB, D))],
        compiler_params=_params(40),
    )(x2, p, tgt, g_pg, w_pg, b_pg, w_ple, g_ple, g_final)


def _tn(name, a, b, col_chunk=None, a_block=None, carry=None):
    L = a.shape[-2]
    m, n = a.shape[-1], b.shape[-1]
    a_col = 0
    if a_block is not None:
        a_col, m = a_block
    if a.ndim == 3 or b.ndim == 3:
        nj, bn = (a if a.ndim == 3 else b).shape[0], n
        a_spec = (pl.BlockSpec((None, TK, m), lambda j, t: (j, t, 0)) if a.ndim == 3
                  else pl.BlockSpec((TK, m), lambda j, t: (t, 0)))
        b_spec = (pl.BlockSpec((None, TK, n), lambda j, t: (j, t, 0)) if b.ndim == 3
                  else pl.BlockSpec((TK, n), lambda j, t: (t, 0)))
        out_spec, out_shape = pl.BlockSpec((None, m, n), lambda j, t: (j, 0, 0)), _sds((nj, m, n))
    else:
        bn = col_chunk
        if bn is None:
            bn = next((cand for cand in (1024, 512) if n > cand and n % cand == 0), n)
        nj = n // bn
        a_spec = pl.BlockSpec((TK, m), lambda j, t: (t, a_col))
        b_spec = pl.BlockSpec((TK, bn), lambda j, t: (t, j))
        if col_chunk is None:
            out_spec, out_shape = pl.BlockSpec((m, bn), lambda j, t: (0, j)), _sds((m, n))
        else:
            out_spec, out_shape = pl.BlockSpec((None, m, bn), lambda j, t: (j, 0, 0)), _sds((nj, m, bn))

    def body(a_ref, b_ref, o_ref):
        @pl.when(pl.program_id(1) == 0)
        def _():
            o_ref[...] = jnp.zeros_like(o_ref)

        o_ref[...] += _mm_tn(a_ref[...], b_ref[...])

    outs = _pallas_call(
        body, carry, name=name, grid=(nj, L // TK), in_specs=[a_spec, b_spec], out_specs=[out_spec],
        out_shape=[out_shape],
        compiler_params=pltpu.CompilerParams(dimension_semantics=("arbitrary", "arbitrary"),
                                             vmem_limit_bytes=40 * VMEM_MB),
    )(a, b)
    return outs[0] if carry is None else outs


LANE = 128


def _tn_blocks(name, a, bs, ga, gb, carry=None):
    L, m, n, nb = a.shape[0], a.shape[1], bs[0].shape[1], len(bs)
    per = LANE // ga
    wb = per * gb
    n_super = m // LANE

    def body(a_ref, *refs):
        b_refs, o_refs, acc_refs = refs[:nb], refs[nb:2 * nb], refs[2 * nb:]
        t = pl.program_id(0)

        @pl.when(t == 0)
        def _():
            for acc in acc_refs:
                acc[...] = jnp.zeros_like(acc)

        lhs = a_ref[...].astype(BF)
        for b_ref, acc in zip(b_refs, acc_refs):
            rhs = b_ref[...].astype(BF)
            for j in range(n_super):
                acc[j] += _mm_tn(lhs[:, j * LANE:(j + 1) * LANE], rhs[:, j * wb:(j + 1) * wb])

        @pl.when(t == L // TK - 1)
        def _():
            own = (lax.broadcasted_iota(jnp.int32, (LANE, wb), 0) // ga) == (lax.broadcasted_iota(jnp.int32, (LANE, wb), 1) // gb)
            for o_ref, acc in zip(o_refs, acc_refs):
                for j in range(n_super):
                    kept = jnp.where(own, acc[j], 0.0)
                    o_ref[:, j * wb:(j + 1) * wb] = jnp.sum(kept.reshape(per, ga, wb), axis=0)

    outs = _pallas_call(
        body, carry, name=name, grid=(L // TK,),
        in_specs=[pl.BlockSpec((TK, m), lambda t: (t, 0))] + [pl.BlockSpec((TK, n), lambda t: (t, 0))] * nb,
        out_specs=[_full((ga, n))] * nb, out_shape=[_sds((ga, n))] * nb,
        scratch_shapes=[pltpu.VMEM((n_super, LANE, wb), F32)] * nb,
        compiler_params=_params(48),
    )(*_in_hbm([a] + list(bs)))
    return list(outs)


def _s5_discretize(lam_re, lam_im, log_dt, b_re, b_im):
    dt = jnp.exp(log_dt)[:, None]
    mag = jnp.exp(lam_re * dt)
    ar = mag * jnp.cos(lam_im * dt)
    ai = mag * jnp.sin(lam_im * dt)
    den = lam_re * lam_re + lam_im * lam_im
    nr = ar - 1.0
    fr = (nr * lam_re + ai * lam_im) / den
    fi = (ai * lam_re - nr * lam_im) / den
    bbr = fr[:, None, :] * b_re - fi[:, None, :] * b_im
    bbi = fr[:, None, :] * b_im + fi[:, None, :] * b_re
    return ar, ai, bbr, bbi


def _prepare(by_rows, block_cols, ar, ai):
    n = len(by_rows)

    def body(*refs):
        srcs, (ar_ref, ai_ref), dense, (con_ref, rev_ref) = refs[:n], refs[n:n + 2], refs[n + 2:2 * n + 2], refs[2 * n + 2:]
        for src, out, c in zip(srcs, dense, block_cols):
            r, width = src.shape
            groups = width // c
            tiled = jnp.broadcast_to(src[...][None], (groups, r, width)).reshape(groups * r, width)
            own = (lax.broadcasted_iota(jnp.int32, tiled.shape, 0) // r) == (lax.broadcasted_iota(jnp.int32, tiled.shape, 1) // c)
            out[...] = jnp.where(own, tiled, 0.0).astype(BF)
        a_r, a_i = ar_ref[...], ai_ref[...]
        pw = [(jnp.ones_like(a_r), jnp.zeros_like(a_i))]
        for _ in range(SUB):
            pr, pi = pw[-1]
            pw.append((pr * a_r - pi * a_i, pr * a_i + pi * a_r))
        row = _row_iota(GN)
        for ref, reverse in ((con_ref, False), (rev_ref, True)):
            sign = -1.0 if reverse else 1.0
            for j, sh in enumerate((1, 2, 4)):
                keep = (row < SUB - sh) if reverse else (row >= sh)
                ref[2 * j * SUB:(2 * j + 1) * SUB, :] = jnp.where(keep, pw[sh][0], 0.0)
                ref[(2 * j + 1) * SUB:(2 * j + 2) * SUB, :] = jnp.where(keep, sign * pw[sh][1], 0.0)
            p_r, p_i = jnp.zeros((SUB, GN), F32), jnp.zeros((SUB, GN), F32)
            for i in range(SUB):
                k = SUB - i if reverse else i + 1
                p_r = jnp.where(row == i, pw[k][0], p_r)
                p_i = jnp.where(row == i, sign * pw[k][1], p_i)
            ref[6 * SUB:7 * SUB, :] = p_r
            ref[7 * SUB:8 * SUB, :] = p_i

    dense_shapes = [(b.shape[1] // c * b.shape[0], b.shape[1]) for b, c in zip(by_rows, block_cols)]
    outs = _pallas_call(
        body, name="prepare", grid=(1,), in_specs=[_full(b.shape) for b in by_rows] + [_full((1, GN))] * 2,
        out_specs=[_full(s) for s in dense_shapes] + [_full((8 * SUB, GN))] * 2,
        out_shape=[_sds(s, BF) for s in dense_shapes] + [_sds((8 * SUB, GN))] * 2,
        compiler_params=_params(48),
    )(*by_rows, ar, ai)
    return outs[:n], outs[n], outs[n + 1]


def _local_step(x, p, tgt, w, comm):
    rows_of = lambda a: a.reshape(NCHIP * a.shape[1], a.shape[2])
    quarters = lambda a: a.reshape(NCHIP, a.shape[0] // NCHIP, a.shape[1])

    def gathering(names, call):
        carry = comm.gather(names)
        outs = list(call(carry))
        own = len(outs) - len(carry.out_shapes)
        w.update(zip(names, outs[own:]))
        return outs[:own]

    w.update(comm.first())
    w_glu = rows_of(w["w_glu"])
    ar, ai, bbr, bbi = _s5_discretize(w["lam_re"], w["lam_im"], w["log_dt"], w["s5_b_re"], w["s5_b_im"])
    by_row = lambda b: jnp.transpose(b, (1, 0, 2)).reshape(b.shape[1], -1)
    (bbr_d, bbi_d, ccr_d, cci_d, wr_d, wi_d), con, con_rev = _prepare(
        [by_row(b) for b in (bbr, bbi, w["s5_c_re"], w["s5_c_im"], w["w_r"], w["w_i"])], [NS] * 4 + [HD] * 2,
        ar.reshape(1, GN), ai.reshape(1, GN))
    dsk = w["s5_d"].reshape(1, S5W)
    lam = w["lru_lambda"].reshape(1, LW)
    sp = jax.nn.softplus(-lam)
    b_r, b_i = w["b_r"].reshape(1, LW), w["b_i"].reshape(1, LW)
    row = lambda name: w[name].reshape(1, -1)

    h, ua, ub, gp = gathering(["w_a_out", "w_b_out", "w_o"], lambda carry: _inproj_fwd(
        x, row("g_mix"), w["w_in"], row("b_in"), carry))
    sr, si, y, zg, ya = gathering(["w_ffn_gate"], lambda carry: _s5_fwd(
        ua, bbr_d, bbi_d, ccr_d, cci_d, dsk, con, w_glu, row("b_glu"), carry))
    xc, rg, ig, yb, hp = gathering(["w_ffn_up"], lambda carry: _lru_fwd(
        ub, w["conv_w"], row("conv_b"), wr_d, wi_d, b_r, b_i, sp, carry))
    w_b_out, w_o = rows_of(w["w_b_out"]), rows_of(w["w_o"])
    x1, pa, pb, merged = gathering(["w_ffn_down"], lambda carry: _merge_fwd(
        x, ya, yb, gp, w["w_a_out"], w_b_out, w_o, carry))
    x2, h2, gg, uu = gathering(["w_ple_gate", "w_ple"], lambda carry: _ffn_fwd(
        x1, row("g_ffn"), w["w_ffn_gate"], w["w_ffn_up"], w["w_ffn_down"], carry))
    w_pg = rows_of(w["w_ple_gate"])
    dx2, n2, dpre, de0, acc_p = _ple_loss(x2, p, tgt, row("g_ple_gate"), w_pg, row("b_ple_gate"),
                                          w["w_ple"], row("g_ple"), row("g_final"))
    comm.reduce("ple", {"w_ple_gate": quarters(_tn("dw_ple_gate", n2, dpre)),
                        "w_ple": _tn("dw_ple", p, de0, col_chunk=AC)})
    dx1, act, dgg, duu, acc_f = comm.run(lambda carry: _ffn_bwd(
        x1, dx2, gg, uu, row("g_ffn"), w["w_ffn_gate"], w["w_ffn_up"], w["w_ffn_down"], carry))
    comm.reduce("ffn_gate", {"w_ffn_gate": _tn("dw_ffn_gate", dgg, h2)})
    comm.reduce("ffn_up", {"w_ffn_up": comm.run(lambda carry: _tn("dw_ffn_up", duu, h2, carry=carry))[0]})
    comm.reduce("ffn_down", {"w_ffn_down": comm.run(lambda carry: _tn("dw_ffn_down", act, dx2, carry=carry))[0]})
    dya, dyb, dgp, dpa, dpb = comm.run(lambda carry: _merge_bwd(
        dx1, gp, pa, pb, w["w_a_out"], w_b_out, w_o, carry))
    comm.reduce("merge", {"w_o": quarters(_tn("dw_o", merged, dx1)), "w_a_out": _tn("dw_a_out", ya, dpa, col_chunk=AC),
                          "w_b_out": quarters(_tn("dw_b_out", yb, dpb))})
    dua, dq, dy, lr, li, acc_a, acc_s = comm.run(lambda carry: _s5_bwd(
        dya, y, ua, sr, si, bbr_d, bbi_d, ccr_d, cci_d, dsk, con_rev, w_glu, row("b_glu"), carry))
    dub, dpr, dpi, acc_l = comm.run(lambda carry: _lru_bwd(
        dyb, xc, rg, ig, hp, ub, w["conv_w"], wr_d, wi_d, sp, -_sig(-lam), carry))
    gx, dz, acc_g, acc_b = _inproj_bwd(x, dx1, dua, dub, dgp, row("g_mix"), w["w_in"])
    half = (D // 2,)
    comm.reduce("in_lo", {"w_in_lo": comm.run(lambda carry: _tn(
        "dw_in_lo", h, dz, col_chunk=QC, a_block=(0,) + half, carry=carry))[0]})
    comm.reduce("in_hi", {"w_in_hi": comm.run(lambda carry: _tn(
        "dw_in_hi", h, dz, col_chunk=QC, a_block=(1,) + half, carry=carry))[0], "w_glu": quarters(_tn("dw_glu", zg, dq))})
    d_wr, d_wi = comm.run(lambda carry: _tn_blocks("dw_r_i", xc, [dpr, dpi], HD, HD, carry))
    d_bbr, d_bbi = comm.run(lambda carry: _tn_blocks("d_bb", ua, [lr, li], NP, NS, carry))
    d_ccr, d_cci = comm.run(lambda carry: _tn_blocks("d_cc", dy, [sr, si], NP, NS, carry))
    comm.drain()
    sums = {"ple": acc_p, "ffn": acc_f, "mix": acc_g, "b_in": acc_b, "lru": acc_l, "s5": acc_s, "s5_a": acc_a}
    blocks = {"bb_re": d_bbr, "bb_im": d_bbi,
              "cc_re": d_ccr, "cc_im": d_cci,
              "w_r": d_wr, "w_i": d_wi}
    return gx, sums, blocks


def _replicated_grads(w, sums, blocks):
    grouped = lambda e, groups: jnp.transpose(e.reshape(e.shape[0], groups, -1), (1, 0, 2))
    d_ar, d_ai = sums["s5_a"][0].reshape(NG, NS), sums["s5_a"][1].reshape(NG, NS)
    d_bbr, d_bbi = grouped(blocks["bb_re"], NG), grouped(blocks["bb_im"], NG)
    _, vjp = jax.vjp(_s5_discretize, w["lam_re"], w["lam_im"], w["log_dt"], w["s5_b_re"], w["s5_b_im"])
    g = dict(zip(("lam_re", "lam_im", "log_dt", "s5_b_re", "s5_b_im"), vjp((d_ar, d_ai, d_bbr, d_bbi))))
    g["s5_c_re"] = grouped(blocks["cc_re"], NG)
    g["s5_c_im"] = -grouped(blocks["cc_im"], NG)
    g["w_r"], g["w_i"] = grouped(blocks["w_r"], NH), grouped(blocks["w_i"], NH)
    g["s5_d"] = sums["s5"][0].reshape(NG, NP)
    g["b_r"] = sums["lru"][1].reshape(NH, HD)
    g["b_i"] = sums["lru"][2].reshape(NH, HD)
    return g


ACC_ROWS = {"g_mix": ("mix", 0), "b_in": ("b_in", 0), "g_ffn": ("ffn", 0), "g_ple_gate": ("ple", 0),
            "b_ple_gate": ("ple", 1), "g_ple": ("ple", 2), "g_final": ("ple", 3), "b_glu": ("s5", 1),
            "lru_lambda": ("lru", 0), "conv_b": ("lru", 3)}
LOSS_ROW = ("ple", 4)
CONV_W_ROWS = ("lru", 4)


SHARDED = [("w_in", (D, QC)), ("w_glu", (S5W // NCHIP, S5W)), ("w_a_out", (S5W, AC)), ("w_b_out", (LW // NCHIP, D)),
           ("w_o", (D // NCHIP, D)), ("w_ffn_gate", (FC, D)), ("w_ffn_up", (FC, D)), ("w_ffn_down", (FC, D)),
           ("w_ple_gate", (D // NCHIP, D)), ("w_ple", (PLE, AC))]
NSH = len(SHARDED)
TRANSPOSED = ("w_ffn_gate", "w_ffn_up", "s5_b_re", "s5_b_im")
CONV_SHARD = (4, LW // NCHIP)


def _mesh_pos():
    return lax.axis_index("x"), lax.axis_index("y"), lax.axis_index("c")


def _other_chips(x, y):
    return [(1 - x, y), (x, 1 - y), (1 - x, 1 - y)]


def _half_rows(c, rows, align):
    return pl.ds(pl.multiple_of(c * (rows // 2), align), rows // 2)


def _run_now(name, carry):
    c_in, c_out = len(carry.operands), len(carry.out_shapes)

    def body(*refs):
        ins, outs, sems = refs[:c_in], refs[c_in:c_in + c_out], refs[c_in + c_out:]
        carry.start(ins, outs, sems)
        carry.finish(ins, outs, sems)

    return pl.pallas_call(body, name=name, in_specs=[ANY] * c_in, out_specs=[ANY] * c_out,
                          out_shape=list(carry.out_shapes), scratch_shapes=list(carry.sems),
                          input_output_aliases=dict(carry.aliases))(*_in_hbm(carry.operands))


def _gather_group(shards, split):
    n = len(shards)

    def copies(srcs, outs, sems):
        send_sems, recv_sems = sems
        x, y, c = _mesh_pos()
        k0 = 2 * x + y
        sib = (x, y, 1 - c)
        chips = _other_chips(x, y)

        def remote(src, dst, j, i, to):
            return pltpu.make_async_remote_copy(src_ref=src, dst_ref=dst, send_sem=send_sems.at[j, i],
                                                recv_sem=recv_sems.at[j, i], device_id=to, device_id_type=MESH)

        def rows(ref, i, core, *lead):
            if not split[i]:
                return ref.at[lead] if lead else ref
            return ref.at[(*lead, _half_rows(core, shards[i].shape[0], 16))]

        own = [remote(s, o.at[k0], 6, i, sib) for i, (s, o) in enumerate(zip(srcs, outs))]
        ici, landed, fwd, fwd_landed = [], [], [], []
        for j, chip in enumerate(chips):
            kj = 2 * chip[0] + chip[1]
            pairs = list(enumerate(zip(srcs, outs)))
            ici.append([remote(rows(s, i, c), rows(o, i, c, k0), j, i, (*chip, c)) for i, (s, o) in pairs])
            landed.append([remote(rows(s, i, c), rows(o, i, c, kj), j, i, (*chip, c)) for i, (s, o) in pairs])
            fwd.append([remote(rows(o, i, c, kj), rows(o, i, c, kj), 3 + j, i, sib) for i, (s, o) in pairs if split[i]])
            fwd_landed.append([remote(rows(o, i, 1 - c, kj), rows(o, i, 1 - c, kj), 3 + j, i, sib)
                               for i, (s, o) in pairs if split[i]])
        return own, ici, landed, fwd, fwd_landed

    def start(srcs, outs, sems):
        own, ici, _, _, _ = copies(srcs, outs, sems)
        for cp in own + [cp for per_chip in ici for cp in per_chip]:
            cp.start()

    def finish(srcs, outs, sems):
        own, ici, landed, fwd, fwd_landed = copies(srcs, outs, sems)
        passed = [i for i in range(n) if split[i]]
        for j in range(3):
            for i, cp in enumerate(landed[j]):
                cp.wait_recv()
                if split[i]:
                    fwd[j][passed.index(i)].start()
        for j in range(3):
            for cp in fwd_landed[j]:
                cp.wait_recv()
        for cp in own:
            cp.wait_recv()
        for cp in own + [cp for per_chip in ici + fwd for cp in per_chip]:
            cp.wait_send()

    return _Carried(shards, [_sds((NCHIP,) + s.shape, s.dtype) for s in shards],
                    [pltpu.SemaphoreType.DMA((7, n)), pltpu.SemaphoreType.DMA((7, n))], start, finish)


def _each_copy(copies, carried, out_shapes, sems, aliases=None):
    def start(ins, outs, sem_refs):
        for cp in copies(ins, outs, sem_refs):
            cp.start()

    def finish(ins, outs, sem_refs):
        for cp in copies(ins, outs, sem_refs):
            cp.wait()

    return _Carried(carried, out_shapes, sems, start, finish, aliases)


def _swap_group(grads):
    n = len(grads)

    def copies(srcs, outs, sems):
        send_sems, recv_sems = sems
        x, y, c = _mesh_pos()
        return [pltpu.make_async_remote_copy(src_ref=s.at[:, _half_rows(1 - c, s.shape[1], 8)], dst_ref=o,
                                             send_sem=send_sems.at[i], recv_sem=recv_sems.at[i], device_id=(x, y, 1 - c),
                                             device_id_type=MESH) for i, (s, o) in enumerate(zip(srcs, outs))]

    return _each_copy(copies, grads, [_sds((NCHIP, g.shape[1] // 2, g.shape[2])) for g in grads],
                      [pltpu.SemaphoreType.DMA((n,)), pltpu.SemaphoreType.DMA((n,))])


def _add_sibling_group(tag, c_idx, grads, gots):
    n = len(grads)

    def body(c_ref, *refs):
        for g, rx, p, pb in zip(refs[:n], refs[n:2 * n], refs[2 * n:3 * n], refs[3 * n:]):
            s = g[...] + rx[...]
            p[...] = s
            pb[...] = s.astype(BF)

    halves = [pl.BlockSpec((None,) + rx.shape[1:], lambda k, c_ref: (k, 0, 0)) for rx in gots]
    mine = [pl.BlockSpec((None,) + rx.shape[1:], lambda k, c_ref: (k, c_ref[0], 0)) for rx in gots]
    outs = _pallas_call(
        body, name="add_sibling_" + tag,
        grid_spec=pltpu.PrefetchScalarGridSpec(num_scalar_prefetch=1, grid=(NCHIP,), in_specs=mine + halves,
                                               out_specs=halves + halves),
        out_shape=[_sds(rx.shape) for rx in gots] + [_sds(rx.shape, BF) for rx in gots],
        compiler_params=_params(48),
    )(c_idx, *_in_hbm(list(grads) + list(gots)))
    return outs[:n], outs[n:]


def _exchange_group(parts):
    n = len(parts)

    def copies(srcs, outs, sems):
        send_sems, recv_sems = sems
        x, y, c = _mesh_pos()
        return [pltpu.make_async_remote_copy(
            src_ref=s.at[2 * chip[0] + chip[1]], dst_ref=o.at[j], send_sem=send_sems.at[j, i],
            recv_sem=recv_sems.at[j, i], device_id=(*chip, c), device_id_type=MESH)
            for j, chip in enumerate(_other_chips(x, y)) for i, (s, o) in enumerate(zip(srcs, outs))]

    return _each_copy(copies, parts, [_sds((3,) + p.shape[1:], BF) for p in parts],
                      [pltpu.SemaphoreType.DMA((3, n)), pltpu.SemaphoreType.DMA((3, n))])


def _add_chips_group(tag, kc_idx, parts, arrived):
    n = len(parts)

    def body(kc_ref, *refs):
        for p, rx, t in zip(refs[:n], refs[n:2 * n], refs[2 * n:]):
            t[...] = ((p[...] + rx[0].astype(F32)) + rx[1].astype(F32)) + rx[2].astype(F32)

    outs = _pallas_call(
        body, name="add_chips_" + tag,
        grid_spec=pltpu.PrefetchScalarGridSpec(
            num_scalar_prefetch=1, grid=(1,),
            in_specs=([pl.BlockSpec((None,) + rx.shape[1:], lambda i, kc_ref: (kc_ref[0], 0, 0)) for rx in arrived]
                      + [pl.BlockSpec(rx.shape, lambda i, kc_ref: (0, 0, 0)) for rx in arrived]),
            out_specs=[pl.BlockSpec((None,) + rx.shape[1:], lambda i, kc_ref: (kc_ref[1], 0, 0)) for rx in arrived]),
        out_shape=[_sds((2,) + rx.shape[1:]) for rx in arrived],
        compiler_params=_params(48),
    )(kc_idx, *_in_hbm(list(parts) + list(arrived)))
    return list(outs)


def _join_group(halves):
    n = len(halves)

    def copies(bufs, sems):
        send_sems, recv_sems = sems
        x, y, c = _mesh_pos()
        sib = (x, y, 1 - c)
        sends = [pltpu.make_async_remote_copy(src_ref=b.at[c], dst_ref=b.at[c], send_sem=send_sems.at[i],
                                              recv_sem=recv_sems.at[i], device_id=sib, device_id_type=MESH)
                 for i, b in enumerate(bufs)]
        landed = [pltpu.make_async_remote_copy(src_ref=b.at[c], dst_ref=b.at[1 - c], send_sem=send_sems.at[i],
                                               recv_sem=recv_sems.at[i], device_id=sib, device_id_type=MESH)
                  for i, b in enumerate(bufs)]
        return sends, landed

    def start(_, bufs, sems):
        for cp in copies(bufs, sems)[0]:
            cp.start()

    def finish(_, bufs, sems):
        sends, landed = copies(bufs, sems)
        for cp in landed:
            cp.wait_recv()
        for cp in sends:
            cp.wait_send()

    return _Carried(halves, [_sds(h.shape) for h in halves],
                    [pltpu.SemaphoreType.DMA((n,)), pltpu.SemaphoreType.DMA((n,))], start, finish,
                    {i: i for i in range(n)})


def _combine(carries):
    operands, out_shapes, sems, aliases, spans = [], [], [], {}, []
    for c in carries:
        aliases.update({len(operands) + i: len(out_shapes) + o for i, o in c.aliases.items()})
        spans.append((len(operands), len(out_shapes), len(sems)))
        operands += list(c.operands)
        out_shapes += list(c.out_shapes)
        sems += list(c.sems)

    def each(phase):
        def run(ins, outs, sem_refs):
            for c, (a, b, s) in zip(carries, spans):
                getattr(c, phase)(ins[a:a + len(c.operands)], outs[b:b + len(c.out_shapes)], sem_refs[s:s + len(c.sems)])
        return run

    return _Carried(operands, out_shapes, sems, each("start"), each("finish"), aliases)


def _allreduce_small(arrays, wire):
    n = len(arrays)
    halves = [(a.shape[0], a.shape[1] // 2) for a in arrays]

    def body(*refs):
        srcs, outs = refs[:n], refs[n:2 * n]
        mine_bufs, sib_bufs, chip_bufs, total_bufs = (refs[k * n:(k + 1) * n] for k in range(2, 6))
        send_sems, recv_sems, local_sems = refs[6 * n:]
        x, y, c = _mesh_pos()
        k0 = 2 * x + y
        sib = (x, y, 1 - c)

        def remote(src, dst, j, i, to):
            return pltpu.make_async_remote_copy(src_ref=src, dst_ref=dst, send_sem=send_sems.at[j, i],
                                                recv_sem=recv_sems.at[j, i], device_id=to, device_id_type=MESH)

        def cols(ref, i, core):
            return ref.at[:, pl.ds(pl.multiple_of(core * halves[i][1], LANE), halves[i][1])]

        swaps = [remote(cols(s, i, 1 - c), b, 0, i, sib) for i, (s, b) in enumerate(zip(srcs, sib_bufs))]
        own = [pltpu.make_async_copy(cols(s, i, c), m, local_sems.at[i]) for i, (s, m) in enumerate(zip(srcs, mine_bufs))]
        for cp in swaps + own:
            cp.start()
        for cp in swaps + own:
            cp.wait()
        for m, b, buf in zip(mine_bufs, sib_bufs, chip_bufs):
            buf[k0] = (m[...] + b[...]).astype(buf.dtype)
        chips = _other_chips(x, y)
        sends = [remote(buf.at[k0], buf.at[k0], 1 + j, i, (*chip, c))
                 for j, chip in enumerate(chips) for i, buf in enumerate(chip_bufs)]
        for cp in sends:
            cp.start()
        for j, chip in enumerate(chips):
            for i, buf in enumerate(chip_bufs):
                remote(buf.at[k0], buf.at[2 * chip[0] + chip[1]], 1 + j, i, (*chip, c)).wait_recv()
        for cp in sends:
            cp.wait_send()
        for t, buf in zip(total_bufs, chip_bufs):
            t[...] = ((buf[0].astype(F32) + buf[1].astype(F32)) + buf[2].astype(F32)) + buf[3].astype(F32)
        joins = [remote(t, cols(o, i, c), 4, i, sib) for i, (t, o) in enumerate(zip(total_bufs, outs))]
        keep = [pltpu.make_async_copy(t, cols(o, i, c), local_sems.at[i]) for i, (t, o) in enumerate(zip(total_bufs, outs))]
        for cp in joins + keep:
            cp.start()
        for i, (t, o) in enumerate(zip(total_bufs, outs)):
            remote(t, cols(o, i, 1 - c), 4, i, sib).wait_recv()
        for cp in joins:
            cp.wait_send()
        for cp in keep:
            cp.wait()

    specs = [_full(a.shape) for a in arrays]
    return _pallas_call(
        body, name="allreduce_small", grid=(1,), in_specs=specs, out_specs=specs,
        out_shape=[_sds(a.shape) for a in arrays],
        scratch_shapes=([pltpu.VMEM(h, F32) for h in halves] + [pltpu.VMEM(h, F32) for h in halves]
                        + [pltpu.VMEM((NCHIP,) + h, dt) for h, dt in zip(halves, wire)] + [pltpu.VMEM(h, F32) for h in halves]
                        + [pltpu.SemaphoreType.DMA((5, n)), pltpu.SemaphoreType.DMA((5, n)), pltpu.SemaphoreType.DMA((n,))]),
        compiler_params=_params(32),
    )(*arrays)


def _adamw_terms(w, g, m, v):
    m = ADAM_B1 * m + (1.0 - ADAM_B1) * g
    v = ADAM_B2 * v + (1.0 - ADAM_B2) * jnp.square(g)
    m_hat = m / (1.0 - ADAM_B1 ** ADAM_STEP)
    v_hat = v / (1.0 - ADAM_B2 ** ADAM_STEP)
    return -ADAM_LR * (m_hat / (jnp.sqrt(v_hat) + ADAM_EPS) + ADAM_WD * w), m, v


ADAM_STEPS = 4


def _adamw_group(tag, ws, gs, ms, vs):
    n = len(ws)

    def body(*refs):
        ins, outs = refs[:4 * n], refs[4 * n:]
        for i in range(n):
            w, g, m, v = (ins[k * n + i][...] for k in range(4))
            outs[i][...] = g
            outs[n + i][...], outs[2 * n + i][...], outs[3 * n + i][...] = _adamw_terms(w, g, m, v)

    specs = [pl.BlockSpec((w.shape[0] // ADAM_STEPS, w.shape[1]), lambda i: (i, 0)) for w in ws]
    outs = _pallas_call(
        body, name="adamw_" + tag, grid=(ADAM_STEPS,), in_specs=specs * 4, out_specs=specs * 4,
        out_shape=[_sds(w.shape) for w in ws] * 4, compiler_params=_params(48),
    )(*_in_hbm(list(ws) + list(gs) + list(ms) + list(vs)))
    return outs[:n], outs[n:2 * n], outs[2 * n:3 * n], outs[3 * n:]


def _adamw_replicated(sums, row_of, direct):
    ns, nr, nd = len(sums), len(row_of), len(direct)

    def body(*refs):
        sum_refs = refs[:ns]
        ins = refs[ns:ns + 3 * nr + 4 * nd]
        outs = refs[ns + 3 * nr + 4 * nd:]
        for i, (_, _, _, si, row) in enumerate(row_of):
            w_ref, m_ref, v_ref = ins[3 * i:3 * i + 3]
            g = sum_refs[si][row:row + 1, :]
            outs[4 * i][...] = g
            outs[4 * i + 1][...], outs[4 * i + 2][...], outs[4 * i + 3][...] = _adamw_terms(w_ref[...], g, m_ref[...], v_ref[...])
        for i in range(nd):
            w_ref, m_ref, v_ref, g_ref = ins[3 * nr + 4 * i:3 * nr + 4 * i + 4]
            o = outs[4 * (nr + i):4 * (nr + i) + 4]
            g = g_ref[...]
            o[0][...] = g
            o[1][...], o[2][...], o[3][...] = _adamw_terms(w_ref[...], g, m_ref[...], v_ref[...])

    operands = list(sums)
    shapes = []
    for w, m, v, _, _ in row_of:
        operands += [w, m, v]
        shapes += [w.shape] * 4
    for w, m, v, g in direct:
        operands += [w, m, v, g]
        shapes += [w.shape] * 4
    flat = _pallas_call(
        body, name="adamw_replicated", grid=(1,), in_specs=[_full(a.shape) for a in operands],
        out_specs=[_full(s) for s in shapes], out_shape=[_sds(s) for s in shapes],
        compiler_params=_params(56),
    )(*operands)
    return [flat[4 * i:4 * i + 4] for i in range(nr + nd)]


class _Exchanges:
    def __init__(self, shards, conv_w, chip, core, apply):
        self.shards, self.conv_w, self.apply = shards, conv_w, apply
        self.active, self.calls = [], 0
        self.core_idx = jnp.reshape(core, (1,)).astype(jnp.int32)
        self.chip_core_idx = jnp.stack([chip, core]).astype(jnp.int32)

    def first(self):
        names = ["w_in", "w_glu"]
        got = _run_now("gather_first", _gather_group([self.shards[n] for n in names] + [self.conv_w],
                                                     [True, True, False]))
        out = dict(zip(names, got))
        out["conv_w"] = jnp.transpose(got[2], (1, 0, 2)).reshape(4, LW)
        return out

    def gather(self, names):
        return _gather_group([self.shards[n] for n in names], [True] * len(names))

    def reduce(self, tag, grads):
        self.active.append({"tag": tag, "names": list(grads), "stage": 0, "grads": list(grads.values())})

    def run(self, call):
        groups = self.active
        carries = [self._exchange_of(g) for g in groups]
        carry = _combine(carries)
        outs = list(call(carry))
        own = len(outs) - len(carry.out_shapes)
        landed = outs[own:]
        for g, c in zip(groups, carries):
            self._sum_after(g, landed[:len(c.out_shapes)])
            landed = landed[len(c.out_shapes):]
        self.active = [g for g in groups if g["stage"] < 3]
        return outs[:own]

    def _exchange_of(self, g):
        if g["stage"] == 0:
            return _swap_group(g["grads"])
        if g["stage"] == 1:
            return _exchange_group(g["bf16"])
        return _join_group(g["halves"])

    def _sum_after(self, g, landed):
        if g["stage"] == 0:
            g["f32"], g["bf16"] = _add_sibling_group(g["tag"], self.core_idx, g["grads"], landed)
        elif g["stage"] == 1:
            g["halves"] = _add_chips_group(g["tag"], self.chip_core_idx, g["f32"], landed)
        else:
            self.apply(g["tag"], g["names"], [t.reshape(2 * t.shape[1], t.shape[2]) for t in landed])
        g["stage"] += 1

    def drain(self):
        while self.active:
            self.calls += 1
            self.run(lambda carry: _run_now("reduce_%d" % self.calls, carry))


INPUT_NAMES = (["x", "p"] + [n for n in
               ["g_mix", "w_in", "b_in", "lam_re", "lam_im", "log_dt", "s5_b_re", "s5_b_im", "s5_c_re", "s5_c_im", "s5_d",
                "w_glu", "b_glu", "conv_w", "conv_b", "w_r", "b_r", "w_i", "b_i", "lru_lambda", "w_a_out", "w_b_out", "w_o",
                "g_ffn", "w_ffn_gate", "w_ffn_up", "w_ffn_down", "g_ple_gate", "w_ple_gate", "b_ple_gate", "w_ple", "g_ple",
                "g_final"]])
WEIGHT_NAMES = INPUT_NAMES[2:]


def kernel(*args):
    names = INPUT_NAMES + ["loss_target"] + ["m_" + n for n in WEIGHT_NAMES] + ["v_" + n for n in WEIGHT_NAMES]
    assert len(args) == len(names)
    given = dict(zip(names, args))

    def view(name):
        a = given[name]
        return jnp.swapaxes(a, -1, -2) if name.endswith(TRANSPOSED) else a

    def unview(name, a):
        return jnp.swapaxes(a, -1, -2) if name in TRANSPOSED else a

    def local(name):
        return view(name) if name.endswith("g_final") else view(name)[0]

    xi, yi, ci = _mesh_pos()
    k0 = 2 * xi + yi
    x, p, tgt = given["x"][0], given["p"][0, 0], given["loss_target"][0]

    results = {}

    row_halves = {}

    def apply(tag, names, totals):
        totals = dict(zip(names, totals))
        row_halves.update({n: totals.pop(n) for n in names if n in ("w_in_lo", "w_in_hi")})
        if len(row_halves) == 2:
            totals["w_in"] = jnp.concatenate([row_halves.pop("w_in_lo"), row_halves.pop("w_in_hi")])
        names = list(totals)
        if not names:
            return
        new = _adamw_group(tag, [local(n) for n in names], list(totals.values()), [local("m_" + n) for n in names],
                           [local("v_" + n) for n in names])
        for kind, arrays in zip(("grad", "delta", "new_m", "new_v"), new):
            for n, arr in zip(names, arrays):
                results[kind, n] = unview(n, arr[None])

    comm = _Exchanges({n: local(n).astype(BF) for n, _ in SHARDED}, local("conv_w"), k0, ci, apply)
    w = {n: local(n) for n in WEIGHT_NAMES if n != "conv_w" and n not in dict(SHARDED)}
    gx, sums, blocks = _local_step(x, p, tgt, w, comm)

    sum_names, block_names = list(sums), list(blocks)
    red = _allreduce_small([sums[n] for n in sum_names] + [blocks[n] for n in block_names],
                           [F32] * len(sum_names) + [BF] * len(block_names))
    sums = dict(zip(sum_names, red[:len(sum_names)]))
    blocks = dict(zip(block_names, red[len(sum_names):]))
    loss = jnp.sum(sums[LOSS_ROW[0]][LOSS_ROW[1]])
    direct_g = _replicated_grads(w, sums, blocks)
    conv_rows = sums[CONV_W_ROWS[0]][CONV_W_ROWS[1]:CONV_W_ROWS[1] + 4]
    direct_g["conv_w"] = lax.dynamic_slice(conv_rows, (0, k0 * CONV_SHARD[1]), CONV_SHARD)
    as_row = lambda a: a.reshape(1, -1)
    row_names = list(ACC_ROWS)
    row_of = [(as_row(given[n]), as_row(given["m_" + n]), as_row(given["v_" + n]),
               sum_names.index(ACC_ROWS[n][0]), ACC_ROWS[n][1]) for n in row_names]
    direct_names = list(direct_g)
    direct = [(view(n), view("m_" + n), view("v_" + n), direct_g[n].reshape(view(n).shape)) for n in direct_names]
    done = _adamw_replicated([sums[n] for n in sum_names], row_of, direct)
    for n, four in zip(row_names + direct_names, done):
        for kind, arr in zip(("grad", "delta", "new_m", "new_v"), four):
            results[kind, n] = unview(n, arr).reshape(given[n].shape)

    out = [loss, gx[None]]
    for kind in ("grad", "delta", "new_m", "new_v"):
        out += [results[kind, n] for n in WEIGHT_NAMES]
    return tuple(out)
```

```python
import functools
import math

import jax
import jax.numpy as jnp
from jax import lax
from jax.experimental import pallas as pl
from jax.experimental.pallas import tpu as pltpu

F32 = jnp.float32
BF = jnp.bfloat16

D = 1024
S5W = 512
NG, NS, NP = 32, 64, 16
GN = NG * NS
LW = 1024
NH, HD = 16, 64
LRU_C = 8.0
FH = 2816
NCHIP = 4
FC = FH // NCHIP
PLE = 256
INC = S5W + LW + 2 * D
EPS = 1e-6
ADAM_LR, ADAM_B1, ADAM_B2, ADAM_EPS, ADAM_WD, ADAM_STEP = 0.001, 0.9, 0.999, 1e-08, 0.01, 10

TM = 256
TK = 512
LC = 512
SUB = 8
VMEM_MB = 1024 * 1024
MESH = pl.DeviceIdType.MESH
ANY = pl.BlockSpec(memory_space=pl.ANY)


def _mm(a, b):
    return jnp.dot(a.astype(BF), b.astype(BF), preferred_element_type=F32)


def _mm_nt(a, b):
    return lax.dot_general(a.astype(BF), b.astype(BF), (((1,), (1,)), ((), ())), preferred_element_type=F32)


def _mm_tn(a, b):
    return lax.dot_general(a.astype(BF), b.astype(BF), (((0,), (0,)), ((), ())), preferred_element_type=F32)


def _rms(x):
    r = lax.rsqrt(jnp.mean(x * x, axis=-1, keepdims=True) + EPS)
    return x * r, r


def _rms_bwd(dy, xh, r, g):
    dxh = dy * g
    return r * (dxh - xh * jnp.mean(dxh * xh, axis=-1, keepdims=True))


def _colsum(x):
    return jnp.sum(x, axis=0, keepdims=True)


def _sig(x):
    return jax.nn.sigmoid(x)


def _gelu_grad(x):
    c = math.sqrt(2.0 / math.pi)
    t = jnp.tanh(c * (x + 0.044715 * x * x * x))
    return 0.5 * (1.0 + t) + 0.5 * x * (1.0 - t * t) * c * (1.0 + 3.0 * 0.044715 * x * x)


def _neg_expm1(x):
    series = -x * (1.0 + x * (0.5 + x * (1.0 / 6.0 + x * (1.0 / 24.0))))
    return jnp.where(x > -0.03, series, 1.0 - jnp.exp(x))


def _tok(width):
    return pl.BlockSpec((TM, width), lambda i: (i, 0))


def _tok_rev(width, nt):
    return pl.BlockSpec((TM, width), lambda i: (nt - 1 - i, 0))


def _full(shape):
    return pl.BlockSpec(shape, lambda i: (0,) * len(shape))


def _params(vmem_mb, **kw):
    return pltpu.CompilerParams(dimension_semantics=("arbitrary",), vmem_limit_bytes=vmem_mb * VMEM_MB, **kw)


def _sds(shape, dtype=F32):
    return jax.ShapeDtypeStruct(shape, dtype)


class _Carried:
    def __init__(self, operands, out_shapes, sems, start, finish, aliases=None):
        self.operands, self.out_shapes, self.sems = list(operands), list(out_shapes), list(sems)
        self.start, self.finish, self.aliases = start, finish, dict(aliases or {})


def _in_hbm(arrays):
    return [pltpu.with_memory_space_constraint(a, pltpu.HBM) for a in arrays]


def _pallas_call(body, carry=None, **kw):
    if carry is None:
        return pl.pallas_call(body, **kw)

    def at_step(corner):
        hit = [pl.program_id(d) == (size - 1 if corner else 0) for d, size in enumerate(kw["grid"])]
        return functools.reduce(jnp.logical_and, hit)

    name, grid, compiler_params = kw["name"], kw["grid"], kw["compiler_params"]
    in_specs, out_specs, out_shape = list(kw["in_specs"]), list(kw["out_specs"]), list(kw["out_shape"])
    scratch_shapes = list(kw.get("scratch_shapes", ()))
    n_in, n_out, n_scr = len(in_specs), len(out_specs), len(scratch_shapes)
    c_in, c_out = len(carry.operands), len(carry.out_shapes)

    def full_body(*refs):
        ins, refs = refs[:n_in], refs[n_in:]
        c_ins, refs = refs[:c_in], refs[c_in:]
        outs, refs = refs[:n_out], refs[n_out:]
        c_outs, refs = refs[:c_out], refs[c_out:]
        scratch, c_sems = refs[:n_scr], refs[n_scr:]

        @pl.when(at_step(0))
        def _():
            carry.start(c_ins, c_outs, c_sems)

        body(*ins, *outs, *scratch)

        @pl.when(at_step(1))
        def _():
            carry.finish(c_ins, c_outs, c_sems)

    call = pl.pallas_call(
        full_body, name=name, grid=grid, in_specs=in_specs + [ANY] * c_in, out_specs=out_specs + [ANY] * c_out,
        out_shape=out_shape + list(carry.out_shapes), scratch_shapes=scratch_shapes + list(carry.sems),
        input_output_aliases={n_in + i: n_out + o for i, o in carry.aliases.items()},
        compiler_params=compiler_params)
    return lambda *operands: call(*operands, *_in_hbm(carry.operands))


def _row_iota(width):
    return lax.broadcasted_iota(jnp.int32, (SUB, width), 0)


def _bcast_row(x, row):
    return jnp.broadcast_to(x[row:row + 1, :], x.shape)


def _slab(k):
    return pl.ds(pl.multiple_of(k * SUB, SUB), SUB)


QC = INC // NCHIP
Z_PARTS = ((0, S5W), (S5W, S5W + LW), (S5W + LW, INC))


def _inproj_fwd(x, g_mix, w_in, b_in, carry=None):
    L = x.shape[0]

    def body(x_ref, g_ref, w_hbm, b_ref, h_ref, ua_ref, ub_ref, gp_ref, w_vm):
        @pl.when(pl.program_id(0) == 0)
        def _():
            pltpu.sync_copy(w_hbm, w_vm)

        xh, _ = _rms(x_ref[...])
        h = (xh * g_ref[...]).astype(BF)
        h_ref[...] = h
        for k in range(NCHIP):
            lo, hi = k * QC, (k + 1) * QC
            z = jnp.dot(h, w_vm[k], preferred_element_type=F32) + b_ref[:, lo:hi]
            for ref, (a, b) in zip((ua_ref, ub_ref, gp_ref), Z_PARTS):
                s, e = max(lo, a), min(hi, b)
                if s < e:
                    ref[:, s - a:e - a] = z[:, s - lo:e - lo]

    return _pallas_call(
        body, carry, name="inproj_fwd", grid=(L // TM,),
        in_specs=[_tok(D), _full((1, D)), ANY, _full((1, INC))],
        out_specs=[_tok(D), _tok(S5W), _tok(LW), _tok(2 * D)],
        out_shape=[_sds((L, D), BF), _sds((L, S5W)), _sds((L, LW)), _sds((L, 2 * D))],
        scratch_shapes=[pltpu.VMEM((NCHIP, D, QC), BF)],
        compiler_params=_params(40),
    )(x, g_mix, w_in, b_in)


def _inproj_bwd(x, dx1, dua, dub, dgp, g_mix, w_in, carry=None):
    L = x.shape[0]

    def body(x_ref, dx1_ref, dua_ref, dub_ref, dgp_ref, g_ref, w_hbm, gx_ref, dz_ref, dg_ref, db_ref, w_vm):
        @pl.when(pl.program_id(0) == 0)
        def _():
            pltpu.sync_copy(w_hbm, w_vm)
            dg_ref[...] = jnp.zeros_like(dg_ref)
            db_ref[...] = jnp.zeros_like(db_ref)

        for src, (a, b) in zip((dua_ref, dub_ref, dgp_ref), Z_PARTS):
            d = src[...]
            dz_ref[:, a:b] = d.astype(BF)
            db_ref[0:1, a:b] += _colsum(d)
        dh = jnp.zeros((TM, D), F32)
        for k in range(NCHIP):
            dh = dh + lax.dot_general(dz_ref[:, k * QC:(k + 1) * QC], w_vm[k], (((1,), (1,)), ((), ())),
                                      preferred_element_type=F32)
        xh, r = _rms(x_ref[...])
        dg_ref[0:1, :] += _colsum(dh * xh)
        gx_ref[...] = dx1_ref[...] + _rms_bwd(dh, xh, r, g_ref[...])

    return _pallas_call(
        body, carry, name="inproj_bwd", grid=(L // TM,),
        in_specs=[_tok(D), _tok(D), _tok(S5W), _tok(LW), _tok(2 * D), _full((1, D)), ANY],
        out_specs=[_tok(D), _tok(INC), _full((SUB, D)), _full((SUB, INC))],
        out_shape=[_sds((L, D)), _sds((L, INC), BF), _sds((SUB, D)), _sds((SUB, INC))],
        scratch_shapes=[pltpu.VMEM((NCHIP, D, QC), BF)],
        compiler_params=_params(40),
    )(x, dx1, dua, dub, dgp, g_mix, w_in)


def _cscan(xr_ref, xi_ref, con_ref, cr_ref, ci_ref, reverse):
    n_slab = xr_ref.shape[0] // SUB
    width = xr_ref.shape[1]
    for lc in range(width // LC):
        cols = slice(lc * LC, (lc + 1) * LC)
        con = [con_ref[SUB * j:SUB * (j + 1), cols] for j in range(8)]

        def step(k, carry, cols=cols, con=con):
            cr, ci = carry
            rows = _slab(n_slab - 1 - k if reverse else k)
            xr, xi = xr_ref[rows, cols], xi_ref[rows, cols]
            for j, sh in enumerate((1, 2, 4)):
                mr, mi = con[2 * j], con[2 * j + 1]
                pr = pltpu.roll(xr, SUB - sh if reverse else sh, 0)
                pi = pltpu.roll(xi, SUB - sh if reverse else sh, 0)
                xr, xi = xr + mr * pr - mi * pi, xi + mr * pi + mi * pr
            xr, xi = xr + con[6] * cr - con[7] * ci, xi + con[6] * ci + con[7] * cr
            xr_ref[rows, cols] = xr
            xi_ref[rows, cols] = xi
            row = 0 if reverse else SUB - 1
            return _bcast_row(xr, row), _bcast_row(xi, row)

        cr, ci = lax.fori_loop(0, n_slab, step, (cr_ref[:, cols], ci_ref[:, cols]))
        cr_ref[:, cols] = cr
        ci_ref[:, cols] = ci


def _s5_fwd(ua, bbr, bbi, ccr, cci, dsk, con, w_glu, b_glu, carry=None):
    L = ua.shape[0]

    def body(ua_ref, bbr_hbm, bbi_hbm, ccr_hbm, cci_hbm, dsk_ref, con_ref, wg_ref, bg_ref,
             sr_ref, si_ref, y_ref, zg_ref, ya_ref, bbr_vm, bbi_vm, ccr_vm, cci_vm, cr_ref, ci_ref):
        @pl.when(pl.program_id(0) == 0)
        def _():
            pltpu.sync_copy(bbr_hbm, bbr_vm)
            pltpu.sync_copy(bbi_hbm, bbi_vm)
            pltpu.sync_copy(ccr_hbm, ccr_vm)
            pltpu.sync_copy(cci_hbm, cci_vm)
            cr_ref[...] = jnp.zeros_like(cr_ref)
            ci_ref[...] = jnp.zeros_like(ci_ref)

        u = ua_ref[...]
        ub = u.astype(BF)
        sr_ref[...] = jnp.dot(ub, bbr_vm[...], preferred_element_type=F32)
        si_ref[...] = jnp.dot(ub, bbi_vm[...], preferred_element_type=F32)
        _cscan(sr_ref, si_ref, con_ref, cr_ref, ci_ref, reverse=False)
        y = _mm_nt(sr_ref[...], ccr_vm[...]) - _mm_nt(si_ref[...], cci_vm[...]) + dsk_ref[...] * u
        y_ref[...] = y
        zg = jax.nn.gelu(y)
        zg_ref[...] = zg.astype(BF)
        q = _mm(zg, wg_ref[...]) + bg_ref[...]
        ya_ref[...] = (zg * _sig(q)).astype(BF)

    return _pallas_call(
        body, carry, name="s5_fwd", grid=(L // TM,),
        in_specs=[_tok(S5W), ANY, ANY, ANY, ANY, _full((1, S5W)), _full((8 * SUB, GN)),
                  _full((S5W, S5W)), _full((1, S5W))],
        out_specs=[_tok(GN), _tok(GN), _tok(S5W), _tok(S5W), _tok(S5W)],
        out_shape=[_sds((L, GN)), _sds((L, GN)), _sds((L, S5W)), _sds((L, S5W), BF), _sds((L, S5W), BF)],
        scratch_shapes=[pltpu.VMEM((S5W, GN), BF), pltpu.VMEM((S5W, GN), BF), pltpu.VMEM((S5W, GN), BF),
                        pltpu.VMEM((S5W, GN), BF),pltpu.VMEM((SUB, GN), F32), pltpu.VMEM((SUB, GN), F32)],
        compiler_params=_params(44),
    )(ua, bbr, bbi, ccr, cci, dsk, con, w_glu, b_glu)


def _s5_bwd(dya, y, ua, sr, si, bbr, bbi, ccr, cci, dsk, con_rev, w_glu, b_glu, carry=None):
    L = ua.shape[0]
    nt = L // TM
    spt = TM // SUB
    n_slab = spt

    def halo_map(i):
        return (jnp.maximum((nt - 1 - i) * spt - 1, 0), 0)

    def body(dya_ref, y_ref, ua_ref, sr_ref, si_ref, hr_ref, hi_ref, bbr_hbm, bbi_hbm, ccr_hbm, cci_hbm,
             dsk_ref, con_ref, wg_ref, bg_ref,
             dua_ref, dq_ref, dy_ref, lr_ref, li_ref, da_ref, dsm_ref,
             bbr_vm, bbi_vm, ccr_vm, cci_vm, cr_ref, ci_ref):
        i = pl.program_id(0)

        @pl.when(i == 0)
        def _():
            pltpu.sync_copy(bbr_hbm, bbr_vm)
            pltpu.sync_copy(bbi_hbm, bbi_vm)
            pltpu.sync_copy(ccr_hbm, ccr_vm)
            pltpu.sync_copy(cci_hbm, cci_vm)
            cr_ref[...] = jnp.zeros_like(cr_ref)
            ci_ref[...] = jnp.zeros_like(ci_ref)
            da_ref[...] = jnp.zeros_like(da_ref)
            dsm_ref[...] = jnp.zeros_like(dsm_ref)

        u = ua_ref[...]
        yv = y_ref[...]
        dya = dya_ref[...]
        zg = jax.nn.gelu(yv)
        sg = _sig(_mm(zg, wg_ref[...]) + bg_ref[...])
        dq = dya * zg * sg * (1.0 - sg)
        dq_ref[...] = dq.astype(BF)
        dzg = dya * sg + _mm_nt(dq, wg_ref[...])
        dy = dzg * _gelu_grad(yv)
        dyb = dy.astype(BF)
        dy_ref[...] = dyb
        dsm_ref[0:1, :] += _colsum(dy * u)
        dsm_ref[1:2, :] += _colsum(dq)
        lr_ref[...] = jnp.dot(dyb, ccr_vm[...], preferred_element_type=F32)
        li_ref[...] = -jnp.dot(dyb, cci_vm[...], preferred_element_type=F32)
        _cscan(lr_ref, li_ref, con_ref, cr_ref, ci_ref, reverse=True)

        first_tile = (i == nt - 1)
        row = _row_iota(LC)
        for lc in range(GN // LC):
            cols = slice(lc * LC, (lc + 1) * LC)
            h_r = jnp.where(first_tile, 0.0, hr_ref[:, cols])
            h_i = jnp.where(first_tile, 0.0, hi_ref[:, cols])

            def step(k, acc, cols=cols, h_r=h_r, h_i=h_i):
                ar, ai = acc
                rows = _slab(k)
                prev = _slab(jnp.maximum(k - 1, 0))
                pr = jnp.where(k == 0, h_r, sr_ref[prev, cols])
                pi = jnp.where(k == 0, h_i, si_ref[prev, cols])
                spr = pltpu.roll(jnp.where(row == SUB - 1, pr, sr_ref[rows, cols]), 1, 0)
                spi = pltpu.roll(jnp.where(row == SUB - 1, pi, si_ref[rows, cols]), 1, 0)
                lr, li = lr_ref[rows, cols], li_ref[rows, cols]
                return ar + lr * spr + li * spi, ai + li * spr - lr * spi

            zero = jnp.zeros((SUB, LC), F32)
            ar, ai = lax.fori_loop(0, n_slab, step, (zero, zero))
            da_ref[0:1, cols] += _colsum(ar)
            da_ref[1:2, cols] += _colsum(ai)

        dua_ref[...] = (dy * dsk_ref[...] + _mm_nt(lr_ref[...], bbr_vm[...]) + _mm_nt(li_ref[...], bbi_vm[...]))

    return _pallas_call(
        body, carry, name="s5_bwd", grid=(nt,),
        in_specs=[_tok_rev(S5W, nt), _tok_rev(S5W, nt), _tok_rev(S5W, nt), _tok_rev(GN, nt), _tok_rev(GN, nt),
                  pl.BlockSpec((SUB, GN), halo_map), pl.BlockSpec((SUB, GN), halo_map),
                  ANY, ANY, ANY, ANY, _full((1, S5W)), _full((8 * SUB, GN)), _full((S5W, S5W)), _full((1, S5W))],
        out_specs=[_tok_rev(S5W, nt), _tok_rev(S5W, nt), _tok_rev(S5W, nt), _tok_rev(GN, nt), _tok_rev(GN, nt),
                   _full((SUB, GN)), _full((SUB, S5W))],
        out_shape=[_sds((L, S5W)), _sds((L, S5W), BF), _sds((L, S5W), BF), _sds((L, GN)), _sds((L, GN)),
                   _sds((SUB, GN)), _sds((SUB, S5W))],
        scratch_shapes=[pltpu.VMEM((S5W, GN), BF), pltpu.VMEM((S5W, GN), BF), pltpu.VMEM((S5W, GN), BF),
                        pltpu.VMEM((S5W, GN), BF),pltpu.VMEM((SUB, GN), F32), pltpu.VMEM((SUB, GN), F32)],
        compiler_params=_params(52),
    )(dya, y, ua, sr, si, sr, si, bbr, bbi, ccr, cci, dsk, con_rev, w_glu, b_glu)


def _lru_gate_terms(rg, sp):
    log_a = -LRU_C * rg * sp
    a = jnp.exp(log_a)
    mult = jnp.sqrt(_neg_expm1(2.0 * log_a))
    return a, mult


def _lru_fwd(ub, conv_w, conv_b, wr, wi, b_r, b_i, sp, carry=None):
    L = ub.shape[0]
    n_slab = TM // SUB

    def body(ub_ref, cw_ref, cb_ref, wr_ref, wi_ref, br_ref, bi_ref, sp_ref,
             xc_ref, rg_ref, ig_ref, h_ref, hp_ref, a_ref, halo_ref, carry_ref):
        @pl.when(pl.program_id(0) == 0)
        def _():
            halo_ref[...] = jnp.zeros_like(halo_ref)
            carry_ref[...] = jnp.zeros_like(carry_ref)

        row = _row_iota(LW)
        taps = [cw_ref[k:k + 1, :] for k in range(4)]
        cb = cb_ref[...]

        def conv_step(k, prev):
            rows = _slab(k)
            cur = ub_ref[rows, :]
            acc = taps[3] * cur + cb
            for j in (1, 2, 3):
                acc = acc + taps[3 - j] * pltpu.roll(jnp.where(row >= SUB - j, prev, cur), j, 0)
            xc_ref[rows, :] = acc
            return cur

        halo_ref[...] = lax.fori_loop(0, n_slab, conv_step, halo_ref[...])

        xc = xc_ref[...]
        xcb = xc.astype(BF)
        rg = _sig(jnp.dot(xcb, wr_ref[...], preferred_element_type=F32) + br_ref[...])
        ig = _sig(jnp.dot(xcb, wi_ref[...], preferred_element_type=F32) + bi_ref[...])
        rg_ref[...] = rg
        ig_ref[...] = ig
        a, mult = _lru_gate_terms(rg, sp_ref[...])
        a_ref[...] = a
        h_ref[...] = mult * ig * xc

        rowc = _row_iota(LC)
        for lc in range(LW // LC):
            cols = slice(lc * LC, (lc + 1) * LC)

            def step(k, c, cols=cols):
                rows = _slab(k)
                av, b = a_ref[rows, cols], h_ref[rows, cols]
                for sh in (1, 2, 4):
                    keep = rowc >= sh
                    b = b + av * jnp.where(keep, pltpu.roll(b, sh, 0), 0.0)
                    av = av * jnp.where(keep, pltpu.roll(av, sh, 0), 1.0)
                h = b + av * c
                h_ref[rows, cols] = h
                hp_ref[rows, cols] = jnp.where(rowc == 0, c, pltpu.roll(h, 1, 0))
                return _bcast_row(h, SUB - 1)

            carry_ref[:, cols] = lax.fori_loop(0, n_slab, step, carry_ref[:, cols])

    return _pallas_call(
        body, carry, name="lru_fwd", grid=(L // TM,),
        in_specs=[_tok(LW), _full((4, LW)), _full((1, LW)), _full((LW, LW)), _full((LW, LW)),
                  _full((1, LW)), _full((1, LW)), _full((1, LW))],
        out_specs=[_tok(LW)] * 5,
        out_shape=[_sds((L, LW))] * 5,
        scratch_shapes=[pltpu.VMEM((TM, LW), F32), pltpu.VMEM((SUB, LW), F32), pltpu.VMEM((SUB, LW), F32)],
        compiler_params=_params(40),
    )(ub, conv_w, conv_b, wr, wi, b_r, b_i, sp)


def _lru_bwd(dyb, xc, rg, ig, hp, ub, conv_w, wr, wi, sp, dsp, carry=None):
    L = ub.shape[0]
    nt = L // TM
    spt = TM // SUB
    n_slab = spt

    def halo_map(i):
        return (jnp.maximum((nt - 1 - i) * spt - 1, 0), 0)

    def body(dh_ref, xc_ref, rg_ref, ig_ref, hp_ref, ub_ref, uh_ref, cw_ref, wr_ref, wi_ref, sp_ref, dsp_ref,
             dub_ref, dpr_ref, dpi_ref, acc_ref, a_ref, lam_ref, dxc_ref, carry_ref, next_ref):
        i = pl.program_id(0)

        @pl.when(i == 0)
        def _():
            carry_ref[...] = jnp.zeros_like(carry_ref)
            next_ref[...] = jnp.zeros_like(next_ref)
            acc_ref[...] = jnp.zeros_like(acc_ref)

        sp = sp_ref[...]
        rg, ig, xc = rg_ref[...], ig_ref[...], xc_ref[...]
        a, mult = _lru_gate_terms(rg, sp)
        a_ref[...] = a

        rowc = _row_iota(LC)
        for lc in range(LW // LC):
            cols = slice(lc * LC, (lc + 1) * LC)

            def step(k, c, cols=cols):
                rows = _slab(n_slab - 1 - k)
                av, dh = a_ref[rows, cols], dh_ref[rows, cols]
                b = av * dh
                for sh in (1, 2, 4):
                    keep = rowc < SUB - sh
                    b = b + av * jnp.where(keep, pltpu.roll(b, SUB - sh, 0), 0.0)
                    av = av * jnp.where(keep, pltpu.roll(av, SUB - sh, 0), 1.0)
                mu = b + av * c
                lam_ref[rows, cols] = dh + jnp.where(rowc == SUB - 1, c, pltpu.roll(mu, SUB - 1, 0))
                return _bcast_row(mu, 0)

            carry_ref[:, cols] = lax.fori_loop(0, n_slab, step, carry_ref[:, cols])

        lam = lam_ref[...]
        d_a = lam * hp_ref[...]
        d_mult = lam * ig * xc
        d_ig = lam * mult * xc
        dxc = lam * mult * ig
        d_log_a = d_a * a - d_mult * a * a / mult
        d_rg = (-LRU_C) * sp * d_log_a
        acc_ref[0:1, :] += _colsum((-LRU_C) * rg * d_log_a) * dsp_ref[...]
        dpr = d_rg * rg * (1.0 - rg)
        dpi = d_ig * ig * (1.0 - ig)
        acc_ref[1:2, :] += _colsum(dpr)
        acc_ref[2:3, :] += _colsum(dpi)
        dprb, dpib = dpr.astype(BF), dpi.astype(BF)
        dpr_ref[...] = dprb
        dpi_ref[...] = dpib
        dxc = dxc + _mm_nt(dprb, wr_ref[...]) + _mm_nt(dpib, wi_ref[...])
        dxc_ref[...] = dxc
        acc_ref[3:4, :] += _colsum(dxc)

        row = _row_iota(LW)
        taps = [cw_ref[k:k + 1, :] for k in range(4)]
        u_halo = jnp.where(i == nt - 1, 0.0, uh_ref[...])
        nxt_tile = next_ref[...]

        def conv_step(k, accs):
            rows = _slab(k)
            cur = dxc_ref[rows, :]
            nxt = jnp.where(k == n_slab - 1, nxt_tile, dxc_ref[_slab(jnp.minimum(k + 1, n_slab - 1)), :])
            ucur = ub_ref[rows, :]
            uprev = jnp.where(k == 0, u_halo, ub_ref[_slab(jnp.maximum(k - 1, 0)), :])
            du = taps[3] * cur
            new = [accs[3] + cur * ucur]
            for j in (1, 2, 3):
                du = du + taps[3 - j] * pltpu.roll(jnp.where(row < j, nxt, cur), SUB - j, 0)
                new.append(accs[3 - j] + cur * pltpu.roll(jnp.where(row >= SUB - j, uprev, ucur), j, 0))
            dub_ref[rows, :] = du
            return tuple(new[::-1])

        zero = jnp.zeros((SUB, LW), F32)
        accs = lax.fori_loop(0, n_slab, conv_step, (zero, zero, zero, zero))
        for k in range(4):
            acc_ref[4 + k:5 + k, :] += _colsum(accs[k])
        next_ref[...] = dxc_ref[0:SUB, :]

    return _pallas_call(
        body, carry, name="lru_bwd", grid=(nt,),
        in_specs=[_tok_rev(LW, nt)] * 6 + [pl.BlockSpec((SUB, LW), halo_map), _full((4, LW)),
                                           _full((LW, LW)), _full((LW, LW)), _full((1, LW)), _full((1, LW))],
        out_specs=[_tok_rev(LW, nt), _tok_rev(LW, nt), _tok_rev(LW, nt), _full((SUB, LW))],
        out_shape=[_sds((L, LW)), _sds((L, LW), BF), _sds((L, LW), BF), _sds((SUB, LW))],
        scratch_shapes=[pltpu.VMEM((TM, LW), F32), pltpu.VMEM((TM, LW), F32), pltpu.VMEM((TM, LW), F32),
                        pltpu.VMEM((SUB, LW), F32), pltpu.VMEM((SUB, LW), F32)],
        compiler_params=_params(48),
    )(dyb, xc, rg, ig, hp, ub, ub, conv_w, wr, wi, sp, dsp)


AC = D // NCHIP


def _merge_fwd(x, ya, yb, gp, w_a, w_b, w_o, carry=None):
    L = x.shape[0]

    def body(x_ref, ya_ref, yb_ref, gp_ref, wa_ref, wb_ref, wo_ref, x1_ref, pa_ref, pb_ref, mg_ref):
        ya = ya_ref[...]
        for k in range(NCHIP):
            pa_ref[:, k * AC:(k + 1) * AC] = jnp.dot(ya, wa_ref[k], preferred_element_type=F32)
        pb = _mm(yb_ref[...], wb_ref[...])
        pb_ref[...] = pb
        gp = gp_ref[...]
        merged = (_sig(gp[:, :D]) * pa_ref[...] + _sig(gp[:, D:]) * pb).astype(BF)
        mg_ref[...] = merged
        x1_ref[...] = x_ref[...] + jnp.dot(merged, wo_ref[...], preferred_element_type=F32)

    return _pallas_call(
        body, carry, name="merge_fwd", grid=(L // TM,),
        in_specs=[_tok(D), _tok(S5W), _tok(LW), _tok(2 * D), _full((NCHIP, S5W, AC)), _full((LW, D)), _full((D, D))],
        out_specs=[_tok(D), _tok(D), _tok(D), _tok(D)],
        out_shape=[_sds((L, D)), _sds((L, D)), _sds((L, D)), _sds((L, D), BF)],
        compiler_params=_params(40),
    )(x, ya, yb, gp, w_a, w_b, w_o)


def _merge_bwd(dx1, gp, pa, pb, w_a, w_b, w_o, carry=None):
    L = dx1.shape[0]

    def body(dx1_ref, gp_ref, pa_ref, pb_ref, wa_ref, wb_ref, wo_ref, dya_ref, dyb_ref, dgp_ref, dpa_ref, dpb_ref):
        dm = _mm_nt(dx1_ref[...], wo_ref[...])
        gp = gp_ref[...]
        sa, sb = _sig(gp[:, :D]), _sig(gp[:, D:])
        dpa = (dm * sa).astype(BF)
        dpb = (dm * sb).astype(BF)
        dpa_ref[...] = dpa
        dpb_ref[...] = dpb
        dgp_ref[:, :D] = dm * pa_ref[...] * sa * (1.0 - sa)
        dgp_ref[:, D:] = dm * pb_ref[...] * sb * (1.0 - sb)
        dya = jnp.zeros((TM, S5W), F32)
        for k in range(NCHIP):
            dya = dya + _mm_nt(dpa[:, k * AC:(k + 1) * AC], wa_ref[k])
        dya_ref[...] = dya
        dyb_ref[...] = _mm_nt(dpb, wb_ref[...])

    return _pallas_call(
        body, carry, name="merge_bwd", grid=(L // TM,),
        in_specs=[_tok(D), _tok(2 * D), _tok(D), _tok(D), _full((NCHIP, S5W, AC)), _full((LW, D)), _full((D, D))],
        out_specs=[_tok(S5W), _tok(LW), _tok(2 * D), _tok(D), _tok(D)],
        out_shape=[_sds((L, S5W)), _sds((L, LW)), _sds((L, 2 * D)), _sds((L, D), BF), _sds((L, D), BF)],
        compiler_params=_params(40),
    )(dx1, gp, pa, pb, w_a, w_b, w_o)


def _chunk_tok(width):
    return pl.BlockSpec((NCHIP, TM, width), lambda i: (0, i, 0))


def _ffn_fwd(x1, g_ffn, wg, wu, wd, carry=None):
    L = x1.shape[0]

    def body(x_ref, g_ref, wg_hbm, wu_hbm, wd_hbm, x2_ref, h2_ref, gg_ref, uu_ref, wg_vm, wu_vm, wd_vm):
        @pl.when(pl.program_id(0) == 0)
        def _():
            pltpu.sync_copy(wg_hbm, wg_vm)
            pltpu.sync_copy(wu_hbm, wu_vm)
            pltpu.sync_copy(wd_hbm, wd_vm)

        x = x_ref[...]
        xh, _ = _rms(x)
        h2 = (xh * g_ref[...]).astype(BF)
        h2_ref[...] = h2
        out = x
        for c in range(NCHIP):
            gg = lax.dot_general(h2, wg_vm[c], (((1,), (1,)), ((), ())), preferred_element_type=F32)
            uu = lax.dot_general(h2, wu_vm[c], (((1,), (1,)), ((), ())), preferred_element_type=F32)
            gg_ref[c] = gg.astype(BF)
            uu_ref[c] = uu.astype(BF)
            act = (gg * _sig(gg) * uu).astype(BF)
            out = out + jnp.dot(act, wd_vm[c], preferred_element_type=F32)
        x2_ref[...] = out

    return _pallas_call(
        body, carry, name="ffn_fwd", grid=(L // TM,),
        in_specs=[_tok(D), _full((1, D)), ANY, ANY, ANY],
        out_specs=[_tok(D), _tok(D), _chunk_tok(FC), _chunk_tok(FC)],
        out_shape=[_sds((L, D)), _sds((L, D), BF), _sds((NCHIP, L, FC), BF), _sds((NCHIP, L, FC), BF)],
        scratch_shapes=[pltpu.VMEM((NCHIP, FC, D), BF)] * 3,
        compiler_params=_params(52),
    )(x1, g_ffn, wg, wu, wd)


def _ffn_bwd(x1, dx2, gg, uu, g_ffn, wg, wu, wd, carry=None):
    L = x1.shape[0]

    def body(x_ref, dx2_ref, gg_ref, uu_ref, g_ref, wg_hbm, wu_hbm, wd_hbm,
             dx1_ref, act_ref, dgg_ref, duu_ref, dg_ref, wg_vm, wu_vm, wd_vm):
        @pl.when(pl.program_id(0) == 0)
        def _():
            pltpu.sync_copy(wg_hbm, wg_vm)
            pltpu.sync_copy(wu_hbm, wu_vm)
            pltpu.sync_copy(wd_hbm, wd_vm)
            dg_ref[...] = jnp.zeros_like(dg_ref)

        dx2 = dx2_ref[...]
        dx2b = dx2.astype(BF)
        dh2 = jnp.zeros((TM, D), F32)
        for c in range(NCHIP):
            g = gg_ref[c].astype(F32)
            u = uu_ref[c].astype(F32)
            s = _sig(g)
            silu = g * s
            act_ref[c] = (silu * u).astype(BF)
            dact = lax.dot_general(dx2b, wd_vm[c], (((1,), (1,)), ((), ())), preferred_element_type=F32)
            dg = (dact * u * s * (1.0 + g * (1.0 - s))).astype(BF)
            du = (dact * silu).astype(BF)
            dgg_ref[c] = dg
            duu_ref[c] = du
            dh2 = dh2 + jnp.dot(dg, wg_vm[c], preferred_element_type=F32)
            dh2 = dh2 + jnp.dot(du, wu_vm[c], preferred_element_type=F32)
        xh, r = _rms(x_ref[...])
        dg_ref[0:1, :] += _colsum(dh2 * xh)
        dx1_ref[...] = dx2 + _rms_bwd(dh2, xh, r, g_ref[...])

    return _pallas_call(
        body, carry, name="ffn_bwd", grid=(L // TM,),
        in_specs=[_tok(D), _tok(D), _chunk_tok(FC), _chunk_tok(FC), _full((1, D)), ANY, ANY, ANY],
        out_specs=[_tok(D), _chunk_tok(FC), _chunk_tok(FC), _chunk_tok(FC), _full((SUB, D))],
        out_shape=[_sds((L, D)), _sds((NCHIP, L, FC), BF), _sds((NCHIP, L, FC), BF), _sds((NCHIP, L, FC), BF),
                   _sds((SUB, D))],
        scratch_shapes=[pltpu.VMEM((NCHIP, FC, D), BF)] * 3,
        compiler_params=_params(56),
    )(x1, dx2, gg, uu, g_ffn, wg, wu, wd)


def _ple_loss(x2, p, tgt, g_pg, w_pg, b_pg, w_ple, g_ple, g_final):
    L = x2.shape[0]

    def body(x2_ref, p_ref, t_ref, gpg_ref, wpg_ref, bpg_ref, wple_ref, gple_ref, gf_ref,
             dx2_ref, n2_ref, dpre_ref, de0_ref, acc_ref):
        @pl.when(pl.program_id(0) == 0)
        def _():
            acc_ref[...] = jnp.zeros_like(acc_ref)

        x2 = x2_ref[...]
        x2h, r2 = _rms(x2)
        n2 = (x2h * gpg_ref[...]).astype(BF)
        n2_ref[...] = n2
        gate = _sig(jnp.dot(n2, wpg_ref[...], preferred_element_type=F32) + bpg_ref[...])
        pb = p_ref[...].astype(BF)
        e0 = jnp.concatenate([jnp.dot(pb, wple_ref[k], preferred_element_type=F32) for k in range(NCHIP)], axis=1)
        e0h, re = _rms(e0)
        e = e0h * gple_ref[...]
        x3 = x2 + gate * e
        x3h, r3 = _rms(x3)
        diff = x3h * gf_ref[...] - t_ref[...]
        acc_ref[4:5, :] += _colsum(diff * diff) * (0.5 / D)
        dy = diff * (1.0 / D)
        acc_ref[3:4, :] += _colsum(dy * x3h)
        dx3 = _rms_bwd(dy, x3h, r3, gf_ref[...])
        de = dx3 * gate
        acc_ref[2:3, :] += _colsum(de * e0h)
        de0_ref[...] = _rms_bwd(de, e0h, re, gple_ref[...]).astype(BF)
        dpre = dx3 * e * gate * (1.0 - gate)
        acc_ref[1:2, :] += _colsum(dpre)
        dpreb = dpre.astype(BF)
        dpre_ref[...] = dpreb
        dn2 = lax.dot_general(dpreb, wpg_ref[...], (((1,), (1,)), ((), ())), preferred_element_type=F32)
        acc_ref[0:1, :] += _colsum(dn2 * x2h)
        dx2_ref[...] = dx3 + _rms_bwd(dn2, x2h, r2, gpg_ref[...])

    return _pallas_call(
        body, name="ple_loss", grid=(L // TM,),
        in_specs=[_tok(D), _tok(PLE), _tok(D), _full((1, D)), _full((D, D)), _full((1, D)), _full((NCHIP, PLE, AC)),
                  _full((1, D)), _full((1, D))],
        out_specs=[_tok(D), _tok(D), _tok(D), _tok(D), _full((SUB, D))],
        out_shape=[_sds((L, D)), _sds((L, D), BF), _sds((L, D), BF), _sds((L, D), BF), _sds((SUB, D))],
        compiler_params=_params(40),
    )(x2, p, tgt, g_pg, w_pg, b_pg, w_ple, g_ple, g_final)


def _tn(name, a, b, col_chunk=None, a_block=None, carry=None):
    L = a.shape[-2]
    m, n = a.shape[-1], b.shape[-1]
    a_col = 0
    if a_block is not None:
        a_col, m = a_block
    if a.ndim == 3 or b.ndim == 3:
        nj, bn = (a if a.ndim == 3 else b).shape[0], n
        a_spec = (pl.BlockSpec((None, TK, m), lambda j, t: (j, t, 0)) if a.ndim == 3
                  else pl.BlockSpec((TK, m), lambda j, t: (t, 0)))
        b_spec = (pl.BlockSpec((None, TK, n), lambda j, t: (j, t, 0)) if b.ndim == 3
                  else pl.BlockSpec((TK, n), lambda j, t: (t, 0)))
        out_spec, out_shape = pl.BlockSpec((None, m, n), lambda j, t: (j, 0, 0)), _sds((nj, m, n))
    else:
        bn = col_chunk
        if bn is None:
            bn = next((cand for cand in (1024, 512) if n > cand and n % cand == 0), n)
        nj = n // bn
        a_spec = pl.BlockSpec((TK, m), lambda j, t: (t, a_col))
        b_spec = pl.BlockSpec((TK, bn), lambda j, t: (t, j))
        if col_chunk is None:
            out_spec, out_shape = pl.BlockSpec((m, bn), lambda j, t: (0, j)), _sds((m, n))
        else:
            out_spec, out_shape = pl.BlockSpec((None, m, bn), lambda j, t: (j, 0, 0)), _sds((nj, m, bn))

    def body(a_ref, b_ref, o_ref):
        @pl.when(pl.program_id(1) == 0)
        def _():
            o_ref[...] = jnp.zeros_like(o_ref)

        o_ref[...] += _mm_tn(a_ref[...], b_ref[...])

    outs = _pallas_call(
        body, carry, name=name, grid=(nj, L // TK), in_specs=[a_spec, b_spec], out_specs=[out_spec],
        out_shape=[pltpu.HBM(out_shape.shape, out_shape.dtype)],
        compiler_params=pltpu.CompilerParams(dimension_semantics=("arbitrary", "arbitrary"),
                                             vmem_limit_bytes=40 * VMEM_MB),
    )(a, b)
    return outs[0] if carry is None else outs


LANE = 128


def _tn_blocks(name, a, bs, ga, gb, carry=None):
    L, m, n, nb = a.shape[0], a.shape[1], bs[0].shape[1], len(bs)
    per = LANE // ga
    wb = per * gb
    n_super = m // LANE

    def body(a_ref, *refs):
        b_refs, o_refs, acc_refs = refs[:nb], refs[nb:2 * nb], refs[2 * nb:]
        t = pl.program_id(0)

        @pl.when(t == 0)
        def _():
            for acc in acc_refs:
                acc[...] = jnp.zeros_like(acc)

        lhs = a_ref[...].astype(BF)
        for b_ref, acc in zip(b_refs, acc_refs):
            rhs = b_ref[...].astype(BF)
            for j in range(n_super):
                acc[j] += _mm_tn(lhs[:, j * LANE:(j + 1) * LANE], rhs[:, j * wb:(j + 1) * wb])

        @pl.when(t == L // TK - 1)
        def _():
            own = (lax.broadcasted_iota(jnp.int32, (LANE, wb), 0) // ga) == (lax.broadcasted_iota(jnp.int32, (LANE, wb), 1) // gb)
            for o_ref, acc in zip(o_refs, acc_refs):
                for j in range(n_super):
                    kept = jnp.where(own, acc[j], 0.0)
                    o_ref[:, j * wb:(j + 1) * wb] = jnp.sum(kept.reshape(per, ga, wb), axis=0)

    outs = _pallas_call(
        body, carry, name=name, grid=(L // TK,),
        in_specs=[pl.BlockSpec((TK, m), lambda t: (t, 0))] + [pl.BlockSpec((TK, n), lambda t: (t, 0))] * nb,
        out_specs=[_full((ga, n))] * nb, out_shape=[_sds((ga, n))] * nb,
        scratch_shapes=[pltpu.VMEM((n_super, LANE, wb), F32)] * nb,
        compiler_params=_params(48),
    )(*_in_hbm([a] + list(bs)))
    return list(outs)


def _s5_discretize(lam_re, lam_im, log_dt, b_re, b_im):
    dt = jnp.exp(log_dt)[:, None]
    mag = jnp.exp(lam_re * dt)
    ar = mag * jnp.cos(lam_im * dt)
    ai = mag * jnp.sin(lam_im * dt)
    den = lam_re * lam_re + lam_im * lam_im
    nr = ar - 1.0
    fr = (nr * lam_re + ai * lam_im) / den
    fi = (ai * lam_re - nr * lam_im) / den
    bbr = fr[:, None, :] * b_re - fi[:, None, :] * b_im
    bbi = fr[:, None, :] * b_im + fi[:, None, :] * b_re
    return ar, ai, bbr, bbi


def _prepare(by_rows, block_cols, ar, ai):
    n = len(by_rows)

    def body(*refs):
        srcs, (ar_ref, ai_ref), dense, (con_ref, rev_ref) = refs[:n], refs[n:n + 2], refs[n + 2:2 * n + 2], refs[2 * n + 2:]
        for src, out, c in zip(srcs, dense, block_cols):
            r, width = src.shape
            groups = width // c
            tiled = jnp.broadcast_to(src[...][None], (groups, r, width)).reshape(groups * r, width)
            own = (lax.broadcasted_iota(jnp.int32, tiled.shape, 0) // r) == (lax.broadcasted_iota(jnp.int32, tiled.shape, 1) // c)
            out[...] = jnp.where(own, tiled, 0.0).astype(BF)
        a_r, a_i = ar_ref[...], ai_ref[...]
        pw = [(jnp.ones_like(a_r), jnp.zeros_like(a_i))]
        for _ in range(SUB):
            pr, pi = pw[-1]
            pw.append((pr * a_r - pi * a_i, pr * a_i + pi * a_r))
        row = _row_iota(GN)
        for ref, reverse in ((con_ref, False), (rev_ref, True)):
            sign = -1.0 if reverse else 1.0
            for j, sh in enumerate((1, 2, 4)):
                keep = (row < SUB - sh) if reverse else (row >= sh)
                ref[2 * j * SUB:(2 * j + 1) * SUB, :] = jnp.where(keep, pw[sh][0], 0.0)
                ref[(2 * j + 1) * SUB:(2 * j + 2) * SUB, :] = jnp.where(keep, sign * pw[sh][1], 0.0)
            p_r, p_i = jnp.zeros((SUB, GN), F32), jnp.zeros((SUB, GN), F32)
            for i in range(SUB):
                k = SUB - i if reverse else i + 1
                p_r = jnp.where(row == i, pw[k][0], p_r)
                p_i = jnp.where(row == i, sign * pw[k][1], p_i)
            ref[6 * SUB:7 * SUB, :] = p_r
            ref[7 * SUB:8 * SUB, :] = p_i

    dense_shapes = [(b.shape[1] // c * b.shape[0], b.shape[1]) for b, c in zip(by_rows, block_cols)]
    outs = _pallas_call(
        body, name="prepare", grid=(1,), in_specs=[_full(b.shape) for b in by_rows] + [_full((1, GN))] * 2,
        out_specs=[_full(s) for s in dense_shapes] + [_full((8 * SUB, GN))] * 2,
        out_shape=[_sds(s, BF) for s in dense_shapes] + [_sds((8 * SUB, GN))] * 2,
        compiler_params=_params(48),
    )(*by_rows, ar, ai)
    return outs[:n], outs[n], outs[n + 1]


def _local_step(x, p, tgt, w, comm):
    rows_of = lambda a: a.reshape(NCHIP * a.shape[1], a.shape[2])
    quarters = lambda a: a.reshape(NCHIP, a.shape[0] // NCHIP, a.shape[1])

    def gathering(names, call):
        carry = comm.gather(names)
        outs = list(call(carry))
        own = len(outs) - len(carry.out_shapes)
        w.update(zip(names, outs[own:]))
        return outs[:own]

    w.update(comm.first())
    w_glu = rows_of(w["w_glu"])
    ar, ai, bbr, bbi = _s5_discretize(w["lam_re"], w["lam_im"], w["log_dt"], w["s5_b_re"], w["s5_b_im"])
    by_row = lambda b: jnp.transpose(b, (1, 0, 2)).reshape(b.shape[1], -1)
    (bbr_d, bbi_d, ccr_d, cci_d, wr_d, wi_d), con, con_rev = _prepare(
        [by_row(b) for b in (bbr, bbi, w["s5_c_re"], w["s5_c_im"], w["w_r"], w["w_i"])], [NS] * 4 + [HD] * 2,
        ar.reshape(1, GN), ai.reshape(1, GN))
    dsk = w["s5_d"].reshape(1, S5W)
    lam = w["lru_lambda"].reshape(1, LW)
    sp = jax.nn.softplus(-lam)
    b_r, b_i = w["b_r"].reshape(1, LW), w["b_i"].reshape(1, LW)
    row = lambda name: w[name].reshape(1, -1)

    h, ua, ub, gp = gathering(["w_a_out", "w_b_out", "w_o"], lambda carry: _inproj_fwd(
        x, row("g_mix"), w["w_in"], row("b_in"), carry))
    sr, si, y, zg, ya = gathering(["w_ffn_gate"], lambda carry: _s5_fwd(
        ua, bbr_d, bbi_d, ccr_d, cci_d, dsk, con, w_glu, row("b_glu"), carry))
    xc, rg, ig, yb, hp = gathering(["w_ffn_up"], lambda carry: _lru_fwd(
        ub, w["conv_w"], row("conv_b"), wr_d, wi_d, b_r, b_i, sp, carry))
    w_b_out, w_o = rows_of(w["w_b_out"]), rows_of(w["w_o"])
    x1, pa, pb, merged = gathering(["w_ffn_down"], lambda carry: _merge_fwd(
        x, ya, yb, gp, w["w_a_out"], w_b_out, w_o, carry))
    x2, h2, gg, uu = gathering(["w_ple_gate", "w_ple"], lambda carry: _ffn_fwd(
        x1, row("g_ffn"), w["w_ffn_gate"], w["w_ffn_up"], w["w_ffn_down"], carry))
    w_pg = rows_of(w["w_ple_gate"])
    dx2, n2, dpre, de0, acc_p = _ple_loss(x2, p, tgt, row("g_ple_gate"), w_pg, row("b_ple_gate"),
                                          w["w_ple"], row("g_ple"), row("g_final"))
    comm.reduce("ple", {"w_ple_gate": quarters(_tn("dw_ple_gate", n2, dpre)),
                        "w_ple": _tn("dw_ple", p, de0, col_chunk=AC)})
    dx1, act, dgg, duu, acc_f = comm.run(lambda carry: _ffn_bwd(
        x1, dx2, gg, uu, row("g_ffn"), w["w_ffn_gate"], w["w_ffn_up"], w["w_ffn_down"], carry))
    comm.reduce("ffn_gate", {"w_ffn_gate": _tn("dw_ffn_gate", dgg, h2)})
    comm.reduce("ffn_up", {"w_ffn_up": comm.run(lambda carry: _tn("dw_ffn_up", duu, h2, carry=carry))[0]})
    comm.reduce("ffn_down", {"w_ffn_down": comm.run(lambda carry: _tn("dw_ffn_down", act, dx2, carry=carry))[0]})
    dya, dyb, dgp, dpa, dpb = comm.run(lambda carry: _merge_bwd(
        dx1, gp, pa, pb, w["w_a_out"], w_b_out, w_o, carry))
    comm.reduce("merge", {"w_o": quarters(_tn("dw_o", merged, dx1)), "w_a_out": _tn("dw_a_out", ya, dpa, col_chunk=AC),
                          "w_b_out": quarters(_tn("dw_b_out", yb, dpb))})
    dua, dq, dy, lr, li, acc_a, acc_s = comm.run(lambda carry: _s5_bwd(
        dya, y, ua, sr, si, bbr_d, bbi_d, ccr_d, cci_d, dsk, con_rev, w_glu, row("b_glu"), carry))
    dub, dpr, dpi, acc_l = comm.run(lambda carry: _lru_bwd(
        dyb, xc, rg, ig, hp, ub, w["conv_w"], wr_d, wi_d, sp, -_sig(-lam), carry))
    gx, dz, acc_g, acc_b = _inproj_bwd(x, dx1, dua, dub, dgp, row("g_mix"), w["w_in"])
    half = (D // 2,)
    comm.reduce("in_lo", {"w_in_lo": comm.run(lambda carry: _tn(
        "dw_in_lo", h, dz, col_chunk=QC, a_block=(0,) + half, carry=carry))[0]})
    comm.reduce("in_hi", {"w_in_hi": comm.run(lambda carry: _tn(
        "dw_in_hi", h, dz, col_chunk=QC, a_block=(1,) + half, carry=carry))[0], "w_glu": quarters(_tn("dw_glu", zg, dq))})
    d_wr, d_wi = comm.run(lambda carry: _tn_blocks("dw_r_i", xc, [dpr, dpi], HD, HD, carry))
    d_bbr, d_bbi = comm.run(lambda carry: _tn_blocks("d_bb", ua, [lr, li], NP, NS, carry))
    d_ccr, d_cci = comm.run(lambda carry: _tn_blocks("d_cc", dy, [sr, si], NP, NS, carry))
    comm.drain()
    sums = {"ple": acc_p, "ffn": acc_f, "mix": acc_g, "b_in": acc_b, "lru": acc_l, "s5": acc_s, "s5_a": acc_a}
    blocks = {"bb_re": d_bbr, "bb_im": d_bbi,
              "cc_re": d_ccr, "cc_im": d_cci,
              "w_r": d_wr, "w_i": d_wi}
    return gx, sums, blocks


def _replicated_grads(w, sums, blocks):
    grouped = lambda e, groups: jnp.transpose(e.reshape(e.shape[0], groups, -1), (1, 0, 2))
    d_ar, d_ai = sums["s5_a"][0].reshape(NG, NS), sums["s5_a"][1].reshape(NG, NS)
    d_bbr, d_bbi = grouped(blocks["bb_re"], NG), grouped(blocks["bb_im"], NG)
    _, vjp = jax.vjp(_s5_discretize, w["lam_re"], w["lam_im"], w["log_dt"], w["s5_b_re"], w["s5_b_im"])
    g = dict(zip(("lam_re", "lam_im", "log_dt", "s5_b_re", "s5_b_im"), vjp((d_ar, d_ai, d_bbr, d_bbi))))
    g["s5_c_re"] = grouped(blocks["cc_re"], NG)
    g["s5_c_im"] = -grouped(blocks["cc_im"], NG)
    g["w_r"], g["w_i"] = grouped(blocks["w_r"], NH), grouped(blocks["w_i"], NH)
    g["s5_d"] = sums["s5"][0].reshape(NG, NP)
    g["b_r"] = sums["lru"][1].reshape(NH, HD)
    g["b_i"] = sums["lru"][2].reshape(NH, HD)
    return g


ACC_ROWS = {"g_mix": ("mix", 0), "b_in": ("b_in", 0), "g_ffn": ("ffn", 0), "g_ple_gate": ("ple", 0),
            "b_ple_gate": ("ple", 1), "g_ple": ("ple", 2), "g_final": ("ple", 3), "b_glu": ("s5", 1),
            "lru_lambda": ("lru", 0), "conv_b": ("lru", 3)}
LOSS_ROW = ("ple", 4)
CONV_W_ROWS = ("lru", 4)


SHARDED = [("w_in", (D, QC)), ("w_glu", (S5W // NCHIP, S5W)), ("w_a_out", (S5W, AC)), ("w_b_out", (LW // NCHIP, D)),
           ("w_o", (D // NCHIP, D)), ("w_ffn_gate", (FC, D)), ("w_ffn_up", (FC, D)), ("w_ffn_down", (FC, D)),
           ("w_ple_gate", (D // NCHIP, D)), ("w_ple", (PLE, AC))]
NSH = len(SHARDED)
TRANSPOSED = ("w_ffn_gate", "w_ffn_up", "s5_b_re", "s5_b_im")
CONV_SHARD = (4, LW // NCHIP)


def _mesh_pos():
    return lax.axis_index("x"), lax.axis_index("y"), lax.axis_index("c")


def _other_chips(x, y):
    return [(1 - x, y), (x, 1 - y), (1 - x, 1 - y)]


def _half_rows(c, rows, align):
    return pl.ds(pl.multiple_of(c * (rows // 2), align), rows // 2)


def _run_now(name, carry):
    c_in, c_out = len(carry.operands), len(carry.out_shapes)

    def body(*refs):
        ins, outs, sems = refs[:c_in], refs[c_in:c_in + c_out], refs[c_in + c_out:]
        carry.start(ins, outs, sems)
        carry.finish(ins, outs, sems)

    return pl.pallas_call(body, name=name, in_specs=[ANY] * c_in, out_specs=[ANY] * c_out,
                          out_shape=list(carry.out_shapes), scratch_shapes=list(carry.sems),
                          input_output_aliases=dict(carry.aliases))(*_in_hbm(carry.operands))


def _gather_group(shards, split):
    n = len(shards)

    def copies(srcs, outs, sems):
        send_sems, recv_sems = sems
        x, y, c = _mesh_pos()
        k0 = 2 * x + y
        sib = (x, y, 1 - c)
        chips = _other_chips(x, y)

        def remote(src, dst, j, i, to):
            return pltpu.make_async_remote_copy(src_ref=src, dst_ref=dst, send_sem=send_sems.at[j, i],
                                                recv_sem=recv_sems.at[j, i], device_id=to, device_id_type=MESH)

        def rows(ref, i, core, *lead):
            if not split[i]:
                return ref.at[lead] if lead else ref
            return ref.at[(*lead, _half_rows(core, shards[i].shape[0], 16))]

        own = [remote(s, o.at[k0], 6, i, sib) for i, (s, o) in enumerate(zip(srcs, outs))]
        ici, landed, fwd, fwd_landed = [], [], [], []
        for j, chip in enumerate(chips):
            kj = 2 * chip[0] + chip[1]
            pairs = list(enumerate(zip(srcs, outs)))
            ici.append([remote(rows(s, i, c), rows(o, i, c, k0), j, i, (*chip, c)) for i, (s, o) in pairs])
            landed.append([remote(rows(s, i, c), rows(o, i, c, kj), j, i, (*chip, c)) for i, (s, o) in pairs])
            fwd.append([remote(rows(o, i, c, kj), rows(o, i, c, kj), 3 + j, i, sib) for i, (s, o) in pairs if split[i]])
            fwd_landed.append([remote(rows(o, i, 1 - c, kj), rows(o, i, 1 - c, kj), 3 + j, i, sib)
                               for i, (s, o) in pairs if split[i]])
        return own, ici, landed, fwd, fwd_landed

    def start(srcs, outs, sems):
        own, ici, _, _, _ = copies(srcs, outs, sems)
        for cp in own + [cp for per_chip in ici for cp in per_chip]:
            cp.start()

    def finish(srcs, outs, sems):
        own, ici, landed, fwd, fwd_landed = copies(srcs, outs, sems)
        passed = [i for i in range(n) if split[i]]
        for j in range(3):
            for i, cp in enumerate(landed[j]):
                cp.wait_recv()
                if split[i]:
                    fwd[j][passed.index(i)].start()
        for j in range(3):
            for cp in fwd_landed[j]:
                cp.wait_recv()
        for cp in own:
            cp.wait_recv()
        for cp in own + [cp for per_chip in ici + fwd for cp in per_chip]:
            cp.wait_send()

    return _Carried(shards, [_sds((NCHIP,) + s.shape, s.dtype) for s in shards],
                    [pltpu.SemaphoreType.DMA((7, n)), pltpu.SemaphoreType.DMA((7, n))], start, finish)


def _each_copy(copies, carried, out_shapes, sems, aliases=None):
    def start(ins, outs, sem_refs):
        for cp in copies(ins, outs, sem_refs):
            cp.start()

    def finish(ins, outs, sem_refs):
        for cp in copies(ins, outs, sem_refs):
            cp.wait()

    return _Carried(carried, out_shapes, sems, start, finish, aliases)


def _swap_group(grads):
    n = len(grads)

    def copies(srcs, outs, sems):
        send_sems, recv_sems = sems
        x, y, c = _mesh_pos()
        return [pltpu.make_async_remote_copy(src_ref=s.at[:, _half_rows(1 - c, s.shape[1], 8)], dst_ref=o,
                                             send_sem=send_sems.at[i], recv_sem=recv_sems.at[i], device_id=(x, y, 1 - c),
                                             device_id_type=MESH) for i, (s, o) in enumerate(zip(srcs, outs))]

    return _each_copy(copies, grads, [pltpu.HBM((NCHIP, g.shape[1] // 2, g.shape[2]), F32) for g in grads],
                      [pltpu.SemaphoreType.DMA((n,)), pltpu.SemaphoreType.DMA((n,))])


def _add_sibling_group(tag, c_idx, grads, gots):
    n = len(grads)

    def body(c_ref, *refs):
        for g, rx, p, pb in zip(refs[:n], refs[n:2 * n], refs[2 * n:3 * n], refs[3 * n:]):
            s = g[...] + rx[...]
            p[...] = s
            pb[...] = s.astype(BF)

    halves = [pl.BlockSpec((None,) + rx.shape[1:], lambda k, c_ref: (k, 0, 0)) for rx in gots]
    mine = [pl.BlockSpec((None,) + rx.shape[1:], lambda k, c_ref: (k, c_ref[0], 0)) for rx in gots]
    outs = _pallas_call(
        body, name="add_sibling_" + tag,
        grid_spec=pltpu.PrefetchScalarGridSpec(num_scalar_prefetch=1, grid=(NCHIP,), in_specs=mine + halves,
                                               out_specs=halves + halves),
        out_shape=[pltpu.HBM(rx.shape, F32) for rx in gots] + [pltpu.HBM(rx.shape, BF) for rx in gots],
        compiler_params=_params(48),
    )(c_idx, *_in_hbm(list(grads) + list(gots)))
    return outs[:n], outs[n:]


def _exchange_group(parts):
    n = len(parts)

    def copies(srcs, outs, sems):
        send_sems, recv_sems = sems
        x, y, c = _mesh_pos()
        return [pltpu.make_async_remote_copy(
            src_ref=s.at[2 * chip[0] + chip[1]], dst_ref=o.at[j], send_sem=send_sems.at[j, i],
            recv_sem=recv_sems.at[j, i], device_id=(*chip, c), device_id_type=MESH)
            for j, chip in enumerate(_other_chips(x, y)) for i, (s, o) in enumerate(zip(srcs, outs))]

    return _each_copy(copies, parts, [pltpu.HBM((3,) + p.shape[1:], BF) for p in parts],
                      [pltpu.SemaphoreType.DMA((3, n)), pltpu.SemaphoreType.DMA((3, n))])


def _add_chips_group(tag, kc_idx, parts, arrived):
    n = len(parts)

    def body(kc_ref, *refs):
        for p, rx, t in zip(refs[:n], refs[n:2 * n], refs[2 * n:]):
            t[...] = ((p[...] + rx[0].astype(F32)) + rx[1].astype(F32)) + rx[2].astype(F32)

    outs = _pallas_call(
        body, name="add_chips_" + tag,
        grid_spec=pltpu.PrefetchScalarGridSpec(
            num_scalar_prefetch=1, grid=(1,),
            in_specs=([pl.BlockSpec((None,) + rx.shape[1:], lambda i, kc_ref: (kc_ref[0], 0, 0)) for rx in arrived]
                      + [pl.BlockSpec(rx.shape, lambda i, kc_ref: (0, 0, 0)) for rx in arrived]),
            out_specs=[pl.BlockSpec((None,) + rx.shape[1:], lambda i, kc_ref: (kc_ref[1], 0, 0)) for rx in arrived]),
        out_shape=[pltpu.HBM((2,) + rx.shape[1:], F32) for rx in arrived],
        compiler_params=_params(48),
    )(kc_idx, *_in_hbm(list(parts) + list(arrived)))
    return list(outs)


def _join_group(halves):
    n = len(halves)

    def copies(bufs, sems):
        send_sems, recv_sems = sems
        x, y, c = _mesh_pos()
        sib = (x, y, 1 - c)
        sends = [pltpu.make_async_remote_copy(src_ref=b.at[c], dst_ref=b.at[c], send_sem=send_sems.at[i],
                                              recv_sem=recv_sems.at[i], device_id=sib, device_id_type=MESH)
                 for i, b in enumerate(bufs)]
        landed = [pltpu.make_async_remote_copy(src_ref=b.at[c], dst_ref=b.at[1 - c], send_sem=send_sems.at[i],
                                               recv_sem=recv_sems.at[i], device_id=sib, device_id_type=MESH)
                  for i, b in enumerate(bufs)]
        return sends, landed

    def start(_, bufs, sems):
        for cp in copies(bufs, sems)[0]:
            cp.start()

    def finish(_, bufs, sems):
        sends, landed = copies(bufs, sems)
        for cp in landed:
            cp.wait_recv()
        for cp in sends:
            cp.wait_send()

    return _Carried(halves, [pltpu.HBM(h.shape, F32) for h in halves],
                    [pltpu.SemaphoreType.DMA((n,)), pltpu.SemaphoreType.DMA((n,))], start, finish,
                    {i: i for i in range(n)})


def _combine(carries):
    operands, out_shapes, sems, aliases, spans = [], [], [], {}, []
    for c in carries:
        aliases.update({len(operands) + i: len(out_shapes) + o for i, o in c.aliases.items()})
        spans.append((len(operands), len(out_shapes), len(sems)))
        operands += list(c.operands)
        out_shapes += list(c.out_shapes)
        sems += list(c.sems)

    def each(phase):
        def run(ins, outs, sem_refs):
            for c, (a, b, s) in zip(carries, spans):
                getattr(c, phase)(ins[a:a + len(c.operands)], outs[b:b + len(c.out_shapes)], sem_refs[s:s + len(c.sems)])
        return run

    return _Carried(operands, out_shapes, sems, each("start"), each("finish"), aliases)


def _allreduce_small(arrays, wire):
    n = len(arrays)
    halves = [(a.shape[0], a.shape[1] // 2) for a in arrays]

    def body(*refs):
        srcs, outs = refs[:n], refs[n:2 * n]
        mine_bufs, sib_bufs, chip_bufs, total_bufs = (refs[k * n:(k + 1) * n] for k in range(2, 6))
        send_sems, recv_sems, local_sems = refs[6 * n:]
        x, y, c = _mesh_pos()
        k0 = 2 * x + y
        sib = (x, y, 1 - c)

        def remote(src, dst, j, i, to):
            return pltpu.make_async_remote_copy(src_ref=src, dst_ref=dst, send_sem=send_sems.at[j, i],
                                                recv_sem=recv_sems.at[j, i], device_id=to, device_id_type=MESH)

        def cols(ref, i, core):
            return ref.at[:, pl.ds(pl.multiple_of(core * halves[i][1], LANE), halves[i][1])]

        swaps = [remote(cols(s, i, 1 - c), b, 0, i, sib) for i, (s, b) in enumerate(zip(srcs, sib_bufs))]
        own = [pltpu.make_async_copy(cols(s, i, c), m, local_sems.at[i]) for i, (s, m) in enumerate(zip(srcs, mine_bufs))]
        for cp in swaps + own:
            cp.start()
        for cp in swaps + own:
            cp.wait()
        for m, b, buf in zip(mine_bufs, sib_bufs, chip_bufs):
            buf[k0] = (m[...] + b[...]).astype(buf.dtype)
        chips = _other_chips(x, y)
        sends = [remote(buf.at[k0], buf.at[k0], 1 + j, i, (*chip, c))
                 for j, chip in enumerate(chips) for i, buf in enumerate(chip_bufs)]
        for cp in sends:
            cp.start()
        for j, chip in enumerate(chips):
            for i, buf in enumerate(chip_bufs):
                remote(buf.at[k0], buf.at[2 * chip[0] + chip[1]], 1 + j, i, (*chip, c)).wait_recv()
        for cp in sends:
            cp.wait_send()
        for t, buf in zip(total_bufs, chip_bufs):
            t[...] = ((buf[0].astype(F32) + buf[1].astype(F32)) + buf[2].astype(F32)) + buf[3].astype(F32)
        joins = [remote(t, cols(o, i, c), 4, i, sib) for i, (t, o) in enumerate(zip(total_bufs, outs))]
        keep = [pltpu.make_async_copy(t, cols(o, i, c), local_sems.at[i]) for i, (t, o) in enumerate(zip(total_bufs, outs))]
        for cp in joins + keep:
            cp.start()
        for i, (t, o) in enumerate(zip(total_bufs, outs)):
            remote(t, cols(o, i, 1 - c), 4, i, sib).wait_recv()
        for cp in joins:
            cp.wait_send()
        for cp in keep:
            cp.wait()

    specs = [_full(a.shape) for a in arrays]
    return _pallas_call(
        body, name="allreduce_small", grid=(1,), in_specs=specs, out_specs=specs,
        out_shape=[_sds(a.shape) for a in arrays],
        scratch_shapes=([pltpu.VMEM(h, F32) for h in halves] + [pltpu.VMEM(h, F32) for h in halves]
                        + [pltpu.VMEM((NCHIP,) + h, dt) for h, dt in zip(halves, wire)] + [pltpu.VMEM(h, F32) for h in halves]
                        + [pltpu.SemaphoreType.DMA((5, n)), pltpu.SemaphoreType.DMA((5, n)), pltpu.SemaphoreType.DMA((n,))]),
        compiler_params=_params(32),
    )(*arrays)


def _adamw_terms(w, g, m, v):
    m = ADAM_B1 * m + (1.0 - ADAM_B1) * g
    v = ADAM_B2 * v + (1.0 - ADAM_B2) * jnp.square(g)
    m_hat = m / (1.0 - ADAM_B1 ** ADAM_STEP)
    v_hat = v / (1.0 - ADAM_B2 ** ADAM_STEP)
    return -ADAM_LR * (m_hat / (jnp.sqrt(v_hat) + ADAM_EPS) + ADAM_WD * w), m, v


ADAM_STEPS = 4


def _adamw_group(tag, ws, gs, ms, vs):
    n = len(ws)

    def body(*refs):
        ins, outs = refs[:4 * n], refs[4 * n:]
        for i in range(n):
            w, g, m, v = (ins[k * n + i][...] for k in range(4))
            outs[i][...] = g
            outs[n + i][...], outs[2 * n + i][...], outs[3 * n + i][...] = _adamw_terms(w, g, m, v)

    specs = [pl.BlockSpec((w.shape[0] // ADAM_STEPS, w.shape[1]), lambda i: (i, 0)) for w in ws]
    outs = _pallas_call(
        body, name="adamw_" + tag, grid=(ADAM_STEPS,), in_specs=specs * 4, out_specs=specs * 4,
        out_shape=[_sds(w.shape) for w in ws] * 4, compiler_params=_params(48),
    )(*_in_hbm(list(ws) + list(gs) + list(ms) + list(vs)))
    return outs[:n], outs[n:2 * n], outs[2 * n:3 * n], outs[3 * n:]


def _adamw_replicated(sums, row_of, direct):
    ns, nr, nd = len(sums), len(row_of), len(direct)

    def body(*refs):
        sum_refs = refs[:ns]
        ins = refs[ns:ns + 3 * nr + 4 * nd]
        outs = refs[ns + 3 * nr + 4 * nd:]
        for i, (_, _, _, si, row) in enumerate(row_of):
            w_ref, m_ref, v_ref = ins[3 * i:3 * i + 3]
            g = sum_refs[si][row:row + 1, :]
            outs[4 * i][...] = g
            outs[4 * i + 1][...], outs[4 * i + 2][...], outs[4 * i + 3][...] = _adamw_terms(w_ref[...], g, m_ref[...], v_ref[...])
        for i in range(nd):
            w_ref, m_ref, v_ref, g_ref = ins[3 * nr + 4 * i:3 * nr + 4 * i + 4]
            o = outs[4 * (nr + i):4 * (nr + i) + 4]
            g = g_ref[...]
            o[0][...] = g
            o[1][...], o[2][...], o[3][...] = _adamw_terms(w_ref[...], g, m_ref[...], v_ref[...])

    operands = list(sums)
    shapes = []
    for w, m, v, _, _ in row_of:
        operands += [w, m, v]
        shapes += [w.shape] * 4
    for w, m, v, g in direct:
        operands += [w, m, v, g]
        shapes += [w.shape] * 4
    flat = _pallas_call(
        body, name="adamw_replicated", grid=(1,), in_specs=[_full(a.shape) for a in operands],
        out_specs=[_full(s) for s in shapes], out_shape=[_sds(s) for s in shapes],
        compiler_params=_params(56),
    )(*operands)
    return [flat[4 * i:4 * i + 4] for i in range(nr + nd)]


class _Exchanges:
    def __init__(self, shards, conv_w, chip, core, apply):
        self.shards, self.conv_w, self.apply = shards, conv_w, apply
        self.active, self.calls = [], 0
        self.core_idx = jnp.reshape(core, (1,)).astype(jnp.int32)
        self.chip_core_idx = jnp.stack([chip, core]).astype(jnp.int32)

    def first(self):
        names = ["w_in", "w_glu"]
        got = _run_now("gather_first", _gather_group([self.shards[n] for n in names] + [self.conv_w],
                                                     [True, True, False]))
        out = dict(zip(names, got))
        out["conv_w"] = jnp.transpose(got[2], (1, 0, 2)).reshape(4, LW)
        return out

    def gather(self, names):
        return _gather_group([self.shards[n] for n in names], [True] * len(names))

    def reduce(self, tag, grads):
        self.active.append({"tag": tag, "names": list(grads), "stage": 0, "grads": list(grads.values())})

    def run(self, call):
        groups = self.active
        carries = [self._exchange_of(g) for g in groups]
        carry = _combine(carries)
        outs = list(call(carry))
        own = len(outs) - len(carry.out_shapes)
        landed = outs[own:]
        for g, c in zip(groups, carries):
            self._sum_after(g, landed[:len(c.out_shapes)])
            landed = landed[len(c.out_shapes):]
        self.active = [g for g in groups if g["stage"] < 3]
        return outs[:own]

    def _exchange_of(self, g):
        if g["stage"] == 0:
            return _swap_group(g["grads"])
        if g["stage"] == 1:
            return _exchange_group(g["bf16"])
        return _join_group(g["halves"])

    def _sum_after(self, g, landed):
        if g["stage"] == 0:
            g["f32"], g["bf16"] = _add_sibling_group(g["tag"], self.core_idx, g["grads"], landed)
        elif g["stage"] == 1:
            g["halves"] = _add_chips_group(g["tag"], self.chip_core_idx, g["f32"], landed)
        else:
            self.apply(g["tag"], g["names"], [t.reshape(2 * t.shape[1], t.shape[2]) for t in landed])
        g["stage"] += 1

    def drain(self):
        while self.active:
            self.calls += 1
            self.run(lambda carry: _run_now("reduce_%d" % self.calls, carry))


INPUT_NAMES = (["x", "p"] + [n for n in
               ["g_mix", "w_in", "b_in", "lam_re", "lam_im", "log_dt", "s5_b_re", "s5_b_im", "s5_c_re", "s5_c_im", "s5_d",
                "w_glu", "b_glu", "conv_w", "conv_b", "w_r", "b_r", "w_i", "b_i", "lru_lambda", "w_a_out", "w_b_out", "w_o",
                "g_ffn", "w_ffn_gate", "w_ffn_up", "w_ffn_down", "g_ple_gate", "w_ple_gate", "b_ple_gate", "w_ple", "g_ple",
                "g_final"]])
WEIGHT_NAMES = INPUT_NAMES[2:]


def kernel(*args):
    names = INPUT_NAMES + ["loss_target"] + ["m_" + n for n in WEIGHT_NAMES] + ["v_" + n for n in WEIGHT_NAMES]
    assert len(args) == len(names)
    given = dict(zip(names, args))

    def view(name):
        a = given[name]
        return jnp.swapaxes(a, -1, -2) if name.endswith(TRANSPOSED) else a

    def unview(name, a):
        return jnp.swapaxes(a, -1, -2) if name in TRANSPOSED else a

    def local(name):
        return view(name) if name.endswith("g_final") else view(name)[0]

    xi, yi, ci = _mesh_pos()
    k0 = 2 * xi + yi
    x, p, tgt = given["x"][0], given["p"][0, 0], given["loss_target"][0]

    results = {}

    row_halves = {}

    def apply(tag, names, totals):
        totals = dict(zip(names, totals))
        row_halves.update({n: totals.pop(n) for n in names if n in ("w_in_lo", "w_in_hi")})
        if len(row_halves) == 2:
            totals["w_in"] = jnp.concatenate([row_halves.pop("w_in_lo"), row_halves.pop("w_in_hi")])
        names = list(totals)
        if not names:
            return
        new = _adamw_group(tag, [local(n) for n in names], list(totals.values()), [local("m_" + n) for n in names],
                           [local("v_" + n) for n in names])
        for kind, arrays in zip(("grad", "delta", "new_m", "new_v"), new):
            for n, arr in zip(names, arrays):
                results[kind, n] = unview(n, arr[None])

    comm = _Exchanges({n: local(n).astype(BF) for n, _ in SHARDED}, local("conv_w"), k0, ci, apply)
    w = {n: local(n) for n in WEIGHT_NAMES if n != "conv_w" and n not in dict(SHARDED)}
    gx, sums, blocks = _local_step(x, p, tgt, w, comm)

    sum_names, block_names = list(sums), list(blocks)
    red = _allreduce_small([sums[n] for n in sum_names] + [blocks[n] for n in block_names],
                           [F32] * len(sum_names) + [BF] * len(block_names))
    sums = dict(zip(sum_names, red[:len(sum_names)]))
    blocks = dict(zip(block_names, red[len(sum_names):]))
    loss = jnp.sum(sums[LOSS_ROW[0]][LOSS_ROW[1]])
    direct_g = _replicated_grads(w, sums, blocks)
    conv_rows = sums[CONV_W_ROWS[0]][CONV_W_ROWS[1]:CONV_W_ROWS[1] + 4]
    direct_g["conv_w"] = lax.dynamic_slice(conv_rows, (0, k0 * CONV_SHARD[1]), CONV_SHARD)
    as_row = lambda a: a.reshape(1, -1)
    row_names = list(ACC_ROWS)
    row_of = [(as_row(given[n]), as_row(given["m_" + n]), as_row(given["v_" + n]),
               sum_names.index(ACC_ROWS[n][0]), ACC_ROWS[n][1]) for n in row_names]
    direct_names = list(direct_g)
    direct = [(view(n), view("m_" + n), view("v_" + n), direct_g[n].reshape(view(n).shape)) for n in direct_names]
    done = _adamw_replicated([sums[n] for n in sum_names], row_of, direct)
    for n, four in zip(row_names + direct_names, done):
        for kind, arr in zip(("grad", "delta", "new_m", "new_v"), four):
            results[kind, n] = unview(n, arr).reshape(given[n].shape)

    out = [loss, gx[None]]
    for kind in ("grad", "delta", "new_m", "new_v"):
        out += [results[kind, n] for n in WEIGHT_NAMES]
    return tuple(out)
```

```python
import functools
import math

import jax
import jax.numpy as jnp
from jax import lax
from jax.experimental import pallas as pl
from jax.experimental.pallas import tpu as pltpu

F32 = jnp.float32
BF = jnp.bfloat16

D = 1024
S5W = 512
NG, NS, NP = 32, 64, 16
GN = NG * NS
LW = 1024
NH, HD = 16, 64
LRU_C = 8.0
FH = 2816
NCHIP = 4
FC = FH // NCHIP
PLE = 256
INC = S5W + LW + 2 * D
EPS = 1e-6
ADAM_LR, ADAM_B1, ADAM_B2, ADAM_EPS, ADAM_WD, ADAM_STEP = 0.001, 0.9, 0.999, 1e-08, 0.01, 10

TM = 256
TK = 512
LC = 512
SUB = 8
VMEM_MB = 1024 * 1024
MESH = pl.DeviceIdType.MESH
ANY = pl.BlockSpec(memory_space=pl.ANY)


def _mm(a, b):
    return jnp.dot(a.astype(BF), b.astype(BF), preferred_element_type=F32)


def _mm_nt(a, b):
    return lax.dot_general(a.astype(BF), b.astype(BF), (((1,), (1,)), ((), ())), preferred_element_type=F32)


def _mm_tn(a, b):
    return lax.dot_general(a.astype(BF), b.astype(BF), (((0,), (0,)), ((), ())), preferred_element_type=F32)


def _rms(x):
    r = lax.rsqrt(jnp.mean(x * x, axis=-1, keepdims=True) + EPS)
    return x * r, r


def _rms_bwd(dy, xh, r, g):
    dxh = dy * g
    return r * (dxh - xh * jnp.mean(dxh * xh, axis=-1, keepdims=True))


def _colsum(x):
    return jnp.sum(x, axis=0, keepdims=True)


def _sig(x):
    return jax.nn.sigmoid(x)


def _gelu_grad(x):
    c = math.sqrt(2.0 / math.pi)
    t = jnp.tanh(c * (x + 0.044715 * x * x * x))
    return 0.5 * (1.0 + t) + 0.5 * x * (1.0 - t * t) * c * (1.0 + 3.0 * 0.044715 * x * x)


def _neg_expm1(x):
    series = -x * (1.0 + x * (0.5 + x * (1.0 / 6.0 + x * (1.0 / 24.0))))
    return jnp.where(x > -0.03, series, 1.0 - jnp.exp(x))


def _tok(width):
    return pl.BlockSpec((TM, width), lambda i: (i, 0))


def _tok_rev(width, nt):
    return pl.BlockSpec((TM, width), lambda i: (nt - 1 - i, 0))


def _full(shape):
    return pl.BlockSpec(shape, lambda i: (0,) * len(shape))


def _params(vmem_mb, **kw):
    return pltpu.CompilerParams(dimension_semantics=("arbitrary",), vmem_limit_bytes=vmem_mb * VMEM_MB, **kw)


def _sds(shape, dtype=F32):
    return jax.ShapeDtypeStruct(shape, dtype)


class _Carried:
    def __init__(self, operands, out_shapes, sems, start, finish, aliases=None):
        self.operands, self.out_shapes, self.sems = list(operands), list(out_shapes), list(sems)
        self.start, self.finish, self.aliases = start, finish, dict(aliases or {})


def _in_hbm(arrays):
    return [pltpu.with_memory_space_constraint(a, pltpu.HBM) for a in arrays]


def _pallas_call(body, carry=None, **kw):
    if carry is None:
        return pl.pallas_call(body, **kw)

    def at_step(corner):
        hit = [pl.program_id(d) == (size - 1 if corner else 0) for d, size in enumerate(kw["grid"])]
        return functools.reduce(jnp.logical_and, hit)

    name, grid, compiler_params = kw["name"], kw["grid"], kw["compiler_params"]
    in_specs, out_specs, out_shape = list(kw["in_specs"]), list(kw["out_specs"]), list(kw["out_shape"])
    scratch_shapes = list(kw.get("scratch_shapes", ()))
    n_in, n_out, n_scr = len(in_specs), len(out_specs), len(scratch_shapes)
    c_in, c_out = len(carry.operands), len(carry.out_shapes)

    def full_body(*refs):
        ins, refs = refs[:n_in], refs[n_in:]
        c_ins, refs = refs[:c_in], refs[c_in:]
        outs, refs = refs[:n_out], refs[n_out:]
        c_outs, refs = refs[:c_out], refs[c_out:]
        scratch, c_sems = refs[:n_scr], refs[n_scr:]

        @pl.when(at_step(0))
        def _():
            carry.start(c_ins, c_outs, c_sems)

        body(*ins, *outs, *scratch)

        @pl.when(at_step(1))
        def _():
            carry.finish(c_ins, c_outs, c_sems)

    call = pl.pallas_call(
        full_body, name=name, grid=grid, in_specs=in_specs + [ANY] * c_in, out_specs=out_specs + [ANY] * c_out,
        out_shape=out_shape + list(carry.out_shapes), scratch_shapes=scratch_shapes + list(carry.sems),
        input_output_aliases={n_in + i: n_out + o for i, o in carry.aliases.items()},
        compiler_params=compiler_params)
    return lambda *operands: call(*operands, *_in_hbm(carry.operands))


def _row_iota(width):
    return lax.broadcasted_iota(jnp.int32, (SUB, width), 0)


def _bcast_row(x, row):
    return jnp.broadcast_to(x[row:row + 1, :], x.shape)


def _slab(k):
    return pl.ds(pl.multiple_of(k * SUB, SUB), SUB)


QC = INC // NCHIP
Z_PARTS = ((0, S5W), (S5W, S5W + LW), (S5W + LW, INC))


def _inproj_fwd(x, g_mix, w_in, b_in, carry=None):
    L = x.shape[0]

    def body(x_ref, g_ref, w_hbm, b_ref, h_ref, ua_ref, ub_ref, gp_ref, w_vm):
        @pl.when(pl.program_id(0) == 0)
        def _():
            pltpu.sync_copy(w_hbm, w_vm)

        xh, _ = _rms(x_ref[...])
        h = (xh * g_ref[...]).astype(BF)
        h_ref[...] = h
        for k in range(NCHIP):
            lo, hi = k * QC, (k + 1) * QC
            z = jnp.dot(h, w_vm[k], preferred_element_type=F32) + b_ref[:, lo:hi]
            for ref, (a, b) in zip((ua_ref, ub_ref, gp_ref), Z_PARTS):
                s, e = max(lo, a), min(hi, b)
                if s < e:
                    ref[:, s - a:e - a] = z[:, s - lo:e - lo]

    return _pallas_call(
        body, carry, name="inproj_fwd", grid=(L // TM,),
        in_specs=[_tok(D), _full((1, D)), ANY, _full((1, INC))],
        out_specs=[_tok(D), _tok(S5W), _tok(LW), _tok(2 * D)],
        out_shape=[_sds((L, D), BF), _sds((L, S5W)), _sds((L, LW)), _sds((L, 2 * D))],
        scratch_shapes=[pltpu.VMEM((NCHIP, D, QC), BF)],
        compiler_params=_params(40),
    )(x, g_mix, w_in, b_in)


def _inproj_bwd(x, dx1, dua, dub, dgp, g_mix, w_in, carry=None):
    L = x.shape[0]

    def body(x_ref, dx1_ref, dua_ref, dub_ref, dgp_ref, g_ref, w_hbm, gx_ref, dz_ref, dg_ref, db_ref, w_vm):
        @pl.when(pl.program_id(0) == 0)
        def _():
            pltpu.sync_copy(w_hbm, w_vm)
            dg_ref[...] = jnp.zeros_like(dg_ref)
            db_ref[...] = jnp.zeros_like(db_ref)

        for src, (a, b) in zip((dua_ref, dub_ref, dgp_ref), Z_PARTS):
            d = src[...]
            dz_ref[:, a:b] = d.astype(BF)
            db_ref[0:1, a:b] += _colsum(d)
        dh = jnp.zeros((TM, D), F32)
        for k in range(NCHIP):
            dh = dh + lax.dot_general(dz_ref[:, k * QC:(k + 1) * QC], w_vm[k], (((1,), (1,)), ((), ())),
                                      preferred_element_type=F32)
        xh, r = _rms(x_ref[...])
        dg_ref[0:1, :] += _colsum(dh * xh)
        gx_ref[...] = dx1_ref[...] + _rms_bwd(dh, xh, r, g_ref[...])

    return _pallas_call(
        body, carry, name="inproj_bwd", grid=(L // TM,),
        in_specs=[_tok(D), _tok(D), _tok(S5W), _tok(LW), _tok(2 * D), _full((1, D)), ANY],
        out_specs=[_tok(D), _tok(INC), _full((SUB, D)), _full((SUB, INC))],
        out_shape=[_sds((L, D)), _sds((L, INC), BF), _sds((SUB, D)), _sds((SUB, INC))],
        scratch_shapes=[pltpu.VMEM((NCHIP, D, QC), BF)],
        compiler_params=_params(40),
    )(x, dx1, dua, dub, dgp, g_mix, w_in)


def _cscan(xr_ref, xi_ref, con_ref, cr_ref, ci_ref, reverse):
    n_slab = xr_ref.shape[0] // SUB
    width = xr_ref.shape[1]
    for lc in range(width // LC):
        cols = slice(lc * LC, (lc + 1) * LC)
        con = [con_ref[SUB * j:SUB * (j + 1), cols] for j in range(8)]

        def step(k, carry, cols=cols, con=con):
            cr, ci = carry
            rows = _slab(n_slab - 1 - k if reverse else k)
            xr, xi = xr_ref[rows, cols], xi_ref[rows, cols]
            for j, sh in enumerate((1, 2, 4)):
                mr, mi = con[2 * j], con[2 * j + 1]
                pr = pltpu.roll(xr, SUB - sh if reverse else sh, 0)
                pi = pltpu.roll(xi, SUB - sh if reverse else sh, 0)
                xr, xi = xr + mr * pr - mi * pi, xi + mr * pi + mi * pr
            xr, xi = xr + con[6] * cr - con[7] * ci, xi + con[6] * ci + con[7] * cr
            xr_ref[rows, cols] = xr
            xi_ref[rows, cols] = xi
            row = 0 if reverse else SUB - 1
            return _bcast_row(xr, row), _bcast_row(xi, row)

        cr, ci = lax.fori_loop(0, n_slab, step, (cr_ref[:, cols], ci_ref[:, cols]))
        cr_ref[:, cols] = cr
        ci_ref[:, cols] = ci


def _s5_fwd(ua, bbr, bbi, ccr, cci, dsk, con, w_glu, b_glu, carry=None):
    L = ua.shape[0]

    def body(ua_ref, bbr_hbm, bbi_hbm, ccr_hbm, cci_hbm, dsk_ref, con_ref, wg_ref, bg_ref,
             sr_ref, si_ref, y_ref, zg_ref, ya_ref, bbr_vm, bbi_vm, ccr_vm, cci_vm, cr_ref, ci_ref):
        @pl.when(pl.program_id(0) == 0)
        def _():
            pltpu.sync_copy(bbr_hbm, bbr_vm)
            pltpu.sync_copy(bbi_hbm, bbi_vm)
            pltpu.sync_copy(ccr_hbm, ccr_vm)
            pltpu.sync_copy(cci_hbm, cci_vm)
            cr_ref[...] = jnp.zeros_like(cr_ref)
            ci_ref[...] = jnp.zeros_like(ci_ref)

        u = ua_ref[...]
        ub = u.astype(BF)
        sr_ref[...] = jnp.dot(ub, bbr_vm[...], preferred_element_type=F32)
        si_ref[...] = jnp.dot(ub, bbi_vm[...], preferred_element_type=F32)
        _cscan(sr_ref, si_ref, con_ref, cr_ref, ci_ref, reverse=False)
        y = _mm_nt(sr_ref[...], ccr_vm[...]) - _mm_nt(si_ref[...], cci_vm[...]) + dsk_ref[...] * u
        y_ref[...] = y
        zg = jax.nn.gelu(y)
        zg_ref[...] = zg.astype(BF)
        q = _mm(zg, wg_ref[...]) + bg_ref[...]
        ya_ref[...] = (zg * _sig(q)).astype(BF)

    return _pallas_call(
        body, carry, name="s5_fwd", grid=(L // TM,),
        in_specs=[_tok(S5W), ANY, ANY, ANY, ANY, _full((1, S5W)), _full((8 * SUB, GN)),
                  _full((S5W, S5W)), _full((1, S5W))],
        out_specs=[_tok(GN), _tok(GN), _tok(S5W), _tok(S5W), _tok(S5W)],
        out_shape=[_sds((L, GN)), _sds((L, GN)), _sds((L, S5W)), _sds((L, S5W), BF), _sds((L, S5W), BF)],
        scratch_shapes=[pltpu.VMEM((S5W, GN), BF), pltpu.VMEM((S5W, GN), BF), pltpu.VMEM((S5W, GN), BF),
                        pltpu.VMEM((S5W, GN), BF),pltpu.VMEM((SUB, GN), F32), pltpu.VMEM((SUB, GN), F32)],
        compiler_params=_params(44),
    )(ua, bbr, bbi, ccr, cci, dsk, con, w_glu, b_glu)


def _s5_bwd(dya, y, ua, sr, si, bbr, bbi, ccr, cci, dsk, con_rev, w_glu, b_glu, carry=None):
    L = ua.shape[0]
    nt = L // TM
    spt = TM // SUB
    n_slab = spt

    def halo_map(i):
        return (jnp.maximum((nt - 1 - i) * spt - 1, 0), 0)

    def body(dya_ref, y_ref, ua_ref, sr_ref, si_ref, hr_ref, hi_ref, bbr_hbm, bbi_hbm, ccr_hbm, cci_hbm,
             dsk_ref, con_ref, wg_ref, bg_ref,
             dua_ref, dq_ref, dy_ref, lr_ref, li_ref, da_ref, dsm_ref,
             bbr_vm, bbi_vm, ccr_vm, cci_vm, cr_ref, ci_ref):
        i = pl.program_id(0)

        @pl.when(i == 0)
        def _():
            pltpu.sync_copy(bbr_hbm, bbr_vm)
            pltpu.sync_copy(bbi_hbm, bbi_vm)
            pltpu.sync_copy(ccr_hbm, ccr_vm)
            pltpu.sync_copy(cci_hbm, cci_vm)
            cr_ref[...] = jnp.zeros_like(cr_ref)
            ci_ref[...] = jnp.zeros_like(ci_ref)
            da_ref[...] = jnp.zeros_like(da_ref)
            dsm_ref[...] = jnp.zeros_like(dsm_ref)

        u = ua_ref[...]
        yv = y_ref[...]
        dya = dya_ref[...]
        zg = jax.nn.gelu(yv)
        sg = _sig(_mm(zg, wg_ref[...]) + bg_ref[...])
        dq = dya * zg * sg * (1.0 - sg)
        dq_ref[...] = dq.astype(BF)
        dzg = dya * sg + _mm_nt(dq, wg_ref[...])
        dy = dzg * _gelu_grad(yv)
        dyb = dy.astype(BF)
        dy_ref[...] = dyb
        dsm_ref[0:1, :] += _colsum(dy * u)
        dsm_ref[1:2, :] += _colsum(dq)
        lr_ref[...] = jnp.dot(dyb, ccr_vm[...], preferred_element_type=F32)
        li_ref[...] = -jnp.dot(dyb, cci_vm[...], preferred_element_type=F32)
        _cscan(lr_ref, li_ref, con_ref, cr_ref, ci_ref, reverse=True)

        first_tile = (i == nt - 1)
        row = _row_iota(LC)
        for lc in range(GN // LC):
            cols = slice(lc * LC, (lc + 1) * LC)
            h_r = jnp.where(first_tile, 0.0, hr_ref[:, cols])
            h_i = jnp.where(first_tile, 0.0, hi_ref[:, cols])

            def step(k, acc, cols=cols, h_r=h_r, h_i=h_i):
                ar, ai = acc
                rows = _slab(k)
                prev = _slab(jnp.maximum(k - 1, 0))
                pr = jnp.where(k == 0, h_r, sr_ref[prev, cols])
                pi = jnp.where(k == 0, h_i, si_ref[prev, cols])
                spr = pltpu.roll(jnp.where(row == SUB - 1, pr, sr_ref[rows, cols]), 1, 0)
                spi = pltpu.roll(jnp.where(row == SUB - 1, pi, si_ref[rows, cols]), 1, 0)
                lr, li = lr_ref[rows, cols], li_ref[rows, cols]
                return ar + lr * spr + li * spi, ai + li * spr - lr * spi

            zero = jnp.zeros((SUB, LC), F32)
            ar, ai = lax.fori_loop(0, n_slab, step, (zero, zero))
            da_ref[0:1, cols] += _colsum(ar)
            da_ref[1:2, cols] += _colsum(ai)

        dua_ref[...] = (dy * dsk_ref[...] + _mm_nt(lr_ref[...], bbr_vm[...]) + _mm_nt(li_ref[...], bbi_vm[...]))

    return _pallas_call(
        body, carry, name="s5_bwd", grid=(nt,),
        in_specs=[_tok_rev(S5W, nt), _tok_rev(S5W, nt), _tok_rev(S5W, nt), _tok_rev(GN, nt), _tok_rev(GN, nt),
                  pl.BlockSpec((SUB, GN), halo_map), pl.BlockSpec((SUB, GN), halo_map),
                  ANY, ANY, ANY, ANY, _full((1, S5W)), _full((8 * SUB, GN)), _full((S5W, S5W)), _full((1, S5W))],
        out_specs=[_tok_rev(S5W, nt), _tok_rev(S5W, nt), _tok_rev(S5W, nt), _tok_rev(GN, nt), _tok_rev(GN, nt),
                   _full((SUB, GN)), _full((SUB, S5W))],
        out_shape=[_sds((L, S5W)), _sds((L, S5W), BF), _sds((L, S5W), BF), _sds((L, GN)), _sds((L, GN)),
                   _sds((SUB, GN)), _sds((SUB, S5W))],
        scratch_shapes=[pltpu.VMEM((S5W, GN), BF), pltpu.VMEM((S5W, GN), BF), pltpu.VMEM((S5W, GN), BF),
                        pltpu.VMEM((S5W, GN), BF),pltpu.VMEM((SUB, GN), F32), pltpu.VMEM((SUB, GN), F32)],
        compiler_params=_params(52),
    )(dya, y, ua, sr, si, sr, si, bbr, bbi, ccr, cci, dsk, con_rev, w_glu, b_glu)


def _lru_gate_terms(rg, sp):
    log_a = -LRU_C * rg * sp
    a = jnp.exp(log_a)
    mult = jnp.sqrt(_neg_expm1(2.0 * log_a))
    return a, mult


def _lru_fwd(ub, conv_w, conv_b, wr, wi, b_r, b_i, sp, carry=None):
    L = ub.shape[0]
    n_slab = TM // SUB

    def body(ub_ref, cw_ref, cb_ref, wr_ref, wi_ref, br_ref, bi_ref, sp_ref,
             xc_ref, rg_ref, ig_ref, h_ref, hp_ref, a_ref, halo_ref, carry_ref):
        @pl.when(pl.program_id(0) == 0)
        def _():
            halo_ref[...] = jnp.zeros_like(halo_ref)
            carry_ref[...] = jnp.zeros_like(carry_ref)

        row = _row_iota(LW)
        taps = [cw_ref[k:k + 1, :] for k in range(4)]
        cb = cb_ref[...]

        def conv_step(k, prev):
            rows = _slab(k)
            cur = ub_ref[rows, :]
            acc = taps[3] * cur + cb
            for j in (1, 2, 3):
                acc = acc + taps[3 - j] * pltpu.roll(jnp.where(row >= SUB - j, prev, cur), j, 0)
            xc_ref[rows, :] = acc
            return cur

        halo_ref[...] = lax.fori_loop(0, n_slab, conv_step, halo_ref[...])

        xc = xc_ref[...]
        xcb = xc.astype(BF)
        rg = _sig(jnp.dot(xcb, wr_ref[...], preferred_element_type=F32) + br_ref[...])
        ig = _sig(jnp.dot(xcb, wi_ref[...], preferred_element_type=F32) + bi_ref[...])
        rg_ref[...] = rg
        ig_ref[...] = ig
        a, mult = _lru_gate_terms(rg, sp_ref[...])
        a_ref[...] = a
        h_ref[...] = mult * ig * xc

        rowc = _row_iota(LC)
        for lc in range(LW // LC):
            cols = slice(lc * LC, (lc + 1) * LC)

            def step(k, c, cols=cols):
                rows = _slab(k)
                av, b = a_ref[rows, cols], h_ref[rows, cols]
                for sh in (1, 2, 4):
                    keep = rowc >= sh
                    b = b + av * jnp.where(keep, pltpu.roll(b, sh, 0), 0.0)
                    av = av * jnp.where(keep, pltpu.roll(av, sh, 0), 1.0)
                h = b + av * c
                h_ref[rows, cols] = h
                hp_ref[rows, cols] = jnp.where(rowc == 0, c, pltpu.roll(h, 1, 0))
                return _bcast_row(h, SUB - 1)

            carry_ref[:, cols] = lax.fori_loop(0, n_slab, step, carry_ref[:, cols])

    return _pallas_call(
        body, carry, name="lru_fwd", grid=(L // TM,),
        in_specs=[_tok(LW), _full((4, LW)), _full((1, LW)), _full((LW, LW)), _full((LW, LW)),
                  _full((1, LW)), _full((1, LW)), _full((1, LW))],
        out_specs=[_tok(LW)] * 5,
        out_shape=[_sds((L, LW))] * 5,
        scratch_shapes=[pltpu.VMEM((TM, LW), F32), pltpu.VMEM((SUB, LW), F32), pltpu.VMEM((SUB, LW), F32)],
        compiler_params=_params(40),
    )(ub, conv_w, conv_b, wr, wi, b_r, b_i, sp)


def _lru_bwd(dyb, xc, rg, ig, hp, ub, conv_w, wr, wi, sp, dsp, carry=None):
    L = ub.shape[0]
    nt = L // TM
    spt = TM // SUB
    n_slab = spt

    def halo_map(i):
        return (jnp.maximum((nt - 1 - i) * spt - 1, 0), 0)

    def body(dh_ref, xc_ref, rg_ref, ig_ref, hp_ref, ub_ref, uh_ref, cw_ref, wr_ref, wi_ref, sp_ref, dsp_ref,
             dub_ref, dpr_ref, dpi_ref, acc_ref, a_ref, lam_ref, dxc_ref, carry_ref, next_ref):
        i = pl.program_id(0)

        @pl.when(i == 0)
        def _():
            carry_ref[...] = jnp.zeros_like(carry_ref)
            next_ref[...] = jnp.zeros_like(next_ref)
            acc_ref[...] = jnp.zeros_like(acc_ref)

        sp = sp_ref[...]
        rg, ig, xc = rg_ref[...], ig_ref[...], xc_ref[...]
        a, mult = _lru_gate_terms(rg, sp)
        a_ref[...] = a

        rowc = _row_iota(LC)
        for lc in range(LW // LC):
            cols = slice(lc * LC, (lc + 1) * LC)

            def step(k, c, cols=cols):
                rows = _slab(n_slab - 1 - k)
                av, dh = a_ref[rows, cols], dh_ref[rows, cols]
                b = av * dh
                for sh in (1, 2, 4):
                    keep = rowc < SUB - sh
                    b = b + av * jnp.where(keep, pltpu.roll(b, SUB - sh, 0), 0.0)
                    av = av * jnp.where(keep, pltpu.roll(av, SUB - sh, 0), 1.0)
                mu = b + av * c
                lam_ref[rows, cols] = dh + jnp.where(rowc == SUB - 1, c, pltpu.roll(mu, SUB - 1, 0))
                return _bcast_row(mu, 0)

            carry_ref[:, cols] = lax.fori_loop(0, n_slab, step, carry_ref[:, cols])

        lam = lam_ref[...]
        d_a = lam * hp_ref[...]
        d_mult = lam * ig * xc
        d_ig = lam * mult * xc
        dxc = lam * mult * ig
        d_log_a = d_a * a - d_mult * a * a / mult
        d_rg = (-LRU_C) * sp * d_log_a
        acc_ref[0:1, :] += _colsum((-LRU_C) * rg * d_log_a) * dsp_ref[...]
        dpr = d_rg * rg * (1.0 - rg)
        dpi = d_ig * ig * (1.0 - ig)
        acc_ref[1:2, :] += _colsum(dpr)
        acc_ref[2:3, :] += _colsum(dpi)
        dprb, dpib = dpr.astype(BF), dpi.astype(BF)
        dpr_ref[...] = dprb
        dpi_ref[...] = dpib
        dxc = dxc + _mm_nt(dprb, wr_ref[...]) + _mm_nt(dpib, wi_ref[...])
        dxc_ref[...] = dxc
        acc_ref[3:4, :] += _colsum(dxc)

        row = _row_iota(LW)
        taps = [cw_ref[k:k + 1, :] for k in range(4)]
        u_halo = jnp.where(i == nt - 1, 0.0, uh_ref[...])
        nxt_tile = next_ref[...]

        def conv_step(k, accs):
            rows = _slab(k)
            cur = dxc_ref[rows, :]
            nxt = jnp.where(k == n_slab - 1, nxt_tile, dxc_ref[_slab(jnp.minimum(k + 1, n_slab - 1)), :])
            ucur = ub_ref[rows, :]
            uprev = jnp.where(k == 0, u_halo, ub_ref[_slab(jnp.maximum(k - 1, 0)), :])
            du = taps[3] * cur
            new = [accs[3] + cur * ucur]
            for j in (1, 2, 3):
                du = du + taps[3 - j] * pltpu.roll(jnp.where(row < j, nxt, cur), SUB - j, 0)
                new.append(accs[3 - j] + cur * pltpu.roll(jnp.where(row >= SUB - j, uprev, ucur), j, 0))
            dub_ref[rows, :] = du
            return tuple(new[::-1])

        zero = jnp.zeros((SUB, LW), F32)
        accs = lax.fori_loop(0, n_slab, conv_step, (zero, zero, zero, zero))
        for k in range(4):
            acc_ref[4 + k:5 + k, :] += _colsum(accs[k])
        next_ref[...] = dxc_ref[0:SUB, :]

    return _pallas_call(
        body, carry, name="lru_bwd", grid=(nt,),
        in_specs=[_tok_rev(LW, nt)] * 6 + [pl.BlockSpec((SUB, LW), halo_map), _full((4, LW)),
                                           _full((LW, LW)), _full((LW, LW)), _full((1, LW)), _full((1, LW))],
        out_specs=[_tok_rev(LW, nt), _tok_rev(LW, nt), _tok_rev(LW, nt), _full((SUB, LW))],
        out_shape=[_sds((L, LW)), _sds((L, LW), BF), _sds((L, LW), BF), _sds((SUB, LW))],
        scratch_shapes=[pltpu.VMEM((TM, LW), F32), pltpu.VMEM((TM, LW), F32), pltpu.VMEM((TM, LW), F32),
                        pltpu.VMEM((SUB, LW), F32), pltpu.VMEM((SUB, LW), F32)],
        compiler_params=_params(48),
    )(dyb, xc, rg, ig, hp, ub, ub, conv_w, wr, wi, sp, dsp)


AC = D // NCHIP


def _merge_fwd(x, ya, yb, gp, w_a, w_b, w_o, carry=None):
    L = x.shape[0]

    def body(x_ref, ya_ref, yb_ref, gp_ref, wa_ref, wb_ref, wo_ref, x1_ref, pa_ref, pb_ref, mg_ref):
        ya = ya_ref[...]
        for k in range(NCHIP):
            pa_ref[:, k * AC:(k + 1) * AC] = jnp.dot(ya, wa_ref[k], preferred_element_type=F32)
        pb = _mm(yb_ref[...], wb_ref[...])
        pb_ref[...] = pb
        gp = gp_ref[...]
        merged = (_sig(gp[:, :D]) * pa_ref[...] + _sig(gp[:, D:]) * pb).astype(BF)
        mg_ref[...] = merged
        x1_ref[...] = x_ref[...] + jnp.dot(merged, wo_ref[...], preferred_element_type=F32)

    return _pallas_call(
        body, carry, name="merge_fwd", grid=(L // TM,),
        in_specs=[_tok(D), _tok(S5W), _tok(LW), _tok(2 * D), _full((NCHIP, S5W, AC)), _full((LW, D)), _full((D, D))],
        out_specs=[_tok(D), _tok(D), _tok(D), _tok(D)],
        out_shape=[_sds((L, D)), _sds((L, D)), _sds((L, D)), _sds((L, D), BF)],
        compiler_params=_params(40),
    )(x, ya, yb, gp, w_a, w_b, w_o)


def _merge_bwd(dx1, gp, pa, pb, w_a, w_b, w_o, carry=None):
    L = dx1.shape[0]

    def body(dx1_ref, gp_ref, pa_ref, pb_ref, wa_ref, wb_ref, wo_ref, dya_ref, dyb_ref, dgp_ref, dpa_ref, dpb_ref):
        dm = _mm_nt(dx1_ref[...], wo_ref[...])
        gp = gp_ref[...]
        sa, sb = _sig(gp[:, :D]), _sig(gp[:, D:])
        dpa = (dm * sa).astype(BF)
        dpb = (dm * sb).astype(BF)
        dpa_ref[...] = dpa
        dpb_ref[...] = dpb
        dgp_ref[:, :D] = dm * pa_ref[...] * sa * (1.0 - sa)
        dgp_ref[:, D:] = dm * pb_ref[...] * sb * (1.0 - sb)
        dya = jnp.zeros((TM, S5W), F32)
        for k in range(NCHIP):
            dya = dya + _mm_nt(dpa[:, k * AC:(k + 1) * AC], wa_ref[k])
        dya_ref[...] = dya
        dyb_ref[...] = _mm_nt(dpb, wb_ref[...])

    return _pallas_call(
        body, carry, name="merge_bwd", grid=(L // TM,),
        in_specs=[_tok(D), _tok(2 * D), _tok(D), _tok(D), _full((NCHIP, S5W, AC)), _full((LW, D)), _full((D, D))],
        out_specs=[_tok(S5W), _tok(LW), _tok(2 * D), _tok(D), _tok(D)],
        out_shape=[_sds((L, S5W)), _sds((L, LW)), _sds((L, 2 * D)), _sds((L, D), BF), _sds((L, D), BF)],
        compiler_params=_params(40),
    )(dx1, gp, pa, pb, w_a, w_b, w_o)


def _chunk_tok(width):
    return pl.BlockSpec((NCHIP, TM, width), lambda i: (0, i, 0))


def _ffn_fwd(x1, g_ffn, wg, wu, wd, carry=None):
    L = x1.shape[0]

    def body(x_ref, g_ref, wg_hbm, wu_hbm, wd_hbm, x2_ref, h2_ref, gg_ref, uu_ref, wg_vm, wu_vm, wd_vm):
        @pl.when(pl.program_id(0) == 0)
        def _():
            pltpu.sync_copy(wg_hbm, wg_vm)
            pltpu.sync_copy(wu_hbm, wu_vm)
            pltpu.sync_copy(wd_hbm, wd_vm)

        x = x_ref[...]
        xh, _ = _rms(x)
        h2 = (xh * g_ref[...]).astype(BF)
        h2_ref[...] = h2
        out = x
        for c in range(NCHIP):
            gg = lax.dot_general(h2, wg_vm[c], (((1,), (1,)), ((), ())), preferred_element_type=F32)
            uu = lax.dot_general(h2, wu_vm[c], (((1,), (1,)), ((), ())), preferred_element_type=F32)
            gg_ref[c] = gg.astype(BF)
            uu_ref[c] = uu.astype(BF)
            act = (gg * _sig(gg) * uu).astype(BF)
            out = out + jnp.dot(act, wd_vm[c], preferred_element_type=F32)
        x2_ref[...] = out

    return _pallas_call(
        body, carry, name="ffn_fwd", grid=(L // TM,),
        in_specs=[_tok(D), _full((1, D)), ANY, ANY, ANY],
        out_specs=[_tok(D), _tok(D), _chunk_tok(FC), _chunk_tok(FC)],
        out_shape=[_sds((L, D)), _sds((L, D), BF), _sds((NCHIP, L, FC), BF), _sds((NCHIP, L, FC), BF)],
        scratch_shapes=[pltpu.VMEM((NCHIP, FC, D), BF)] * 3,
        compiler_params=_params(52),
    )(x1, g_ffn, wg, wu, wd)


def _ffn_bwd(x1, dx2, gg, uu, g_ffn, wg, wu, wd, carry=None):
    L = x1.shape[0]

    def body(x_ref, dx2_ref, gg_ref, uu_ref, g_ref, wg_hbm, wu_hbm, wd_hbm,
             dx1_ref, act_ref, dgg_ref, duu_ref, dg_ref, wg_vm, wu_vm, wd_vm):
        @pl.when(pl.program_id(0) == 0)
        def _():
            pltpu.sync_copy(wg_hbm, wg_vm)
            pltpu.sync_copy(wu_hbm, wu_vm)
            pltpu.sync_copy(wd_hbm, wd_vm)
            dg_ref[...] = jnp.zeros_like(dg_ref)

        dx2 = dx2_ref[...]
        dx2b = dx2.astype(BF)
        dh2 = jnp.zeros((TM, D), F32)
        for c in range(NCHIP):
            g = gg_ref[c].astype(F32)
            u = uu_ref[c].astype(F32)
            s = _sig(g)
            silu = g * s
            act_ref[c] = (silu * u).astype(BF)
            dact = lax.dot_general(dx2b, wd_vm[c], (((1,), (1,)), ((), ())), preferred_element_type=F32)
            dg = (dact * u * s * (1.0 + g * (1.0 - s))).astype(BF)
            du = (dact * silu).astype(BF)
            dgg_ref[c] = dg
            duu_ref[c] = du
            dh2 = dh2 + jnp.dot(dg, wg_vm[c], preferred_element_type=F32)
            dh2 = dh2 + jnp.dot(du, wu_vm[c], preferred_element_type=F32)
        xh, r = _rms(x_ref[...])
        dg_ref[0:1, :] += _colsum(dh2 * xh)
        dx1_ref[...] = dx2 + _rms_bwd(dh2, xh, r, g_ref[...])

    return _pallas_call(
        body, carry, name="ffn_bwd", grid=(L // TM,),
        in_specs=[_tok(D), _tok(D), _chunk_tok(FC), _chunk_tok(FC), _full((1, D)), ANY, ANY, ANY],
        out_specs=[_tok(D), _chunk_tok(FC), _chunk_tok(FC), _chunk_tok(FC), _full((SUB, D))],
        out_shape=[_sds((L, D)), _sds((NCHIP, L, FC), BF), _sds((NCHIP, L, FC), BF), _sds((NCHIP, L, FC), BF),
                   _sds((SUB, D))],
        scratch_shapes=[pltpu.VMEM((NCHIP, FC, D), BF)] * 3,
        compiler_params=_params(56),
    )(x1, dx2, gg, uu, g_ffn, wg, wu, wd)


def _ple_loss(x2, p, tgt, g_pg, w_pg, b_pg, w_ple, g_ple, g_final):
    L = x2.shape[0]

    def body(x2_ref, p_ref, t_ref, gpg_ref, wpg_ref, bpg_ref, wple_ref, gple_ref, gf_ref,
             dx2_ref, n2_ref, dpre_ref, de0_ref, acc_ref):
        @pl.when(pl.program_id(0) == 0)
        def _():
            acc_ref[...] = jnp.zeros_like(acc_ref)

        x2 = x2_ref[...]
        x2h, r2 = _rms(x2)
        n2 = (x2h * gpg_ref[...]).astype(BF)
        n2_ref[...] = n2
        gate = _sig(jnp.dot(n2, wpg_ref[...], preferred_element_type=F32) + bpg_ref[...])
        pb = p_ref[...].astype(BF)
        e0 = jnp.concatenate([jnp.dot(pb, wple_ref[k], preferred_element_type=F32) for k in range(NCHIP)], axis=1)
        e0h, re = _rms(e0)
        e = e0h * gple_ref[...]
        x3 = x2 + gate * e
        x3h, r3 = _rms(x3)
        diff = x3h * gf_ref[...] - t_ref[...]
        acc_ref[4:5, :] += _colsum(diff * diff) * (0.5 / D)
        dy = diff * (1.0 / D)
        acc_ref[3:4, :] += _colsum(dy * x3h)
        dx3 = _rms_bwd(dy, x3h, r3, gf_ref[...])
        de = dx3 * gate
        acc_ref[2:3, :] += _colsum(de * e0h)
        de0_ref[...] = _rms_bwd(de, e0h, re, gple_ref[...]).astype(BF)
        dpre = dx3 * e * gate * (1.0 - gate)
        acc_ref[1:2, :] += _colsum(dpre)
        dpreb = dpre.astype(BF)
        dpre_ref[...] = dpreb
        dn2 = lax.dot_general(dpreb, wpg_ref[...], (((1,), (1,)), ((), ())), preferred_element_type=F32)
        acc_ref[0:1, :] += _colsum(dn2 * x2h)
        dx2_ref[...] = dx3 + _rms_bwd(dn2, x2h, r2, gpg_ref[...])

    return _pallas_call(
        body, name="ple_loss", grid=(L // TM,),
        in_specs=[_tok(D), _tok(PLE), _tok(D), _full((1, D)), _full((D, D)), _full((1, D)), _full((NCHIP, PLE, AC)),
                  _full((1, D)), _full((1, D))],
        out_specs=[_tok(D), _tok(D), _tok(D), _tok(D), _full((SUB, D))],
        out_shape=[_sds((L, D)), _sds((L, D), BF), _sds((L, D), BF), _sds((L, D), BF), _sds((SUB, D))],
        compiler_params=_params(40),
    )(x2, p, tgt, g_pg, w_pg, b_pg, w_ple, g_ple, g_final)


def _tn(name, a, b, col_chunk=None, a_block=None, carry=None):
    L = a.shape[-2]
    m, n = a.shape[-1], b.shape[-1]
    a_col = 0
    if a_block is not None:
        a_col, m = a_block
    if a.ndim == 3 or b.ndim == 3:
        nj, bn = (a if a.ndim == 3 else b).shape[0], n
        a_spec = (pl.BlockSpec((None, TK, m), lambda j, t: (j, t, 0)) if a.ndim == 3
                  else pl.BlockSpec((TK, m), lambda j, t: (t, 0)))
        b_spec = (pl.BlockSpec((None, TK, n), lambda j, t: (j, t, 0)) if b.ndim == 3
                  else pl.BlockSpec((TK, n), lambda j, t: (t, 0)))
        out_spec, out_shape = pl.BlockSpec((None, m, n), lambda j, t: (j, 0, 0)), _sds((nj, m, n))
    else:
        bn = col_chunk
        if bn is None:
            bn = next((cand for cand in (1024, 512) if n > cand and n % cand == 0), n)
        nj = n // bn
        a_spec = pl.BlockSpec((TK, m), lambda j, t: (t, a_col))
        b_spec = pl.BlockSpec((TK, bn), lambda j, t: (t, j))
        if col_chunk is None:
            out_spec, out_shape = pl.BlockSpec((m, bn), lambda j, t: (0, j)), _sds((m, n))
        else:
            out_spec, out_shape = pl.BlockSpec((None, m, bn), lambda j, t: (j, 0, 0)), _sds((nj, m, bn))

    def body(a_ref, b_ref, o_ref):
        @pl.when(pl.program_id(1) == 0)
        def _():
            o_ref[...] = jnp.zeros_like(o_ref)

        o_ref[...] += _mm_tn(a_ref[...], b_ref[...])

    outs = _pallas_call(
        body, carry, name=name, grid=(nj, L // TK), in_specs=[a_spec, b_spec], out_specs=[out_spec],
        out_shape=[pltpu.HBM(out_shape.shape, out_shape.dtype)],
        compiler_params=pltpu.CompilerParams(dimension_semantics=("arbitrary", "arbitrary"),
                                             vmem_limit_bytes=40 * VMEM_MB),
    )(a, b)
    return outs[0] if carry is None else outs


LANE = 128


def _tn_blocks(name, a, bs, ga, gb, carry=None):
    L, m, n, nb = a.shape[0], a.shape[1], bs[0].shape[1], len(bs)
    per = LANE // ga
    wb = per * gb
    n_super = m // LANE

    def body(a_ref, *refs):
        b_refs, o_refs, acc_refs = refs[:nb], refs[nb:2 * nb], refs[2 * nb:]
        t = pl.program_id(0)

        @pl.when(t == 0)
        def _():
            for acc in acc_refs:
                acc[...] = jnp.zeros_like(acc)

        lhs = a_ref[...].astype(BF)
        for b_ref, acc in zip(b_refs, acc_refs):
            rhs = b_ref[...].astype(BF)
            for j in range(n_super):
                acc[j] += _mm_tn(lhs[:, j * LANE:(j + 1) * LANE], rhs[:, j * wb:(j + 1) * wb])

        @pl.when(t == L // TK - 1)
        def _():
            own = (lax.broadcasted_iota(jnp.int32, (LANE, wb), 0) // ga) == (lax.broadcasted_iota(jnp.int32, (LANE, wb), 1) // gb)
            for o_ref, acc in zip(o_refs, acc_refs):
                for j in range(n_super):
                    kept = jnp.where(own, acc[j], 0.0)
                    o_ref[:, j * wb:(j + 1) * wb] = jnp.sum(kept.reshape(per, ga, wb), axis=0)

    outs = _pallas_call(
        body, carry, name=name, grid=(L // TK,),
        in_specs=[pl.BlockSpec((TK, m), lambda t: (t, 0))] + [pl.BlockSpec((TK, n), lambda t: (t, 0))] * nb,
        out_specs=[_full((ga, n))] * nb, out_shape=[_sds((ga, n))] * nb,
        scratch_shapes=[pltpu.VMEM((n_super, LANE, wb), F32)] * nb,
        compiler_params=_params(48),
    )(*_in_hbm([a] + list(bs)))
    return list(outs)


def _s5_discretize(lam_re, lam_im, log_dt, b_re, b_im):
    dt = jnp.exp(log_dt)[:, None]
    mag = jnp.exp(lam_re * dt)
    ar = mag * jnp.cos(lam_im * dt)
    ai = mag * jnp.sin(lam_im * dt)
    den = lam_re * lam_re + lam_im * lam_im
    nr = ar - 1.0
    fr = (nr * lam_re + ai * lam_im) / den
    fi = (ai * lam_re - nr * lam_im) / den
    bbr = fr[:, None, :] * b_re - fi[:, None, :] * b_im
    bbi = fr[:, None, :] * b_im + fi[:, None, :] * b_re
    return ar, ai, bbr, bbi


def _prepare(by_rows, block_cols, ar, ai):
    n = len(by_rows)

    def body(*refs):
        srcs, (ar_ref, ai_ref), dense, (con_ref, rev_ref) = refs[:n], refs[n:n + 2], refs[n + 2:2 * n + 2], refs[2 * n + 2:]
        for src, out, c in zip(srcs, dense, block_cols):
            r, width = src.shape
            groups = width // c
            tiled = jnp.broadcast_to(src[...][None], (groups, r, width)).reshape(groups * r, width)
            own = (lax.broadcasted_iota(jnp.int32, tiled.shape, 0) // r) == (lax.broadcasted_iota(jnp.int32, tiled.shape, 1) // c)
            out[...] = jnp.where(own, tiled, 0.0).astype(BF)
        a_r, a_i = ar_ref[...], ai_ref[...]
        pw = [(jnp.ones_like(a_r), jnp.zeros_like(a_i))]
        for _ in range(SUB):
            pr, pi = pw[-1]
            pw.append((pr * a_r - pi * a_i, pr * a_i + pi * a_r))
        row = _row_iota(GN)
        for ref, reverse in ((con_ref, False), (rev_ref, True)):
            sign = -1.0 if reverse else 1.0
            for j, sh in enumerate((1, 2, 4)):
                keep = (row < SUB - sh) if reverse else (row >= sh)
                ref[2 * j * SUB:(2 * j + 1) * SUB, :] = jnp.where(keep, pw[sh][0], 0.0)
                ref[(2 * j + 1) * SUB:(2 * j + 2) * SUB, :] = jnp.where(keep, sign * pw[sh][1], 0.0)
            p_r, p_i = jnp.zeros((SUB, GN), F32), jnp.zeros((SUB, GN), F32)
            for i in range(SUB):
                k = SUB - i if reverse else i + 1
                p_r = jnp.where(row == i, pw[k][0], p_r)
                p_i = jnp.where(row == i, sign * pw[k][1], p_i)
            ref[6 * SUB:7 * SUB, :] = p_r
            ref[7 * SUB:8 * SUB, :] = p_i

    dense_shapes = [(b.shape[1] // c * b.shape[0], b.shape[1]) for b, c in zip(by_rows, block_cols)]
    outs = _pallas_call(
        body, name="prepare", grid=(1,), in_specs=[_full(b.shape) for b in by_rows] + [_full((1, GN))] * 2,
        out_specs=[_full(s) for s in dense_shapes] + [_full((8 * SUB, GN))] * 2,
        out_shape=[_sds(s, BF) for s in dense_shapes] + [_sds((8 * SUB, GN))] * 2,
        compiler_params=_params(48),
    )(*by_rows, ar, ai)
    return outs[:n], outs[n], outs[n + 1]


def _local_step(x, p, tgt, w, comm):
    rows_of = lambda a: a.reshape(NCHIP * a.shape[1], a.shape[2])
    quarters = lambda a: a.reshape(NCHIP, a.shape[0] // NCHIP, a.shape[1])

    def gathering(names, call):
        carry = comm.gather(names)
        outs = list(call(carry))
        own = len(outs) - len(carry.out_shapes)
        w.update(zip(names, outs[own:]))
        return outs[:own]

    w.update(comm.first())
    w_glu = rows_of(w["w_glu"])
    ar, ai, bbr, bbi = _s5_discretize(w["lam_re"], w["lam_im"], w["log_dt"], w["s5_b_re"], w["s5_b_im"])
    by_row = lambda b: jnp.transpose(b, (1, 0, 2)).reshape(b.shape[1], -1)
    (bbr_d, bbi_d, ccr_d, cci_d, wr_d, wi_d), con, con_rev = _prepare(
        [by_row(b) for b in (bbr, bbi, w["s5_c_re"], w["s5_c_im"], w["w_r"], w["w_i"])], [NS] * 4 + [HD] * 2,
        ar.reshape(1, GN), ai.reshape(1, GN))
    dsk = w["s5_d"].reshape(1, S5W)
    lam = w["lru_lambda"].reshape(1, LW)
    sp = jax.nn.softplus(-lam)
    b_r, b_i = w["b_r"].reshape(1, LW), w["b_i"].reshape(1, LW)
    row = lambda name: w[name].reshape(1, -1)

    h, ua, ub, gp = gathering(["w_a_out", "w_b_out"], lambda carry: _inproj_fwd(
        x, row("g_mix"), w["w_in"], row("b_in"), carry))
    sr, si, y, zg, ya = gathering(["w_o", "w_ffn_gate"], lambda carry: _s5_fwd(
        ua, bbr_d, bbi_d, ccr_d, cci_d, dsk, con, w_glu, row("b_glu"), carry))
    xc, rg, ig, yb, hp = gathering(["w_ffn_up"], lambda carry: _lru_fwd(
        ub, w["conv_w"], row("conv_b"), wr_d, wi_d, b_r, b_i, sp, carry))
    w_b_out, w_o = rows_of(w["w_b_out"]), rows_of(w["w_o"])
    x1, pa, pb, merged = gathering(["w_ffn_down"], lambda carry: _merge_fwd(
        x, ya, yb, gp, w["w_a_out"], w_b_out, w_o, carry))
    x2, h2, gg, uu = gathering(["w_ple_gate", "w_ple"], lambda carry: _ffn_fwd(
        x1, row("g_ffn"), w["w_ffn_gate"], w["w_ffn_up"], w["w_ffn_down"], carry))
    w_pg = rows_of(w["w_ple_gate"])
    dx2, n2, dpre, de0, acc_p = _ple_loss(x2, p, tgt, row("g_ple_gate"), w_pg, row("b_ple_gate"),
                                          w["w_ple"], row("g_ple"), row("g_final"))
    comm.reduce("ple", {"w_ple_gate": quarters(_tn("dw_ple_gate", n2, dpre)),
                        "w_ple": _tn("dw_ple", p, de0, col_chunk=AC)})
    dx1, act, dgg, duu, acc_f = comm.run(lambda carry: _ffn_bwd(
        x1, dx2, gg, uu, row("g_ffn"), w["w_ffn_gate"], w["w_ffn_up"], w["w_ffn_down"], carry))
    comm.reduce("ffn_gate", {"w_ffn_gate": _tn("dw_ffn_gate", dgg, h2)})
    comm.reduce("ffn_up", {"w_ffn_up": comm.run(lambda carry: _tn("dw_ffn_up", duu, h2, carry=carry))[0]})
    comm.reduce("ffn_down", {"w_ffn_down": comm.run(lambda carry: _tn("dw_ffn_down", act, dx2, carry=carry))[0]})
    dya, dyb, dgp, dpa, dpb = comm.run(lambda carry: _merge_bwd(
        dx1, gp, pa, pb, w["w_a_out"], w_b_out, w_o, carry))
    comm.reduce("merge", {"w_o": quarters(_tn("dw_o", merged, dx1)), "w_a_out": _tn("dw_a_out", ya, dpa, col_chunk=AC),
                          "w_b_out": quarters(_tn("dw_b_out", yb, dpb))})
    dua, dq, dy, lr, li, acc_a, acc_s = comm.run(lambda carry: _s5_bwd(
        dya, y, ua, sr, si, bbr_d, bbi_d, ccr_d, cci_d, dsk, con_rev, w_glu, row("b_glu"), carry))
    dub, dpr, dpi, acc_l = comm.run(lambda carry: _lru_bwd(
        dyb, xc, rg, ig, hp, ub, w["conv_w"], wr_d, wi_d, sp, -_sig(-lam), carry))
    gx, dz, acc_g, acc_b = _inproj_bwd(x, dx1, dua, dub, dgp, row("g_mix"), w["w_in"])
    half = (D // 2,)
    comm.reduce("in_lo", {"w_in_lo": comm.run(lambda carry: _tn(
        "dw_in_lo", h, dz, col_chunk=QC, a_block=(0,) + half, carry=carry))[0]})
    comm.reduce("in_hi", {"w_in_hi": comm.run(lambda carry: _tn(
        "dw_in_hi", h, dz, col_chunk=QC, a_block=(1,) + half, carry=carry))[0], "w_glu": quarters(_tn("dw_glu", zg, dq))})
    d_wr, d_wi = comm.run(lambda carry: _tn_blocks("dw_r_i", xc, [dpr, dpi], HD, HD, carry))
    d_bbr, d_bbi = comm.run(lambda carry: _tn_blocks("d_bb", ua, [lr, li], NP, NS, carry))
    d_ccr, d_cci = comm.run(lambda carry: _tn_blocks("d_cc", dy, [sr, si], NP, NS, carry))
    comm.drain()
    sums = {"ple": acc_p, "ffn": acc_f, "mix": acc_g, "b_in": acc_b, "lru": acc_l, "s5": acc_s, "s5_a": acc_a}
    blocks = {"bb_re": d_bbr, "bb_im": d_bbi,
              "cc_re": d_ccr, "cc_im": d_cci,
              "w_r": d_wr, "w_i": d_wi}
    return gx, sums, blocks


def _replicated_grads(w, sums, blocks):
    grouped = lambda e, groups: jnp.transpose(e.reshape(e.shape[0], groups, -1), (1, 0, 2))
    d_ar, d_ai = sums["s5_a"][0].reshape(NG, NS), sums["s5_a"][1].reshape(NG, NS)
    d_bbr, d_bbi = grouped(blocks["bb_re"], NG), grouped(blocks["bb_im"], NG)
    _, vjp = jax.vjp(_s5_discretize, w["lam_re"], w["lam_im"], w["log_dt"], w["s5_b_re"], w["s5_b_im"])
    g = dict(zip(("lam_re", "lam_im", "log_dt", "s5_b_re", "s5_b_im"), vjp((d_ar, d_ai, d_bbr, d_bbi))))
    g["s5_c_re"] = grouped(blocks["cc_re"], NG)
    g["s5_c_im"] = -grouped(blocks["cc_im"], NG)
    g["w_r"], g["w_i"] = grouped(blocks["w_r"], NH), grouped(blocks["w_i"], NH)
    g["s5_d"] = sums["s5"][0].reshape(NG, NP)
    g["b_r"] = sums["lru"][1].reshape(NH, HD)
    g["b_i"] = sums["lru"][2].reshape(NH, HD)
    return g


ACC_ROWS = {"g_mix": ("mix", 0), "b_in": ("b_in", 0), "g_ffn": ("ffn", 0), "g_ple_gate": ("ple", 0),
            "b_ple_gate": ("ple", 1), "g_ple": ("ple", 2), "g_final": ("ple", 3), "b_glu": ("s5", 1),
            "lru_lambda": ("lru", 0), "conv_b": ("lru", 3)}
LOSS_ROW = ("ple", 4)
CONV_W_ROWS = ("lru", 4)


SHARDED = [("w_in", (D, QC)), ("w_glu", (S5W // NCHIP, S5W)), ("w_a_out", (S5W, AC)), ("w_b_out", (LW // NCHIP, D)),
           ("w_o", (D // NCHIP, D)), ("w_ffn_gate", (FC, D)), ("w_ffn_up", (FC, D)), ("w_ffn_down", (FC, D)),
           ("w_ple_gate", (D // NCHIP, D)), ("w_ple", (PLE, AC))]
NSH = len(SHARDED)
TRANSPOSED = ("w_ffn_gate", "w_ffn_up", "s5_b_re", "s5_b_im")
CONV_SHARD = (4, LW // NCHIP)


def _mesh_pos():
    return lax.axis_index("x"), lax.axis_index("y"), lax.axis_index("c")


def _other_chips(x, y):
    return [(1 - x, y), (x, 1 - y), (1 - x, 1 - y)]


def _half_rows(c, rows, align):
    return pl.ds(pl.multiple_of(c * (rows // 2), align), rows // 2)


def _run_now(name, carry):
    c_in, c_out = len(carry.operands), len(carry.out_shapes)

    def body(*refs):
        ins, outs, sems = refs[:c_in], refs[c_in:c_in + c_out], refs[c_in + c_out:]
        carry.start(ins, outs, sems)
        carry.finish(ins, outs, sems)

    return pl.pallas_call(body, name=name, in_specs=[ANY] * c_in, out_specs=[ANY] * c_out,
                          out_shape=list(carry.out_shapes), scratch_shapes=list(carry.sems),
                          input_output_aliases=dict(carry.aliases))(*_in_hbm(carry.operands))


def _gather_group(shards, split):
    n = len(shards)

    def copies(srcs, outs, sems):
        send_sems, recv_sems = sems
        x, y, c = _mesh_pos()
        k0 = 2 * x + y
        sib = (x, y, 1 - c)
        chips = _other_chips(x, y)

        def remote(src, dst, j, i, to):
            return pltpu.make_async_remote_copy(src_ref=src, dst_ref=dst, send_sem=send_sems.at[j, i],
                                                recv_sem=recv_sems.at[j, i], device_id=to, device_id_type=MESH)

        def rows(ref, i, core, *lead):
            if not split[i]:
                return ref.at[lead] if lead else ref
            return ref.at[(*lead, _half_rows(core, shards[i].shape[0], 16))]

        own = [remote(s, o.at[k0], 6, i, sib) for i, (s, o) in enumerate(zip(srcs, outs))]
        ici, landed, fwd, fwd_landed = [], [], [], []
        for j, chip in enumerate(chips):
            kj = 2 * chip[0] + chip[1]
            pairs = list(enumerate(zip(srcs, outs)))
            ici.append([remote(rows(s, i, c), rows(o, i, c, k0), j, i, (*chip, c)) for i, (s, o) in pairs])
            landed.append([remote(rows(s, i, c), rows(o, i, c, kj), j, i, (*chip, c)) for i, (s, o) in pairs])
            fwd.append([remote(rows(o, i, c, kj), rows(o, i, c, kj), 3 + j, i, sib) for i, (s, o) in pairs if split[i]])
            fwd_landed.append([remote(rows(o, i, 1 - c, kj), rows(o, i, 1 - c, kj), 3 + j, i, sib)
                               for i, (s, o) in pairs if split[i]])
        return own, ici, landed, fwd, fwd_landed

    def start(srcs, outs, sems):
        own, ici, _, _, _ = copies(srcs, outs, sems)
        for cp in own + [cp for per_chip in ici for cp in per_chip]:
            cp.start()

    def finish(srcs, outs, sems):
        own, ici, landed, fwd, fwd_landed = copies(srcs, outs, sems)
        passed = [i for i in range(n) if split[i]]
        for j in range(3):
            for i, cp in enumerate(landed[j]):
                cp.wait_recv()
                if split[i]:
                    fwd[j][passed.index(i)].start()
        for j in range(3):
            for cp in fwd_landed[j]:
                cp.wait_recv()
        for cp in own:
            cp.wait_recv()
        for cp in own + [cp for per_chip in ici + fwd for cp in per_chip]:
            cp.wait_send()

    return _Carried(shards, [_sds((NCHIP,) + s.shape, s.dtype) for s in shards],
                    [pltpu.SemaphoreType.DMA((7, n)), pltpu.SemaphoreType.DMA((7, n))], start, finish)


def _each_copy(copies, carried, out_shapes, sems, aliases=None):
    def start(ins, outs, sem_refs):
        for cp in copies(ins, outs, sem_refs):
            cp.start()

    def finish(ins, outs, sem_refs):
        for cp in copies(ins, outs, sem_refs):
            cp.wait()

    return _Carried(carried, out_shapes, sems, start, finish, aliases)


def _swap_group(grads):
    n = len(grads)

    def copies(srcs, outs, sems):
        send_sems, recv_sems = sems
        x, y, c = _mesh_pos()
        return [pltpu.make_async_remote_copy(src_ref=s.at[:, _half_rows(1 - c, s.shape[1], 8)], dst_ref=o,
                                             send_sem=send_sems.at[i], recv_sem=recv_sems.at[i], device_id=(x, y, 1 - c),
                                             device_id_type=MESH) for i, (s, o) in enumerate(zip(srcs, outs))]

    return _each_copy(copies, grads, [pltpu.HBM((NCHIP, g.shape[1] // 2, g.shape[2]), F32) for g in grads],
                      [pltpu.SemaphoreType.DMA((n,)), pltpu.SemaphoreType.DMA((n,))])


def _add_sibling_group(tag, kc_idx, grads, gots):
    n = len(grads)

    def body(kc_ref, *refs):
        for g, rx, p, pb in zip(refs[:n], refs[n:2 * n], refs[2 * n:3 * n], refs[3 * n:]):
            s = g[...] + rx[...]
            pb[...] = s.astype(BF)

            @pl.when(pl.program_id(0) == kc_ref[0])
            def _():
                p[...] = s

    halves = [pl.BlockSpec((None,) + rx.shape[1:], lambda k, kc_ref: (k, 0, 0)) for rx in gots]
    mine = [pl.BlockSpec((None,) + rx.shape[1:], lambda k, kc_ref: (k, kc_ref[1], 0)) for rx in gots]
    own = [pl.BlockSpec(rx.shape[1:], lambda k, kc_ref: (0, 0)) for rx in gots]
    outs = _pallas_call(
        body, name="add_sibling_" + tag,
        grid_spec=pltpu.PrefetchScalarGridSpec(num_scalar_prefetch=1, grid=(NCHIP,), in_specs=mine + halves,
                                               out_specs=own + halves),
        out_shape=[pltpu.HBM(rx.shape[1:], F32) for rx in gots] + [pltpu.HBM(rx.shape, BF) for rx in gots],
        compiler_params=_params(48),
    )(kc_idx, *_in_hbm(list(grads) + list(gots)))
    return outs[:n], outs[n:]


def _exchange_group(parts):
    n = len(parts)

    def copies(srcs, outs, sems):
        send_sems, recv_sems = sems
        x, y, c = _mesh_pos()
        return [pltpu.make_async_remote_copy(
            src_ref=s.at[2 * chip[0] + chip[1]], dst_ref=o.at[j], send_sem=send_sems.at[j, i],
            recv_sem=recv_sems.at[j, i], device_id=(*chip, c), device_id_type=MESH)
            for j, chip in enumerate(_other_chips(x, y)) for i, (s, o) in enumerate(zip(srcs, outs))]

    return _each_copy(copies, parts, [pltpu.HBM((3,) + p.shape[1:], BF) for p in parts],
                      [pltpu.SemaphoreType.DMA((3, n)), pltpu.SemaphoreType.DMA((3, n))])


def _add_chips_group(tag, kc_idx, parts, arrived):
    n = len(parts)

    def body(kc_ref, *refs):
        for p, rx, t in zip(refs[:n], refs[n:2 * n], refs[2 * n:]):
            t[...] = ((p[...] + rx[0].astype(F32)) + rx[1].astype(F32)) + rx[2].astype(F32)

    outs = _pallas_call(
        body, name="add_chips_" + tag,
        grid_spec=pltpu.PrefetchScalarGridSpec(
            num_scalar_prefetch=1, grid=(1,),
            in_specs=([pl.BlockSpec(rx.shape[1:], lambda i, kc_ref: (0, 0)) for rx in arrived]
                      + [pl.BlockSpec(rx.shape, lambda i, kc_ref: (0, 0, 0)) for rx in arrived]),
            out_specs=[pl.BlockSpec((None,) + rx.shape[1:], lambda i, kc_ref: (kc_ref[1], 0, 0)) for rx in arrived]),
        out_shape=[pltpu.HBM((2,) + rx.shape[1:], F32) for rx in arrived],
        compiler_params=_params(48),
    )(kc_idx, *_in_hbm(list(parts) + list(arrived)))
    return list(outs)


def _join_group(halves):
    n = len(halves)

    def copies(bufs, sems):
        send_sems, recv_sems = sems
        x, y, c = _mesh_pos()
        sib = (x, y, 1 - c)
        sends = [pltpu.make_async_remote_copy(src_ref=b.at[c], dst_ref=b.at[c], send_sem=send_sems.at[i],
                                              recv_sem=recv_sems.at[i], device_id=sib, device_id_type=MESH)
                 for i, b in enumerate(bufs)]
        landed = [pltpu.make_async_remote_copy(src_ref=b.at[c], dst_ref=b.at[1 - c], send_sem=send_sems.at[i],
                                               recv_sem=recv_sems.at[i], device_id=sib, device_id_type=MESH)
                  for i, b in enumerate(bufs)]
        return sends, landed

    def start(_, bufs, sems):
        for cp in copies(bufs, sems)[0]:
            cp.start()

    def finish(_, bufs, sems):
        sends, landed = copies(bufs, sems)
        for cp in landed:
            cp.wait_recv()
        for cp in sends:
            cp.wait_send()

    return _Carried(halves, [pltpu.HBM(h.shape, F32) for h in halves],
                    [pltpu.SemaphoreType.DMA((n,)), pltpu.SemaphoreType.DMA((n,))], start, finish,
                    {i: i for i in range(n)})


def _combine(carries):
    operands, out_shapes, sems, aliases, spans = [], [], [], {}, []
    for c in carries:
        aliases.update({len(operands) + i: len(out_shapes) + o for i, o in c.aliases.items()})
        spans.append((len(operands), len(out_shapes), len(sems)))
        operands += list(c.operands)
        out_shapes += list(c.out_shapes)
        sems += list(c.sems)

    def each(phase):
        def run(ins, outs, sem_refs):
            for c, (a, b, s) in zip(carries, spans):
                getattr(c, phase)(ins[a:a + len(c.operands)], outs[b:b + len(c.out_shapes)], sem_refs[s:s + len(c.sems)])
        return run

    return _Carried(operands, out_shapes, sems, each("start"), each("finish"), aliases)


def _allreduce_small(arrays, wire):
    n = len(arrays)
    halves = [(a.shape[0], a.shape[1] // 2) for a in arrays]

    def body(*refs):
        srcs, outs = refs[:n], refs[n:2 * n]
        mine_bufs, sib_bufs, chip_bufs, total_bufs = (refs[k * n:(k + 1) * n] for k in range(2, 6))
        send_sems, recv_sems, local_sems = refs[6 * n:]
        x, y, c = _mesh_pos()
        k0 = 2 * x + y
        sib = (x, y, 1 - c)

        def remote(src, dst, j, i, to):
            return pltpu.make_async_remote_copy(src_ref=src, dst_ref=dst, send_sem=send_sems.at[j, i],
                                                recv_sem=recv_sems.at[j, i], device_id=to, device_id_type=MESH)

        def cols(ref, i, core):
            return ref.at[:, pl.ds(pl.multiple_of(core * halves[i][1], LANE), halves[i][1])]

        swaps = [remote(cols(s, i, 1 - c), b, 0, i, sib) for i, (s, b) in enumerate(zip(srcs, sib_bufs))]
        own = [pltpu.make_async_copy(cols(s, i, c), m, local_sems.at[i]) for i, (s, m) in enumerate(zip(srcs, mine_bufs))]
        for cp in swaps + own:
            cp.start()
        for cp in swaps + own:
            cp.wait()
        for m, b, buf in zip(mine_bufs, sib_bufs, chip_bufs):
            buf[k0] = (m[...] + b[...]).astype(buf.dtype)
        chips = _other_chips(x, y)
        sends = [remote(buf.at[k0], buf.at[k0], 1 + j, i, (*chip, c))
                 for j, chip in enumerate(chips) for i, buf in enumerate(chip_bufs)]
        for cp in sends:
            cp.start()
        for j, chip in enumerate(chips):
            for i, buf in enumerate(chip_bufs):
                remote(buf.at[k0], buf.at[2 * chip[0] + chip[1]], 1 + j, i, (*chip, c)).wait_recv()
        for cp in sends:
            cp.wait_send()
        for t, buf in zip(total_bufs, chip_bufs):
            t[...] = ((buf[0].astype(F32) + buf[1].astype(F32)) + buf[2].astype(F32)) + buf[3].astype(F32)
        joins = [remote(t, cols(o, i, c), 4, i, sib) for i, (t, o) in enumerate(zip(total_bufs, outs))]
        keep = [pltpu.make_async_copy(t, cols(o, i, c), local_sems.at[i]) for i, (t, o) in enumerate(zip(total_bufs, outs))]
        for cp in joins + keep:
            cp.start()
        for i, (t, o) in enumerate(zip(total_bufs, outs)):
            remote(t, cols(o, i, 1 - c), 4, i, sib).wait_recv()
        for cp in joins:
            cp.wait_send()
        for cp in keep:
            cp.wait()

    specs = [_full(a.shape) for a in arrays]
    return _pallas_call(
        body, name="allreduce_small", grid=(1,), in_specs=specs, out_specs=specs,
        out_shape=[_sds(a.shape) for a in arrays],
        scratch_shapes=([pltpu.VMEM(h, F32) for h in halves] + [pltpu.VMEM(h, F32) for h in halves]
                        + [pltpu.VMEM((NCHIP,) + h, dt) for h, dt in zip(halves, wire)] + [pltpu.VMEM(h, F32) for h in halves]
                        + [pltpu.SemaphoreType.DMA((5, n)), pltpu.SemaphoreType.DMA((5, n)), pltpu.SemaphoreType.DMA((n,))]),
        compiler_params=_params(32),
    )(*arrays)


def _adamw_terms(w, g, m, v):
    m = ADAM_B1 * m + (1.0 - ADAM_B1) * g
    v = ADAM_B2 * v + (1.0 - ADAM_B2) * jnp.square(g)
    m_hat = m / (1.0 - ADAM_B1 ** ADAM_STEP)
    v_hat = v / (1.0 - ADAM_B2 ** ADAM_STEP)
    return -ADAM_LR * (m_hat / (jnp.sqrt(v_hat) + ADAM_EPS) + ADAM_WD * w), m, v


ADAM_STEPS = 8


def _adamw_group(tag, ws, gs, ms, vs):
    n = len(ws)

    def body(*refs):
        ins, outs = refs[:4 * n], refs[4 * n:]
        for i in range(n):
            w, g, m, v = (ins[k * n + i][...] for k in range(4))
            outs[i][...] = g
            outs[n + i][...], outs[2 * n + i][...], outs[3 * n + i][...] = _adamw_terms(w, g, m, v)

    specs = [pl.BlockSpec((w.shape[0] // ADAM_STEPS, w.shape[1]), lambda i: (i, 0)) for w in ws]
    outs = _pallas_call(
        body, name="adamw_" + tag, grid=(ADAM_STEPS,), in_specs=specs * 4, out_specs=specs * 4,
        out_shape=[_sds(w.shape) for w in ws] * 4, compiler_params=_params(48),
    )(*_in_hbm(list(ws) + list(gs) + list(ms) + list(vs)))
    return outs[:n], outs[n:2 * n], outs[2 * n:3 * n], outs[3 * n:]


def _adamw_replicated(sums, row_of, direct):
    ns, nr, nd = len(sums), len(row_of), len(direct)

    def body(*refs):
        sum_refs = refs[:ns]
        ins = refs[ns:ns + 3 * nr + 4 * nd]
        outs = refs[ns + 3 * nr + 4 * nd:]
        for i, (_, _, _, si, row) in enumerate(row_of):
            w_ref, m_ref, v_ref = ins[3 * i:3 * i + 3]
            g = sum_refs[si][row:row + 1, :]
            outs[4 * i][...] = g
            outs[4 * i + 1][...], outs[4 * i + 2][...], outs[4 * i + 3][...] = _adamw_terms(w_ref[...], g, m_ref[...], v_ref[...])
        for i in range(nd):
            w_ref, m_ref, v_ref, g_ref = ins[3 * nr + 4 * i:3 * nr + 4 * i + 4]
            o = outs[4 * (nr + i):4 * (nr + i) + 4]
            g = g_ref[...]
            o[0][...] = g
            o[1][...], o[2][...], o[3][...] = _adamw_terms(w_ref[...], g, m_ref[...], v_ref[...])

    operands = list(sums)
    shapes = []
    for w, m, v, _, _ in row_of:
        operands += [w, m, v]
        shapes += [w.shape] * 4
    for w, m, v, g in direct:
        operands += [w, m, v, g]
        shapes += [w.shape] * 4
    flat = _pallas_call(
        body, name="adamw_replicated", grid=(1,), in_specs=[_full(a.shape) for a in operands],
        out_specs=[_full(s) for s in shapes], out_shape=[_sds(s) for s in shapes],
        compiler_params=_params(56),
    )(*operands)
    return [flat[4 * i:4 * i + 4] for i in range(nr + nd)]


class _Exchanges:
    def __init__(self, shards, conv_w, chip, core, apply):
        self.shards, self.conv_w, self.apply = shards, conv_w, apply
        self.active, self.calls = [], 0
        self.core_idx = jnp.reshape(core, (1,)).astype(jnp.int32)
        self.chip_core_idx = jnp.stack([chip, core]).astype(jnp.int32)

    def first(self):
        names = ["w_in", "w_glu"]
        got = _run_now("gather_first", _gather_group([self.shards[n] for n in names] + [self.conv_w],
                                                     [True, True, False]))
        out = dict(zip(names, got))
        out["conv_w"] = jnp.transpose(got[2], (1, 0, 2)).reshape(4, LW)
        return out

    def gather(self, names):
        return _gather_group([self.shards[n] for n in names], [True] * len(names))

    def reduce(self, tag, grads):
        self.active.append({"tag": tag, "names": list(grads), "stage": 0, "grads": list(grads.values())})

    def run(self, call):
        groups = self.active
        carries = [self._exchange_of(g) for g in groups]
        carry = _combine(carries)
        outs = list(call(carry))
        own = len(outs) - len(carry.out_shapes)
        landed = outs[own:]
        for g, c in zip(groups, carries):
            self._sum_after(g, landed[:len(c.out_shapes)])
            landed = landed[len(c.out_shapes):]
        self.active = [g for g in groups if g["stage"] < 3]
        return outs[:own]

    def _exchange_of(self, g):
        if g["stage"] == 0:
            return _swap_group(g["grads"])
        if g["stage"] == 1:
            return _exchange_group(g["bf16"])
        return _join_group(g["halves"])

    def _sum_after(self, g, landed):
        if g["stage"] == 0:
            g["f32"], g["bf16"] = _add_sibling_group(g["tag"], self.chip_core_idx, g["grads"], landed)
        elif g["stage"] == 1:
            g["halves"] = _add_chips_group(g["tag"], self.chip_core_idx, g["f32"], landed)
        else:
            self.apply(g["tag"], g["names"], [t.reshape(2 * t.shape[1], t.shape[2]) for t in landed])
        g["stage"] += 1

    def drain(self):
        while self.active:
            self.calls += 1
            self.run(lambda carry: _run_now("reduce_%d" % self.calls, carry))


INPUT_NAMES = (["x", "p"] + [n for n in
               ["g_mix", "w_in", "b_in", "lam_re", "lam_im", "log_dt", "s5_b_re", "s5_b_im", "s5_c_re", "s5_c_im", "s5_d",
                "w_glu", "b_glu", "conv_w", "conv_b", "w_r", "b_r", "w_i", "b_i", "lru_lambda", "w_a_out", "w_b_out", "w_o",
                "g_ffn", "w_ffn_gate", "w_ffn_up", "w_ffn_down", "g_ple_gate", "w_ple_gate", "b_ple_gate", "w_ple", "g_ple",
                "g_final"]])
WEIGHT_NAMES = INPUT_NAMES[2:]


def kernel(*args):
    names = INPUT_NAMES + ["loss_target"] + ["m_" + n for n in WEIGHT_NAMES] + ["v_" + n for n in WEIGHT_NAMES]
    assert len(args) == len(names)
    given = dict(zip(names, args))

    def view(name):
        a = given[name]
        return jnp.swapaxes(a, -1, -2) if name.endswith(TRANSPOSED) else a

    def unview(name, a):
        return jnp.swapaxes(a, -1, -2) if name in TRANSPOSED else a

    def local(name):
        return view(name) if name.endswith("g_final") else view(name)[0]

    xi, yi, ci = _mesh_pos()
    k0 = 2 * xi + yi
    x, p, tgt = given["x"][0], given["p"][0, 0], given["loss_target"][0]

    results = {}

    row_halves = {}

    def apply(tag, names, totals):
        totals = dict(zip(names, totals))
        row_halves.update({n: totals.pop(n) for n in names if n in ("w_in_lo", "w_in_hi")})
        if len(row_halves) == 2:
            totals["w_in"] = jnp.concatenate([row_halves.pop("w_in_lo"), row_halves.pop("w_in_hi")])
        names = list(totals)
        if not names:
            return
        new = _adamw_group(tag, [local(n) for n in names], list(totals.values()), [local("m_" + n) for n in names],
                           [local("v_" + n) for n in names])
        for kind, arrays in zip(("grad", "delta", "new_m", "new_v"), new):
            for n, arr in zip(names, arrays):
                results[kind, n] = unview(n, arr[None])

    comm = _Exchanges({n: local(n).astype(BF) for n, _ in SHARDED}, local("conv_w"), k0, ci, apply)
    w = {n: local(n) for n in WEIGHT_NAMES if n != "conv_w" and n not in dict(SHARDED)}
    gx, sums, blocks = _local_step(x, p, tgt, w, comm)

    sum_names, block_names = list(sums), list(blocks)
    red = _allreduce_small([sums[n] for n in sum_names] + [blocks[n] for n in block_names],
                           [F32] * len(sum_names) + [BF] * len(block_names))
    sums = dict(zip(sum_names, red[:len(sum_names)]))
    blocks = dict(zip(block_names, red[len(sum_names):]))
    loss = jnp.sum(sums[LOSS_ROW[0]][LOSS_ROW[1]])
    direct_g = _replicated_grads(w, sums, blocks)
    conv_rows = sums[CONV_W_ROWS[0]][CONV_W_ROWS[1]:CONV_W_ROWS[1] + 4]
    direct_g["conv_w"] = lax.dynamic_slice(conv_rows, (0, k0 * CONV_SHARD[1]), CONV_SHARD)
    as_row = lambda a: a.reshape(1, -1)
    row_names = list(ACC_ROWS)
    row_of = [(as_row(given[n]), as_row(given["m_" + n]), as_row(given["v_" + n]),
               sum_names.index(ACC_ROWS[n][0]), ACC_ROWS[n][1]) for n in row_names]
    direct_names = list(direct_g)
    direct = [(view(n), view("m_" + n), view("v_" + n), direct_g[n].reshape(view(n).shape)) for n in direct_names]
    done = _adamw_replicated([sums[n] for n in sum_names], row_of, direct)
    for n, four in zip(row_names + direct_names, done):
        for kind, arr in zip(("grad", "delta", "new_m", "new_v"), four):
            results[kind, n] = unview(n, arr).reshape(given[n].shape)

    out = [loss, gx[None]]
    for kind in ("grad", "delta", "new_m", "new_v"):
        out += [results[kind, n] for n in WEIGHT_NAMES]
    return tuple(out)
```

```python
import functools
import math

import jax
import jax.numpy as jnp
from jax import lax
from jax.experimental import pallas as pl
from jax.experimental.pallas import tpu as pltpu

F32 = jnp.float32
BF = jnp.bfloat16

D = 1024
S5W = 512
NG, NS, NP = 32, 64, 16
GN = NG * NS
LW = 1024
NH, HD = 16, 64
LRU_C = 8.0
FH = 2816
NCHIP = 4
FC = FH // NCHIP
PLE = 256
INC = S5W + LW + 2 * D
EPS = 1e-6
ADAM_LR, ADAM_B1, ADAM_B2, ADAM_EPS, ADAM_WD, ADAM_STEP = 0.001, 0.9, 0.999, 1e-08, 0.01, 10

TM = 256
TK = 1024
LC = 512
SUB = 8
VMEM_MB = 1024 * 1024
MESH = pl.DeviceIdType.MESH
ANY = pl.BlockSpec(memory_space=pl.ANY)


def _mm(a, b):
    return jnp.dot(a.astype(BF), b.astype(BF), preferred_element_type=F32)


def _mm_nt(a, b):
    return lax.dot_general(a.astype(BF), b.astype(BF), (((1,), (1,)), ((), ())), preferred_element_type=F32)


def _mm_tn(a, b):
    return lax.dot_general(a.astype(BF), b.astype(BF), (((0,), (0,)), ((), ())), preferred_element_type=F32)


def _rms(x):
    r = lax.rsqrt(jnp.mean(x * x, axis=-1, keepdims=True) + EPS)
    return x * r, r


def _rms_bwd(dy, xh, r, g):
    dxh = dy * g
    return r * (dxh - xh * jnp.mean(dxh * xh, axis=-1, keepdims=True))


def _colsum(x):
    return jnp.sum(x, axis=0, keepdims=True)


def _sig(x):
    return jax.nn.sigmoid(x)


def _gelu_grad(x):
    c = math.sqrt(2.0 / math.pi)
    t = jnp.tanh(c * (x + 0.044715 * x * x * x))
    return 0.5 * (1.0 + t) + 0.5 * x * (1.0 - t * t) * c * (1.0 + 3.0 * 0.044715 * x * x)


def _neg_expm1(x):
    series = -x * (1.0 + x * (0.5 + x * (1.0 / 6.0 + x * (1.0 / 24.0))))
    return jnp.where(x > -0.03, series, 1.0 - jnp.exp(x))


def _tok(width):
    return pl.BlockSpec((TM, width), lambda i: (i, 0))


def _tok_rev(width, nt):
    return pl.BlockSpec((TM, width), lambda i: (nt - 1 - i, 0))


def _full(shape):
    return pl.BlockSpec(shape, lambda i: (0,) * len(shape))


def _params(vmem_mb, **kw):
    return pltpu.CompilerParams(dimension_semantics=("arbitrary",), vmem_limit_bytes=vmem_mb * VMEM_MB, **kw)


def _sds(shape, dtype=F32):
    return jax.ShapeDtypeStruct(shape, dtype)


class _Carried:
    def __init__(self, operands, out_shapes, sems, start, finish, aliases=None):
        self.operands, self.out_shapes, self.sems = list(operands), list(out_shapes), list(sems)
        self.start, self.finish, self.aliases = start, finish, dict(aliases or {})


def _in_hbm(arrays):
    return [pltpu.with_memory_space_constraint(a, pltpu.HBM) for a in arrays]


def _pallas_call(body, carry=None, **kw):
    if carry is None:
        return pl.pallas_call(body, **kw)

    def at_step(corner):
        hit = [pl.program_id(d) == (size - 1 if corner else 0) for d, size in enumerate(kw["grid"])]
        return functools.reduce(jnp.logical_and, hit)

    name, grid, compiler_params = kw["name"], kw["grid"], kw["compiler_params"]
    in_specs, out_specs, out_shape = list(kw["in_specs"]), list(kw["out_specs"]), list(kw["out_shape"])
    scratch_shapes = list(kw.get("scratch_shapes", ()))
    n_in, n_out, n_scr = len(in_specs), len(out_specs), len(scratch_shapes)
    c_in, c_out = len(carry.operands), len(carry.out_shapes)

    def full_body(*refs):
        ins, refs = refs[:n_in], refs[n_in:]
        c_ins, refs = refs[:c_in], refs[c_in:]
        outs, refs = refs[:n_out], refs[n_out:]
        c_outs, refs = refs[:c_out], refs[c_out:]
        scratch, c_sems = refs[:n_scr], refs[n_scr:]

        @pl.when(at_step(0))
        def _():
            carry.start(c_ins, c_outs, c_sems)

        body(*ins, *outs, *scratch)

        @pl.when(at_step(1))
        def _():
            carry.finish(c_ins, c_outs, c_sems)

    call = pl.pallas_call(
        full_body, name=name, grid=grid, in_specs=in_specs + [ANY] * c_in, out_specs=out_specs + [ANY] * c_out,
        out_shape=out_shape + list(carry.out_shapes), scratch_shapes=scratch_shapes + list(carry.sems),
        input_output_aliases={n_in + i: n_out + o for i, o in carry.aliases.items()},
        compiler_params=compiler_params)
    return lambda *operands: call(*operands, *_in_hbm(carry.operands))


def _resident(pairs, sems):
    first = pl.program_id(0) == 0
    copies = [pltpu.make_async_copy(src, dst, sems.at[j]) for j, (src, dst) in enumerate(pairs)]

    @pl.when(first)
    def _():
        for cp in copies:
            cp.start()

    def wait(j):
        @pl.when(first)
        def _():
            copies[j].wait()

    return wait


def _row_iota(width):
    return lax.broadcasted_iota(jnp.int32, (SUB, width), 0)


def _bcast_row(x, row):
    return jnp.broadcast_to(x[row:row + 1, :], x.shape)


def _slab(k):
    return pl.ds(pl.multiple_of(k * SUB, SUB), SUB)


QC = INC // NCHIP
Z_PARTS = ((0, S5W), (S5W, S5W + LW), (S5W + LW, INC))


def _inproj_fwd(x, g_mix, w_in, b_in, carry=None):
    L = x.shape[0]

    def body(x_ref, g_ref, w_hbm, b_ref, h_ref, ua_ref, ub_ref, gp_ref, w_vm, w_sems):
        landed = _resident([(w_hbm.at[k], w_vm.at[k]) for k in range(NCHIP)], w_sems)
        xh, _ = _rms(x_ref[...])
        h = (xh * g_ref[...]).astype(BF)
        h_ref[...] = h
        for k in range(NCHIP):
            lo, hi = k * QC, (k + 1) * QC
            landed(k)
            z = jnp.dot(h, w_vm[k], preferred_element_type=F32) + b_ref[:, lo:hi]
            for ref, (a, b) in zip((ua_ref, ub_ref, gp_ref), Z_PARTS):
                s, e = max(lo, a), min(hi, b)
                if s < e:
                    ref[:, s - a:e - a] = z[:, s - lo:e - lo]

    return _pallas_call(
        body, carry, name="inproj_fwd", grid=(L // TM,),
        in_specs=[_tok(D), _full((1, D)), ANY, _full((1, INC))],
        out_specs=[_tok(D), _tok(S5W), _tok(LW), _tok(2 * D)],
        out_shape=[_sds((L, D), BF), _sds((L, S5W)), _sds((L, LW)), _sds((L, 2 * D))],
        scratch_shapes=[pltpu.VMEM((NCHIP, D, QC), BF), pltpu.SemaphoreType.DMA((NCHIP,))],
        compiler_params=_params(40),
    )(x, g_mix, w_in, b_in)


def _inproj_bwd(x, dx1, dua, dub, dgp, g_mix, w_in, carry=None):
    L = x.shape[0]

    def body(x_ref, dx1_ref, dua_ref, dub_ref, dgp_ref, g_ref, w_hbm, gx_ref, dz_ref, dg_ref, db_ref, w_vm, w_sems):
        landed = _resident([(w_hbm.at[k], w_vm.at[k]) for k in range(NCHIP)], w_sems)

        @pl.when(pl.program_id(0) == 0)
        def _():
            dg_ref[...] = jnp.zeros_like(dg_ref)
            db_ref[...] = jnp.zeros_like(db_ref)

        for src, (a, b) in zip((dua_ref, dub_ref, dgp_ref), Z_PARTS):
            d = src[...]
            dz_ref[:, a:b] = d.astype(BF)
            db_ref[0:1, a:b] += _colsum(d)
        dh = jnp.zeros((TM, D), F32)
        for k in range(NCHIP):
            landed(k)
            dh = dh + lax.dot_general(dz_ref[:, k * QC:(k + 1) * QC], w_vm[k], (((1,), (1,)), ((), ())),
                                      preferred_element_type=F32)
        xh, r = _rms(x_ref[...])
        dg_ref[0:1, :] += _colsum(dh * xh)
        gx_ref[...] = dx1_ref[...] + _rms_bwd(dh, xh, r, g_ref[...])

    return _pallas_call(
        body, carry, name="inproj_bwd", grid=(L // TM,),
        in_specs=[_tok(D), _tok(D), _tok(S5W), _tok(LW), _tok(2 * D), _full((1, D)), ANY],
        out_specs=[_tok(D), _tok(INC), _full((SUB, D)), _full((SUB, INC))],
        out_shape=[_sds((L, D)), _sds((L, INC), BF), _sds((SUB, D)), _sds((SUB, INC))],
        scratch_shapes=[pltpu.VMEM((NCHIP, D, QC), BF), pltpu.SemaphoreType.DMA((NCHIP,))],
        compiler_params=_params(40),
    )(x, dx1, dua, dub, dgp, g_mix, w_in)


def _cscan(xr_ref, xi_ref, con_ref, cr_ref, ci_ref, reverse):
    n_slab = xr_ref.shape[0] // SUB
    width = xr_ref.shape[1]
    for lc in range(width // LC):
        cols = slice(lc * LC, (lc + 1) * LC)
        con = [con_ref[SUB * j:SUB * (j + 1), cols] for j in range(8)]

        def step(k, carry, cols=cols, con=con):
            cr, ci = carry
            rows = _slab(n_slab - 1 - k if reverse else k)
            xr, xi = xr_ref[rows, cols], xi_ref[rows, cols]
            for j, sh in enumerate((1, 2, 4)):
                mr, mi = con[2 * j], con[2 * j + 1]
                pr = pltpu.roll(xr, SUB - sh if reverse else sh, 0)
                pi = pltpu.roll(xi, SUB - sh if reverse else sh, 0)
                xr, xi = xr + mr * pr - mi * pi, xi + mr * pi + mi * pr
            xr, xi = xr + con[6] * cr - con[7] * ci, xi + con[6] * ci + con[7] * cr
            xr_ref[rows, cols] = xr
            xi_ref[rows, cols] = xi
            row = 0 if reverse else SUB - 1
            return _bcast_row(xr, row), _bcast_row(xi, row)

        cr, ci = lax.fori_loop(0, n_slab, step, (cr_ref[:, cols], ci_ref[:, cols]))
        cr_ref[:, cols] = cr
        ci_ref[:, cols] = ci


def _s5_fwd(ua, bbr, bbi, ccr, cci, dsk, con, w_glu, b_glu, carry=None):
    L = ua.shape[0]

    def body(ua_ref, bbr_hbm, bbi_hbm, ccr_hbm, cci_hbm, dsk_ref, con_ref, wg_ref, bg_ref,
             sr_ref, si_ref, y_ref, zg_ref, ya_ref, bbr_vm, bbi_vm, ccr_vm, cci_vm, cr_ref, ci_ref, w_sems):
        landed = _resident([(bbr_hbm, bbr_vm), (bbi_hbm, bbi_vm), (ccr_hbm, ccr_vm), (cci_hbm, cci_vm)], w_sems)

        @pl.when(pl.program_id(0) == 0)
        def _():
            cr_ref[...] = jnp.zeros_like(cr_ref)
            ci_ref[...] = jnp.zeros_like(ci_ref)

        u = ua_ref[...]
        ub = u.astype(BF)
        landed(0)
        sr_ref[...] = jnp.dot(ub, bbr_vm[...], preferred_element_type=F32)
        landed(1)
        si_ref[...] = jnp.dot(ub, bbi_vm[...], preferred_element_type=F32)
        _cscan(sr_ref, si_ref, con_ref, cr_ref, ci_ref, reverse=False)
        landed(2)
        landed(3)
        y =_mm_nt(sr_ref[...], ccr_vm[...]) - _mm_nt(si_ref[...], cci_vm[...]) + dsk_ref[...] * u
        y_ref[...] = y
        zg = jax.nn.gelu(y)
        zg_ref[...] = zg.astype(BF)
        q = _mm(zg, wg_ref[...]) + bg_ref[...]
        ya_ref[...] = (zg * _sig(q)).astype(BF)

    return _pallas_call(
        body, carry, name="s5_fwd", grid=(L // TM,),
        in_specs=[_tok(S5W), ANY, ANY, ANY, ANY, _full((1, S5W)), _full((8 * SUB, GN)),
                  _full((S5W, S5W)), _full((1, S5W))],
        out_specs=[_tok(GN), _tok(GN), _tok(S5W), _tok(S5W), _tok(S5W)],
        out_shape=[_sds((L, GN)), _sds((L, GN)), _sds((L, S5W)), _sds((L, S5W), BF), _sds((L, S5W), BF)],
        scratch_shapes=[pltpu.VMEM((S5W, GN), BF), pltpu.VMEM((S5W, GN), BF), pltpu.VMEM((S5W, GN), BF),
                        pltpu.VMEM((S5W, GN), BF), pltpu.VMEM((SUB, GN), F32), pltpu.VMEM((SUB, GN), F32),
                        pltpu.SemaphoreType.DMA((4,))],
        compiler_params=_params(44),
    )(ua, bbr, bbi, ccr, cci, dsk, con, w_glu, b_glu)


def _s5_bwd(dya, y, ua, sr, si, bbr, bbi, ccr, cci, dsk, con_rev, w_glu, b_glu, carry=None):
    L = ua.shape[0]
    nt = L // TM
    spt = TM // SUB
    n_slab = spt

    def halo_map(i):
        return (jnp.maximum((nt - 1 - i) * spt - 1, 0), 0)

    def body(dya_ref, y_ref, ua_ref, sr_ref, si_ref, hr_ref, hi_ref, bbr_hbm, bbi_hbm, ccr_hbm, cci_hbm,
             dsk_ref, con_ref, wg_ref, bg_ref,
             dua_ref, dq_ref, dy_ref, lr_ref, li_ref, da_ref, dsm_ref,
             bbr_vm, bbi_vm, ccr_vm, cci_vm, cr_ref, ci_ref, w_sems):
        i = pl.program_id(0)
        landed = _resident([(ccr_hbm, ccr_vm), (cci_hbm, cci_vm), (bbr_hbm, bbr_vm), (bbi_hbm, bbi_vm)], w_sems)

        @pl.when(i == 0)
        def _():
            cr_ref[...] = jnp.zeros_like(cr_ref)
            ci_ref[...] = jnp.zeros_like(ci_ref)
            da_ref[...] = jnp.zeros_like(da_ref)
            dsm_ref[...] = jnp.zeros_like(dsm_ref)

        u = ua_ref[...]
        yv = y_ref[...]
        dya = dya_ref[...]
        zg = jax.nn.gelu(yv)
        sg = _sig(_mm(zg, wg_ref[...]) + bg_ref[...])
        dq = dya * zg * sg * (1.0 - sg)
        dq_ref[...] = dq.astype(BF)
        dzg = dya * sg + _mm_nt(dq, wg_ref[...])
        dy = dzg * _gelu_grad(yv)
        dyb = dy.astype(BF)
        dy_ref[...] = dyb
        dsm_ref[0:1, :] += _colsum(dy * u)
        dsm_ref[1:2, :] += _colsum(dq)
        landed(0)
        lr_ref[...] = jnp.dot(dyb, ccr_vm[...], preferred_element_type=F32)
        landed(1)
        li_ref[...] = -jnp.dot(dyb, cci_vm[...], preferred_element_type=F32)
        _cscan(lr_ref, li_ref, con_ref, cr_ref, ci_ref, reverse=True)

        first_tile = (i == nt - 1)
        row = _row_iota(LC)
        for lc in range(GN // LC):
            cols = slice(lc * LC, (lc + 1) * LC)
            h_r = jnp.where(first_tile, 0.0, hr_ref[:, cols])
            h_i = jnp.where(first_tile, 0.0, hi_ref[:, cols])

            def step(k, acc, cols=cols, h_r=h_r, h_i=h_i):
                ar, ai = acc
                rows = _slab(k)
                prev = _slab(jnp.maximum(k - 1, 0))
                pr = jnp.where(k == 0, h_r, sr_ref[prev, cols])
                pi = jnp.where(k == 0, h_i, si_ref[prev, cols])
                spr = pltpu.roll(jnp.where(row == SUB - 1, pr, sr_ref[rows, cols]), 1, 0)
                spi = pltpu.roll(jnp.where(row == SUB - 1, pi, si_ref[rows, cols]), 1, 0)
                lr, li = lr_ref[rows, cols], li_ref[rows, cols]
                return ar + lr * spr + li * spi, ai + li * spr - lr * spi

            zero = jnp.zeros((SUB, LC), F32)
            ar, ai = lax.fori_loop(0, n_slab, step, (zero, zero))
            da_ref[0:1, cols] += _colsum(ar)
            da_ref[1:2, cols] += _colsum(ai)

        landed(2)
        landed(3)
        dua_ref[...] = (dy * dsk_ref[...] + _mm_nt(lr_ref[...], bbr_vm[...]) + _mm_nt(li_ref[...], bbi_vm[...]))

    return _pallas_call(
        body, carry, name="s5_bwd", grid=(nt,),
        in_specs=[_tok_rev(S5W, nt), _tok_rev(S5W, nt), _tok_rev(S5W, nt), _tok_rev(GN, nt), _tok_rev(GN, nt),
                  pl.BlockSpec((SUB, GN), halo_map), pl.BlockSpec((SUB, GN), halo_map),
                  ANY, ANY, ANY, ANY, _full((1, S5W)), _full((8 * SUB, GN)), _full((S5W, S5W)), _full((1, S5W))],
        out_specs=[_tok_rev(S5W, nt), _tok_rev(S5W, nt), _tok_rev(S5W, nt), _tok_rev(GN, nt), _tok_rev(GN, nt),
                   _full((SUB, GN)), _full((SUB, S5W))],
        out_shape=[_sds((L, S5W)), _sds((L, S5W), BF), _sds((L, S5W), BF), _sds((L, GN)), _sds((L, GN)),
                   _sds((SUB, GN)), _sds((SUB, S5W))],
        scratch_shapes=[pltpu.VMEM((S5W, GN), BF), pltpu.VMEM((S5W, GN), BF), pltpu.VMEM((S5W, GN), BF),
                        pltpu.VMEM((S5W, GN), BF), pltpu.VMEM((SUB, GN), F32), pltpu.VMEM((SUB, GN), F32),
                        pltpu.SemaphoreType.DMA((4,))],
        compiler_params=_params(52),
    )(dya, y, ua, sr, si, sr, si, bbr, bbi, ccr, cci, dsk, con_rev, w_glu, b_glu)


def _lru_gate_terms(rg, sp):
    log_a = -LRU_C * rg * sp
    a = jnp.exp(log_a)
    mult = jnp.sqrt(_neg_expm1(2.0 * log_a))
    return a, mult


def _lru_fwd(ub, conv_w, conv_b, wr, wi, b_r, b_i, sp, carry=None):
    L = ub.shape[0]
    n_slab = TM // SUB

    def body(ub_ref, cw_ref, cb_ref, wr_ref, wi_ref, br_ref, bi_ref, sp_ref,
             xc_ref, rg_ref, ig_ref, h_ref, hp_ref, a_ref, halo_ref, carry_ref):
        @pl.when(pl.program_id(0) == 0)
        def _():
            halo_ref[...] = jnp.zeros_like(halo_ref)
            carry_ref[...] = jnp.zeros_like(carry_ref)

        row = _row_iota(LW)
        taps = [cw_ref[k:k + 1, :] for k in range(4)]
        cb = cb_ref[...]

        def conv_step(k, prev):
            rows = _slab(k)
            cur = ub_ref[rows, :]
            acc = taps[3] * cur + cb
            for j in (1, 2, 3):
                acc = acc + taps[3 - j] * pltpu.roll(jnp.where(row >= SUB - j, prev, cur), j, 0)
            xc_ref[rows, :] = acc
            return cur

        halo_ref[...] = lax.fori_loop(0, n_slab, conv_step, halo_ref[...])

        xc = xc_ref[...]
        xcb = xc.astype(BF)
        rg = _sig(jnp.dot(xcb, wr_ref[...], preferred_element_type=F32) + br_ref[...])
        ig = _sig(jnp.dot(xcb, wi_ref[...], preferred_element_type=F32) + bi_ref[...])
        rg_ref[...] = rg
        ig_ref[...] = ig
        a, mult = _lru_gate_terms(rg, sp_ref[...])
        a_ref[...] = a
        h_ref[...] = mult * ig * xc

        rowc = _row_iota(LC)
        for lc in range(LW // LC):
            cols = slice(lc * LC, (lc + 1) * LC)

            def step(k, c, cols=cols):
                rows = _slab(k)
                av, b = a_ref[rows, cols], h_ref[rows, cols]
                for sh in (1, 2, 4):
                    keep = rowc >= sh
                    b = b + av * jnp.where(keep, pltpu.roll(b, sh, 0), 0.0)
                    av = av * jnp.where(keep, pltpu.roll(av, sh, 0), 1.0)
                h = b + av * c
                h_ref[rows, cols] = h
                hp_ref[rows, cols] = jnp.where(rowc == 0, c, pltpu.roll(h, 1, 0))
                return _bcast_row(h, SUB - 1)

            carry_ref[:, cols] = lax.fori_loop(0, n_slab, step, carry_ref[:, cols])

    return _pallas_call(
        body, carry, name="lru_fwd", grid=(L // TM,),
        in_specs=[_tok(LW), _full((4, LW)), _full((1, LW)), _full((LW, LW)), _full((LW, LW)),
                  _full((1, LW)), _full((1, LW)), _full((1, LW))],
        out_specs=[_tok(LW)] * 5,
        out_shape=[_sds((L, LW))] * 5,
        scratch_shapes=[pltpu.VMEM((TM, LW), F32), pltpu.VMEM((SUB, LW), F32), pltpu.VMEM((SUB, LW), F32)],
        compiler_params=_params(40),
    )(ub, conv_w, conv_b, wr, wi, b_r, b_i, sp)


def _lru_bwd(dyb, xc, rg, ig, hp, ub, conv_w, wr, wi, sp, dsp, carry=None):
    L = ub.shape[0]
    nt = L // TM
    spt = TM // SUB
    n_slab = spt

    def halo_map(i):
        return (jnp.maximum((nt - 1 - i) * spt - 1, 0), 0)

    def body(dh_ref, xc_ref, rg_ref, ig_ref, hp_ref, ub_ref, uh_ref, cw_ref, wr_ref, wi_ref, sp_ref, dsp_ref,
             dub_ref, dpr_ref, dpi_ref, acc_ref, a_ref, lam_ref, dxc_ref, carry_ref, next_ref):
        i = pl.program_id(0)

        @pl.when(i == 0)
        def _():
            carry_ref[...] = jnp.zeros_like(carry_ref)
            next_ref[...] = jnp.zeros_like(next_ref)
            acc_ref[...] = jnp.zeros_like(acc_ref)

        sp = sp_ref[...]
        rg, ig, xc = rg_ref[...], ig_ref[...], xc_ref[...]
        a, mult = _lru_gate_terms(rg, sp)
        a_ref[...] = a

        rowc = _row_iota(LC)
        for lc in range(LW // LC):
            cols = slice(lc * LC, (lc + 1) * LC)

            def step(k, c, cols=cols):
                rows = _slab(n_slab - 1 - k)
                av, dh = a_ref[rows, cols], dh_ref[rows, cols]
                b = av * dh
                for sh in (1, 2, 4):
                    keep = rowc < SUB - sh
                    b = b + av * jnp.where(keep, pltpu.roll(b, SUB - sh, 0), 0.0)
                    av = av * jnp.where(keep, pltpu.roll(av, SUB - sh, 0), 1.0)
                mu = b + av * c
                lam_ref[rows, cols] = dh + jnp.where(rowc == SUB - 1, c, pltpu.roll(mu, SUB - 1, 0))
                return _bcast_row(mu, 0)

            carry_ref[:, cols] = lax.fori_loop(0, n_slab, step, carry_ref[:, cols])

        lam = lam_ref[...]
        d_a = lam * hp_ref[...]
        d_mult = lam * ig * xc
        d_ig = lam * mult * xc
        dxc = lam * mult * ig
        d_log_a = d_a * a - d_mult * a * a / mult
        d_rg = (-LRU_C) * sp * d_log_a
        acc_ref[0:1, :] += _colsum((-LRU_C) * rg * d_log_a) * dsp_ref[...]
        dpr = d_rg * rg * (1.0 - rg)
        dpi = d_ig * ig * (1.0 - ig)
        acc_ref[1:2, :] += _colsum(dpr)
        acc_ref[2:3, :] += _colsum(dpi)
        dprb, dpib = dpr.astype(BF), dpi.astype(BF)
        dpr_ref[...] = dprb
        dpi_ref[...] = dpib
        dxc = dxc + _mm_nt(dprb, wr_ref[...]) + _mm_nt(dpib, wi_ref[...])
        dxc_ref[...] = dxc
        acc_ref[3:4, :] += _colsum(dxc)

        row = _row_iota(LW)
        taps = [cw_ref[k:k + 1, :] for k in range(4)]
        u_halo = jnp.where(i == nt - 1, 0.0, uh_ref[...])
        nxt_tile = next_ref[...]

        def conv_step(k, accs):
            rows = _slab(k)
            cur = dxc_ref[rows, :]
            nxt = jnp.where(k == n_slab - 1, nxt_tile, dxc_ref[_slab(jnp.minimum(k + 1, n_slab - 1)), :])
            ucur = ub_ref[rows, :]
            uprev = jnp.where(k == 0, u_halo, ub_ref[_slab(jnp.maximum(k - 1, 0)), :])
            du = taps[3] * cur
            new = [accs[3] + cur * ucur]
            for j in (1, 2, 3):
                du = du + taps[3 - j] * pltpu.roll(jnp.where(row < j, nxt, cur), SUB - j, 0)
                new.append(accs[3 - j] + cur * pltpu.roll(jnp.where(row >= SUB - j, uprev, ucur), j, 0))
            dub_ref[rows, :] = du
            return tuple(new[::-1])

        zero = jnp.zeros((SUB, LW), F32)
        accs = lax.fori_loop(0, n_slab, conv_step, (zero, zero, zero, zero))
        for k in range(4):
            acc_ref[4 + k:5 + k, :] += _colsum(accs[k])
        next_ref[...] = dxc_ref[0:SUB, :]

    return _pallas_call(
        body, carry, name="lru_bwd", grid=(nt,),
        in_specs=[_tok_rev(LW, nt)] * 6 + [pl.BlockSpec((SUB, LW), halo_map), _full((4, LW)),
                                           _full((LW, LW)), _full((LW, LW)), _full((1, LW)), _full((1, LW))],
        out_specs=[_tok_rev(LW, nt), _tok_rev(LW, nt), _tok_rev(LW, nt), _full((SUB, LW))],
        out_shape=[_sds((L, LW)), _sds((L, LW), BF), _sds((L, LW), BF), _sds((SUB, LW))],
        scratch_shapes=[pltpu.VMEM((TM, LW), F32), pltpu.VMEM((TM, LW), F32), pltpu.VMEM((TM, LW), F32),
                        pltpu.VMEM((SUB, LW), F32), pltpu.VMEM((SUB, LW), F32)],
        compiler_params=_params(48),
    )(dyb, xc, rg, ig, hp, ub, ub, conv_w, wr, wi, sp, dsp)


AC = D // NCHIP


def _merge_fwd(x, ya, yb, gp, w_a, w_b, w_o, carry=None):
    L = x.shape[0]

    def body(x_ref, ya_ref, yb_ref, gp_ref, wa_ref, wb_ref, wo_ref, x1_ref, pa_ref, pb_ref, mg_ref):
        ya = ya_ref[...]
        for k in range(NCHIP):
            pa_ref[:, k * AC:(k + 1) * AC] = jnp.dot(ya, wa_ref[k], preferred_element_type=F32)
        pb = _mm(yb_ref[...], wb_ref[...])
        pb_ref[...] = pb
        gp = gp_ref[...]
        merged = (_sig(gp[:, :D]) * pa_ref[...] + _sig(gp[:, D:]) * pb).astype(BF)
        mg_ref[...] = merged
        x1_ref[...] = x_ref[...] + jnp.dot(merged, wo_ref[...], preferred_element_type=F32)

    return _pallas_call(
        body, carry, name="merge_fwd", grid=(L // TM,),
        in_specs=[_tok(D), _tok(S5W), _tok(LW), _tok(2 * D), _full((NCHIP, S5W, AC)), _full((LW, D)), _full((D, D))],
        out_specs=[_tok(D), _tok(D), _tok(D), _tok(D)],
        out_shape=[_sds((L, D)), _sds((L, D)), _sds((L, D)), _sds((L, D), BF)],
        compiler_params=_params(40),
    )(x, ya, yb, gp, w_a, w_b, w_o)


def _merge_bwd(dx1, gp, pa, pb, w_a, w_b, w_o, carry=None):
    L = dx1.shape[0]

    def body(dx1_ref, gp_ref, pa_ref, pb_ref, wa_ref, wb_ref, wo_ref, dya_ref, dyb_ref, dgp_ref, dpa_ref, dpb_ref):
        dm = _mm_nt(dx1_ref[...], wo_ref[...])
        gp = gp_ref[...]
        sa, sb = _sig(gp[:, :D]), _sig(gp[:, D:])
        dpa = (dm * sa).astype(BF)
        dpb = (dm * sb).astype(BF)
        dpa_ref[...] = dpa
        dpb_ref[...] = dpb
        dgp_ref[:, :D] = dm * pa_ref[...] * sa * (1.0 - sa)
        dgp_ref[:, D:] = dm * pb_ref[...] * sb * (1.0 - sb)
        dya = jnp.zeros((TM, S5W), F32)
        for k in range(NCHIP):
            dya = dya + _mm_nt(dpa[:, k * AC:(k + 1) * AC], wa_ref[k])
        dya_ref[...] = dya
        dyb_ref[...] = _mm_nt(dpb, wb_ref[...])

    return _pallas_call(
        body, carry, name="merge_bwd", grid=(L // TM,),
        in_specs=[_tok(D), _tok(2 * D), _tok(D), _tok(D), _full((NCHIP, S5W, AC)), _full((LW, D)), _full((D, D))],
        out_specs=[_tok(S5W), _tok(LW), _tok(2 * D), _tok(D), _tok(D)],
        out_shape=[_sds((L, S5W)), _sds((L, LW)), _sds((L, 2 * D)), _sds((L, D), BF), _sds((L, D), BF)],
        compiler_params=_params(40),
    )(dx1, gp, pa, pb, w_a, w_b, w_o)


def _chunk_tok(width):
    return pl.BlockSpec((NCHIP, TM, width), lambda i: (0, i, 0))


def _ffn_fwd(x1, g_ffn, wg, wu, wd, carry=None):
    L = x1.shape[0]

    def body(x_ref, g_ref, wg_hbm, wu_hbm, wd_hbm, x2_ref, h2_ref, gg_ref, uu_ref, wg_vm, wu_vm, wd_vm, w_sems):
        landed = _resident([(src.at[c], dst.at[c]) for c in range(NCHIP)
                            for src, dst in ((wg_hbm, wg_vm), (wu_hbm, wu_vm), (wd_hbm, wd_vm))], w_sems)
        x = x_ref[...]
        xh, _ = _rms(x)
        h2 = (xh * g_ref[...]).astype(BF)
        h2_ref[...] = h2
        out = x
        for c in range(NCHIP):
            for j in range(3):
                landed(3 * c + j)
            gg = lax.dot_general(h2, wg_vm[c], (((1,), (1,)), ((), ())), preferred_element_type=F32)
            uu = lax.dot_general(h2, wu_vm[c], (((1,), (1,)), ((), ())), preferred_element_type=F32)
            gg_ref[c] = gg.astype(BF)
            uu_ref[c] = uu.astype(BF)
            act = (gg * _sig(gg) * uu).astype(BF)
            out = out + jnp.dot(act, wd_vm[c], preferred_element_type=F32)
        x2_ref[...] = out

    return _pallas_call(
        body, carry, name="ffn_fwd", grid=(L // TM,),
        in_specs=[_tok(D), _full((1, D)), ANY, ANY, ANY],
        out_specs=[_tok(D), _tok(D), _chunk_tok(FC), _chunk_tok(FC)],
        out_shape=[_sds((L, D)), _sds((L, D), BF), _sds((NCHIP, L, FC), BF), _sds((NCHIP, L, FC), BF)],
        scratch_shapes=[pltpu.VMEM((NCHIP, FC, D), BF)] * 3 + [pltpu.SemaphoreType.DMA((3 * NCHIP,))],
        compiler_params=_params(52),
    )(x1, g_ffn, wg, wu, wd)


def _ffn_bwd(x1, dx2, gg, uu, g_ffn, wg, wu, wd, carry=None):
    L = x1.shape[0]

    def body(x_ref, dx2_ref, gg_ref, uu_ref, g_ref, wg_hbm, wu_hbm, wd_hbm,
             dx1_ref, act_ref, dgg_ref, duu_ref, dg_ref, wg_vm, wu_vm, wd_vm, w_sems):
        landed = _resident([(src.at[c], dst.at[c]) for c in range(NCHIP)
                            for src, dst in ((wg_hbm, wg_vm), (wu_hbm, wu_vm), (wd_hbm, wd_vm))], w_sems)

        @pl.when(pl.program_id(0) == 0)
        def _():
            dg_ref[...] = jnp.zeros_like(dg_ref)

        dx2 = dx2_ref[...]
        dx2b = dx2.astype(BF)
        dh2 = jnp.zeros((TM, D), F32)
        for c in range(NCHIP):
            for j in range(3):
                landed(3 * c + j)
            g = gg_ref[c].astype(F32)
            u = uu_ref[c].astype(F32)
            s = _sig(g)
            silu = g * s
            act_ref[c] = (silu * u).astype(BF)
            dact = lax.dot_general(dx2b, wd_vm[c], (((1,), (1,)), ((), ())), preferred_element_type=F32)
            dg = (dact * u * s * (1.0 + g * (1.0 - s))).astype(BF)
            du = (dact * silu).astype(BF)
            dgg_ref[c] = dg
            duu_ref[c] = du
            dh2 = dh2 + jnp.dot(dg, wg_vm[c], preferred_element_type=F32)
            dh2 = dh2 + jnp.dot(du, wu_vm[c], preferred_element_type=F32)
        xh, r = _rms(x_ref[...])
        dg_ref[0:1, :] += _colsum(dh2 * xh)
        dx1_ref[...] = dx2 + _rms_bwd(dh2, xh, r, g_ref[...])

    return _pallas_call(
        body, carry, name="ffn_bwd", grid=(L // TM,),
        in_specs=[_tok(D), _tok(D), _chunk_tok(FC), _chunk_tok(FC), _full((1, D)), ANY, ANY, ANY],
        out_specs=[_tok(D), _chunk_tok(FC), _chunk_tok(FC), _chunk_tok(FC), _full((SUB, D))],
        out_shape=[_sds((L, D)), _sds((NCHIP, L, FC), BF), _sds((NCHIP, L, FC), BF), _sds((NCHIP, L, FC), BF),
                   _sds((SUB, D))],
        scratch_shapes=[pltpu.VMEM((NCHIP, FC, D), BF)] * 3 + [pltpu.SemaphoreType.DMA((3 * NCHIP,))],
        compiler_params=_params(56),
    )(x1, dx2, gg, uu, g_ffn, wg, wu, wd)


def _ple_loss(x2, p, tgt, g_pg, w_pg, b_pg, w_ple, g_ple, g_final):
    L = x2.shape[0]

    def body(x2_ref, p_ref, t_ref, gpg_ref, wpg_ref, bpg_ref, wple_ref, gple_ref, gf_ref,
             dx2_ref, n2_ref, dpre_ref, de0_ref, acc_ref):
        @pl.when(pl.program_id(0) == 0)
        def _():
            acc_ref[...] = jnp.zeros_like(acc_ref)

        x2 = x2_ref[...]
        x2h, r2 = _rms(x2)
        n2 = (x2h * gpg_ref[...]).astype(BF)
        n2_ref[...] = n2
        gate = _sig(jnp.dot(n2, wpg_ref[...], preferred_element_type=F32) + bpg_ref[...])
        pb = p_ref[...].astype(BF)
        e0 = jnp.concatenate([jnp.dot(pb, wple_ref[k], preferred_element_type=F32) for k in range(NCHIP)], axis=1)
        e0h, re = _rms(e0)
        e = e0h * gple_ref[...]
        x3 = x2 + gate * e
        x3h, r3 = _rms(x3)
        diff = x3h * gf_ref[...] - t_ref[...]
        acc_ref[4:5, :] += _colsum(diff * diff) * (0.5 / D)
        dy = diff * (1.0 / D)
        acc_ref[3:4, :] += _colsum(dy * x3h)
        dx3 = _rms_bwd(dy, x3h, r3, gf_ref[...])
        de = dx3 * gate
        acc_ref[2:3, :] += _colsum(de * e0h)
        de0_ref[...] = _rms_bwd(de, e0h, re, gple_ref[...]).astype(BF)
        dpre = dx3 * e * gate * (1.0 - gate)
        acc_ref[1:2, :] += _colsum(dpre)
        dpreb = dpre.astype(BF)
        dpre_ref[...] = dpreb
        dn2 = lax.dot_general(dpreb, wpg_ref[...], (((1,), (1,)), ((), ())), preferred_element_type=F32)
        acc_ref[0:1, :] += _colsum(dn2 * x2h)
        dx2_ref[...] = dx3 + _rms_bwd(dn2, x2h, r2, gpg_ref[...])

    return _pallas_call(
        body, name="ple_loss", grid=(L // TM,),
        in_specs=[_tok(D), _tok(PLE), _tok(D), _full((1, D)), _full((D, D)), _full((1, D)), _full((NCHIP, PLE, AC)),
                  _full((1, D)), _full((1, D))],
        out_specs=[_tok(D), _tok(D), _tok(D), _tok(D), _full((SUB, D))],
        out_shape=[_sds((L, D)), _sds((L, D), BF), _sds((L, D), BF), _sds((L, D), BF), _sds((SUB, D))],
        compiler_params=_params(40),
    )(x2, p, tgt, g_pg, w_pg, b_pg, w_ple, g_ple, g_final)


def _tn(name, a, b, col_chunk=None, a_block=None, carry=None):
    L = a.shape[-2]
    m, n = a.shape[-1], b.shape[-1]
    a_col = 0
    if a_block is not None:
        a_col, m = a_block
    if a.ndim == 3 or b.ndim == 3:
        nj, bn = (a if a.ndim == 3 else b).shape[0], n
        a_spec = (pl.BlockSpec((None, TK, m), lambda j, t: (j, t, 0)) if a.ndim == 3
                  else pl.BlockSpec((TK, m), lambda j, t: (t, 0)))
        b_spec = (pl.BlockSpec((None, TK, n), lambda j, t: (j, t, 0)) if b.ndim == 3
                  else pl.BlockSpec((TK, n), lambda j, t: (t, 0)))
        out_spec, out_shape = pl.BlockSpec((None, m, n), lambda j, t: (j, 0, 0)), _sds((nj, m, n))
    else:
        bn = col_chunk
        if bn is None:
            bn = next((cand for cand in (1024, 512) if n > cand and n % cand == 0), n)
        nj = n // bn
        a_spec = pl.BlockSpec((TK, m), lambda j, t: (t, a_col))
        b_spec = pl.BlockSpec((TK, bn), lambda j, t: (t, j))
        if col_chunk is None:
            out_spec, out_shape = pl.BlockSpec((m, bn), lambda j, t: (0, j)), _sds((m, n))
        else:
            out_spec, out_shape = pl.BlockSpec((None, m, bn), lambda j, t: (j, 0, 0)), _sds((nj, m, bn))

    def body(a_ref, b_ref, o_ref):
        @pl.when(pl.program_id(1) == 0)
        def _():
            o_ref[...] = jnp.zeros_like(o_ref)

        o_ref[...] += _mm_tn(a_ref[...], b_ref[...])

    outs = _pallas_call(
        body, carry, name=name, grid=(nj, L // TK), in_specs=[a_spec, b_spec], out_specs=[out_spec],
        out_shape=[pltpu.HBM(out_shape.shape, out_shape.dtype)],
        compiler_params=pltpu.CompilerParams(dimension_semantics=("arbitrary", "arbitrary"),
                                             vmem_limit_bytes=40 * VMEM_MB),
    )(a, b)
    return outs[0] if carry is None else outs


LANE = 128


def _tn_blocks(name, a, bs, ga, gb, carry=None):
    L, m, n, nb = a.shape[0], a.shape[1], bs[0].shape[1], len(bs)
    per = LANE // ga
    wb = per * gb
    n_super = m // LANE

    def body(a_ref, *refs):
        b_refs, o_refs, acc_refs = refs[:nb], refs[nb:2 * nb], refs[2 * nb:]
        t = pl.program_id(0)

        @pl.when(t == 0)
        def _():
            for acc in acc_refs:
                acc[...] = jnp.zeros_like(acc)

        lhs = a_ref[...].astype(BF)
        for b_ref, acc in zip(b_refs, acc_refs):
            rhs = b_ref[...].astype(BF)
            for j in range(n_super):
                acc[j] += _mm_tn(lhs[:, j * LANE:(j + 1) * LANE], rhs[:, j * wb:(j + 1) * wb])

        @pl.when(t == L // TK - 1)
        def _():
            own = (lax.broadcasted_iota(jnp.int32, (LANE, wb), 0) // ga) == (lax.broadcasted_iota(jnp.int32, (LANE, wb), 1) // gb)
            for o_ref, acc in zip(o_refs, acc_refs):
                for j in range(n_super):
                    kept = jnp.where(own, acc[j], 0.0)
                    o_ref[:, j * wb:(j + 1) * wb] = jnp.sum(kept.reshape(per, ga, wb), axis=0)

    outs = _pallas_call(
        body, carry, name=name, grid=(L // TK,),
        in_specs=[pl.BlockSpec((TK, m), lambda t: (t, 0))] + [pl.BlockSpec((TK, n), lambda t: (t, 0))] * nb,
        out_specs=[_full((ga, n))] * nb, out_shape=[_sds((ga, n))] * nb,
        scratch_shapes=[pltpu.VMEM((n_super, LANE, wb), F32)] * nb,
        compiler_params=_params(48),
    )(*_in_hbm([a] + list(bs)))
    return list(outs)


def _s5_discretize(lam_re, lam_im, log_dt, b_re, b_im):
    dt = jnp.exp(log_dt)[:, None]
    mag = jnp.exp(lam_re * dt)
    ar = mag * jnp.cos(lam_im * dt)
    ai = mag * jnp.sin(lam_im * dt)
    den = lam_re * lam_re + lam_im * lam_im
    nr = ar - 1.0
    fr = (nr * lam_re + ai * lam_im) / den
    fi = (ai * lam_re - nr * lam_im) / den
    bbr = fr[:, None, :] * b_re - fi[:, None, :] * b_im
    bbi = fr[:, None, :] * b_im + fi[:, None, :] * b_re
    return ar, ai, bbr, bbi


def _prepare(by_rows, block_cols, ar, ai):
    n = len(by_rows)

    def body(*refs):
        srcs, (ar_ref, ai_ref), dense, (con_ref, rev_ref) = refs[:n], refs[n:n + 2], refs[n + 2:2 * n + 2], refs[2 * n + 2:]
        for src, out, c in zip(srcs, dense, block_cols):
            r, width = src.shape
            groups = width // c
            tiled = jnp.broadcast_to(src[...][None], (groups, r, width)).reshape(groups * r, width)
            own = (lax.broadcasted_iota(jnp.int32, tiled.shape, 0) // r) == (lax.broadcasted_iota(jnp.int32, tiled.shape, 1) // c)
            out[...] = jnp.where(own, tiled, 0.0).astype(BF)
        a_r, a_i = ar_ref[...], ai_ref[...]
        pw = [(jnp.ones_like(a_r), jnp.zeros_like(a_i))]
        for _ in range(SUB):
            pr, pi = pw[-1]
            pw.append((pr * a_r - pi * a_i, pr * a_i + pi * a_r))
        row = _row_iota(GN)
        for ref, reverse in ((con_ref, False), (rev_ref, True)):
            sign = -1.0 if reverse else 1.0
            for j, sh in enumerate((1, 2, 4)):
                keep = (row < SUB - sh) if reverse else (row >= sh)
                ref[2 * j * SUB:(2 * j + 1) * SUB, :] = jnp.where(keep, pw[sh][0], 0.0)
                ref[(2 * j + 1) * SUB:(2 * j + 2) * SUB, :] = jnp.where(keep, sign * pw[sh][1], 0.0)
            p_r, p_i = jnp.zeros((SUB, GN), F32), jnp.zeros((SUB, GN), F32)
            for i in range(SUB):
                k = SUB - i if reverse else i + 1
                p_r = jnp.where(row == i, pw[k][0], p_r)
                p_i = jnp.where(row == i, sign * pw[k][1], p_i)
            ref[6 * SUB:7 * SUB, :] = p_r
            ref[7 * SUB:8 * SUB, :] = p_i

    dense_shapes = [(b.shape[1] // c * b.shape[0], b.shape[1]) for b, c in zip(by_rows, block_cols)]
    outs = _pallas_call(
        body, name="prepare", grid=(1,), in_specs=[_full(b.shape) for b in by_rows] + [_full((1, GN))] * 2,
        out_specs=[_full(s) for s in dense_shapes] + [_full((8 * SUB, GN))] * 2,
        out_shape=[_sds(s, BF) for s in dense_shapes] + [_sds((8 * SUB, GN))] * 2,
        compiler_params=_params(48),
    )(*by_rows, ar, ai)
    return outs[:n], outs[n], outs[n + 1]


def _local_step(x, p, tgt, w, comm):
    rows_of = lambda a: a.reshape(NCHIP * a.shape[1], a.shape[2])
    quarters = lambda a: a.reshape(NCHIP, a.shape[0] // NCHIP, a.shape[1])

    def gathering(names, call):
        carry = comm.gather(names)
        outs = list(call(carry))
        own = len(outs) - len(carry.out_shapes)
        w.update(zip(names, outs[own:]))
        return outs[:own]

    w.update(comm.first())
    w_glu = rows_of(w["w_glu"])
    ar, ai, bbr, bbi = _s5_discretize(w["lam_re"], w["lam_im"], w["log_dt"], w["s5_b_re"], w["s5_b_im"])
    by_row = lambda b: jnp.transpose(b, (1, 0, 2)).reshape(b.shape[1], -1)
    (bbr_d, bbi_d, ccr_d, cci_d, wr_d, wi_d), con, con_rev = _prepare(
        [by_row(b) for b in (bbr, bbi, w["s5_c_re"], w["s5_c_im"], w["w_r"], w["w_i"])], [NS] * 4 + [HD] * 2,
        ar.reshape(1, GN), ai.reshape(1, GN))
    dsk = w["s5_d"].reshape(1, S5W)
    lam = w["lru_lambda"].reshape(1, LW)
    sp = jax.nn.softplus(-lam)
    b_r, b_i = w["b_r"].reshape(1, LW), w["b_i"].reshape(1, LW)
    row = lambda name: w[name].reshape(1, -1)

    h, ua, ub, gp = gathering(["w_a_out", "w_b_out"], lambda carry: _inproj_fwd(
        x, row("g_mix"), w["w_in"], row("b_in"), carry))
    sr, si, y, zg, ya = gathering(["w_o", "w_ffn_gate"], lambda carry: _s5_fwd(
        ua, bbr_d, bbi_d, ccr_d, cci_d, dsk, con, w_glu, row("b_glu"), carry))
    xc, rg, ig, yb, hp = gathering(["w_ffn_up"], lambda carry: _lru_fwd(
        ub, w["conv_w"], row("conv_b"), wr_d, wi_d, b_r, b_i, sp, carry))
    w_b_out, w_o = rows_of(w["w_b_out"]), rows_of(w["w_o"])
    x1, pa, pb, merged = gathering(["w_ffn_down"], lambda carry: _merge_fwd(
        x, ya, yb, gp, w["w_a_out"], w_b_out, w_o, carry))
    x2, h2, gg, uu = gathering(["w_ple_gate", "w_ple"], lambda carry: _ffn_fwd(
        x1, row("g_ffn"), w["w_ffn_gate"], w["w_ffn_up"], w["w_ffn_down"], carry))
    w_pg = rows_of(w["w_ple_gate"])
    dx2, n2, dpre, de0, acc_p = _ple_loss(x2, p, tgt, row("g_ple_gate"), w_pg, row("b_ple_gate"),
                                          w["w_ple"], row("g_ple"), row("g_final"))
    comm.reduce("ple", {"w_ple_gate": quarters(_tn("dw_ple_gate", n2, dpre)),
                        "w_ple": _tn("dw_ple", p, de0, col_chunk=AC)})
    dx1, act, dgg, duu, acc_f = comm.run(lambda carry: _ffn_bwd(
        x1, dx2, gg, uu, row("g_ffn"), w["w_ffn_gate"], w["w_ffn_up"], w["w_ffn_down"], carry))
    comm.reduce("ffn_gate", {"w_ffn_gate": _tn("dw_ffn_gate", dgg, h2)})
    comm.reduce("ffn_up", {"w_ffn_up": comm.run(lambda carry: _tn("dw_ffn_up", duu, h2, carry=carry))[0]})
    comm.reduce("ffn_down", {"w_ffn_down": comm.run(lambda carry: _tn("dw_ffn_down", act, dx2, carry=carry))[0]})
    dya, dyb, dgp, dpa, dpb = comm.run(lambda carry: _merge_bwd(
        dx1, gp, pa, pb, w["w_a_out"], w_b_out, w_o, carry))
    comm.reduce("merge", {"w_o": quarters(_tn("dw_o", merged, dx1)), "w_a_out": _tn("dw_a_out", ya, dpa, col_chunk=AC),
                          "w_b_out": quarters(_tn("dw_b_out", yb, dpb))})
    dua, dq, dy, lr, li, acc_a, acc_s = comm.run(lambda carry: _s5_bwd(
        dya, y, ua, sr, si, bbr_d, bbi_d, ccr_d, cci_d, dsk, con_rev, w_glu, row("b_glu"), carry))
    dub, dpr, dpi, acc_l = comm.run(lambda carry: _lru_bwd(
        dyb, xc, rg, ig, hp, ub, w["conv_w"], wr_d, wi_d, sp, -_sig(-lam), carry))
    gx, dz, acc_g, acc_b = _inproj_bwd(x, dx1, dua, dub, dgp, row("g_mix"), w["w_in"])
    half = (D // 2,)
    comm.reduce("in_lo", {"w_in_lo": comm.run(lambda carry: _tn(
        "dw_in_lo", h, dz, col_chunk=QC, a_block=(0,) + half, carry=carry))[0]})
    comm.reduce("in_hi", {"w_in_hi": comm.run(lambda carry: _tn(
        "dw_in_hi", h, dz, col_chunk=QC, a_block=(1,) + half, carry=carry))[0], "w_glu": quarters(_tn("dw_glu", zg, dq))})
    d_wr, d_wi = comm.run(lambda carry: _tn_blocks("dw_r_i", xc, [dpr, dpi], HD, HD, carry))
    d_bbr, d_bbi = comm.run(lambda carry: _tn_blocks("d_bb", ua, [lr, li], NP, NS, carry))
    d_ccr, d_cci = comm.run(lambda carry: _tn_blocks("d_cc", dy, [sr, si], NP, NS, carry))
    comm.drain()
    sums = {"ple": acc_p, "ffn": acc_f, "mix": acc_g, "b_in": acc_b, "lru": acc_l, "s5": acc_s, "s5_a": acc_a}
    blocks = {"bb_re": d_bbr, "bb_im": d_bbi,
              "cc_re": d_ccr, "cc_im": d_cci,
              "w_r": d_wr, "w_i": d_wi}
    return gx, sums, blocks


def _replicated_grads(w, sums, blocks):
    grouped = lambda e, groups: jnp.transpose(e.reshape(e.shape[0], groups, -1), (1, 0, 2))
    d_ar, d_ai = sums["s5_a"][0].reshape(NG, NS), sums["s5_a"][1].reshape(NG, NS)
    d_bbr, d_bbi = grouped(blocks["bb_re"], NG), grouped(blocks["bb_im"], NG)
    _, vjp = jax.vjp(_s5_discretize, w["lam_re"], w["lam_im"], w["log_dt"], w["s5_b_re"], w["s5_b_im"])
    g = dict(zip(("lam_re", "lam_im", "log_dt", "s5_b_re", "s5_b_im"), vjp((d_ar, d_ai, d_bbr, d_bbi))))
    g["s5_c_re"] = grouped(blocks["cc_re"], NG)
    g["s5_c_im"] = -grouped(blocks["cc_im"], NG)
    g["w_r"], g["w_i"] = grouped(blocks["w_r"], NH), grouped(blocks["w_i"], NH)
    g["s5_d"] = sums["s5"][0].reshape(NG, NP)
    g["b_r"] = sums["lru"][1].reshape(NH, HD)
    g["b_i"] = sums["lru"][2].reshape(NH, HD)
    return g


ACC_ROWS = {"g_mix": ("mix", 0), "b_in": ("b_in", 0), "g_ffn": ("ffn", 0), "g_ple_gate": ("ple", 0),
            "b_ple_gate": ("ple", 1), "g_ple": ("ple", 2), "g_final": ("ple", 3), "b_glu": ("s5", 1),
            "lru_lambda": ("lru", 0), "conv_b": ("lru", 3)}
LOSS_ROW = ("ple", 4)
CONV_W_ROWS = ("lru", 4)


SHARDED = [("w_in", (D, QC)), ("w_glu", (S5W // NCHIP, S5W)), ("w_a_out", (S5W, AC)), ("w_b_out", (LW // NCHIP, D)),
           ("w_o", (D // NCHIP, D)), ("w_ffn_gate", (FC, D)), ("w_ffn_up", (FC, D)), ("w_ffn_down", (FC, D)),
           ("w_ple_gate", (D // NCHIP, D)), ("w_ple", (PLE, AC))]
NSH = len(SHARDED)
TRANSPOSED = ("w_ffn_gate", "w_ffn_up", "s5_b_re", "s5_b_im")
CONV_SHARD = (4, LW // NCHIP)


def _mesh_pos():
    return lax.axis_index("x"), lax.axis_index("y"), lax.axis_index("c")


def _other_chips(x, y):
    return [(1 - x, y), (x, 1 - y), (1 - x, 1 - y)]


def _half_rows(c, rows, align):
    return pl.ds(pl.multiple_of(c * (rows // 2), align), rows // 2)


def _run_now(name, carry):
    c_in, c_out = len(carry.operands), len(carry.out_shapes)

    def body(*refs):
        ins, outs, sems = refs[:c_in], refs[c_in:c_in + c_out], refs[c_in + c_out:]
        carry.start(ins, outs, sems)
        carry.finish(ins, outs, sems)

    return pl.pallas_call(body, name=name, in_specs=[ANY] * c_in, out_specs=[ANY] * c_out,
                          out_shape=list(carry.out_shapes), scratch_shapes=list(carry.sems),
                          input_output_aliases=dict(carry.aliases))(*_in_hbm(carry.operands))


def _gather_group(shards, split):
    n = len(shards)

    def copies(srcs, outs, sems):
        send_sems, recv_sems = sems
        x, y, c = _mesh_pos()
        k0 = 2 * x + y
        sib = (x, y, 1 - c)
        chips = _other_chips(x, y)

        def remote(src, dst, j, i, to):
            return pltpu.make_async_remote_copy(src_ref=src, dst_ref=dst, send_sem=send_sems.at[j, i],
                                                recv_sem=recv_sems.at[j, i], device_id=to, device_id_type=MESH)

        def rows(ref, i, core, *lead):
            if not split[i]:
                return ref.at[lead] if lead else ref
            return ref.at[(*lead, _half_rows(core, shards[i].shape[0], 16))]

        own = [remote(s, o.at[k0], 6, i, sib) for i, (s, o) in enumerate(zip(srcs, outs))]
        ici, landed, fwd, fwd_landed = [], [], [], []
        for j, chip in enumerate(chips):
            kj = 2 * chip[0] + chip[1]
            pairs = list(enumerate(zip(srcs, outs)))
            ici.append([remote(rows(s, i, c), rows(o, i, c, k0), j, i, (*chip, c)) for i, (s, o) in pairs])
            landed.append([remote(rows(s, i, c), rows(o, i, c, kj), j, i, (*chip, c)) for i, (s, o) in pairs])
            fwd.append([remote(rows(o, i, c, kj), rows(o, i, c, kj), 3 + j, i, sib) for i, (s, o) in pairs if split[i]])
            fwd_landed.append([remote(rows(o, i, 1 - c, kj), rows(o, i, 1 - c, kj), 3 + j, i, sib)
                               for i, (s, o) in pairs if split[i]])
        return own, ici, landed, fwd, fwd_landed

    def start(srcs, outs, sems):
        own, ici, _, _, _ = copies(srcs, outs, sems)
        for cp in own + [cp for per_chip in ici for cp in per_chip]:
            cp.start()

    def finish(srcs, outs, sems):
        own, ici, landed, fwd, fwd_landed = copies(srcs, outs, sems)
        passed = [i for i in range(n) if split[i]]
        for j in range(3):
            for i, cp in enumerate(landed[j]):
                cp.wait_recv()
                if split[i]:
                    fwd[j][passed.index(i)].start()
        for j in range(3):
            for cp in fwd_landed[j]:
                cp.wait_recv()
        for cp in own:
            cp.wait_recv()
        for cp in own + [cp for per_chip in ici + fwd for cp in per_chip]:
            cp.wait_send()

    return _Carried(shards, [_sds((NCHIP,) + s.shape, s.dtype) for s in shards],
                    [pltpu.SemaphoreType.DMA((7, n)), pltpu.SemaphoreType.DMA((7, n))], start, finish)


def _each_copy(copies, carried, out_shapes, sems, aliases=None):
    def start(ins, outs, sem_refs):
        for cp in copies(ins, outs, sem_refs):
            cp.start()

    def finish(ins, outs, sem_refs):
        for cp in copies(ins, outs, sem_refs):
            cp.wait()

    return _Carried(carried, out_shapes, sems, start, finish, aliases)


def _swap_group(grads):
    n = len(grads)

    def copies(srcs, outs, sems):
        send_sems, recv_sems = sems
        x, y, c = _mesh_pos()
        return [pltpu.make_async_remote_copy(src_ref=s.at[:, _half_rows(1 - c, s.shape[1], 8)], dst_ref=o,
                                             send_sem=send_sems.at[i], recv_sem=recv_sems.at[i], device_id=(x, y, 1 - c),
                                             device_id_type=MESH) for i, (s, o) in enumerate(zip(srcs, outs))]

    return _each_copy(copies, grads, [pltpu.HBM((NCHIP, g.shape[1] // 2, g.shape[2]), F32) for g in grads],
                      [pltpu.SemaphoreType.DMA((n,)), pltpu.SemaphoreType.DMA((n,))])


def _add_sibling_group(tag, kc_idx, grads, gots):
    n = len(grads)

    def body(kc_ref, *refs):
        for g, rx, p, pb in zip(refs[:n], refs[n:2 * n], refs[2 * n:3 * n], refs[3 * n:]):
            s = g[...] + rx[...]
            pb[...] = s.astype(BF)

            @pl.when(pl.program_id(0) == kc_ref[0])
            def _():
                p[...] = s

    halves = [pl.BlockSpec((None,) + rx.shape[1:], lambda k, kc_ref: (k, 0, 0)) for rx in gots]
    mine = [pl.BlockSpec((None,) + rx.shape[1:], lambda k, kc_ref: (k, kc_ref[1], 0)) for rx in gots]
    own = [pl.BlockSpec(rx.shape[1:], lambda k, kc_ref: (0, 0)) for rx in gots]
    outs = _pallas_call(
        body, name="add_sibling_" + tag,
        grid_spec=pltpu.PrefetchScalarGridSpec(num_scalar_prefetch=1, grid=(NCHIP,), in_specs=mine + halves,
                                               out_specs=own + halves),
        out_shape=[pltpu.HBM(rx.shape[1:], F32) for rx in gots] + [pltpu.HBM(rx.shape, BF) for rx in gots],
        compiler_params=_params(48),
    )(kc_idx, *_in_hbm(list(grads) + list(gots)))
    return outs[:n], outs[n:]


def _exchange_group(parts):
    n = len(parts)

    def copies(srcs, outs, sems):
        send_sems, recv_sems = sems
        x, y, c = _mesh_pos()
        return [pltpu.make_async_remote_copy(
            src_ref=s.at[2 * chip[0] + chip[1]], dst_ref=o.at[j], send_sem=send_sems.at[j, i],
            recv_sem=recv_sems.at[j, i], device_id=(*chip, c), device_id_type=MESH)
            for j, chip in enumerate(_other_chips(x, y)) for i, (s, o) in enumerate(zip(srcs, outs))]

    return _each_copy(copies, parts, [pltpu.HBM((3,) + p.shape[1:], BF) for p in parts],
                      [pltpu.SemaphoreType.DMA((3, n)), pltpu.SemaphoreType.DMA((3, n))])


def _add_chips_group(tag, kc_idx, parts, arrived):
    n = len(parts)

    def body(kc_ref, *refs):
        for p, rx, t in zip(refs[:n], refs[n:2 * n], refs[2 * n:]):
            t[...] = ((p[...] + rx[0].astype(F32)) + rx[1].astype(F32)) + rx[2].astype(F32)

    outs = _pallas_call(
        body, name="add_chips_" + tag,
        grid_spec=pltpu.PrefetchScalarGridSpec(
            num_scalar_prefetch=1, grid=(1,),
            in_specs=([pl.BlockSpec(rx.shape[1:], lambda i, kc_ref: (0, 0)) for rx in arrived]
                      + [pl.BlockSpec(rx.shape, lambda i, kc_ref: (0, 0, 0)) for rx in arrived]),
            out_specs=[pl.BlockSpec((None,) + rx.shape[1:], lambda i, kc_ref: (kc_ref[1], 0, 0)) for rx in arrived]),
        out_shape=[pltpu.HBM((2,) + rx.shape[1:], F32) for rx in arrived],
        compiler_params=_params(48),
    )(kc_idx, *_in_hbm(list(parts) + list(arrived)))
    return list(outs)


def _join_group(halves):
    n = len(halves)

    def copies(bufs, sems):
        send_sems, recv_sems = sems
        x, y, c = _mesh_pos()
        sib = (x, y, 1 - c)
        sends = [pltpu.make_async_remote_copy(src_ref=b.at[c], dst_ref=b.at[c], send_sem=send_sems.at[i],
                                              recv_sem=recv_sems.at[i], device_id=sib, device_id_type=MESH)
                 for i, b in enumerate(bufs)]
        landed = [pltpu.make_async_remote_copy(src_ref=b.at[c], dst_ref=b.at[1 - c], send_sem=send_sems.at[i],
                                               recv_sem=recv_sems.at[i], device_id=sib, device_id_type=MESH)
                  for i, b in enumerate(bufs)]
        return sends, landed

    def start(_, bufs, sems):
        for cp in copies(bufs, sems)[0]:
            cp.start()

    def finish(_, bufs, sems):
        sends, landed = copies(bufs, sems)
        for cp in landed:
            cp.wait_recv()
        for cp in sends:
            cp.wait_send()

    return _Carried(halves, [pltpu.HBM(h.shape, F32) for h in halves],
                    [pltpu.SemaphoreType.DMA((n,)), pltpu.SemaphoreType.DMA((n,))], start, finish,
                    {i: i for i in range(n)})


def _combine(carries):
    operands, out_shapes, sems, aliases, spans = [], [], [], {}, []
    for c in carries:
        aliases.update({len(operands) + i: len(out_shapes) + o for i, o in c.aliases.items()})
        spans.append((len(operands), len(out_shapes), len(sems)))
        operands += list(c.operands)
        out_shapes += list(c.out_shapes)
        sems += list(c.sems)

    def each(phase):
        def run(ins, outs, sem_refs):
            for c, (a, b, s) in zip(carries, spans):
                getattr(c, phase)(ins[a:a + len(c.operands)], outs[b:b + len(c.out_shapes)], sem_refs[s:s + len(c.sems)])
        return run

    return _Carried(operands, out_shapes, sems, each("start"), each("finish"), aliases)


def _allreduce_small(arrays, wire):
    n = len(arrays)
    halves = [(a.shape[0], a.shape[1] // 2) for a in arrays]

    def body(*refs):
        srcs, outs = refs[:n], refs[n:2 * n]
        mine_bufs, sib_bufs, chip_bufs, total_bufs = (refs[k * n:(k + 1) * n] for k in range(2, 6))
        send_sems, recv_sems, local_sems = refs[6 * n:]
        x, y, c = _mesh_pos()
        k0 = 2 * x + y
        sib = (x, y, 1 - c)

        def remote(src, dst, j, i, to):
            return pltpu.make_async_remote_copy(src_ref=src, dst_ref=dst, send_sem=send_sems.at[j, i],
                                                recv_sem=recv_sems.at[j, i], device_id=to, device_id_type=MESH)

        def cols(ref, i, core):
            return ref.at[:, pl.ds(pl.multiple_of(core * halves[i][1], LANE), halves[i][1])]

        swaps = [remote(cols(s, i, 1 - c), b, 0, i, sib) for i, (s, b) in enumerate(zip(srcs, sib_bufs))]
        own = [pltpu.make_async_copy(cols(s, i, c), m, local_sems.at[i]) for i, (s, m) in enumerate(zip(srcs, mine_bufs))]
        for cp in swaps + own:
            cp.start()
        for cp in swaps + own:
            cp.wait()
        for m, b, buf in zip(mine_bufs, sib_bufs, chip_bufs):
            buf[k0] = (m[...] + b[...]).astype(buf.dtype)
        chips = _other_chips(x, y)
        sends = [remote(buf.at[k0], buf.at[k0], 1 + j, i, (*chip, c))
                 for j, chip in enumerate(chips) for i, buf in enumerate(chip_bufs)]
        for cp in sends:
            cp.start()
        for j, chip in enumerate(chips):
            for i, buf in enumerate(chip_bufs):
                remote(buf.at[k0], buf.at[2 * chip[0] + chip[1]], 1 + j, i, (*chip, c)).wait_recv()
        for cp in sends:
            cp.wait_send()
        for t, buf in zip(total_bufs, chip_bufs):
            t[...] = ((buf[0].astype(F32) + buf[1].astype(F32)) + buf[2].astype(F32)) + buf[3].astype(F32)
        joins = [remote(t, cols(o, i, c), 4, i, sib) for i, (t, o) in enumerate(zip(total_bufs, outs))]
        keep = [pltpu.make_async_copy(t, cols(o, i, c), local_sems.at[i]) for i, (t, o) in enumerate(zip(total_bufs, outs))]
        for cp in joins + keep:
            cp.start()
        for i, (t, o) in enumerate(zip(total_bufs, outs)):
            remote(t, cols(o, i, 1 - c), 4, i, sib).wait_recv()
        for cp in joins:
            cp.wait_send()
        for cp in keep:
            cp.wait()

    specs = [_full(a.shape) for a in arrays]
    return _pallas_call(
        body, name="allreduce_small", grid=(1,), in_specs=specs, out_specs=specs,
        out_shape=[_sds(a.shape) for a in arrays],
        scratch_shapes=([pltpu.VMEM(h, F32) for h in halves] + [pltpu.VMEM(h, F32) for h in halves]
                        + [pltpu.VMEM((NCHIP,) + h, dt) for h, dt in zip(halves, wire)] + [pltpu.VMEM(h, F32) for h in halves]
                        + [pltpu.SemaphoreType.DMA((5, n)), pltpu.SemaphoreType.DMA((5, n)), pltpu.SemaphoreType.DMA((n,))]),
        compiler_params=_params(32),
    )(*arrays)


def _adamw_terms(w, g, m, v):
    m = ADAM_B1 * m + (1.0 - ADAM_B1) * g
    v = ADAM_B2 * v + (1.0 - ADAM_B2) * jnp.square(g)
    m_hat = m / (1.0 - ADAM_B1 ** ADAM_STEP)
    v_hat = v / (1.0 - ADAM_B2 ** ADAM_STEP)
    return -ADAM_LR * (m_hat / (jnp.sqrt(v_hat) + ADAM_EPS) + ADAM_WD * w), m, v


ADAM_STEPS = 4


def _adamw_group(tag, ws, gs, ms, vs):
    n = len(ws)

    def body(*refs):
        ins, outs = refs[:4 * n], refs[4 * n:]
        for i in range(n):
            w, g, m, v = (ins[k * n + i][...] for k in range(4))
            outs[i][...] = g
            outs[n + i][...], outs[2 * n + i][...], outs[3 * n + i][...] = _adamw_terms(w, g, m, v)

    specs = [pl.BlockSpec((w.shape[0] // ADAM_STEPS, w.shape[1]), lambda i: (i, 0)) for w in ws]
    outs = _pallas_call(
        body, name="adamw_" + tag, grid=(ADAM_STEPS,), in_specs=specs * 4, out_specs=specs * 4,
        out_shape=[_sds(w.shape) for w in ws] * 4, compiler_params=_params(48),
    )(*_in_hbm(list(ws) + list(gs) + list(ms) + list(vs)))
    return outs[:n], outs[n:2 * n], outs[2 * n:3 * n], outs[3 * n:]


def _adamw_replicated(sums, row_of, direct):
    ns, nr, nd = len(sums), len(row_of), len(direct)

    def body(*refs):
        sum_refs = refs[:ns]
        ins = refs[ns:ns + 3 * nr + 4 * nd]
        outs = refs[ns + 3 * nr + 4 * nd:]
        for i, (_, _, _, si, row) in enumerate(row_of):
            w_ref, m_ref, v_ref = ins[3 * i:3 * i + 3]
            g = sum_refs[si][row:row + 1, :]
            outs[4 * i][...] = g
            outs[4 * i + 1][...], outs[4 * i + 2][...], outs[4 * i + 3][...] = _adamw_terms(w_ref[...], g, m_ref[...], v_ref[...])
        for i in range(nd):
            w_ref, m_ref, v_ref, g_ref = ins[3 * nr + 4 * i:3 * nr + 4 * i + 4]
            o = outs[4 * (nr + i):4 * (nr + i) + 4]
            g = g_ref[...]
            o[0][...] = g
            o[1][...], o[2][...], o[3][...] = _adamw_terms(w_ref[...], g, m_ref[...], v_ref[...])

    operands = list(sums)
    shapes = []
    for w, m, v, _, _ in row_of:
        operands += [w, m, v]
        shapes += [w.shape] * 4
    for w, m, v, g in direct:
        operands += [w, m, v, g]
        shapes += [w.shape] * 4
    flat = _pallas_call(
        body, name="adamw_replicated", grid=(1,), in_specs=[_full(a.shape) for a in operands],
        out_specs=[_full(s) for s in shapes], out_shape=[_sds(s) for s in shapes],
        compiler_params=_params(56),
    )(*operands)
    return [flat[4 * i:4 * i + 4] for i in range(nr + nd)]


class _Exchanges:
    def __init__(self, shards, conv_w, chip, core, apply):
        self.shards, self.conv_w, self.apply = shards, conv_w, apply
        self.active, self.calls = [], 0
        self.core_idx = jnp.reshape(core, (1,)).astype(jnp.int32)
        self.chip_core_idx = jnp.stack([chip, core]).astype(jnp.int32)

    def first(self):
        names = ["w_in", "w_glu"]
        got = _run_now("gather_first", _gather_group([self.shards[n] for n in names] + [self.conv_w],
                                                     [True, True, False]))
        out = dict(zip(names, got))
        out["conv_w"] = jnp.transpose(got[2], (1, 0, 2)).reshape(4, LW)
        return out

    def gather(self, names):
        return _gather_group([self.shards[n] for n in names], [True] * len(names))

    def reduce(self, tag, grads):
        self.active.append({"tag": tag, "names": list(grads), "stage": 0, "grads": list(grads.values())})

    def run(self, call):
        groups = self.active
        carries = [self._exchange_of(g) for g in groups]
        carry = _combine(carries)
        outs = list(call(carry))
        own = len(outs) - len(carry.out_shapes)
        landed = outs[own:]
        for g, c in zip(groups, carries):
            self._sum_after(g, landed[:len(c.out_shapes)])
            landed = landed[len(c.out_shapes):]
        self.active = [g for g in groups if g["stage"] < 3]
        return outs[:own]

    def _exchange_of(self, g):
        if g["stage"] == 0:
            return _swap_group(g["grads"])
        if g["stage"] == 1:
            return _exchange_group(g["bf16"])
        return _join_group(g["halves"])

    def _sum_after(self, g, landed):
        if g["stage"] == 0:
            g["f32"], g["bf16"] = _add_sibling_group(g["tag"], self.chip_core_idx, g["grads"], landed)
        elif g["stage"] == 1:
            g["halves"] = _add_chips_group(g["tag"], self.chip_core_idx, g["f32"], landed)
        else:
            self.apply(g["tag"], g["names"], [t.reshape(2 * t.shape[1], t.shape[2]) for t in landed])
        g["stage"] += 1

    def drain(self):
        while self.active:
            self.calls += 1
            self.run(lambda carry: _run_now("reduce_%d" % self.calls, carry))


INPUT_NAMES = (["x", "p"] + [n for n in
               ["g_mix", "w_in", "b_in", "lam_re", "lam_im", "log_dt", "s5_b_re", "s5_b_im", "s5_c_re", "s5_c_im", "s5_d",
                "w_glu", "b_glu", "conv_w", "conv_b", "w_r", "b_r", "w_i", "b_i", "lru_lambda", "w_a_out", "w_b_out", "w_o",
                "g_ffn", "w_ffn_gate", "w_ffn_up", "w_ffn_down", "g_ple_gate", "w_ple_gate", "b_ple_gate", "w_ple", "g_ple",
                "g_final"]])
WEIGHT_NAMES = INPUT_NAMES[2:]


def kernel(*args):
    names = INPUT_NAMES + ["loss_target"] + ["m_" + n for n in WEIGHT_NAMES] + ["v_" + n for n in WEIGHT_NAMES]
    assert len(args) == len(names)
    given = dict(zip(names, args))

    def view(name):
        a = given[name]
        return jnp.swapaxes(a, -1, -2) if name.endswith(TRANSPOSED) else a

    def unview(name, a):
        return jnp.swapaxes(a, -1, -2) if name in TRANSPOSED else a

    def local(name):
        return view(name) if name.endswith("g_final") else view(name)[0]

    xi, yi, ci = _mesh_pos()
    k0 = 2 * xi + yi
    x, p, tgt = given["x"][0], given["p"][0, 0], given["loss_target"][0]

    results = {}

    row_halves = {}

    def apply(tag, names, totals):
        totals = dict(zip(names, totals))
        row_halves.update({n: totals.pop(n) for n in names if n in ("w_in_lo", "w_in_hi")})
        if len(row_halves) == 2:
            totals["w_in"] = jnp.concatenate([row_halves.pop("w_in_lo"), row_halves.pop("w_in_hi")])
        names = list(totals)
        if not names:
            return
        new = _adamw_group(tag, [local(n) for n in names], list(totals.values()), [local("m_" + n) for n in names],
                           [local("v_" + n) for n in names])
        for kind, arrays in zip(("grad", "delta", "new_m", "new_v"), new):
            for n, arr in zip(names, arrays):
                results[kind, n] = unview(n, arr[None])

    comm = _Exchanges({n: local(n).astype(BF) for n, _ in SHARDED}, local("conv_w"), k0, ci, apply)
    w = {n: local(n) for n in WEIGHT_NAMES if n != "conv_w" and n not in dict(SHARDED)}
    gx, sums, blocks = _local_step(x, p, tgt, w, comm)

    sum_names, block_names = list(sums), list(blocks)
    red = _allreduce_small([sums[n] for n in sum_names] + [blocks[n] for n in block_names],
                           [F32] * len(sum_names) + [BF] * len(block_names))
    sums = dict(zip(sum_names, red[:len(sum_names)]))
    blocks = dict(zip(block_names, red[len(sum_names):]))
    loss = jnp.sum(sums[LOSS_ROW[0]][LOSS_ROW[1]])
    direct_g = _replicated_grads(w, sums, blocks)
    conv_rows = sums[CONV_W_ROWS[0]][CONV_W_ROWS[1]:CONV_W_ROWS[1] + 4]
    direct_g["conv_w"] = lax.dynamic_slice(conv_rows, (0, k0 * CONV_SHARD[1]), CONV_SHARD)
    as_row = lambda a: a.reshape(1, -1)
    row_names = list(ACC_ROWS)
    row_of = [(as_row(given[n]), as_row(given["m_" + n]), as_row(given["v_" + n]),
               sum_names.index(ACC_ROWS[n][0]), ACC_ROWS[n][1]) for n in row_names]
    direct_names = list(direct_g)
    direct = [(view(n), view("m_" + n), view("v_" + n), direct_g[n].reshape(view(n).shape)) for n in direct_names]
    done = _adamw_replicated([sums[n] for n in sum_names], row_of, direct)
    for n, four in zip(row_names + direct_names, done):
        for kind, arr in zip(("grad", "delta", "new_m", "new_v"), four):
            results[kind, n] = unview(n, arr).reshape(given[n].shape)

    out = [loss, gx[None]]
    for kind in ("grad", "delta", "new_m", "new_v"):
        out += [results[kind, n] for n in WEIGHT_NAMES]
    return tuple(out)
```

```python
import functools
import math

import jax
import jax.numpy as jnp
from jax import lax
from jax.experimental import pallas as pl
from jax.experimental.pallas import tpu as pltpu

F32 = jnp.float32
BF = jnp.bfloat16

D = 1024
S5W = 512
NG, NS, NP = 32, 64, 16
GN = NG * NS
LW = 1024
NH, HD = 16, 64
LRU_C = 8.0
FH = 2816
NCHIP = 4
FC = FH // NCHIP
PLE = 256
INC = S5W + LW + 2 * D
EPS = 1e-6
ADAM_LR, ADAM_B1, ADAM_B2, ADAM_EPS, ADAM_WD, ADAM_STEP = 0.001, 0.9, 0.999, 1e-08, 0.01, 10

TM = 256
TK = 1024
LC = 512
SUB = 8
VMEM_MB = 1024 * 1024
MESH = pl.DeviceIdType.MESH
ANY = pl.BlockSpec(memory_space=pl.ANY)


def _mm(a, b):
    return jnp.dot(a.astype(BF), b.astype(BF), preferred_element_type=F32)


def _mm_nt(a, b):
    return lax.dot_general(a.astype(BF), b.astype(BF), (((1,), (1,)), ((), ())), preferred_element_type=F32)


def _mm_tn(a, b):
    return lax.dot_general(a.astype(BF), b.astype(BF), (((0,), (0,)), ((), ())), preferred_element_type=F32)


def _rms(x):
    r = lax.rsqrt(jnp.mean(x * x, axis=-1, keepdims=True) + EPS)
    return x * r, r


def _rms_bwd(dy, xh, r, g):
    dxh = dy * g
    return r * (dxh - xh * jnp.mean(dxh * xh, axis=-1, keepdims=True))


def _colsum(x):
    return jnp.sum(x, axis=0, keepdims=True)


def _sig(x):
    return jax.nn.sigmoid(x)


def _gelu_grad(x):
    c = math.sqrt(2.0 / math.pi)
    t = jnp.tanh(c * (x + 0.044715 * x * x * x))
    return 0.5 * (1.0 + t) + 0.5 * x * (1.0 - t * t) * c * (1.0 + 3.0 * 0.044715 * x * x)


def _neg_expm1(x):
    series = -x * (1.0 + x * (0.5 + x * (1.0 / 6.0 + x * (1.0 / 24.0))))
    return jnp.where(x > -0.03, series, 1.0 - jnp.exp(x))


def _tok(width):
    return pl.BlockSpec((TM, width), lambda i: (i, 0))


def _tok_rev(width, nt):
    return pl.BlockSpec((TM, width), lambda i: (nt - 1 - i, 0))


def _full(shape):
    return pl.BlockSpec(shape, lambda i: (0,) * len(shape))


def _params(vmem_mb, **kw):
    return pltpu.CompilerParams(dimension_semantics=("arbitrary",), vmem_limit_bytes=vmem_mb * VMEM_MB, **kw)


def _sds(shape, dtype=F32):
    return jax.ShapeDtypeStruct(shape, dtype)


class _Carried:
    def __init__(self, operands, out_shapes, sems, start, finish, aliases=None):
        self.operands, self.out_shapes, self.sems = list(operands), list(out_shapes), list(sems)
        self.start, self.finish, self.aliases = start, finish, dict(aliases or {})


def _in_hbm(arrays):
    return [pltpu.with_memory_space_constraint(a, pltpu.HBM) for a in arrays]


def _pallas_call(body, carry=None, **kw):
    if carry is None:
        return pl.pallas_call(body, **kw)

    def at_step(corner):
        hit = [pl.program_id(d) == (size - 1 if corner else 0) for d, size in enumerate(kw["grid"])]
        return functools.reduce(jnp.logical_and, hit)

    name, grid, compiler_params = kw["name"], kw["grid"], kw["compiler_params"]
    in_specs, out_specs, out_shape = list(kw["in_specs"]), list(kw["out_specs"]), list(kw["out_shape"])
    scratch_shapes = list(kw.get("scratch_shapes", ()))
    n_in, n_out, n_scr = len(in_specs), len(out_specs), len(scratch_shapes)
    c_in, c_out = len(carry.operands), len(carry.out_shapes)

    def full_body(*refs):
        ins, refs = refs[:n_in], refs[n_in:]
        c_ins, refs = refs[:c_in], refs[c_in:]
        outs, refs = refs[:n_out], refs[n_out:]
        c_outs, refs = refs[:c_out], refs[c_out:]
        scratch, c_sems = refs[:n_scr], refs[n_scr:]

        @pl.when(at_step(0))
        def _():
            carry.start(c_ins, c_outs, c_sems)

        body(*ins, *outs, *scratch)

        @pl.when(at_step(1))
        def _():
            carry.finish(c_ins, c_outs, c_sems)

    call = pl.pallas_call(
        full_body, name=name, grid=grid, in_specs=in_specs + [ANY] * c_in, out_specs=out_specs + [ANY] * c_out,
        out_shape=out_shape + list(carry.out_shapes), scratch_shapes=scratch_shapes + list(carry.sems),
        input_output_aliases={n_in + i: n_out + o for i, o in carry.aliases.items()},
        compiler_params=compiler_params)
    return lambda *operands: call(*operands, *_in_hbm(carry.operands))


def _resident(pairs, sems):
    first = pl.program_id(0) == 0
    copies = [pltpu.make_async_copy(src, dst, sems.at[j]) for j, (src, dst) in enumerate(pairs)]

    @pl.when(first)
    def _():
        for cp in copies:
            cp.start()

    def wait(j):
        @pl.when(first)
        def _():
            copies[j].wait()

    return wait


def _resident_now(pairs, sems):
    @pl.when(pl.program_id(0) == 0)
    def _():
        copies = [pltpu.make_async_copy(src, dst, sems.at[j]) for j, (src, dst) in enumerate(pairs)]
        for cp in copies:
            cp.start()
        for cp in copies:
            cp.wait()


def _row_iota(width):
    return lax.broadcasted_iota(jnp.int32, (SUB, width), 0)


def _bcast_row(x, row):
    return jnp.broadcast_to(x[row:row + 1, :], x.shape)


def _slab(k):
    return pl.ds(pl.multiple_of(k * SUB, SUB), SUB)


QC = INC // NCHIP
Z_PARTS = ((0, S5W), (S5W, S5W + LW), (S5W + LW, INC))


def _inproj_fwd(x, g_mix, w_in, b_in, carry=None):
    L = x.shape[0]

    def body(x_ref, g_ref, w_hbm, b_ref, h_ref, ua_ref, ub_ref, gp_ref, w_vm, w_sems):
        _resident_now([(w_hbm.at[k], w_vm.at[k]) for k in range(NCHIP)], w_sems)
        xh, _ = _rms(x_ref[...])
        h = (xh * g_ref[...]).astype(BF)
        h_ref[...] = h
        for k in range(NCHIP):
            lo, hi = k * QC, (k + 1) * QC
            z = jnp.dot(h, w_vm[k], preferred_element_type=F32) + b_ref[:, lo:hi]
            for ref, (a, b) in zip((ua_ref, ub_ref, gp_ref), Z_PARTS):
                s, e = max(lo, a), min(hi, b)
                if s < e:
                    ref[:, s - a:e - a] = z[:, s - lo:e - lo]

    return _pallas_call(
        body, carry, name="inproj_fwd", grid=(L // TM,),
        in_specs=[_tok(D), _full((1, D)), ANY, _full((1, INC))],
        out_specs=[_tok(D), _tok(S5W), _tok(LW), _tok(2 * D)],
        out_shape=[_sds((L, D), BF), _sds((L, S5W)), _sds((L, LW)), _sds((L, 2 * D))],
        scratch_shapes=[pltpu.VMEM((NCHIP, D, QC), BF), pltpu.SemaphoreType.DMA((NCHIP,))],
        compiler_params=_params(40),
    )(x, g_mix, w_in, b_in)


def _inproj_bwd(x, dx1, dua, dub, dgp, g_mix, w_in, carry=None):
    L = x.shape[0]

    def body(x_ref, dx1_ref, dua_ref, dub_ref, dgp_ref, g_ref, w_hbm, gx_ref, dz_ref, dg_ref, db_ref, w_vm, w_sems):
        _resident_now([(w_hbm.at[k], w_vm.at[k]) for k in range(NCHIP)], w_sems)

        @pl.when(pl.program_id(0) == 0)
        def _():
            dg_ref[...] = jnp.zeros_like(dg_ref)
            db_ref[...] = jnp.zeros_like(db_ref)

        for src, (a, b) in zip((dua_ref, dub_ref, dgp_ref), Z_PARTS):
            d = src[...]
            dz_ref[:, a:b] = d.astype(BF)
            db_ref[0:1, a:b] += _colsum(d)
        dh = jnp.zeros((TM, D), F32)
        for k in range(NCHIP):
            dh = dh + lax.dot_general(dz_ref[:, k * QC:(k + 1) * QC], w_vm[k], (((1,), (1,)), ((), ())),
                                      preferred_element_type=F32)
        xh, r = _rms(x_ref[...])
        dg_ref[0:1, :] += _colsum(dh * xh)
        gx_ref[...] = dx1_ref[...] + _rms_bwd(dh, xh, r, g_ref[...])

    return _pallas_call(
        body, carry, name="inproj_bwd", grid=(L // TM,),
        in_specs=[_tok(D), _tok(D), _tok(S5W), _tok(LW), _tok(2 * D), _full((1, D)), ANY],
        out_specs=[_tok(D), _tok(INC), _full((SUB, D)), _full((SUB, INC))],
        out_shape=[_sds((L, D)), _sds((L, INC), BF), _sds((SUB, D)), _sds((SUB, INC))],
        scratch_shapes=[pltpu.VMEM((NCHIP, D, QC), BF), pltpu.SemaphoreType.DMA((NCHIP,))],
        compiler_params=_params(40),
    )(x, dx1, dua, dub, dgp, g_mix, w_in)


def _cscan(xr_ref, xi_ref, con_ref, cr_ref, ci_ref, reverse):
    n_slab = xr_ref.shape[0] // SUB
    width = xr_ref.shape[1]
    for lc in range(width // LC):
        cols = slice(lc * LC, (lc + 1) * LC)
        con = [con_ref[SUB * j:SUB * (j + 1), cols] for j in range(8)]

        def step(k, carry, cols=cols, con=con):
            cr, ci = carry
            rows = _slab(n_slab - 1 - k if reverse else k)
            xr, xi = xr_ref[rows, cols], xi_ref[rows, cols]
            for j, sh in enumerate((1, 2, 4)):
                mr, mi = con[2 * j], con[2 * j + 1]
                pr = pltpu.roll(xr, SUB - sh if reverse else sh, 0)
                pi = pltpu.roll(xi, SUB - sh if reverse else sh, 0)
                xr, xi = xr + mr * pr - mi * pi, xi + mr * pi + mi * pr
            xr, xi = xr + con[6] * cr - con[7] * ci, xi + con[6] * ci + con[7] * cr
            xr_ref[rows, cols] = xr
            xi_ref[rows, cols] = xi
            row = 0 if reverse else SUB - 1
            return _bcast_row(xr, row), _bcast_row(xi, row)

        cr, ci = lax.fori_loop(0, n_slab, step, (cr_ref[:, cols], ci_ref[:, cols]))
        cr_ref[:, cols] = cr
        ci_ref[:, cols] = ci


def _s5_fwd(ua, bbr, bbi, ccr, cci, dsk, con, w_glu, b_glu, carry=None):
    L = ua.shape[0]

    def body(ua_ref, bbr_hbm, bbi_hbm, ccr_hbm, cci_hbm, dsk_ref, con_ref, wg_ref, bg_ref,
             sr_ref, si_ref, y_ref, zg_ref, ya_ref, bbr_vm, bbi_vm, ccr_vm, cci_vm, cr_ref, ci_ref, w_sems):
        landed = _resident([(bbr_hbm, bbr_vm), (bbi_hbm, bbi_vm), (ccr_hbm, ccr_vm), (cci_hbm, cci_vm)], w_sems)

        @pl.when(pl.program_id(0) == 0)
        def _():
            cr_ref[...] = jnp.zeros_like(cr_ref)
            ci_ref[...] = jnp.zeros_like(ci_ref)

        u = ua_ref[...]
        ub = u.astype(BF)
        landed(0)
        sr_ref[...] = jnp.dot(ub, bbr_vm[...], preferred_element_type=F32)
        landed(1)
        si_ref[...] = jnp.dot(ub, bbi_vm[...], preferred_element_type=F32)
        _cscan(sr_ref, si_ref, con_ref, cr_ref, ci_ref, reverse=False)
        landed(2)
        landed(3)
        y =_mm_nt(sr_ref[...], ccr_vm[...]) - _mm_nt(si_ref[...], cci_vm[...]) + dsk_ref[...] * u
        y_ref[...] = y
        zg = jax.nn.gelu(y)
        zg_ref[...] = zg.astype(BF)
        q = _mm(zg, wg_ref[...]) + bg_ref[...]
        ya_ref[...] = (zg * _sig(q)).astype(BF)

    return _pallas_call(
        body, carry, name="s5_fwd", grid=(L // TM,),
        in_specs=[_tok(S5W), ANY, ANY, ANY, ANY, _full((1, S5W)), _full((8 * SUB, GN)),
                  _full((S5W, S5W)), _full((1, S5W))],
        out_specs=[_tok(GN), _tok(GN), _tok(S5W), _tok(S5W), _tok(S5W)],
        out_shape=[_sds((L, GN)), _sds((L, GN)), _sds((L, S5W)), _sds((L, S5W), BF), _sds((L, S5W), BF)],
        scratch_shapes=[pltpu.VMEM((S5W, GN), BF), pltpu.VMEM((S5W, GN), BF), pltpu.VMEM((S5W, GN), BF),
                        pltpu.VMEM((S5W, GN), BF), pltpu.VMEM((SUB, GN), F32), pltpu.VMEM((SUB, GN), F32),
                        pltpu.SemaphoreType.DMA((4,))],
        compiler_params=_params(44),
    )(ua, bbr, bbi, ccr, cci, dsk, con, w_glu, b_glu)


def _s5_bwd(dya, y, ua, sr, si, bbr, bbi, ccr, cci, dsk, con_rev, w_glu, b_glu, carry=None):
    L = ua.shape[0]
    nt = L // TM
    spt = TM // SUB
    n_slab = spt

    def halo_map(i):
        return (jnp.maximum((nt - 1 - i) * spt - 1, 0), 0)

    def body(dya_ref, y_ref, ua_ref, sr_ref, si_ref, hr_ref, hi_ref, bbr_hbm, bbi_hbm, ccr_hbm, cci_hbm,
             dsk_ref, con_ref, wg_ref, bg_ref,
             dua_ref, dq_ref, dy_ref, lr_ref, li_ref, da_ref, dsm_ref,
             bbr_vm, bbi_vm, ccr_vm, cci_vm, cr_ref, ci_ref, w_sems):
        i = pl.program_id(0)
        landed = _resident([(ccr_hbm, ccr_vm), (cci_hbm, cci_vm), (bbr_hbm, bbr_vm), (bbi_hbm, bbi_vm)], w_sems)

        @pl.when(i == 0)
        def _():
            cr_ref[...] = jnp.zeros_like(cr_ref)
            ci_ref[...] = jnp.zeros_like(ci_ref)
            da_ref[...] = jnp.zeros_like(da_ref)
            dsm_ref[...] = jnp.zeros_like(dsm_ref)

        u = ua_ref[...]
        yv = y_ref[...]
        dya = dya_ref[...]
        zg = jax.nn.gelu(yv)
        sg = _sig(_mm(zg, wg_ref[...]) + bg_ref[...])
        dq = dya * zg * sg * (1.0 - sg)
        dq_ref[...] = dq.astype(BF)
        dzg = dya * sg + _mm_nt(dq, wg_ref[...])
        dy = dzg * _gelu_grad(yv)
        dyb = dy.astype(BF)
        dy_ref[...] = dyb
        dsm_ref[0:1, :] += _colsum(dy * u)
        dsm_ref[1:2, :] += _colsum(dq)
        landed(0)
        lr_ref[...] = jnp.dot(dyb, ccr_vm[...], preferred_element_type=F32)
        landed(1)
        li_ref[...] = -jnp.dot(dyb, cci_vm[...], preferred_element_type=F32)
        _cscan(lr_ref, li_ref, con_ref, cr_ref, ci_ref, reverse=True)

        first_tile = (i == nt - 1)
        row = _row_iota(LC)
        for lc in range(GN // LC):
            cols = slice(lc * LC, (lc + 1) * LC)
            h_r = jnp.where(first_tile, 0.0, hr_ref[:, cols])
            h_i = jnp.where(first_tile, 0.0, hi_ref[:, cols])

            def step(k, acc, cols=cols, h_r=h_r, h_i=h_i):
                ar, ai = acc
                rows = _slab(k)
                prev = _slab(jnp.maximum(k - 1, 0))
                pr = jnp.where(k == 0, h_r, sr_ref[prev, cols])
                pi = jnp.where(k == 0, h_i, si_ref[prev, cols])
                spr = pltpu.roll(jnp.where(row == SUB - 1, pr, sr_ref[rows, cols]), 1, 0)
                spi = pltpu.roll(jnp.where(row == SUB - 1, pi, si_ref[rows, cols]), 1, 0)
                lr, li = lr_ref[rows, cols], li_ref[rows, cols]
                return ar + lr * spr + li * spi, ai + li * spr - lr * spi

            zero = jnp.zeros((SUB, LC), F32)
            ar, ai = lax.fori_loop(0, n_slab, step, (zero, zero))
            da_ref[0:1, cols] += _colsum(ar)
            da_ref[1:2, cols] += _colsum(ai)

        landed(2)
        landed(3)
        dua_ref[...] = (dy * dsk_ref[...] + _mm_nt(lr_ref[...], bbr_vm[...]) + _mm_nt(li_ref[...], bbi_vm[...]))

    return _pallas_call(
        body, carry, name="s5_bwd", grid=(nt,),
        in_specs=[_tok_rev(S5W, nt), _tok_rev(S5W, nt), _tok_rev(S5W, nt), _tok_rev(GN, nt), _tok_rev(GN, nt),
                  pl.BlockSpec((SUB, GN), halo_map), pl.BlockSpec((SUB, GN), halo_map),
                  ANY, ANY, ANY, ANY, _full((1, S5W)), _full((8 * SUB, GN)), _full((S5W, S5W)), _full((1, S5W))],
        out_specs=[_tok_rev(S5W, nt), _tok_rev(S5W, nt), _tok_rev(S5W, nt), _tok_rev(GN, nt), _tok_rev(GN, nt),
                   _full((SUB, GN)), _full((SUB, S5W))],
        out_shape=[_sds((L, S5W)), _sds((L, S5W), BF), _sds((L, S5W), BF), _sds((L, GN)), _sds((L, GN)),
                   _sds((SUB, GN)), _sds((SUB, S5W))],
        scratch_shapes=[pltpu.VMEM((S5W, GN), BF), pltpu.VMEM((S5W, GN), BF), pltpu.VMEM((S5W, GN), BF),
                        pltpu.VMEM((S5W, GN), BF), pltpu.VMEM((SUB, GN), F32), pltpu.VMEM((SUB, GN), F32),
                        pltpu.SemaphoreType.DMA((4,))],
        compiler_params=_params(52),
    )(dya, y, ua, sr, si, sr, si, bbr, bbi, ccr, cci, dsk, con_rev, w_glu, b_glu)


def _lru_gate_terms(rg, sp):
    log_a = -LRU_C * rg * sp
    a = jnp.exp(log_a)
    mult = jnp.sqrt(_neg_expm1(2.0 * log_a))
    return a, mult


def _lru_fwd(ub, conv_w, conv_b, wr, wi, b_r, b_i, sp, carry=None):
    L = ub.shape[0]
    n_slab = TM // SUB

    def body(ub_ref, cw_ref, cb_ref, wr_ref, wi_ref, br_ref, bi_ref, sp_ref,
             xc_ref, rg_ref, ig_ref, h_ref, hp_ref, a_ref, halo_ref, carry_ref):
        @pl.when(pl.program_id(0) == 0)
        def _():
            halo_ref[...] = jnp.zeros_like(halo_ref)
            carry_ref[...] = jnp.zeros_like(carry_ref)

        row = _row_iota(LW)
        taps = [cw_ref[k:k + 1, :] for k in range(4)]
        cb = cb_ref[...]

        def conv_step(k, prev):
            rows = _slab(k)
            cur = ub_ref[rows, :]
            acc = taps[3] * cur + cb
            for j in (1, 2, 3):
                acc = acc + taps[3 - j] * pltpu.roll(jnp.where(row >= SUB - j, prev, cur), j, 0)
            xc_ref[rows, :] = acc
            return cur

        halo_ref[...] = lax.fori_loop(0, n_slab, conv_step, halo_ref[...])

        xc = xc_ref[...]
        xcb = xc.astype(BF)
        rg = _sig(jnp.dot(xcb, wr_ref[...], preferred_element_type=F32) + br_ref[...])
        ig = _sig(jnp.dot(xcb, wi_ref[...], preferred_element_type=F32) + bi_ref[...])
        rg_ref[...] = rg
        ig_ref[...] = ig
        a, mult = _lru_gate_terms(rg, sp_ref[...])
        a_ref[...] = a
        h_ref[...] = mult * ig * xc

        rowc = _row_iota(LC)
        for lc in range(LW // LC):
            cols = slice(lc * LC, (lc + 1) * LC)

            def step(k, c, cols=cols):
                rows = _slab(k)
                av, b = a_ref[rows, cols], h_ref[rows, cols]
                for sh in (1, 2, 4):
                    keep = rowc >= sh
                    b = b + av * jnp.where(keep, pltpu.roll(b, sh, 0), 0.0)
                    av = av * jnp.where(keep, pltpu.roll(av, sh, 0), 1.0)
                h = b + av * c
                h_ref[rows, cols] = h
                hp_ref[rows, cols] = jnp.where(rowc == 0, c, pltpu.roll(h, 1, 0))
                return _bcast_row(h, SUB - 1)

            carry_ref[:, cols] = lax.fori_loop(0, n_slab, step, carry_ref[:, cols])

    return _pallas_call(
        body, carry, name="lru_fwd", grid=(L // TM,),
        in_specs=[_tok(LW), _full((4, LW)), _full((1, LW)), _full((LW, LW)), _full((LW, LW)),
                  _full((1, LW)), _full((1, LW)), _full((1, LW))],
        out_specs=[_tok(LW)] * 5,
        out_shape=[_sds((L, LW))] * 5,
        scratch_shapes=[pltpu.VMEM((TM, LW), F32), pltpu.VMEM((SUB, LW), F32), pltpu.VMEM((SUB, LW), F32)],
        compiler_params=_params(40),
    )(ub, conv_w, conv_b, wr, wi, b_r, b_i, sp)


def _lru_bwd(dyb, xc, rg, ig, hp, ub, conv_w, wr, wi, sp, dsp, carry=None):
    L = ub.shape[0]
    nt = L // TM
    spt = TM // SUB
    n_slab = spt

    def halo_map(i):
        return (jnp.maximum((nt - 1 - i) * spt - 1, 0), 0)

    def body(dh_ref, xc_ref, rg_ref, ig_ref, hp_ref, ub_ref, uh_ref, cw_ref, wr_ref, wi_ref, sp_ref, dsp_ref,
             dub_ref, dpr_ref, dpi_ref, acc_ref, a_ref, lam_ref, dxc_ref, carry_ref, next_ref):
        i = pl.program_id(0)

        @pl.when(i == 0)
        def _():
            carry_ref[...] = jnp.zeros_like(carry_ref)
            next_ref[...] = jnp.zeros_like(next_ref)
            acc_ref[...] = jnp.zeros_like(acc_ref)

        sp = sp_ref[...]
        rg, ig, xc = rg_ref[...], ig_ref[...], xc_ref[...]
        a, mult = _lru_gate_terms(rg, sp)
        a_ref[...] = a

        rowc = _row_iota(LC)
        for lc in range(LW // LC):
            cols = slice(lc * LC, (lc + 1) * LC)

            def step(k, c, cols=cols):
                rows = _slab(n_slab - 1 - k)
                av, dh = a_ref[rows, cols], dh_ref[rows, cols]
                b = av * dh
                for sh in (1, 2, 4):
                    keep = rowc < SUB - sh
                    b = b + av * jnp.where(keep, pltpu.roll(b, SUB - sh, 0), 0.0)
                    av = av * jnp.where(keep, pltpu.roll(av, SUB - sh, 0), 1.0)
                mu = b + av * c
                lam_ref[rows, cols] = dh + jnp.where(rowc == SUB - 1, c, pltpu.roll(mu, SUB - 1, 0))
                return _bcast_row(mu, 0)

            carry_ref[:, cols] = lax.fori_loop(0, n_slab, step, carry_ref[:, cols])

        lam = lam_ref[...]
        d_a = lam * hp_ref[...]
        d_mult = lam * ig * xc
        d_ig = lam * mult * xc
        dxc = lam * mult * ig
        d_log_a = d_a * a - d_mult * a * a / mult
        d_rg = (-LRU_C) * sp * d_log_a
        acc_ref[0:1, :] += _colsum((-LRU_C) * rg * d_log_a) * dsp_ref[...]
        dpr = d_rg * rg * (1.0 - rg)
        dpi = d_ig * ig * (1.0 - ig)
        acc_ref[1:2, :] += _colsum(dpr)
        acc_ref[2:3, :] += _colsum(dpi)
        dprb, dpib = dpr.astype(BF), dpi.astype(BF)
        dpr_ref[...] = dprb
        dpi_ref[...] = dpib
        dxc = dxc + _mm_nt(dprb, wr_ref[...]) + _mm_nt(dpib, wi_ref[...])
        dxc_ref[...] = dxc
        acc_ref[3:4, :] += _colsum(dxc)

        row = _row_iota(LW)
        taps = [cw_ref[k:k + 1, :] for k in range(4)]
        u_halo = jnp.where(i == nt - 1, 0.0, uh_ref[...])
        nxt_tile = next_ref[...]

        def conv_step(k, accs):
            rows = _slab(k)
            cur = dxc_ref[rows, :]
            nxt = jnp.where(k == n_slab - 1, nxt_tile, dxc_ref[_slab(jnp.minimum(k + 1, n_slab - 1)), :])
            ucur = ub_ref[rows, :]
            uprev = jnp.where(k == 0, u_halo, ub_ref[_slab(jnp.maximum(k - 1, 0)), :])
            du = taps[3] * cur
            new = [accs[3] + cur * ucur]
            for j in (1, 2, 3):
                du = du + taps[3 - j] * pltpu.roll(jnp.where(row < j, nxt, cur), SUB - j, 0)
                new.append(accs[3 - j] + cur * pltpu.roll(jnp.where(row >= SUB - j, uprev, ucur), j, 0))
            dub_ref[rows, :] = du
            return tuple(new[::-1])

        zero = jnp.zeros((SUB, LW), F32)
        accs = lax.fori_loop(0, n_slab, conv_step, (zero, zero, zero, zero))
        for k in range(4):
            acc_ref[4 + k:5 + k, :] += _colsum(accs[k])
        next_ref[...] = dxc_ref[0:SUB, :]

    return _pallas_call(
        body, carry, name="lru_bwd", grid=(nt,),
        in_specs=[_tok_rev(LW, nt)] * 6 + [pl.BlockSpec((SUB, LW), halo_map), _full((4, LW)),
                                           _full((LW, LW)), _full((LW, LW)), _full((1, LW)), _full((1, LW))],
        out_specs=[_tok_rev(LW, nt), _tok_rev(LW, nt), _tok_rev(LW, nt), _full((SUB, LW))],
        out_shape=[_sds((L, LW)), _sds((L, LW), BF), _sds((L, LW), BF), _sds((SUB, LW))],
        scratch_shapes=[pltpu.VMEM((TM, LW), F32), pltpu.VMEM((TM, LW), F32), pltpu.VMEM((TM, LW), F32),
                        pltpu.VMEM((SUB, LW), F32), pltpu.VMEM((SUB, LW), F32)],
        compiler_params=_params(48),
    )(dyb, xc, rg, ig, hp, ub, ub, conv_w, wr, wi, sp, dsp)


AC = D // NCHIP


def _merge_fwd(x, ya, yb, gp, w_a, w_b, w_o, carry=None):
    L = x.shape[0]

    def body(x_ref, ya_ref, yb_ref, gp_ref, wa_ref, wb_ref, wo_ref, x1_ref, pa_ref, pb_ref, mg_ref):
        ya = ya_ref[...]
        for k in range(NCHIP):
            pa_ref[:, k * AC:(k + 1) * AC] = jnp.dot(ya, wa_ref[k], preferred_element_type=F32)
        pb = _mm(yb_ref[...], wb_ref[...])
        pb_ref[...] = pb
        gp = gp_ref[...]
        merged = (_sig(gp[:, :D]) * pa_ref[...] + _sig(gp[:, D:]) * pb).astype(BF)
        mg_ref[...] = merged
        x1_ref[...] = x_ref[...] + jnp.dot(merged, wo_ref[...], preferred_element_type=F32)

    return _pallas_call(
        body, carry, name="merge_fwd", grid=(L // TM,),
        in_specs=[_tok(D), _tok(S5W), _tok(LW), _tok(2 * D), _full((NCHIP, S5W, AC)), _full((LW, D)), _full((D, D))],
        out_specs=[_tok(D), _tok(D), _tok(D), _tok(D)],
        out_shape=[_sds((L, D)), _sds((L, D)), _sds((L, D)), _sds((L, D), BF)],
        compiler_params=_params(40),
    )(x, ya, yb, gp, w_a, w_b, w_o)


def _merge_bwd(dx1, gp, pa, pb, w_a, w_b, w_o, carry=None):
    L = dx1.shape[0]

    def body(dx1_ref, gp_ref, pa_ref, pb_ref, wa_ref, wb_ref, wo_ref, dya_ref, dyb_ref, dgp_ref, dpa_ref, dpb_ref):
        dm = _mm_nt(dx1_ref[...], wo_ref[...])
        gp = gp_ref[...]
        sa, sb = _sig(gp[:, :D]), _sig(gp[:, D:])
        dpa = (dm * sa).astype(BF)
        dpb = (dm * sb).astype(BF)
        dpa_ref[...] = dpa
        dpb_ref[...] = dpb
        dgp_ref[:, :D] = dm * pa_ref[...] * sa * (1.0 - sa)
        dgp_ref[:, D:] = dm * pb_ref[...] * sb * (1.0 - sb)
        dya = jnp.zeros((TM, S5W), F32)
        for k in range(NCHIP):
            dya = dya + _mm_nt(dpa[:, k * AC:(k + 1) * AC], wa_ref[k])
        dya_ref[...] = dya
        dyb_ref[...] = _mm_nt(dpb, wb_ref[...])

    return _pallas_call(
        body, carry, name="merge_bwd", grid=(L // TM,),
        in_specs=[_tok(D), _tok(2 * D), _tok(D), _tok(D), _full((NCHIP, S5W, AC)), _full((LW, D)), _full((D, D))],
        out_specs=[_tok(S5W), _tok(LW), _tok(2 * D), _tok(D), _tok(D)],
        out_shape=[_sds((L, S5W)), _sds((L, LW)), _sds((L, 2 * D)), _sds((L, D), BF), _sds((L, D), BF)],
        compiler_params=_params(40),
    )(dx1, gp, pa, pb, w_a, w_b, w_o)


def _chunk_tok(width):
    return pl.BlockSpec((NCHIP, TM, width), lambda i: (0, i, 0))


def _ffn_fwd(x1, g_ffn, wg, wu, wd, carry=None):
    L = x1.shape[0]

    def body(x_ref, g_ref, wg_hbm, wu_hbm, wd_hbm, x2_ref, h2_ref, gg_ref, uu_ref, wg_vm, wu_vm, wd_vm, w_sems):
        _resident_now([(src.at[c], dst.at[c]) for c in range(NCHIP)
                       for src, dst in ((wg_hbm, wg_vm), (wu_hbm, wu_vm), (wd_hbm, wd_vm))], w_sems)
        x = x_ref[...]
        xh, _ = _rms(x)
        h2 = (xh * g_ref[...]).astype(BF)
        h2_ref[...] = h2
        out = x
        for c in range(NCHIP):
            gg = lax.dot_general(h2, wg_vm[c], (((1,), (1,)), ((), ())), preferred_element_type=F32)
            uu = lax.dot_general(h2, wu_vm[c], (((1,), (1,)), ((), ())), preferred_element_type=F32)
            gg_ref[c] = gg.astype(BF)
            uu_ref[c] = uu.astype(BF)
            act = (gg * _sig(gg) * uu).astype(BF)
            out = out + jnp.dot(act, wd_vm[c], preferred_element_type=F32)
        x2_ref[...] = out

    return _pallas_call(
        body, carry, name="ffn_fwd", grid=(L // TM,),
        in_specs=[_tok(D), _full((1, D)), ANY, ANY, ANY],
        out_specs=[_tok(D), _tok(D), _chunk_tok(FC), _chunk_tok(FC)],
        out_shape=[_sds((L, D)), _sds((L, D), BF), _sds((NCHIP, L, FC), BF), _sds((NCHIP, L, FC), BF)],
        scratch_shapes=[pltpu.VMEM((NCHIP, FC, D), BF)] * 3 + [pltpu.SemaphoreType.DMA((3 * NCHIP,))],
        compiler_params=_params(52),
    )(x1, g_ffn, wg, wu, wd)


def _ffn_bwd(x1, dx2, gg, uu, g_ffn, wg, wu, wd, carry=None):
    L = x1.shape[0]

    def body(x_ref, dx2_ref, gg_ref, uu_ref, g_ref, wg_hbm, wu_hbm, wd_hbm,
             dx1_ref, act_ref, dgg_ref, duu_ref, dg_ref, wg_vm, wu_vm, wd_vm, w_sems):
        _resident_now([(src.at[c], dst.at[c]) for c in range(NCHIP)
                       for src, dst in ((wg_hbm, wg_vm), (wu_hbm, wu_vm), (wd_hbm, wd_vm))], w_sems)

        @pl.when(pl.program_id(0) == 0)
        def _():
            dg_ref[...] = jnp.zeros_like(dg_ref)

        dx2 = dx2_ref[...]
        dx2b = dx2.astype(BF)
        dh2 = jnp.zeros((TM, D), F32)
        for c in range(NCHIP):
            g = gg_ref[c].astype(F32)
            u = uu_ref[c].astype(F32)
            s = _sig(g)
            silu = g * s
            act_ref[c] = (silu * u).astype(BF)
            dact = lax.dot_general(dx2b, wd_vm[c], (((1,), (1,)), ((), ())), preferred_element_type=F32)
            dg = (dact * u * s * (1.0 + g * (1.0 - s))).astype(BF)
            du = (dact * silu).astype(BF)
            dgg_ref[c] = dg
            duu_ref[c] = du
            dh2 = dh2 + jnp.dot(dg, wg_vm[c], preferred_element_type=F32)
            dh2 = dh2 + jnp.dot(du, wu_vm[c], preferred_element_type=F32)
        xh, r = _rms(x_ref[...])
        dg_ref[0:1, :] += _colsum(dh2 * xh)
        dx1_ref[...] = dx2 + _rms_bwd(dh2, xh, r, g_ref[...])

    return _pallas_call(
        body, carry, name="ffn_bwd", grid=(L // TM,),
        in_specs=[_tok(D), _tok(D), _chunk_tok(FC), _chunk_tok(FC), _full((1, D)), ANY, ANY, ANY],
        out_specs=[_tok(D), _chunk_tok(FC), _chunk_tok(FC), _chunk_tok(FC), _full((SUB, D))],
        out_shape=[_sds((L, D)), _sds((NCHIP, L, FC), BF), _sds((NCHIP, L, FC), BF), _sds((NCHIP, L, FC), BF),
                   _sds((SUB, D))],
        scratch_shapes=[pltpu.VMEM((NCHIP, FC, D), BF)] * 3 + [pltpu.SemaphoreType.DMA((3 * NCHIP,))],
        compiler_params=_params(56),
    )(x1, dx2, gg, uu, g_ffn, wg, wu, wd)


def _ple_loss(x2, p, tgt, g_pg, w_pg, b_pg, w_ple, g_ple, g_final):
    L = x2.shape[0]

    def body(x2_ref, p_ref, t_ref, gpg_ref, wpg_ref, bpg_ref, wple_ref, gple_ref, gf_ref,
             dx2_ref, n2_ref, dpre_ref, de0_ref, acc_ref):
        @pl.when(pl.program_id(0) == 0)
        def _():
            acc_ref[...] = jnp.zeros_like(acc_ref)

        x2 = x2_ref[...]
        x2h, r2 = _rms(x2)
        n2 = (x2h * gpg_ref[...]).astype(BF)
        n2_ref[...] = n2
        gate = _sig(jnp.dot(n2, wpg_ref[...], preferred_element_type=F32) + bpg_ref[...])
        pb = p_ref[...].astype(BF)
        e0 = jnp.concatenate([jnp.dot(pb, wple_ref[k], preferred_element_type=F32) for k in range(NCHIP)], axis=1)
        e0h, re = _rms(e0)
        e = e0h * gple_ref[...]
        x3 = x2 + gate * e
        x3h, r3 = _rms(x3)
        diff = x3h * gf_ref[...] - t_ref[...]
        acc_ref[4:5, :] += _colsum(diff * diff) * (0.5 / D)
        dy = diff * (1.0 / D)
        acc_ref[3:4, :] += _colsum(dy * x3h)
        dx3 = _rms_bwd(dy, x3h, r3, gf_ref[...])
        de = dx3 * gate
        acc_ref[2:3, :] += _colsum(de * e0h)
        de0_ref[...] = _rms_bwd(de, e0h, re, gple_ref[...]).astype(BF)
        dpre = dx3 * e * gate * (1.0 - gate)
        acc_ref[1:2, :] += _colsum(dpre)
        dpreb = dpre.astype(BF)
        dpre_ref[...] = dpreb
        dn2 = lax.dot_general(dpreb, wpg_ref[...], (((1,), (1,)), ((), ())), preferred_element_type=F32)
        acc_ref[0:1, :] += _colsum(dn2 * x2h)
        dx2_ref[...] = dx3 + _rms_bwd(dn2, x2h, r2, gpg_ref[...])

    return _pallas_call(
        body, name="ple_loss", grid=(L // TM,),
        in_specs=[_tok(D), _tok(PLE), _tok(D), _full((1, D)), _full((D, D)), _full((1, D)), _full((NCHIP, PLE, AC)),
                  _full((1, D)), _full((1, D))],
        out_specs=[_tok(D), _tok(D), _tok(D), _tok(D), _full((SUB, D))],
        out_shape=[_sds((L, D)), _sds((L, D), BF), _sds((L, D), BF), _sds((L, D), BF), _sds((SUB, D))],
        compiler_params=_params(40),
    )(x2, p, tgt, g_pg, w_pg, b_pg, w_ple, g_ple, g_final)


def _tn(name, a, b, col_chunk=None, a_block=None, carry=None):
    L = a.shape[-2]
    m, n = a.shape[-1], b.shape[-1]
    a_col = 0
    if a_block is not None:
        a_col, m = a_block
    if a.ndim == 3 or b.ndim == 3:
        nj, bn = (a if a.ndim == 3 else b).shape[0], n
        a_spec = (pl.BlockSpec((None, TK, m), lambda j, t: (j, t, 0)) if a.ndim == 3
                  else pl.BlockSpec((TK, m), lambda j, t: (t, 0)))
        b_spec = (pl.BlockSpec((None, TK, n), lambda j, t: (j, t, 0)) if b.ndim == 3
                  else pl.BlockSpec((TK, n), lambda j, t: (t, 0)))
        out_spec, out_shape = pl.BlockSpec((None, m, n), lambda j, t: (j, 0, 0)), _sds((nj, m, n))
    else:
        bn = col_chunk
        if bn is None:
            bn = next((cand for cand in (1024, 512) if n > cand and n % cand == 0), n)
        nj = n // bn
        a_spec = pl.BlockSpec((TK, m), lambda j, t: (t, a_col))
        b_spec = pl.BlockSpec((TK, bn), lambda j, t: (t, j))
        if col_chunk is None:
            out_spec, out_shape = pl.BlockSpec((m, bn), lambda j, t: (0, j)), _sds((m, n))
        else:
            out_spec, out_shape = pl.BlockSpec((None, m, bn), lambda j, t: (j, 0, 0)), _sds((nj, m, bn))

    def body(a_ref, b_ref, o_ref):
        @pl.when(pl.program_id(1) == 0)
        def _():
            o_ref[...] = jnp.zeros_like(o_ref)

        o_ref[...] += _mm_tn(a_ref[...], b_ref[...])

    outs = _pallas_call(
        body, carry, name=name, grid=(nj, L // TK), in_specs=[a_spec, b_spec], out_specs=[out_spec],
        out_shape=[pltpu.HBM(out_shape.shape, out_shape.dtype)],
        compiler_params=pltpu.CompilerParams(dimension_semantics=("arbitrary", "arbitrary"),
                                             vmem_limit_bytes=40 * VMEM_MB),
    )(a, b)
    return outs[0] if carry is None else outs


LANE = 128


def _tn_blocks(name, a, bs, ga, gb, carry=None):
    L, m, n, nb = a.shape[0], a.shape[1], bs[0].shape[1], len(bs)
    per = LANE // ga
    wb = per * gb
    n_super = m // LANE

    def body(a_ref, *refs):
        b_refs, o_refs, acc_refs = refs[:nb], refs[nb:2 * nb], refs[2 * nb:]
        t = pl.program_id(0)

        @pl.when(t == 0)
        def _():
            for acc in acc_refs:
                acc[...] = jnp.zeros_like(acc)

        lhs = a_ref[...].astype(BF)
        for b_ref, acc in zip(b_refs, acc_refs):
            rhs = b_ref[...].astype(BF)
            for j in range(n_super):
                acc[j] += _mm_tn(lhs[:, j * LANE:(j + 1) * LANE], rhs[:, j * wb:(j + 1) * wb])

        @pl.when(t == L // TK - 1)
        def _():
            own = (lax.broadcasted_iota(jnp.int32, (LANE, wb), 0) // ga) == (lax.broadcasted_iota(jnp.int32, (LANE, wb), 1) // gb)
            for o_ref, acc in zip(o_refs, acc_refs):
                for j in range(n_super):
                    kept = jnp.where(own, acc[j], 0.0)
                    o_ref[:, j * wb:(j + 1) * wb] = jnp.sum(kept.reshape(per, ga, wb), axis=0)

    outs = _pallas_call(
        body, carry, name=name, grid=(L // TK,),
        in_specs=[pl.BlockSpec((TK, m), lambda t: (t, 0))] + [pl.BlockSpec((TK, n), lambda t: (t, 0))] * nb,
        out_specs=[_full((ga, n))] * nb, out_shape=[_sds((ga, n))] * nb,
        scratch_shapes=[pltpu.VMEM((n_super, LANE, wb), F32)] * nb,
        compiler_params=_params(48),
    )(*_in_hbm([a] + list(bs)))
    return list(outs)


def _s5_discretize(lam_re, lam_im, log_dt, b_re, b_im):
    dt = jnp.exp(log_dt)[:, None]
    mag = jnp.exp(lam_re * dt)
    ar = mag * jnp.cos(lam_im * dt)
    ai = mag * jnp.sin(lam_im * dt)
    den = lam_re * lam_re + lam_im * lam_im
    nr = ar - 1.0
    fr = (nr * lam_re + ai * lam_im) / den
    fi = (ai * lam_re - nr * lam_im) / den
    bbr = fr[:, None, :] * b_re - fi[:, None, :] * b_im
    bbi = fr[:, None, :] * b_im + fi[:, None, :] * b_re
    return ar, ai, bbr, bbi


def _prepare(by_rows, block_cols, ar, ai):
    n = len(by_rows)

    def body(*refs):
        srcs, (ar_ref, ai_ref), dense, (con_ref, rev_ref) = refs[:n], refs[n:n + 2], refs[n + 2:2 * n + 2], refs[2 * n + 2:]
        for src, out, c in zip(srcs, dense, block_cols):
            r, width = src.shape
            groups = width // c
            tiled = jnp.broadcast_to(src[...][None], (groups, r, width)).reshape(groups * r, width)
            own = (lax.broadcasted_iota(jnp.int32, tiled.shape, 0) // r) == (lax.broadcasted_iota(jnp.int32, tiled.shape, 1) // c)
            out[...] = jnp.where(own, tiled, 0.0).astype(BF)
        a_r, a_i = ar_ref[...], ai_ref[...]
        pw = [(jnp.ones_like(a_r), jnp.zeros_like(a_i))]
        for _ in range(SUB):
            pr, pi = pw[-1]
            pw.append((pr * a_r - pi * a_i, pr * a_i + pi * a_r))
        row = _row_iota(GN)
        for ref, reverse in ((con_ref, False), (rev_ref, True)):
            sign = -1.0 if reverse else 1.0
            for j, sh in enumerate((1, 2, 4)):
                keep = (row < SUB - sh) if reverse else (row >= sh)
                ref[2 * j * SUB:(2 * j + 1) * SUB, :] = jnp.where(keep, pw[sh][0], 0.0)
                ref[(2 * j + 1) * SUB:(2 * j + 2) * SUB, :] = jnp.where(keep, sign * pw[sh][1], 0.0)
            p_r, p_i = jnp.zeros((SUB, GN), F32), jnp.zeros((SUB, GN), F32)
            for i in range(SUB):
                k = SUB - i if reverse else i + 1
                p_r = jnp.where(row == i, pw[k][0], p_r)
                p_i = jnp.where(row == i, sign * pw[k][1], p_i)
            ref[6 * SUB:7 * SUB, :] = p_r
            ref[7 * SUB:8 * SUB, :] = p_i

    dense_shapes = [(b.shape[1] // c * b.shape[0], b.shape[1]) for b, c in zip(by_rows, block_cols)]
    outs = _pallas_call(
        body, name="prepare", grid=(1,), in_specs=[_full(b.shape) for b in by_rows] + [_full((1, GN))] * 2,
        out_specs=[_full(s) for s in dense_shapes] + [_full((8 * SUB, GN))] * 2,
        out_shape=[_sds(s, BF) for s in dense_shapes] + [_sds((8 * SUB, GN))] * 2,
        compiler_params=_params(48),
    )(*by_rows, ar, ai)
    return outs[:n], outs[n], outs[n + 1]


def _local_step(x, p, tgt, w, comm):
    rows_of = lambda a: a.reshape(NCHIP * a.shape[1], a.shape[2])
    quarters = lambda a: a.reshape(NCHIP, a.shape[0] // NCHIP, a.shape[1])

    def gathering(names, call):
        carry = comm.gather(names)
        outs = list(call(carry))
        own = len(outs) - len(carry.out_shapes)
        w.update(zip(names, outs[own:]))
        return outs[:own]

    w.update(comm.first())
    w_glu = rows_of(w["w_glu"])
    ar, ai, bbr, bbi = _s5_discretize(w["lam_re"], w["lam_im"], w["log_dt"], w["s5_b_re"], w["s5_b_im"])
    by_row = lambda b: jnp.transpose(b, (1, 0, 2)).reshape(b.shape[1], -1)
    (bbr_d, bbi_d, ccr_d, cci_d, wr_d, wi_d), con, con_rev = _prepare(
        [by_row(b) for b in (bbr, bbi, w["s5_c_re"], w["s5_c_im"], w["w_r"], w["w_i"])], [NS] * 4 + [HD] * 2,
        ar.reshape(1, GN), ai.reshape(1, GN))
    dsk = w["s5_d"].reshape(1, S5W)
    lam = w["lru_lambda"].reshape(1, LW)
    sp = jax.nn.softplus(-lam)
    b_r, b_i = w["b_r"].reshape(1, LW), w["b_i"].reshape(1, LW)
    row = lambda name: w[name].reshape(1, -1)

    h, ua, ub, gp = gathering(["w_a_out", "w_b_out"], lambda carry: _inproj_fwd(
        x, row("g_mix"), w["w_in"], row("b_in"), carry))
    sr, si, y, zg, ya = gathering(["w_o", "w_ffn_gate"], lambda carry: _s5_fwd(
        ua, bbr_d, bbi_d, ccr_d, cci_d, dsk, con, w_glu, row("b_glu"), carry))
    xc, rg, ig, yb, hp = gathering(["w_ffn_up"], lambda carry: _lru_fwd(
        ub, w["conv_w"], row("conv_b"), wr_d, wi_d, b_r, b_i, sp, carry))
    w_b_out, w_o = rows_of(w["w_b_out"]), rows_of(w["w_o"])
    x1, pa, pb, merged = gathering(["w_ffn_down"], lambda carry: _merge_fwd(
        x, ya, yb, gp, w["w_a_out"], w_b_out, w_o, carry))
    x2, h2, gg, uu = gathering(["w_ple_gate", "w_ple"], lambda carry: _ffn_fwd(
        x1, row("g_ffn"), w["w_ffn_gate"], w["w_ffn_up"], w["w_ffn_down"], carry))
    w_pg = rows_of(w["w_ple_gate"])
    dx2, n2, dpre, de0, acc_p = _ple_loss(x2, p, tgt, row("g_ple_gate"), w_pg, row("b_ple_gate"),
                                          w["w_ple"], row("g_ple"), row("g_final"))
    comm.reduce("ple", {"w_ple_gate": quarters(_tn("dw_ple_gate", n2, dpre)),
                        "w_ple": _tn("dw_ple", p, de0, col_chunk=AC)})
    dx1, act, dgg, duu, acc_f = comm.run(lambda carry: _ffn_bwd(
        x1, dx2, gg, uu, row("g_ffn"), w["w_ffn_gate"], w["w_ffn_up"], w["w_ffn_down"], carry))
    comm.reduce("ffn_gate", {"w_ffn_gate": _tn("dw_ffn_gate", dgg, h2)})
    comm.reduce("ffn_up", {"w_ffn_up": comm.run(lambda carry: _tn("dw_ffn_up", duu, h2, carry=carry))[0]})
    comm.reduce("ffn_down", {"w_ffn_down": comm.run(lambda carry: _tn("dw_ffn_down", act, dx2, carry=carry))[0]})
    dya, dyb, dgp, dpa, dpb = comm.run(lambda carry: _merge_bwd(
        dx1, gp, pa, pb, w["w_a_out"], w_b_out, w_o, carry))
    comm.reduce("merge", {"w_o": quarters(_tn("dw_o", merged, dx1)), "w_a_out": _tn("dw_a_out", ya, dpa, col_chunk=AC),
                          "w_b_out": quarters(_tn("dw_b_out", yb, dpb))})
    dua, dq, dy, lr, li, acc_a, acc_s = comm.run(lambda carry: _s5_bwd(
        dya, y, ua, sr, si, bbr_d, bbi_d, ccr_d, cci_d, dsk, con_rev, w_glu, row("b_glu"), carry))
    dub, dpr, dpi, acc_l = comm.run(lambda carry: _lru_bwd(
        dyb, xc, rg, ig, hp, ub, w["conv_w"], wr_d, wi_d, sp, -_sig(-lam), carry))
    gx, dz, acc_g, acc_b = _inproj_bwd(x, dx1, dua, dub, dgp, row("g_mix"), w["w_in"])
    half = (D // 2,)
    comm.reduce("in_lo", {"w_in_lo": comm.run(lambda carry: _tn(
        "dw_in_lo", h, dz, col_chunk=QC, a_block=(0,) + half, carry=carry))[0]})
    comm.reduce("in_hi", {"w_in_hi": comm.run(lambda carry: _tn(
        "dw_in_hi", h, dz, col_chunk=QC, a_block=(1,) + half, carry=carry))[0], "w_glu": quarters(_tn("dw_glu", zg, dq))})
    d_wr, d_wi = comm.run(lambda carry: _tn_blocks("dw_r_i", xc, [dpr, dpi], HD, HD, carry))
    d_bbr, d_bbi = comm.run(lambda carry: _tn_blocks("d_bb", ua, [lr, li], NP, NS, carry))
    d_ccr, d_cci = comm.run(lambda carry: _tn_blocks("d_cc", dy, [sr, si], NP, NS, carry))
    comm.drain()
    sums = {"ple": acc_p, "ffn": acc_f, "mix": acc_g, "b_in": acc_b, "lru": acc_l, "s5": acc_s, "s5_a": acc_a}
    blocks = {"bb_re": d_bbr, "bb_im": d_bbi,
              "cc_re": d_ccr, "cc_im": d_cci,
              "w_r": d_wr, "w_i": d_wi}
    return gx, sums, blocks


def _replicated_grads(w, sums, blocks):
    grouped = lambda e, groups: jnp.transpose(e.reshape(e.shape[0], groups, -1), (1, 0, 2))
    d_ar, d_ai = sums["s5_a"][0].reshape(NG, NS), sums["s5_a"][1].reshape(NG, NS)
    d_bbr, d_bbi = grouped(blocks["bb_re"], NG), grouped(blocks["bb_im"], NG)
    _, vjp = jax.vjp(_s5_discretize, w["lam_re"], w["lam_im"], w["log_dt"], w["s5_b_re"], w["s5_b_im"])
    g = dict(zip(("lam_re", "lam_im", "log_dt", "s5_b_re", "s5_b_im"), vjp((d_ar, d_ai, d_bbr, d_bbi))))
    g["s5_c_re"] = grouped(blocks["cc_re"], NG)
    g["s5_c_im"] = -grouped(blocks["cc_im"], NG)
    g["w_r"], g["w_i"] = grouped(blocks["w_r"], NH), grouped(blocks["w_i"], NH)
    g["s5_d"] = sums["s5"][0].reshape(NG, NP)
    g["b_r"] = sums["lru"][1].reshape(NH, HD)
    g["b_i"] = sums["lru"][2].reshape(NH, HD)
    return g


ACC_ROWS = {"g_mix": ("mix", 0), "b_in": ("b_in", 0), "g_ffn": ("ffn", 0), "g_ple_gate": ("ple", 0),
            "b_ple_gate": ("ple", 1), "g_ple": ("ple", 2), "g_final": ("ple", 3), "b_glu": ("s5", 1),
            "lru_lambda": ("lru", 0), "conv_b": ("lru", 3)}
LOSS_ROW = ("ple", 4)
CONV_W_ROWS = ("lru", 4)


SHARDED = [("w_in", (D, QC)), ("w_glu", (S5W // NCHIP, S5W)), ("w_a_out", (S5W, AC)), ("w_b_out", (LW // NCHIP, D)),
           ("w_o", (D // NCHIP, D)), ("w_ffn_gate", (FC, D)), ("w_ffn_up", (FC, D)), ("w_ffn_down", (FC, D)),
           ("w_ple_gate", (D // NCHIP, D)), ("w_ple", (PLE, AC))]
NSH = len(SHARDED)
TRANSPOSED = ("w_ffn_gate", "w_ffn_up", "s5_b_re", "s5_b_im")
CONV_SHARD = (4, LW // NCHIP)


def _mesh_pos():
    return lax.axis_index("x"), lax.axis_index("y"), lax.axis_index("c")


def _other_chips(x, y):
    return [(1 - x, y), (x, 1 - y), (1 - x, 1 - y)]


def _half_rows(c, rows, align):
    return pl.ds(pl.multiple_of(c * (rows // 2), align), rows // 2)


def _run_now(name, carry):
    c_in, c_out = len(carry.operands), len(carry.out_shapes)

    def body(*refs):
        ins, outs, sems = refs[:c_in], refs[c_in:c_in + c_out], refs[c_in + c_out:]
        carry.start(ins, outs, sems)
        carry.finish(ins, outs, sems)

    return pl.pallas_call(body, name=name, in_specs=[ANY] * c_in, out_specs=[ANY] * c_out,
                          out_shape=list(carry.out_shapes), scratch_shapes=list(carry.sems),
                          input_output_aliases=dict(carry.aliases))(*_in_hbm(carry.operands))


def _gather_group(shards, split):
    n = len(shards)

    def copies(srcs, outs, sems):
        send_sems, recv_sems = sems
        x, y, c = _mesh_pos()
        k0 = 2 * x + y
        sib = (x, y, 1 - c)
        chips = _other_chips(x, y)

        def remote(src, dst, j, i, to):
            return pltpu.make_async_remote_copy(src_ref=src, dst_ref=dst, send_sem=send_sems.at[j, i],
                                                recv_sem=recv_sems.at[j, i], device_id=to, device_id_type=MESH)

        def rows(ref, i, core, *lead):
            if not split[i]:
                return ref.at[lead] if lead else ref
            return ref.at[(*lead, _half_rows(core, shards[i].shape[0], 16))]

        own = [remote(s, o.at[k0], 6, i, sib) for i, (s, o) in enumerate(zip(srcs, outs))]
        ici, landed, fwd, fwd_landed = [], [], [], []
        for j, chip in enumerate(chips):
            kj = 2 * chip[0] + chip[1]
            pairs = list(enumerate(zip(srcs, outs)))
            ici.append([remote(rows(s, i, c), rows(o, i, c, k0), j, i, (*chip, c)) for i, (s, o) in pairs])
            landed.append([remote(rows(s, i, c), rows(o, i, c, kj), j, i, (*chip, c)) for i, (s, o) in pairs])
            fwd.append([remote(rows(o, i, c, kj), rows(o, i, c, kj), 3 + j, i, sib) for i, (s, o) in pairs if split[i]])
            fwd_landed.append([remote(rows(o, i, 1 - c, kj), rows(o, i, 1 - c, kj), 3 + j, i, sib)
                               for i, (s, o) in pairs if split[i]])
        return own, ici, landed, fwd, fwd_landed

    def start(srcs, outs, sems):
        own, ici, _, _, _ = copies(srcs, outs, sems)
        for cp in own + [cp for per_chip in ici for cp in per_chip]:
            cp.start()

    def finish(srcs, outs, sems):
        own, ici, landed, fwd, fwd_landed = copies(srcs, outs, sems)
        passed = [i for i in range(n) if split[i]]
        for j in range(3):
            for i, cp in enumerate(landed[j]):
                cp.wait_recv()
                if split[i]:
                    fwd[j][passed.index(i)].start()
        for j in range(3):
            for cp in fwd_landed[j]:
                cp.wait_recv()
        for cp in own:
            cp.wait_recv()
        for cp in own + [cp for per_chip in ici + fwd for cp in per_chip]:
            cp.wait_send()

    return _Carried(shards, [_sds((NCHIP,) + s.shape, s.dtype) for s in shards],
                    [pltpu.SemaphoreType.DMA((7, n)), pltpu.SemaphoreType.DMA((7, n))], start, finish)


def _each_copy(copies, carried, out_shapes, sems, aliases=None):
    def start(ins, outs, sem_refs):
        for cp in copies(ins, outs, sem_refs):
            cp.start()

    def finish(ins, outs, sem_refs):
        for cp in copies(ins, outs, sem_refs):
            cp.wait()

    return _Carried(carried, out_shapes, sems, start, finish, aliases)


def _swap_group(grads):
    n = len(grads)

    def copies(srcs, outs, sems):
        send_sems, recv_sems = sems
        x, y, c = _mesh_pos()
        return [pltpu.make_async_remote_copy(src_ref=s.at[:, _half_rows(1 - c, s.shape[1], 8)], dst_ref=o,
                                             send_sem=send_sems.at[i], recv_sem=recv_sems.at[i], device_id=(x, y, 1 - c),
                                             device_id_type=MESH) for i, (s, o) in enumerate(zip(srcs, outs))]

    return _each_copy(copies, grads, [pltpu.HBM((NCHIP, g.shape[1] // 2, g.shape[2]), F32) for g in grads],
                      [pltpu.SemaphoreType.DMA((n,)), pltpu.SemaphoreType.DMA((n,))])


def _add_sibling_group(tag, kc_idx, grads, gots):
    n = len(grads)

    def body(kc_ref, *refs):
        for g, rx, p, pb in zip(refs[:n], refs[n:2 * n], refs[2 * n:3 * n], refs[3 * n:]):
            s = g[...] + rx[...]
            pb[...] = s.astype(BF)

            @pl.when(pl.program_id(0) == kc_ref[0])
            def _():
                p[...] = s

    halves = [pl.BlockSpec((None,) + rx.shape[1:], lambda k, kc_ref: (k, 0, 0)) for rx in gots]
    mine = [pl.BlockSpec((None,) + rx.shape[1:], lambda k, kc_ref: (k, kc_ref[1], 0)) for rx in gots]
    own = [pl.BlockSpec(rx.shape[1:], lambda k, kc_ref: (0, 0)) for rx in gots]
    outs = _pallas_call(
        body, name="add_sibling_" + tag,
        grid_spec=pltpu.PrefetchScalarGridSpec(num_scalar_prefetch=1, grid=(NCHIP,), in_specs=mine + halves,
                                               out_specs=own + halves),
        out_shape=[pltpu.HBM(rx.shape[1:], F32) for rx in gots] + [pltpu.HBM(rx.shape, BF) for rx in gots],
        compiler_params=_params(48),
    )(kc_idx, *_in_hbm(list(grads) + list(gots)))
    return outs[:n], outs[n:]


def _exchange_group(parts):
    n = len(parts)

    def copies(srcs, outs, sems):
        send_sems, recv_sems = sems
        x, y, c = _mesh_pos()
        return [pltpu.make_async_remote_copy(
            src_ref=s.at[2 * chip[0] + chip[1]], dst_ref=o.at[j], send_sem=send_sems.at[j, i],
            recv_sem=recv_sems.at[j, i], device_id=(*chip, c), device_id_type=MESH)
            for j, chip in enumerate(_other_chips(x, y)) for i, (s, o) in enumerate(zip(srcs, outs))]

    return _each_copy(copies, parts, [pltpu.HBM((3,) + p.shape[1:], BF) for p in parts],
                      [pltpu.SemaphoreType.DMA((3, n)), pltpu.SemaphoreType.DMA((3, n))])


def _add_chips_group(tag, kc_idx, parts, arrived):
    n = len(parts)

    def body(kc_ref, *refs):
        for p, rx, t in zip(refs[:n], refs[n:2 * n], refs[2 * n:]):
            t[...] = ((p[...] + rx[0].astype(F32)) + rx[1].astype(F32)) + rx[2].astype(F32)

    outs = _pallas_call(
        body, name="add_chips_" + tag,
        grid_spec=pltpu.PrefetchScalarGridSpec(
            num_scalar_prefetch=1, grid=(1,),
            in_specs=([pl.BlockSpec(rx.shape[1:], lambda i, kc_ref: (0, 0)) for rx in arrived]
                      + [pl.BlockSpec(rx.shape, lambda i, kc_ref: (0, 0, 0)) for rx in arrived]),
            out_specs=[pl.BlockSpec((None,) + rx.shape[1:], lambda i, kc_ref: (kc_ref[1], 0, 0)) for rx in arrived]),
        out_shape=[pltpu.HBM((2,) + rx.shape[1:], F32) for rx in arrived],
        compiler_params=_params(48),
    )(kc_idx, *_in_hbm(list(parts) + list(arrived)))
    return list(outs)


def _join_group(halves):
    n = len(halves)

    def copies(bufs, sems):
        send_sems, recv_sems = sems
        x, y, c = _mesh_pos()
        sib = (x, y, 1 - c)
        sends = [pltpu.make_async_remote_copy(src_ref=b.at[c], dst_ref=b.at[c], send_sem=send_sems.at[i],
                                              recv_sem=recv_sems.at[i], device_id=sib, device_id_type=MESH)
                 for i, b in enumerate(bufs)]
        landed = [pltpu.make_async_remote_copy(src_ref=b.at[c], dst_ref=b.at[1 - c], send_sem=send_sems.at[i],
                                               recv_sem=recv_sems.at[i], device_id=sib, device_id_type=MESH)
                  for i, b in enumerate(bufs)]
        return sends, landed

    def start(_, bufs, sems):
        for cp in copies(bufs, sems)[0]:
            cp.start()

    def finish(_, bufs, sems):
        sends, landed = copies(bufs, sems)
        for cp in landed:
            cp.wait_recv()
        for cp in sends:
            cp.wait_send()

    return _Carried(halves, [pltpu.HBM(h.shape, F32) for h in halves],
                    [pltpu.SemaphoreType.DMA((n,)), pltpu.SemaphoreType.DMA((n,))], start, finish,
                    {i: i for i in range(n)})


def _combine(carries):
    operands, out_shapes, sems, aliases, spans = [], [], [], {}, []
    for c in carries:
        aliases.update({len(operands) + i: len(out_shapes) + o for i, o in c.aliases.items()})
        spans.append((len(operands), len(out_shapes), len(sems)))
        operands += list(c.operands)
        out_shapes += list(c.out_shapes)
        sems += list(c.sems)

    def each(phase):
        def run(ins, outs, sem_refs):
            for c, (a, b, s) in zip(carries, spans):
                getattr(c, phase)(ins[a:a + len(c.operands)], outs[b:b + len(c.out_shapes)], sem_refs[s:s + len(c.sems)])
        return run

    return _Carried(operands, out_shapes, sems, each("start"), each("finish"), aliases)


def _allreduce_small(arrays, wire):
    n = len(arrays)
    halves = [(a.shape[0], a.shape[1] // 2) for a in arrays]

    def body(*refs):
        srcs, outs = refs[:n], refs[n:2 * n]
        mine_bufs, sib_bufs, chip_bufs, total_bufs = (refs[k * n:(k + 1) * n] for k in range(2, 6))
        send_sems, recv_sems, local_sems = refs[6 * n:]
        x, y, c = _mesh_pos()
        k0 = 2 * x + y
        sib = (x, y, 1 - c)

        def remote(src, dst, j, i, to):
            return pltpu.make_async_remote_copy(src_ref=src, dst_ref=dst, send_sem=send_sems.at[j, i],
                                                recv_sem=recv_sems.at[j, i], device_id=to, device_id_type=MESH)

        def cols(ref, i, core):
            return ref.at[:, pl.ds(pl.multiple_of(core * halves[i][1], LANE), halves[i][1])]

        swaps = [remote(cols(s, i, 1 - c), b, 0, i, sib) for i, (s, b) in enumerate(zip(srcs, sib_bufs))]
        own = [pltpu.make_async_copy(cols(s, i, c), m, local_sems.at[i]) for i, (s, m) in enumerate(zip(srcs, mine_bufs))]
        for cp in swaps + own:
            cp.start()
        for cp in swaps + own:
            cp.wait()
        for m, b, buf in zip(mine_bufs, sib_bufs, chip_bufs):
            buf[k0] = (m[...] + b[...]).astype(buf.dtype)
        chips = _other_chips(x, y)
        sends = [remote(buf.at[k0], buf.at[k0], 1 + j, i, (*chip, c))
                 for j, chip in enumerate(chips) for i, buf in enumerate(chip_bufs)]
        for cp in sends:
            cp.start()
        for j, chip in enumerate(chips):
            for i, buf in enumerate(chip_bufs):
                remote(buf.at[k0], buf.at[2 * chip[0] + chip[1]], 1 + j, i, (*chip, c)).wait_recv()
        for cp in sends:
            cp.wait_send()
        for t, buf in zip(total_bufs, chip_bufs):
            t[...] = ((buf[0].astype(F32) + buf[1].astype(F32)) + buf[2].astype(F32)) + buf[3].astype(F32)
        joins = [remote(t, cols(o, i, c), 4, i, sib) for i, (t, o) in enumerate(zip(total_bufs, outs))]
        keep = [pltpu.make_async_copy(t, cols(o, i, c), local_sems.at[i]) for i, (t, o) in enumerate(zip(total_bufs, outs))]
        for cp in joins + keep:
            cp.start()
        for i, (t, o) in enumerate(zip(total_bufs, outs)):
            remote(t, cols(o, i, 1 - c), 4, i, sib).wait_recv()
        for cp in joins:
            cp.wait_send()
        for cp in keep:
            cp.wait()

    specs = [_full(a.shape) for a in arrays]
    return _pallas_call(
        body, name="allreduce_small", grid=(1,), in_specs=specs, out_specs=specs,
        out_shape=[_sds(a.shape) for a in arrays],
        scratch_shapes=([pltpu.VMEM(h, F32) for h in halves] + [pltpu.VMEM(h, F32) for h in halves]
                        + [pltpu.VMEM((NCHIP,) + h, dt) for h, dt in zip(halves, wire)] + [pltpu.VMEM(h, F32) for h in halves]
                        + [pltpu.SemaphoreType.DMA((5, n)), pltpu.SemaphoreType.DMA((5, n)), pltpu.SemaphoreType.DMA((n,))]),
        compiler_params=_params(32),
    )(*arrays)


def _adamw_terms(w, g, m, v):
    m = ADAM_B1 * m + (1.0 - ADAM_B1) * g
    v = ADAM_B2 * v + (1.0 - ADAM_B2) * jnp.square(g)
    m_hat = m / (1.0 - ADAM_B1 ** ADAM_STEP)
    v_hat = v / (1.0 - ADAM_B2 ** ADAM_STEP)
    return -ADAM_LR * (m_hat / (jnp.sqrt(v_hat) + ADAM_EPS) + ADAM_WD * w), m, v


ADAM_STEPS = 4


def _adamw_group(tag, ws, gs, ms, vs):
    n = len(ws)

    def body(*refs):
        ins, outs = refs[:4 * n], refs[4 * n:]
        for i in range(n):
            w, g, m, v = (ins[k * n + i][...] for k in range(4))
            outs[i][...] = g
            outs[n + i][...], outs[2 * n + i][...], outs[3 * n + i][...] = _adamw_terms(w, g, m, v)

    specs = [pl.BlockSpec((w.shape[0] // ADAM_STEPS, w.shape[1]), lambda i: (i, 0)) for w in ws]
    outs = _pallas_call(
        body, name="adamw_" + tag, grid=(ADAM_STEPS,), in_specs=specs * 4, out_specs=specs * 4,
        out_shape=[_sds(w.shape) for w in ws] * 4, compiler_params=_params(48),
    )(*_in_hbm(list(ws) + list(gs) + list(ms) + list(vs)))
    return outs[:n], outs[n:2 * n], outs[2 * n:3 * n], outs[3 * n:]


def _adamw_replicated(sums, row_of, direct):
    ns, nr, nd = len(sums), len(row_of), len(direct)

    def body(*refs):
        sum_refs = refs[:ns]
        ins = refs[ns:ns + 3 * nr + 4 * nd]
        outs = refs[ns + 3 * nr + 4 * nd:]
        for i, (_, _, _, si, row) in enumerate(row_of):
            w_ref, m_ref, v_ref = ins[3 * i:3 * i + 3]
            g = sum_refs[si][row:row + 1, :]
            outs[4 * i][...] = g
            outs[4 * i + 1][...], outs[4 * i + 2][...], outs[4 * i + 3][...] = _adamw_terms(w_ref[...], g, m_ref[...], v_ref[...])
        for i in range(nd):
            w_ref, m_ref, v_ref, g_ref = ins[3 * nr + 4 * i:3 * nr + 4 * i + 4]
            o = outs[4 * (nr + i):4 * (nr + i) + 4]
            g = g_ref[...]
            o[0][...] = g
            o[1][...], o[2][...], o[3][...] = _adamw_terms(w_ref[...], g, m_ref[...], v_ref[...])

    operands = list(sums)
    shapes = []
    for w, m, v, _, _ in row_of:
        operands += [w, m, v]
        shapes += [w.shape] * 4
    for w, m, v, g in direct:
        operands += [w, m, v, g]
        shapes += [w.shape] * 4
    flat = _pallas_call(
        body, name="adamw_replicated", grid=(1,), in_specs=[_full(a.shape) for a in operands],
        out_specs=[_full(s) for s in shapes], out_shape=[_sds(s) for s in shapes],
        compiler_params=_params(56),
    )(*operands)
    return [flat[4 * i:4 * i + 4] for i in range(nr + nd)]


class _Exchanges:
    def __init__(self, shards, conv_w, chip, core, apply):
        self.shards, self.conv_w, self.apply = shards, conv_w, apply
        self.active, self.calls = [], 0
        self.core_idx = jnp.reshape(core, (1,)).astype(jnp.int32)
        self.chip_core_idx = jnp.stack([chip, core]).astype(jnp.int32)

    def first(self):
        names = ["w_in", "w_glu"]
        got = _run_now("gather_first", _gather_group([self.shards[n] for n in names] + [self.conv_w],
                                                     [True, True, False]))
        out = dict(zip(names, got))
        out["conv_w"] = jnp.transpose(got[2], (1, 0, 2)).reshape(4, LW)
        return out

    def gather(self, names):
        return _gather_group([self.shards[n] for n in names], [True] * len(names))

    def reduce(self, tag, grads):
        self.active.append({"tag": tag, "names": list(grads), "stage": 0, "grads": list(grads.values())})

    def run(self, call):
        groups = self.active
        carries = [self._exchange_of(g) for g in groups]
        carry = _combine(carries)
        outs = list(call(carry))
        own = len(outs) - len(carry.out_shapes)
        landed = outs[own:]
        for g, c in zip(groups, carries):
            self._sum_after(g, landed[:len(c.out_shapes)])
            landed = landed[len(c.out_shapes):]
        self.active = [g for g in groups if g["stage"] < 3]
        return outs[:own]

    def _exchange_of(self, g):
        if g["stage"] == 0:
            return _swap_group(g["grads"])
        if g["stage"] == 1:
            return _exchange_group(g["bf16"])
        return _join_group(g["halves"])

    def _sum_after(self, g, landed):
        if g["stage"] == 0:
            g["f32"], g["bf16"] = _add_sibling_group(g["tag"], self.chip_core_idx, g["grads"], landed)
        elif g["stage"] == 1:
            g["halves"] = _add_chips_group(g["tag"], self.chip_core_idx, g["f32"], landed)
        else:
            self.apply(g["tag"], g["names"], [t.reshape(2 * t.shape[1], t.shape[2]) for t in landed])
        g["stage"] += 1

    def drain(self):
        while self.active:
            self.calls += 1
            self.run(lambda carry: _run_now("reduce_%d" % self.calls, carry))


INPUT_NAMES = (["x", "p"] + [n for n in
               ["g_mix", "w_in", "b_in", "lam_re", "lam_im", "log_dt", "s5_b_re", "s5_b_im", "s5_c_re", "s5_c_im", "s5_d",
                "w_glu", "b_glu", "conv_w", "conv_b", "w_r", "b_r", "w_i", "b_i", "lru_lambda", "w_a_out", "w_b_out", "w_o",
                "g_ffn", "w_ffn_gate", "w_ffn_up", "w_ffn_down", "g_ple_gate", "w_ple_gate", "b_ple_gate", "w_ple", "g_ple",
                "g_final"]])
WEIGHT_NAMES = INPUT_NAMES[2:]


def kernel(*args):
    names = INPUT_NAMES + ["loss_target"] + ["m_" + n for n in WEIGHT_NAMES] + ["v_" + n for n in WEIGHT_NAMES]
    assert len(args) == len(names)
    given = dict(zip(names, args))

    def view(name):
        a = given[name]
        return jnp.swapaxes(a, -1, -2) if name.endswith(TRANSPOSED) else a

    def unview(name, a):
        return jnp.swapaxes(a, -1, -2) if name in TRANSPOSED else a

    def local(name):
        return view(name) if name.endswith("g_final") else view(name)[0]

    xi, yi, ci = _mesh_pos()
    k0 = 2 * xi + yi
    x, p, tgt = given["x"][0], given["p"][0, 0], given["loss_target"][0]

    results = {}

    row_halves = {}

    def apply(tag, names, totals):
        totals = dict(zip(names, totals))
        row_halves.update({n: totals.pop(n) for n in names if n in ("w_in_lo", "w_in_hi")})
        if len(row_halves) == 2:
            totals["w_in"] = jnp.concatenate([row_halves.pop("w_in_lo"), row_halves.pop("w_in_hi")])
        names = list(totals)
        if not names:
            return
        new = _adamw_group(tag, [local(n) for n in names], list(totals.values()), [local("m_" + n) for n in names],
                           [local("v_" + n) for n in names])
        for kind, arrays in zip(("grad", "delta", "new_m", "new_v"), new):
            for n, arr in zip(names, arrays):
                results[kind, n] = unview(n, arr[None])

    comm = _Exchanges({n: local(n).astype(BF) for n, _ in SHARDED}, local("conv_w"), k0, ci, apply)
    w = {n: local(n) for n in WEIGHT_NAMES if n != "conv_w" and n not in dict(SHARDED)}
    gx, sums, blocks = _local_step(x, p, tgt, w, comm)

    sum_names, block_names = list(sums), list(blocks)
    red = _allreduce_small([sums[n] for n in sum_names] + [blocks[n] for n in block_names],
                           [F32] * len(sum_names) + [BF] * len(block_names))
    sums = dict(zip(sum_names, red[:len(sum_names)]))
    blocks = dict(zip(block_names, red[len(sum_names):]))
    loss = jnp.sum(sums[LOSS_ROW[0]][LOSS_ROW[1]])
    direct_g = _replicated_grads(w, sums, blocks)
    conv_rows = sums[CONV_W_ROWS[0]][CONV_W_ROWS[1]:CONV_W_ROWS[1] + 4]
    direct_g["conv_w"] = lax.dynamic_slice(conv_rows, (0, k0 * CONV_SHARD[1]), CONV_SHARD)
    as_row = lambda a: a.reshape(1, -1)
    row_names = list(ACC_ROWS)
    row_of = [(as_row(given[n]), as_row(given["m_" + n]), as_row(given["v_" + n]),
               sum_names.index(ACC_ROWS[n][0]), ACC_ROWS[n][1]) for n in row_names]
    direct_names = list(direct_g)
    direct = [(view(n), view("m_" + n), view("v_" + n), direct_g[n].reshape(view(n).shape)) for n in direct_names]
    done = _adamw_replicated([sums[n] for n in sum_names], row_of, direct)
    for n, four in zip(row_names + direct_names, done):
        for kind, arr in zip(("grad", "delta", "new_m", "new_v"), four):
            results[kind, n] = unview(n, arr).reshape(given[n].shape)

    out = [loss, gx[None]]
    for kind in ("grad", "delta", "new_m", "new_v"):
        out += [results[kind, n] for n in WEIGHT_NAMES]
    return tuple(out)
```

```python
import functools
import math

import jax
import jax.numpy as jnp
from jax import lax
from jax.experimental import pallas as pl
from jax.experimental.pallas import tpu as pltpu

F32 = jnp.float32
BF = jnp.bfloat16

D = 1024
S5W = 512
NG, NS, NP = 32, 64, 16
GN = NG * NS
LW = 1024
NH, HD = 16, 64
LRU_C = 8.0
FH = 2816
NCHIP = 4
FC = FH // NCHIP
PLE = 256
INC = S5W + LW + 2 * D
EPS = 1e-6
ADAM_LR, ADAM_B1, ADAM_B2, ADAM_EPS, ADAM_WD, ADAM_STEP = 0.001, 0.9, 0.999, 1e-08, 0.01, 10

TM = 256
TK = 1024
LC = 512
SUB = 8
VMEM_MB = 1024 * 1024
MESH = pl.DeviceIdType.MESH
ANY = pl.BlockSpec(memory_space=pl.ANY)


def _mm(a, b):
    return jnp.dot(a.astype(BF), b.astype(BF), preferred_element_type=F32)


def _mm_nt(a, b):
    return lax.dot_general(a.astype(BF), b.astype(BF), (((1,), (1,)), ((), ())), preferred_element_type=F32)


def _mm_tn(a, b):
    return lax.dot_general(a.astype(BF), b.astype(BF), (((0,), (0,)), ((), ())), preferred_element_type=F32)


def _rms(x):
    r = lax.rsqrt(jnp.mean(x * x, axis=-1, keepdims=True) + EPS)
    return x * r, r


def _rms_bwd(dy, xh, r, g):
    dxh = dy * g
    return r * (dxh - xh * jnp.mean(dxh * xh, axis=-1, keepdims=True))


def _colsum(x):
    return jnp.sum(x, axis=0, keepdims=True)


def _sig(x):
    return jax.nn.sigmoid(x)


def _gelu_grad(x):
    c = math.sqrt(2.0 / math.pi)
    t = jnp.tanh(c * (x + 0.044715 * x * x * x))
    return 0.5 * (1.0 + t) + 0.5 * x * (1.0 - t * t) * c * (1.0 + 3.0 * 0.044715 * x * x)


def _neg_expm1(x):
    series = -x * (1.0 + x * (0.5 + x * (1.0 / 6.0 + x * (1.0 / 24.0))))
    return jnp.where(x > -0.03, series, 1.0 - jnp.exp(x))


def _tok(width):
    return pl.BlockSpec((TM, width), lambda i: (i, 0))


def _tok_rev(width, nt):
    return pl.BlockSpec((TM, width), lambda i: (nt - 1 - i, 0))


def _full(shape):
    return pl.BlockSpec(shape, lambda i: (0,) * len(shape))


def _params(vmem_mb, **kw):
    return pltpu.CompilerParams(dimension_semantics=("arbitrary",), vmem_limit_bytes=vmem_mb * VMEM_MB, **kw)


def _sds(shape, dtype=F32):
    return jax.ShapeDtypeStruct(shape, dtype)


class _Carried:
    def __init__(self, operands, out_shapes, sems, start, finish, aliases=None):
        self.operands, self.out_shapes, self.sems = list(operands), list(out_shapes), list(sems)
        self.start, self.finish, self.aliases = start, finish, dict(aliases or {})


def _in_hbm(arrays):
    return [pltpu.with_memory_space_constraint(a, pltpu.HBM) for a in arrays]


def _pallas_call(body, carry=None, **kw):
    if carry is None:
        return pl.pallas_call(body, **kw)

    def at_step(corner):
        hit = [pl.program_id(d) == (size - 1 if corner else 0) for d, size in enumerate(kw["grid"])]
        return functools.reduce(jnp.logical_and, hit)

    name, grid, compiler_params = kw["name"], kw["grid"], kw["compiler_params"]
    in_specs, out_specs, out_shape = list(kw["in_specs"]), list(kw["out_specs"]), list(kw["out_shape"])
    scratch_shapes = list(kw.get("scratch_shapes", ()))
    n_in, n_out, n_scr = len(in_specs), len(out_specs), len(scratch_shapes)
    c_in, c_out = len(carry.operands), len(carry.out_shapes)

    def full_body(*refs):
        ins, refs = refs[:n_in], refs[n_in:]
        c_ins, refs = refs[:c_in], refs[c_in:]
        outs, refs = refs[:n_out], refs[n_out:]
        c_outs, refs = refs[:c_out], refs[c_out:]
        scratch, c_sems = refs[:n_scr], refs[n_scr:]

        @pl.when(at_step(0))
        def _():
            carry.start(c_ins, c_outs, c_sems)

        body(*ins, *outs, *scratch)

        @pl.when(at_step(1))
        def _():
            carry.finish(c_ins, c_outs, c_sems)

    call = pl.pallas_call(
        full_body, name=name, grid=grid, in_specs=in_specs + [ANY] * c_in, out_specs=out_specs + [ANY] * c_out,
        out_shape=out_shape + list(carry.out_shapes), scratch_shapes=scratch_shapes + list(carry.sems),
        input_output_aliases={n_in + i: n_out + o for i, o in carry.aliases.items()},
        compiler_params=compiler_params)
    return lambda *operands: call(*operands, *_in_hbm(carry.operands))


def _resident(pairs, sems):
    first = pl.program_id(0) == 0
    copies = [pltpu.make_async_copy(src, dst, sems.at[j]) for j, (src, dst) in enumerate(pairs)]

    @pl.when(first)
    def _():
        for cp in copies:
            cp.start()

    def wait(j):
        @pl.when(first)
        def _():
            copies[j].wait()

    return wait


def _resident_now(pairs, sems):
    @pl.when(pl.program_id(0) == 0)
    def _():
        copies = [pltpu.make_async_copy(src, dst, sems.at[j]) for j, (src, dst) in enumerate(pairs)]
        for cp in copies:
            cp.start()
        for cp in copies:
            cp.wait()


def _row_iota(width):
    return lax.broadcasted_iota(jnp.int32, (SUB, width), 0)


def _bcast_row(x, row):
    return jnp.broadcast_to(x[row:row + 1, :], x.shape)


def _slab(k):
    return pl.ds(pl.multiple_of(k * SUB, SUB), SUB)


QC = INC // NCHIP
Z_PARTS = ((0, S5W), (S5W, S5W + LW), (S5W + LW, INC))


def _inproj_fwd(x, g_mix, w_in, b_in, carry=None):
    L = x.shape[0]

    def body(x_ref, g_ref, w_hbm, b_ref, h_ref, ua_ref, ub_ref, gp_ref, w_vm, w_sems):
        _resident_now([(w_hbm.at[k], w_vm.at[k]) for k in range(NCHIP)], w_sems)
        xh, _ = _rms(x_ref[...])
        h = (xh * g_ref[...]).astype(BF)
        h_ref[...] = h
        for k in range(NCHIP):
            lo, hi = k * QC, (k + 1) * QC
            z = jnp.dot(h, w_vm[k], preferred_element_type=F32) + b_ref[:, lo:hi]
            for ref, (a, b) in zip((ua_ref, ub_ref, gp_ref), Z_PARTS):
                s, e = max(lo, a), min(hi, b)
                if s < e:
                    ref[:, s - a:e - a] = z[:, s - lo:e - lo]

    return _pallas_call(
        body, carry, name="inproj_fwd", grid=(L // TM,),
        in_specs=[_tok(D), _full((1, D)), ANY, _full((1, INC))],
        out_specs=[_tok(D), _tok(S5W), _tok(LW), _tok(2 * D)],
        out_shape=[_sds((L, D), BF), _sds((L, S5W)), _sds((L, LW)), _sds((L, 2 * D))],
        scratch_shapes=[pltpu.VMEM((NCHIP, D, QC), BF), pltpu.SemaphoreType.DMA((NCHIP,))],
        compiler_params=_params(40),
    )(x, g_mix, w_in, b_in)


def _inproj_bwd(x, dx1, dua, dub, dgp, g_mix, w_in, carry=None):
    L = x.shape[0]

    def body(x_ref, dx1_ref, dua_ref, dub_ref, dgp_ref, g_ref, w_hbm, gx_ref, dz_ref, dg_ref, db_ref, w_vm, w_sems):
        _resident_now([(w_hbm.at[k], w_vm.at[k]) for k in range(NCHIP)], w_sems)

        @pl.when(pl.program_id(0) == 0)
        def _():
            dg_ref[...] = jnp.zeros_like(dg_ref)
            db_ref[...] = jnp.zeros_like(db_ref)

        for src, (a, b) in zip((dua_ref, dub_ref, dgp_ref), Z_PARTS):
            d = src[...]
            dz_ref[:, a:b] = d.astype(BF)
            db_ref[0:1, a:b] += _colsum(d)
        dh = jnp.zeros((TM, D), F32)
        for k in range(NCHIP):
            dh = dh + lax.dot_general(dz_ref[:, k * QC:(k + 1) * QC], w_vm[k], (((1,), (1,)), ((), ())),
                                      preferred_element_type=F32)
        xh, r = _rms(x_ref[...])
        dg_ref[0:1, :] += _colsum(dh * xh)
        gx_ref[...] = dx1_ref[...] + _rms_bwd(dh, xh, r, g_ref[...])

    return _pallas_call(
        body, carry, name="inproj_bwd", grid=(L // TM,),
        in_specs=[_tok(D), _tok(D), _tok(S5W), _tok(LW), _tok(2 * D), _full((1, D)), ANY],
        out_specs=[_tok(D), _tok(INC), _full((SUB, D)), _full((SUB, INC))],
        out_shape=[_sds((L, D)), _sds((L, INC), BF), _sds((SUB, D)), _sds((SUB, INC))],
        scratch_shapes=[pltpu.VMEM((NCHIP, D, QC), BF), pltpu.SemaphoreType.DMA((NCHIP,))],
        compiler_params=_params(40),
    )(x, dx1, dua, dub, dgp, g_mix, w_in)


def _cscan(xr_ref, xi_ref, con_ref, cr_ref, ci_ref, reverse):
    n_slab = xr_ref.shape[0] // SUB
    width = xr_ref.shape[1]
    for lc in range(width // LC):
        cols = slice(lc * LC, (lc + 1) * LC)
        con = [con_ref[SUB * j:SUB * (j + 1), cols] for j in range(8)]

        def step(k, carry, cols=cols, con=con):
            cr, ci = carry
            rows = _slab(n_slab - 1 - k if reverse else k)
            xr, xi = xr_ref[rows, cols], xi_ref[rows, cols]
            for j, sh in enumerate((1, 2, 4)):
                mr, mi = con[2 * j], con[2 * j + 1]
                pr = pltpu.roll(xr, SUB - sh if reverse else sh, 0)
                pi = pltpu.roll(xi, SUB - sh if reverse else sh, 0)
                xr, xi = xr + mr * pr - mi * pi, xi + mr * pi + mi * pr
            xr, xi = xr + con[6] * cr - con[7] * ci, xi + con[6] * ci + con[7] * cr
            xr_ref[rows, cols] = xr
            xi_ref[rows, cols] = xi
            row = 0 if reverse else SUB - 1
            return _bcast_row(xr, row), _bcast_row(xi, row)

        cr, ci = lax.fori_loop(0, n_slab, step, (cr_ref[:, cols], ci_ref[:, cols]))
        cr_ref[:, cols] = cr
        ci_ref[:, cols] = ci


def _s5_fwd(ua, bbr, bbi, ccr, cci, dsk, con, w_glu, b_glu, carry=None):
    L = ua.shape[0]

    def body(ua_ref, bbr_hbm, bbi_hbm, ccr_hbm, cci_hbm, dsk_ref, con_ref, wg_ref, bg_ref,
             sr_ref, si_ref, y_ref, zg_ref, ya_ref, bbr_vm, bbi_vm, ccr_vm, cci_vm, cr_ref, ci_ref, w_sems):
        landed = _resident([(bbr_hbm, bbr_vm), (bbi_hbm, bbi_vm), (ccr_hbm, ccr_vm), (cci_hbm, cci_vm)], w_sems)

        @pl.when(pl.program_id(0) == 0)
        def _():
            cr_ref[...] = jnp.zeros_like(cr_ref)
            ci_ref[...] = jnp.zeros_like(ci_ref)

        u = ua_ref[...]
        ub = u.astype(BF)
        landed(0)
        sr_ref[...] = jnp.dot(ub, bbr_vm[...], preferred_element_type=F32)
        landed(1)
        si_ref[...] = jnp.dot(ub, bbi_vm[...], preferred_element_type=F32)
        _cscan(sr_ref, si_ref, con_ref, cr_ref, ci_ref, reverse=False)
        landed(2)
        landed(3)
        y =_mm_nt(sr_ref[...], ccr_vm[...]) - _mm_nt(si_ref[...], cci_vm[...]) + dsk_ref[...] * u
        y_ref[...] = y
        zg = jax.nn.gelu(y)
        zg_ref[...] = zg.astype(BF)
        q = _mm(zg, wg_ref[...]) + bg_ref[...]
        ya_ref[...] = (zg * _sig(q)).astype(BF)

    return _pallas_call(
        body, carry, name="s5_fwd", grid=(L // TM,),
        in_specs=[_tok(S5W), ANY, ANY, ANY, ANY, _full((1, S5W)), _full((8 * SUB, GN)),
                  _full((S5W, S5W)), _full((1, S5W))],
        out_specs=[_tok(GN), _tok(GN), _tok(S5W), _tok(S5W), _tok(S5W)],
        out_shape=[_sds((L, GN)), _sds((L, GN)), _sds((L, S5W)), _sds((L, S5W), BF), _sds((L, S5W), BF)],
        scratch_shapes=[pltpu.VMEM((S5W, GN), BF), pltpu.VMEM((S5W, GN), BF), pltpu.VMEM((S5W, GN), BF),
                        pltpu.VMEM((S5W, GN), BF), pltpu.VMEM((SUB, GN), F32), pltpu.VMEM((SUB, GN), F32),
                        pltpu.SemaphoreType.DMA((4,))],
        compiler_params=_params(44),
    )(ua, bbr, bbi, ccr, cci, dsk, con, w_glu, b_glu)


def _s5_bwd(dya, y, ua, sr, si, bbr, bbi, ccr, cci, dsk, con_rev, w_glu, b_glu, carry=None):
    L = ua.shape[0]
    nt = L // TM
    spt = TM // SUB
    n_slab = spt

    def halo_map(i):
        return (jnp.maximum((nt - 1 - i) * spt - 1, 0), 0)

    def body(dya_ref, y_ref, ua_ref, sr_ref, si_ref, hr_ref, hi_ref, bbr_hbm, bbi_hbm, ccr_hbm, cci_hbm,
             dsk_ref, con_ref, wg_ref, bg_ref,
             dua_ref, dq_ref, dy_ref, lr_ref, li_ref, da_ref, dsm_ref,
             bbr_vm, bbi_vm, ccr_vm, cci_vm, cr_ref, ci_ref, w_sems):
        i = pl.program_id(0)
        landed = _resident([(ccr_hbm, ccr_vm), (cci_hbm, cci_vm), (bbr_hbm, bbr_vm), (bbi_hbm, bbi_vm)], w_sems)

        @pl.when(i == 0)
        def _():
            cr_ref[...] = jnp.zeros_like(cr_ref)
            ci_ref[...] = jnp.zeros_like(ci_ref)
            da_ref[...] = jnp.zeros_like(da_ref)
            dsm_ref[...] = jnp.zeros_like(dsm_ref)

        u = ua_ref[...]
        yv = y_ref[...]
        dya = dya_ref[...]
        zg = jax.nn.gelu(yv)
        sg = _sig(_mm(zg, wg_ref[...]) + bg_ref[...])
        dq = dya * zg * sg * (1.0 - sg)
        dq_ref[...] = dq.astype(BF)
        dzg = dya * sg + _mm_nt(dq, wg_ref[...])
        dy = dzg * _gelu_grad(yv)
        dyb = dy.astype(BF)
        dy_ref[...] = dyb
        dsm_ref[0:1, :] += _colsum(dy * u)
        dsm_ref[1:2, :] += _colsum(dq)
        landed(0)
        lr_ref[...] = jnp.dot(dyb, ccr_vm[...], preferred_element_type=F32)
        landed(1)
        li_ref[...] = -jnp.dot(dyb, cci_vm[...], preferred_element_type=F32)
        _cscan(lr_ref, li_ref, con_ref, cr_ref, ci_ref, reverse=True)

        first_tile = (i == nt - 1)
        row = _row_iota(LC)
        for lc in range(GN // LC):
            cols = slice(lc * LC, (lc + 1) * LC)
            h_r = jnp.where(first_tile, 0.0, hr_ref[:, cols])
            h_i = jnp.where(first_tile, 0.0, hi_ref[:, cols])

            def step(k, acc, cols=cols, h_r=h_r, h_i=h_i):
                ar, ai = acc
                rows = _slab(k)
                prev = _slab(jnp.maximum(k - 1, 0))
                pr = jnp.where(k == 0, h_r, sr_ref[prev, cols])
                pi = jnp.where(k == 0, h_i, si_ref[prev, cols])
                spr = pltpu.roll(jnp.where(row == SUB - 1, pr, sr_ref[rows, cols]), 1, 0)
                spi = pltpu.roll(jnp.where(row == SUB - 1, pi, si_ref[rows, cols]), 1, 0)
                lr, li = lr_ref[rows, cols], li_ref[rows, cols]
                return ar + lr * spr + li * spi, ai + li * spr - lr * spi

            zero = jnp.zeros((SUB, LC), F32)
            ar, ai = lax.fori_loop(0, n_slab, step, (zero, zero))
            da_ref[0:1, cols] += _colsum(ar)
            da_ref[1:2, cols] += _colsum(ai)

        landed(2)
        landed(3)
        dua_ref[...] = (dy * dsk_ref[...] + _mm_nt(lr_ref[...], bbr_vm[...]) + _mm_nt(li_ref[...], bbi_vm[...]))

    return _pallas_call(
        body, carry, name="s5_bwd", grid=(nt,),
        in_specs=[_tok_rev(S5W, nt), _tok_rev(S5W, nt), _tok_rev(S5W, nt), _tok_rev(GN, nt), _tok_rev(GN, nt),
                  pl.BlockSpec((SUB, GN), halo_map), pl.BlockSpec((SUB, GN), halo_map),
                  ANY, ANY, ANY, ANY, _full((1, S5W)), _full((8 * SUB, GN)), _full((S5W, S5W)), _full((1, S5W))],
        out_specs=[_tok_rev(S5W, nt), _tok_rev(S5W, nt), _tok_rev(S5W, nt), _tok_rev(GN, nt), _tok_rev(GN, nt),
                   _full((SUB, GN)), _full((SUB, S5W))],
        out_shape=[_sds((L, S5W)), _sds((L, S5W), BF), _sds((L, S5W), BF), _sds((L, GN)), _sds((L, GN)),
                   _sds((SUB, GN)), _sds((SUB, S5W))],
        scratch_shapes=[pltpu.VMEM((S5W, GN), BF), pltpu.VMEM((S5W, GN), BF), pltpu.VMEM((S5W, GN), BF),
                        pltpu.VMEM((S5W, GN), BF), pltpu.VMEM((SUB, GN), F32), pltpu.VMEM((SUB, GN), F32),
                        pltpu.SemaphoreType.DMA((4,))],
        compiler_params=_params(52),
    )(dya, y, ua, sr, si, sr, si, bbr, bbi, ccr, cci, dsk, con_rev, w_glu, b_glu)


def _lru_gate_terms(rg, sp):
    log_a = -LRU_C * rg * sp
    a = jnp.exp(log_a)
    mult = jnp.sqrt(_neg_expm1(2.0 * log_a))
    return a, mult


def _lru_fwd(ub, conv_w, conv_b, wr, wi, b_r, b_i, sp, carry=None):
    L = ub.shape[0]
    n_slab = TM // SUB

    def body(ub_ref, cw_ref, cb_ref, wr_ref, wi_ref, br_ref, bi_ref, sp_ref,
             xc_ref, rg_ref, ig_ref, h_ref, hp_ref, a_ref, halo_ref, carry_ref):
        @pl.when(pl.program_id(0) == 0)
        def _():
            halo_ref[...] = jnp.zeros_like(halo_ref)
            carry_ref[...] = jnp.zeros_like(carry_ref)

        row = _row_iota(LW)
        taps = [cw_ref[k:k + 1, :] for k in range(4)]
        cb = cb_ref[...]

        def conv_step(k, prev):
            rows = _slab(k)
            cur = ub_ref[rows, :]
            acc = taps[3] * cur + cb
            for j in (1, 2, 3):
                acc = acc + taps[3 - j] * pltpu.roll(jnp.where(row >= SUB - j, prev, cur), j, 0)
            xc_ref[rows, :] = acc
            return cur

        halo_ref[...] = lax.fori_loop(0, n_slab, conv_step, halo_ref[...])

        xc = xc_ref[...]
        xcb = xc.astype(BF)
        rg = _sig(jnp.dot(xcb, wr_ref[...], preferred_element_type=F32) + br_ref[...])
        ig = _sig(jnp.dot(xcb, wi_ref[...], preferred_element_type=F32) + bi_ref[...])
        rg_ref[...] = rg
        ig_ref[...] = ig
        a, mult = _lru_gate_terms(rg, sp_ref[...])
        a_ref[...] = a
        h_ref[...] = mult * ig * xc

        rowc = _row_iota(LC)
        for lc in range(LW // LC):
            cols = slice(lc * LC, (lc + 1) * LC)

            def step(k, c, cols=cols):
                rows = _slab(k)
                av, b = a_ref[rows, cols], h_ref[rows, cols]
                for sh in (1, 2, 4):
                    keep = rowc >= sh
                    b = b + av * jnp.where(keep, pltpu.roll(b, sh, 0), 0.0)
                    av = av * jnp.where(keep, pltpu.roll(av, sh, 0), 1.0)
                h = b + av * c
                h_ref[rows, cols] = h
                hp_ref[rows, cols] = jnp.where(rowc == 0, c, pltpu.roll(h, 1, 0))
                return _bcast_row(h, SUB - 1)

            carry_ref[:, cols] = lax.fori_loop(0, n_slab, step, carry_ref[:, cols])

    return _pallas_call(
        body, carry, name="lru_fwd", grid=(L // TM,),
        in_specs=[_tok(LW), _full((4, LW)), _full((1, LW)), _full((LW, LW)), _full((LW, LW)),
                  _full((1, LW)), _full((1, LW)), _full((1, LW))],
        out_specs=[_tok(LW)] * 5,
        out_shape=[_sds((L, LW))] * 5,
        scratch_shapes=[pltpu.VMEM((TM, LW), F32), pltpu.VMEM((SUB, LW), F32), pltpu.VMEM((SUB, LW), F32)],
        compiler_params=_params(40),
    )(ub, conv_w, conv_b, wr, wi, b_r, b_i, sp)


def _lru_bwd(dyb, xc, rg, ig, hp, ub, conv_w, wr, wi, sp, dsp, carry=None):
    L = ub.shape[0]
    nt = L // TM
    spt = TM // SUB
    n_slab = spt

    def halo_map(i):
        return (jnp.maximum((nt - 1 - i) * spt - 1, 0), 0)

    def body(dh_ref, xc_ref, rg_ref, ig_ref, hp_ref, ub_ref, uh_ref, cw_ref, wr_ref, wi_ref, sp_ref, dsp_ref,
             dub_ref, dpr_ref, dpi_ref, acc_ref, a_ref, lam_ref, dxc_ref, carry_ref, next_ref):
        i = pl.program_id(0)

        @pl.when(i == 0)
        def _():
            carry_ref[...] = jnp.zeros_like(carry_ref)
            next_ref[...] = jnp.zeros_like(next_ref)
            acc_ref[...] = jnp.zeros_like(acc_ref)

        sp = sp_ref[...]
        rg, ig, xc = rg_ref[...], ig_ref[...], xc_ref[...]
        a, mult = _lru_gate_terms(rg, sp)
        a_ref[...] = a

        rowc = _row_iota(LC)
        for lc in range(LW // LC):
            cols = slice(lc * LC, (lc + 1) * LC)

            def step(k, c, cols=cols):
                rows = _slab(n_slab - 1 - k)
                av, dh = a_ref[rows, cols], dh_ref[rows, cols]
                b = av * dh
                for sh in (1, 2, 4):
                    keep = rowc < SUB - sh
                    b = b + av * jnp.where(keep, pltpu.roll(b, SUB - sh, 0), 0.0)
                    av = av * jnp.where(keep, pltpu.roll(av, SUB - sh, 0), 1.0)
                mu = b + av * c
                lam_ref[rows, cols] = dh + jnp.where(rowc == SUB - 1, c, pltpu.roll(mu, SUB - 1, 0))
                return _bcast_row(mu, 0)

            carry_ref[:, cols] = lax.fori_loop(0, n_slab, step, carry_ref[:, cols])

        lam = lam_ref[...]
        d_a = lam * hp_ref[...]
        d_mult = lam * ig * xc
        d_ig = lam * mult * xc
        dxc = lam * mult * ig
        d_log_a = d_a * a - d_mult * a * a / mult
        d_rg = (-LRU_C) * sp * d_log_a
        acc_ref[0:1, :] += _colsum((-LRU_C) * rg * d_log_a) * dsp_ref[...]
        dpr = d_rg * rg * (1.0 - rg)
        dpi = d_ig * ig * (1.0 - ig)
        acc_ref[1:2, :] += _colsum(dpr)
        acc_ref[2:3, :] += _colsum(dpi)
        dprb, dpib = dpr.astype(BF), dpi.astype(BF)
        dpr_ref[...] = dprb
        dpi_ref[...] = dpib
        dxc = dxc + _mm_nt(dprb, wr_ref[...]) + _mm_nt(dpib, wi_ref[...])
        dxc_ref[...] = dxc
        acc_ref[3:4, :] += _colsum(dxc)

        row = _row_iota(LW)
        taps = [cw_ref[k:k + 1, :] for k in range(4)]
        u_halo = jnp.where(i == nt - 1, 0.0, uh_ref[...])
        nxt_tile = next_ref[...]

        def conv_step(k, accs):
            rows = _slab(k)
            cur = dxc_ref[rows, :]
            nxt = jnp.where(k == n_slab - 1, nxt_tile, dxc_ref[_slab(jnp.minimum(k + 1, n_slab - 1)), :])
            ucur = ub_ref[rows, :]
            uprev = jnp.where(k == 0, u_halo, ub_ref[_slab(jnp.maximum(k - 1, 0)), :])
            du = taps[3] * cur
            new = [accs[3] + cur * ucur]
            for j in (1, 2, 3):
                du = du + taps[3 - j] * pltpu.roll(jnp.where(row < j, nxt, cur), SUB - j, 0)
                new.append(accs[3 - j] + cur * pltpu.roll(jnp.where(row >= SUB - j, uprev, ucur), j, 0))
            dub_ref[rows, :] = du
            return tuple(new[::-1])

        zero = jnp.zeros((SUB, LW), F32)
        accs = lax.fori_loop(0, n_slab, conv_step, (zero, zero, zero, zero))
        for k in range(4):
            acc_ref[4 + k:5 + k, :] += _colsum(accs[k])
        next_ref[...] = dxc_ref[0:SUB, :]

    return _pallas_call(
        body, carry, name="lru_bwd", grid=(nt,),
        in_specs=[_tok_rev(LW, nt)] * 6 + [pl.BlockSpec((SUB, LW), halo_map), _full((4, LW)),
                                           _full((LW, LW)), _full((LW, LW)), _full((1, LW)), _full((1, LW))],
        out_specs=[_tok_rev(LW, nt), _tok_rev(LW, nt), _tok_rev(LW, nt), _full((SUB, LW))],
        out_shape=[_sds((L, LW)), _sds((L, LW), BF), _sds((L, LW), BF), _sds((SUB, LW))],
        scratch_shapes=[pltpu.VMEM((TM, LW), F32), pltpu.VMEM((TM, LW), F32), pltpu.VMEM((TM, LW), F32),
                        pltpu.VMEM((SUB, LW), F32), pltpu.VMEM((SUB, LW), F32)],
        compiler_params=_params(48),
    )(dyb, xc, rg, ig, hp, ub, ub, conv_w, wr, wi, sp, dsp)


AC = D // NCHIP


def _merge_fwd(x, ya, yb, gp, w_a, w_b, w_o, carry=None):
    L = x.shape[0]

    def body(x_ref, ya_ref, yb_ref, gp_ref, wa_ref, wb_ref, wo_ref, x1_ref, pa_ref, pb_ref, mg_ref):
        ya = ya_ref[...]
        for k in range(NCHIP):
            pa_ref[:, k * AC:(k + 1) * AC] = jnp.dot(ya, wa_ref[k], preferred_element_type=F32)
        pb = _mm(yb_ref[...], wb_ref[...])
        pb_ref[...] = pb
        gp = gp_ref[...]
        merged = (_sig(gp[:, :D]) * pa_ref[...] + _sig(gp[:, D:]) * pb).astype(BF)
        mg_ref[...] = merged
        x1_ref[...] = x_ref[...] + jnp.dot(merged, wo_ref[...], preferred_element_type=F32)

    return _pallas_call(
        body, carry, name="merge_fwd", grid=(L // TM,),
        in_specs=[_tok(D), _tok(S5W), _tok(LW), _tok(2 * D), _full((NCHIP, S5W, AC)), _full((LW, D)), _full((D, D))],
        out_specs=[_tok(D), _tok(D), _tok(D), _tok(D)],
        out_shape=[_sds((L, D)), _sds((L, D)), _sds((L, D)), _sds((L, D), BF)],
        compiler_params=_params(40),
    )(x, ya, yb, gp, w_a, w_b, w_o)


def _merge_bwd(dx1, gp, pa, pb, w_a, w_b, w_o, carry=None):
    L = dx1.shape[0]

    def body(dx1_ref, gp_ref, pa_ref, pb_ref, wa_ref, wb_ref, wo_ref, dya_ref, dyb_ref, dgp_ref, dpa_ref, dpb_ref):
        dm = _mm_nt(dx1_ref[...], wo_ref[...])
        gp = gp_ref[...]
        sa, sb = _sig(gp[:, :D]), _sig(gp[:, D:])
        dpa = (dm * sa).astype(BF)
        dpb = (dm * sb).astype(BF)
        dpa_ref[...] = dpa
        dpb_ref[...] = dpb
        dgp_ref[:, :D] = dm * pa_ref[...] * sa * (1.0 - sa)
        dgp_ref[:, D:] = dm * pb_ref[...] * sb * (1.0 - sb)
        dya = jnp.zeros((TM, S5W), F32)
        for k in range(NCHIP):
            dya = dya + _mm_nt(dpa[:, k * AC:(k + 1) * AC], wa_ref[k])
        dya_ref[...] = dya
        dyb_ref[...] = _mm_nt(dpb, wb_ref[...])

    return _pallas_call(
        body, carry, name="merge_bwd", grid=(L // TM,),
        in_specs=[_tok(D), _tok(2 * D), _tok(D), _tok(D), _full((NCHIP, S5W, AC)), _full((LW, D)), _full((D, D))],
        out_specs=[_tok(S5W), _tok(LW), _tok(2 * D), _tok(D), _tok(D)],
        out_shape=[_sds((L, S5W)), _sds((L, LW)), _sds((L, 2 * D)), _sds((L, D), BF), _sds((L, D), BF)],
        compiler_params=_params(40),
    )(dx1, gp, pa, pb, w_a, w_b, w_o)


def _chunk_tok(width):
    return pl.BlockSpec((NCHIP, TM, width), lambda i: (0, i, 0))


def _ffn_fwd(x1, g_ffn, wg, wu, wd, carry=None):
    L = x1.shape[0]

    def body(x_ref, g_ref, wg_hbm, wu_hbm, wd_hbm, x2_ref, h2_ref, gg_ref, uu_ref, wg_vm, wu_vm, wd_vm, w_sems):
        _resident_now([(src.at[c], dst.at[c]) for c in range(NCHIP)
                       for src, dst in ((wg_hbm, wg_vm), (wu_hbm, wu_vm), (wd_hbm, wd_vm))], w_sems)
        x = x_ref[...]
        xh, _ = _rms(x)
        h2 = (xh * g_ref[...]).astype(BF)
        h2_ref[...] = h2
        out = x
        for c in range(NCHIP):
            gg = lax.dot_general(h2, wg_vm[c], (((1,), (1,)), ((), ())), preferred_element_type=F32)
            uu = lax.dot_general(h2, wu_vm[c], (((1,), (1,)), ((), ())), preferred_element_type=F32)
            gg_ref[c] = gg.astype(BF)
            uu_ref[c] = uu.astype(BF)
            act = (gg * _sig(gg) * uu).astype(BF)
            out = out + jnp.dot(act, wd_vm[c], preferred_element_type=F32)
        x2_ref[...] = out

    return _pallas_call(
        body, carry, name="ffn_fwd", grid=(L // TM,),
        in_specs=[_tok(D), _full((1, D)), ANY, ANY, ANY],
        out_specs=[_tok(D), _tok(D), _chunk_tok(FC), _chunk_tok(FC)],
        out_shape=[_sds((L, D)), _sds((L, D), BF), _sds((NCHIP, L, FC), BF), _sds((NCHIP, L, FC), BF)],
        scratch_shapes=[pltpu.VMEM((NCHIP, FC, D), BF)] * 3 + [pltpu.SemaphoreType.DMA((3 * NCHIP,))],
        compiler_params=_params(52),
    )(x1, g_ffn, wg, wu, wd)


def _ffn_bwd(x1, dx2, gg, uu, g_ffn, wg, wu, wd, carry=None):
    L = x1.shape[0]

    def body(x_ref, dx2_ref, gg_ref, uu_ref, g_ref, wg_hbm, wu_hbm, wd_hbm,
             dx1_ref, act_ref, dgg_ref, duu_ref, dg_ref, wg_vm, wu_vm, wd_vm, w_sems):
        _resident_now([(src.at[c], dst.at[c]) for c in range(NCHIP)
                       for src, dst in ((wg_hbm, wg_vm), (wu_hbm, wu_vm), (wd_hbm, wd_vm))], w_sems)

        @pl.when(pl.program_id(0) == 0)
        def _():
            dg_ref[...] = jnp.zeros_like(dg_ref)

        dx2 = dx2_ref[...]
        dx2b = dx2.astype(BF)
        dh2 = jnp.zeros((TM, D), F32)
        for c in range(NCHIP):
            g = gg_ref[c].astype(F32)
            u = uu_ref[c].astype(F32)
            s = _sig(g)
            silu = g * s
            act_ref[c] = (silu * u).astype(BF)
            dact = lax.dot_general(dx2b, wd_vm[c], (((1,), (1,)), ((), ())), preferred_element_type=F32)
            dg = (dact * u * s * (1.0 + g * (1.0 - s))).astype(BF)
            du = (dact * silu).astype(BF)
            dgg_ref[c] = dg
            duu_ref[c] = du
            dh2 = dh2 + jnp.dot(dg, wg_vm[c], preferred_element_type=F32)
            dh2 = dh2 + jnp.dot(du, wu_vm[c], preferred_element_type=F32)
        xh, r = _rms(x_ref[...])
        dg_ref[0:1, :] += _colsum(dh2 * xh)
        dx1_ref[...] = dx2 + _rms_bwd(dh2, xh, r, g_ref[...])

    return _pallas_call(
        body, carry, name="ffn_bwd", grid=(L // TM,),
        in_specs=[_tok(D), _tok(D), _chunk_tok(FC), _chunk_tok(FC), _full((1, D)), ANY, ANY, ANY],
        out_specs=[_tok(D), _chunk_tok(FC), _chunk_tok(FC), _chunk_tok(FC), _full((SUB, D))],
        out_shape=[_sds((L, D)), _sds((NCHIP, L, FC), BF), _sds((NCHIP, L, FC), BF), _sds((NCHIP, L, FC), BF),
                   _sds((SUB, D))],
        scratch_shapes=[pltpu.VMEM((NCHIP, FC, D), BF)] * 3 + [pltpu.SemaphoreType.DMA((3 * NCHIP,))],
        compiler_params=_params(56),
    )(x1, dx2, gg, uu, g_ffn, wg, wu, wd)


def _ple_loss(x2, p, tgt, g_pg, w_pg, b_pg, w_ple, g_ple, g_final):
    L = x2.shape[0]

    def body(x2_ref, p_ref, t_ref, gpg_ref, wpg_ref, bpg_ref, wple_ref, gple_ref, gf_ref,
             dx2_ref, n2_ref, dpre_ref, de0_ref, acc_ref):
        @pl.when(pl.program_id(0) == 0)
        def _():
            acc_ref[...] = jnp.zeros_like(acc_ref)

        x2 = x2_ref[...]
        x2h, r2 = _rms(x2)
        n2 = (x2h * gpg_ref[...]).astype(BF)
        n2_ref[...] = n2
        gate = _sig(jnp.dot(n2, wpg_ref[...], preferred_element_type=F32) + bpg_ref[...])
        pb = p_ref[...].astype(BF)
        e0 = jnp.concatenate([jnp.dot(pb, wple_ref[k], preferred_element_type=F32) for k in range(NCHIP)], axis=1)
        e0h, re = _rms(e0)
        e = e0h * gple_ref[...]
        x3 = x2 + gate * e
        x3h, r3 = _rms(x3)
        diff = x3h * gf_ref[...] - t_ref[...]
        acc_ref[4:5, :] += _colsum(diff * diff) * (0.5 / D)
        dy = diff * (1.0 / D)
        acc_ref[3:4, :] += _colsum(dy * x3h)
        dx3 = _rms_bwd(dy, x3h, r3, gf_ref[...])
        de = dx3 * gate
        acc_ref[2:3, :] += _colsum(de * e0h)
        de0_ref[...] = _rms_bwd(de, e0h, re, gple_ref[...]).astype(BF)
        dpre = dx3 * e * gate * (1.0 - gate)
        acc_ref[1:2, :] += _colsum(dpre)
        dpreb = dpre.astype(BF)
        dpre_ref[...] = dpreb
        dn2 = lax.dot_general(dpreb, wpg_ref[...], (((1,), (1,)), ((), ())), preferred_element_type=F32)
        acc_ref[0:1, :] += _colsum(dn2 * x2h)
        dx2_ref[...] = dx3 + _rms_bwd(dn2, x2h, r2, gpg_ref[...])

    return _pallas_call(
        body, name="ple_loss", grid=(L // TM,),
        in_specs=[_tok(D), _tok(PLE), _tok(D), _full((1, D)), _full((D, D)), _full((1, D)), _full((NCHIP, PLE, AC)),
                  _full((1, D)), _full((1, D))],
        out_specs=[_tok(D), _tok(D), _tok(D), _tok(D), _full((SUB, D))],
        out_shape=[_sds((L, D)), _sds((L, D), BF), _sds((L, D), BF), _sds((L, D), BF), _sds((SUB, D))],
        compiler_params=_params(40),
    )(x2, p, tgt, g_pg, w_pg, b_pg, w_ple, g_ple, g_final)


def _tn(name, a, b, col_chunk=None, a_block=None, carry=None):
    L = a.shape[-2]
    m, n = a.shape[-1], b.shape[-1]
    a_col = 0
    if a_block is not None:
        a_col, m = a_block
    tk = L
    if a.ndim == 3 or b.ndim == 3:
        nj, bn = (a if a.ndim == 3 else b).shape[0], n
        a_spec = (pl.BlockSpec((None, tk, m), lambda j, t: (j, t, 0)) if a.ndim == 3
                  else pl.BlockSpec((tk, m), lambda j, t: (t, 0)))
        b_spec = (pl.BlockSpec((None, tk, n), lambda j, t: (j, t, 0)) if b.ndim == 3
                  else pl.BlockSpec((tk, n), lambda j, t: (t, 0)))
        out_spec, out_shape = pl.BlockSpec((None, m, n), lambda j, t: (j, 0, 0)), _sds((nj, m, n))
    else:
        bn = col_chunk
        if bn is None:
            bn = next((cand for cand in (1024, 512) if n > cand and n % cand == 0), n)
        nj = n // bn
        a_spec = pl.BlockSpec((tk, m), lambda j, t: (t, a_col))
        b_spec = pl.BlockSpec((tk, bn), lambda j, t: (t, j))
        if col_chunk is None:
            out_spec, out_shape = pl.BlockSpec((m, bn), lambda j, t: (0, j)), _sds((m, n))
        else:
            out_spec, out_shape = pl.BlockSpec((None, m, bn), lambda j, t: (j, 0, 0)), _sds((nj, m, bn))

    def body(a_ref, b_ref, o_ref):
        o_ref[...] = _mm_tn(a_ref[...], b_ref[...])

    outs = _pallas_call(
        body, carry, name=name, grid=(nj, L // tk), in_specs=[a_spec, b_spec], out_specs=[out_spec],
        out_shape=[pltpu.HBM(out_shape.shape, out_shape.dtype)],
        compiler_params=pltpu.CompilerParams(dimension_semantics=("arbitrary", "arbitrary"),
                                             vmem_limit_bytes=52 * VMEM_MB),
    )(a, b)
    return outs[0] if carry is None else outs


LANE = 128


def _tn_blocks(name, a, bs, ga, gb, carry=None):
    L, m, n, nb = a.shape[0], a.shape[1], bs[0].shape[1], len(bs)
    per = LANE // ga
    wb = per * gb
    n_super = m // LANE

    def body(a_ref, *refs):
        b_refs, o_refs, acc_refs = refs[:nb], refs[nb:2 * nb], refs[2 * nb:]
        t = pl.program_id(0)

        @pl.when(t == 0)
        def _():
            for acc in acc_refs:
                acc[...] = jnp.zeros_like(acc)

        lhs = a_ref[...].astype(BF)
        for b_ref, acc in zip(b_refs, acc_refs):
            rhs = b_ref[...].astype(BF)
            for j in range(n_super):
                acc[j] += _mm_tn(lhs[:, j * LANE:(j + 1) * LANE], rhs[:, j * wb:(j + 1) * wb])

        @pl.when(t == L // TK - 1)
        def _():
            own = (lax.broadcasted_iota(jnp.int32, (LANE, wb), 0) // ga) == (lax.broadcasted_iota(jnp.int32, (LANE, wb), 1) // gb)
            for o_ref, acc in zip(o_refs, acc_refs):
                for j in range(n_super):
                    kept = jnp.where(own, acc[j], 0.0)
                    o_ref[:, j * wb:(j + 1) * wb] = jnp.sum(kept.reshape(per, ga, wb), axis=0)

    outs = _pallas_call(
        body, carry, name=name, grid=(L // TK,),
        in_specs=[pl.BlockSpec((TK, m), lambda t: (t, 0))] + [pl.BlockSpec((TK, n), lambda t: (t, 0))] * nb,
        out_specs=[_full((ga, n))] * nb, out_shape=[_sds((ga, n))] * nb,
        scratch_shapes=[pltpu.VMEM((n_super, LANE, wb), F32)] * nb,
        compiler_params=_params(48),
    )(*_in_hbm([a] + list(bs)))
    return list(outs)


def _s5_discretize(lam_re, lam_im, log_dt, b_re, b_im):
    dt = jnp.exp(log_dt)[:, None]
    mag = jnp.exp(lam_re * dt)
    ar = mag * jnp.cos(lam_im * dt)
    ai = mag * jnp.sin(lam_im * dt)
    den = lam_re * lam_re + lam_im * lam_im
    nr = ar - 1.0
    fr = (nr * lam_re + ai * lam_im) / den
    fi = (ai * lam_re - nr * lam_im) / den
    bbr = fr[:, None, :] * b_re - fi[:, None, :] * b_im
    bbi = fr[:, None, :] * b_im + fi[:, None, :] * b_re
    return ar, ai, bbr, bbi


def _prepare(by_rows, block_cols, ar, ai):
    n = len(by_rows)

    def body(*refs):
        srcs, (ar_ref, ai_ref), dense, (con_ref, rev_ref) = refs[:n], refs[n:n + 2], refs[n + 2:2 * n + 2], refs[2 * n + 2:]
        for src, out, c in zip(srcs, dense, block_cols):
            r, width = src.shape
            groups = width // c
            tiled = jnp.broadcast_to(src[...][None], (groups, r, width)).reshape(groups * r, width)
            own = (lax.broadcasted_iota(jnp.int32, tiled.shape, 0) // r) == (lax.broadcasted_iota(jnp.int32, tiled.shape, 1) // c)
            out[...] = jnp.where(own, tiled, 0.0).astype(BF)
        a_r, a_i = ar_ref[...], ai_ref[...]
        pw = [(jnp.ones_like(a_r), jnp.zeros_like(a_i))]
        for _ in range(SUB):
            pr, pi = pw[-1]
            pw.append((pr * a_r - pi * a_i, pr * a_i + pi * a_r))
        row = _row_iota(GN)
        for ref, reverse in ((con_ref, False), (rev_ref, True)):
            sign = -1.0 if reverse else 1.0
            for j, sh in enumerate((1, 2, 4)):
                keep = (row < SUB - sh) if reverse else (row >= sh)
                ref[2 * j * SUB:(2 * j + 1) * SUB, :] = jnp.where(keep, pw[sh][0], 0.0)
                ref[(2 * j + 1) * SUB:(2 * j + 2) * SUB, :] = jnp.where(keep, sign * pw[sh][1], 0.0)
            p_r, p_i = jnp.zeros((SUB, GN), F32), jnp.zeros((SUB, GN), F32)
            for i in range(SUB):
                k = SUB - i if reverse else i + 1
                p_r = jnp.where(row == i, pw[k][0], p_r)
                p_i = jnp.where(row == i, sign * pw[k][1], p_i)
            ref[6 * SUB:7 * SUB, :] = p_r
            ref[7 * SUB:8 * SUB, :] = p_i

    dense_shapes = [(b.shape[1] // c * b.shape[0], b.shape[1]) for b, c in zip(by_rows, block_cols)]
    outs = _pallas_call(
        body, name="prepare", grid=(1,), in_specs=[_full(b.shape) for b in by_rows] + [_full((1, GN))] * 2,
        out_specs=[_full(s) for s in dense_shapes] + [_full((8 * SUB, GN))] * 2,
        out_shape=[_sds(s, BF) for s in dense_shapes] + [_sds((8 * SUB, GN))] * 2,
        compiler_params=_params(48),
    )(*by_rows, ar, ai)
    return outs[:n], outs[n], outs[n + 1]


def _local_step(x, p, tgt, w, comm):
    rows_of = lambda a: a.reshape(NCHIP * a.shape[1], a.shape[2])
    quarters = lambda a: a.reshape(NCHIP, a.shape[0] // NCHIP, a.shape[1])

    def gathering(names, call):
        carry = comm.gather(names)
        outs = list(call(carry))
        own = len(outs) - len(carry.out_shapes)
        w.update(zip(names, outs[own:]))
        return outs[:own]

    w.update(comm.first())
    w_glu = rows_of(w["w_glu"])
    ar, ai, bbr, bbi = _s5_discretize(w["lam_re"], w["lam_im"], w["log_dt"], w["s5_b_re"], w["s5_b_im"])
    by_row = lambda b: jnp.transpose(b, (1, 0, 2)).reshape(b.shape[1], -1)
    (bbr_d, bbi_d, ccr_d, cci_d, wr_d, wi_d), con, con_rev = _prepare(
        [by_row(b) for b in (bbr, bbi, w["s5_c_re"], w["s5_c_im"], w["w_r"], w["w_i"])], [NS] * 4 + [HD] * 2,
        ar.reshape(1, GN), ai.reshape(1, GN))
    dsk = w["s5_d"].reshape(1, S5W)
    lam = w["lru_lambda"].reshape(1, LW)
    sp = jax.nn.softplus(-lam)
    b_r, b_i = w["b_r"].reshape(1, LW), w["b_i"].reshape(1, LW)
    row = lambda name: w[name].reshape(1, -1)

    h, ua, ub, gp = gathering(["w_a_out", "w_b_out"], lambda carry: _inproj_fwd(
        x, row("g_mix"), w["w_in"], row("b_in"), carry))
    sr, si, y, zg, ya = gathering(["w_o", "w_ffn_gate"], lambda carry: _s5_fwd(
        ua, bbr_d, bbi_d, ccr_d, cci_d, dsk, con, w_glu, row("b_glu"), carry))
    xc, rg, ig, yb, hp = gathering(["w_ffn_up"], lambda carry: _lru_fwd(
        ub, w["conv_w"], row("conv_b"), wr_d, wi_d, b_r, b_i, sp, carry))
    w_b_out, w_o = rows_of(w["w_b_out"]), rows_of(w["w_o"])
    x1, pa, pb, merged = gathering(["w_ffn_down"], lambda carry: _merge_fwd(
        x, ya, yb, gp, w["w_a_out"], w_b_out, w_o, carry))
    x2, h2, gg, uu = gathering(["w_ple_gate", "w_ple"], lambda carry: _ffn_fwd(
        x1, row("g_ffn"), w["w_ffn_gate"], w["w_ffn_up"], w["w_ffn_down"], carry))
    w_pg = rows_of(w["w_ple_gate"])
    dx2, n2, dpre, de0, acc_p = _ple_loss(x2, p, tgt, row("g_ple_gate"), w_pg, row("b_ple_gate"),
                                          w["w_ple"], row("g_ple"), row("g_final"))
    comm.reduce("ple", {"w_ple_gate": quarters(_tn("dw_ple_gate", n2, dpre)),
                        "w_ple": _tn("dw_ple", p, de0, col_chunk=AC)})
    dx1, act, dgg, duu, acc_f = comm.run(lambda carry: _ffn_bwd(
        x1, dx2, gg, uu, row("g_ffn"), w["w_ffn_gate"], w["w_ffn_up"], w["w_ffn_down"], carry))
    comm.reduce("ffn_gate", {"w_ffn_gate": _tn("dw_ffn_gate", dgg, h2)})
    comm.reduce("ffn_up", {"w_ffn_up": comm.run(lambda carry: _tn("dw_ffn_up", duu, h2, carry=carry))[0]})
    comm.reduce("ffn_down", {"w_ffn_down": comm.run(lambda carry: _tn("dw_ffn_down", act, dx2, carry=carry))[0]})
    dya, dyb, dgp, dpa, dpb = comm.run(lambda carry: _merge_bwd(
        dx1, gp, pa, pb, w["w_a_out"], w_b_out, w_o, carry))
    comm.reduce("merge", {"w_o": quarters(_tn("dw_o", merged, dx1)), "w_a_out": _tn("dw_a_out", ya, dpa, col_chunk=AC),
                          "w_b_out": quarters(_tn("dw_b_out", yb, dpb))})
    dua, dq, dy, lr, li, acc_a, acc_s = comm.run(lambda carry: _s5_bwd(
        dya, y, ua, sr, si, bbr_d, bbi_d, ccr_d, cci_d, dsk, con_rev, w_glu, row("b_glu"), carry))
    dub, dpr, dpi, acc_l = comm.run(lambda carry: _lru_bwd(
        dyb, xc, rg, ig, hp, ub, w["conv_w"], wr_d, wi_d, sp, -_sig(-lam), carry))
    gx, dz, acc_g, acc_b = _inproj_bwd(x, dx1, dua, dub, dgp, row("g_mix"), w["w_in"])
    half = (D // 2,)
    comm.reduce("in_lo", {"w_in_lo": comm.run(lambda carry: _tn(
        "dw_in_lo", h, dz, col_chunk=QC, a_block=(0,) + half, carry=carry))[0]})
    comm.reduce("in_hi", {"w_in_hi": comm.run(lambda carry: _tn(
        "dw_in_hi", h, dz, col_chunk=QC, a_block=(1,) + half, carry=carry))[0], "w_glu": quarters(_tn("dw_glu", zg, dq))})
    d_wr, d_wi = comm.run(lambda carry: _tn_blocks("dw_r_i", xc, [dpr, dpi], HD, HD, carry))
    d_bbr, d_bbi = comm.run(lambda carry: _tn_blocks("d_bb", ua, [lr, li], NP, NS, carry))
    d_ccr, d_cci = comm.run(lambda carry: _tn_blocks("d_cc", dy, [sr, si], NP, NS, carry))
    comm.drain()
    sums = {"ple": acc_p, "ffn": acc_f, "mix": acc_g, "b_in": acc_b, "lru": acc_l, "s5": acc_s, "s5_a": acc_a}
    blocks = {"bb_re": d_bbr, "bb_im": d_bbi,
              "cc_re": d_ccr, "cc_im": d_cci,
              "w_r": d_wr, "w_i": d_wi}
    return gx, sums, blocks


def _replicated_grads(w, sums, blocks):
    grouped = lambda e, groups: jnp.transpose(e.reshape(e.shape[0], groups, -1), (1, 0, 2))
    d_ar, d_ai = sums["s5_a"][0].reshape(NG, NS), sums["s5_a"][1].reshape(NG, NS)
    d_bbr, d_bbi = grouped(blocks["bb_re"], NG), grouped(blocks["bb_im"], NG)
    _, vjp = jax.vjp(_s5_discretize, w["lam_re"], w["lam_im"], w["log_dt"], w["s5_b_re"], w["s5_b_im"])
    g = dict(zip(("lam_re", "lam_im", "log_dt", "s5_b_re", "s5_b_im"), vjp((d_ar, d_ai, d_bbr, d_bbi))))
    g["s5_c_re"] = grouped(blocks["cc_re"], NG)
    g["s5_c_im"] = -grouped(blocks["cc_im"], NG)
    g["w_r"], g["w_i"] = grouped(blocks["w_r"], NH), grouped(blocks["w_i"], NH)
    g["s5_d"] = sums["s5"][0].reshape(NG, NP)
    g["b_r"] = sums["lru"][1].reshape(NH, HD)
    g["b_i"] = sums["lru"][2].reshape(NH, HD)
    return g


ACC_ROWS = {"g_mix": ("mix", 0), "b_in": ("b_in", 0), "g_ffn": ("ffn", 0), "g_ple_gate": ("ple", 0),
            "b_ple_gate": ("ple", 1), "g_ple": ("ple", 2), "g_final": ("ple", 3), "b_glu": ("s5", 1),
            "lru_lambda": ("lru", 0), "conv_b": ("lru", 3)}
LOSS_ROW = ("ple", 4)
CONV_W_ROWS = ("lru", 4)


SHARDED = [("w_in", (D, QC)), ("w_glu", (S5W // NCHIP, S5W)), ("w_a_out", (S5W, AC)), ("w_b_out", (LW // NCHIP, D)),
           ("w_o", (D // NCHIP, D)), ("w_ffn_gate", (FC, D)), ("w_ffn_up", (FC, D)), ("w_ffn_down", (FC, D)),
           ("w_ple_gate", (D // NCHIP, D)), ("w_ple", (PLE, AC))]
NSH = len(SHARDED)
TRANSPOSED = ("w_ffn_gate", "w_ffn_up", "s5_b_re", "s5_b_im")
CONV_SHARD = (4, LW // NCHIP)


def _mesh_pos():
    return lax.axis_index("x"), lax.axis_index("y"), lax.axis_index("c")


def _other_chips(x, y):
    return [(1 - x, y), (x, 1 - y), (1 - x, 1 - y)]


def _half_rows(c, rows, align):
    return pl.ds(pl.multiple_of(c * (rows // 2), align), rows // 2)


def _run_now(name, carry):
    c_in, c_out = len(carry.operands), len(carry.out_shapes)

    def body(*refs):
        ins, outs, sems = refs[:c_in], refs[c_in:c_in + c_out], refs[c_in + c_out:]
        carry.start(ins, outs, sems)
        carry.finish(ins, outs, sems)

    return pl.pallas_call(body, name=name, in_specs=[ANY] * c_in, out_specs=[ANY] * c_out,
                          out_shape=list(carry.out_shapes), scratch_shapes=list(carry.sems),
                          input_output_aliases=dict(carry.aliases))(*_in_hbm(carry.operands))


def _gather_group(shards, split):
    n = len(shards)

    def copies(srcs, outs, sems):
        send_sems, recv_sems = sems
        x, y, c = _mesh_pos()
        k0 = 2 * x + y
        sib = (x, y, 1 - c)
        chips = _other_chips(x, y)

        def remote(src, dst, j, i, to):
            return pltpu.make_async_remote_copy(src_ref=src, dst_ref=dst, send_sem=send_sems.at[j, i],
                                                recv_sem=recv_sems.at[j, i], device_id=to, device_id_type=MESH)

        def rows(ref, i, core, *lead):
            if not split[i]:
                return ref.at[lead] if lead else ref
            return ref.at[(*lead, _half_rows(core, shards[i].shape[0], 16))]

        own = [remote(s, o.at[k0], 6, i, sib) for i, (s, o) in enumerate(zip(srcs, outs))]
        ici, landed, fwd, fwd_landed = [], [], [], []
        for j, chip in enumerate(chips):
            kj = 2 * chip[0] + chip[1]
            pairs = list(enumerate(zip(srcs, outs)))
            ici.append([remote(rows(s, i, c), rows(o, i, c, k0), j, i, (*chip, c)) for i, (s, o) in pairs])
            landed.append([remote(rows(s, i, c), rows(o, i, c, kj), j, i, (*chip, c)) for i, (s, o) in pairs])
            fwd.append([remote(rows(o, i, c, kj), rows(o, i, c, kj), 3 + j, i, sib) for i, (s, o) in pairs if split[i]])
            fwd_landed.append([remote(rows(o, i, 1 - c, kj), rows(o, i, 1 - c, kj), 3 + j, i, sib)
                               for i, (s, o) in pairs if split[i]])
        return own, ici, landed, fwd, fwd_landed

    def start(srcs, outs, sems):
        own, ici, _, _, _ = copies(srcs, outs, sems)
        for cp in own + [cp for per_chip in ici for cp in per_chip]:
            cp.start()

    def finish(srcs, outs, sems):
        own, ici, landed, fwd, fwd_landed = copies(srcs, outs, sems)
        passed = [i for i in range(n) if split[i]]
        for j in range(3):
            for i, cp in enumerate(landed[j]):
                cp.wait_recv()
                if split[i]:
                    fwd[j][passed.index(i)].start()
        for j in range(3):
            for cp in fwd_landed[j]:
                cp.wait_recv()
        for cp in own:
            cp.wait_recv()
        for cp in own + [cp for per_chip in ici + fwd for cp in per_chip]:
            cp.wait_send()

    return _Carried(shards, [_sds((NCHIP,) + s.shape, s.dtype) for s in shards],
                    [pltpu.SemaphoreType.DMA((7, n)), pltpu.SemaphoreType.DMA((7, n))], start, finish)


def _each_copy(copies, carried, out_shapes, sems, aliases=None):
    def start(ins, outs, sem_refs):
        for cp in copies(ins, outs, sem_refs):
            cp.start()

    def finish(ins, outs, sem_refs):
        for cp in copies(ins, outs, sem_refs):
            cp.wait()

    return _Carried(carried, out_shapes, sems, start, finish, aliases)


def _swap_group(grads):
    n = len(grads)

    def copies(srcs, outs, sems):
        send_sems, recv_sems = sems
        x, y, c = _mesh_pos()
        return [pltpu.make_async_remote_copy(src_ref=s.at[:, _half_rows(1 - c, s.shape[1], 8)], dst_ref=o,
                                             send_sem=send_sems.at[i], recv_sem=recv_sems.at[i], device_id=(x, y, 1 - c),
                                             device_id_type=MESH) for i, (s, o) in enumerate(zip(srcs, outs))]

    return _each_copy(copies, grads, [pltpu.HBM((NCHIP, g.shape[1] // 2, g.shape[2]), F32) for g in grads],
                      [pltpu.SemaphoreType.DMA((n,)), pltpu.SemaphoreType.DMA((n,))])


def _add_sibling_group(tag, kc_idx, grads, gots):
    n = len(grads)

    def body(kc_ref, *refs):
        for g, rx, p, pb in zip(refs[:n], refs[n:2 * n], refs[2 * n:3 * n], refs[3 * n:]):
            s = g[...] + rx[...]
            pb[...] = s.astype(BF)

            @pl.when(pl.program_id(0) == kc_ref[0])
            def _():
                p[...] = s

    halves = [pl.BlockSpec((None,) + rx.shape[1:], lambda k, kc_ref: (k, 0, 0)) for rx in gots]
    mine = [pl.BlockSpec((None,) + rx.shape[1:], lambda k, kc_ref: (k, kc_ref[1], 0)) for rx in gots]
    own = [pl.BlockSpec(rx.shape[1:], lambda k, kc_ref: (0, 0)) for rx in gots]
    outs = _pallas_call(
        body, name="add_sibling_" + tag,
        grid_spec=pltpu.PrefetchScalarGridSpec(num_scalar_prefetch=1, grid=(NCHIP,), in_specs=mine + halves,
                                               out_specs=own + halves),
        out_shape=[pltpu.HBM(rx.shape[1:], F32) for rx in gots] + [pltpu.HBM(rx.shape, BF) for rx in gots],
        compiler_params=_params(48),
    )(kc_idx, *_in_hbm(list(grads) + list(gots)))
    return outs[:n], outs[n:]


def _exchange_group(parts):
    n = len(parts)

    def copies(srcs, outs, sems):
        send_sems, recv_sems = sems
        x, y, c = _mesh_pos()
        return [pltpu.make_async_remote_copy(
            src_ref=s.at[2 * chip[0] + chip[1]], dst_ref=o.at[j], send_sem=send_sems.at[j, i],
            recv_sem=recv_sems.at[j, i], device_id=(*chip, c), device_id_type=MESH)
            for j, chip in enumerate(_other_chips(x, y)) for i, (s, o) in enumerate(zip(srcs, outs))]

    return _each_copy(copies, parts, [pltpu.HBM((3,) + p.shape[1:], BF) for p in parts],
                      [pltpu.SemaphoreType.DMA((3, n)), pltpu.SemaphoreType.DMA((3, n))])


def _add_chips_group(tag, kc_idx, parts, arrived):
    n = len(parts)

    def body(kc_ref, *refs):
        for p, rx, t in zip(refs[:n], refs[n:2 * n], refs[2 * n:]):
            t[...] = ((p[...] + rx[0].astype(F32)) + rx[1].astype(F32)) + rx[2].astype(F32)

    outs = _pallas_call(
        body, name="add_chips_" + tag,
        grid_spec=pltpu.PrefetchScalarGridSpec(
            num_scalar_prefetch=1, grid=(1,),
            in_specs=([pl.BlockSpec(rx.shape[1:], lambda i, kc_ref: (0, 0)) for rx in arrived]
                      + [pl.BlockSpec(rx.shape, lambda i, kc_ref: (0, 0, 0)) for rx in arrived]),
            out_specs=[pl.BlockSpec((None,) + rx.shape[1:], lambda i, kc_ref: (kc_ref[1], 0, 0)) for rx in arrived]),
        out_shape=[pltpu.HBM((2,) + rx.shape[1:], F32) for rx in arrived],
        compiler_params=_params(48),
    )(kc_idx, *_in_hbm(list(parts) + list(arrived)))
    return list(outs)


def _join_group(halves):
    n = len(halves)

    def copies(bufs, sems):
        send_sems, recv_sems = sems
        x, y, c = _mesh_pos()
        sib = (x, y, 1 - c)
        sends = [pltpu.make_async_remote_copy(src_ref=b.at[c], dst_ref=b.at[c], send_sem=send_sems.at[i],
                                              recv_sem=recv_sems.at[i], device_id=sib, device_id_type=MESH)
                 for i, b in enumerate(bufs)]
        landed = [pltpu.make_async_remote_copy(src_ref=b.at[c], dst_ref=b.at[1 - c], send_sem=send_sems.at[i],
                                               recv_sem=recv_sems.at[i], device_id=sib, device_id_type=MESH)
                  for i, b in enumerate(bufs)]
        return sends, landed

    def start(_, bufs, sems):
        for cp in copies(bufs, sems)[0]:
            cp.start()

    def finish(_, bufs, sems):
        sends, landed = copies(bufs, sems)
        for cp in landed:
            cp.wait_recv()
        for cp in sends:
            cp.wait_send()

    return _Carried(halves, [pltpu.HBM(h.shape, F32) for h in halves],
                    [pltpu.SemaphoreType.DMA((n,)), pltpu.SemaphoreType.DMA((n,))], start, finish,
                    {i: i for i in range(n)})


def _combine(carries):
    operands, out_shapes, sems, aliases, spans = [], [], [], {}, []
    for c in carries:
        aliases.update({len(operands) + i: len(out_shapes) + o for i, o in c.aliases.items()})
        spans.append((len(operands), len(out_shapes), len(sems)))
        operands += list(c.operands)
        out_shapes += list(c.out_shapes)
        sems += list(c.sems)

    def each(phase):
        def run(ins, outs, sem_refs):
            for c, (a, b, s) in zip(carries, spans):
                getattr(c, phase)(ins[a:a + len(c.operands)], outs[b:b + len(c.out_shapes)], sem_refs[s:s + len(c.sems)])
        return run

    return _Carried(operands, out_shapes, sems, each("start"), each("finish"), aliases)


def _allreduce_small(arrays, wire):
    n = len(arrays)
    halves = [(a.shape[0], a.shape[1] // 2) for a in arrays]

    def body(*refs):
        srcs, outs = refs[:n], refs[n:2 * n]
        mine_bufs, sib_bufs, chip_bufs, total_bufs = (refs[k * n:(k + 1) * n] for k in range(2, 6))
        send_sems, recv_sems, local_sems = refs[6 * n:]
        x, y, c = _mesh_pos()
        k0 = 2 * x + y
        sib = (x, y, 1 - c)

        def remote(src, dst, j, i, to):
            return pltpu.make_async_remote_copy(src_ref=src, dst_ref=dst, send_sem=send_sems.at[j, i],
                                                recv_sem=recv_sems.at[j, i], device_id=to, device_id_type=MESH)

        def cols(ref, i, core):
            return ref.at[:, pl.ds(pl.multiple_of(core * halves[i][1], LANE), halves[i][1])]

        swaps = [remote(cols(s, i, 1 - c), b, 0, i, sib) for i, (s, b) in enumerate(zip(srcs, sib_bufs))]
        own = [pltpu.make_async_copy(cols(s, i, c), m, local_sems.at[i]) for i, (s, m) in enumerate(zip(srcs, mine_bufs))]
        for cp in swaps + own:
            cp.start()
        for cp in swaps + own:
            cp.wait()
        for m, b, buf in zip(mine_bufs, sib_bufs, chip_bufs):
            buf[k0] = (m[...] + b[...]).astype(buf.dtype)
        chips = _other_chips(x, y)
        sends = [remote(buf.at[k0], buf.at[k0], 1 + j, i, (*chip, c))
                 for j, chip in enumerate(chips) for i, buf in enumerate(chip_bufs)]
        for cp in sends:
            cp.start()
        for j, chip in enumerate(chips):
            for i, buf in enumerate(chip_bufs):
                remote(buf.at[k0], buf.at[2 * chip[0] + chip[1]], 1 + j, i, (*chip, c)).wait_recv()
        for cp in sends:
            cp.wait_send()
        for t, buf in zip(total_bufs, chip_bufs):
            t[...] = ((buf[0].astype(F32) + buf[1].astype(F32)) + buf[2].astype(F32)) + buf[3].astype(F32)
        joins = [remote(t, cols(o, i, c), 4, i, sib) for i, (t, o) in enumerate(zip(total_bufs, outs))]
        keep = [pltpu.make_async_copy(t, cols(o, i, c), local_sems.at[i]) for i, (t, o) in enumerate(zip(total_bufs, outs))]
        for cp in joins + keep:
            cp.start()
        for i, (t, o) in enumerate(zip(total_bufs, outs)):
            remote(t, cols(o, i, 1 - c), 4, i, sib).wait_recv()
        for cp in joins:
            cp.wait_send()
        for cp in keep:
            cp.wait()

    specs = [_full(a.shape) for a in arrays]
    return _pallas_call(
        body, name="allreduce_small", grid=(1,), in_specs=specs, out_specs=specs,
        out_shape=[_sds(a.shape) for a in arrays],
        scratch_shapes=([pltpu.VMEM(h, F32) for h in halves] + [pltpu.VMEM(h, F32) for h in halves]
                        + [pltpu.VMEM((NCHIP,) + h, dt) for h, dt in zip(halves, wire)] + [pltpu.VMEM(h, F32) for h in halves]
                        + [pltpu.SemaphoreType.DMA((5, n)), pltpu.SemaphoreType.DMA((5, n)), pltpu.SemaphoreType.DMA((n,))]),
        compiler_params=_params(32),
    )(*arrays)


def _adamw_terms(w, g, m, v):
    m = ADAM_B1 * m + (1.0 - ADAM_B1) * g
    v = ADAM_B2 * v + (1.0 - ADAM_B2) * jnp.square(g)
    m_hat = m / (1.0 - ADAM_B1 ** ADAM_STEP)
    v_hat = v / (1.0 - ADAM_B2 ** ADAM_STEP)
    return -ADAM_LR * (m_hat / (jnp.sqrt(v_hat) + ADAM_EPS) + ADAM_WD * w), m, v


ADAM_STEPS = 4


def _adamw_group(tag, ws, gs, ms, vs):
    n = len(ws)

    def body(*refs):
        ins, outs = refs[:4 * n], refs[4 * n:]
        for i in range(n):
            w, g, m, v = (ins[k * n + i][...] for k in range(4))
            outs[i][...] = g
            outs[n + i][...], outs[2 * n + i][...], outs[3 * n + i][...] = _adamw_terms(w, g, m, v)

    specs = [pl.BlockSpec((w.shape[0] // ADAM_STEPS, w.shape[1]), lambda i: (i, 0)) for w in ws]
    outs = _pallas_call(
        body, name="adamw_" + tag, grid=(ADAM_STEPS,), in_specs=specs * 4, out_specs=specs * 4,
        out_shape=[_sds(w.shape) for w in ws] * 4, compiler_params=_params(48),
    )(*_in_hbm(list(ws) + list(gs) + list(ms) + list(vs)))
    return outs[:n], outs[n:2 * n], outs[2 * n:3 * n], outs[3 * n:]


def _adamw_replicated(sums, row_of, direct):
    ns, nr, nd = len(sums), len(row_of), len(direct)

    def body(*refs):
        sum_refs = refs[:ns]
        ins = refs[ns:ns + 3 * nr + 4 * nd]
        outs = refs[ns + 3 * nr + 4 * nd:]
        for i, (_, _, _, si, row) in enumerate(row_of):
            w_ref, m_ref, v_ref = ins[3 * i:3 * i + 3]
            g = sum_refs[si][row:row + 1, :]
            outs[4 * i][...] = g
            outs[4 * i + 1][...], outs[4 * i + 2][...], outs[4 * i + 3][...] = _adamw_terms(w_ref[...], g, m_ref[...], v_ref[...])
        for i in range(nd):
            w_ref, m_ref, v_ref, g_ref = ins[3 * nr + 4 * i:3 * nr + 4 * i + 4]
            o = outs[4 * (nr + i):4 * (nr + i) + 4]
            g = g_ref[...]
            o[0][...] = g
            o[1][...], o[2][...], o[3][...] = _adamw_terms(w_ref[...], g, m_ref[...], v_ref[...])

    operands = list(sums)
    shapes = []
    for w, m, v, _, _ in row_of:
        operands += [w, m, v]
        shapes += [w.shape] * 4
    for w, m, v, g in direct:
        operands += [w, m, v, g]
        shapes += [w.shape] * 4
    flat = _pallas_call(
        body, name="adamw_replicated", grid=(1,), in_specs=[_full(a.shape) for a in operands],
        out_specs=[_full(s) for s in shapes], out_shape=[_sds(s) for s in shapes],
        compiler_params=_params(56),
    )(*operands)
    return [flat[4 * i:4 * i + 4] for i in range(nr + nd)]


class _Exchanges:
    def __init__(self, shards, conv_w, chip, core, apply):
        self.shards, self.conv_w, self.apply = shards, conv_w, apply
        self.active, self.calls = [], 0
        self.core_idx = jnp.reshape(core, (1,)).astype(jnp.int32)
        self.chip_core_idx = jnp.stack([chip, core]).astype(jnp.int32)

    def first(self):
        names = ["w_in", "w_glu"]
        got = _run_now("gather_first", _gather_group([self.shards[n] for n in names] + [self.conv_w],
                                                     [True, True, False]))
        out = dict(zip(names, got))
        out["conv_w"] = jnp.transpose(got[2], (1, 0, 2)).reshape(4, LW)
        return out

    def gather(self, names):
        return _gather_group([self.shards[n] for n in names], [True] * len(names))

    def reduce(self, tag, grads):
        self.active.append({"tag": tag, "names": list(grads), "stage": 0, "grads": list(grads.values())})

    def run(self, call):
        groups = self.active
        carries = [self._exchange_of(g) for g in groups]
        carry = _combine(carries)
        outs = list(call(carry))
        own = len(outs) - len(carry.out_shapes)
        landed = outs[own:]
        for g, c in zip(groups, carries):
            self._sum_after(g, landed[:len(c.out_shapes)])
            landed = landed[len(c.out_shapes):]
        self.active = [g for g in groups if g["stage"] < 3]
        return outs[:own]

    def _exchange_of(self, g):
        if g["stage"] == 0:
            return _swap_group(g["grads"])
        if g["stage"] == 1:
            return _exchange_group(g["bf16"])
        return _join_group(g["halves"])

    def _sum_after(self, g, landed):
        if g["stage"] == 0:
            g["f32"], g["bf16"] = _add_sibling_group(g["tag"], self.chip_core_idx, g["grads"], landed)
        elif g["stage"] == 1:
            g["halves"] = _add_chips_group(g["tag"], self.chip_core_idx, g["f32"], landed)
        else:
            self.apply(g["tag"], g["names"], [t.reshape(2 * t.shape[1], t.shape[2]) for t in landed])
        g["stage"] += 1

    def drain(self):
        while self.active:
            self.calls += 1
            self.run(lambda carry: _run_now("reduce_%d" % self.calls, carry))


INPUT_NAMES = (["x", "p"] + [n for n in
               ["g_mix", "w_in", "b_in", "lam_re", "lam_im", "log_dt", "s5_b_re", "s5_b_im", "s5_c_re", "s5_c_im", "s5_d",
                "w_glu", "b_glu", "conv_w", "conv_b", "w_r", "b_r", "w_i", "b_i", "lru_lambda", "w_a_out", "w_b_out", "w_o",
                "g_ffn", "w_ffn_gate", "w_ffn_up", "w_ffn_down", "g_ple_gate", "w_ple_gate", "b_ple_gate", "w_ple", "g_ple",
                "g_final"]])
WEIGHT_NAMES = INPUT_NAMES[2:]


def kernel(*args):
    names = INPUT_NAMES + ["loss_target"] + ["m_" + n for n in WEIGHT_NAMES] + ["v_" + n for n in WEIGHT_NAMES]
    assert len(args) == len(names)
    given = dict(zip(names, args))

    def view(name):
        a = given[name]
        return jnp.swapaxes(a, -1, -2) if name.endswith(TRANSPOSED) else a

    def unview(name, a):
        return jnp.swapaxes(a, -1, -2) if name in TRANSPOSED else a

    def local(name):
        return view(name) if name.endswith("g_final") else view(name)[0]

    xi, yi, ci = _mesh_pos()
    k0 = 2 * xi + yi
    x, p, tgt = given["x"][0], given["p"][0, 0], given["loss_target"][0]

    results = {}

    row_halves = {}

    def apply(tag, names, totals):
        totals = dict(zip(names, totals))
        row_halves.update({n: totals.pop(n) for n in names if n in ("w_in_lo", "w_in_hi")})
        if len(row_halves) == 2:
            totals["w_in"] = jnp.concatenate([row_halves.pop("w_in_lo"), row_halves.pop("w_in_hi")])
        names = list(totals)
        if not names:
            return
        new = _adamw_group(tag, [local(n) for n in names], list(totals.values()), [local("m_" + n) for n in names],
                           [local("v_" + n) for n in names])
        for kind, arrays in zip(("grad", "delta", "new_m", "new_v"), new):
            for n, arr in zip(names, arrays):
                results[kind, n] = unview(n, arr[None])

    comm = _Exchanges({n: local(n).astype(BF) for n, _ in SHARDED}, local("conv_w"), k0, ci, apply)
    w = {n: local(n) for n in WEIGHT_NAMES if n != "conv_w" and n not in dict(SHARDED)}
    gx, sums, blocks = _local_step(x, p, tgt, w, comm)

    sum_names, block_names = list(sums), list(blocks)
    red = _allreduce_small([sums[n] for n in sum_names] + [blocks[n] for n in block_names],
                           [F32] * len(sum_names) + [BF] * len(block_names))
    sums = dict(zip(sum_names, red[:len(sum_names)]))
    blocks = dict(zip(block_names, red[len(sum_names):]))
    loss = jnp.sum(sums[LOSS_ROW[0]][LOSS_ROW[1]])
    direct_g = _replicated_grads(w, sums, blocks)
    conv_rows = sums[CONV_W_ROWS[0]][CONV_W_ROWS[1]:CONV_W_ROWS[1] + 4]
    direct_g["conv_w"] = lax.dynamic_slice(conv_rows, (0, k0 * CONV_SHARD[1]), CONV_SHARD)
    as_row = lambda a: a.reshape(1, -1)
    row_names = list(ACC_ROWS)
    row_of = [(as_row(given[n]), as_row(given["m_" + n]), as_row(given["v_" + n]),
               sum_names.index(ACC_ROWS[n][0]), ACC_ROWS[n][1]) for n in row_names]
    direct_names = list(direct_g)
    direct = [(view(n), view("m_" + n), view("v_" + n), direct_g[n].reshape(view(n).shape)) for n in direct_names]
    done = _adamw_replicated([sums[n] for n in sum_names], row_of, direct)
    for n, four in zip(row_names + direct_names, done):
        for kind, arr in zip(("grad", "delta", "new_m", "new_v"), four):
            results[kind, n] = unview(n, arr).reshape(given[n].shape)

    out = [loss, gx[None]]
    for kind in ("grad", "delta", "new_m", "new_v"):
        out += [results[kind, n] for n in WEIGHT_NAMES]
    return tuple(out)
```

```python
import functools
import math

import jax
import jax.numpy as jnp
from jax import lax
from jax.experimental import pallas as pl
from jax.experimental.pallas import tpu as pltpu

F32 = jnp.float32
BF = jnp.bfloat16

D = 1024
S5W = 512
NG, NS, NP = 32, 64, 16
GN = NG * NS
LW = 1024
NH, HD = 16, 64
LRU_C = 8.0
FH = 2816
NCHIP = 4
FC = FH // NCHIP
PLE = 256
INC = S5W + LW + 2 * D
EPS = 1e-6
ADAM_LR, ADAM_B1, ADAM_B2, ADAM_EPS, ADAM_WD, ADAM_STEP = 0.001, 0.9, 0.999, 1e-08, 0.01, 10

TM = 256
TK = 1024
LC = 512
SUB = 8
VMEM_MB = 1024 * 1024
MESH = pl.DeviceIdType.MESH
ANY = pl.BlockSpec(memory_space=pl.ANY)


def _mm(a, b):
    return jnp.dot(a.astype(BF), b.astype(BF), preferred_element_type=F32)


def _mm_nt(a, b):
    return lax.dot_general(a.astype(BF), b.astype(BF), (((1,), (1,)), ((), ())), preferred_element_type=F32)


def _mm_tn(a, b):
    return lax.dot_general(a.astype(BF), b.astype(BF), (((0,), (0,)), ((), ())), preferred_element_type=F32)


def _blockdiag_mm(x, blocks_ref):
    n, rows, _ = blocks_ref.shape
    return jnp.concatenate([jnp.dot(x[:, j * rows:(j + 1) * rows], blocks_ref[j], preferred_element_type=F32)
                            for j in range(n)], axis=1)


def _blockdiag_mm_t(x, blocks_ref):
    n, _, wide = blocks_ref.shape
    return jnp.concatenate([lax.dot_general(x[:, j * wide:(j + 1) * wide], blocks_ref[j], (((1,), (1,)), ((), ())),
                                            preferred_element_type=F32) for j in range(n)], axis=1)


def _rms(x):
    r = lax.rsqrt(jnp.mean(x * x, axis=-1, keepdims=True) + EPS)
    return x * r, r


def _rms_bwd(dy, xh, r, g):
    dxh = dy * g
    return r * (dxh - xh * jnp.mean(dxh * xh, axis=-1, keepdims=True))


def _colsum(x):
    return jnp.sum(x, axis=0, keepdims=True)


def _sig(x):
    return jax.nn.sigmoid(x)


def _gelu_grad(x):
    c = math.sqrt(2.0 / math.pi)
    t = jnp.tanh(c * (x + 0.044715 * x * x * x))
    return 0.5 * (1.0 + t) + 0.5 * x * (1.0 - t * t) * c * (1.0 + 3.0 * 0.044715 * x * x)


def _neg_expm1(x):
    series = -x * (1.0 + x * (0.5 + x * (1.0 / 6.0 + x * (1.0 / 24.0))))
    return jnp.where(x > -0.03, series, 1.0 - jnp.exp(x))


def _tok(width):
    return pl.BlockSpec((TM, width), lambda i: (i, 0))


def _tok_rev(width, nt):
    return pl.BlockSpec((TM, width), lambda i: (nt - 1 - i, 0))


def _full(shape):
    return pl.BlockSpec(shape, lambda i: (0,) * len(shape))


def _params(vmem_mb, **kw):
    return pltpu.CompilerParams(dimension_semantics=("arbitrary",), vmem_limit_bytes=vmem_mb * VMEM_MB, **kw)


def _sds(shape, dtype=F32):
    return jax.ShapeDtypeStruct(shape, dtype)


class _Carried:
    def __init__(self, operands, out_shapes, sems, start, finish, aliases=None):
        self.operands, self.out_shapes, self.sems = list(operands), list(out_shapes), list(sems)
        self.start, self.finish, self.aliases = start, finish, dict(aliases or {})


def _in_hbm(arrays):
    return [pltpu.with_memory_space_constraint(a, pltpu.HBM) for a in arrays]


def _pallas_call(body, carry=None, **kw):
    if carry is None:
        return pl.pallas_call(body, **kw)

    def at_step(corner):
        hit = [pl.program_id(d) == (size - 1 if corner else 0) for d, size in enumerate(kw["grid"])]
        return functools.reduce(jnp.logical_and, hit)

    name, grid, compiler_params = kw["name"], kw["grid"], kw["compiler_params"]
    in_specs, out_specs, out_shape = list(kw["in_specs"]), list(kw["out_specs"]), list(kw["out_shape"])
    scratch_shapes = list(kw.get("scratch_shapes", ()))
    n_in, n_out, n_scr = len(in_specs), len(out_specs), len(scratch_shapes)
    c_in, c_out = len(carry.operands), len(carry.out_shapes)

    def full_body(*refs):
        ins, refs = refs[:n_in], refs[n_in:]
        c_ins, refs = refs[:c_in], refs[c_in:]
        outs, refs = refs[:n_out], refs[n_out:]
        c_outs, refs = refs[:c_out], refs[c_out:]
        scratch, c_sems = refs[:n_scr], refs[n_scr:]

        @pl.when(at_step(0))
        def _():
            carry.start(c_ins, c_outs, c_sems)

        body(*ins, *outs, *scratch)

        @pl.when(at_step(1))
        def _():
            carry.finish(c_ins, c_outs, c_sems)

    call = pl.pallas_call(
        full_body, name=name, grid=grid, in_specs=in_specs + [ANY] * c_in, out_specs=out_specs + [ANY] * c_out,
        out_shape=out_shape + list(carry.out_shapes), scratch_shapes=scratch_shapes + list(carry.sems),
        input_output_aliases={n_in + i: n_out + o for i, o in carry.aliases.items()},
        compiler_params=compiler_params)
    return lambda *operands: call(*operands, *_in_hbm(carry.operands))


def _resident(pairs, sems):
    first = pl.program_id(0) == 0
    copies = [pltpu.make_async_copy(src, dst, sems.at[j]) for j, (src, dst) in enumerate(pairs)]

    @pl.when(first)
    def _():
        for cp in copies:
            cp.start()

    def wait(j):
        @pl.when(first)
        def _():
            copies[j].wait()

    return wait


def _resident_now(pairs, sems):
    @pl.when(pl.program_id(0) == 0)
    def _():
        copies = [pltpu.make_async_copy(src, dst, sems.at[j]) for j, (src, dst) in enumerate(pairs)]
        for cp in copies:
            cp.start()
        for cp in copies:
            cp.wait()


def _row_iota(width):
    return lax.broadcasted_iota(jnp.int32, (SUB, width), 0)


def _bcast_row(x, row):
    return jnp.broadcast_to(x[row:row + 1, :], x.shape)


def _slab(k):
    return pl.ds(pl.multiple_of(k * SUB, SUB), SUB)


QC = INC // NCHIP
Z_PARTS = ((0, S5W), (S5W, S5W + LW), (S5W + LW, INC))


def _inproj_fwd(x, g_mix, w_in, b_in, carry=None):
    L = x.shape[0]

    def body(x_ref, g_ref, w_hbm, b_ref, h_ref, ua_ref, ub_ref, gp_ref, w_vm, w_sems):
        _resident_now([(w_hbm.at[k], w_vm.at[k]) for k in range(NCHIP)], w_sems)
        xh, _ = _rms(x_ref[...])
        h = (xh * g_ref[...]).astype(BF)
        h_ref[...] = h
        for k in range(NCHIP):
            lo, hi = k * QC, (k + 1) * QC
            z = jnp.dot(h, w_vm[k], preferred_element_type=F32) + b_ref[:, lo:hi]
            for ref, (a, b) in zip((ua_ref, ub_ref, gp_ref), Z_PARTS):
                s, e = max(lo, a), min(hi, b)
                if s < e:
                    ref[:, s - a:e - a] = z[:, s - lo:e - lo]

    return _pallas_call(
        body, carry, name="inproj_fwd", grid=(L // TM,),
        in_specs=[_tok(D), _full((1, D)), ANY, _full((1, INC))],
        out_specs=[_tok(D), _tok(S5W), _tok(LW), _tok(2 * D)],
        out_shape=[_sds((L, D), BF), _sds((L, S5W)), _sds((L, LW)), _sds((L, 2 * D))],
        scratch_shapes=[pltpu.VMEM((NCHIP, D, QC), BF), pltpu.SemaphoreType.DMA((NCHIP,))],
        compiler_params=_params(40),
    )(x, g_mix, w_in, b_in)


def _inproj_bwd(x, dx1, dua, dub, dgp, g_mix, w_in, carry=None):
    L = x.shape[0]

    def body(x_ref, dx1_ref, dua_ref, dub_ref, dgp_ref, g_ref, w_hbm, gx_ref, dz_ref, dg_ref, db_ref, w_vm, w_sems):
        _resident_now([(w_hbm.at[k], w_vm.at[k]) for k in range(NCHIP)], w_sems)

        @pl.when(pl.program_id(0) == 0)
        def _():
            dg_ref[...] = jnp.zeros_like(dg_ref)
            db_ref[...] = jnp.zeros_like(db_ref)

        for src, (a, b) in zip((dua_ref, dub_ref, dgp_ref), Z_PARTS):
            d = src[...]
            dz_ref[:, a:b] = d.astype(BF)
            db_ref[0:1, a:b] += _colsum(d)
        dh = jnp.zeros((TM, D), F32)
        for k in range(NCHIP):
            dh = dh + lax.dot_general(dz_ref[:, k * QC:(k + 1) * QC], w_vm[k], (((1,), (1,)), ((), ())),
                                      preferred_element_type=F32)
        xh, r = _rms(x_ref[...])
        dg_ref[0:1, :] += _colsum(dh * xh)
        gx_ref[...] = dx1_ref[...] + _rms_bwd(dh, xh, r, g_ref[...])

    return _pallas_call(
        body, carry, name="inproj_bwd", grid=(L // TM,),
        in_specs=[_tok(D), _tok(D), _tok(S5W), _tok(LW), _tok(2 * D), _full((1, D)), ANY],
        out_specs=[_tok(D), _tok(INC), _full((SUB, D)), _full((SUB, INC))],
        out_shape=[_sds((L, D)), _sds((L, INC), BF), _sds((SUB, D)), _sds((SUB, INC))],
        scratch_shapes=[pltpu.VMEM((NCHIP, D, QC), BF), pltpu.SemaphoreType.DMA((NCHIP,))],
        compiler_params=_params(40),
    )(x, dx1, dua, dub, dgp, g_mix, w_in)


def _cscan(xr_ref, xi_ref, con_ref, cr_ref, ci_ref, reverse):
    n_slab = xr_ref.shape[0] // SUB
    width = xr_ref.shape[1]
    for lc in range(width // LC):
        cols = slice(lc * LC, (lc + 1) * LC)
        con = [con_ref[SUB * j:SUB * (j + 1), cols] for j in range(8)]

        def step(k, carry, cols=cols, con=con):
            cr, ci = carry
            rows = _slab(n_slab - 1 - k if reverse else k)
            xr, xi = xr_ref[rows, cols], xi_ref[rows, cols]
            for j, sh in enumerate((1, 2, 4)):
                mr, mi = con[2 * j], con[2 * j + 1]
                pr = pltpu.roll(xr, SUB - sh if reverse else sh, 0)
                pi = pltpu.roll(xi, SUB - sh if reverse else sh, 0)
                xr, xi = xr + mr * pr - mi * pi, xi + mr * pi + mi * pr
            xr, xi = xr + con[6] * cr - con[7] * ci, xi + con[6] * ci + con[7] * cr
            xr_ref[rows, cols] = xr
            xi_ref[rows, cols] = xi
            row = 0 if reverse else SUB - 1
            return _bcast_row(xr, row), _bcast_row(xi, row)

        cr, ci = lax.fori_loop(0, n_slab, step, (cr_ref[:, cols], ci_ref[:, cols]))
        cr_ref[:, cols] = cr
        ci_ref[:, cols] = ci


def _s5_fwd(ua, bbr, bbi, ccr, cci, dsk, con, w_glu, b_glu, carry=None):
    L = ua.shape[0]

    def body(ua_ref, bbr_hbm, bbi_hbm, ccr_hbm, cci_hbm, dsk_ref, con_ref, wg_ref, bg_ref,
             sr_ref, si_ref, y_ref, zg_ref, ya_ref, bbr_vm, bbi_vm, ccr_vm, cci_vm, cr_ref, ci_ref, w_sems):
        landed = _resident([(bbr_hbm, bbr_vm), (bbi_hbm, bbi_vm), (ccr_hbm, ccr_vm), (cci_hbm, cci_vm)], w_sems)

        @pl.when(pl.program_id(0) == 0)
        def _():
            cr_ref[...] = jnp.zeros_like(cr_ref)
            ci_ref[...] = jnp.zeros_like(ci_ref)

        u = ua_ref[...]
        ub = u.astype(BF)
        landed(0)
        sr_ref[...] = _blockdiag_mm(ub, bbr_vm)
        landed(1)
        si_ref[...] = _blockdiag_mm(ub, bbi_vm)
        _cscan(sr_ref, si_ref, con_ref, cr_ref, ci_ref, reverse=False)
        landed(2)
        landed(3)
        y = (_blockdiag_mm_t(sr_ref[...].astype(BF), ccr_vm) - _blockdiag_mm_t(si_ref[...].astype(BF), cci_vm)
             + dsk_ref[...] * u)
        y_ref[...] = y
        zg = jax.nn.gelu(y)
        zg_ref[...] = zg.astype(BF)
        q = _mm(zg, wg_ref[...]) + bg_ref[...]
        ya_ref[...] = (zg * _sig(q)).astype(BF)

    return _pallas_call(
        body, carry, name="s5_fwd", grid=(L // TM,),
        in_specs=[_tok(S5W), ANY, ANY, ANY, ANY, _full((1, S5W)), _full((8 * SUB, GN)),
                  _full((S5W, S5W)), _full((1, S5W))],
        out_specs=[_tok(GN), _tok(GN), _tok(S5W), _tok(S5W), _tok(S5W)],
        out_shape=[_sds((L, GN)), _sds((L, GN)), _sds((L, S5W)), _sds((L, S5W), BF), _sds((L, S5W), BF)],
        scratch_shapes=[pltpu.VMEM((S5W // 128, 128, GN // (S5W // 128)), BF)] * 4 + [
                        pltpu.VMEM((SUB, GN), F32), pltpu.VMEM((SUB, GN), F32),
                        pltpu.SemaphoreType.DMA((4,))],
        compiler_params=_params(44),
    )(ua, bbr, bbi, ccr, cci, dsk, con, w_glu, b_glu)


def _s5_bwd(dya, y, ua, sr, si, bbr, bbi, ccr, cci, dsk, con_rev, w_glu, b_glu, carry=None):
    L = ua.shape[0]
    nt = L // TM
    spt = TM // SUB
    n_slab = spt

    def halo_map(i):
        return (jnp.maximum((nt - 1 - i) * spt - 1, 0), 0)

    def body(dya_ref, y_ref, ua_ref, sr_ref, si_ref, hr_ref, hi_ref, bbr_hbm, bbi_hbm, ccr_hbm, cci_hbm,
             dsk_ref, con_ref, wg_ref, bg_ref,
             dua_ref, dq_ref, dy_ref, lr_ref, li_ref, da_ref, dsm_ref,
             bbr_vm, bbi_vm, ccr_vm, cci_vm, cr_ref, ci_ref, w_sems):
        i = pl.program_id(0)
        landed = _resident([(ccr_hbm, ccr_vm), (cci_hbm, cci_vm), (bbr_hbm, bbr_vm), (bbi_hbm, bbi_vm)], w_sems)

        @pl.when(i == 0)
        def _():
            cr_ref[...] = jnp.zeros_like(cr_ref)
            ci_ref[...] = jnp.zeros_like(ci_ref)
            da_ref[...] = jnp.zeros_like(da_ref)
            dsm_ref[...] = jnp.zeros_like(dsm_ref)

        u = ua_ref[...]
        yv = y_ref[...]
        dya = dya_ref[...]
        zg = jax.nn.gelu(yv)
        sg = _sig(_mm(zg, wg_ref[...]) + bg_ref[...])
        dq = dya * zg * sg * (1.0 - sg)
        dq_ref[...] = dq.astype(BF)
        dzg = dya * sg + _mm_nt(dq, wg_ref[...])
        dy = dzg * _gelu_grad(yv)
        dyb = dy.astype(BF)
        dy_ref[...] = dyb
        dsm_ref[0:1, :] += _colsum(dy * u)
        dsm_ref[1:2, :] += _colsum(dq)
        landed(0)
        lr_ref[...] = _blockdiag_mm(dyb, ccr_vm)
        landed(1)
        li_ref[...] = -_blockdiag_mm(dyb, cci_vm)
        _cscan(lr_ref, li_ref, con_ref, cr_ref, ci_ref, reverse=True)

        first_tile = (i == nt - 1)
        row = _row_iota(LC)
        for lc in range(GN // LC):
            cols = slice(lc * LC, (lc + 1) * LC)
            h_r = jnp.where(first_tile, 0.0, hr_ref[:, cols])
            h_i = jnp.where(first_tile, 0.0, hi_ref[:, cols])

            def step(k, acc, cols=cols, h_r=h_r, h_i=h_i):
                ar, ai = acc
                rows = _slab(k)
                prev = _slab(jnp.maximum(k - 1, 0))
                pr = jnp.where(k == 0, h_r, sr_ref[prev, cols])
                pi = jnp.where(k == 0, h_i, si_ref[prev, cols])
                spr = pltpu.roll(jnp.where(row == SUB - 1, pr, sr_ref[rows, cols]), 1, 0)
                spi = pltpu.roll(jnp.where(row == SUB - 1, pi, si_ref[rows, cols]), 1, 0)
                lr, li = lr_ref[rows, cols], li_ref[rows, cols]
                return ar + lr * spr + li * spi, ai + li * spr - lr * spi

            zero = jnp.zeros((SUB, LC), F32)
            ar, ai = lax.fori_loop(0, n_slab, step, (zero, zero))
            da_ref[0:1, cols] += _colsum(ar)
            da_ref[1:2, cols] += _colsum(ai)

        landed(2)
        landed(3)
        dua_ref[...] = (dy * dsk_ref[...] + _blockdiag_mm_t(lr_ref[...].astype(BF), bbr_vm)
                        + _blockdiag_mm_t(li_ref[...].astype(BF), bbi_vm))

    return _pallas_call(
        body, carry, name="s5_bwd", grid=(nt,),
        in_specs=[_tok_rev(S5W, nt), _tok_rev(S5W, nt), _tok_rev(S5W, nt), _tok_rev(GN, nt), _tok_rev(GN, nt),
                  pl.BlockSpec((SUB, GN), halo_map), pl.BlockSpec((SUB, GN), halo_map),
                  ANY, ANY, ANY, ANY, _full((1, S5W)), _full((8 * SUB, GN)), _full((S5W, S5W)), _full((1, S5W))],
        out_specs=[_tok_rev(S5W, nt), _tok_rev(S5W, nt), _tok_rev(S5W, nt), _tok_rev(GN, nt), _tok_rev(GN, nt),
                   _full((SUB, GN)), _full((SUB, S5W))],
        out_shape=[_sds((L, S5W)), _sds((L, S5W), BF), _sds((L, S5W), BF), _sds((L, GN)), _sds((L, GN)),
                   _sds((SUB, GN)), _sds((SUB, S5W))],
        scratch_shapes=[pltpu.VMEM((S5W // 128, 128, GN // (S5W // 128)), BF)] * 4 + [
                        pltpu.VMEM((SUB, GN), F32), pltpu.VMEM((SUB, GN), F32),
                        pltpu.SemaphoreType.DMA((4,))],
        compiler_params=_params(52),
    )(dya, y, ua, sr, si, sr, si, bbr, bbi, ccr, cci, dsk, con_rev, w_glu, b_glu)


def _lru_gate_terms(rg, sp):
    log_a = -LRU_C * rg * sp
    a = jnp.exp(log_a)
    mult = jnp.sqrt(_neg_expm1(2.0 * log_a))
    return a, mult


def _lru_fwd(ub, conv_w, conv_b, wr, wi, b_r, b_i, sp, carry=None):
    L = ub.shape[0]
    n_slab = TM // SUB

    def body(ub_ref, cw_ref, cb_ref, wr_ref, wi_ref, br_ref, bi_ref, sp_ref,
             xc_ref, rg_ref, ig_ref, h_ref, hp_ref, a_ref, halo_ref, carry_ref):
        @pl.when(pl.program_id(0) == 0)
        def _():
            halo_ref[...] = jnp.zeros_like(halo_ref)
            carry_ref[...] = jnp.zeros_like(carry_ref)

        row = _row_iota(LW)
        taps = [cw_ref[k:k + 1, :] for k in range(4)]
        cb = cb_ref[...]

        def conv_step(k, prev):
            rows = _slab(k)
            cur = ub_ref[rows, :]
            acc = taps[3] * cur + cb
            for j in (1, 2, 3):
                acc = acc + taps[3 - j] * pltpu.roll(jnp.where(row >= SUB - j, prev, cur), j, 0)
            xc_ref[rows, :] = acc
            return cur

        halo_ref[...] = lax.fori_loop(0, n_slab, conv_step, halo_ref[...])

        xc = xc_ref[...]
        xcb = xc.astype(BF)
        rg = _sig(_blockdiag_mm(xcb, wr_ref) + br_ref[...])
        ig = _sig(_blockdiag_mm(xcb, wi_ref) + bi_ref[...])
        rg_ref[...] = rg
        ig_ref[...] = ig
        a, mult = _lru_gate_terms(rg, sp_ref[...])
        a_ref[...] = a
        h_ref[...] = mult * ig * xc

        rowc = _row_iota(LC)
        for lc in range(LW // LC):
            cols = slice(lc * LC, (lc + 1) * LC)

            def step(k, c, cols=cols):
                rows = _slab(k)
                av, b = a_ref[rows, cols], h_ref[rows, cols]
                for sh in (1, 2, 4):
                    keep = rowc >= sh
                    b = b + av * jnp.where(keep, pltpu.roll(b, sh, 0), 0.0)
                    av = av * jnp.where(keep, pltpu.roll(av, sh, 0), 1.0)
                h = b + av * c
                h_ref[rows, cols] = h
                hp_ref[rows, cols] = jnp.where(rowc == 0, c, pltpu.roll(h, 1, 0))
                return _bcast_row(h, SUB - 1)

            carry_ref[:, cols] = lax.fori_loop(0, n_slab, step, carry_ref[:, cols])

    return _pallas_call(
        body, carry, name="lru_fwd", grid=(L // TM,),
        in_specs=[_tok(LW), _full((4, LW)), _full((1, LW)), _full((LW // 128, 128, 128)), _full((LW // 128, 128, 128)),
                  _full((1, LW)), _full((1, LW)), _full((1, LW))],
        out_specs=[_tok(LW)] * 5,
        out_shape=[_sds((L, LW))] * 5,
        scratch_shapes=[pltpu.VMEM((TM, LW), F32), pltpu.VMEM((SUB, LW), F32), pltpu.VMEM((SUB, LW), F32)],
        compiler_params=_params(40),
    )(ub, conv_w, conv_b, wr, wi, b_r, b_i, sp)


def _lru_bwd(dyb, xc, rg, ig, hp, ub, conv_w, wr, wi, sp, dsp, carry=None):
    L = ub.shape[0]
    nt = L // TM
    spt = TM // SUB
    n_slab = spt

    def halo_map(i):
        return (jnp.maximum((nt - 1 - i) * spt - 1, 0), 0)

    def body(dh_ref, xc_ref, rg_ref, ig_ref, hp_ref, ub_ref, uh_ref, cw_ref, wr_ref, wi_ref, sp_ref, dsp_ref,
             dub_ref, dpr_ref, dpi_ref, acc_ref, a_ref, lam_ref, dxc_ref, carry_ref, next_ref):
        i = pl.program_id(0)

        @pl.when(i == 0)
        def _():
            carry_ref[...] = jnp.zeros_like(carry_ref)
            next_ref[...] = jnp.zeros_like(next_ref)
            acc_ref[...] = jnp.zeros_like(acc_ref)

        sp = sp_ref[...]
        rg, ig, xc = rg_ref[...], ig_ref[...], xc_ref[...]
        a, mult = _lru_gate_terms(rg, sp)
        a_ref[...] = a

        rowc = _row_iota(LC)
        for lc in range(LW // LC):
            cols = slice(lc * LC, (lc + 1) * LC)

            def step(k, c, cols=cols):
                rows = _slab(n_slab - 1 - k)
                av, dh = a_ref[rows, cols], dh_ref[rows, cols]
                b = av * dh
                for sh in (1, 2, 4):
                    keep = rowc < SUB - sh
                    b = b + av * jnp.where(keep, pltpu.roll(b, SUB - sh, 0), 0.0)
                    av = av * jnp.where(keep, pltpu.roll(av, SUB - sh, 0), 1.0)
                mu = b + av * c
                lam_ref[rows, cols] = dh + jnp.where(rowc == SUB - 1, c, pltpu.roll(mu, SUB - 1, 0))
                return _bcast_row(mu, 0)

            carry_ref[:, cols] = lax.fori_loop(0, n_slab, step, carry_ref[:, cols])

        lam = lam_ref[...]
        d_a = lam * hp_ref[...]
        d_mult = lam * ig * xc
        d_ig = lam * mult * xc
        dxc = lam * mult * ig
        d_log_a = d_a * a - d_mult * a * a / mult
        d_rg = (-LRU_C) * sp * d_log_a
        acc_ref[0:1, :] += _colsum((-LRU_C) * rg * d_log_a) * dsp_ref[...]
        dpr = d_rg * rg * (1.0 - rg)
        dpi = d_ig * ig * (1.0 - ig)
        acc_ref[1:2, :] += _colsum(dpr)
        acc_ref[2:3, :] += _colsum(dpi)
        dprb, dpib = dpr.astype(BF), dpi.astype(BF)
        dpr_ref[...] = dprb
        dpi_ref[...] = dpib
        dxc = dxc + _blockdiag_mm_t(dprb, wr_ref) + _blockdiag_mm_t(dpib, wi_ref)
        dxc_ref[...] = dxc
        acc_ref[3:4, :] += _colsum(dxc)

        row = _row_iota(LW)
        taps = [cw_ref[k:k + 1, :] for k in range(4)]
        u_halo = jnp.where(i == nt - 1, 0.0, uh_ref[...])
        nxt_tile = next_ref[...]

        def conv_step(k, accs):
            rows = _slab(k)
            cur = dxc_ref[rows, :]
            nxt = jnp.where(k == n_slab - 1, nxt_tile, dxc_ref[_slab(jnp.minimum(k + 1, n_slab - 1)), :])
            ucur = ub_ref[rows, :]
            uprev = jnp.where(k == 0, u_halo, ub_ref[_slab(jnp.maximum(k - 1, 0)), :])
            du = taps[3] * cur
            new = [accs[3] + cur * ucur]
            for j in (1, 2, 3):
                du = du + taps[3 - j] * pltpu.roll(jnp.where(row < j, nxt, cur), SUB - j, 0)
                new.append(accs[3 - j] + cur * pltpu.roll(jnp.where(row >= SUB - j, uprev, ucur), j, 0))
            dub_ref[rows, :] = du
            return tuple(new[::-1])

        zero = jnp.zeros((SUB, LW), F32)
        accs = lax.fori_loop(0, n_slab, conv_step, (zero, zero, zero, zero))
        for k in range(4):
            acc_ref[4 + k:5 + k, :] += _colsum(accs[k])
        next_ref[...] = dxc_ref[0:SUB, :]

    return _pallas_call(
        body, carry, name="lru_bwd", grid=(nt,),
        in_specs=[_tok_rev(LW, nt)] * 6 + [pl.BlockSpec((SUB, LW), halo_map), _full((4, LW)),
                                           _full((LW // 128, 128, 128)), _full((LW // 128, 128, 128)), _full((1, LW)), _full((1, LW))],
        out_specs=[_tok_rev(LW, nt), _tok_rev(LW, nt), _tok_rev(LW, nt), _full((SUB, LW))],
        out_shape=[_sds((L, LW)), _sds((L, LW), BF), _sds((L, LW), BF), _sds((SUB, LW))],
        scratch_shapes=[pltpu.VMEM((TM, LW), F32), pltpu.VMEM((TM, LW), F32), pltpu.VMEM((TM, LW), F32),
                        pltpu.VMEM((SUB, LW), F32), pltpu.VMEM((SUB, LW), F32)],
        compiler_params=_params(48),
    )(dyb, xc, rg, ig, hp, ub, ub, conv_w, wr, wi, sp, dsp)


AC = D // NCHIP


def _merge_fwd(x, ya, yb, gp, w_a, w_b, w_o, carry=None):
    L = x.shape[0]

    def body(x_ref, ya_ref, yb_ref, gp_ref, wa_ref, wb_ref, wo_ref, x1_ref, pa_ref, pb_ref, mg_ref):
        ya = ya_ref[...]
        for k in range(NCHIP):
            pa_ref[:, k * AC:(k + 1) * AC] = jnp.dot(ya, wa_ref[k], preferred_element_type=F32)
        pb = _mm(yb_ref[...], wb_ref[...])
        pb_ref[...] = pb
        gp = gp_ref[...]
        merged = (_sig(gp[:, :D]) * pa_ref[...] + _sig(gp[:, D:]) * pb).astype(BF)
        mg_ref[...] = merged
        x1_ref[...] = x_ref[...] + jnp.dot(merged, wo_ref[...], preferred_element_type=F32)

    return _pallas_call(
        body, carry, name="merge_fwd", grid=(L // TM,),
        in_specs=[_tok(D), _tok(S5W), _tok(LW), _tok(2 * D), _full((NCHIP, S5W, AC)), _full((LW, D)), _full((D, D))],
        out_specs=[_tok(D), _tok(D), _tok(D), _tok(D)],
        out_shape=[_sds((L, D)), _sds((L, D)), _sds((L, D)), _sds((L, D), BF)],
        compiler_params=_params(40),
    )(x, ya, yb, gp, w_a, w_b, w_o)


def _merge_bwd(dx1, gp, pa, pb, w_a, w_b, w_o, carry=None):
    L = dx1.shape[0]

    def body(dx1_ref, gp_ref, pa_ref, pb_ref, wa_ref, wb_ref, wo_ref, dya_ref, dyb_ref, dgp_ref, dpa_ref, dpb_ref):
        dm = _mm_nt(dx1_ref[...], wo_ref[...])
        gp = gp_ref[...]
        sa, sb = _sig(gp[:, :D]), _sig(gp[:, D:])
        dpa = (dm * sa).astype(BF)
        dpb = (dm * sb).astype(BF)
        dpa_ref[...] = dpa
        dpb_ref[...] = dpb
        dgp_ref[:, :D] = dm * pa_ref[...] * sa * (1.0 - sa)
        dgp_ref[:, D:] = dm * pb_ref[...] * sb * (1.0 - sb)
        dya = jnp.zeros((TM, S5W), F32)
        for k in range(NCHIP):
            dya = dya + _mm_nt(dpa[:, k * AC:(k + 1) * AC], wa_ref[k])
        dya_ref[...] = dya
        dyb_ref[...] = _mm_nt(dpb, wb_ref[...])

    return _pallas_call(
        body, carry, name="merge_bwd", grid=(L // TM,),
        in_specs=[_tok(D), _tok(2 * D), _tok(D), _tok(D), _full((NCHIP, S5W, AC)), _full((LW, D)), _full((D, D))],
        out_specs=[_tok(S5W), _tok(LW), _tok(2 * D), _tok(D), _tok(D)],
        out_shape=[_sds((L, S5W)), _sds((L, LW)), _sds((L, 2 * D)), _sds((L, D), BF), _sds((L, D), BF)],
        compiler_params=_params(40),
    )(dx1, gp, pa, pb, w_a, w_b, w_o)


def _chunk_tok(width):
    return pl.BlockSpec((NCHIP, TM, width), lambda i: (0, i, 0))


def _ffn_fwd(x1, g_ffn, wg, wu, wd, carry=None):
    L = x1.shape[0]

    def body(x_ref, g_ref, wg_hbm, wu_hbm, wd_hbm, x2_ref, h2_ref, gg_ref, uu_ref, wg_vm, wu_vm, wd_vm, w_sems):
        _resident_now([(src.at[c], dst.at[c]) for c in range(NCHIP)
                       for src, dst in ((wg_hbm, wg_vm), (wu_hbm, wu_vm), (wd_hbm, wd_vm))], w_sems)
        x = x_ref[...]
        xh, _ = _rms(x)
        h2 = (xh * g_ref[...]).astype(BF)
        h2_ref[...] = h2
        out = x
        for c in range(NCHIP):
            gg = lax.dot_general(h2, wg_vm[c], (((1,), (1,)), ((), ())), preferred_element_type=F32)
            uu = lax.dot_general(h2, wu_vm[c], (((1,), (1,)), ((), ())), preferred_element_type=F32)
            gg_ref[c] = gg.astype(BF)
            uu_ref[c] = uu.astype(BF)
            act = (gg * _sig(gg) * uu).astype(BF)
            out = out + jnp.dot(act, wd_vm[c], preferred_element_type=F32)
        x2_ref[...] = out

    return _pallas_call(
        body, carry, name="ffn_fwd", grid=(L // TM,),
        in_specs=[_tok(D), _full((1, D)), ANY, ANY, ANY],
        out_specs=[_tok(D), _tok(D), _chunk_tok(FC), _chunk_tok(FC)],
        out_shape=[_sds((L, D)), _sds((L, D), BF), _sds((NCHIP, L, FC), BF), _sds((NCHIP, L, FC), BF)],
        scratch_shapes=[pltpu.VMEM((NCHIP, FC, D), BF)] * 3 + [pltpu.SemaphoreType.DMA((3 * NCHIP,))],
        compiler_params=_params(52),
    )(x1, g_ffn, wg, wu, wd)


def _ffn_bwd(x1, dx2, gg, uu, g_ffn, wg, wu, wd, carry=None):
    L = x1.shape[0]

    def body(x_ref, dx2_ref, gg_ref, uu_ref, g_ref, wg_hbm, wu_hbm, wd_hbm,
             dx1_ref, act_ref, dgg_ref, duu_ref, dg_ref, wg_vm, wu_vm, wd_vm, w_sems):
        _resident_now([(src.at[c], dst.at[c]) for c in range(NCHIP)
                       for src, dst in ((wg_hbm, wg_vm), (wu_hbm, wu_vm), (wd_hbm, wd_vm))], w_sems)

        @pl.when(pl.program_id(0) == 0)
        def _():
            dg_ref[...] = jnp.zeros_like(dg_ref)

        dx2 = dx2_ref[...]
        dx2b = dx2.astype(BF)
        dh2 = jnp.zeros((TM, D), F32)
        for c in range(NCHIP):
            g = gg_ref[c].astype(F32)
            u = uu_ref[c].astype(F32)
            s = _sig(g)
            silu = g * s
            act_ref[c] = (silu * u).astype(BF)
            dact = lax.dot_general(dx2b, wd_vm[c], (((1,), (1,)), ((), ())), preferred_element_type=F32)
            dg = (dact * u * s * (1.0 + g * (1.0 - s))).astype(BF)
            du = (dact * silu).astype(BF)
            dgg_ref[c] = dg
            duu_ref[c] = du
            dh2 = dh2 + jnp.dot(dg, wg_vm[c], preferred_element_type=F32)
            dh2 = dh2 + jnp.dot(du, wu_vm[c], preferred_element_type=F32)
        xh, r = _rms(x_ref[...])
        dg_ref[0:1, :] += _colsum(dh2 * xh)
        dx1_ref[...] = dx2 + _rms_bwd(dh2, xh, r, g_ref[...])

    return _pallas_call(
        body, carry, name="ffn_bwd", grid=(L // TM,),
        in_specs=[_tok(D), _tok(D), _chunk_tok(FC), _chunk_tok(FC), _full((1, D)), ANY, ANY, ANY],
        out_specs=[_tok(D), _chunk_tok(FC), _chunk_tok(FC), _chunk_tok(FC), _full((SUB, D))],
        out_shape=[_sds((L, D)), _sds((NCHIP, L, FC), BF), _sds((NCHIP, L, FC), BF), _sds((NCHIP, L, FC), BF),
                   _sds((SUB, D))],
        scratch_shapes=[pltpu.VMEM((NCHIP, FC, D), BF)] * 3 + [pltpu.SemaphoreType.DMA((3 * NCHIP,))],
        compiler_params=_params(56),
    )(x1, dx2, gg, uu, g_ffn, wg, wu, wd)


def _ple_loss(x2, p, tgt, g_pg, w_pg, b_pg, w_ple, g_ple, g_final):
    L = x2.shape[0]

    def body(x2_ref, p_ref, t_ref, gpg_ref, wpg_ref, bpg_ref, wple_ref, gple_ref, gf_ref,
             dx2_ref, n2_ref, dpre_ref, de0_ref, acc_ref):
        @pl.when(pl.program_id(0) == 0)
        def _():
            acc_ref[...] = jnp.zeros_like(acc_ref)

        x2 = x2_ref[...]
        x2h, r2 = _rms(x2)
        n2 = (x2h * gpg_ref[...]).astype(BF)
        n2_ref[...] = n2
        gate = _sig(jnp.dot(n2, wpg_ref[...], preferred_element_type=F32) + bpg_ref[...])
        pb = p_ref[...].astype(BF)
        e0 = jnp.concatenate([jnp.dot(pb, wple_ref[k], preferred_element_type=F32) for k in range(NCHIP)], axis=1)
        e0h, re = _rms(e0)
        e = e0h * gple_ref[...]
        x3 = x2 + gate * e
        x3h, r3 = _rms(x3)
        diff = x3h * gf_ref[...] - t_ref[...]
        acc_ref[4:5, :] += _colsum(diff * diff) * (0.5 / D)
        dy = diff * (1.0 / D)
        acc_ref[3:4, :] += _colsum(dy * x3h)
        dx3 = _rms_bwd(dy, x3h, r3, gf_ref[...])
        de = dx3 * gate
        acc_ref[2:3, :] += _colsum(de * e0h)
        de0_ref[...] = _rms_bwd(de, e0h, re, gple_ref[...]).astype(BF)
        dpre = dx3 * e * gate * (1.0 - gate)
        acc_ref[1:2, :] += _colsum(dpre)
        dpreb = dpre.astype(BF)
        dpre_ref[...] = dpreb
        dn2 = lax.dot_general(dpreb, wpg_ref[...], (((1,), (1,)), ((), ())), preferred_element_type=F32)
        acc_ref[0:1, :] += _colsum(dn2 * x2h)
        dx2_ref[...] = dx3 + _rms_bwd(dn2, x2h, r2, gpg_ref[...])

    return _pallas_call(
        body, name="ple_loss", grid=(L // TM,),
        in_specs=[_tok(D), _tok(PLE), _tok(D), _full((1, D)), _full((D, D)), _full((1, D)), _full((NCHIP, PLE, AC)),
                  _full((1, D)), _full((1, D))],
        out_specs=[_tok(D), _tok(D), _tok(D), _tok(D), _full((SUB, D))],
        out_shape=[_sds((L, D)), _sds((L, D), BF), _sds((L, D), BF), _sds((L, D), BF), _sds((SUB, D))],
        compiler_params=_params(40),
    )(x2, p, tgt, g_pg, w_pg, b_pg, w_ple, g_ple, g_final)


def _tn(name, a, b, col_chunk=None, a_block=None, carry=None):
    L = a.shape[-2]
    m, n = a.shape[-1], b.shape[-1]
    a_col = 0
    if a_block is not None:
        a_col, m = a_block
    tk = L if (a.ndim == 3 or b.ndim == 3 or a_block is not None) else TK
    if a.ndim == 3 or b.ndim == 3:
        nj, bn = (a if a.ndim == 3 else b).shape[0], n
        a_spec = (pl.BlockSpec((None, tk, m), lambda j, t: (j, t, 0)) if a.ndim == 3
                  else pl.BlockSpec((tk, m), lambda j, t: (t, 0)))
        b_spec = (pl.BlockSpec((None, tk, n), lambda j, t: (j, t, 0)) if b.ndim == 3
                  else pl.BlockSpec((tk, n), lambda j, t: (t, 0)))
        out_spec, out_shape = pl.BlockSpec((None, m, n), lambda j, t: (j, 0, 0)), _sds((nj, m, n))
    else:
        bn = col_chunk
        if bn is None:
            bn = next((cand for cand in (1024, 512) if n > cand and n % cand == 0), n)
        nj = n // bn
        a_spec = pl.BlockSpec((tk, m), lambda j, t: (t, a_col))
        b_spec = pl.BlockSpec((tk, bn), lambda j, t: (t, j))
        if col_chunk is None:
            out_spec, out_shape = pl.BlockSpec((m, bn), lambda j, t: (0, j)), _sds((m, n))
        else:
            out_spec, out_shape = pl.BlockSpec((None, m, bn), lambda j, t: (j, 0, 0)), _sds((nj, m, bn))

    def body(a_ref, b_ref, o_ref):
        product = _mm_tn(a_ref[...], b_ref[...])
        if tk == L:
            o_ref[...] = product
        else:
            @pl.when(pl.program_id(1) == 0)
            def _():
                o_ref[...] = product

            @pl.when(pl.program_id(1) > 0)
            def _():
                o_ref[...] += product

    outs = _pallas_call(
        body, carry, name=name, grid=(nj, L // tk), in_specs=[a_spec, b_spec], out_specs=[out_spec],
        out_shape=[pltpu.HBM(out_shape.shape, out_shape.dtype)],
        compiler_params=pltpu.CompilerParams(dimension_semantics=("arbitrary", "arbitrary"),
                                             vmem_limit_bytes=52 * VMEM_MB),
    )(a, b)
    return outs[0] if carry is None else outs


LANE = 128


def _tn_blocks(name, a, bs, ga, gb, carry=None):
    L, m, n, nb = a.shape[0], a.shape[1], bs[0].shape[1], len(bs)
    per = LANE // ga
    wb = per * gb
    n_super = m // LANE

    def body(a_ref, *refs):
        b_refs, o_refs, acc_refs = refs[:nb], refs[nb:2 * nb], refs[2 * nb:]
        t = pl.program_id(0)

        @pl.when(t == 0)
        def _():
            for acc in acc_refs:
                acc[...] = jnp.zeros_like(acc)

        lhs = a_ref[...].astype(BF)
        for b_ref, acc in zip(b_refs, acc_refs):
            rhs = b_ref[...].astype(BF)
            for j in range(n_super):
                acc[j] += _mm_tn(lhs[:, j * LANE:(j + 1) * LANE], rhs[:, j * wb:(j + 1) * wb])

        @pl.when(t == L // TK - 1)
        def _():
            own = (lax.broadcasted_iota(jnp.int32, (LANE, wb), 0) // ga) == (lax.broadcasted_iota(jnp.int32, (LANE, wb), 1) // gb)
            for o_ref, acc in zip(o_refs, acc_refs):
                for j in range(n_super):
                    kept = jnp.where(own, acc[j], 0.0)
                    o_ref[:, j * wb:(j + 1) * wb] = jnp.sum(kept.reshape(per, ga, wb), axis=0)

    outs = _pallas_call(
        body, carry, name=name, grid=(L // TK,),
        in_specs=[pl.BlockSpec((TK, m), lambda t: (t, 0))] + [pl.BlockSpec((TK, n), lambda t: (t, 0))] * nb,
        out_specs=[_full((ga, n))] * nb, out_shape=[_sds((ga, n))] * nb,
        scratch_shapes=[pltpu.VMEM((n_super, LANE, wb), F32)] * nb,
        compiler_params=_params(48),
    )(*_in_hbm([a] + list(bs)))
    return list(outs)


def _s5_discretize(lam_re, lam_im, log_dt, b_re, b_im):
    dt = jnp.exp(log_dt)[:, None]
    mag = jnp.exp(lam_re * dt)
    ar = mag * jnp.cos(lam_im * dt)
    ai = mag * jnp.sin(lam_im * dt)
    den = lam_re * lam_re + lam_im * lam_im
    nr = ar - 1.0
    fr = (nr * lam_re + ai * lam_im) / den
    fi = (ai * lam_re - nr * lam_im) / den
    bbr = fr[:, None, :] * b_re - fi[:, None, :] * b_im
    bbi = fr[:, None, :] * b_im + fi[:, None, :] * b_re
    return ar, ai, bbr, bbi


def _prepare(by_rows, block_cols, ar, ai):
    n = len(by_rows)

    def body(*refs):
        srcs, (ar_ref, ai_ref), dense, (con_ref, rev_ref) = refs[:n], refs[n:n + 2], refs[n + 2:2 * n + 2], refs[2 * n + 2:]
        for src, out, c in zip(srcs, dense, block_cols):
            r = src.shape[0]
            per = LANE // r
            wide = per * c
            own = (lax.broadcasted_iota(jnp.int32, (LANE, wide), 0) // r) == (lax.broadcasted_iota(jnp.int32, (LANE, wide), 1) // c)
            for j in range(out.shape[0]):
                tiled = jnp.broadcast_to(src[:, j * wide:(j + 1) * wide][None], (per, r, wide)).reshape(LANE, wide)
                out[j] = jnp.where(own, tiled, 0.0).astype(BF)
        a_r, a_i = ar_ref[...], ai_ref[...]
        pw = [(jnp.ones_like(a_r), jnp.zeros_like(a_i))]
        for _ in range(SUB):
            pr, pi = pw[-1]
            pw.append((pr * a_r - pi * a_i, pr * a_i + pi * a_r))
        row = _row_iota(GN)
        for ref, reverse in ((con_ref, False), (rev_ref, True)):
            sign = -1.0 if reverse else 1.0
            for j, sh in enumerate((1, 2, 4)):
                keep = (row < SUB - sh) if reverse else (row >= sh)
                ref[2 * j * SUB:(2 * j + 1) * SUB, :] = jnp.where(keep, pw[sh][0], 0.0)
                ref[(2 * j + 1) * SUB:(2 * j + 2) * SUB, :] = jnp.where(keep, sign * pw[sh][1], 0.0)
            p_r, p_i = jnp.zeros((SUB, GN), F32), jnp.zeros((SUB, GN), F32)
            for i in range(SUB):
                k = SUB - i if reverse else i + 1
                p_r = jnp.where(row == i, pw[k][0], p_r)
                p_i = jnp.where(row == i, sign * pw[k][1], p_i)
            ref[6 * SUB:7 * SUB, :] = p_r
            ref[7 * SUB:8 * SUB, :] = p_i

    dense_shapes = [(b.shape[1] // (LANE // b.shape[0] * c), LANE, LANE // b.shape[0] * c)
                    for b, c in zip(by_rows, block_cols)]
    outs = _pallas_call(
        body, name="prepare", grid=(1,), in_specs=[_full(b.shape) for b in by_rows] + [_full((1, GN))] * 2,
        out_specs=[_full(s) for s in dense_shapes] + [_full((8 * SUB, GN))] * 2,
        out_shape=[_sds(s, BF) for s in dense_shapes] + [_sds((8 * SUB, GN))] * 2,
        compiler_params=_params(48),
    )(*by_rows, ar, ai)
    return outs[:n], outs[n], outs[n + 1]


def _local_step(x, p, tgt, w, comm):
    rows_of = lambda a: a.reshape(NCHIP * a.shape[1], a.shape[2])
    quarters = lambda a: a.reshape(NCHIP, a.shape[0] // NCHIP, a.shape[1])

    def gathering(names, call):
        carry = comm.gather(names)
        outs = list(call(carry))
        own = len(outs) - len(carry.out_shapes)
        w.update(zip(names, outs[own:]))
        return outs[:own]

    w.update(comm.first())
    w_glu = rows_of(w["w_glu"])
    ar, ai, bbr, bbi = _s5_discretize(w["lam_re"], w["lam_im"], w["log_dt"], w["s5_b_re"], w["s5_b_im"])
    by_row = lambda b: jnp.transpose(b, (1, 0, 2)).reshape(b.shape[1], -1)
    (bbr_d, bbi_d, ccr_d, cci_d, wr_d, wi_d), con, con_rev = _prepare(
        [by_row(b) for b in (bbr, bbi, w["s5_c_re"], w["s5_c_im"], w["w_r"], w["w_i"])], [NS] * 4 + [HD] * 2,
        ar.reshape(1, GN), ai.reshape(1, GN))
    dsk = w["s5_d"].reshape(1, S5W)
    lam = w["lru_lambda"].reshape(1, LW)
    sp = jax.nn.softplus(-lam)
    b_r, b_i = w["b_r"].reshape(1, LW), w["b_i"].reshape(1, LW)
    row = lambda name: w[name].reshape(1, -1)

    h, ua, ub, gp = gathering(["w_a_out", "w_b_out"], lambda carry: _inproj_fwd(
        x, row("g_mix"), w["w_in"], row("b_in"), carry))
    sr, si, y, zg, ya = gathering(["w_o", "w_ffn_gate"], lambda carry: _s5_fwd(
        ua, bbr_d, bbi_d, ccr_d, cci_d, dsk, con, w_glu, row("b_glu"), carry))
    xc, rg, ig, yb, hp = gathering(["w_ffn_up"], lambda carry: _lru_fwd(
        ub, w["conv_w"], row("conv_b"), wr_d, wi_d, b_r, b_i, sp, carry))
    w_b_out, w_o = rows_of(w["w_b_out"]), rows_of(w["w_o"])
    x1, pa, pb, merged = gathering(["w_ffn_down"], lambda carry: _merge_fwd(
        x, ya, yb, gp, w["w_a_out"], w_b_out, w_o, carry))
    x2, h2, gg, uu = gathering(["w_ple_gate", "w_ple"], lambda carry: _ffn_fwd(
        x1, row("g_ffn"), w["w_ffn_gate"], w["w_ffn_up"], w["w_ffn_down"], carry))
    w_pg = rows_of(w["w_ple_gate"])
    dx2, n2, dpre, de0, acc_p = _ple_loss(x2, p, tgt, row("g_ple_gate"), w_pg, row("b_ple_gate"),
                                          w["w_ple"], row("g_ple"), row("g_final"))
    comm.reduce("ple", {"w_ple_gate": quarters(_tn("dw_ple_gate", n2, dpre)),
                        "w_ple": _tn("dw_ple", p, de0, col_chunk=AC)})
    dx1, act, dgg, duu, acc_f = comm.run(lambda carry: _ffn_bwd(
        x1, dx2, gg, uu, row("g_ffn"), w["w_ffn_gate"], w["w_ffn_up"], w["w_ffn_down"], carry))
    comm.reduce("ffn_gate", {"w_ffn_gate": _tn("dw_ffn_gate", dgg, h2)})
    comm.reduce("ffn_up", {"w_ffn_up": comm.run(lambda carry: _tn("dw_ffn_up", duu, h2, carry=carry))[0]})
    comm.reduce("ffn_down", {"w_ffn_down": comm.run(lambda carry: _tn("dw_ffn_down", act, dx2, carry=carry))[0]})
    dya, dyb, dgp, dpa, dpb = comm.run(lambda carry: _merge_bwd(
        dx1, gp, pa, pb, w["w_a_out"], w_b_out, w_o, carry))
    comm.reduce("merge", {"w_o": quarters(_tn("dw_o", merged, dx1)), "w_a_out": _tn("dw_a_out", ya, dpa, col_chunk=AC),
                          "w_b_out": quarters(_tn("dw_b_out", yb, dpb))})
    dua, dq, dy, lr, li, acc_a, acc_s = comm.run(lambda carry: _s5_bwd(
        dya, y, ua, sr, si, bbr_d, bbi_d, ccr_d, cci_d, dsk, con_rev, w_glu, row("b_glu"), carry))
    dub, dpr, dpi, acc_l = comm.run(lambda carry: _lru_bwd(
        dyb, xc, rg, ig, hp, ub, w["conv_w"], wr_d, wi_d, sp, -_sig(-lam), carry))
    gx, dz, acc_g, acc_b = _inproj_bwd(x, dx1, dua, dub, dgp, row("g_mix"), w["w_in"])
    half = (D // 2,)
    comm.reduce("in_lo", {"w_in_lo": comm.run(lambda carry: _tn(
        "dw_in_lo", h, dz, col_chunk=QC, a_block=(0,) + half, carry=carry))[0]})
    comm.reduce("in_hi", {"w_in_hi": comm.run(lambda carry: _tn(
        "dw_in_hi", h, dz, col_chunk=QC, a_block=(1,) + half, carry=carry))[0], "w_glu": quarters(_tn("dw_glu", zg, dq))})
    d_wr, d_wi = comm.run(lambda carry: _tn_blocks("dw_r_i", xc, [dpr, dpi], HD, HD, carry))
    d_bbr, d_bbi = comm.run(lambda carry: _tn_blocks("d_bb", ua, [lr, li], NP, NS, carry))
    d_ccr, d_cci = comm.run(lambda carry: _tn_blocks("d_cc", dy, [sr, si], NP, NS, carry))
    comm.drain()
    sums = {"ple": acc_p, "ffn": acc_f, "mix": acc_g, "b_in": acc_b, "lru": acc_l, "s5": acc_s, "s5_a": acc_a}
    blocks = {"bb_re": d_bbr, "bb_im": d_bbi,
              "cc_re": d_ccr, "cc_im": d_cci,
              "w_r": d_wr, "w_i": d_wi}
    return gx, sums, blocks


def _replicated_grads(w, sums, blocks):
    grouped = lambda e, groups: jnp.transpose(e.reshape(e.shape[0], groups, -1), (1, 0, 2))
    d_ar, d_ai = sums["s5_a"][0].reshape(NG, NS), sums["s5_a"][1].reshape(NG, NS)
    d_bbr, d_bbi = grouped(blocks["bb_re"], NG), grouped(blocks["bb_im"], NG)
    _, vjp = jax.vjp(_s5_discretize, w["lam_re"], w["lam_im"], w["log_dt"], w["s5_b_re"], w["s5_b_im"])
    g = dict(zip(("lam_re", "lam_im", "log_dt", "s5_b_re", "s5_b_im"), vjp((d_ar, d_ai, d_bbr, d_bbi))))
    g["s5_c_re"] = grouped(blocks["cc_re"], NG)
    g["s5_c_im"] = -grouped(blocks["cc_im"], NG)
    g["w_r"], g["w_i"] = grouped(blocks["w_r"], NH), grouped(blocks["w_i"], NH)
    g["s5_d"] = sums["s5"][0].reshape(NG, NP)
    g["b_r"] = sums["lru"][1].reshape(NH, HD)
    g["b_i"] = sums["lru"][2].reshape(NH, HD)
    return g


ACC_ROWS = {"g_mix": ("mix", 0), "b_in": ("b_in", 0), "g_ffn": ("ffn", 0), "g_ple_gate": ("ple", 0),
            "b_ple_gate": ("ple", 1), "g_ple": ("ple", 2), "g_final": ("ple", 3), "b_glu": ("s5", 1),
            "lru_lambda": ("lru", 0), "conv_b": ("lru", 3)}
LOSS_ROW = ("ple", 4)
CONV_W_ROWS = ("lru", 4)


SHARDED = [("w_in", (D, QC)), ("w_glu", (S5W // NCHIP, S5W)), ("w_a_out", (S5W, AC)), ("w_b_out", (LW // NCHIP, D)),
           ("w_o", (D // NCHIP, D)), ("w_ffn_gate", (FC, D)), ("w_ffn_up", (FC, D)), ("w_ffn_down", (FC, D)),
           ("w_ple_gate", (D // NCHIP, D)), ("w_ple", (PLE, AC))]
NSH = len(SHARDED)
TRANSPOSED = ("w_ffn_gate", "w_ffn_up", "s5_b_re", "s5_b_im")
CONV_SHARD = (4, LW // NCHIP)


def _mesh_pos():
    return lax.axis_index("x"), lax.axis_index("y"), lax.axis_index("c")


def _other_chips(x, y):
    return [(1 - x, y), (x, 1 - y), (1 - x, 1 - y)]


def _half_rows(c, rows, align):
    return pl.ds(pl.multiple_of(c * (rows // 2), align), rows // 2)


def _run_now(name, carry):
    c_in, c_out = len(carry.operands), len(carry.out_shapes)

    def body(*refs):
        ins, outs, sems = refs[:c_in], refs[c_in:c_in + c_out], refs[c_in + c_out:]
        carry.start(ins, outs, sems)
        carry.finish(ins, outs, sems)

    return pl.pallas_call(body, name=name, in_specs=[ANY] * c_in, out_specs=[ANY] * c_out,
                          out_shape=list(carry.out_shapes), scratch_shapes=list(carry.sems),
                          input_output_aliases=dict(carry.aliases))(*_in_hbm(carry.operands))


def _gather_group(shards, split):
    n = len(shards)

    def copies(srcs, outs, sems):
        send_sems, recv_sems = sems
        x, y, c = _mesh_pos()
        k0 = 2 * x + y
        sib = (x, y, 1 - c)
        chips = _other_chips(x, y)

        def remote(src, dst, j, i, to):
            return pltpu.make_async_remote_copy(src_ref=src, dst_ref=dst, send_sem=send_sems.at[j, i],
                                                recv_sem=recv_sems.at[j, i], device_id=to, device_id_type=MESH)

        def rows(ref, i, core, *lead):
            if not split[i]:
                return ref.at[lead] if lead else ref
            return ref.at[(*lead, _half_rows(core, shards[i].shape[0], 16))]

        own = [remote(s, o.at[k0], 6, i, sib) for i, (s, o) in enumerate(zip(srcs, outs))]
        ici, landed, fwd, fwd_landed = [], [], [], []
        for j, chip in enumerate(chips):
            kj = 2 * chip[0] + chip[1]
            pairs = list(enumerate(zip(srcs, outs)))
            ici.append([remote(rows(s, i, c), rows(o, i, c, k0), j, i, (*chip, c)) for i, (s, o) in pairs])
            landed.append([remote(rows(s, i, c), rows(o, i, c, kj), j, i, (*chip, c)) for i, (s, o) in pairs])
            fwd.append([remote(rows(o, i, c, kj), rows(o, i, c, kj), 3 + j, i, sib) for i, (s, o) in pairs if split[i]])
            fwd_landed.append([remote(rows(o, i, 1 - c, kj), rows(o, i, 1 - c, kj), 3 + j, i, sib)
                               for i, (s, o) in pairs if split[i]])
        return own, ici, landed, fwd, fwd_landed

    def start(srcs, outs, sems):
        own, ici, _, _, _ = copies(srcs, outs, sems)
        for cp in own + [cp for per_chip in ici for cp in per_chip]:
            cp.start()

    def finish(srcs, outs, sems):
        own, ici, landed, fwd, fwd_landed = copies(srcs, outs, sems)
        passed = [i for i in range(n) if split[i]]
        for j in range(3):
            for i, cp in enumerate(landed[j]):
                cp.wait_recv()
                if split[i]:
                    fwd[j][passed.index(i)].start()
        for j in range(3):
            for cp in fwd_landed[j]:
                cp.wait_recv()
        for cp in own:
            cp.wait_recv()
        for cp in own + [cp for per_chip in ici + fwd for cp in per_chip]:
            cp.wait_send()

    return _Carried(shards, [_sds((NCHIP,) + s.shape, s.dtype) for s in shards],
                    [pltpu.SemaphoreType.DMA((7, n)), pltpu.SemaphoreType.DMA((7, n))], start, finish)


def _each_copy(copies, carried, out_shapes, sems, aliases=None):
    def start(ins, outs, sem_refs):
        for cp in copies(ins, outs, sem_refs):
            cp.start()

    def finish(ins, outs, sem_refs):
        for cp in copies(ins, outs, sem_refs):
            cp.wait()

    return _Carried(carried, out_shapes, sems, start, finish, aliases)


def _swap_group(grads):
    n = len(grads)

    def copies(srcs, outs, sems):
        send_sems, recv_sems = sems
        x, y, c = _mesh_pos()
        return [pltpu.make_async_remote_copy(src_ref=s.at[:, _half_rows(1 - c, s.shape[1], 8)], dst_ref=o,
                                             send_sem=send_sems.at[i], recv_sem=recv_sems.at[i], device_id=(x, y, 1 - c),
                                             device_id_type=MESH) for i, (s, o) in enumerate(zip(srcs, outs))]

    return _each_copy(copies, grads, [pltpu.HBM((NCHIP, g.shape[1] // 2, g.shape[2]), F32) for g in grads],
                      [pltpu.SemaphoreType.DMA((n,)), pltpu.SemaphoreType.DMA((n,))])


def _add_sibling_group(tag, kc_idx, grads, gots):
    n = len(grads)

    def body(kc_ref, *refs):
        for g, rx, p, pb in zip(refs[:n], refs[n:2 * n], refs[2 * n:3 * n], refs[3 * n:]):
            s = g[...] + rx[...]
            pb[...] = s.astype(BF)

            @pl.when(pl.program_id(0) == kc_ref[0])
            def _():
                p[...] = s

    halves = [pl.BlockSpec((None,) + rx.shape[1:], lambda k, kc_ref: (k, 0, 0)) for rx in gots]
    mine = [pl.BlockSpec((None,) + rx.shape[1:], lambda k, kc_ref: (k, kc_ref[1], 0)) for rx in gots]
    own = [pl.BlockSpec(rx.shape[1:], lambda k, kc_ref: (0, 0)) for rx in gots]
    outs = _pallas_call(
        body, name="add_sibling_" + tag,
        grid_spec=pltpu.PrefetchScalarGridSpec(num_scalar_prefetch=1, grid=(NCHIP,), in_specs=mine + halves,
                                               out_specs=own + halves),
        out_shape=[pltpu.HBM(rx.shape[1:], F32) for rx in gots] + [pltpu.HBM(rx.shape, BF) for rx in gots],
        compiler_params=_params(48),
    )(kc_idx, *_in_hbm(list(grads) + list(gots)))
    return outs[:n], outs[n:]


def _exchange_group(parts):
    n = len(parts)

    def copies(srcs, outs, sems):
        send_sems, recv_sems = sems
        x, y, c = _mesh_pos()
        return [pltpu.make_async_remote_copy(
            src_ref=s.at[2 * chip[0] + chip[1]], dst_ref=o.at[j], send_sem=send_sems.at[j, i],
            recv_sem=recv_sems.at[j, i], device_id=(*chip, c), device_id_type=MESH)
            for j, chip in enumerate(_other_chips(x, y)) for i, (s, o) in enumerate(zip(srcs, outs))]

    return _each_copy(copies, parts, [pltpu.HBM((3,) + p.shape[1:], BF) for p in parts],
                      [pltpu.SemaphoreType.DMA((3, n)), pltpu.SemaphoreType.DMA((3, n))])


def _add_chips_group(tag, kc_idx, parts, arrived):
    n = len(parts)

    def body(kc_ref, *refs):
        for p, rx, t in zip(refs[:n], refs[n:2 * n], refs[2 * n:]):
            t[...] = ((p[...] + rx[0].astype(F32)) + rx[1].astype(F32)) + rx[2].astype(F32)

    outs = _pallas_call(
        body, name="add_chips_" + tag,
        grid_spec=pltpu.PrefetchScalarGridSpec(
            num_scalar_prefetch=1, grid=(1,),
            in_specs=([pl.BlockSpec(rx.shape[1:], lambda i, kc_ref: (0, 0)) for rx in arrived]
                      + [pl.BlockSpec(rx.shape, lambda i, kc_ref: (0, 0, 0)) for rx in arrived]),
            out_specs=[pl.BlockSpec((None,) + rx.shape[1:], lambda i, kc_ref: (kc_ref[1], 0, 0)) for rx in arrived]),
        out_shape=[pltpu.HBM((2,) + rx.shape[1:], F32) for rx in arrived],
        compiler_params=_params(48),
    )(kc_idx, *_in_hbm(list(parts) + list(arrived)))
    return list(outs)


def _join_group(halves):
    n = len(halves)

    def copies(bufs, sems):
        send_sems, recv_sems = sems
        x, y, c = _mesh_pos()
        sib = (x, y, 1 - c)
        sends = [pltpu.make_async_remote_copy(src_ref=b.at[c], dst_ref=b.at[c], send_sem=send_sems.at[i],
                                              recv_sem=recv_sems.at[i], device_id=sib, device_id_type=MESH)
                 for i, b in enumerate(bufs)]
        landed = [pltpu.make_async_remote_copy(src_ref=b.at[c], dst_ref=b.at[1 - c], send_sem=send_sems.at[i],
                                               recv_sem=recv_sems.at[i], device_id=sib, device_id_type=MESH)
                  for i, b in enumerate(bufs)]
        return sends, landed

    def start(_, bufs, sems):
        for cp in copies(bufs, sems)[0]:
            cp.start()

    def finish(_, bufs, sems):
        sends, landed = copies(bufs, sems)
        for cp in landed:
            cp.wait_recv()
        for cp in sends:
            cp.wait_send()

    return _Carried(halves, [pltpu.HBM(h.shape, F32) for h in halves],
                    [pltpu.SemaphoreType.DMA((n,)), pltpu.SemaphoreType.DMA((n,))], start, finish,
                    {i: i for i in range(n)})


def _combine(carries):
    operands, out_shapes, sems, aliases, spans = [], [], [], {}, []
    for c in carries:
        aliases.update({len(operands) + i: len(out_shapes) + o for i, o in c.aliases.items()})
        spans.append((len(operands), len(out_shapes), len(sems)))
        operands += list(c.operands)
        out_shapes += list(c.out_shapes)
        sems += list(c.sems)

    def each(phase):
        def run(ins, outs, sem_refs):
            for c, (a, b, s) in zip(carries, spans):
                getattr(c, phase)(ins[a:a + len(c.operands)], outs[b:b + len(c.out_shapes)], sem_refs[s:s + len(c.sems)])
        return run

    return _Carried(operands, out_shapes, sems, each("start"), each("finish"), aliases)


def _allreduce_small(arrays, wire):
    n = len(arrays)
    halves = [(a.shape[0], a.shape[1] // 2) for a in arrays]

    def body(*refs):
        srcs, outs = refs[:n], refs[n:2 * n]
        mine_bufs, sib_bufs, chip_bufs, total_bufs = (refs[k * n:(k + 1) * n] for k in range(2, 6))
        send_sems, recv_sems, local_sems = refs[6 * n:]
        x, y, c = _mesh_pos()
        k0 = 2 * x + y
        sib = (x, y, 1 - c)

        def remote(src, dst, j, i, to):
            return pltpu.make_async_remote_copy(src_ref=src, dst_ref=dst, send_sem=send_sems.at[j, i],
                                                recv_sem=recv_sems.at[j, i], device_id=to, device_id_type=MESH)

        def cols(ref, i, core):
            return ref.at[:, pl.ds(pl.multiple_of(core * halves[i][1], LANE), halves[i][1])]

        swaps = [remote(cols(s, i, 1 - c), b, 0, i, sib) for i, (s, b) in enumerate(zip(srcs, sib_bufs))]
        own = [pltpu.make_async_copy(cols(s, i, c), m, local_sems.at[i]) for i, (s, m) in enumerate(zip(srcs, mine_bufs))]
        for cp in swaps + own:
            cp.start()
        for cp in swaps + own:
            cp.wait()
        for m, b, buf in zip(mine_bufs, sib_bufs, chip_bufs):
            buf[k0] = (m[...] + b[...]).astype(buf.dtype)
        chips = _other_chips(x, y)
        sends = [remote(buf.at[k0], buf.at[k0], 1 + j, i, (*chip, c))
                 for j, chip in enumerate(chips) for i, buf in enumerate(chip_bufs)]
        for cp in sends:
            cp.start()
        for j, chip in enumerate(chips):
            for i, buf in enumerate(chip_bufs):
                remote(buf.at[k0], buf.at[2 * chip[0] + chip[1]], 1 + j, i, (*chip, c)).wait_recv()
        for cp in sends:
            cp.wait_send()
        for t, buf in zip(total_bufs, chip_bufs):
            t[...] = ((buf[0].astype(F32) + buf[1].astype(F32)) + buf[2].astype(F32)) + buf[3].astype(F32)
        joins = [remote(t, cols(o, i, c), 4, i, sib) for i, (t, o) in enumerate(zip(total_bufs, outs))]
        keep = [pltpu.make_async_copy(t, cols(o, i, c), local_sems.at[i]) for i, (t, o) in enumerate(zip(total_bufs, outs))]
        for cp in joins + keep:
            cp.start()
        for i, (t, o) in enumerate(zip(total_bufs, outs)):
            remote(t, cols(o, i, 1 - c), 4, i, sib).wait_recv()
        for cp in joins:
            cp.wait_send()
        for cp in keep:
            cp.wait()

    specs = [_full(a.shape) for a in arrays]
    return _pallas_call(
        body, name="allreduce_small", grid=(1,), in_specs=specs, out_specs=specs,
        out_shape=[_sds(a.shape) for a in arrays],
        scratch_shapes=([pltpu.VMEM(h, F32) for h in halves] + [pltpu.VMEM(h, F32) for h in halves]
                        + [pltpu.VMEM((NCHIP,) + h, dt) for h, dt in zip(halves, wire)] + [pltpu.VMEM(h, F32) for h in halves]
                        + [pltpu.SemaphoreType.DMA((5, n)), pltpu.SemaphoreType.DMA((5, n)), pltpu.SemaphoreType.DMA((n,))]),
        compiler_params=_params(32),
    )(*arrays)


def _adamw_terms(w, g, m, v):
    m = ADAM_B1 * m + (1.0 - ADAM_B1) * g
    v = ADAM_B2 * v + (1.0 - ADAM_B2) * jnp.square(g)
    m_hat = m / (1.0 - ADAM_B1 ** ADAM_STEP)
    v_hat = v / (1.0 - ADAM_B2 ** ADAM_STEP)
    return -ADAM_LR * (m_hat / (jnp.sqrt(v_hat) + ADAM_EPS) + ADAM_WD * w), m, v


ADAM_STEPS = 4


def _adamw_group(tag, ws, gs, ms, vs):
    n = len(ws)

    def body(*refs):
        ins, outs = refs[:4 * n], refs[4 * n:]
        for i in range(n):
            w, g, m, v = (ins[k * n + i][...] for k in range(4))
            outs[i][...] = g
            outs[n + i][...], outs[2 * n + i][...], outs[3 * n + i][...] = _adamw_terms(w, g, m, v)

    specs = [pl.BlockSpec((w.shape[0] // ADAM_STEPS, w.shape[1]), lambda i: (i, 0)) for w in ws]
    outs = _pallas_call(
        body, name="adamw_" + tag, grid=(ADAM_STEPS,), in_specs=specs * 4, out_specs=specs * 4,
        out_shape=[_sds(w.shape) for w in ws] * 4, compiler_params=_params(48),
    )(*_in_hbm(list(ws) + list(gs) + list(ms) + list(vs)))
    return outs[:n], outs[n:2 * n], outs[2 * n:3 * n], outs[3 * n:]


def _adamw_replicated(sums, row_of, direct):
    ns, nr, nd = len(sums), len(row_of), len(direct)

    def body(*refs):
        sum_refs = refs[:ns]
        ins = refs[ns:ns + 3 * nr + 4 * nd]
        outs = refs[ns + 3 * nr + 4 * nd:]
        for i, (_, _, _, si, row) in enumerate(row_of):
            w_ref, m_ref, v_ref = ins[3 * i:3 * i + 3]
            g = sum_refs[si][row:row + 1, :]
            outs[4 * i][...] = g
            outs[4 * i + 1][...], outs[4 * i + 2][...], outs[4 * i + 3][...] = _adamw_terms(w_ref[...], g, m_ref[...], v_ref[...])
        for i in range(nd):
            w_ref, m_ref, v_ref, g_ref = ins[3 * nr + 4 * i:3 * nr + 4 * i + 4]
            o = outs[4 * (nr + i):4 * (nr + i) + 4]
            g = g_ref[...]
            o[0][...] = g
            o[1][...], o[2][...], o[3][...] = _adamw_terms(w_ref[...], g, m_ref[...], v_ref[...])

    operands = list(sums)
    shapes = []
    for w, m, v, _, _ in row_of:
        operands += [w, m, v]
        shapes += [w.shape] * 4
    for w, m, v, g in direct:
        operands += [w, m, v, g]
        shapes += [w.shape] * 4
    flat = _pallas_call(
        body, name="adamw_replicated", grid=(1,), in_specs=[_full(a.shape) for a in operands],
        out_specs=[_full(s) for s in shapes], out_shape=[_sds(s) for s in shapes],
        compiler_params=_params(56),
    )(*operands)
    return [flat[4 * i:4 * i + 4] for i in range(nr + nd)]


class _Exchanges:
    def __init__(self, shards, conv_w, chip, core, apply):
        self.shards, self.conv_w, self.apply = shards, conv_w, apply
        self.active, self.calls = [], 0
        self.core_idx = jnp.reshape(core, (1,)).astype(jnp.int32)
        self.chip_core_idx = jnp.stack([chip, core]).astype(jnp.int32)

    def first(self):
        names = ["w_in", "w_glu"]
        got = _run_now("gather_first", _gather_group([self.shards[n] for n in names] + [self.conv_w],
                                                     [True, True, False]))
        out = dict(zip(names, got))
        out["conv_w"] = jnp.transpose(got[2], (1, 0, 2)).reshape(4, LW)
        return out

    def gather(self, names):
        return _gather_group([self.shards[n] for n in names], [True] * len(names))

    def reduce(self, tag, grads):
        self.active.append({"tag": tag, "names": list(grads), "stage": 0, "grads": list(grads.values())})

    def run(self, call):
        groups = self.active
        carries = [self._exchange_of(g) for g in groups]
        carry = _combine(carries)
        outs = list(call(carry))
        own = len(outs) - len(carry.out_shapes)
        landed = outs[own:]
        for g, c in zip(groups, carries):
            self._sum_after(g, landed[:len(c.out_shapes)])
            landed = landed[len(c.out_shapes):]
        self.active = [g for g in groups if g["stage"] < 3]
        return outs[:own]

    def _exchange_of(self, g):
        if g["stage"] == 0:
            return _swap_group(g["grads"])
        if g["stage"] == 1:
            return _exchange_group(g["bf16"])
        return _join_group(g["halves"])

    def _sum_after(self, g, landed):
        if g["stage"] == 0:
            g["f32"], g["bf16"] = _add_sibling_group(g["tag"], self.chip_core_idx, g["grads"], landed)
        elif g["stage"] == 1:
            g["halves"] = _add_chips_group(g["tag"], self.chip_core_idx, g["f32"], landed)
        else:
            self.apply(g["tag"], g["names"], [t.reshape(2 * t.shape[1], t.shape[2]) for t in landed])
        g["stage"] += 1

    def drain(self):
        while self.active:
            self.calls += 1
            self.run(lambda carry: _run_now("reduce_%d" % self.calls, carry))


INPUT_NAMES = (["x", "p"] + [n for n in
               ["g_mix", "w_in", "b_in", "lam_re", "lam_im", "log_dt", "s5_b_re", "s5_b_im", "s5_c_re", "s5_c_im", "s5_d",
                "w_glu", "b_glu", "conv_w", "conv_b", "w_r", "b_r", "w_i", "b_i", "lru_lambda", "w_a_out", "w_b_out", "w_o",
                "g_ffn", "w_ffn_gate", "w_ffn_up", "w_ffn_down", "g_ple_gate", "w_ple_gate", "b_ple_gate", "w_ple", "g_ple",
                "g_final"]])
WEIGHT_NAMES = INPUT_NAMES[2:]


def kernel(*args):
    names = INPUT_NAMES + ["loss_target"] + ["m_" + n for n in WEIGHT_NAMES] + ["v_" + n for n in WEIGHT_NAMES]
    assert len(args) == len(names)
    given = dict(zip(names, args))

    def view(name):
        a = given[name]
        return jnp.swapaxes(a, -1, -2) if name.endswith(TRANSPOSED) else a

    def unview(name, a):
        return jnp.swapaxes(a, -1, -2) if name in TRANSPOSED else a

    def local(name):
        return view(name) if name.endswith("g_final") else view(name)[0]

    xi, yi, ci = _mesh_pos()
    k0 = 2 * xi + yi
    x, p, tgt = given["x"][0], given["p"][0, 0], given["loss_target"][0]

    results = {}

    row_halves = {}

    def apply(tag, names, totals):
        totals = dict(zip(names, totals))
        row_halves.update({n: totals.pop(n) for n in names if n in ("w_in_lo", "w_in_hi")})
        if len(row_halves) == 2:
            totals["w_in"] = jnp.concatenate([row_halves.pop("w_in_lo"), row_halves.pop("w_in_hi")])
        names = list(totals)
        if not names:
            return
        new = _adamw_group(tag, [local(n) for n in names], list(totals.values()), [local("m_" + n) for n in names],
                           [local("v_" + n) for n in names])
        for kind, arrays in zip(("grad", "delta", "new_m", "new_v"), new):
            for n, arr in zip(names, arrays):
                results[kind, n] = unview(n, arr[None])

    comm = _Exchanges({n: local(n).astype(BF) for n, _ in SHARDED}, local("conv_w"), k0, ci, apply)
    w = {n: local(n) for n in WEIGHT_NAMES if n != "conv_w" and n not in dict(SHARDED)}
    gx, sums, blocks = _local_step(x, p, tgt, w, comm)

    sum_names, block_names = list(sums), list(blocks)
    red = _allreduce_small([sums[n] for n in sum_names] + [blocks[n] for n in block_names],
                           [F32] * len(sum_names) + [BF] * len(block_names))
    sums = dict(zip(sum_names, red[:len(sum_names)]))
    blocks = dict(zip(block_names, red[len(sum_names):]))
    loss = jnp.sum(sums[LOSS_ROW[0]][LOSS_ROW[1]])
    direct_g = _replicated_grads(w, sums, blocks)
    conv_rows = sums[CONV_W_ROWS[0]][CONV_W_ROWS[1]:CONV_W_ROWS[1] + 4]
    direct_g["conv_w"] = lax.dynamic_slice(conv_rows, (0, k0 * CONV_SHARD[1]), CONV_SHARD)
    as_row = lambda a: a.reshape(1, -1)
    row_names = list(ACC_ROWS)
    row_of = [(as_row(given[n]), as_row(given["m_" + n]), as_row(given["v_" + n]),
               sum_names.index(ACC_ROWS[n][0]), ACC_ROWS[n][1]) for n in row_names]
    direct_names = list(direct_g)
    direct = [(view(n), view("m_" + n), view("v_" + n), direct_g[n].reshape(view(n).shape)) for n in direct_names]
    done = _adamw_replicated([sums[n] for n in sum_names], row_of, direct)
    for n, four in zip(row_names + direct_names, done):
        for kind, arr in zip(("grad", "delta", "new_m", "new_v"), four):
            results[kind, n] = unview(n, arr).reshape(given[n].shape)

    out = [loss, gx[None]]
    for kind in ("grad", "delta", "new_m", "new_v"):
        out += [results[kind, n] for n in WEIGHT_NAMES]
    return tuple(out)
```

```python
import functools
import math

import jax
import jax.numpy as jnp
from jax import lax
from jax.experimental import pallas as pl
from jax.experimental.pallas import tpu as pltpu

F32 = jnp.float32
BF = jnp.bfloat16

D = 1024
S5W = 512
NG, NS, NP = 32, 64, 16
GN = NG * NS
LW = 1024
NH, HD = 16, 64
LRU_C = 8.0
FH = 2816
NCHIP = 4
FC = FH // NCHIP
PLE = 256
INC = S5W + LW + 2 * D
EPS = 1e-6
ADAM_LR, ADAM_B1, ADAM_B2, ADAM_EPS, ADAM_WD, ADAM_STEP = 0.001, 0.9, 0.999, 1e-08, 0.01, 10

TM = 256
TK = 1024
LC = 512
SUB = 8
VMEM_MB = 1024 * 1024
MESH = pl.DeviceIdType.MESH
ANY = pl.BlockSpec(memory_space=pl.ANY)


def _mm(a, b):
    return jnp.dot(a.astype(BF), b.astype(BF), preferred_element_type=F32)


def _mm_nt(a, b):
    return lax.dot_general(a.astype(BF), b.astype(BF), (((1,), (1,)), ((), ())), preferred_element_type=F32)


def _mm_tn(a, b):
    return lax.dot_general(a.astype(BF), b.astype(BF), (((0,), (0,)), ((), ())), preferred_element_type=F32)


def _blockdiag_mm(x, blocks_ref):
    n, rows, _ = blocks_ref.shape
    return jnp.concatenate([jnp.dot(x[:, j * rows:(j + 1) * rows], blocks_ref[j], preferred_element_type=F32)
                            for j in range(n)], axis=1)


def _blockdiag_mm_t(x, blocks_ref):
    n, _, wide = blocks_ref.shape
    return jnp.concatenate([lax.dot_general(x[:, j * wide:(j + 1) * wide], blocks_ref[j], (((1,), (1,)), ((), ())),
                                            preferred_element_type=F32) for j in range(n)], axis=1)


def _rms(x):
    r = lax.rsqrt(jnp.mean(x * x, axis=-1, keepdims=True) + EPS)
    return x * r, r


def _rms_bwd(dy, xh, r, g):
    dxh = dy * g
    return r * (dxh - xh * jnp.mean(dxh * xh, axis=-1, keepdims=True))


def _colsum(x):
    return jnp.sum(x, axis=0, keepdims=True)


def _sig(x):
    return jax.nn.sigmoid(x)


def _gelu_grad(x):
    c = math.sqrt(2.0 / math.pi)
    t = jnp.tanh(c * (x + 0.044715 * x * x * x))
    return 0.5 * (1.0 + t) + 0.5 * x * (1.0 - t * t) * c * (1.0 + 3.0 * 0.044715 * x * x)


def _neg_expm1(x):
    series = -x * (1.0 + x * (0.5 + x * (1.0 / 6.0 + x * (1.0 / 24.0))))
    return jnp.where(x > -0.03, series, 1.0 - jnp.exp(x))


def _tok(width):
    return pl.BlockSpec((TM, width), lambda i: (i, 0))


def _tok_rev(width, nt):
    return pl.BlockSpec((TM, width), lambda i: (nt - 1 - i, 0))


def _full(shape):
    return pl.BlockSpec(shape, lambda i: (0,) * len(shape))


def _params(vmem_mb, **kw):
    return pltpu.CompilerParams(dimension_semantics=("arbitrary",), vmem_limit_bytes=vmem_mb * VMEM_MB, **kw)


def _sds(shape, dtype=F32):
    return jax.ShapeDtypeStruct(shape, dtype)


class _Carried:
    def __init__(self, operands, out_shapes, sems, start, finish, aliases=None):
        self.operands, self.out_shapes, self.sems = list(operands), list(out_shapes), list(sems)
        self.start, self.finish, self.aliases = start, finish, dict(aliases or {})


def _in_hbm(arrays):
    return [pltpu.with_memory_space_constraint(a, pltpu.HBM) for a in arrays]


def _pallas_call(body, carry=None, **kw):
    if carry is None:
        return pl.pallas_call(body, **kw)

    def at_step(corner):
        hit = [pl.program_id(d) == (size - 1 if corner else 0) for d, size in enumerate(kw["grid"])]
        return functools.reduce(jnp.logical_and, hit)

    name, grid, compiler_params = kw["name"], kw["grid"], kw["compiler_params"]
    in_specs, out_specs, out_shape = list(kw["in_specs"]), list(kw["out_specs"]), list(kw["out_shape"])
    scratch_shapes = list(kw.get("scratch_shapes", ()))
    n_in, n_out, n_scr = len(in_specs), len(out_specs), len(scratch_shapes)
    c_in, c_out = len(carry.operands), len(carry.out_shapes)

    def full_body(*refs):
        ins, refs = refs[:n_in], refs[n_in:]
        c_ins, refs = refs[:c_in], refs[c_in:]
        outs, refs = refs[:n_out], refs[n_out:]
        c_outs, refs = refs[:c_out], refs[c_out:]
        scratch, c_sems = refs[:n_scr], refs[n_scr:]

        @pl.when(at_step(0))
        def _():
            carry.start(c_ins, c_outs, c_sems)

        body(*ins, *outs, *scratch)

        @pl.when(at_step(1))
        def _():
            carry.finish(c_ins, c_outs, c_sems)

    call = pl.pallas_call(
        full_body, name=name, grid=grid, in_specs=in_specs + [ANY] * c_in, out_specs=out_specs + [ANY] * c_out,
        out_shape=out_shape + list(carry.out_shapes), scratch_shapes=scratch_shapes + list(carry.sems),
        input_output_aliases={n_in + i: n_out + o for i, o in carry.aliases.items()},
        compiler_params=compiler_params)
    return lambda *operands: call(*operands, *_in_hbm(carry.operands))


def _resident(pairs, sems):
    first = pl.program_id(0) == 0
    copies = [pltpu.make_async_copy(src, dst, sems.at[j]) for j, (src, dst) in enumerate(pairs)]

    @pl.when(first)
    def _():
        for cp in copies:
            cp.start()

    def wait(j):
        @pl.when(first)
        def _():
            copies[j].wait()

    return wait


def _resident_now(pairs, sems):
    @pl.when(pl.program_id(0) == 0)
    def _():
        copies = [pltpu.make_async_copy(src, dst, sems.at[j]) for j, (src, dst) in enumerate(pairs)]
        for cp in copies:
            cp.start()
        for cp in copies:
            cp.wait()


def _row_iota(width):
    return lax.broadcasted_iota(jnp.int32, (SUB, width), 0)


def _bcast_row(x, row):
    return jnp.broadcast_to(x[row:row + 1, :], x.shape)


def _slab(k):
    return pl.ds(pl.multiple_of(k * SUB, SUB), SUB)


QC = INC // NCHIP
Z_PARTS = ((0, S5W), (S5W, S5W + LW), (S5W + LW, INC))


def _inproj_fwd(x, g_mix, w_in, b_in, carry=None):
    L = x.shape[0]

    def body(x_ref, g_ref, w_hbm, b_ref, h_ref, ua_ref, ub_ref, gp_ref, w_vm, w_sems):
        _resident_now([(w_hbm.at[k], w_vm.at[k]) for k in range(NCHIP)], w_sems)
        xh, _ = _rms(x_ref[...])
        h = (xh * g_ref[...]).astype(BF)
        h_ref[...] = h
        for k in range(NCHIP):
            lo, hi = k * QC, (k + 1) * QC
            z = jnp.dot(h, w_vm[k], preferred_element_type=F32) + b_ref[:, lo:hi]
            for ref, (a, b) in zip((ua_ref, ub_ref, gp_ref), Z_PARTS):
                s, e = max(lo, a), min(hi, b)
                if s < e:
                    ref[:, s - a:e - a] = z[:, s - lo:e - lo]

    return _pallas_call(
        body, carry, name="inproj_fwd", grid=(L // TM,),
        in_specs=[_tok(D), _full((1, D)), ANY, _full((1, INC))],
        out_specs=[_tok(D), _tok(S5W), _tok(LW), _tok(2 * D)],
        out_shape=[_sds((L, D), BF), _sds((L, S5W)), _sds((L, LW)), _sds((L, 2 * D))],
        scratch_shapes=[pltpu.VMEM((NCHIP, D, QC), BF), pltpu.SemaphoreType.DMA((NCHIP,))],
        compiler_params=_params(40),
    )(x, g_mix, w_in, b_in)


def _inproj_bwd(x, dx1, dua, dub, dgp, g_mix, w_in, carry=None):
    L = x.shape[0]

    def body(x_ref, dx1_ref, dua_ref, dub_ref, dgp_ref, g_ref, w_hbm, gx_ref, dz_ref, dg_ref, db_ref, w_vm, w_sems):
        _resident_now([(w_hbm.at[k], w_vm.at[k]) for k in range(NCHIP)], w_sems)

        @pl.when(pl.program_id(0) == 0)
        def _():
            dg_ref[...] = jnp.zeros_like(dg_ref)
            db_ref[...] = jnp.zeros_like(db_ref)

        for src, (a, b) in zip((dua_ref, dub_ref, dgp_ref), Z_PARTS):
            d = src[...]
            dz_ref[:, a:b] = d.astype(BF)
            db_ref[0:1, a:b] += _colsum(d)
        dh = jnp.zeros((TM, D), F32)
        for k in range(NCHIP):
            dh = dh + lax.dot_general(dz_ref[:, k * QC:(k + 1) * QC], w_vm[k], (((1,), (1,)), ((), ())),
                                      preferred_element_type=F32)
        xh, r = _rms(x_ref[...])
        dg_ref[0:1, :] += _colsum(dh * xh)
        gx_ref[...] = dx1_ref[...] + _rms_bwd(dh, xh, r, g_ref[...])

    return _pallas_call(
        body, carry, name="inproj_bwd", grid=(L // TM,),
        in_specs=[_tok(D), _tok(D), _tok(S5W), _tok(LW), _tok(2 * D), _full((1, D)), ANY],
        out_specs=[_tok(D), _tok(INC), _full((SUB, D)), _full((SUB, INC))],
        out_shape=[_sds((L, D)), _sds((L, INC), BF), _sds((SUB, D)), _sds((SUB, INC))],
        scratch_shapes=[pltpu.VMEM((NCHIP, D, QC), BF), pltpu.SemaphoreType.DMA((NCHIP,))],
        compiler_params=_params(40),
    )(x, dx1, dua, dub, dgp, g_mix, w_in)


def _cscan(xr_ref, xi_ref, con_ref, cr_ref, ci_ref, reverse):
    n_slab = xr_ref.shape[0] // SUB
    width = xr_ref.shape[1]
    for lc in range(width // LC):
        cols = slice(lc * LC, (lc + 1) * LC)
        con = [con_ref[SUB * j:SUB * (j + 1), cols] for j in range(8)]

        def step(k, carry, cols=cols, con=con):
            cr, ci = carry
            rows = _slab(n_slab - 1 - k if reverse else k)
            xr, xi = xr_ref[rows, cols], xi_ref[rows, cols]
            for j, sh in enumerate((1, 2, 4)):
                mr, mi = con[2 * j], con[2 * j + 1]
                pr = pltpu.roll(xr, SUB - sh if reverse else sh, 0)
                pi = pltpu.roll(xi, SUB - sh if reverse else sh, 0)
                xr, xi = xr + mr * pr - mi * pi, xi + mr * pi + mi * pr
            xr, xi = xr + con[6] * cr - con[7] * ci, xi + con[6] * ci + con[7] * cr
            xr_ref[rows, cols] = xr
            xi_ref[rows, cols] = xi
            row = 0 if reverse else SUB - 1
            return _bcast_row(xr, row), _bcast_row(xi, row)

        cr, ci = lax.fori_loop(0, n_slab, step, (cr_ref[:, cols], ci_ref[:, cols]))
        cr_ref[:, cols] = cr
        ci_ref[:, cols] = ci


def _s5_fwd(ua, bbr, bbi, ccr, cci, dsk, con, w_glu, b_glu, carry=None):
    L = ua.shape[0]

    def body(ua_ref, bbr_hbm, bbi_hbm, ccr_hbm, cci_hbm, dsk_ref, con_ref, wg_ref, bg_ref,
             sr_ref, si_ref, y_ref, zg_ref, ya_ref, bbr_vm, bbi_vm, ccr_vm, cci_vm, cr_ref, ci_ref, w_sems):
        landed = _resident([(bbr_hbm, bbr_vm), (bbi_hbm, bbi_vm), (ccr_hbm, ccr_vm), (cci_hbm, cci_vm)], w_sems)

        @pl.when(pl.program_id(0) == 0)
        def _():
            cr_ref[...] = jnp.zeros_like(cr_ref)
            ci_ref[...] = jnp.zeros_like(ci_ref)

        u = ua_ref[...]
        ub = u.astype(BF)
        landed(0)
        sr_ref[...] = _blockdiag_mm(ub, bbr_vm)
        landed(1)
        si_ref[...] = _blockdiag_mm(ub, bbi_vm)
        _cscan(sr_ref, si_ref, con_ref, cr_ref, ci_ref, reverse=False)
        landed(2)
        landed(3)
        y = (_blockdiag_mm_t(sr_ref[...].astype(BF), ccr_vm) - _blockdiag_mm_t(si_ref[...].astype(BF), cci_vm)
             + dsk_ref[...] * u)
        y_ref[...] = y
        zg = jax.nn.gelu(y)
        zg_ref[...] = zg.astype(BF)
        q = _mm(zg, wg_ref[...]) + bg_ref[...]
        ya_ref[...] = (zg * _sig(q)).astype(BF)

    return _pallas_call(
        body, carry, name="s5_fwd", grid=(L // TM,),
        in_specs=[_tok(S5W), ANY, ANY, ANY, ANY, _full((1, S5W)), _full((8 * SUB, GN)),
                  _full((S5W, S5W)), _full((1, S5W))],
        out_specs=[_tok(GN), _tok(GN), _tok(S5W), _tok(S5W), _tok(S5W)],
        out_shape=[_sds((L, GN)), _sds((L, GN)), _sds((L, S5W)), _sds((L, S5W), BF), _sds((L, S5W), BF)],
        scratch_shapes=[pltpu.VMEM((S5W // 128, 128, GN // (S5W // 128)), BF)] * 4 + [
                        pltpu.VMEM((SUB, GN), F32), pltpu.VMEM((SUB, GN), F32),
                        pltpu.SemaphoreType.DMA((4,))],
        compiler_params=_params(44),
    )(ua, bbr, bbi, ccr, cci, dsk, con, w_glu, b_glu)


def _s5_bwd(dya, y, ua, sr, si, bbr, bbi, ccr, cci, dsk, con_rev, w_glu, b_glu, carry=None):
    L = ua.shape[0]
    nt = L // TM
    spt = TM // SUB
    n_slab = spt

    def halo_map(i):
        return (jnp.maximum((nt - 1 - i) * spt - 1, 0), 0)

    def body(dya_ref, y_ref, ua_ref, sr_ref, si_ref, hr_ref, hi_ref, bbr_hbm, bbi_hbm, ccr_hbm, cci_hbm,
             dsk_ref, con_ref, wg_ref, bg_ref,
             dua_ref, dq_ref, dy_ref, lr_ref, li_ref, da_ref, dsm_ref,
             bbr_vm, bbi_vm, ccr_vm, cci_vm, cr_ref, ci_ref, w_sems):
        i = pl.program_id(0)
        landed = _resident([(ccr_hbm, ccr_vm), (cci_hbm, cci_vm), (bbr_hbm, bbr_vm), (bbi_hbm, bbi_vm)], w_sems)

        @pl.when(i == 0)
        def _():
            cr_ref[...] = jnp.zeros_like(cr_ref)
            ci_ref[...] = jnp.zeros_like(ci_ref)
            da_ref[...] = jnp.zeros_like(da_ref)
            dsm_ref[...] = jnp.zeros_like(dsm_ref)

        u = ua_ref[...]
        yv = y_ref[...]
        dya = dya_ref[...]
        zg = jax.nn.gelu(yv)
        sg = _sig(_mm(zg, wg_ref[...]) + bg_ref[...])
        dq = dya * zg * sg * (1.0 - sg)
        dq_ref[...] = dq.astype(BF)
        dzg = dya * sg + _mm_nt(dq, wg_ref[...])
        dy = dzg * _gelu_grad(yv)
        dyb = dy.astype(BF)
        dy_ref[...] = dyb
        dsm_ref[0:1, :] += _colsum(dy * u)
        dsm_ref[1:2, :] += _colsum(dq)
        landed(0)
        lr_ref[...] = _blockdiag_mm(dyb, ccr_vm)
        landed(1)
        li_ref[...] = -_blockdiag_mm(dyb, cci_vm)
        _cscan(lr_ref, li_ref, con_ref, cr_ref, ci_ref, reverse=True)

        first_tile = (i == nt - 1)
        row = _row_iota(LC)
        for lc in range(GN // LC):
            cols = slice(lc * LC, (lc + 1) * LC)
            h_r = jnp.where(first_tile, 0.0, hr_ref[:, cols])
            h_i = jnp.where(first_tile, 0.0, hi_ref[:, cols])

            def step(k, acc, cols=cols, h_r=h_r, h_i=h_i):
                ar, ai = acc
                rows = _slab(k)
                prev = _slab(jnp.maximum(k - 1, 0))
                pr = jnp.where(k == 0, h_r, sr_ref[prev, cols])
                pi = jnp.where(k == 0, h_i, si_ref[prev, cols])
                spr = pltpu.roll(jnp.where(row == SUB - 1, pr, sr_ref[rows, cols]), 1, 0)
                spi = pltpu.roll(jnp.where(row == SUB - 1, pi, si_ref[rows, cols]), 1, 0)
                lr, li = lr_ref[rows, cols], li_ref[rows, cols]
                return ar + lr * spr + li * spi, ai + li * spr - lr * spi

            zero = jnp.zeros((SUB, LC), F32)
            ar, ai = lax.fori_loop(0, n_slab, step, (zero, zero))
            da_ref[0:1, cols] += _colsum(ar)
            da_ref[1:2, cols] += _colsum(ai)

        landed(2)
        landed(3)
        dua_ref[...] = (dy * dsk_ref[...] + _blockdiag_mm_t(lr_ref[...].astype(BF), bbr_vm)
                        + _blockdiag_mm_t(li_ref[...].astype(BF), bbi_vm))

    return _pallas_call(
        body, carry, name="s5_bwd", grid=(nt,),
        in_specs=[_tok_rev(S5W, nt), _tok_rev(S5W, nt), _tok_rev(S5W, nt), _tok_rev(GN, nt), _tok_rev(GN, nt),
                  pl.BlockSpec((SUB, GN), halo_map), pl.BlockSpec((SUB, GN), halo_map),
                  ANY, ANY, ANY, ANY, _full((1, S5W)), _full((8 * SUB, GN)), _full((S5W, S5W)), _full((1, S5W))],
        out_specs=[_tok_rev(S5W, nt), _tok_rev(S5W, nt), _tok_rev(S5W, nt), _tok_rev(GN, nt), _tok_rev(GN, nt),
                   _full((SUB, GN)), _full((SUB, S5W))],
        out_shape=[_sds((L, S5W)), _sds((L, S5W), BF), _sds((L, S5W), BF), _sds((L, GN)), _sds((L, GN)),
                   _sds((SUB, GN)), _sds((SUB, S5W))],
        scratch_shapes=[pltpu.VMEM((S5W // 128, 128, GN // (S5W // 128)), BF)] * 4 + [
                        pltpu.VMEM((SUB, GN), F32), pltpu.VMEM((SUB, GN), F32),
                        pltpu.SemaphoreType.DMA((4,))],
        compiler_params=_params(52),
    )(dya, y, ua, sr, si, sr, si, bbr, bbi, ccr, cci, dsk, con_rev, w_glu, b_glu)


def _lru_gate_terms(rg, sp):
    log_a = -LRU_C * rg * sp
    a = jnp.exp(log_a)
    mult = jnp.sqrt(_neg_expm1(2.0 * log_a))
    return a, mult


def _lru_fwd(ub, conv_w, conv_b, wr, wi, b_r, b_i, sp, carry=None):
    L = ub.shape[0]
    n_slab = TM // SUB

    def body(ub_ref, cw_ref, cb_ref, wr_ref, wi_ref, br_ref, bi_ref, sp_ref,
             xc_ref, rg_ref, ig_ref, h_ref, hp_ref, a_ref, halo_ref, carry_ref):
        @pl.when(pl.program_id(0) == 0)
        def _():
            halo_ref[...] = jnp.zeros_like(halo_ref)
            carry_ref[...] = jnp.zeros_like(carry_ref)

        row = _row_iota(LW)
        taps = [cw_ref[k:k + 1, :] for k in range(4)]
        cb = cb_ref[...]

        def conv_step(k, prev):
            rows = _slab(k)
            cur = ub_ref[rows, :]
            acc = taps[3] * cur + cb
            for j in (1, 2, 3):
                acc = acc + taps[3 - j] * pltpu.roll(jnp.where(row >= SUB - j, prev, cur), j, 0)
            xc_ref[rows, :] = acc
            return cur

        halo_ref[...] = lax.fori_loop(0, n_slab, conv_step, halo_ref[...])

        xc = xc_ref[...]
        xcb = xc.astype(BF)
        rg = _sig(_blockdiag_mm(xcb, wr_ref) + br_ref[...])
        ig = _sig(_blockdiag_mm(xcb, wi_ref) + bi_ref[...])
        rg_ref[...] = rg
        ig_ref[...] = ig
        a, mult = _lru_gate_terms(rg, sp_ref[...])
        a_ref[...] = a
        h_ref[...] = mult * ig * xc

        rowc = _row_iota(LC)
        for lc in range(LW // LC):
            cols = slice(lc * LC, (lc + 1) * LC)

            def step(k, c, cols=cols):
                rows = _slab(k)
                av, b = a_ref[rows, cols], h_ref[rows, cols]
                for sh in (1, 2, 4):
                    keep = rowc >= sh
                    b = b + av * jnp.where(keep, pltpu.roll(b, sh, 0), 0.0)
                    av = av * jnp.where(keep, pltpu.roll(av, sh, 0), 1.0)
                h = b + av * c
                h_ref[rows, cols] = h
                hp_ref[rows, cols] = jnp.where(rowc == 0, c, pltpu.roll(h, 1, 0))
                return _bcast_row(h, SUB - 1)

            carry_ref[:, cols] = lax.fori_loop(0, n_slab, step, carry_ref[:, cols])

    return _pallas_call(
        body, carry, name="lru_fwd", grid=(L // TM,),
        in_specs=[_tok(LW), _full((4, LW)), _full((1, LW)), _full((LW // 128, 128, 128)), _full((LW // 128, 128, 128)),
                  _full((1, LW)), _full((1, LW)), _full((1, LW))],
        out_specs=[_tok(LW)] * 5,
        out_shape=[_sds((L, LW))] * 5,
        scratch_shapes=[pltpu.VMEM((TM, LW), F32), pltpu.VMEM((SUB, LW), F32), pltpu.VMEM((SUB, LW), F32)],
        compiler_params=_params(40),
    )(ub, conv_w, conv_b, wr, wi, b_r, b_i, sp)


def _lru_bwd(dyb, xc, rg, ig, hp, ub, conv_w, wr, wi, sp, dsp, carry=None):
    L = ub.shape[0]
    nt = L // TM
    spt = TM // SUB
    n_slab = spt

    def halo_map(i):
        return (jnp.maximum((nt - 1 - i) * spt - 1, 0), 0)

    def body(dh_ref, xc_ref, rg_ref, ig_ref, hp_ref, ub_ref, uh_ref, cw_ref, wr_ref, wi_ref, sp_ref, dsp_ref,
             dub_ref, dpr_ref, dpi_ref, acc_ref, a_ref, lam_ref, dxc_ref, carry_ref, next_ref):
        i = pl.program_id(0)

        @pl.when(i == 0)
        def _():
            carry_ref[...] = jnp.zeros_like(carry_ref)
            next_ref[...] = jnp.zeros_like(next_ref)
            acc_ref[...] = jnp.zeros_like(acc_ref)

        sp = sp_ref[...]
        rg, ig, xc = rg_ref[...], ig_ref[...], xc_ref[...]
        a, mult = _lru_gate_terms(rg, sp)
        a_ref[...] = a

        rowc = _row_iota(LC)
        for lc in range(LW // LC):
            cols = slice(lc * LC, (lc + 1) * LC)

            def step(k, c, cols=cols):
                rows = _slab(n_slab - 1 - k)
                av, dh = a_ref[rows, cols], dh_ref[rows, cols]
                b = av * dh
                for sh in (1, 2, 4):
                    keep = rowc < SUB - sh
                    b = b + av * jnp.where(keep, pltpu.roll(b, SUB - sh, 0), 0.0)
                    av = av * jnp.where(keep, pltpu.roll(av, SUB - sh, 0), 1.0)
                mu = b + av * c
                lam_ref[rows, cols] = dh + jnp.where(rowc == SUB - 1, c, pltpu.roll(mu, SUB - 1, 0))
                return _bcast_row(mu, 0)

            carry_ref[:, cols] = lax.fori_loop(0, n_slab, step, carry_ref[:, cols])

        lam = lam_ref[...]
        d_a = lam * hp_ref[...]
        d_mult = lam * ig * xc
        d_ig = lam * mult * xc
        dxc = lam * mult * ig
        d_log_a = d_a * a - d_mult * a * a / mult
        d_rg = (-LRU_C) * sp * d_log_a
        acc_ref[0:1, :] += _colsum((-LRU_C) * rg * d_log_a) * dsp_ref[...]
        dpr = d_rg * rg * (1.0 - rg)
        dpi = d_ig * ig * (1.0 - ig)
        acc_ref[1:2, :] += _colsum(dpr)
        acc_ref[2:3, :] += _colsum(dpi)
        dprb, dpib = dpr.astype(BF), dpi.astype(BF)
        dpr_ref[...] = dprb
        dpi_ref[...] = dpib
        dxc = dxc + _blockdiag_mm_t(dprb, wr_ref) + _blockdiag_mm_t(dpib, wi_ref)
        dxc_ref[...] = dxc
        acc_ref[3:4, :] += _colsum(dxc)

        row = _row_iota(LW)
        taps = [cw_ref[k:k + 1, :] for k in range(4)]
        u_halo = jnp.where(i == nt - 1, 0.0, uh_ref[...])
        nxt_tile = next_ref[...]

        def conv_step(k, accs):
            rows = _slab(k)
            cur = dxc_ref[rows, :]
            nxt = jnp.where(k == n_slab - 1, nxt_tile, dxc_ref[_slab(jnp.minimum(k + 1, n_slab - 1)), :])
            ucur = ub_ref[rows, :]
            uprev = jnp.where(k == 0, u_halo, ub_ref[_slab(jnp.maximum(k - 1, 0)), :])
            du = taps[3] * cur
            new = [accs[3] + cur * ucur]
            for j in (1, 2, 3):
                du = du + taps[3 - j] * pltpu.roll(jnp.where(row < j, nxt, cur), SUB - j, 0)
                new.append(accs[3 - j] + cur * pltpu.roll(jnp.where(row >= SUB - j, uprev, ucur), j, 0))
            dub_ref[rows, :] = du
            return tuple(new[::-1])

        zero = jnp.zeros((SUB, LW), F32)
        accs = lax.fori_loop(0, n_slab, conv_step, (zero, zero, zero, zero))
        for k in range(4):
            acc_ref[4 + k:5 + k, :] += _colsum(accs[k])
        next_ref[...] = dxc_ref[0:SUB, :]

    return _pallas_call(
        body, carry, name="lru_bwd", grid=(nt,),
        in_specs=[_tok_rev(LW, nt)] * 6 + [pl.BlockSpec((SUB, LW), halo_map), _full((4, LW)),
                                           _full((LW // 128, 128, 128)), _full((LW // 128, 128, 128)), _full((1, LW)), _full((1, LW))],
        out_specs=[_tok_rev(LW, nt), _tok_rev(LW, nt), _tok_rev(LW, nt), _full((SUB, LW))],
        out_shape=[_sds((L, LW)), _sds((L, LW), BF), _sds((L, LW), BF), _sds((SUB, LW))],
        scratch_shapes=[pltpu.VMEM((TM, LW), F32), pltpu.VMEM((TM, LW), F32), pltpu.VMEM((TM, LW), F32),
                        pltpu.VMEM((SUB, LW), F32), pltpu.VMEM((SUB, LW), F32)],
        compiler_params=_params(48),
    )(dyb, xc, rg, ig, hp, ub, ub, conv_w, wr, wi, sp, dsp)


AC = D // NCHIP


def _merge_fwd(x, ya, yb, gp, w_a, w_b, w_o, carry=None):
    L = x.shape[0]

    def body(x_ref, ya_ref, yb_ref, gp_ref, wa_ref, wb_ref, wo_ref, x1_ref, pa_ref, pb_ref, mg_ref):
        ya = ya_ref[...]
        for k in range(NCHIP):
            pa_ref[:, k * AC:(k + 1) * AC] = jnp.dot(ya, wa_ref[k], preferred_element_type=F32)
        pb = _mm(yb_ref[...], wb_ref[...])
        pb_ref[...] = pb
        gp = gp_ref[...]
        merged = (_sig(gp[:, :D]) * pa_ref[...] + _sig(gp[:, D:]) * pb).astype(BF)
        mg_ref[...] = merged
        x1_ref[...] = x_ref[...] + jnp.dot(merged, wo_ref[...], preferred_element_type=F32)

    return _pallas_call(
        body, carry, name="merge_fwd", grid=(L // TM,),
        in_specs=[_tok(D), _tok(S5W), _tok(LW), _tok(2 * D), _full((NCHIP, S5W, AC)), _full((LW, D)), _full((D, D))],
        out_specs=[_tok(D), _tok(D), _tok(D), _tok(D)],
        out_shape=[_sds((L, D)), _sds((L, D)), _sds((L, D)), _sds((L, D), BF)],
        compiler_params=_params(40),
    )(x, ya, yb, gp, w_a, w_b, w_o)


def _merge_bwd(dx1, gp, pa, pb, w_a, w_b, w_o, carry=None):
    L = dx1.shape[0]

    def body(dx1_ref, gp_ref, pa_ref, pb_ref, wa_ref, wb_ref, wo_ref, dya_ref, dyb_ref, dgp_ref, dpa_ref, dpb_ref):
        dm = _mm_nt(dx1_ref[...], wo_ref[...])
        gp = gp_ref[...]
        sa, sb = _sig(gp[:, :D]), _sig(gp[:, D:])
        dpa = (dm * sa).astype(BF)
        dpb = (dm * sb).astype(BF)
        dpa_ref[...] = dpa
        dpb_ref[...] = dpb
        dgp_ref[:, :D] = dm * pa_ref[...] * sa * (1.0 - sa)
        dgp_ref[:, D:] = dm * pb_ref[...] * sb * (1.0 - sb)
        dya = jnp.zeros((TM, S5W), F32)
        for k in range(NCHIP):
            dya = dya + _mm_nt(dpa[:, k * AC:(k + 1) * AC], wa_ref[k])
        dya_ref[...] = dya
        dyb_ref[...] = _mm_nt(dpb, wb_ref[...])

    return _pallas_call(
        body, carry, name="merge_bwd", grid=(L // TM,),
        in_specs=[_tok(D), _tok(2 * D), _tok(D), _tok(D), _full((NCHIP, S5W, AC)), _full((LW, D)), _full((D, D))],
        out_specs=[_tok(S5W), _tok(LW), _tok(2 * D), _tok(D), _tok(D)],
        out_shape=[_sds((L, S5W)), _sds((L, LW)), _sds((L, 2 * D)), _sds((L, D), BF), _sds((L, D), BF)],
        compiler_params=_params(40),
    )(dx1, gp, pa, pb, w_a, w_b, w_o)


def _chunk_tok(width):
    return pl.BlockSpec((NCHIP, TM, width), lambda i: (0, i, 0))


def _ffn_fwd(x1, g_ffn, wg, wu, wd, carry=None):
    L = x1.shape[0]

    def body(x_ref, g_ref, wg_hbm, wu_hbm, wd_hbm, x2_ref, h2_ref, gg_ref, uu_ref, wg_vm, wu_vm, wd_vm, w_sems):
        _resident_now([(src.at[c], dst.at[c]) for c in range(NCHIP)
                       for src, dst in ((wg_hbm, wg_vm), (wu_hbm, wu_vm), (wd_hbm, wd_vm))], w_sems)
        x = x_ref[...]
        xh, _ = _rms(x)
        h2 = (xh * g_ref[...]).astype(BF)
        h2_ref[...] = h2
        out = x
        for c in range(NCHIP):
            gg = lax.dot_general(h2, wg_vm[c], (((1,), (1,)), ((), ())), preferred_element_type=F32)
            uu = lax.dot_general(h2, wu_vm[c], (((1,), (1,)), ((), ())), preferred_element_type=F32)
            gg_ref[c] = gg.astype(BF)
            uu_ref[c] = uu.astype(BF)
            act = (gg * _sig(gg) * uu).astype(BF)
            out = out + jnp.dot(act, wd_vm[c], preferred_element_type=F32)
        x2_ref[...] = out

    return _pallas_call(
        body, carry, name="ffn_fwd", grid=(L // TM,),
        in_specs=[_tok(D), _full((1, D)), ANY, ANY, ANY],
        out_specs=[_tok(D), _tok(D), _chunk_tok(FC), _chunk_tok(FC)],
        out_shape=[_sds((L, D)), _sds((L, D), BF), _sds((NCHIP, L, FC), BF), _sds((NCHIP, L, FC), BF)],
        scratch_shapes=[pltpu.VMEM((NCHIP, FC, D), BF)] * 3 + [pltpu.SemaphoreType.DMA((3 * NCHIP,))],
        compiler_params=_params(52),
    )(x1, g_ffn, wg, wu, wd)


def _ffn_bwd(x1, dx2, gg, uu, g_ffn, wg, wu, wd, carry=None):
    L = x1.shape[0]

    def body(x_ref, dx2_ref, gg_ref, uu_ref, g_ref, wg_hbm, wu_hbm, wd_hbm,
             dx1_ref, act_ref, dgg_ref, duu_ref, dg_ref, wg_vm, wu_vm, wd_vm, w_sems):
        _resident_now([(src.at[c], dst.at[c]) for c in range(NCHIP)
                       for src, dst in ((wg_hbm, wg_vm), (wu_hbm, wu_vm), (wd_hbm, wd_vm))], w_sems)

        @pl.when(pl.program_id(0) == 0)
        def _():
            dg_ref[...] = jnp.zeros_like(dg_ref)

        dx2 = dx2_ref[...]
        dx2b = dx2.astype(BF)
        dh2 = jnp.zeros((TM, D), F32)
        for c in range(NCHIP):
            g = gg_ref[c].astype(F32)
            u = uu_ref[c].astype(F32)
            s = _sig(g)
            silu = g * s
            act_ref[c] = (silu * u).astype(BF)
            dact = lax.dot_general(dx2b, wd_vm[c], (((1,), (1,)), ((), ())), preferred_element_type=F32)
            dg = (dact * u * s * (1.0 + g * (1.0 - s))).astype(BF)
            du = (dact * silu).astype(BF)
            dgg_ref[c] = dg
            duu_ref[c] = du
            dh2 = dh2 + jnp.dot(dg, wg_vm[c], preferred_element_type=F32)
            dh2 = dh2 + jnp.dot(du, wu_vm[c], preferred_element_type=F32)
        xh, r = _rms(x_ref[...])
        dg_ref[0:1, :] += _colsum(dh2 * xh)
        dx1_ref[...] = dx2 + _rms_bwd(dh2, xh, r, g_ref[...])

    return _pallas_call(
        body, carry, name="ffn_bwd", grid=(L // TM,),
        in_specs=[_tok(D), _tok(D), _chunk_tok(FC), _chunk_tok(FC), _full((1, D)), ANY, ANY, ANY],
        out_specs=[_tok(D), _chunk_tok(FC), _chunk_tok(FC), _chunk_tok(FC), _full((SUB, D))],
        out_shape=[_sds((L, D)), _sds((NCHIP, L, FC), BF), _sds((NCHIP, L, FC), BF), _sds((NCHIP, L, FC), BF),
                   _sds((SUB, D))],
        scratch_shapes=[pltpu.VMEM((NCHIP, FC, D), BF)] * 3 + [pltpu.SemaphoreType.DMA((3 * NCHIP,))],
        compiler_params=_params(56),
    )(x1, dx2, gg, uu, g_ffn, wg, wu, wd)


def _ple_loss(x2, p, tgt, g_pg, w_pg, b_pg, w_ple, g_ple, g_final):
    L = x2.shape[0]

    def body(x2_ref, p_ref, t_ref, gpg_ref, wpg_ref, bpg_ref, wple_ref, gple_ref, gf_ref,
             dx2_ref, n2_ref, dpre_ref, de0_ref, acc_ref):
        @pl.when(pl.program_id(0) == 0)
        def _():
            acc_ref[...] = jnp.zeros_like(acc_ref)

        x2 = x2_ref[...]
        x2h, r2 = _rms(x2)
        n2 = (x2h * gpg_ref[...]).astype(BF)
        n2_ref[...] = n2
        gate = _sig(jnp.dot(n2, wpg_ref[...], preferred_element_type=F32) + bpg_ref[...])
        pb = p_ref[...].astype(BF)
        e0 = jnp.concatenate([jnp.dot(pb, wple_ref[k], preferred_element_type=F32) for k in range(NCHIP)], axis=1)
        e0h, re = _rms(e0)
        e = e0h * gple_ref[...]
        x3 = x2 + gate * e
        x3h, r3 = _rms(x3)
        diff = x3h * gf_ref[...] - t_ref[...]
        acc_ref[4:5, :] += _colsum(diff * diff) * (0.5 / D)
        dy = diff * (1.0 / D)
        acc_ref[3:4, :] += _colsum(dy * x3h)
        dx3 = _rms_bwd(dy, x3h, r3, gf_ref[...])
        de = dx3 * gate
        acc_ref[2:3, :] += _colsum(de * e0h)
        de0_ref[...] = _rms_bwd(de, e0h, re, gple_ref[...]).astype(BF)
        dpre = dx3 * e * gate * (1.0 - gate)
        acc_ref[1:2, :] += _colsum(dpre)
        dpreb = dpre.astype(BF)
        dpre_ref[...] = dpreb
        dn2 = lax.dot_general(dpreb, wpg_ref[...], (((1,), (1,)), ((), ())), preferred_element_type=F32)
        acc_ref[0:1, :] += _colsum(dn2 * x2h)
        dx2_ref[...] = dx3 + _rms_bwd(dn2, x2h, r2, gpg_ref[...])

    return _pallas_call(
        body, name="ple_loss", grid=(L // TM,),
        in_specs=[_tok(D), _tok(PLE), _tok(D), _full((1, D)), _full((D, D)), _full((1, D)), _full((NCHIP, PLE, AC)),
                  _full((1, D)), _full((1, D))],
        out_specs=[_tok(D), _tok(D), _tok(D), _tok(D), _full((SUB, D))],
        out_shape=[_sds((L, D)), _sds((L, D), BF), _sds((L, D), BF), _sds((L, D), BF), _sds((SUB, D))],
        compiler_params=_params(40),
    )(x2, p, tgt, g_pg, w_pg, b_pg, w_ple, g_ple, g_final)


def _tn(name, a, b, col_chunk=None, a_block=None, carry=None):
    L = a.shape[-2]
    m, n = a.shape[-1], b.shape[-1]
    a_col = 0
    if a_block is not None:
        a_col, m = a_block
    tk = L if (a.ndim == 3 or b.ndim == 3 or a_block is not None) else TK
    if a.ndim == 3 or b.ndim == 3:
        nj, bn = (a if a.ndim == 3 else b).shape[0], n
        a_spec = (pl.BlockSpec((None, tk, m), lambda j, t: (j, t, 0)) if a.ndim == 3
                  else pl.BlockSpec((tk, m), lambda j, t: (t, 0)))
        b_spec = (pl.BlockSpec((None, tk, n), lambda j, t: (j, t, 0)) if b.ndim == 3
                  else pl.BlockSpec((tk, n), lambda j, t: (t, 0)))
        out_spec, out_shape = pl.BlockSpec((None, m, n), lambda j, t: (j, 0, 0)), _sds((nj, m, n))
    else:
        bn = col_chunk
        if bn is None:
            bn = next((cand for cand in (1024, 512) if n > cand and n % cand == 0), n)
        nj = n // bn
        a_spec = pl.BlockSpec((tk, m), lambda j, t: (t, a_col))
        b_spec = pl.BlockSpec((tk, bn), lambda j, t: (t, j))
        if col_chunk is None:
            out_spec, out_shape = pl.BlockSpec((m, bn), lambda j, t: (0, j)), _sds((m, n))
        else:
            out_spec, out_shape = pl.BlockSpec((None, m, bn), lambda j, t: (j, 0, 0)), _sds((nj, m, bn))

    def body(a_ref, b_ref, o_ref):
        if tk == L:
            o_ref[...] = _mm_tn(a_ref[...], b_ref[...])
        else:
            @pl.when(pl.program_id(1) == 0)
            def _():
                o_ref[...] = jnp.zeros_like(o_ref)

            o_ref[...] += _mm_tn(a_ref[...], b_ref[...])

    outs = _pallas_call(
        body, carry, name=name, grid=(nj, L // tk), in_specs=[a_spec, b_spec], out_specs=[out_spec],
        out_shape=[pltpu.HBM(out_shape.shape, out_shape.dtype)],
        compiler_params=pltpu.CompilerParams(dimension_semantics=("arbitrary", "arbitrary"),
                                             vmem_limit_bytes=52 * VMEM_MB),
    )(a, b)
    return outs[0] if carry is None else outs


LANE = 128


def _tn_blocks(name, a, bs, ga, gb, carry=None):
    L, m, n, nb = a.shape[0], a.shape[1], bs[0].shape[1], len(bs)
    per = LANE // ga
    wb = per * gb
    n_super = m // LANE

    def body(a_ref, *refs):
        b_refs, o_refs, acc_refs = refs[:nb], refs[nb:2 * nb], refs[2 * nb:]
        t = pl.program_id(0)

        @pl.when(t == 0)
        def _():
            for acc in acc_refs:
                acc[...] = jnp.zeros_like(acc)

        lhs = a_ref[...].astype(BF)
        for b_ref, acc in zip(b_refs, acc_refs):
            rhs = b_ref[...].astype(BF)
            for j in range(n_super):
                acc[j] += _mm_tn(lhs[:, j * LANE:(j + 1) * LANE], rhs[:, j * wb:(j + 1) * wb])

        @pl.when(t == L // TK - 1)
        def _():
            own = (lax.broadcasted_iota(jnp.int32, (LANE, wb), 0) // ga) == (lax.broadcasted_iota(jnp.int32, (LANE, wb), 1) // gb)
            for o_ref, acc in zip(o_refs, acc_refs):
                for j in range(n_super):
                    kept = jnp.where(own, acc[j], 0.0)
                    o_ref[:, j * wb:(j + 1) * wb] = jnp.sum(kept.reshape(per, ga, wb), axis=0)

    outs = _pallas_call(
        body, carry, name=name, grid=(L // TK,),
        in_specs=[pl.BlockSpec((TK, m), lambda t: (t, 0))] + [pl.BlockSpec((TK, n), lambda t: (t, 0))] * nb,
        out_specs=[_full((ga, n))] * nb, out_shape=[_sds((ga, n))] * nb,
        scratch_shapes=[pltpu.VMEM((n_super, LANE, wb), F32)] * nb,
        compiler_params=_params(48),
    )(*_in_hbm([a] + list(bs)))
    return list(outs)


def _s5_discretize(lam_re, lam_im, log_dt, b_re, b_im):
    dt = jnp.exp(log_dt)[:, None]
    mag = jnp.exp(lam_re * dt)
    ar = mag * jnp.cos(lam_im * dt)
    ai = mag * jnp.sin(lam_im * dt)
    den = lam_re * lam_re + lam_im * lam_im
    nr = ar - 1.0
    fr = (nr * lam_re + ai * lam_im) / den
    fi = (ai * lam_re - nr * lam_im) / den
    bbr = fr[:, None, :] * b_re - fi[:, None, :] * b_im
    bbi = fr[:, None, :] * b_im + fi[:, None, :] * b_re
    return ar, ai, bbr, bbi


def _prepare(by_rows, block_cols, ar, ai):
    n = len(by_rows)

    def body(*refs):
        srcs, (ar_ref, ai_ref), dense, (con_ref, rev_ref) = refs[:n], refs[n:n + 2], refs[n + 2:2 * n + 2], refs[2 * n + 2:]
        for src, out, c in zip(srcs, dense, block_cols):
            r = src.shape[0]
            per = LANE // r
            wide = per * c
            own = (lax.broadcasted_iota(jnp.int32, (LANE, wide), 0) // r) == (lax.broadcasted_iota(jnp.int32, (LANE, wide), 1) // c)
            for j in range(out.shape[0]):
                tiled = jnp.broadcast_to(src[:, j * wide:(j + 1) * wide][None], (per, r, wide)).reshape(LANE, wide)
                out[j] = jnp.where(own, tiled, 0.0).astype(BF)
        a_r, a_i = ar_ref[...], ai_ref[...]
        pw = [(jnp.ones_like(a_r), jnp.zeros_like(a_i))]
        for _ in range(SUB):
            pr, pi = pw[-1]
            pw.append((pr * a_r - pi * a_i, pr * a_i + pi * a_r))
        row = _row_iota(GN)
        for ref, reverse in ((con_ref, False), (rev_ref, True)):
            sign = -1.0 if reverse else 1.0
            for j, sh in enumerate((1, 2, 4)):
                keep = (row < SUB - sh) if reverse else (row >= sh)
                ref[2 * j * SUB:(2 * j + 1) * SUB, :] = jnp.where(keep, pw[sh][0], 0.0)
                ref[(2 * j + 1) * SUB:(2 * j + 2) * SUB, :] = jnp.where(keep, sign * pw[sh][1], 0.0)
            p_r, p_i = jnp.zeros((SUB, GN), F32), jnp.zeros((SUB, GN), F32)
            for i in range(SUB):
                k = SUB - i if reverse else i + 1
                p_r = jnp.where(row == i, pw[k][0], p_r)
                p_i = jnp.where(row == i, sign * pw[k][1], p_i)
            ref[6 * SUB:7 * SUB, :] = p_r
            ref[7 * SUB:8 * SUB, :] = p_i

    dense_shapes = [(b.shape[1] // (LANE // b.shape[0] * c), LANE, LANE // b.shape[0] * c)
                    for b, c in zip(by_rows, block_cols)]
    outs = _pallas_call(
        body, name="prepare", grid=(1,), in_specs=[_full(b.shape) for b in by_rows] + [_full((1, GN))] * 2,
        out_specs=[_full(s) for s in dense_shapes] + [_full((8 * SUB, GN))] * 2,
        out_shape=[_sds(s, BF) for s in dense_shapes] + [_sds((8 * SUB, GN))] * 2,
        compiler_params=_params(48),
    )(*by_rows, ar, ai)
    return outs[:n], outs[n], outs[n + 1]


def _local_step(x, p, tgt, w, comm):
    rows_of = lambda a: a.reshape(NCHIP * a.shape[1], a.shape[2])
    quarters = lambda a: a.reshape(NCHIP, a.shape[0] // NCHIP, a.shape[1])

    def gathering(names, call):
        carry = comm.gather(names)
        outs = list(call(carry))
        own = len(outs) - len(carry.out_shapes)
        w.update(zip(names, outs[own:]))
        return outs[:own]

    w.update(comm.first())
    w_glu = rows_of(w["w_glu"])
    ar, ai, bbr, bbi = _s5_discretize(w["lam_re"], w["lam_im"], w["log_dt"], w["s5_b_re"], w["s5_b_im"])
    by_row = lambda b: jnp.transpose(b, (1, 0, 2)).reshape(b.shape[1], -1)
    (bbr_d, bbi_d, ccr_d, cci_d, wr_d, wi_d), con, con_rev = _prepare(
        [by_row(b) for b in (bbr, bbi, w["s5_c_re"], w["s5_c_im"], w["w_r"], w["w_i"])], [NS] * 4 + [HD] * 2,
        ar.reshape(1, GN), ai.reshape(1, GN))
    dsk = w["s5_d"].reshape(1, S5W)
    lam = w["lru_lambda"].reshape(1, LW)
    sp = jax.nn.softplus(-lam)
    b_r, b_i = w["b_r"].reshape(1, LW), w["b_i"].reshape(1, LW)
    row = lambda name: w[name].reshape(1, -1)

    h, ua, ub, gp = gathering(["w_a_out", "w_b_out"], lambda carry: _inproj_fwd(
        x, row("g_mix"), w["w_in"], row("b_in"), carry))
    sr, si, y, zg, ya = gathering(["w_o", "w_ffn_gate"], lambda carry: _s5_fwd(
        ua, bbr_d, bbi_d, ccr_d, cci_d, dsk, con, w_glu, row("b_glu"), carry))
    xc, rg, ig, yb, hp = gathering(["w_ffn_up"], lambda carry: _lru_fwd(
        ub, w["conv_w"], row("conv_b"), wr_d, wi_d, b_r, b_i, sp, carry))
    w_b_out, w_o = rows_of(w["w_b_out"]), rows_of(w["w_o"])
    x1, pa, pb, merged = gathering(["w_ffn_down"], lambda carry: _merge_fwd(
        x, ya, yb, gp, w["w_a_out"], w_b_out, w_o, carry))
    x2, h2, gg, uu = gathering(["w_ple_gate", "w_ple"], lambda carry: _ffn_fwd(
        x1, row("g_ffn"), w["w_ffn_gate"], w["w_ffn_up"], w["w_ffn_down"], carry))
    w_pg = rows_of(w["w_ple_gate"])
    dx2, n2, dpre, de0, acc_p = _ple_loss(x2, p, tgt, row("g_ple_gate"), w_pg, row("b_ple_gate"),
                                          w["w_ple"], row("g_ple"), row("g_final"))
    comm.reduce("ple", {"w_ple_gate": quarters(_tn("dw_ple_gate", n2, dpre)),
                        "w_ple": _tn("dw_ple", p, de0, col_chunk=AC)})
    dx1, act, dgg, duu, acc_f = comm.run(lambda carry: _ffn_bwd(
        x1, dx2, gg, uu, row("g_ffn"), w["w_ffn_gate"], w["w_ffn_up"], w["w_ffn_down"], carry))
    comm.reduce("ffn_gate", {"w_ffn_gate": _tn("dw_ffn_gate", dgg, h2)})
    comm.reduce("ffn_up", {"w_ffn_up": comm.run(lambda carry: _tn("dw_ffn_up", duu, h2, carry=carry))[0]})
    comm.reduce("ffn_down", {"w_ffn_down": comm.run(lambda carry: _tn("dw_ffn_down", act, dx2, carry=carry),
                                                    hold=("ffn_gate",))[0]})
    dya, dyb, dgp, dpa, dpb = comm.run(lambda carry: _merge_bwd(
        dx1, gp, pa, pb, w["w_a_out"], w_b_out, w_o, carry), hold=("ffn_gate", "ffn_up"))
    comm.reduce("merge", {"w_o": quarters(_tn("dw_o", merged, dx1)), "w_a_out": _tn("dw_a_out", ya, dpa, col_chunk=AC),
                          "w_b_out": quarters(_tn("dw_b_out", yb, dpb))})
    dua, dq, dy, lr, li, acc_a, acc_s = comm.run(lambda carry: _s5_bwd(
        dya, y, ua, sr, si, bbr_d, bbi_d, ccr_d, cci_d, dsk, con_rev, w_glu, row("b_glu"), carry), hold=("ffn_down",))
    dub, dpr, dpi, acc_l = comm.run(lambda carry: _lru_bwd(
        dyb, xc, rg, ig, hp, ub, w["conv_w"], wr_d, wi_d, sp, -_sig(-lam), carry))
    gx, dz, acc_g, acc_b = _inproj_bwd(x, dx1, dua, dub, dgp, row("g_mix"), w["w_in"])
    half = (D // 2,)
    comm.reduce("in_lo", {"w_in_lo": comm.run(lambda carry: _tn(
        "dw_in_lo", h, dz, col_chunk=QC, a_block=(0,) + half, carry=carry))[0]})
    comm.reduce("in_hi", {"w_in_hi": comm.run(lambda carry: _tn(
        "dw_in_hi", h, dz, col_chunk=QC, a_block=(1,) + half, carry=carry))[0], "w_glu": quarters(_tn("dw_glu", zg, dq))})
    d_wr, d_wi = comm.run(lambda carry: _tn_blocks("dw_r_i", xc, [dpr, dpi], HD, HD, carry))
    d_bbr, d_bbi = comm.run(lambda carry: _tn_blocks("d_bb", ua, [lr, li], NP, NS, carry))
    d_ccr, d_cci = comm.run(lambda carry: _tn_blocks("d_cc", dy, [sr, si], NP, NS, carry))
    comm.drain()
    sums = {"ple": acc_p, "ffn": acc_f, "mix": acc_g, "b_in": acc_b, "lru": acc_l, "s5": acc_s, "s5_a": acc_a}
    blocks = {"bb_re": d_bbr, "bb_im": d_bbi,
              "cc_re": d_ccr, "cc_im": d_cci,
              "w_r": d_wr, "w_i": d_wi}
    return gx, sums, blocks


def _replicated_grads(w, sums, blocks):
    grouped = lambda e, groups: jnp.transpose(e.reshape(e.shape[0], groups, -1), (1, 0, 2))
    d_ar, d_ai = sums["s5_a"][0].reshape(NG, NS), sums["s5_a"][1].reshape(NG, NS)
    d_bbr, d_bbi = grouped(blocks["bb_re"], NG), grouped(blocks["bb_im"], NG)
    _, vjp = jax.vjp(_s5_discretize, w["lam_re"], w["lam_im"], w["log_dt"], w["s5_b_re"], w["s5_b_im"])
    g = dict(zip(("lam_re", "lam_im", "log_dt", "s5_b_re", "s5_b_im"), vjp((d_ar, d_ai, d_bbr, d_bbi))))
    g["s5_c_re"] = grouped(blocks["cc_re"], NG)
    g["s5_c_im"] = -grouped(blocks["cc_im"], NG)
    g["w_r"], g["w_i"] = grouped(blocks["w_r"], NH), grouped(blocks["w_i"], NH)
    g["s5_d"] = sums["s5"][0].reshape(NG, NP)
    g["b_r"] = sums["lru"][1].reshape(NH, HD)
    g["b_i"] = sums["lru"][2].reshape(NH, HD)
    return g


ACC_ROWS = {"g_mix": ("mix", 0), "b_in": ("b_in", 0), "g_ffn": ("ffn", 0), "g_ple_gate": ("ple", 0),
            "b_ple_gate": ("ple", 1), "g_ple": ("ple", 2), "g_final": ("ple", 3), "b_glu": ("s5", 1),
            "lru_lambda": ("lru", 0), "conv_b": ("lru", 3)}
LOSS_ROW = ("ple", 4)
CONV_W_ROWS = ("lru", 4)


SHARDED = [("w_in", (D, QC)), ("w_glu", (S5W // NCHIP, S5W)), ("w_a_out", (S5W, AC)), ("w_b_out", (LW // NCHIP, D)),
           ("w_o", (D // NCHIP, D)), ("w_ffn_gate", (FC, D)), ("w_ffn_up", (FC, D)), ("w_ffn_down", (FC, D)),
           ("w_ple_gate", (D // NCHIP, D)), ("w_ple", (PLE, AC))]
NSH = len(SHARDED)
TRANSPOSED = ("w_ffn_gate", "w_ffn_up", "s5_b_re", "s5_b_im")
CONV_SHARD = (4, LW // NCHIP)


def _mesh_pos():
    return lax.axis_index("x"), lax.axis_index("y"), lax.axis_index("c")


def _other_chips(x, y):
    return [(1 - x, y), (x, 1 - y), (1 - x, 1 - y)]


def _half_rows(c, rows, align):
    return pl.ds(pl.multiple_of(c * (rows // 2), align), rows // 2)


def _run_now(name, carry):
    c_in, c_out = len(carry.operands), len(carry.out_shapes)

    def body(*refs):
        ins, outs, sems = refs[:c_in], refs[c_in:c_in + c_out], refs[c_in + c_out:]
        carry.start(ins, outs, sems)
        carry.finish(ins, outs, sems)

    return pl.pallas_call(body, name=name, in_specs=[ANY] * c_in, out_specs=[ANY] * c_out,
                          out_shape=list(carry.out_shapes), scratch_shapes=list(carry.sems),
                          input_output_aliases=dict(carry.aliases))(*_in_hbm(carry.operands))


def _gather_group(shards, split):
    n = len(shards)

    def copies(srcs, outs, sems):
        send_sems, recv_sems = sems
        x, y, c = _mesh_pos()
        k0 = 2 * x + y
        sib = (x, y, 1 - c)
        chips = _other_chips(x, y)

        def remote(src, dst, j, i, to):
            return pltpu.make_async_remote_copy(src_ref=src, dst_ref=dst, send_sem=send_sems.at[j, i],
                                                recv_sem=recv_sems.at[j, i], device_id=to, device_id_type=MESH)

        def rows(ref, i, core, *lead):
            if not split[i]:
                return ref.at[lead] if lead else ref
            return ref.at[(*lead, _half_rows(core, shards[i].shape[0], 16))]

        own = [remote(s, o.at[k0], 6, i, sib) for i, (s, o) in enumerate(zip(srcs, outs))]
        ici, landed, fwd, fwd_landed = [], [], [], []
        for j, chip in enumerate(chips):
            kj = 2 * chip[0] + chip[1]
            pairs = list(enumerate(zip(srcs, outs)))
            ici.append([remote(rows(s, i, c), rows(o, i, c, k0), j, i, (*chip, c)) for i, (s, o) in pairs])
            landed.append([remote(rows(s, i, c), rows(o, i, c, kj), j, i, (*chip, c)) for i, (s, o) in pairs])
            fwd.append([remote(rows(o, i, c, kj), rows(o, i, c, kj), 3 + j, i, sib) for i, (s, o) in pairs if split[i]])
            fwd_landed.append([remote(rows(o, i, 1 - c, kj), rows(o, i, 1 - c, kj), 3 + j, i, sib)
                               for i, (s, o) in pairs if split[i]])
        return own, ici, landed, fwd, fwd_landed

    def start(srcs, outs, sems):
        own, ici, _, _, _ = copies(srcs, outs, sems)
        for cp in own + [cp for per_chip in ici for cp in per_chip]:
            cp.start()

    def finish(srcs, outs, sems):
        own, ici, landed, fwd, fwd_landed = copies(srcs, outs, sems)
        passed = [i for i in range(n) if split[i]]
        for j in range(3):
            for i, cp in enumerate(landed[j]):
                cp.wait_recv()
                if split[i]:
                    fwd[j][passed.index(i)].start()
        for j in range(3):
            for cp in fwd_landed[j]:
                cp.wait_recv()
        for cp in own:
            cp.wait_recv()
        for cp in own + [cp for per_chip in ici + fwd for cp in per_chip]:
            cp.wait_send()

    return _Carried(shards, [_sds((NCHIP,) + s.shape, s.dtype) for s in shards],
                    [pltpu.SemaphoreType.DMA((7, n)), pltpu.SemaphoreType.DMA((7, n))], start, finish)


def _each_copy(copies, carried, out_shapes, sems, aliases=None):
    def start(ins, outs, sem_refs):
        for cp in copies(ins, outs, sem_refs):
            cp.start()

    def finish(ins, outs, sem_refs):
        for cp in copies(ins, outs, sem_refs):
            cp.wait()

    return _Carried(carried, out_shapes, sems, start, finish, aliases)


def _swap_group(grads):
    n = len(grads)

    def copies(srcs, outs, sems):
        send_sems, recv_sems = sems
        x, y, c = _mesh_pos()
        return [pltpu.make_async_remote_copy(src_ref=s.at[:, _half_rows(1 - c, s.shape[1], 8)], dst_ref=o,
                                             send_sem=send_sems.at[i], recv_sem=recv_sems.at[i], device_id=(x, y, 1 - c),
                                             device_id_type=MESH) for i, (s, o) in enumerate(zip(srcs, outs))]

    return _each_copy(copies, grads, [pltpu.HBM((NCHIP, g.shape[1] // 2, g.shape[2]), F32) for g in grads],
                      [pltpu.SemaphoreType.DMA((n,)), pltpu.SemaphoreType.DMA((n,))])


def _add_sibling_group(tag, kc_idx, grads, gots):
    n = len(grads)

    def body(kc_ref, *refs):
        for g, rx, p, pb in zip(refs[:n], refs[n:2 * n], refs[2 * n:3 * n], refs[3 * n:]):
            s = g[...] + rx[...]
            pb[...] = s.astype(BF)

            @pl.when(pl.program_id(0) == kc_ref[0])
            def _():
                p[...] = s

    halves = [pl.BlockSpec((None,) + rx.shape[1:], lambda k, kc_ref: (k, 0, 0)) for rx in gots]
    mine = [pl.BlockSpec((None,) + rx.shape[1:], lambda k, kc_ref: (k, kc_ref[1], 0)) for rx in gots]
    own = [pl.BlockSpec(rx.shape[1:], lambda k, kc_ref: (0, 0)) for rx in gots]
    outs = _pallas_call(
        body, name="add_sibling_" + tag,
        grid_spec=pltpu.PrefetchScalarGridSpec(num_scalar_prefetch=1, grid=(NCHIP,), in_specs=mine + halves,
                                               out_specs=own + halves),
        out_shape=[pltpu.HBM(rx.shape[1:], F32) for rx in gots] + [pltpu.HBM(rx.shape, BF) for rx in gots],
        compiler_params=_params(48),
    )(kc_idx, *_in_hbm(list(grads) + list(gots)))
    return outs[:n], outs[n:]


def _exchange_group(parts):
    n = len(parts)

    def copies(srcs, outs, sems):
        send_sems, recv_sems = sems
        x, y, c = _mesh_pos()
        return [pltpu.make_async_remote_copy(
            src_ref=s.at[2 * chip[0] + chip[1]], dst_ref=o.at[j], send_sem=send_sems.at[j, i],
            recv_sem=recv_sems.at[j, i], device_id=(*chip, c), device_id_type=MESH)
            for j, chip in enumerate(_other_chips(x, y)) for i, (s, o) in enumerate(zip(srcs, outs))]

    return _each_copy(copies, parts, [pltpu.HBM((3,) + p.shape[1:], BF) for p in parts],
                      [pltpu.SemaphoreType.DMA((3, n)), pltpu.SemaphoreType.DMA((3, n))])


def _add_chips_group(tag, kc_idx, parts, arrived):
    n = len(parts)

    def body(kc_ref, *refs):
        for p, rx, t in zip(refs[:n], refs[n:2 * n], refs[2 * n:]):
            t[...] = ((p[...] + rx[0].astype(F32)) + rx[1].astype(F32)) + rx[2].astype(F32)

    outs = _pallas_call(
        body, name="add_chips_" + tag,
        grid_spec=pltpu.PrefetchScalarGridSpec(
            num_scalar_prefetch=1, grid=(1,),
            in_specs=([pl.BlockSpec(rx.shape[1:], lambda i, kc_ref: (0, 0)) for rx in arrived]
                      + [pl.BlockSpec(rx.shape, lambda i, kc_ref: (0, 0, 0)) for rx in arrived]),
            out_specs=[pl.BlockSpec((None,) + rx.shape[1:], lambda i, kc_ref: (kc_ref[1], 0, 0)) for rx in arrived]),
        out_shape=[pltpu.HBM((2,) + rx.shape[1:], F32) for rx in arrived],
        compiler_params=_params(48),
    )(kc_idx, *_in_hbm(list(parts) + list(arrived)))
    return list(outs)


def _join_group(halves):
    n = len(halves)

    def copies(bufs, sems):
        send_sems, recv_sems = sems
        x, y, c = _mesh_pos()
        sib = (x, y, 1 - c)
        sends = [pltpu.make_async_remote_copy(src_ref=b.at[c], dst_ref=b.at[c], send_sem=send_sems.at[i],
                                              recv_sem=recv_sems.at[i], device_id=sib, device_id_type=MESH)
                 for i, b in enumerate(bufs)]
        landed = [pltpu.make_async_remote_copy(src_ref=b.at[c], dst_ref=b.at[1 - c], send_sem=send_sems.at[i],
                                               recv_sem=recv_sems.at[i], device_id=sib, device_id_type=MESH)
                  for i, b in enumerate(bufs)]
        return sends, landed

    def start(_, bufs, sems):
        for cp in copies(bufs, sems)[0]:
            cp.start()

    def finish(_, bufs, sems):
        sends, landed = copies(bufs, sems)
        for cp in landed:
            cp.wait_recv()
        for cp in sends:
            cp.wait_send()

    return _Carried(halves, [pltpu.HBM(h.shape, F32) for h in halves],
                    [pltpu.SemaphoreType.DMA((n,)), pltpu.SemaphoreType.DMA((n,))], start, finish,
                    {i: i for i in range(n)})


def _combine(carries):
    operands, out_shapes, sems, aliases, spans = [], [], [], {}, []
    for c in carries:
        aliases.update({len(operands) + i: len(out_shapes) + o for i, o in c.aliases.items()})
        spans.append((len(operands), len(out_shapes), len(sems)))
        operands += list(c.operands)
        out_shapes += list(c.out_shapes)
        sems += list(c.sems)

    def each(phase):
        def run(ins, outs, sem_refs):
            for c, (a, b, s) in zip(carries, spans):
                getattr(c, phase)(ins[a:a + len(c.operands)], outs[b:b + len(c.out_shapes)], sem_refs[s:s + len(c.sems)])
        return run

    return _Carried(operands, out_shapes, sems, each("start"), each("finish"), aliases)


def _allreduce_small(arrays, wire):
    n = len(arrays)
    halves = [(a.shape[0], a.shape[1] // 2) for a in arrays]

    def body(*refs):
        srcs, outs = refs[:n], refs[n:2 * n]
        mine_bufs, sib_bufs, chip_bufs, total_bufs = (refs[k * n:(k + 1) * n] for k in range(2, 6))
        send_sems, recv_sems, local_sems = refs[6 * n:]
        x, y, c = _mesh_pos()
        k0 = 2 * x + y
        sib = (x, y, 1 - c)

        def remote(src, dst, j, i, to):
            return pltpu.make_async_remote_copy(src_ref=src, dst_ref=dst, send_sem=send_sems.at[j, i],
                                                recv_sem=recv_sems.at[j, i], device_id=to, device_id_type=MESH)

        def cols(ref, i, core):
            return ref.at[:, pl.ds(pl.multiple_of(core * halves[i][1], LANE), halves[i][1])]

        swaps = [remote(cols(s, i, 1 - c), b, 0, i, sib) for i, (s, b) in enumerate(zip(srcs, sib_bufs))]
        own = [pltpu.make_async_copy(cols(s, i, c), m, local_sems.at[i]) for i, (s, m) in enumerate(zip(srcs, mine_bufs))]
        for cp in swaps + own:
            cp.start()
        for cp in swaps + own:
            cp.wait()
        for m, b, buf in zip(mine_bufs, sib_bufs, chip_bufs):
            buf[k0] = (m[...] + b[...]).astype(buf.dtype)
        chips = _other_chips(x, y)
        sends = [remote(buf.at[k0], buf.at[k0], 1 + j, i, (*chip, c))
                 for j, chip in enumerate(chips) for i, buf in enumerate(chip_bufs)]
        for cp in sends:
            cp.start()
        for j, chip in enumerate(chips):
            for i, buf in enumerate(chip_bufs):
                remote(buf.at[k0], buf.at[2 * chip[0] + chip[1]], 1 + j, i, (*chip, c)).wait_recv()
        for cp in sends:
            cp.wait_send()
        for t, buf in zip(total_bufs, chip_bufs):
            t[...] = ((buf[0].astype(F32) + buf[1].astype(F32)) + buf[2].astype(F32)) + buf[3].astype(F32)
        joins = [remote(t, cols(o, i, c), 4, i, sib) for i, (t, o) in enumerate(zip(total_bufs, outs))]
        keep = [pltpu.make_async_copy(t, cols(o, i, c), local_sems.at[i]) for i, (t, o) in enumerate(zip(total_bufs, outs))]
        for cp in joins + keep:
            cp.start()
        for i, (t, o) in enumerate(zip(total_bufs, outs)):
            remote(t, cols(o, i, 1 - c), 4, i, sib).wait_recv()
        for cp in joins:
            cp.wait_send()
        for cp in keep:
            cp.wait()

    specs = [_full(a.shape) for a in arrays]
    return _pallas_call(
        body, name="allreduce_small", grid=(1,), in_specs=specs, out_specs=specs,
        out_shape=[_sds(a.shape) for a in arrays],
        scratch_shapes=([pltpu.VMEM(h, F32) for h in halves] + [pltpu.VMEM(h, F32) for h in halves]
                        + [pltpu.VMEM((NCHIP,) + h, dt) for h, dt in zip(halves, wire)] + [pltpu.VMEM(h, F32) for h in halves]
                        + [pltpu.SemaphoreType.DMA((5, n)), pltpu.SemaphoreType.DMA((5, n)), pltpu.SemaphoreType.DMA((n,))]),
        compiler_params=_params(32),
    )(*arrays)


def _adamw_terms(w, g, m, v):
    m = ADAM_B1 * m + (1.0 - ADAM_B1) * g
    v = ADAM_B2 * v + (1.0 - ADAM_B2) * jnp.square(g)
    m_hat = m / (1.0 - ADAM_B1 ** ADAM_STEP)
    v_hat = v / (1.0 - ADAM_B2 ** ADAM_STEP)
    return -ADAM_LR * (m_hat / (jnp.sqrt(v_hat) + ADAM_EPS) + ADAM_WD * w), m, v


ADAM_STEPS = 4


def _adamw_group(tag, ws, gs, ms, vs):
    n = len(ws)

    def body(*refs):
        ins, outs = refs[:4 * n], refs[4 * n:]
        for i in range(n):
            w, g, m, v = (ins[k * n + i][...] for k in range(4))
            outs[i][...] = g
            outs[n + i][...], outs[2 * n + i][...], outs[3 * n + i][...] = _adamw_terms(w, g, m, v)

    specs = [pl.BlockSpec((w.shape[0] // ADAM_STEPS, w.shape[1]), lambda i: (i, 0)) for w in ws]
    outs = _pallas_call(
        body, name="adamw_" + tag, grid=(ADAM_STEPS,), in_specs=specs * 4, out_specs=specs * 4,
        out_shape=[_sds(w.shape) for w in ws] * 4, compiler_params=_params(48),
    )(*_in_hbm(list(ws) + list(gs) + list(ms) + list(vs)))
    return outs[:n], outs[n:2 * n], outs[2 * n:3 * n], outs[3 * n:]


def _adamw_replicated(sums, row_of, direct):
    ns, nr, nd = len(sums), len(row_of), len(direct)

    def body(*refs):
        sum_refs = refs[:ns]
        ins = refs[ns:ns + 3 * nr + 4 * nd]
        outs = refs[ns + 3 * nr + 4 * nd:]
        for i, (_, _, _, si, row) in enumerate(row_of):
            w_ref, m_ref, v_ref = ins[3 * i:3 * i + 3]
            g = sum_refs[si][row:row + 1, :]
            outs[4 * i][...] = g
            outs[4 * i + 1][...], outs[4 * i + 2][...], outs[4 * i + 3][...] = _adamw_terms(w_ref[...], g, m_ref[...], v_ref[...])
        for i in range(nd):
            w_ref, m_ref, v_ref, g_ref = ins[3 * nr + 4 * i:3 * nr + 4 * i + 4]
            o = outs[4 * (nr + i):4 * (nr + i) + 4]
            g = g_ref[...]
            o[0][...] = g
            o[1][...], o[2][...], o[3][...] = _adamw_terms(w_ref[...], g, m_ref[...], v_ref[...])

    operands = list(sums)
    shapes = []
    for w, m, v, _, _ in row_of:
        operands += [w, m, v]
        shapes += [w.shape] * 4
    for w, m, v, g in direct:
        operands += [w, m, v, g]
        shapes += [w.shape] * 4
    flat = _pallas_call(
        body, name="adamw_replicated", grid=(1,), in_specs=[_full(a.shape) for a in operands],
        out_specs=[_full(s) for s in shapes], out_shape=[_sds(s) for s in shapes],
        compiler_params=_params(56),
    )(*operands)
    return [flat[4 * i:4 * i + 4] for i in range(nr + nd)]


class _Exchanges:
    def __init__(self, shards, conv_w, chip, core, apply):
        self.shards, self.conv_w, self.apply = shards, conv_w, apply
        self.active, self.calls = [], 0
        self.core_idx = jnp.reshape(core, (1,)).astype(jnp.int32)
        self.chip_core_idx = jnp.stack([chip, core]).astype(jnp.int32)

    def first(self):
        names = ["w_in", "w_glu"]
        got = _run_now("gather_first", _gather_group([self.shards[n] for n in names] + [self.conv_w],
                                                     [True, True, False]))
        out = dict(zip(names, got))
        out["conv_w"] = jnp.transpose(got[2], (1, 0, 2)).reshape(4, LW)
        return out

    def gather(self, names):
        return _gather_group([self.shards[n] for n in names], [True] * len(names))

    def reduce(self, tag, grads):
        self.active.append({"tag": tag, "names": list(grads), "stage": 0, "grads": list(grads.values())})

    def run(self, call, hold=()):
        groups = [g for g in self.active if g["tag"] not in hold]
        carries = [self._exchange_of(g) for g in groups]
        carry = _combine(carries)
        outs = list(call(carry))
        own = len(outs) - len(carry.out_shapes)
        landed = outs[own:]
        for g, c in zip(groups, carries):
            self._sum_after(g, landed[:len(c.out_shapes)])
            landed = landed[len(c.out_shapes):]
        self.active = [g for g in self.active if g["stage"] < 3]
        return outs[:own]

    def _exchange_of(self, g):
        if g["stage"] == 0:
            return _swap_group(g["grads"])
        if g["stage"] == 1:
            return _exchange_group(g["bf16"])
        return _join_group(g["halves"])

    def _sum_after(self, g, landed):
        if g["stage"] == 0:
            g["f32"], g["bf16"] = _add_sibling_group(g["tag"], self.chip_core_idx, g["grads"], landed)
        elif g["stage"] == 1:
            g["halves"] = _add_chips_group(g["tag"], self.chip_core_idx, g["f32"], landed)
        else:
            self.apply(g["tag"], g["names"], [t.reshape(2 * t.shape[1], t.shape[2]) for t in landed])
        g["stage"] += 1

    def drain(self):
        while self.active:
            self.calls += 1
            self.run(lambda carry: _run_now("reduce_%d" % self.calls, carry))


INPUT_NAMES = (["x", "p"] + [n for n in
               ["g_mix", "w_in", "b_in", "lam_re", "lam_im", "log_dt", "s5_b_re", "s5_b_im", "s5_c_re", "s5_c_im", "s5_d",
                "w_glu", "b_glu", "conv_w", "conv_b", "w_r", "b_r", "w_i", "b_i", "lru_lambda", "w_a_out", "w_b_out", "w_o",
                "g_ffn", "w_ffn_gate", "w_ffn_up", "w_ffn_down", "g_ple_gate", "w_ple_gate", "b_ple_gate", "w_ple", "g_ple",
                "g_final"]])
WEIGHT_NAMES = INPUT_NAMES[2:]


def kernel(*args):
    names = INPUT_NAMES + ["loss_target"] + ["m_" + n for n in WEIGHT_NAMES] + ["v_" + n for n in WEIGHT_NAMES]
    assert len(args) == len(names)
    given = dict(zip(names, args))

    def view(name):
        a = given[name]
        return jnp.swapaxes(a, -1, -2) if name.endswith(TRANSPOSED) else a

    def unview(name, a):
        return jnp.swapaxes(a, -1, -2) if name in TRANSPOSED else a

    def local(name):
        return view(name) if name.endswith("g_final") else view(name)[0]

    xi, yi, ci = _mesh_pos()
    k0 = 2 * xi + yi
    x, p, tgt = given["x"][0], given["p"][0, 0], given["loss_target"][0]

    results = {}

    row_halves = {}

    def apply(tag, names, totals):
        totals = dict(zip(names, totals))
        row_halves.update({n: totals.pop(n) for n in names if n in ("w_in_lo", "w_in_hi")})
        if len(row_halves) == 2:
            totals["w_in"] = jnp.concatenate([row_halves.pop("w_in_lo"), row_halves.pop("w_in_hi")])
        names = list(totals)
        if not names:
            return
        new = _adamw_group(tag, [local(n) for n in names], list(totals.values()), [local("m_" + n) for n in names],
                           [local("v_" + n) for n in names])
        for kind, arrays in zip(("grad", "delta", "new_m", "new_v"), new):
            for n, arr in zip(names, arrays):
                results[kind, n] = unview(n, arr[None])

    comm = _Exchanges({n: local(n).astype(BF) for n, _ in SHARDED}, local("conv_w"), k0, ci, apply)
    w = {n: local(n) for n in WEIGHT_NAMES if n != "conv_w" and n not in dict(SHARDED)}
    gx, sums, blocks = _local_step(x, p, tgt, w, comm)

    sum_names, block_names = list(sums), list(blocks)
    red = _allreduce_small([sums[n] for n in sum_names] + [blocks[n] for n in block_names],
                           [F32] * len(sum_names) + [BF] * len(block_names))
    sums = dict(zip(sum_names, red[:len(sum_names)]))
    blocks = dict(zip(block_names, red[len(sum_names):]))
    loss = jnp.sum(sums[LOSS_ROW[0]][LOSS_ROW[1]])
    direct_g = _replicated_grads(w, sums, blocks)
    conv_rows = sums[CONV_W_ROWS[0]][CONV_W_ROWS[1]:CONV_W_ROWS[1] + 4]
    direct_g["conv_w"] = lax.dynamic_slice(conv_rows, (0, k0 * CONV_SHARD[1]), CONV_SHARD)
    as_row = lambda a: a.reshape(1, -1)
    row_names = list(ACC_ROWS)
    row_of = [(as_row(given[n]), as_row(given["m_" + n]), as_row(given["v_" + n]),
               sum_names.index(ACC_ROWS[n][0]), ACC_ROWS[n][1]) for n in row_names]
    direct_names = list(direct_g)
    direct = [(view(n), view("m_" + n), view("v_" + n), direct_g[n].reshape(view(n).shape)) for n in direct_names]
    done = _adamw_replicated([sums[n] for n in sum_names], row_of, direct)
    for n, four in zip(row_names + direct_names, done):
        for kind, arr in zip(("grad", "delta", "new_m", "new_v"), four):
            results[kind, n] = unview(n, arr).reshape(given[n].shape)

    out = [loss, gx[None]]
    for kind in ("grad", "delta", "new_m", "new_v"):
        out += [results[kind, n] for n in WEIGHT_NAMES]
    return tuple(out)
```

```python
import functools
import math

import jax
import jax.numpy as jnp
from jax import lax
from jax.experimental import pallas as pl
from jax.experimental.pallas import tpu as pltpu

F32 = jnp.float32
BF = jnp.bfloat16

D = 1024
S5W = 512
NG, NS, NP = 32, 64, 16
GN = NG * NS
LW = 1024
NH, HD = 16, 64
LRU_C = 8.0
FH = 2816
NCHIP = 4
FC = FH // NCHIP
PLE = 256
INC = S5W + LW + 2 * D
EPS = 1e-6
ADAM_LR, ADAM_B1, ADAM_B2, ADAM_EPS, ADAM_WD, ADAM_STEP = 0.001, 0.9, 0.999, 1e-08, 0.01, 10

TM = 256
TK = 1024
LC = 512
SUB = 8
VMEM_MB = 1024 * 1024
MESH = pl.DeviceIdType.MESH
ANY = pl.BlockSpec(memory_space=pl.ANY)


def _mm(a, b):
    return jnp.dot(a.astype(BF), b.astype(BF), preferred_element_type=F32)


def _mm_nt(a, b):
    return lax.dot_general(a.astype(BF), b.astype(BF), (((1,), (1,)), ((), ())), preferred_element_type=F32)


def _mm_tn(a, b):
    return lax.dot_general(a.astype(BF), b.astype(BF), (((0,), (0,)), ((), ())), preferred_element_type=F32)


def _blockdiag_mm(x, blocks_ref):
    n, rows, _ = blocks_ref.shape
    return jnp.concatenate([jnp.dot(x[:, j * rows:(j + 1) * rows], blocks_ref[j], preferred_element_type=F32)
                            for j in range(n)], axis=1)


def _blockdiag_mm_t(x, blocks_ref):
    n, _, wide = blocks_ref.shape
    return jnp.concatenate([lax.dot_general(x[:, j * wide:(j + 1) * wide], blocks_ref[j], (((1,), (1,)), ((), ())),
                                            preferred_element_type=F32) for j in range(n)], axis=1)


def _rms(x):
    r = lax.rsqrt(jnp.mean(x * x, axis=-1, keepdims=True) + EPS)
    return x * r, r


def _rms_bwd(dy, xh, r, g):
    dxh = dy * g
    return r * (dxh - xh * jnp.mean(dxh * xh, axis=-1, keepdims=True))


def _colsum(x):
    return jnp.sum(x, axis=0, keepdims=True)


def _sig(x):
    return jax.nn.sigmoid(x)


def _gelu_grad(x):
    c = math.sqrt(2.0 / math.pi)
    t = jnp.tanh(c * (x + 0.044715 * x * x * x))
    return 0.5 * (1.0 + t) + 0.5 * x * (1.0 - t * t) * c * (1.0 + 3.0 * 0.044715 * x * x)


def _neg_expm1(x):
    series = -x * (1.0 + x * (0.5 + x * (1.0 / 6.0 + x * (1.0 / 24.0))))
    return jnp.where(x > -0.03, series, 1.0 - jnp.exp(x))


def _tok(width):
    return pl.BlockSpec((TM, width), lambda i: (i, 0))


def _tok_rev(width, nt):
    return pl.BlockSpec((TM, width), lambda i: (nt - 1 - i, 0))


def _full(shape):
    return pl.BlockSpec(shape, lambda i: (0,) * len(shape))


def _params(vmem_mb, **kw):
    return pltpu.CompilerParams(dimension_semantics=("arbitrary",), vmem_limit_bytes=vmem_mb * VMEM_MB, **kw)


def _sds(shape, dtype=F32):
    return jax.ShapeDtypeStruct(shape, dtype)


class _Carried:
    def __init__(self, operands, out_shapes, sems, start, finish, aliases=None):
        self.operands, self.out_shapes, self.sems = list(operands), list(out_shapes), list(sems)
        self.start, self.finish, self.aliases = start, finish, dict(aliases or {})


def _in_hbm(arrays):
    return [pltpu.with_memory_space_constraint(a, pltpu.HBM) for a in arrays]


def _pallas_call(body, carry=None, **kw):
    if carry is None:
        return pl.pallas_call(body, **kw)

    def at_step(corner):
        hit = [pl.program_id(d) == (size - 1 if corner else 0) for d, size in enumerate(kw["grid"])]
        return functools.reduce(jnp.logical_and, hit)

    name, grid, compiler_params = kw["name"], kw["grid"], kw["compiler_params"]
    in_specs, out_specs, out_shape = list(kw["in_specs"]), list(kw["out_specs"]), list(kw["out_shape"])
    scratch_shapes = list(kw.get("scratch_shapes", ()))
    n_in, n_out, n_scr = len(in_specs), len(out_specs), len(scratch_shapes)
    c_in, c_out = len(carry.operands), len(carry.out_shapes)

    def full_body(*refs):
        ins, refs = refs[:n_in], refs[n_in:]
        c_ins, refs = refs[:c_in], refs[c_in:]
        outs, refs = refs[:n_out], refs[n_out:]
        c_outs, refs = refs[:c_out], refs[c_out:]
        scratch, c_sems = refs[:n_scr], refs[n_scr:]

        @pl.when(at_step(0))
        def _():
            carry.start(c_ins, c_outs, c_sems)

        body(*ins, *outs, *scratch)

        @pl.when(at_step(1))
        def _():
            carry.finish(c_ins, c_outs, c_sems)

    call = pl.pallas_call(
        full_body, name=name, grid=grid, in_specs=in_specs + [ANY] * c_in, out_specs=out_specs + [ANY] * c_out,
        out_shape=out_shape + list(carry.out_shapes), scratch_shapes=scratch_shapes + list(carry.sems),
        input_output_aliases={n_in + i: n_out + o for i, o in carry.aliases.items()},
        compiler_params=compiler_params)
    return lambda *operands: call(*operands, *_in_hbm(carry.operands))


def _resident(pairs, sems):
    first = pl.program_id(0) == 0
    copies = [pltpu.make_async_copy(src, dst, sems.at[j]) for j, (src, dst) in enumerate(pairs)]

    @pl.when(first)
    def _():
        for cp in copies:
            cp.start()

    def wait(j):
        @pl.when(first)
        def _():
            copies[j].wait()

    return wait


def _resident_now(pairs, sems):
    @pl.when(pl.program_id(0) == 0)
    def _():
        copies = [pltpu.make_async_copy(src, dst, sems.at[j]) for j, (src, dst) in enumerate(pairs)]
        for cp in copies:
            cp.start()
        for cp in copies:
            cp.wait()


def _row_iota(width):
    return lax.broadcasted_iota(jnp.int32, (SUB, width), 0)


def _bcast_row(x, row):
    return jnp.broadcast_to(x[row:row + 1, :], x.shape)


def _slab(k):
    return pl.ds(pl.multiple_of(k * SUB, SUB), SUB)


QC = INC // NCHIP
Z_PARTS = ((0, S5W), (S5W, S5W + LW), (S5W + LW, INC))


def _inproj_fwd(x, g_mix, w_in, b_in, carry=None):
    L = x.shape[0]

    def body(x_ref, g_ref, w_hbm, b_ref, h_ref, ua_ref, ub_ref, gp_ref, w_vm, w_sems):
        _resident_now([(w_hbm.at[k], w_vm.at[k]) for k in range(NCHIP)], w_sems)
        xh, _ = _rms(x_ref[...])
        h = (xh * g_ref[...]).astype(BF)
        h_ref[...] = h
        for k in range(NCHIP):
            lo, hi = k * QC, (k + 1) * QC
            z = jnp.dot(h, w_vm[k], preferred_element_type=F32) + b_ref[:, lo:hi]
            for ref, (a, b) in zip((ua_ref, ub_ref, gp_ref), Z_PARTS):
                s, e = max(lo, a), min(hi, b)
                if s < e:
                    ref[:, s - a:e - a] = z[:, s - lo:e - lo]

    return _pallas_call(
        body, carry, name="inproj_fwd", grid=(L // TM,),
        in_specs=[_tok(D), _full((1, D)), ANY, _full((1, INC))],
        out_specs=[_tok(D), _tok(S5W), _tok(LW), _tok(2 * D)],
        out_shape=[_sds((L, D), BF), _sds((L, S5W)), _sds((L, LW)), _sds((L, 2 * D))],
        scratch_shapes=[pltpu.VMEM((NCHIP, D, QC), BF), pltpu.SemaphoreType.DMA((NCHIP,))],
        compiler_params=_params(40),
    )(x, g_mix, w_in, b_in)


def _inproj_bwd(x, dx1, dua, dub, dgp, g_mix, w_in, carry=None):
    L = x.shape[0]

    def body(x_ref, dx1_ref, dua_ref, dub_ref, dgp_ref, g_ref, w_hbm, gx_ref, dz_ref, dg_ref, db_ref, w_vm, w_sems):
        _resident_now([(w_hbm.at[k], w_vm.at[k]) for k in range(NCHIP)], w_sems)

        @pl.when(pl.program_id(0) == 0)
        def _():
            dg_ref[...] = jnp.zeros_like(dg_ref)
            db_ref[...] = jnp.zeros_like(db_ref)

        for src, (a, b) in zip((dua_ref, dub_ref, dgp_ref), Z_PARTS):
            d = src[...]
            dz_ref[:, a:b] = d.astype(BF)
            db_ref[0:1, a:b] += _colsum(d)
        dh = jnp.zeros((TM, D), F32)
        for k in range(NCHIP):
            dh = dh + lax.dot_general(dz_ref[:, k * QC:(k + 1) * QC], w_vm[k], (((1,), (1,)), ((), ())),
                                      preferred_element_type=F32)
        xh, r = _rms(x_ref[...])
        dg_ref[0:1, :] += _colsum(dh * xh)
        gx_ref[...] = dx1_ref[...] + _rms_bwd(dh, xh, r, g_ref[...])

    return _pallas_call(
        body, carry, name="inproj_bwd", grid=(L // TM,),
        in_specs=[_tok(D), _tok(D), _tok(S5W), _tok(LW), _tok(2 * D), _full((1, D)), ANY],
        out_specs=[_tok(D), _tok(INC), _full((SUB, D)), _full((SUB, INC))],
        out_shape=[_sds((L, D)), _sds((L, INC), BF), _sds((SUB, D)), _sds((SUB, INC))],
        scratch_shapes=[pltpu.VMEM((NCHIP, D, QC), BF), pltpu.SemaphoreType.DMA((NCHIP,))],
        compiler_params=_params(40),
    )(x, dx1, dua, dub, dgp, g_mix, w_in)


def _cscan(xr_ref, xi_ref, con_ref, cr_ref, ci_ref, reverse):
    n_slab = xr_ref.shape[0] // SUB
    width = xr_ref.shape[1]
    for lc in range(width // LC):
        cols = slice(lc * LC, (lc + 1) * LC)
        con = [con_ref[SUB * j:SUB * (j + 1), cols] for j in range(8)]

        def step(k, carry, cols=cols, con=con):
            cr, ci = carry
            rows = _slab(n_slab - 1 - k if reverse else k)
            xr, xi = xr_ref[rows, cols], xi_ref[rows, cols]
            for j, sh in enumerate((1, 2, 4)):
                mr, mi = con[2 * j], con[2 * j + 1]
                pr = pltpu.roll(xr, SUB - sh if reverse else sh, 0)
                pi = pltpu.roll(xi, SUB - sh if reverse else sh, 0)
                xr, xi = xr + mr * pr - mi * pi, xi + mr * pi + mi * pr
            xr, xi = xr + con[6] * cr - con[7] * ci, xi + con[6] * ci + con[7] * cr
            xr_ref[rows, cols] = xr
            xi_ref[rows, cols] = xi
            row = 0 if reverse else SUB - 1
            return _bcast_row(xr, row), _bcast_row(xi, row)

        cr, ci = lax.fori_loop(0, n_slab, step, (cr_ref[:, cols], ci_ref[:, cols]))
        cr_ref[:, cols] = cr
        ci_ref[:, cols] = ci


def _s5_fwd(ua, bbr, bbi, ccr, cci, dsk, con, w_glu, b_glu, carry=None):
    L = ua.shape[0]

    def body(ua_ref, bbr_hbm, bbi_hbm, ccr_hbm, cci_hbm, dsk_ref, con_ref, wg_ref, bg_ref,
             sr_ref, si_ref, y_ref, zg_ref, ya_ref, bbr_vm, bbi_vm, ccr_vm, cci_vm, cr_ref, ci_ref, w_sems):
        landed = _resident([(bbr_hbm, bbr_vm), (bbi_hbm, bbi_vm), (ccr_hbm, ccr_vm), (cci_hbm, cci_vm)], w_sems)

        @pl.when(pl.program_id(0) == 0)
        def _():
            cr_ref[...] = jnp.zeros_like(cr_ref)
            ci_ref[...] = jnp.zeros_like(ci_ref)

        u = ua_ref[...]
        ub = u.astype(BF)
        landed(0)
        sr_ref[...] = _blockdiag_mm(ub, bbr_vm)
        landed(1)
        si_ref[...] = _blockdiag_mm(ub, bbi_vm)
        _cscan(sr_ref, si_ref, con_ref, cr_ref, ci_ref, reverse=False)
        landed(2)
        landed(3)
        y = (_blockdiag_mm_t(sr_ref[...].astype(BF), ccr_vm) - _blockdiag_mm_t(si_ref[...].astype(BF), cci_vm)
             + dsk_ref[...] * u)
        y_ref[...] = y
        zg = jax.nn.gelu(y)
        zg_ref[...] = zg.astype(BF)
        q = _mm(zg, wg_ref[...]) + bg_ref[...]
        ya_ref[...] = (zg * _sig(q)).astype(BF)

    return _pallas_call(
        body, carry, name="s5_fwd", grid=(L // TM,),
        in_specs=[_tok(S5W), ANY, ANY, ANY, ANY, _full((1, S5W)), _full((8 * SUB, GN)),
                  _full((S5W, S5W)), _full((1, S5W))],
        out_specs=[_tok(GN), _tok(GN), _tok(S5W), _tok(S5W), _tok(S5W)],
        out_shape=[_sds((L, GN)), _sds((L, GN)), _sds((L, S5W)), _sds((L, S5W), BF), _sds((L, S5W), BF)],
        scratch_shapes=[pltpu.VMEM((S5W // 128, 128, GN // (S5W // 128)), BF)] * 4 + [
                        pltpu.VMEM((SUB, GN), F32), pltpu.VMEM((SUB, GN), F32),
                        pltpu.SemaphoreType.DMA((4,))],
        compiler_params=_params(44),
    )(ua, bbr, bbi, ccr, cci, dsk, con, w_glu, b_glu)


def _s5_bwd(dya, y, ua, sr, si, bbr, bbi, ccr, cci, dsk, con_rev, w_glu, b_glu, carry=None):
    L = ua.shape[0]
    nt = L // TM
    spt = TM // SUB
    n_slab = spt

    def halo_map(i):
        return (jnp.maximum((nt - 1 - i) * spt - 1, 0), 0)

    def body(dya_ref, y_ref, ua_ref, sr_ref, si_ref, hr_ref, hi_ref, bbr_hbm, bbi_hbm, ccr_hbm, cci_hbm,
             dsk_ref, con_ref, wg_ref, bg_ref,
             dua_ref, dq_ref, dy_ref, lr_ref, li_ref, da_ref, dsm_ref,
             bbr_vm, bbi_vm, ccr_vm, cci_vm, cr_ref, ci_ref, w_sems):
        i = pl.program_id(0)
        landed = _resident([(ccr_hbm, ccr_vm), (cci_hbm, cci_vm), (bbr_hbm, bbr_vm), (bbi_hbm, bbi_vm)], w_sems)

        @pl.when(i == 0)
        def _():
            cr_ref[...] = jnp.zeros_like(cr_ref)
            ci_ref[...] = jnp.zeros_like(ci_ref)
            da_ref[...] = jnp.zeros_like(da_ref)
            dsm_ref[...] = jnp.zeros_like(dsm_ref)

        u = ua_ref[...]
        yv = y_ref[...]
        dya = dya_ref[...]
        zg = jax.nn.gelu(yv)
        sg = _sig(_mm(zg, wg_ref[...]) + bg_ref[...])
        dq = dya * zg * sg * (1.0 - sg)
        dq_ref[...] = dq.astype(BF)
        dzg = dya * sg + _mm_nt(dq, wg_ref[...])
        dy = dzg * _gelu_grad(yv)
        dyb = dy.astype(BF)
        dy_ref[...] = dyb
        dsm_ref[0:1, :] += _colsum(dy * u)
        dsm_ref[1:2, :] += _colsum(dq)
        landed(0)
        lr_ref[...] = _blockdiag_mm(dyb, ccr_vm)
        landed(1)
        li_ref[...] = -_blockdiag_mm(dyb, cci_vm)
        _cscan(lr_ref, li_ref, con_ref, cr_ref, ci_ref, reverse=True)

        first_tile = (i == nt - 1)
        row = _row_iota(LC)
        for lc in range(GN // LC):
            cols = slice(lc * LC, (lc + 1) * LC)
            h_r = jnp.where(first_tile, 0.0, hr_ref[:, cols])
            h_i = jnp.where(first_tile, 0.0, hi_ref[:, cols])

            def step(k, acc, cols=cols, h_r=h_r, h_i=h_i):
                ar, ai = acc
                rows = _slab(k)
                prev = _slab(jnp.maximum(k - 1, 0))
                pr = jnp.where(k == 0, h_r, sr_ref[prev, cols])
                pi = jnp.where(k == 0, h_i, si_ref[prev, cols])
                spr = pltpu.roll(jnp.where(row == SUB - 1, pr, sr_ref[rows, cols]), 1, 0)
                spi = pltpu.roll(jnp.where(row == SUB - 1, pi, si_ref[rows, cols]), 1, 0)
                lr, li = lr_ref[rows, cols], li_ref[rows, cols]
                return ar + lr * spr + li * spi, ai + li * spr - lr * spi

            zero = jnp.zeros((SUB, LC), F32)
            ar, ai = lax.fori_loop(0, n_slab, step, (zero, zero))
            da_ref[0:1, cols] += _colsum(ar)
            da_ref[1:2, cols] += _colsum(ai)

        landed(2)
        landed(3)
        dua_ref[...] = (dy * dsk_ref[...] + _blockdiag_mm_t(lr_ref[...].astype(BF), bbr_vm)
                        + _blockdiag_mm_t(li_ref[...].astype(BF), bbi_vm))

    return _pallas_call(
        body, carry, name="s5_bwd", grid=(nt,),
        in_specs=[_tok_rev(S5W, nt), _tok_rev(S5W, nt), _tok_rev(S5W, nt), _tok_rev(GN, nt), _tok_rev(GN, nt),
                  pl.BlockSpec((SUB, GN), halo_map), pl.BlockSpec((SUB, GN), halo_map),
                  ANY, ANY, ANY, ANY, _full((1, S5W)), _full((8 * SUB, GN)), _full((S5W, S5W)), _full((1, S5W))],
        out_specs=[_tok_rev(S5W, nt), _tok_rev(S5W, nt), _tok_rev(S5W, nt), _tok_rev(GN, nt), _tok_rev(GN, nt),
                   _full((SUB, GN)), _full((SUB, S5W))],
        out_shape=[_sds((L, S5W)), _sds((L, S5W), BF), _sds((L, S5W), BF), _sds((L, GN)), _sds((L, GN)),
                   _sds((SUB, GN)), _sds((SUB, S5W))],
        scratch_shapes=[pltpu.VMEM((S5W // 128, 128, GN // (S5W // 128)), BF)] * 4 + [
                        pltpu.VMEM((SUB, GN), F32), pltpu.VMEM((SUB, GN), F32),
                        pltpu.SemaphoreType.DMA((4,))],
        compiler_params=_params(52),
    )(dya, y, ua, sr, si, sr, si, bbr, bbi, ccr, cci, dsk, con_rev, w_glu, b_glu)


def _lru_gate_terms(rg, sp):
    log_a = -LRU_C * rg * sp
    a = jnp.exp(log_a)
    mult = jnp.sqrt(_neg_expm1(2.0 * log_a))
    return a, mult


def _lru_fwd(ub, conv_w, conv_b, wr, wi, b_r, b_i, sp, carry=None):
    L = ub.shape[0]
    n_slab = TM // SUB

    def body(ub_ref, cw_ref, cb_ref, wr_ref, wi_ref, br_ref, bi_ref, sp_ref,
             xc_ref, rg_ref, ig_ref, h_ref, hp_ref, a_ref, halo_ref, carry_ref):
        @pl.when(pl.program_id(0) == 0)
        def _():
            halo_ref[...] = jnp.zeros_like(halo_ref)
            carry_ref[...] = jnp.zeros_like(carry_ref)

        row = _row_iota(LW)
        taps = [cw_ref[k:k + 1, :] for k in range(4)]
        cb = cb_ref[...]

        def conv_step(k, prev):
            rows = _slab(k)
            cur = ub_ref[rows, :]
            acc = taps[3] * cur + cb
            for j in (1, 2, 3):
                acc = acc + taps[3 - j] * pltpu.roll(jnp.where(row >= SUB - j, prev, cur), j, 0)
            xc_ref[rows, :] = acc
            return cur

        halo_ref[...] = lax.fori_loop(0, n_slab, conv_step, halo_ref[...])

        xc = xc_ref[...]
        xcb = xc.astype(BF)
        rg = _sig(_blockdiag_mm(xcb, wr_ref) + br_ref[...])
        ig = _sig(_blockdiag_mm(xcb, wi_ref) + bi_ref[...])
        rg_ref[...] = rg
        ig_ref[...] = ig
        a, mult = _lru_gate_terms(rg, sp_ref[...])
        a_ref[...] = a
        h_ref[...] = mult * ig * xc

        rowc = _row_iota(LC)
        for lc in range(LW // LC):
            cols = slice(lc * LC, (lc + 1) * LC)

            def step(k, c, cols=cols):
                rows = _slab(k)
                av, b = a_ref[rows, cols], h_ref[rows, cols]
                for sh in (1, 2, 4):
                    keep = rowc >= sh
                    b = b + av * jnp.where(keep, pltpu.roll(b, sh, 0), 0.0)
                    av = av * jnp.where(keep, pltpu.roll(av, sh, 0), 1.0)
                h = b + av * c
                h_ref[rows, cols] = h
                hp_ref[rows, cols] = jnp.where(rowc == 0, c, pltpu.roll(h, 1, 0))
                return _bcast_row(h, SUB - 1)

            carry_ref[:, cols] = lax.fori_loop(0, n_slab, step, carry_ref[:, cols])

    return _pallas_call(
        body, carry, name="lru_fwd", grid=(L // TM,),
        in_specs=[_tok(LW), _full((4, LW)), _full((1, LW)), _full((LW // 128, 128, 128)), _full((LW // 128, 128, 128)),
                  _full((1, LW)), _full((1, LW)), _full((1, LW))],
        out_specs=[_tok(LW)] * 5,
        out_shape=[_sds((L, LW))] * 5,
        scratch_shapes=[pltpu.VMEM((TM, LW), F32), pltpu.VMEM((SUB, LW), F32), pltpu.VMEM((SUB, LW), F32)],
        compiler_params=_params(40),
    )(ub, conv_w, conv_b, wr, wi, b_r, b_i, sp)


def _lru_bwd(dyb, xc, rg, ig, hp, ub, conv_w, wr, wi, sp, dsp, carry=None):
    L = ub.shape[0]
    nt = L // TM
    spt = TM // SUB
    n_slab = spt

    def halo_map(i):
        return (jnp.maximum((nt - 1 - i) * spt - 1, 0), 0)

    def body(dh_ref, xc_ref, rg_ref, ig_ref, hp_ref, ub_ref, uh_ref, cw_ref, wr_ref, wi_ref, sp_ref, dsp_ref,
             dub_ref, dpr_ref, dpi_ref, acc_ref, a_ref, lam_ref, dxc_ref, carry_ref, next_ref):
        i = pl.program_id(0)

        @pl.when(i == 0)
        def _():
            carry_ref[...] = jnp.zeros_like(carry_ref)
            next_ref[...] = jnp.zeros_like(next_ref)
            acc_ref[...] = jnp.zeros_like(acc_ref)

        sp = sp_ref[...]
        rg, ig, xc = rg_ref[...], ig_ref[...], xc_ref[...]
        a, mult = _lru_gate_terms(rg, sp)
        a_ref[...] = a

        rowc = _row_iota(LC)
        for lc in range(LW // LC):
            cols = slice(lc * LC, (lc + 1) * LC)

            def step(k, c, cols=cols):
                rows = _slab(n_slab - 1 - k)
                av, dh = a_ref[rows, cols], dh_ref[rows, cols]
                b = av * dh
                for sh in (1, 2, 4):
                    keep = rowc < SUB - sh
                    b = b + av * jnp.where(keep, pltpu.roll(b, SUB - sh, 0), 0.0)
                    av = av * jnp.where(keep, pltpu.roll(av, SUB - sh, 0), 1.0)
                mu = b + av * c
                lam_ref[rows, cols] = dh + jnp.where(rowc == SUB - 1, c, pltpu.roll(mu, SUB - 1, 0))
                return _bcast_row(mu, 0)

            carry_ref[:, cols] = lax.fori_loop(0, n_slab, step, carry_ref[:, cols])

        lam = lam_ref[...]
        d_a = lam * hp_ref[...]
        d_mult = lam * ig * xc
        d_ig = lam * mult * xc
        dxc = lam * mult * ig
        d_log_a = d_a * a - d_mult * a * a / mult
        d_rg = (-LRU_C) * sp * d_log_a
        acc_ref[0:1, :] += _colsum((-LRU_C) * rg * d_log_a) * dsp_ref[...]
        dpr = d_rg * rg * (1.0 - rg)
        dpi = d_ig * ig * (1.0 - ig)
        acc_ref[1:2, :] += _colsum(dpr)
        acc_ref[2:3, :] += _colsum(dpi)
        dprb, dpib = dpr.astype(BF), dpi.astype(BF)
        dpr_ref[...] = dprb
        dpi_ref[...] = dpib
        dxc = dxc + _blockdiag_mm_t(dprb, wr_ref) + _blockdiag_mm_t(dpib, wi_ref)
        dxc_ref[...] = dxc
        acc_ref[3:4, :] += _colsum(dxc)

        row = _row_iota(LW)
        taps = [cw_ref[k:k + 1, :] for k in range(4)]
        u_halo = jnp.where(i == nt - 1, 0.0, uh_ref[...])
        nxt_tile = next_ref[...]

        def conv_step(k, accs):
            rows = _slab(k)
            cur = dxc_ref[rows, :]
            nxt = jnp.where(k == n_slab - 1, nxt_tile, dxc_ref[_slab(jnp.minimum(k + 1, n_slab - 1)), :])
            ucur = ub_ref[rows, :]
            uprev = jnp.where(k == 0, u_halo, ub_ref[_slab(jnp.maximum(k - 1, 0)), :])
            du = taps[3] * cur
            new = [accs[3] + cur * ucur]
            for j in (1, 2, 3):
                du = du + taps[3 - j] * pltpu.roll(jnp.where(row < j, nxt, cur), SUB - j, 0)
                new.append(accs[3 - j] + cur * pltpu.roll(jnp.where(row >= SUB - j, uprev, ucur), j, 0))
            dub_ref[rows, :] = du
            return tuple(new[::-1])

        zero = jnp.zeros((SUB, LW), F32)
        accs = lax.fori_loop(0, n_slab, conv_step, (zero, zero, zero, zero))
        for k in range(4):
            acc_ref[4 + k:5 + k, :] += _colsum(accs[k])
        next_ref[...] = dxc_ref[0:SUB, :]

    return _pallas_call(
        body, carry, name="lru_bwd", grid=(nt,),
        in_specs=[_tok_rev(LW, nt)] * 6 + [pl.BlockSpec((SUB, LW), halo_map), _full((4, LW)),
                                           _full((LW // 128, 128, 128)), _full((LW // 128, 128, 128)), _full((1, LW)), _full((1, LW))],
        out_specs=[_tok_rev(LW, nt), _tok_rev(LW, nt), _tok_rev(LW, nt), _full((SUB, LW))],
        out_shape=[_sds((L, LW)), _sds((L, LW), BF), _sds((L, LW), BF), _sds((SUB, LW))],
        scratch_shapes=[pltpu.VMEM((TM, LW), F32), pltpu.VMEM((TM, LW), F32), pltpu.VMEM((TM, LW), F32),
                        pltpu.VMEM((SUB, LW), F32), pltpu.VMEM((SUB, LW), F32)],
        compiler_params=_params(48),
    )(dyb, xc, rg, ig, hp, ub, ub, conv_w, wr, wi, sp, dsp)


AC = D // NCHIP


def _merge_fwd(x, ya, yb, gp, w_a, w_b, w_o, carry=None):
    L = x.shape[0]

    def body(x_ref, ya_ref, yb_ref, gp_ref, wa_ref, wb_ref, wo_ref, x1_ref, pa_ref, pb_ref, mg_ref):
        ya = ya_ref[...]
        for k in range(NCHIP):
            pa_ref[:, k * AC:(k + 1) * AC] = jnp.dot(ya, wa_ref[k], preferred_element_type=F32)
        pb = _mm(yb_ref[...], wb_ref[...])
        pb_ref[...] = pb
        gp = gp_ref[...]
        merged = (_sig(gp[:, :D]) * pa_ref[...] + _sig(gp[:, D:]) * pb).astype(BF)
        mg_ref[...] = merged
        x1_ref[...] = x_ref[...] + jnp.dot(merged, wo_ref[...], preferred_element_type=F32)

    return _pallas_call(
        body, carry, name="merge_fwd", grid=(L // TM,),
        in_specs=[_tok(D), _tok(S5W), _tok(LW), _tok(2 * D), _full((NCHIP, S5W, AC)), _full((LW, D)), _full((D, D))],
        out_specs=[_tok(D), _tok(D), _tok(D), _tok(D)],
        out_shape=[_sds((L, D)), _sds((L, D)), _sds((L, D)), _sds((L, D), BF)],
        compiler_params=_params(40),
    )(x, ya, yb, gp, w_a, w_b, w_o)


def _merge_bwd(dx1, gp, pa, pb, w_a, w_b, w_o, carry=None):
    L = dx1.shape[0]

    def body(dx1_ref, gp_ref, pa_ref, pb_ref, wa_ref, wb_ref, wo_ref, dya_ref, dyb_ref, dgp_ref, dpa_ref, dpb_ref):
        dm = _mm_nt(dx1_ref[...], wo_ref[...])
        gp = gp_ref[...]
        sa, sb = _sig(gp[:, :D]), _sig(gp[:, D:])
        dpa = (dm * sa).astype(BF)
        dpb = (dm * sb).astype(BF)
        dpa_ref[...] = dpa
        dpb_ref[...] = dpb
        dgp_ref[:, :D] = dm * pa_ref[...] * sa * (1.0 - sa)
        dgp_ref[:, D:] = dm * pb_ref[...] * sb * (1.0 - sb)
        dya = jnp.zeros((TM, S5W), F32)
        for k in range(NCHIP):
            dya = dya + _mm_nt(dpa[:, k * AC:(k + 1) * AC], wa_ref[k])
        dya_ref[...] = dya
        dyb_ref[...] = _mm_nt(dpb, wb_ref[...])

    return _pallas_call(
        body, carry, name="merge_bwd", grid=(L // TM,),
        in_specs=[_tok(D), _tok(2 * D), _tok(D), _tok(D), _full((NCHIP, S5W, AC)), _full((LW, D)), _full((D, D))],
        out_specs=[_tok(S5W), _tok(LW), _tok(2 * D), _tok(D), _tok(D)],
        out_shape=[_sds((L, S5W)), _sds((L, LW)), _sds((L, 2 * D)), _sds((L, D), BF), _sds((L, D), BF)],
        compiler_params=_params(40),
    )(dx1, gp, pa, pb, w_a, w_b, w_o)


def _chunk_tok(width):
    return pl.BlockSpec((NCHIP, TM, width), lambda i: (0, i, 0))


def _ffn_fwd(x1, g_ffn, wg, wu, wd, carry=None):
    L = x1.shape[0]

    def body(x_ref, g_ref, wg_hbm, wu_hbm, wd_hbm, x2_ref, h2_ref, gg_ref, uu_ref, wg_vm, wu_vm, wd_vm, w_sems):
        _resident_now([(src.at[c], dst.at[c]) for c in range(NCHIP)
                       for src, dst in ((wg_hbm, wg_vm), (wu_hbm, wu_vm), (wd_hbm, wd_vm))], w_sems)
        x = x_ref[...]
        xh, _ = _rms(x)
        h2 = (xh * g_ref[...]).astype(BF)
        h2_ref[...] = h2
        out = x
        for c in range(NCHIP):
            gg = lax.dot_general(h2, wg_vm[c], (((1,), (1,)), ((), ())), preferred_element_type=F32)
            uu = lax.dot_general(h2, wu_vm[c], (((1,), (1,)), ((), ())), preferred_element_type=F32)
            gg_ref[c] = gg.astype(BF)
            uu_ref[c] = uu.astype(BF)
            act = (gg * _sig(gg) * uu).astype(BF)
            out = out + jnp.dot(act, wd_vm[c], preferred_element_type=F32)
        x2_ref[...] = out

    return _pallas_call(
        body, carry, name="ffn_fwd", grid=(L // TM,),
        in_specs=[_tok(D), _full((1, D)), ANY, ANY, ANY],
        out_specs=[_tok(D), _tok(D), _chunk_tok(FC), _chunk_tok(FC)],
        out_shape=[_sds((L, D)), _sds((L, D), BF), _sds((NCHIP, L, FC), BF), _sds((NCHIP, L, FC), BF)],
        scratch_shapes=[pltpu.VMEM((NCHIP, FC, D), BF)] * 3 + [pltpu.SemaphoreType.DMA((3 * NCHIP,))],
        compiler_params=_params(52),
    )(x1, g_ffn, wg, wu, wd)


def _ffn_bwd(x1, dx2, gg, uu, g_ffn, wg, wu, wd, carry=None):
    L = x1.shape[0]

    def body(x_ref, dx2_ref, gg_ref, uu_ref, g_ref, wg_hbm, wu_hbm, wd_hbm,
             dx1_ref, act_ref, dgg_ref, duu_ref, dg_ref, wg_vm, wu_vm, wd_vm, w_sems):
        _resident_now([(src.at[c], dst.at[c]) for c in range(NCHIP)
                       for src, dst in ((wg_hbm, wg_vm), (wu_hbm, wu_vm), (wd_hbm, wd_vm))], w_sems)

        @pl.when(pl.program_id(0) == 0)
        def _():
            dg_ref[...] = jnp.zeros_like(dg_ref)

        dx2 = dx2_ref[...]
        dx2b = dx2.astype(BF)
        dh2 = jnp.zeros((TM, D), F32)
        for c in range(NCHIP):
            g = gg_ref[c].astype(F32)
            u = uu_ref[c].astype(F32)
            s = _sig(g)
            silu = g * s
            act_ref[c] = (silu * u).astype(BF)
            dact = lax.dot_general(dx2b, wd_vm[c], (((1,), (1,)), ((), ())), preferred_element_type=F32)
            dg = (dact * u * s * (1.0 + g * (1.0 - s))).astype(BF)
            du = (dact * silu).astype(BF)
            dgg_ref[c] = dg
            duu_ref[c] = du
            dh2 = dh2 + jnp.dot(dg, wg_vm[c], preferred_element_type=F32)
            dh2 = dh2 + jnp.dot(du, wu_vm[c], preferred_element_type=F32)
        xh, r = _rms(x_ref[...])
        dg_ref[0:1, :] += _colsum(dh2 * xh)
        dx1_ref[...] = dx2 + _rms_bwd(dh2, xh, r, g_ref[...])

    return _pallas_call(
        body, carry, name="ffn_bwd", grid=(L // TM,),
        in_specs=[_tok(D), _tok(D), _chunk_tok(FC), _chunk_tok(FC), _full((1, D)), ANY, ANY, ANY],
        out_specs=[_tok(D), _chunk_tok(FC), _chunk_tok(FC), _chunk_tok(FC), _full((SUB, D))],
        out_shape=[_sds((L, D)), _sds((NCHIP, L, FC), BF), _sds((NCHIP, L, FC), BF), _sds((NCHIP, L, FC), BF),
                   _sds((SUB, D))],
        scratch_shapes=[pltpu.VMEM((NCHIP, FC, D), BF)] * 3 + [pltpu.SemaphoreType.DMA((3 * NCHIP,))],
        compiler_params=_params(56),
    )(x1, dx2, gg, uu, g_ffn, wg, wu, wd)


def _ple_loss(x2, p, tgt, g_pg, w_pg, b_pg, w_ple, g_ple, g_final):
    L = x2.shape[0]

    def body(x2_ref, p_ref, t_ref, gpg_ref, wpg_ref, bpg_ref, wple_ref, gple_ref, gf_ref,
             dx2_ref, n2_ref, dpre_ref, de0_ref, acc_ref):
        @pl.when(pl.program_id(0) == 0)
        def _():
            acc_ref[...] = jnp.zeros_like(acc_ref)

        x2 = x2_ref[...]
        x2h, r2 = _rms(x2)
        n2 = (x2h * gpg_ref[...]).astype(BF)
        n2_ref[...] = n2
        gate = _sig(jnp.dot(n2, wpg_ref[...], preferred_element_type=F32) + bpg_ref[...])
        pb = p_ref[...].astype(BF)
        e0 = jnp.concatenate([jnp.dot(pb, wple_ref[k], preferred_element_type=F32) for k in range(NCHIP)], axis=1)
        e0h, re = _rms(e0)
        e = e0h * gple_ref[...]
        x3 = x2 + gate * e
        x3h, r3 = _rms(x3)
        diff = x3h * gf_ref[...] - t_ref[...]
        acc_ref[4:5, :] += _colsum(diff * diff) * (0.5 / D)
        dy = diff * (1.0 / D)
        acc_ref[3:4, :] += _colsum(dy * x3h)
        dx3 = _rms_bwd(dy, x3h, r3, gf_ref[...])
        de = dx3 * gate
        acc_ref[2:3, :] += _colsum(de * e0h)
        de0_ref[...] = _rms_bwd(de, e0h, re, gple_ref[...]).astype(BF)
        dpre = dx3 * e * gate * (1.0 - gate)
        acc_ref[1:2, :] += _colsum(dpre)
        dpreb = dpre.astype(BF)
        dpre_ref[...] = dpreb
        dn2 = lax.dot_general(dpreb, wpg_ref[...], (((1,), (1,)), ((), ())), preferred_element_type=F32)
        acc_ref[0:1, :] += _colsum(dn2 * x2h)
        dx2_ref[...] = dx3 + _rms_bwd(dn2, x2h, r2, gpg_ref[...])

    return _pallas_call(
        body, name="ple_loss", grid=(L // TM,),
        in_specs=[_tok(D), _tok(PLE), _tok(D), _full((1, D)), _full((D, D)), _full((1, D)), _full((NCHIP, PLE, AC)),
                  _full((1, D)), _full((1, D))],
        out_specs=[_tok(D), _tok(D), _tok(D), _tok(D), _full((SUB, D))],
        out_shape=[_sds((L, D)), _sds((L, D), BF), _sds((L, D), BF), _sds((L, D), BF), _sds((SUB, D))],
        compiler_params=_params(40),
    )(x2, p, tgt, g_pg, w_pg, b_pg, w_ple, g_ple, g_final)


def _tn(name, a, b, col_chunk=None, a_block=None, carry=None):
    L = a.shape[-2]
    m, n = a.shape[-1], b.shape[-1]
    a_col = 0
    if a_block is not None:
        a_col, m = a_block
    tk = L if (a.ndim == 3 or b.ndim == 3 or a_block is not None) else TK
    if a.ndim == 3 or b.ndim == 3:
        nj, bn = (a if a.ndim == 3 else b).shape[0], n
        a_spec = (pl.BlockSpec((None, tk, m), lambda j, t: (j, t, 0)) if a.ndim == 3
                  else pl.BlockSpec((tk, m), lambda j, t: (t, 0)))
        b_spec = (pl.BlockSpec((None, tk, n), lambda j, t: (j, t, 0)) if b.ndim == 3
                  else pl.BlockSpec((tk, n), lambda j, t: (t, 0)))
        out_spec, out_shape = pl.BlockSpec((None, m, n), lambda j, t: (j, 0, 0)), _sds((nj, m, n))
    else:
        bn = col_chunk
        if bn is None:
            bn = next((cand for cand in (1024, 512) if n > cand and n % cand == 0), n)
        nj = n // bn
        a_spec = pl.BlockSpec((tk, m), lambda j, t: (t, a_col))
        b_spec = pl.BlockSpec((tk, bn), lambda j, t: (t, j))
        if col_chunk is None:
            out_spec, out_shape = pl.BlockSpec((m, bn), lambda j, t: (0, j)), _sds((m, n))
        else:
            out_spec, out_shape = pl.BlockSpec((None, m, bn), lambda j, t: (j, 0, 0)), _sds((nj, m, bn))

    def body(a_ref, b_ref, o_ref):
        if tk == L:
            o_ref[...] = _mm_tn(a_ref[...], b_ref[...])
        else:
            @pl.when(pl.program_id(1) == 0)
            def _():
                o_ref[...] = jnp.zeros_like(o_ref)

            o_ref[...] += _mm_tn(a_ref[...], b_ref[...])

    outs = _pallas_call(
        body, carry, name=name, grid=(nj, L // tk), in_specs=[a_spec, b_spec], out_specs=[out_spec],
        out_shape=[pltpu.HBM(out_shape.shape, out_shape.dtype)],
        compiler_params=pltpu.CompilerParams(dimension_semantics=("arbitrary", "arbitrary"),
                                             vmem_limit_bytes=(52 if tk == L else 28) * VMEM_MB),
    )(a, b)
    return outs[0] if carry is None else outs


LANE = 128


def _tn_blocks(name, a, bs, ga, gb, carry=None):
    L, m, n, nb = a.shape[0], a.shape[1], bs[0].shape[1], len(bs)
    per = LANE // ga
    wb = per * gb
    n_super = m // LANE

    def body(a_ref, *refs):
        b_refs, o_refs, acc_refs = refs[:nb], refs[nb:2 * nb], refs[2 * nb:]
        t = pl.program_id(0)

        @pl.when(t == 0)
        def _():
            for acc in acc_refs:
                acc[...] = jnp.zeros_like(acc)

        lhs = a_ref[...].astype(BF)
        for b_ref, acc in zip(b_refs, acc_refs):
            rhs = b_ref[...].astype(BF)
            for j in range(n_super):
                acc[j] += _mm_tn(lhs[:, j * LANE:(j + 1) * LANE], rhs[:, j * wb:(j + 1) * wb])

        @pl.when(t == L // TK - 1)
        def _():
            own = (lax.broadcasted_iota(jnp.int32, (LANE, wb), 0) // ga) == (lax.broadcasted_iota(jnp.int32, (LANE, wb), 1) // gb)
            for o_ref, acc in zip(o_refs, acc_refs):
                for j in range(n_super):
                    kept = jnp.where(own, acc[j], 0.0)
                    o_ref[:, j * wb:(j + 1) * wb] = jnp.sum(kept.reshape(per, ga, wb), axis=0)

    outs = _pallas_call(
        body, carry, name=name, grid=(L // TK,),
        in_specs=[pl.BlockSpec((TK, m), lambda t: (t, 0))] + [pl.BlockSpec((TK, n), lambda t: (t, 0))] * nb,
        out_specs=[_full((ga, n))] * nb, out_shape=[_sds((ga, n))] * nb,
        scratch_shapes=[pltpu.VMEM((n_super, LANE, wb), F32)] * nb,
        compiler_params=_params(48),
    )(*_in_hbm([a] + list(bs)))
    return list(outs)


def _s5_discretize(lam_re, lam_im, log_dt, b_re, b_im):
    dt = jnp.exp(log_dt)[:, None]
    mag = jnp.exp(lam_re * dt)
    ar = mag * jnp.cos(lam_im * dt)
    ai = mag * jnp.sin(lam_im * dt)
    den = lam_re * lam_re + lam_im * lam_im
    nr = ar - 1.0
    fr = (nr * lam_re + ai * lam_im) / den
    fi = (ai * lam_re - nr * lam_im) / den
    bbr = fr[:, None, :] * b_re - fi[:, None, :] * b_im
    bbi = fr[:, None, :] * b_im + fi[:, None, :] * b_re
    return ar, ai, bbr, bbi


def _prepare(by_rows, block_cols, ar, ai):
    n = len(by_rows)

    def body(*refs):
        srcs, (ar_ref, ai_ref), dense, (con_ref, rev_ref) = refs[:n], refs[n:n + 2], refs[n + 2:2 * n + 2], refs[2 * n + 2:]
        for src, out, c in zip(srcs, dense, block_cols):
            r = src.shape[0]
            per = LANE // r
            wide = per * c
            own = (lax.broadcasted_iota(jnp.int32, (LANE, wide), 0) // r) == (lax.broadcasted_iota(jnp.int32, (LANE, wide), 1) // c)
            for j in range(out.shape[0]):
                tiled = jnp.broadcast_to(src[:, j * wide:(j + 1) * wide][None], (per, r, wide)).reshape(LANE, wide)
                out[j] = jnp.where(own, tiled, 0.0).astype(BF)
        a_r, a_i = ar_ref[...], ai_ref[...]
        pw = [(jnp.ones_like(a_r), jnp.zeros_like(a_i))]
        for _ in range(SUB):
            pr, pi = pw[-1]
            pw.append((pr * a_r - pi * a_i, pr * a_i + pi * a_r))
        row = _row_iota(GN)
        for ref, reverse in ((con_ref, False), (rev_ref, True)):
            sign = -1.0 if reverse else 1.0
            for j, sh in enumerate((1, 2, 4)):
                keep = (row < SUB - sh) if reverse else (row >= sh)
                ref[2 * j * SUB:(2 * j + 1) * SUB, :] = jnp.where(keep, pw[sh][0], 0.0)
                ref[(2 * j + 1) * SUB:(2 * j + 2) * SUB, :] = jnp.where(keep, sign * pw[sh][1], 0.0)
            p_r, p_i = jnp.zeros((SUB, GN), F32), jnp.zeros((SUB, GN), F32)
            for i in range(SUB):
                k = SUB - i if reverse else i + 1
                p_r = jnp.where(row == i, pw[k][0], p_r)
                p_i = jnp.where(row == i, sign * pw[k][1], p_i)
            ref[6 * SUB:7 * SUB, :] = p_r
            ref[7 * SUB:8 * SUB, :] = p_i

    dense_shapes = [(b.shape[1] // (LANE // b.shape[0] * c), LANE, LANE // b.shape[0] * c)
                    for b, c in zip(by_rows, block_cols)]
    outs = _pallas_call(
        body, name="prepare", grid=(1,), in_specs=[_full(b.shape) for b in by_rows] + [_full((1, GN))] * 2,
        out_specs=[_full(s) for s in dense_shapes] + [_full((8 * SUB, GN))] * 2,
        out_shape=[_sds(s, BF) for s in dense_shapes] + [_sds((8 * SUB, GN))] * 2,
        compiler_params=_params(48),
    )(*by_rows, ar, ai)
    return outs[:n], outs[n], outs[n + 1]


def _local_step(x, p, tgt, w, comm):
    rows_of = lambda a: a.reshape(NCHIP * a.shape[1], a.shape[2])
    quarters = lambda a: a.reshape(NCHIP, a.shape[0] // NCHIP, a.shape[1])

    def gathering(names, call):
        carry = comm.gather(names)
        outs = list(call(carry))
        own = len(outs) - len(carry.out_shapes)
        w.update(zip(names, outs[own:]))
        return outs[:own]

    w.update(comm.first())
    w_glu = rows_of(w["w_glu"])
    ar, ai, bbr, bbi = _s5_discretize(w["lam_re"], w["lam_im"], w["log_dt"], w["s5_b_re"], w["s5_b_im"])
    by_row = lambda b: jnp.transpose(b, (1, 0, 2)).reshape(b.shape[1], -1)
    (bbr_d, bbi_d, ccr_d, cci_d, wr_d, wi_d), con, con_rev = _prepare(
        [by_row(b) for b in (bbr, bbi, w["s5_c_re"], w["s5_c_im"], w["w_r"], w["w_i"])], [NS] * 4 + [HD] * 2,
        ar.reshape(1, GN), ai.reshape(1, GN))
    dsk = w["s5_d"].reshape(1, S5W)
    lam = w["lru_lambda"].reshape(1, LW)
    sp = jax.nn.softplus(-lam)
    b_r, b_i = w["b_r"].reshape(1, LW), w["b_i"].reshape(1, LW)
    row = lambda name: w[name].reshape(1, -1)

    h, ua, ub, gp = gathering(["w_a_out", "w_b_out"], lambda carry: _inproj_fwd(
        x, row("g_mix"), w["w_in"], row("b_in"), carry))
    sr, si, y, zg, ya = gathering(["w_o", "w_ffn_gate"], lambda carry: _s5_fwd(
        ua, bbr_d, bbi_d, ccr_d, cci_d, dsk, con, w_glu, row("b_glu"), carry))
    xc, rg, ig, yb, hp = gathering(["w_ffn_up"], lambda carry: _lru_fwd(
        ub, w["conv_w"], row("conv_b"), wr_d, wi_d, b_r, b_i, sp, carry))
    w_b_out, w_o = rows_of(w["w_b_out"]), rows_of(w["w_o"])
    x1, pa, pb, merged = gathering(["w_ffn_down"], lambda carry: _merge_fwd(
        x, ya, yb, gp, w["w_a_out"], w_b_out, w_o, carry))
    x2, h2, gg, uu = gathering(["w_ple_gate", "w_ple"], lambda carry: _ffn_fwd(
        x1, row("g_ffn"), w["w_ffn_gate"], w["w_ffn_up"], w["w_ffn_down"], carry))
    w_pg = rows_of(w["w_ple_gate"])
    dx2, n2, dpre, de0, acc_p = _ple_loss(x2, p, tgt, row("g_ple_gate"), w_pg, row("b_ple_gate"),
                                          w["w_ple"], row("g_ple"), row("g_final"))
    comm.reduce("ple", {"w_ple_gate": quarters(_tn("dw_ple_gate", n2, dpre)),
                        "w_ple": _tn("dw_ple", p, de0, col_chunk=AC)})
    dx1, act, dgg, duu, acc_f = comm.run(lambda carry: _ffn_bwd(
        x1, dx2, gg, uu, row("g_ffn"), w["w_ffn_gate"], w["w_ffn_up"], w["w_ffn_down"], carry))
    comm.reduce("ffn_gate", {"w_ffn_gate": _tn("dw_ffn_gate", dgg, h2)})
    comm.reduce("w_o", {"w_o": quarters(_tn("dw_o", merged, dx1))})
    comm.reduce("ffn_up", {"w_ffn_up": comm.run(lambda carry: _tn("dw_ffn_up", duu, h2, carry=carry))[0]})
    comm.reduce("ffn_down", {"w_ffn_down": comm.run(lambda carry: _tn("dw_ffn_down", act, dx2, carry=carry),
                                                    hold=("ffn_gate", "w_o"))[0]})
    dya, dyb, dgp, dpa, dpb = comm.run(lambda carry: _merge_bwd(
        dx1, gp, pa, pb, w["w_a_out"], w_b_out, w_o, carry), hold=("ffn_gate", "ffn_up"))
    comm.reduce("merge", {"w_a_out": _tn("dw_a_out", ya, dpa, col_chunk=AC), "w_b_out": quarters(_tn("dw_b_out", yb, dpb))})
    dua, dq, dy, lr, li, acc_a, acc_s = comm.run(lambda carry: _s5_bwd(
        dya, y, ua, sr, si, bbr_d, bbi_d, ccr_d, cci_d, dsk, con_rev, w_glu, row("b_glu"), carry), hold=("ffn_down",))
    dub, dpr, dpi, acc_l = comm.run(lambda carry: _lru_bwd(
        dyb, xc, rg, ig, hp, ub, w["conv_w"], wr_d, wi_d, sp, -_sig(-lam), carry))
    gx, dz, acc_g, acc_b = _inproj_bwd(x, dx1, dua, dub, dgp, row("g_mix"), w["w_in"])
    half = (D // 2,)
    comm.reduce("in_lo", {"w_in_lo": comm.run(lambda carry: _tn(
        "dw_in_lo", h, dz, col_chunk=QC, a_block=(0,) + half, carry=carry))[0]})
    comm.reduce("in_hi", {"w_in_hi": comm.run(lambda carry: _tn(
        "dw_in_hi", h, dz, col_chunk=QC, a_block=(1,) + half, carry=carry))[0], "w_glu": quarters(_tn("dw_glu", zg, dq))})
    d_wr, d_wi = comm.run(lambda carry: _tn_blocks("dw_r_i", xc, [dpr, dpi], HD, HD, carry))
    d_bbr, d_bbi = comm.run(lambda carry: _tn_blocks("d_bb", ua, [lr, li], NP, NS, carry))
    d_ccr, d_cci = comm.run(lambda carry: _tn_blocks("d_cc", dy, [sr, si], NP, NS, carry))
    comm.drain()
    sums = {"ple": acc_p, "ffn": acc_f, "mix": acc_g, "b_in": acc_b, "lru": acc_l, "s5": acc_s, "s5_a": acc_a}
    blocks = {"bb_re": d_bbr, "bb_im": d_bbi,
              "cc_re": d_ccr, "cc_im": d_cci,
              "w_r": d_wr, "w_i": d_wi}
    return gx, sums, blocks


def _replicated_grads(w, sums, blocks):
    grouped = lambda e, groups: jnp.transpose(e.reshape(e.shape[0], groups, -1), (1, 0, 2))
    d_ar, d_ai = sums["s5_a"][0].reshape(NG, NS), sums["s5_a"][1].reshape(NG, NS)
    d_bbr, d_bbi = grouped(blocks["bb_re"], NG), grouped(blocks["bb_im"], NG)
    _, vjp = jax.vjp(_s5_discretize, w["lam_re"], w["lam_im"], w["log_dt"], w["s5_b_re"], w["s5_b_im"])
    g = dict(zip(("lam_re", "lam_im", "log_dt", "s5_b_re", "s5_b_im"), vjp((d_ar, d_ai, d_bbr, d_bbi))))
    g["s5_c_re"] = grouped(blocks["cc_re"], NG)
    g["s5_c_im"] = -grouped(blocks["cc_im"], NG)
    g["w_r"], g["w_i"] = grouped(blocks["w_r"], NH), grouped(blocks["w_i"], NH)
    g["s5_d"] = sums["s5"][0].reshape(NG, NP)
    g["b_r"] = sums["lru"][1].reshape(NH, HD)
    g["b_i"] = sums["lru"][2].reshape(NH, HD)
    return g


ACC_ROWS = {"g_mix": ("mix", 0), "b_in": ("b_in", 0), "g_ffn": ("ffn", 0), "g_ple_gate": ("ple", 0),
            "b_ple_gate": ("ple", 1), "g_ple": ("ple", 2), "g_final": ("ple", 3), "b_glu": ("s5", 1),
            "lru_lambda": ("lru", 0), "conv_b": ("lru", 3)}
LOSS_ROW = ("ple", 4)
CONV_W_ROWS = ("lru", 4)


SHARDED = [("w_in", (D, QC)), ("w_glu", (S5W // NCHIP, S5W)), ("w_a_out", (S5W, AC)), ("w_b_out", (LW // NCHIP, D)),
           ("w_o", (D // NCHIP, D)), ("w_ffn_gate", (FC, D)), ("w_ffn_up", (FC, D)), ("w_ffn_down", (FC, D)),
           ("w_ple_gate", (D // NCHIP, D)), ("w_ple", (PLE, AC))]
NSH = len(SHARDED)
TRANSPOSED = ("w_ffn_gate", "w_ffn_up", "s5_b_re", "s5_b_im")
CONV_SHARD = (4, LW // NCHIP)


def _mesh_pos():
    return lax.axis_index("x"), lax.axis_index("y"), lax.axis_index("c")


def _other_chips(x, y):
    return [(1 - x, y), (x, 1 - y), (1 - x, 1 - y)]


def _half_rows(c, rows, align):
    return pl.ds(pl.multiple_of(c * (rows // 2), align), rows // 2)


def _run_now(name, carry):
    c_in, c_out = len(carry.operands), len(carry.out_shapes)

    def body(*refs):
        ins, outs, sems = refs[:c_in], refs[c_in:c_in + c_out], refs[c_in + c_out:]
        carry.start(ins, outs, sems)
        carry.finish(ins, outs, sems)

    return pl.pallas_call(body, name=name, in_specs=[ANY] * c_in, out_specs=[ANY] * c_out,
                          out_shape=list(carry.out_shapes), scratch_shapes=list(carry.sems),
                          input_output_aliases=dict(carry.aliases))(*_in_hbm(carry.operands))


def _gather_group(shards, split):
    n = len(shards)

    def copies(srcs, outs, sems):
        send_sems, recv_sems = sems
        x, y, c = _mesh_pos()
        k0 = 2 * x + y
        sib = (x, y, 1 - c)
        chips = _other_chips(x, y)

        def remote(src, dst, j, i, to):
            return pltpu.make_async_remote_copy(src_ref=src, dst_ref=dst, send_sem=send_sems.at[j, i],
                                                recv_sem=recv_sems.at[j, i], device_id=to, device_id_type=MESH)

        def rows(ref, i, core, *lead):
            if not split[i]:
                return ref.at[lead] if lead else ref
            return ref.at[(*lead, _half_rows(core, shards[i].shape[0], 16))]

        own = [remote(s, o.at[k0], 6, i, sib) for i, (s, o) in enumerate(zip(srcs, outs))]
        ici, landed, fwd, fwd_landed = [], [], [], []
        for j, chip in enumerate(chips):
            kj = 2 * chip[0] + chip[1]
            pairs = list(enumerate(zip(srcs, outs)))
            ici.append([remote(rows(s, i, c), rows(o, i, c, k0), j, i, (*chip, c)) for i, (s, o) in pairs])
            landed.append([remote(rows(s, i, c), rows(o, i, c, kj), j, i, (*chip, c)) for i, (s, o) in pairs])
            fwd.append([remote(rows(o, i, c, kj), rows(o, i, c, kj), 3 + j, i, sib) for i, (s, o) in pairs if split[i]])
            fwd_landed.append([remote(rows(o, i, 1 - c, kj), rows(o, i, 1 - c, kj), 3 + j, i, sib)
                               for i, (s, o) in pairs if split[i]])
        return own, ici, landed, fwd, fwd_landed

    def start(srcs, outs, sems):
        own, ici, _, _, _ = copies(srcs, outs, sems)
        for cp in own + [cp for per_chip in ici for cp in per_chip]:
            cp.start()

    def finish(srcs, outs, sems):
        own, ici, landed, fwd, fwd_landed = copies(srcs, outs, sems)
        passed = [i for i in range(n) if split[i]]
        for j in range(3):
            for i, cp in enumerate(landed[j]):
                cp.wait_recv()
                if split[i]:
                    fwd[j][passed.index(i)].start()
        for j in range(3):
            for cp in fwd_landed[j]:
                cp.wait_recv()
        for cp in own:
            cp.wait_recv()
        for cp in own + [cp for per_chip in ici + fwd for cp in per_chip]:
            cp.wait_send()

    return _Carried(shards, [_sds((NCHIP,) + s.shape, s.dtype) for s in shards],
                    [pltpu.SemaphoreType.DMA((7, n)), pltpu.SemaphoreType.DMA((7, n))], start, finish)


def _each_copy(copies, carried, out_shapes, sems, aliases=None):
    def start(ins, outs, sem_refs):
        for cp in copies(ins, outs, sem_refs):
            cp.start()

    def finish(ins, outs, sem_refs):
        for cp in copies(ins, outs, sem_refs):
            cp.wait()

    return _Carried(carried, out_shapes, sems, start, finish, aliases)


def _swap_group(grads):
    n = len(grads)

    def copies(srcs, outs, sems):
        send_sems, recv_sems = sems
        x, y, c = _mesh_pos()
        return [pltpu.make_async_remote_copy(src_ref=s.at[:, _half_rows(1 - c, s.shape[1], 8)], dst_ref=o,
                                             send_sem=send_sems.at[i], recv_sem=recv_sems.at[i], device_id=(x, y, 1 - c),
                                             device_id_type=MESH) for i, (s, o) in enumerate(zip(srcs, outs))]

    return _each_copy(copies, grads, [pltpu.HBM((NCHIP, g.shape[1] // 2, g.shape[2]), F32) for g in grads],
                      [pltpu.SemaphoreType.DMA((n,)), pltpu.SemaphoreType.DMA((n,))])


def _add_sibling_group(tag, kc_idx, grads, gots):
    n = len(grads)

    def body(kc_ref, *refs):
        for g, rx, p, pb in zip(refs[:n], refs[n:2 * n], refs[2 * n:3 * n], refs[3 * n:]):
            s = g[...] + rx[...]
            pb[...] = s.astype(BF)

            @pl.when(pl.program_id(0) == kc_ref[0])
            def _():
                p[...] = s

    halves = [pl.BlockSpec((None,) + rx.shape[1:], lambda k, kc_ref: (k, 0, 0)) for rx in gots]
    mine = [pl.BlockSpec((None,) + rx.shape[1:], lambda k, kc_ref: (k, kc_ref[1], 0)) for rx in gots]
    own = [pl.BlockSpec(rx.shape[1:], lambda k, kc_ref: (0, 0)) for rx in gots]
    outs = _pallas_call(
        body, name="add_sibling_" + tag,
        grid_spec=pltpu.PrefetchScalarGridSpec(num_scalar_prefetch=1, grid=(NCHIP,), in_specs=mine + halves,
                                               out_specs=own + halves),
        out_shape=[pltpu.HBM(rx.shape[1:], F32) for rx in gots] + [pltpu.HBM(rx.shape, BF) for rx in gots],
        compiler_params=_params(48),
    )(kc_idx, *_in_hbm(list(grads) + list(gots)))
    return outs[:n], outs[n:]


def _exchange_group(parts):
    n = len(parts)

    def copies(srcs, outs, sems):
        send_sems, recv_sems = sems
        x, y, c = _mesh_pos()
        return [pltpu.make_async_remote_copy(
            src_ref=s.at[2 * chip[0] + chip[1]], dst_ref=o.at[j], send_sem=send_sems.at[j, i],
            recv_sem=recv_sems.at[j, i], device_id=(*chip, c), device_id_type=MESH)
            for j, chip in enumerate(_other_chips(x, y)) for i, (s, o) in enumerate(zip(srcs, outs))]

    return _each_copy(copies, parts, [pltpu.HBM((3,) + p.shape[1:], BF) for p in parts],
                      [pltpu.SemaphoreType.DMA((3, n)), pltpu.SemaphoreType.DMA((3, n))])


def _add_chips_group(tag, kc_idx, parts, arrived):
    n = len(parts)

    def body(kc_ref, *refs):
        for p, rx, t in zip(refs[:n], refs[n:2 * n], refs[2 * n:]):
            t[...] = ((p[...] + rx[0].astype(F32)) + rx[1].astype(F32)) + rx[2].astype(F32)

    outs = _pallas_call(
        body, name="add_chips_" + tag,
        grid_spec=pltpu.PrefetchScalarGridSpec(
            num_scalar_prefetch=1, grid=(1,),
            in_specs=([pl.BlockSpec(rx.shape[1:], lambda i, kc_ref: (0, 0)) for rx in arrived]
                      + [pl.BlockSpec(rx.shape, lambda i, kc_ref: (0, 0, 0)) for rx in arrived]),
            out_specs=[pl.BlockSpec((None,) + rx.shape[1:], lambda i, kc_ref: (kc_ref[1], 0, 0)) for rx in arrived]),
        out_shape=[pltpu.HBM((2,) + rx.shape[1:], F32) for rx in arrived],
        compiler_params=_params(48),
    )(kc_idx, *_in_hbm(list(parts) + list(arrived)))
    return list(outs)


def _join_group(halves):
    n = len(halves)

    def copies(bufs, sems):
        send_sems, recv_sems = sems
        x, y, c = _mesh_pos()
        sib = (x, y, 1 - c)
        sends = [pltpu.make_async_remote_copy(src_ref=b.at[c], dst_ref=b.at[c], send_sem=send_sems.at[i],
                                              recv_sem=recv_sems.at[i], device_id=sib, device_id_type=MESH)
                 for i, b in enumerate(bufs)]
        landed = [pltpu.make_async_remote_copy(src_ref=b.at[c], dst_ref=b.at[1 - c], send_sem=send_sems.at[i],
                                               recv_sem=recv_sems.at[i], device_id=sib, device_id_type=MESH)
                  for i, b in enumerate(bufs)]
        return sends, landed

    def start(_, bufs, sems):
        for cp in copies(bufs, sems)[0]:
            cp.start()

    def finish(_, bufs, sems):
        sends, landed = copies(bufs, sems)
        for cp in landed:
            cp.wait_recv()
        for cp in sends:
            cp.wait_send()

    return _Carried(halves, [pltpu.HBM(h.shape, F32) for h in halves],
                    [pltpu.SemaphoreType.DMA((n,)), pltpu.SemaphoreType.DMA((n,))], start, finish,
                    {i: i for i in range(n)})


def _combine(carries):
    operands, out_shapes, sems, aliases, spans = [], [], [], {}, []
    for c in carries:
        aliases.update({len(operands) + i: len(out_shapes) + o for i, o in c.aliases.items()})
        spans.append((len(operands), len(out_shapes), len(sems)))
        operands += list(c.operands)
        out_shapes += list(c.out_shapes)
        sems += list(c.sems)

    def each(phase):
        def run(ins, outs, sem_refs):
            for c, (a, b, s) in zip(carries, spans):
                getattr(c, phase)(ins[a:a + len(c.operands)], outs[b:b + len(c.out_shapes)], sem_refs[s:s + len(c.sems)])
        return run

    return _Carried(operands, out_shapes, sems, each("start"), each("finish"), aliases)


def _allreduce_small(arrays, wire):
    n = len(arrays)
    halves = [(a.shape[0], a.shape[1] // 2) for a in arrays]

    def body(*refs):
        srcs, outs = refs[:n], refs[n:2 * n]
        mine_bufs, sib_bufs, chip_bufs, total_bufs = (refs[k * n:(k + 1) * n] for k in range(2, 6))
        send_sems, recv_sems, local_sems = refs[6 * n:]
        x, y, c = _mesh_pos()
        k0 = 2 * x + y
        sib = (x, y, 1 - c)

        def remote(src, dst, j, i, to):
            return pltpu.make_async_remote_copy(src_ref=src, dst_ref=dst, send_sem=send_sems.at[j, i],
                                                recv_sem=recv_sems.at[j, i], device_id=to, device_id_type=MESH)

        def cols(ref, i, core):
            return ref.at[:, pl.ds(pl.multiple_of(core * halves[i][1], LANE), halves[i][1])]

        swaps = [remote(cols(s, i, 1 - c), b, 0, i, sib) for i, (s, b) in enumerate(zip(srcs, sib_bufs))]
        own = [pltpu.make_async_copy(cols(s, i, c), m, local_sems.at[i]) for i, (s, m) in enumerate(zip(srcs, mine_bufs))]
        for cp in swaps + own:
            cp.start()
        for cp in swaps + own:
            cp.wait()
        for m, b, buf in zip(mine_bufs, sib_bufs, chip_bufs):
            buf[k0] = (m[...] + b[...]).astype(buf.dtype)
        chips = _other_chips(x, y)
        sends = [remote(buf.at[k0], buf.at[k0], 1 + j, i, (*chip, c))
                 for j, chip in enumerate(chips) for i, buf in enumerate(chip_bufs)]
        for cp in sends:
            cp.start()
        for j, chip in enumerate(chips):
            for i, buf in enumerate(chip_bufs):
                remote(buf.at[k0], buf.at[2 * chip[0] + chip[1]], 1 + j, i, (*chip, c)).wait_recv()
        for cp in sends:
            cp.wait_send()
        for t, buf in zip(total_bufs, chip_bufs):
            t[...] = ((buf[0].astype(F32) + buf[1].astype(F32)) + buf[2].astype(F32)) + buf[3].astype(F32)
        joins = [remote(t, cols(o, i, c), 4, i, sib) for i, (t, o) in enumerate(zip(total_bufs, outs))]
        keep = [pltpu.make_async_copy(t, cols(o, i, c), local_sems.at[i]) for i, (t, o) in enumerate(zip(total_bufs, outs))]
        for cp in joins + keep:
            cp.start()
        for i, (t, o) in enumerate(zip(total_bufs, outs)):
            remote(t, cols(o, i, 1 - c), 4, i, sib).wait_recv()
        for cp in joins:
            cp.wait_send()
        for cp in keep:
            cp.wait()

    specs = [_full(a.shape) for a in arrays]
    return _pallas_call(
        body, name="allreduce_small", grid=(1,), in_specs=specs, out_specs=specs,
        out_shape=[_sds(a.shape) for a in arrays],
        scratch_shapes=([pltpu.VMEM(h, F32) for h in halves] + [pltpu.VMEM(h, F32) for h in halves]
                        + [pltpu.VMEM((NCHIP,) + h, dt) for h, dt in zip(halves, wire)] + [pltpu.VMEM(h, F32) for h in halves]
                        + [pltpu.SemaphoreType.DMA((5, n)), pltpu.SemaphoreType.DMA((5, n)), pltpu.SemaphoreType.DMA((n,))]),
        compiler_params=_params(32),
    )(*arrays)


def _adamw_terms(w, g, m, v):
    m = ADAM_B1 * m + (1.0 - ADAM_B1) * g
    v = ADAM_B2 * v + (1.0 - ADAM_B2) * jnp.square(g)
    m_hat = m / (1.0 - ADAM_B1 ** ADAM_STEP)
    v_hat = v / (1.0 - ADAM_B2 ** ADAM_STEP)
    return -ADAM_LR * (m_hat / (jnp.sqrt(v_hat) + ADAM_EPS) + ADAM_WD * w), m, v


ADAM_STEPS = 4


def _adamw_group(tag, ws, gs, ms, vs):
    n = len(ws)

    def body(*refs):
        ins, outs = refs[:4 * n], refs[4 * n:]
        for i in range(n):
            w, g, m, v = (ins[k * n + i][...] for k in range(4))
            outs[i][...] = g
            outs[n + i][...], outs[2 * n + i][...], outs[3 * n + i][...] = _adamw_terms(w, g, m, v)

    specs = [pl.BlockSpec((w.shape[0] // ADAM_STEPS, w.shape[1]), lambda i: (i, 0)) for w in ws]
    outs = _pallas_call(
        body, name="adamw_" + tag, grid=(ADAM_STEPS,), in_specs=specs * 4, out_specs=specs * 4,
        out_shape=[_sds(w.shape) for w in ws] * 4, compiler_params=_params(48),
    )(*_in_hbm(list(ws) + list(gs) + list(ms) + list(vs)))
    return outs[:n], outs[n:2 * n], outs[2 * n:3 * n], outs[3 * n:]


def _adamw_replicated(sums, row_of, direct):
    ns, nr, nd = len(sums), len(row_of), len(direct)

    def body(*refs):
        sum_refs = refs[:ns]
        ins = refs[ns:ns + 3 * nr + 4 * nd]
        outs = refs[ns + 3 * nr + 4 * nd:]
        for i, (_, _, _, si, row) in enumerate(row_of):
            w_ref, m_ref, v_ref = ins[3 * i:3 * i + 3]
            g = sum_refs[si][row:row + 1, :]
            outs[4 * i][...] = g
            outs[4 * i + 1][...], outs[4 * i + 2][...], outs[4 * i + 3][...] = _adamw_terms(w_ref[...], g, m_ref[...], v_ref[...])
        for i in range(nd):
            w_ref, m_ref, v_ref, g_ref = ins[3 * nr + 4 * i:3 * nr + 4 * i + 4]
            o = outs[4 * (nr + i):4 * (nr + i) + 4]
            g = g_ref[...]
            o[0][...] = g
            o[1][...], o[2][...], o[3][...] = _adamw_terms(w_ref[...], g, m_ref[...], v_ref[...])

    operands = list(sums)
    shapes = []
    for w, m, v, _, _ in row_of:
        operands += [w, m, v]
        shapes += [w.shape] * 4
    for w, m, v, g in direct:
        operands += [w, m, v, g]
        shapes += [w.shape] * 4
    flat = _pallas_call(
        body, name="adamw_replicated", grid=(1,), in_specs=[_full(a.shape) for a in operands],
        out_specs=[_full(s) for s in shapes], out_shape=[_sds(s) for s in shapes],
        compiler_params=_params(56),
    )(*operands)
    return [flat[4 * i:4 * i + 4] for i in range(nr + nd)]


class _Exchanges:
    def __init__(self, shards, conv_w, chip, core, apply):
        self.shards, self.conv_w, self.apply = shards, conv_w, apply
        self.active, self.calls = [], 0
        self.core_idx = jnp.reshape(core, (1,)).astype(jnp.int32)
        self.chip_core_idx = jnp.stack([chip, core]).astype(jnp.int32)

    def first(self):
        names = ["w_in", "w_glu"]
        got = _run_now("gather_first", _gather_group([self.shards[n] for n in names] + [self.conv_w],
                                                     [True, True, False]))
        out = dict(zip(names, got))
        out["conv_w"] = jnp.transpose(got[2], (1, 0, 2)).reshape(4, LW)
        return out

    def gather(self, names):
        return _gather_group([self.shards[n] for n in names], [True] * len(names))

    def reduce(self, tag, grads):
        self.active.append({"tag": tag, "names": list(grads), "stage": 0, "grads": list(grads.values())})

    def run(self, call, hold=()):
        groups = [g for g in self.active if g["tag"] not in hold]
        carries = [self._exchange_of(g) for g in groups]
        carry = _combine(carries)
        outs = list(call(carry))
        own = len(outs) - len(carry.out_shapes)
        landed = outs[own:]
        for g, c in zip(groups, carries):
            self._sum_after(g, landed[:len(c.out_shapes)])
            landed = landed[len(c.out_shapes):]
        self.active = [g for g in self.active if g["stage"] < 3]
        return outs[:own]

    def _exchange_of(self, g):
        if g["stage"] == 0:
            return _swap_group(g["grads"])
        if g["stage"] == 1:
            return _exchange_group(g["bf16"])
        return _join_group(g["halves"])

    def _sum_after(self, g, landed):
        if g["stage"] == 0:
            g["f32"], g["bf16"] = _add_sibling_group(g["tag"], self.chip_core_idx, g["grads"], landed)
        elif g["stage"] == 1:
            g["halves"] = _add_chips_group(g["tag"], self.chip_core_idx, g["f32"], landed)
        else:
            self.apply(g["tag"], g["names"], [t.reshape(2 * t.shape[1], t.shape[2]) for t in landed])
        g["stage"] += 1

    def drain(self):
        while self.active:
            self.calls += 1
            self.run(lambda carry: _run_now("reduce_%d" % self.calls, carry))


INPUT_NAMES = (["x", "p"] + [n for n in
               ["g_mix", "w_in", "b_in", "lam_re", "lam_im", "log_dt", "s5_b_re", "s5_b_im", "s5_c_re", "s5_c_im", "s5_d",
                "w_glu", "b_glu", "conv_w", "conv_b", "w_r", "b_r", "w_i", "b_i", "lru_lambda", "w_a_out", "w_b_out", "w_o",
                "g_ffn", "w_ffn_gate", "w_ffn_up", "w_ffn_down", "g_ple_gate", "w_ple_gate", "b_ple_gate", "w_ple", "g_ple",
                "g_final"]])
WEIGHT_NAMES = INPUT_NAMES[2:]


def kernel(*args):
    names = INPUT_NAMES + ["loss_target"] + ["m_" + n for n in WEIGHT_NAMES] + ["v_" + n for n in WEIGHT_NAMES]
    assert len(args) == len(names)
    given = dict(zip(names, args))

    def view(name):
        a = given[name]
        return jnp.swapaxes(a, -1, -2) if name.endswith(TRANSPOSED) else a

    def unview(name, a):
        return jnp.swapaxes(a, -1, -2) if name in TRANSPOSED else a

    def local(name):
        return view(name) if name.endswith("g_final") else view(name)[0]

    xi, yi, ci = _mesh_pos()
    k0 = 2 * xi + yi
    x, p, tgt = given["x"][0], given["p"][0, 0], given["loss_target"][0]

    results = {}

    row_halves = {}

    def apply(tag, names, totals):
        totals = dict(zip(names, totals))
        row_halves.update({n: totals.pop(n) for n in names if n in ("w_in_lo", "w_in_hi")})
        if len(row_halves) == 2:
            totals["w_in"] = jnp.concatenate([row_halves.pop("w_in_lo"), row_halves.pop("w_in_hi")])
        names = list(totals)
        if not names:
            return
        new = _adamw_group(tag, [local(n) for n in names], list(totals.values()), [local("m_" + n) for n in names],
                           [local("v_" + n) for n in names])
        for kind, arrays in zip(("grad", "delta", "new_m", "new_v"), new):
            for n, arr in zip(names, arrays):
                results[kind, n] = unview(n, arr[None])

    comm = _Exchanges({n: local(n).astype(BF) for n, _ in SHARDED}, local("conv_w"), k0, ci, apply)
    w = {n: local(n) for n in WEIGHT_NAMES if n != "conv_w" and n not in dict(SHARDED)}
    gx, sums, blocks = _local_step(x, p, tgt, w, comm)

    sum_names, block_names = list(sums), list(blocks)
    red = _allreduce_small([sums[n] for n in sum_names] + [blocks[n] for n in block_names],
                           [F32] * len(sum_names) + [BF] * len(block_names))
    sums = dict(zip(sum_names, red[:len(sum_names)]))
    blocks = dict(zip(block_names, red[len(sum_names):]))
    loss = jnp.sum(sums[LOSS_ROW[0]][LOSS_ROW[1]])
    direct_g = _replicated_grads(w, sums, blocks)
    conv_rows = sums[CONV_W_ROWS[0]][CONV_W_ROWS[1]:CONV_W_ROWS[1] + 4]
    direct_g["conv_w"] = lax.dynamic_slice(conv_rows, (0, k0 * CONV_SHARD[1]), CONV_SHARD)
    as_row = lambda a: a.reshape(1, -1)
    row_names = list(ACC_ROWS)
    row_of = [(as_row(given[n]), as_row(given["m_" + n]), as_row(given["v_" + n]),
               sum_names.index(ACC_ROWS[n][0]), ACC_ROWS[n][1]) for n in row_names]
    direct_names = list(direct_g)
    direct = [(view(n), view("m_" + n), view("v_" + n), direct_g[n].reshape(view(n).shape)) for n in direct_names]
    done = _adamw_replicated([sums[n] for n in sum_names], row_of, direct)
    for n, four in zip(row_names + direct_names, done):
        for kind, arr in zip(("grad", "delta", "new_m", "new_v"), four):
            results[kind, n] = unview(n, arr).reshape(given[n].shape)

    out = [loss, gx[None]]
    for kind in ("grad", "delta", "new_m", "new_v"):
        out += [results[kind, n] for n in WEIGHT_NAMES]
    return tuple(out)
```

```python
import functools
import math

import jax
import jax.numpy as jnp
from jax import lax
from jax.experimental import pallas as pl
from jax.experimental.pallas import tpu as pltpu

F32 = jnp.float32
BF = jnp.bfloat16

D = 1024
S5W = 512
NG, NS, NP = 32, 64, 16
GN = NG * NS
LW = 1024
NH, HD = 16, 64
LRU_C = 8.0
FH = 2816
NCHIP = 4
FC = FH // NCHIP
PLE = 256
INC = S5W + LW + 2 * D
EPS = 1e-6
ADAM_LR, ADAM_B1, ADAM_B2, ADAM_EPS, ADAM_WD, ADAM_STEP = 0.001, 0.9, 0.999, 1e-08, 0.01, 10

TM = 256
TK = 1024
LC = 512
SUB = 8
VMEM_MB = 1024 * 1024
MESH = pl.DeviceIdType.MESH
ANY = pl.BlockSpec(memory_space=pl.ANY)


def _mm(a, b):
    return jnp.dot(a.astype(BF), b.astype(BF), preferred_element_type=F32)


def _mm_nt(a, b):
    return lax.dot_general(a.astype(BF), b.astype(BF), (((1,), (1,)), ((), ())), preferred_element_type=F32)


def _mm_tn(a, b):
    return lax.dot_general(a.astype(BF), b.astype(BF), (((0,), (0,)), ((), ())), preferred_element_type=F32)


def _blockdiag_mm(x, blocks_ref):
    n, rows, _ = blocks_ref.shape
    return jnp.concatenate([jnp.dot(x[:, j * rows:(j + 1) * rows], blocks_ref[j], preferred_element_type=F32)
                            for j in range(n)], axis=1)


def _blockdiag_mm_t(x, blocks_ref):
    n, _, wide = blocks_ref.shape
    return jnp.concatenate([lax.dot_general(x[:, j * wide:(j + 1) * wide], blocks_ref[j], (((1,), (1,)), ((), ())),
                                            preferred_element_type=F32) for j in range(n)], axis=1)


def _rms(x):
    r = lax.rsqrt(jnp.mean(x * x, axis=-1, keepdims=True) + EPS)
    return x * r, r


def _rms_bwd(dy, xh, r, g):
    dxh = dy * g
    return r * (dxh - xh * jnp.mean(dxh * xh, axis=-1, keepdims=True))


def _colsum(x):
    return jnp.sum(x, axis=0, keepdims=True)


def _sig(x):
    return jax.nn.sigmoid(x)


def _gelu_grad(x):
    c = math.sqrt(2.0 / math.pi)
    t = jnp.tanh(c * (x + 0.044715 * x * x * x))
    return 0.5 * (1.0 + t) + 0.5 * x * (1.0 - t * t) * c * (1.0 + 3.0 * 0.044715 * x * x)


def _neg_expm1(x):
    series = -x * (1.0 + x * (0.5 + x * (1.0 / 6.0 + x * (1.0 / 24.0))))
    return jnp.where(x > -0.03, series, 1.0 - jnp.exp(x))


def _tok(width):
    return pl.BlockSpec((TM, width), lambda i: (i, 0))


def _tok_rev(width, nt):
    return pl.BlockSpec((TM, width), lambda i: (nt - 1 - i, 0))


def _full(shape):
    return pl.BlockSpec(shape, lambda i: (0,) * len(shape))


def _params(vmem_mb, **kw):
    return pltpu.CompilerParams(dimension_semantics=("arbitrary",), vmem_limit_bytes=vmem_mb * VMEM_MB, **kw)


def _sds(shape, dtype=F32):
    return jax.ShapeDtypeStruct(shape, dtype)


class _Carried:
    def __init__(self, operands, out_shapes, sems, start, finish, aliases=None):
        self.operands, self.out_shapes, self.sems = list(operands), list(out_shapes), list(sems)
        self.start, self.finish, self.aliases = start, finish, dict(aliases or {})


def _in_hbm(arrays):
    return [pltpu.with_memory_space_constraint(a, pltpu.HBM) for a in arrays]


def _pallas_call(body, carry=None, **kw):
    if carry is None:
        return pl.pallas_call(body, **kw)

    def at_step(corner):
        hit = [pl.program_id(d) == (size - 1 if corner else 0) for d, size in enumerate(kw["grid"])]
        return functools.reduce(jnp.logical_and, hit)

    name, grid, compiler_params = kw["name"], kw["grid"], kw["compiler_params"]
    in_specs, out_specs, out_shape = list(kw["in_specs"]), list(kw["out_specs"]), list(kw["out_shape"])
    scratch_shapes = list(kw.get("scratch_shapes", ()))
    n_in, n_out, n_scr = len(in_specs), len(out_specs), len(scratch_shapes)
    c_in, c_out = len(carry.operands), len(carry.out_shapes)

    def full_body(*refs):
        ins, refs = refs[:n_in], refs[n_in:]
        c_ins, refs = refs[:c_in], refs[c_in:]
        outs, refs = refs[:n_out], refs[n_out:]
        c_outs, refs = refs[:c_out], refs[c_out:]
        scratch, c_sems = refs[:n_scr], refs[n_scr:]

        @pl.when(at_step(0))
        def _():
            carry.start(c_ins, c_outs, c_sems)

        body(*ins, *outs, *scratch)

        @pl.when(at_step(1))
        def _():
            carry.finish(c_ins, c_outs, c_sems)

    call = pl.pallas_call(
        full_body, name=name, grid=grid, in_specs=in_specs + [ANY] * c_in, out_specs=out_specs + [ANY] * c_out,
        out_shape=out_shape + list(carry.out_shapes), scratch_shapes=scratch_shapes + list(carry.sems),
        input_output_aliases={n_in + i: n_out + o for i, o in carry.aliases.items()},
        compiler_params=compiler_params)
    return lambda *operands: call(*operands, *_in_hbm(carry.operands))


def _resident(pairs, sems):
    first = pl.program_id(0) == 0
    copies = [pltpu.make_async_copy(src, dst, sems.at[j]) for j, (src, dst) in enumerate(pairs)]

    @pl.when(first)
    def _():
        for cp in copies:
            cp.start()

    def wait(j):
        @pl.when(first)
        def _():
            copies[j].wait()

    return wait


def _resident_now(pairs, sems):
    @pl.when(pl.program_id(0) == 0)
    def _():
        copies = [pltpu.make_async_copy(src, dst, sems.at[j]) for j, (src, dst) in enumerate(pairs)]
        for cp in copies:
            cp.start()
        for cp in copies:
            cp.wait()


def _row_iota(width):
    return lax.broadcasted_iota(jnp.int32, (SUB, width), 0)


def _bcast_row(x, row):
    return jnp.broadcast_to(x[row:row + 1, :], x.shape)


def _slab(k):
    return pl.ds(pl.multiple_of(k * SUB, SUB), SUB)


QC = INC // NCHIP
Z_PARTS = ((0, S5W), (S5W, S5W + LW), (S5W + LW, INC))


def _inproj_fwd(x, g_mix, w_in, b_in, carry=None):
    L = x.shape[0]

    def body(x_ref, g_ref, w_hbm, b_ref, h_ref, ua_ref, ub_ref, gp_ref, w_vm, w_sems):
        _resident_now([(w_hbm.at[k], w_vm.at[k]) for k in range(NCHIP)], w_sems)
        xh, _ = _rms(x_ref[...])
        h = (xh * g_ref[...]).astype(BF)
        h_ref[...] = h
        for k in range(NCHIP):
            lo, hi = k * QC, (k + 1) * QC
            z = jnp.dot(h, w_vm[k], preferred_element_type=F32) + b_ref[:, lo:hi]
            for ref, (a, b) in zip((ua_ref, ub_ref, gp_ref), Z_PARTS):
                s, e = max(lo, a), min(hi, b)
                if s < e:
                    ref[:, s - a:e - a] = z[:, s - lo:e - lo]

    return _pallas_call(
        body, carry, name="inproj_fwd", grid=(L // TM,),
        in_specs=[_tok(D), _full((1, D)), ANY, _full((1, INC))],
        out_specs=[_tok(D), _tok(S5W), _tok(LW), _tok(2 * D)],
        out_shape=[_sds((L, D), BF), _sds((L, S5W)), _sds((L, LW)), _sds((L, 2 * D))],
        scratch_shapes=[pltpu.VMEM((NCHIP, D, QC), BF), pltpu.SemaphoreType.DMA((NCHIP,))],
        compiler_params=_params(32),
    )(x, g_mix, w_in, b_in)


def _inproj_bwd(x, dx1, dua, dub, dgp, g_mix, w_in, carry=None):
    L = x.shape[0]

    def body(x_ref, dx1_ref, dua_ref, dub_ref, dgp_ref, g_ref, w_hbm, gx_ref, dz_ref, dg_ref, db_ref, w_vm, w_sems):
        _resident_now([(w_hbm.at[k], w_vm.at[k]) for k in range(NCHIP)], w_sems)

        @pl.when(pl.program_id(0) == 0)
        def _():
            dg_ref[...] = jnp.zeros_like(dg_ref)
            db_ref[...] = jnp.zeros_like(db_ref)

        for src, (a, b) in zip((dua_ref, dub_ref, dgp_ref), Z_PARTS):
            d = src[...]
            dz_ref[:, a:b] = d.astype(BF)
            db_ref[0:1, a:b] += _colsum(d)
        dh = jnp.zeros((TM, D), F32)
        for k in range(NCHIP):
            dh = dh + lax.dot_general(dz_ref[:, k * QC:(k + 1) * QC], w_vm[k], (((1,), (1,)), ((), ())),
                                      preferred_element_type=F32)
        xh, r = _rms(x_ref[...])
        dg_ref[0:1, :] += _colsum(dh * xh)
        gx_ref[...] = dx1_ref[...] + _rms_bwd(dh, xh, r, g_ref[...])

    return _pallas_call(
        body, carry, name="inproj_bwd", grid=(L // TM,),
        in_specs=[_tok(D), _tok(D), _tok(S5W), _tok(LW), _tok(2 * D), _full((1, D)), ANY],
        out_specs=[_tok(D), _tok(INC), _full((SUB, D)), _full((SUB, INC))],
        out_shape=[_sds((L, D)), _sds((L, INC), BF), _sds((SUB, D)), _sds((SUB, INC))],
        scratch_shapes=[pltpu.VMEM((NCHIP, D, QC), BF), pltpu.SemaphoreType.DMA((NCHIP,))],
        compiler_params=_params(32),
    )(x, dx1, dua, dub, dgp, g_mix, w_in)


def _cscan(xr_ref, xi_ref, con_ref, cr_ref, ci_ref, reverse):
    n_slab = xr_ref.shape[0] // SUB
    width = xr_ref.shape[1]
    for lc in range(width // LC):
        cols = slice(lc * LC, (lc + 1) * LC)
        con = [con_ref[SUB * j:SUB * (j + 1), cols] for j in range(8)]

        def step(k, carry, cols=cols, con=con):
            cr, ci = carry
            rows = _slab(n_slab - 1 - k if reverse else k)
            xr, xi = xr_ref[rows, cols], xi_ref[rows, cols]
            for j, sh in enumerate((1, 2, 4)):
                mr, mi = con[2 * j], con[2 * j + 1]
                pr = pltpu.roll(xr, SUB - sh if reverse else sh, 0)
                pi = pltpu.roll(xi, SUB - sh if reverse else sh, 0)
                xr, xi = xr + mr * pr - mi * pi, xi + mr * pi + mi * pr
            xr, xi = xr + con[6] * cr - con[7] * ci, xi + con[6] * ci + con[7] * cr
            xr_ref[rows, cols] = xr
            xi_ref[rows, cols] = xi
            row = 0 if reverse else SUB - 1
            return _bcast_row(xr, row), _bcast_row(xi, row)

        cr, ci = lax.fori_loop(0, n_slab, step, (cr_ref[:, cols], ci_ref[:, cols]))
        cr_ref[:, cols] = cr
        ci_ref[:, cols] = ci


def _s5_fwd(ua, bbr, bbi, ccr, cci, dsk, con, w_glu, b_glu, carry=None):
    L = ua.shape[0]

    def body(ua_ref, bbr_hbm, bbi_hbm, ccr_hbm, cci_hbm, dsk_ref, con_ref, wg_ref, bg_ref,
             sr_ref, si_ref, y_ref, zg_ref, ya_ref, bbr_vm, bbi_vm, ccr_vm, cci_vm, cr_ref, ci_ref, w_sems):
        landed = _resident([(bbr_hbm, bbr_vm), (bbi_hbm, bbi_vm), (ccr_hbm, ccr_vm), (cci_hbm, cci_vm)], w_sems)

        @pl.when(pl.program_id(0) == 0)
        def _():
            cr_ref[...] = jnp.zeros_like(cr_ref)
            ci_ref[...] = jnp.zeros_like(ci_ref)

        u = ua_ref[...]
        ub = u.astype(BF)
        landed(0)
        sr_ref[...] = _blockdiag_mm(ub, bbr_vm)
        landed(1)
        si_ref[...] = _blockdiag_mm(ub, bbi_vm)
        _cscan(sr_ref, si_ref, con_ref, cr_ref, ci_ref, reverse=False)
        landed(2)
        landed(3)
        y = (_blockdiag_mm_t(sr_ref[...].astype(BF), ccr_vm) - _blockdiag_mm_t(si_ref[...].astype(BF), cci_vm)
             + dsk_ref[...] * u)
        y_ref[...] = y
        zg = jax.nn.gelu(y)
        zg_ref[...] = zg.astype(BF)
        q = _mm(zg, wg_ref[...]) + bg_ref[...]
        ya_ref[...] = (zg * _sig(q)).astype(BF)

    return _pallas_call(
        body, carry, name="s5_fwd", grid=(L // TM,),
        in_specs=[_tok(S5W), ANY, ANY, ANY, ANY, _full((1, S5W)), _full((8 * SUB, GN)),
                  _full((S5W, S5W)), _full((1, S5W))],
        out_specs=[_tok(GN), _tok(GN), _tok(S5W), _tok(S5W), _tok(S5W)],
        out_shape=[_sds((L, GN)), _sds((L, GN)), _sds((L, S5W)), _sds((L, S5W), BF), _sds((L, S5W), BF)],
        scratch_shapes=[pltpu.VMEM((S5W // 128, 128, GN // (S5W // 128)), BF)] * 4 + [
                        pltpu.VMEM((SUB, GN), F32), pltpu.VMEM((SUB, GN), F32),
                        pltpu.SemaphoreType.DMA((4,))],
        compiler_params=_params(20),
    )(ua, bbr, bbi, ccr, cci, dsk, con, w_glu, b_glu)


def _s5_bwd(dya, y, ua, sr, si, bbr, bbi, ccr, cci, dsk, con_rev, w_glu, b_glu, carry=None):
    L = ua.shape[0]
    nt = L // TM
    spt = TM // SUB
    n_slab = spt

    def halo_map(i):
        return (jnp.maximum((nt - 1 - i) * spt - 1, 0), 0)

    def body(dya_ref, y_ref, ua_ref, sr_ref, si_ref, hr_ref, hi_ref, bbr_hbm, bbi_hbm, ccr_hbm, cci_hbm,
             dsk_ref, con_ref, wg_ref, bg_ref,
             dua_ref, dq_ref, dy_ref, lr_ref, li_ref, da_ref, dsm_ref,
             bbr_vm, bbi_vm, ccr_vm, cci_vm, cr_ref, ci_ref, w_sems):
        i = pl.program_id(0)
        landed = _resident([(ccr_hbm, ccr_vm), (cci_hbm, cci_vm), (bbr_hbm, bbr_vm), (bbi_hbm, bbi_vm)], w_sems)

        @pl.when(i == 0)
        def _():
            cr_ref[...] = jnp.zeros_like(cr_ref)
            ci_ref[...] = jnp.zeros_like(ci_ref)
            da_ref[...] = jnp.zeros_like(da_ref)
            dsm_ref[...] = jnp.zeros_like(dsm_ref)

        u = ua_ref[...]
        yv = y_ref[...]
        dya = dya_ref[...]
        zg = jax.nn.gelu(yv)
        sg = _sig(_mm(zg, wg_ref[...]) + bg_ref[...])
        dq = dya * zg * sg * (1.0 - sg)
        dq_ref[...] = dq.astype(BF)
        dzg = dya * sg + _mm_nt(dq, wg_ref[...])
        dy = dzg * _gelu_grad(yv)
        dyb = dy.astype(BF)
        dy_ref[...] = dyb
        dsm_ref[0:1, :] += _colsum(dy * u)
        dsm_ref[1:2, :] += _colsum(dq)
        landed(0)
        lr_ref[...] = _blockdiag_mm(dyb, ccr_vm)
        landed(1)
        li_ref[...] = -_blockdiag_mm(dyb, cci_vm)
        _cscan(lr_ref, li_ref, con_ref, cr_ref, ci_ref, reverse=True)

        first_tile = (i == nt - 1)
        row = _row_iota(LC)
        for lc in range(GN // LC):
            cols = slice(lc * LC, (lc + 1) * LC)
            h_r = jnp.where(first_tile, 0.0, hr_ref[:, cols])
            h_i = jnp.where(first_tile, 0.0, hi_ref[:, cols])

            def step(k, acc, cols=cols, h_r=h_r, h_i=h_i):
                ar, ai = acc
                rows = _slab(k)
                prev = _slab(jnp.maximum(k - 1, 0))
                pr = jnp.where(k == 0, h_r, sr_ref[prev, cols])
                pi = jnp.where(k == 0, h_i, si_ref[prev, cols])
                spr = pltpu.roll(jnp.where(row == SUB - 1, pr, sr_ref[rows, cols]), 1, 0)
                spi = pltpu.roll(jnp.where(row == SUB - 1, pi, si_ref[rows, cols]), 1, 0)
                lr, li = lr_ref[rows, cols], li_ref[rows, cols]
                return ar + lr * spr + li * spi, ai + li * spr - lr * spi

            zero = jnp.zeros((SUB, LC), F32)
            ar, ai = lax.fori_loop(0, n_slab, step, (zero, zero))
            da_ref[0:1, cols] += _colsum(ar)
            da_ref[1:2, cols] += _colsum(ai)

        landed(2)
        landed(3)
        dua_ref[...] = (dy * dsk_ref[...] + _blockdiag_mm_t(lr_ref[...].astype(BF), bbr_vm)
                        + _blockdiag_mm_t(li_ref[...].astype(BF), bbi_vm))

    return _pallas_call(
        body, carry, name="s5_bwd", grid=(nt,),
        in_specs=[_tok_rev(S5W, nt), _tok_rev(S5W, nt), _tok_rev(S5W, nt), _tok_rev(GN, nt), _tok_rev(GN, nt),
                  pl.BlockSpec((SUB, GN), halo_map), pl.BlockSpec((SUB, GN), halo_map),
                  ANY, ANY, ANY, ANY, _full((1, S5W)), _full((8 * SUB, GN)), _full((S5W, S5W)), _full((1, S5W))],
        out_specs=[_tok_rev(S5W, nt), _tok_rev(S5W, nt), _tok_rev(S5W, nt), _tok_rev(GN, nt), _tok_rev(GN, nt),
                   _full((SUB, GN)), _full((SUB, S5W))],
        out_shape=[_sds((L, S5W)), _sds((L, S5W), BF), _sds((L, S5W), BF), _sds((L, GN)), _sds((L, GN)),
                   _sds((SUB, GN)), _sds((SUB, S5W))],
        scratch_shapes=[pltpu.VMEM((S5W // 128, 128, GN // (S5W // 128)), BF)] * 4 + [
                        pltpu.VMEM((SUB, GN), F32), pltpu.VMEM((SUB, GN), F32),
                        pltpu.SemaphoreType.DMA((4,))],
        compiler_params=_params(38),
    )(dya, y, ua, sr, si, sr, si, bbr, bbi, ccr, cci, dsk, con_rev, w_glu, b_glu)


def _lru_gate_terms(rg, sp):
    log_a = -LRU_C * rg * sp
    a = jnp.exp(log_a)
    mult = jnp.sqrt(_neg_expm1(2.0 * log_a))
    return a, mult


def _lru_fwd(ub, conv_w, conv_b, wr, wi, b_r, b_i, sp, carry=None):
    L = ub.shape[0]
    n_slab = TM // SUB

    def body(ub_ref, cw_ref, cb_ref, wr_ref, wi_ref, br_ref, bi_ref, sp_ref,
             xc_ref, rg_ref, ig_ref, h_ref, hp_ref, a_ref, halo_ref, carry_ref):
        @pl.when(pl.program_id(0) == 0)
        def _():
            halo_ref[...] = jnp.zeros_like(halo_ref)
            carry_ref[...] = jnp.zeros_like(carry_ref)

        row = _row_iota(LW)
        taps = [cw_ref[k:k + 1, :] for k in range(4)]
        cb = cb_ref[...]

        def conv_step(k, prev):
            rows = _slab(k)
            cur = ub_ref[rows, :]
            acc = taps[3] * cur + cb
            for j in (1, 2, 3):
                acc = acc + taps[3 - j] * pltpu.roll(jnp.where(row >= SUB - j, prev, cur), j, 0)
            xc_ref[rows, :] = acc
            return cur

        halo_ref[...] = lax.fori_loop(0, n_slab, conv_step, halo_ref[...])

        xc = xc_ref[...]
        xcb = xc.astype(BF)
        rg = _sig(_blockdiag_mm(xcb, wr_ref) + br_ref[...])
        ig = _sig(_blockdiag_mm(xcb, wi_ref) + bi_ref[...])
        rg_ref[...] = rg
        ig_ref[...] = ig
        a, mult = _lru_gate_terms(rg, sp_ref[...])
        a_ref[...] = a
        h_ref[...] = mult * ig * xc

        rowc = _row_iota(LC)
        for lc in range(LW // LC):
            cols = slice(lc * LC, (lc + 1) * LC)

            def step(k, c, cols=cols):
                rows = _slab(k)
                av, b = a_ref[rows, cols], h_ref[rows, cols]
                for sh in (1, 2, 4):
                    keep = rowc >= sh
                    b = b + av * jnp.where(keep, pltpu.roll(b, sh, 0), 0.0)
                    av = av * jnp.where(keep, pltpu.roll(av, sh, 0), 1.0)
                h = b + av * c
                h_ref[rows, cols] = h
                hp_ref[rows, cols] = jnp.where(rowc == 0, c, pltpu.roll(h, 1, 0))
                return _bcast_row(h, SUB - 1)

            carry_ref[:, cols] = lax.fori_loop(0, n_slab, step, carry_ref[:, cols])

    return _pallas_call(
        body, carry, name="lru_fwd", grid=(L // TM,),
        in_specs=[_tok(LW), _full((4, LW)), _full((1, LW)), _full((LW // 128, 128, 128)), _full((LW // 128, 128, 128)),
                  _full((1, LW)), _full((1, LW)), _full((1, LW))],
        out_specs=[_tok(LW)] * 5,
        out_shape=[_sds((L, LW))] * 5,
        scratch_shapes=[pltpu.VMEM((TM, LW), F32), pltpu.VMEM((SUB, LW), F32), pltpu.VMEM((SUB, LW), F32)],
        compiler_params=_params(32),
    )(ub, conv_w, conv_b, wr, wi, b_r, b_i, sp)


def _lru_bwd(dyb, xc, rg, ig, hp, ub, conv_w, wr, wi, sp, dsp, carry=None):
    L = ub.shape[0]
    nt = L // TM
    spt = TM // SUB
    n_slab = spt

    def halo_map(i):
        return (jnp.maximum((nt - 1 - i) * spt - 1, 0), 0)

    def body(dh_ref, xc_ref, rg_ref, ig_ref, hp_ref, ub_ref, uh_ref, cw_ref, wr_ref, wi_ref, sp_ref, dsp_ref,
             dub_ref, dpr_ref, dpi_ref, acc_ref, a_ref, lam_ref, dxc_ref, carry_ref, next_ref):
        i = pl.program_id(0)

        @pl.when(i == 0)
        def _():
            carry_ref[...] = jnp.zeros_like(carry_ref)
            next_ref[...] = jnp.zeros_like(next_ref)
            acc_ref[...] = jnp.zeros_like(acc_ref)

        sp = sp_ref[...]
        rg, ig, xc = rg_ref[...], ig_ref[...], xc_ref[...]
        a, mult = _lru_gate_terms(rg, sp)
        a_ref[...] = a

        rowc = _row_iota(LC)
        for lc in range(LW // LC):
            cols = slice(lc * LC, (lc + 1) * LC)

            def step(k, c, cols=cols):
                rows = _slab(n_slab - 1 - k)
                av, dh = a_ref[rows, cols], dh_ref[rows, cols]
                b = av * dh
                for sh in (1, 2, 4):
                    keep = rowc < SUB - sh
                    b = b + av * jnp.where(keep, pltpu.roll(b, SUB - sh, 0), 0.0)
                    av = av * jnp.where(keep, pltpu.roll(av, SUB - sh, 0), 1.0)
                mu = b + av * c
                lam_ref[rows, cols] = dh + jnp.where(rowc == SUB - 1, c, pltpu.roll(mu, SUB - 1, 0))
                return _bcast_row(mu, 0)

            carry_ref[:, cols] = lax.fori_loop(0, n_slab, step, carry_ref[:, cols])

        lam = lam_ref[...]
        d_a = lam * hp_ref[...]
        d_mult = lam * ig * xc
        d_ig = lam * mult * xc
        dxc = lam * mult * ig
        d_log_a = d_a * a - d_mult * a * a / mult
        d_rg = (-LRU_C) * sp * d_log_a
        acc_ref[0:1, :] += _colsum((-LRU_C) * rg * d_log_a) * dsp_ref[...]
        dpr = d_rg * rg * (1.0 - rg)
        dpi = d_ig * ig * (1.0 - ig)
        acc_ref[1:2, :] += _colsum(dpr)
        acc_ref[2:3, :] += _colsum(dpi)
        dprb, dpib = dpr.astype(BF), dpi.astype(BF)
        dpr_ref[...] = dprb
        dpi_ref[...] = dpib
        dxc = dxc + _blockdiag_mm_t(dprb, wr_ref) + _blockdiag_mm_t(dpib, wi_ref)
        dxc_ref[...] = dxc
        acc_ref[3:4, :] += _colsum(dxc)

        row = _row_iota(LW)
        taps = [cw_ref[k:k + 1, :] for k in range(4)]
        u_halo = jnp.where(i == nt - 1, 0.0, uh_ref[...])
        nxt_tile = next_ref[...]

        def conv_step(k, accs):
            rows = _slab(k)
            cur = dxc_ref[rows, :]
            nxt = jnp.where(k == n_slab - 1, nxt_tile, dxc_ref[_slab(jnp.minimum(k + 1, n_slab - 1)), :])
            ucur = ub_ref[rows, :]
            uprev = jnp.where(k == 0, u_halo, ub_ref[_slab(jnp.maximum(k - 1, 0)), :])
            du = taps[3] * cur
            new = [accs[3] + cur * ucur]
            for j in (1, 2, 3):
                du = du + taps[3 - j] * pltpu.roll(jnp.where(row < j, nxt, cur), SUB - j, 0)
                new.append(accs[3 - j] + cur * pltpu.roll(jnp.where(row >= SUB - j, uprev, ucur), j, 0))
            dub_ref[rows, :] = du
            return tuple(new[::-1])

        zero = jnp.zeros((SUB, LW), F32)
        accs = lax.fori_loop(0, n_slab, conv_step, (zero, zero, zero, zero))
        for k in range(4):
            acc_ref[4 + k:5 + k, :] += _colsum(accs[k])
        next_ref[...] = dxc_ref[0:SUB, :]

    return _pallas_call(
        body, carry, name="lru_bwd", grid=(nt,),
        in_specs=[_tok_rev(LW, nt)] * 6 + [pl.BlockSpec((SUB, LW), halo_map), _full((4, LW)),
                                           _full((LW // 128, 128, 128)), _full((LW // 128, 128, 128)), _full((1, LW)), _full((1, LW))],
        out_specs=[_tok_rev(LW, nt), _tok_rev(LW, nt), _tok_rev(LW, nt), _full((SUB, LW))],
        out_shape=[_sds((L, LW)), _sds((L, LW), BF), _sds((L, LW), BF), _sds((SUB, LW))],
        scratch_shapes=[pltpu.VMEM((TM, LW), F32), pltpu.VMEM((TM, LW), F32), pltpu.VMEM((TM, LW), F32),
                        pltpu.VMEM((SUB, LW), F32), pltpu.VMEM((SUB, LW), F32)],
        compiler_params=_params(32),
    )(dyb, xc, rg, ig, hp, ub, ub, conv_w, wr, wi, sp, dsp)


AC = D // NCHIP


def _merge_fwd(x, ya, yb, gp, w_a, w_b, w_o, carry=None):
    L = x.shape[0]

    def body(x_ref, ya_ref, yb_ref, gp_ref, wa_ref, wb_ref, wo_ref, x1_ref, pa_ref, pb_ref, mg_ref):
        ya = ya_ref[...]
        for k in range(NCHIP):
            pa_ref[:, k * AC:(k + 1) * AC] = jnp.dot(ya, wa_ref[k], preferred_element_type=F32)
        pb = _mm(yb_ref[...], wb_ref[...])
        pb_ref[...] = pb
        gp = gp_ref[...]
        merged = (_sig(gp[:, :D]) * pa_ref[...] + _sig(gp[:, D:]) * pb).astype(BF)
        mg_ref[...] = merged
        x1_ref[...] = x_ref[...] + jnp.dot(merged, wo_ref[...], preferred_element_type=F32)

    return _pallas_call(
        body, carry, name="merge_fwd", grid=(L // TM,),
        in_specs=[_tok(D), _tok(S5W), _tok(LW), _tok(2 * D), _full((NCHIP, S5W, AC)), _full((LW, D)), _full((D, D))],
        out_specs=[_tok(D), _tok(D), _tok(D), _tok(D)],
        out_shape=[_sds((L, D)), _sds((L, D)), _sds((L, D)), _sds((L, D), BF)],
        compiler_params=_params(32),
    )(x, ya, yb, gp, w_a, w_b, w_o)


def _merge_bwd(dx1, gp, pa, pb, w_a, w_b, w_o, carry=None):
    L = dx1.shape[0]

    def body(dx1_ref, gp_ref, pa_ref, pb_ref, wa_ref, wb_ref, wo_ref, dya_ref, dyb_ref, dgp_ref, dpa_ref, dpb_ref):
        dm = _mm_nt(dx1_ref[...], wo_ref[...])
        gp = gp_ref[...]
        sa, sb = _sig(gp[:, :D]), _sig(gp[:, D:])
        dpa = (dm * sa).astype(BF)
        dpb = (dm * sb).astype(BF)
        dpa_ref[...] = dpa
        dpb_ref[...] = dpb
        dgp_ref[:, :D] = dm * pa_ref[...] * sa * (1.0 - sa)
        dgp_ref[:, D:] = dm * pb_ref[...] * sb * (1.0 - sb)
        dya = jnp.zeros((TM, S5W), F32)
        for k in range(NCHIP):
            dya = dya + _mm_nt(dpa[:, k * AC:(k + 1) * AC], wa_ref[k])
        dya_ref[...] = dya
        dyb_ref[...] = _mm_nt(dpb, wb_ref[...])

    return _pallas_call(
        body, carry, name="merge_bwd", grid=(L // TM,),
        in_specs=[_tok(D), _tok(2 * D), _tok(D), _tok(D), _full((NCHIP, S5W, AC)), _full((LW, D)), _full((D, D))],
        out_specs=[_tok(S5W), _tok(LW), _tok(2 * D), _tok(D), _tok(D)],
        out_shape=[_sds((L, S5W)), _sds((L, LW)), _sds((L, 2 * D)), _sds((L, D), BF), _sds((L, D), BF)],
        compiler_params=_params(32),
    )(dx1, gp, pa, pb, w_a, w_b, w_o)


def _chunk_tok(width):
    return pl.BlockSpec((NCHIP, TM, width), lambda i: (0, i, 0))


def _ffn_fwd(x1, g_ffn, wg, wu, wd, carry=None):
    L = x1.shape[0]

    def body(x_ref, g_ref, wg_hbm, wu_hbm, wd_hbm, x2_ref, h2_ref, gg_ref, uu_ref, wg_vm, wu_vm, wd_vm, w_sems):
        _resident_now([(src.at[c], dst.at[c]) for c in range(NCHIP)
                       for src, dst in ((wg_hbm, wg_vm), (wu_hbm, wu_vm), (wd_hbm, wd_vm))], w_sems)
        x = x_ref[...]
        xh, _ = _rms(x)
        h2 = (xh * g_ref[...]).astype(BF)
        h2_ref[...] = h2
        out = x
        for c in range(NCHIP):
            gg = lax.dot_general(h2, wg_vm[c], (((1,), (1,)), ((), ())), preferred_element_type=F32)
            uu = lax.dot_general(h2, wu_vm[c], (((1,), (1,)), ((), ())), preferred_element_type=F32)
            gg_ref[c] = gg.astype(BF)
            uu_ref[c] = uu.astype(BF)
            act = (gg * _sig(gg) * uu).astype(BF)
            out = out + jnp.dot(act, wd_vm[c], preferred_element_type=F32)
        x2_ref[...] = out

    return _pallas_call(
        body, carry, name="ffn_fwd", grid=(L // TM,),
        in_specs=[_tok(D), _full((1, D)), ANY, ANY, ANY],
        out_specs=[_tok(D), _tok(D), _chunk_tok(FC), _chunk_tok(FC)],
        out_shape=[_sds((L, D)), _sds((L, D), BF), _sds((NCHIP, L, FC), BF), _sds((NCHIP, L, FC), BF)],
        scratch_shapes=[pltpu.VMEM((NCHIP, FC, D), BF)] * 3 + [pltpu.SemaphoreType.DMA((3 * NCHIP,))],
        compiler_params=_params(38),
    )(x1, g_ffn, wg, wu, wd)


def _ffn_bwd(x1, dx2, gg, uu, g_ffn, wg, wu, wd, carry=None):
    L = x1.shape[0]

    def body(x_ref, dx2_ref, gg_ref, uu_ref, g_ref, wg_hbm, wu_hbm, wd_hbm,
             dx1_ref, act_ref, dgg_ref, duu_ref, dg_ref, wg_vm, wu_vm, wd_vm, w_sems):
        _resident_now([(src.at[c], dst.at[c]) for c in range(NCHIP)
                       for src, dst in ((wg_hbm, wg_vm), (wu_hbm, wu_vm), (wd_hbm, wd_vm))], w_sems)

        @pl.when(pl.program_id(0) == 0)
        def _():
            dg_ref[...] = jnp.zeros_like(dg_ref)

        dx2 = dx2_ref[...]
        dx2b = dx2.astype(BF)
        dh2 = jnp.zeros((TM, D), F32)
        for c in range(NCHIP):
            g = gg_ref[c].astype(F32)
            u = uu_ref[c].astype(F32)
            s = _sig(g)
            silu = g * s
            act_ref[c] = (silu * u).astype(BF)
            dact = lax.dot_general(dx2b, wd_vm[c], (((1,), (1,)), ((), ())), preferred_element_type=F32)
            dg = (dact * u * s * (1.0 + g * (1.0 - s))).astype(BF)
            du = (dact * silu).astype(BF)
            dgg_ref[c] = dg
            duu_ref[c] = du
            dh2 = dh2 + jnp.dot(dg, wg_vm[c], preferred_element_type=F32)
            dh2 = dh2 + jnp.dot(du, wu_vm[c], preferred_element_type=F32)
        xh, r = _rms(x_ref[...])
        dg_ref[0:1, :] += _colsum(dh2 * xh)
        dx1_ref[...] = dx2 + _rms_bwd(dh2, xh, r, g_ref[...])

    return _pallas_call(
        body, carry, name="ffn_bwd", grid=(L // TM,),
        in_specs=[_tok(D), _tok(D), _chunk_tok(FC), _chunk_tok(FC), _full((1, D)), ANY, ANY, ANY],
        out_specs=[_tok(D), _chunk_tok(FC), _chunk_tok(FC), _chunk_tok(FC), _full((SUB, D))],
        out_shape=[_sds((L, D)), _sds((NCHIP, L, FC), BF), _sds((NCHIP, L, FC), BF), _sds((NCHIP, L, FC), BF),
                   _sds((SUB, D))],
        scratch_shapes=[pltpu.VMEM((NCHIP, FC, D), BF)] * 3 + [pltpu.SemaphoreType.DMA((3 * NCHIP,))],
        compiler_params=_params(50),
    )(x1, dx2, gg, uu, g_ffn, wg, wu, wd)


def _ple_loss(x2, p, tgt, g_pg, w_pg, b_pg, w_ple, g_ple, g_final):
    L = x2.shape[0]

    def body(x2_ref, p_ref, t_ref, gpg_ref, wpg_ref, bpg_ref, wple_ref, gple_ref, gf_ref,
             dx2_ref, n2_ref, dpre_ref, de0_ref, acc_ref):
        @pl.when(pl.program_id(0) == 0)
        def _():
            acc_ref[...] = jnp.zeros_like(acc_ref)

        x2 = x2_ref[...]
        x2h, r2 = _rms(x2)
        n2 = (x2h * gpg_ref[...]).astype(BF)
        n2_ref[...] = n2
        gate = _sig(jnp.dot(n2, wpg_ref[...], preferred_element_type=F32) + bpg_ref[...])
        pb = p_ref[...].astype(BF)
        e0 = jnp.concatenate([jnp.dot(pb, wple_ref[k], preferred_element_type=F32) for k in range(NCHIP)], axis=1)
        e0h, re = _rms(e0)
        e = e0h * gple_ref[...]
        x3 = x2 + gate * e
        x3h, r3 = _rms(x3)
        diff = x3h * gf_ref[...] - t_ref[...]
        acc_ref[4:5, :] += _colsum(diff * diff) * (0.5 / D)
        dy = diff * (1.0 / D)
        acc_ref[3:4, :] += _colsum(dy * x3h)
        dx3 = _rms_bwd(dy, x3h, r3, gf_ref[...])
        de = dx3 * gate
        acc_ref[2:3, :] += _colsum(de * e0h)
        de0_ref[...] = _rms_bwd(de, e0h, re, gple_ref[...]).astype(BF)
        dpre = dx3 * e * gate * (1.0 - gate)
        acc_ref[1:2, :] += _colsum(dpre)
        dpreb = dpre.astype(BF)
        dpre_ref[...] = dpreb
        dn2 = lax.dot_general(dpreb, wpg_ref[...], (((1,), (1,)), ((), ())), preferred_element_type=F32)
        acc_ref[0:1, :] += _colsum(dn2 * x2h)
        dx2_ref[...] = dx3 + _rms_bwd(dn2, x2h, r2, gpg_ref[...])

    return _pallas_call(
        body, name="ple_loss", grid=(L // TM,),
        in_specs=[_tok(D), _tok(PLE), _tok(D), _full((1, D)), _full((D, D)), _full((1, D)), _full((NCHIP, PLE, AC)),
                  _full((1, D)), _full((1, D))],
        out_specs=[_tok(D), _tok(D), _tok(D), _tok(D), _full((SUB, D))],
        out_shape=[_sds((L, D)), _sds((L, D), BF), _sds((L, D), BF), _sds((L, D), BF), _sds((SUB, D))],
        compiler_params=_params(32),
    )(x2, p, tgt, g_pg, w_pg, b_pg, w_ple, g_ple, g_final)


def _tn(name, a, b, col_chunk=None, a_block=None, carry=None):
    L = a.shape[-2]
    m, n = a.shape[-1], b.shape[-1]
    a_col = 0
    if a_block is not None:
        a_col, m = a_block
    tk = L if (a.ndim == 3 or b.ndim == 3 or a_block is not None) else TK
    if a.ndim == 3 or b.ndim == 3:
        nj, bn = (a if a.ndim == 3 else b).shape[0], n
        a_spec = (pl.BlockSpec((None, tk, m), lambda j, t: (j, t, 0)) if a.ndim == 3
                  else pl.BlockSpec((tk, m), lambda j, t: (t, 0)))
        b_spec = (pl.BlockSpec((None, tk, n), lambda j, t: (j, t, 0)) if b.ndim == 3
                  else pl.BlockSpec((tk, n), lambda j, t: (t, 0)))
        out_spec, out_shape = pl.BlockSpec((None, m, n), lambda j, t: (j, 0, 0)), _sds((nj, m, n))
    else:
        bn = col_chunk
        if bn is None:
            bn = next((cand for cand in (1024, 512) if n > cand and n % cand == 0), n)
        nj = n // bn
        a_spec = pl.BlockSpec((tk, m), lambda j, t: (t, a_col))
        b_spec = pl.BlockSpec((tk, bn), lambda j, t: (t, j))
        if col_chunk is None:
            out_spec, out_shape = pl.BlockSpec((m, bn), lambda j, t: (0, j)), _sds((m, n))
        else:
            out_spec, out_shape = pl.BlockSpec((None, m, bn), lambda j, t: (j, 0, 0)), _sds((nj, m, bn))

    def body(a_ref, b_ref, o_ref):
        if tk == L:
            o_ref[...] = _mm_tn(a_ref[...], b_ref[...])
        else:
            @pl.when(pl.program_id(1) == 0)
            def _():
                o_ref[...] = jnp.zeros_like(o_ref)

            o_ref[...] += _mm_tn(a_ref[...], b_ref[...])

    outs = _pallas_call(
        body, carry, name=name, grid=(nj, L // tk), in_specs=[a_spec, b_spec], out_specs=[out_spec],
        out_shape=[pltpu.HBM(out_shape.shape, out_shape.dtype)],
        compiler_params=pltpu.CompilerParams(dimension_semantics=("arbitrary", "arbitrary"),
                                             vmem_limit_bytes=(30 if tk == L else 12) * VMEM_MB),
    )(a, b)
    return outs[0] if carry is None else outs


LANE = 128


def _tn_blocks(name, a, bs, ga, gb, carry=None):
    L, m, n, nb = a.shape[0], a.shape[1], bs[0].shape[1], len(bs)
    per = LANE // ga
    wb = per * gb
    n_super = m // LANE

    def body(a_ref, *refs):
        b_refs, o_refs, acc_refs = refs[:nb], refs[nb:2 * nb], refs[2 * nb:]
        t = pl.program_id(0)

        @pl.when(t == 0)
        def _():
            for acc in acc_refs:
                acc[...] = jnp.zeros_like(acc)

        lhs = a_ref[...].astype(BF)
        for b_ref, acc in zip(b_refs, acc_refs):
            rhs = b_ref[...].astype(BF)
            for j in range(n_super):
                acc[j] += _mm_tn(lhs[:, j * LANE:(j + 1) * LANE], rhs[:, j * wb:(j + 1) * wb])

        @pl.when(t == L // TK - 1)
        def _():
            own = (lax.broadcasted_iota(jnp.int32, (LANE, wb), 0) // ga) == (lax.broadcasted_iota(jnp.int32, (LANE, wb), 1) // gb)
            for o_ref, acc in zip(o_refs, acc_refs):
                for j in range(n_super):
                    kept = jnp.where(own, acc[j], 0.0)
                    o_ref[:, j * wb:(j + 1) * wb] = jnp.sum(kept.reshape(per, ga, wb), axis=0)

    outs = _pallas_call(
        body, carry, name=name, grid=(L // TK,),
        in_specs=[pl.BlockSpec((TK, m), lambda t: (t, 0))] + [pl.BlockSpec((TK, n), lambda t: (t, 0))] * nb,
        out_specs=[_full((ga, n))] * nb, out_shape=[_sds((ga, n))] * nb,
        scratch_shapes=[pltpu.VMEM((n_super, LANE, wb), F32)] * nb,
        compiler_params=_params(46),
    )(*_in_hbm([a] + list(bs)))
    return list(outs)


def _s5_discretize(lam_re, lam_im, log_dt, b_re, b_im):
    dt = jnp.exp(log_dt)[:, None]
    mag = jnp.exp(lam_re * dt)
    ar = mag * jnp.cos(lam_im * dt)
    ai = mag * jnp.sin(lam_im * dt)
    den = lam_re * lam_re + lam_im * lam_im
    nr = ar - 1.0
    fr = (nr * lam_re + ai * lam_im) / den
    fi = (ai * lam_re - nr * lam_im) / den
    bbr = fr[:, None, :] * b_re - fi[:, None, :] * b_im
    bbi = fr[:, None, :] * b_im + fi[:, None, :] * b_re
    return ar, ai, bbr, bbi


def _prepare(by_rows, block_cols, ar, ai):
    n = len(by_rows)

    def body(*refs):
        srcs, (ar_ref, ai_ref), dense, (con_ref, rev_ref) = refs[:n], refs[n:n + 2], refs[n + 2:2 * n + 2], refs[2 * n + 2:]
        for src, out, c in zip(srcs, dense, block_cols):
            r = src.shape[0]
            per = LANE // r
            wide = per * c
            own = (lax.broadcasted_iota(jnp.int32, (LANE, wide), 0) // r) == (lax.broadcasted_iota(jnp.int32, (LANE, wide), 1) // c)
            for j in range(out.shape[0]):
                tiled = jnp.broadcast_to(src[:, j * wide:(j + 1) * wide][None], (per, r, wide)).reshape(LANE, wide)
                out[j] = jnp.where(own, tiled, 0.0).astype(BF)
        a_r, a_i = ar_ref[...], ai_ref[...]
        pw = [(jnp.ones_like(a_r), jnp.zeros_like(a_i))]
        for _ in range(SUB):
            pr, pi = pw[-1]
            pw.append((pr * a_r - pi * a_i, pr * a_i + pi * a_r))
        row = _row_iota(GN)
        for ref, reverse in ((con_ref, False), (rev_ref, True)):
            sign = -1.0 if reverse else 1.0
            for j, sh in enumerate((1, 2, 4)):
                keep = (row < SUB - sh) if reverse else (row >= sh)
                ref[2 * j * SUB:(2 * j + 1) * SUB, :] = jnp.where(keep, pw[sh][0], 0.0)
                ref[(2 * j + 1) * SUB:(2 * j + 2) * SUB, :] = jnp.where(keep, sign * pw[sh][1], 0.0)
            p_r, p_i = jnp.zeros((SUB, GN), F32), jnp.zeros((SUB, GN), F32)
            for i in range(SUB):
                k = SUB - i if reverse else i + 1
                p_r = jnp.where(row == i, pw[k][0], p_r)
                p_i = jnp.where(row == i, sign * pw[k][1], p_i)
            ref[6 * SUB:7 * SUB, :] = p_r
            ref[7 * SUB:8 * SUB, :] = p_i

    dense_shapes = [(b.shape[1] // (LANE // b.shape[0] * c), LANE, LANE // b.shape[0] * c)
                    for b, c in zip(by_rows, block_cols)]
    outs = _pallas_call(
        body, name="prepare", grid=(1,), in_specs=[_full(b.shape) for b in by_rows] + [_full((1, GN))] * 2,
        out_specs=[_full(s) for s in dense_shapes] + [_full((8 * SUB, GN))] * 2,
        out_shape=[_sds(s, BF) for s in dense_shapes] + [_sds((8 * SUB, GN))] * 2,
        compiler_params=_params(24),
    )(*by_rows, ar, ai)
    return outs[:n], outs[n], outs[n + 1]


def _local_step(x, p, tgt, w, comm):
    rows_of = lambda a: a.reshape(NCHIP * a.shape[1], a.shape[2])
    quarters = lambda a: a.reshape(NCHIP, a.shape[0] // NCHIP, a.shape[1])

    def gathering(names, call):
        carry = comm.gather(names)
        outs = list(call(carry))
        own = len(outs) - len(carry.out_shapes)
        w.update(zip(names, outs[own:]))
        return outs[:own]

    w.update(comm.first())
    w_glu = rows_of(w["w_glu"])
    ar, ai, bbr, bbi = _s5_discretize(w["lam_re"], w["lam_im"], w["log_dt"], w["s5_b_re"], w["s5_b_im"])
    by_row = lambda b: jnp.transpose(b, (1, 0, 2)).reshape(b.shape[1], -1)
    (bbr_d, bbi_d, ccr_d, cci_d, wr_d, wi_d), con, con_rev = _prepare(
        [by_row(b) for b in (bbr, bbi, w["s5_c_re"], w["s5_c_im"], w["w_r"], w["w_i"])], [NS] * 4 + [HD] * 2,
        ar.reshape(1, GN), ai.reshape(1, GN))
    dsk = w["s5_d"].reshape(1, S5W)
    lam = w["lru_lambda"].reshape(1, LW)
    sp = jax.nn.softplus(-lam)
    b_r, b_i = w["b_r"].reshape(1, LW), w["b_i"].reshape(1, LW)
    row = lambda name: w[name].reshape(1, -1)

    h, ua, ub, gp = gathering(["w_a_out", "w_b_out"], lambda carry: _inproj_fwd(
        x, row("g_mix"), w["w_in"], row("b_in"), carry))
    sr, si, y, zg, ya = gathering(["w_o", "w_ffn_gate"], lambda carry: _s5_fwd(
        ua, bbr_d, bbi_d, ccr_d, cci_d, dsk, con, w_glu, row("b_glu"), carry))
    xc, rg, ig, yb, hp = gathering(["w_ffn_up"], lambda carry: _lru_fwd(
        ub, w["conv_w"], row("conv_b"), wr_d, wi_d, b_r, b_i, sp, carry))
    w_b_out, w_o = rows_of(w["w_b_out"]), rows_of(w["w_o"])
    x1, pa, pb, merged = gathering(["w_ffn_down"], lambda carry: _merge_fwd(
        x, ya, yb, gp, w["w_a_out"], w_b_out, w_o, carry))
    x2, h2, gg, uu = gathering(["w_ple_gate", "w_ple"], lambda carry: _ffn_fwd(
        x1, row("g_ffn"), w["w_ffn_gate"], w["w_ffn_up"], w["w_ffn_down"], carry))
    w_pg = rows_of(w["w_ple_gate"])
    dx2, n2, dpre, de0, acc_p = _ple_loss(x2, p, tgt, row("g_ple_gate"), w_pg, row("b_ple_gate"),
                                          w["w_ple"], row("g_ple"), row("g_final"))
    comm.reduce("ple", {"w_ple_gate": quarters(_tn("dw_ple_gate", n2, dpre)),
                        "w_ple": _tn("dw_ple", p, de0, col_chunk=AC)})
    dx1, act, dgg, duu, acc_f = comm.run(lambda carry: _ffn_bwd(
        x1, dx2, gg, uu, row("g_ffn"), w["w_ffn_gate"], w["w_ffn_up"], w["w_ffn_down"], carry))
    comm.reduce("ffn_gate", {"w_ffn_gate": _tn("dw_ffn_gate", dgg, h2)})
    comm.reduce("w_o", {"w_o": quarters(_tn("dw_o", merged, dx1))})
    comm.reduce("ffn_up", {"w_ffn_up": comm.run(lambda carry: _tn("dw_ffn_up", duu, h2, carry=carry))[0]})
    comm.reduce("ffn_down", {"w_ffn_down": comm.run(lambda carry: _tn("dw_ffn_down", act, dx2, carry=carry),
                                                    hold=("ffn_gate", "w_o"))[0]})
    dya, dyb, dgp, dpa, dpb = comm.run(lambda carry: _merge_bwd(
        dx1, gp, pa, pb, w["w_a_out"], w_b_out, w_o, carry), hold=("ffn_gate", "ffn_up"))
    comm.reduce("merge", {"w_a_out": _tn("dw_a_out", ya, dpa, col_chunk=AC), "w_b_out": quarters(_tn("dw_b_out", yb, dpb))})
    dua, dq, dy, lr, li, acc_a, acc_s = comm.run(lambda carry: _s5_bwd(
        dya, y, ua, sr, si, bbr_d, bbi_d, ccr_d, cci_d, dsk, con_rev, w_glu, row("b_glu"), carry), hold=("ffn_down",))
    dub, dpr, dpi, acc_l = comm.run(lambda carry: _lru_bwd(
        dyb, xc, rg, ig, hp, ub, w["conv_w"], wr_d, wi_d, sp, -_sig(-lam), carry))
    gx, dz, acc_g, acc_b = _inproj_bwd(x, dx1, dua, dub, dgp, row("g_mix"), w["w_in"])
    half = (D // 2,)
    comm.reduce("in_lo", {"w_in_lo": comm.run(lambda carry: _tn(
        "dw_in_lo", h, dz, col_chunk=QC, a_block=(0,) + half, carry=carry))[0]})
    comm.reduce("in_hi", {"w_in_hi": comm.run(lambda carry: _tn(
        "dw_in_hi", h, dz, col_chunk=QC, a_block=(1,) + half, carry=carry))[0], "w_glu": quarters(_tn("dw_glu", zg, dq))})
    d_wr, d_wi = comm.run(lambda carry: _tn_blocks("dw_r_i", xc, [dpr, dpi], HD, HD, carry))
    d_bbr, d_bbi = comm.run(lambda carry: _tn_blocks("d_bb", ua, [lr, li], NP, NS, carry))
    d_ccr, d_cci = comm.run(lambda carry: _tn_blocks("d_cc", dy, [sr, si], NP, NS, carry))
    comm.drain()
    sums = {"ple": acc_p, "ffn": acc_f, "mix": acc_g, "b_in": acc_b, "lru": acc_l, "s5": acc_s, "s5_a": acc_a}
    blocks = {"bb_re": d_bbr, "bb_im": d_bbi,
              "cc_re": d_ccr, "cc_im": d_cci,
              "w_r": d_wr, "w_i": d_wi}
    return gx, sums, blocks


def _replicated_grads(w, sums, blocks):
    grouped = lambda e, groups: jnp.transpose(e.reshape(e.shape[0], groups, -1), (1, 0, 2))
    d_ar, d_ai = sums["s5_a"][0].reshape(NG, NS), sums["s5_a"][1].reshape(NG, NS)
    d_bbr, d_bbi = grouped(blocks["bb_re"], NG), grouped(blocks["bb_im"], NG)
    _, vjp = jax.vjp(_s5_discretize, w["lam_re"], w["lam_im"], w["log_dt"], w["s5_b_re"], w["s5_b_im"])
    g = dict(zip(("lam_re", "lam_im", "log_dt", "s5_b_re", "s5_b_im"), vjp((d_ar, d_ai, d_bbr, d_bbi))))
    g["s5_c_re"] = grouped(blocks["cc_re"], NG)
    g["s5_c_im"] = -grouped(blocks["cc_im"], NG)
    g["w_r"], g["w_i"] = grouped(blocks["w_r"], NH), grouped(blocks["w_i"], NH)
    g["s5_d"] = sums["s5"][0].reshape(NG, NP)
    g["b_r"] = sums["lru"][1].reshape(NH, HD)
    g["b_i"] = sums["lru"][2].reshape(NH, HD)
    return g


ACC_ROWS = {"g_mix": ("mix", 0), "b_in": ("b_in", 0), "g_ffn": ("ffn", 0), "g_ple_gate": ("ple", 0),
            "b_ple_gate": ("ple", 1), "g_ple": ("ple", 2), "g_final": ("ple", 3), "b_glu": ("s5", 1),
            "lru_lambda": ("lru", 0), "conv_b": ("lru", 3)}
LOSS_ROW = ("ple", 4)
CONV_W_ROWS = ("lru", 4)


SHARDED = [("w_in", (D, QC)), ("w_glu", (S5W // NCHIP, S5W)), ("w_a_out", (S5W, AC)), ("w_b_out", (LW // NCHIP, D)),
           ("w_o", (D // NCHIP, D)), ("w_ffn_gate", (FC, D)), ("w_ffn_up", (FC, D)), ("w_ffn_down", (FC, D)),
           ("w_ple_gate", (D // NCHIP, D)), ("w_ple", (PLE, AC))]
NSH = len(SHARDED)
TRANSPOSED = ("w_ffn_gate", "w_ffn_up", "s5_b_re", "s5_b_im")
CONV_SHARD = (4, LW // NCHIP)


def _mesh_pos():
    return lax.axis_index("x"), lax.axis_index("y"), lax.axis_index("c")


def _other_chips(x, y):
    return [(1 - x, y), (x, 1 - y), (1 - x, 1 - y)]


def _half_rows(c, rows, align):
    return pl.ds(pl.multiple_of(c * (rows // 2), align), rows // 2)


def _run_now(name, carry):
    c_in, c_out = len(carry.operands), len(carry.out_shapes)

    def body(*refs):
        ins, outs, sems = refs[:c_in], refs[c_in:c_in + c_out], refs[c_in + c_out:]
        carry.start(ins, outs, sems)
        carry.finish(ins, outs, sems)

    return pl.pallas_call(body, name=name, in_specs=[ANY] * c_in, out_specs=[ANY] * c_out,
                          out_shape=list(carry.out_shapes), scratch_shapes=list(carry.sems),
                          input_output_aliases=dict(carry.aliases))(*_in_hbm(carry.operands))


def _gather_group(shards, split):
    n = len(shards)

    def copies(srcs, outs, sems):
        send_sems, recv_sems = sems
        x, y, c = _mesh_pos()
        k0 = 2 * x + y
        sib = (x, y, 1 - c)
        chips = _other_chips(x, y)

        def remote(src, dst, j, i, to):
            return pltpu.make_async_remote_copy(src_ref=src, dst_ref=dst, send_sem=send_sems.at[j, i],
                                                recv_sem=recv_sems.at[j, i], device_id=to, device_id_type=MESH)

        def rows(ref, i, core, *lead):
            if not split[i]:
                return ref.at[lead] if lead else ref
            return ref.at[(*lead, _half_rows(core, shards[i].shape[0], 16))]

        own = [remote(s, o.at[k0], 6, i, sib) for i, (s, o) in enumerate(zip(srcs, outs))]
        ici, landed, fwd, fwd_landed = [], [], [], []
        for j, chip in enumerate(chips):
            kj = 2 * chip[0] + chip[1]
            pairs = list(enumerate(zip(srcs, outs)))
            ici.append([remote(rows(s, i, c), rows(o, i, c, k0), j, i, (*chip, c)) for i, (s, o) in pairs])
            landed.append([remote(rows(s, i, c), rows(o, i, c, kj), j, i, (*chip, c)) for i, (s, o) in pairs])
            fwd.append([remote(rows(o, i, c, kj), rows(o, i, c, kj), 3 + j, i, sib) for i, (s, o) in pairs if split[i]])
            fwd_landed.append([remote(rows(o, i, 1 - c, kj), rows(o, i, 1 - c, kj), 3 + j, i, sib)
                               for i, (s, o) in pairs if split[i]])
        return own, ici, landed, fwd, fwd_landed

    def start(srcs, outs, sems):
        own, ici, _, _, _ = copies(srcs, outs, sems)
        for cp in own + [cp for per_chip in ici for cp in per_chip]:
            cp.start()

    def finish(srcs, outs, sems):
        own, ici, landed, fwd, fwd_landed = copies(srcs, outs, sems)
        passed = [i for i in range(n) if split[i]]
        for j in range(3):
            for i, cp in enumerate(landed[j]):
                cp.wait_recv()
                if split[i]:
                    fwd[j][passed.index(i)].start()
        for j in range(3):
            for cp in fwd_landed[j]:
                cp.wait_recv()
        for cp in own:
            cp.wait_recv()
        for cp in own + [cp for per_chip in ici + fwd for cp in per_chip]:
            cp.wait_send()

    return _Carried(shards, [_sds((NCHIP,) + s.shape, s.dtype) for s in shards],
                    [pltpu.SemaphoreType.DMA((7, n)), pltpu.SemaphoreType.DMA((7, n))], start, finish)


def _each_copy(copies, carried, out_shapes, sems, aliases=None):
    def start(ins, outs, sem_refs):
        for cp in copies(ins, outs, sem_refs):
            cp.start()

    def finish(ins, outs, sem_refs):
        for cp in copies(ins, outs, sem_refs):
            cp.wait()

    return _Carried(carried, out_shapes, sems, start, finish, aliases)


def _swap_group(grads):
    n = len(grads)

    def copies(srcs, outs, sems):
        send_sems, recv_sems = sems
        x, y, c = _mesh_pos()
        return [pltpu.make_async_remote_copy(src_ref=s.at[:, _half_rows(1 - c, s.shape[1], 8)], dst_ref=o,
                                             send_sem=send_sems.at[i], recv_sem=recv_sems.at[i], device_id=(x, y, 1 - c),
                                             device_id_type=MESH) for i, (s, o) in enumerate(zip(srcs, outs))]

    return _each_copy(copies, grads, [pltpu.HBM((NCHIP, g.shape[1] // 2, g.shape[2]), F32) for g in grads],
                      [pltpu.SemaphoreType.DMA((n,)), pltpu.SemaphoreType.DMA((n,))])


def _add_sibling_group(tag, kc_idx, grads, gots):
    n = len(grads)

    def body(kc_ref, *refs):
        for g, rx, p, pb in zip(refs[:n], refs[n:2 * n], refs[2 * n:3 * n], refs[3 * n:]):
            s = g[...] + rx[...]
            pb[...] = s.astype(BF)

            @pl.when(pl.program_id(0) == kc_ref[0])
            def _():
                p[...] = s

    halves = [pl.BlockSpec((None,) + rx.shape[1:], lambda k, kc_ref: (k, 0, 0)) for rx in gots]
    mine = [pl.BlockSpec((None,) + rx.shape[1:], lambda k, kc_ref: (k, kc_ref[1], 0)) for rx in gots]
    own = [pl.BlockSpec(rx.shape[1:], lambda k, kc_ref: (0, 0)) for rx in gots]
    outs = _pallas_call(
        body, name="add_sibling_" + tag,
        grid_spec=pltpu.PrefetchScalarGridSpec(num_scalar_prefetch=1, grid=(NCHIP,), in_specs=mine + halves,
                                               out_specs=own + halves),
        out_shape=[pltpu.HBM(rx.shape[1:], F32) for rx in gots] + [pltpu.HBM(rx.shape, BF) for rx in gots],
        compiler_params=_params(24),
    )(kc_idx, *_in_hbm(list(grads) + list(gots)))
    return outs[:n], outs[n:]


def _exchange_group(parts):
    n = len(parts)

    def copies(srcs, outs, sems):
        send_sems, recv_sems = sems
        x, y, c = _mesh_pos()
        return [pltpu.make_async_remote_copy(
            src_ref=s.at[2 * chip[0] + chip[1]], dst_ref=o.at[j], send_sem=send_sems.at[j, i],
            recv_sem=recv_sems.at[j, i], device_id=(*chip, c), device_id_type=MESH)
            for j, chip in enumerate(_other_chips(x, y)) for i, (s, o) in enumerate(zip(srcs, outs))]

    return _each_copy(copies, parts, [pltpu.HBM((3,) + p.shape[1:], BF) for p in parts],
                      [pltpu.SemaphoreType.DMA((3, n)), pltpu.SemaphoreType.DMA((3, n))])


def _add_chips_group(tag, kc_idx, parts, arrived):
    n = len(parts)

    def body(kc_ref, *refs):
        for p, rx, t in zip(refs[:n], refs[n:2 * n], refs[2 * n:]):
            t[...] = ((p[...] + rx[0].astype(F32)) + rx[1].astype(F32)) + rx[2].astype(F32)

    outs = _pallas_call(
        body, name="add_chips_" + tag,
        grid_spec=pltpu.PrefetchScalarGridSpec(
            num_scalar_prefetch=1, grid=(1,),
            in_specs=([pl.BlockSpec(rx.shape[1:], lambda i, kc_ref: (0, 0)) for rx in arrived]
                      + [pl.BlockSpec(rx.shape, lambda i, kc_ref: (0, 0, 0)) for rx in arrived]),
            out_specs=[pl.BlockSpec((None,) + rx.shape[1:], lambda i, kc_ref: (kc_ref[1], 0, 0)) for rx in arrived]),
        out_shape=[pltpu.HBM((2,) + rx.shape[1:], F32) for rx in arrived],
        compiler_params=_params(24),
    )(kc_idx, *_in_hbm(list(parts) + list(arrived)))
    return list(outs)


def _join_group(halves):
    n = len(halves)

    def copies(bufs, sems):
        send_sems, recv_sems = sems
        x, y, c = _mesh_pos()
        sib = (x, y, 1 - c)
        sends = [pltpu.make_async_remote_copy(src_ref=b.at[c], dst_ref=b.at[c], send_sem=send_sems.at[i],
                                              recv_sem=recv_sems.at[i], device_id=sib, device_id_type=MESH)
                 for i, b in enumerate(bufs)]
        landed = [pltpu.make_async_remote_copy(src_ref=b.at[c], dst_ref=b.at[1 - c], send_sem=send_sems.at[i],
                                               recv_sem=recv_sems.at[i], device_id=sib, device_id_type=MESH)
                  for i, b in enumerate(bufs)]
        return sends, landed

    def start(_, bufs, sems):
        for cp in copies(bufs, sems)[0]:
            cp.start()

    def finish(_, bufs, sems):
        sends, landed = copies(bufs, sems)
        for cp in landed:
            cp.wait_recv()
        for cp in sends:
            cp.wait_send()

    return _Carried(halves, [pltpu.HBM(h.shape, F32) for h in halves],
                    [pltpu.SemaphoreType.DMA((n,)), pltpu.SemaphoreType.DMA((n,))], start, finish,
                    {i: i for i in range(n)})


def _combine(carries):
    operands, out_shapes, sems, aliases, spans = [], [], [], {}, []
    for c in carries:
        aliases.update({len(operands) + i: len(out_shapes) + o for i, o in c.aliases.items()})
        spans.append((len(operands), len(out_shapes), len(sems)))
        operands += list(c.operands)
        out_shapes += list(c.out_shapes)
        sems += list(c.sems)

    def each(phase):
        def run(ins, outs, sem_refs):
            for c, (a, b, s) in zip(carries, spans):
                getattr(c, phase)(ins[a:a + len(c.operands)], outs[b:b + len(c.out_shapes)], sem_refs[s:s + len(c.sems)])
        return run

    return _Carried(operands, out_shapes, sems, each("start"), each("finish"), aliases)


def _allreduce_small(arrays, wire):
    n = len(arrays)
    halves = [(a.shape[0], a.shape[1] // 2) for a in arrays]

    def body(*refs):
        srcs, outs = refs[:n], refs[n:2 * n]
        mine_bufs, sib_bufs, chip_bufs, total_bufs = (refs[k * n:(k + 1) * n] for k in range(2, 6))
        send_sems, recv_sems, local_sems = refs[6 * n:]
        x, y, c = _mesh_pos()
        k0 = 2 * x + y
        sib = (x, y, 1 - c)

        def remote(src, dst, j, i, to):
            return pltpu.make_async_remote_copy(src_ref=src, dst_ref=dst, send_sem=send_sems.at[j, i],
                                                recv_sem=recv_sems.at[j, i], device_id=to, device_id_type=MESH)

        def cols(ref, i, core):
            return ref.at[:, pl.ds(pl.multiple_of(core * halves[i][1], LANE), halves[i][1])]

        swaps = [remote(cols(s, i, 1 - c), b, 0, i, sib) for i, (s, b) in enumerate(zip(srcs, sib_bufs))]
        own = [pltpu.make_async_copy(cols(s, i, c), m, local_sems.at[i]) for i, (s, m) in enumerate(zip(srcs, mine_bufs))]
        for cp in swaps + own:
            cp.start()
        for cp in swaps + own:
            cp.wait()
        for m, b, buf in zip(mine_bufs, sib_bufs, chip_bufs):
            buf[k0] = (m[...] + b[...]).astype(buf.dtype)
        chips = _other_chips(x, y)
        sends = [remote(buf.at[k0], buf.at[k0], 1 + j, i, (*chip, c))
                 for j, chip in enumerate(chips) for i, buf in enumerate(chip_bufs)]
        for cp in sends:
            cp.start()
        for j, chip in enumerate(chips):
            for i, buf in enumerate(chip_bufs):
                remote(buf.at[k0], buf.at[2 * chip[0] + chip[1]], 1 + j, i, (*chip, c)).wait_recv()
        for cp in sends:
            cp.wait_send()
        for t, buf in zip(total_bufs, chip_bufs):
            t[...] = ((buf[0].astype(F32) + buf[1].astype(F32)) + buf[2].astype(F32)) + buf[3].astype(F32)
        joins = [remote(t, cols(o, i, c), 4, i, sib) for i, (t, o) in enumerate(zip(total_bufs, outs))]
        keep = [pltpu.make_async_copy(t, cols(o, i, c), local_sems.at[i]) for i, (t, o) in enumerate(zip(total_bufs, outs))]
        for cp in joins + keep:
            cp.start()
        for i, (t, o) in enumerate(zip(total_bufs, outs)):
            remote(t, cols(o, i, 1 - c), 4, i, sib).wait_recv()
        for cp in joins:
            cp.wait_send()
        for cp in keep:
            cp.wait()

    specs = [_full(a.shape) for a in arrays]
    return _pallas_call(
        body, name="allreduce_small", grid=(1,), in_specs=specs, out_specs=specs,
        out_shape=[_sds(a.shape) for a in arrays],
        scratch_shapes=([pltpu.VMEM(h, F32) for h in halves] + [pltpu.VMEM(h, F32) for h in halves]
                        + [pltpu.VMEM((NCHIP,) + h, dt) for h, dt in zip(halves, wire)] + [pltpu.VMEM(h, F32) for h in halves]
                        + [pltpu.SemaphoreType.DMA((5, n)), pltpu.SemaphoreType.DMA((5, n)), pltpu.SemaphoreType.DMA((n,))]),
        compiler_params=_params(8),
    )(*arrays)


def _adamw_terms(w, g, m, v):
    m = ADAM_B1 * m + (1.0 - ADAM_B1) * g
    v = ADAM_B2 * v + (1.0 - ADAM_B2) * jnp.square(g)
    m_hat = m / (1.0 - ADAM_B1 ** ADAM_STEP)
    v_hat = v / (1.0 - ADAM_B2 ** ADAM_STEP)
    return -ADAM_LR * (m_hat / (jnp.sqrt(v_hat) + ADAM_EPS) + ADAM_WD * w), m, v


ADAM_STEPS = 4


def _adamw_group(tag, ws, gs, ms, vs):
    n = len(ws)

    def body(*refs):
        ins, outs = refs[:4 * n], refs[4 * n:]
        for i in range(n):
            w, g, m, v = (ins[k * n + i][...] for k in range(4))
            outs[i][...] = g
            outs[n + i][...], outs[2 * n + i][...], outs[3 * n + i][...] = _adamw_terms(w, g, m, v)

    specs = [pl.BlockSpec((w.shape[0] // ADAM_STEPS, w.shape[1]), lambda i: (i, 0)) for w in ws]
    outs = _pallas_call(
        body, name="adamw_" + tag, grid=(ADAM_STEPS,), in_specs=specs * 4, out_specs=specs * 4,
        out_shape=[_sds(w.shape) for w in ws] * 4, compiler_params=_params(24),
    )(*_in_hbm(list(ws) + list(gs) + list(ms) + list(vs)))
    return outs[:n], outs[n:2 * n], outs[2 * n:3 * n], outs[3 * n:]


def _adamw_replicated(sums, row_of, direct):
    ns, nr, nd = len(sums), len(row_of), len(direct)

    def body(*refs):
        sum_refs = refs[:ns]
        ins = refs[ns:ns + 3 * nr + 4 * nd]
        outs = refs[ns + 3 * nr + 4 * nd:]
        for i, (_, _, _, si, row) in enumerate(row_of):
            w_ref, m_ref, v_ref = ins[3 * i:3 * i + 3]
            g = sum_refs[si][row:row + 1, :]
            outs[4 * i][...] = g
            outs[4 * i + 1][...], outs[4 * i + 2][...], outs[4 * i + 3][...] = _adamw_terms(w_ref[...], g, m_ref[...], v_ref[...])
        for i in range(nd):
            w_ref, m_ref, v_ref, g_ref = ins[3 * nr + 4 * i:3 * nr + 4 * i + 4]
            o = outs[4 * (nr + i):4 * (nr + i) + 4]
            g = g_ref[...]
            o[0][...] = g
            o[1][...], o[2][...], o[3][...] = _adamw_terms(w_ref[...], g, m_ref[...], v_ref[...])

    operands = list(sums)
    shapes = []
    for w, m, v, _, _ in row_of:
        operands += [w, m, v]
        shapes += [w.shape] * 4
    for w, m, v, g in direct:
        operands += [w, m, v, g]
        shapes += [w.shape] * 4
    flat = _pallas_call(
        body, name="adamw_replicated", grid=(1,), in_specs=[_full(a.shape) for a in operands],
        out_specs=[_full(s) for s in shapes], out_shape=[_sds(s) for s in shapes],
        compiler_params=_params(50),
    )(*operands)
    return [flat[4 * i:4 * i + 4] for i in range(nr + nd)]


class _Exchanges:
    def __init__(self, shards, conv_w, chip, core, apply):
        self.shards, self.conv_w, self.apply = shards, conv_w, apply
        self.active, self.calls = [], 0
        self.core_idx = jnp.reshape(core, (1,)).astype(jnp.int32)
        self.chip_core_idx = jnp.stack([chip, core]).astype(jnp.int32)

    def first(self):
        names = ["w_in", "w_glu"]
        got = _run_now("gather_first", _gather_group([self.shards[n] for n in names] + [self.conv_w],
                                                     [True, True, False]))
        out = dict(zip(names, got))
        out["conv_w"] = jnp.transpose(got[2], (1, 0, 2)).reshape(4, LW)
        return out

    def gather(self, names):
        return _gather_group([self.shards[n] for n in names], [True] * len(names))

    def reduce(self, tag, grads):
        self.active.append({"tag": tag, "names": list(grads), "stage": 0, "grads": list(grads.values())})

    def run(self, call, hold=()):
        groups = [g for g in self.active if g["tag"] not in hold]
        carries = [self._exchange_of(g) for g in groups]
        carry = _combine(carries)
        outs = list(call(carry))
        own = len(outs) - len(carry.out_shapes)
        landed = outs[own:]
        for g, c in zip(groups, carries):
            self._sum_after(g, landed[:len(c.out_shapes)])
            landed = landed[len(c.out_shapes):]
        self.active = [g for g in self.active if g["stage"] < 3]
        return outs[:own]

    def _exchange_of(self, g):
        if g["stage"] == 0:
            return _swap_group(g["grads"])
        if g["stage"] == 1:
            return _exchange_group(g["bf16"])
        return _join_group(g["halves"])

    def _sum_after(self, g, landed):
        if g["stage"] == 0:
            g["f32"], g["bf16"] = _add_sibling_group(g["tag"], self.chip_core_idx, g["grads"], landed)
        elif g["stage"] == 1:
            g["halves"] = _add_chips_group(g["tag"], self.chip_core_idx, g["f32"], landed)
        else:
            self.apply(g["tag"], g["names"], [t.reshape(2 * t.shape[1], t.shape[2]) for t in landed])
        g["stage"] += 1

    def drain(self):
        while self.active:
            self.calls += 1
            self.run(lambda carry: _run_now("reduce_%d" % self.calls, carry))


INPUT_NAMES = (["x", "p"] + [n for n in
               ["g_mix", "w_in", "b_in", "lam_re", "lam_im", "log_dt", "s5_b_re", "s5_b_im", "s5_c_re", "s5_c_im", "s5_d",
                "w_glu", "b_glu", "conv_w", "conv_b", "w_r", "b_r", "w_i", "b_i", "lru_lambda", "w_a_out", "w_b_out", "w_o",
                "g_ffn", "w_ffn_gate", "w_ffn_up", "w_ffn_down", "g_ple_gate", "w_ple_gate", "b_ple_gate", "w_ple", "g_ple",
                "g_final"]])
WEIGHT_NAMES = INPUT_NAMES[2:]


def kernel(*args):
    names = INPUT_NAMES + ["loss_target"] + ["m_" + n for n in WEIGHT_NAMES] + ["v_" + n for n in WEIGHT_NAMES]
    assert len(args) == len(names)
    given = dict(zip(names, args))

    def view(name):
        a = given[name]
        return jnp.swapaxes(a, -1, -2) if name.endswith(TRANSPOSED) else a

    def unview(name, a):
        return jnp.swapaxes(a, -1, -2) if name in TRANSPOSED else a

    def local(name):
        return view(name) if name.endswith("g_final") else view(name)[0]

    xi, yi, ci = _mesh_pos()
    k0 = 2 * xi + yi
    x, p, tgt = given["x"][0], given["p"][0, 0], given["loss_target"][0]

    results = {}

    row_halves = {}

    def apply(tag, names, totals):
        totals = dict(zip(names, totals))
        row_halves.update({n: totals.pop(n) for n in names if n in ("w_in_lo", "w_in_hi")})
        if len(row_halves) == 2:
            totals["w_in"] = jnp.concatenate([row_halves.pop("w_in_lo"), row_halves.pop("w_in_hi")])
        names = list(totals)
        if not names:
            return
        new = _adamw_group(tag, [local(n) for n in names], list(totals.values()), [local("m_" + n) for n in names],
                           [local("v_" + n) for n in names])
        for kind, arrays in zip(("grad", "delta", "new_m", "new_v"), new):
            for n, arr in zip(names, arrays):
                results[kind, n] = unview(n, arr[None])

    comm = _Exchanges({n: local(n).astype(BF) for n, _ in SHARDED}, local("conv_w"), k0, ci, apply)
    w = {n: local(n) for n in WEIGHT_NAMES if n != "conv_w" and n not in dict(SHARDED)}
    gx, sums, blocks = _local_step(x, p, tgt, w, comm)

    sum_names, block_names = list(sums), list(blocks)
    red = _allreduce_small([sums[n] for n in sum_names] + [blocks[n] for n in block_names],
                           [F32] * len(sum_names) + [BF] * len(block_names))
    sums = dict(zip(sum_names, red[:len(sum_names)]))
    blocks = dict(zip(block_names, red[len(sum_names):]))
    loss = jnp.sum(sums[LOSS_ROW[0]][LOSS_ROW[1]])
    direct_g = _replicated_grads(w, sums, blocks)
    conv_rows = sums[CONV_W_ROWS[0]][CONV_W_ROWS[1]:CONV_W_ROWS[1] + 4]
    direct_g["conv_w"] = lax.dynamic_slice(conv_rows, (0, k0 * CONV_SHARD[1]), CONV_SHARD)
    as_row = lambda a: a.reshape(1, -1)
    row_names = list(ACC_ROWS)
    row_of = [(as_row(given[n]), as_row(given["m_" + n]), as_row(given["v_" + n]),
               sum_names.index(ACC_ROWS[n][0]), ACC_ROWS[n][1]) for n in row_names]
    direct_names = list(direct_g)
    direct = [(view(n), view("m_" + n), view("v_" + n), direct_g[n].reshape(view(n).shape)) for n in direct_names]
    done = _adamw_replicated([sums[n] for n in sum_names], row_of, direct)
    for n, four in zip(row_names + direct_names, done):
        for kind, arr in zip(("grad", "delta", "new_m", "new_v"), four):
            results[kind, n] = unview(n, arr).reshape(given[n].shape)

    out = [loss, gx[None]]
    for kind in ("grad", "delta", "new_m", "new_v"):
        out += [results[kind, n] for n in WEIGHT_NAMES]
    return tuple(out)
```

```python
import functools
import math

import jax
import jax.numpy as jnp
from jax import lax
from jax.experimental import pallas as pl
from jax.experimental.pallas import tpu as pltpu

F32 = jnp.float32
BF = jnp.bfloat16

D = 1024
S5W = 512
NG, NS, NP = 32, 64, 16
GN = NG * NS
LW = 1024
NH, HD = 16, 64
LRU_C = 8.0
FH = 2816
NCHIP = 4
FC = FH // NCHIP
PLE = 256
INC = S5W + LW + 2 * D
EPS = 1e-6
ADAM_LR, ADAM_B1, ADAM_B2, ADAM_EPS, ADAM_WD, ADAM_STEP = 0.001, 0.9, 0.999, 1e-08, 0.01, 10

TM = 256
TK = 1024
LC = 512
SUB = 8
VMEM_MB = 1024 * 1024
MESH = pl.DeviceIdType.MESH
ANY = pl.BlockSpec(memory_space=pl.ANY)


def _mm(a, b):
    return jnp.dot(a.astype(BF), b.astype(BF), preferred_element_type=F32)


def _mm_nt(a, b):
    return lax.dot_general(a.astype(BF), b.astype(BF), (((1,), (1,)), ((), ())), preferred_element_type=F32)


def _mm_tn(a, b):
    return lax.dot_general(a.astype(BF), b.astype(BF), (((0,), (0,)), ((), ())), preferred_element_type=F32)


def _blockdiag_mm(x, blocks_ref):
    n, rows, _ = blocks_ref.shape
    return jnp.concatenate([jnp.dot(x[:, j * rows:(j + 1) * rows], blocks_ref[j], preferred_element_type=F32)
                            for j in range(n)], axis=1)


def _blockdiag_mm_t(x, blocks_ref):
    n, _, wide = blocks_ref.shape
    return jnp.concatenate([lax.dot_general(x[:, j * wide:(j + 1) * wide], blocks_ref[j], (((1,), (1,)), ((), ())),
                                            preferred_element_type=F32) for j in range(n)], axis=1)


def _rms(x):
    r = lax.rsqrt(jnp.mean(x * x, axis=-1, keepdims=True) + EPS)
    return x * r, r


def _rms_bwd(dy, xh, r, g):
    dxh = dy * g
    return r * (dxh - xh * jnp.mean(dxh * xh, axis=-1, keepdims=True))


def _colsum(x):
    return jnp.sum(x, axis=0, keepdims=True)


def _sig(x):
    return jax.nn.sigmoid(x)


def _gelu_grad(x):
    c = math.sqrt(2.0 / math.pi)
    t = jnp.tanh(c * (x + 0.044715 * x * x * x))
    return 0.5 * (1.0 + t) + 0.5 * x * (1.0 - t * t) * c * (1.0 + 3.0 * 0.044715 * x * x)


def _neg_expm1(x):
    series = -x * (1.0 + x * (0.5 + x * (1.0 / 6.0 + x * (1.0 / 24.0))))
    return jnp.where(x > -0.03, series, 1.0 - jnp.exp(x))


def _tok(width):
    return pl.BlockSpec((TM, width), lambda i: (i, 0))


def _tok_rev(width, nt):
    return pl.BlockSpec((TM, width), lambda i: (nt - 1 - i, 0))


def _full(shape):
    return pl.BlockSpec(shape, lambda i: (0,) * len(shape))


def _params(vmem_mb, **kw):
    return pltpu.CompilerParams(dimension_semantics=("arbitrary",), vmem_limit_bytes=vmem_mb * VMEM_MB, **kw)


def _sds(shape, dtype=F32):
    return jax.ShapeDtypeStruct(shape, dtype)


class _Carried:
    def __init__(self, operands, out_shapes, sems, start, finish, aliases=None):
        self.operands, self.out_shapes, self.sems = list(operands), list(out_shapes), list(sems)
        self.start, self.finish, self.aliases = start, finish, dict(aliases or {})


def _in_hbm(arrays):
    return [pltpu.with_memory_space_constraint(a, pltpu.HBM) for a in arrays]


def _pallas_call(body, carry=None, **kw):
    if carry is None:
        return pl.pallas_call(body, **kw)

    def at_step(corner):
        hit = [pl.program_id(d) == (size - 1 if corner else 0) for d, size in enumerate(kw["grid"])]
        return functools.reduce(jnp.logical_and, hit)

    name, grid, compiler_params = kw["name"], kw["grid"], kw["compiler_params"]
    in_specs, out_specs, out_shape = list(kw["in_specs"]), list(kw["out_specs"]), list(kw["out_shape"])
    scratch_shapes = list(kw.get("scratch_shapes", ()))
    n_in, n_out, n_scr = len(in_specs), len(out_specs), len(scratch_shapes)
    c_in, c_out = len(carry.operands), len(carry.out_shapes)

    def full_body(*refs):
        ins, refs = refs[:n_in], refs[n_in:]
        c_ins, refs = refs[:c_in], refs[c_in:]
        outs, refs = refs[:n_out], refs[n_out:]
        c_outs, refs = refs[:c_out], refs[c_out:]
        scratch, c_sems = refs[:n_scr], refs[n_scr:]

        @pl.when(at_step(0))
        def _():
            carry.start(c_ins, c_outs, c_sems)

        body(*ins, *outs, *scratch)

        @pl.when(at_step(1))
        def _():
            carry.finish(c_ins, c_outs, c_sems)

    call = pl.pallas_call(
        full_body, name=name, grid=grid, in_specs=in_specs + [ANY] * c_in, out_specs=out_specs + [ANY] * c_out,
        out_shape=out_shape + list(carry.out_shapes), scratch_shapes=scratch_shapes + list(carry.sems),
        input_output_aliases={n_in + i: n_out + o for i, o in carry.aliases.items()},
        compiler_params=compiler_params)
    return lambda *operands: call(*operands, *_in_hbm(carry.operands))


def _resident(pairs, sems):
    first = pl.program_id(0) == 0
    copies = [pltpu.make_async_copy(src, dst, sems.at[j]) for j, (src, dst) in enumerate(pairs)]

    @pl.when(first)
    def _():
        for cp in copies:
            cp.start()

    def wait(j):
        @pl.when(first)
        def _():
            copies[j].wait()

    return wait


def _resident_now(pairs, sems):
    @pl.when(pl.program_id(0) == 0)
    def _():
        copies = [pltpu.make_async_copy(src, dst, sems.at[j]) for j, (src, dst) in enumerate(pairs)]
        for cp in copies:
            cp.start()
        for cp in copies:
            cp.wait()


def _row_iota(width):
    return lax.broadcasted_iota(jnp.int32, (SUB, width), 0)


def _bcast_row(x, row):
    return jnp.broadcast_to(x[row:row + 1, :], x.shape)


def _slab(k):
    return pl.ds(pl.multiple_of(k * SUB, SUB), SUB)


QC = INC // NCHIP
Z_PARTS = ((0, S5W), (S5W, S5W + LW), (S5W + LW, INC))


def _inproj_fwd(x, g_mix, w_in, b_in, carry=None):
    L = x.shape[0]

    def body(x_ref, g_ref, w_hbm, b_ref, h_ref, ua_ref, ub_ref, gp_ref, w_vm, w_sems):
        _resident_now([(w_hbm.at[k], w_vm.at[k]) for k in range(NCHIP)], w_sems)
        xh, _ = _rms(x_ref[...])
        h = (xh * g_ref[...]).astype(BF)
        h_ref[...] = h
        for k in range(NCHIP):
            lo, hi = k * QC, (k + 1) * QC
            z = jnp.dot(h, w_vm[k], preferred_element_type=F32) + b_ref[:, lo:hi]
            for ref, (a, b) in zip((ua_ref, ub_ref, gp_ref), Z_PARTS):
                s, e = max(lo, a), min(hi, b)
                if s < e:
                    ref[:, s - a:e - a] = z[:, s - lo:e - lo]

    return _pallas_call(
        body, carry, name="inproj_fwd", grid=(L // TM,),
        in_specs=[_tok(D), _full((1, D)), ANY, _full((1, INC))],
        out_specs=[_tok(D), _tok(S5W), _tok(LW), _tok(2 * D)],
        out_shape=[_sds((L, D), BF), _sds((L, S5W)), _sds((L, LW)), _sds((L, 2 * D))],
        scratch_shapes=[pltpu.VMEM((NCHIP, D, QC), BF), pltpu.SemaphoreType.DMA((NCHIP,))],
        compiler_params=_params(40),
    )(x, g_mix, w_in, b_in)


def _inproj_bwd(x, dx1, dua, dub, dgp, g_mix, w_in, carry=None):
    L = x.shape[0]

    def body(x_ref, dx1_ref, dua_ref, dub_ref, dgp_ref, g_ref, w_hbm, gx_ref, dz_ref, dg_ref, db_ref, w_vm, w_sems):
        _resident_now([(w_hbm.at[k], w_vm.at[k]) for k in range(NCHIP)], w_sems)

        @pl.when(pl.program_id(0) == 0)
        def _():
            dg_ref[...] = jnp.zeros_like(dg_ref)
            db_ref[...] = jnp.zeros_like(db_ref)

        for src, (a, b) in zip((dua_ref, dub_ref, dgp_ref), Z_PARTS):
            d = src[...]
            dz_ref[:, a:b] = d.astype(BF)
            db_ref[0:1, a:b] += _colsum(d)
        dh = jnp.zeros((TM, D), F32)
        for k in range(NCHIP):
            dh = dh + lax.dot_general(dz_ref[:, k * QC:(k + 1) * QC], w_vm[k], (((1,), (1,)), ((), ())),
                                      preferred_element_type=F32)
        xh, r = _rms(x_ref[...])
        dg_ref[0:1, :] += _colsum(dh * xh)
        gx_ref[...] = dx1_ref[...] + _rms_bwd(dh, xh, r, g_ref[...])

    return _pallas_call(
        body, carry, name="inproj_bwd", grid=(L // TM,),
        in_specs=[_tok(D), _tok(D), _tok(S5W), _tok(LW), _tok(2 * D), _full((1, D)), ANY],
        out_specs=[_tok(D), _tok(INC), _full((SUB, D)), _full((SUB, INC))],
        out_shape=[_sds((L, D)), _sds((L, INC), BF), _sds((SUB, D)), _sds((SUB, INC))],
        scratch_shapes=[pltpu.VMEM((NCHIP, D, QC), BF), pltpu.SemaphoreType.DMA((NCHIP,))],
        compiler_params=_params(40),
    )(x, dx1, dua, dub, dgp, g_mix, w_in)


def _cscan(xr_ref, xi_ref, con_ref, cr_ref, ci_ref, reverse):
    n_slab = xr_ref.shape[0] // SUB
    width = xr_ref.shape[1]
    for lc in range(width // LC):
        cols = slice(lc * LC, (lc + 1) * LC)
        con = [con_ref[SUB * j:SUB * (j + 1), cols] for j in range(8)]

        def step(k, carry, cols=cols, con=con):
            cr, ci = carry
            rows = _slab(n_slab - 1 - k if reverse else k)
            xr, xi = xr_ref[rows, cols], xi_ref[rows, cols]
            for j, sh in enumerate((1, 2, 4)):
                mr, mi = con[2 * j], con[2 * j + 1]
                pr = pltpu.roll(xr, SUB - sh if reverse else sh, 0)
                pi = pltpu.roll(xi, SUB - sh if reverse else sh, 0)
                xr, xi = xr + mr * pr - mi * pi, xi + mr * pi + mi * pr
            xr, xi = xr + con[6] * cr - con[7] * ci, xi + con[6] * ci + con[7] * cr
            xr_ref[rows, cols] = xr
            xi_ref[rows, cols] = xi
            row = 0 if reverse else SUB - 1
            return _bcast_row(xr, row), _bcast_row(xi, row)

        cr, ci = lax.fori_loop(0, n_slab, step, (cr_ref[:, cols], ci_ref[:, cols]))
        cr_ref[:, cols] = cr
        ci_ref[:, cols] = ci


def _s5_fwd(ua, bbr, bbi, ccr, cci, dsk, con, w_glu, b_glu, carry=None):
    L = ua.shape[0]

    def body(ua_ref, bbr_hbm, bbi_hbm, ccr_hbm, cci_hbm, dsk_ref, con_ref, wg_ref, bg_ref,
             sr_ref, si_ref, y_ref, zg_ref, ya_ref, bbr_vm, bbi_vm, ccr_vm, cci_vm, cr_ref, ci_ref, w_sems):
        landed = _resident([(bbr_hbm, bbr_vm), (bbi_hbm, bbi_vm), (ccr_hbm, ccr_vm), (cci_hbm, cci_vm)], w_sems)

        @pl.when(pl.program_id(0) == 0)
        def _():
            cr_ref[...] = jnp.zeros_like(cr_ref)
            ci_ref[...] = jnp.zeros_like(ci_ref)

        u = ua_ref[...]
        ub = u.astype(BF)
        landed(0)
        sr_ref[...] = _blockdiag_mm(ub, bbr_vm)
        landed(1)
        si_ref[...] = _blockdiag_mm(ub, bbi_vm)
        _cscan(sr_ref, si_ref, con_ref, cr_ref, ci_ref, reverse=False)
        landed(2)
        landed(3)
        y = (_blockdiag_mm_t(sr_ref[...].astype(BF), ccr_vm) - _blockdiag_mm_t(si_ref[...].astype(BF), cci_vm)
             + dsk_ref[...] * u)
        y_ref[...] = y
        zg = jax.nn.gelu(y)
        zg_ref[...] = zg.astype(BF)
        q = _mm(zg, wg_ref[...]) + bg_ref[...]
        ya_ref[...] = (zg * _sig(q)).astype(BF)

    return _pallas_call(
        body, carry, name="s5_fwd", grid=(L // TM,),
        in_specs=[_tok(S5W), ANY, ANY, ANY, ANY, _full((1, S5W)), _full((8 * SUB, GN)),
                  _full((S5W, S5W)), _full((1, S5W))],
        out_specs=[_tok(GN), _tok(GN), _tok(S5W), _tok(S5W), _tok(S5W)],
        out_shape=[_sds((L, GN)), _sds((L, GN)), _sds((L, S5W)), _sds((L, S5W), BF), _sds((L, S5W), BF)],
        scratch_shapes=[pltpu.VMEM((S5W // 128, 128, GN // (S5W // 128)), BF)] * 4 + [
                        pltpu.VMEM((SUB, GN), F32), pltpu.VMEM((SUB, GN), F32),
                        pltpu.SemaphoreType.DMA((4,))],
        compiler_params=_params(44),
    )(ua, bbr, bbi, ccr, cci, dsk, con, w_glu, b_glu)


def _s5_bwd(dya, y, ua, sr, si, bbr, bbi, ccr, cci, dsk, con_rev, w_glu, b_glu, carry=None):
    L = ua.shape[0]
    nt = L // TM
    spt = TM // SUB
    n_slab = spt

    def halo_map(i):
        return (jnp.maximum((nt - 1 - i) * spt - 1, 0), 0)

    def body(dya_ref, y_ref, ua_ref, sr_ref, si_ref, hr_ref, hi_ref, bbr_hbm, bbi_hbm, ccr_hbm, cci_hbm,
             dsk_ref, con_ref, wg_ref, bg_ref,
             dua_ref, dq_ref, dy_ref, lr_ref, li_ref, da_ref, dsm_ref,
             bbr_vm, bbi_vm, ccr_vm, cci_vm, cr_ref, ci_ref, w_sems):
        i = pl.program_id(0)
        landed = _resident([(ccr_hbm, ccr_vm), (cci_hbm, cci_vm), (bbr_hbm, bbr_vm), (bbi_hbm, bbi_vm)], w_sems)

        @pl.when(i == 0)
        def _():
            cr_ref[...] = jnp.zeros_like(cr_ref)
            ci_ref[...] = jnp.zeros_like(ci_ref)
            da_ref[...] = jnp.zeros_like(da_ref)
            dsm_ref[...] = jnp.zeros_like(dsm_ref)

        u = ua_ref[...]
        yv = y_ref[...]
        dya = dya_ref[...]
        zg = jax.nn.gelu(yv)
        sg = _sig(_mm(zg, wg_ref[...]) + bg_ref[...])
        dq = dya * zg * sg * (1.0 - sg)
        dq_ref[...] = dq.astype(BF)
        dzg = dya * sg + _mm_nt(dq, wg_ref[...])
        dy = dzg * _gelu_grad(yv)
        dyb = dy.astype(BF)
        dy_ref[...] = dyb
        dsm_ref[0:1, :] += _colsum(dy * u)
        dsm_ref[1:2, :] += _colsum(dq)
        landed(0)
        lr_ref[...] = _blockdiag_mm(dyb, ccr_vm)
        landed(1)
        li_ref[...] = -_blockdiag_mm(dyb, cci_vm)
        _cscan(lr_ref, li_ref, con_ref, cr_ref, ci_ref, reverse=True)

        first_tile = (i == nt - 1)
        row = _row_iota(LC)
        for lc in range(GN // LC):
            cols = slice(lc * LC, (lc + 1) * LC)
            h_r = jnp.where(first_tile, 0.0, hr_ref[:, cols])
            h_i = jnp.where(first_tile, 0.0, hi_ref[:, cols])

            def step(k, acc, cols=cols, h_r=h_r, h_i=h_i):
                ar, ai = acc
                rows = _slab(k)
                prev = _slab(jnp.maximum(k - 1, 0))
                pr = jnp.where(k == 0, h_r, sr_ref[prev, cols])
                pi = jnp.where(k == 0, h_i, si_ref[prev, cols])
                spr = pltpu.roll(jnp.where(row == SUB - 1, pr, sr_ref[rows, cols]), 1, 0)
                spi = pltpu.roll(jnp.where(row == SUB - 1, pi, si_ref[rows, cols]), 1, 0)
                lr, li = lr_ref[rows, cols], li_ref[rows, cols]
                return ar + lr * spr + li * spi, ai + li * spr - lr * spi

            zero = jnp.zeros((SUB, LC), F32)
            ar, ai = lax.fori_loop(0, n_slab, step, (zero, zero))
            da_ref[0:1, cols] += _colsum(ar)
            da_ref[1:2, cols] += _colsum(ai)

        landed(2)
        landed(3)
        dua_ref[...] = (dy * dsk_ref[...] + _blockdiag_mm_t(lr_ref[...].astype(BF), bbr_vm)
                        + _blockdiag_mm_t(li_ref[...].astype(BF), bbi_vm))

    return _pallas_call(
        body, carry, name="s5_bwd", grid=(nt,),
        in_specs=[_tok_rev(S5W, nt), _tok_rev(S5W, nt), _tok_rev(S5W, nt), _tok_rev(GN, nt), _tok_rev(GN, nt),
                  pl.BlockSpec((SUB, GN), halo_map), pl.BlockSpec((SUB, GN), halo_map),
                  ANY, ANY, ANY, ANY, _full((1, S5W)), _full((8 * SUB, GN)), _full((S5W, S5W)), _full((1, S5W))],
        out_specs=[_tok_rev(S5W, nt), _tok_rev(S5W, nt), _tok_rev(S5W, nt), _tok_rev(GN, nt), _tok_rev(GN, nt),
                   _full((SUB, GN)), _full((SUB, S5W))],
        out_shape=[_sds((L, S5W)), _sds((L, S5W), BF), _sds((L, S5W), BF), _sds((L, GN)), _sds((L, GN)),
                   _sds((SUB, GN)), _sds((SUB, S5W))],
        scratch_shapes=[pltpu.VMEM((S5W // 128, 128, GN // (S5W // 128)), BF)] * 4 + [
                        pltpu.VMEM((SUB, GN), F32), pltpu.VMEM((SUB, GN), F32),
                        pltpu.SemaphoreType.DMA((4,))],
        compiler_params=_params(52),
    )(dya, y, ua, sr, si, sr, si, bbr, bbi, ccr, cci, dsk, con_rev, w_glu, b_glu)


def _lru_gate_terms(rg, sp):
    log_a = -LRU_C * rg * sp
    a = jnp.exp(log_a)
    mult = jnp.sqrt(_neg_expm1(2.0 * log_a))
    return a, mult


def _lru_fwd(ub, conv_w, conv_b, wr, wi, b_r, b_i, sp, carry=None):
    L = ub.shape[0]
    n_slab = TM // SUB

    def body(ub_ref, cw_ref, cb_ref, wr_ref, wi_ref, br_ref, bi_ref, sp_ref,
             xc_ref, rg_ref, ig_ref, h_ref, hp_ref, a_ref, halo_ref, carry_ref):
        @pl.when(pl.program_id(0) == 0)
        def _():
            halo_ref[...] = jnp.zeros_like(halo_ref)
            carry_ref[...] = jnp.zeros_like(carry_ref)

        row = _row_iota(LW)
        taps = [cw_ref[k:k + 1, :] for k in range(4)]
        cb = cb_ref[...]

        def conv_step(k, prev):
            rows = _slab(k)
            cur = ub_ref[rows, :]
            acc = taps[3] * cur + cb
            for j in (1, 2, 3):
                acc = acc + taps[3 - j] * pltpu.roll(jnp.where(row >= SUB - j, prev, cur), j, 0)
            xc_ref[rows, :] = acc
            return cur

        halo_ref[...] = lax.fori_loop(0, n_slab, conv_step, halo_ref[...])

        xc = xc_ref[...]
        xcb = xc.astype(BF)
        rg = _sig(_blockdiag_mm(xcb, wr_ref) + br_ref[...])
        ig = _sig(_blockdiag_mm(xcb, wi_ref) + bi_ref[...])
        rg_ref[...] = rg
        ig_ref[...] = ig
        a, mult = _lru_gate_terms(rg, sp_ref[...])
        a_ref[...] = a
        h_ref[...] = mult * ig * xc

        rowc = _row_iota(LC)
        for lc in range(LW // LC):
            cols = slice(lc * LC, (lc + 1) * LC)

            def step(k, c, cols=cols):
                rows = _slab(k)
                av, b = a_ref[rows, cols], h_ref[rows, cols]
                for sh in (1, 2, 4):
                    keep = rowc >= sh
                    b = b + av * jnp.where(keep, pltpu.roll(b, sh, 0), 0.0)
                    av = av * jnp.where(keep, pltpu.roll(av, sh, 0), 1.0)
                h = b + av * c
                h_ref[rows, cols] = h
                hp_ref[rows, cols] = jnp.where(rowc == 0, c, pltpu.roll(h, 1, 0))
                return _bcast_row(h, SUB - 1)

            carry_ref[:, cols] = lax.fori_loop(0, n_slab, step, carry_ref[:, cols])

    return _pallas_call(
        body, carry, name="lru_fwd", grid=(L // TM,),
        in_specs=[_tok(LW), _full((4, LW)), _full((1, LW)), _full((LW // 128, 128, 128)), _full((LW // 128, 128, 128)),
                  _full((1, LW)), _full((1, LW)), _full((1, LW))],
        out_specs=[_tok(LW)] * 5,
        out_shape=[_sds((L, LW))] * 5,
        scratch_shapes=[pltpu.VMEM((TM, LW), F32), pltpu.VMEM((SUB, LW), F32), pltpu.VMEM((SUB, LW), F32)],
        compiler_params=_params(40),
    )(ub, conv_w, conv_b, wr, wi, b_r, b_i, sp)


def _lru_bwd(dyb, xc, rg, ig, hp, ub, conv_w, wr, wi, sp, dsp, carry=None):
    L = ub.shape[0]
    nt = L // TM
    spt = TM // SUB
    n_slab = spt

    def halo_map(i):
        return (jnp.maximum((nt - 1 - i) * spt - 1, 0), 0)

    def body(dh_ref, xc_ref, rg_ref, ig_ref, hp_ref, ub_ref, uh_ref, cw_ref, wr_ref, wi_ref, sp_ref, dsp_ref,
             dub_ref, dpr_ref, dpi_ref, acc_ref, a_ref, lam_ref, dxc_ref, carry_ref, next_ref):
        i = pl.program_id(0)

        @pl.when(i == 0)
        def _():
            carry_ref[...] = jnp.zeros_like(carry_ref)
            next_ref[...] = jnp.zeros_like(next_ref)
            acc_ref[...] = jnp.zeros_like(acc_ref)

        sp = sp_ref[...]
        rg, ig, xc = rg_ref[...], ig_ref[...], xc_ref[...]
        a, mult = _lru_gate_terms(rg, sp)
        a_ref[...] = a

        rowc = _row_iota(LC)
        for lc in range(LW // LC):
            cols = slice(lc * LC, (lc + 1) * LC)

            def step(k, c, cols=cols):
                rows = _slab(n_slab - 1 - k)
                av, dh = a_ref[rows, cols], dh_ref[rows, cols]
                b = av * dh
                for sh in (1, 2, 4):
                    keep = rowc < SUB - sh
                    b = b + av * jnp.where(keep, pltpu.roll(b, SUB - sh, 0), 0.0)
                    av = av * jnp.where(keep, pltpu.roll(av, SUB - sh, 0), 1.0)
                mu = b + av * c
                lam_ref[rows, cols] = dh + jnp.where(rowc == SUB - 1, c, pltpu.roll(mu, SUB - 1, 0))
                return _bcast_row(mu, 0)

            carry_ref[:, cols] = lax.fori_loop(0, n_slab, step, carry_ref[:, cols])

        lam = lam_ref[...]
        d_a = lam * hp_ref[...]
        d_mult = lam * ig * xc
        d_ig = lam * mult * xc
        dxc = lam * mult * ig
        d_log_a = d_a * a - d_mult * a * a / mult
        d_rg = (-LRU_C) * sp * d_log_a
        acc_ref[0:1, :] += _colsum((-LRU_C) * rg * d_log_a) * dsp_ref[...]
        dpr = d_rg * rg * (1.0 - rg)
        dpi = d_ig * ig * (1.0 - ig)
        acc_ref[1:2, :] += _colsum(dpr)
        acc_ref[2:3, :] += _colsum(dpi)
        dprb, dpib = dpr.astype(BF), dpi.astype(BF)
        dpr_ref[...] = dprb
        dpi_ref[...] = dpib
        dxc = dxc + _blockdiag_mm_t(dprb, wr_ref) + _blockdiag_mm_t(dpib, wi_ref)
        dxc_ref[...] = dxc
        acc_ref[3:4, :] += _colsum(dxc)

        row = _row_iota(LW)
        taps = [cw_ref[k:k + 1, :] for k in range(4)]
        u_halo = jnp.where(i == nt - 1, 0.0, uh_ref[...])
        nxt_tile = next_ref[...]

        def conv_step(k, accs):
            rows = _slab(k)
            cur = dxc_ref[rows, :]
            nxt = jnp.where(k == n_slab - 1, nxt_tile, dxc_ref[_slab(jnp.minimum(k + 1, n_slab - 1)), :])
            ucur = ub_ref[rows, :]
            uprev = jnp.where(k == 0, u_halo, ub_ref[_slab(jnp.maximum(k - 1, 0)), :])
            du = taps[3] * cur
            new = [accs[3] + cur * ucur]
            for j in (1, 2, 3):
                du = du + taps[3 - j] * pltpu.roll(jnp.where(row < j, nxt, cur), SUB - j, 0)
                new.append(accs[3 - j] + cur * pltpu.roll(jnp.where(row >= SUB - j, uprev, ucur), j, 0))
            dub_ref[rows, :] = du
            return tuple(new[::-1])

        zero = jnp.zeros((SUB, LW), F32)
        accs = lax.fori_loop(0, n_slab, conv_step, (zero, zero, zero, zero))
        for k in range(4):
            acc_ref[4 + k:5 + k, :] += _colsum(accs[k])
        next_ref[...] = dxc_ref[0:SUB, :]

    return _pallas_call(
        body, carry, name="lru_bwd", grid=(nt,),
        in_specs=[_tok_rev(LW, nt)] * 6 + [pl.BlockSpec((SUB, LW), halo_map), _full((4, LW)),
                                           _full((LW // 128, 128, 128)), _full((LW // 128, 128, 128)), _full((1, LW)), _full((1, LW))],
        out_specs=[_tok_rev(LW, nt), _tok_rev(LW, nt), _tok_rev(LW, nt), _full((SUB, LW))],
        out_shape=[_sds((L, LW)), _sds((L, LW), BF), _sds((L, LW), BF), _sds((SUB, LW))],
        scratch_shapes=[pltpu.VMEM((TM, LW), F32), pltpu.VMEM((TM, LW), F32), pltpu.VMEM((TM, LW), F32),
                        pltpu.VMEM((SUB, LW), F32), pltpu.VMEM((SUB, LW), F32)],
        compiler_params=_params(48),
    )(dyb, xc, rg, ig, hp, ub, ub, conv_w, wr, wi, sp, dsp)


AC = D // NCHIP


def _merge_fwd(x, ya, yb, gp, w_a, w_b, w_o, carry=None):
    L = x.shape[0]

    def body(x_ref, ya_ref, yb_ref, gp_ref, wa_ref, wb_ref, wo_ref, x1_ref, pa_ref, pb_ref, mg_ref):
        ya = ya_ref[...]
        for k in range(NCHIP):
            pa_ref[:, k * AC:(k + 1) * AC] = jnp.dot(ya, wa_ref[k], preferred_element_type=F32)
        pb = _mm(yb_ref[...], wb_ref[...])
        pb_ref[...] = pb
        gp = gp_ref[...]
        merged = (_sig(gp[:, :D]) * pa_ref[...] + _sig(gp[:, D:]) * pb).astype(BF)
        mg_ref[...] = merged
        x1_ref[...] = x_ref[...] + jnp.dot(merged, wo_ref[...], preferred_element_type=F32)

    return _pallas_call(
        body, carry, name="merge_fwd", grid=(L // TM,),
        in_specs=[_tok(D), _tok(S5W), _tok(LW), _tok(2 * D), _full((NCHIP, S5W, AC)), _full((LW, D)), _full((D, D))],
        out_specs=[_tok(D), _tok(D), _tok(D), _tok(D)],
        out_shape=[_sds((L, D)), _sds((L, D)), _sds((L, D)), _sds((L, D), BF)],
        compiler_params=_params(40),
    )(x, ya, yb, gp, w_a, w_b, w_o)


def _merge_bwd(dx1, gp, pa, pb, w_a, w_b, w_o, carry=None):
    L = dx1.shape[0]

    def body(dx1_ref, gp_ref, pa_ref, pb_ref, wa_ref, wb_ref, wo_ref, dya_ref, dyb_ref, dgp_ref, dpa_ref, dpb_ref):
        dm = _mm_nt(dx1_ref[...], wo_ref[...])
        gp = gp_ref[...]
        sa, sb = _sig(gp[:, :D]), _sig(gp[:, D:])
        dpa = (dm * sa).astype(BF)
        dpb = (dm * sb).astype(BF)
        dpa_ref[...] = dpa
        dpb_ref[...] = dpb
        dgp_ref[:, :D] = dm * pa_ref[...] * sa * (1.0 - sa)
        dgp_ref[:, D:] = dm * pb_ref[...] * sb * (1.0 - sb)
        dya = jnp.zeros((TM, S5W), F32)
        for k in range(NCHIP):
            dya = dya + _mm_nt(dpa[:, k * AC:(k + 1) * AC], wa_ref[k])
        dya_ref[...] = dya
        dyb_ref[...] = _mm_nt(dpb, wb_ref[...])

    return _pallas_call(
        body, carry, name="merge_bwd", grid=(L // TM,),
        in_specs=[_tok(D), _tok(2 * D), _tok(D), _tok(D), _full((NCHIP, S5W, AC)), _full((LW, D)), _full((D, D))],
        out_specs=[_tok(S5W), _tok(LW), _tok(2 * D), _tok(D), _tok(D)],
        out_shape=[_sds((L, S5W)), _sds((L, LW)), _sds((L, 2 * D)), _sds((L, D), BF), _sds((L, D), BF)],
        compiler_params=_params(40),
    )(dx1, gp, pa, pb, w_a, w_b, w_o)


def _chunk_tok(width):
    return pl.BlockSpec((NCHIP, TM, width), lambda i: (0, i, 0))


def _ffn_fwd(x1, g_ffn, wg, wu, wd, carry=None):
    L = x1.shape[0]

    def body(x_ref, g_ref, wg_hbm, wu_hbm, wd_hbm, x2_ref, h2_ref, gg_ref, uu_ref, wg_vm, wu_vm, wd_vm, w_sems):
        _resident_now([(src.at[c], dst.at[c]) for c in range(NCHIP)
                       for src, dst in ((wg_hbm, wg_vm), (wu_hbm, wu_vm), (wd_hbm, wd_vm))], w_sems)
        x = x_ref[...]
        xh, _ = _rms(x)
        h2 = (xh * g_ref[...]).astype(BF)
        h2_ref[...] = h2
        out = x
        for c in range(NCHIP):
            gg = lax.dot_general(h2, wg_vm[c], (((1,), (1,)), ((), ())), preferred_element_type=F32)
            uu = lax.dot_general(h2, wu_vm[c], (((1,), (1,)), ((), ())), preferred_element_type=F32)
            gg_ref[c] = gg.astype(BF)
            uu_ref[c] = uu.astype(BF)
            act = (gg * _sig(gg) * uu).astype(BF)
            out = out + jnp.dot(act, wd_vm[c], preferred_element_type=F32)
        x2_ref[...] = out

    return _pallas_call(
        body, carry, name="ffn_fwd", grid=(L // TM,),
        in_specs=[_tok(D), _full((1, D)), ANY, ANY, ANY],
        out_specs=[_tok(D), _tok(D), _chunk_tok(FC), _chunk_tok(FC)],
        out_shape=[_sds((L, D)), _sds((L, D), BF), _sds((NCHIP, L, FC), BF), _sds((NCHIP, L, FC), BF)],
        scratch_shapes=[pltpu.VMEM((NCHIP, FC, D), BF)] * 3 + [pltpu.SemaphoreType.DMA((3 * NCHIP,))],
        compiler_params=_params(52),
    )(x1, g_ffn, wg, wu, wd)


def _ffn_bwd(x1, dx2, gg, uu, g_ffn, wg, wu, wd, carry=None):
    L = x1.shape[0]

    def body(x_ref, dx2_ref, gg_ref, uu_ref, g_ref, wg_hbm, wu_hbm, wd_hbm,
             dx1_ref, act_ref, dgg_ref, duu_ref, dg_ref, wg_vm, wu_vm, wd_vm, w_sems):
        _resident_now([(src.at[c], dst.at[c]) for c in range(NCHIP)
                       for src, dst in ((wg_hbm, wg_vm), (wu_hbm, wu_vm), (wd_hbm, wd_vm))], w_sems)

        @pl.when(pl.program_id(0) == 0)
        def _():
            dg_ref[...] = jnp.zeros_like(dg_ref)

        dx2 = dx2_ref[...]
        dx2b = dx2.astype(BF)
        dh2 = jnp.zeros((TM, D), F32)
        for c in range(NCHIP):
            g = gg_ref[c].astype(F32)
            u = uu_ref[c].astype(F32)
            s = _sig(g)
            silu = g * s
            act_ref[c] = (silu * u).astype(BF)
            dact = lax.dot_general(dx2b, wd_vm[c], (((1,), (1,)), ((), ())), preferred_element_type=F32)
            dg = (dact * u * s * (1.0 + g * (1.0 - s))).astype(BF)
            du = (dact * silu).astype(BF)
            dgg_ref[c] = dg
            duu_ref[c] = du
            dh2 = dh2 + jnp.dot(dg, wg_vm[c], preferred_element_type=F32)
            dh2 = dh2 + jnp.dot(du, wu_vm[c], preferred_element_type=F32)
        xh, r = _rms(x_ref[...])
        dg_ref[0:1, :] += _colsum(dh2 * xh)
        dx1_ref[...] = dx2 + _rms_bwd(dh2, xh, r, g_ref[...])

    return _pallas_call(
        body, carry, name="ffn_bwd", grid=(L // TM,),
        in_specs=[_tok(D), _tok(D), _chunk_tok(FC), _chunk_tok(FC), _full((1, D)), ANY, ANY, ANY],
        out_specs=[_tok(D), _chunk_tok(FC), _chunk_tok(FC), _chunk_tok(FC), _full((SUB, D))],
        out_shape=[_sds((L, D)), _sds((NCHIP, L, FC), BF), _sds((NCHIP, L, FC), BF), _sds((NCHIP, L, FC), BF),
                   _sds((SUB, D))],
        scratch_shapes=[pltpu.VMEM((NCHIP, FC, D), BF)] * 3 + [pltpu.SemaphoreType.DMA((3 * NCHIP,))],
        compiler_params=_params(56),
    )(x1, dx2, gg, uu, g_ffn, wg, wu, wd)


def _ple_loss(x2, p, tgt, g_pg, w_pg, b_pg, w_ple, g_ple, g_final):
    L = x2.shape[0]

    def body(x2_ref, p_ref, t_ref, gpg_ref, wpg_ref, bpg_ref, wple_ref, gple_ref, gf_ref,
             dx2_ref, n2_ref, dpre_ref, de0_ref, acc_ref):
        @pl.when(pl.program_id(0) == 0)
        def _():
            acc_ref[...] = jnp.zeros_like(acc_ref)

        x2 = x2_ref[...]
        x2h, r2 = _rms(x2)
        n2 = (x2h * gpg_ref[...]).astype(BF)
        n2_ref[...] = n2
        gate = _sig(jnp.dot(n2, wpg_ref[...], preferred_element_type=F32) + bpg_ref[...])
        pb = p_ref[...].astype(BF)
        e0 = jnp.concatenate([jnp.dot(pb, wple_ref[k], preferred_element_type=F32) for k in range(NCHIP)], axis=1)
        e0h, re = _rms(e0)
        e = e0h * gple_ref[...]
        x3 = x2 + gate * e
        x3h, r3 = _rms(x3)
        diff = x3h * gf_ref[...] - t_ref[...]
        acc_ref[4:5, :] += _colsum(diff * diff) * (0.5 / D)
        dy = diff * (1.0 / D)
        acc_ref[3:4, :] += _colsum(dy * x3h)
        dx3 = _rms_bwd(dy, x3h, r3, gf_ref[...])
        de = dx3 * gate
        acc_ref[2:3, :] += _colsum(de * e0h)
        de0_ref[...] = _rms_bwd(de, e0h, re, gple_ref[...]).astype(BF)
        dpre = dx3 * e * gate * (1.0 - gate)
        acc_ref[1:2, :] += _colsum(dpre)
        dpreb = dpre.astype(BF)
        dpre_ref[...] = dpreb
        dn2 = lax.dot_general(dpreb, wpg_ref[...], (((1,), (1,)), ((), ())), preferred_element_type=F32)
        acc_ref[0:1, :] += _colsum(dn2 * x2h)
        dx2_ref[...] = dx3 + _rms_bwd(dn2, x2h, r2, gpg_ref[...])

    return _pallas_call(
        body, name="ple_loss", grid=(L // TM,),
        in_specs=[_tok(D), _tok(PLE), _tok(D), _full((1, D)), _full((D, D)), _full((1, D)), _full((NCHIP, PLE, AC)),
                  _full((1, D)), _full((1, D))],
        out_specs=[_tok(D), _tok(D), _tok(D), _tok(D), _full((SUB, D))],
        out_shape=[_sds((L, D)), _sds((L, D), BF), _sds((L, D), BF), _sds((L, D), BF), _sds((SUB, D))],
        compiler_params=_params(40),
    )(x2, p, tgt, g_pg, w_pg, b_pg, w_ple, g_ple, g_final)


def _tn(name, a, b, col_chunk=None, a_block=None, carry=None):
    L = a.shape[-2]
    m, n = a.shape[-1], b.shape[-1]
    a_col = 0
    if a_block is not None:
        a_col, m = a_block
    tk = L if (a.ndim == 3 or b.ndim == 3 or a_block is not None) else TK
    if a.ndim == 3 or b.ndim == 3:
        nj, bn = (a if a.ndim == 3 else b).shape[0], n
        a_spec = (pl.BlockSpec((None, tk, m), lambda j, t: (j, t, 0)) if a.ndim == 3
                  else pl.BlockSpec((tk, m), lambda j, t: (t, 0)))
        b_spec = (pl.BlockSpec((None, tk, n), lambda j, t: (j, t, 0)) if b.ndim == 3
                  else pl.BlockSpec((tk, n), lambda j, t: (t, 0)))
        out_spec, out_shape = pl.BlockSpec((None, m, n), lambda j, t: (j, 0, 0)), _sds((nj, m, n))
    else:
        bn = col_chunk
        if bn is None:
            bn = next((cand for cand in (1024, 512) if n > cand and n % cand == 0), n)
        nj = n // bn
        a_spec = pl.BlockSpec((tk, m), lambda j, t: (t, a_col))
        b_spec = pl.BlockSpec((tk, bn), lambda j, t: (t, j))
        if col_chunk is None:
            out_spec, out_shape = pl.BlockSpec((m, bn), lambda j, t: (0, j)), _sds((m, n))
        else:
            out_spec, out_shape = pl.BlockSpec((None, m, bn), lambda j, t: (j, 0, 0)), _sds((nj, m, bn))

    def body(a_ref, b_ref, o_ref):
        if tk == L:
            o_ref[...] = _mm_tn(a_ref[...], b_ref[...])
        else:
            @pl.when(pl.program_id(1) == 0)
            def _():
                o_ref[...] = jnp.zeros_like(o_ref)

            o_ref[...] += _mm_tn(a_ref[...], b_ref[...])

    outs = _pallas_call(
        body, carry, name=name, grid=(nj, L // tk), in_specs=[a_spec, b_spec], out_specs=[out_spec],
        out_shape=[pltpu.HBM(out_shape.shape, out_shape.dtype)],
        compiler_params=pltpu.CompilerParams(dimension_semantics=("arbitrary", "arbitrary"),
                                             vmem_limit_bytes=(30 if tk == L else 28) * VMEM_MB),
    )(a, b)
    return outs[0] if carry is None else outs


LANE = 128


def _tn_blocks(name, a, bs, ga, gb, carry=None):
    L, m, n, nb = a.shape[0], a.shape[1], bs[0].shape[1], len(bs)
    per = LANE // ga
    wb = per * gb
    n_super = m // LANE

    def body(a_ref, *refs):
        b_refs, o_refs, acc_refs = refs[:nb], refs[nb:2 * nb], refs[2 * nb:]
        t = pl.program_id(0)

        @pl.when(t == 0)
        def _():
            for acc in acc_refs:
                acc[...] = jnp.zeros_like(acc)

        lhs = a_ref[...].astype(BF)
        for b_ref, acc in zip(b_refs, acc_refs):
            rhs = b_ref[...].astype(BF)
            for j in range(n_super):
                acc[j] += _mm_tn(lhs[:, j * LANE:(j + 1) * LANE], rhs[:, j * wb:(j + 1) * wb])

        @pl.when(t == L // TK - 1)
        def _():
            own = (lax.broadcasted_iota(jnp.int32, (LANE, wb), 0) // ga) == (lax.broadcasted_iota(jnp.int32, (LANE, wb), 1) // gb)
            for o_ref, acc in zip(o_refs, acc_refs):
                for j in range(n_super):
                    kept = jnp.where(own, acc[j], 0.0)
                    o_ref[:, j * wb:(j + 1) * wb] = jnp.sum(kept.reshape(per, ga, wb), axis=0)

    outs = _pallas_call(
        body, carry, name=name, grid=(L // TK,),
        in_specs=[pl.BlockSpec((TK, m), lambda t: (t, 0))] + [pl.BlockSpec((TK, n), lambda t: (t, 0))] * nb,
        out_specs=[_full((ga, n))] * nb, out_shape=[_sds((ga, n))] * nb,
        scratch_shapes=[pltpu.VMEM((n_super, LANE, wb), F32)] * nb,
        compiler_params=_params(48),
    )(*_in_hbm([a] + list(bs)))
    return list(outs)


def _s5_discretize(lam_re, lam_im, log_dt, b_re, b_im):
    dt = jnp.exp(log_dt)[:, None]
    mag = jnp.exp(lam_re * dt)
    ar = mag * jnp.cos(lam_im * dt)
    ai = mag * jnp.sin(lam_im * dt)
    den = lam_re * lam_re + lam_im * lam_im
    nr = ar - 1.0
    fr = (nr * lam_re + ai * lam_im) / den
    fi = (ai * lam_re - nr * lam_im) / den
    bbr = fr[:, None, :] * b_re - fi[:, None, :] * b_im
    bbi = fr[:, None, :] * b_im + fi[:, None, :] * b_re
    return ar, ai, bbr, bbi


def _prepare(by_rows, block_cols, ar, ai):
    n = len(by_rows)

    def body(*refs):
        srcs, (ar_ref, ai_ref), dense, (con_ref, rev_ref) = refs[:n], refs[n:n + 2], refs[n + 2:2 * n + 2], refs[2 * n + 2:]
        for src, out, c in zip(srcs, dense, block_cols):
            r = src.shape[0]
            per = LANE // r
            wide = per * c
            own = (lax.broadcasted_iota(jnp.int32, (LANE, wide), 0) // r) == (lax.broadcasted_iota(jnp.int32, (LANE, wide), 1) // c)
            for j in range(out.shape[0]):
                tiled = jnp.broadcast_to(src[:, j * wide:(j + 1) * wide][None], (per, r, wide)).reshape(LANE, wide)
                out[j] = jnp.where(own, tiled, 0.0).astype(BF)
        a_r, a_i = ar_ref[...], ai_ref[...]
        pw = [(jnp.ones_like(a_r), jnp.zeros_like(a_i))]
        for _ in range(SUB):
            pr, pi = pw[-1]
            pw.append((pr * a_r - pi * a_i, pr * a_i + pi * a_r))
        row = _row_iota(GN)
        for ref, reverse in ((con_ref, False), (rev_ref, True)):
            sign = -1.0 if reverse else 1.0
            for j, sh in enumerate((1, 2, 4)):
                keep = (row < SUB - sh) if reverse else (row >= sh)
                ref[2 * j * SUB:(2 * j + 1) * SUB, :] = jnp.where(keep, pw[sh][0], 0.0)
                ref[(2 * j + 1) * SUB:(2 * j + 2) * SUB, :] = jnp.where(keep, sign * pw[sh][1], 0.0)
            p_r, p_i = jnp.zeros((SUB, GN), F32), jnp.zeros((SUB, GN), F32)
            for i in range(SUB):
                k = SUB - i if reverse else i + 1
                p_r = jnp.where(row == i, pw[k][0], p_r)
                p_i = jnp.where(row == i, sign * pw[k][1], p_i)
            ref[6 * SUB:7 * SUB, :] = p_r
            ref[7 * SUB:8 * SUB, :] = p_i

    dense_shapes = [(b.shape[1] // (LANE // b.shape[0] * c), LANE, LANE // b.shape[0] * c)
                    for b, c in zip(by_rows, block_cols)]
    outs = _pallas_call(
        body, name="prepare", grid=(1,), in_specs=[_full(b.shape) for b in by_rows] + [_full((1, GN))] * 2,
        out_specs=[_full(s) for s in dense_shapes] + [_full((8 * SUB, GN))] * 2,
        out_shape=[_sds(s, BF) for s in dense_shapes] + [_sds((8 * SUB, GN))] * 2,
        compiler_params=_params(48),
    )(*by_rows, ar, ai)
    return outs[:n], outs[n], outs[n + 1]


def _local_step(x, p, tgt, w, comm):
    rows_of = lambda a: a.reshape(NCHIP * a.shape[1], a.shape[2])
    quarters = lambda a: a.reshape(NCHIP, a.shape[0] // NCHIP, a.shape[1])

    def gathering(names, call):
        carry = comm.gather(names)
        outs = list(call(carry))
        own = len(outs) - len(carry.out_shapes)
        w.update(zip(names, outs[own:]))
        return outs[:own]

    w.update(comm.first())
    w_glu = rows_of(w["w_glu"])
    ar, ai, bbr, bbi = _s5_discretize(w["lam_re"], w["lam_im"], w["log_dt"], w["s5_b_re"], w["s5_b_im"])
    by_row = lambda b: jnp.transpose(b, (1, 0, 2)).reshape(b.shape[1], -1)
    (bbr_d, bbi_d, ccr_d, cci_d, wr_d, wi_d), con, con_rev = _prepare(
        [by_row(b) for b in (bbr, bbi, w["s5_c_re"], w["s5_c_im"], w["w_r"], w["w_i"])], [NS] * 4 + [HD] * 2,
        ar.reshape(1, GN), ai.reshape(1, GN))
    dsk = w["s5_d"].reshape(1, S5W)
    lam = w["lru_lambda"].reshape(1, LW)
    sp = jax.nn.softplus(-lam)
    b_r, b_i = w["b_r"].reshape(1, LW), w["b_i"].reshape(1, LW)
    row = lambda name: w[name].reshape(1, -1)

    h, ua, ub, gp = gathering(["w_a_out", "w_b_out"], lambda carry: _inproj_fwd(
        x, row("g_mix"), w["w_in"], row("b_in"), carry))
    sr, si, y, zg, ya = gathering(["w_o", "w_ffn_gate"], lambda carry: _s5_fwd(
        ua, bbr_d, bbi_d, ccr_d, cci_d, dsk, con, w_glu, row("b_glu"), carry))
    xc, rg, ig, yb, hp = gathering(["w_ffn_up"], lambda carry: _lru_fwd(
        ub, w["conv_w"], row("conv_b"), wr_d, wi_d, b_r, b_i, sp, carry))
    w_b_out, w_o = rows_of(w["w_b_out"]), rows_of(w["w_o"])
    x1, pa, pb, merged = gathering(["w_ffn_down"], lambda carry: _merge_fwd(
        x, ya, yb, gp, w["w_a_out"], w_b_out, w_o, carry))
    x2, h2, gg, uu = gathering(["w_ple_gate", "w_ple"], lambda carry: _ffn_fwd(
        x1, row("g_ffn"), w["w_ffn_gate"], w["w_ffn_up"], w["w_ffn_down"], carry))
    w_pg = rows_of(w["w_ple_gate"])
    dx2, n2, dpre, de0, acc_p = _ple_loss(x2, p, tgt, row("g_ple_gate"), w_pg, row("b_ple_gate"),
                                          w["w_ple"], row("g_ple"), row("g_final"))
    comm.reduce("ple", {"w_ple_gate": quarters(_tn("dw_ple_gate", n2, dpre)),
                        "w_ple": _tn("dw_ple", p, de0, col_chunk=AC)})
    dx1, act, dgg, duu, acc_f = comm.run(lambda carry: _ffn_bwd(
        x1, dx2, gg, uu, row("g_ffn"), w["w_ffn_gate"], w["w_ffn_up"], w["w_ffn_down"], carry))
    comm.reduce("ffn_gate", {"w_ffn_gate": _tn("dw_ffn_gate", dgg, h2)})
    comm.reduce("w_o", {"w_o": quarters(_tn("dw_o", merged, dx1))})
    comm.reduce("ffn_up", {"w_ffn_up": comm.run(lambda carry: _tn("dw_ffn_up", duu, h2, carry=carry))[0]})
    comm.reduce("ffn_down", {"w_ffn_down": comm.run(lambda carry: _tn("dw_ffn_down", act, dx2, carry=carry),
                                                    hold=("ffn_gate", "w_o"))[0]})
    dya, dyb, dgp, dpa, dpb = comm.run(lambda carry: _merge_bwd(
        dx1, gp, pa, pb, w["w_a_out"], w_b_out, w_o, carry), hold=("ffn_gate", "ffn_up"))
    comm.reduce("merge", {"w_a_out": _tn("dw_a_out", ya, dpa, col_chunk=AC), "w_b_out": quarters(_tn("dw_b_out", yb, dpb))})
    dua, dq, dy, lr, li, acc_a, acc_s = comm.run(lambda carry: _s5_bwd(
        dya, y, ua, sr, si, bbr_d, bbi_d, ccr_d, cci_d, dsk, con_rev, w_glu, row("b_glu"), carry), hold=("ffn_down",))
    dub, dpr, dpi, acc_l = comm.run(lambda carry: _lru_bwd(
        dyb, xc, rg, ig, hp, ub, w["conv_w"], wr_d, wi_d, sp, -_sig(-lam), carry))
    gx, dz, acc_g, acc_b = _inproj_bwd(x, dx1, dua, dub, dgp, row("g_mix"), w["w_in"])
    half = (D // 2,)
    comm.reduce("in_lo", {"w_in_lo": comm.run(lambda carry: _tn(
        "dw_in_lo", h, dz, col_chunk=QC, a_block=(0,) + half, carry=carry))[0]})
    comm.reduce("in_hi", {"w_in_hi": comm.run(lambda carry: _tn(
        "dw_in_hi", h, dz, col_chunk=QC, a_block=(1,) + half, carry=carry))[0], "w_glu": quarters(_tn("dw_glu", zg, dq))})
    d_wr, d_wi = comm.run(lambda carry: _tn_blocks("dw_r_i", xc, [dpr, dpi], HD, HD, carry))
    d_bbr, d_bbi = comm.run(lambda carry: _tn_blocks("d_bb", ua, [lr, li], NP, NS, carry))
    d_ccr, d_cci = comm.run(lambda carry: _tn_blocks("d_cc", dy, [sr, si], NP, NS, carry))
    comm.drain()
    sums = {"ple": acc_p, "ffn": acc_f, "mix": acc_g, "b_in": acc_b, "lru": acc_l, "s5": acc_s, "s5_a": acc_a}
    blocks = {"bb_re": d_bbr, "bb_im": d_bbi,
              "cc_re": d_ccr, "cc_im": d_cci,
              "w_r": d_wr, "w_i": d_wi}
    return gx, sums, blocks


def _replicated_grads(w, sums, blocks):
    grouped = lambda e, groups: jnp.transpose(e.reshape(e.shape[0], groups, -1), (1, 0, 2))
    d_ar, d_ai = sums["s5_a"][0].reshape(NG, NS), sums["s5_a"][1].reshape(NG, NS)
    d_bbr, d_bbi = grouped(blocks["bb_re"], NG), grouped(blocks["bb_im"], NG)
    _, vjp = jax.vjp(_s5_discretize, w["lam_re"], w["lam_im"], w["log_dt"], w["s5_b_re"], w["s5_b_im"])
    g = dict(zip(("lam_re", "lam_im", "log_dt", "s5_b_re", "s5_b_im"), vjp((d_ar, d_ai, d_bbr, d_bbi))))
    g["s5_c_re"] = grouped(blocks["cc_re"], NG)
    g["s5_c_im"] = -grouped(blocks["cc_im"], NG)
    g["w_r"], g["w_i"] = grouped(blocks["w_r"], NH), grouped(blocks["w_i"], NH)
    g["s5_d"] = sums["s5"][0].reshape(NG, NP)
    g["b_r"] = sums["lru"][1].reshape(NH, HD)
    g["b_i"] = sums["lru"][2].reshape(NH, HD)
    return g


ACC_ROWS = {"g_mix": ("mix", 0), "b_in": ("b_in", 0), "g_ffn": ("ffn", 0), "g_ple_gate": ("ple", 0),
            "b_ple_gate": ("ple", 1), "g_ple": ("ple", 2), "g_final": ("ple", 3), "b_glu": ("s5", 1),
            "lru_lambda": ("lru", 0), "conv_b": ("lru", 3)}
LOSS_ROW = ("ple", 4)
CONV_W_ROWS = ("lru", 4)


SHARDED = [("w_in", (D, QC)), ("w_glu", (S5W // NCHIP, S5W)), ("w_a_out", (S5W, AC)), ("w_b_out", (LW // NCHIP, D)),
           ("w_o", (D // NCHIP, D)), ("w_ffn_gate", (FC, D)), ("w_ffn_up", (FC, D)), ("w_ffn_down", (FC, D)),
           ("w_ple_gate", (D // NCHIP, D)), ("w_ple", (PLE, AC))]
NSH = len(SHARDED)
TRANSPOSED = ("w_ffn_gate", "w_ffn_up", "s5_b_re", "s5_b_im")
CONV_SHARD = (4, LW // NCHIP)


def _mesh_pos():
    return lax.axis_index("x"), lax.axis_index("y"), lax.axis_index("c")


def _other_chips(x, y):
    return [(1 - x, y), (x, 1 - y), (1 - x, 1 - y)]


def _half_rows(c, rows, align):
    return pl.ds(pl.multiple_of(c * (rows // 2), align), rows // 2)


def _run_now(name, carry):
    c_in, c_out = len(carry.operands), len(carry.out_shapes)

    def body(*refs):
        ins, outs, sems = refs[:c_in], refs[c_in:c_in + c_out], refs[c_in + c_out:]
        carry.start(ins, outs, sems)
        carry.finish(ins, outs, sems)

    return pl.pallas_call(body, name=name, in_specs=[ANY] * c_in, out_specs=[ANY] * c_out,
                          out_shape=list(carry.out_shapes), scratch_shapes=list(carry.sems),
                          input_output_aliases=dict(carry.aliases))(*_in_hbm(carry.operands))


def _gather_group(shards, split):
    n = len(shards)

    def copies(srcs, outs, sems):
        send_sems, recv_sems = sems
        x, y, c = _mesh_pos()
        k0 = 2 * x + y
        sib = (x, y, 1 - c)
        chips = _other_chips(x, y)

        def remote(src, dst, j, i, to):
            return pltpu.make_async_remote_copy(src_ref=src, dst_ref=dst, send_sem=send_sems.at[j, i],
                                                recv_sem=recv_sems.at[j, i], device_id=to, device_id_type=MESH)

        def rows(ref, i, core, *lead):
            if not split[i]:
                return ref.at[lead] if lead else ref
            return ref.at[(*lead, _half_rows(core, shards[i].shape[0], 16))]

        own = [remote(s, o.at[k0], 6, i, sib) for i, (s, o) in enumerate(zip(srcs, outs))]
        ici, landed, fwd, fwd_landed = [], [], [], []
        for j, chip in enumerate(chips):
            kj = 2 * chip[0] + chip[1]
            pairs = list(enumerate(zip(srcs, outs)))
            ici.append([remote(rows(s, i, c), rows(o, i, c, k0), j, i, (*chip, c)) for i, (s, o) in pairs])
            landed.append([remote(rows(s, i, c), rows(o, i, c, kj), j, i, (*chip, c)) for i, (s, o) in pairs])
            fwd.append([remote(rows(o, i, c, kj), rows(o, i, c, kj), 3 + j, i, sib) for i, (s, o) in pairs if split[i]])
            fwd_landed.append([remote(rows(o, i, 1 - c, kj), rows(o, i, 1 - c, kj), 3 + j, i, sib)
                               for i, (s, o) in pairs if split[i]])
        return own, ici, landed, fwd, fwd_landed

    def start(srcs, outs, sems):
        own, ici, _, _, _ = copies(srcs, outs, sems)
        for cp in own + [cp for per_chip in ici for cp in per_chip]:
            cp.start()

    def finish(srcs, outs, sems):
        own, ici, landed, fwd, fwd_landed = copies(srcs, outs, sems)
        passed = [i for i in range(n) if split[i]]
        for j in range(3):
            for i, cp in enumerate(landed[j]):
                cp.wait_recv()
                if split[i]:
                    fwd[j][passed.index(i)].start()
        for j in range(3):
            for cp in fwd_landed[j]:
                cp.wait_recv()
        for cp in own:
            cp.wait_recv()
        for cp in own + [cp for per_chip in ici + fwd for cp in per_chip]:
            cp.wait_send()

    return _Carried(shards, [_sds((NCHIP,) + s.shape, s.dtype) for s in shards],
                    [pltpu.SemaphoreType.DMA((7, n)), pltpu.SemaphoreType.DMA((7, n))], start, finish)


def _each_copy(copies, carried, out_shapes, sems, aliases=None):
    def start(ins, outs, sem_refs):
        for cp in copies(ins, outs, sem_refs):
            cp.start()

    def finish(ins, outs, sem_refs):
        for cp in copies(ins, outs, sem_refs):
            cp.wait()

    return _Carried(carried, out_shapes, sems, start, finish, aliases)


def _swap_group(grads):
    n = len(grads)

    def copies(srcs, outs, sems):
        send_sems, recv_sems = sems
        x, y, c = _mesh_pos()
        return [pltpu.make_async_remote_copy(src_ref=s.at[:, _half_rows(1 - c, s.shape[1], 8)], dst_ref=o,
                                             send_sem=send_sems.at[i], recv_sem=recv_sems.at[i], device_id=(x, y, 1 - c),
                                             device_id_type=MESH) for i, (s, o) in enumerate(zip(srcs, outs))]

    return _each_copy(copies, grads, [pltpu.HBM((NCHIP, g.shape[1] // 2, g.shape[2]), F32) for g in grads],
                      [pltpu.SemaphoreType.DMA((n,)), pltpu.SemaphoreType.DMA((n,))])


def _add_sibling_group(tag, kc_idx, grads, gots):
    n = len(grads)

    def body(kc_ref, *refs):
        for g, rx, p, pb in zip(refs[:n], refs[n:2 * n], refs[2 * n:3 * n], refs[3 * n:]):
            s = g[...] + rx[...]
            pb[...] = s.astype(BF)

            @pl.when(pl.program_id(0) == kc_ref[0])
            def _():
                p[...] = s

    halves = [pl.BlockSpec((None,) + rx.shape[1:], lambda k, kc_ref: (k, 0, 0)) for rx in gots]
    mine = [pl.BlockSpec((None,) + rx.shape[1:], lambda k, kc_ref: (k, kc_ref[1], 0)) for rx in gots]
    own = [pl.BlockSpec(rx.shape[1:], lambda k, kc_ref: (0, 0)) for rx in gots]
    outs = _pallas_call(
        body, name="add_sibling_" + tag,
        grid_spec=pltpu.PrefetchScalarGridSpec(num_scalar_prefetch=1, grid=(NCHIP,), in_specs=mine + halves,
                                               out_specs=own + halves),
        out_shape=[pltpu.HBM(rx.shape[1:], F32) for rx in gots] + [pltpu.HBM(rx.shape, BF) for rx in gots],
        compiler_params=_params(48),
    )(kc_idx, *_in_hbm(list(grads) + list(gots)))
    return outs[:n], outs[n:]


def _exchange_group(parts):
    n = len(parts)

    def copies(srcs, outs, sems):
        send_sems, recv_sems = sems
        x, y, c = _mesh_pos()
        return [pltpu.make_async_remote_copy(
            src_ref=s.at[2 * chip[0] + chip[1]], dst_ref=o.at[j], send_sem=send_sems.at[j, i],
            recv_sem=recv_sems.at[j, i], device_id=(*chip, c), device_id_type=MESH)
            for j, chip in enumerate(_other_chips(x, y)) for i, (s, o) in enumerate(zip(srcs, outs))]

    return _each_copy(copies, parts, [pltpu.HBM((3,) + p.shape[1:], BF) for p in parts],
                      [pltpu.SemaphoreType.DMA((3, n)), pltpu.SemaphoreType.DMA((3, n))])


def _add_chips_group(tag, kc_idx, parts, arrived):
    n = len(parts)

    def body(kc_ref, *refs):
        for p, rx, t in zip(refs[:n], refs[n:2 * n], refs[2 * n:]):
            t[...] = ((p[...] + rx[0].astype(F32)) + rx[1].astype(F32)) + rx[2].astype(F32)

    outs = _pallas_call(
        body, name="add_chips_" + tag,
        grid_spec=pltpu.PrefetchScalarGridSpec(
            num_scalar_prefetch=1, grid=(1,),
            in_specs=([pl.BlockSpec(rx.shape[1:], lambda i, kc_ref: (0, 0)) for rx in arrived]
                      + [pl.BlockSpec(rx.shape, lambda i, kc_ref: (0, 0, 0)) for rx in arrived]),
            out_specs=[pl.BlockSpec((None,) + rx.shape[1:], lambda i, kc_ref: (kc_ref[1], 0, 0)) for rx in arrived]),
        out_shape=[pltpu.HBM((2,) + rx.shape[1:], F32) for rx in arrived],
        compiler_params=_params(48),
    )(kc_idx, *_in_hbm(list(parts) + list(arrived)))
    return list(outs)


def _join_group(halves):
    n = len(halves)

    def copies(bufs, sems):
        send_sems, recv_sems = sems
        x, y, c = _mesh_pos()
        sib = (x, y, 1 - c)
        sends = [pltpu.make_async_remote_copy(src_ref=b.at[c], dst_ref=b.at[c], send_sem=send_sems.at[i],
                                              recv_sem=recv_sems.at[i], device_id=sib, device_id_type=MESH)
                 for i, b in enumerate(bufs)]
        landed = [pltpu.make_async_remote_copy(src_ref=b.at[c], dst_ref=b.at[1 - c], send_sem=send_sems.at[i],
                                               recv_sem=recv_sems.at[i], device_id=sib, device_id_type=MESH)
                  for i, b in enumerate(bufs)]
        return sends, landed

    def start(_, bufs, sems):
        for cp in copies(bufs, sems)[0]:
            cp.start()

    def finish(_, bufs, sems):
        sends, landed = copies(bufs, sems)
        for cp in landed:
            cp.wait_recv()
        for cp in sends:
            cp.wait_send()

    return _Carried(halves, [pltpu.HBM(h.shape, F32) for h in halves],
                    [pltpu.SemaphoreType.DMA((n,)), pltpu.SemaphoreType.DMA((n,))], start, finish,
                    {i: i for i in range(n)})


def _combine(carries):
    operands, out_shapes, sems, aliases, spans = [], [], [], {}, []
    for c in carries:
        aliases.update({len(operands) + i: len(out_shapes) + o for i, o in c.aliases.items()})
        spans.append((len(operands), len(out_shapes), len(sems)))
        operands += list(c.operands)
        out_shapes += list(c.out_shapes)
        sems += list(c.sems)

    def each(phase):
        def run(ins, outs, sem_refs):
            for c, (a, b, s) in zip(carries, spans):
                getattr(c, phase)(ins[a:a + len(c.operands)], outs[b:b + len(c.out_shapes)], sem_refs[s:s + len(c.sems)])
        return run

    return _Carried(operands, out_shapes, sems, each("start"), each("finish"), aliases)


def _allreduce_small(arrays, wire):
    n = len(arrays)
    halves = [(a.shape[0], a.shape[1] // 2) for a in arrays]

    def body(*refs):
        srcs, outs = refs[:n], refs[n:2 * n]
        mine_bufs, sib_bufs, chip_bufs, total_bufs = (refs[k * n:(k + 1) * n] for k in range(2, 6))
        send_sems, recv_sems, local_sems = refs[6 * n:]
        x, y, c = _mesh_pos()
        k0 = 2 * x + y
        sib = (x, y, 1 - c)

        def remote(src, dst, j, i, to):
            return pltpu.make_async_remote_copy(src_ref=src, dst_ref=dst, send_sem=send_sems.at[j, i],
                                                recv_sem=recv_sems.at[j, i], device_id=to, device_id_type=MESH)

        def cols(ref, i, core):
            return ref.at[:, pl.ds(pl.multiple_of(core * halves[i][1], LANE), halves[i][1])]

        swaps = [remote(cols(s, i, 1 - c), b, 0, i, sib) for i, (s, b) in enumerate(zip(srcs, sib_bufs))]
        own = [pltpu.make_async_copy(cols(s, i, c), m, local_sems.at[i]) for i, (s, m) in enumerate(zip(srcs, mine_bufs))]
        for cp in swaps + own:
            cp.start()
        for cp in swaps + own:
            cp.wait()
        for m, b, buf in zip(mine_bufs, sib_bufs, chip_bufs):
            buf[k0] = (m[...] + b[...]).astype(buf.dtype)
        chips = _other_chips(x, y)
        sends = [remote(buf.at[k0], buf.at[k0], 1 + j, i, (*chip, c))
                 for j, chip in enumerate(chips) for i, buf in enumerate(chip_bufs)]
        for cp in sends:
            cp.start()
        for j, chip in enumerate(chips):
            for i, buf in enumerate(chip_bufs):
                remote(buf.at[k0], buf.at[2 * chip[0] + chip[1]], 1 + j, i, (*chip, c)).wait_recv()
        for cp in sends:
            cp.wait_send()
        for t, buf in zip(total_bufs, chip_bufs):
            t[...] = ((buf[0].astype(F32) + buf[1].astype(F32)) + buf[2].astype(F32)) + buf[3].astype(F32)
        joins = [remote(t, cols(o, i, c), 4, i, sib) for i, (t, o) in enumerate(zip(total_bufs, outs))]
        keep = [pltpu.make_async_copy(t, cols(o, i, c), local_sems.at[i]) for i, (t, o) in enumerate(zip(total_bufs, outs))]
        for cp in joins + keep:
            cp.start()
        for i, (t, o) in enumerate(zip(total_bufs, outs)):
            remote(t, cols(o, i, 1 - c), 4, i, sib).wait_recv()
        for cp in joins:
            cp.wait_send()
        for cp in keep:
            cp.wait()

    specs = [_full(a.shape) for a in arrays]
    return _pallas_call(
        body, name="allreduce_small", grid=(1,), in_specs=specs, out_specs=specs,
        out_shape=[_sds(a.shape) for a in arrays],
        scratch_shapes=([pltpu.VMEM(h, F32) for h in halves] + [pltpu.VMEM(h, F32) for h in halves]
                        + [pltpu.VMEM((NCHIP,) + h, dt) for h, dt in zip(halves, wire)] + [pltpu.VMEM(h, F32) for h in halves]
                        + [pltpu.SemaphoreType.DMA((5, n)), pltpu.SemaphoreType.DMA((5, n)), pltpu.SemaphoreType.DMA((n,))]),
        compiler_params=_params(32),
    )(*arrays)


def _adamw_terms(w, g, m, v):
    m = ADAM_B1 * m + (1.0 - ADAM_B1) * g
    v = ADAM_B2 * v + (1.0 - ADAM_B2) * jnp.square(g)
    m_hat = m / (1.0 - ADAM_B1 ** ADAM_STEP)
    v_hat = v / (1.0 - ADAM_B2 ** ADAM_STEP)
    return -ADAM_LR * (m_hat / (jnp.sqrt(v_hat) + ADAM_EPS) + ADAM_WD * w), m, v


ADAM_STEPS = 4


def _adamw_group(tag, ws, gs, ms, vs):
    n = len(ws)

    def body(*refs):
        ins, outs = refs[:4 * n], refs[4 * n:]
        for i in range(n):
            w, g, m, v = (ins[k * n + i][...] for k in range(4))
            outs[i][...] = g
            outs[n + i][...], outs[2 * n + i][...], outs[3 * n + i][...] = _adamw_terms(w, g, m, v)

    specs = [pl.BlockSpec((w.shape[0] // ADAM_STEPS, w.shape[1]), lambda i: (i, 0)) for w in ws]
    outs = _pallas_call(
        body, name="adamw_" + tag, grid=(ADAM_STEPS,), in_specs=specs * 4, out_specs=specs * 4,
        out_shape=[_sds(w.shape) for w in ws] * 4, compiler_params=_params(48),
    )(*_in_hbm(list(ws) + list(gs) + list(ms) + list(vs)))
    return outs[:n], outs[n:2 * n], outs[2 * n:3 * n], outs[3 * n:]


def _adamw_replicated(sums, row_of, direct):
    ns, nr, nd = len(sums), len(row_of), len(direct)

    def body(*refs):
        sum_refs = refs[:ns]
        ins = refs[ns:ns + 3 * nr + 4 * nd]
        outs = refs[ns + 3 * nr + 4 * nd:]
        for i, (_, _, _, si, row) in enumerate(row_of):
            w_ref, m_ref, v_ref = ins[3 * i:3 * i + 3]
            g = sum_refs[si][row:row + 1, :]
            outs[4 * i][...] = g
            outs[4 * i + 1][...], outs[4 * i + 2][...], outs[4 * i + 3][...] = _adamw_terms(w_ref[...], g, m_ref[...], v_ref[...])
        for i in range(nd):
            w_ref, m_ref, v_ref, g_ref = ins[3 * nr + 4 * i:3 * nr + 4 * i + 4]
            o = outs[4 * (nr + i):4 * (nr + i) + 4]
            g = g_ref[...]
            o[0][...] = g
            o[1][...], o[2][...], o[3][...] = _adamw_terms(w_ref[...], g, m_ref[...], v_ref[...])

    operands = list(sums)
    shapes = []
    for w, m, v, _, _ in row_of:
        operands += [w, m, v]
        shapes += [w.shape] * 4
    for w, m, v, g in direct:
        operands += [w, m, v, g]
        shapes += [w.shape] * 4
    flat = _pallas_call(
        body, name="adamw_replicated", grid=(1,), in_specs=[_full(a.shape) for a in operands],
        out_specs=[_full(s) for s in shapes], out_shape=[_sds(s) for s in shapes],
        compiler_params=_params(56),
    )(*operands)
    return [flat[4 * i:4 * i + 4] for i in range(nr + nd)]


class _Exchanges:
    def __init__(self, shards, conv_w, chip, core, apply):
        self.shards, self.conv_w, self.apply = shards, conv_w, apply
        self.active, self.calls = [], 0
        self.core_idx = jnp.reshape(core, (1,)).astype(jnp.int32)
        self.chip_core_idx = jnp.stack([chip, core]).astype(jnp.int32)

    def first(self):
        names = ["w_in", "w_glu"]
        got = _run_now("gather_first", _gather_group([self.shards[n] for n in names] + [self.conv_w],
                                                     [True, True, False]))
        out = dict(zip(names, got))
        out["conv_w"] = jnp.transpose(got[2], (1, 0, 2)).reshape(4, LW)
        return out

    def gather(self, names):
        return _gather_group([self.shards[n] for n in names], [True] * len(names))

    def reduce(self, tag, grads):
        self.active.append({"tag": tag, "names": list(grads), "stage": 0, "grads": list(grads.values())})

    def run(self, call, hold=()):
        groups = [g for g in self.active if g["tag"] not in hold]
        carries = [self._exchange_of(g) for g in groups]
        carry = _combine(carries)
        outs = list(call(carry))
        own = len(outs) - len(carry.out_shapes)
        landed = outs[own:]
        for g, c in zip(groups, carries):
            self._sum_after(g, landed[:len(c.out_shapes)])
            landed = landed[len(c.out_shapes):]
        self.active = [g for g in self.active if g["stage"] < 3]
        return outs[:own]

    def _exchange_of(self, g):
        if g["stage"] == 0:
            return _swap_group(g["grads"])
        if g["stage"] == 1:
            return _exchange_group(g["bf16"])
        return _join_group(g["halves"])

    def _sum_after(self, g, landed):
        if g["stage"] == 0:
            g["f32"], g["bf16"] = _add_sibling_group(g["tag"], self.chip_core_idx, g["grads"], landed)
        elif g["stage"] == 1:
            g["halves"] = _add_chips_group(g["tag"], self.chip_core_idx, g["f32"], landed)
        else:
            self.apply(g["tag"], g["names"], [t.reshape(2 * t.shape[1], t.shape[2]) for t in landed])
        g["stage"] += 1

    def drain(self):
        while self.active:
            self.calls += 1
            self.run(lambda carry: _run_now("reduce_%d" % self.calls, carry))


INPUT_NAMES = (["x", "p"] + [n for n in
               ["g_mix", "w_in", "b_in", "lam_re", "lam_im", "log_dt", "s5_b_re", "s5_b_im", "s5_c_re", "s5_c_im", "s5_d",
                "w_glu", "b_glu", "conv_w", "conv_b", "w_r", "b_r", "w_i", "b_i", "lru_lambda", "w_a_out", "w_b_out", "w_o",
                "g_ffn", "w_ffn_gate", "w_ffn_up", "w_ffn_down", "g_ple_gate", "w_ple_gate", "b_ple_gate", "w_ple", "g_ple",
                "g_final"]])
WEIGHT_NAMES = INPUT_NAMES[2:]


def kernel(*args):
    names = INPUT_NAMES + ["loss_target"] + ["m_" + n for n in WEIGHT_NAMES] + ["v_" + n for n in WEIGHT_NAMES]
    assert len(args) == len(names)
    given = dict(zip(names, args))

    def view(name):
        a = given[name]
        return jnp.swapaxes(a, -1, -2) if name.endswith(TRANSPOSED) else a

    def unview(name, a):
        return jnp.swapaxes(a, -1, -2) if name in TRANSPOSED else a

    def local(name):
        return view(name) if name.endswith("g_final") else view(name)[0]

    xi, yi, ci = _mesh_pos()
    k0 = 2 * xi + yi
    x, p, tgt = given["x"][0], given["p"][0, 0], given["loss_target"][0]

    results = {}

    row_halves = {}

    def apply(tag, names, totals):
        totals = dict(zip(names, totals))
        row_halves.update({n: totals.pop(n) for n in names if n in ("w_in_lo", "w_in_hi")})
        if len(row_halves) == 2:
            totals["w_in"] = jnp.concatenate([row_halves.pop("w_in_lo"), row_halves.pop("w_in_hi")])
        names = list(totals)
        if not names:
            return
        new = _adamw_group(tag, [local(n) for n in names], list(totals.values()), [local("m_" + n) for n in names],
                           [local("v_" + n) for n in names])
        for kind, arrays in zip(("grad", "delta", "new_m", "new_v"), new):
            for n, arr in zip(names, arrays):
                results[kind, n] = unview(n, arr[None])

    comm = _Exchanges({n: local(n).astype(BF) for n, _ in SHARDED}, local("conv_w"), k0, ci, apply)
    w = {n: local(n) for n in WEIGHT_NAMES if n != "conv_w" and n not in dict(SHARDED)}
    gx, sums, blocks = _local_step(x, p, tgt, w, comm)

    sum_names, block_names = list(sums), list(blocks)
    red = _allreduce_small([sums[n] for n in sum_names] + [blocks[n] for n in block_names],
                           [F32] * len(sum_names) + [BF] * len(block_names))
    sums = dict(zip(sum_names, red[:len(sum_names)]))
    blocks = dict(zip(block_names, red[len(sum_names):]))
    loss = jnp.sum(sums[LOSS_ROW[0]][LOSS_ROW[1]])
    direct_g = _replicated_grads(w, sums, blocks)
    conv_rows = sums[CONV_W_ROWS[0]][CONV_W_ROWS[1]:CONV_W_ROWS[1] + 4]
    direct_g["conv_w"] = lax.dynamic_slice(conv_rows, (0, k0 * CONV_SHARD[1]), CONV_SHARD)
    as_row = lambda a: a.reshape(1, -1)
    row_names = list(ACC_ROWS)
    row_of = [(as_row(given[n]), as_row(given["m_" + n]), as_row(given["v_" + n]),
               sum_names.index(ACC_ROWS[n][0]), ACC_ROWS[n][1]) for n in row_names]
    direct_names = list(direct_g)
    direct = [(view(n), view("m_" + n), view("v_" + n), direct_g[n].reshape(view(n).shape)) for n in direct_names]
    done = _adamw_replicated([sums[n] for n in sum_names], row_of, direct)
    for n, four in zip(row_names + direct_names, done):
        for kind, arr in zip(("grad", "delta", "new_m", "new_v"), four):
            results[kind, n] = unview(n, arr).reshape(given[n].shape)

    out = [loss, gx[None]]
    for kind in ("grad", "delta", "new_m", "new_v"):
        out += [results[kind, n] for n in WEIGHT_NAMES]
    return tuple(out)
```

```python
import functools
import math

import jax
import jax.numpy as jnp
from jax import lax
from jax.experimental import pallas as pl
from jax.experimental.pallas import tpu as pltpu

F32 = jnp.float32
BF = jnp.bfloat16

D = 1024
S5W = 512
NG, NS, NP = 32, 64, 16
GN = NG * NS
LW = 1024
NH, HD = 16, 64
LRU_C = 8.0
FH = 2816
NCHIP = 4
FC = FH // NCHIP
PLE = 256
INC = S5W + LW + 2 * D
EPS = 1e-6
ADAM_LR, ADAM_B1, ADAM_B2, ADAM_EPS, ADAM_WD, ADAM_STEP = 0.001, 0.9, 0.999, 1e-08, 0.01, 10

TM = 256
TK = 1024
LC = 512
SUB = 8
VMEM_MB = 1024 * 1024
MESH = pl.DeviceIdType.MESH
ANY = pl.BlockSpec(memory_space=pl.ANY)


def _mm(a, b):
    return jnp.dot(a.astype(BF), b.astype(BF), preferred_element_type=F32)


def _mm_nt(a, b):
    return lax.dot_general(a.astype(BF), b.astype(BF), (((1,), (1,)), ((), ())), preferred_element_type=F32)


def _mm_tn(a, b):
    return lax.dot_general(a.astype(BF), b.astype(BF), (((0,), (0,)), ((), ())), preferred_element_type=F32)


def _blockdiag_mm(x, blocks_ref):
    n, rows, _ = blocks_ref.shape
    return jnp.concatenate([jnp.dot(x[:, j * rows:(j + 1) * rows], blocks_ref[j], preferred_element_type=F32)
                            for j in range(n)], axis=1)


def _blockdiag_mm_t(x, blocks_ref):
    n, _, wide = blocks_ref.shape
    return jnp.concatenate([lax.dot_general(x[:, j * wide:(j + 1) * wide], blocks_ref[j], (((1,), (1,)), ((), ())),
                                            preferred_element_type=F32) for j in range(n)], axis=1)


def _rms(x):
    r = lax.rsqrt(jnp.mean(x * x, axis=-1, keepdims=True) + EPS)
    return x * r, r


def _rms_bwd(dy, xh, r, g):
    dxh = dy * g
    return r * (dxh - xh * jnp.mean(dxh * xh, axis=-1, keepdims=True))


def _colsum(x):
    return jnp.sum(x, axis=0, keepdims=True)


def _sig(x):
    return jax.nn.sigmoid(x)


def _gelu_grad(x):
    c = math.sqrt(2.0 / math.pi)
    t = jnp.tanh(c * (x + 0.044715 * x * x * x))
    return 0.5 * (1.0 + t) + 0.5 * x * (1.0 - t * t) * c * (1.0 + 3.0 * 0.044715 * x * x)


def _neg_expm1(x):
    series = -x * (1.0 + x * (0.5 + x * (1.0 / 6.0 + x * (1.0 / 24.0))))
    return jnp.where(x > -0.03, series, 1.0 - jnp.exp(x))


def _tok(width):
    return pl.BlockSpec((TM, width), lambda i: (i, 0))


def _tok_rev(width, nt):
    return pl.BlockSpec((TM, width), lambda i: (nt - 1 - i, 0))


def _full(shape):
    return pl.BlockSpec(shape, lambda i: (0,) * len(shape))


def _params(vmem_mb, **kw):
    return pltpu.CompilerParams(dimension_semantics=("arbitrary",), vmem_limit_bytes=vmem_mb * VMEM_MB, **kw)


def _sds(shape, dtype=F32):
    return jax.ShapeDtypeStruct(shape, dtype)


class _Carried:
    def __init__(self, operands, out_shapes, sems, start, finish, aliases=None):
        self.operands, self.out_shapes, self.sems = list(operands), list(out_shapes), list(sems)
        self.start, self.finish, self.aliases = start, finish, dict(aliases or {})


def _in_hbm(arrays):
    return [pltpu.with_memory_space_constraint(a, pltpu.HBM) for a in arrays]


def _pallas_call(body, carry=None, **kw):
    if carry is None:
        return pl.pallas_call(body, **kw)

    def at_step(corner):
        hit = [pl.program_id(d) == (size - 1 if corner else 0) for d, size in enumerate(kw["grid"])]
        return functools.reduce(jnp.logical_and, hit)

    name, grid, compiler_params = kw["name"], kw["grid"], kw["compiler_params"]
    in_specs, out_specs, out_shape = list(kw["in_specs"]), list(kw["out_specs"]), list(kw["out_shape"])
    scratch_shapes = list(kw.get("scratch_shapes", ()))
    n_in, n_out, n_scr = len(in_specs), len(out_specs), len(scratch_shapes)
    c_in, c_out = len(carry.operands), len(carry.out_shapes)

    def full_body(*refs):
        ins, refs = refs[:n_in], refs[n_in:]
        c_ins, refs = refs[:c_in], refs[c_in:]
        outs, refs = refs[:n_out], refs[n_out:]
        c_outs, refs = refs[:c_out], refs[c_out:]
        scratch, c_sems = refs[:n_scr], refs[n_scr:]

        @pl.when(at_step(0))
        def _():
            carry.start(c_ins, c_outs, c_sems)

        body(*ins, *outs, *scratch)

        @pl.when(at_step(1))
        def _():
            carry.finish(c_ins, c_outs, c_sems)

    call = pl.pallas_call(
        full_body, name=name, grid=grid, in_specs=in_specs + [ANY] * c_in, out_specs=out_specs + [ANY] * c_out,
        out_shape=out_shape + list(carry.out_shapes), scratch_shapes=scratch_shapes + list(carry.sems),
        input_output_aliases={n_in + i: n_out + o for i, o in carry.aliases.items()},
        compiler_params=compiler_params)
    return lambda *operands: call(*operands, *_in_hbm(carry.operands))


def _resident(pairs, sems):
    first = pl.program_id(0) == 0
    copies = [pltpu.make_async_copy(src, dst, sems.at[j]) for j, (src, dst) in enumerate(pairs)]

    @pl.when(first)
    def _():
        for cp in copies:
            cp.start()

    def wait(j):
        @pl.when(first)
        def _():
            copies[j].wait()

    return wait


def _resident_now(pairs, sems):
    @pl.when(pl.program_id(0) == 0)
    def _():
        copies = [pltpu.make_async_copy(src, dst, sems.at[j]) for j, (src, dst) in enumerate(pairs)]
        for cp in copies:
            cp.start()
        for cp in copies:
            cp.wait()


def _row_iota(width):
    return lax.broadcasted_iota(jnp.int32, (SUB, width), 0)


def _bcast_row(x, row):
    return jnp.broadcast_to(x[row:row + 1, :], x.shape)


def _slab(k):
    return pl.ds(pl.multiple_of(k * SUB, SUB), SUB)


QC = INC // NCHIP
Z_PARTS = ((0, S5W), (S5W, S5W + LW), (S5W + LW, INC))


def _inproj_fwd(x, g_mix, w_in, b_in, carry=None):
    L = x.shape[0]

    def body(x_ref, g_ref, w_hbm, b_ref, h_ref, ua_ref, ub_ref, gp_ref, w_vm, w_sems):
        _resident_now([(w_hbm.at[k], w_vm.at[k]) for k in range(NCHIP)], w_sems)
        xh, _ = _rms(x_ref[...])
        h = (xh * g_ref[...]).astype(BF)
        h_ref[...] = h
        for k in range(NCHIP):
            lo, hi = k * QC, (k + 1) * QC
            z = jnp.dot(h, w_vm[k], preferred_element_type=F32) + b_ref[:, lo:hi]
            for ref, (a, b) in zip((ua_ref, ub_ref, gp_ref), Z_PARTS):
                s, e = max(lo, a), min(hi, b)
                if s < e:
                    ref[:, s - a:e - a] = z[:, s - lo:e - lo]

    return _pallas_call(
        body, carry, name="inproj_fwd", grid=(L // TM,),
        in_specs=[_tok(D), _full((1, D)), ANY, _full((1, INC))],
        out_specs=[_tok(D), _tok(S5W), _tok(LW), _tok(2 * D)],
        out_shape=[_sds((L, D), BF), _sds((L, S5W)), _sds((L, LW)), _sds((L, 2 * D))],
        scratch_shapes=[pltpu.VMEM((NCHIP, D, QC), BF), pltpu.SemaphoreType.DMA((NCHIP,))],
        compiler_params=_params(40),
    )(x, g_mix, w_in, b_in)


def _inproj_bwd(x, dx1, dua, dub, dgp, g_mix, w_in, carry=None):
    L = x.shape[0]

    def body(x_ref, dx1_ref, dua_ref, dub_ref, dgp_ref, g_ref, w_hbm, gx_ref, dz_ref, dg_ref, db_ref, w_vm, w_sems):
        _resident_now([(w_hbm.at[k], w_vm.at[k]) for k in range(NCHIP)], w_sems)

        @pl.when(pl.program_id(0) == 0)
        def _():
            dg_ref[...] = jnp.zeros_like(dg_ref)
            db_ref[...] = jnp.zeros_like(db_ref)

        for src, (a, b) in zip((dua_ref, dub_ref, dgp_ref), Z_PARTS):
            d = src[...]
            dz_ref[:, a:b] = d.astype(BF)
            db_ref[0:1, a:b] += _colsum(d)
        dh = jnp.zeros((TM, D), F32)
        for k in range(NCHIP):
            dh = dh + lax.dot_general(dz_ref[:, k * QC:(k + 1) * QC], w_vm[k], (((1,), (1,)), ((), ())),
                                      preferred_element_type=F32)
        xh, r = _rms(x_ref[...])
        dg_ref[0:1, :] += _colsum(dh * xh)
        gx_ref[...] = dx1_ref[...] + _rms_bwd(dh, xh, r, g_ref[...])

    return _pallas_call(
        body, carry, name="inproj_bwd", grid=(L // TM,),
        in_specs=[_tok(D), _tok(D), _tok(S5W), _tok(LW), _tok(2 * D), _full((1, D)), ANY],
        out_specs=[_tok(D), _tok(INC), _full((SUB, D)), _full((SUB, INC))],
        out_shape=[_sds((L, D)), _sds((L, INC), BF), _sds((SUB, D)), _sds((SUB, INC))],
        scratch_shapes=[pltpu.VMEM((NCHIP, D, QC), BF), pltpu.SemaphoreType.DMA((NCHIP,))],
        compiler_params=_params(40),
    )(x, dx1, dua, dub, dgp, g_mix, w_in)


def _cscan(xr_ref, xi_ref, con_ref, cr_ref, ci_ref, reverse):
    n_slab = xr_ref.shape[0] // SUB
    width = xr_ref.shape[1]
    for lc in range(width // LC):
        cols = slice(lc * LC, (lc + 1) * LC)
        con = [con_ref[SUB * j:SUB * (j + 1), cols] for j in range(8)]

        def step(k, carry, cols=cols, con=con):
            cr, ci = carry
            rows = _slab(n_slab - 1 - k if reverse else k)
            xr, xi = xr_ref[rows, cols], xi_ref[rows, cols]
            for j, sh in enumerate((1, 2, 4)):
                mr, mi = con[2 * j], con[2 * j + 1]
                pr = pltpu.roll(xr, SUB - sh if reverse else sh, 0)
                pi = pltpu.roll(xi, SUB - sh if reverse else sh, 0)
                xr, xi = xr + mr * pr - mi * pi, xi + mr * pi + mi * pr
            xr, xi = xr + con[6] * cr - con[7] * ci, xi + con[6] * ci + con[7] * cr
            xr_ref[rows, cols] = xr
            xi_ref[rows, cols] = xi
            row = 0 if reverse else SUB - 1
            return _bcast_row(xr, row), _bcast_row(xi, row)

        cr, ci = lax.fori_loop(0, n_slab, step, (cr_ref[:, cols], ci_ref[:, cols]))
        cr_ref[:, cols] = cr
        ci_ref[:, cols] = ci


def _s5_fwd(ua, bbr, bbi, ccr, cci, dsk, con, w_glu, b_glu, carry=None):
    L = ua.shape[0]

    def body(ua_ref, bbr_hbm, bbi_hbm, ccr_hbm, cci_hbm, dsk_ref, con_ref, wg_ref, bg_ref,
             sr_ref, si_ref, y_ref, zg_ref, ya_ref, bbr_vm, bbi_vm, ccr_vm, cci_vm, cr_ref, ci_ref, w_sems):
        landed = _resident([(bbr_hbm, bbr_vm), (bbi_hbm, bbi_vm), (ccr_hbm, ccr_vm), (cci_hbm, cci_vm)], w_sems)

        @pl.when(pl.program_id(0) == 0)
        def _():
            cr_ref[...] = jnp.zeros_like(cr_ref)
            ci_ref[...] = jnp.zeros_like(ci_ref)

        u = ua_ref[...]
        ub = u.astype(BF)
        landed(0)
        sr_ref[...] = _blockdiag_mm(ub, bbr_vm)
        landed(1)
        si_ref[...] = _blockdiag_mm(ub, bbi_vm)
        _cscan(sr_ref, si_ref, con_ref, cr_ref, ci_ref, reverse=False)
        landed(2)
        landed(3)
        y = (_blockdiag_mm_t(sr_ref[...].astype(BF), ccr_vm) - _blockdiag_mm_t(si_ref[...].astype(BF), cci_vm)
             + dsk_ref[...] * u)
        y_ref[...] = y
        zg = jax.nn.gelu(y)
        zg_ref[...] = zg.astype(BF)
        q = _mm(zg, wg_ref[...]) + bg_ref[...]
        ya_ref[...] = (zg * _sig(q)).astype(BF)

    return _pallas_call(
        body, carry, name="s5_fwd", grid=(L // TM,),
        in_specs=[_tok(S5W), ANY, ANY, ANY, ANY, _full((1, S5W)), _full((8 * SUB, GN)),
                  _full((S5W, S5W)), _full((1, S5W))],
        out_specs=[_tok(GN), _tok(GN), _tok(S5W), _tok(S5W), _tok(S5W)],
        out_shape=[_sds((L, GN)), _sds((L, GN)), _sds((L, S5W)), _sds((L, S5W), BF), _sds((L, S5W), BF)],
        scratch_shapes=[pltpu.VMEM((S5W // 128, 128, GN // (S5W // 128)), BF)] * 4 + [
                        pltpu.VMEM((SUB, GN), F32), pltpu.VMEM((SUB, GN), F32),
                        pltpu.SemaphoreType.DMA((4,))],
        compiler_params=_params(44),
    )(ua, bbr, bbi, ccr, cci, dsk, con, w_glu, b_glu)


def _s5_bwd(dya, y, ua, sr, si, bbr, bbi, ccr, cci, dsk, con_rev, w_glu, b_glu, carry=None):
    L = ua.shape[0]
    nt = L // TM
    spt = TM // SUB
    n_slab = spt

    def halo_map(i):
        return (jnp.maximum((nt - 1 - i) * spt - 1, 0), 0)

    def body(dya_ref, y_ref, ua_ref, sr_ref, si_ref, hr_ref, hi_ref, bbr_hbm, bbi_hbm, ccr_hbm, cci_hbm,
             dsk_ref, con_ref, wg_ref, bg_ref,
             dua_ref, dq_ref, dy_ref, lr_ref, li_ref, da_ref, dsm_ref,
             bbr_vm, bbi_vm, ccr_vm, cci_vm, cr_ref, ci_ref, w_sems):
        i = pl.program_id(0)
        landed = _resident([(ccr_hbm, ccr_vm), (cci_hbm, cci_vm), (bbr_hbm, bbr_vm), (bbi_hbm, bbi_vm)], w_sems)

        @pl.when(i == 0)
        def _():
            cr_ref[...] = jnp.zeros_like(cr_ref)
            ci_ref[...] = jnp.zeros_like(ci_ref)
            da_ref[...] = jnp.zeros_like(da_ref)
            dsm_ref[...] = jnp.zeros_like(dsm_ref)

        u = ua_ref[...]
        yv = y_ref[...]
        dya = dya_ref[...]
        zg = jax.nn.gelu(yv)
        sg = _sig(_mm(zg, wg_ref[...]) + bg_ref[...])
        dq = dya * zg * sg * (1.0 - sg)
        dq_ref[...] = dq.astype(BF)
        dzg = dya * sg + _mm_nt(dq, wg_ref[...])
        dy = dzg * _gelu_grad(yv)
        dyb = dy.astype(BF)
        dy_ref[...] = dyb
        dsm_ref[0:1, :] += _colsum(dy * u)
        dsm_ref[1:2, :] += _colsum(dq)
        landed(0)
        lr_ref[...] = _blockdiag_mm(dyb, ccr_vm)
        landed(1)
        li_ref[...] = -_blockdiag_mm(dyb, cci_vm)
        _cscan(lr_ref, li_ref, con_ref, cr_ref, ci_ref, reverse=True)

        first_tile = (i == nt - 1)
        row = _row_iota(LC)
        for lc in range(GN // LC):
            cols = slice(lc * LC, (lc + 1) * LC)
            h_r = jnp.where(first_tile, 0.0, hr_ref[:, cols])
            h_i = jnp.where(first_tile, 0.0, hi_ref[:, cols])

            def step(k, acc, cols=cols, h_r=h_r, h_i=h_i):
                ar, ai = acc
                rows = _slab(k)
                prev = _slab(jnp.maximum(k - 1, 0))
                pr = jnp.where(k == 0, h_r, sr_ref[prev, cols])
                pi = jnp.where(k == 0, h_i, si_ref[prev, cols])
                spr = pltpu.roll(jnp.where(row == SUB - 1, pr, sr_ref[rows, cols]), 1, 0)
                spi = pltpu.roll(jnp.where(row == SUB - 1, pi, si_ref[rows, cols]), 1, 0)
                lr, li = lr_ref[rows, cols], li_ref[rows, cols]
                return ar + lr * spr + li * spi, ai + li * spr - lr * spi

            zero = jnp.zeros((SUB, LC), F32)
            ar, ai = lax.fori_loop(0, n_slab, step, (zero, zero))
            da_ref[0:1, cols] += _colsum(ar)
            da_ref[1:2, cols] += _colsum(ai)

        landed(2)
        landed(3)
        dua_ref[...] = (dy * dsk_ref[...] + _blockdiag_mm_t(lr_ref[...].astype(BF), bbr_vm)
                        + _blockdiag_mm_t(li_ref[...].astype(BF), bbi_vm))

    return _pallas_call(
        body, carry, name="s5_bwd", grid=(nt,),
        in_specs=[_tok_rev(S5W, nt), _tok_rev(S5W, nt), _tok_rev(S5W, nt), _tok_rev(GN, nt), _tok_rev(GN, nt),
                  pl.BlockSpec((SUB, GN), halo_map), pl.BlockSpec((SUB, GN), halo_map),
                  ANY, ANY, ANY, ANY, _full((1, S5W)), _full((8 * SUB, GN)), _full((S5W, S5W)), _full((1, S5W))],
        out_specs=[_tok_rev(S5W, nt), _tok_rev(S5W, nt), _tok_rev(S5W, nt), _tok_rev(GN, nt), _tok_rev(GN, nt),
                   _full((SUB, GN)), _full((SUB, S5W))],
        out_shape=[_sds((L, S5W)), _sds((L, S5W), BF), _sds((L, S5W), BF), _sds((L, GN)), _sds((L, GN)),
                   _sds((SUB, GN)), _sds((SUB, S5W))],
        scratch_shapes=[pltpu.VMEM((S5W // 128, 128, GN // (S5W // 128)), BF)] * 4 + [
                        pltpu.VMEM((SUB, GN), F32), pltpu.VMEM((SUB, GN), F32),
                        pltpu.SemaphoreType.DMA((4,))],
        compiler_params=_params(52),
    )(dya, y, ua, sr, si, sr, si, bbr, bbi, ccr, cci, dsk, con_rev, w_glu, b_glu)


def _lru_gate_terms(rg, sp):
    log_a = -LRU_C * rg * sp
    a = jnp.exp(log_a)
    mult = jnp.sqrt(_neg_expm1(2.0 * log_a))
    return a, mult


def _lru_fwd(ub, conv_w, conv_b, wr, wi, b_r, b_i, sp, carry=None):
    L = ub.shape[0]
    n_slab = TM // SUB

    def body(ub_ref, cw_ref, cb_ref, wr_ref, wi_ref, br_ref, bi_ref, sp_ref,
             xc_ref, rg_ref, ig_ref, h_ref, hp_ref, a_ref, halo_ref, carry_ref):
        @pl.when(pl.program_id(0) == 0)
        def _():
            halo_ref[...] = jnp.zeros_like(halo_ref)
            carry_ref[...] = jnp.zeros_like(carry_ref)

        row = _row_iota(LW)
        taps = [cw_ref[k:k + 1, :] for k in range(4)]
        cb = cb_ref[...]

        def conv_step(k, prev):
            rows = _slab(k)
            cur = ub_ref[rows, :]
            acc = taps[3] * cur + cb
            for j in (1, 2, 3):
                acc = acc + taps[3 - j] * pltpu.roll(jnp.where(row >= SUB - j, prev, cur), j, 0)
            xc_ref[rows, :] = acc
            return cur

        halo_ref[...] = lax.fori_loop(0, n_slab, conv_step, halo_ref[...])

        xc = xc_ref[...]
        xcb = xc.astype(BF)
        rg = _sig(_blockdiag_mm(xcb, wr_ref) + br_ref[...])
        ig = _sig(_blockdiag_mm(xcb, wi_ref) + bi_ref[...])
        rg_ref[...] = rg
        ig_ref[...] = ig
        a, mult = _lru_gate_terms(rg, sp_ref[...])
        a_ref[...] = a
        h_ref[...] = mult * ig * xc

        rowc = _row_iota(LC)
        for lc in range(LW // LC):
            cols = slice(lc * LC, (lc + 1) * LC)

            def step(k, c, cols=cols):
                rows = _slab(k)
                av, b = a_ref[rows, cols], h_ref[rows, cols]
                for sh in (1, 2, 4):
                    keep = rowc >= sh
                    b = b + av * jnp.where(keep, pltpu.roll(b, sh, 0), 0.0)
                    av = av * jnp.where(keep, pltpu.roll(av, sh, 0), 1.0)
                h = b + av * c
                h_ref[rows, cols] = h
                hp_ref[rows, cols] = jnp.where(rowc == 0, c, pltpu.roll(h, 1, 0))
                return _bcast_row(h, SUB - 1)

            carry_ref[:, cols] = lax.fori_loop(0, n_slab, step, carry_ref[:, cols])

    return _pallas_call(
        body, carry, name="lru_fwd", grid=(L // TM,),
        in_specs=[_tok(LW), _full((4, LW)), _full((1, LW)), _full((LW // 128, 128, 128)), _full((LW // 128, 128, 128)),
                  _full((1, LW)), _full((1, LW)), _full((1, LW))],
        out_specs=[_tok(LW)] * 5,
        out_shape=[_sds((L, LW))] * 5,
        scratch_shapes=[pltpu.VMEM((TM, LW), F32), pltpu.VMEM((SUB, LW), F32), pltpu.VMEM((SUB, LW), F32)],
        compiler_params=_params(40),
    )(ub, conv_w, conv_b, wr, wi, b_r, b_i, sp)


def _lru_bwd(dyb, xc, rg, ig, hp, ub, conv_w, wr, wi, sp, dsp, carry=None):
    L = ub.shape[0]
    nt = L // TM
    spt = TM // SUB
    n_slab = spt

    def halo_map(i):
        return (jnp.maximum((nt - 1 - i) * spt - 1, 0), 0)

    def body(dh_ref, xc_ref, rg_ref, ig_ref, hp_ref, ub_ref, uh_ref, cw_ref, wr_ref, wi_ref, sp_ref, dsp_ref,
             dub_ref, dpr_ref, dpi_ref, acc_ref, a_ref, lam_ref, dxc_ref, carry_ref, next_ref):
        i = pl.program_id(0)

        @pl.when(i == 0)
        def _():
            carry_ref[...] = jnp.zeros_like(carry_ref)
            next_ref[...] = jnp.zeros_like(next_ref)
            acc_ref[...] = jnp.zeros_like(acc_ref)

        sp = sp_ref[...]
        rg, ig, xc = rg_ref[...], ig_ref[...], xc_ref[...]
        a, mult = _lru_gate_terms(rg, sp)
        a_ref[...] = a

        rowc = _row_iota(LC)
        for lc in range(LW // LC):
            cols = slice(lc * LC, (lc + 1) * LC)

            def step(k, c, cols=cols):
                rows = _slab(n_slab - 1 - k)
                av, dh = a_ref[rows, cols], dh_ref[rows, cols]
                b = av * dh
                for sh in (1, 2, 4):
                    keep = rowc < SUB - sh
                    b = b + av * jnp.where(keep, pltpu.roll(b, SUB - sh, 0), 0.0)
                    av = av * jnp.where(keep, pltpu.roll(av, SUB - sh, 0), 1.0)
                mu = b + av * c
                lam_ref[rows, cols] = dh + jnp.where(rowc == SUB - 1, c, pltpu.roll(mu, SUB - 1, 0))
                return _bcast_row(mu, 0)

            carry_ref[:, cols] = lax.fori_loop(0, n_slab, step, carry_ref[:, cols])

        lam = lam_ref[...]
        d_a = lam * hp_ref[...]
        d_mult = lam * ig * xc
        d_ig = lam * mult * xc
        dxc = lam * mult * ig
        d_log_a = d_a * a - d_mult * a * a / mult
        d_rg = (-LRU_C) * sp * d_log_a
        acc_ref[0:1, :] += _colsum((-LRU_C) * rg * d_log_a) * dsp_ref[...]
        dpr = d_rg * rg * (1.0 - rg)
        dpi = d_ig * ig * (1.0 - ig)
        acc_ref[1:2, :] += _colsum(dpr)
        acc_ref[2:3, :] += _colsum(dpi)
        dprb, dpib = dpr.astype(BF), dpi.astype(BF)
        dpr_ref[...] = dprb
        dpi_ref[...] = dpib
        dxc = dxc + _blockdiag_mm_t(dprb, wr_ref) + _blockdiag_mm_t(dpib, wi_ref)
        dxc_ref[...] = dxc
        acc_ref[3:4, :] += _colsum(dxc)

        row = _row_iota(LW)
        taps = [cw_ref[k:k + 1, :] for k in range(4)]
        u_halo = jnp.where(i == nt - 1, 0.0, uh_ref[...])
        nxt_tile = next_ref[...]

        def conv_step(k, accs):
            rows = _slab(k)
            cur = dxc_ref[rows, :]
            nxt = jnp.where(k == n_slab - 1, nxt_tile, dxc_ref[_slab(jnp.minimum(k + 1, n_slab - 1)), :])
            ucur = ub_ref[rows, :]
            uprev = jnp.where(k == 0, u_halo, ub_ref[_slab(jnp.maximum(k - 1, 0)), :])
            du = taps[3] * cur
            new = [accs[3] + cur * ucur]
            for j in (1, 2, 3):
                du = du + taps[3 - j] * pltpu.roll(jnp.where(row < j, nxt, cur), SUB - j, 0)
                new.append(accs[3 - j] + cur * pltpu.roll(jnp.where(row >= SUB - j, uprev, ucur), j, 0))
            dub_ref[rows, :] = du
            return tuple(new[::-1])

        zero = jnp.zeros((SUB, LW), F32)
        accs = lax.fori_loop(0, n_slab, conv_step, (zero, zero, zero, zero))
        for k in range(4):
            acc_ref[4 + k:5 + k, :] += _colsum(accs[k])
        next_ref[...] = dxc_ref[0:SUB, :]

    return _pallas_call(
        body, carry, name="lru_bwd", grid=(nt,),
        in_specs=[_tok_rev(LW, nt)] * 6 + [pl.BlockSpec((SUB, LW), halo_map), _full((4, LW)),
                                           _full((LW // 128, 128, 128)), _full((LW // 128, 128, 128)), _full((1, LW)), _full((1, LW))],
        out_specs=[_tok_rev(LW, nt), _tok_rev(LW, nt), _tok_rev(LW, nt), _full((SUB, LW))],
        out_shape=[_sds((L, LW)), _sds((L, LW), BF), _sds((L, LW), BF), _sds((SUB, LW))],
        scratch_shapes=[pltpu.VMEM((TM, LW), F32), pltpu.VMEM((TM, LW), F32), pltpu.VMEM((TM, LW), F32),
                        pltpu.VMEM((SUB, LW), F32), pltpu.VMEM((SUB, LW), F32)],
        compiler_params=_params(48),
    )(dyb, xc, rg, ig, hp, ub, ub, conv_w, wr, wi, sp, dsp)


AC = D // NCHIP


def _merge_fwd(x, ya, yb, gp, w_a, w_b, w_o, carry=None):
    L = x.shape[0]

    def body(x_ref, ya_ref, yb_ref, gp_ref, wa_ref, wb_ref, wo_ref, x1_ref, pa_ref, pb_ref, mg_ref):
        ya = ya_ref[...]
        for k in range(NCHIP):
            pa_ref[:, k * AC:(k + 1) * AC] = jnp.dot(ya, wa_ref[k], preferred_element_type=F32)
        pb = _mm(yb_ref[...], wb_ref[...])
        pb_ref[...] = pb
        gp = gp_ref[...]
        merged = (_sig(gp[:, :D]) * pa_ref[...] + _sig(gp[:, D:]) * pb).astype(BF)
        mg_ref[...] = merged
        x1_ref[...] = x_ref[...] + jnp.dot(merged, wo_ref[...], preferred_element_type=F32)

    return _pallas_call(
        body, carry, name="merge_fwd", grid=(L // TM,),
        in_specs=[_tok(D), _tok(S5W), _tok(LW), _tok(2 * D), _full((NCHIP, S5W, AC)), _full((LW, D)), _full((D, D))],
        out_specs=[_tok(D), _tok(D), _tok(D), _tok(D)],
        out_shape=[_sds((L, D)), _sds((L, D)), _sds((L, D)), _sds((L, D), BF)],
        compiler_params=_params(40),
    )(x, ya, yb, gp, w_a, w_b, w_o)


def _merge_bwd(dx1, gp, pa, pb, w_a, w_b, w_o, carry=None):
    L = dx1.shape[0]

    def body(dx1_ref, gp_ref, pa_ref, pb_ref, wa_ref, wb_ref, wo_ref, dya_ref, dyb_ref, dgp_ref, dpa_ref, dpb_ref):
        dm = _mm_nt(dx1_ref[...], wo_ref[...])
        gp = gp_ref[...]
        sa, sb = _sig(gp[:, :D]), _sig(gp[:, D:])
        dpa = (dm * sa).astype(BF)
        dpb = (dm * sb).astype(BF)
        dpa_ref[...] = dpa
        dpb_ref[...] = dpb
        dgp_ref[:, :D] = dm * pa_ref[...] * sa * (1.0 - sa)
        dgp_ref[:, D:] = dm * pb_ref[...] * sb * (1.0 - sb)
        dya = jnp.zeros((TM, S5W), F32)
        for k in range(NCHIP):
            dya = dya + _mm_nt(dpa[:, k * AC:(k + 1) * AC], wa_ref[k])
        dya_ref[...] = dya
        dyb_ref[...] = _mm_nt(dpb, wb_ref[...])

    return _pallas_call(
        body, carry, name="merge_bwd", grid=(L // TM,),
        in_specs=[_tok(D), _tok(2 * D), _tok(D), _tok(D), _full((NCHIP, S5W, AC)), _full((LW, D)), _full((D, D))],
        out_specs=[_tok(S5W), _tok(LW), _tok(2 * D), _tok(D), _tok(D)],
        out_shape=[_sds((L, S5W)), _sds((L, LW)), _sds((L, 2 * D)), _sds((L, D), BF), _sds((L, D), BF)],
        compiler_params=_params(40),
    )(dx1, gp, pa, pb, w_a, w_b, w_o)


def _chunk_tok(width):
    return pl.BlockSpec((NCHIP, TM, width), lambda i: (0, i, 0))


def _ffn_fwd(x1, g_ffn, wg, wu, wd, carry=None):
    L = x1.shape[0]

    def body(x_ref, g_ref, wg_hbm, wu_hbm, wd_hbm, x2_ref, h2_ref, gg_ref, uu_ref, wg_vm, wu_vm, wd_vm, w_sems):
        _resident_now([(src.at[c], dst.at[c]) for c in range(NCHIP)
                       for src, dst in ((wg_hbm, wg_vm), (wu_hbm, wu_vm), (wd_hbm, wd_vm))], w_sems)
        x = x_ref[...]
        xh, _ = _rms(x)
        h2 = (xh * g_ref[...]).astype(BF)
        h2_ref[...] = h2
        out = x
        for c in range(NCHIP):
            gg = lax.dot_general(h2, wg_vm[c], (((1,), (1,)), ((), ())), preferred_element_type=F32)
            uu = lax.dot_general(h2, wu_vm[c], (((1,), (1,)), ((), ())), preferred_element_type=F32)
            gg_ref[c] = gg.astype(BF)
            uu_ref[c] = uu.astype(BF)
            act = (gg * _sig(gg) * uu).astype(BF)
            out = out + jnp.dot(act, wd_vm[c], preferred_element_type=F32)
        x2_ref[...] = out

    return _pallas_call(
        body, carry, name="ffn_fwd", grid=(L // TM,),
        in_specs=[_tok(D), _full((1, D)), ANY, ANY, ANY],
        out_specs=[_tok(D), _tok(D), _chunk_tok(FC), _chunk_tok(FC)],
        out_shape=[_sds((L, D)), _sds((L, D), BF), _sds((NCHIP, L, FC), BF), _sds((NCHIP, L, FC), BF)],
        scratch_shapes=[pltpu.VMEM((NCHIP, FC, D), BF)] * 3 + [pltpu.SemaphoreType.DMA((3 * NCHIP,))],
        compiler_params=_params(52),
    )(x1, g_ffn, wg, wu, wd)


def _ffn_bwd(x1, dx2, gg, uu, g_ffn, wg, wu, wd, carry=None):
    L = x1.shape[0]

    def body(x_ref, dx2_ref, gg_ref, uu_ref, g_ref, wg_hbm, wu_hbm, wd_hbm,
             dx1_ref, act_ref, dgg_ref, duu_ref, dg_ref, wg_vm, wu_vm, wd_vm, w_sems):
        _resident_now([(src.at[c], dst.at[c]) for c in range(NCHIP)
                       for src, dst in ((wg_hbm, wg_vm), (wu_hbm, wu_vm), (wd_hbm, wd_vm))], w_sems)

        @pl.when(pl.program_id(0) == 0)
        def _():
            dg_ref[...] = jnp.zeros_like(dg_ref)

        dx2 = dx2_ref[...]
        dx2b = dx2.astype(BF)
        dh2 = jnp.zeros((TM, D), F32)
        for c in range(NCHIP):
            g = gg_ref[c].astype(F32)
            u = uu_ref[c].astype(F32)
            s = _sig(g)
            silu = g * s
            act_ref[c] = (silu * u).astype(BF)
            dact = lax.dot_general(dx2b, wd_vm[c], (((1,), (1,)), ((), ())), preferred_element_type=F32)
            dg = (dact * u * s * (1.0 + g * (1.0 - s))).astype(BF)
            du = (dact * silu).astype(BF)
            dgg_ref[c] = dg
            duu_ref[c] = du
            dh2 = dh2 + jnp.dot(dg, wg_vm[c], preferred_element_type=F32)
            dh2 = dh2 + jnp.dot(du, wu_vm[c], preferred_element_type=F32)
        xh, r = _rms(x_ref[...])
        dg_ref[0:1, :] += _colsum(dh2 * xh)
        dx1_ref[...] = dx2 + _rms_bwd(dh2, xh, r, g_ref[...])

    return _pallas_call(
        body, carry, name="ffn_bwd", grid=(L // TM,),
        in_specs=[_tok(D), _tok(D), _chunk_tok(FC), _chunk_tok(FC), _full((1, D)), ANY, ANY, ANY],
        out_specs=[_tok(D), _chunk_tok(FC), _chunk_tok(FC), _chunk_tok(FC), _full((SUB, D))],
        out_shape=[_sds((L, D)), _sds((NCHIP, L, FC), BF), _sds((NCHIP, L, FC), BF), _sds((NCHIP, L, FC), BF),
                   _sds((SUB, D))],
        scratch_shapes=[pltpu.VMEM((NCHIP, FC, D), BF)] * 3 + [pltpu.SemaphoreType.DMA((3 * NCHIP,))],
        compiler_params=_params(56),
    )(x1, dx2, gg, uu, g_ffn, wg, wu, wd)


def _ple_loss(x2, p, tgt, g_pg, w_pg, b_pg, w_ple, g_ple, g_final):
    L = x2.shape[0]

    def body(x2_ref, p_ref, t_ref, gpg_ref, wpg_ref, bpg_ref, wple_ref, gple_ref, gf_ref,
             dx2_ref, n2_ref, dpre_ref, de0_ref, acc_ref):
        @pl.when(pl.program_id(0) == 0)
        def _():
            acc_ref[...] = jnp.zeros_like(acc_ref)

        x2 = x2_ref[...]
        x2h, r2 = _rms(x2)
        n2 = (x2h * gpg_ref[...]).astype(BF)
        n2_ref[...] = n2
        gate = _sig(jnp.dot(n2, wpg_ref[...], preferred_element_type=F32) + bpg_ref[...])
        pb = p_ref[...].astype(BF)
        e0 = jnp.concatenate([jnp.dot(pb, wple_ref[k], preferred_element_type=F32) for k in range(NCHIP)], axis=1)
        e0h, re = _rms(e0)
        e = e0h * gple_ref[...]
        x3 = x2 + gate * e
        x3h, r3 = _rms(x3)
        diff = x3h * gf_ref[...] - t_ref[...]
        acc_ref[4:5, :] += _colsum(diff * diff) * (0.5 / D)
        dy = diff * (1.0 / D)
        acc_ref[3:4, :] += _colsum(dy * x3h)
        dx3 = _rms_bwd(dy, x3h, r3, gf_ref[...])
        de = dx3 * gate
        acc_ref[2:3, :] += _colsum(de * e0h)
        de0_ref[...] = _rms_bwd(de, e0h, re, gple_ref[...]).astype(BF)
        dpre = dx3 * e * gate * (1.0 - gate)
        acc_ref[1:2, :] += _colsum(dpre)
        dpreb = dpre.astype(BF)
        dpre_ref[...] = dpreb
        dn2 = lax.dot_general(dpreb, wpg_ref[...], (((1,), (1,)), ((), ())), preferred_element_type=F32)
        acc_ref[0:1, :] += _colsum(dn2 * x2h)
        dx2_ref[...] = dx3 + _rms_bwd(dn2, x2h, r2, gpg_ref[...])

    return _pallas_call(
        body, name="ple_loss", grid=(L // TM,),
        in_specs=[_tok(D), _tok(PLE), _tok(D), _full((1, D)), _full((D, D)), _full((1, D)), _full((NCHIP, PLE, AC)),
                  _full((1, D)), _full((1, D))],
        out_specs=[_tok(D), _tok(D), _tok(D), _tok(D), _full((SUB, D))],
        out_shape=[_sds((L, D)), _sds((L, D), BF), _sds((L, D), BF), _sds((L, D), BF), _sds((SUB, D))],
        compiler_params=_params(40),
    )(x2, p, tgt, g_pg, w_pg, b_pg, w_ple, g_ple, g_final)


def _tn(name, a, b, col_chunk=None, a_block=None, carry=None):
    L = a.shape[-2]
    m, n = a.shape[-1], b.shape[-1]
    a_col = 0
    if a_block is not None:
        a_col, m = a_block
    tk = L if (a.ndim == 3 or b.ndim == 3 or a_block is not None) else TK
    if a.ndim == 3 or b.ndim == 3:
        nj, bn = (a if a.ndim == 3 else b).shape[0], n
        a_spec = (pl.BlockSpec((None, tk, m), lambda j, t: (j, t, 0)) if a.ndim == 3
                  else pl.BlockSpec((tk, m), lambda j, t: (t, 0)))
        b_spec = (pl.BlockSpec((None, tk, n), lambda j, t: (j, t, 0)) if b.ndim == 3
                  else pl.BlockSpec((tk, n), lambda j, t: (t, 0)))
        out_spec, out_shape = pl.BlockSpec((None, m, n), lambda j, t: (j, 0, 0)), _sds((nj, m, n))
    else:
        bn = col_chunk
        if bn is None:
            bn = next((cand for cand in (1024, 512) if n > cand and n % cand == 0), n)
        nj = n // bn
        a_spec = pl.BlockSpec((tk, m), lambda j, t: (t, a_col))
        b_spec = pl.BlockSpec((tk, bn), lambda j, t: (t, j))
        if col_chunk is None:
            out_spec, out_shape = pl.BlockSpec((m, bn), lambda j, t: (0, j)), _sds((m, n))
        else:
            out_spec, out_shape = pl.BlockSpec((None, m, bn), lambda j, t: (j, 0, 0)), _sds((nj, m, bn))

    def body(a_ref, b_ref, o_ref):
        if tk == L:
            o_ref[...] = _mm_tn(a_ref[...], b_ref[...])
        else:
            @pl.when(pl.program_id(1) == 0)
            def _():
                o_ref[...] = jnp.zeros_like(o_ref)

            o_ref[...] += _mm_tn(a_ref[...], b_ref[...])

    outs = _pallas_call(
        body, carry, name=name, grid=(nj, L // tk), in_specs=[a_spec, b_spec], out_specs=[out_spec],
        out_shape=[pltpu.HBM(out_shape.shape, out_shape.dtype)],
        compiler_params=pltpu.CompilerParams(dimension_semantics=("arbitrary", "arbitrary"),
                                             vmem_limit_bytes=(30 if tk == L else 28) * VMEM_MB),
    )(*(_in_hbm([a, b]) if tk == L else (a, b)))
    return outs[0] if carry is None else outs


LANE = 128


def _tn_blocks(name, a, bs, ga, gb, carry=None):
    L, m, n, nb = a.shape[0], a.shape[1], bs[0].shape[1], len(bs)
    per = LANE // ga
    wb = per * gb
    n_super = m // LANE

    def body(a_ref, *refs):
        b_refs, o_refs, acc_refs = refs[:nb], refs[nb:2 * nb], refs[2 * nb:]
        t = pl.program_id(0)

        @pl.when(t == 0)
        def _():
            for acc in acc_refs:
                acc[...] = jnp.zeros_like(acc)

        lhs = a_ref[...].astype(BF)
        for b_ref, acc in zip(b_refs, acc_refs):
            rhs = b_ref[...].astype(BF)
            for j in range(n_super):
                acc[j] += _mm_tn(lhs[:, j * LANE:(j + 1) * LANE], rhs[:, j * wb:(j + 1) * wb])

        @pl.when(t == L // TK - 1)
        def _():
            own = (lax.broadcasted_iota(jnp.int32, (LANE, wb), 0) // ga) == (lax.broadcasted_iota(jnp.int32, (LANE, wb), 1) // gb)
            for o_ref, acc in zip(o_refs, acc_refs):
                for j in range(n_super):
                    kept = jnp.where(own, acc[j], 0.0)
                    o_ref[:, j * wb:(j + 1) * wb] = jnp.sum(kept.reshape(per, ga, wb), axis=0)

    outs = _pallas_call(
        body, carry, name=name, grid=(L // TK,),
        in_specs=[pl.BlockSpec((TK, m), lambda t: (t, 0))] + [pl.BlockSpec((TK, n), lambda t: (t, 0))] * nb,
        out_specs=[_full((ga, n))] * nb, out_shape=[_sds((ga, n))] * nb,
        scratch_shapes=[pltpu.VMEM((n_super, LANE, wb), F32)] * nb,
        compiler_params=_params(48),
    )(*_in_hbm([a] + list(bs)))
    return list(outs)


def _s5_discretize(lam_re, lam_im, log_dt, b_re, b_im):
    dt = jnp.exp(log_dt)[:, None]
    mag = jnp.exp(lam_re * dt)
    ar = mag * jnp.cos(lam_im * dt)
    ai = mag * jnp.sin(lam_im * dt)
    den = lam_re * lam_re + lam_im * lam_im
    nr = ar - 1.0
    fr = (nr * lam_re + ai * lam_im) / den
    fi = (ai * lam_re - nr * lam_im) / den
    bbr = fr[:, None, :] * b_re - fi[:, None, :] * b_im
    bbi = fr[:, None, :] * b_im + fi[:, None, :] * b_re
    return ar, ai, bbr, bbi


def _prepare(by_rows, block_cols, ar, ai):
    n = len(by_rows)

    def body(*refs):
        srcs, (ar_ref, ai_ref), dense, (con_ref, rev_ref) = refs[:n], refs[n:n + 2], refs[n + 2:2 * n + 2], refs[2 * n + 2:]
        for src, out, c in zip(srcs, dense, block_cols):
            r = src.shape[0]
            per = LANE // r
            wide = per * c
            own = (lax.broadcasted_iota(jnp.int32, (LANE, wide), 0) // r) == (lax.broadcasted_iota(jnp.int32, (LANE, wide), 1) // c)
            for j in range(out.shape[0]):
                tiled = jnp.broadcast_to(src[:, j * wide:(j + 1) * wide][None], (per, r, wide)).reshape(LANE, wide)
                out[j] = jnp.where(own, tiled, 0.0).astype(BF)
        a_r, a_i = ar_ref[...], ai_ref[...]
        pw = [(jnp.ones_like(a_r), jnp.zeros_like(a_i))]
        for _ in range(SUB):
            pr, pi = pw[-1]
            pw.append((pr * a_r - pi * a_i, pr * a_i + pi * a_r))
        row = _row_iota(GN)
        for ref, reverse in ((con_ref, False), (rev_ref, True)):
            sign = -1.0 if reverse else 1.0
            for j, sh in enumerate((1, 2, 4)):
                keep = (row < SUB - sh) if reverse else (row >= sh)
                ref[2 * j * SUB:(2 * j + 1) * SUB, :] = jnp.where(keep, pw[sh][0], 0.0)
                ref[(2 * j + 1) * SUB:(2 * j + 2) * SUB, :] = jnp.where(keep, sign * pw[sh][1], 0.0)
            p_r, p_i = jnp.zeros((SUB, GN), F32), jnp.zeros((SUB, GN), F32)
            for i in range(SUB):
                k = SUB - i if reverse else i + 1
                p_r = jnp.where(row == i, pw[k][0], p_r)
                p_i = jnp.where(row == i, sign * pw[k][1], p_i)
            ref[6 * SUB:7 * SUB, :] = p_r
            ref[7 * SUB:8 * SUB, :] = p_i

    dense_shapes = [(b.shape[1] // (LANE // b.shape[0] * c), LANE, LANE // b.shape[0] * c)
                    for b, c in zip(by_rows, block_cols)]
    outs = _pallas_call(
        body, name="prepare", grid=(1,), in_specs=[_full(b.shape) for b in by_rows] + [_full((1, GN))] * 2,
        out_specs=[_full(s) for s in dense_shapes] + [_full((8 * SUB, GN))] * 2,
        out_shape=[_sds(s, BF) for s in dense_shapes] + [_sds((8 * SUB, GN))] * 2,
        compiler_params=_params(48),
    )(*by_rows, ar, ai)
    return outs[:n], outs[n], outs[n + 1]


def _local_step(x, p, tgt, w, comm):
    rows_of = lambda a: a.reshape(NCHIP * a.shape[1], a.shape[2])
    quarters = lambda a: a.reshape(NCHIP, a.shape[0] // NCHIP, a.shape[1])

    def gathering(names, call):
        carry = comm.gather(names)
        outs = list(call(carry))
        own = len(outs) - len(carry.out_shapes)
        w.update(zip(names, outs[own:]))
        return outs[:own]

    w.update(comm.first())
    w_glu = rows_of(w["w_glu"])
    ar, ai, bbr, bbi = _s5_discretize(w["lam_re"], w["lam_im"], w["log_dt"], w["s5_b_re"], w["s5_b_im"])
    by_row = lambda b: jnp.transpose(b, (1, 0, 2)).reshape(b.shape[1], -1)
    (bbr_d, bbi_d, ccr_d, cci_d, wr_d, wi_d), con, con_rev = _prepare(
        [by_row(b) for b in (bbr, bbi, w["s5_c_re"], w["s5_c_im"], w["w_r"], w["w_i"])], [NS] * 4 + [HD] * 2,
        ar.reshape(1, GN), ai.reshape(1, GN))
    dsk = w["s5_d"].reshape(1, S5W)
    lam = w["lru_lambda"].reshape(1, LW)
    sp = jax.nn.softplus(-lam)
    b_r, b_i = w["b_r"].reshape(1, LW), w["b_i"].reshape(1, LW)
    row = lambda name: w[name].reshape(1, -1)

    h, ua, ub, gp = gathering(["w_a_out", "w_b_out"], lambda carry: _inproj_fwd(
        x, row("g_mix"), w["w_in"], row("b_in"), carry))
    sr, si, y, zg, ya = gathering(["w_o", "w_ffn_gate"], lambda carry: _s5_fwd(
        ua, bbr_d, bbi_d, ccr_d, cci_d, dsk, con, w_glu, row("b_glu"), carry))
    xc, rg, ig, yb, hp = gathering(["w_ffn_up"], lambda carry: _lru_fwd(
        ub, w["conv_w"], row("conv_b"), wr_d, wi_d, b_r, b_i, sp, carry))
    w_b_out, w_o = rows_of(w["w_b_out"]), rows_of(w["w_o"])
    x1, pa, pb, merged = gathering(["w_ffn_down"], lambda carry: _merge_fwd(
        x, ya, yb, gp, w["w_a_out"], w_b_out, w_o, carry))
    x2, h2, gg, uu = gathering(["w_ple_gate", "w_ple"], lambda carry: _ffn_fwd(
        x1, row("g_ffn"), w["w_ffn_gate"], w["w_ffn_up"], w["w_ffn_down"], carry))
    w_pg = rows_of(w["w_ple_gate"])
    dx2, n2, dpre, de0, acc_p = _ple_loss(x2, p, tgt, row("g_ple_gate"), w_pg, row("b_ple_gate"),
                                          w["w_ple"], row("g_ple"), row("g_final"))
    comm.reduce("ple", {"w_ple_gate": quarters(_tn("dw_ple_gate", n2, dpre)),
                        "w_ple": _tn("dw_ple", p, de0, col_chunk=AC)})
    dx1, act, dgg, duu, acc_f = comm.run(lambda carry: _ffn_bwd(
        x1, dx2, gg, uu, row("g_ffn"), w["w_ffn_gate"], w["w_ffn_up"], w["w_ffn_down"], carry))
    comm.reduce("ffn_gate", {"w_ffn_gate": _tn("dw_ffn_gate", dgg, h2)})
    comm.reduce("w_o", {"w_o": quarters(_tn("dw_o", merged, dx1))})
    comm.reduce("ffn_up", {"w_ffn_up": comm.run(lambda carry: _tn("dw_ffn_up", duu, h2, carry=carry))[0]})
    comm.reduce("ffn_down", {"w_ffn_down": comm.run(lambda carry: _tn("dw_ffn_down", act, dx2, carry=carry),
                                                    hold=("ffn_gate", "w_o"))[0]})
    dya, dyb, dgp, dpa, dpb = comm.run(lambda carry: _merge_bwd(
        dx1, gp, pa, pb, w["w_a_out"], w_b_out, w_o, carry), hold=("ffn_gate", "ffn_up"))
    comm.reduce("merge", {"w_a_out": _tn("dw_a_out", ya, dpa, col_chunk=AC), "w_b_out": quarters(_tn("dw_b_out", yb, dpb))})
    dua, dq, dy, lr, li, acc_a, acc_s = comm.run(lambda carry: _s5_bwd(
        dya, y, ua, sr, si, bbr_d, bbi_d, ccr_d, cci_d, dsk, con_rev, w_glu, row("b_glu"), carry), hold=("ffn_down",))
    dub, dpr, dpi, acc_l = comm.run(lambda carry: _lru_bwd(
        dyb, xc, rg, ig, hp, ub, w["conv_w"], wr_d, wi_d, sp, -_sig(-lam), carry))
    gx, dz, acc_g, acc_b = _inproj_bwd(x, dx1, dua, dub, dgp, row("g_mix"), w["w_in"])
    half = (D // 2,)
    comm.reduce("in_lo", {"w_in_lo": comm.run(lambda carry: _tn(
        "dw_in_lo", h, dz, col_chunk=QC, a_block=(0,) + half, carry=carry))[0]})
    comm.reduce("in_hi", {"w_in_hi": comm.run(lambda carry: _tn(
        "dw_in_hi", h, dz, col_chunk=QC, a_block=(1,) + half, carry=carry))[0], "w_glu": quarters(_tn("dw_glu", zg, dq))})
    d_wr, d_wi = comm.run(lambda carry: _tn_blocks("dw_r_i", xc, [dpr, dpi], HD, HD, carry))
    d_bbr, d_bbi = comm.run(lambda carry: _tn_blocks("d_bb", ua, [lr, li], NP, NS, carry))
    d_ccr, d_cci = comm.run(lambda carry: _tn_blocks("d_cc", dy, [sr, si], NP, NS, carry))
    comm.drain()
    sums = {"ple": acc_p, "ffn": acc_f, "mix": acc_g, "b_in": acc_b, "lru": acc_l, "s5": acc_s, "s5_a": acc_a}
    blocks = {"bb_re": d_bbr, "bb_im": d_bbi,
              "cc_re": d_ccr, "cc_im": d_cci,
              "w_r": d_wr, "w_i": d_wi}
    return gx, sums, blocks


def _replicated_grads(w, sums, blocks):
    grouped = lambda e, groups: jnp.transpose(e.reshape(e.shape[0], groups, -1), (1, 0, 2))
    d_ar, d_ai = sums["s5_a"][0].reshape(NG, NS), sums["s5_a"][1].reshape(NG, NS)
    d_bbr, d_bbi = grouped(blocks["bb_re"], NG), grouped(blocks["bb_im"], NG)
    _, vjp = jax.vjp(_s5_discretize, w["lam_re"], w["lam_im"], w["log_dt"], w["s5_b_re"], w["s5_b_im"])
    g = dict(zip(("lam_re", "lam_im", "log_dt", "s5_b_re", "s5_b_im"), vjp((d_ar, d_ai, d_bbr, d_bbi))))
    g["s5_c_re"] = grouped(blocks["cc_re"], NG)
    g["s5_c_im"] = -grouped(blocks["cc_im"], NG)
    g["w_r"], g["w_i"] = grouped(blocks["w_r"], NH), grouped(blocks["w_i"], NH)
    g["s5_d"] = sums["s5"][0].reshape(NG, NP)
    g["b_r"] = sums["lru"][1].reshape(NH, HD)
    g["b_i"] = sums["lru"][2].reshape(NH, HD)
    return g


ACC_ROWS = {"g_mix": ("mix", 0), "b_in": ("b_in", 0), "g_ffn": ("ffn", 0), "g_ple_gate": ("ple", 0),
            "b_ple_gate": ("ple", 1), "g_ple": ("ple", 2), "g_final": ("ple", 3), "b_glu": ("s5", 1),
            "lru_lambda": ("lru", 0), "conv_b": ("lru", 3)}
LOSS_ROW = ("ple", 4)
CONV_W_ROWS = ("lru", 4)


SHARDED = [("w_in", (D, QC)), ("w_glu", (S5W // NCHIP, S5W)), ("w_a_out", (S5W, AC)), ("w_b_out", (LW // NCHIP, D)),
           ("w_o", (D // NCHIP, D)), ("w_ffn_gate", (FC, D)), ("w_ffn_up", (FC, D)), ("w_ffn_down", (FC, D)),
           ("w_ple_gate", (D // NCHIP, D)), ("w_ple", (PLE, AC))]
TRANSPOSED = ("w_ffn_gate", "w_ffn_up", "s5_b_re", "s5_b_im")
CONV_SHARD = (4, LW // NCHIP)


def _mesh_pos():
    return lax.axis_index("x"), lax.axis_index("y"), lax.axis_index("c")


def _other_chips(x, y):
    return [(1 - x, y), (x, 1 - y), (1 - x, 1 - y)]


def _half_rows(c, rows, align):
    return pl.ds(pl.multiple_of(c * (rows // 2), align), rows // 2)


def _run_now(name, carry):
    c_in, c_out = len(carry.operands), len(carry.out_shapes)

    def body(*refs):
        ins, outs, sems = refs[:c_in], refs[c_in:c_in + c_out], refs[c_in + c_out:]
        carry.start(ins, outs, sems)
        carry.finish(ins, outs, sems)

    return pl.pallas_call(body, name=name, in_specs=[ANY] * c_in, out_specs=[ANY] * c_out,
                          out_shape=list(carry.out_shapes), scratch_shapes=list(carry.sems),
                          input_output_aliases=dict(carry.aliases))(*_in_hbm(carry.operands))


def _gather_group(shards, split):
    n = len(shards)

    def copies(srcs, outs, sems):
        send_sems, recv_sems = sems
        x, y, c = _mesh_pos()
        k0 = 2 * x + y
        sib = (x, y, 1 - c)
        chips = _other_chips(x, y)

        def remote(src, dst, j, i, to):
            return pltpu.make_async_remote_copy(src_ref=src, dst_ref=dst, send_sem=send_sems.at[j, i],
                                                recv_sem=recv_sems.at[j, i], device_id=to, device_id_type=MESH)

        def rows(ref, i, core, *lead):
            if not split[i]:
                return ref.at[lead] if lead else ref
            return ref.at[(*lead, _half_rows(core, shards[i].shape[0], 16))]

        own = [remote(s, o.at[k0], 6, i, sib) for i, (s, o) in enumerate(zip(srcs, outs))]
        ici, landed, fwd, fwd_landed = [], [], [], []
        for j, chip in enumerate(chips):
            kj = 2 * chip[0] + chip[1]
            pairs = list(enumerate(zip(srcs, outs)))
            ici.append([remote(rows(s, i, c), rows(o, i, c, k0), j, i, (*chip, c)) for i, (s, o) in pairs])
            landed.append([remote(rows(s, i, c), rows(o, i, c, kj), j, i, (*chip, c)) for i, (s, o) in pairs])
            fwd.append([remote(rows(o, i, c, kj), rows(o, i, c, kj), 3 + j, i, sib) for i, (s, o) in pairs if split[i]])
            fwd_landed.append([remote(rows(o, i, 1 - c, kj), rows(o, i, 1 - c, kj), 3 + j, i, sib)
                               for i, (s, o) in pairs if split[i]])
        return own, ici, landed, fwd, fwd_landed

    def start(srcs, outs, sems):
        own, ici, _, _, _ = copies(srcs, outs, sems)
        for cp in own + [cp for per_chip in ici for cp in per_chip]:
            cp.start()

    def finish(srcs, outs, sems):
        own, ici, landed, fwd, fwd_landed = copies(srcs, outs, sems)
        passed = [i for i in range(n) if split[i]]
        for j in range(3):
            for i, cp in enumerate(landed[j]):
                cp.wait_recv()
                if split[i]:
                    fwd[j][passed.index(i)].start()
        for j in range(3):
            for cp in fwd_landed[j]:
                cp.wait_recv()
        for cp in own:
            cp.wait_recv()
        for cp in own + [cp for per_chip in ici + fwd for cp in per_chip]:
            cp.wait_send()

    return _Carried(shards, [_sds((NCHIP,) + s.shape, s.dtype) for s in shards],
                    [pltpu.SemaphoreType.DMA((7, n)), pltpu.SemaphoreType.DMA((7, n))], start, finish)


def _each_copy(copies, carried, out_shapes, sems, aliases=None):
    def start(ins, outs, sem_refs):
        for cp in copies(ins, outs, sem_refs):
            cp.start()

    def finish(ins, outs, sem_refs):
        for cp in copies(ins, outs, sem_refs):
            cp.wait()

    return _Carried(carried, out_shapes, sems, start, finish, aliases)


def _swap_group(grads):
    n = len(grads)

    def copies(srcs, outs, sems):
        send_sems, recv_sems = sems
        x, y, c = _mesh_pos()
        return [pltpu.make_async_remote_copy(src_ref=s.at[:, _half_rows(1 - c, s.shape[1], 8)], dst_ref=o,
                                             send_sem=send_sems.at[i], recv_sem=recv_sems.at[i], device_id=(x, y, 1 - c),
                                             device_id_type=MESH) for i, (s, o) in enumerate(zip(srcs, outs))]

    return _each_copy(copies, grads, [pltpu.HBM((NCHIP, g.shape[1] // 2, g.shape[2]), F32) for g in grads],
                      [pltpu.SemaphoreType.DMA((n,)), pltpu.SemaphoreType.DMA((n,))])


def _add_sibling_group(tag, kc_idx, grads, gots):
    n = len(grads)

    def body(kc_ref, *refs):
        for g, rx, p, pb in zip(refs[:n], refs[n:2 * n], refs[2 * n:3 * n], refs[3 * n:]):
            s = g[...] + rx[...]
            pb[...] = s.astype(BF)

            @pl.when(pl.program_id(0) == kc_ref[0])
            def _():
                p[...] = s

    halves = [pl.BlockSpec((None,) + rx.shape[1:], lambda k, kc_ref: (k, 0, 0)) for rx in gots]
    mine = [pl.BlockSpec((None,) + rx.shape[1:], lambda k, kc_ref: (k, kc_ref[1], 0)) for rx in gots]
    own = [pl.BlockSpec(rx.shape[1:], lambda k, kc_ref: (0, 0)) for rx in gots]
    outs = _pallas_call(
        body, name="add_sibling_" + tag,
        grid_spec=pltpu.PrefetchScalarGridSpec(num_scalar_prefetch=1, grid=(NCHIP,), in_specs=mine + halves,
                                               out_specs=own + halves),
        out_shape=[pltpu.HBM(rx.shape[1:], F32) for rx in gots] + [pltpu.HBM(rx.shape, BF) for rx in gots],
        compiler_params=_params(48),
    )(kc_idx, *_in_hbm(list(grads) + list(gots)))
    return outs[:n], outs[n:]


def _exchange_group(parts):
    n = len(parts)

    def copies(srcs, outs, sems):
        send_sems, recv_sems = sems
        x, y, c = _mesh_pos()
        return [pltpu.make_async_remote_copy(
            src_ref=s.at[2 * chip[0] + chip[1]], dst_ref=o.at[j], send_sem=send_sems.at[j, i],
            recv_sem=recv_sems.at[j, i], device_id=(*chip, c), device_id_type=MESH)
            for j, chip in enumerate(_other_chips(x, y)) for i, (s, o) in enumerate(zip(srcs, outs))]

    return _each_copy(copies, parts, [pltpu.HBM((3,) + p.shape[1:], BF) for p in parts],
                      [pltpu.SemaphoreType.DMA((3, n)), pltpu.SemaphoreType.DMA((3, n))])


def _add_chips_group(tag, kc_idx, parts, arrived):
    n = len(parts)

    def body(kc_ref, *refs):
        for p, rx, t in zip(refs[:n], refs[n:2 * n], refs[2 * n:]):
            t[...] = ((p[...] + rx[0].astype(F32)) + rx[1].astype(F32)) + rx[2].astype(F32)

    outs = _pallas_call(
        body, name="add_chips_" + tag,
        grid_spec=pltpu.PrefetchScalarGridSpec(
            num_scalar_prefetch=1, grid=(1,),
            in_specs=([pl.BlockSpec(rx.shape[1:], lambda i, kc_ref: (0, 0)) for rx in arrived]
                      + [pl.BlockSpec(rx.shape, lambda i, kc_ref: (0, 0, 0)) for rx in arrived]),
            out_specs=[pl.BlockSpec((None,) + rx.shape[1:], lambda i, kc_ref: (kc_ref[1], 0, 0)) for rx in arrived]),
        out_shape=[pltpu.HBM((2,) + rx.shape[1:], F32) for rx in arrived],
        compiler_params=_params(48),
    )(kc_idx, *_in_hbm(list(parts) + list(arrived)))
    return list(outs)


def _join_group(halves):
    n = len(halves)

    def copies(bufs, sems):
        send_sems, recv_sems = sems
        x, y, c = _mesh_pos()
        sib = (x, y, 1 - c)
        sends = [pltpu.make_async_remote_copy(src_ref=b.at[c], dst_ref=b.at[c], send_sem=send_sems.at[i],
                                              recv_sem=recv_sems.at[i], device_id=sib, device_id_type=MESH)
                 for i, b in enumerate(bufs)]
        landed = [pltpu.make_async_remote_copy(src_ref=b.at[c], dst_ref=b.at[1 - c], send_sem=send_sems.at[i],
                                               recv_sem=recv_sems.at[i], device_id=sib, device_id_type=MESH)
                  for i, b in enumerate(bufs)]
        return sends, landed

    def start(_, bufs, sems):
        for cp in copies(bufs, sems)[0]:
            cp.start()

    def finish(_, bufs, sems):
        sends, landed = copies(bufs, sems)
        for cp in landed:
            cp.wait_recv()
        for cp in sends:
            cp.wait_send()

    return _Carried(halves, [pltpu.HBM(h.shape, F32) for h in halves],
                    [pltpu.SemaphoreType.DMA((n,)), pltpu.SemaphoreType.DMA((n,))], start, finish,
                    {i: i for i in range(n)})


def _combine(carries):
    operands, out_shapes, sems, aliases, spans = [], [], [], {}, []
    for c in carries:
        aliases.update({len(operands) + i: len(out_shapes) + o for i, o in c.aliases.items()})
        spans.append((len(operands), len(out_shapes), len(sems)))
        operands += list(c.operands)
        out_shapes += list(c.out_shapes)
        sems += list(c.sems)

    def each(phase):
        def run(ins, outs, sem_refs):
            for c, (a, b, s) in zip(carries, spans):
                getattr(c, phase)(ins[a:a + len(c.operands)], outs[b:b + len(c.out_shapes)], sem_refs[s:s + len(c.sems)])
        return run

    return _Carried(operands, out_shapes, sems, each("start"), each("finish"), aliases)


def _allreduce_small(arrays, wire):
    n = len(arrays)
    halves = [(a.shape[0], a.shape[1] // 2) for a in arrays]

    def body(*refs):
        srcs, outs = refs[:n], refs[n:2 * n]
        mine_bufs, sib_bufs, chip_bufs, total_bufs = (refs[k * n:(k + 1) * n] for k in range(2, 6))
        send_sems, recv_sems, local_sems = refs[6 * n:]
        x, y, c = _mesh_pos()
        k0 = 2 * x + y
        sib = (x, y, 1 - c)

        def remote(src, dst, j, i, to):
            return pltpu.make_async_remote_copy(src_ref=src, dst_ref=dst, send_sem=send_sems.at[j, i],
                                                recv_sem=recv_sems.at[j, i], device_id=to, device_id_type=MESH)

        def cols(ref, i, core):
            return ref.at[:, pl.ds(pl.multiple_of(core * halves[i][1], LANE), halves[i][1])]

        swaps = [remote(cols(s, i, 1 - c), b, 0, i, sib) for i, (s, b) in enumerate(zip(srcs, sib_bufs))]
        own = [pltpu.make_async_copy(cols(s, i, c), m, local_sems.at[i]) for i, (s, m) in enumerate(zip(srcs, mine_bufs))]
        for cp in swaps + own:
            cp.start()
        for cp in swaps + own:
            cp.wait()
        for m, b, buf in zip(mine_bufs, sib_bufs, chip_bufs):
            buf[k0] = (m[...] + b[...]).astype(buf.dtype)
        chips = _other_chips(x, y)
        sends = [remote(buf.at[k0], buf.at[k0], 1 + j, i, (*chip, c))
                 for j, chip in enumerate(chips) for i, buf in enumerate(chip_bufs)]
        for cp in sends:
            cp.start()
        for j, chip in enumerate(chips):
            for i, buf in enumerate(chip_bufs):
                remote(buf.at[k0], buf.at[2 * chip[0] + chip[1]], 1 + j, i, (*chip, c)).wait_recv()
        for cp in sends:
            cp.wait_send()
        for t, buf in zip(total_bufs, chip_bufs):
            t[...] = ((buf[0].astype(F32) + buf[1].astype(F32)) + buf[2].astype(F32)) + buf[3].astype(F32)
        joins = [remote(t, cols(o, i, c), 4, i, sib) for i, (t, o) in enumerate(zip(total_bufs, outs))]
        keep = [pltpu.make_async_copy(t, cols(o, i, c), local_sems.at[i]) for i, (t, o) in enumerate(zip(total_bufs, outs))]
        for cp in joins + keep:
            cp.start()
        for i, (t, o) in enumerate(zip(total_bufs, outs)):
            remote(t, cols(o, i, 1 - c), 4, i, sib).wait_recv()
        for cp in joins:
            cp.wait_send()
        for cp in keep:
            cp.wait()

    specs = [_full(a.shape) for a in arrays]
    return _pallas_call(
        body, name="allreduce_small", grid=(1,), in_specs=specs, out_specs=specs,
        out_shape=[_sds(a.shape) for a in arrays],
        scratch_shapes=([pltpu.VMEM(h, F32) for h in halves] + [pltpu.VMEM(h, F32) for h in halves]
                        + [pltpu.VMEM((NCHIP,) + h, dt) for h, dt in zip(halves, wire)] + [pltpu.VMEM(h, F32) for h in halves]
                        + [pltpu.SemaphoreType.DMA((5, n)), pltpu.SemaphoreType.DMA((5, n)), pltpu.SemaphoreType.DMA((n,))]),
        compiler_params=_params(32),
    )(*arrays)


def _adamw_terms(w, g, m, v):
    m = ADAM_B1 * m + (1.0 - ADAM_B1) * g
    v = ADAM_B2 * v + (1.0 - ADAM_B2) * jnp.square(g)
    m_hat = m / (1.0 - ADAM_B1 ** ADAM_STEP)
    v_hat = v / (1.0 - ADAM_B2 ** ADAM_STEP)
    return -ADAM_LR * (m_hat / (jnp.sqrt(v_hat) + ADAM_EPS) + ADAM_WD * w), m, v


ADAM_STEPS = 4


def _adamw_group(tag, ws, gs, ms, vs):
    n = len(ws)

    def body(*refs):
        ins, outs = refs[:4 * n], refs[4 * n:]
        for i in range(n):
            w, g, m, v = (ins[k * n + i][...] for k in range(4))
            outs[i][...] = g
            outs[n + i][...], outs[2 * n + i][...], outs[3 * n + i][...] = _adamw_terms(w, g, m, v)

    specs = [pl.BlockSpec((w.shape[0] // ADAM_STEPS, w.shape[1]), lambda i: (i, 0)) for w in ws]
    outs = _pallas_call(
        body, name="adamw_" + tag, grid=(ADAM_STEPS,), in_specs=specs * 4, out_specs=specs * 4,
        out_shape=[_sds(w.shape) for w in ws] * 4, compiler_params=_params(48),
    )(*_in_hbm(list(ws) + list(gs) + list(ms) + list(vs)))
    return outs[:n], outs[n:2 * n], outs[2 * n:3 * n], outs[3 * n:]


def _adamw_replicated(sums, row_of, direct):
    ns, nr, nd = len(sums), len(row_of), len(direct)

    def body(*refs):
        sum_refs = refs[:ns]
        ins = refs[ns:ns + 3 * nr + 4 * nd]
        outs = refs[ns + 3 * nr + 4 * nd:]
        for i, (_, _, _, si, row) in enumerate(row_of):
            w_ref, m_ref, v_ref = ins[3 * i:3 * i + 3]
            g = sum_refs[si][row:row + 1, :]
            outs[4 * i][...] = g
            outs[4 * i + 1][...], outs[4 * i + 2][...], outs[4 * i + 3][...] = _adamw_terms(w_ref[...], g, m_ref[...], v_ref[...])
        for i in range(nd):
            w_ref, m_ref, v_ref, g_ref = ins[3 * nr + 4 * i:3 * nr + 4 * i + 4]
            o = outs[4 * (nr + i):4 * (nr + i) + 4]
            g = g_ref[...]
            o[0][...] = g
            o[1][...], o[2][...], o[3][...] = _adamw_terms(w_ref[...], g, m_ref[...], v_ref[...])

    operands = list(sums)
    shapes = []
    for w, m, v, _, _ in row_of:
        operands += [w, m, v]
        shapes += [w.shape] * 4
    for w, m, v, g in direct:
        operands += [w, m, v, g]
        shapes += [w.shape] * 4
    flat = _pallas_call(
        body, name="adamw_replicated", grid=(1,), in_specs=[_full(a.shape) for a in operands],
        out_specs=[_full(s) for s in shapes], out_shape=[_sds(s) for s in shapes],
        compiler_params=_params(56),
    )(*operands)
    return [flat[4 * i:4 * i + 4] for i in range(nr + nd)]


class _Exchanges:
    def __init__(self, shards, conv_w, chip, core, apply):
        self.shards, self.conv_w, self.apply = shards, conv_w, apply
        self.active, self.calls = [], 0
        self.chip_core_idx = jnp.stack([chip, core]).astype(jnp.int32)

    def first(self):
        names = ["w_in", "w_glu"]
        got = _run_now("gather_first", _gather_group([self.shards[n] for n in names] + [self.conv_w],
                                                     [True, True, False]))
        out = dict(zip(names, got))
        out["conv_w"] = jnp.transpose(got[2], (1, 0, 2)).reshape(4, LW)
        return out

    def gather(self, names):
        return _gather_group([self.shards[n] for n in names], [True] * len(names))

    def reduce(self, tag, grads):
        self.active.append({"tag": tag, "names": list(grads), "stage": 0, "grads": list(grads.values())})

    def run(self, call, hold=()):
        groups = [g for g in self.active if g["tag"] not in hold]
        carries = [self._exchange_of(g) for g in groups]
        carry = _combine(carries)
        outs = list(call(carry))
        own = len(outs) - len(carry.out_shapes)
        landed = outs[own:]
        for g, c in zip(groups, carries):
            self._sum_after(g, landed[:len(c.out_shapes)])
            landed = landed[len(c.out_shapes):]
        self.active = [g for g in self.active if g["stage"] < 3]
        return outs[:own]

    def _exchange_of(self, g):
        if g["stage"] == 0:
            return _swap_group(g["grads"])
        if g["stage"] == 1:
            return _exchange_group(g["bf16"])
        return _join_group(g["halves"])

    def _sum_after(self, g, landed):
        if g["stage"] == 0:
            g["f32"], g["bf16"] = _add_sibling_group(g["tag"], self.chip_core_idx, g["grads"], landed)
        elif g["stage"] == 1:
            g["halves"] = _add_chips_group(g["tag"], self.chip_core_idx, g["f32"], landed)
        else:
            self.apply(g["tag"], g["names"], [t.reshape(2 * t.shape[1], t.shape[2]) for t in landed])
        g["stage"] += 1

    def drain(self):
        while self.active:
            self.calls += 1
            self.run(lambda carry: _run_now("reduce_%d" % self.calls, carry))


INPUT_NAMES = (["x", "p"] + [n for n in
               ["g_mix", "w_in", "b_in", "lam_re", "lam_im", "log_dt", "s5_b_re", "s5_b_im", "s5_c_re", "s5_c_im", "s5_d",
                "w_glu", "b_glu", "conv_w", "conv_b", "w_r", "b_r", "w_i", "b_i", "lru_lambda", "w_a_out", "w_b_out", "w_o",
                "g_ffn", "w_ffn_gate", "w_ffn_up", "w_ffn_down", "g_ple_gate", "w_ple_gate", "b_ple_gate", "w_ple", "g_ple",
                "g_final"]])
WEIGHT_NAMES = INPUT_NAMES[2:]


def kernel(*args):
    names = INPUT_NAMES + ["loss_target"] + ["m_" + n for n in WEIGHT_NAMES] + ["v_" + n for n in WEIGHT_NAMES]
    assert len(args) == len(names)
    given = dict(zip(names, args))

    def view(name):
        a = given[name]
        return jnp.swapaxes(a, -1, -2) if name.endswith(TRANSPOSED) else a

    def unview(name, a):
        return jnp.swapaxes(a, -1, -2) if name in TRANSPOSED else a

    def local(name):
        return view(name) if name.endswith("g_final") else view(name)[0]

    xi, yi, ci = _mesh_pos()
    k0 = 2 * xi + yi
    x, p, tgt = given["x"][0], given["p"][0, 0], given["loss_target"][0]

    results = {}

    row_halves = {}

    def apply(tag, names, totals):
        totals = dict(zip(names, totals))
        row_halves.update({n: totals.pop(n) for n in names if n in ("w_in_lo", "w_in_hi")})
        if len(row_halves) == 2:
            totals["w_in"] = jnp.concatenate([row_halves.pop("w_in_lo"), row_halves.pop("w_in_hi")])
        names = list(totals)
        if not names:
            return
        new = _adamw_group(tag, [local(n) for n in names], list(totals.values()), [local("m_" + n) for n in names],
                           [local("v_" + n) for n in names])
        for kind, arrays in zip(("grad", "delta", "new_m", "new_v"), new):
            for n, arr in zip(names, arrays):
                results[kind, n] = unview(n, arr[None])

    comm = _Exchanges({n: local(n).astype(BF) for n, _ in SHARDED}, local("conv_w"), k0, ci, apply)
    w = {n: local(n) for n in WEIGHT_NAMES if n != "conv_w" and n not in dict(SHARDED)}
    gx, sums, blocks = _local_step(x, p, tgt, w, comm)

    sum_names, block_names = list(sums), list(blocks)
    red = _allreduce_small([sums[n] for n in sum_names] + [blocks[n] for n in block_names],
                           [F32] * len(sum_names) + [BF] * len(block_names))
    sums = dict(zip(sum_names, red[:len(sum_names)]))
    blocks = dict(zip(block_names, red[len(sum_names):]))
    loss = jnp.sum(sums[LOSS_ROW[0]][LOSS_ROW[1]])
    direct_g = _replicated_grads(w, sums, blocks)
    conv_rows = sums[CONV_W_ROWS[0]][CONV_W_ROWS[1]:CONV_W_ROWS[1] + 4]
    direct_g["conv_w"] = lax.dynamic_slice(conv_rows, (0, k0 * CONV_SHARD[1]), CONV_SHARD)
    as_row = lambda a: a.reshape(1, -1)
    row_names = list(ACC_ROWS)
    row_of = [(as_row(given[n]), as_row(given["m_" + n]), as_row(given["v_" + n]),
               sum_names.index(ACC_ROWS[n][0]), ACC_ROWS[n][1]) for n in row_names]
    direct_names = list(direct_g)
    direct = [(view(n), view("m_" + n), view("v_" + n), direct_g[n].reshape(view(n).shape)) for n in direct_names]
    done = _adamw_replicated([sums[n] for n in sum_names], row_of, direct)
    for n, four in zip(row_names + direct_names, done):
        for kind, arr in zip(("grad", "delta", "new_m", "new_v"), four):
            results[kind, n] = unview(n, arr).reshape(given[n].shape)

    out = [loss, gx[None]]
    for kind in ("grad", "delta", "new_m", "new_v"):
        out += [results[kind, n] for n in WEIGHT_NAMES]
    return tuple(out)
```

```python
import functools
import math

import jax
import jax.numpy as jnp
from jax import lax
from jax.experimental import pallas as pl
from jax.experimental.pallas import tpu as pltpu

F32 = jnp.float32
BF = jnp.bfloat16

D = 1024
S5W = 512
NG, NS, NP = 32, 64, 16
GN = NG * NS
LW = 1024
NH, HD = 16, 64
LRU_C = 8.0
FH = 2816
NCHIP = 4
FC = FH // NCHIP
PLE = 256
INC = S5W + LW + 2 * D
EPS = 1e-6
ADAM_LR, ADAM_B1, ADAM_B2, ADAM_EPS, ADAM_WD, ADAM_STEP = 0.001, 0.9, 0.999, 1e-08, 0.01, 10

TM = 256
TK = 1024
LC = 512
SUB = 8
VMEM_MB = 1024 * 1024
MESH = pl.DeviceIdType.MESH
ANY = pl.BlockSpec(memory_space=pl.ANY)


def _mm(a, b):
    return jnp.dot(a.astype(BF), b.astype(BF), preferred_element_type=F32)


def _mm_nt(a, b):
    return lax.dot_general(a.astype(BF), b.astype(BF), (((1,), (1,)), ((), ())), preferred_element_type=F32)


def _mm_tn(a, b):
    return lax.dot_general(a.astype(BF), b.astype(BF), (((0,), (0,)), ((), ())), preferred_element_type=F32)


def _blockdiag_mm(x, blocks_ref):
    n, rows, _ = blocks_ref.shape
    return jnp.concatenate([jnp.dot(x[:, j * rows:(j + 1) * rows], blocks_ref[j], preferred_element_type=F32)
                            for j in range(n)], axis=1)


def _blockdiag_mm_t(x, blocks_ref):
    n, _, wide = blocks_ref.shape
    return jnp.concatenate([lax.dot_general(x[:, j * wide:(j + 1) * wide], blocks_ref[j], (((1,), (1,)), ((), ())),
                                            preferred_element_type=F32) for j in range(n)], axis=1)


def _rms(x):
    r = lax.rsqrt(jnp.mean(x * x, axis=-1, keepdims=True) + EPS)
    return x * r, r


def _rms_bwd(dy, xh, r, g):
    dxh = dy * g
    return r * (dxh - xh * jnp.mean(dxh * xh, axis=-1, keepdims=True))


def _colsum(x):
    return jnp.sum(x, axis=0, keepdims=True)


def _sig(x):
    return jax.nn.sigmoid(x)


def _gelu_grad(x):
    c = math.sqrt(2.0 / math.pi)
    t = jnp.tanh(c * (x + 0.044715 * x * x * x))
    return 0.5 * (1.0 + t) + 0.5 * x * (1.0 - t * t) * c * (1.0 + 3.0 * 0.044715 * x * x)


def _neg_expm1(x):
    series = -x * (1.0 + x * (0.5 + x * (1.0 / 6.0 + x * (1.0 / 24.0))))
    return jnp.where(x > -0.03, series, 1.0 - jnp.exp(x))


def _tok(width):
    return pl.BlockSpec((TM, width), lambda i: (i, 0))


def _tok_rev(width, nt):
    return pl.BlockSpec((TM, width), lambda i: (nt - 1 - i, 0))


def _full(shape):
    return pl.BlockSpec(shape, lambda i: (0,) * len(shape))


def _params(vmem_mb, **kw):
    return pltpu.CompilerParams(dimension_semantics=("arbitrary",), vmem_limit_bytes=vmem_mb * VMEM_MB, **kw)


def _sds(shape, dtype=F32):
    return jax.ShapeDtypeStruct(shape, dtype)


class _Carried:
    def __init__(self, operands, out_shapes, sems, start, finish, aliases=None):
        self.operands, self.out_shapes, self.sems = list(operands), list(out_shapes), list(sems)
        self.start, self.finish, self.aliases = start, finish, dict(aliases or {})


def _in_hbm(arrays):
    return [pltpu.with_memory_space_constraint(a, pltpu.HBM) for a in arrays]


def _pallas_call(body, carry=None, **kw):
    if carry is None:
        return pl.pallas_call(body, **kw)

    def at_step(corner):
        hit = [pl.program_id(d) == (size - 1 if corner else 0) for d, size in enumerate(kw["grid"])]
        return functools.reduce(jnp.logical_and, hit)

    name, grid, compiler_params = kw["name"], kw["grid"], kw["compiler_params"]
    in_specs, out_specs, out_shape = list(kw["in_specs"]), list(kw["out_specs"]), list(kw["out_shape"])
    scratch_shapes = list(kw.get("scratch_shapes", ()))
    n_in, n_out, n_scr = len(in_specs), len(out_specs), len(scratch_shapes)
    c_in, c_out = len(carry.operands), len(carry.out_shapes)

    def full_body(*refs):
        ins, refs = refs[:n_in], refs[n_in:]
        c_ins, refs = refs[:c_in], refs[c_in:]
        outs, refs = refs[:n_out], refs[n_out:]
        c_outs, refs = refs[:c_out], refs[c_out:]
        scratch, c_sems = refs[:n_scr], refs[n_scr:]

        @pl.when(at_step(0))
        def _():
            carry.start(c_ins, c_outs, c_sems)

        body(*ins, *outs, *scratch)

        @pl.when(at_step(1))
        def _():
            carry.finish(c_ins, c_outs, c_sems)

    call = pl.pallas_call(
        full_body, name=name, grid=grid, in_specs=in_specs + [ANY] * c_in, out_specs=out_specs + [ANY] * c_out,
        out_shape=out_shape + list(carry.out_shapes), scratch_shapes=scratch_shapes + list(carry.sems),
        input_output_aliases={n_in + i: n_out + o for i, o in carry.aliases.items()},
        compiler_params=compiler_params)
    return lambda *operands: call(*operands, *_in_hbm(carry.operands))


def _resident(pairs, sems):
    first = pl.program_id(0) == 0
    copies = [pltpu.make_async_copy(src, dst, sems.at[j]) for j, (src, dst) in enumerate(pairs)]

    @pl.when(first)
    def _():
        for cp in copies:
            cp.start()

    def wait(j):
        @pl.when(first)
        def _():
            copies[j].wait()

    return wait


def _resident_now(pairs, sems):
    @pl.when(pl.program_id(0) == 0)
    def _():
        copies = [pltpu.make_async_copy(src, dst, sems.at[j]) for j, (src, dst) in enumerate(pairs)]
        for cp in copies:
            cp.start()
        for cp in copies:
            cp.wait()


def _row_iota(width):
    return lax.broadcasted_iota(jnp.int32, (SUB, width), 0)


def _bcast_row(x, row):
    return jnp.broadcast_to(x[row:row + 1, :], x.shape)


def _slab(k):
    return pl.ds(pl.multiple_of(k * SUB, SUB), SUB)


QC = INC // NCHIP
Z_PARTS = ((0, S5W), (S5W, S5W + LW), (S5W + LW, INC))


def _inproj_fwd(x, g_mix, w_in, b_in, carry=None):
    L = x.shape[0]

    def body(x_ref, g_ref, w_hbm, b_ref, h_ref, ua_ref, ub_ref, gp_ref, w_vm, w_sems):
        _resident_now([(w_hbm.at[k], w_vm.at[k]) for k in range(NCHIP)], w_sems)
        xh, _ = _rms(x_ref[...])
        h = (xh * g_ref[...]).astype(BF)
        h_ref[...] = h
        for k in range(NCHIP):
            lo, hi = k * QC, (k + 1) * QC
            z = jnp.dot(h, w_vm[k], preferred_element_type=F32) + b_ref[:, lo:hi]
            for ref, (a, b) in zip((ua_ref, ub_ref, gp_ref), Z_PARTS):
                s, e = max(lo, a), min(hi, b)
                if s < e:
                    ref[:, s - a:e - a] = z[:, s - lo:e - lo]

    return _pallas_call(
        body, carry, name="inproj_fwd", grid=(L // TM,),
        in_specs=[_tok(D), _full((1, D)), ANY, _full((1, INC))],
        out_specs=[_tok(D), _tok(S5W), _tok(LW), _tok(2 * D)],
        out_shape=[_sds((L, D), BF), _sds((L, S5W)), _sds((L, LW)), _sds((L, 2 * D))],
        scratch_shapes=[pltpu.VMEM((NCHIP, D, QC), BF), pltpu.SemaphoreType.DMA((NCHIP,))],
        compiler_params=_params(40),
    )(x, g_mix, w_in, b_in)


def _inproj_bwd(x, dx1, dua, dub, dgp, g_mix, w_in, carry=None):
    L = x.shape[0]

    def body(x_ref, dx1_ref, dua_ref, dub_ref, dgp_ref, g_ref, w_hbm, gx_ref, dz_ref, dg_ref, db_ref, w_vm, w_sems):
        _resident_now([(w_hbm.at[k], w_vm.at[k]) for k in range(NCHIP)], w_sems)

        @pl.when(pl.program_id(0) == 0)
        def _():
            dg_ref[...] = jnp.zeros_like(dg_ref)
            db_ref[...] = jnp.zeros_like(db_ref)

        for src, (a, b) in zip((dua_ref, dub_ref, dgp_ref), Z_PARTS):
            d = src[...]
            dz_ref[:, a:b] = d.astype(BF)
            db_ref[0:1, a:b] += _colsum(d)
        dh = jnp.zeros((TM, D), F32)
        for k in range(NCHIP):
            dh = dh + lax.dot_general(dz_ref[:, k * QC:(k + 1) * QC], w_vm[k], (((1,), (1,)), ((), ())),
                                      preferred_element_type=F32)
        xh, r = _rms(x_ref[...])
        dg_ref[0:1, :] += _colsum(dh * xh)
        gx_ref[...] = dx1_ref[...] + _rms_bwd(dh, xh, r, g_ref[...])

    return _pallas_call(
        body, carry, name="inproj_bwd", grid=(L // TM,),
        in_specs=[_tok(D), _tok(D), _tok(S5W), _tok(LW), _tok(2 * D), _full((1, D)), ANY],
        out_specs=[_tok(D), _tok(INC), _full((SUB, D)), _full((SUB, INC))],
        out_shape=[_sds((L, D)), _sds((L, INC), BF), _sds((SUB, D)), _sds((SUB, INC))],
        scratch_shapes=[pltpu.VMEM((NCHIP, D, QC), BF), pltpu.SemaphoreType.DMA((NCHIP,))],
        compiler_params=_params(40),
    )(x, dx1, dua, dub, dgp, g_mix, w_in)


def _cscan(xr_ref, xi_ref, con_ref, cr_ref, ci_ref, reverse):
    n_slab = xr_ref.shape[0] // SUB
    width = xr_ref.shape[1]
    for lc in range(width // LC):
        cols = slice(lc * LC, (lc + 1) * LC)
        con = [con_ref[SUB * j:SUB * (j + 1), cols] for j in range(8)]

        def step(k, carry, cols=cols, con=con):
            cr, ci = carry
            rows = _slab(n_slab - 1 - k if reverse else k)
            xr, xi = xr_ref[rows, cols], xi_ref[rows, cols]
            for j, sh in enumerate((1, 2, 4)):
                mr, mi = con[2 * j], con[2 * j + 1]
                pr = pltpu.roll(xr, SUB - sh if reverse else sh, 0)
                pi = pltpu.roll(xi, SUB - sh if reverse else sh, 0)
                xr, xi = xr + mr * pr - mi * pi, xi + mr * pi + mi * pr
            xr, xi = xr + con[6] * cr - con[7] * ci, xi + con[6] * ci + con[7] * cr
            xr_ref[rows, cols] = xr
            xi_ref[rows, cols] = xi
            row = 0 if reverse else SUB - 1
            return _bcast_row(xr, row), _bcast_row(xi, row)

        cr, ci = lax.fori_loop(0, n_slab, step, (cr_ref[:, cols], ci_ref[:, cols]))
        cr_ref[:, cols] = cr
        ci_ref[:, cols] = ci


def _s5_fwd(ua, bbr, bbi, ccr, cci, dsk, con, w_glu, b_glu, carry=None):
    L = ua.shape[0]

    def body(ua_ref, bbr_hbm, bbi_hbm, ccr_hbm, cci_hbm, dsk_ref, con_ref, wg_ref, bg_ref,
             sr_ref, si_ref, y_ref, zg_ref, ya_ref, bbr_vm, bbi_vm, ccr_vm, cci_vm, cr_ref, ci_ref, w_sems):
        landed = _resident([(bbr_hbm, bbr_vm), (bbi_hbm, bbi_vm), (ccr_hbm, ccr_vm), (cci_hbm, cci_vm)], w_sems)

        @pl.when(pl.program_id(0) == 0)
        def _():
            cr_ref[...] = jnp.zeros_like(cr_ref)
            ci_ref[...] = jnp.zeros_like(ci_ref)

        u = ua_ref[...]
        ub = u.astype(BF)
        landed(0)
        sr_ref[...] = _blockdiag_mm(ub, bbr_vm)
        landed(1)
        si_ref[...] = _blockdiag_mm(ub, bbi_vm)
        _cscan(sr_ref, si_ref, con_ref, cr_ref, ci_ref, reverse=False)
        landed(2)
        landed(3)
        y = (_blockdiag_mm_t(sr_ref[...].astype(BF), ccr_vm) - _blockdiag_mm_t(si_ref[...].astype(BF), cci_vm)
             + dsk_ref[...] * u)
        y_ref[...] = y
        zg = jax.nn.gelu(y)
        zg_ref[...] = zg.astype(BF)
        q = _mm(zg, wg_ref[...]) + bg_ref[...]
        ya_ref[...] = (zg * _sig(q)).astype(BF)

    return _pallas_call(
        body, carry, name="s5_fwd", grid=(L // TM,),
        in_specs=[_tok(S5W), ANY, ANY, ANY, ANY, _full((1, S5W)), _full((8 * SUB, GN)),
                  _full((S5W, S5W)), _full((1, S5W))],
        out_specs=[_tok(GN), _tok(GN), _tok(S5W), _tok(S5W), _tok(S5W)],
        out_shape=[_sds((L, GN)), _sds((L, GN)), _sds((L, S5W)), _sds((L, S5W), BF), _sds((L, S5W), BF)],
        scratch_shapes=[pltpu.VMEM((S5W // 128, 128, GN // (S5W // 128)), BF)] * 4 + [
                        pltpu.VMEM((SUB, GN), F32), pltpu.VMEM((SUB, GN), F32),
                        pltpu.SemaphoreType.DMA((4,))],
        compiler_params=_params(44),
    )(ua, bbr, bbi, ccr, cci, dsk, con, w_glu, b_glu)


def _s5_bwd(dya, y, ua, sr, si, bbr, bbi, ccr, cci, dsk, con_rev, w_glu, b_glu, carry=None):
    L = ua.shape[0]
    nt = L // TM
    spt = TM // SUB
    n_slab = spt

    def halo_map(i):
        return (jnp.maximum((nt - 1 - i) * spt - 1, 0), 0)

    def body(dya_ref, y_ref, ua_ref, sr_ref, si_ref, hr_ref, hi_ref, bbr_hbm, bbi_hbm, ccr_hbm, cci_hbm,
             dsk_ref, con_ref, wg_ref, bg_ref,
             dua_ref, dq_ref, dy_ref, lr_ref, li_ref, da_ref, dsm_ref,
             bbr_vm, bbi_vm, ccr_vm, cci_vm, cr_ref, ci_ref, w_sems):
        i = pl.program_id(0)
        landed = _resident([(ccr_hbm, ccr_vm), (cci_hbm, cci_vm), (bbr_hbm, bbr_vm), (bbi_hbm, bbi_vm)], w_sems)

        @pl.when(i == 0)
        def _():
            cr_ref[...] = jnp.zeros_like(cr_ref)
            ci_ref[...] = jnp.zeros_like(ci_ref)
            da_ref[...] = jnp.zeros_like(da_ref)
            dsm_ref[...] = jnp.zeros_like(dsm_ref)

        u = ua_ref[...]
        yv = y_ref[...]
        dya = dya_ref[...]
        zg = jax.nn.gelu(yv)
        sg = _sig(_mm(zg, wg_ref[...]) + bg_ref[...])
        dq = dya * zg * sg * (1.0 - sg)
        dq_ref[...] = dq.astype(BF)
        dzg = dya * sg + _mm_nt(dq, wg_ref[...])
        dy = dzg * _gelu_grad(yv)
        dyb = dy.astype(BF)
        dy_ref[...] = dyb
        dsm_ref[0:1, :] += _colsum(dy * u)
        dsm_ref[1:2, :] += _colsum(dq)
        landed(0)
        lr_ref[...] = _blockdiag_mm(dyb, ccr_vm)
        landed(1)
        li_ref[...] = -_blockdiag_mm(dyb, cci_vm)
        _cscan(lr_ref, li_ref, con_ref, cr_ref, ci_ref, reverse=True)

        first_tile = (i == nt - 1)
        row = _row_iota(LC)
        for lc in range(GN // LC):
            cols = slice(lc * LC, (lc + 1) * LC)
            h_r = jnp.where(first_tile, 0.0, hr_ref[:, cols])
            h_i = jnp.where(first_tile, 0.0, hi_ref[:, cols])

            def step(k, acc, cols=cols, h_r=h_r, h_i=h_i):
                ar, ai = acc
                rows = _slab(k)
                prev = _slab(jnp.maximum(k - 1, 0))
                pr = jnp.where(k == 0, h_r, sr_ref[prev, cols])
                pi = jnp.where(k == 0, h_i, si_ref[prev, cols])
                spr = pltpu.roll(jnp.where(row == SUB - 1, pr, sr_ref[rows, cols]), 1, 0)
                spi = pltpu.roll(jnp.where(row == SUB - 1, pi, si_ref[rows, cols]), 1, 0)
                lr, li = lr_ref[rows, cols], li_ref[rows, cols]
                return ar + lr * spr + li * spi, ai + li * spr - lr * spi

            zero = jnp.zeros((SUB, LC), F32)
            ar, ai = lax.fori_loop(0, n_slab, step, (zero, zero))
            da_ref[0:1, cols] += _colsum(ar)
            da_ref[1:2, cols] += _colsum(ai)

        landed(2)
        landed(3)
        dua_ref[...] = (dy * dsk_ref[...] + _blockdiag_mm_t(lr_ref[...].astype(BF), bbr_vm)
                        + _blockdiag_mm_t(li_ref[...].astype(BF), bbi_vm))

    return _pallas_call(
        body, carry, name="s5_bwd", grid=(nt,),
        in_specs=[_tok_rev(S5W, nt), _tok_rev(S5W, nt), _tok_rev(S5W, nt), _tok_rev(GN, nt), _tok_rev(GN, nt),
                  pl.BlockSpec((SUB, GN), halo_map), pl.BlockSpec((SUB, GN), halo_map),
                  ANY, ANY, ANY, ANY, _full((1, S5W)), _full((8 * SUB, GN)), _full((S5W, S5W)), _full((1, S5W))],
        out_specs=[_tok_rev(S5W, nt), _tok_rev(S5W, nt), _tok_rev(S5W, nt), _tok_rev(GN, nt), _tok_rev(GN, nt),
                   _full((SUB, GN)), _full((SUB, S5W))],
        out_shape=[_sds((L, S5W)), _sds((L, S5W), BF), _sds((L, S5W), BF), _sds((L, GN)), _sds((L, GN)),
                   _sds((SUB, GN)), _sds((SUB, S5W))],
        scratch_shapes=[pltpu.VMEM((S5W // 128, 128, GN // (S5W // 128)), BF)] * 4 + [
                        pltpu.VMEM((SUB, GN), F32), pltpu.VMEM((SUB, GN), F32),
                        pltpu.SemaphoreType.DMA((4,))],
        compiler_params=_params(52),
    )(dya, y, ua, sr, si, sr, si, bbr, bbi, ccr, cci, dsk, con_rev, w_glu, b_glu)


def _lru_gate_terms(rg, sp):
    log_a = -LRU_C * rg * sp
    a = jnp.exp(log_a)
    mult = jnp.sqrt(_neg_expm1(2.0 * log_a))
    return a, mult


def _lru_fwd(ub, conv_w, conv_b, wr, wi, b_r, b_i, sp, carry=None):
    L = ub.shape[0]
    n_slab = TM // SUB

    def body(ub_ref, cw_ref, cb_ref, wr_ref, wi_ref, br_ref, bi_ref, sp_ref,
             xc_ref, rg_ref, ig_ref, h_ref, hp_ref, a_ref, halo_ref, carry_ref):
        @pl.when(pl.program_id(0) == 0)
        def _():
            halo_ref[...] = jnp.zeros_like(halo_ref)
            carry_ref[...] = jnp.zeros_like(carry_ref)

        row = _row_iota(LW)
        taps = [cw_ref[k:k + 1, :] for k in range(4)]
        cb = cb_ref[...]

        def conv_step(k, prev):
            rows = _slab(k)
            cur = ub_ref[rows, :]
            acc = taps[3] * cur + cb
            for j in (1, 2, 3):
                acc = acc + taps[3 - j] * pltpu.roll(jnp.where(row >= SUB - j, prev, cur), j, 0)
            xc_ref[rows, :] = acc
            return cur

        halo_ref[...] = lax.fori_loop(0, n_slab, conv_step, halo_ref[...])

        xc = xc_ref[...]
        xcb = xc.astype(BF)
        rg = _sig(_blockdiag_mm(xcb, wr_ref) + br_ref[...])
        ig = _sig(_blockdiag_mm(xcb, wi_ref) + bi_ref[...])
        rg_ref[...] = rg
        ig_ref[...] = ig
        a, mult = _lru_gate_terms(rg, sp_ref[...])
        a_ref[...] = a
        h_ref[...] = mult * ig * xc

        rowc = _row_iota(LC)
        for lc in range(LW // LC):
            cols = slice(lc * LC, (lc + 1) * LC)

            def step(k, c, cols=cols):
                rows = _slab(k)
                av, b = a_ref[rows, cols], h_ref[rows, cols]
                for sh in (1, 2, 4):
                    keep = rowc >= sh
                    b = b + av * jnp.where(keep, pltpu.roll(b, sh, 0), 0.0)
                    av = av * jnp.where(keep, pltpu.roll(av, sh, 0), 1.0)
                h = b + av * c
                h_ref[rows, cols] = h
                hp_ref[rows, cols] = jnp.where(rowc == 0, c, pltpu.roll(h, 1, 0))
                return _bcast_row(h, SUB - 1)

            carry_ref[:, cols] = lax.fori_loop(0, n_slab, step, carry_ref[:, cols])

    return _pallas_call(
        body, carry, name="lru_fwd", grid=(L // TM,),
        in_specs=[_tok(LW), _full((4, LW)), _full((1, LW)), _full((LW // 128, 128, 128)), _full((LW // 128, 128, 128)),
                  _full((1, LW)), _full((1, LW)), _full((1, LW))],
        out_specs=[_tok(LW)] * 5,
        out_shape=[_sds((L, LW))] * 5,
        scratch_shapes=[pltpu.VMEM((TM, LW), F32), pltpu.VMEM((SUB, LW), F32), pltpu.VMEM((SUB, LW), F32)],
        compiler_params=_params(40),
    )(ub, conv_w, conv_b, wr, wi, b_r, b_i, sp)


def _lru_bwd(dyb, xc, rg, ig, hp, ub, conv_w, wr, wi, sp, dsp, carry=None):
    L = ub.shape[0]
    nt = L // TM
    spt = TM // SUB
    n_slab = spt

    def halo_map(i):
        return (jnp.maximum((nt - 1 - i) * spt - 1, 0), 0)

    def body(dh_ref, xc_ref, rg_ref, ig_ref, hp_ref, ub_ref, uh_ref, cw_ref, wr_ref, wi_ref, sp_ref, dsp_ref,
             dub_ref, dpr_ref, dpi_ref, acc_ref, a_ref, lam_ref, dxc_ref, carry_ref, next_ref):
        i = pl.program_id(0)

        @pl.when(i == 0)
        def _():
            carry_ref[...] = jnp.zeros_like(carry_ref)
            next_ref[...] = jnp.zeros_like(next_ref)
            acc_ref[...] = jnp.zeros_like(acc_ref)

        sp = sp_ref[...]
        rg, ig, xc = rg_ref[...], ig_ref[...], xc_ref[...]
        a, mult = _lru_gate_terms(rg, sp)
        a_ref[...] = a

        rowc = _row_iota(LC)
        for lc in range(LW // LC):
            cols = slice(lc * LC, (lc + 1) * LC)

            def step(k, c, cols=cols):
                rows = _slab(n_slab - 1 - k)
                av, dh = a_ref[rows, cols], dh_ref[rows, cols]
                b = av * dh
                for sh in (1, 2, 4):
                    keep = rowc < SUB - sh
                    b = b + av * jnp.where(keep, pltpu.roll(b, SUB - sh, 0), 0.0)
                    av = av * jnp.where(keep, pltpu.roll(av, SUB - sh, 0), 1.0)
                mu = b + av * c
                lam_ref[rows, cols] = dh + jnp.where(rowc == SUB - 1, c, pltpu.roll(mu, SUB - 1, 0))
                return _bcast_row(mu, 0)

            carry_ref[:, cols] = lax.fori_loop(0, n_slab, step, carry_ref[:, cols])

        lam = lam_ref[...]
        d_a = lam * hp_ref[...]
        d_mult = lam * ig * xc
        d_ig = lam * mult * xc
        dxc = lam * mult * ig
        d_log_a = d_a * a - d_mult * a * a / mult
        d_rg = (-LRU_C) * sp * d_log_a
        acc_ref[0:1, :] += _colsum((-LRU_C) * rg * d_log_a) * dsp_ref[...]
        dpr = d_rg * rg * (1.0 - rg)
        dpi = d_ig * ig * (1.0 - ig)
        acc_ref[1:2, :] += _colsum(dpr)
        acc_ref[2:3, :] += _colsum(dpi)
        dprb, dpib = dpr.astype(BF), dpi.astype(BF)
        dpr_ref[...] = dprb
        dpi_ref[...] = dpib
        dxc = dxc + _blockdiag_mm_t(dprb, wr_ref) + _blockdiag_mm_t(dpib, wi_ref)
        dxc_ref[...] = dxc
        acc_ref[3:4, :] += _colsum(dxc)

        row = _row_iota(LW)
        taps = [cw_ref[k:k + 1, :] for k in range(4)]
        u_halo = jnp.where(i == nt - 1, 0.0, uh_ref[...])
        nxt_tile = next_ref[...]

        def conv_step(k, accs):
            rows = _slab(k)
            cur = dxc_ref[rows, :]
            nxt = jnp.where(k == n_slab - 1, nxt_tile, dxc_ref[_slab(jnp.minimum(k + 1, n_slab - 1)), :])
            ucur = ub_ref[rows, :]
            uprev = jnp.where(k == 0, u_halo, ub_ref[_slab(jnp.maximum(k - 1, 0)), :])
            du = taps[3] * cur
            new = [accs[3] + cur * ucur]
            for j in (1, 2, 3):
                du = du + taps[3 - j] * pltpu.roll(jnp.where(row < j, nxt, cur), SUB - j, 0)
                new.append(accs[3 - j] + cur * pltpu.roll(jnp.where(row >= SUB - j, uprev, ucur), j, 0))
            dub_ref[rows, :] = du
            return tuple(new[::-1])

        zero = jnp.zeros((SUB, LW), F32)
        accs = lax.fori_loop(0, n_slab, conv_step, (zero, zero, zero, zero))
        for k in range(4):
            acc_ref[4 + k:5 + k, :] += _colsum(accs[k])
        next_ref[...] = dxc_ref[0:SUB, :]

    return _pallas_call(
        body, carry, name="lru_bwd", grid=(nt,),
        in_specs=[_tok_rev(LW, nt)] * 6 + [pl.BlockSpec((SUB, LW), halo_map), _full((4, LW)),
                                           _full((LW // 128, 128, 128)), _full((LW // 128, 128, 128)), _full((1, LW)), _full((1, LW))],
        out_specs=[_tok_rev(LW, nt), _tok_rev(LW, nt), _tok_rev(LW, nt), _full((SUB, LW))],
        out_shape=[_sds((L, LW)), _sds((L, LW), BF), _sds((L, LW), BF), _sds((SUB, LW))],
        scratch_shapes=[pltpu.VMEM((TM, LW), F32), pltpu.VMEM((TM, LW), F32), pltpu.VMEM((TM, LW), F32),
                        pltpu.VMEM((SUB, LW), F32), pltpu.VMEM((SUB, LW), F32)],
        compiler_params=_params(48),
    )(dyb, xc, rg, ig, hp, ub, ub, conv_w, wr, wi, sp, dsp)


AC = D // NCHIP


def _merge_fwd(x, ya, yb, gp, w_a, w_b, w_o, carry=None):
    L = x.shape[0]

    def body(x_ref, ya_ref, yb_ref, gp_ref, wa_ref, wb_ref, wo_ref, x1_ref, pa_ref, pb_ref, mg_ref):
        ya = ya_ref[...]
        for k in range(NCHIP):
            pa_ref[:, k * AC:(k + 1) * AC] = jnp.dot(ya, wa_ref[k], preferred_element_type=F32)
        pb = _mm(yb_ref[...], wb_ref[...])
        pb_ref[...] = pb
        gp = gp_ref[...]
        merged = (_sig(gp[:, :D]) * pa_ref[...] + _sig(gp[:, D:]) * pb).astype(BF)
        mg_ref[...] = merged
        x1_ref[...] = x_ref[...] + jnp.dot(merged, wo_ref[...], preferred_element_type=F32)

    return _pallas_call(
        body, carry, name="merge_fwd", grid=(L // TM,),
        in_specs=[_tok(D), _tok(S5W), _tok(LW), _tok(2 * D), _full((NCHIP, S5W, AC)), _full((LW, D)), _full((D, D))],
        out_specs=[_tok(D), _tok(D), _tok(D), _tok(D)],
        out_shape=[_sds((L, D)), _sds((L, D)), _sds((L, D)), _sds((L, D), BF)],
        compiler_params=_params(40),
    )(x, ya, yb, gp, w_a, w_b, w_o)


def _merge_bwd(dx1, gp, pa, pb, w_a, w_b, w_o, carry=None):
    L = dx1.shape[0]

    def body(dx1_ref, gp_ref, pa_ref, pb_ref, wa_ref, wb_ref, wo_ref, dya_ref, dyb_ref, dgp_ref, dpa_ref, dpb_ref):
        dm = _mm_nt(dx1_ref[...], wo_ref[...])
        gp = gp_ref[...]
        sa, sb = _sig(gp[:, :D]), _sig(gp[:, D:])
        dpa = (dm * sa).astype(BF)
        dpb = (dm * sb).astype(BF)
        dpa_ref[...] = dpa
        dpb_ref[...] = dpb
        dgp_ref[:, :D] = dm * pa_ref[...] * sa * (1.0 - sa)
        dgp_ref[:, D:] = dm * pb_ref[...] * sb * (1.0 - sb)
        dya = jnp.zeros((TM, S5W), F32)
        for k in range(NCHIP):
            dya = dya + _mm_nt(dpa[:, k * AC:(k + 1) * AC], wa_ref[k])
        dya_ref[...] = dya
        dyb_ref[...] = _mm_nt(dpb, wb_ref[...])

    return _pallas_call(
        body, carry, name="merge_bwd", grid=(L // TM,),
        in_specs=[_tok(D), _tok(2 * D), _tok(D), _tok(D), _full((NCHIP, S5W, AC)), _full((LW, D)), _full((D, D))],
        out_specs=[_tok(S5W), _tok(LW), _tok(2 * D), _tok(D), _tok(D)],
        out_shape=[_sds((L, S5W)), _sds((L, LW)), _sds((L, 2 * D)), _sds((L, D), BF), _sds((L, D), BF)],
        compiler_params=_params(40),
    )(dx1, gp, pa, pb, w_a, w_b, w_o)


def _chunk_tok(width):
    return pl.BlockSpec((NCHIP, TM, width), lambda i: (0, i, 0))


def _ffn_fwd(x1, g_ffn, wg, wu, wd, carry=None):
    L = x1.shape[0]

    def body(x_ref, g_ref, wg_hbm, wu_hbm, wd_hbm, x2_ref, h2_ref, gg_ref, uu_ref, wg_vm, wu_vm, wd_vm, w_sems):
        _resident_now([(src.at[c], dst.at[c]) for c in range(NCHIP)
                       for src, dst in ((wg_hbm, wg_vm), (wu_hbm, wu_vm), (wd_hbm, wd_vm))], w_sems)
        x = x_ref[...]
        xh, _ = _rms(x)
        h2 = (xh * g_ref[...]).astype(BF)
        h2_ref[...] = h2
        out = x
        for c in range(NCHIP):
            gg = lax.dot_general(h2, wg_vm[c], (((1,), (1,)), ((), ())), preferred_element_type=F32)
            uu = lax.dot_general(h2, wu_vm[c], (((1,), (1,)), ((), ())), preferred_element_type=F32)
            gg_ref[c] = gg.astype(BF)
            uu_ref[c] = uu.astype(BF)
            act = (gg * _sig(gg) * uu).astype(BF)
            out = out + jnp.dot(act, wd_vm[c], preferred_element_type=F32)
        x2_ref[...] = out

    return _pallas_call(
        body, carry, name="ffn_fwd", grid=(L // TM,),
        in_specs=[_tok(D), _full((1, D)), ANY, ANY, ANY],
        out_specs=[_tok(D), _tok(D), _chunk_tok(FC), _chunk_tok(FC)],
        out_shape=[_sds((L, D)), _sds((L, D), BF), _sds((NCHIP, L, FC), BF), _sds((NCHIP, L, FC), BF)],
        scratch_shapes=[pltpu.VMEM((NCHIP, FC, D), BF)] * 3 + [pltpu.SemaphoreType.DMA((3 * NCHIP,))],
        compiler_params=_params(52),
    )(x1, g_ffn, wg, wu, wd)


def _ffn_bwd(x1, dx2, gg, uu, g_ffn, wg, wu, wd, carry=None):
    L = x1.shape[0]

    def body(x_ref, dx2_ref, gg_ref, uu_ref, g_ref, wg_hbm, wu_hbm, wd_hbm,
             dx1_ref, act_ref, dgg_ref, duu_ref, dg_ref, wg_vm, wu_vm, wd_vm, w_sems):
        _resident_now([(src.at[c], dst.at[c]) for c in range(NCHIP)
                       for src, dst in ((wg_hbm, wg_vm), (wu_hbm, wu_vm), (wd_hbm, wd_vm))], w_sems)

        @pl.when(pl.program_id(0) == 0)
        def _():
            dg_ref[...] = jnp.zeros_like(dg_ref)

        dx2 = dx2_ref[...]
        dx2b = dx2.astype(BF)
        dh2 = jnp.zeros((TM, D), F32)
        for c in range(NCHIP):
            g = gg_ref[c].astype(F32)
            u = uu_ref[c].astype(F32)
            s = _sig(g)
            silu = g * s
            act_ref[c] = (silu * u).astype(BF)
            dact = lax.dot_general(dx2b, wd_vm[c], (((1,), (1,)), ((), ())), preferred_element_type=F32)
            dg = (dact * u * s * (1.0 + g * (1.0 - s))).astype(BF)
            du = (dact * silu).astype(BF)
            dgg_ref[c] = dg
            duu_ref[c] = du
            dh2 = dh2 + jnp.dot(dg, wg_vm[c], preferred_element_type=F32)
            dh2 = dh2 + jnp.dot(du, wu_vm[c], preferred_element_type=F32)
        xh, r = _rms(x_ref[...])
        dg_ref[0:1, :] += _colsum(dh2 * xh)
        dx1_ref[...] = dx2 + _rms_bwd(dh2, xh, r, g_ref[...])

    return _pallas_call(
        body, carry, name="ffn_bwd", grid=(L // TM,),
        in_specs=[_tok(D), _tok(D), _chunk_tok(FC), _chunk_tok(FC), _full((1, D)), ANY, ANY, ANY],
        out_specs=[_tok(D), _chunk_tok(FC), _chunk_tok(FC), _chunk_tok(FC), _full((SUB, D))],
        out_shape=[_sds((L, D)), _sds((NCHIP, L, FC), BF), _sds((NCHIP, L, FC), BF), _sds((NCHIP, L, FC), BF),
                   _sds((SUB, D))],
        scratch_shapes=[pltpu.VMEM((NCHIP, FC, D), BF)] * 3 + [pltpu.SemaphoreType.DMA((3 * NCHIP,))],
        compiler_params=_params(56),
    )(x1, dx2, gg, uu, g_ffn, wg, wu, wd)


def _ple_loss(x2, p, tgt, g_pg, w_pg, b_pg, w_ple, g_ple, g_final):
    L = x2.shape[0]

    def body(x2_ref, p_ref, t_ref, gpg_ref, wpg_ref, bpg_ref, wple_ref, gple_ref, gf_ref,
             dx2_ref, n2_ref, dpre_ref, de0_ref, acc_ref):
        @pl.when(pl.program_id(0) == 0)
        def _():
            acc_ref[...] = jnp.zeros_like(acc_ref)

        x2 = x2_ref[...]
        x2h, r2 = _rms(x2)
        n2 = (x2h * gpg_ref[...]).astype(BF)
        n2_ref[...] = n2
        gate = _sig(jnp.dot(n2, wpg_ref[...], preferred_element_type=F32) + bpg_ref[...])
        pb = p_ref[...].astype(BF)
        e0 = jnp.concatenate([jnp.dot(pb, wple_ref[k], preferred_element_type=F32) for k in range(NCHIP)], axis=1)
        e0h, re = _rms(e0)
        e = e0h * gple_ref[...]
        x3 = x2 + gate * e
        x3h, r3 = _rms(x3)
        diff = x3h * gf_ref[...] - t_ref[...]
        acc_ref[4:5, :] += _colsum(diff * diff) * (0.5 / D)
        dy = diff * (1.0 / D)
        acc_ref[3:4, :] += _colsum(dy * x3h)
        dx3 = _rms_bwd(dy, x3h, r3, gf_ref[...])
        de = dx3 * gate
        acc_ref[2:3, :] += _colsum(de * e0h)
        de0_ref[...] = _rms_bwd(de, e0h, re, gple_ref[...]).astype(BF)
        dpre = dx3 * e * gate * (1.0 - gate)
        acc_ref[1:2, :] += _colsum(dpre)
        dpreb = dpre.astype(BF)
        dpre_ref[...] = dpreb
        dn2 = lax.dot_general(dpreb, wpg_ref[...], (((1,), (1,)), ((), ())), preferred_element_type=F32)
        acc_ref[0:1, :] += _colsum(dn2 * x2h)
        dx2_ref[...] = dx3 + _rms_bwd(dn2, x2h, r2, gpg_ref[...])

    return _pallas_call(
        body, name="ple_loss", grid=(L // TM,),
        in_specs=[_tok(D), _tok(PLE), _tok(D), _full((1, D)), _full((D, D)), _full((1, D)), _full((NCHIP, PLE, AC)),
                  _full((1, D)), _full((1, D))],
        out_specs=[_tok(D), _tok(D), _tok(D), _tok(D), _full((SUB, D))],
        out_shape=[_sds((L, D)), _sds((L, D), BF), _sds((L, D), BF), _sds((L, D), BF), _sds((SUB, D))],
        compiler_params=_params(40),
    )(x2, p, tgt, g_pg, w_pg, b_pg, w_ple, g_ple, g_final)


def _tn(name, a, b, col_chunk=None, a_block=None, carry=None):
    L = a.shape[-2]
    m, n = a.shape[-1], b.shape[-1]
    a_col = 0
    if a_block is not None:
        a_col, m = a_block
    tk = L if (a.ndim == 3 or b.ndim == 3 or a_block is not None) else TK
    if a.ndim == 3 or b.ndim == 3:
        nj, bn = (a if a.ndim == 3 else b).shape[0], n
        a_spec = (pl.BlockSpec((None, tk, m), lambda j, t: (j, t, 0)) if a.ndim == 3
                  else pl.BlockSpec((tk, m), lambda j, t: (t, 0)))
        b_spec = (pl.BlockSpec((None, tk, n), lambda j, t: (j, t, 0)) if b.ndim == 3
                  else pl.BlockSpec((tk, n), lambda j, t: (t, 0)))
        out_spec, out_shape = pl.BlockSpec((None, m, n), lambda j, t: (j, 0, 0)), _sds((nj, m, n))
    else:
        bn = col_chunk
        if bn is None:
            bn = next((cand for cand in (1024, 512) if n > cand and n % cand == 0), n)
        nj = n // bn
        a_spec = pl.BlockSpec((tk, m), lambda j, t: (t, a_col))
        b_spec = pl.BlockSpec((tk, bn), lambda j, t: (t, j))
        if col_chunk is None:
            out_spec, out_shape = pl.BlockSpec((m, bn), lambda j, t: (0, j)), _sds((m, n))
        else:
            out_spec, out_shape = pl.BlockSpec((None, m, bn), lambda j, t: (j, 0, 0)), _sds((nj, m, bn))

    def body(a_ref, b_ref, o_ref):
        if tk == L:
            o_ref[...] = _mm_tn(a_ref[...], b_ref[...])
        else:
            @pl.when(pl.program_id(1) == 0)
            def _():
                o_ref[...] = jnp.zeros_like(o_ref)

            o_ref[...] += _mm_tn(a_ref[...], b_ref[...])

    outs = _pallas_call(
        body, carry, name=name, grid=(nj, L // tk), in_specs=[a_spec, b_spec], out_specs=[out_spec],
        out_shape=[pltpu.HBM(out_shape.shape, out_shape.dtype)],
        compiler_params=pltpu.CompilerParams(dimension_semantics=("arbitrary", "arbitrary"),
                                             vmem_limit_bytes=(30 if tk == L else 28) * VMEM_MB),
    )(*(_in_hbm([a, b]) if tk == L else (a, b)))
    return outs[0] if carry is None else outs


LANE = 128


def _tn_blocks(name, a, bs, ga, gb, carry=None):
    L, m, n, nb = a.shape[0], a.shape[1], bs[0].shape[1], len(bs)
    per = LANE // ga
    wb = per * gb
    n_super = m // LANE

    def body(a_ref, *refs):
        b_refs, o_refs, acc_refs = refs[:nb], refs[nb:2 * nb], refs[2 * nb:]
        t = pl.program_id(0)

        @pl.when(t == 0)
        def _():
            for acc in acc_refs:
                acc[...] = jnp.zeros_like(acc)

        lhs = a_ref[...].astype(BF)
        for b_ref, acc in zip(b_refs, acc_refs):
            rhs = b_ref[...].astype(BF)
            for j in range(n_super):
                acc[j] += _mm_tn(lhs[:, j * LANE:(j + 1) * LANE], rhs[:, j * wb:(j + 1) * wb])

        @pl.when(t == L // TK - 1)
        def _():
            own = (lax.broadcasted_iota(jnp.int32, (LANE, wb), 0) // ga) == (lax.broadcasted_iota(jnp.int32, (LANE, wb), 1) // gb)
            for o_ref, acc in zip(o_refs, acc_refs):
                for j in range(n_super):
                    kept = jnp.where(own, acc[j], 0.0)
                    o_ref[:, j * wb:(j + 1) * wb] = jnp.sum(kept.reshape(per, ga, wb), axis=0)

    outs = _pallas_call(
        body, carry, name=name, grid=(L // TK,),
        in_specs=[pl.BlockSpec((TK, m), lambda t: (t, 0))] + [pl.BlockSpec((TK, n), lambda t: (t, 0))] * nb,
        out_specs=[_full((ga, n))] * nb, out_shape=[_sds((ga, n))] * nb,
        scratch_shapes=[pltpu.VMEM((n_super, LANE, wb), F32)] * nb,
        compiler_params=_params(48),
    )(*_in_hbm([a] + list(bs)))
    return list(outs)


def _s5_discretize(lam_re, lam_im, log_dt, b_re, b_im):
    dt = jnp.exp(log_dt)[:, None]
    mag = jnp.exp(lam_re * dt)
    ar = mag * jnp.cos(lam_im * dt)
    ai = mag * jnp.sin(lam_im * dt)
    den = lam_re * lam_re + lam_im * lam_im
    nr = ar - 1.0
    fr = (nr * lam_re + ai * lam_im) / den
    fi = (ai * lam_re - nr * lam_im) / den
    bbr = fr[:, None, :] * b_re - fi[:, None, :] * b_im
    bbi = fr[:, None, :] * b_im + fi[:, None, :] * b_re
    return ar, ai, bbr, bbi


def _prepare(by_rows, block_cols, ar, ai):
    n = len(by_rows)

    def body(*refs):
        srcs, (ar_ref, ai_ref), dense, (con_ref, rev_ref) = refs[:n], refs[n:n + 2], refs[n + 2:2 * n + 2], refs[2 * n + 2:]
        for src, out, c in zip(srcs, dense, block_cols):
            r = src.shape[0]
            per = LANE // r
            wide = per * c
            own = (lax.broadcasted_iota(jnp.int32, (LANE, wide), 0) // r) == (lax.broadcasted_iota(jnp.int32, (LANE, wide), 1) // c)
            for j in range(out.shape[0]):
                tiled = jnp.broadcast_to(src[:, j * wide:(j + 1) * wide][None], (per, r, wide)).reshape(LANE, wide)
                out[j] = jnp.where(own, tiled, 0.0).astype(BF)
        a_r, a_i = ar_ref[...], ai_ref[...]
        pw = [(jnp.ones_like(a_r), jnp.zeros_like(a_i))]
        for _ in range(SUB):
            pr, pi = pw[-1]
            pw.append((pr * a_r - pi * a_i, pr * a_i + pi * a_r))
        row = _row_iota(GN)
        for ref, reverse in ((con_ref, False), (rev_ref, True)):
            sign = -1.0 if reverse else 1.0
            for j, sh in enumerate((1, 2, 4)):
                keep = (row < SUB - sh) if reverse else (row >= sh)
                ref[2 * j * SUB:(2 * j + 1) * SUB, :] = jnp.where(keep, pw[sh][0], 0.0)
                ref[(2 * j + 1) * SUB:(2 * j + 2) * SUB, :] = jnp.where(keep, sign * pw[sh][1], 0.0)
            p_r, p_i = jnp.zeros((SUB, GN), F32), jnp.zeros((SUB, GN), F32)
            for i in range(SUB):
                k = SUB - i if reverse else i + 1
                p_r = jnp.where(row == i, pw[k][0], p_r)
                p_i = jnp.where(row == i, sign * pw[k][1], p_i)
            ref[6 * SUB:7 * SUB, :] = p_r
            ref[7 * SUB:8 * SUB, :] = p_i

    dense_shapes = [(b.shape[1] // (LANE // b.shape[0] * c), LANE, LANE // b.shape[0] * c)
                    for b, c in zip(by_rows, block_cols)]
    outs = _pallas_call(
        body, name="prepare", grid=(1,), in_specs=[_full(b.shape) for b in by_rows] + [_full((1, GN))] * 2,
        out_specs=[_full(s) for s in dense_shapes] + [_full((8 * SUB, GN))] * 2,
        out_shape=[_sds(s, BF) for s in dense_shapes] + [_sds((8 * SUB, GN))] * 2,
        compiler_params=_params(48),
    )(*by_rows, ar, ai)
    return outs[:n], outs[n], outs[n + 1]


def _local_step(x, p, tgt, w, comm):
    rows_of = lambda a: a.reshape(NCHIP * a.shape[1], a.shape[2])
    quarters = lambda a: a.reshape(NCHIP, a.shape[0] // NCHIP, a.shape[1])

    def gathering(names, call):
        carry = comm.gather(names)
        outs = list(call(carry))
        own = len(outs) - len(carry.out_shapes)
        w.update(zip(names, outs[own:]))
        return outs[:own]

    w.update(comm.first())
    ar, ai, bbr, bbi = _s5_discretize(w["lam_re"], w["lam_im"], w["log_dt"], w["s5_b_re"], w["s5_b_im"])
    by_row = lambda b: jnp.transpose(b, (1, 0, 2)).reshape(b.shape[1], -1)
    (bbr_d, bbi_d, ccr_d, cci_d, wr_d, wi_d), con, con_rev = _prepare(
        [by_row(b) for b in (bbr, bbi, w["s5_c_re"], w["s5_c_im"], w["w_r"], w["w_i"])], [NS] * 4 + [HD] * 2,
        ar.reshape(1, GN), ai.reshape(1, GN))
    dsk = w["s5_d"].reshape(1, S5W)
    lam = w["lru_lambda"].reshape(1, LW)
    sp = jax.nn.softplus(-lam)
    b_r, b_i = w["b_r"].reshape(1, LW), w["b_i"].reshape(1, LW)
    row = lambda name: w[name].reshape(1, -1)

    h, ua, ub, gp = gathering(["w_glu", "w_a_out", "w_b_out"], lambda carry: _inproj_fwd(
        x, row("g_mix"), w["w_in"], row("b_in"), carry))
    w_glu = rows_of(w["w_glu"])
    sr, si, y, zg, ya = gathering(["w_o", "w_ffn_gate"], lambda carry: _s5_fwd(
        ua, bbr_d, bbi_d, ccr_d, cci_d, dsk, con, w_glu, row("b_glu"), carry))
    xc, rg, ig, yb, hp = gathering(["w_ffn_up"], lambda carry: _lru_fwd(
        ub, w["conv_w"], row("conv_b"), wr_d, wi_d, b_r, b_i, sp, carry))
    w_b_out, w_o = rows_of(w["w_b_out"]), rows_of(w["w_o"])
    x1, pa, pb, merged = gathering(["w_ffn_down"], lambda carry: _merge_fwd(
        x, ya, yb, gp, w["w_a_out"], w_b_out, w_o, carry))
    x2, h2, gg, uu = gathering(["w_ple_gate", "w_ple"], lambda carry: _ffn_fwd(
        x1, row("g_ffn"), w["w_ffn_gate"], w["w_ffn_up"], w["w_ffn_down"], carry))
    w_pg = rows_of(w["w_ple_gate"])
    dx2, n2, dpre, de0, acc_p = _ple_loss(x2, p, tgt, row("g_ple_gate"), w_pg, row("b_ple_gate"),
                                          w["w_ple"], row("g_ple"), row("g_final"))
    comm.reduce("ple", {"w_ple_gate": quarters(_tn("dw_ple_gate", n2, dpre)),
                        "w_ple": _tn("dw_ple", p, de0, col_chunk=AC)})
    dx1, act, dgg, duu, acc_f = comm.run(lambda carry: _ffn_bwd(
        x1, dx2, gg, uu, row("g_ffn"), w["w_ffn_gate"], w["w_ffn_up"], w["w_ffn_down"], carry))
    comm.reduce("ffn_gate", {"w_ffn_gate": _tn("dw_ffn_gate", dgg, h2)})
    comm.reduce("w_o", {"w_o": quarters(_tn("dw_o", merged, dx1))})
    comm.reduce("ffn_up", {"w_ffn_up": comm.run(lambda carry: _tn("dw_ffn_up", duu, h2, carry=carry))[0]})
    comm.reduce("ffn_down", {"w_ffn_down": comm.run(lambda carry: _tn("dw_ffn_down", act, dx2, carry=carry),
                                                    hold=("ffn_gate", "w_o"))[0]})
    dya, dyb, dgp, dpa, dpb = comm.run(lambda carry: _merge_bwd(
        dx1, gp, pa, pb, w["w_a_out"], w_b_out, w_o, carry), hold=("ffn_gate", "ffn_up"))
    comm.reduce("merge", {"w_a_out": _tn("dw_a_out", ya, dpa, col_chunk=AC), "w_b_out": quarters(_tn("dw_b_out", yb, dpb))})
    dua, dq, dy, lr, li, acc_a, acc_s = comm.run(lambda carry: _s5_bwd(
        dya, y, ua, sr, si, bbr_d, bbi_d, ccr_d, cci_d, dsk, con_rev, w_glu, row("b_glu"), carry), hold=("ffn_down",))
    dub, dpr, dpi, acc_l = comm.run(lambda carry: _lru_bwd(
        dyb, xc, rg, ig, hp, ub, w["conv_w"], wr_d, wi_d, sp, -_sig(-lam), carry))
    gx, dz, acc_g, acc_b = _inproj_bwd(x, dx1, dua, dub, dgp, row("g_mix"), w["w_in"])
    half = (D // 2,)
    comm.reduce("in_lo", {"w_in_lo": comm.run(lambda carry: _tn(
        "dw_in_lo", h, dz, col_chunk=QC, a_block=(0,) + half, carry=carry))[0]})
    comm.reduce("in_hi", {"w_in_hi": comm.run(lambda carry: _tn(
        "dw_in_hi", h, dz, col_chunk=QC, a_block=(1,) + half, carry=carry))[0], "w_glu": quarters(_tn("dw_glu", zg, dq))})
    d_wr, d_wi = comm.run(lambda carry: _tn_blocks("dw_r_i", xc, [dpr, dpi], HD, HD, carry))
    d_bbr, d_bbi = comm.run(lambda carry: _tn_blocks("d_bb", ua, [lr, li], NP, NS, carry))
    d_ccr, d_cci = comm.run(lambda carry: _tn_blocks("d_cc", dy, [sr, si], NP, NS, carry))
    comm.drain()
    sums = {"ple": acc_p, "ffn": acc_f, "mix": acc_g, "b_in": acc_b, "lru": acc_l, "s5": acc_s, "s5_a": acc_a}
    blocks = {"bb_re": d_bbr, "bb_im": d_bbi,
              "cc_re": d_ccr, "cc_im": d_cci,
              "w_r": d_wr, "w_i": d_wi}
    return gx, sums, blocks


def _replicated_grads(w, sums, blocks):
    grouped = lambda e, groups: jnp.transpose(e.reshape(e.shape[0], groups, -1), (1, 0, 2))
    d_ar, d_ai = sums["s5_a"][0].reshape(NG, NS), sums["s5_a"][1].reshape(NG, NS)
    d_bbr, d_bbi = grouped(blocks["bb_re"], NG), grouped(blocks["bb_im"], NG)
    _, vjp = jax.vjp(_s5_discretize, w["lam_re"], w["lam_im"], w["log_dt"], w["s5_b_re"], w["s5_b_im"])
    g = dict(zip(("lam_re", "lam_im", "log_dt", "s5_b_re", "s5_b_im"), vjp((d_ar, d_ai, d_bbr, d_bbi))))
    g["s5_c_re"] = grouped(blocks["cc_re"], NG)
    g["s5_c_im"] = -grouped(blocks["cc_im"], NG)
    g["w_r"], g["w_i"] = grouped(blocks["w_r"], NH), grouped(blocks["w_i"], NH)
    g["s5_d"] = sums["s5"][0].reshape(NG, NP)
    g["b_r"] = sums["lru"][1].reshape(NH, HD)
    g["b_i"] = sums["lru"][2].reshape(NH, HD)
    return g


ACC_ROWS = {"g_mix": ("mix", 0), "b_in": ("b_in", 0), "g_ffn": ("ffn", 0), "g_ple_gate": ("ple", 0),
            "b_ple_gate": ("ple", 1), "g_ple": ("ple", 2), "g_final": ("ple", 3), "b_glu": ("s5", 1),
            "lru_lambda": ("lru", 0), "conv_b": ("lru", 3)}
LOSS_ROW = ("ple", 4)
CONV_W_ROWS = ("lru", 4)


SHARDED = [("w_in", (D, QC)), ("w_glu", (S5W // NCHIP, S5W)), ("w_a_out", (S5W, AC)), ("w_b_out", (LW // NCHIP, D)),
           ("w_o", (D // NCHIP, D)), ("w_ffn_gate", (FC, D)), ("w_ffn_up", (FC, D)), ("w_ffn_down", (FC, D)),
           ("w_ple_gate", (D // NCHIP, D)), ("w_ple", (PLE, AC))]
TRANSPOSED = ("w_ffn_gate", "w_ffn_up", "s5_b_re", "s5_b_im")
CONV_SHARD = (4, LW // NCHIP)


def _mesh_pos():
    return lax.axis_index("x"), lax.axis_index("y"), lax.axis_index("c")


def _other_chips(x, y):
    return [(1 - x, y), (x, 1 - y), (1 - x, 1 - y)]


def _half_rows(c, rows, align):
    return pl.ds(pl.multiple_of(c * (rows // 2), align), rows // 2)


def _run_now(name, carry):
    c_in, c_out = len(carry.operands), len(carry.out_shapes)

    def body(*refs):
        ins, outs, sems = refs[:c_in], refs[c_in:c_in + c_out], refs[c_in + c_out:]
        carry.start(ins, outs, sems)
        carry.finish(ins, outs, sems)

    return pl.pallas_call(body, name=name, in_specs=[ANY] * c_in, out_specs=[ANY] * c_out,
                          out_shape=list(carry.out_shapes), scratch_shapes=list(carry.sems),
                          input_output_aliases=dict(carry.aliases))(*_in_hbm(carry.operands))


def _gather_group(shards, split):
    n = len(shards)

    def copies(srcs, outs, sems):
        send_sems, recv_sems = sems
        x, y, c = _mesh_pos()
        k0 = 2 * x + y
        sib = (x, y, 1 - c)
        chips = _other_chips(x, y)

        def remote(src, dst, j, i, to):
            return pltpu.make_async_remote_copy(src_ref=src, dst_ref=dst, send_sem=send_sems.at[j, i],
                                                recv_sem=recv_sems.at[j, i], device_id=to, device_id_type=MESH)

        def rows(ref, i, core, *lead):
            if not split[i]:
                return ref.at[lead] if lead else ref
            return ref.at[(*lead, _half_rows(core, shards[i].shape[0], 16))]

        own = [remote(s, o.at[k0], 6, i, sib) for i, (s, o) in enumerate(zip(srcs, outs))]
        ici, landed, fwd, fwd_landed = [], [], [], []
        for j, chip in enumerate(chips):
            kj = 2 * chip[0] + chip[1]
            pairs = list(enumerate(zip(srcs, outs)))
            ici.append([remote(rows(s, i, c), rows(o, i, c, k0), j, i, (*chip, c)) for i, (s, o) in pairs])
            landed.append([remote(rows(s, i, c), rows(o, i, c, kj), j, i, (*chip, c)) for i, (s, o) in pairs])
            fwd.append([remote(rows(o, i, c, kj), rows(o, i, c, kj), 3 + j, i, sib) for i, (s, o) in pairs if split[i]])
            fwd_landed.append([remote(rows(o, i, 1 - c, kj), rows(o, i, 1 - c, kj), 3 + j, i, sib)
                               for i, (s, o) in pairs if split[i]])
        return own, ici, landed, fwd, fwd_landed

    def start(srcs, outs, sems):
        own, ici, _, _, _ = copies(srcs, outs, sems)
        for cp in own + [cp for per_chip in ici for cp in per_chip]:
            cp.start()

    def finish(srcs, outs, sems):
        own, ici, landed, fwd, fwd_landed = copies(srcs, outs, sems)
        passed = [i for i in range(n) if split[i]]
        for j in range(3):
            for i, cp in enumerate(landed[j]):
                cp.wait_recv()
                if split[i]:
                    fwd[j][passed.index(i)].start()
        for j in range(3):
            for cp in fwd_landed[j]:
                cp.wait_recv()
        for cp in own:
            cp.wait_recv()
        for cp in own + [cp for per_chip in ici + fwd for cp in per_chip]:
            cp.wait_send()

    return _Carried(shards, [_sds((NCHIP,) + s.shape, s.dtype) for s in shards],
                    [pltpu.SemaphoreType.DMA((7, n)), pltpu.SemaphoreType.DMA((7, n))], start, finish)


def _each_copy(copies, carried, out_shapes, sems, aliases=None):
    def start(ins, outs, sem_refs):
        for cp in copies(ins, outs, sem_refs):
            cp.start()

    def finish(ins, outs, sem_refs):
        for cp in copies(ins, outs, sem_refs):
            cp.wait()

    return _Carried(carried, out_shapes, sems, start, finish, aliases)


def _swap_group(grads):
    n = len(grads)

    def copies(srcs, outs, sems):
        send_sems, recv_sems = sems
        x, y, c = _mesh_pos()
        return [pltpu.make_async_remote_copy(src_ref=s.at[:, _half_rows(1 - c, s.shape[1], 8)], dst_ref=o,
                                             send_sem=send_sems.at[i], recv_sem=recv_sems.at[i], device_id=(x, y, 1 - c),
                                             device_id_type=MESH) for i, (s, o) in enumerate(zip(srcs, outs))]

    return _each_copy(copies, grads, [pltpu.HBM((NCHIP, g.shape[1] // 2, g.shape[2]), F32) for g in grads],
                      [pltpu.SemaphoreType.DMA((n,)), pltpu.SemaphoreType.DMA((n,))])


def _add_sibling_group(tag, kc_idx, grads, gots):
    n = len(grads)

    def body(kc_ref, *refs):
        for g, rx, p, pb in zip(refs[:n], refs[n:2 * n], refs[2 * n:3 * n], refs[3 * n:]):
            s = g[...] + rx[...]
            pb[...] = s.astype(BF)

            @pl.when(pl.program_id(0) == kc_ref[0])
            def _():
                p[...] = s

    halves = [pl.BlockSpec((None,) + rx.shape[1:], lambda k, kc_ref: (k, 0, 0)) for rx in gots]
    mine = [pl.BlockSpec((None,) + rx.shape[1:], lambda k, kc_ref: (k, kc_ref[1], 0)) for rx in gots]
    own = [pl.BlockSpec(rx.shape[1:], lambda k, kc_ref: (0, 0)) for rx in gots]
    outs = _pallas_call(
        body, name="add_sibling_" + tag,
        grid_spec=pltpu.PrefetchScalarGridSpec(num_scalar_prefetch=1, grid=(NCHIP,), in_specs=mine + halves,
                                               out_specs=own + halves),
        out_shape=[pltpu.HBM(rx.shape[1:], F32) for rx in gots] + [pltpu.HBM(rx.shape, BF) for rx in gots],
        compiler_params=_params(48),
    )(kc_idx, *_in_hbm(list(grads) + list(gots)))
    return outs[:n], outs[n:]


def _exchange_group(parts):
    n = len(parts)

    def copies(srcs, outs, sems):
        send_sems, recv_sems = sems
        x, y, c = _mesh_pos()
        return [pltpu.make_async_remote_copy(
            src_ref=s.at[2 * chip[0] + chip[1]], dst_ref=o.at[j], send_sem=send_sems.at[j, i],
            recv_sem=recv_sems.at[j, i], device_id=(*chip, c), device_id_type=MESH)
            for j, chip in enumerate(_other_chips(x, y)) for i, (s, o) in enumerate(zip(srcs, outs))]

    return _each_copy(copies, parts, [pltpu.HBM((3,) + p.shape[1:], BF) for p in parts],
                      [pltpu.SemaphoreType.DMA((3, n)), pltpu.SemaphoreType.DMA((3, n))])


def _add_chips_group(tag, kc_idx, parts, arrived):
    n = len(parts)

    def body(kc_ref, *refs):
        for p, rx, t in zip(refs[:n], refs[n:2 * n], refs[2 * n:]):
            t[...] = ((p[...] + rx[0].astype(F32)) + rx[1].astype(F32)) + rx[2].astype(F32)

    outs = _pallas_call(
        body, name="add_chips_" + tag,
        grid_spec=pltpu.PrefetchScalarGridSpec(
            num_scalar_prefetch=1, grid=(1,),
            in_specs=([pl.BlockSpec(rx.shape[1:], lambda i, kc_ref: (0, 0)) for rx in arrived]
                      + [pl.BlockSpec(rx.shape, lambda i, kc_ref: (0, 0, 0)) for rx in arrived]),
            out_specs=[pl.BlockSpec((None,) + rx.shape[1:], lambda i, kc_ref: (kc_ref[1], 0, 0)) for rx in arrived]),
        out_shape=[pltpu.HBM((2,) + rx.shape[1:], F32) for rx in arrived],
        compiler_params=_params(48),
    )(kc_idx, *_in_hbm(list(parts) + list(arrived)))
    return list(outs)


def _join_group(halves):
    n = len(halves)

    def copies(bufs, sems):
        send_sems, recv_sems = sems
        x, y, c = _mesh_pos()
        sib = (x, y, 1 - c)
        sends = [pltpu.make_async_remote_copy(src_ref=b.at[c], dst_ref=b.at[c], send_sem=send_sems.at[i],
                                              recv_sem=recv_sems.at[i], device_id=sib, device_id_type=MESH)
                 for i, b in enumerate(bufs)]
        landed = [pltpu.make_async_remote_copy(src_ref=b.at[c], dst_ref=b.at[1 - c], send_sem=send_sems.at[i],
                                               recv_sem=recv_sems.at[i], device_id=sib, device_id_type=MESH)
                  for i, b in enumerate(bufs)]
        return sends, landed

    def start(_, bufs, sems):
        for cp in copies(bufs, sems)[0]:
            cp.start()

    def finish(_, bufs, sems):
        sends, landed = copies(bufs, sems)
        for cp in landed:
            cp.wait_recv()
        for cp in sends:
            cp.wait_send()

    return _Carried(halves, [pltpu.HBM(h.shape, F32) for h in halves],
                    [pltpu.SemaphoreType.DMA((n,)), pltpu.SemaphoreType.DMA((n,))], start, finish,
                    {i: i for i in range(n)})


def _combine(carries):
    operands, out_shapes, sems, aliases, spans = [], [], [], {}, []
    for c in carries:
        aliases.update({len(operands) + i: len(out_shapes) + o for i, o in c.aliases.items()})
        spans.append((len(operands), len(out_shapes), len(sems)))
        operands += list(c.operands)
        out_shapes += list(c.out_shapes)
        sems += list(c.sems)

    def each(phase):
        def run(ins, outs, sem_refs):
            for c, (a, b, s) in zip(carries, spans):
                getattr(c, phase)(ins[a:a + len(c.operands)], outs[b:b + len(c.out_shapes)], sem_refs[s:s + len(c.sems)])
        return run

    return _Carried(operands, out_shapes, sems, each("start"), each("finish"), aliases)


def _allreduce_small(arrays, wire):
    n = len(arrays)
    halves = [(a.shape[0], a.shape[1] // 2) for a in arrays]

    def body(*refs):
        srcs, outs = refs[:n], refs[n:2 * n]
        mine_bufs, sib_bufs, chip_bufs, total_bufs = (refs[k * n:(k + 1) * n] for k in range(2, 6))
        send_sems, recv_sems, local_sems = refs[6 * n:]
        x, y, c = _mesh_pos()
        k0 = 2 * x + y
        sib = (x, y, 1 - c)

        def remote(src, dst, j, i, to):
            return pltpu.make_async_remote_copy(src_ref=src, dst_ref=dst, send_sem=send_sems.at[j, i],
                                                recv_sem=recv_sems.at[j, i], device_id=to, device_id_type=MESH)

        def cols(ref, i, core):
            return ref.at[:, pl.ds(pl.multiple_of(core * halves[i][1], LANE), halves[i][1])]

        swaps = [remote(cols(s, i, 1 - c), b, 0, i, sib) for i, (s, b) in enumerate(zip(srcs, sib_bufs))]
        own = [pltpu.make_async_copy(cols(s, i, c), m, local_sems.at[i]) for i, (s, m) in enumerate(zip(srcs, mine_bufs))]
        for cp in swaps + own:
            cp.start()
        for cp in swaps + own:
            cp.wait()
        for m, b, buf in zip(mine_bufs, sib_bufs, chip_bufs):
            buf[k0] = (m[...] + b[...]).astype(buf.dtype)
        chips = _other_chips(x, y)
        sends = [remote(buf.at[k0], buf.at[k0], 1 + j, i, (*chip, c))
                 for j, chip in enumerate(chips) for i, buf in enumerate(chip_bufs)]
        for cp in sends:
            cp.start()
        for j, chip in enumerate(chips):
            for i, buf in enumerate(chip_bufs):
                remote(buf.at[k0], buf.at[2 * chip[0] + chip[1]], 1 + j, i, (*chip, c)).wait_recv()
        for cp in sends:
            cp.wait_send()
        for t, buf in zip(total_bufs, chip_bufs):
            t[...] = ((buf[0].astype(F32) + buf[1].astype(F32)) + buf[2].astype(F32)) + buf[3].astype(F32)
        joins = [remote(t, cols(o, i, c), 4, i, sib) for i, (t, o) in enumerate(zip(total_bufs, outs))]
        keep = [pltpu.make_async_copy(t, cols(o, i, c), local_sems.at[i]) for i, (t, o) in enumerate(zip(total_bufs, outs))]
        for cp in joins + keep:
            cp.start()
        for i, (t, o) in enumerate(zip(total_bufs, outs)):
            remote(t, cols(o, i, 1 - c), 4, i, sib).wait_recv()
        for cp in joins:
            cp.wait_send()
        for cp in keep:
            cp.wait()

    specs = [_full(a.shape) for a in arrays]
    return _pallas_call(
        body, name="allreduce_small", grid=(1,), in_specs=specs, out_specs=specs,
        out_shape=[_sds(a.shape) for a in arrays],
        scratch_shapes=([pltpu.VMEM(h, F32) for h in halves] + [pltpu.VMEM(h, F32) for h in halves]
                        + [pltpu.VMEM((NCHIP,) + h, dt) for h, dt in zip(halves, wire)] + [pltpu.VMEM(h, F32) for h in halves]
                        + [pltpu.SemaphoreType.DMA((5, n)), pltpu.SemaphoreType.DMA((5, n)), pltpu.SemaphoreType.DMA((n,))]),
        compiler_params=_params(32),
    )(*arrays)


def _adamw_terms(w, g, m, v):
    m = ADAM_B1 * m + (1.0 - ADAM_B1) * g
    v = ADAM_B2 * v + (1.0 - ADAM_B2) * jnp.square(g)
    m_hat = m / (1.0 - ADAM_B1 ** ADAM_STEP)
    v_hat = v / (1.0 - ADAM_B2 ** ADAM_STEP)
    return -ADAM_LR * (m_hat / (jnp.sqrt(v_hat) + ADAM_EPS) + ADAM_WD * w), m, v


ADAM_STEPS = 4


def _adamw_group(tag, ws, gs, ms, vs):
    n = len(ws)

    def body(*refs):
        ins, outs = refs[:4 * n], refs[4 * n:]
        for i in range(n):
            w, g, m, v = (ins[k * n + i][...] for k in range(4))
            outs[i][...] = g
            outs[n + i][...], outs[2 * n + i][...], outs[3 * n + i][...] = _adamw_terms(w, g, m, v)

    specs = [pl.BlockSpec((w.shape[0] // ADAM_STEPS, w.shape[1]), lambda i: (i, 0)) for w in ws]
    outs = _pallas_call(
        body, name="adamw_" + tag, grid=(ADAM_STEPS,), in_specs=specs * 4, out_specs=specs * 4,
        out_shape=[_sds(w.shape) for w in ws] * 4, compiler_params=_params(48),
    )(*_in_hbm(list(ws) + list(gs) + list(ms) + list(vs)))
    return outs[:n], outs[n:2 * n], outs[2 * n:3 * n], outs[3 * n:]


def _adamw_half_group(tag, kc_idx, parts, arrived, ws, ms, vs):
    n = len(parts)

    def body(kc_ref, *refs):
        ins, outs = refs[:5 * n], refs[5 * n:]
        for i in range(n):
            p, rx, w, m, v = (ins[k * n + i] for k in range(5))
            g = ((p[...] + rx[0].astype(F32)) + rx[1].astype(F32)) + rx[2].astype(F32)
            outs[i][...] = g
            outs[n + i][...], outs[2 * n + i][...], outs[3 * n + i][...] = _adamw_terms(w[...], g, m[...], v[...])

    steps = ADAM_STEPS // 2
    blocks = [(p.shape[0] // steps, p.shape[1]) for p in parts]
    mine = [pl.BlockSpec(b, lambda i, kc_ref: (i, 0)) for b in blocks]
    landed = [pl.BlockSpec((3,) + b, lambda i, kc_ref: (0, i, 0)) for b in blocks]
    state = [pl.BlockSpec(b, lambda i, kc_ref: (kc_ref[1] * steps + i, 0)) for b in blocks]
    halves = [pl.BlockSpec((None,) + b, lambda i, kc_ref: (kc_ref[1], i, 0)) for b in blocks]
    outs = _pallas_call(
        body, name="adamw_" + tag,
        grid_spec=pltpu.PrefetchScalarGridSpec(num_scalar_prefetch=1, grid=(steps,),
                                               in_specs=mine + landed + state * 3, out_specs=halves * 4),
        out_shape=[pltpu.HBM((2,) + p.shape, F32) for p in parts] * 4,
        compiler_params=_params(48),
    )(kc_idx, *_in_hbm(list(parts) + list(arrived) + list(ws) + list(ms) + list(vs)))
    return list(outs)


def _adamw_replicated(sums, row_of, direct):
    ns, nr, nd = len(sums), len(row_of), len(direct)

    def body(*refs):
        sum_refs = refs[:ns]
        ins = refs[ns:ns + 3 * nr + 4 * nd]
        outs = refs[ns + 3 * nr + 4 * nd:]
        for i, (_, _, _, si, row) in enumerate(row_of):
            w_ref, m_ref, v_ref = ins[3 * i:3 * i + 3]
            g = sum_refs[si][row:row + 1, :]
            outs[4 * i][...] = g
            outs[4 * i + 1][...], outs[4 * i + 2][...], outs[4 * i + 3][...] = _adamw_terms(w_ref[...], g, m_ref[...], v_ref[...])
        for i in range(nd):
            w_ref, m_ref, v_ref, g_ref = ins[3 * nr + 4 * i:3 * nr + 4 * i + 4]
            o = outs[4 * (nr + i):4 * (nr + i) + 4]
            g = g_ref[...]
            o[0][...] = g
            o[1][...], o[2][...], o[3][...] = _adamw_terms(w_ref[...], g, m_ref[...], v_ref[...])

    operands = list(sums)
    shapes = []
    for w, m, v, _, _ in row_of:
        operands += [w, m, v]
        shapes += [w.shape] * 4
    for w, m, v, g in direct:
        operands += [w, m, v, g]
        shapes += [w.shape] * 4
    flat = _pallas_call(
        body, name="adamw_replicated", grid=(1,), in_specs=[_full(a.shape) for a in operands],
        out_specs=[_full(s) for s in shapes], out_shape=[_sds(s) for s in shapes],
        compiler_params=_params(56),
    )(*operands)
    return [flat[4 * i:4 * i + 4] for i in range(nr + nd)]


class _Exchanges:
    def __init__(self, shards, conv_w, chip, core, state, record, apply):
        self.shards, self.conv_w, self.state, self.record, self.apply = shards, conv_w, state, record, apply
        self.active, self.calls = [], 0
        self.chip_core_idx = jnp.stack([chip, core]).astype(jnp.int32)

    def first(self):
        w_in, conv_w = _run_now("gather_first", _gather_group([self.shards["w_in"], self.conv_w], [True, False]))
        return {"w_in": w_in, "conv_w": jnp.transpose(conv_w, (1, 0, 2)).reshape(4, LW)}

    def gather(self, names):
        return _gather_group([self.shards[n] for n in names], [True] * len(names))

    def reduce(self, tag, grads):
        self.active.append({"tag": tag, "names": list(grads), "stage": 0, "grads": list(grads.values()),
                            "whole": all(n in self.shards for n in grads)})

    def run(self, call, hold=()):
        groups = [g for g in self.active if g["tag"] not in hold]
        carries = [self._exchange_of(g) for g in groups]
        carry = _combine(carries)
        outs = list(call(carry))
        own = len(outs) - len(carry.out_shapes)
        landed = outs[own:]
        for g, c in zip(groups, carries):
            self._sum_after(g, landed[:len(c.out_shapes)])
            landed = landed[len(c.out_shapes):]
        self.active = [g for g in self.active if g["stage"] < 3]
        return outs[:own]

    def _exchange_of(self, g):
        if g["stage"] == 0:
            return _swap_group(g["grads"])
        if g["stage"] == 1:
            return _exchange_group(g["bf16"])
        return _join_group(g["halves"])

    def _sum_after(self, g, landed):
        if g["stage"] == 0:
            g["f32"], g["bf16"] = _add_sibling_group(g["tag"], self.chip_core_idx, g["grads"], landed)
        elif g["stage"] == 1:
            if g["whole"]:
                ws, ms, vs = zip(*[self.state(n) for n in g["names"]])
                g["halves"] = _adamw_half_group(g["tag"], self.chip_core_idx, g["f32"], landed, ws, ms, vs)
            else:
                g["halves"] = _add_chips_group(g["tag"], self.chip_core_idx, g["f32"], landed)
        else:
            joined = [t.reshape(2 * t.shape[1], t.shape[2]) for t in landed]
            if g["whole"]:
                n = len(g["names"])
                self.record(g["names"], [joined[k * n:(k + 1) * n] for k in range(4)])
            else:
                self.apply(g["tag"], g["names"], joined)
        g["stage"] += 1

    def drain(self):
        while self.active:
            self.calls += 1
            self.run(lambda carry: _run_now("reduce_%d" % self.calls, carry))


INPUT_NAMES = (["x", "p"] + [n for n in
               ["g_mix", "w_in", "b_in", "lam_re", "lam_im", "log_dt", "s5_b_re", "s5_b_im", "s5_c_re", "s5_c_im", "s5_d",
                "w_glu", "b_glu", "conv_w", "conv_b", "w_r", "b_r", "w_i", "b_i", "lru_lambda", "w_a_out", "w_b_out", "w_o",
                "g_ffn", "w_ffn_gate", "w_ffn_up", "w_ffn_down", "g_ple_gate", "w_ple_gate", "b_ple_gate", "w_ple", "g_ple",
                "g_final"]])
WEIGHT_NAMES = INPUT_NAMES[2:]


def kernel(*args):
    names = INPUT_NAMES + ["loss_target"] + ["m_" + n for n in WEIGHT_NAMES] + ["v_" + n for n in WEIGHT_NAMES]
    assert len(args) == len(names)
    given = dict(zip(names, args))

    def view(name):
        a = given[name]
        return jnp.swapaxes(a, -1, -2) if name.endswith(TRANSPOSED) else a

    def unview(name, a):
        return jnp.swapaxes(a, -1, -2) if name in TRANSPOSED else a

    def local(name):
        return view(name) if name.endswith("g_final") else view(name)[0]

    xi, yi, ci = _mesh_pos()
    k0 = 2 * xi + yi
    x, p, tgt = given["x"][0], given["p"][0, 0], given["loss_target"][0]

    results = {}

    row_halves = {}

    def apply(tag, names, totals):
        totals = dict(zip(names, totals))
        row_halves.update({n: totals.pop(n) for n in names if n in ("w_in_lo", "w_in_hi")})
        if len(row_halves) == 2:
            totals["w_in"] = jnp.concatenate([row_halves.pop("w_in_lo"), row_halves.pop("w_in_hi")])
        names = list(totals)
        if not names:
            return
        ws, ms, vs = zip(*[state(n) for n in names])
        record(names, _adamw_group(tag, ws, list(totals.values()), ms, vs))

    def state(n):
        return local(n), local("m_" + n), local("v_" + n)

    def record(names, new):
        for kind, arrays in zip(("grad", "delta", "new_m", "new_v"), new):
            for n, arr in zip(names, arrays):
                results[kind, n] = unview(n, arr[None])

    comm = _Exchanges({n: local(n).astype(BF) for n, _ in SHARDED}, local("conv_w"), k0, ci, state, record, apply)
    w = {n: local(n) for n in WEIGHT_NAMES if n != "conv_w" and n not in dict(SHARDED)}
    gx, sums, blocks = _local_step(x, p, tgt, w, comm)

    sum_names, block_names = list(sums), list(blocks)
    red = _allreduce_small([sums[n] for n in sum_names] + [blocks[n] for n in block_names],
                           [F32] * len(sum_names) + [BF] * len(block_names))
    sums = dict(zip(sum_names, red[:len(sum_names)]))
    blocks = dict(zip(block_names, red[len(sum_names):]))
    loss = jnp.sum(sums[LOSS_ROW[0]][LOSS_ROW[1]])
    direct_g = _replicated_grads(w, sums, blocks)
    conv_rows = sums[CONV_W_ROWS[0]][CONV_W_ROWS[1]:CONV_W_ROWS[1] + 4]
    direct_g["conv_w"] = lax.dynamic_slice(conv_rows, (0, k0 * CONV_SHARD[1]), CONV_SHARD)
    as_row = lambda a: a.reshape(1, -1)
    row_names = list(ACC_ROWS)
    row_of = [(as_row(given[n]), as_row(given["m_" + n]), as_row(given["v_" + n]),
               sum_names.index(ACC_ROWS[n][0]), ACC_ROWS[n][1]) for n in row_names]
    direct_names = list(direct_g)
    direct = [(view(n), view("m_" + n), view("v_" + n), direct_g[n].reshape(view(n).shape)) for n in direct_names]
    done = _adamw_replicated([sums[n] for n in sum_names], row_of, direct)
    for n, four in zip(row_names + direct_names, done):
        for kind, arr in zip(("grad", "delta", "new_m", "new_v"), four):
            results[kind, n] = unview(n, arr).reshape(given[n].shape)

    out = [loss, gx[None]]
    for kind in ("grad", "delta", "new_m", "new_v"):
        out += [results[kind, n] for n in WEIGHT_NAMES]
    return tuple(out)
```

```python
import functools
import math

import jax
import jax.numpy as jnp
from jax import lax
from jax.experimental import pallas as pl
from jax.experimental.pallas import tpu as pltpu

F32 = jnp.float32
BF = jnp.bfloat16

D = 1024
S5W = 512
NG, NS, NP = 32, 64, 16
GN = NG * NS
LW = 1024
NH, HD = 16, 64
LRU_C = 8.0
FH = 2816
NCHIP = 4
FC = FH // NCHIP
PLE = 256
INC = S5W + LW + 2 * D
EPS = 1e-6
ADAM_LR, ADAM_B1, ADAM_B2, ADAM_EPS, ADAM_WD, ADAM_STEP = 0.001, 0.9, 0.999, 1e-08, 0.01, 10

TM = 256
TK = 1024
LC = 512
SUB = 8
VMEM_MB = 1024 * 1024
MESH = pl.DeviceIdType.MESH
ANY = pl.BlockSpec(memory_space=pl.ANY)


def _mm(a, b):
    return jnp.dot(a.astype(BF), b.astype(BF), preferred_element_type=F32)


def _mm_nt(a, b):
    return lax.dot_general(a.astype(BF), b.astype(BF), (((1,), (1,)), ((), ())), preferred_element_type=F32)


def _mm_tn(a, b):
    return lax.dot_general(a.astype(BF), b.astype(BF), (((0,), (0,)), ((), ())), preferred_element_type=F32)


def _blockdiag_mm(x, blocks_ref):
    n, rows, _ = blocks_ref.shape
    return jnp.concatenate([jnp.dot(x[:, j * rows:(j + 1) * rows], blocks_ref[j], preferred_element_type=F32)
                            for j in range(n)], axis=1)


def _blockdiag_mm_t(x, blocks_ref):
    n, _, wide = blocks_ref.shape
    return jnp.concatenate([lax.dot_general(x[:, j * wide:(j + 1) * wide], blocks_ref[j], (((1,), (1,)), ((), ())),
                                            preferred_element_type=F32) for j in range(n)], axis=1)


def _rms(x):
    r = lax.rsqrt(jnp.mean(x * x, axis=-1, keepdims=True) + EPS)
    return x * r, r


def _rms_bwd(dy, xh, r, g):
    dxh = dy * g
    return r * (dxh - xh * jnp.mean(dxh * xh, axis=-1, keepdims=True))


def _colsum(x):
    return jnp.sum(x, axis=0, keepdims=True)


def _sig(x):
    return jax.nn.sigmoid(x)


def _gelu_grad(x):
    c = math.sqrt(2.0 / math.pi)
    t = jnp.tanh(c * (x + 0.044715 * x * x * x))
    return 0.5 * (1.0 + t) + 0.5 * x * (1.0 - t * t) * c * (1.0 + 3.0 * 0.044715 * x * x)


def _neg_expm1(x):
    series = -x * (1.0 + x * (0.5 + x * (1.0 / 6.0 + x * (1.0 / 24.0))))
    return jnp.where(x > -0.03, series, 1.0 - jnp.exp(x))


def _tok(width):
    return pl.BlockSpec((TM, width), lambda i: (i, 0))


def _tok_rev(width, nt):
    return pl.BlockSpec((TM, width), lambda i: (nt - 1 - i, 0))


def _full(shape):
    return pl.BlockSpec(shape, lambda i: (0,) * len(shape))


def _params(vmem_mb, **kw):
    return pltpu.CompilerParams(dimension_semantics=("arbitrary",), vmem_limit_bytes=vmem_mb * VMEM_MB, **kw)


def _sds(shape, dtype=F32):
    return jax.ShapeDtypeStruct(shape, dtype)


def _far(shape, dtype=F32):
    return pltpu.HBM(shape, dtype)


class _Carried:
    def __init__(self, operands, out_shapes, sems, start, finish, aliases=None):
        self.operands, self.out_shapes, self.sems = list(operands), list(out_shapes), list(sems)
        self.start, self.finish, self.aliases = start, finish, dict(aliases or {})


def _in_hbm(arrays):
    return [pltpu.with_memory_space_constraint(a, pltpu.HBM) for a in arrays]


def _pallas_call(body, carry=None, **kw):
    if carry is None:
        return pl.pallas_call(body, **kw)

    def at_step(corner):
        hit = [pl.program_id(d) == (size - 1 if corner else 0) for d, size in enumerate(kw["grid"])]
        return functools.reduce(jnp.logical_and, hit)

    name, grid, compiler_params = kw["name"], kw["grid"], kw["compiler_params"]
    in_specs, out_specs, out_shape = list(kw["in_specs"]), list(kw["out_specs"]), list(kw["out_shape"])
    scratch_shapes = list(kw.get("scratch_shapes", ()))
    n_in, n_out, n_scr = len(in_specs), len(out_specs), len(scratch_shapes)
    c_in, c_out = len(carry.operands), len(carry.out_shapes)

    def full_body(*refs):
        ins, refs = refs[:n_in], refs[n_in:]
        c_ins, refs = refs[:c_in], refs[c_in:]
        outs, refs = refs[:n_out], refs[n_out:]
        c_outs, refs = refs[:c_out], refs[c_out:]
        scratch, c_sems = refs[:n_scr], refs[n_scr:]

        @pl.when(at_step(0))
        def _():
            carry.start(c_ins, c_outs, c_sems)

        body(*ins, *outs, *scratch)

        @pl.when(at_step(1))
        def _():
            carry.finish(c_ins, c_outs, c_sems)

    call = pl.pallas_call(
        full_body, name=name, grid=grid, in_specs=in_specs + [ANY] * c_in, out_specs=out_specs + [ANY] * c_out,
        out_shape=out_shape + list(carry.out_shapes), scratch_shapes=scratch_shapes + list(carry.sems),
        input_output_aliases={n_in + i: n_out + o for i, o in carry.aliases.items()},
        compiler_params=compiler_params)
    return lambda *operands: call(*operands, *_in_hbm(carry.operands))


def _resident(pairs, sems):
    first = pl.program_id(0) == 0
    copies = [pltpu.make_async_copy(src, dst, sems.at[j]) for j, (src, dst) in enumerate(pairs)]

    @pl.when(first)
    def _():
        for cp in copies:
            cp.start()

    def wait(j):
        @pl.when(first)
        def _():
            copies[j].wait()

    return wait


def _resident_now(pairs, sems):
    @pl.when(pl.program_id(0) == 0)
    def _():
        copies = [pltpu.make_async_copy(src, dst, sems.at[j]) for j, (src, dst) in enumerate(pairs)]
        for cp in copies:
            cp.start()
        for cp in copies:
            cp.wait()


def _row_iota(width):
    return lax.broadcasted_iota(jnp.int32, (SUB, width), 0)


def _bcast_row(x, row):
    return jnp.broadcast_to(x[row:row + 1, :], x.shape)


def _slab(k):
    return pl.ds(pl.multiple_of(k * SUB, SUB), SUB)


QC = INC // NCHIP
Z_PARTS = ((0, S5W), (S5W, S5W + LW), (S5W + LW, INC))


def _inproj_fwd(x, g_mix, w_in, b_in, carry=None):
    L = x.shape[0]

    def body(x_ref, g_ref, w_hbm, b_ref, h_ref, ua_ref, ub_ref, gp_ref, w_vm, w_sems):
        _resident_now([(w_hbm.at[k], w_vm.at[k]) for k in range(NCHIP)], w_sems)
        xh, _ = _rms(x_ref[...])
        h = (xh * g_ref[...]).astype(BF)
        h_ref[...] = h
        for k in range(NCHIP):
            lo, hi = k * QC, (k + 1) * QC
            z = jnp.dot(h, w_vm[k], preferred_element_type=F32) + b_ref[:, lo:hi]
            for ref, (a, b) in zip((ua_ref, ub_ref, gp_ref), Z_PARTS):
                s, e = max(lo, a), min(hi, b)
                if s < e:
                    ref[:, s - a:e - a] = z[:, s - lo:e - lo]

    return _pallas_call(
        body, carry, name="inproj_fwd", grid=(L // TM,),
        in_specs=[_tok(D), _full((1, D)), ANY, _full((1, INC))],
        out_specs=[_tok(D), _tok(S5W), _tok(LW), _tok(2 * D)],
        out_shape=[_far((L, D), BF), _far((L, S5W)), _sds((L, LW)), _sds((L, 2 * D))],
        scratch_shapes=[pltpu.VMEM((NCHIP, D, QC), BF), pltpu.SemaphoreType.DMA((NCHIP,))],
        compiler_params=_params(40),
    )(x, g_mix, w_in, b_in)


def _inproj_bwd(x, dx1, dua, dub, dgp, g_mix, w_in, carry=None):
    L = x.shape[0]

    def body(x_ref, dx1_ref, dua_ref, dub_ref, dgp_ref, g_ref, w_hbm, gx_ref, dz_ref, dg_ref, db_ref, w_vm, w_sems):
        _resident_now([(w_hbm.at[k], w_vm.at[k]) for k in range(NCHIP)], w_sems)

        @pl.when(pl.program_id(0) == 0)
        def _():
            dg_ref[...] = jnp.zeros_like(dg_ref)
            db_ref[...] = jnp.zeros_like(db_ref)

        for src, (a, b) in zip((dua_ref, dub_ref, dgp_ref), Z_PARTS):
            d = src[...]
            dz_ref[:, a:b] = d.astype(BF)
            db_ref[0:1, a:b] += _colsum(d)
        dh = jnp.zeros((TM, D), F32)
        for k in range(NCHIP):
            dh = dh + lax.dot_general(dz_ref[:, k * QC:(k + 1) * QC], w_vm[k], (((1,), (1,)), ((), ())),
                                      preferred_element_type=F32)
        xh, r = _rms(x_ref[...])
        dg_ref[0:1, :] += _colsum(dh * xh)
        gx_ref[...] = dx1_ref[...] + _rms_bwd(dh, xh, r, g_ref[...])

    return _pallas_call(
        body, carry, name="inproj_bwd", grid=(L // TM,),
        in_specs=[_tok(D), _tok(D), _tok(S5W), _tok(LW), _tok(2 * D), _full((1, D)), ANY],
        out_specs=[_tok(D), _tok(INC), _full((SUB, D)), _full((SUB, INC))],
        out_shape=[_sds((L, D)), _sds((L, INC), BF), _sds((SUB, D)), _sds((SUB, INC))],
        scratch_shapes=[pltpu.VMEM((NCHIP, D, QC), BF), pltpu.SemaphoreType.DMA((NCHIP,))],
        compiler_params=_params(40),
    )(x, dx1, dua, dub, dgp, g_mix, w_in)


def _cscan(xr_ref, xi_ref, con_ref, cr_ref, ci_ref, reverse):
    n_slab = xr_ref.shape[0] // SUB
    width = xr_ref.shape[1]
    for lc in range(width // LC):
        cols = slice(lc * LC, (lc + 1) * LC)
        con = [con_ref[SUB * j:SUB * (j + 1), cols] for j in range(8)]

        def step(k, carry, cols=cols, con=con):
            cr, ci = carry
            rows = _slab(n_slab - 1 - k if reverse else k)
            xr, xi = xr_ref[rows, cols], xi_ref[rows, cols]
            for j, sh in enumerate((1, 2, 4)):
                mr, mi = con[2 * j], con[2 * j + 1]
                pr = pltpu.roll(xr, SUB - sh if reverse else sh, 0)
                pi = pltpu.roll(xi, SUB - sh if reverse else sh, 0)
                xr, xi = xr + mr * pr - mi * pi, xi + mr * pi + mi * pr
            xr, xi = xr + con[6] * cr - con[7] * ci, xi + con[6] * ci + con[7] * cr
            xr_ref[rows, cols] = xr
            xi_ref[rows, cols] = xi
            row = 0 if reverse else SUB - 1
            return _bcast_row(xr, row), _bcast_row(xi, row)

        cr, ci = lax.fori_loop(0, n_slab, step, (cr_ref[:, cols], ci_ref[:, cols]))
        cr_ref[:, cols] = cr
        ci_ref[:, cols] = ci


def _s5_fwd(ua, bbr, bbi, ccr, cci, dsk, con, w_glu, b_glu, carry=None):
    L = ua.shape[0]

    def body(ua_ref, bbr_hbm, bbi_hbm, ccr_hbm, cci_hbm, dsk_ref, con_ref, wg_ref, bg_ref,
             sr_ref, si_ref, y_ref, zg_ref, ya_ref, bbr_vm, bbi_vm, ccr_vm, cci_vm, cr_ref, ci_ref, w_sems):
        landed = _resident([(bbr_hbm, bbr_vm), (bbi_hbm, bbi_vm), (ccr_hbm, ccr_vm), (cci_hbm, cci_vm)], w_sems)

        @pl.when(pl.program_id(0) == 0)
        def _():
            cr_ref[...] = jnp.zeros_like(cr_ref)
            ci_ref[...] = jnp.zeros_like(ci_ref)

        u = ua_ref[...]
        ub = u.astype(BF)
        landed(0)
        sr_ref[...] = _blockdiag_mm(ub, bbr_vm)
        landed(1)
        si_ref[...] = _blockdiag_mm(ub, bbi_vm)
        _cscan(sr_ref, si_ref, con_ref, cr_ref, ci_ref, reverse=False)
        landed(2)
        landed(3)
        y = (_blockdiag_mm_t(sr_ref[...].astype(BF), ccr_vm) - _blockdiag_mm_t(si_ref[...].astype(BF), cci_vm)
             + dsk_ref[...] * u)
        y_ref[...] = y
        zg = jax.nn.gelu(y)
        zg_ref[...] = zg.astype(BF)
        q = _mm(zg, wg_ref[...]) + bg_ref[...]
        ya_ref[...] = (zg * _sig(q)).astype(BF)

    return _pallas_call(
        body, carry, name="s5_fwd", grid=(L // TM,),
        in_specs=[_tok(S5W), ANY, ANY, ANY, ANY, _full((1, S5W)), _full((8 * SUB, GN)),
                  _full((S5W, S5W)), _full((1, S5W))],
        out_specs=[_tok(GN), _tok(GN), _tok(S5W), _tok(S5W), _tok(S5W)],
        out_shape=[_sds((L, GN)), _sds((L, GN)), _sds((L, S5W)), _far((L, S5W), BF), _far((L, S5W), BF)],
        scratch_shapes=[pltpu.VMEM((S5W // 128, 128, GN // (S5W // 128)), BF)] * 4 + [
                        pltpu.VMEM((SUB, GN), F32), pltpu.VMEM((SUB, GN), F32),
                        pltpu.SemaphoreType.DMA((4,))],
        compiler_params=_params(44),
    )(ua, bbr, bbi, ccr, cci, dsk, con, w_glu, b_glu)


def _s5_bwd(dya, y, ua, sr, si, bbr, bbi, ccr, cci, dsk, con_rev, w_glu, b_glu, carry=None):
    L = ua.shape[0]
    nt = L // TM
    spt = TM // SUB
    n_slab = spt

    def halo_map(i):
        return (jnp.maximum((nt - 1 - i) * spt - 1, 0), 0)

    def body(dya_ref, y_ref, ua_ref, sr_ref, si_ref, hr_ref, hi_ref, bbr_hbm, bbi_hbm, ccr_hbm, cci_hbm,
             dsk_ref, con_ref, wg_ref, bg_ref,
             dua_ref, dq_ref, dy_ref, lr_ref, li_ref, da_ref, dsm_ref,
             bbr_vm, bbi_vm, ccr_vm, cci_vm, cr_ref, ci_ref, w_sems):
        i = pl.program_id(0)
        landed = _resident([(ccr_hbm, ccr_vm), (cci_hbm, cci_vm), (bbr_hbm, bbr_vm), (bbi_hbm, bbi_vm)], w_sems)

        @pl.when(i == 0)
        def _():
            cr_ref[...] = jnp.zeros_like(cr_ref)
            ci_ref[...] = jnp.zeros_like(ci_ref)
            da_ref[...] = jnp.zeros_like(da_ref)
            dsm_ref[...] = jnp.zeros_like(dsm_ref)

        u = ua_ref[...]
        yv = y_ref[...]
        dya = dya_ref[...]
        zg = jax.nn.gelu(yv)
        sg = _sig(_mm(zg, wg_ref[...]) + bg_ref[...])
        dq = dya * zg * sg * (1.0 - sg)
        dq_ref[...] = dq.astype(BF)
        dzg = dya * sg + _mm_nt(dq, wg_ref[...])
        dy = dzg * _gelu_grad(yv)
        dyb = dy.astype(BF)
        dy_ref[...] = dyb
        dsm_ref[0:1, :] += _colsum(dy * u)
        dsm_ref[1:2, :] += _colsum(dq)
        landed(0)
        lr_ref[...] = _blockdiag_mm(dyb, ccr_vm)
        landed(1)
        li_ref[...] = -_blockdiag_mm(dyb, cci_vm)
        _cscan(lr_ref, li_ref, con_ref, cr_ref, ci_ref, reverse=True)

        first_tile = (i == nt - 1)
        row = _row_iota(LC)
        for lc in range(GN // LC):
            cols = slice(lc * LC, (lc + 1) * LC)
            h_r = jnp.where(first_tile, 0.0, hr_ref[:, cols])
            h_i = jnp.where(first_tile, 0.0, hi_ref[:, cols])

            def step(k, acc, cols=cols, h_r=h_r, h_i=h_i):
                ar, ai = acc
                rows = _slab(k)
                prev = _slab(jnp.maximum(k - 1, 0))
                pr = jnp.where(k == 0, h_r, sr_ref[prev, cols])
                pi = jnp.where(k == 0, h_i, si_ref[prev, cols])
                spr = pltpu.roll(jnp.where(row == SUB - 1, pr, sr_ref[rows, cols]), 1, 0)
                spi = pltpu.roll(jnp.where(row == SUB - 1, pi, si_ref[rows, cols]), 1, 0)
                lr, li = lr_ref[rows, cols], li_ref[rows, cols]
                return ar + lr * spr + li * spi, ai + li * spr - lr * spi

            zero = jnp.zeros((SUB, LC), F32)
            ar, ai = lax.fori_loop(0, n_slab, step, (zero, zero))
            da_ref[0:1, cols] += _colsum(ar)
            da_ref[1:2, cols] += _colsum(ai)

        landed(2)
        landed(3)
        dua_ref[...] = (dy * dsk_ref[...] + _blockdiag_mm_t(lr_ref[...].astype(BF), bbr_vm)
                        + _blockdiag_mm_t(li_ref[...].astype(BF), bbi_vm))

    return _pallas_call(
        body, carry, name="s5_bwd", grid=(nt,),
        in_specs=[_tok_rev(S5W, nt), _tok_rev(S5W, nt), _tok_rev(S5W, nt), _tok_rev(GN, nt), _tok_rev(GN, nt),
                  pl.BlockSpec((SUB, GN), halo_map), pl.BlockSpec((SUB, GN), halo_map),
                  ANY, ANY, ANY, ANY, _full((1, S5W)), _full((8 * SUB, GN)), _full((S5W, S5W)), _full((1, S5W))],
        out_specs=[_tok_rev(S5W, nt), _tok_rev(S5W, nt), _tok_rev(S5W, nt), _tok_rev(GN, nt), _tok_rev(GN, nt),
                   _full((SUB, GN)), _full((SUB, S5W))],
        out_shape=[_sds((L, S5W)), _sds((L, S5W), BF), _sds((L, S5W), BF), _sds((L, GN)), _sds((L, GN)),
                   _sds((SUB, GN)), _sds((SUB, S5W))],
        scratch_shapes=[pltpu.VMEM((S5W // 128, 128, GN // (S5W // 128)), BF)] * 4 + [
                        pltpu.VMEM((SUB, GN), F32), pltpu.VMEM((SUB, GN), F32),
                        pltpu.SemaphoreType.DMA((4,))],
        compiler_params=_params(52),
    )(dya, y, ua, sr, si, sr, si, bbr, bbi, ccr, cci, dsk, con_rev, w_glu, b_glu)


def _lru_gate_terms(rg, sp):
    log_a = -LRU_C * rg * sp
    a = jnp.exp(log_a)
    mult = jnp.sqrt(_neg_expm1(2.0 * log_a))
    return a, mult


def _lru_fwd(ub, conv_w, conv_b, wr, wi, b_r, b_i, sp, carry=None):
    L = ub.shape[0]
    n_slab = TM // SUB

    def body(ub_ref, cw_ref, cb_ref, wr_ref, wi_ref, br_ref, bi_ref, sp_ref,
             xc_ref, rg_ref, ig_ref, h_ref, hp_ref, a_ref, halo_ref, carry_ref):
        @pl.when(pl.program_id(0) == 0)
        def _():
            halo_ref[...] = jnp.zeros_like(halo_ref)
            carry_ref[...] = jnp.zeros_like(carry_ref)

        row = _row_iota(LW)
        taps = [cw_ref[k:k + 1, :] for k in range(4)]
        cb = cb_ref[...]

        def conv_step(k, prev):
            rows = _slab(k)
            cur = ub_ref[rows, :]
            acc = taps[3] * cur + cb
            for j in (1, 2, 3):
                acc = acc + taps[3 - j] * pltpu.roll(jnp.where(row >= SUB - j, prev, cur), j, 0)
            xc_ref[rows, :] = acc
            return cur

        halo_ref[...] = lax.fori_loop(0, n_slab, conv_step, halo_ref[...])

        xc = xc_ref[...]
        xcb = xc.astype(BF)
        rg = _sig(_blockdiag_mm(xcb, wr_ref) + br_ref[...])
        ig = _sig(_blockdiag_mm(xcb, wi_ref) + bi_ref[...])
        rg_ref[...] = rg
        ig_ref[...] = ig
        a, mult = _lru_gate_terms(rg, sp_ref[...])
        a_ref[...] = a
        h_ref[...] = mult * ig * xc

        rowc = _row_iota(LC)
        for lc in range(LW // LC):
            cols = slice(lc * LC, (lc + 1) * LC)

            def step(k, c, cols=cols):
                rows = _slab(k)
                av, b = a_ref[rows, cols], h_ref[rows, cols]
                for sh in (1, 2, 4):
                    keep = rowc >= sh
                    b = b + av * jnp.where(keep, pltpu.roll(b, sh, 0), 0.0)
                    av = av * jnp.where(keep, pltpu.roll(av, sh, 0), 1.0)
                h = b + av * c
                h_ref[rows, cols] = h
                hp_ref[rows, cols] = jnp.where(rowc == 0, c, pltpu.roll(h, 1, 0))
                return _bcast_row(h, SUB - 1)

            carry_ref[:, cols] = lax.fori_loop(0, n_slab, step, carry_ref[:, cols])

    return _pallas_call(
        body, carry, name="lru_fwd", grid=(L // TM,),
        in_specs=[_tok(LW), _full((4, LW)), _full((1, LW)), _full((LW // 128, 128, 128)), _full((LW // 128, 128, 128)),
                  _full((1, LW)), _full((1, LW)), _full((1, LW))],
        out_specs=[_tok(LW)] * 5,
        out_shape=[_sds((L, LW))] * 5,
        scratch_shapes=[pltpu.VMEM((TM, LW), F32), pltpu.VMEM((SUB, LW), F32), pltpu.VMEM((SUB, LW), F32)],
        compiler_params=_params(40),
    )(ub, conv_w, conv_b, wr, wi, b_r, b_i, sp)


def _lru_bwd(dyb, xc, rg, ig, hp, ub, conv_w, wr, wi, sp, dsp, carry=None):
    L = ub.shape[0]
    nt = L // TM
    spt = TM // SUB
    n_slab = spt

    def halo_map(i):
        return (jnp.maximum((nt - 1 - i) * spt - 1, 0), 0)

    def body(dh_ref, xc_ref, rg_ref, ig_ref, hp_ref, ub_ref, uh_ref, cw_ref, wr_ref, wi_ref, sp_ref, dsp_ref,
             dub_ref, dpr_ref, dpi_ref, acc_ref, a_ref, lam_ref, dxc_ref, carry_ref, next_ref):
        i = pl.program_id(0)

        @pl.when(i == 0)
        def _():
            carry_ref[...] = jnp.zeros_like(carry_ref)
            next_ref[...] = jnp.zeros_like(next_ref)
            acc_ref[...] = jnp.zeros_like(acc_ref)

        sp = sp_ref[...]
        rg, ig, xc = rg_ref[...], ig_ref[...], xc_ref[...]
        a, mult = _lru_gate_terms(rg, sp)
        a_ref[...] = a

        rowc = _row_iota(LC)
        for lc in range(LW // LC):
            cols = slice(lc * LC, (lc + 1) * LC)

            def step(k, c, cols=cols):
                rows = _slab(n_slab - 1 - k)
                av, dh = a_ref[rows, cols], dh_ref[rows, cols]
                b = av * dh
                for sh in (1, 2, 4):
                    keep = rowc < SUB - sh
                    b = b + av * jnp.where(keep, pltpu.roll(b, SUB - sh, 0), 0.0)
                    av = av * jnp.where(keep, pltpu.roll(av, SUB - sh, 0), 1.0)
                mu = b + av * c
                lam_ref[rows, cols] = dh + jnp.where(rowc == SUB - 1, c, pltpu.roll(mu, SUB - 1, 0))
                return _bcast_row(mu, 0)

            carry_ref[:, cols] = lax.fori_loop(0, n_slab, step, carry_ref[:, cols])

        lam = lam_ref[...]
        d_a = lam * hp_ref[...]
        d_mult = lam * ig * xc
        d_ig = lam * mult * xc
        dxc = lam * mult * ig
        d_log_a = d_a * a - d_mult * a * a / mult
        d_rg = (-LRU_C) * sp * d_log_a
        acc_ref[0:1, :] += _colsum((-LRU_C) * rg * d_log_a) * dsp_ref[...]
        dpr = d_rg * rg * (1.0 - rg)
        dpi = d_ig * ig * (1.0 - ig)
        acc_ref[1:2, :] += _colsum(dpr)
        acc_ref[2:3, :] += _colsum(dpi)
        dprb, dpib = dpr.astype(BF), dpi.astype(BF)
        dpr_ref[...] = dprb
        dpi_ref[...] = dpib
        dxc = dxc + _blockdiag_mm_t(dprb, wr_ref) + _blockdiag_mm_t(dpib, wi_ref)
        dxc_ref[...] = dxc
        acc_ref[3:4, :] += _colsum(dxc)

        row = _row_iota(LW)
        taps = [cw_ref[k:k + 1, :] for k in range(4)]
        u_halo = jnp.where(i == nt - 1, 0.0, uh_ref[...])
        nxt_tile = next_ref[...]

        def conv_step(k, accs):
            rows = _slab(k)
            cur = dxc_ref[rows, :]
            nxt = jnp.where(k == n_slab - 1, nxt_tile, dxc_ref[_slab(jnp.minimum(k + 1, n_slab - 1)), :])
            ucur = ub_ref[rows, :]
            uprev = jnp.where(k == 0, u_halo, ub_ref[_slab(jnp.maximum(k - 1, 0)), :])
            du = taps[3] * cur
            new = [accs[3] + cur * ucur]
            for j in (1, 2, 3):
                du = du + taps[3 - j] * pltpu.roll(jnp.where(row < j, nxt, cur), SUB - j, 0)
                new.append(accs[3 - j] + cur * pltpu.roll(jnp.where(row >= SUB - j, uprev, ucur), j, 0))
            dub_ref[rows, :] = du
            return tuple(new[::-1])

        zero = jnp.zeros((SUB, LW), F32)
        accs = lax.fori_loop(0, n_slab, conv_step, (zero, zero, zero, zero))
        for k in range(4):
            acc_ref[4 + k:5 + k, :] += _colsum(accs[k])
        next_ref[...] = dxc_ref[0:SUB, :]

    return _pallas_call(
        body, carry, name="lru_bwd", grid=(nt,),
        in_specs=[_tok_rev(LW, nt)] * 6 + [pl.BlockSpec((SUB, LW), halo_map), _full((4, LW)),
                                           _full((LW // 128, 128, 128)), _full((LW // 128, 128, 128)), _full((1, LW)), _full((1, LW))],
        out_specs=[_tok_rev(LW, nt), _tok_rev(LW, nt), _tok_rev(LW, nt), _full((SUB, LW))],
        out_shape=[_sds((L, LW)), _far((L, LW), BF), _far((L, LW), BF), _sds((SUB, LW))],
        scratch_shapes=[pltpu.VMEM((TM, LW), F32), pltpu.VMEM((TM, LW), F32), pltpu.VMEM((TM, LW), F32),
                        pltpu.VMEM((SUB, LW), F32), pltpu.VMEM((SUB, LW), F32)],
        compiler_params=_params(48),
    )(dyb, xc, rg, ig, hp, ub, ub, conv_w, wr, wi, sp, dsp)


AC = D // NCHIP


def _merge_fwd(x, ya, yb, gp, w_a, w_b, w_o, carry=None):
    L = x.shape[0]

    def body(x_ref, ya_ref, yb_ref, gp_ref, wa_ref, wb_ref, wo_ref, x1_ref, pa_ref, pb_ref, mg_ref):
        ya = ya_ref[...]
        for k in range(NCHIP):
            pa_ref[:, k * AC:(k + 1) * AC] = jnp.dot(ya, wa_ref[k], preferred_element_type=F32)
        pb = _mm(yb_ref[...], wb_ref[...])
        pb_ref[...] = pb
        gp = gp_ref[...]
        merged = (_sig(gp[:, :D]) * pa_ref[...] + _sig(gp[:, D:]) * pb).astype(BF)
        mg_ref[...] = merged
        x1_ref[...] = x_ref[...] + jnp.dot(merged, wo_ref[...], preferred_element_type=F32)

    return _pallas_call(
        body, carry, name="merge_fwd", grid=(L // TM,),
        in_specs=[_tok(D), _tok(S5W), _tok(LW), _tok(2 * D), _full((NCHIP, S5W, AC)), _full((LW, D)), _full((D, D))],
        out_specs=[_tok(D), _tok(D), _tok(D), _tok(D)],
        out_shape=[_sds((L, D)), _sds((L, D)), _sds((L, D)), _far((L, D), BF)],
        compiler_params=_params(40),
    )(x, ya, yb, gp, w_a, w_b, w_o)


def _merge_bwd(dx1, gp, pa, pb, w_a, w_b, w_o, carry=None):
    L = dx1.shape[0]

    def body(dx1_ref, gp_ref, pa_ref, pb_ref, wa_ref, wb_ref, wo_ref, dya_ref, dyb_ref, dgp_ref, dpa_ref, dpb_ref):
        dm = _mm_nt(dx1_ref[...], wo_ref[...])
        gp = gp_ref[...]
        sa, sb = _sig(gp[:, :D]), _sig(gp[:, D:])
        dpa = (dm * sa).astype(BF)
        dpb = (dm * sb).astype(BF)
        dpa_ref[...] = dpa
        dpb_ref[...] = dpb
        dgp_ref[:, :D] = dm * pa_ref[...] * sa * (1.0 - sa)
        dgp_ref[:, D:] = dm * pb_ref[...] * sb * (1.0 - sb)
        dya = jnp.zeros((TM, S5W), F32)
        for k in range(NCHIP):
            dya = dya + _mm_nt(dpa[:, k * AC:(k + 1) * AC], wa_ref[k])
        dya_ref[...] = dya
        dyb_ref[...] = _mm_nt(dpb, wb_ref[...])

    return _pallas_call(
        body, carry, name="merge_bwd", grid=(L // TM,),
        in_specs=[_tok(D), _tok(2 * D), _tok(D), _tok(D), _full((NCHIP, S5W, AC)), _full((LW, D)), _full((D, D))],
        out_specs=[_tok(S5W), _tok(LW), _tok(2 * D), _tok(D), _tok(D)],
        out_shape=[_sds((L, S5W)), _sds((L, LW)), _sds((L, 2 * D)), _far((L, D), BF), _far((L, D), BF)],
        compiler_params=_params(40),
    )(dx1, gp, pa, pb, w_a, w_b, w_o)


def _chunk_tok(width):
    return pl.BlockSpec((NCHIP, TM, width), lambda i: (0, i, 0))


def _ffn_fwd(x1, g_ffn, wg, wu, wd, carry=None):
    L = x1.shape[0]

    def body(x_ref, g_ref, wg_hbm, wu_hbm, wd_hbm, x2_ref, h2_ref, gg_ref, uu_ref, wg_vm, wu_vm, wd_vm, w_sems):
        _resident_now([(src.at[c], dst.at[c]) for c in range(NCHIP)
                       for src, dst in ((wg_hbm, wg_vm), (wu_hbm, wu_vm), (wd_hbm, wd_vm))], w_sems)
        x = x_ref[...]
        xh, _ = _rms(x)
        h2 = (xh * g_ref[...]).astype(BF)
        h2_ref[...] = h2
        out = x
        for c in range(NCHIP):
            gg = lax.dot_general(h2, wg_vm[c], (((1,), (1,)), ((), ())), preferred_element_type=F32)
            uu = lax.dot_general(h2, wu_vm[c], (((1,), (1,)), ((), ())), preferred_element_type=F32)
            gg_ref[c] = gg.astype(BF)
            uu_ref[c] = uu.astype(BF)
            act = (gg * _sig(gg) * uu).astype(BF)
            out = out + jnp.dot(act, wd_vm[c], preferred_element_type=F32)
        x2_ref[...] = out

    return _pallas_call(
        body, carry, name="ffn_fwd", grid=(L // TM,),
        in_specs=[_tok(D), _full((1, D)), ANY, ANY, ANY],
        out_specs=[_tok(D), _tok(D), _chunk_tok(FC), _chunk_tok(FC)],
        out_shape=[_sds((L, D)), _sds((L, D), BF), _sds((NCHIP, L, FC), BF), _sds((NCHIP, L, FC), BF)],
        scratch_shapes=[pltpu.VMEM((NCHIP, FC, D), BF)] * 3 + [pltpu.SemaphoreType.DMA((3 * NCHIP,))],
        compiler_params=_params(52),
    )(x1, g_ffn, wg, wu, wd)


def _ffn_bwd(x1, dx2, gg, uu, g_ffn, wg, wu, wd, carry=None):
    L = x1.shape[0]

    def body(x_ref, dx2_ref, gg_ref, uu_ref, g_ref, wg_hbm, wu_hbm, wd_hbm,
             dx1_ref, act_ref, dgg_ref, duu_ref, dg_ref, wg_vm, wu_vm, wd_vm, w_sems):
        _resident_now([(src.at[c], dst.at[c]) for c in range(NCHIP)
                       for src, dst in ((wg_hbm, wg_vm), (wu_hbm, wu_vm), (wd_hbm, wd_vm))], w_sems)

        @pl.when(pl.program_id(0) == 0)
        def _():
            dg_ref[...] = jnp.zeros_like(dg_ref)

        dx2 = dx2_ref[...]
        dx2b = dx2.astype(BF)
        dh2 = jnp.zeros((TM, D), F32)
        for c in range(NCHIP):
            g = gg_ref[c].astype(F32)
            u = uu_ref[c].astype(F32)
            s = _sig(g)
            silu = g * s
            act_ref[c] = (silu * u).astype(BF)
            dact = lax.dot_general(dx2b, wd_vm[c], (((1,), (1,)), ((), ())), preferred_element_type=F32)
            dg = (dact * u * s * (1.0 + g * (1.0 - s))).astype(BF)
            du = (dact * silu).astype(BF)
            dgg_ref[c] = dg
            duu_ref[c] = du
            dh2 = dh2 + jnp.dot(dg, wg_vm[c], preferred_element_type=F32)
            dh2 = dh2 + jnp.dot(du, wu_vm[c], preferred_element_type=F32)
        xh, r = _rms(x_ref[...])
        dg_ref[0:1, :] += _colsum(dh2 * xh)
        dx1_ref[...] = dx2 + _rms_bwd(dh2, xh, r, g_ref[...])

    return _pallas_call(
        body, carry, name="ffn_bwd", grid=(L // TM,),
        in_specs=[_tok(D), _tok(D), _chunk_tok(FC), _chunk_tok(FC), _full((1, D)), ANY, ANY, ANY],
        out_specs=[_tok(D), _chunk_tok(FC), _chunk_tok(FC), _chunk_tok(FC), _full((SUB, D))],
        out_shape=[_sds((L, D)), _sds((NCHIP, L, FC), BF), _sds((NCHIP, L, FC), BF), _sds((NCHIP, L, FC), BF),
                   _sds((SUB, D))],
        scratch_shapes=[pltpu.VMEM((NCHIP, FC, D), BF)] * 3 + [pltpu.SemaphoreType.DMA((3 * NCHIP,))],
        compiler_params=_params(56),
    )(x1, dx2, gg, uu, g_ffn, wg, wu, wd)


def _ple_loss(x2, p, tgt, g_pg, w_pg, b_pg, w_ple, g_ple, g_final):
    L = x2.shape[0]

    def body(x2_ref, p_ref, t_ref, gpg_ref, wpg_ref, bpg_ref, wple_ref, gple_ref, gf_ref,
             dx2_ref, n2_ref, dpre_ref, de0_ref, acc_ref):
        @pl.when(pl.program_id(0) == 0)
        def _():
            acc_ref[...] = jnp.zeros_like(acc_ref)

        x2 = x2_ref[...]
        x2h, r2 = _rms(x2)
        n2 = (x2h * gpg_ref[...]).astype(BF)
        n2_ref[...] = n2
        gate = _sig(jnp.dot(n2, wpg_ref[...], preferred_element_type=F32) + bpg_ref[...])
        pb = p_ref[...].astype(BF)
        e0 = jnp.concatenate([jnp.dot(pb, wple_ref[k], preferred_element_type=F32) for k in range(NCHIP)], axis=1)
        e0h, re = _rms(e0)
        e = e0h * gple_ref[...]
        x3 = x2 + gate * e
        x3h, r3 = _rms(x3)
        diff = x3h * gf_ref[...] - t_ref[...]
        acc_ref[4:5, :] += _colsum(diff * diff) * (0.5 / D)
        dy = diff * (1.0 / D)
        acc_ref[3:4, :] += _colsum(dy * x3h)
        dx3 = _rms_bwd(dy, x3h, r3, gf_ref[...])
        de = dx3 * gate
        acc_ref[2:3, :] += _colsum(de * e0h)
        de0_ref[...] = _rms_bwd(de, e0h, re, gple_ref[...]).astype(BF)
        dpre = dx3 * e * gate * (1.0 - gate)
        acc_ref[1:2, :] += _colsum(dpre)
        dpreb = dpre.astype(BF)
        dpre_ref[...] = dpreb
        dn2 = lax.dot_general(dpreb, wpg_ref[...], (((1,), (1,)), ((), ())), preferred_element_type=F32)
        acc_ref[0:1, :] += _colsum(dn2 * x2h)
        dx2_ref[...] = dx3 + _rms_bwd(dn2, x2h, r2, gpg_ref[...])

    return _pallas_call(
        body, name="ple_loss", grid=(L // TM,),
        in_specs=[_tok(D), _tok(PLE), _tok(D), _full((1, D)), _full((D, D)), _full((1, D)), _full((NCHIP, PLE, AC)),
                  _full((1, D)), _full((1, D))],
        out_specs=[_tok(D), _tok(D), _tok(D), _tok(D), _full((SUB, D))],
        out_shape=[_sds((L, D)), _sds((L, D), BF), _sds((L, D), BF), _sds((L, D), BF), _sds((SUB, D))],
        compiler_params=_params(40),
    )(x2, p, tgt, g_pg, w_pg, b_pg, w_ple, g_ple, g_final)


def _tn(name, a, b, col_chunk=None, a_block=None, carry=None):
    L = a.shape[-2]
    m, n = a.shape[-1], b.shape[-1]
    a_col = 0
    if a_block is not None:
        a_col, m = a_block
    tk = L if (a.ndim == 3 or b.ndim == 3 or a_block is not None) else TK
    if a.ndim == 3 or b.ndim == 3:
        nj, bn = (a if a.ndim == 3 else b).shape[0], n
        a_spec = (pl.BlockSpec((None, tk, m), lambda j, t: (j, t, 0)) if a.ndim == 3
                  else pl.BlockSpec((tk, m), lambda j, t: (t, 0)))
        b_spec = (pl.BlockSpec((None, tk, n), lambda j, t: (j, t, 0)) if b.ndim == 3
                  else pl.BlockSpec((tk, n), lambda j, t: (t, 0)))
        out_spec, out_shape = pl.BlockSpec((None, m, n), lambda j, t: (j, 0, 0)), _sds((nj, m, n))
    else:
        bn = col_chunk
        if bn is None:
            bn = next((cand for cand in (1024, 512) if n > cand and n % cand == 0), n)
        nj = n // bn
        a_spec = pl.BlockSpec((tk, m), lambda j, t: (t, a_col))
        b_spec = pl.BlockSpec((tk, bn), lambda j, t: (t, j))
        if col_chunk is None:
            out_spec, out_shape = pl.BlockSpec((m, bn), lambda j, t: (0, j)), _sds((m, n))
        else:
            out_spec, out_shape = pl.BlockSpec((None, m, bn), lambda j, t: (j, 0, 0)), _sds((nj, m, bn))

    def body(a_ref, b_ref, o_ref):
        if tk == L:
            o_ref[...] = _mm_tn(a_ref[...], b_ref[...])
        else:
            @pl.when(pl.program_id(1) == 0)
            def _():
                o_ref[...] = jnp.zeros_like(o_ref)

            o_ref[...] += _mm_tn(a_ref[...], b_ref[...])

    outs = _pallas_call(
        body, carry, name=name, grid=(nj, L // tk), in_specs=[a_spec, b_spec], out_specs=[out_spec],
        out_shape=[pltpu.HBM(out_shape.shape, out_shape.dtype)],
        compiler_params=pltpu.CompilerParams(dimension_semantics=("arbitrary", "arbitrary"),
                                             vmem_limit_bytes=(30 if tk == L else 28) * VMEM_MB),
    )(*(_in_hbm([a, b]) if tk == L else (a, b)))
    return outs[0] if carry is None else outs


LANE = 128


def _tn_blocks(name, a, bs, ga, gb, carry=None):
    L, m, n, nb = a.shape[0], a.shape[1], bs[0].shape[1], len(bs)
    per = LANE // ga
    wb = per * gb
    n_super = m // LANE

    def body(a_ref, *refs):
        b_refs, o_refs, acc_refs = refs[:nb], refs[nb:2 * nb], refs[2 * nb:]
        t = pl.program_id(0)

        @pl.when(t == 0)
        def _():
            for acc in acc_refs:
                acc[...] = jnp.zeros_like(acc)

        lhs = a_ref[...].astype(BF)
        for b_ref, acc in zip(b_refs, acc_refs):
            rhs = b_ref[...].astype(BF)
            for j in range(n_super):
                acc[j] += _mm_tn(lhs[:, j * LANE:(j + 1) * LANE], rhs[:, j * wb:(j + 1) * wb])

        @pl.when(t == L // TK - 1)
        def _():
            own = (lax.broadcasted_iota(jnp.int32, (LANE, wb), 0) // ga) == (lax.broadcasted_iota(jnp.int32, (LANE, wb), 1) // gb)
            for o_ref, acc in zip(o_refs, acc_refs):
                for j in range(n_super):
                    kept = jnp.where(own, acc[j], 0.0)
                    o_ref[:, j * wb:(j + 1) * wb] = jnp.sum(kept.reshape(per, ga, wb), axis=0)

    outs = _pallas_call(
        body, carry, name=name, grid=(L // TK,),
        in_specs=[pl.BlockSpec((TK, m), lambda t: (t, 0))] + [pl.BlockSpec((TK, n), lambda t: (t, 0))] * nb,
        out_specs=[_full((ga, n))] * nb, out_shape=[_sds((ga, n))] * nb,
        scratch_shapes=[pltpu.VMEM((n_super, LANE, wb), F32)] * nb,
        compiler_params=_params(48),
    )(*_in_hbm([a] + list(bs)))
    return list(outs)


def _s5_discretize(lam_re, lam_im, log_dt, b_re, b_im):
    dt = jnp.exp(log_dt)[:, None]
    mag = jnp.exp(lam_re * dt)
    ar = mag * jnp.cos(lam_im * dt)
    ai = mag * jnp.sin(lam_im * dt)
    den = lam_re * lam_re + lam_im * lam_im
    nr = ar - 1.0
    fr = (nr * lam_re + ai * lam_im) / den
    fi = (ai * lam_re - nr * lam_im) / den
    bbr = fr[:, None, :] * b_re - fi[:, None, :] * b_im
    bbi = fr[:, None, :] * b_im + fi[:, None, :] * b_re
    return ar, ai, bbr, bbi


def _prepare(by_rows, block_cols, ar, ai):
    n = len(by_rows)

    def body(*refs):
        srcs, (ar_ref, ai_ref), dense, (con_ref, rev_ref) = refs[:n], refs[n:n + 2], refs[n + 2:2 * n + 2], refs[2 * n + 2:]
        for src, out, c in zip(srcs, dense, block_cols):
            r = src.shape[0]
            per = LANE // r
            wide = per * c
            own = (lax.broadcasted_iota(jnp.int32, (LANE, wide), 0) // r) == (lax.broadcasted_iota(jnp.int32, (LANE, wide), 1) // c)
            for j in range(out.shape[0]):
                tiled = jnp.broadcast_to(src[:, j * wide:(j + 1) * wide][None], (per, r, wide)).reshape(LANE, wide)
                out[j] = jnp.where(own, tiled, 0.0).astype(BF)
        a_r, a_i = ar_ref[...], ai_ref[...]
        pw = [(jnp.ones_like(a_r), jnp.zeros_like(a_i))]
        for _ in range(SUB):
            pr, pi = pw[-1]
            pw.append((pr * a_r - pi * a_i, pr * a_i + pi * a_r))
        row = _row_iota(GN)
        for ref, reverse in ((con_ref, False), (rev_ref, True)):
            sign = -1.0 if reverse else 1.0
            for j, sh in enumerate((1, 2, 4)):
                keep = (row < SUB - sh) if reverse else (row >= sh)
                ref[2 * j * SUB:(2 * j + 1) * SUB, :] = jnp.where(keep, pw[sh][0], 0.0)
                ref[(2 * j + 1) * SUB:(2 * j + 2) * SUB, :] = jnp.where(keep, sign * pw[sh][1], 0.0)
            p_r, p_i = jnp.zeros((SUB, GN), F32), jnp.zeros((SUB, GN), F32)
            for i in range(SUB):
                k = SUB - i if reverse else i + 1
                p_r = jnp.where(row == i, pw[k][0], p_r)
                p_i = jnp.where(row == i, sign * pw[k][1], p_i)
            ref[6 * SUB:7 * SUB, :] = p_r
            ref[7 * SUB:8 * SUB, :] = p_i

    dense_shapes = [(b.shape[1] // (LANE // b.shape[0] * c), LANE, LANE // b.shape[0] * c)
                    for b, c in zip(by_rows, block_cols)]
    outs = _pallas_call(
        body, name="prepare", grid=(1,), in_specs=[_full(b.shape) for b in by_rows] + [_full((1, GN))] * 2,
        out_specs=[_full(s) for s in dense_shapes] + [_full((8 * SUB, GN))] * 2,
        out_shape=[_far(s, BF) for s in dense_shapes] + [_sds((8 * SUB, GN)), _far((8 * SUB, GN))],
        compiler_params=_params(48),
    )(*by_rows, ar, ai)
    return outs[:n], outs[n], outs[n + 1]


def _local_step(x, p, tgt, w, comm):
    rows_of = lambda a: a.reshape(NCHIP * a.shape[1], a.shape[2])
    quarters = lambda a: a.reshape(NCHIP, a.shape[0] // NCHIP, a.shape[1])

    def gathering(names, call):
        carry = comm.gather(names)
        outs = list(call(carry))
        own = len(outs) - len(carry.out_shapes)
        w.update(zip(names, outs[own:]))
        return outs[:own]

    w.update(comm.first())
    w_glu = rows_of(w["w_glu"])
    ar, ai, bbr, bbi = _s5_discretize(w["lam_re"], w["lam_im"], w["log_dt"], w["s5_b_re"], w["s5_b_im"])
    by_row = lambda b: jnp.transpose(b, (1, 0, 2)).reshape(b.shape[1], -1)
    (bbr_d, bbi_d, ccr_d, cci_d, wr_d, wi_d), con, con_rev = _prepare(
        [by_row(b) for b in (bbr, bbi, w["s5_c_re"], w["s5_c_im"], w["w_r"], w["w_i"])], [NS] * 4 + [HD] * 2,
        ar.reshape(1, GN), ai.reshape(1, GN))
    dsk = w["s5_d"].reshape(1, S5W)
    lam = w["lru_lambda"].reshape(1, LW)
    sp = jax.nn.softplus(-lam)
    b_r, b_i = w["b_r"].reshape(1, LW), w["b_i"].reshape(1, LW)
    row = lambda name: w[name].reshape(1, -1)

    h, ua, ub, gp = gathering(["w_a_out", "w_b_out"], lambda carry: _inproj_fwd(
        x, row("g_mix"), w["w_in"], row("b_in"), carry))
    sr, si, y, zg, ya = gathering(["w_o", "w_ffn_gate"], lambda carry: _s5_fwd(
        ua, bbr_d, bbi_d, ccr_d, cci_d, dsk, con, w_glu, row("b_glu"), carry))
    xc, rg, ig, yb, hp = gathering(["w_ffn_up"], lambda carry: _lru_fwd(
        ub, w["conv_w"], row("conv_b"), wr_d, wi_d, b_r, b_i, sp, carry))
    w_b_out, w_o = rows_of(w["w_b_out"]), rows_of(w["w_o"])
    x1, pa, pb, merged = gathering(["w_ffn_down"], lambda carry: _merge_fwd(
        x, ya, yb, gp, w["w_a_out"], w_b_out, w_o, carry))
    x2, h2, gg, uu = gathering(["w_ple_gate", "w_ple"], lambda carry: _ffn_fwd(
        x1, row("g_ffn"), w["w_ffn_gate"], w["w_ffn_up"], w["w_ffn_down"], carry))
    w_pg = rows_of(w["w_ple_gate"])
    dx2, n2, dpre, de0, acc_p = _ple_loss(x2, p, tgt, row("g_ple_gate"), w_pg, row("b_ple_gate"),
                                          w["w_ple"], row("g_ple"), row("g_final"))
    comm.reduce("ple", {"w_ple_gate": quarters(_tn("dw_ple_gate", n2, dpre)),
                        "w_ple": _tn("dw_ple", p, de0, col_chunk=AC)})
    dx1, act, dgg, duu, acc_f = comm.run(lambda carry: _ffn_bwd(
        x1, dx2, gg, uu, row("g_ffn"), w["w_ffn_gate"], w["w_ffn_up"], w["w_ffn_down"], carry))
    comm.reduce("ffn_gate", {"w_ffn_gate": _tn("dw_ffn_gate", dgg, h2)})
    comm.reduce("w_o", {"w_o": quarters(_tn("dw_o", merged, dx1))})
    comm.reduce("ffn_up", {"w_ffn_up": comm.run(lambda carry: _tn("dw_ffn_up", duu, h2, carry=carry))[0]})
    comm.reduce("ffn_down", {"w_ffn_down": comm.run(lambda carry: _tn("dw_ffn_down", act, dx2, carry=carry),
                                                    hold=("ffn_gate", "w_o"))[0]})
    dya, dyb, dgp, dpa, dpb = comm.run(lambda carry: _merge_bwd(
        dx1, gp, pa, pb, w["w_a_out"], w_b_out, w_o, carry), hold=("ffn_gate", "ffn_up"))
    comm.reduce("merge", {"w_a_out": _tn("dw_a_out", ya, dpa, col_chunk=AC), "w_b_out": quarters(_tn("dw_b_out", yb, dpb))})
    dua, dq, dy, lr, li, acc_a, acc_s = comm.run(lambda carry: _s5_bwd(
        dya, y, ua, sr, si, bbr_d, bbi_d, ccr_d, cci_d, dsk, con_rev, w_glu, row("b_glu"), carry), hold=("ffn_down",))
    dub, dpr, dpi, acc_l = comm.run(lambda carry: _lru_bwd(
        dyb, xc, rg, ig, hp, ub, w["conv_w"], wr_d, wi_d, sp, -_sig(-lam), carry))
    gx, dz, acc_g, acc_b = _inproj_bwd(x, dx1, dua, dub, dgp, row("g_mix"), w["w_in"])
    half = (D // 2,)
    comm.reduce("in_lo", {"w_in_lo": comm.run(lambda carry: _tn(
        "dw_in_lo", h, dz, col_chunk=QC, a_block=(0,) + half, carry=carry))[0]})
    comm.reduce("in_hi", {"w_in_hi": comm.run(lambda carry: _tn(
        "dw_in_hi", h, dz, col_chunk=QC, a_block=(1,) + half, carry=carry))[0], "w_glu": quarters(_tn("dw_glu", zg, dq))})
    d_wr, d_wi = comm.run(lambda carry: _tn_blocks("dw_r_i", xc, [dpr, dpi], HD, HD, carry))
    d_bbr, d_bbi = comm.run(lambda carry: _tn_blocks("d_bb", ua, [lr, li], NP, NS, carry))
    d_ccr, d_cci = comm.run(lambda carry: _tn_blocks("d_cc", dy, [sr, si], NP, NS, carry))
    comm.drain()
    sums = {"ple": acc_p, "ffn": acc_f, "mix": acc_g, "b_in": acc_b, "lru": acc_l, "s5": acc_s, "s5_a": acc_a}
    blocks = {"bb_re": d_bbr, "bb_im": d_bbi,
              "cc_re": d_ccr, "cc_im": d_cci,
              "w_r": d_wr, "w_i": d_wi}
    return gx, sums, blocks


def _replicated_grads(w, sums, blocks):
    grouped = lambda e, groups: jnp.transpose(e.reshape(e.shape[0], groups, -1), (1, 0, 2))
    d_ar, d_ai = sums["s5_a"][0].reshape(NG, NS), sums["s5_a"][1].reshape(NG, NS)
    d_bbr, d_bbi = grouped(blocks["bb_re"], NG), grouped(blocks["bb_im"], NG)
    _, vjp = jax.vjp(_s5_discretize, w["lam_re"], w["lam_im"], w["log_dt"], w["s5_b_re"], w["s5_b_im"])
    g = dict(zip(("lam_re", "lam_im", "log_dt", "s5_b_re", "s5_b_im"), vjp((d_ar, d_ai, d_bbr, d_bbi))))
    g["s5_c_re"] = grouped(blocks["cc_re"], NG)
    g["s5_c_im"] = -grouped(blocks["cc_im"], NG)
    g["w_r"], g["w_i"] = grouped(blocks["w_r"], NH), grouped(blocks["w_i"], NH)
    g["s5_d"] = sums["s5"][0].reshape(NG, NP)
    g["b_r"] = sums["lru"][1].reshape(NH, HD)
    g["b_i"] = sums["lru"][2].reshape(NH, HD)
    return g


ACC_ROWS = {"g_mix": ("mix", 0), "b_in": ("b_in", 0), "g_ffn": ("ffn", 0), "g_ple_gate": ("ple", 0),
            "b_ple_gate": ("ple", 1), "g_ple": ("ple", 2), "g_final": ("ple", 3), "b_glu": ("s5", 1),
            "lru_lambda": ("lru", 0), "conv_b": ("lru", 3)}
LOSS_ROW = ("ple", 4)
CONV_W_ROWS = ("lru", 4)


SHARDED = [("w_in", (D, QC)), ("w_glu", (S5W // NCHIP, S5W)), ("w_a_out", (S5W, AC)), ("w_b_out", (LW // NCHIP, D)),
           ("w_o", (D // NCHIP, D)), ("w_ffn_gate", (FC, D)), ("w_ffn_up", (FC, D)), ("w_ffn_down", (FC, D)),
           ("w_ple_gate", (D // NCHIP, D)), ("w_ple", (PLE, AC))]
TRANSPOSED = ("w_ffn_gate", "w_ffn_up", "s5_b_re", "s5_b_im")
CONV_SHARD = (4, LW // NCHIP)


def _mesh_pos():
    return lax.axis_index("x"), lax.axis_index("y"), lax.axis_index("c")


def _other_chips(x, y):
    return [(1 - x, y), (x, 1 - y), (1 - x, 1 - y)]


def _half_rows(c, rows, align):
    return pl.ds(pl.multiple_of(c * (rows // 2), align), rows // 2)


def _run_now(name, carry):
    c_in, c_out = len(carry.operands), len(carry.out_shapes)

    def body(*refs):
        ins, outs, sems = refs[:c_in], refs[c_in:c_in + c_out], refs[c_in + c_out:]
        carry.start(ins, outs, sems)
        carry.finish(ins, outs, sems)

    return pl.pallas_call(body, name=name, in_specs=[ANY] * c_in, out_specs=[ANY] * c_out,
                          out_shape=list(carry.out_shapes), scratch_shapes=list(carry.sems),
                          input_output_aliases=dict(carry.aliases))(*_in_hbm(carry.operands))


def _gather_group(shards, split):
    n = len(shards)

    def copies(srcs, outs, sems):
        send_sems, recv_sems = sems
        x, y, c = _mesh_pos()
        k0 = 2 * x + y
        sib = (x, y, 1 - c)
        chips = _other_chips(x, y)

        def remote(src, dst, j, i, to):
            return pltpu.make_async_remote_copy(src_ref=src, dst_ref=dst, send_sem=send_sems.at[j, i],
                                                recv_sem=recv_sems.at[j, i], device_id=to, device_id_type=MESH)

        def rows(ref, i, core, *lead):
            if not split[i]:
                return ref.at[lead] if lead else ref
            return ref.at[(*lead, _half_rows(core, shards[i].shape[0], 16))]

        own = [remote(s, o.at[k0], 6, i, sib) for i, (s, o) in enumerate(zip(srcs, outs))]
        ici, landed, fwd, fwd_landed = [], [], [], []
        for j, chip in enumerate(chips):
            kj = 2 * chip[0] + chip[1]
            pairs = list(enumerate(zip(srcs, outs)))
            ici.append([remote(rows(s, i, c), rows(o, i, c, k0), j, i, (*chip, c)) for i, (s, o) in pairs])
            landed.append([remote(rows(s, i, c), rows(o, i, c, kj), j, i, (*chip, c)) for i, (s, o) in pairs])
            fwd.append([remote(rows(o, i, c, kj), rows(o, i, c, kj), 3 + j, i, sib) for i, (s, o) in pairs if split[i]])
            fwd_landed.append([remote(rows(o, i, 1 - c, kj), rows(o, i, 1 - c, kj), 3 + j, i, sib)
                               for i, (s, o) in pairs if split[i]])
        return own, ici, landed, fwd, fwd_landed

    def start(srcs, outs, sems):
        own, ici, _, _, _ = copies(srcs, outs, sems)
        for cp in own + [cp for per_chip in ici for cp in per_chip]:
            cp.start()

    def finish(srcs, outs, sems):
        own, ici, landed, fwd, fwd_landed = copies(srcs, outs, sems)
        passed = [i for i in range(n) if split[i]]
        for j in range(3):
            for i, cp in enumerate(landed[j]):
                cp.wait_recv()
                if split[i]:
                    fwd[j][passed.index(i)].start()
        for j in range(3):
            for cp in fwd_landed[j]:
                cp.wait_recv()
        for cp in own:
            cp.wait_recv()
        for cp in own + [cp for per_chip in ici + fwd for cp in per_chip]:
            cp.wait_send()

    return _Carried(shards, [_far((NCHIP,) + s.shape, s.dtype) for s in shards],
                    [pltpu.SemaphoreType.DMA((7, n)), pltpu.SemaphoreType.DMA((7, n))], start, finish)


def _each_copy(copies, carried, out_shapes, sems, aliases=None):
    def start(ins, outs, sem_refs):
        for cp in copies(ins, outs, sem_refs):
            cp.start()

    def finish(ins, outs, sem_refs):
        for cp in copies(ins, outs, sem_refs):
            cp.wait()

    return _Carried(carried, out_shapes, sems, start, finish, aliases)


def _swap_group(grads):
    n = len(grads)

    def copies(srcs, outs, sems):
        send_sems, recv_sems = sems
        x, y, c = _mesh_pos()
        return [pltpu.make_async_remote_copy(src_ref=s.at[:, _half_rows(1 - c, s.shape[1], 8)], dst_ref=o,
                                             send_sem=send_sems.at[i], recv_sem=recv_sems.at[i], device_id=(x, y, 1 - c),
                                             device_id_type=MESH) for i, (s, o) in enumerate(zip(srcs, outs))]

    return _each_copy(copies, grads, [pltpu.HBM((NCHIP, g.shape[1] // 2, g.shape[2]), F32) for g in grads],
                      [pltpu.SemaphoreType.DMA((n,)), pltpu.SemaphoreType.DMA((n,))])


def _add_sibling_group(tag, kc_idx, grads, gots):
    n = len(grads)

    def body(kc_ref, *refs):
        for g, rx, p, pb in zip(refs[:n], refs[n:2 * n], refs[2 * n:3 * n], refs[3 * n:]):
            s = g[...] + rx[...]
            pb[...] = s.astype(BF)

            @pl.when(pl.program_id(0) == kc_ref[0])
            def _():
                p[...] = s

    halves = [pl.BlockSpec((None,) + rx.shape[1:], lambda k, kc_ref: (k, 0, 0)) for rx in gots]
    mine = [pl.BlockSpec((None,) + rx.shape[1:], lambda k, kc_ref: (k, kc_ref[1], 0)) for rx in gots]
    own = [pl.BlockSpec(rx.shape[1:], lambda k, kc_ref: (0, 0)) for rx in gots]
    outs = _pallas_call(
        body, name="add_sibling_" + tag,
        grid_spec=pltpu.PrefetchScalarGridSpec(num_scalar_prefetch=1, grid=(NCHIP,), in_specs=mine + halves,
                                               out_specs=own + halves),
        out_shape=[pltpu.HBM(rx.shape[1:], F32) for rx in gots] + [pltpu.HBM(rx.shape, BF) for rx in gots],
        compiler_params=_params(48),
    )(kc_idx, *_in_hbm(list(grads) + list(gots)))
    return outs[:n], outs[n:]


def _exchange_group(parts):
    n = len(parts)

    def copies(srcs, outs, sems):
        send_sems, recv_sems = sems
        x, y, c = _mesh_pos()
        return [pltpu.make_async_remote_copy(
            src_ref=s.at[2 * chip[0] + chip[1]], dst_ref=o.at[j], send_sem=send_sems.at[j, i],
            recv_sem=recv_sems.at[j, i], device_id=(*chip, c), device_id_type=MESH)
            for j, chip in enumerate(_other_chips(x, y)) for i, (s, o) in enumerate(zip(srcs, outs))]

    return _each_copy(copies, parts, [pltpu.HBM((3,) + p.shape[1:], BF) for p in parts],
                      [pltpu.SemaphoreType.DMA((3, n)), pltpu.SemaphoreType.DMA((3, n))])


def _add_chips_group(tag, kc_idx, parts, arrived):
    n = len(parts)

    def body(kc_ref, *refs):
        for p, rx, t in zip(refs[:n], refs[n:2 * n], refs[2 * n:]):
            t[...] = ((p[...] + rx[0].astype(F32)) + rx[1].astype(F32)) + rx[2].astype(F32)

    outs = _pallas_call(
        body, name="add_chips_" + tag,
        grid_spec=pltpu.PrefetchScalarGridSpec(
            num_scalar_prefetch=1, grid=(1,),
            in_specs=([pl.BlockSpec(rx.shape[1:], lambda i, kc_ref: (0, 0)) for rx in arrived]
                      + [pl.BlockSpec(rx.shape, lambda i, kc_ref: (0, 0, 0)) for rx in arrived]),
            out_specs=[pl.BlockSpec((None,) + rx.shape[1:], lambda i, kc_ref: (kc_ref[1], 0, 0)) for rx in arrived]),
        out_shape=[pltpu.HBM((2,) + rx.shape[1:], F32) for rx in arrived],
        compiler_params=_params(48),
    )(kc_idx, *_in_hbm(list(parts) + list(arrived)))
    return list(outs)


def _join_group(halves):
    n = len(halves)

    def copies(bufs, sems):
        send_sems, recv_sems = sems
        x, y, c = _mesh_pos()
        sib = (x, y, 1 - c)
        sends = [pltpu.make_async_remote_copy(src_ref=b.at[c], dst_ref=b.at[c], send_sem=send_sems.at[i],
                                              recv_sem=recv_sems.at[i], device_id=sib, device_id_type=MESH)
                 for i, b in enumerate(bufs)]
        landed = [pltpu.make_async_remote_copy(src_ref=b.at[c], dst_ref=b.at[1 - c], send_sem=send_sems.at[i],
                                               recv_sem=recv_sems.at[i], device_id=sib, device_id_type=MESH)
                  for i, b in enumerate(bufs)]
        return sends, landed

    def start(_, bufs, sems):
        for cp in copies(bufs, sems)[0]:
            cp.start()

    def finish(_, bufs, sems):
        sends, landed = copies(bufs, sems)
        for cp in landed:
            cp.wait_recv()
        for cp in sends:
            cp.wait_send()

    return _Carried(halves, [pltpu.HBM(h.shape, F32) for h in halves],
                    [pltpu.SemaphoreType.DMA((n,)), pltpu.SemaphoreType.DMA((n,))], start, finish,
                    {i: i for i in range(n)})


def _combine(carries):
    operands, out_shapes, sems, aliases, spans = [], [], [], {}, []
    for c in carries:
        aliases.update({len(operands) + i: len(out_shapes) + o for i, o in c.aliases.items()})
        spans.append((len(operands), len(out_shapes), len(sems)))
        operands += list(c.operands)
        out_shapes += list(c.out_shapes)
        sems += list(c.sems)

    def each(phase):
        def run(ins, outs, sem_refs):
            for c, (a, b, s) in zip(carries, spans):
                getattr(c, phase)(ins[a:a + len(c.operands)], outs[b:b + len(c.out_shapes)], sem_refs[s:s + len(c.sems)])
        return run

    return _Carried(operands, out_shapes, sems, each("start"), each("finish"), aliases)


def _allreduce_small(arrays, wire):
    n = len(arrays)
    halves = [(a.shape[0], a.shape[1] // 2) for a in arrays]

    def body(*refs):
        srcs, outs = refs[:n], refs[n:2 * n]
        mine_bufs, sib_bufs, chip_bufs, total_bufs = (refs[k * n:(k + 1) * n] for k in range(2, 6))
        send_sems, recv_sems, local_sems = refs[6 * n:]
        x, y, c = _mesh_pos()
        k0 = 2 * x + y
        sib = (x, y, 1 - c)

        def remote(src, dst, j, i, to):
            return pltpu.make_async_remote_copy(src_ref=src, dst_ref=dst, send_sem=send_sems.at[j, i],
                                                recv_sem=recv_sems.at[j, i], device_id=to, device_id_type=MESH)

        def cols(ref, i, core):
            return ref.at[:, pl.ds(pl.multiple_of(core * halves[i][1], LANE), halves[i][1])]

        swaps = [remote(cols(s, i, 1 - c), b, 0, i, sib) for i, (s, b) in enumerate(zip(srcs, sib_bufs))]
        own = [pltpu.make_async_copy(cols(s, i, c), m, local_sems.at[i]) for i, (s, m) in enumerate(zip(srcs, mine_bufs))]
        for cp in swaps + own:
            cp.start()
        for cp in swaps + own:
            cp.wait()
        for m, b, buf in zip(mine_bufs, sib_bufs, chip_bufs):
            buf[k0] = (m[...] + b[...]).astype(buf.dtype)
        chips = _other_chips(x, y)
        sends = [remote(buf.at[k0], buf.at[k0], 1 + j, i, (*chip, c))
                 for j, chip in enumerate(chips) for i, buf in enumerate(chip_bufs)]
        for cp in sends:
            cp.start()
        for j, chip in enumerate(chips):
            for i, buf in enumerate(chip_bufs):
                remote(buf.at[k0], buf.at[2 * chip[0] + chip[1]], 1 + j, i, (*chip, c)).wait_recv()
        for cp in sends:
            cp.wait_send()
        for t, buf in zip(total_bufs, chip_bufs):
            t[...] = ((buf[0].astype(F32) + buf[1].astype(F32)) + buf[2].astype(F32)) + buf[3].astype(F32)
        joins = [remote(t, cols(o, i, c), 4, i, sib) for i, (t, o) in enumerate(zip(total_bufs, outs))]
        keep = [pltpu.make_async_copy(t, cols(o, i, c), local_sems.at[i]) for i, (t, o) in enumerate(zip(total_bufs, outs))]
        for cp in joins + keep:
            cp.start()
        for i, (t, o) in enumerate(zip(total_bufs, outs)):
            remote(t, cols(o, i, 1 - c), 4, i, sib).wait_recv()
        for cp in joins:
            cp.wait_send()
        for cp in keep:
            cp.wait()

    specs = [_full(a.shape) for a in arrays]
    return _pallas_call(
        body, name="allreduce_small", grid=(1,), in_specs=specs, out_specs=specs,
        out_shape=[_sds(a.shape) for a in arrays],
        scratch_shapes=([pltpu.VMEM(h, F32) for h in halves] + [pltpu.VMEM(h, F32) for h in halves]
                        + [pltpu.VMEM((NCHIP,) + h, dt) for h, dt in zip(halves, wire)] + [pltpu.VMEM(h, F32) for h in halves]
                        + [pltpu.SemaphoreType.DMA((5, n)), pltpu.SemaphoreType.DMA((5, n)), pltpu.SemaphoreType.DMA((n,))]),
        compiler_params=_params(32),
    )(*arrays)


def _adamw_terms(w, g, m, v):
    m = ADAM_B1 * m + (1.0 - ADAM_B1) * g
    v = ADAM_B2 * v + (1.0 - ADAM_B2) * jnp.square(g)
    m_hat = m / (1.0 - ADAM_B1 ** ADAM_STEP)
    v_hat = v / (1.0 - ADAM_B2 ** ADAM_STEP)
    return -ADAM_LR * (m_hat / (jnp.sqrt(v_hat) + ADAM_EPS) + ADAM_WD * w), m, v


ADAM_STEPS = 4


def _adamw_group(tag, ws, gs, ms, vs):
    n = len(ws)

    def body(*refs):
        ins, outs = refs[:4 * n], refs[4 * n:]
        for i in range(n):
            w, g, m, v = (ins[k * n + i][...] for k in range(4))
            outs[i][...] = g
            outs[n + i][...], outs[2 * n + i][...], outs[3 * n + i][...] = _adamw_terms(w, g, m, v)

    specs = [pl.BlockSpec((w.shape[0] // ADAM_STEPS, w.shape[1]), lambda i: (i, 0)) for w in ws]
    outs = _pallas_call(
        body, name="adamw_" + tag, grid=(ADAM_STEPS,), in_specs=specs * 4, out_specs=specs * 4,
        out_shape=[_sds(w.shape) for w in ws] * 4, compiler_params=_params(48),
    )(*_in_hbm(list(ws) + list(gs) + list(ms) + list(vs)))
    return outs[:n], outs[n:2 * n], outs[2 * n:3 * n], outs[3 * n:]


def _adamw_replicated(sums, row_of, direct):
    ns, nr, nd = len(sums), len(row_of), len(direct)

    def body(*refs):
        sum_refs = refs[:ns]
        ins = refs[ns:ns + 3 * nr + 4 * nd]
        outs = refs[ns + 3 * nr + 4 * nd:]
        for i, (_, _, _, si, row) in enumerate(row_of):
            w_ref, m_ref, v_ref = ins[3 * i:3 * i + 3]
            g = sum_refs[si][row:row + 1, :]
            outs[4 * i][...] = g
            outs[4 * i + 1][...], outs[4 * i + 2][...], outs[4 * i + 3][...] = _adamw_terms(w_ref[...], g, m_ref[...], v_ref[...])
        for i in range(nd):
            w_ref, m_ref, v_ref, g_ref = ins[3 * nr + 4 * i:3 * nr + 4 * i + 4]
            o = outs[4 * (nr + i):4 * (nr + i) + 4]
            g = g_ref[...]
            o[0][...] = g
            o[1][...], o[2][...], o[3][...] = _adamw_terms(w_ref[...], g, m_ref[...], v_ref[...])

    operands = list(sums)
    shapes = []
    for w, m, v, _, _ in row_of:
        operands += [w, m, v]
        shapes += [w.shape] * 4
    for w, m, v, g in direct:
        operands += [w, m, v, g]
        shapes += [w.shape] * 4
    flat = _pallas_call(
        body, name="adamw_replicated", grid=(1,), in_specs=[_full(a.shape) for a in operands],
        out_specs=[_full(s) for s in shapes], out_shape=[_sds(s) for s in shapes],
        compiler_params=_params(56),
    )(*operands)
    return [flat[4 * i:4 * i + 4] for i in range(nr + nd)]


class _Exchanges:
    def __init__(self, shards, conv_w, chip, core, apply):
        self.shards, self.conv_w, self.apply = shards, conv_w, apply
        self.active, self.calls = [], 0
        self.chip_core_idx = jnp.stack([chip, core]).astype(jnp.int32)

    def first(self):
        names = ["w_in", "w_glu"]
        got = _run_now("gather_first", _gather_group([self.shards[n] for n in names] + [self.conv_w],
                                                     [True, True, False]))
        out = dict(zip(names, got))
        out["conv_w"] = jnp.transpose(got[2], (1, 0, 2)).reshape(4, LW)
        return out

    def gather(self, names):
        return _gather_group([self.shards[n] for n in names], [True] * len(names))

    def reduce(self, tag, grads):
        self.active.append({"tag": tag, "names": list(grads), "stage": 0, "grads": list(grads.values())})

    def run(self, call, hold=()):
        groups = [g for g in self.active if g["tag"] not in hold]
        carries = [self._exchange_of(g) for g in groups]
        carry = _combine(carries)
        outs = list(call(carry))
        own = len(outs) - len(carry.out_shapes)
        landed = outs[own:]
        for g, c in zip(groups, carries):
            self._sum_after(g, landed[:len(c.out_shapes)])
            landed = landed[len(c.out_shapes):]
        self.active = [g for g in self.active if g["stage"] < 3]
        return outs[:own]

    def _exchange_of(self, g):
        if g["stage"] == 0:
            return _swap_group(g["grads"])
        if g["stage"] == 1:
            return _exchange_group(g["bf16"])
        return _join_group(g["halves"])

    def _sum_after(self, g, landed):
        if g["stage"] == 0:
            g["f32"], g["bf16"] = _add_sibling_group(g["tag"], self.chip_core_idx, g["grads"], landed)
        elif g["stage"] == 1:
            g["halves"] = _add_chips_group(g["tag"], self.chip_core_idx, g["f32"], landed)
        else:
            self.apply(g["tag"], g["names"], [t.reshape(2 * t.shape[1], t.shape[2]) for t in landed])
        g["stage"] += 1

    def drain(self):
        while self.active:
            self.calls += 1
            self.run(lambda carry: _run_now("reduce_%d" % self.calls, carry))


INPUT_NAMES = (["x", "p"] + [n for n in
               ["g_mix", "w_in", "b_in", "lam_re", "lam_im", "log_dt", "s5_b_re", "s5_b_im", "s5_c_re", "s5_c_im", "s5_d",
                "w_glu", "b_glu", "conv_w", "conv_b", "w_r", "b_r", "w_i", "b_i", "lru_lambda", "w_a_out", "w_b_out", "w_o",
                "g_ffn", "w_ffn_gate", "w_ffn_up", "w_ffn_down", "g_ple_gate", "w_ple_gate", "b_ple_gate", "w_ple", "g_ple",
                "g_final"]])
WEIGHT_NAMES = INPUT_NAMES[2:]


def kernel(*args):
    names = INPUT_NAMES + ["loss_target"] + ["m_" + n for n in WEIGHT_NAMES] + ["v_" + n for n in WEIGHT_NAMES]
    assert len(args) == len(names)
    given = dict(zip(names, args))

    def view(name):
        a = given[name]
        return jnp.swapaxes(a, -1, -2) if name.endswith(TRANSPOSED) else a

    def unview(name, a):
        return jnp.swapaxes(a, -1, -2) if name in TRANSPOSED else a

    def local(name):
        return view(name) if name.endswith("g_final") else view(name)[0]

    xi, yi, ci = _mesh_pos()
    k0 = 2 * xi + yi
    x, p, tgt = given["x"][0], given["p"][0, 0], given["loss_target"][0]

    results = {}

    row_halves = {}

    def apply(tag, names, totals):
        totals = dict(zip(names, totals))
        row_halves.update({n: totals.pop(n) for n in names if n in ("w_in_lo", "w_in_hi")})
        if len(row_halves) == 2:
            totals["w_in"] = jnp.concatenate([row_halves.pop("w_in_lo"), row_halves.pop("w_in_hi")])
        names = list(totals)
        if not names:
            return
        new = _adamw_group(tag, [local(n) for n in names], list(totals.values()), [local("m_" + n) for n in names],
                           [local("v_" + n) for n in names])
        for kind, arrays in zip(("grad", "delta", "new_m", "new_v"), new):
            for n, arr in zip(names, arrays):
                results[kind, n] = unview(n, arr[None])

    comm = _Exchanges({n: local(n).astype(BF) for n, _ in SHARDED}, local("conv_w"), k0, ci, apply)
    w = {n: local(n) for n in WEIGHT_NAMES if n != "conv_w" and n not in dict(SHARDED)}
    gx, sums, blocks = _local_step(x, p, tgt, w, comm)

    sum_names, block_names = list(sums), list(blocks)
    red = _allreduce_small([sums[n] for n in sum_names] + [blocks[n] for n in block_names],
                           [F32] * len(sum_names) + [BF] * len(block_names))
    sums = dict(zip(sum_names, red[:len(sum_names)]))
    blocks = dict(zip(block_names, red[len(sum_names):]))
    loss = jnp.sum(sums[LOSS_ROW[0]][LOSS_ROW[1]])
    direct_g = _replicated_grads(w, sums, blocks)
    conv_rows = sums[CONV_W_ROWS[0]][CONV_W_ROWS[1]:CONV_W_ROWS[1] + 4]
    direct_g["conv_w"] = lax.dynamic_slice(conv_rows, (0, k0 * CONV_SHARD[1]), CONV_SHARD)
    as_row = lambda a: a.reshape(1, -1)
    row_names = list(ACC_ROWS)
    row_of = [(as_row(given[n]), as_row(given["m_" + n]), as_row(given["v_" + n]),
               sum_names.index(ACC_ROWS[n][0]), ACC_ROWS[n][1]) for n in row_names]
    direct_names = list(direct_g)
    direct = [(view(n), view("m_" + n), view("v_" + n), direct_g[n].reshape(view(n).shape)) for n in direct_names]
    done = _adamw_replicated([sums[n] for n in sum_names], row_of, direct)
    for n, four in zip(row_names + direct_names, done):
        for kind, arr in zip(("grad", "delta", "new_m", "new_v"), four):
            results[kind, n] = unview(n, arr).reshape(given[n].shape)

    out = [loss, gx[None]]
    for kind in ("grad", "delta", "new_m", "new_v"):
        out += [results[kind, n] for n in WEIGHT_NAMES]
    return tuple(out)
```

```python
import functools
import math

import jax
import jax.numpy as jnp
from jax import lax
from jax.experimental import pallas as pl
from jax.experimental.pallas import tpu as pltpu

F32 = jnp.float32
BF = jnp.bfloat16

D = 1024
S5W = 512
NG, NS, NP = 32, 64, 16
GN = NG * NS
LW = 1024
NH, HD = 16, 64
LRU_C = 8.0
FH = 2816
NCHIP = 4
FC = FH // NCHIP
PLE = 256
INC = S5W + LW + 2 * D
EPS = 1e-6
ADAM_LR, ADAM_B1, ADAM_B2, ADAM_EPS, ADAM_WD, ADAM_STEP = 0.001, 0.9, 0.999, 1e-08, 0.01, 10

TM = 256
TK = 1024
LC = 512
SUB = 8
VMEM_MB = 1024 * 1024
MESH = pl.DeviceIdType.MESH
ANY = pl.BlockSpec(memory_space=pl.ANY)


def _mm(a, b):
    return jnp.dot(a.astype(BF), b.astype(BF), preferred_element_type=F32)


def _mm_nt(a, b):
    return lax.dot_general(a.astype(BF), b.astype(BF), (((1,), (1,)), ((), ())), preferred_element_type=F32)


def _mm_tn(a, b):
    return lax.dot_general(a.astype(BF), b.astype(BF), (((0,), (0,)), ((), ())), preferred_element_type=F32)


def _blockdiag_mm(x, blocks_ref):
    n, rows, _ = blocks_ref.shape
    return jnp.concatenate([jnp.dot(x[:, j * rows:(j + 1) * rows], blocks_ref[j], preferred_element_type=F32)
                            for j in range(n)], axis=1)


def _blockdiag_mm_t(x, blocks_ref):
    n, _, wide = blocks_ref.shape
    return jnp.concatenate([lax.dot_general(x[:, j * wide:(j + 1) * wide], blocks_ref[j], (((1,), (1,)), ((), ())),
                                            preferred_element_type=F32) for j in range(n)], axis=1)


def _rms(x):
    r = lax.rsqrt(jnp.mean(x * x, axis=-1, keepdims=True) + EPS)
    return x * r, r


def _rms_bwd(dy, xh, r, g):
    dxh = dy * g
    return r * (dxh - xh * jnp.mean(dxh * xh, axis=-1, keepdims=True))


def _colsum(x):
    return jnp.sum(x, axis=0, keepdims=True)


def _sig(x):
    return jax.nn.sigmoid(x)


def _gelu_grad(x):
    c = math.sqrt(2.0 / math.pi)
    t = jnp.tanh(c * (x + 0.044715 * x * x * x))
    return 0.5 * (1.0 + t) + 0.5 * x * (1.0 - t * t) * c * (1.0 + 3.0 * 0.044715 * x * x)


def _neg_expm1(x):
    series = -x * (1.0 + x * (0.5 + x * (1.0 / 6.0 + x * (1.0 / 24.0))))
    return jnp.where(x > -0.03, series, 1.0 - jnp.exp(x))


def _tok(width):
    return pl.BlockSpec((TM, width), lambda i: (i, 0))


def _tok_rev(width, nt):
    return pl.BlockSpec((TM, width), lambda i: (nt - 1 - i, 0))


def _full(shape):
    return pl.BlockSpec(shape, lambda i: (0,) * len(shape))


def _params(vmem_mb, **kw):
    return pltpu.CompilerParams(dimension_semantics=("arbitrary",), vmem_limit_bytes=vmem_mb * VMEM_MB, **kw)


def _sds(shape, dtype=F32):
    return jax.ShapeDtypeStruct(shape, dtype)


def _far(shape, dtype=F32):
    return pltpu.HBM(shape, dtype)


class _Carried:
    def __init__(self, operands, out_shapes, sems, start, finish, aliases=None):
        self.operands, self.out_shapes, self.sems = list(operands), list(out_shapes), list(sems)
        self.start, self.finish, self.aliases = start, finish, dict(aliases or {})


def _in_hbm(arrays):
    return [pltpu.with_memory_space_constraint(a, pltpu.HBM) for a in arrays]


def _pallas_call(body, carry=None, **kw):
    if carry is None:
        return pl.pallas_call(body, **kw)

    def at_step(corner):
        hit = [pl.program_id(d) == (size - 1 if corner else 0) for d, size in enumerate(kw["grid"])]
        return functools.reduce(jnp.logical_and, hit)

    name, grid, compiler_params = kw["name"], kw["grid"], kw["compiler_params"]
    in_specs, out_specs, out_shape = list(kw["in_specs"]), list(kw["out_specs"]), list(kw["out_shape"])
    scratch_shapes = list(kw.get("scratch_shapes", ()))
    n_in, n_out, n_scr = len(in_specs), len(out_specs), len(scratch_shapes)
    c_in, c_out = len(carry.operands), len(carry.out_shapes)

    def full_body(*refs):
        ins, refs = refs[:n_in], refs[n_in:]
        c_ins, refs = refs[:c_in], refs[c_in:]
        outs, refs = refs[:n_out], refs[n_out:]
        c_outs, refs = refs[:c_out], refs[c_out:]
        scratch, c_sems = refs[:n_scr], refs[n_scr:]

        @pl.when(at_step(0))
        def _():
            carry.start(c_ins, c_outs, c_sems)

        body(*ins, *outs, *scratch)

        @pl.when(at_step(1))
        def _():
            carry.finish(c_ins, c_outs, c_sems)

    call = pl.pallas_call(
        full_body, name=name, grid=grid, in_specs=in_specs + [ANY] * c_in, out_specs=out_specs + [ANY] * c_out,
        out_shape=out_shape + list(carry.out_shapes), scratch_shapes=scratch_shapes + list(carry.sems),
        input_output_aliases={n_in + i: n_out + o for i, o in carry.aliases.items()},
        compiler_params=compiler_params)
    return lambda *operands: call(*operands, *_in_hbm(carry.operands))


def _resident(pairs, sems):
    first = pl.program_id(0) == 0
    copies = [pltpu.make_async_copy(src, dst, sems.at[j]) for j, (src, dst) in enumerate(pairs)]

    @pl.when(first)
    def _():
        for cp in copies:
            cp.start()

    def wait(j):
        @pl.when(first)
        def _():
            copies[j].wait()

    return wait


def _resident_now(pairs, sems):
    @pl.when(pl.program_id(0) == 0)
    def _():
        copies = [pltpu.make_async_copy(src, dst, sems.at[j]) for j, (src, dst) in enumerate(pairs)]
        for cp in copies:
            cp.start()
        for cp in copies:
            cp.wait()


def _row_iota(width):
    return lax.broadcasted_iota(jnp.int32, (SUB, width), 0)


def _bcast_row(x, row):
    return jnp.broadcast_to(x[row:row + 1, :], x.shape)


def _slab(k):
    return pl.ds(pl.multiple_of(k * SUB, SUB), SUB)


QC = INC // NCHIP
Z_PARTS = ((0, S5W), (S5W, S5W + LW), (S5W + LW, INC))


def _inproj_fwd(x, g_mix, w_in, b_in, carry=None):
    L = x.shape[0]

    def body(x_ref, g_ref, w_hbm, b_ref, h_ref, ua_ref, ub_ref, gp_ref, w_vm, w_sems):
        _resident_now([(w_hbm.at[k], w_vm.at[k]) for k in range(NCHIP)], w_sems)
        xh, _ = _rms(x_ref[...])
        h = (xh * g_ref[...]).astype(BF)
        h_ref[...] = h
        for k in range(NCHIP):
            lo, hi = k * QC, (k + 1) * QC
            z = jnp.dot(h, w_vm[k], preferred_element_type=F32) + b_ref[:, lo:hi]
            for ref, (a, b) in zip((ua_ref, ub_ref, gp_ref), Z_PARTS):
                s, e = max(lo, a), min(hi, b)
                if s < e:
                    ref[:, s - a:e - a] = z[:, s - lo:e - lo]

    return _pallas_call(
        body, carry, name="inproj_fwd", grid=(L // TM,),
        in_specs=[_tok(D), _full((1, D)), ANY, _full((1, INC))],
        out_specs=[_tok(D), _tok(S5W), _tok(LW), _tok(2 * D)],
        out_shape=[_far((L, D), BF), _far((L, S5W)), _far((L, LW)), _sds((L, 2 * D))],
        scratch_shapes=[pltpu.VMEM((NCHIP, D, QC), BF), pltpu.SemaphoreType.DMA((NCHIP,))],
        compiler_params=_params(40),
    )(x, g_mix, *_in_hbm([w_in]), b_in)


def _inproj_bwd(x, dx1, dua, dub, dgp, g_mix, w_in, carry=None):
    L = x.shape[0]

    def body(x_ref, dx1_ref, dua_ref, dub_ref, dgp_ref, g_ref, w_hbm, gx_ref, dz_ref, dg_ref, db_ref, w_vm, w_sems):
        _resident_now([(w_hbm.at[k], w_vm.at[k]) for k in range(NCHIP)], w_sems)

        @pl.when(pl.program_id(0) == 0)
        def _():
            dg_ref[...] = jnp.zeros_like(dg_ref)
            db_ref[...] = jnp.zeros_like(db_ref)

        for src, (a, b) in zip((dua_ref, dub_ref, dgp_ref), Z_PARTS):
            d = src[...]
            dz_ref[:, a:b] = d.astype(BF)
            db_ref[0:1, a:b] += _colsum(d)
        dh = jnp.zeros((TM, D), F32)
        for k in range(NCHIP):
            dh = dh + lax.dot_general(dz_ref[:, k * QC:(k + 1) * QC], w_vm[k], (((1,), (1,)), ((), ())),
                                      preferred_element_type=F32)
        xh, r = _rms(x_ref[...])
        dg_ref[0:1, :] += _colsum(dh * xh)
        gx_ref[...] = dx1_ref[...] + _rms_bwd(dh, xh, r, g_ref[...])

    return _pallas_call(
        body, carry, name="inproj_bwd", grid=(L // TM,),
        in_specs=[_tok(D), _tok(D), _tok(S5W), _tok(LW), _tok(2 * D), _full((1, D)), ANY],
        out_specs=[_tok(D), _tok(INC), _full((SUB, D)), _full((SUB, INC))],
        out_shape=[_sds((L, D)), _sds((L, INC), BF), _sds((SUB, D)), _sds((SUB, INC))],
        scratch_shapes=[pltpu.VMEM((NCHIP, D, QC), BF), pltpu.SemaphoreType.DMA((NCHIP,))],
        compiler_params=_params(40),
    )(x, dx1, dua, dub, dgp, g_mix, *_in_hbm([w_in]))


def _cscan(xr_ref, xi_ref, con_ref, cr_ref, ci_ref, reverse):
    n_slab = xr_ref.shape[0] // SUB
    width = xr_ref.shape[1]
    for lc in range(width // LC):
        cols = slice(lc * LC, (lc + 1) * LC)
        con = [con_ref[SUB * j:SUB * (j + 1), cols] for j in range(8)]

        def step(k, carry, cols=cols, con=con):
            cr, ci = carry
            rows = _slab(n_slab - 1 - k if reverse else k)
            xr, xi = xr_ref[rows, cols], xi_ref[rows, cols]
            for j, sh in enumerate((1, 2, 4)):
                mr, mi = con[2 * j], con[2 * j + 1]
                pr = pltpu.roll(xr, SUB - sh if reverse else sh, 0)
                pi = pltpu.roll(xi, SUB - sh if reverse else sh, 0)
                xr, xi = xr + mr * pr - mi * pi, xi + mr * pi + mi * pr
            xr, xi = xr + con[6] * cr - con[7] * ci, xi + con[6] * ci + con[7] * cr
            xr_ref[rows, cols] = xr
            xi_ref[rows, cols] = xi
            row = 0 if reverse else SUB - 1
            return _bcast_row(xr, row), _bcast_row(xi, row)

        cr, ci = lax.fori_loop(0, n_slab, step, (cr_ref[:, cols], ci_ref[:, cols]))
        cr_ref[:, cols] = cr
        ci_ref[:, cols] = ci


def _s5_fwd(ua, bbr, bbi, ccr, cci, dsk, con, w_glu, b_glu, carry=None):
    L = ua.shape[0]

    def body(ua_ref, bbr_hbm, bbi_hbm, ccr_hbm, cci_hbm, dsk_ref, con_ref, wg_ref, bg_ref,
             sr_ref, si_ref, y_ref, zg_ref, ya_ref, bbr_vm, bbi_vm, ccr_vm, cci_vm, cr_ref, ci_ref, w_sems):
        landed = _resident([(bbr_hbm, bbr_vm), (bbi_hbm, bbi_vm), (ccr_hbm, ccr_vm), (cci_hbm, cci_vm)], w_sems)

        @pl.when(pl.program_id(0) == 0)
        def _():
            cr_ref[...] = jnp.zeros_like(cr_ref)
            ci_ref[...] = jnp.zeros_like(ci_ref)

        u = ua_ref[...]
        ub = u.astype(BF)
        landed(0)
        sr_ref[...] = _blockdiag_mm(ub, bbr_vm)
        landed(1)
        si_ref[...] = _blockdiag_mm(ub, bbi_vm)
        _cscan(sr_ref, si_ref, con_ref, cr_ref, ci_ref, reverse=False)
        landed(2)
        landed(3)
        y = (_blockdiag_mm_t(sr_ref[...].astype(BF), ccr_vm) - _blockdiag_mm_t(si_ref[...].astype(BF), cci_vm)
             + dsk_ref[...] * u)
        y_ref[...] = y
        zg = jax.nn.gelu(y)
        zg_ref[...] = zg.astype(BF)
        q = _mm(zg, wg_ref[...]) + bg_ref[...]
        ya_ref[...] = (zg * _sig(q)).astype(BF)

    return _pallas_call(
        body, carry, name="s5_fwd", grid=(L // TM,),
        in_specs=[_tok(S5W), ANY, ANY, ANY, ANY, _full((1, S5W)), _full((8 * SUB, GN)),
                  _full((S5W, S5W)), _full((1, S5W))],
        out_specs=[_tok(GN), _tok(GN), _tok(S5W), _tok(S5W), _tok(S5W)],
        out_shape=[_sds((L, GN)), _sds((L, GN)), _far((L, S5W)), _far((L, S5W), BF), _far((L, S5W), BF)],
        scratch_shapes=[pltpu.VMEM((S5W // 128, 128, GN // (S5W // 128)), BF)] * 4 + [
                        pltpu.VMEM((SUB, GN), F32), pltpu.VMEM((SUB, GN), F32),
                        pltpu.SemaphoreType.DMA((4,))],
        compiler_params=_params(44),
    )(ua, bbr, bbi, ccr, cci, dsk, con, w_glu, b_glu)


def _s5_bwd(dya, y, ua, sr, si, bbr, bbi, ccr, cci, dsk, con_rev, w_glu, b_glu, carry=None):
    L = ua.shape[0]
    nt = L // TM
    spt = TM // SUB
    n_slab = spt

    def halo_map(i):
        return (jnp.maximum((nt - 1 - i) * spt - 1, 0), 0)

    def body(dya_ref, y_ref, ua_ref, sr_ref, si_ref, hr_ref, hi_ref, bbr_hbm, bbi_hbm, ccr_hbm, cci_hbm,
             dsk_ref, con_ref, wg_ref, bg_ref,
             dua_ref, dq_ref, dy_ref, lr_ref, li_ref, da_ref, dsm_ref,
             bbr_vm, bbi_vm, ccr_vm, cci_vm, cr_ref, ci_ref, w_sems):
        i = pl.program_id(0)
        landed = _resident([(ccr_hbm, ccr_vm), (cci_hbm, cci_vm), (bbr_hbm, bbr_vm), (bbi_hbm, bbi_vm)], w_sems)

        @pl.when(i == 0)
        def _():
            cr_ref[...] = jnp.zeros_like(cr_ref)
            ci_ref[...] = jnp.zeros_like(ci_ref)
            da_ref[...] = jnp.zeros_like(da_ref)
            dsm_ref[...] = jnp.zeros_like(dsm_ref)

        u = ua_ref[...]
        yv = y_ref[...]
        dya = dya_ref[...]
        zg = jax.nn.gelu(yv)
        sg = _sig(_mm(zg, wg_ref[...]) + bg_ref[...])
        dq = dya * zg * sg * (1.0 - sg)
        dq_ref[...] = dq.astype(BF)
        dzg = dya * sg + _mm_nt(dq, wg_ref[...])
        dy = dzg * _gelu_grad(yv)
        dyb = dy.astype(BF)
        dy_ref[...] = dyb
        dsm_ref[0:1, :] += _colsum(dy * u)
        dsm_ref[1:2, :] += _colsum(dq)
        landed(0)
        lr_ref[...] = _blockdiag_mm(dyb, ccr_vm)
        landed(1)
        li_ref[...] = -_blockdiag_mm(dyb, cci_vm)
        _cscan(lr_ref, li_ref, con_ref, cr_ref, ci_ref, reverse=True)

        first_tile = (i == nt - 1)
        row = _row_iota(LC)
        for lc in range(GN // LC):
            cols = slice(lc * LC, (lc + 1) * LC)
            h_r = jnp.where(first_tile, 0.0, hr_ref[:, cols])
            h_i = jnp.where(first_tile, 0.0, hi_ref[:, cols])

            def step(k, acc, cols=cols, h_r=h_r, h_i=h_i):
                ar, ai = acc
                rows = _slab(k)
                prev = _slab(jnp.maximum(k - 1, 0))
                pr = jnp.where(k == 0, h_r, sr_ref[prev, cols])
                pi = jnp.where(k == 0, h_i, si_ref[prev, cols])
                spr = pltpu.roll(jnp.where(row == SUB - 1, pr, sr_ref[rows, cols]), 1, 0)
                spi = pltpu.roll(jnp.where(row == SUB - 1, pi, si_ref[rows, cols]), 1, 0)
                lr, li = lr_ref[rows, cols], li_ref[rows, cols]
                return ar + lr * spr + li * spi, ai + li * spr - lr * spi

            zero = jnp.zeros((SUB, LC), F32)
            ar, ai = lax.fori_loop(0, n_slab, step, (zero, zero))
            da_ref[0:1, cols] += _colsum(ar)
            da_ref[1:2, cols] += _colsum(ai)

        landed(2)
        landed(3)
        dua_ref[...] = (dy * dsk_ref[...] + _blockdiag_mm_t(lr_ref[...].astype(BF), bbr_vm)
                        + _blockdiag_mm_t(li_ref[...].astype(BF), bbi_vm))

    return _pallas_call(
        body, carry, name="s5_bwd", grid=(nt,),
        in_specs=[_tok_rev(S5W, nt), _tok_rev(S5W, nt), _tok_rev(S5W, nt), _tok_rev(GN, nt), _tok_rev(GN, nt),
                  pl.BlockSpec((SUB, GN), halo_map), pl.BlockSpec((SUB, GN), halo_map),
                  ANY, ANY, ANY, ANY, _full((1, S5W)), _full((8 * SUB, GN)), _full((S5W, S5W)), _full((1, S5W))],
        out_specs=[_tok_rev(S5W, nt), _tok_rev(S5W, nt), _tok_rev(S5W, nt), _tok_rev(GN, nt), _tok_rev(GN, nt),
                   _full((SUB, GN)), _full((SUB, S5W))],
        out_shape=[_sds((L, S5W)), _sds((L, S5W), BF), _sds((L, S5W), BF), _sds((L, GN)), _sds((L, GN)),
                   _sds((SUB, GN)), _sds((SUB, S5W))],
        scratch_shapes=[pltpu.VMEM((S5W // 128, 128, GN // (S5W // 128)), BF)] * 4 + [
                        pltpu.VMEM((SUB, GN), F32), pltpu.VMEM((SUB, GN), F32),
                        pltpu.SemaphoreType.DMA((4,))],
        compiler_params=_params(52),
    )(dya, y, ua, sr, si, sr, si, bbr, bbi, ccr, cci, dsk, con_rev, w_glu, b_glu)


def _lru_gate_terms(rg, sp):
    log_a = -LRU_C * rg * sp
    a = jnp.exp(log_a)
    mult = jnp.sqrt(_neg_expm1(2.0 * log_a))
    return a, mult


def _lru_fwd(ub, conv_w, conv_b, wr, wi, b_r, b_i, sp, carry=None):
    L = ub.shape[0]
    n_slab = TM // SUB

    def body(ub_ref, cw_ref, cb_ref, wr_ref, wi_ref, br_ref, bi_ref, sp_ref,
             xc_ref, rg_ref, ig_ref, h_ref, hp_ref, a_ref, halo_ref, carry_ref):
        @pl.when(pl.program_id(0) == 0)
        def _():
            halo_ref[...] = jnp.zeros_like(halo_ref)
            carry_ref[...] = jnp.zeros_like(carry_ref)

        row = _row_iota(LW)
        taps = [cw_ref[k:k + 1, :] for k in range(4)]
        cb = cb_ref[...]

        def conv_step(k, prev):
            rows = _slab(k)
            cur = ub_ref[rows, :]
            acc = taps[3] * cur + cb
            for j in (1, 2, 3):
                acc = acc + taps[3 - j] * pltpu.roll(jnp.where(row >= SUB - j, prev, cur), j, 0)
            xc_ref[rows, :] = acc
            return cur

        halo_ref[...] = lax.fori_loop(0, n_slab, conv_step, halo_ref[...])

        xc = xc_ref[...]
        xcb = xc.astype(BF)
        rg = _sig(_blockdiag_mm(xcb, wr_ref) + br_ref[...])
        ig = _sig(_blockdiag_mm(xcb, wi_ref) + bi_ref[...])
        rg_ref[...] = rg
        ig_ref[...] = ig
        a, mult = _lru_gate_terms(rg, sp_ref[...])
        a_ref[...] = a
        h_ref[...] = mult * ig * xc

        rowc = _row_iota(LC)
        for lc in range(LW // LC):
            cols = slice(lc * LC, (lc + 1) * LC)

            def step(k, c, cols=cols):
                rows = _slab(k)
                av, b = a_ref[rows, cols], h_ref[rows, cols]
                for sh in (1, 2, 4):
                    keep = rowc >= sh
                    b = b + av * jnp.where(keep, pltpu.roll(b, sh, 0), 0.0)
                    av = av * jnp.where(keep, pltpu.roll(av, sh, 0), 1.0)
                h = b + av * c
                h_ref[rows, cols] = h
                hp_ref[rows, cols] = jnp.where(rowc == 0, c, pltpu.roll(h, 1, 0))
                return _bcast_row(h, SUB - 1)

            carry_ref[:, cols] = lax.fori_loop(0, n_slab, step, carry_ref[:, cols])

    return _pallas_call(
        body, carry, name="lru_fwd", grid=(L // TM,),
        in_specs=[_tok(LW), _full((4, LW)), _full((1, LW)), _full((LW // 128, 128, 128)), _full((LW // 128, 128, 128)),
                  _full((1, LW)), _full((1, LW)), _full((1, LW))],
        out_specs=[_tok(LW)] * 5,
        out_shape=[_far((L, LW))] * 5,
        scratch_shapes=[pltpu.VMEM((TM, LW), F32), pltpu.VMEM((SUB, LW), F32), pltpu.VMEM((SUB, LW), F32)],
        compiler_params=_params(40),
    )(ub, conv_w, conv_b, wr, wi, b_r, b_i, sp)


def _lru_bwd(dyb, xc, rg, ig, hp, ub, conv_w, wr, wi, sp, dsp, carry=None):
    L = ub.shape[0]
    nt = L // TM
    spt = TM // SUB
    n_slab = spt

    def halo_map(i):
        return (jnp.maximum((nt - 1 - i) * spt - 1, 0), 0)

    def body(dh_ref, xc_ref, rg_ref, ig_ref, hp_ref, ub_ref, uh_ref, cw_ref, wr_ref, wi_ref, sp_ref, dsp_ref,
             dub_ref, dpr_ref, dpi_ref, acc_ref, a_ref, lam_ref, dxc_ref, carry_ref, next_ref):
        i = pl.program_id(0)

        @pl.when(i == 0)
        def _():
            carry_ref[...] = jnp.zeros_like(carry_ref)
            next_ref[...] = jnp.zeros_like(next_ref)
            acc_ref[...] = jnp.zeros_like(acc_ref)

        sp = sp_ref[...]
        rg, ig, xc = rg_ref[...], ig_ref[...], xc_ref[...]
        a, mult = _lru_gate_terms(rg, sp)
        a_ref[...] = a

        rowc = _row_iota(LC)
        for lc in range(LW // LC):
            cols = slice(lc * LC, (lc + 1) * LC)

            def step(k, c, cols=cols):
                rows = _slab(n_slab - 1 - k)
                av, dh = a_ref[rows, cols], dh_ref[rows, cols]
                b = av * dh
                for sh in (1, 2, 4):
                    keep = rowc < SUB - sh
                    b = b + av * jnp.where(keep, pltpu.roll(b, SUB - sh, 0), 0.0)
                    av = av * jnp.where(keep, pltpu.roll(av, SUB - sh, 0), 1.0)
                mu = b + av * c
                lam_ref[rows, cols] = dh + jnp.where(rowc == SUB - 1, c, pltpu.roll(mu, SUB - 1, 0))
                return _bcast_row(mu, 0)

            carry_ref[:, cols] = lax.fori_loop(0, n_slab, step, carry_ref[:, cols])

        lam = lam_ref[...]
        d_a = lam * hp_ref[...]
        d_mult = lam * ig * xc
        d_ig = lam * mult * xc
        dxc = lam * mult * ig
        d_log_a = d_a * a - d_mult * a * a / mult
        d_rg = (-LRU_C) * sp * d_log_a
        acc_ref[0:1, :] += _colsum((-LRU_C) * rg * d_log_a) * dsp_ref[...]
        dpr = d_rg * rg * (1.0 - rg)
        dpi = d_ig * ig * (1.0 - ig)
        acc_ref[1:2, :] += _colsum(dpr)
        acc_ref[2:3, :] += _colsum(dpi)
        dprb, dpib = dpr.astype(BF), dpi.astype(BF)
        dpr_ref[...] = dprb
        dpi_ref[...] = dpib
        dxc = dxc + _blockdiag_mm_t(dprb, wr_ref) + _blockdiag_mm_t(dpib, wi_ref)
        dxc_ref[...] = dxc
        acc_ref[3:4, :] += _colsum(dxc)

        row = _row_iota(LW)
        taps = [cw_ref[k:k + 1, :] for k in range(4)]
        u_halo = jnp.where(i == nt - 1, 0.0, uh_ref[...])
        nxt_tile = next_ref[...]

        def conv_step(k, accs):
            rows = _slab(k)
            cur = dxc_ref[rows, :]
            nxt = jnp.where(k == n_slab - 1, nxt_tile, dxc_ref[_slab(jnp.minimum(k + 1, n_slab - 1)), :])
            ucur = ub_ref[rows, :]
            uprev = jnp.where(k == 0, u_halo, ub_ref[_slab(jnp.maximum(k - 1, 0)), :])
            du = taps[3] * cur
            new = [accs[3] + cur * ucur]
            for j in (1, 2, 3):
                du = du + taps[3 - j] * pltpu.roll(jnp.where(row < j, nxt, cur), SUB - j, 0)
                new.append(accs[3 - j] + cur * pltpu.roll(jnp.where(row >= SUB - j, uprev, ucur), j, 0))
            dub_ref[rows, :] = du
            return tuple(new[::-1])

        zero = jnp.zeros((SUB, LW), F32)
        accs = lax.fori_loop(0, n_slab, conv_step, (zero, zero, zero, zero))
        for k in range(4):
            acc_ref[4 + k:5 + k, :] += _colsum(accs[k])
        next_ref[...] = dxc_ref[0:SUB, :]

    return _pallas_call(
        body, carry, name="lru_bwd", grid=(nt,),
        in_specs=[_tok_rev(LW, nt)] * 6 + [pl.BlockSpec((SUB, LW), halo_map), _full((4, LW)),
                                           _full((LW // 128, 128, 128)), _full((LW // 128, 128, 128)), _full((1, LW)), _full((1, LW))],
        out_specs=[_tok_rev(LW, nt), _tok_rev(LW, nt), _tok_rev(LW, nt), _full((SUB, LW))],
        out_shape=[_sds((L, LW)), _far((L, LW), BF), _far((L, LW), BF), _sds((SUB, LW))],
        scratch_shapes=[pltpu.VMEM((TM, LW), F32), pltpu.VMEM((TM, LW), F32), pltpu.VMEM((TM, LW), F32),
                        pltpu.VMEM((SUB, LW), F32), pltpu.VMEM((SUB, LW), F32)],
        compiler_params=_params(48),
    )(dyb, xc, rg, ig, hp, ub, ub, conv_w, wr, wi, sp, dsp)


AC = D // NCHIP


def _merge_fwd(x, ya, yb, gp, w_a, w_b, w_o, carry=None):
    L = x.shape[0]

    def body(x_ref, ya_ref, yb_ref, gp_ref, wa_ref, wb_ref, wo_ref, x1_ref, pa_ref, pb_ref, mg_ref):
        ya = ya_ref[...]
        for k in range(NCHIP):
            pa_ref[:, k * AC:(k + 1) * AC] = jnp.dot(ya, wa_ref[k], preferred_element_type=F32)
        pb = _mm(yb_ref[...], wb_ref[...])
        pb_ref[...] = pb
        gp = gp_ref[...]
        merged = (_sig(gp[:, :D]) * pa_ref[...] + _sig(gp[:, D:]) * pb).astype(BF)
        mg_ref[...] = merged
        x1_ref[...] = x_ref[...] + jnp.dot(merged, wo_ref[...], preferred_element_type=F32)

    return _pallas_call(
        body, carry, name="merge_fwd", grid=(L // TM,),
        in_specs=[_tok(D), _tok(S5W), _tok(LW), _tok(2 * D), _full((NCHIP, S5W, AC)), _full((LW, D)), _full((D, D))],
        out_specs=[_tok(D), _tok(D), _tok(D), _tok(D)],
        out_shape=[_sds((L, D)), _sds((L, D)), _sds((L, D)), _far((L, D), BF)],
        compiler_params=_params(40),
    )(x, ya, yb, gp, w_a, w_b, w_o)


def _merge_bwd(dx1, gp, pa, pb, w_a, w_b, w_o, carry=None):
    L = dx1.shape[0]

    def body(dx1_ref, gp_ref, pa_ref, pb_ref, wa_ref, wb_ref, wo_ref, dya_ref, dyb_ref, dgp_ref, dpa_ref, dpb_ref):
        dm = _mm_nt(dx1_ref[...], wo_ref[...])
        gp = gp_ref[...]
        sa, sb = _sig(gp[:, :D]), _sig(gp[:, D:])
        dpa = (dm * sa).astype(BF)
        dpb = (dm * sb).astype(BF)
        dpa_ref[...] = dpa
        dpb_ref[...] = dpb
        dgp_ref[:, :D] = dm * pa_ref[...] * sa * (1.0 - sa)
        dgp_ref[:, D:] = dm * pb_ref[...] * sb * (1.0 - sb)
        dya = jnp.zeros((TM, S5W), F32)
        for k in range(NCHIP):
            dya = dya + _mm_nt(dpa[:, k * AC:(k + 1) * AC], wa_ref[k])
        dya_ref[...] = dya
        dyb_ref[...] = _mm_nt(dpb, wb_ref[...])

    return _pallas_call(
        body, carry, name="merge_bwd", grid=(L // TM,),
        in_specs=[_tok(D), _tok(2 * D), _tok(D), _tok(D), _full((NCHIP, S5W, AC)), _full((LW, D)), _full((D, D))],
        out_specs=[_tok(S5W), _tok(LW), _tok(2 * D), _tok(D), _tok(D)],
        out_shape=[_far((L, S5W)), _far((L, LW)), _sds((L, 2 * D)), _far((L, D), BF), _far((L, D), BF)],
        compiler_params=_params(40),
    )(dx1, gp, pa, pb, w_a, w_b, w_o)


def _chunk_tok(width):
    return pl.BlockSpec((NCHIP, TM, width), lambda i: (0, i, 0))


def _ffn_fwd(x1, g_ffn, wg, wu, wd, carry=None):
    L = x1.shape[0]

    def body(x_ref, g_ref, wg_hbm, wu_hbm, wd_hbm, x2_ref, h2_ref, gg_ref, uu_ref, wg_vm, wu_vm, wd_vm, w_sems):
        _resident_now([(src.at[c], dst.at[c]) for c in range(NCHIP)
                       for src, dst in ((wg_hbm, wg_vm), (wu_hbm, wu_vm), (wd_hbm, wd_vm))], w_sems)
        x = x_ref[...]
        xh, _ = _rms(x)
        h2 = (xh * g_ref[...]).astype(BF)
        h2_ref[...] = h2
        out = x
        for c in range(NCHIP):
            gg = lax.dot_general(h2, wg_vm[c], (((1,), (1,)), ((), ())), preferred_element_type=F32)
            uu = lax.dot_general(h2, wu_vm[c], (((1,), (1,)), ((), ())), preferred_element_type=F32)
            gg_ref[c] = gg.astype(BF)
            uu_ref[c] = uu.astype(BF)
            act = (gg * _sig(gg) * uu).astype(BF)
            out = out + jnp.dot(act, wd_vm[c], preferred_element_type=F32)
        x2_ref[...] = out

    return _pallas_call(
        body, carry, name="ffn_fwd", grid=(L // TM,),
        in_specs=[_tok(D), _full((1, D)), ANY, ANY, ANY],
        out_specs=[_tok(D), _tok(D), _chunk_tok(FC), _chunk_tok(FC)],
        out_shape=[_sds((L, D)), _sds((L, D), BF), _sds((NCHIP, L, FC), BF), _sds((NCHIP, L, FC), BF)],
        scratch_shapes=[pltpu.VMEM((NCHIP, FC, D), BF)] * 3 + [pltpu.SemaphoreType.DMA((3 * NCHIP,))],
        compiler_params=_params(52),
    )(x1, g_ffn, wg, wu, wd)


def _ffn_bwd(x1, dx2, gg, uu, g_ffn, wg, wu, wd, carry=None):
    L = x1.shape[0]

    def body(x_ref, dx2_ref, gg_ref, uu_ref, g_ref, wg_hbm, wu_hbm, wd_hbm,
             dx1_ref, act_ref, dgg_ref, duu_ref, dg_ref, wg_vm, wu_vm, wd_vm, w_sems):
        _resident_now([(src.at[c], dst.at[c]) for c in range(NCHIP)
                       for src, dst in ((wg_hbm, wg_vm), (wu_hbm, wu_vm), (wd_hbm, wd_vm))], w_sems)

        @pl.when(pl.program_id(0) == 0)
        def _():
            dg_ref[...] = jnp.zeros_like(dg_ref)

        dx2 = dx2_ref[...]
        dx2b = dx2.astype(BF)
        dh2 = jnp.zeros((TM, D), F32)
        for c in range(NCHIP):
            g = gg_ref[c].astype(F32)
            u = uu_ref[c].astype(F32)
            s = _sig(g)
            silu = g * s
            act_ref[c] = (silu * u).astype(BF)
            dact = lax.dot_general(dx2b, wd_vm[c], (((1,), (1,)), ((), ())), preferred_element_type=F32)
            dg = (dact * u * s * (1.0 + g * (1.0 - s))).astype(BF)
            du = (dact * silu).astype(BF)
            dgg_ref[c] = dg
            duu_ref[c] = du
            dh2 = dh2 + jnp.dot(dg, wg_vm[c], preferred_element_type=F32)
            dh2 = dh2 + jnp.dot(du, wu_vm[c], preferred_element_type=F32)
        xh, r = _rms(x_ref[...])
        dg_ref[0:1, :] += _colsum(dh2 * xh)
        dx1_ref[...] = dx2 + _rms_bwd(dh2, xh, r, g_ref[...])

    return _pallas_call(
        body, carry, name="ffn_bwd", grid=(L // TM,),
        in_specs=[_tok(D), _tok(D), _chunk_tok(FC), _chunk_tok(FC), _full((1, D)), ANY, ANY, ANY],
        out_specs=[_tok(D), _chunk_tok(FC), _chunk_tok(FC), _chunk_tok(FC), _full((SUB, D))],
        out_shape=[_sds((L, D)), _sds((NCHIP, L, FC), BF), _sds((NCHIP, L, FC), BF), _sds((NCHIP, L, FC), BF),
                   _sds((SUB, D))],
        scratch_shapes=[pltpu.VMEM((NCHIP, FC, D), BF)] * 3 + [pltpu.SemaphoreType.DMA((3 * NCHIP,))],
        compiler_params=_params(56),
    )(x1, dx2, gg, uu, g_ffn, wg, wu, wd)


def _ple_loss(x2, p, tgt, g_pg, w_pg, b_pg, w_ple, g_ple, g_final):
    L = x2.shape[0]

    def body(x2_ref, p_ref, t_ref, gpg_ref, wpg_ref, bpg_ref, wple_ref, gple_ref, gf_ref,
             dx2_ref, n2_ref, dpre_ref, de0_ref, acc_ref):
        @pl.when(pl.program_id(0) == 0)
        def _():
            acc_ref[...] = jnp.zeros_like(acc_ref)

        x2 = x2_ref[...]
        x2h, r2 = _rms(x2)
        n2 = (x2h * gpg_ref[...]).astype(BF)
        n2_ref[...] = n2
        gate = _sig(jnp.dot(n2, wpg_ref[...], preferred_element_type=F32) + bpg_ref[...])
        pb = p_ref[...].astype(BF)
        e0 = jnp.concatenate([jnp.dot(pb, wple_ref[k], preferred_element_type=F32) for k in range(NCHIP)], axis=1)
        e0h, re = _rms(e0)
        e = e0h * gple_ref[...]
        x3 = x2 + gate * e
        x3h, r3 = _rms(x3)
        diff = x3h * gf_ref[...] - t_ref[...]
        acc_ref[4:5, :] += _colsum(diff * diff) * (0.5 / D)
        dy = diff * (1.0 / D)
        acc_ref[3:4, :] += _colsum(dy * x3h)
        dx3 = _rms_bwd(dy, x3h, r3, gf_ref[...])
        de = dx3 * gate
        acc_ref[2:3, :] += _colsum(de * e0h)
        de0_ref[...] = _rms_bwd(de, e0h, re, gple_ref[...]).astype(BF)
        dpre = dx3 * e * gate * (1.0 - gate)
        acc_ref[1:2, :] += _colsum(dpre)
        dpreb = dpre.astype(BF)
        dpre_ref[...] = dpreb
        dn2 = lax.dot_general(dpreb, wpg_ref[...], (((1,), (1,)), ((), ())), preferred_element_type=F32)
        acc_ref[0:1, :] += _colsum(dn2 * x2h)
        dx2_ref[...] = dx3 + _rms_bwd(dn2, x2h, r2, gpg_ref[...])

    return _pallas_call(
        body, name="ple_loss", grid=(L // TM,),
        in_specs=[_tok(D), _tok(PLE), _tok(D), _full((1, D)), _full((D, D)), _full((1, D)), _full((NCHIP, PLE, AC)),
                  _full((1, D)), _full((1, D))],
        out_specs=[_tok(D), _tok(D), _tok(D), _tok(D), _full((SUB, D))],
        out_shape=[_sds((L, D)), _sds((L, D), BF), _sds((L, D), BF), _sds((L, D), BF), _sds((SUB, D))],
        compiler_params=_params(40),
    )(x2, p, tgt, g_pg, w_pg, b_pg, w_ple, g_ple, g_final)


def _tn(name, a, b, col_chunk=None, a_block=None, carry=None):
    L = a.shape[-2]
    m, n = a.shape[-1], b.shape[-1]
    a_col = 0
    if a_block is not None:
        a_col, m = a_block
    tk = L if (a.ndim == 3 or b.ndim == 3 or a_block is not None) else TK
    if a.ndim == 3 or b.ndim == 3:
        nj, bn = (a if a.ndim == 3 else b).shape[0], n
        a_spec = (pl.BlockSpec((None, tk, m), lambda j, t: (j, t, 0)) if a.ndim == 3
                  else pl.BlockSpec((tk, m), lambda j, t: (t, 0)))
        b_spec = (pl.BlockSpec((None, tk, n), lambda j, t: (j, t, 0)) if b.ndim == 3
                  else pl.BlockSpec((tk, n), lambda j, t: (t, 0)))
        out_spec, out_shape = pl.BlockSpec((None, m, n), lambda j, t: (j, 0, 0)), _sds((nj, m, n))
    else:
        bn = col_chunk
        if bn is None:
            bn = next((cand for cand in (1024, 512) if n > cand and n % cand == 0), n)
        nj = n // bn
        a_spec = pl.BlockSpec((tk, m), lambda j, t: (t, a_col))
        b_spec = pl.BlockSpec((tk, bn), lambda j, t: (t, j))
        if col_chunk is None:
            out_spec, out_shape = pl.BlockSpec((m, bn), lambda j, t: (0, j)), _sds((m, n))
        else:
            out_spec, out_shape = pl.BlockSpec((None, m, bn), lambda j, t: (j, 0, 0)), _sds((nj, m, bn))

    def body(a_ref, b_ref, o_ref):
        if tk == L:
            o_ref[...] = _mm_tn(a_ref[...], b_ref[...])
        else:
            @pl.when(pl.program_id(1) == 0)
            def _():
                o_ref[...] = jnp.zeros_like(o_ref)

            o_ref[...] += _mm_tn(a_ref[...], b_ref[...])

    outs = _pallas_call(
        body, carry, name=name, grid=(nj, L // tk), in_specs=[a_spec, b_spec], out_specs=[out_spec],
        out_shape=[pltpu.HBM(out_shape.shape, out_shape.dtype)],
        compiler_params=pltpu.CompilerParams(dimension_semantics=("arbitrary", "arbitrary"),
                                             vmem_limit_bytes=(30 if tk == L else 28) * VMEM_MB),
    )(*(_in_hbm([a, b]) if tk == L else (a, b)))
    return outs[0] if carry is None else outs


LANE = 128


def _tn_blocks(name, a, bs, ga, gb, carry=None):
    L, m, n, nb = a.shape[0], a.shape[1], bs[0].shape[1], len(bs)
    per = LANE // ga
    wb = per * gb
    n_super = m // LANE

    def body(a_ref, *refs):
        b_refs, o_refs, acc_refs = refs[:nb], refs[nb:2 * nb], refs[2 * nb:]
        t = pl.program_id(0)

        @pl.when(t == 0)
        def _():
            for acc in acc_refs:
                acc[...] = jnp.zeros_like(acc)

        lhs = a_ref[...].astype(BF)
        for b_ref, acc in zip(b_refs, acc_refs):
            rhs = b_ref[...].astype(BF)
            for j in range(n_super):
                acc[j] += _mm_tn(lhs[:, j * LANE:(j + 1) * LANE], rhs[:, j * wb:(j + 1) * wb])

        @pl.when(t == L // TK - 1)
        def _():
            own = (lax.broadcasted_iota(jnp.int32, (LANE, wb), 0) // ga) == (lax.broadcasted_iota(jnp.int32, (LANE, wb), 1) // gb)
            for o_ref, acc in zip(o_refs, acc_refs):
                for j in range(n_super):
                    kept = jnp.where(own, acc[j], 0.0)
                    o_ref[:, j * wb:(j + 1) * wb] = jnp.sum(kept.reshape(per, ga, wb), axis=0)

    outs = _pallas_call(
        body, carry, name=name, grid=(L // TK,),
        in_specs=[pl.BlockSpec((TK, m), lambda t: (t, 0))] + [pl.BlockSpec((TK, n), lambda t: (t, 0))] * nb,
        out_specs=[_full((ga, n))] * nb, out_shape=[_sds((ga, n))] * nb,
        scratch_shapes=[pltpu.VMEM((n_super, LANE, wb), F32)] * nb,
        compiler_params=_params(48),
    )(*_in_hbm([a] + list(bs)))
    return list(outs)


def _s5_discretize(lam_re, lam_im, log_dt, b_re, b_im):
    dt = jnp.exp(log_dt)[:, None]
    mag = jnp.exp(lam_re * dt)
    ar = mag * jnp.cos(lam_im * dt)
    ai = mag * jnp.sin(lam_im * dt)
    den = lam_re * lam_re + lam_im * lam_im
    nr = ar - 1.0
    fr = (nr * lam_re + ai * lam_im) / den
    fi = (ai * lam_re - nr * lam_im) / den
    bbr = fr[:, None, :] * b_re - fi[:, None, :] * b_im
    bbi = fr[:, None, :] * b_im + fi[:, None, :] * b_re
    return ar, ai, bbr, bbi


def _prepare(by_rows, block_cols, ar, ai):
    n = len(by_rows)

    def body(*refs):
        srcs, (ar_ref, ai_ref), dense, (con_ref, rev_ref) = refs[:n], refs[n:n + 2], refs[n + 2:2 * n + 2], refs[2 * n + 2:]
        for src, out, c in zip(srcs, dense, block_cols):
            r = src.shape[0]
            per = LANE // r
            wide = per * c
            own = (lax.broadcasted_iota(jnp.int32, (LANE, wide), 0) // r) == (lax.broadcasted_iota(jnp.int32, (LANE, wide), 1) // c)
            for j in range(out.shape[0]):
                tiled = jnp.broadcast_to(src[:, j * wide:(j + 1) * wide][None], (per, r, wide)).reshape(LANE, wide)
                out[j] = jnp.where(own, tiled, 0.0).astype(BF)
        a_r, a_i = ar_ref[...], ai_ref[...]
        pw = [(jnp.ones_like(a_r), jnp.zeros_like(a_i))]
        for _ in range(SUB):
            pr, pi = pw[-1]
            pw.append((pr * a_r - pi * a_i, pr * a_i + pi * a_r))
        row = _row_iota(GN)
        for ref, reverse in ((con_ref, False), (rev_ref, True)):
            sign = -1.0 if reverse else 1.0
            for j, sh in enumerate((1, 2, 4)):
                keep = (row < SUB - sh) if reverse else (row >= sh)
                ref[2 * j * SUB:(2 * j + 1) * SUB, :] = jnp.where(keep, pw[sh][0], 0.0)
                ref[(2 * j + 1) * SUB:(2 * j + 2) * SUB, :] = jnp.where(keep, sign * pw[sh][1], 0.0)
            p_r, p_i = jnp.zeros((SUB, GN), F32), jnp.zeros((SUB, GN), F32)
            for i in range(SUB):
                k = SUB - i if reverse else i + 1
                p_r = jnp.where(row == i, pw[k][0], p_r)
                p_i = jnp.where(row == i, sign * pw[k][1], p_i)
            ref[6 * SUB:7 * SUB, :] = p_r
            ref[7 * SUB:8 * SUB, :] = p_i

    dense_shapes = [(b.shape[1] // (LANE // b.shape[0] * c), LANE, LANE // b.shape[0] * c)
                    for b, c in zip(by_rows, block_cols)]
    outs = _pallas_call(
        body, name="prepare", grid=(1,), in_specs=[_full(b.shape) for b in by_rows] + [_full((1, GN))] * 2,
        out_specs=[_full(s) for s in dense_shapes] + [_full((8 * SUB, GN))] * 2,
        out_shape=[_far(s, BF) for s in dense_shapes] + [_sds((8 * SUB, GN)), _far((8 * SUB, GN))],
        compiler_params=_params(48),
    )(*by_rows, ar, ai)
    return outs[:n], outs[n], outs[n + 1]


def _local_step(x, p, tgt, w, comm):
    rows_of = lambda a: a.reshape(NCHIP * a.shape[1], a.shape[2])
    quarters = lambda a: a.reshape(NCHIP, a.shape[0] // NCHIP, a.shape[1])

    def gathering(names, call):
        carry = comm.gather(names)
        outs = list(call(carry))
        own = len(outs) - len(carry.out_shapes)
        w.update(zip(names, outs[own:]))
        return outs[:own]

    w.update(comm.first())
    w_glu = rows_of(w["w_glu"])
    ar, ai, bbr, bbi = _s5_discretize(w["lam_re"], w["lam_im"], w["log_dt"], w["s5_b_re"], w["s5_b_im"])
    by_row = lambda b: jnp.transpose(b, (1, 0, 2)).reshape(b.shape[1], -1)
    (bbr_d, bbi_d, ccr_d, cci_d, wr_d, wi_d), con, con_rev = _prepare(
        [by_row(b) for b in (bbr, bbi, w["s5_c_re"], w["s5_c_im"], w["w_r"], w["w_i"])], [NS] * 4 + [HD] * 2,
        ar.reshape(1, GN), ai.reshape(1, GN))
    dsk = w["s5_d"].reshape(1, S5W)
    lam = w["lru_lambda"].reshape(1, LW)
    sp = jax.nn.softplus(-lam)
    b_r, b_i = w["b_r"].reshape(1, LW), w["b_i"].reshape(1, LW)
    row = lambda name: w[name].reshape(1, -1)

    h, ua, ub, gp = gathering(["w_a_out", "w_b_out"], lambda carry: _inproj_fwd(
        x, row("g_mix"), w["w_in"], row("b_in"), carry))
    sr, si, y, zg, ya = gathering(["w_o", "w_ffn_gate"], lambda carry: _s5_fwd(
        ua, bbr_d, bbi_d, ccr_d, cci_d, dsk, con, w_glu, row("b_glu"), carry))
    xc, rg, ig, yb, hp = gathering(["w_ffn_up"], lambda carry: _lru_fwd(
        ub, w["conv_w"], row("conv_b"), wr_d, wi_d, b_r, b_i, sp, carry))
    w_b_out, w_o = rows_of(w["w_b_out"]), rows_of(w["w_o"])
    x1, pa, pb, merged = gathering(["w_ffn_down"], lambda carry: _merge_fwd(
        x, ya, yb, gp, w["w_a_out"], w_b_out, w_o, carry))
    x2, h2, gg, uu = gathering(["w_ple_gate", "w_ple"], lambda carry: _ffn_fwd(
        x1, row("g_ffn"), w["w_ffn_gate"], w["w_ffn_up"], w["w_ffn_down"], carry))
    w_pg = rows_of(w["w_ple_gate"])
    dx2, n2, dpre, de0, acc_p = _ple_loss(x2, p, tgt, row("g_ple_gate"), w_pg, row("b_ple_gate"),
                                          w["w_ple"], row("g_ple"), row("g_final"))
    comm.reduce("ple", {"w_ple_gate": quarters(_tn("dw_ple_gate", n2, dpre)),
                        "w_ple": _tn("dw_ple", p, de0, col_chunk=AC)})
    dx1, act, dgg, duu, acc_f = comm.run(lambda carry: _ffn_bwd(
        x1, dx2, gg, uu, row("g_ffn"), w["w_ffn_gate"], w["w_ffn_up"], w["w_ffn_down"], carry))
    comm.reduce("ffn_gate", {"w_ffn_gate": _tn("dw_ffn_gate", dgg, h2)})
    comm.reduce("w_o", {"w_o": quarters(_tn("dw_o", *_in_hbm([merged, dx1])))})
    comm.reduce("ffn_up", {"w_ffn_up": comm.run(lambda carry: _tn("dw_ffn_up", duu, h2, carry=carry))[0]})
    comm.reduce("ffn_down", {"w_ffn_down": comm.run(lambda carry: _tn("dw_ffn_down", act, dx2, carry=carry),
                                                    hold=("ffn_gate", "w_o"))[0]})
    dya, dyb, dgp, dpa, dpb = comm.run(lambda carry: _merge_bwd(
        dx1, gp, pa, pb, w["w_a_out"], w_b_out, w_o, carry), hold=("ffn_gate", "ffn_up"))
    comm.reduce("merge", {"w_a_out": _tn("dw_a_out", ya, dpa, col_chunk=AC), "w_b_out": quarters(_tn("dw_b_out", yb, dpb))})
    dua, dq, dy, lr, li, acc_a, acc_s = comm.run(lambda carry: _s5_bwd(
        dya, y, ua, sr, si, bbr_d, bbi_d, ccr_d, cci_d, dsk, con_rev, w_glu, row("b_glu"), carry), hold=("ffn_down",))
    dub, dpr, dpi, acc_l = comm.run(lambda carry: _lru_bwd(
        dyb, xc, rg, ig, hp, ub, w["conv_w"], wr_d, wi_d, sp, -_sig(-lam), carry))
    gx, dz, acc_g, acc_b = _inproj_bwd(x, dx1, dua, dub, dgp, row("g_mix"), w["w_in"])
    half = (D // 2,)
    comm.reduce("in_lo", {"w_in_lo": comm.run(lambda carry: _tn(
        "dw_in_lo", h, dz, col_chunk=QC, a_block=(0,) + half, carry=carry))[0]})
    comm.reduce("in_hi", {"w_in_hi": comm.run(lambda carry: _tn(
        "dw_in_hi", h, dz, col_chunk=QC, a_block=(1,) + half, carry=carry))[0], "w_glu": quarters(_tn("dw_glu", zg, dq))})
    d_wr, d_wi = comm.run(lambda carry: _tn_blocks("dw_r_i", xc, [dpr, dpi], HD, HD, carry))
    d_bbr, d_bbi = comm.run(lambda carry: _tn_blocks("d_bb", ua, [lr, li], NP, NS, carry))
    d_ccr, d_cci = comm.run(lambda carry: _tn_blocks("d_cc", dy, [sr, si], NP, NS, carry))
    comm.drain()
    sums = {"ple": acc_p, "ffn": acc_f, "mix": acc_g, "b_in": acc_b, "lru": acc_l, "s5": acc_s, "s5_a": acc_a}
    blocks = {"bb_re": d_bbr, "bb_im": d_bbi,
              "cc_re": d_ccr, "cc_im": d_cci,
              "w_r": d_wr, "w_i": d_wi}
    return gx, sums, blocks


def _replicated_grads(w, sums, blocks):
    grouped = lambda e, groups: jnp.transpose(e.reshape(e.shape[0], groups, -1), (1, 0, 2))
    d_ar, d_ai = sums["s5_a"][0].reshape(NG, NS), sums["s5_a"][1].reshape(NG, NS)
    d_bbr, d_bbi = grouped(blocks["bb_re"], NG), grouped(blocks["bb_im"], NG)
    _, vjp = jax.vjp(_s5_discretize, w["lam_re"], w["lam_im"], w["log_dt"], w["s5_b_re"], w["s5_b_im"])
    g = dict(zip(("lam_re", "lam_im", "log_dt", "s5_b_re", "s5_b_im"), vjp((d_ar, d_ai, d_bbr, d_bbi))))
    g["s5_c_re"] = grouped(blocks["cc_re"], NG)
    g["s5_c_im"] = -grouped(blocks["cc_im"], NG)
    g["w_r"], g["w_i"] = grouped(blocks["w_r"], NH), grouped(blocks["w_i"], NH)
    g["s5_d"] = sums["s5"][0].reshape(NG, NP)
    g["b_r"] = sums["lru"][1].reshape(NH, HD)
    g["b_i"] = sums["lru"][2].reshape(NH, HD)
    return g


ACC_ROWS = {"g_mix": ("mix", 0), "b_in": ("b_in", 0), "g_ffn": ("ffn", 0), "g_ple_gate": ("ple", 0),
            "b_ple_gate": ("ple", 1), "g_ple": ("ple", 2), "g_final": ("ple", 3), "b_glu": ("s5", 1),
            "lru_lambda": ("lru", 0), "conv_b": ("lru", 3)}
LOSS_ROW = ("ple", 4)
CONV_W_ROWS = ("lru", 4)


SHARDED = [("w_in", (D, QC)), ("w_glu", (S5W // NCHIP, S5W)), ("w_a_out", (S5W, AC)), ("w_b_out", (LW // NCHIP, D)),
           ("w_o", (D // NCHIP, D)), ("w_ffn_gate", (FC, D)), ("w_ffn_up", (FC, D)), ("w_ffn_down", (FC, D)),
           ("w_ple_gate", (D // NCHIP, D)), ("w_ple", (PLE, AC))]
TRANSPOSED = ("w_ffn_gate", "w_ffn_up", "s5_b_re", "s5_b_im")
CONV_SHARD = (4, LW // NCHIP)


def _mesh_pos():
    return lax.axis_index("x"), lax.axis_index("y"), lax.axis_index("c")


def _other_chips(x, y):
    return [(1 - x, y), (x, 1 - y), (1 - x, 1 - y)]


def _half_rows(c, rows, align):
    return pl.ds(pl.multiple_of(c * (rows // 2), align), rows // 2)


def _run_now(name, carry):
    c_in, c_out = len(carry.operands), len(carry.out_shapes)

    def body(*refs):
        ins, outs, sems = refs[:c_in], refs[c_in:c_in + c_out], refs[c_in + c_out:]
        carry.start(ins, outs, sems)
        carry.finish(ins, outs, sems)

    return pl.pallas_call(body, name=name, in_specs=[ANY] * c_in, out_specs=[ANY] * c_out,
                          out_shape=list(carry.out_shapes), scratch_shapes=list(carry.sems),
                          input_output_aliases=dict(carry.aliases))(*_in_hbm(carry.operands))


def _gather_group(shards, split):
    n = len(shards)

    def copies(srcs, outs, sems):
        send_sems, recv_sems = sems
        x, y, c = _mesh_pos()
        k0 = 2 * x + y
        sib = (x, y, 1 - c)
        chips = _other_chips(x, y)

        def remote(src, dst, j, i, to):
            return pltpu.make_async_remote_copy(src_ref=src, dst_ref=dst, send_sem=send_sems.at[j, i],
                                                recv_sem=recv_sems.at[j, i], device_id=to, device_id_type=MESH)

        def rows(ref, i, core, *lead):
            if not split[i]:
                return ref.at[lead] if lead else ref
            return ref.at[(*lead, _half_rows(core, shards[i].shape[0], 16))]

        own = [remote(s, o.at[k0], 6, i, sib) for i, (s, o) in enumerate(zip(srcs, outs))]
        ici, landed, fwd, fwd_landed = [], [], [], []
        for j, chip in enumerate(chips):
            kj = 2 * chip[0] + chip[1]
            pairs = list(enumerate(zip(srcs, outs)))
            ici.append([remote(rows(s, i, c), rows(o, i, c, k0), j, i, (*chip, c)) for i, (s, o) in pairs])
            landed.append([remote(rows(s, i, c), rows(o, i, c, kj), j, i, (*chip, c)) for i, (s, o) in pairs])
            fwd.append([remote(rows(o, i, c, kj), rows(o, i, c, kj), 3 + j, i, sib) for i, (s, o) in pairs if split[i]])
            fwd_landed.append([remote(rows(o, i, 1 - c, kj), rows(o, i, 1 - c, kj), 3 + j, i, sib)
                               for i, (s, o) in pairs if split[i]])
        return own, ici, landed, fwd, fwd_landed

    def start(srcs, outs, sems):
        own, ici, _, _, _ = copies(srcs, outs, sems)
        for cp in own + [cp for per_chip in ici for cp in per_chip]:
            cp.start()

    def finish(srcs, outs, sems):
        own, ici, landed, fwd, fwd_landed = copies(srcs, outs, sems)
        passed = [i for i in range(n) if split[i]]
        for j in range(3):
            for i, cp in enumerate(landed[j]):
                cp.wait_recv()
                if split[i]:
                    fwd[j][passed.index(i)].start()
        for j in range(3):
            for cp in fwd_landed[j]:
                cp.wait_recv()
        for cp in own:
            cp.wait_recv()
        for cp in own + [cp for per_chip in ici + fwd for cp in per_chip]:
            cp.wait_send()

    return _Carried(shards, [_far((NCHIP,) + s.shape, s.dtype) for s in shards],
                    [pltpu.SemaphoreType.DMA((7, n)), pltpu.SemaphoreType.DMA((7, n))], start, finish)


def _each_copy(copies, carried, out_shapes, sems, aliases=None):
    def start(ins, outs, sem_refs):
        for cp in copies(ins, outs, sem_refs):
            cp.start()

    def finish(ins, outs, sem_refs):
        for cp in copies(ins, outs, sem_refs):
            cp.wait()

    return _Carried(carried, out_shapes, sems, start, finish, aliases)


def _swap_group(grads):
    n = len(grads)

    def copies(srcs, outs, sems):
        send_sems, recv_sems = sems
        x, y, c = _mesh_pos()
        return [pltpu.make_async_remote_copy(src_ref=s.at[:, _half_rows(1 - c, s.shape[1], 8)], dst_ref=o,
                                             send_sem=send_sems.at[i], recv_sem=recv_sems.at[i], device_id=(x, y, 1 - c),
                                             device_id_type=MESH) for i, (s, o) in enumerate(zip(srcs, outs))]

    return _each_copy(copies, grads, [pltpu.HBM((NCHIP, g.shape[1] // 2, g.shape[2]), F32) for g in grads],
                      [pltpu.SemaphoreType.DMA((n,)), pltpu.SemaphoreType.DMA((n,))])


def _add_sibling_group(tag, kc_idx, grads, gots):
    n = len(grads)

    def body(kc_ref, *refs):
        for g, rx, p, pb in zip(refs[:n], refs[n:2 * n], refs[2 * n:3 * n], refs[3 * n:]):
            s = g[...] + rx[...]
            pb[...] = s.astype(BF)

            @pl.when(pl.program_id(0) == kc_ref[0])
            def _():
                p[...] = s

    halves = [pl.BlockSpec((None,) + rx.shape[1:], lambda k, kc_ref: (k, 0, 0)) for rx in gots]
    mine = [pl.BlockSpec((None,) + rx.shape[1:], lambda k, kc_ref: (k, kc_ref[1], 0)) for rx in gots]
    own = [pl.BlockSpec(rx.shape[1:], lambda k, kc_ref: (0, 0)) for rx in gots]
    outs = _pallas_call(
        body, name="add_sibling_" + tag,
        grid_spec=pltpu.PrefetchScalarGridSpec(num_scalar_prefetch=1, grid=(NCHIP,), in_specs=mine + halves,
                                               out_specs=own + halves),
        out_shape=[pltpu.HBM(rx.shape[1:], F32) for rx in gots] + [pltpu.HBM(rx.shape, BF) for rx in gots],
        compiler_params=_params(48),
    )(kc_idx, *_in_hbm(list(grads) + list(gots)))
    return outs[:n], outs[n:]


def _exchange_group(parts):
    n = len(parts)

    def copies(srcs, outs, sems):
        send_sems, recv_sems = sems
        x, y, c = _mesh_pos()
        return [pltpu.make_async_remote_copy(
            src_ref=s.at[2 * chip[0] + chip[1]], dst_ref=o.at[j], send_sem=send_sems.at[j, i],
            recv_sem=recv_sems.at[j, i], device_id=(*chip, c), device_id_type=MESH)
            for j, chip in enumerate(_other_chips(x, y)) for i, (s, o) in enumerate(zip(srcs, outs))]

    return _each_copy(copies, parts, [pltpu.HBM((3,) + p.shape[1:], BF) for p in parts],
                      [pltpu.SemaphoreType.DMA((3, n)), pltpu.SemaphoreType.DMA((3, n))])


def _add_chips_group(tag, kc_idx, parts, arrived):
    n = len(parts)

    def body(kc_ref, *refs):
        for p, rx, t in zip(refs[:n], refs[n:2 * n], refs[2 * n:]):
            t[...] = ((p[...] + rx[0].astype(F32)) + rx[1].astype(F32)) + rx[2].astype(F32)

    outs = _pallas_call(
        body, name="add_chips_" + tag,
        grid_spec=pltpu.PrefetchScalarGridSpec(
            num_scalar_prefetch=1, grid=(1,),
            in_specs=([pl.BlockSpec(rx.shape[1:], lambda i, kc_ref: (0, 0)) for rx in arrived]
                      + [pl.BlockSpec(rx.shape, lambda i, kc_ref: (0, 0, 0)) for rx in arrived]),
            out_specs=[pl.BlockSpec((None,) + rx.shape[1:], lambda i, kc_ref: (kc_ref[1], 0, 0)) for rx in arrived]),
        out_shape=[pltpu.HBM((2,) + rx.shape[1:], F32) for rx in arrived],
        compiler_params=_params(48),
    )(kc_idx, *_in_hbm(list(parts) + list(arrived)))
    return list(outs)


def _join_group(halves):
    n = len(halves)

    def copies(bufs, sems):
        send_sems, recv_sems = sems
        x, y, c = _mesh_pos()
        sib = (x, y, 1 - c)
        sends = [pltpu.make_async_remote_copy(src_ref=b.at[c], dst_ref=b.at[c], send_sem=send_sems.at[i],
                                              recv_sem=recv_sems.at[i], device_id=sib, device_id_type=MESH)
                 for i, b in enumerate(bufs)]
        landed = [pltpu.make_async_remote_copy(src_ref=b.at[c], dst_ref=b.at[1 - c], send_sem=send_sems.at[i],
                                               recv_sem=recv_sems.at[i], device_id=sib, device_id_type=MESH)
                  for i, b in enumerate(bufs)]
        return sends, landed

    def start(_, bufs, sems):
        for cp in copies(bufs, sems)[0]:
            cp.start()

    def finish(_, bufs, sems):
        sends, landed = copies(bufs, sems)
        for cp in landed:
            cp.wait_recv()
        for cp in sends:
            cp.wait_send()

    return _Carried(halves, [pltpu.HBM(h.shape, F32) for h in halves],
                    [pltpu.SemaphoreType.DMA((n,)), pltpu.SemaphoreType.DMA((n,))], start, finish,
                    {i: i for i in range(n)})


def _combine(carries):
    operands, out_shapes, sems, aliases, spans = [], [], [], {}, []
    for c in carries:
        aliases.update({len(operands) + i: len(out_shapes) + o for i, o in c.aliases.items()})
        spans.append((len(operands), len(out_shapes), len(sems)))
        operands += list(c.operands)
        out_shapes += list(c.out_shapes)
        sems += list(c.sems)

    def each(phase):
        def run(ins, outs, sem_refs):
            for c, (a, b, s) in zip(carries, spans):
                getattr(c, phase)(ins[a:a + len(c.operands)], outs[b:b + len(c.out_shapes)], sem_refs[s:s + len(c.sems)])
        return run

    return _Carried(operands, out_shapes, sems, each("start"), each("finish"), aliases)


def _allreduce_small(arrays, wire):
    n = len(arrays)
    halves = [(a.shape[0], a.shape[1] // 2) for a in arrays]

    def body(*refs):
        srcs, outs = refs[:n], refs[n:2 * n]
        mine_bufs, sib_bufs, chip_bufs, total_bufs = (refs[k * n:(k + 1) * n] for k in range(2, 6))
        send_sems, recv_sems, local_sems = refs[6 * n:]
        x, y, c = _mesh_pos()
        k0 = 2 * x + y
        sib = (x, y, 1 - c)

        def remote(src, dst, j, i, to):
            return pltpu.make_async_remote_copy(src_ref=src, dst_ref=dst, send_sem=send_sems.at[j, i],
                                                recv_sem=recv_sems.at[j, i], device_id=to, device_id_type=MESH)

        def cols(ref, i, core):
            return ref.at[:, pl.ds(pl.multiple_of(core * halves[i][1], LANE), halves[i][1])]

        swaps = [remote(cols(s, i, 1 - c), b, 0, i, sib) for i, (s, b) in enumerate(zip(srcs, sib_bufs))]
        own = [pltpu.make_async_copy(cols(s, i, c), m, local_sems.at[i]) for i, (s, m) in enumerate(zip(srcs, mine_bufs))]
        for cp in swaps + own:
            cp.start()
        for cp in swaps + own:
            cp.wait()
        for m, b, buf in zip(mine_bufs, sib_bufs, chip_bufs):
            buf[k0] = (m[...] + b[...]).astype(buf.dtype)
        chips = _other_chips(x, y)
        sends = [remote(buf.at[k0], buf.at[k0], 1 + j, i, (*chip, c))
                 for j, chip in enumerate(chips) for i, buf in enumerate(chip_bufs)]
        for cp in sends:
            cp.start()
        for j, chip in enumerate(chips):
            for i, buf in enumerate(chip_bufs):
                remote(buf.at[k0], buf.at[2 * chip[0] + chip[1]], 1 + j, i, (*chip, c)).wait_recv()
        for cp in sends:
            cp.wait_send()
        for t, buf in zip(total_bufs, chip_bufs):
            t[...] = ((buf[0].astype(F32) + buf[1].astype(F32)) + buf[2].astype(F32)) + buf[3].astype(F32)
        joins = [remote(t, cols(o, i, c), 4, i, sib) for i, (t, o) in enumerate(zip(total_bufs, outs))]
        keep = [pltpu.make_async_copy(t, cols(o, i, c), local_sems.at[i]) for i, (t, o) in enumerate(zip(total_bufs, outs))]
        for cp in joins + keep:
            cp.start()
        for i, (t, o) in enumerate(zip(total_bufs, outs)):
            remote(t, cols(o, i, 1 - c), 4, i, sib).wait_recv()
        for cp in joins:
            cp.wait_send()
        for cp in keep:
            cp.wait()

    specs = [_full(a.shape) for a in arrays]
    return _pallas_call(
        body, name="allreduce_small", grid=(1,), in_specs=specs, out_specs=specs,
        out_shape=[_sds(a.shape) for a in arrays],
        scratch_shapes=([pltpu.VMEM(h, F32) for h in halves] + [pltpu.VMEM(h, F32) for h in halves]
                        + [pltpu.VMEM((NCHIP,) + h, dt) for h, dt in zip(halves, wire)] + [pltpu.VMEM(h, F32) for h in halves]
                        + [pltpu.SemaphoreType.DMA((5, n)), pltpu.SemaphoreType.DMA((5, n)), pltpu.SemaphoreType.DMA((n,))]),
        compiler_params=_params(32),
    )(*arrays)


def _adamw_terms(w, g, m, v):
    m = ADAM_B1 * m + (1.0 - ADAM_B1) * g
    v = ADAM_B2 * v + (1.0 - ADAM_B2) * jnp.square(g)
    m_hat = m / (1.0 - ADAM_B1 ** ADAM_STEP)
    v_hat = v / (1.0 - ADAM_B2 ** ADAM_STEP)
    return -ADAM_LR * (m_hat / (jnp.sqrt(v_hat) + ADAM_EPS) + ADAM_WD * w), m, v


ADAM_STEPS = 4


def _adamw_group(tag, ws, gs, ms, vs):
    n = len(ws)

    def body(*refs):
        ins, outs = refs[:4 * n], refs[4 * n:]
        for i in range(n):
            w, g, m, v = (ins[k * n + i][...] for k in range(4))
            outs[i][...] = g
            outs[n + i][...], outs[2 * n + i][...], outs[3 * n + i][...] = _adamw_terms(w, g, m, v)

    specs = [pl.BlockSpec((w.shape[0] // ADAM_STEPS, w.shape[1]), lambda i: (i, 0)) for w in ws]
    outs = _pallas_call(
        body, name="adamw_" + tag, grid=(ADAM_STEPS,), in_specs=specs * 4, out_specs=specs * 4,
        out_shape=[_sds(w.shape) for w in ws] * 4, compiler_params=_params(48),
    )(*_in_hbm(list(ws) + list(gs) + list(ms) + list(vs)))
    return outs[:n], outs[n:2 * n], outs[2 * n:3 * n], outs[3 * n:]


def _adamw_replicated(sums, row_of, direct):
    ns, nr, nd = len(sums), len(row_of), len(direct)

    def body(*refs):
        sum_refs = refs[:ns]
        ins = refs[ns:ns + 3 * nr + 4 * nd]
        outs = refs[ns + 3 * nr + 4 * nd:]
        for i, (_, _, _, si, row) in enumerate(row_of):
            w_ref, m_ref, v_ref = ins[3 * i:3 * i + 3]
            g = sum_refs[si][row:row + 1, :]
            outs[4 * i][...] = g
            outs[4 * i + 1][...], outs[4 * i + 2][...], outs[4 * i + 3][...] = _adamw_terms(w_ref[...], g, m_ref[...], v_ref[...])
        for i in range(nd):
            w_ref, m_ref, v_ref, g_ref = ins[3 * nr + 4 * i:3 * nr + 4 * i + 4]
            o = outs[4 * (nr + i):4 * (nr + i) + 4]
            g = g_ref[...]
            o[0][...] = g
            o[1][...], o[2][...], o[3][...] = _adamw_terms(w_ref[...], g, m_ref[...], v_ref[...])

    operands = list(sums)
    shapes = []
    for w, m, v, _, _ in row_of:
        operands += [w, m, v]
        shapes += [w.shape] * 4
    for w, m, v, g in direct:
        operands += [w, m, v, g]
        shapes += [w.shape] * 4
    flat = _pallas_call(
        body, name="adamw_replicated", grid=(1,), in_specs=[_full(a.shape) for a in operands],
        out_specs=[_full(s) for s in shapes], out_shape=[_sds(s) for s in shapes],
        compiler_params=_params(56),
    )(*operands)
    return [flat[4 * i:4 * i + 4] for i in range(nr + nd)]


class _Exchanges:
    def __init__(self, shards, conv_w, chip, core, apply):
        self.shards, self.conv_w, self.apply = shards, conv_w, apply
        self.active, self.calls = [], 0
        self.chip_core_idx = jnp.stack([chip, core]).astype(jnp.int32)

    def first(self):
        names = ["w_in", "w_glu"]
        got = _run_now("gather_first", _gather_group([self.shards[n] for n in names] + [self.conv_w],
                                                     [True, True, False]))
        out = dict(zip(names, got))
        out["conv_w"] = jnp.transpose(got[2], (1, 0, 2)).reshape(4, LW)
        return out

    def gather(self, names):
        return _gather_group([self.shards[n] for n in names], [True] * len(names))

    def reduce(self, tag, grads):
        self.active.append({"tag": tag, "names": list(grads), "stage": 0, "grads": list(grads.values())})

    def run(self, call, hold=()):
        groups = [g for g in self.active if g["tag"] not in hold]
        carries = [self._exchange_of(g) for g in groups]
        carry = _combine(carries)
        outs = list(call(carry))
        own = len(outs) - len(carry.out_shapes)
        landed = outs[own:]
        for g, c in zip(groups, carries):
            self._sum_after(g, landed[:len(c.out_shapes)])
            landed = landed[len(c.out_shapes):]
        self.active = [g for g in self.active if g["stage"] < 3]
        return outs[:own]

    def _exchange_of(self, g):
        if g["stage"] == 0:
            return _swap_group(g["grads"])
        if g["stage"] == 1:
            return _exchange_group(g["bf16"])
        return _join_group(g["halves"])

    def _sum_after(self, g, landed):
        if g["stage"] == 0:
            g["f32"], g["bf16"] = _add_sibling_group(g["tag"], self.chip_core_idx, g["grads"], landed)
        elif g["stage"] == 1:
            g["halves"] = _add_chips_group(g["tag"], self.chip_core_idx, g["f32"], landed)
        else:
            self.apply(g["tag"], g["names"], [t.reshape(2 * t.shape[1], t.shape[2]) for t in landed])
        g["stage"] += 1

    def drain(self):
        while self.active:
            self.calls += 1
            self.run(lambda carry: _run_now("reduce_%d" % self.calls, carry))


INPUT_NAMES = (["x", "p"] + [n for n in
               ["g_mix", "w_in", "b_in", "lam_re", "lam_im", "log_dt", "s5_b_re", "s5_b_im", "s5_c_re", "s5_c_im", "s5_d",
                "w_glu", "b_glu", "conv_w", "conv_b", "w_r", "b_r", "w_i", "b_i", "lru_lambda", "w_a_out", "w_b_out", "w_o",
                "g_ffn", "w_ffn_gate", "w_ffn_up", "w_ffn_down", "g_ple_gate", "w_ple_gate", "b_ple_gate", "w_ple", "g_ple",
                "g_final"]])
WEIGHT_NAMES = INPUT_NAMES[2:]


def kernel(*args):
    names = INPUT_NAMES + ["loss_target"] + ["m_" + n for n in WEIGHT_NAMES] + ["v_" + n for n in WEIGHT_NAMES]
    assert len(args) == len(names)
    given = dict(zip(names, args))

    def view(name):
        a = given[name]
        return jnp.swapaxes(a, -1, -2) if name.endswith(TRANSPOSED) else a

    def unview(name, a):
        return jnp.swapaxes(a, -1, -2) if name in TRANSPOSED else a

    def local(name):
        return view(name) if name.endswith("g_final") else view(name)[0]

    xi, yi, ci = _mesh_pos()
    k0 = 2 * xi + yi
    x, p, tgt = given["x"][0], given["p"][0, 0], given["loss_target"][0]

    results = {}

    row_halves = {}

    def apply(tag, names, totals):
        totals = dict(zip(names, totals))
        row_halves.update({n: totals.pop(n) for n in names if n in ("w_in_lo", "w_in_hi")})
        if len(row_halves) == 2:
            totals["w_in"] = jnp.concatenate([row_halves.pop("w_in_lo"), row_halves.pop("w_in_hi")])
        names = list(totals)
        if not names:
            return
        new = _adamw_group(tag, [local(n) for n in names], list(totals.values()), [local("m_" + n) for n in names],
                           [local("v_" + n) for n in names])
        for kind, arrays in zip(("grad", "delta", "new_m", "new_v"), new):
            for n, arr in zip(names, arrays):
                results[kind, n] = unview(n, arr[None])

    comm = _Exchanges({n: local(n).astype(BF) for n, _ in SHARDED}, local("conv_w"), k0, ci, apply)
    w = {n: local(n) for n in WEIGHT_NAMES if n != "conv_w" and n not in dict(SHARDED)}
    gx, sums, blocks = _local_step(x, p, tgt, w, comm)

    sum_names, block_names = list(sums), list(blocks)
    red = _allreduce_small([sums[n] for n in sum_names] + [blocks[n] for n in block_names],
                           [F32] * len(sum_names) + [BF] * len(block_names))
    sums = dict(zip(sum_names, red[:len(sum_names)]))
    blocks = dict(zip(block_names, red[len(sum_names):]))
    loss = jnp.sum(sums[LOSS_ROW[0]][LOSS_ROW[1]])
    direct_g = _replicated_grads(w, sums, blocks)
    conv_rows = sums[CONV_W_ROWS[0]][CONV_W_ROWS[1]:CONV_W_ROWS[1] + 4]
    direct_g["conv_w"] = lax.dynamic_slice(conv_rows, (0, k0 * CONV_SHARD[1]), CONV_SHARD)
    as_row = lambda a: a.reshape(1, -1)
    row_names = list(ACC_ROWS)
    row_of = [(as_row(given[n]), as_row(given["m_" + n]), as_row(given["v_" + n]),
               sum_names.index(ACC_ROWS[n][0]), ACC_ROWS[n][1]) for n in row_names]
    direct_names = list(direct_g)
    direct = [(view(n), view("m_" + n), view("v_" + n), direct_g[n].reshape(view(n).shape)) for n in direct_names]
    done = _adamw_replicated([sums[n] for n in sum_names], row_of, direct)
    for n, four in zip(row_names + direct_names, done):
        for kind, arr in zip(("grad", "delta", "new_m", "new_v"), four):
            results[kind, n] = unview(n, arr).reshape(given[n].shape)

    out = [loss, gx[None]]
    for kind in ("grad", "delta", "new_m", "new_v"):
        out += [results[kind, n] for n in WEIGHT_NAMES]
    return tuple(out)
```

```python
import functools
import math

import jax
import jax.numpy as jnp
from jax import lax
from jax.experimental import pallas as pl
from jax.experimental.pallas import tpu as pltpu

F32 = jnp.float32
BF = jnp.bfloat16

D = 1024
S5W = 512
NG, NS, NP = 32, 64, 16
GN = NG * NS
LW = 1024
NH, HD = 16, 64
LRU_C = 8.0
FH = 2816
NCHIP = 4
FC = FH // NCHIP
PLE = 256
INC = S5W + LW + 2 * D
EPS = 1e-6
ADAM_LR, ADAM_B1, ADAM_B2, ADAM_EPS, ADAM_WD, ADAM_STEP = 0.001, 0.9, 0.999, 1e-08, 0.01, 10

TM = 256
TK = 1024
LC = 512
SUB = 8
VMEM_MB = 1024 * 1024
MESH = pl.DeviceIdType.MESH
ANY = pl.BlockSpec(memory_space=pl.ANY)


def _mm(a, b):
    return jnp.dot(a.astype(BF), b.astype(BF), preferred_element_type=F32)


def _mm_nt(a, b):
    return lax.dot_general(a.astype(BF), b.astype(BF), (((1,), (1,)), ((), ())), preferred_element_type=F32)


def _mm_tn(a, b):
    return lax.dot_general(a.astype(BF), b.astype(BF), (((0,), (0,)), ((), ())), preferred_element_type=F32)


def _blockdiag_mm(x, blocks_ref):
    n, rows, _ = blocks_ref.shape
    return jnp.concatenate([jnp.dot(x[:, j * rows:(j + 1) * rows], blocks_ref[j], preferred_element_type=F32)
                            for j in range(n)], axis=1)


def _blockdiag_mm_t(x, blocks_ref):
    n, _, wide = blocks_ref.shape
    return jnp.concatenate([lax.dot_general(x[:, j * wide:(j + 1) * wide], blocks_ref[j], (((1,), (1,)), ((), ())),
                                            preferred_element_type=F32) for j in range(n)], axis=1)


def _rms(x):
    r = lax.rsqrt(jnp.mean(x * x, axis=-1, keepdims=True) + EPS)
    return x * r, r


def _rms_bwd(dy, xh, r, g):
    dxh = dy * g
    return r * (dxh - xh * jnp.mean(dxh * xh, axis=-1, keepdims=True))


def _colsum(x):
    return jnp.sum(x, axis=0, keepdims=True)


def _sig(x):
    return jax.nn.sigmoid(x)


def _gelu_grad(x):
    c = math.sqrt(2.0 / math.pi)
    t = jnp.tanh(c * (x + 0.044715 * x * x * x))
    return 0.5 * (1.0 + t) + 0.5 * x * (1.0 - t * t) * c * (1.0 + 3.0 * 0.044715 * x * x)


def _neg_expm1(x):
    series = -x * (1.0 + x * (0.5 + x * (1.0 / 6.0 + x * (1.0 / 24.0))))
    return jnp.where(x > -0.03, series, 1.0 - jnp.exp(x))


def _tok(width):
    return pl.BlockSpec((TM, width), lambda i: (i, 0))


def _tok_rev(width, nt):
    return pl.BlockSpec((TM, width), lambda i: (nt - 1 - i, 0))


def _full(shape):
    return pl.BlockSpec(shape, lambda i: (0,) * len(shape))


def _params(vmem_mb, **kw):
    return pltpu.CompilerParams(dimension_semantics=("arbitrary",), vmem_limit_bytes=vmem_mb * VMEM_MB, **kw)


def _sds(shape, dtype=F32):
    return jax.ShapeDtypeStruct(shape, dtype)


def _far(shape, dtype=F32):
    return pltpu.HBM(shape, dtype)


class _Carried:
    def __init__(self, operands, out_shapes, sems, start, finish, aliases=None):
        self.operands, self.out_shapes, self.sems = list(operands), list(out_shapes), list(sems)
        self.start, self.finish, self.aliases = start, finish, dict(aliases or {})


def _in_hbm(arrays):
    return [pltpu.with_memory_space_constraint(a, pltpu.HBM) for a in arrays]


def _pallas_call(body, carry=None, **kw):
    if carry is None:
        return pl.pallas_call(body, **kw)

    def at_step(corner):
        hit = [pl.program_id(d) == (size - 1 if corner else 0) for d, size in enumerate(kw["grid"])]
        return functools.reduce(jnp.logical_and, hit)

    name, grid, compiler_params = kw["name"], kw["grid"], kw["compiler_params"]
    in_specs, out_specs, out_shape = list(kw["in_specs"]), list(kw["out_specs"]), list(kw["out_shape"])
    scratch_shapes = list(kw.get("scratch_shapes", ()))
    n_in, n_out, n_scr = len(in_specs), len(out_specs), len(scratch_shapes)
    c_in, c_out = len(carry.operands), len(carry.out_shapes)

    def full_body(*refs):
        ins, refs = refs[:n_in], refs[n_in:]
        c_ins, refs = refs[:c_in], refs[c_in:]
        outs, refs = refs[:n_out], refs[n_out:]
        c_outs, refs = refs[:c_out], refs[c_out:]
        scratch, c_sems = refs[:n_scr], refs[n_scr:]

        @pl.when(at_step(0))
        def _():
            carry.start(c_ins, c_outs, c_sems)

        body(*ins, *outs, *scratch)

        @pl.when(at_step(1))
        def _():
            carry.finish(c_ins, c_outs, c_sems)

    call = pl.pallas_call(
        full_body, name=name, grid=grid, in_specs=in_specs + [ANY] * c_in, out_specs=out_specs + [ANY] * c_out,
        out_shape=out_shape + list(carry.out_shapes), scratch_shapes=scratch_shapes + list(carry.sems),
        input_output_aliases={n_in + i: n_out + o for i, o in carry.aliases.items()},
        compiler_params=compiler_params)
    return lambda *operands: call(*operands, *_in_hbm(carry.operands))


def _resident(pairs, sems):
    first = pl.program_id(0) == 0
    copies = [pltpu.make_async_copy(src, dst, sems.at[j]) for j, (src, dst) in enumerate(pairs)]

    @pl.when(first)
    def _():
        for cp in copies:
            cp.start()

    def wait(j):
        @pl.when(first)
        def _():
            copies[j].wait()

    return wait


def _resident_now(pairs, sems):
    @pl.when(pl.program_id(0) == 0)
    def _():
        copies = [pltpu.make_async_copy(src, dst, sems.at[j]) for j, (src, dst) in enumerate(pairs)]
        for cp in copies:
            cp.start()
        for cp in copies:
            cp.wait()


def _row_iota(width):
    return lax.broadcasted_iota(jnp.int32, (SUB, width), 0)


def _bcast_row(x, row):
    return jnp.broadcast_to(x[row:row + 1, :], x.shape)


def _slab(k):
    return pl.ds(pl.multiple_of(k * SUB, SUB), SUB)


QC = INC // NCHIP
Z_PARTS = ((0, S5W), (S5W, S5W + LW), (S5W + LW, INC))


def _inproj_fwd(x, g_mix, w_in, b_in, carry=None):
    L = x.shape[0]

    def body(x_ref, g_ref, w_hbm, b_ref, h_ref, ua_ref, ub_ref, gp_ref, w_vm, w_sems):
        _resident_now([(w_hbm.at[k], w_vm.at[k]) for k in range(NCHIP)], w_sems)
        xh, _ = _rms(x_ref[...])
        h = (xh * g_ref[...]).astype(BF)
        h_ref[...] = h
        for k in range(NCHIP):
            lo, hi = k * QC, (k + 1) * QC
            z = jnp.dot(h, w_vm[k], preferred_element_type=F32) + b_ref[:, lo:hi]
            for ref, (a, b) in zip((ua_ref, ub_ref, gp_ref), Z_PARTS):
                s, e = max(lo, a), min(hi, b)
                if s < e:
                    ref[:, s - a:e - a] = z[:, s - lo:e - lo]

    return _pallas_call(
        body, carry, name="inproj_fwd", grid=(L // TM,),
        in_specs=[_tok(D), _full((1, D)), ANY, _full((1, INC))],
        out_specs=[_tok(D), _tok(S5W), _tok(LW), _tok(2 * D)],
        out_shape=[_far((L, D), BF), _far((L, S5W)), _far((L, LW)), _sds((L, 2 * D))],
        scratch_shapes=[pltpu.VMEM((NCHIP, D, QC), BF), pltpu.SemaphoreType.DMA((NCHIP,))],
        compiler_params=_params(40),
    )(*_in_hbm([x]), g_mix, *_in_hbm([w_in]), b_in)


def _inproj_bwd(x, dx1, dua, dub, dgp, g_mix, w_in, carry=None):
    L = x.shape[0]

    def body(x_ref, dx1_ref, dua_ref, dub_ref, dgp_ref, g_ref, w_hbm, gx_ref, dz_ref, dg_ref, db_ref, w_vm, w_sems):
        _resident_now([(w_hbm.at[k], w_vm.at[k]) for k in range(NCHIP)], w_sems)

        @pl.when(pl.program_id(0) == 0)
        def _():
            dg_ref[...] = jnp.zeros_like(dg_ref)
            db_ref[...] = jnp.zeros_like(db_ref)

        for src, (a, b) in zip((dua_ref, dub_ref, dgp_ref), Z_PARTS):
            d = src[...]
            dz_ref[:, a:b] = d.astype(BF)
            db_ref[0:1, a:b] += _colsum(d)
        dh = jnp.zeros((TM, D), F32)
        for k in range(NCHIP):
            dh = dh + lax.dot_general(dz_ref[:, k * QC:(k + 1) * QC], w_vm[k], (((1,), (1,)), ((), ())),
                                      preferred_element_type=F32)
        xh, r = _rms(x_ref[...])
        dg_ref[0:1, :] += _colsum(dh * xh)
        gx_ref[...] = dx1_ref[...] + _rms_bwd(dh, xh, r, g_ref[...])

    return _pallas_call(
        body, carry, name="inproj_bwd", grid=(L // TM,),
        in_specs=[_tok(D), _tok(D), _tok(S5W), _tok(LW), _tok(2 * D), _full((1, D)), ANY],
        out_specs=[_tok(D), _tok(INC), _full((SUB, D)), _full((SUB, INC))],
        out_shape=[_sds((L, D)), _sds((L, INC), BF), _sds((SUB, D)), _sds((SUB, INC))],
        scratch_shapes=[pltpu.VMEM((NCHIP, D, QC), BF), pltpu.SemaphoreType.DMA((NCHIP,))],
        compiler_params=_params(40),
    )(x, dx1, *_in_hbm([dua]), dub, dgp, g_mix, *_in_hbm([w_in]))


def _cscan(xr_ref, xi_ref, con_ref, cr_ref, ci_ref, reverse):
    n_slab = xr_ref.shape[0] // SUB
    width = xr_ref.shape[1]
    for lc in range(width // LC):
        cols = slice(lc * LC, (lc + 1) * LC)
        con = [con_ref[SUB * j:SUB * (j + 1), cols] for j in range(8)]

        def step(k, carry, cols=cols, con=con):
            cr, ci = carry
            rows = _slab(n_slab - 1 - k if reverse else k)
            xr, xi = xr_ref[rows, cols], xi_ref[rows, cols]
            for j, sh in enumerate((1, 2, 4)):
                mr, mi = con[2 * j], con[2 * j + 1]
                pr = pltpu.roll(xr, SUB - sh if reverse else sh, 0)
                pi = pltpu.roll(xi, SUB - sh if reverse else sh, 0)
                xr, xi = xr + mr * pr - mi * pi, xi + mr * pi + mi * pr
            xr, xi = xr + con[6] * cr - con[7] * ci, xi + con[6] * ci + con[7] * cr
            xr_ref[rows, cols] = xr
            xi_ref[rows, cols] = xi
            row = 0 if reverse else SUB - 1
            return _bcast_row(xr, row), _bcast_row(xi, row)

        cr, ci = lax.fori_loop(0, n_slab, step, (cr_ref[:, cols], ci_ref[:, cols]))
        cr_ref[:, cols] = cr
        ci_ref[:, cols] = ci


def _s5_fwd(ua, bbr, bbi, ccr, cci, dsk, con, w_glu, b_glu, carry=None):
    L = ua.shape[0]

    def body(ua_ref, bbr_hbm, bbi_hbm, ccr_hbm, cci_hbm, dsk_ref, con_ref, wg_ref, bg_ref,
             sr_ref, si_ref, y_ref, zg_ref, ya_ref, bbr_vm, bbi_vm, ccr_vm, cci_vm, cr_ref, ci_ref, w_sems):
        landed = _resident([(bbr_hbm, bbr_vm), (bbi_hbm, bbi_vm), (ccr_hbm, ccr_vm), (cci_hbm, cci_vm)], w_sems)

        @pl.when(pl.program_id(0) == 0)
        def _():
            cr_ref[...] = jnp.zeros_like(cr_ref)
            ci_ref[...] = jnp.zeros_like(ci_ref)

        u = ua_ref[...]
        ub = u.astype(BF)
        landed(0)
        sr_ref[...] = _blockdiag_mm(ub, bbr_vm)
        landed(1)
        si_ref[...] = _blockdiag_mm(ub, bbi_vm)
        _cscan(sr_ref, si_ref, con_ref, cr_ref, ci_ref, reverse=False)
        landed(2)
        landed(3)
        y = (_blockdiag_mm_t(sr_ref[...].astype(BF), ccr_vm) - _blockdiag_mm_t(si_ref[...].astype(BF), cci_vm)
             + dsk_ref[...] * u)
        y_ref[...] = y
        zg = jax.nn.gelu(y)
        zg_ref[...] = zg.astype(BF)
        q = _mm(zg, wg_ref[...]) + bg_ref[...]
        ya_ref[...] = (zg * _sig(q)).astype(BF)

    return _pallas_call(
        body, carry, name="s5_fwd", grid=(L // TM,),
        in_specs=[_tok(S5W), ANY, ANY, ANY, ANY, _full((1, S5W)), _full((8 * SUB, GN)),
                  _full((S5W, S5W)), _full((1, S5W))],
        out_specs=[_tok(GN), _tok(GN), _tok(S5W), _tok(S5W), _tok(S5W)],
        out_shape=[_sds((L, GN)), _sds((L, GN)), _far((L, S5W)), _far((L, S5W), BF), _far((L, S5W), BF)],
        scratch_shapes=[pltpu.VMEM((S5W // 128, 128, GN // (S5W // 128)), BF)] * 4 + [
                        pltpu.VMEM((SUB, GN), F32), pltpu.VMEM((SUB, GN), F32),
                        pltpu.SemaphoreType.DMA((4,))],
        compiler_params=_params(44),
    )(*_in_hbm([ua]), bbr, bbi, ccr, cci, dsk, con, w_glu, b_glu)


def _s5_bwd(dya, y, ua, sr, si, bbr, bbi, ccr, cci, dsk, con_rev, w_glu, b_glu, carry=None):
    L = ua.shape[0]
    nt = L // TM
    spt = TM // SUB
    n_slab = spt

    def halo_map(i):
        return (jnp.maximum((nt - 1 - i) * spt - 1, 0), 0)

    def body(dya_ref, y_ref, ua_ref, sr_ref, si_ref, hr_ref, hi_ref, bbr_hbm, bbi_hbm, ccr_hbm, cci_hbm,
             dsk_ref, con_ref, wg_ref, bg_ref,
             dua_ref, dq_ref, dy_ref, lr_ref, li_ref, da_ref, dsm_ref,
             bbr_vm, bbi_vm, ccr_vm, cci_vm, cr_ref, ci_ref, w_sems):
        i = pl.program_id(0)
        landed = _resident([(ccr_hbm, ccr_vm), (cci_hbm, cci_vm), (bbr_hbm, bbr_vm), (bbi_hbm, bbi_vm)], w_sems)

        @pl.when(i == 0)
        def _():
            cr_ref[...] = jnp.zeros_like(cr_ref)
            ci_ref[...] = jnp.zeros_like(ci_ref)
            da_ref[...] = jnp.zeros_like(da_ref)
            dsm_ref[...] = jnp.zeros_like(dsm_ref)

        u = ua_ref[...]
        yv = y_ref[...]
        dya = dya_ref[...]
        zg = jax.nn.gelu(yv)
        sg = _sig(_mm(zg, wg_ref[...]) + bg_ref[...])
        dq = dya * zg * sg * (1.0 - sg)
        dq_ref[...] = dq.astype(BF)
        dzg = dya * sg + _mm_nt(dq, wg_ref[...])
        dy = dzg * _gelu_grad(yv)
        dyb = dy.astype(BF)
        dy_ref[...] = dyb
        dsm_ref[0:1, :] += _colsum(dy * u)
        dsm_ref[1:2, :] += _colsum(dq)
        landed(0)
        lr_ref[...] = _blockdiag_mm(dyb, ccr_vm)
        landed(1)
        li_ref[...] = -_blockdiag_mm(dyb, cci_vm)
        _cscan(lr_ref, li_ref, con_ref, cr_ref, ci_ref, reverse=True)

        first_tile = (i == nt - 1)
        row = _row_iota(LC)
        for lc in range(GN // LC):
            cols = slice(lc * LC, (lc + 1) * LC)
            h_r = jnp.where(first_tile, 0.0, hr_ref[:, cols])
            h_i = jnp.where(first_tile, 0.0, hi_ref[:, cols])

            def step(k, acc, cols=cols, h_r=h_r, h_i=h_i):
                ar, ai = acc
                rows = _slab(k)
                prev = _slab(jnp.maximum(k - 1, 0))
                pr = jnp.where(k == 0, h_r, sr_ref[prev, cols])
                pi = jnp.where(k == 0, h_i, si_ref[prev, cols])
                spr = pltpu.roll(jnp.where(row == SUB - 1, pr, sr_ref[rows, cols]), 1, 0)
                spi = pltpu.roll(jnp.where(row == SUB - 1, pi, si_ref[rows, cols]), 1, 0)
                lr, li = lr_ref[rows, cols], li_ref[rows, cols]
                return ar + lr * spr + li * spi, ai + li * spr - lr * spi

            zero = jnp.zeros((SUB, LC), F32)
            ar, ai = lax.fori_loop(0, n_slab, step, (zero, zero))
            da_ref[0:1, cols] += _colsum(ar)
            da_ref[1:2, cols] += _colsum(ai)

        landed(2)
        landed(3)
        dua_ref[...] = (dy * dsk_ref[...] + _blockdiag_mm_t(lr_ref[...].astype(BF), bbr_vm)
                        + _blockdiag_mm_t(li_ref[...].astype(BF), bbi_vm))

    return _pallas_call(
        body, carry, name="s5_bwd", grid=(nt,),
        in_specs=[_tok_rev(S5W, nt), _tok_rev(S5W, nt), _tok_rev(S5W, nt), _tok_rev(GN, nt), _tok_rev(GN, nt),
                  pl.BlockSpec((SUB, GN), halo_map), pl.BlockSpec((SUB, GN), halo_map),
                  ANY, ANY, ANY, ANY, _full((1, S5W)), _full((8 * SUB, GN)), _full((S5W, S5W)), _full((1, S5W))],
        out_specs=[_tok_rev(S5W, nt), _tok_rev(S5W, nt), _tok_rev(S5W, nt), _tok_rev(GN, nt), _tok_rev(GN, nt),
                   _full((SUB, GN)), _full((SUB, S5W))],
        out_shape=[_sds((L, S5W)), _sds((L, S5W), BF), _sds((L, S5W), BF), _sds((L, GN)), _sds((L, GN)),
                   _sds((SUB, GN)), _sds((SUB, S5W))],
        scratch_shapes=[pltpu.VMEM((S5W // 128, 128, GN // (S5W // 128)), BF)] * 4 + [
                        pltpu.VMEM((SUB, GN), F32), pltpu.VMEM((SUB, GN), F32),
                        pltpu.SemaphoreType.DMA((4,))],
        compiler_params=_params(52),
    )(dya, y, ua, sr, si, sr, si, bbr, bbi, ccr, cci, dsk, con_rev, w_glu, b_glu)


def _lru_gate_terms(rg, sp):
    log_a = -LRU_C * rg * sp
    a = jnp.exp(log_a)
    mult = jnp.sqrt(_neg_expm1(2.0 * log_a))
    return a, mult


def _lru_fwd(ub, conv_w, conv_b, wr, wi, b_r, b_i, sp, carry=None):
    L = ub.shape[0]
    n_slab = TM // SUB

    def body(ub_ref, cw_ref, cb_ref, wr_ref, wi_ref, br_ref, bi_ref, sp_ref,
             xc_ref, rg_ref, ig_ref, h_ref, hp_ref, a_ref, halo_ref, carry_ref):
        @pl.when(pl.program_id(0) == 0)
        def _():
            halo_ref[...] = jnp.zeros_like(halo_ref)
            carry_ref[...] = jnp.zeros_like(carry_ref)

        row = _row_iota(LW)
        taps = [cw_ref[k:k + 1, :] for k in range(4)]
        cb = cb_ref[...]

        def conv_step(k, prev):
            rows = _slab(k)
            cur = ub_ref[rows, :]
            acc = taps[3] * cur + cb
            for j in (1, 2, 3):
                acc = acc + taps[3 - j] * pltpu.roll(jnp.where(row >= SUB - j, prev, cur), j, 0)
            xc_ref[rows, :] = acc
            return cur

        halo_ref[...] = lax.fori_loop(0, n_slab, conv_step, halo_ref[...])

        xc = xc_ref[...]
        xcb = xc.astype(BF)
        rg = _sig(_blockdiag_mm(xcb, wr_ref) + br_ref[...])
        ig = _sig(_blockdiag_mm(xcb, wi_ref) + bi_ref[...])
        rg_ref[...] = rg
        ig_ref[...] = ig
        a, mult = _lru_gate_terms(rg, sp_ref[...])
        a_ref[...] = a
        h_ref[...] = mult * ig * xc

        rowc = _row_iota(LC)
        for lc in range(LW // LC):
            cols = slice(lc * LC, (lc + 1) * LC)

            def step(k, c, cols=cols):
                rows = _slab(k)
                av, b = a_ref[rows, cols], h_ref[rows, cols]
                for sh in (1, 2, 4):
                    keep = rowc >= sh
                    b = b + av * jnp.where(keep, pltpu.roll(b, sh, 0), 0.0)
                    av = av * jnp.where(keep, pltpu.roll(av, sh, 0), 1.0)
                h = b + av * c
                h_ref[rows, cols] = h
                hp_ref[rows, cols] = jnp.where(rowc == 0, c, pltpu.roll(h, 1, 0))
                return _bcast_row(h, SUB - 1)

            carry_ref[:, cols] = lax.fori_loop(0, n_slab, step, carry_ref[:, cols])

    return _pallas_call(
        body, carry, name="lru_fwd", grid=(L // TM,),
        in_specs=[_tok(LW), _full((4, LW)), _full((1, LW)), _full((LW // 128, 128, 128)), _full((LW // 128, 128, 128)),
                  _full((1, LW)), _full((1, LW)), _full((1, LW))],
        out_specs=[_tok(LW)] * 5,
        out_shape=[_far((L, LW))] * 5,
        scratch_shapes=[pltpu.VMEM((TM, LW), F32), pltpu.VMEM((SUB, LW), F32), pltpu.VMEM((SUB, LW), F32)],
        compiler_params=_params(40),
    )(*_in_hbm([ub]), conv_w, conv_b, wr, wi, b_r, b_i, sp)


def _lru_bwd(dyb, xc, rg, ig, hp, ub, conv_w, wr, wi, sp, dsp, carry=None):
    L = ub.shape[0]
    nt = L // TM
    spt = TM // SUB
    n_slab = spt

    def halo_map(i):
        return (jnp.maximum((nt - 1 - i) * spt - 1, 0), 0)

    def body(dh_ref, xc_ref, rg_ref, ig_ref, hp_ref, ub_ref, uh_ref, cw_ref, wr_ref, wi_ref, sp_ref, dsp_ref,
             dub_ref, dpr_ref, dpi_ref, acc_ref, a_ref, lam_ref, dxc_ref, carry_ref, next_ref):
        i = pl.program_id(0)

        @pl.when(i == 0)
        def _():
            carry_ref[...] = jnp.zeros_like(carry_ref)
            next_ref[...] = jnp.zeros_like(next_ref)
            acc_ref[...] = jnp.zeros_like(acc_ref)

        sp = sp_ref[...]
        rg, ig, xc = rg_ref[...], ig_ref[...], xc_ref[...]
        a, mult = _lru_gate_terms(rg, sp)
        a_ref[...] = a

        rowc = _row_iota(LC)
        for lc in range(LW // LC):
            cols = slice(lc * LC, (lc + 1) * LC)

            def step(k, c, cols=cols):
                rows = _slab(n_slab - 1 - k)
                av, dh = a_ref[rows, cols], dh_ref[rows, cols]
                b = av * dh
                for sh in (1, 2, 4):
                    keep = rowc < SUB - sh
                    b = b + av * jnp.where(keep, pltpu.roll(b, SUB - sh, 0), 0.0)
                    av = av * jnp.where(keep, pltpu.roll(av, SUB - sh, 0), 1.0)
                mu = b + av * c
                lam_ref[rows, cols] = dh + jnp.where(rowc == SUB - 1, c, pltpu.roll(mu, SUB - 1, 0))
                return _bcast_row(mu, 0)

            carry_ref[:, cols] = lax.fori_loop(0, n_slab, step, carry_ref[:, cols])

        lam = lam_ref[...]
        d_a = lam * hp_ref[...]
        d_mult = lam * ig * xc
        d_ig = lam * mult * xc
        dxc = lam * mult * ig
        d_log_a = d_a * a - d_mult * a * a / mult
        d_rg = (-LRU_C) * sp * d_log_a
        acc_ref[0:1, :] += _colsum((-LRU_C) * rg * d_log_a) * dsp_ref[...]
        dpr = d_rg * rg * (1.0 - rg)
        dpi = d_ig * ig * (1.0 - ig)
        acc_ref[1:2, :] += _colsum(dpr)
        acc_ref[2:3, :] += _colsum(dpi)
        dprb, dpib = dpr.astype(BF), dpi.astype(BF)
        dpr_ref[...] = dprb
        dpi_ref[...] = dpib
        dxc = dxc + _blockdiag_mm_t(dprb, wr_ref) + _blockdiag_mm_t(dpib, wi_ref)
        dxc_ref[...] = dxc
        acc_ref[3:4, :] += _colsum(dxc)

        row = _row_iota(LW)
        taps = [cw_ref[k:k + 1, :] for k in range(4)]
        u_halo = jnp.where(i == nt - 1, 0.0, uh_ref[...])
        nxt_tile = next_ref[...]

        def conv_step(k, accs):
            rows = _slab(k)
            cur = dxc_ref[rows, :]
            nxt = jnp.where(k == n_slab - 1, nxt_tile, dxc_ref[_slab(jnp.minimum(k + 1, n_slab - 1)), :])
            ucur = ub_ref[rows, :]
            uprev = jnp.where(k == 0, u_halo, ub_ref[_slab(jnp.maximum(k - 1, 0)), :])
            du = taps[3] * cur
            new = [accs[3] + cur * ucur]
            for j in (1, 2, 3):
                du = du + taps[3 - j] * pltpu.roll(jnp.where(row < j, nxt, cur), SUB - j, 0)
                new.append(accs[3 - j] + cur * pltpu.roll(jnp.where(row >= SUB - j, uprev, ucur), j, 0))
            dub_ref[rows, :] = du
            return tuple(new[::-1])

        zero = jnp.zeros((SUB, LW), F32)
        accs = lax.fori_loop(0, n_slab, conv_step, (zero, zero, zero, zero))
        for k in range(4):
            acc_ref[4 + k:5 + k, :] += _colsum(accs[k])
        next_ref[...] = dxc_ref[0:SUB, :]

    return _pallas_call(
        body, carry, name="lru_bwd", grid=(nt,),
        in_specs=[_tok_rev(LW, nt)] * 6 + [pl.BlockSpec((SUB, LW), halo_map), _full((4, LW)),
                                           _full((LW // 128, 128, 128)), _full((LW // 128, 128, 128)), _full((1, LW)), _full((1, LW))],
        out_specs=[_tok_rev(LW, nt), _tok_rev(LW, nt), _tok_rev(LW, nt), _full((SUB, LW))],
        out_shape=[_sds((L, LW)), _far((L, LW), BF), _far((L, LW), BF), _sds((SUB, LW))],
        scratch_shapes=[pltpu.VMEM((TM, LW), F32), pltpu.VMEM((TM, LW), F32), pltpu.VMEM((TM, LW), F32),
                        pltpu.VMEM((SUB, LW), F32), pltpu.VMEM((SUB, LW), F32)],
        compiler_params=_params(48),
    )(dyb, xc, rg, ig, hp, ub, ub, conv_w, wr, wi, sp, dsp)


AC = D // NCHIP


def _merge_fwd(x, ya, yb, gp, w_a, w_b, w_o, carry=None):
    L = x.shape[0]

    def body(x_ref, ya_ref, yb_ref, gp_ref, wa_ref, wb_ref, wo_ref, x1_ref, pa_ref, pb_ref, mg_ref):
        ya = ya_ref[...]
        for k in range(NCHIP):
            pa_ref[:, k * AC:(k + 1) * AC] = jnp.dot(ya, wa_ref[k], preferred_element_type=F32)
        pb = _mm(yb_ref[...], wb_ref[...])
        pb_ref[...] = pb
        gp = gp_ref[...]
        merged = (_sig(gp[:, :D]) * pa_ref[...] + _sig(gp[:, D:]) * pb).astype(BF)
        mg_ref[...] = merged
        x1_ref[...] = x_ref[...] + jnp.dot(merged, wo_ref[...], preferred_element_type=F32)

    return _pallas_call(
        body, carry, name="merge_fwd", grid=(L // TM,),
        in_specs=[_tok(D), _tok(S5W), _tok(LW), _tok(2 * D), _full((NCHIP, S5W, AC)), _full((LW, D)), _full((D, D))],
        out_specs=[_tok(D), _tok(D), _tok(D), _tok(D)],
        out_shape=[_sds((L, D)), _sds((L, D)), _sds((L, D)), _far((L, D), BF)],
        compiler_params=_params(40),
    )(x, ya, yb, gp, w_a, w_b, w_o)


def _merge_bwd(dx1, gp, pa, pb, w_a, w_b, w_o, carry=None):
    L = dx1.shape[0]

    def body(dx1_ref, gp_ref, pa_ref, pb_ref, wa_ref, wb_ref, wo_ref, dya_ref, dyb_ref, dgp_ref, dpa_ref, dpb_ref):
        dm = _mm_nt(dx1_ref[...], wo_ref[...])
        gp = gp_ref[...]
        sa, sb = _sig(gp[:, :D]), _sig(gp[:, D:])
        dpa = (dm * sa).astype(BF)
        dpb = (dm * sb).astype(BF)
        dpa_ref[...] = dpa
        dpb_ref[...] = dpb
        dgp_ref[:, :D] = dm * pa_ref[...] * sa * (1.0 - sa)
        dgp_ref[:, D:] = dm * pb_ref[...] * sb * (1.0 - sb)
        dya = jnp.zeros((TM, S5W), F32)
        for k in range(NCHIP):
            dya = dya + _mm_nt(dpa[:, k * AC:(k + 1) * AC], wa_ref[k])
        dya_ref[...] = dya
        dyb_ref[...] = _mm_nt(dpb, wb_ref[...])

    return _pallas_call(
        body, carry, name="merge_bwd", grid=(L // TM,),
        in_specs=[_tok(D), _tok(2 * D), _tok(D), _tok(D), _full((NCHIP, S5W, AC)), _full((LW, D)), _full((D, D))],
        out_specs=[_tok(S5W), _tok(LW), _tok(2 * D), _tok(D), _tok(D)],
        out_shape=[_far((L, S5W)), _far((L, LW)), _sds((L, 2 * D)), _far((L, D), BF), _far((L, D), BF)],
        compiler_params=_params(40),
    )(dx1, gp, pa, pb, w_a, w_b, w_o)


def _chunk_tok(width):
    return pl.BlockSpec((NCHIP, TM, width), lambda i: (0, i, 0))


def _ffn_fwd(x1, g_ffn, wg, wu, wd, carry=None):
    L = x1.shape[0]

    def body(x_ref, g_ref, wg_hbm, wu_hbm, wd_hbm, x2_ref, h2_ref, gg_ref, uu_ref, wg_vm, wu_vm, wd_vm, w_sems):
        _resident_now([(src.at[c], dst.at[c]) for c in range(NCHIP)
                       for src, dst in ((wg_hbm, wg_vm), (wu_hbm, wu_vm), (wd_hbm, wd_vm))], w_sems)
        x = x_ref[...]
        xh, _ = _rms(x)
        h2 = (xh * g_ref[...]).astype(BF)
        h2_ref[...] = h2
        out = x
        for c in range(NCHIP):
            gg = lax.dot_general(h2, wg_vm[c], (((1,), (1,)), ((), ())), preferred_element_type=F32)
            uu = lax.dot_general(h2, wu_vm[c], (((1,), (1,)), ((), ())), preferred_element_type=F32)
            gg_ref[c] = gg.astype(BF)
            uu_ref[c] = uu.astype(BF)
            act = (gg * _sig(gg) * uu).astype(BF)
            out = out + jnp.dot(act, wd_vm[c], preferred_element_type=F32)
        x2_ref[...] = out

    return _pallas_call(
        body, carry, name="ffn_fwd", grid=(L // TM,),
        in_specs=[_tok(D), _full((1, D)), ANY, ANY, ANY],
        out_specs=[_tok(D), _tok(D), _chunk_tok(FC), _chunk_tok(FC)],
        out_shape=[_sds((L, D)), _sds((L, D), BF), _sds((NCHIP, L, FC), BF), _sds((NCHIP, L, FC), BF)],
        scratch_shapes=[pltpu.VMEM((NCHIP, FC, D), BF)] * 3 + [pltpu.SemaphoreType.DMA((3 * NCHIP,))],
        compiler_params=_params(52),
    )(x1, g_ffn, wg, wu, wd)


def _ffn_bwd(x1, dx2, gg, uu, g_ffn, wg, wu, wd, carry=None):
    L = x1.shape[0]

    def body(x_ref, dx2_ref, gg_ref, uu_ref, g_ref, wg_hbm, wu_hbm, wd_hbm,
             dx1_ref, act_ref, dgg_ref, duu_ref, dg_ref, wg_vm, wu_vm, wd_vm, w_sems):
        _resident_now([(src.at[c], dst.at[c]) for c in range(NCHIP)
                       for src, dst in ((wg_hbm, wg_vm), (wu_hbm, wu_vm), (wd_hbm, wd_vm))], w_sems)

        @pl.when(pl.program_id(0) == 0)
        def _():
            dg_ref[...] = jnp.zeros_like(dg_ref)

        dx2 = dx2_ref[...]
        dx2b = dx2.astype(BF)
        dh2 = jnp.zeros((TM, D), F32)
        for c in range(NCHIP):
            g = gg_ref[c].astype(F32)
            u = uu_ref[c].astype(F32)
            s = _sig(g)
            silu = g * s
            act_ref[c] = (silu * u).astype(BF)
            dact = lax.dot_general(dx2b, wd_vm[c], (((1,), (1,)), ((), ())), preferred_element_type=F32)
            dg = (dact * u * s * (1.0 + g * (1.0 - s))).astype(BF)
            du = (dact * silu).astype(BF)
            dgg_ref[c] = dg
            duu_ref[c] = du
            dh2 = dh2 + jnp.dot(dg, wg_vm[c], preferred_element_type=F32)
            dh2 = dh2 + jnp.dot(du, wu_vm[c], preferred_element_type=F32)
        xh, r = _rms(x_ref[...])
        dg_ref[0:1, :] += _colsum(dh2 * xh)
        dx1_ref[...] = dx2 + _rms_bwd(dh2, xh, r, g_ref[...])

    return _pallas_call(
        body, carry, name="ffn_bwd", grid=(L // TM,),
        in_specs=[_tok(D), _tok(D), _chunk_tok(FC), _chunk_tok(FC), _full((1, D)), ANY, ANY, ANY],
        out_specs=[_tok(D), _chunk_tok(FC), _chunk_tok(FC), _chunk_tok(FC), _full((SUB, D))],
        out_shape=[_sds((L, D)), _sds((NCHIP, L, FC), BF), _sds((NCHIP, L, FC), BF), _sds((NCHIP, L, FC), BF),
                   _sds((SUB, D))],
        scratch_shapes=[pltpu.VMEM((NCHIP, FC, D), BF)] * 3 + [pltpu.SemaphoreType.DMA((3 * NCHIP,))],
        compiler_params=_params(56),
    )(x1, dx2, gg, uu, g_ffn, wg, wu, wd)


def _ple_loss(x2, p, tgt, g_pg, w_pg, b_pg, w_ple, g_ple, g_final):
    L = x2.shape[0]

    def body(x2_ref, p_ref, t_ref, gpg_ref, wpg_ref, bpg_ref, wple_ref, gple_ref, gf_ref,
             dx2_ref, n2_ref, dpre_ref, de0_ref, acc_ref):
        @pl.when(pl.program_id(0) == 0)
        def _():
            acc_ref[...] = jnp.zeros_like(acc_ref)

        x2 = x2_ref[...]
        x2h, r2 = _rms(x2)
        n2 = (x2h * gpg_ref[...]).astype(BF)
        n2_ref[...] = n2
        gate = _sig(jnp.dot(n2, wpg_ref[...], preferred_element_type=F32) + bpg_ref[...])
        pb = p_ref[...].astype(BF)
        e0 = jnp.concatenate([jnp.dot(pb, wple_ref[k], preferred_element_type=F32) for k in range(NCHIP)], axis=1)
        e0h, re = _rms(e0)
        e = e0h * gple_ref[...]
        x3 = x2 + gate * e
        x3h, r3 = _rms(x3)
        diff = x3h * gf_ref[...] - t_ref[...]
        acc_ref[4:5, :] += _colsum(diff * diff) * (0.5 / D)
        dy = diff * (1.0 / D)
        acc_ref[3:4, :] += _colsum(dy * x3h)
        dx3 = _rms_bwd(dy, x3h, r3, gf_ref[...])
        de = dx3 * gate
        acc_ref[2:3, :] += _colsum(de * e0h)
        de0_ref[...] = _rms_bwd(de, e0h, re, gple_ref[...]).astype(BF)
        dpre = dx3 * e * gate * (1.0 - gate)
        acc_ref[1:2, :] += _colsum(dpre)
        dpreb = dpre.astype(BF)
        dpre_ref[...] = dpreb
        dn2 = lax.dot_general(dpreb, wpg_ref[...], (((1,), (1,)), ((), ())), preferred_element_type=F32)
        acc_ref[0:1, :] += _colsum(dn2 * x2h)
        dx2_ref[...] = dx3 + _rms_bwd(dn2, x2h, r2, gpg_ref[...])

    return _pallas_call(
        body, name="ple_loss", grid=(L // TM,),
        in_specs=[_tok(D), _tok(PLE), _tok(D), _full((1, D)), _full((D, D)), _full((1, D)), _full((NCHIP, PLE, AC)),
                  _full((1, D)), _full((1, D))],
        out_specs=[_tok(D), _tok(D), _tok(D), _tok(D), _full((SUB, D))],
        out_shape=[_sds((L, D)), _sds((L, D), BF), _sds((L, D), BF), _sds((L, D), BF), _sds((SUB, D))],
        compiler_params=_params(40),
    )(x2, p, tgt, g_pg, w_pg, b_pg, w_ple, g_ple, g_final)


def _tn(name, a, b, col_chunk=None, a_block=None, carry=None):
    L = a.shape[-2]
    m, n = a.shape[-1], b.shape[-1]
    a_col = 0
    if a_block is not None:
        a_col, m = a_block
    tk = L if (a.ndim == 3 or b.ndim == 3 or a_block is not None) else TK
    if a.ndim == 3 or b.ndim == 3:
        nj, bn = (a if a.ndim == 3 else b).shape[0], n
        a_spec = (pl.BlockSpec((None, tk, m), lambda j, t: (j, t, 0)) if a.ndim == 3
                  else pl.BlockSpec((tk, m), lambda j, t: (t, 0)))
        b_spec = (pl.BlockSpec((None, tk, n), lambda j, t: (j, t, 0)) if b.ndim == 3
                  else pl.BlockSpec((tk, n), lambda j, t: (t, 0)))
        out_spec, out_shape = pl.BlockSpec((None, m, n), lambda j, t: (j, 0, 0)), _sds((nj, m, n))
    else:
        bn = col_chunk
        if bn is None:
            bn = next((cand for cand in (1024, 512) if n > cand and n % cand == 0), n)
        nj = n // bn
        a_spec = pl.BlockSpec((tk, m), lambda j, t: (t, a_col))
        b_spec = pl.BlockSpec((tk, bn), lambda j, t: (t, j))
        if col_chunk is None:
            out_spec, out_shape = pl.BlockSpec((m, bn), lambda j, t: (0, j)), _sds((m, n))
        else:
            out_spec, out_shape = pl.BlockSpec((None, m, bn), lambda j, t: (j, 0, 0)), _sds((nj, m, bn))

    def body(a_ref, b_ref, o_ref):
        if tk == L:
            o_ref[...] = _mm_tn(a_ref[...], b_ref[...])
        else:
            @pl.when(pl.program_id(1) == 0)
            def _():
                o_ref[...] = jnp.zeros_like(o_ref)

            o_ref[...] += _mm_tn(a_ref[...], b_ref[...])

    outs = _pallas_call(
        body, carry, name=name, grid=(nj, L // tk), in_specs=[a_spec, b_spec], out_specs=[out_spec],
        out_shape=[pltpu.HBM(out_shape.shape, out_shape.dtype)],
        compiler_params=pltpu.CompilerParams(dimension_semantics=("arbitrary", "arbitrary"),
                                             vmem_limit_bytes=(30 if tk == L else 28) * VMEM_MB),
    )(*(_in_hbm([a, b]) if tk == L else (a, b)))
    return outs[0] if carry is None else outs


LANE = 128


def _tn_blocks(name, a, bs, ga, gb, carry=None):
    L, m, n, nb = a.shape[0], a.shape[1], bs[0].shape[1], len(bs)
    per = LANE // ga
    wb = per * gb
    n_super = m // LANE

    def body(a_ref, *refs):
        b_refs, o_refs, acc_refs = refs[:nb], refs[nb:2 * nb], refs[2 * nb:]
        t = pl.program_id(0)

        @pl.when(t == 0)
        def _():
            for acc in acc_refs:
                acc[...] = jnp.zeros_like(acc)

        lhs = a_ref[...].astype(BF)
        for b_ref, acc in zip(b_refs, acc_refs):
            rhs = b_ref[...].astype(BF)
            for j in range(n_super):
                acc[j] += _mm_tn(lhs[:, j * LANE:(j + 1) * LANE], rhs[:, j * wb:(j + 1) * wb])

        @pl.when(t == L // TK - 1)
        def _():
            own = (lax.broadcasted_iota(jnp.int32, (LANE, wb), 0) // ga) == (lax.broadcasted_iota(jnp.int32, (LANE, wb), 1) // gb)
            for o_ref, acc in zip(o_refs, acc_refs):
                for j in range(n_super):
                    kept = jnp.where(own, acc[j], 0.0)
                    o_ref[:, j * wb:(j + 1) * wb] = jnp.sum(kept.reshape(per, ga, wb), axis=0)

    outs = _pallas_call(
        body, carry, name=name, grid=(L // TK,),
        in_specs=[pl.BlockSpec((TK, m), lambda t: (t, 0))] + [pl.BlockSpec((TK, n), lambda t: (t, 0))] * nb,
        out_specs=[_full((ga, n))] * nb, out_shape=[_sds((ga, n))] * nb,
        scratch_shapes=[pltpu.VMEM((n_super, LANE, wb), F32)] * nb,
        compiler_params=_params(48),
    )(*_in_hbm([a] + list(bs)))
    return list(outs)


def _s5_discretize(lam_re, lam_im, log_dt, b_re, b_im):
    dt = jnp.exp(log_dt)[:, None]
    mag = jnp.exp(lam_re * dt)
    ar = mag * jnp.cos(lam_im * dt)
    ai = mag * jnp.sin(lam_im * dt)
    den = lam_re * lam_re + lam_im * lam_im
    nr = ar - 1.0
    fr = (nr * lam_re + ai * lam_im) / den
    fi = (ai * lam_re - nr * lam_im) / den
    bbr = fr[:, None, :] * b_re - fi[:, None, :] * b_im
    bbi = fr[:, None, :] * b_im + fi[:, None, :] * b_re
    return ar, ai, bbr, bbi


def _prepare(by_rows, block_cols, ar, ai):
    n = len(by_rows)

    def body(*refs):
        srcs, (ar_ref, ai_ref), dense, (con_ref, rev_ref) = refs[:n], refs[n:n + 2], refs[n + 2:2 * n + 2], refs[2 * n + 2:]
        for src, out, c in zip(srcs, dense, block_cols):
            r = src.shape[0]
            per = LANE // r
            wide = per * c
            own = (lax.broadcasted_iota(jnp.int32, (LANE, wide), 0) // r) == (lax.broadcasted_iota(jnp.int32, (LANE, wide), 1) // c)
            for j in range(out.shape[0]):
                tiled = jnp.broadcast_to(src[:, j * wide:(j + 1) * wide][None], (per, r, wide)).reshape(LANE, wide)
                out[j] = jnp.where(own, tiled, 0.0).astype(BF)
        a_r, a_i = ar_ref[...], ai_ref[...]
        pw = [(jnp.ones_like(a_r), jnp.zeros_like(a_i))]
        for _ in range(SUB):
            pr, pi = pw[-1]
            pw.append((pr * a_r - pi * a_i, pr * a_i + pi * a_r))
        row = _row_iota(GN)
        for ref, reverse in ((con_ref, False), (rev_ref, True)):
            sign = -1.0 if reverse else 1.0
            for j, sh in enumerate((1, 2, 4)):
                keep = (row < SUB - sh) if reverse else (row >= sh)
                ref[2 * j * SUB:(2 * j + 1) * SUB, :] = jnp.where(keep, pw[sh][0], 0.0)
                ref[(2 * j + 1) * SUB:(2 * j + 2) * SUB, :] = jnp.where(keep, sign * pw[sh][1], 0.0)
            p_r, p_i = jnp.zeros((SUB, GN), F32), jnp.zeros((SUB, GN), F32)
            for i in range(SUB):
                k = SUB - i if reverse else i + 1
                p_r = jnp.where(row == i, pw[k][0], p_r)
                p_i = jnp.where(row == i, sign * pw[k][1], p_i)
            ref[6 * SUB:7 * SUB, :] = p_r
            ref[7 * SUB:8 * SUB, :] = p_i

    dense_shapes = [(b.shape[1] // (LANE // b.shape[0] * c), LANE, LANE // b.shape[0] * c)
                    for b, c in zip(by_rows, block_cols)]
    outs = _pallas_call(
        body, name="prepare", grid=(1,), in_specs=[_full(b.shape) for b in by_rows] + [_full((1, GN))] * 2,
        out_specs=[_full(s) for s in dense_shapes] + [_full((8 * SUB, GN))] * 2,
        out_shape=[_far(s, BF) for s in dense_shapes] + [_sds((8 * SUB, GN)), _far((8 * SUB, GN))],
        compiler_params=_params(48),
    )(*by_rows, ar, ai)
    return outs[:n], outs[n], outs[n + 1]


def _local_step(x, p, tgt, w, comm):
    rows_of = lambda a: a.reshape(NCHIP * a.shape[1], a.shape[2])
    quarters = lambda a: a.reshape(NCHIP, a.shape[0] // NCHIP, a.shape[1])

    def gathering(names, call):
        carry = comm.gather(names)
        outs = list(call(carry))
        own = len(outs) - len(carry.out_shapes)
        w.update(zip(names, outs[own:]))
        return outs[:own]

    w.update(comm.first())
    w_glu = rows_of(w["w_glu"])
    ar, ai, bbr, bbi = _s5_discretize(w["lam_re"], w["lam_im"], w["log_dt"], w["s5_b_re"], w["s5_b_im"])
    by_row = lambda b: jnp.transpose(b, (1, 0, 2)).reshape(b.shape[1], -1)
    (bbr_d, bbi_d, ccr_d, cci_d, wr_d, wi_d), con, con_rev = _prepare(
        [by_row(b) for b in (bbr, bbi, w["s5_c_re"], w["s5_c_im"], w["w_r"], w["w_i"])], [NS] * 4 + [HD] * 2,
        ar.reshape(1, GN), ai.reshape(1, GN))
    dsk = w["s5_d"].reshape(1, S5W)
    lam = w["lru_lambda"].reshape(1, LW)
    sp = jax.nn.softplus(-lam)
    b_r, b_i = w["b_r"].reshape(1, LW), w["b_i"].reshape(1, LW)
    row = lambda name: w[name].reshape(1, -1)

    h, ua, ub, gp = gathering(["w_a_out", "w_b_out"], lambda carry: _inproj_fwd(
        x, row("g_mix"), w["w_in"], row("b_in"), carry))
    sr, si, y, zg, ya = gathering(["w_o", "w_ffn_gate"], lambda carry: _s5_fwd(
        ua, bbr_d, bbi_d, ccr_d, cci_d, dsk, con, w_glu, row("b_glu"), carry))
    xc, rg, ig, yb, hp = gathering(["w_ffn_up"], lambda carry: _lru_fwd(
        ub, w["conv_w"], row("conv_b"), wr_d, wi_d, b_r, b_i, sp, carry))
    w_b_out, w_o = rows_of(w["w_b_out"]), rows_of(w["w_o"])
    x1, pa, pb, merged = gathering(["w_ffn_down"], lambda carry: _merge_fwd(
        x, ya, yb, gp, w["w_a_out"], w_b_out, w_o, carry))
    x2, h2, gg, uu = gathering(["w_ple_gate", "w_ple"], lambda carry: _ffn_fwd(
        x1, row("g_ffn"), w["w_ffn_gate"], w["w_ffn_up"], w["w_ffn_down"], carry))
    w_pg = rows_of(w["w_ple_gate"])
    dx2, n2, dpre, de0, acc_p = _ple_loss(x2, p, tgt, row("g_ple_gate"), w_pg, row("b_ple_gate"),
                                          w["w_ple"], row("g_ple"), row("g_final"))
    comm.reduce("ple", {"w_ple_gate": quarters(_tn("dw_ple_gate", n2, dpre)),
                        "w_ple": _tn("dw_ple", p, de0, col_chunk=AC)})
    dx1, act, dgg, duu, acc_f = comm.run(lambda carry: _ffn_bwd(
        x1, dx2, gg, uu, row("g_ffn"), w["w_ffn_gate"], w["w_ffn_up"], w["w_ffn_down"], carry))
    comm.reduce("ffn_gate", {"w_ffn_gate": _tn("dw_ffn_gate", dgg, h2)})
    comm.reduce("w_o", {"w_o": quarters(_tn("dw_o", *_in_hbm([merged, dx1])))})
    comm.reduce("ffn_up", {"w_ffn_up": comm.run(lambda carry: _tn("dw_ffn_up", duu, h2, carry=carry))[0]})
    comm.reduce("ffn_down", {"w_ffn_down": comm.run(lambda carry: _tn("dw_ffn_down", act, dx2, carry=carry),
                                                    hold=("ffn_gate", "w_o"))[0]})
    dya, dyb, dgp, dpa, dpb = comm.run(lambda carry: _merge_bwd(
        dx1, gp, pa, pb, w["w_a_out"], w_b_out, w_o, carry), hold=("ffn_gate", "ffn_up"))
    comm.reduce("merge", {"w_a_out": _tn("dw_a_out", ya, dpa, col_chunk=AC), "w_b_out": quarters(_tn("dw_b_out", yb, dpb))})
    dua, dq, dy, lr, li, acc_a, acc_s = comm.run(lambda carry: _s5_bwd(
        dya, y, ua, sr, si, bbr_d, bbi_d, ccr_d, cci_d, dsk, con_rev, w_glu, row("b_glu"), carry), hold=("ffn_down",))
    dub, dpr, dpi, acc_l = comm.run(lambda carry: _lru_bwd(
        dyb, xc, rg, ig, hp, ub, w["conv_w"], wr_d, wi_d, sp, -_sig(-lam), carry))
    gx, dz, acc_g, acc_b = _inproj_bwd(x, dx1, dua, dub, dgp, row("g_mix"), w["w_in"])
    half = (D // 2,)
    comm.reduce("in_lo", {"w_in_lo": comm.run(lambda carry: _tn(
        "dw_in_lo", h, dz, col_chunk=QC, a_block=(0,) + half, carry=carry))[0]})
    comm.reduce("in_hi", {"w_in_hi": comm.run(lambda carry: _tn(
        "dw_in_hi", h, dz, col_chunk=QC, a_block=(1,) + half, carry=carry))[0], "w_glu": quarters(_tn("dw_glu", zg, dq))})
    d_wr, d_wi = comm.run(lambda carry: _tn_blocks("dw_r_i", xc, [dpr, dpi], HD, HD, carry))
    d_bbr, d_bbi = comm.run(lambda carry: _tn_blocks("d_bb", ua, [lr, li], NP, NS, carry))
    d_ccr, d_cci = comm.run(lambda carry: _tn_blocks("d_cc", dy, [sr, si], NP, NS, carry))
    comm.drain()
    sums = {"ple": acc_p, "ffn": acc_f, "mix": acc_g, "b_in": acc_b, "lru": acc_l, "s5": acc_s, "s5_a": acc_a}
    blocks = {"bb_re": d_bbr, "bb_im": d_bbi,
              "cc_re": d_ccr, "cc_im": d_cci,
              "w_r": d_wr, "w_i": d_wi}
    return gx, sums, blocks


def _replicated_grads(w, sums, blocks):
    grouped = lambda e, groups: jnp.transpose(e.reshape(e.shape[0], groups, -1), (1, 0, 2))
    d_ar, d_ai = sums["s5_a"][0].reshape(NG, NS), sums["s5_a"][1].reshape(NG, NS)
    d_bbr, d_bbi = grouped(blocks["bb_re"], NG), grouped(blocks["bb_im"], NG)
    _, vjp = jax.vjp(_s5_discretize, w["lam_re"], w["lam_im"], w["log_dt"], w["s5_b_re"], w["s5_b_im"])
    g = dict(zip(("lam_re", "lam_im", "log_dt", "s5_b_re", "s5_b_im"), vjp((d_ar, d_ai, d_bbr, d_bbi))))
    g["s5_c_re"] = grouped(blocks["cc_re"], NG)
    g["s5_c_im"] = -grouped(blocks["cc_im"], NG)
    g["w_r"], g["w_i"] = grouped(blocks["w_r"], NH), grouped(blocks["w_i"], NH)
    g["s5_d"] = sums["s5"][0].reshape(NG, NP)
    g["b_r"] = sums["lru"][1].reshape(NH, HD)
    g["b_i"] = sums["lru"][2].reshape(NH, HD)
    return g


ACC_ROWS = {"g_mix": ("mix", 0), "b_in": ("b_in", 0), "g_ffn": ("ffn", 0), "g_ple_gate": ("ple", 0),
            "b_ple_gate": ("ple", 1), "g_ple": ("ple", 2), "g_final": ("ple", 3), "b_glu": ("s5", 1),
            "lru_lambda": ("lru", 0), "conv_b": ("lru", 3)}
LOSS_ROW = ("ple", 4)
CONV_W_ROWS = ("lru", 4)


SHARDED = [("w_in", (D, QC)), ("w_glu", (S5W // NCHIP, S5W)), ("w_a_out", (S5W, AC)), ("w_b_out", (LW // NCHIP, D)),
           ("w_o", (D // NCHIP, D)), ("w_ffn_gate", (FC, D)), ("w_ffn_up", (FC, D)), ("w_ffn_down", (FC, D)),
           ("w_ple_gate", (D // NCHIP, D)), ("w_ple", (PLE, AC))]
TRANSPOSED = ("w_ffn_gate", "w_ffn_up", "s5_b_re", "s5_b_im")
CONV_SHARD = (4, LW // NCHIP)


def _mesh_pos():
    return lax.axis_index("x"), lax.axis_index("y"), lax.axis_index("c")


def _other_chips(x, y):
    return [(1 - x, y), (x, 1 - y), (1 - x, 1 - y)]


def _half_rows(c, rows, align):
    return pl.ds(pl.multiple_of(c * (rows // 2), align), rows // 2)


def _run_now(name, carry):
    c_in, c_out = len(carry.operands), len(carry.out_shapes)

    def body(*refs):
        ins, outs, sems = refs[:c_in], refs[c_in:c_in + c_out], refs[c_in + c_out:]
        carry.start(ins, outs, sems)
        carry.finish(ins, outs, sems)

    return pl.pallas_call(body, name=name, in_specs=[ANY] * c_in, out_specs=[ANY] * c_out,
                          out_shape=list(carry.out_shapes), scratch_shapes=list(carry.sems),
                          input_output_aliases=dict(carry.aliases))(*_in_hbm(carry.operands))


def _gather_group(shards, split):
    n = len(shards)

    def copies(srcs, outs, sems):
        send_sems, recv_sems = sems
        x, y, c = _mesh_pos()
        k0 = 2 * x + y
        sib = (x, y, 1 - c)
        chips = _other_chips(x, y)

        def remote(src, dst, j, i, to):
            return pltpu.make_async_remote_copy(src_ref=src, dst_ref=dst, send_sem=send_sems.at[j, i],
                                                recv_sem=recv_sems.at[j, i], device_id=to, device_id_type=MESH)

        def rows(ref, i, core, *lead):
            if not split[i]:
                return ref.at[lead] if lead else ref
            return ref.at[(*lead, _half_rows(core, shards[i].shape[0], 16))]

        own = [remote(s, o.at[k0], 6, i, sib) for i, (s, o) in enumerate(zip(srcs, outs))]
        ici, landed, fwd, fwd_landed = [], [], [], []
        for j, chip in enumerate(chips):
            kj = 2 * chip[0] + chip[1]
            pairs = list(enumerate(zip(srcs, outs)))
            ici.append([remote(rows(s, i, c), rows(o, i, c, k0), j, i, (*chip, c)) for i, (s, o) in pairs])
            landed.append([remote(rows(s, i, c), rows(o, i, c, kj), j, i, (*chip, c)) for i, (s, o) in pairs])
            fwd.append([remote(rows(o, i, c, kj), rows(o, i, c, kj), 3 + j, i, sib) for i, (s, o) in pairs if split[i]])
            fwd_landed.append([remote(rows(o, i, 1 - c, kj), rows(o, i, 1 - c, kj), 3 + j, i, sib)
                               for i, (s, o) in pairs if split[i]])
        return own, ici, landed, fwd, fwd_landed

    def start(srcs, outs, sems):
        own, ici, _, _, _ = copies(srcs, outs, sems)
        for cp in own + [cp for per_chip in ici for cp in per_chip]:
            cp.start()

    def finish(srcs, outs, sems):
        own, ici, landed, fwd, fwd_landed = copies(srcs, outs, sems)
        passed = [i for i in range(n) if split[i]]
        for j in range(3):
            for i, cp in enumerate(landed[j]):
                cp.wait_recv()
                if split[i]:
                    fwd[j][passed.index(i)].start()
        for j in range(3):
            for cp in fwd_landed[j]:
                cp.wait_recv()
        for cp in own:
            cp.wait_recv()
        for cp in own + [cp for per_chip in ici + fwd for cp in per_chip]:
            cp.wait_send()

    return _Carried(shards, [_far((NCHIP,) + s.shape, s.dtype) for s in shards],
                    [pltpu.SemaphoreType.DMA((7, n)), pltpu.SemaphoreType.DMA((7, n))], start, finish)


def _each_copy(copies, carried, out_shapes, sems, aliases=None):
    def start(ins, outs, sem_refs):
        for cp in copies(ins, outs, sem_refs):
            cp.start()

    def finish(ins, outs, sem_refs):
        for cp in copies(ins, outs, sem_refs):
            cp.wait()

    return _Carried(carried, out_shapes, sems, start, finish, aliases)


def _swap_group(grads):
    n = len(grads)

    def copies(srcs, outs, sems):
        send_sems, recv_sems = sems
        x, y, c = _mesh_pos()
        return [pltpu.make_async_remote_copy(src_ref=s.at[:, _half_rows(1 - c, s.shape[1], 8)], dst_ref=o,
                                             send_sem=send_sems.at[i], recv_sem=recv_sems.at[i], device_id=(x, y, 1 - c),
                                             device_id_type=MESH) for i, (s, o) in enumerate(zip(srcs, outs))]

    return _each_copy(copies, grads, [pltpu.HBM((NCHIP, g.shape[1] // 2, g.shape[2]), F32) for g in grads],
                      [pltpu.SemaphoreType.DMA((n,)), pltpu.SemaphoreType.DMA((n,))])


def _add_sibling_group(tag, kc_idx, grads, gots):
    n = len(grads)

    def body(kc_ref, *refs):
        for g, rx, p, pb in zip(refs[:n], refs[n:2 * n], refs[2 * n:3 * n], refs[3 * n:]):
            s = g[...] + rx[...]
            pb[...] = s.astype(BF)

            @pl.when(pl.program_id(0) == kc_ref[0])
            def _():
                p[...] = s

    halves = [pl.BlockSpec((None,) + rx.shape[1:], lambda k, kc_ref: (k, 0, 0)) for rx in gots]
    mine = [pl.BlockSpec((None,) + rx.shape[1:], lambda k, kc_ref: (k, kc_ref[1], 0)) for rx in gots]
    own = [pl.BlockSpec(rx.shape[1:], lambda k, kc_ref: (0, 0)) for rx in gots]
    outs = _pallas_call(
        body, name="add_sibling_" + tag,
        grid_spec=pltpu.PrefetchScalarGridSpec(num_scalar_prefetch=1, grid=(NCHIP,), in_specs=mine + halves,
                                               out_specs=own + halves),
        out_shape=[pltpu.HBM(rx.shape[1:], F32) for rx in gots] + [pltpu.HBM(rx.shape, BF) for rx in gots],
        compiler_params=_params(48),
    )(kc_idx, *_in_hbm(list(grads) + list(gots)))
    return outs[:n], outs[n:]


def _exchange_group(parts):
    n = len(parts)

    def copies(srcs, outs, sems):
        send_sems, recv_sems = sems
        x, y, c = _mesh_pos()
        return [pltpu.make_async_remote_copy(
            src_ref=s.at[2 * chip[0] + chip[1]], dst_ref=o.at[j], send_sem=send_sems.at[j, i],
            recv_sem=recv_sems.at[j, i], device_id=(*chip, c), device_id_type=MESH)
            for j, chip in enumerate(_other_chips(x, y)) for i, (s, o) in enumerate(zip(srcs, outs))]

    return _each_copy(copies, parts, [pltpu.HBM((3,) + p.shape[1:], BF) for p in parts],
                      [pltpu.SemaphoreType.DMA((3, n)), pltpu.SemaphoreType.DMA((3, n))])


def _add_chips_group(tag, kc_idx, parts, arrived):
    n = len(parts)

    def body(kc_ref, *refs):
        for p, rx, t in zip(refs[:n], refs[n:2 * n], refs[2 * n:]):
            t[...] = ((p[...] + rx[0].astype(F32)) + rx[1].astype(F32)) + rx[2].astype(F32)

    outs = _pallas_call(
        body, name="add_chips_" + tag,
        grid_spec=pltpu.PrefetchScalarGridSpec(
            num_scalar_prefetch=1, grid=(1,),
            in_specs=([pl.BlockSpec(rx.shape[1:], lambda i, kc_ref: (0, 0)) for rx in arrived]
                      + [pl.BlockSpec(rx.shape, lambda i, kc_ref: (0, 0, 0)) for rx in arrived]),
            out_specs=[pl.BlockSpec((None,) + rx.shape[1:], lambda i, kc_ref: (kc_ref[1], 0, 0)) for rx in arrived]),
        out_shape=[pltpu.HBM((2,) + rx.shape[1:], F32) for rx in arrived],
        compiler_params=_params(48),
    )(kc_idx, *_in_hbm(list(parts) + list(arrived)))
    return list(outs)


def _join_group(halves):
    n = len(halves)

    def copies(bufs, sems):
        send_sems, recv_sems = sems
        x, y, c = _mesh_pos()
        sib = (x, y, 1 - c)
        sends = [pltpu.make_async_remote_copy(src_ref=b.at[c], dst_ref=b.at[c], send_sem=send_sems.at[i],
                                              recv_sem=recv_sems.at[i], device_id=sib, device_id_type=MESH)
                 for i, b in enumerate(bufs)]
        landed = [pltpu.make_async_remote_copy(src_ref=b.at[c], dst_ref=b.at[1 - c], send_sem=send_sems.at[i],
                                               recv_sem=recv_sems.at[i], device_id=sib, device_id_type=MESH)
                  for i, b in enumerate(bufs)]
        return sends, landed

    def start(_, bufs, sems):
        for cp in copies(bufs, sems)[0]:
            cp.start()

    def finish(_, bufs, sems):
        sends, landed = copies(bufs, sems)
        for cp in landed:
            cp.wait_recv()
        for cp in sends:
            cp.wait_send()

    return _Carried(halves, [pltpu.HBM(h.shape, F32) for h in halves],
                    [pltpu.SemaphoreType.DMA((n,)), pltpu.SemaphoreType.DMA((n,))], start, finish,
                    {i: i for i in range(n)})


def _combine(carries):
    operands, out_shapes, sems, aliases, spans = [], [], [], {}, []
    for c in carries:
        aliases.update({len(operands) + i: len(out_shapes) + o for i, o in c.aliases.items()})
        spans.append((len(operands), len(out_shapes), len(sems)))
        operands += list(c.operands)
        out_shapes += list(c.out_shapes)
        sems += list(c.sems)

    def each(phase):
        def run(ins, outs, sem_refs):
            for c, (a, b, s) in zip(carries, spans):
                getattr(c, phase)(ins[a:a + len(c.operands)], outs[b:b + len(c.out_shapes)], sem_refs[s:s + len(c.sems)])
        return run

    return _Carried(operands, out_shapes, sems, each("start"), each("finish"), aliases)


def _allreduce_small(arrays, wire):
    n = len(arrays)
    halves = [(a.shape[0], a.shape[1] // 2) for a in arrays]

    def body(*refs):
        srcs, outs = refs[:n], refs[n:2 * n]
        mine_bufs, sib_bufs, chip_bufs, total_bufs = (refs[k * n:(k + 1) * n] for k in range(2, 6))
        send_sems, recv_sems, local_sems = refs[6 * n:]
        x, y, c = _mesh_pos()
        k0 = 2 * x + y
        sib = (x, y, 1 - c)

        def remote(src, dst, j, i, to):
            return pltpu.make_async_remote_copy(src_ref=src, dst_ref=dst, send_sem=send_sems.at[j, i],
                                                recv_sem=recv_sems.at[j, i], device_id=to, device_id_type=MESH)

        def cols(ref, i, core):
            return ref.at[:, pl.ds(pl.multiple_of(core * halves[i][1], LANE), halves[i][1])]

        swaps = [remote(cols(s, i, 1 - c), b, 0, i, sib) for i, (s, b) in enumerate(zip(srcs, sib_bufs))]
        own = [pltpu.make_async_copy(cols(s, i, c), m, local_sems.at[i]) for i, (s, m) in enumerate(zip(srcs, mine_bufs))]
        for cp in swaps + own:
            cp.start()
        for cp in swaps + own:
            cp.wait()
        for m, b, buf in zip(mine_bufs, sib_bufs, chip_bufs):
            buf[k0] = (m[...] + b[...]).astype(buf.dtype)
        chips = _other_chips(x, y)
        sends = [remote(buf.at[k0], buf.at[k0], 1 + j, i, (*chip, c))
                 for j, chip in enumerate(chips) for i, buf in enumerate(chip_bufs)]
        for cp in sends:
            cp.start()
        for j, chip in enumerate(chips):
            for i, buf in enumerate(chip_bufs):
                remote(buf.at[k0], buf.at[2 * chip[0] + chip[1]], 1 + j, i, (*chip, c)).wait_recv()
        for cp in sends:
            cp.wait_send()
        for t, buf in zip(total_bufs, chip_bufs):
            t[...] = ((buf[0].astype(F32) + buf[1].astype(F32)) + buf[2].astype(F32)) + buf[3].astype(F32)
        joins = [remote(t, cols(o, i, c), 4, i, sib) for i, (t, o) in enumerate(zip(total_bufs, outs))]
        keep = [pltpu.make_async_copy(t, cols(o, i, c), local_sems.at[i]) for i, (t, o) in enumerate(zip(total_bufs, outs))]
        for cp in joins + keep:
            cp.start()
        for i, (t, o) in enumerate(zip(total_bufs, outs)):
            remote(t, cols(o, i, 1 - c), 4, i, sib).wait_recv()
        for cp in joins:
            cp.wait_send()
        for cp in keep:
            cp.wait()

    specs = [_full(a.shape) for a in arrays]
    return _pallas_call(
        body, name="allreduce_small", grid=(1,), in_specs=specs, out_specs=specs,
        out_shape=[_sds(a.shape) for a in arrays],
        scratch_shapes=([pltpu.VMEM(h, F32) for h in halves] + [pltpu.VMEM(h, F32) for h in halves]
                        + [pltpu.VMEM((NCHIP,) + h, dt) for h, dt in zip(halves, wire)] + [pltpu.VMEM(h, F32) for h in halves]
                        + [pltpu.SemaphoreType.DMA((5, n)), pltpu.SemaphoreType.DMA((5, n)), pltpu.SemaphoreType.DMA((n,))]),
        compiler_params=_params(32),
    )(*arrays)


def _adamw_terms(w, g, m, v):
    m = ADAM_B1 * m + (1.0 - ADAM_B1) * g
    v = ADAM_B2 * v + (1.0 - ADAM_B2) * jnp.square(g)
    m_hat = m / (1.0 - ADAM_B1 ** ADAM_STEP)
    v_hat = v / (1.0 - ADAM_B2 ** ADAM_STEP)
    return -ADAM_LR * (m_hat / (jnp.sqrt(v_hat) + ADAM_EPS) + ADAM_WD * w), m, v


ADAM_STEPS = 4


def _adamw_group(tag, ws, gs, ms, vs):
    n = len(ws)

    def body(*refs):
        ins, outs = refs[:4 * n], refs[4 * n:]
        for i in range(n):
            w, g, m, v = (ins[k * n + i][...] for k in range(4))
            outs[i][...] = g
            outs[n + i][...], outs[2 * n + i][...], outs[3 * n + i][...] = _adamw_terms(w, g, m, v)

    specs = [pl.BlockSpec((w.shape[0] // ADAM_STEPS, w.shape[1]), lambda i: (i, 0)) for w in ws]
    outs = _pallas_call(
        body, name="adamw_" + tag, grid=(ADAM_STEPS,), in_specs=specs * 4, out_specs=specs * 4,
        out_shape=[_sds(w.shape) for w in ws] * 4, compiler_params=_params(48),
    )(*_in_hbm(list(ws) + list(gs) + list(ms) + list(vs)))
    return outs[:n], outs[n:2 * n], outs[2 * n:3 * n], outs[3 * n:]


def _adamw_replicated(sums, row_of, direct):
    ns, nr, nd = len(sums), len(row_of), len(direct)

    def body(*refs):
        sum_refs = refs[:ns]
        ins = refs[ns:ns + 3 * nr + 4 * nd]
        outs = refs[ns + 3 * nr + 4 * nd:]
        for i, (_, _, _, si, row) in enumerate(row_of):
            w_ref, m_ref, v_ref = ins[3 * i:3 * i + 3]
            g = sum_refs[si][row:row + 1, :]
            outs[4 * i][...] = g
            outs[4 * i + 1][...], outs[4 * i + 2][...], outs[4 * i + 3][...] = _adamw_terms(w_ref[...], g, m_ref[...], v_ref[...])
        for i in range(nd):
            w_ref, m_ref, v_ref, g_ref = ins[3 * nr + 4 * i:3 * nr + 4 * i + 4]
            o = outs[4 * (nr + i):4 * (nr + i) + 4]
            g = g_ref[...]
            o[0][...] = g
            o[1][...], o[2][...], o[3][...] = _adamw_terms(w_ref[...], g, m_ref[...], v_ref[...])

    operands = list(sums)
    shapes = []
    for w, m, v, _, _ in row_of:
        operands += [w, m, v]
        shapes += [w.shape] * 4
    for w, m, v, g in direct:
        operands += [w, m, v, g]
        shapes += [w.shape] * 4
    flat = _pallas_call(
        body, name="adamw_replicated", grid=(1,), in_specs=[_full(a.shape) for a in operands],
        out_specs=[_full(s) for s in shapes], out_shape=[_sds(s) for s in shapes],
        compiler_params=_params(56),
    )(*operands)
    return [flat[4 * i:4 * i + 4] for i in range(nr + nd)]


class _Exchanges:
    def __init__(self, shards, conv_w, chip, core, apply):
        self.shards, self.conv_w, self.apply = shards, conv_w, apply
        self.active, self.calls = [], 0
        self.chip_core_idx = jnp.stack([chip, core]).astype(jnp.int32)

    def first(self):
        names = ["w_in", "w_glu"]
        got = _run_now("gather_first", _gather_group([self.shards[n] for n in names] + [self.conv_w],
                                                     [True, True, False]))
        out = dict(zip(names, got))
        out["conv_w"] = jnp.transpose(got[2], (1, 0, 2)).reshape(4, LW)
        return out

    def gather(self, names):
        return _gather_group([self.shards[n] for n in names], [True] * len(names))

    def reduce(self, tag, grads):
        self.active.append({"tag": tag, "names": list(grads), "stage": 0, "grads": list(grads.values())})

    def run(self, call, hold=()):
        groups = [g for g in self.active if g["tag"] not in hold]
        carries = [self._exchange_of(g) for g in groups]
        carry = _combine(carries)
        outs = list(call(carry))
        own = len(outs) - len(carry.out_shapes)
        landed = outs[own:]
        for g, c in zip(groups, carries):
            self._sum_after(g, landed[:len(c.out_shapes)])
            landed = landed[len(c.out_shapes):]
        self.active = [g for g in self.active if g["stage"] < 3]
        return outs[:own]

    def _exchange_of(self, g):
        if g["stage"] == 0:
            return _swap_group(g["grads"])
        if g["stage"] == 1:
            return _exchange_group(g["bf16"])
        return _join_group(g["halves"])

    def _sum_after(self, g, landed):
        if g["stage"] == 0:
            g["f32"], g["bf16"] = _add_sibling_group(g["tag"], self.chip_core_idx, g["grads"], landed)
        elif g["stage"] == 1:
            g["halves"] = _add_chips_group(g["tag"], self.chip_core_idx, g["f32"], landed)
        else:
            self.apply(g["tag"], g["names"], [t.reshape(2 * t.shape[1], t.shape[2]) for t in landed])
        g["stage"] += 1

    def drain(self):
        while self.active:
            self.calls += 1
            self.run(lambda carry: _run_now("reduce_%d" % self.calls, carry))


INPUT_NAMES = (["x", "p"] + [n for n in
               ["g_mix", "w_in", "b_in", "lam_re", "lam_im", "log_dt", "s5_b_re", "s5_b_im", "s5_c_re", "s5_c_im", "s5_d",
                "w_glu", "b_glu", "conv_w", "conv_b", "w_r", "b_r", "w_i", "b_i", "lru_lambda", "w_a_out", "w_b_out", "w_o",
                "g_ffn", "w_ffn_gate", "w_ffn_up", "w_ffn_down", "g_ple_gate", "w_ple_gate", "b_ple_gate", "w_ple", "g_ple",
                "g_final"]])
WEIGHT_NAMES = INPUT_NAMES[2:]


def kernel(*args):
    names = INPUT_NAMES + ["loss_target"] + ["m_" + n for n in WEIGHT_NAMES] + ["v_" + n for n in WEIGHT_NAMES]
    assert len(args) == len(names)
    given = dict(zip(names, args))

    def view(name):
        a = given[name]
        return jnp.swapaxes(a, -1, -2) if name.endswith(TRANSPOSED) else a

    def unview(name, a):
        return jnp.swapaxes(a, -1, -2) if name in TRANSPOSED else a

    def local(name):
        return view(name) if name.endswith("g_final") else view(name)[0]

    xi, yi, ci = _mesh_pos()
    k0 = 2 * xi + yi
    x, p, tgt = given["x"][0], given["p"][0, 0], given["loss_target"][0]

    results = {}

    row_halves = {}

    def apply(tag, names, totals):
        totals = dict(zip(names, totals))
        row_halves.update({n: totals.pop(n) for n in names if n in ("w_in_lo", "w_in_hi")})
        if len(row_halves) == 2:
            totals["w_in"] = jnp.concatenate([row_halves.pop("w_in_lo"), row_halves.pop("w_in_hi")])
        names = list(totals)
        if not names:
            return
        new = _adamw_group(tag, [local(n) for n in names], list(totals.values()), [local("m_" + n) for n in names],
                           [local("v_" + n) for n in names])
        for kind, arrays in zip(("grad", "delta", "new_m", "new_v"), new):
            for n, arr in zip(names, arrays):
                results[kind, n] = unview(n, arr[None])

    comm = _Exchanges({n: local(n).astype(BF) for n, _ in SHARDED}, local("conv_w"), k0, ci, apply)
    w = {n: local(n) for n in WEIGHT_NAMES if n != "conv_w" and n not in dict(SHARDED)}
    gx, sums, blocks = _local_step(x, p, tgt, w, comm)

    sum_names, block_names = list(sums), list(blocks)
    red = _allreduce_small([sums[n] for n in sum_names] + [blocks[n] for n in block_names],
                           [F32] * len(sum_names) + [BF] * len(block_names))
    sums = dict(zip(sum_names, red[:len(sum_names)]))
    blocks = dict(zip(block_names, red[len(sum_names):]))
    loss = jnp.sum(sums[LOSS_ROW[0]][LOSS_ROW[1]])
    direct_g = _replicated_grads(w, sums, blocks)
    conv_rows = sums[CONV_W_ROWS[0]][CONV_W_ROWS[1]:CONV_W_ROWS[1] + 4]
    direct_g["conv_w"] = lax.dynamic_slice(conv_rows, (0, k0 * CONV_SHARD[1]), CONV_SHARD)
    as_row = lambda a: a.reshape(1, -1)
    row_names = list(ACC_ROWS)
    row_of = [(as_row(given[n]), as_row(given["m_" + n]), as_row(given["v_" + n]),
               sum_names.index(ACC_ROWS[n][0]), ACC_ROWS[n][1]) for n in row_names]
    direct_names = list(direct_g)
    direct = [(view(n), view("m_" + n), view("v_" + n), direct_g[n].reshape(view(n).shape)) for n in direct_names]
    done = _adamw_replicated([sums[n] for n in sum_names], row_of, direct)
    for n, four in zip(row_names + direct_names, done):
        for kind, arr in zip(("grad", "delta", "new_m", "new_v"), four):
            results[kind, n] = unview(n, arr).reshape(given[n].shape)

    out = [loss, gx[None]]
    for kind in ("grad", "delta", "new_m", "new_v"):
        out += [results[kind, n] for n in WEIGHT_NAMES]
    return tuple(out)
```

```python
import functools
import math

import jax
import jax.numpy as jnp
from jax import lax
from jax.experimental import pallas as pl
from jax.experimental.pallas import tpu as pltpu

F32 = jnp.float32
BF = jnp.bfloat16

D = 1024
S5W = 512
NG, NS, NP = 32, 64, 16
GN = NG * NS
LW = 1024
NH, HD = 16, 64
LRU_C = 8.0
FH = 2816
NCHIP = 4
FC = FH // NCHIP
PLE = 256
INC = S5W + LW + 2 * D
EPS = 1e-6
ADAM_LR, ADAM_B1, ADAM_B2, ADAM_EPS, ADAM_WD, ADAM_STEP = 0.001, 0.9, 0.999, 1e-08, 0.01, 10

TM = 256
TK = 1024
LC = 512
SUB = 8
VMEM_MB = 1024 * 1024
MESH = pl.DeviceIdType.MESH
ANY = pl.BlockSpec(memory_space=pl.ANY)


def _mm(a, b):
    return jnp.dot(a.astype(BF), b.astype(BF), preferred_element_type=F32)


def _mm_nt(a, b):
    return lax.dot_general(a.astype(BF), b.astype(BF), (((1,), (1,)), ((), ())), preferred_element_type=F32)


def _mm_tn(a, b):
    return lax.dot_general(a.astype(BF), b.astype(BF), (((0,), (0,)), ((), ())), preferred_element_type=F32)


def _blockdiag_mm(x, blocks_ref):
    n, rows, _ = blocks_ref.shape
    return jnp.concatenate([jnp.dot(x[:, j * rows:(j + 1) * rows], blocks_ref[j], preferred_element_type=F32)
                            for j in range(n)], axis=1)


def _blockdiag_mm_t(x, blocks_ref):
    n, _, wide = blocks_ref.shape
    return jnp.concatenate([lax.dot_general(x[:, j * wide:(j + 1) * wide], blocks_ref[j], (((1,), (1,)), ((), ())),
                                            preferred_element_type=F32) for j in range(n)], axis=1)


def _rms(x):
    r = lax.rsqrt(jnp.mean(x * x, axis=-1, keepdims=True) + EPS)
    return x * r, r


def _rms_bwd(dy, xh, r, g):
    dxh = dy * g
    return r * (dxh - xh * jnp.mean(dxh * xh, axis=-1, keepdims=True))


def _colsum(x):
    return jnp.sum(x, axis=0, keepdims=True)


def _sig(x):
    return jax.nn.sigmoid(x)


def _gelu_grad(x):
    c = math.sqrt(2.0 / math.pi)
    t = jnp.tanh(c * (x + 0.044715 * x * x * x))
    return 0.5 * (1.0 + t) + 0.5 * x * (1.0 - t * t) * c * (1.0 + 3.0 * 0.044715 * x * x)


def _neg_expm1(x):
    series = -x * (1.0 + x * (0.5 + x * (1.0 / 6.0 + x * (1.0 / 24.0))))
    return jnp.where(x > -0.03, series, 1.0 - jnp.exp(x))


def _tok(width):
    return pl.BlockSpec((TM, width), lambda i: (i, 0))


def _tok_rev(width, nt):
    return pl.BlockSpec((TM, width), lambda i: (nt - 1 - i, 0))


def _full(shape):
    return pl.BlockSpec(shape, lambda i: (0,) * len(shape))


def _params(vmem_mb, **kw):
    return pltpu.CompilerParams(dimension_semantics=("arbitrary",), vmem_limit_bytes=vmem_mb * VMEM_MB, **kw)


def _sds(shape, dtype=F32):
    return jax.ShapeDtypeStruct(shape, dtype)


def _far(shape, dtype=F32):
    return pltpu.HBM(shape, dtype)


class _Carried:
    def __init__(self, operands, out_shapes, sems, start, finish, aliases=None):
        self.operands, self.out_shapes, self.sems = list(operands), list(out_shapes), list(sems)
        self.start, self.finish, self.aliases = start, finish, dict(aliases or {})


def _in_hbm(arrays):
    return [pltpu.with_memory_space_constraint(a, pltpu.HBM) for a in arrays]


def _pallas_call(body, carry=None, **kw):
    if carry is None:
        return pl.pallas_call(body, **kw)

    def at_step(corner):
        hit = [pl.program_id(d) == (size - 1 if corner else 0) for d, size in enumerate(kw["grid"])]
        return functools.reduce(jnp.logical_and, hit)

    name, grid, compiler_params = kw["name"], kw["grid"], kw["compiler_params"]
    in_specs, out_specs, out_shape = list(kw["in_specs"]), list(kw["out_specs"]), list(kw["out_shape"])
    scratch_shapes = list(kw.get("scratch_shapes", ()))
    n_in, n_out, n_scr = len(in_specs), len(out_specs), len(scratch_shapes)
    c_in, c_out = len(carry.operands), len(carry.out_shapes)

    def full_body(*refs):
        ins, refs = refs[:n_in], refs[n_in:]
        c_ins, refs = refs[:c_in], refs[c_in:]
        outs, refs = refs[:n_out], refs[n_out:]
        c_outs, refs = refs[:c_out], refs[c_out:]
        scratch, c_sems = refs[:n_scr], refs[n_scr:]

        @pl.when(at_step(0))
        def _():
            carry.start(c_ins, c_outs, c_sems)

        body(*ins, *outs, *scratch)

        @pl.when(at_step(1))
        def _():
            carry.finish(c_ins, c_outs, c_sems)

    call = pl.pallas_call(
        full_body, name=name, grid=grid, in_specs=in_specs + [ANY] * c_in, out_specs=out_specs + [ANY] * c_out,
        out_shape=out_shape + list(carry.out_shapes), scratch_shapes=scratch_shapes + list(carry.sems),
        input_output_aliases={n_in + i: n_out + o for i, o in carry.aliases.items()},
        compiler_params=compiler_params)
    return lambda *operands: call(*operands, *_in_hbm(carry.operands))


def _resident(pairs, sems):
    first = pl.program_id(0) == 0
    copies = [pltpu.make_async_copy(src, dst, sems.at[j]) for j, (src, dst) in enumerate(pairs)]

    @pl.when(first)
    def _():
        for cp in copies:
            cp.start()

    def wait(j):
        @pl.when(first)
        def _():
            copies[j].wait()

    return wait


def _resident_now(pairs, sems):
    @pl.when(pl.program_id(0) == 0)
    def _():
        copies = [pltpu.make_async_copy(src, dst, sems.at[j]) for j, (src, dst) in enumerate(pairs)]
        for cp in copies:
            cp.start()
        for cp in copies:
            cp.wait()


def _row_iota(width):
    return lax.broadcasted_iota(jnp.int32, (SUB, width), 0)


def _bcast_row(x, row):
    return jnp.broadcast_to(x[row:row + 1, :], x.shape)


def _slab(k):
    return pl.ds(pl.multiple_of(k * SUB, SUB), SUB)


QC = INC // NCHIP
Z_PARTS = ((0, S5W), (S5W, S5W + LW), (S5W + LW, INC))


def _inproj_fwd(x, g_mix, w_in, b_in, carry=None):
    L = x.shape[0]

    def body(x_ref, g_ref, w_hbm, b_ref, h_ref, ua_ref, ub_ref, gp_ref, w_vm, w_sems):
        _resident_now([(w_hbm.at[k], w_vm.at[k]) for k in range(NCHIP)], w_sems)
        xh, _ = _rms(x_ref[...])
        h = (xh * g_ref[...]).astype(BF)
        h_ref[...] = h
        for k in range(NCHIP):
            lo, hi = k * QC, (k + 1) * QC
            z = jnp.dot(h, w_vm[k], preferred_element_type=F32) + b_ref[:, lo:hi]
            for ref, (a, b) in zip((ua_ref, ub_ref, gp_ref), Z_PARTS):
                s, e = max(lo, a), min(hi, b)
                if s < e:
                    ref[:, s - a:e - a] = z[:, s - lo:e - lo]

    return _pallas_call(
        body, carry, name="inproj_fwd", grid=(L // TM,),
        in_specs=[_tok(D), _full((1, D)), ANY, _full((1, INC))],
        out_specs=[_tok(D), _tok(S5W), _tok(LW), _tok(2 * D)],
        out_shape=[_far((L, D), BF), _far((L, S5W)), _far((L, LW)), _sds((L, 2 * D))],
        scratch_shapes=[pltpu.VMEM((NCHIP, D, QC), BF), pltpu.SemaphoreType.DMA((NCHIP,))],
        compiler_params=_params(40),
    )(*_in_hbm([x]), g_mix, *_in_hbm([w_in]), b_in)


def _inproj_bwd(x, dx1, dua, dub, dgp, g_mix, w_in, carry=None):
    L = x.shape[0]

    def body(x_ref, dx1_ref, dua_ref, dub_ref, dgp_ref, g_ref, w_hbm, gx_ref, dz_ref, dg_ref, db_ref, w_vm, w_sems):
        _resident_now([(w_hbm.at[k], w_vm.at[k]) for k in range(NCHIP)], w_sems)

        @pl.when(pl.program_id(0) == 0)
        def _():
            dg_ref[...] = jnp.zeros_like(dg_ref)
            db_ref[...] = jnp.zeros_like(db_ref)

        for src, (a, b) in zip((dua_ref, dub_ref, dgp_ref), Z_PARTS):
            d = src[...]
            dz_ref[:, a:b] = d.astype(BF)
            db_ref[0:1, a:b] += _colsum(d)
        dh = jnp.zeros((TM, D), F32)
        for k in range(NCHIP):
            dh = dh + lax.dot_general(dz_ref[:, k * QC:(k + 1) * QC], w_vm[k], (((1,), (1,)), ((), ())),
                                      preferred_element_type=F32)
        xh, r = _rms(x_ref[...])
        dg_ref[0:1, :] += _colsum(dh * xh)
        gx_ref[...] = dx1_ref[...] + _rms_bwd(dh, xh, r, g_ref[...])

    return _pallas_call(
        body, carry, name="inproj_bwd", grid=(L // TM,),
        in_specs=[_tok(D), _tok(D), _tok(S5W), _tok(LW), _tok(2 * D), _full((1, D)), ANY],
        out_specs=[_tok(D), _tok(INC), _full((SUB, D)), _full((SUB, INC))],
        out_shape=[_sds((L, D)), _sds((L, INC), BF), _sds((SUB, D)), _sds((SUB, INC))],
        scratch_shapes=[pltpu.VMEM((NCHIP, D, QC), BF), pltpu.SemaphoreType.DMA((NCHIP,))],
        compiler_params=_params(40),
    )(x, dx1, *_in_hbm([dua]), dub, dgp, g_mix, *_in_hbm([w_in]))


def _cscan(xr_ref, xi_ref, con_ref, cr_ref, ci_ref, reverse):
    n_slab = xr_ref.shape[0] // SUB
    width = xr_ref.shape[1]
    for lc in range(width // LC):
        cols = slice(lc * LC, (lc + 1) * LC)
        con = [con_ref[SUB * j:SUB * (j + 1), cols] for j in range(8)]

        def step(k, carry, cols=cols, con=con):
            cr, ci = carry
            rows = _slab(n_slab - 1 - k if reverse else k)
            xr, xi = xr_ref[rows, cols], xi_ref[rows, cols]
            for j, sh in enumerate((1, 2, 4)):
                mr, mi = con[2 * j], con[2 * j + 1]
                pr = pltpu.roll(xr, SUB - sh if reverse else sh, 0)
                pi = pltpu.roll(xi, SUB - sh if reverse else sh, 0)
                xr, xi = xr + mr * pr - mi * pi, xi + mr * pi + mi * pr
            xr, xi = xr + con[6] * cr - con[7] * ci, xi + con[6] * ci + con[7] * cr
            xr_ref[rows, cols] = xr
            xi_ref[rows, cols] = xi
            row = 0 if reverse else SUB - 1
            return _bcast_row(xr, row), _bcast_row(xi, row)

        cr, ci = lax.fori_loop(0, n_slab, step, (cr_ref[:, cols], ci_ref[:, cols]))
        cr_ref[:, cols] = cr
        ci_ref[:, cols] = ci


def _s5_fwd(ua, bbr, bbi, ccr, cci, dsk, con, w_glu, b_glu, carry=None):
    L = ua.shape[0]

    def body(ua_ref, bbr_hbm, bbi_hbm, ccr_hbm, cci_hbm, dsk_ref, con_ref, wg_ref, bg_ref,
             sr_ref, si_ref, y_ref, zg_ref, ya_ref, bbr_vm, bbi_vm, ccr_vm, cci_vm, cr_ref, ci_ref, w_sems):
        landed = _resident([(bbr_hbm, bbr_vm), (bbi_hbm, bbi_vm), (ccr_hbm, ccr_vm), (cci_hbm, cci_vm)], w_sems)

        @pl.when(pl.program_id(0) == 0)
        def _():
            cr_ref[...] = jnp.zeros_like(cr_ref)
            ci_ref[...] = jnp.zeros_like(ci_ref)

        u = ua_ref[...]
        ub = u.astype(BF)
        landed(0)
        sr_ref[...] = _blockdiag_mm(ub, bbr_vm)
        landed(1)
        si_ref[...] = _blockdiag_mm(ub, bbi_vm)
        _cscan(sr_ref, si_ref, con_ref, cr_ref, ci_ref, reverse=False)
        landed(2)
        landed(3)
        y = (_blockdiag_mm_t(sr_ref[...].astype(BF), ccr_vm) - _blockdiag_mm_t(si_ref[...].astype(BF), cci_vm)
             + dsk_ref[...] * u)
        y_ref[...] = y
        zg = jax.nn.gelu(y)
        zg_ref[...] = zg.astype(BF)
        q = _mm(zg, wg_ref[...]) + bg_ref[...]
        ya_ref[...] = (zg * _sig(q)).astype(BF)

    return _pallas_call(
        body, carry, name="s5_fwd", grid=(L // TM,),
        in_specs=[_tok(S5W), ANY, ANY, ANY, ANY, _full((1, S5W)), _full((8 * SUB, GN)),
                  _full((S5W, S5W)), _full((1, S5W))],
        out_specs=[_tok(GN), _tok(GN), _tok(S5W), _tok(S5W), _tok(S5W)],
        out_shape=[_sds((L, GN)), _sds((L, GN)), _far((L, S5W)), _far((L, S5W), BF), _far((L, S5W), BF)],
        scratch_shapes=[pltpu.VMEM((S5W // 128, 128, GN // (S5W // 128)), BF)] * 4 + [
                        pltpu.VMEM((SUB, GN), F32), pltpu.VMEM((SUB, GN), F32),
                        pltpu.SemaphoreType.DMA((4,))],
        compiler_params=_params(44),
    )(*_in_hbm([ua]), bbr, bbi, ccr, cci, dsk, con, w_glu, b_glu)


def _s5_bwd(dya, y, ua, sr, si, bbr, bbi, ccr, cci, dsk, con_rev, w_glu, b_glu, carry=None):
    L = ua.shape[0]
    nt = L // TM
    spt = TM // SUB
    n_slab = spt

    def halo_map(i):
        return (jnp.maximum((nt - 1 - i) * spt - 1, 0), 0)

    def body(dya_ref, y_ref, ua_ref, sr_ref, si_ref, hr_ref, hi_ref, bbr_hbm, bbi_hbm, ccr_hbm, cci_hbm,
             dsk_ref, con_ref, wg_ref, bg_ref,
             dua_ref, dq_ref, dy_ref, lr_ref, li_ref, da_ref, dsm_ref,
             bbr_vm, bbi_vm, ccr_vm, cci_vm, cr_ref, ci_ref, w_sems):
        i = pl.program_id(0)
        landed = _resident([(ccr_hbm, ccr_vm), (cci_hbm, cci_vm), (bbr_hbm, bbr_vm), (bbi_hbm, bbi_vm)], w_sems)

        @pl.when(i == 0)
        def _():
            cr_ref[...] = jnp.zeros_like(cr_ref)
            ci_ref[...] = jnp.zeros_like(ci_ref)
            da_ref[...] = jnp.zeros_like(da_ref)
            dsm_ref[...] = jnp.zeros_like(dsm_ref)

        u = ua_ref[...]
        yv = y_ref[...]
        dya = dya_ref[...]
        zg = jax.nn.gelu(yv)
        sg = _sig(_mm(zg, wg_ref[...]) + bg_ref[...])
        dq = dya * zg * sg * (1.0 - sg)
        dq_ref[...] = dq.astype(BF)
        dzg = dya * sg + _mm_nt(dq, wg_ref[...])
        dy = dzg * _gelu_grad(yv)
        dyb = dy.astype(BF)
        dy_ref[...] = dyb
        dsm_ref[0:1, :] += _colsum(dy * u)
        dsm_ref[1:2, :] += _colsum(dq)
        landed(0)
        lr_ref[...] = _blockdiag_mm(dyb, ccr_vm)
        landed(1)
        li_ref[...] = -_blockdiag_mm(dyb, cci_vm)
        _cscan(lr_ref, li_ref, con_ref, cr_ref, ci_ref, reverse=True)

        first_tile = (i == nt - 1)
        row = _row_iota(LC)
        for lc in range(GN // LC):
            cols = slice(lc * LC, (lc + 1) * LC)
            h_r = jnp.where(first_tile, 0.0, hr_ref[:, cols])
            h_i = jnp.where(first_tile, 0.0, hi_ref[:, cols])

            def step(k, acc, cols=cols, h_r=h_r, h_i=h_i):
                ar, ai = acc
                rows = _slab(k)
                prev = _slab(jnp.maximum(k - 1, 0))
                pr = jnp.where(k == 0, h_r, sr_ref[prev, cols])
                pi = jnp.where(k == 0, h_i, si_ref[prev, cols])
                spr = pltpu.roll(jnp.where(row == SUB - 1, pr, sr_ref[rows, cols]), 1, 0)
                spi = pltpu.roll(jnp.where(row == SUB - 1, pi, si_ref[rows, cols]), 1, 0)
                lr, li = lr_ref[rows, cols], li_ref[rows, cols]
                return ar + lr * spr + li * spi, ai + li * spr - lr * spi

            zero = jnp.zeros((SUB, LC), F32)
            ar, ai = lax.fori_loop(0, n_slab, step, (zero, zero))
            da_ref[0:1, cols] += _colsum(ar)
            da_ref[1:2, cols] += _colsum(ai)

        landed(2)
        landed(3)
        dua_ref[...] = (dy * dsk_ref[...] + _blockdiag_mm_t(lr_ref[...].astype(BF), bbr_vm)
                        + _blockdiag_mm_t(li_ref[...].astype(BF), bbi_vm))

    return _pallas_call(
        body, carry, name="s5_bwd", grid=(nt,),
        in_specs=[_tok_rev(S5W, nt), _tok_rev(S5W, nt), _tok_rev(S5W, nt), _tok_rev(GN, nt), _tok_rev(GN, nt),
                  pl.BlockSpec((SUB, GN), halo_map), pl.BlockSpec((SUB, GN), halo_map),
                  ANY, ANY, ANY, ANY, _full((1, S5W)), _full((8 * SUB, GN)), _full((S5W, S5W)), _full((1, S5W))],
        out_specs=[_tok_rev(S5W, nt), _tok_rev(S5W, nt), _tok_rev(S5W, nt), _tok_rev(GN, nt), _tok_rev(GN, nt),
                   _full((SUB, GN)), _full((SUB, S5W))],
        out_shape=[_sds((L, S5W)), _sds((L, S5W), BF), _sds((L, S5W), BF), _sds((L, GN)), _sds((L, GN)),
                   _sds((SUB, GN)), _sds((SUB, S5W))],
        scratch_shapes=[pltpu.VMEM((S5W // 128, 128, GN // (S5W // 128)), BF)] * 4 + [
                        pltpu.VMEM((SUB, GN), F32), pltpu.VMEM((SUB, GN), F32),
                        pltpu.SemaphoreType.DMA((4,))],
        compiler_params=_params(52),
    )(dya, y, ua, sr, si, sr, si, bbr, bbi, ccr, cci, dsk, con_rev, w_glu, b_glu)


def _lru_gate_terms(rg, sp):
    log_a = -LRU_C * rg * sp
    a = jnp.exp(log_a)
    mult = jnp.sqrt(_neg_expm1(2.0 * log_a))
    return a, mult


def _lru_fwd(ub, conv_w, conv_b, wr, wi, b_r, b_i, sp, carry=None):
    L = ub.shape[0]
    n_slab = TM // SUB

    def body(ub_ref, cw_ref, cb_ref, wr_ref, wi_ref, br_ref, bi_ref, sp_ref,
             xc_ref, rg_ref, ig_ref, h_ref, hp_ref, a_ref, halo_ref, carry_ref):
        @pl.when(pl.program_id(0) == 0)
        def _():
            halo_ref[...] = jnp.zeros_like(halo_ref)
            carry_ref[...] = jnp.zeros_like(carry_ref)

        row = _row_iota(LW)
        taps = [cw_ref[k:k + 1, :] for k in range(4)]
        cb = cb_ref[...]

        def conv_step(k, prev):
            rows = _slab(k)
            cur = ub_ref[rows, :]
            acc = taps[3] * cur + cb
            for j in (1, 2, 3):
                acc = acc + taps[3 - j] * pltpu.roll(jnp.where(row >= SUB - j, prev, cur), j, 0)
            xc_ref[rows, :] = acc
            return cur

        halo_ref[...] = lax.fori_loop(0, n_slab, conv_step, halo_ref[...])

        xc = xc_ref[...]
        xcb = xc.astype(BF)
        rg = _sig(_blockdiag_mm(xcb, wr_ref) + br_ref[...])
        ig = _sig(_blockdiag_mm(xcb, wi_ref) + bi_ref[...])
        rg_ref[...] = rg
        ig_ref[...] = ig
        a, mult = _lru_gate_terms(rg, sp_ref[...])
        a_ref[...] = a
        h_ref[...] = mult * ig * xc

        rowc = _row_iota(LC)
        for lc in range(LW // LC):
            cols = slice(lc * LC, (lc + 1) * LC)

            def step(k, c, cols=cols):
                rows = _slab(k)
                av, b = a_ref[rows, cols], h_ref[rows, cols]
                for sh in (1, 2, 4):
                    keep = rowc >= sh
                    b = b + av * jnp.where(keep, pltpu.roll(b, sh, 0), 0.0)
                    av = av * jnp.where(keep, pltpu.roll(av, sh, 0), 1.0)
                h = b + av * c
                h_ref[rows, cols] = h
                hp_ref[rows, cols] = jnp.where(rowc == 0, c, pltpu.roll(h, 1, 0))
                return _bcast_row(h, SUB - 1)

            carry_ref[:, cols] = lax.fori_loop(0, n_slab, step, carry_ref[:, cols])

    return _pallas_call(
        body, carry, name="lru_fwd", grid=(L // TM,),
        in_specs=[_tok(LW), _full((4, LW)), _full((1, LW)), _full((LW // 128, 128, 128)), _full((LW // 128, 128, 128)),
                  _full((1, LW)), _full((1, LW)), _full((1, LW))],
        out_specs=[_tok(LW)] * 5,
        out_shape=[_far((L, LW))] * 5,
        scratch_shapes=[pltpu.VMEM((TM, LW), F32), pltpu.VMEM((SUB, LW), F32), pltpu.VMEM((SUB, LW), F32)],
        compiler_params=_params(40),
    )(*_in_hbm([ub]), conv_w, conv_b, wr, wi, b_r, b_i, sp)


def _lru_bwd(dyb, xc, rg, ig, hp, ub, conv_w, wr, wi, sp, dsp, carry=None):
    L = ub.shape[0]
    nt = L // TM
    spt = TM // SUB
    n_slab = spt

    def halo_map(i):
        return (jnp.maximum((nt - 1 - i) * spt - 1, 0), 0)

    def body(dh_ref, xc_ref, rg_ref, ig_ref, hp_ref, ub_ref, uh_ref, cw_ref, wr_ref, wi_ref, sp_ref, dsp_ref,
             dub_ref, dpr_ref, dpi_ref, acc_ref, a_ref, lam_ref, dxc_ref, carry_ref, next_ref):
        i = pl.program_id(0)

        @pl.when(i == 0)
        def _():
            carry_ref[...] = jnp.zeros_like(carry_ref)
            next_ref[...] = jnp.zeros_like(next_ref)
            acc_ref[...] = jnp.zeros_like(acc_ref)

        sp = sp_ref[...]
        rg, ig, xc = rg_ref[...], ig_ref[...], xc_ref[...]
        a, mult = _lru_gate_terms(rg, sp)
        a_ref[...] = a

        rowc = _row_iota(LC)
        for lc in range(LW // LC):
            cols = slice(lc * LC, (lc + 1) * LC)

            def step(k, c, cols=cols):
                rows = _slab(n_slab - 1 - k)
                av, dh = a_ref[rows, cols], dh_ref[rows, cols]
                b = av * dh
                for sh in (1, 2, 4):
                    keep = rowc < SUB - sh
                    b = b + av * jnp.where(keep, pltpu.roll(b, SUB - sh, 0), 0.0)
                    av = av * jnp.where(keep, pltpu.roll(av, SUB - sh, 0), 1.0)
                mu = b + av * c
                lam_ref[rows, cols] = dh + jnp.where(rowc == SUB - 1, c, pltpu.roll(mu, SUB - 1, 0))
                return _bcast_row(mu, 0)

            carry_ref[:, cols] = lax.fori_loop(0, n_slab, step, carry_ref[:, cols])

        lam = lam_ref[...]
        d_a = lam * hp_ref[...]
        d_mult = lam * ig * xc
        d_ig = lam * mult * xc
        dxc = lam * mult * ig
        d_log_a = d_a * a - d_mult * a * a / mult
        d_rg = (-LRU_C) * sp * d_log_a
        acc_ref[0:1, :] += _colsum((-LRU_C) * rg * d_log_a) * dsp_ref[...]
        dpr = d_rg * rg * (1.0 - rg)
        dpi = d_ig * ig * (1.0 - ig)
        acc_ref[1:2, :] += _colsum(dpr)
        acc_ref[2:3, :] += _colsum(dpi)
        dprb, dpib = dpr.astype(BF), dpi.astype(BF)
        dpr_ref[...] = dprb
        dpi_ref[...] = dpib
        dxc = dxc + _blockdiag_mm_t(dprb, wr_ref) + _blockdiag_mm_t(dpib, wi_ref)
        dxc_ref[...] = dxc
        acc_ref[3:4, :] += _colsum(dxc)

        row = _row_iota(LW)
        taps = [cw_ref[k:k + 1, :] for k in range(4)]
        u_halo = jnp.where(i == nt - 1, 0.0, uh_ref[...])
        nxt_tile = next_ref[...]

        def conv_step(k, accs):
            rows = _slab(k)
            cur = dxc_ref[rows, :]
            nxt = jnp.where(k == n_slab - 1, nxt_tile, dxc_ref[_slab(jnp.minimum(k + 1, n_slab - 1)), :])
            ucur = ub_ref[rows, :]
            uprev = jnp.where(k == 0, u_halo, ub_ref[_slab(jnp.maximum(k - 1, 0)), :])
            du = taps[3] * cur
            new = [accs[3] + cur * ucur]
            for j in (1, 2, 3):
                du = du + taps[3 - j] * pltpu.roll(jnp.where(row < j, nxt, cur), SUB - j, 0)
                new.append(accs[3 - j] + cur * pltpu.roll(jnp.where(row >= SUB - j, uprev, ucur), j, 0))
            dub_ref[rows, :] = du
            return tuple(new[::-1])

        zero = jnp.zeros((SUB, LW), F32)
        accs = lax.fori_loop(0, n_slab, conv_step, (zero, zero, zero, zero))
        for k in range(4):
            acc_ref[4 + k:5 + k, :] += _colsum(accs[k])
        next_ref[...] = dxc_ref[0:SUB, :]

    return _pallas_call(
        body, carry, name="lru_bwd", grid=(nt,),
        in_specs=[_tok_rev(LW, nt)] * 6 + [pl.BlockSpec((SUB, LW), halo_map), _full((4, LW)),
                                           _full((LW // 128, 128, 128)), _full((LW // 128, 128, 128)), _full((1, LW)), _full((1, LW))],
        out_specs=[_tok_rev(LW, nt), _tok_rev(LW, nt), _tok_rev(LW, nt), _full((SUB, LW))],
        out_shape=[_sds((L, LW)), _far((L, LW), BF), _far((L, LW), BF), _sds((SUB, LW))],
        scratch_shapes=[pltpu.VMEM((TM, LW), F32), pltpu.VMEM((TM, LW), F32), pltpu.VMEM((TM, LW), F32),
                        pltpu.VMEM((SUB, LW), F32), pltpu.VMEM((SUB, LW), F32)],
        compiler_params=_params(48),
    )(dyb, xc, rg, ig, hp, ub, ub, conv_w, wr, wi, sp, dsp)


AC = D // NCHIP


def _merge_fwd(x, ya, yb, gp, w_a, w_b, w_o, carry=None):
    L = x.shape[0]

    def body(x_ref, ya_ref, yb_ref, gp_ref, wa_ref, wb_ref, wo_ref, x1_ref, pa_ref, pb_ref, mg_ref):
        ya = ya_ref[...]
        for k in range(NCHIP):
            pa_ref[:, k * AC:(k + 1) * AC] = jnp.dot(ya, wa_ref[k], preferred_element_type=F32)
        pb = _mm(yb_ref[...], wb_ref[...])
        pb_ref[...] = pb
        gp = gp_ref[...]
        merged = (_sig(gp[:, :D]) * pa_ref[...] + _sig(gp[:, D:]) * pb).astype(BF)
        mg_ref[...] = merged
        x1_ref[...] = x_ref[...] + jnp.dot(merged, wo_ref[...], preferred_element_type=F32)

    return _pallas_call(
        body, carry, name="merge_fwd", grid=(L // TM,),
        in_specs=[_tok(D), _tok(S5W), _tok(LW), _tok(2 * D), _full((NCHIP, S5W, AC)), _full((LW, D)), _full((D, D))],
        out_specs=[_tok(D), _tok(D), _tok(D), _tok(D)],
        out_shape=[_sds((L, D)), _sds((L, D)), _sds((L, D)), _far((L, D), BF)],
        compiler_params=_params(40),
    )(x, ya, yb, gp, w_a, w_b, w_o)


def _merge_bwd(dx1, gp, pa, pb, w_a, w_b, w_o, carry=None):
    L = dx1.shape[0]

    def body(dx1_ref, gp_ref, pa_ref, pb_ref, wa_ref, wb_ref, wo_ref, dya_ref, dyb_ref, dgp_ref, dpa_ref, dpb_ref):
        dm = _mm_nt(dx1_ref[...], wo_ref[...])
        gp = gp_ref[...]
        sa, sb = _sig(gp[:, :D]), _sig(gp[:, D:])
        dpa = (dm * sa).astype(BF)
        dpb = (dm * sb).astype(BF)
        dpa_ref[...] = dpa
        dpb_ref[...] = dpb
        dgp_ref[:, :D] = dm * pa_ref[...] * sa * (1.0 - sa)
        dgp_ref[:, D:] = dm * pb_ref[...] * sb * (1.0 - sb)
        dya = jnp.zeros((TM, S5W), F32)
        for k in range(NCHIP):
            dya = dya + _mm_nt(dpa[:, k * AC:(k + 1) * AC], wa_ref[k])
        dya_ref[...] = dya
        dyb_ref[...] = _mm_nt(dpb, wb_ref[...])

    return _pallas_call(
        body, carry, name="merge_bwd", grid=(L // TM,),
        in_specs=[_tok(D), _tok(2 * D), _tok(D), _tok(D), _full((NCHIP, S5W, AC)), _full((LW, D)), _full((D, D))],
        out_specs=[_tok(S5W), _tok(LW), _tok(2 * D), _tok(D), _tok(D)],
        out_shape=[_far((L, S5W)), _far((L, LW)), _sds((L, 2 * D)), _far((L, D), BF), _far((L, D), BF)],
        compiler_params=_params(40),
    )(dx1, gp, pa, pb, w_a, w_b, w_o)


def _chunk_tok(width):
    return pl.BlockSpec((NCHIP, TM, width), lambda i: (0, i, 0))


def _ffn_fwd(x1, g_ffn, wg, wu, wd, carry=None):
    L = x1.shape[0]

    def body(x_ref, g_ref, wg_hbm, wu_hbm, wd_hbm, x2_ref, h2_ref, gg_ref, uu_ref, wg_vm, wu_vm, wd_vm, w_sems):
        _resident_now([(src.at[c], dst.at[c]) for c in range(NCHIP)
                       for src, dst in ((wg_hbm, wg_vm), (wu_hbm, wu_vm), (wd_hbm, wd_vm))], w_sems)
        x = x_ref[...]
        xh, _ = _rms(x)
        h2 = (xh * g_ref[...]).astype(BF)
        h2_ref[...] = h2
        out = x
        for c in range(NCHIP):
            gg = lax.dot_general(h2, wg_vm[c], (((1,), (1,)), ((), ())), preferred_element_type=F32)
            uu = lax.dot_general(h2, wu_vm[c], (((1,), (1,)), ((), ())), preferred_element_type=F32)
            gg_ref[c] = gg.astype(BF)
            uu_ref[c] = uu.astype(BF)
            act = (gg * _sig(gg) * uu).astype(BF)
            out = out + jnp.dot(act, wd_vm[c], preferred_element_type=F32)
        x2_ref[...] = out

    return _pallas_call(
        body, carry, name="ffn_fwd", grid=(L // TM,),
        in_specs=[_tok(D), _full((1, D)), ANY, ANY, ANY],
        out_specs=[_tok(D), _tok(D), _chunk_tok(FC), _chunk_tok(FC)],
        out_shape=[_sds((L, D)), _sds((L, D), BF), _sds((NCHIP, L, FC), BF), _sds((NCHIP, L, FC), BF)],
        scratch_shapes=[pltpu.VMEM((NCHIP, FC, D), BF)] * 3 + [pltpu.SemaphoreType.DMA((3 * NCHIP,))],
        compiler_params=_params(52),
    )(x1, g_ffn, wg, wu, wd)


def _ffn_bwd(x1, dx2, gg, uu, g_ffn, wg, wu, wd, carry=None):
    L = x1.shape[0]

    def body(x_ref, dx2_ref, gg_ref, uu_ref, g_ref, wg_hbm, wu_hbm, wd_hbm,
             dx1_ref, act_ref, dgg_ref, duu_ref, dg_ref, wg_vm, wu_vm, wd_vm, w_sems):
        _resident_now([(src.at[c], dst.at[c]) for c in range(NCHIP)
                       for src, dst in ((wg_hbm, wg_vm), (wu_hbm, wu_vm), (wd_hbm, wd_vm))], w_sems)

        @pl.when(pl.program_id(0) == 0)
        def _():
            dg_ref[...] = jnp.zeros_like(dg_ref)

        dx2 = dx2_ref[...]
        dx2b = dx2.astype(BF)
        dh2 = jnp.zeros((TM, D), F32)
        for c in range(NCHIP):
            g = gg_ref[c].astype(F32)
            u = uu_ref[c].astype(F32)
            s = _sig(g)
            silu = g * s
            act_ref[c] = (silu * u).astype(BF)
            dact = lax.dot_general(dx2b, wd_vm[c], (((1,), (1,)), ((), ())), preferred_element_type=F32)
            dg = (dact * u * s * (1.0 + g * (1.0 - s))).astype(BF)
            du = (dact * silu).astype(BF)
            dgg_ref[c] = dg
            duu_ref[c] = du
            dh2 = dh2 + jnp.dot(dg, wg_vm[c], preferred_element_type=F32)
            dh2 = dh2 + jnp.dot(du, wu_vm[c], preferred_element_type=F32)
        xh, r = _rms(x_ref[...])
        dg_ref[0:1, :] += _colsum(dh2 * xh)
        dx1_ref[...] = dx2 + _rms_bwd(dh2, xh, r, g_ref[...])

    return _pallas_call(
        body, carry, name="ffn_bwd", grid=(L // TM,),
        in_specs=[_tok(D), _tok(D), _chunk_tok(FC), _chunk_tok(FC), _full((1, D)), ANY, ANY, ANY],
        out_specs=[_tok(D), _chunk_tok(FC), _chunk_tok(FC), _chunk_tok(FC), _full((SUB, D))],
        out_shape=[_sds((L, D)), _sds((NCHIP, L, FC), BF), _sds((NCHIP, L, FC), BF), _sds((NCHIP, L, FC), BF),
                   _sds((SUB, D))],
        scratch_shapes=[pltpu.VMEM((NCHIP, FC, D), BF)] * 3 + [pltpu.SemaphoreType.DMA((3 * NCHIP,))],
        compiler_params=_params(56),
    )(x1, dx2, gg, uu, g_ffn, wg, wu, wd)


def _ple_loss(x2, p, tgt, g_pg, w_pg, b_pg, w_ple, g_ple, g_final):
    L = x2.shape[0]

    def body(x2_ref, p_ref, t_ref, gpg_ref, wpg_ref, bpg_ref, wple_ref, gple_ref, gf_ref,
             dx2_ref, n2_ref, dpre_ref, de0_ref, acc_ref):
        @pl.when(pl.program_id(0) == 0)
        def _():
            acc_ref[...] = jnp.zeros_like(acc_ref)

        x2 = x2_ref[...]
        x2h, r2 = _rms(x2)
        n2 = (x2h * gpg_ref[...]).astype(BF)
        n2_ref[...] = n2
        gate = _sig(jnp.dot(n2, wpg_ref[...], preferred_element_type=F32) + bpg_ref[...])
        pb = p_ref[...].astype(BF)
        e0 = jnp.concatenate([jnp.dot(pb, wple_ref[k], preferred_element_type=F32) for k in range(NCHIP)], axis=1)
        e0h, re = _rms(e0)
        e = e0h * gple_ref[...]
        x3 = x2 + gate * e
        x3h, r3 = _rms(x3)
        diff = x3h * gf_ref[...] - t_ref[...]
        acc_ref[4:5, :] += _colsum(diff * diff) * (0.5 / D)
        dy = diff * (1.0 / D)
        acc_ref[3:4, :] += _colsum(dy * x3h)
        dx3 = _rms_bwd(dy, x3h, r3, gf_ref[...])
        de = dx3 * gate
        acc_ref[2:3, :] += _colsum(de * e0h)
        de0_ref[...] = _rms_bwd(de, e0h, re, gple_ref[...]).astype(BF)
        dpre = dx3 * e * gate * (1.0 - gate)
        acc_ref[1:2, :] += _colsum(dpre)
        dpreb = dpre.astype(BF)
        dpre_ref[...] = dpreb
        dn2 = lax.dot_general(dpreb, wpg_ref[...], (((1,), (1,)), ((), ())), preferred_element_type=F32)
        acc_ref[0:1, :] += _colsum(dn2 * x2h)
        dx2_ref[...] = dx3 + _rms_bwd(dn2, x2h, r2, gpg_ref[...])

    return _pallas_call(
        body, name="ple_loss", grid=(L // TM,),
        in_specs=[_tok(D), _tok(PLE), _tok(D), _full((1, D)), _full((D, D)), _full((1, D)), _full((NCHIP, PLE, AC)),
                  _full((1, D)), _full((1, D))],
        out_specs=[_tok(D), _tok(D), _tok(D), _tok(D), _full((SUB, D))],
        out_shape=[_sds((L, D)), _sds((L, D), BF), _sds((L, D), BF), _sds((L, D), BF), _sds((SUB, D))],
        compiler_params=_params(40),
    )(x2, p, tgt, g_pg, w_pg, b_pg, w_ple, g_ple, g_final)


def _tn(name, a, b, col_chunk=None, a_block=None, carry=None):
    L = a.shape[-2]
    m, n = a.shape[-1], b.shape[-1]
    a_col = 0
    if a_block is not None:
        a_col, m = a_block
    tk = L if (a.ndim == 3 or b.ndim == 3 or a_block is not None) else TK
    if a.ndim == 3 or b.ndim == 3:
        nj, bn = (a if a.ndim == 3 else b).shape[0], n
        a_spec = (pl.BlockSpec((None, tk, m), lambda j, t: (j, t, 0)) if a.ndim == 3
                  else pl.BlockSpec((tk, m), lambda j, t: (t, 0)))
        b_spec = (pl.BlockSpec((None, tk, n), lambda j, t: (j, t, 0)) if b.ndim == 3
                  else pl.BlockSpec((tk, n), lambda j, t: (t, 0)))
        out_spec, out_shape = pl.BlockSpec((None, m, n), lambda j, t: (j, 0, 0)), _sds((nj, m, n))
    else:
        bn = col_chunk
        if bn is None:
            bn = next((cand for cand in (1024, 512) if n > cand and n % cand == 0), n)
        nj = n // bn
        a_spec = pl.BlockSpec((tk, m), lambda j, t: (t, a_col))
        b_spec = pl.BlockSpec((tk, bn), lambda j, t: (t, j))
        if col_chunk is None:
            out_spec, out_shape = pl.BlockSpec((m, bn), lambda j, t: (0, j)), _sds((m, n))
        else:
            out_spec, out_shape = pl.BlockSpec((None, m, bn), lambda j, t: (j, 0, 0)), _sds((nj, m, bn))

    def body(a_ref, b_ref, o_ref):
        if tk == L:
            o_ref[...] = _mm_tn(a_ref[...], b_ref[...])
        else:
            @pl.when(pl.program_id(1) == 0)
            def _():
                o_ref[...] = jnp.zeros_like(o_ref)

            o_ref[...] += _mm_tn(a_ref[...], b_ref[...])

    outs = _pallas_call(
        body, carry, name=name, grid=(nj, L // tk), in_specs=[a_spec, b_spec], out_specs=[out_spec],
        out_shape=[pltpu.HBM(out_shape.shape, out_shape.dtype)],
        compiler_params=pltpu.CompilerParams(dimension_semantics=("arbitrary", "arbitrary"),
                                             vmem_limit_bytes=(30 if tk == L else 28) * VMEM_MB),
    )(*(_in_hbm([a, b]) if tk == L else (a, b)))
    return outs[0] if carry is None else outs


LANE = 128


def _tn_blocks(name, a, bs, ga, gb, carry=None):
    L, m, n, nb = a.shape[0], a.shape[1], bs[0].shape[1], len(bs)
    per = LANE // ga
    wb = per * gb
    n_super = m // LANE

    def body(a_ref, *refs):
        b_refs, o_refs, acc_refs = refs[:nb], refs[nb:2 * nb], refs[2 * nb:]
        t = pl.program_id(0)

        @pl.when(t == 0)
        def _():
            for acc in acc_refs:
                acc[...] = jnp.zeros_like(acc)

        lhs = a_ref[...].astype(BF)
        for b_ref, acc in zip(b_refs, acc_refs):
            rhs = b_ref[...].astype(BF)
            for j in range(n_super):
                acc[j] += _mm_tn(lhs[:, j * LANE:(j + 1) * LANE], rhs[:, j * wb:(j + 1) * wb])

        @pl.when(t == L // TK - 1)
        def _():
            own = (lax.broadcasted_iota(jnp.int32, (LANE, wb), 0) // ga) == (lax.broadcasted_iota(jnp.int32, (LANE, wb), 1) // gb)
            for o_ref, acc in zip(o_refs, acc_refs):
                for j in range(n_super):
                    kept = jnp.where(own, acc[j], 0.0)
                    o_ref[:, j * wb:(j + 1) * wb] = jnp.sum(kept.reshape(per, ga, wb), axis=0)

    outs = _pallas_call(
        body, carry, name=name, grid=(L // TK,),
        in_specs=[pl.BlockSpec((TK, m), lambda t: (t, 0))] + [pl.BlockSpec((TK, n), lambda t: (t, 0))] * nb,
        out_specs=[_full((ga, n))] * nb, out_shape=[_sds((ga, n))] * nb,
        scratch_shapes=[pltpu.VMEM((n_super, LANE, wb), F32)] * nb,
        compiler_params=_params(48),
    )(*_in_hbm([a] + list(bs)))
    return list(outs)


def _s5_discretize(lam_re, lam_im, log_dt, b_re, b_im):
    dt = jnp.exp(log_dt)[:, None]
    mag = jnp.exp(lam_re * dt)
    ar = mag * jnp.cos(lam_im * dt)
    ai = mag * jnp.sin(lam_im * dt)
    den = lam_re * lam_re + lam_im * lam_im
    nr = ar - 1.0
    fr = (nr * lam_re + ai * lam_im) / den
    fi = (ai * lam_re - nr * lam_im) / den
    bbr = fr[:, None, :] * b_re - fi[:, None, :] * b_im
    bbi = fr[:, None, :] * b_im + fi[:, None, :] * b_re
    return ar, ai, bbr, bbi


def _prepare(by_rows, block_cols, ar, ai):
    n = len(by_rows)

    def body(*refs):
        srcs, (ar_ref, ai_ref), dense, (con_ref, rev_ref) = refs[:n], refs[n:n + 2], refs[n + 2:2 * n + 2], refs[2 * n + 2:]
        for src, out, c in zip(srcs, dense, block_cols):
            r = src.shape[0]
            per = LANE // r
            wide = per * c
            own = (lax.broadcasted_iota(jnp.int32, (LANE, wide), 0) // r) == (lax.broadcasted_iota(jnp.int32, (LANE, wide), 1) // c)
            for j in range(out.shape[0]):
                tiled = jnp.broadcast_to(src[:, j * wide:(j + 1) * wide][None], (per, r, wide)).reshape(LANE, wide)
                out[j] = jnp.where(own, tiled, 0.0).astype(BF)
        a_r, a_i = ar_ref[...], ai_ref[...]
        pw = [(jnp.ones_like(a_r), jnp.zeros_like(a_i))]
        for _ in range(SUB):
            pr, pi = pw[-1]
            pw.append((pr * a_r - pi * a_i, pr * a_i + pi * a_r))
        row = _row_iota(GN)
        for ref, reverse in ((con_ref, False), (rev_ref, True)):
            sign = -1.0 if reverse else 1.0
            for j, sh in enumerate((1, 2, 4)):
                keep = (row < SUB - sh) if reverse else (row >= sh)
                ref[2 * j * SUB:(2 * j + 1) * SUB, :] = jnp.where(keep, pw[sh][0], 0.0)
                ref[(2 * j + 1) * SUB:(2 * j + 2) * SUB, :] = jnp.where(keep, sign * pw[sh][1], 0.0)
            p_r, p_i = jnp.zeros((SUB, GN), F32), jnp.zeros((SUB, GN), F32)
            for i in range(SUB):
                k = SUB - i if reverse else i + 1
                p_r = jnp.where(row == i, pw[k][0], p_r)
                p_i = jnp.where(row == i, sign * pw[k][1], p_i)
            ref[6 * SUB:7 * SUB, :] = p_r
            ref[7 * SUB:8 * SUB, :] = p_i

    dense_shapes = [(b.shape[1] // (LANE // b.shape[0] * c), LANE, LANE // b.shape[0] * c)
                    for b, c in zip(by_rows, block_cols)]
    outs = _pallas_call(
        body, name="prepare", grid=(1,), in_specs=[_full(b.shape) for b in by_rows] + [_full((1, GN))] * 2,
        out_specs=[_full(s) for s in dense_shapes] + [_full((8 * SUB, GN))] * 2,
        out_shape=[_far(s, BF) for s in dense_shapes] + [_sds((8 * SUB, GN)), _far((8 * SUB, GN))],
        compiler_params=_params(48),
    )(*by_rows, ar, ai)
    return outs[:n], outs[n], outs[n + 1]


def _local_step(x, p, tgt, w, comm):
    rows_of = lambda a: a.reshape(NCHIP * a.shape[1], a.shape[2])
    quarters = lambda a: a.reshape(NCHIP, a.shape[0] // NCHIP, a.shape[1])

    def gathering(names, call):
        carry = comm.gather(names)
        outs = list(call(carry))
        own = len(outs) - len(carry.out_shapes)
        w.update(zip(names, outs[own:]))
        return outs[:own]

    w.update(comm.first())
    ar, ai, bbr, bbi = _s5_discretize(w["lam_re"], w["lam_im"], w["log_dt"], w["s5_b_re"], w["s5_b_im"])
    by_row = lambda b: jnp.transpose(b, (1, 0, 2)).reshape(b.shape[1], -1)
    (bbr_d, bbi_d, ccr_d, cci_d, wr_d, wi_d), con, con_rev = _prepare(
        [by_row(b) for b in (bbr, bbi, w["s5_c_re"], w["s5_c_im"], w["w_r"], w["w_i"])], [NS] * 4 + [HD] * 2,
        ar.reshape(1, GN), ai.reshape(1, GN))
    dsk = w["s5_d"].reshape(1, S5W)
    lam = w["lru_lambda"].reshape(1, LW)
    sp = jax.nn.softplus(-lam)
    b_r, b_i = w["b_r"].reshape(1, LW), w["b_i"].reshape(1, LW)
    row = lambda name: w[name].reshape(1, -1)

    h, ua, ub, gp = gathering(["w_glu", "w_a_out", "w_b_out"], lambda carry: _inproj_fwd(
        x, row("g_mix"), w["w_in"], row("b_in"), carry))
    w_glu = rows_of(w["w_glu"])
    sr, si, y, zg, ya = gathering(["w_o", "w_ffn_gate"], lambda carry: _s5_fwd(
        ua, bbr_d, bbi_d, ccr_d, cci_d, dsk, con, w_glu, row("b_glu"), carry))
    xc, rg, ig, yb, hp = gathering(["w_ffn_up"], lambda carry: _lru_fwd(
        ub, w["conv_w"], row("conv_b"), wr_d, wi_d, b_r, b_i, sp, carry))
    w_b_out, w_o = rows_of(w["w_b_out"]), rows_of(w["w_o"])
    x1, pa, pb, merged = gathering(["w_ffn_down"], lambda carry: _merge_fwd(
        x, ya, yb, gp, w["w_a_out"], w_b_out, w_o, carry))
    x2, h2, gg, uu = gathering(["w_ple_gate", "w_ple"], lambda carry: _ffn_fwd(
        x1, row("g_ffn"), w["w_ffn_gate"], w["w_ffn_up"], w["w_ffn_down"], carry))
    w_pg = rows_of(w["w_ple_gate"])
    dx2, n2, dpre, de0, acc_p = _ple_loss(x2, p, tgt, row("g_ple_gate"), w_pg, row("b_ple_gate"),
                                          w["w_ple"], row("g_ple"), row("g_final"))
    comm.reduce("ple", {"w_ple_gate": quarters(_tn("dw_ple_gate", n2, dpre)),
                        "w_ple": _tn("dw_ple", p, de0, col_chunk=AC)})
    dx1, act, dgg, duu, acc_f = comm.run(lambda carry: _ffn_bwd(
        x1, dx2, gg, uu, row("g_ffn"), w["w_ffn_gate"], w["w_ffn_up"], w["w_ffn_down"], carry))
    comm.reduce("ffn_gate", {"w_ffn_gate": _tn("dw_ffn_gate", dgg, h2)})
    comm.reduce("w_o", {"w_o": quarters(_tn("dw_o", *_in_hbm([merged, dx1])))})
    comm.reduce("ffn_up", {"w_ffn_up": comm.run(lambda carry: _tn("dw_ffn_up", duu, h2, carry=carry))[0]})
    comm.reduce("ffn_down", {"w_ffn_down": comm.run(lambda carry: _tn("dw_ffn_down", act, dx2, carry=carry),
                                                    hold=("ffn_gate", "w_o"))[0]})
    dya, dyb, dgp, dpa, dpb = comm.run(lambda carry: _merge_bwd(
        dx1, gp, pa, pb, w["w_a_out"], w_b_out, w_o, carry), hold=("ffn_gate", "ffn_up"))
    comm.reduce("merge", {"w_a_out": _tn("dw_a_out", ya, dpa, col_chunk=AC), "w_b_out": quarters(_tn("dw_b_out", yb, dpb))})
    dua, dq, dy, lr, li, acc_a, acc_s = comm.run(lambda carry: _s5_bwd(
        dya, y, ua, sr, si, bbr_d, bbi_d, ccr_d, cci_d, dsk, con_rev, w_glu, row("b_glu"), carry), hold=("ffn_down",))
    dub, dpr, dpi, acc_l = comm.run(lambda carry: _lru_bwd(
        dyb, xc, rg, ig, hp, ub, w["conv_w"], wr_d, wi_d, sp, -_sig(-lam), carry))
    gx, dz, acc_g, acc_b = _inproj_bwd(x, dx1, dua, dub, dgp, row("g_mix"), w["w_in"])
    half = (D // 2,)
    comm.reduce("in_lo", {"w_in_lo": comm.run(lambda carry: _tn(
        "dw_in_lo", h, dz, col_chunk=QC, a_block=(0,) + half, carry=carry))[0]})
    comm.reduce("in_hi", {"w_in_hi": comm.run(lambda carry: _tn(
        "dw_in_hi", h, dz, col_chunk=QC, a_block=(1,) + half, carry=carry))[0], "w_glu": quarters(_tn("dw_glu", zg, dq))})
    d_wr, d_wi = comm.run(lambda carry: _tn_blocks("dw_r_i", xc, [dpr, dpi], HD, HD, carry))
    d_bbr, d_bbi = comm.run(lambda carry: _tn_blocks("d_bb", ua, [lr, li], NP, NS, carry))
    d_ccr, d_cci = comm.run(lambda carry: _tn_blocks("d_cc", dy, [sr, si], NP, NS, carry))
    comm.drain()
    sums = {"ple": acc_p, "ffn": acc_f, "mix": acc_g, "b_in": acc_b, "lru": acc_l, "s5": acc_s, "s5_a": acc_a}
    blocks = {"bb_re": d_bbr, "bb_im": d_bbi,
              "cc_re": d_ccr, "cc_im": d_cci,
              "w_r": d_wr, "w_i": d_wi}
    return gx, sums, blocks


def _replicated_grads(w, sums, blocks):
    grouped = lambda e, groups: jnp.transpose(e.reshape(e.shape[0], groups, -1), (1, 0, 2))
    d_ar, d_ai = sums["s5_a"][0].reshape(NG, NS), sums["s5_a"][1].reshape(NG, NS)
    d_bbr, d_bbi = grouped(blocks["bb_re"], NG), grouped(blocks["bb_im"], NG)
    _, vjp = jax.vjp(_s5_discretize, w["lam_re"], w["lam_im"], w["log_dt"], w["s5_b_re"], w["s5_b_im"])
    g = dict(zip(("lam_re", "lam_im", "log_dt", "s5_b_re", "s5_b_im"), vjp((d_ar, d_ai, d_bbr, d_bbi))))
    g["s5_c_re"] = grouped(blocks["cc_re"], NG)
    g["s5_c_im"] = -grouped(blocks["cc_im"], NG)
    g["w_r"], g["w_i"] = grouped(blocks["w_r"], NH), grouped(blocks["w_i"], NH)
    g["s5_d"] = sums["s5"][0].reshape(NG, NP)
    g["b_r"] = sums["lru"][1].reshape(NH, HD)
    g["b_i"] = sums["lru"][2].reshape(NH, HD)
    return g


ACC_ROWS = {"g_mix": ("mix", 0), "b_in": ("b_in", 0), "g_ffn": ("ffn", 0), "g_ple_gate": ("ple", 0),
            "b_ple_gate": ("ple", 1), "g_ple": ("ple", 2), "g_final": ("ple", 3), "b_glu": ("s5", 1),
            "lru_lambda": ("lru", 0), "conv_b": ("lru", 3)}
LOSS_ROW = ("ple", 4)
CONV_W_ROWS = ("lru", 4)


SHARDED = [("w_in", (D, QC)), ("w_glu", (S5W // NCHIP, S5W)), ("w_a_out", (S5W, AC)), ("w_b_out", (LW // NCHIP, D)),
           ("w_o", (D // NCHIP, D)), ("w_ffn_gate", (FC, D)), ("w_ffn_up", (FC, D)), ("w_ffn_down", (FC, D)),
           ("w_ple_gate", (D // NCHIP, D)), ("w_ple", (PLE, AC))]
TRANSPOSED = ("w_ffn_gate", "w_ffn_up", "s5_b_re", "s5_b_im")
CONV_SHARD = (4, LW // NCHIP)


def _mesh_pos():
    return lax.axis_index("x"), lax.axis_index("y"), lax.axis_index("c")


def _other_chips(x, y):
    return [(1 - x, y), (x, 1 - y), (1 - x, 1 - y)]


def _half_rows(c, rows, align):
    return pl.ds(pl.multiple_of(c * (rows // 2), align), rows // 2)


def _run_now(name, carry):
    c_in, c_out = len(carry.operands), len(carry.out_shapes)

    def body(*refs):
        ins, outs, sems = refs[:c_in], refs[c_in:c_in + c_out], refs[c_in + c_out:]
        carry.start(ins, outs, sems)
        carry.finish(ins, outs, sems)

    return pl.pallas_call(body, name=name, in_specs=[ANY] * c_in, out_specs=[ANY] * c_out,
                          out_shape=list(carry.out_shapes), scratch_shapes=list(carry.sems),
                          input_output_aliases=dict(carry.aliases))(*_in_hbm(carry.operands))


def _gather_group(shards, split):
    n = len(shards)

    def copies(srcs, outs, sems):
        send_sems, recv_sems = sems
        x, y, c = _mesh_pos()
        k0 = 2 * x + y
        sib = (x, y, 1 - c)
        chips = _other_chips(x, y)

        def remote(src, dst, j, i, to):
            return pltpu.make_async_remote_copy(src_ref=src, dst_ref=dst, send_sem=send_sems.at[j, i],
                                                recv_sem=recv_sems.at[j, i], device_id=to, device_id_type=MESH)

        def rows(ref, i, core, *lead):
            if not split[i]:
                return ref.at[lead] if lead else ref
            return ref.at[(*lead, _half_rows(core, shards[i].shape[0], 16))]

        own = [remote(s, o.at[k0], 6, i, sib) for i, (s, o) in enumerate(zip(srcs, outs))]
        ici, landed, fwd, fwd_landed = [], [], [], []
        for j, chip in enumerate(chips):
            kj = 2 * chip[0] + chip[1]
            pairs = list(enumerate(zip(srcs, outs)))
            ici.append([remote(rows(s, i, c), rows(o, i, c, k0), j, i, (*chip, c)) for i, (s, o) in pairs])
            landed.append([remote(rows(s, i, c), rows(o, i, c, kj), j, i, (*chip, c)) for i, (s, o) in pairs])
            fwd.append([remote(rows(o, i, c, kj), rows(o, i, c, kj), 3 + j, i, sib) for i, (s, o) in pairs if split[i]])
            fwd_landed.append([remote(rows(o, i, 1 - c, kj), rows(o, i, 1 - c, kj), 3 + j, i, sib)
                               for i, (s, o) in pairs if split[i]])
        return own, ici, landed, fwd, fwd_landed

    def start(srcs, outs, sems):
        own, ici, _, _, _ = copies(srcs, outs, sems)
        for cp in own + [cp for per_chip in ici for cp in per_chip]:
            cp.start()

    def finish(srcs, outs, sems):
        own, ici, landed, fwd, fwd_landed = copies(srcs, outs, sems)
        passed = [i for i in range(n) if split[i]]
        for j in range(3):
            for i, cp in enumerate(landed[j]):
                cp.wait_recv()
                if split[i]:
                    fwd[j][passed.index(i)].start()
        for j in range(3):
            for cp in fwd_landed[j]:
                cp.wait_recv()
        for cp in own:
            cp.wait_recv()
        for cp in own + [cp for per_chip in ici + fwd for cp in per_chip]:
            cp.wait_send()

    return _Carried(shards, [_far((NCHIP,) + s.shape, s.dtype) for s in shards],
                    [pltpu.SemaphoreType.DMA((7, n)), pltpu.SemaphoreType.DMA((7, n))], start, finish)


def _to_bf16_group(name, arrays, carry):
    n = len(arrays)

    def body(*refs):
        for src, dst in zip(refs[:n], refs[n:]):
            dst[...] = src[...].astype(BF)

    specs = [pl.BlockSpec((a.shape[0] // 2, a.shape[1]), lambda i: (i, 0)) for a in arrays]
    return _pallas_call(body, carry, name=name, grid=(2,), in_specs=specs, out_specs=specs,
                        out_shape=[_far(a.shape, BF) for a in arrays], compiler_params=_params(48))(*arrays)


def _each_copy(copies, carried, out_shapes, sems, aliases=None):
    def start(ins, outs, sem_refs):
        for cp in copies(ins, outs, sem_refs):
            cp.start()

    def finish(ins, outs, sem_refs):
        for cp in copies(ins, outs, sem_refs):
            cp.wait()

    return _Carried(carried, out_shapes, sems, start, finish, aliases)


def _swap_group(grads):
    n = len(grads)

    def copies(srcs, outs, sems):
        send_sems, recv_sems = sems
        x, y, c = _mesh_pos()
        return [pltpu.make_async_remote_copy(src_ref=s.at[:, _half_rows(1 - c, s.shape[1], 8)], dst_ref=o,
                                             send_sem=send_sems.at[i], recv_sem=recv_sems.at[i], device_id=(x, y, 1 - c),
                                             device_id_type=MESH) for i, (s, o) in enumerate(zip(srcs, outs))]

    return _each_copy(copies, grads, [pltpu.HBM((NCHIP, g.shape[1] // 2, g.shape[2]), F32) for g in grads],
                      [pltpu.SemaphoreType.DMA((n,)), pltpu.SemaphoreType.DMA((n,))])


def _add_sibling_group(tag, kc_idx, grads, gots):
    n = len(grads)

    def body(kc_ref, *refs):
        for g, rx, p, pb in zip(refs[:n], refs[n:2 * n], refs[2 * n:3 * n], refs[3 * n:]):
            s = g[...] + rx[...]
            pb[...] = s.astype(BF)

            @pl.when(pl.program_id(0) == kc_ref[0])
            def _():
                p[...] = s

    halves = [pl.BlockSpec((None,) + rx.shape[1:], lambda k, kc_ref: (k, 0, 0)) for rx in gots]
    mine = [pl.BlockSpec((None,) + rx.shape[1:], lambda k, kc_ref: (k, kc_ref[1], 0)) for rx in gots]
    own = [pl.BlockSpec(rx.shape[1:], lambda k, kc_ref: (0, 0)) for rx in gots]
    outs = _pallas_call(
        body, name="add_sibling_" + tag,
        grid_spec=pltpu.PrefetchScalarGridSpec(num_scalar_prefetch=1, grid=(NCHIP,), in_specs=mine + halves,
                                               out_specs=own + halves),
        out_shape=[pltpu.HBM(rx.shape[1:], F32) for rx in gots] + [pltpu.HBM(rx.shape, BF) for rx in gots],
        compiler_params=_params(48),
    )(kc_idx, *_in_hbm(list(grads) + list(gots)))
    return outs[:n], outs[n:]


def _exchange_group(parts):
    n = len(parts)

    def copies(srcs, outs, sems):
        send_sems, recv_sems = sems
        x, y, c = _mesh_pos()
        return [pltpu.make_async_remote_copy(
            src_ref=s.at[2 * chip[0] + chip[1]], dst_ref=o.at[j], send_sem=send_sems.at[j, i],
            recv_sem=recv_sems.at[j, i], device_id=(*chip, c), device_id_type=MESH)
            for j, chip in enumerate(_other_chips(x, y)) for i, (s, o) in enumerate(zip(srcs, outs))]

    return _each_copy(copies, parts, [pltpu.HBM((3,) + p.shape[1:], BF) for p in parts],
                      [pltpu.SemaphoreType.DMA((3, n)), pltpu.SemaphoreType.DMA((3, n))])


def _add_chips_group(tag, kc_idx, parts, arrived):
    n = len(parts)

    def body(kc_ref, *refs):
        for p, rx, t in zip(refs[:n], refs[n:2 * n], refs[2 * n:]):
            t[...] = ((p[...] + rx[0].astype(F32)) + rx[1].astype(F32)) + rx[2].astype(F32)

    outs = _pallas_call(
        body, name="add_chips_" + tag,
        grid_spec=pltpu.PrefetchScalarGridSpec(
            num_scalar_prefetch=1, grid=(1,),
            in_specs=([pl.BlockSpec(rx.shape[1:], lambda i, kc_ref: (0, 0)) for rx in arrived]
                      + [pl.BlockSpec(rx.shape, lambda i, kc_ref: (0, 0, 0)) for rx in arrived]),
            out_specs=[pl.BlockSpec((None,) + rx.shape[1:], lambda i, kc_ref: (kc_ref[1], 0, 0)) for rx in arrived]),
        out_shape=[pltpu.HBM((2,) + rx.shape[1:], F32) for rx in arrived],
        compiler_params=_params(48),
    )(kc_idx, *_in_hbm(list(parts) + list(arrived)))
    return list(outs)


def _join_group(halves):
    n = len(halves)

    def copies(bufs, sems):
        send_sems, recv_sems = sems
        x, y, c = _mesh_pos()
        sib = (x, y, 1 - c)
        sends = [pltpu.make_async_remote_copy(src_ref=b.at[c], dst_ref=b.at[c], send_sem=send_sems.at[i],
                                              recv_sem=recv_sems.at[i], device_id=sib, device_id_type=MESH)
                 for i, b in enumerate(bufs)]
        landed = [pltpu.make_async_remote_copy(src_ref=b.at[c], dst_ref=b.at[1 - c], send_sem=send_sems.at[i],
                                               recv_sem=recv_sems.at[i], device_id=sib, device_id_type=MESH)
                  for i, b in enumerate(bufs)]
        return sends, landed

    def start(_, bufs, sems):
        for cp in copies(bufs, sems)[0]:
            cp.start()

    def finish(_, bufs, sems):
        sends, landed = copies(bufs, sems)
        for cp in landed:
            cp.wait_recv()
        for cp in sends:
            cp.wait_send()

    return _Carried(halves, [pltpu.HBM(h.shape, F32) for h in halves],
                    [pltpu.SemaphoreType.DMA((n,)), pltpu.SemaphoreType.DMA((n,))], start, finish,
                    {i: i for i in range(n)})


def _combine(carries):
    operands, out_shapes, sems, aliases, spans = [], [], [], {}, []
    for c in carries:
        aliases.update({len(operands) + i: len(out_shapes) + o for i, o in c.aliases.items()})
        spans.append((len(operands), len(out_shapes), len(sems)))
        operands += list(c.operands)
        out_shapes += list(c.out_shapes)
        sems += list(c.sems)

    def each(phase):
        def run(ins, outs, sem_refs):
            for c, (a, b, s) in zip(carries, spans):
                getattr(c, phase)(ins[a:a + len(c.operands)], outs[b:b + len(c.out_shapes)], sem_refs[s:s + len(c.sems)])
        return run

    return _Carried(operands, out_shapes, sems, each("start"), each("finish"), aliases)


def _allreduce_small(arrays, wire):
    n = len(arrays)
    halves = [(a.shape[0], a.shape[1] // 2) for a in arrays]

    def body(*refs):
        srcs, outs = refs[:n], refs[n:2 * n]
        mine_bufs, sib_bufs, chip_bufs, total_bufs = (refs[k * n:(k + 1) * n] for k in range(2, 6))
        send_sems, recv_sems, local_sems = refs[6 * n:]
        x, y, c = _mesh_pos()
        k0 = 2 * x + y
        sib = (x, y, 1 - c)

        def remote(src, dst, j, i, to):
            return pltpu.make_async_remote_copy(src_ref=src, dst_ref=dst, send_sem=send_sems.at[j, i],
                                                recv_sem=recv_sems.at[j, i], device_id=to, device_id_type=MESH)

        def cols(ref, i, core):
            return ref.at[:, pl.ds(pl.multiple_of(core * halves[i][1], LANE), halves[i][1])]

        swaps = [remote(cols(s, i, 1 - c), b, 0, i, sib) for i, (s, b) in enumerate(zip(srcs, sib_bufs))]
        own = [pltpu.make_async_copy(cols(s, i, c), m, local_sems.at[i]) for i, (s, m) in enumerate(zip(srcs, mine_bufs))]
        for cp in swaps + own:
            cp.start()
        for cp in swaps + own:
            cp.wait()
        for m, b, buf in zip(mine_bufs, sib_bufs, chip_bufs):
            buf[k0] = (m[...] + b[...]).astype(buf.dtype)
        chips = _other_chips(x, y)
        sends = [remote(buf.at[k0], buf.at[k0], 1 + j, i, (*chip, c))
                 for j, chip in enumerate(chips) for i, buf in enumerate(chip_bufs)]
        for cp in sends:
            cp.start()
        for j, chip in enumerate(chips):
            for i, buf in enumerate(chip_bufs):
                remote(buf.at[k0], buf.at[2 * chip[0] + chip[1]], 1 + j, i, (*chip, c)).wait_recv()
        for cp in sends:
            cp.wait_send()
        for t, buf in zip(total_bufs, chip_bufs):
            t[...] = ((buf[0].astype(F32) + buf[1].astype(F32)) + buf[2].astype(F32)) + buf[3].astype(F32)
        joins = [remote(t, cols(o, i, c), 4, i, sib) for i, (t, o) in enumerate(zip(total_bufs, outs))]
        keep = [pltpu.make_async_copy(t, cols(o, i, c), local_sems.at[i]) for i, (t, o) in enumerate(zip(total_bufs, outs))]
        for cp in joins + keep:
            cp.start()
        for i, (t, o) in enumerate(zip(total_bufs, outs)):
            remote(t, cols(o, i, 1 - c), 4, i, sib).wait_recv()
        for cp in joins:
            cp.wait_send()
        for cp in keep:
            cp.wait()

    specs = [_full(a.shape) for a in arrays]
    return _pallas_call(
        body, name="allreduce_small", grid=(1,), in_specs=specs, out_specs=specs,
        out_shape=[_sds(a.shape) for a in arrays],
        scratch_shapes=([pltpu.VMEM(h, F32) for h in halves] + [pltpu.VMEM(h, F32) for h in halves]
                        + [pltpu.VMEM((NCHIP,) + h, dt) for h, dt in zip(halves, wire)] + [pltpu.VMEM(h, F32) for h in halves]
                        + [pltpu.SemaphoreType.DMA((5, n)), pltpu.SemaphoreType.DMA((5, n)), pltpu.SemaphoreType.DMA((n,))]),
        compiler_params=_params(32),
    )(*arrays)


def _adamw_terms(w, g, m, v):
    m = ADAM_B1 * m + (1.0 - ADAM_B1) * g
    v = ADAM_B2 * v + (1.0 - ADAM_B2) * jnp.square(g)
    m_hat = m / (1.0 - ADAM_B1 ** ADAM_STEP)
    v_hat = v / (1.0 - ADAM_B2 ** ADAM_STEP)
    return -ADAM_LR * (m_hat / (jnp.sqrt(v_hat) + ADAM_EPS) + ADAM_WD * w), m, v


ADAM_STEPS = 4


def _adamw_group(tag, ws, gs, ms, vs):
    n = len(ws)

    def body(*refs):
        ins, outs = refs[:4 * n], refs[4 * n:]
        for i in range(n):
            w, g, m, v = (ins[k * n + i][...] for k in range(4))
            outs[i][...] = g
            outs[n + i][...], outs[2 * n + i][...], outs[3 * n + i][...] = _adamw_terms(w, g, m, v)

    specs = [pl.BlockSpec((w.shape[0] // ADAM_STEPS, w.shape[1]), lambda i: (i, 0)) for w in ws]
    outs = _pallas_call(
        body, name="adamw_" + tag, grid=(ADAM_STEPS,), in_specs=specs * 4, out_specs=specs * 4,
        out_shape=[_sds(w.shape) for w in ws] * 4, compiler_params=_params(48),
    )(*_in_hbm(list(ws) + list(gs) + list(ms) + list(vs)))
    return outs[:n], outs[n:2 * n], outs[2 * n:3 * n], outs[3 * n:]


def _adamw_replicated(sums, row_of, direct):
    ns, nr, nd = len(sums), len(row_of), len(direct)

    def body(*refs):
        sum_refs = refs[:ns]
        ins = refs[ns:ns + 3 * nr + 4 * nd]
        outs = refs[ns + 3 * nr + 4 * nd:]
        for i, (_, _, _, si, row) in enumerate(row_of):
            w_ref, m_ref, v_ref = ins[3 * i:3 * i + 3]
            g = sum_refs[si][row:row + 1, :]
            outs[4 * i][...] = g
            outs[4 * i + 1][...], outs[4 * i + 2][...], outs[4 * i + 3][...] = _adamw_terms(w_ref[...], g, m_ref[...], v_ref[...])
        for i in range(nd):
            w_ref, m_ref, v_ref, g_ref = ins[3 * nr + 4 * i:3 * nr + 4 * i + 4]
            o = outs[4 * (nr + i):4 * (nr + i) + 4]
            g = g_ref[...]
            o[0][...] = g
            o[1][...], o[2][...], o[3][...] = _adamw_terms(w_ref[...], g, m_ref[...], v_ref[...])

    operands = list(sums)
    shapes = []
    for w, m, v, _, _ in row_of:
        operands += [w, m, v]
        shapes += [w.shape] * 4
    for w, m, v, g in direct:
        operands += [w, m, v, g]
        shapes += [w.shape] * 4
    flat = _pallas_call(
        body, name="adamw_replicated", grid=(1,), in_specs=[_full(a.shape) for a in operands],
        out_specs=[_full(s) for s in shapes], out_shape=[_sds(s) for s in shapes],
        compiler_params=_params(56),
    )(*operands)
    return [flat[4 * i:4 * i + 4] for i in range(nr + nd)]


class _Exchanges:
    def __init__(self, shards, conv_w, chip, core, apply):
        self.shards, self.conv_w, self.apply = shards, conv_w, apply
        self.active, self.calls = [], 0
        self.chip_core_idx = jnp.stack([chip, core]).astype(jnp.int32)

    def first(self):
        later = [n for n in self.shards if n != "w_in"]
        carry = _gather_group([self.shards["w_in"].astype(BF), self.conv_w], [True, False])
        outs = _to_bf16_group("gather_first", [self.shards[n] for n in later], carry)
        self.shards = dict(zip(later, outs))
        return {"w_in": outs[len(later)], "conv_w": jnp.transpose(outs[len(later) + 1], (1, 0, 2)).reshape(4, LW)}

    def gather(self, names):
        return _gather_group([self.shards[n] for n in names], [True] * len(names))

    def reduce(self, tag, grads):
        self.active.append({"tag": tag, "names": list(grads), "stage": 0, "grads": list(grads.values())})

    def run(self, call, hold=()):
        groups = [g for g in self.active if g["tag"] not in hold]
        carries = [self._exchange_of(g) for g in groups]
        carry = _combine(carries)
        outs = list(call(carry))
        own = len(outs) - len(carry.out_shapes)
        landed = outs[own:]
        for g, c in zip(groups, carries):
            self._sum_after(g, landed[:len(c.out_shapes)])
            landed = landed[len(c.out_shapes):]
        self.active = [g for g in self.active if g["stage"] < 3]
        return outs[:own]

    def _exchange_of(self, g):
        if g["stage"] == 0:
            return _swap_group(g["grads"])
        if g["stage"] == 1:
            return _exchange_group(g["bf16"])
        return _join_group(g["halves"])

    def _sum_after(self, g, landed):
        if g["stage"] == 0:
            g["f32"], g["bf16"] = _add_sibling_group(g["tag"], self.chip_core_idx, g["grads"], landed)
        elif g["stage"] == 1:
            g["halves"] = _add_chips_group(g["tag"], self.chip_core_idx, g["f32"], landed)
        else:
            self.apply(g["tag"], g["names"], [t.reshape(2 * t.shape[1], t.shape[2]) for t in landed])
        g["stage"] += 1

    def drain(self):
        while self.active:
            self.calls += 1
            self.run(lambda carry: _run_now("reduce_%d" % self.calls, carry))


INPUT_NAMES = (["x", "p"] + [n for n in
               ["g_mix", "w_in", "b_in", "lam_re", "lam_im", "log_dt", "s5_b_re", "s5_b_im", "s5_c_re", "s5_c_im", "s5_d",
                "w_glu", "b_glu", "conv_w", "conv_b", "w_r", "b_r", "w_i", "b_i", "lru_lambda", "w_a_out", "w_b_out", "w_o",
                "g_ffn", "w_ffn_gate", "w_ffn_up", "w_ffn_down", "g_ple_gate", "w_ple_gate", "b_ple_gate", "w_ple", "g_ple",
                "g_final"]])
WEIGHT_NAMES = INPUT_NAMES[2:]


def kernel(*args):
    names = INPUT_NAMES + ["loss_target"] + ["m_" + n for n in WEIGHT_NAMES] + ["v_" + n for n in WEIGHT_NAMES]
    assert len(args) == len(names)
    given = dict(zip(names, args))

    def view(name):
        a = given[name]
        return jnp.swapaxes(a, -1, -2) if name.endswith(TRANSPOSED) else a

    def unview(name, a):
        return jnp.swapaxes(a, -1, -2) if name in TRANSPOSED else a

    def local(name):
        return view(name) if name.endswith("g_final") else view(name)[0]

    xi, yi, ci = _mesh_pos()
    k0 = 2 * xi + yi
    x, p, tgt = given["x"][0], given["p"][0, 0], given["loss_target"][0]

    results = {}

    row_halves = {}

    def apply(tag, names, totals):
        totals = dict(zip(names, totals))
        row_halves.update({n: totals.pop(n) for n in names if n in ("w_in_lo", "w_in_hi")})
        if len(row_halves) == 2:
            totals["w_in"] = jnp.concatenate([row_halves.pop("w_in_lo"), row_halves.pop("w_in_hi")])
        names = list(totals)
        if not names:
            return
        new = _adamw_group(tag, [local(n) for n in names], list(totals.values()), [local("m_" + n) for n in names],
                           [local("v_" + n) for n in names])
        for kind, arrays in zip(("grad", "delta", "new_m", "new_v"), new):
            for n, arr in zip(names, arrays):
                results[kind, n] = unview(n, arr[None])

    comm = _Exchanges({n: local(n) for n, _ in SHARDED}, local("conv_w"), k0, ci, apply)
    w = {n: local(n) for n in WEIGHT_NAMES if n != "conv_w" and n not in dict(SHARDED)}
    gx, sums, blocks = _local_step(x, p, tgt, w, comm)

    sum_names, block_names = list(sums), list(blocks)
    red = _allreduce_small([sums[n] for n in sum_names] + [blocks[n] for n in block_names],
                           [F32] * len(sum_names) + [BF] * len(block_names))
    sums = dict(zip(sum_names, red[:len(sum_names)]))
    blocks = dict(zip(block_names, red[len(sum_names):]))
    loss = jnp.sum(sums[LOSS_ROW[0]][LOSS_ROW[1]])
    direct_g = _replicated_grads(w, sums, blocks)
    conv_rows = sums[CONV_W_ROWS[0]][CONV_W_ROWS[1]:CONV_W_ROWS[1] + 4]
    direct_g["conv_w"] = lax.dynamic_slice(conv_rows, (0, k0 * CONV_SHARD[1]), CONV_SHARD)
    as_row = lambda a: a.reshape(1, -1)
    row_names = list(ACC_ROWS)
    row_of = [(as_row(given[n]), as_row(given["m_" + n]), as_row(given["v_" + n]),
               sum_names.index(ACC_ROWS[n][0]), ACC_ROWS[n][1]) for n in row_names]
    direct_names = list(direct_g)
    direct = [(view(n), view("m_" + n), view("v_" + n), direct_g[n].reshape(view(n).shape)) for n in direct_names]
    done = _adamw_replicated([sums[n] for n in sum_names], row_of, direct)
    for n, four in zip(row_names + direct_names, done):
        for kind, arr in zip(("grad", "delta", "new_m", "new_v"), four):
            results[kind, n] = unview(n, arr).reshape(given[n].shape)

    out = [loss, gx[None]]
    for kind in ("grad", "delta", "new_m", "new_v"):
        out += [results[kind, n] for n in WEIGHT_NAMES]
    return tuple(out)
```

```python
import functools
import math

import jax
import jax.numpy as jnp
from jax import lax
from jax.experimental import pallas as pl
from jax.experimental.pallas import tpu as pltpu

F32 = jnp.float32
BF = jnp.bfloat16

D = 1024
S5W = 512
NG, NS, NP = 32, 64, 16
GN = NG * NS
LW = 1024
NH, HD = 16, 64
LRU_C = 8.0
FH = 2816
NCHIP = 4
FC = FH // NCHIP
PLE = 256
INC = S5W + LW + 2 * D
EPS = 1e-6
ADAM_LR, ADAM_B1, ADAM_B2, ADAM_EPS, ADAM_WD, ADAM_STEP = 0.001, 0.9, 0.999, 1e-08, 0.01, 10

TM = 256
TK = 1024
LC = 512
SUB = 8
VMEM_MB = 1024 * 1024
MESH = pl.DeviceIdType.MESH
ANY = pl.BlockSpec(memory_space=pl.ANY)


def _mm(a, b):
    return jnp.dot(a.astype(BF), b.astype(BF), preferred_element_type=F32)


def _mm_nt(a, b):
    return lax.dot_general(a.astype(BF), b.astype(BF), (((1,), (1,)), ((), ())), preferred_element_type=F32)


def _mm_tn(a, b):
    return lax.dot_general(a.astype(BF), b.astype(BF), (((0,), (0,)), ((), ())), preferred_element_type=F32)


def _blockdiag_mm(x, blocks_ref):
    n, rows, _ = blocks_ref.shape
    return jnp.concatenate([jnp.dot(x[:, j * rows:(j + 1) * rows], blocks_ref[j], preferred_element_type=F32)
                            for j in range(n)], axis=1)


def _blockdiag_mm_t(x, blocks_ref):
    n, _, wide = blocks_ref.shape
    return jnp.concatenate([lax.dot_general(x[:, j * wide:(j + 1) * wide], blocks_ref[j], (((1,), (1,)), ((), ())),
                                            preferred_element_type=F32) for j in range(n)], axis=1)


def _rms(x):
    r = lax.rsqrt(jnp.mean(x * x, axis=-1, keepdims=True) + EPS)
    return x * r, r


def _rms_bwd(dy, xh, r, g):
    dxh = dy * g
    return r * (dxh - xh * jnp.mean(dxh * xh, axis=-1, keepdims=True))


def _colsum(x):
    return jnp.sum(x, axis=0, keepdims=True)


def _sig(x):
    return jax.nn.sigmoid(x)


def _gelu_grad(x):
    c = math.sqrt(2.0 / math.pi)
    t = jnp.tanh(c * (x + 0.044715 * x * x * x))
    return 0.5 * (1.0 + t) + 0.5 * x * (1.0 - t * t) * c * (1.0 + 3.0 * 0.044715 * x * x)


def _neg_expm1(x):
    series = -x * (1.0 + x * (0.5 + x * (1.0 / 6.0 + x * (1.0 / 24.0))))
    return jnp.where(x > -0.03, series, 1.0 - jnp.exp(x))


def _tok(width):
    return pl.BlockSpec((TM, width), lambda i: (i, 0))


def _tok_rev(width, nt):
    return pl.BlockSpec((TM, width), lambda i: (nt - 1 - i, 0))


def _full(shape):
    return pl.BlockSpec(shape, lambda i: (0,) * len(shape))


def _params(vmem_mb, **kw):
    return pltpu.CompilerParams(dimension_semantics=("arbitrary",), vmem_limit_bytes=vmem_mb * VMEM_MB, **kw)


def _sds(shape, dtype=F32):
    return jax.ShapeDtypeStruct(shape, dtype)


def _far(shape, dtype=F32):
    return pltpu.HBM(shape, dtype)


class _Carried:
    def __init__(self, operands, out_shapes, sems, start, finish, aliases=None):
        self.operands, self.out_shapes, self.sems = list(operands), list(out_shapes), list(sems)
        self.start, self.finish, self.aliases = start, finish, dict(aliases or {})


def _in_hbm(arrays):
    return [pltpu.with_memory_space_constraint(a, pltpu.HBM) for a in arrays]


def _pallas_call(body, carry=None, **kw):
    if carry is None:
        return pl.pallas_call(body, **kw)

    def at_step(corner):
        hit = [pl.program_id(d) == (size - 1 if corner else 0) for d, size in enumerate(kw["grid"])]
        return functools.reduce(jnp.logical_and, hit)

    name, grid, compiler_params = kw["name"], kw["grid"], kw["compiler_params"]
    in_specs, out_specs, out_shape = list(kw["in_specs"]), list(kw["out_specs"]), list(kw["out_shape"])
    scratch_shapes = list(kw.get("scratch_shapes", ()))
    n_in, n_out, n_scr = len(in_specs), len(out_specs), len(scratch_shapes)
    c_in, c_out = len(carry.operands), len(carry.out_shapes)

    def full_body(*refs):
        ins, refs = refs[:n_in], refs[n_in:]
        c_ins, refs = refs[:c_in], refs[c_in:]
        outs, refs = refs[:n_out], refs[n_out:]
        c_outs, refs = refs[:c_out], refs[c_out:]
        scratch, c_sems = refs[:n_scr], refs[n_scr:]

        @pl.when(at_step(0))
        def _():
            carry.start(c_ins, c_outs, c_sems)

        body(*ins, *outs, *scratch)

        @pl.when(at_step(1))
        def _():
            carry.finish(c_ins, c_outs, c_sems)

    call = pl.pallas_call(
        full_body, name=name, grid=grid, in_specs=in_specs + [ANY] * c_in, out_specs=out_specs + [ANY] * c_out,
        out_shape=out_shape + list(carry.out_shapes), scratch_shapes=scratch_shapes + list(carry.sems),
        input_output_aliases={n_in + i: n_out + o for i, o in carry.aliases.items()},
        compiler_params=compiler_params)
    return lambda *operands: call(*operands, *_in_hbm(carry.operands))


def _resident(pairs, sems):
    first = pl.program_id(0) == 0
    copies = [pltpu.make_async_copy(src, dst, sems.at[j]) for j, (src, dst) in enumerate(pairs)]

    @pl.when(first)
    def _():
        for cp in copies:
            cp.start()

    def wait(j):
        @pl.when(first)
        def _():
            copies[j].wait()

    return wait


def _resident_now(pairs, sems):
    @pl.when(pl.program_id(0) == 0)
    def _():
        copies = [pltpu.make_async_copy(src, dst, sems.at[j]) for j, (src, dst) in enumerate(pairs)]
        for cp in copies:
            cp.start()
        for cp in copies:
            cp.wait()


def _row_iota(width):
    return lax.broadcasted_iota(jnp.int32, (SUB, width), 0)


def _bcast_row(x, row):
    return jnp.broadcast_to(x[row:row + 1, :], x.shape)


def _slab(k):
    return pl.ds(pl.multiple_of(k * SUB, SUB), SUB)


QC = INC // NCHIP
Z_PARTS = ((0, S5W), (S5W, S5W + LW), (S5W + LW, INC))


def _inproj_fwd(x, g_mix, w_in, b_in, carry=None):
    L = x.shape[0]

    def body(x_ref, g_ref, w_hbm, b_ref, h_ref, ua_ref, ub_ref, gp_ref, w_vm, w_sems):
        _resident_now([(w_hbm.at[k], w_vm.at[k]) for k in range(NCHIP)], w_sems)
        xh, _ = _rms(x_ref[...])
        h = (xh * g_ref[...]).astype(BF)
        h_ref[...] = h
        for k in range(NCHIP):
            lo, hi = k * QC, (k + 1) * QC
            z = jnp.dot(h, w_vm[k], preferred_element_type=F32) + b_ref[:, lo:hi]
            for ref, (a, b) in zip((ua_ref, ub_ref, gp_ref), Z_PARTS):
                s, e = max(lo, a), min(hi, b)
                if s < e:
                    ref[:, s - a:e - a] = z[:, s - lo:e - lo]

    return _pallas_call(
        body, carry, name="inproj_fwd", grid=(L // TM,),
        in_specs=[_tok(D), _full((1, D)), ANY, _full((1, INC))],
        out_specs=[_tok(D), _tok(S5W), _tok(LW), _tok(2 * D)],
        out_shape=[_far((L, D), BF), _far((L, S5W)), _far((L, LW)), _sds((L, 2 * D))],
        scratch_shapes=[pltpu.VMEM((NCHIP, D, QC), BF), pltpu.SemaphoreType.DMA((NCHIP,))],
        compiler_params=_params(40),
    )(*_in_hbm([x]), g_mix, *_in_hbm([w_in]), b_in)


def _inproj_bwd(x, dx1, dua, dub, dgp, g_mix, w_in, carry=None):
    L = x.shape[0]

    def body(x_ref, dx1_ref, dua_ref, dub_ref, dgp_ref, g_ref, w_hbm, gx_ref, dz_ref, dg_ref, db_ref, w_vm, w_sems):
        _resident_now([(w_hbm.at[k], w_vm.at[k]) for k in range(NCHIP)], w_sems)

        @pl.when(pl.program_id(0) == 0)
        def _():
            dg_ref[...] = jnp.zeros_like(dg_ref)
            db_ref[...] = jnp.zeros_like(db_ref)

        for src, (a, b) in zip((dua_ref, dub_ref, dgp_ref), Z_PARTS):
            d = src[...]
            dz_ref[:, a:b] = d.astype(BF)
            db_ref[0:1, a:b] += _colsum(d)
        dh = jnp.zeros((TM, D), F32)
        for k in range(NCHIP):
            dh = dh + lax.dot_general(dz_ref[:, k * QC:(k + 1) * QC], w_vm[k], (((1,), (1,)), ((), ())),
                                      preferred_element_type=F32)
        xh, r = _rms(x_ref[...])
        dg_ref[0:1, :] += _colsum(dh * xh)
        gx_ref[...] = dx1_ref[...] + _rms_bwd(dh, xh, r, g_ref[...])

    return _pallas_call(
        body, carry, name="inproj_bwd", grid=(L // TM,),
        in_specs=[_tok(D), _tok(D), _tok(S5W), _tok(LW), _tok(2 * D), _full((1, D)), ANY],
        out_specs=[_tok(D), _tok(INC), _full((SUB, D)), _full((SUB, INC))],
        out_shape=[_sds((L, D)), _sds((L, INC), BF), _sds((SUB, D)), _sds((SUB, INC))],
        scratch_shapes=[pltpu.VMEM((NCHIP, D, QC), BF), pltpu.SemaphoreType.DMA((NCHIP,))],
        compiler_params=_params(40),
    )(x, dx1, *_in_hbm([dua]), dub, dgp, g_mix, *_in_hbm([w_in]))


def _cscan(xr_ref, xi_ref, con_ref, cr_ref, ci_ref, reverse):
    n_slab = xr_ref.shape[0] // SUB
    width = xr_ref.shape[1]
    for lc in range(width // LC):
        cols = slice(lc * LC, (lc + 1) * LC)
        con = [con_ref[SUB * j:SUB * (j + 1), cols] for j in range(8)]

        def step(k, carry, cols=cols, con=con):
            cr, ci = carry
            rows = _slab(n_slab - 1 - k if reverse else k)
            xr, xi = xr_ref[rows, cols], xi_ref[rows, cols]
            for j, sh in enumerate((1, 2, 4)):
                mr, mi = con[2 * j], con[2 * j + 1]
                pr = pltpu.roll(xr, SUB - sh if reverse else sh, 0)
                pi = pltpu.roll(xi, SUB - sh if reverse else sh, 0)
                xr, xi = xr + mr * pr - mi * pi, xi + mr * pi + mi * pr
            xr, xi = xr + con[6] * cr - con[7] * ci, xi + con[6] * ci + con[7] * cr
            xr_ref[rows, cols] = xr
            xi_ref[rows, cols] = xi
            row = 0 if reverse else SUB - 1
            return _bcast_row(xr, row), _bcast_row(xi, row)

        cr, ci = lax.fori_loop(0, n_slab, step, (cr_ref[:, cols], ci_ref[:, cols]))
        cr_ref[:, cols] = cr
        ci_ref[:, cols] = ci


def _s5_fwd(ua, bbr, bbi, ccr, cci, dsk, con, w_glu, b_glu, carry=None):
    L = ua.shape[0]

    def body(ua_ref, bbr_hbm, bbi_hbm, ccr_hbm, cci_hbm, dsk_ref, con_ref, wg_ref, bg_ref,
             sr_ref, si_ref, y_ref, zg_ref, ya_ref, bbr_vm, bbi_vm, ccr_vm, cci_vm, cr_ref, ci_ref, w_sems):
        landed = _resident([(bbr_hbm, bbr_vm), (bbi_hbm, bbi_vm), (ccr_hbm, ccr_vm), (cci_hbm, cci_vm)], w_sems)

        @pl.when(pl.program_id(0) == 0)
        def _():
            cr_ref[...] = jnp.zeros_like(cr_ref)
            ci_ref[...] = jnp.zeros_like(ci_ref)

        u = ua_ref[...]
        ub = u.astype(BF)
        landed(0)
        sr_ref[...] = _blockdiag_mm(ub, bbr_vm)
        landed(1)
        si_ref[...] = _blockdiag_mm(ub, bbi_vm)
        _cscan(sr_ref, si_ref, con_ref, cr_ref, ci_ref, reverse=False)
        landed(2)
        landed(3)
        y = (_blockdiag_mm_t(sr_ref[...].astype(BF), ccr_vm) - _blockdiag_mm_t(si_ref[...].astype(BF), cci_vm)
             + dsk_ref[...] * u)
        y_ref[...] = y
        zg = jax.nn.gelu(y)
        zg_ref[...] = zg.astype(BF)
        q = _mm(zg, wg_ref[...]) + bg_ref[...]
        ya_ref[...] = (zg * _sig(q)).astype(BF)

    return _pallas_call(
        body, carry, name="s5_fwd", grid=(L // TM,),
        in_specs=[_tok(S5W), ANY, ANY, ANY, ANY, _full((1, S5W)), _full((8 * SUB, GN)),
                  _full((S5W, S5W)), _full((1, S5W))],
        out_specs=[_tok(GN), _tok(GN), _tok(S5W), _tok(S5W), _tok(S5W)],
        out_shape=[_sds((L, GN)), _sds((L, GN)), _far((L, S5W)), _far((L, S5W), BF), _far((L, S5W), BF)],
        scratch_shapes=[pltpu.VMEM((S5W // 128, 128, GN // (S5W // 128)), BF)] * 4 + [
                        pltpu.VMEM((SUB, GN), F32), pltpu.VMEM((SUB, GN), F32),
                        pltpu.SemaphoreType.DMA((4,))],
        compiler_params=_params(44),
    )(*_in_hbm([ua, bbr, bbi, ccr, cci]), dsk, con, w_glu, b_glu)


def _s5_bwd(dya, y, ua, sr, si, bbr, bbi, ccr, cci, dsk, con_rev, w_glu, b_glu, carry=None):
    L = ua.shape[0]
    nt = L // TM
    spt = TM // SUB
    n_slab = spt

    def halo_map(i):
        return (jnp.maximum((nt - 1 - i) * spt - 1, 0), 0)

    def body(dya_ref, y_ref, ua_ref, sr_ref, si_ref, hr_ref, hi_ref, bbr_hbm, bbi_hbm, ccr_hbm, cci_hbm,
             dsk_ref, con_ref, wg_ref, bg_ref,
             dua_ref, dq_ref, dy_ref, lr_ref, li_ref, da_ref, dsm_ref,
             bbr_vm, bbi_vm, ccr_vm, cci_vm, cr_ref, ci_ref, w_sems):
        i = pl.program_id(0)
        landed = _resident([(ccr_hbm, ccr_vm), (cci_hbm, cci_vm), (bbr_hbm, bbr_vm), (bbi_hbm, bbi_vm)], w_sems)

        @pl.when(i == 0)
        def _():
            cr_ref[...] = jnp.zeros_like(cr_ref)
            ci_ref[...] = jnp.zeros_like(ci_ref)
            da_ref[...] = jnp.zeros_like(da_ref)
            dsm_ref[...] = jnp.zeros_like(dsm_ref)

        u = ua_ref[...]
        yv = y_ref[...]
        dya = dya_ref[...]
        zg = jax.nn.gelu(yv)
        sg = _sig(_mm(zg, wg_ref[...]) + bg_ref[...])
        dq = dya * zg * sg * (1.0 - sg)
        dq_ref[...] = dq.astype(BF)
        dzg = dya * sg + _mm_nt(dq, wg_ref[...])
        dy = dzg * _gelu_grad(yv)
        dyb = dy.astype(BF)
        dy_ref[...] = dyb
        dsm_ref[0:1, :] += _colsum(dy * u)
        dsm_ref[1:2, :] += _colsum(dq)
        landed(0)
        lr_ref[...] = _blockdiag_mm(dyb, ccr_vm)
        landed(1)
        li_ref[...] = -_blockdiag_mm(dyb, cci_vm)
        _cscan(lr_ref, li_ref, con_ref, cr_ref, ci_ref, reverse=True)

        first_tile = (i == nt - 1)
        row = _row_iota(LC)
        for lc in range(GN // LC):
            cols = slice(lc * LC, (lc + 1) * LC)
            h_r = jnp.where(first_tile, 0.0, hr_ref[:, cols])
            h_i = jnp.where(first_tile, 0.0, hi_ref[:, cols])

            def step(k, acc, cols=cols, h_r=h_r, h_i=h_i):
                ar, ai = acc
                rows = _slab(k)
                prev = _slab(jnp.maximum(k - 1, 0))
                pr = jnp.where(k == 0, h_r, sr_ref[prev, cols])
                pi = jnp.where(k == 0, h_i, si_ref[prev, cols])
                spr = pltpu.roll(jnp.where(row == SUB - 1, pr, sr_ref[rows, cols]), 1, 0)
                spi = pltpu.roll(jnp.where(row == SUB - 1, pi, si_ref[rows, cols]), 1, 0)
                lr, li = lr_ref[rows, cols], li_ref[rows, cols]
                return ar + lr * spr + li * spi, ai + li * spr - lr * spi

            zero = jnp.zeros((SUB, LC), F32)
            ar, ai = lax.fori_loop(0, n_slab, step, (zero, zero))
            da_ref[0:1, cols] += _colsum(ar)
            da_ref[1:2, cols] += _colsum(ai)

        landed(2)
        landed(3)
        dua_ref[...] = (dy * dsk_ref[...] + _blockdiag_mm_t(lr_ref[...].astype(BF), bbr_vm)
                        + _blockdiag_mm_t(li_ref[...].astype(BF), bbi_vm))

    return _pallas_call(
        body, carry, name="s5_bwd", grid=(nt,),
        in_specs=[_tok_rev(S5W, nt), _tok_rev(S5W, nt), _tok_rev(S5W, nt), _tok_rev(GN, nt), _tok_rev(GN, nt),
                  pl.BlockSpec((SUB, GN), halo_map), pl.BlockSpec((SUB, GN), halo_map),
                  ANY, ANY, ANY, ANY, _full((1, S5W)), _full((8 * SUB, GN)), _full((S5W, S5W)), _full((1, S5W))],
        out_specs=[_tok_rev(S5W, nt), _tok_rev(S5W, nt), _tok_rev(S5W, nt), _tok_rev(GN, nt), _tok_rev(GN, nt),
                   _full((SUB, GN)), _full((SUB, S5W))],
        out_shape=[_sds((L, S5W)), _sds((L, S5W), BF), _sds((L, S5W), BF), _sds((L, GN)), _sds((L, GN)),
                   _sds((SUB, GN)), _sds((SUB, S5W))],
        scratch_shapes=[pltpu.VMEM((S5W // 128, 128, GN // (S5W // 128)), BF)] * 4 + [
                        pltpu.VMEM((SUB, GN), F32), pltpu.VMEM((SUB, GN), F32),
                        pltpu.SemaphoreType.DMA((4,))],
        compiler_params=_params(52),
    )(dya, y, ua, sr, si, sr, si, *_in_hbm([bbr, bbi, ccr, cci]), dsk, con_rev, w_glu, b_glu)


def _lru_gate_terms(rg, sp):
    log_a = -LRU_C * rg * sp
    a = jnp.exp(log_a)
    mult = jnp.sqrt(_neg_expm1(2.0 * log_a))
    return a, mult


def _lru_fwd(ub, conv_w, conv_b, wr, wi, b_r, b_i, sp, carry=None):
    L = ub.shape[0]
    n_slab = TM // SUB

    def body(ub_ref, cw_ref, cb_ref, wr_ref, wi_ref, br_ref, bi_ref, sp_ref,
             xc_ref, rg_ref, ig_ref, h_ref, hp_ref, a_ref, halo_ref, carry_ref):
        @pl.when(pl.program_id(0) == 0)
        def _():
            halo_ref[...] = jnp.zeros_like(halo_ref)
            carry_ref[...] = jnp.zeros_like(carry_ref)

        row = _row_iota(LW)
        taps = [cw_ref[k:k + 1, :] for k in range(4)]
        cb = cb_ref[...]

        def conv_step(k, prev):
            rows = _slab(k)
            cur = ub_ref[rows, :]
            acc = taps[3] * cur + cb
            for j in (1, 2, 3):
                acc = acc + taps[3 - j] * pltpu.roll(jnp.where(row >= SUB - j, prev, cur), j, 0)
            xc_ref[rows, :] = acc
            return cur

        halo_ref[...] = lax.fori_loop(0, n_slab, conv_step, halo_ref[...])

        xc = xc_ref[...]
        xcb = xc.astype(BF)
        rg = _sig(_blockdiag_mm(xcb, wr_ref) + br_ref[...])
        ig = _sig(_blockdiag_mm(xcb, wi_ref) + bi_ref[...])
        rg_ref[...] = rg
        ig_ref[...] = ig
        a, mult = _lru_gate_terms(rg, sp_ref[...])
        a_ref[...] = a
        h_ref[...] = mult * ig * xc

        rowc = _row_iota(LC)
        for lc in range(LW // LC):
            cols = slice(lc * LC, (lc + 1) * LC)

            def step(k, c, cols=cols):
                rows = _slab(k)
                av, b = a_ref[rows, cols], h_ref[rows, cols]
                for sh in (1, 2, 4):
                    keep = rowc >= sh
                    b = b + av * jnp.where(keep, pltpu.roll(b, sh, 0), 0.0)
                    av = av * jnp.where(keep, pltpu.roll(av, sh, 0), 1.0)
                h = b + av * c
                h_ref[rows, cols] = h
                hp_ref[rows, cols] = jnp.where(rowc == 0, c, pltpu.roll(h, 1, 0))
                return _bcast_row(h, SUB - 1)

            carry_ref[:, cols] = lax.fori_loop(0, n_slab, step, carry_ref[:, cols])

    return _pallas_call(
        body, carry, name="lru_fwd", grid=(L // TM,),
        in_specs=[_tok(LW), _full((4, LW)), _full((1, LW)), _full((LW // 128, 128, 128)), _full((LW // 128, 128, 128)),
                  _full((1, LW)), _full((1, LW)), _full((1, LW))],
        out_specs=[_tok(LW)] * 5,
        out_shape=[_far((L, LW))] * 5,
        scratch_shapes=[pltpu.VMEM((TM, LW), F32), pltpu.VMEM((SUB, LW), F32), pltpu.VMEM((SUB, LW), F32)],
        compiler_params=_params(40),
    )(*_in_hbm([ub]), conv_w, conv_b, wr, wi, b_r, b_i, sp)


def _lru_bwd(dyb, xc, rg, ig, hp, ub, conv_w, wr, wi, sp, dsp, carry=None):
    L = ub.shape[0]
    nt = L // TM
    spt = TM // SUB
    n_slab = spt

    def halo_map(i):
        return (jnp.maximum((nt - 1 - i) * spt - 1, 0), 0)

    def body(dh_ref, xc_ref, rg_ref, ig_ref, hp_ref, ub_ref, uh_ref, cw_ref, wr_ref, wi_ref, sp_ref, dsp_ref,
             dub_ref, dpr_ref, dpi_ref, acc_ref, a_ref, lam_ref, dxc_ref, carry_ref, next_ref):
        i = pl.program_id(0)

        @pl.when(i == 0)
        def _():
            carry_ref[...] = jnp.zeros_like(carry_ref)
            next_ref[...] = jnp.zeros_like(next_ref)
            acc_ref[...] = jnp.zeros_like(acc_ref)

        sp = sp_ref[...]
        rg, ig, xc = rg_ref[...], ig_ref[...], xc_ref[...]
        a, mult = _lru_gate_terms(rg, sp)
        a_ref[...] = a

        rowc = _row_iota(LC)
        for lc in range(LW // LC):
            cols = slice(lc * LC, (lc + 1) * LC)

            def step(k, c, cols=cols):
                rows = _slab(n_slab - 1 - k)
                av, dh = a_ref[rows, cols], dh_ref[rows, cols]
                b = av * dh
                for sh in (1, 2, 4):
                    keep = rowc < SUB - sh
                    b = b + av * jnp.where(keep, pltpu.roll(b, SUB - sh, 0), 0.0)
                    av = av * jnp.where(keep, pltpu.roll(av, SUB - sh, 0), 1.0)
                mu = b + av * c
                lam_ref[rows, cols] = dh + jnp.where(rowc == SUB - 1, c, pltpu.roll(mu, SUB - 1, 0))
                return _bcast_row(mu, 0)

            carry_ref[:, cols] = lax.fori_loop(0, n_slab, step, carry_ref[:, cols])

        lam = lam_ref[...]
        d_a = lam * hp_ref[...]
        d_mult = lam * ig * xc
        d_ig = lam * mult * xc
        dxc = lam * mult * ig
        d_log_a = d_a * a - d_mult * a * a / mult
        d_rg = (-LRU_C) * sp * d_log_a
        acc_ref[0:1, :] += _colsum((-LRU_C) * rg * d_log_a) * dsp_ref[...]
        dpr = d_rg * rg * (1.0 - rg)
        dpi = d_ig * ig * (1.0 - ig)
        acc_ref[1:2, :] += _colsum(dpr)
        acc_ref[2:3, :] += _colsum(dpi)
        dprb, dpib = dpr.astype(BF), dpi.astype(BF)
        dpr_ref[...] = dprb
        dpi_ref[...] = dpib
        dxc = dxc + _blockdiag_mm_t(dprb, wr_ref) + _blockdiag_mm_t(dpib, wi_ref)
        dxc_ref[...] = dxc
        acc_ref[3:4, :] += _colsum(dxc)

        row = _row_iota(LW)
        taps = [cw_ref[k:k + 1, :] for k in range(4)]
        u_halo = jnp.where(i == nt - 1, 0.0, uh_ref[...])
        nxt_tile = next_ref[...]

        def conv_step(k, accs):
            rows = _slab(k)
            cur = dxc_ref[rows, :]
            nxt = jnp.where(k == n_slab - 1, nxt_tile, dxc_ref[_slab(jnp.minimum(k + 1, n_slab - 1)), :])
            ucur = ub_ref[rows, :]
            uprev = jnp.where(k == 0, u_halo, ub_ref[_slab(jnp.maximum(k - 1, 0)), :])
            du = taps[3] * cur
            new = [accs[3] + cur * ucur]
            for j in (1, 2, 3):
                du = du + taps[3 - j] * pltpu.roll(jnp.where(row < j, nxt, cur), SUB - j, 0)
                new.append(accs[3 - j] + cur * pltpu.roll(jnp.where(row >= SUB - j, uprev, ucur), j, 0))
            dub_ref[rows, :] = du
            return tuple(new[::-1])

        zero = jnp.zeros((SUB, LW), F32)
        accs = lax.fori_loop(0, n_slab, conv_step, (zero, zero, zero, zero))
        for k in range(4):
            acc_ref[4 + k:5 + k, :] += _colsum(accs[k])
        next_ref[...] = dxc_ref[0:SUB, :]

    return _pallas_call(
        body, carry, name="lru_bwd", grid=(nt,),
        in_specs=[_tok_rev(LW, nt)] * 6 + [pl.BlockSpec((SUB, LW), halo_map), _full((4, LW)),
                                           _full((LW // 128, 128, 128)), _full((LW // 128, 128, 128)), _full((1, LW)), _full((1, LW))],
        out_specs=[_tok_rev(LW, nt), _tok_rev(LW, nt), _tok_rev(LW, nt), _full((SUB, LW))],
        out_shape=[_sds((L, LW)), _far((L, LW), BF), _far((L, LW), BF), _sds((SUB, LW))],
        scratch_shapes=[pltpu.VMEM((TM, LW), F32), pltpu.VMEM((TM, LW), F32), pltpu.VMEM((TM, LW), F32),
                        pltpu.VMEM((SUB, LW), F32), pltpu.VMEM((SUB, LW), F32)],
        compiler_params=_params(48),
    )(dyb, xc, rg, ig, hp, ub, ub, conv_w, wr, wi, sp, dsp)


AC = D // NCHIP


def _merge_fwd(x, ya, yb, gp, w_a, w_b, w_o, carry=None):
    L = x.shape[0]

    def body(x_ref, ya_ref, yb_ref, gp_ref, wa_ref, wb_ref, wo_ref, x1_ref, pa_ref, pb_ref, mg_ref):
        ya = ya_ref[...]
        for k in range(NCHIP):
            pa_ref[:, k * AC:(k + 1) * AC] = jnp.dot(ya, wa_ref[k], preferred_element_type=F32)
        pb = _mm(yb_ref[...], wb_ref[...])
        pb_ref[...] = pb
        gp = gp_ref[...]
        merged = (_sig(gp[:, :D]) * pa_ref[...] + _sig(gp[:, D:]) * pb).astype(BF)
        mg_ref[...] = merged
        x1_ref[...] = x_ref[...] + jnp.dot(merged, wo_ref[...], preferred_element_type=F32)

    return _pallas_call(
        body, carry, name="merge_fwd", grid=(L // TM,),
        in_specs=[_tok(D), _tok(S5W), _tok(LW), _tok(2 * D), _full((NCHIP, S5W, AC)), _full((LW, D)), _full((D, D))],
        out_specs=[_tok(D), _tok(D), _tok(D), _tok(D)],
        out_shape=[_sds((L, D)), _sds((L, D)), _sds((L, D)), _far((L, D), BF)],
        compiler_params=_params(40),
    )(x, ya, yb, gp, w_a, w_b, w_o)


def _merge_bwd(dx1, gp, pa, pb, w_a, w_b, w_o, carry=None):
    L = dx1.shape[0]

    def body(dx1_ref, gp_ref, pa_ref, pb_ref, wa_ref, wb_ref, wo_ref, dya_ref, dyb_ref, dgp_ref, dpa_ref, dpb_ref):
        dm = _mm_nt(dx1_ref[...], wo_ref[...])
        gp = gp_ref[...]
        sa, sb = _sig(gp[:, :D]), _sig(gp[:, D:])
        dpa = (dm * sa).astype(BF)
        dpb = (dm * sb).astype(BF)
        dpa_ref[...] = dpa
        dpb_ref[...] = dpb
        dgp_ref[:, :D] = dm * pa_ref[...] * sa * (1.0 - sa)
        dgp_ref[:, D:] = dm * pb_ref[...] * sb * (1.0 - sb)
        dya = jnp.zeros((TM, S5W), F32)
        for k in range(NCHIP):
            dya = dya + _mm_nt(dpa[:, k * AC:(k + 1) * AC], wa_ref[k])
        dya_ref[...] = dya
        dyb_ref[...] = _mm_nt(dpb, wb_ref[...])

    return _pallas_call(
        body, carry, name="merge_bwd", grid=(L // TM,),
        in_specs=[_tok(D), _tok(2 * D), _tok(D), _tok(D), _full((NCHIP, S5W, AC)), _full((LW, D)), _full((D, D))],
        out_specs=[_tok(S5W), _tok(LW), _tok(2 * D), _tok(D), _tok(D)],
        out_shape=[_far((L, S5W)), _far((L, LW)), _sds((L, 2 * D)), _far((L, D), BF), _far((L, D), BF)],
        compiler_params=_params(40),
    )(dx1, gp, pa, pb, w_a, w_b, w_o)


def _chunk_tok(width):
    return pl.BlockSpec((NCHIP, TM, width), lambda i: (0, i, 0))


def _ffn_fwd(x1, g_ffn, wg, wu, wd, carry=None):
    L = x1.shape[0]

    def body(x_ref, g_ref, wg_hbm, wu_hbm, wd_hbm, x2_ref, h2_ref, gg_ref, uu_ref, wg_vm, wu_vm, wd_vm, w_sems):
        _resident_now([(src.at[c], dst.at[c]) for c in range(NCHIP)
                       for src, dst in ((wg_hbm, wg_vm), (wu_hbm, wu_vm), (wd_hbm, wd_vm))], w_sems)
        x = x_ref[...]
        xh, _ = _rms(x)
        h2 = (xh * g_ref[...]).astype(BF)
        h2_ref[...] = h2
        out = x
        for c in range(NCHIP):
            gg = lax.dot_general(h2, wg_vm[c], (((1,), (1,)), ((), ())), preferred_element_type=F32)
            uu = lax.dot_general(h2, wu_vm[c], (((1,), (1,)), ((), ())), preferred_element_type=F32)
            gg_ref[c] = gg.astype(BF)
            uu_ref[c] = uu.astype(BF)
            act = (gg * _sig(gg) * uu).astype(BF)
            out = out + jnp.dot(act, wd_vm[c], preferred_element_type=F32)
        x2_ref[...] = out

    return _pallas_call(
        body, carry, name="ffn_fwd", grid=(L // TM,),
        in_specs=[_tok(D), _full((1, D)), ANY, ANY, ANY],
        out_specs=[_tok(D), _tok(D), _chunk_tok(FC), _chunk_tok(FC)],
        out_shape=[_sds((L, D)), _sds((L, D), BF), _sds((NCHIP, L, FC), BF), _sds((NCHIP, L, FC), BF)],
        scratch_shapes=[pltpu.VMEM((NCHIP, FC, D), BF)] * 3 + [pltpu.SemaphoreType.DMA((3 * NCHIP,))],
        compiler_params=_params(52),
    )(x1, g_ffn, wg, wu, wd)


def _ffn_bwd(x1, dx2, gg, uu, g_ffn, wg, wu, wd, carry=None):
    L = x1.shape[0]

    def body(x_ref, dx2_ref, gg_ref, uu_ref, g_ref, wg_hbm, wu_hbm, wd_hbm,
             dx1_ref, act_ref, dgg_ref, duu_ref, dg_ref, wg_vm, wu_vm, wd_vm, w_sems):
        _resident_now([(src.at[c], dst.at[c]) for c in range(NCHIP)
                       for src, dst in ((wg_hbm, wg_vm), (wu_hbm, wu_vm), (wd_hbm, wd_vm))], w_sems)

        @pl.when(pl.program_id(0) == 0)
        def _():
            dg_ref[...] = jnp.zeros_like(dg_ref)

        dx2 = dx2_ref[...]
        dx2b = dx2.astype(BF)
        dh2 = jnp.zeros((TM, D), F32)
        for c in range(NCHIP):
            g = gg_ref[c].astype(F32)
            u = uu_ref[c].astype(F32)
            s = _sig(g)
            silu = g * s
            act_ref[c] = (silu * u).astype(BF)
            dact = lax.dot_general(dx2b, wd_vm[c], (((1,), (1,)), ((), ())), preferred_element_type=F32)
            dg = (dact * u * s * (1.0 + g * (1.0 - s))).astype(BF)
            du = (dact * silu).astype(BF)
            dgg_ref[c] = dg
            duu_ref[c] = du
            dh2 = dh2 + jnp.dot(dg, wg_vm[c], preferred_element_type=F32)
            dh2 = dh2 + jnp.dot(du, wu_vm[c], preferred_element_type=F32)
        xh, r = _rms(x_ref[...])
        dg_ref[0:1, :] += _colsum(dh2 * xh)
        dx1_ref[...] = dx2 + _rms_bwd(dh2, xh, r, g_ref[...])

    return _pallas_call(
        body, carry, name="ffn_bwd", grid=(L // TM,),
        in_specs=[_tok(D), _tok(D), _chunk_tok(FC), _chunk_tok(FC), _full((1, D)), ANY, ANY, ANY],
        out_specs=[_tok(D), _chunk_tok(FC), _chunk_tok(FC), _chunk_tok(FC), _full((SUB, D))],
        out_shape=[_sds((L, D)), _sds((NCHIP, L, FC), BF), _sds((NCHIP, L, FC), BF), _sds((NCHIP, L, FC), BF),
                   _sds((SUB, D))],
        scratch_shapes=[pltpu.VMEM((NCHIP, FC, D), BF)] * 3 + [pltpu.SemaphoreType.DMA((3 * NCHIP,))],
        compiler_params=_params(56),
    )(x1, dx2, gg, uu, g_ffn, wg, wu, wd)


def _ple_loss(x2, p, tgt, g_pg, w_pg, b_pg, w_ple, g_ple, g_final):
    L = x2.shape[0]

    def body(x2_ref, p_ref, t_ref, gpg_ref, wpg_ref, bpg_ref, wple_ref, gple_ref, gf_ref,
             dx2_ref, n2_ref, dpre_ref, de0_ref, acc_ref):
        @pl.when(pl.program_id(0) == 0)
        def _():
            acc_ref[...] = jnp.zeros_like(acc_ref)

        x2 = x2_ref[...]
        x2h, r2 = _rms(x2)
        n2 = (x2h * gpg_ref[...]).astype(BF)
        n2_ref[...] = n2
        gate = _sig(jnp.dot(n2, wpg_ref[...], preferred_element_type=F32) + bpg_ref[...])
        pb = p_ref[...].astype(BF)
        e0 = jnp.concatenate([jnp.dot(pb, wple_ref[k], preferred_element_type=F32) for k in range(NCHIP)], axis=1)
        e0h, re = _rms(e0)
        e = e0h * gple_ref[...]
        x3 = x2 + gate * e
        x3h, r3 = _rms(x3)
        diff = x3h * gf_ref[...] - t_ref[...]
        acc_ref[4:5, :] += _colsum(diff * diff) * (0.5 / D)
        dy = diff * (1.0 / D)
        acc_ref[3:4, :] += _colsum(dy * x3h)
        dx3 = _rms_bwd(dy, x3h, r3, gf_ref[...])
        de = dx3 * gate
        acc_ref[2:3, :] += _colsum(de * e0h)
        de0_ref[...] = _rms_bwd(de, e0h, re, gple_ref[...]).astype(BF)
        dpre = dx3 * e * gate * (1.0 - gate)
        acc_ref[1:2, :] += _colsum(dpre)
        dpreb = dpre.astype(BF)
        dpre_ref[...] = dpreb
        dn2 = lax.dot_general(dpreb, wpg_ref[...], (((1,), (1,)), ((), ())), preferred_element_type=F32)
        acc_ref[0:1, :] += _colsum(dn2 * x2h)
        dx2_ref[...] = dx3 + _rms_bwd(dn2, x2h, r2, gpg_ref[...])

    return _pallas_call(
        body, name="ple_loss", grid=(L // TM,),
        in_specs=[_tok(D), _tok(PLE), _tok(D), _full((1, D)), _full((D, D)), _full((1, D)), _full((NCHIP, PLE, AC)),
                  _full((1, D)), _full((1, D))],
        out_specs=[_tok(D), _tok(D), _tok(D), _tok(D), _full((SUB, D))],
        out_shape=[_sds((L, D)), _sds((L, D), BF), _sds((L, D), BF), _sds((L, D), BF), _sds((SUB, D))],
        compiler_params=_params(40),
    )(x2, p, tgt, g_pg, *_in_hbm([w_pg]), b_pg, *_in_hbm([w_ple]), g_ple, g_final)


def _tn(name, a, b, col_chunk=None, a_block=None, carry=None):
    L = a.shape[-2]
    m, n = a.shape[-1], b.shape[-1]
    a_col = 0
    if a_block is not None:
        a_col, m = a_block
    tk = L if (a.ndim == 3 or b.ndim == 3 or a_block is not None) else TK
    if a.ndim == 3 or b.ndim == 3:
        nj, bn = (a if a.ndim == 3 else b).shape[0], n
        a_spec = (pl.BlockSpec((None, tk, m), lambda j, t: (j, t, 0)) if a.ndim == 3
                  else pl.BlockSpec((tk, m), lambda j, t: (t, 0)))
        b_spec = (pl.BlockSpec((None, tk, n), lambda j, t: (j, t, 0)) if b.ndim == 3
                  else pl.BlockSpec((tk, n), lambda j, t: (t, 0)))
        out_spec, out_shape = pl.BlockSpec((None, m, n), lambda j, t: (j, 0, 0)), _sds((nj, m, n))
    else:
        bn = col_chunk
        if bn is None:
            bn = next((cand for cand in (1024, 512) if n > cand and n % cand == 0), n)
        nj = n // bn
        a_spec = pl.BlockSpec((tk, m), lambda j, t: (t, a_col))
        b_spec = pl.BlockSpec((tk, bn), lambda j, t: (t, j))
        if col_chunk is None:
            out_spec, out_shape = pl.BlockSpec((m, bn), lambda j, t: (0, j)), _sds((m, n))
        else:
            out_spec, out_shape = pl.BlockSpec((None, m, bn), lambda j, t: (j, 0, 0)), _sds((nj, m, bn))

    def body(a_ref, b_ref, o_ref):
        if tk == L:
            o_ref[...] = _mm_tn(a_ref[...], b_ref[...])
        else:
            @pl.when(pl.program_id(1) == 0)
            def _():
                o_ref[...] = jnp.zeros_like(o_ref)

            o_ref[...] += _mm_tn(a_ref[...], b_ref[...])

    outs = _pallas_call(
        body, carry, name=name, grid=(nj, L // tk), in_specs=[a_spec, b_spec], out_specs=[out_spec],
        out_shape=[pltpu.HBM(out_shape.shape, out_shape.dtype)],
        compiler_params=pltpu.CompilerParams(dimension_semantics=("arbitrary", "arbitrary"),
                                             vmem_limit_bytes=(30 if tk == L else 28) * VMEM_MB),
    )(*(_in_hbm([a, b]) if tk == L else (a, b)))
    return outs[0] if carry is None else outs


LANE = 128


def _tn_blocks(name, a, bs, ga, gb, carry=None):
    L, m, n, nb = a.shape[0], a.shape[1], bs[0].shape[1], len(bs)
    per = LANE // ga
    wb = per * gb
    n_super = m // LANE

    def body(a_ref, *refs):
        b_refs, o_refs, acc_refs = refs[:nb], refs[nb:2 * nb], refs[2 * nb:]
        t = pl.program_id(0)

        @pl.when(t == 0)
        def _():
            for acc in acc_refs:
                acc[...] = jnp.zeros_like(acc)

        lhs = a_ref[...].astype(BF)
        for b_ref, acc in zip(b_refs, acc_refs):
            rhs = b_ref[...].astype(BF)
            for j in range(n_super):
                acc[j] += _mm_tn(lhs[:, j * LANE:(j + 1) * LANE], rhs[:, j * wb:(j + 1) * wb])

        @pl.when(t == L // TK - 1)
        def _():
            own = (lax.broadcasted_iota(jnp.int32, (LANE, wb), 0) // ga) == (lax.broadcasted_iota(jnp.int32, (LANE, wb), 1) // gb)
            for o_ref, acc in zip(o_refs, acc_refs):
                for j in range(n_super):
                    kept = jnp.where(own, acc[j], 0.0)
                    o_ref[:, j * wb:(j + 1) * wb] = jnp.sum(kept.reshape(per, ga, wb), axis=0)

    outs = _pallas_call(
        body, carry, name=name, grid=(L // TK,),
        in_specs=[pl.BlockSpec((TK, m), lambda t: (t, 0))] + [pl.BlockSpec((TK, n), lambda t: (t, 0))] * nb,
        out_specs=[_full((ga, n))] * nb, out_shape=[_sds((ga, n))] * nb,
        scratch_shapes=[pltpu.VMEM((n_super, LANE, wb), F32)] * nb,
        compiler_params=_params(48),
    )(*_in_hbm([a] + list(bs)))
    return list(outs)


def _s5_discretize(lam_re, lam_im, log_dt, b_re, b_im):
    dt = jnp.exp(log_dt)[:, None]
    mag = jnp.exp(lam_re * dt)
    ar = mag * jnp.cos(lam_im * dt)
    ai = mag * jnp.sin(lam_im * dt)
    den = lam_re * lam_re + lam_im * lam_im
    nr = ar - 1.0
    fr = (nr * lam_re + ai * lam_im) / den
    fi = (ai * lam_re - nr * lam_im) / den
    bbr = fr[:, None, :] * b_re - fi[:, None, :] * b_im
    bbi = fr[:, None, :] * b_im + fi[:, None, :] * b_re
    return ar, ai, bbr, bbi


def _prepare(by_rows, block_cols, ar, ai):
    n = len(by_rows)

    def body(*refs):
        srcs, (ar_ref, ai_ref), dense, (con_ref, rev_ref) = refs[:n], refs[n:n + 2], refs[n + 2:2 * n + 2], refs[2 * n + 2:]
        for src, out, c in zip(srcs, dense, block_cols):
            r = src.shape[0]
            per = LANE // r
            wide = per * c
            own = (lax.broadcasted_iota(jnp.int32, (LANE, wide), 0) // r) == (lax.broadcasted_iota(jnp.int32, (LANE, wide), 1) // c)
            for j in range(out.shape[0]):
                tiled = jnp.broadcast_to(src[:, j * wide:(j + 1) * wide][None], (per, r, wide)).reshape(LANE, wide)
                out[j] = jnp.where(own, tiled, 0.0).astype(BF)
        a_r, a_i = ar_ref[...], ai_ref[...]
        pw = [(jnp.ones_like(a_r), jnp.zeros_like(a_i))]
        for _ in range(SUB):
            pr, pi = pw[-1]
            pw.append((pr * a_r - pi * a_i, pr * a_i + pi * a_r))
        row = _row_iota(GN)
        for ref, reverse in ((con_ref, False), (rev_ref, True)):
            sign = -1.0 if reverse else 1.0
            for j, sh in enumerate((1, 2, 4)):
                keep = (row < SUB - sh) if reverse else (row >= sh)
                ref[2 * j * SUB:(2 * j + 1) * SUB, :] = jnp.where(keep, pw[sh][0], 0.0)
                ref[(2 * j + 1) * SUB:(2 * j + 2) * SUB, :] = jnp.where(keep, sign * pw[sh][1], 0.0)
            p_r, p_i = jnp.zeros((SUB, GN), F32), jnp.zeros((SUB, GN), F32)
            for i in range(SUB):
                k = SUB - i if reverse else i + 1
                p_r = jnp.where(row == i, pw[k][0], p_r)
                p_i = jnp.where(row == i, sign * pw[k][1], p_i)
            ref[6 * SUB:7 * SUB, :] = p_r
            ref[7 * SUB:8 * SUB, :] = p_i

    dense_shapes = [(b.shape[1] // (LANE // b.shape[0] * c), LANE, LANE // b.shape[0] * c)
                    for b, c in zip(by_rows, block_cols)]
    outs = _pallas_call(
        body, name="prepare", grid=(1,), in_specs=[_full(b.shape) for b in by_rows] + [_full((1, GN))] * 2,
        out_specs=[_full(s) for s in dense_shapes] + [_full((8 * SUB, GN))] * 2,
        out_shape=[_far(s, BF) for s in dense_shapes] + [_sds((8 * SUB, GN)), _far((8 * SUB, GN))],
        compiler_params=_params(48),
    )(*by_rows, ar, ai)
    return outs[:n], outs[n], outs[n + 1]


def _local_step(x, p, tgt, w, comm):
    rows_of = lambda a: a.reshape(NCHIP * a.shape[1], a.shape[2])
    quarters = lambda a: a.reshape(NCHIP, a.shape[0] // NCHIP, a.shape[1])

    def gathering(names, call):
        carry = comm.gather(names)
        outs = list(call(carry))
        own = len(outs) - len(carry.out_shapes)
        w.update(zip(names, outs[own:]))
        return outs[:own]

    w.update(comm.first())
    ar, ai, bbr, bbi = _s5_discretize(w["lam_re"], w["lam_im"], w["log_dt"], w["s5_b_re"], w["s5_b_im"])
    by_row = lambda b: jnp.transpose(b, (1, 0, 2)).reshape(b.shape[1], -1)
    (bbr_d, bbi_d, ccr_d, cci_d, wr_d, wi_d), con, con_rev = _prepare(
        [by_row(b) for b in (bbr, bbi, w["s5_c_re"], w["s5_c_im"], w["w_r"], w["w_i"])], [NS] * 4 + [HD] * 2,
        ar.reshape(1, GN), ai.reshape(1, GN))
    dsk = w["s5_d"].reshape(1, S5W)
    lam = w["lru_lambda"].reshape(1, LW)
    sp = jax.nn.softplus(-lam)
    b_r, b_i = w["b_r"].reshape(1, LW), w["b_i"].reshape(1, LW)
    row = lambda name: w[name].reshape(1, -1)

    h, ua, ub, gp = gathering(["w_glu", "w_a_out", "w_b_out"], lambda carry: _inproj_fwd(
        x, row("g_mix"), w["w_in"], row("b_in"), carry))
    w_glu = rows_of(w["w_glu"])
    sr, si, y, zg, ya = gathering(["w_o", "w_ffn_gate"], lambda carry: _s5_fwd(
        ua, bbr_d, bbi_d, ccr_d, cci_d, dsk, con, w_glu, row("b_glu"), carry))
    xc, rg, ig, yb, hp = gathering(["w_ffn_up"], lambda carry: _lru_fwd(
        ub, w["conv_w"], row("conv_b"), wr_d, wi_d, b_r, b_i, sp, carry))
    w_b_out, w_o = rows_of(w["w_b_out"]), rows_of(w["w_o"])
    x1, pa, pb, merged = gathering(["w_ffn_down"], lambda carry: _merge_fwd(
        x, ya, yb, gp, w["w_a_out"], w_b_out, w_o, carry))
    x2, h2, gg, uu = gathering(["w_ple_gate", "w_ple"], lambda carry: _ffn_fwd(
        x1, row("g_ffn"), w["w_ffn_gate"], w["w_ffn_up"], w["w_ffn_down"], carry))
    w_pg = rows_of(w["w_ple_gate"])
    dx2, n2, dpre, de0, acc_p = _ple_loss(x2, p, tgt, row("g_ple_gate"), w_pg, row("b_ple_gate"),
                                          w["w_ple"], row("g_ple"), row("g_final"))
    comm.reduce("ple", {"w_ple_gate": quarters(_tn("dw_ple_gate", n2, dpre)),
                        "w_ple": _tn("dw_ple", p, de0, col_chunk=AC)})
    dx1, act, dgg, duu, acc_f = comm.run(lambda carry: _ffn_bwd(
        x1, dx2, gg, uu, row("g_ffn"), w["w_ffn_gate"], w["w_ffn_up"], w["w_ffn_down"], carry))
    comm.reduce("ffn_gate", {"w_ffn_gate": _tn("dw_ffn_gate", dgg, h2)})
    comm.reduce("w_o", {"w_o": quarters(_tn("dw_o", *_in_hbm([merged, dx1])))})
    comm.reduce("ffn_up", {"w_ffn_up": comm.run(lambda carry: _tn("dw_ffn_up", duu, h2, carry=carry))[0]})
    comm.reduce("ffn_down", {"w_ffn_down": comm.run(lambda carry: _tn("dw_ffn_down", act, dx2, carry=carry),
                                                    hold=("ffn_gate", "w_o"))[0]})
    dya, dyb, dgp, dpa, dpb = comm.run(lambda carry: _merge_bwd(
        dx1, gp, pa, pb, w["w_a_out"], w_b_out, w_o, carry), hold=("ffn_gate", "ffn_up"))
    comm.reduce("merge", {"w_a_out": _tn("dw_a_out", ya, dpa, col_chunk=AC), "w_b_out": quarters(_tn("dw_b_out", yb, dpb))})
    dua, dq, dy, lr, li, acc_a, acc_s = comm.run(lambda carry: _s5_bwd(
        dya, y, ua, sr, si, bbr_d, bbi_d, ccr_d, cci_d, dsk, con_rev, w_glu, row("b_glu"), carry), hold=("ffn_down",))
    dub, dpr, dpi, acc_l = comm.run(lambda carry: _lru_bwd(
        dyb, xc, rg, ig, hp, ub, w["conv_w"], wr_d, wi_d, sp, -_sig(-lam), carry))
    gx, dz, acc_g, acc_b = _inproj_bwd(x, dx1, dua, dub, dgp, row("g_mix"), w["w_in"])
    half = (D // 2,)
    comm.reduce("in_lo", {"w_in_lo": comm.run(lambda carry: _tn(
        "dw_in_lo", h, dz, col_chunk=QC, a_block=(0,) + half, carry=carry))[0]})
    comm.reduce("in_hi", {"w_in_hi": comm.run(lambda carry: _tn(
        "dw_in_hi", h, dz, col_chunk=QC, a_block=(1,) + half, carry=carry))[0], "w_glu": quarters(_tn("dw_glu", zg, dq))})
    d_wr, d_wi = comm.run(lambda carry: _tn_blocks("dw_r_i", xc, [dpr, dpi], HD, HD, carry))
    d_bbr, d_bbi = comm.run(lambda carry: _tn_blocks("d_bb", ua, [lr, li], NP, NS, carry))
    d_ccr, d_cci = comm.run(lambda carry: _tn_blocks("d_cc", dy, [sr, si], NP, NS, carry))
    comm.drain()
    sums = {"ple": acc_p, "ffn": acc_f, "mix": acc_g, "b_in": acc_b, "lru": acc_l, "s5": acc_s, "s5_a": acc_a}
    blocks = {"bb_re": d_bbr, "bb_im": d_bbi,
              "cc_re": d_ccr, "cc_im": d_cci,
              "w_r": d_wr, "w_i": d_wi}
    return gx, sums, blocks


def _replicated_grads(w, sums, blocks):
    grouped = lambda e, groups: jnp.transpose(e.reshape(e.shape[0], groups, -1), (1, 0, 2))
    d_ar, d_ai = sums["s5_a"][0].reshape(NG, NS), sums["s5_a"][1].reshape(NG, NS)
    d_bbr, d_bbi = grouped(blocks["bb_re"], NG), grouped(blocks["bb_im"], NG)
    _, vjp = jax.vjp(_s5_discretize, w["lam_re"], w["lam_im"], w["log_dt"], w["s5_b_re"], w["s5_b_im"])
    g = dict(zip(("lam_re", "lam_im", "log_dt", "s5_b_re", "s5_b_im"), vjp((d_ar, d_ai, d_bbr, d_bbi))))
    g["s5_c_re"] = grouped(blocks["cc_re"], NG)
    g["s5_c_im"] = -grouped(blocks["cc_im"], NG)
    g["w_r"], g["w_i"] = grouped(blocks["w_r"], NH), grouped(blocks["w_i"], NH)
    g["s5_d"] = sums["s5"][0].reshape(NG, NP)
    g["b_r"] = sums["lru"][1].reshape(NH, HD)
    g["b_i"] = sums["lru"][2].reshape(NH, HD)
    return g


ACC_ROWS = {"g_mix": ("mix", 0), "b_in": ("b_in", 0), "g_ffn": ("ffn", 0), "g_ple_gate": ("ple", 0),
            "b_ple_gate": ("ple", 1), "g_ple": ("ple", 2), "g_final": ("ple", 3), "b_glu": ("s5", 1),
            "lru_lambda": ("lru", 0), "conv_b": ("lru", 3)}
LOSS_ROW = ("ple", 4)
CONV_W_ROWS = ("lru", 4)


SHARDED = [("w_in", (D, QC)), ("w_glu", (S5W // NCHIP, S5W)), ("w_a_out", (S5W, AC)), ("w_b_out", (LW // NCHIP, D)),
           ("w_o", (D // NCHIP, D)), ("w_ffn_gate", (FC, D)), ("w_ffn_up", (FC, D)), ("w_ffn_down", (FC, D)),
           ("w_ple_gate", (D // NCHIP, D)), ("w_ple", (PLE, AC))]
TRANSPOSED = ("w_ffn_gate", "w_ffn_up", "s5_b_re", "s5_b_im")
CONV_SHARD = (4, LW // NCHIP)


def _mesh_pos():
    return lax.axis_index("x"), lax.axis_index("y"), lax.axis_index("c")


def _other_chips(x, y):
    return [(1 - x, y), (x, 1 - y), (1 - x, 1 - y)]


def _half_rows(c, rows, align):
    return pl.ds(pl.multiple_of(c * (rows // 2), align), rows // 2)


def _run_now(name, carry):
    c_in, c_out = len(carry.operands), len(carry.out_shapes)

    def body(*refs):
        ins, outs, sems = refs[:c_in], refs[c_in:c_in + c_out], refs[c_in + c_out:]
        carry.start(ins, outs, sems)
        carry.finish(ins, outs, sems)

    return pl.pallas_call(body, name=name, in_specs=[ANY] * c_in, out_specs=[ANY] * c_out,
                          out_shape=list(carry.out_shapes), scratch_shapes=list(carry.sems),
                          input_output_aliases=dict(carry.aliases))(*_in_hbm(carry.operands))


def _gather_group(shards, split):
    n = len(shards)

    def copies(srcs, outs, sems):
        send_sems, recv_sems = sems
        x, y, c = _mesh_pos()
        k0 = 2 * x + y
        sib = (x, y, 1 - c)
        chips = _other_chips(x, y)

        def remote(src, dst, j, i, to):
            return pltpu.make_async_remote_copy(src_ref=src, dst_ref=dst, send_sem=send_sems.at[j, i],
                                                recv_sem=recv_sems.at[j, i], device_id=to, device_id_type=MESH)

        def rows(ref, i, core, *lead):
            if not split[i]:
                return ref.at[lead] if lead else ref
            return ref.at[(*lead, _half_rows(core, shards[i].shape[0], 16))]

        own = [remote(s, o.at[k0], 6, i, sib) for i, (s, o) in enumerate(zip(srcs, outs))]
        ici, landed, fwd, fwd_landed = [], [], [], []
        for j, chip in enumerate(chips):
            kj = 2 * chip[0] + chip[1]
            pairs = list(enumerate(zip(srcs, outs)))
            ici.append([remote(rows(s, i, c), rows(o, i, c, k0), j, i, (*chip, c)) for i, (s, o) in pairs])
            landed.append([remote(rows(s, i, c), rows(o, i, c, kj), j, i, (*chip, c)) for i, (s, o) in pairs])
            fwd.append([remote(rows(o, i, c, kj), rows(o, i, c, kj), 3 + j, i, sib) for i, (s, o) in pairs if split[i]])
            fwd_landed.append([remote(rows(o, i, 1 - c, kj), rows(o, i, 1 - c, kj), 3 + j, i, sib)
                               for i, (s, o) in pairs if split[i]])
        return own, ici, landed, fwd, fwd_landed

    def start(srcs, outs, sems):
        own, ici, _, _, _ = copies(srcs, outs, sems)
        for cp in own + [cp for per_chip in ici for cp in per_chip]:
            cp.start()

    def finish(srcs, outs, sems):
        own, ici, landed, fwd, fwd_landed = copies(srcs, outs, sems)
        passed = [i for i in range(n) if split[i]]
        for j in range(3):
            for i, cp in enumerate(landed[j]):
                cp.wait_recv()
                if split[i]:
                    fwd[j][passed.index(i)].start()
        for j in range(3):
            for cp in fwd_landed[j]:
                cp.wait_recv()
        for cp in own:
            cp.wait_recv()
        for cp in own + [cp for per_chip in ici + fwd for cp in per_chip]:
            cp.wait_send()

    return _Carried(shards, [_far((NCHIP,) + s.shape, s.dtype) for s in shards],
                    [pltpu.SemaphoreType.DMA((7, n)), pltpu.SemaphoreType.DMA((7, n))], start, finish)


def _to_bf16_group(name, arrays, carry):
    n = len(arrays)

    def body(*refs):
        for src, dst in zip(refs[:n], refs[n:]):
            dst[...] = src[...].astype(BF)

    specs = [pl.BlockSpec((a.shape[0] // 2, a.shape[1]), lambda i: (i, 0)) for a in arrays]
    return _pallas_call(body, carry, name=name, grid=(2,), in_specs=specs, out_specs=specs,
                        out_shape=[_far(a.shape, BF) for a in arrays], compiler_params=_params(48))(*arrays)


def _each_copy(copies, carried, out_shapes, sems, aliases=None):
    def start(ins, outs, sem_refs):
        for cp in copies(ins, outs, sem_refs):
            cp.start()

    def finish(ins, outs, sem_refs):
        for cp in copies(ins, outs, sem_refs):
            cp.wait()

    return _Carried(carried, out_shapes, sems, start, finish, aliases)


def _swap_group(grads):
    n = len(grads)

    def copies(srcs, outs, sems):
        send_sems, recv_sems = sems
        x, y, c = _mesh_pos()
        return [pltpu.make_async_remote_copy(src_ref=s.at[:, _half_rows(1 - c, s.shape[1], 8)], dst_ref=o,
                                             send_sem=send_sems.at[i], recv_sem=recv_sems.at[i], device_id=(x, y, 1 - c),
                                             device_id_type=MESH) for i, (s, o) in enumerate(zip(srcs, outs))]

    return _each_copy(copies, grads, [pltpu.HBM((NCHIP, g.shape[1] // 2, g.shape[2]), F32) for g in grads],
                      [pltpu.SemaphoreType.DMA((n,)), pltpu.SemaphoreType.DMA((n,))])


def _add_sibling_group(tag, kc_idx, grads, gots):
    n = len(grads)

    def body(kc_ref, *refs):
        for g, rx, p, pb in zip(refs[:n], refs[n:2 * n], refs[2 * n:3 * n], refs[3 * n:]):
            s = g[...] + rx[...]
            pb[...] = s.astype(BF)

            @pl.when(pl.program_id(0) == kc_ref[0])
            def _():
                p[...] = s

    halves = [pl.BlockSpec((None,) + rx.shape[1:], lambda k, kc_ref: (k, 0, 0)) for rx in gots]
    mine = [pl.BlockSpec((None,) + rx.shape[1:], lambda k, kc_ref: (k, kc_ref[1], 0)) for rx in gots]
    own = [pl.BlockSpec(rx.shape[1:], lambda k, kc_ref: (0, 0)) for rx in gots]
    outs = _pallas_call(
        body, name="add_sibling_" + tag,
        grid_spec=pltpu.PrefetchScalarGridSpec(num_scalar_prefetch=1, grid=(NCHIP,), in_specs=mine + halves,
                                               out_specs=own + halves),
        out_shape=[pltpu.HBM(rx.shape[1:], F32) for rx in gots] + [pltpu.HBM(rx.shape, BF) for rx in gots],
        compiler_params=_params(48),
    )(kc_idx, *_in_hbm(list(grads) + list(gots)))
    return outs[:n], outs[n:]


def _exchange_group(parts):
    n = len(parts)

    def copies(srcs, outs, sems):
        send_sems, recv_sems = sems
        x, y, c = _mesh_pos()
        return [pltpu.make_async_remote_copy(
            src_ref=s.at[2 * chip[0] + chip[1]], dst_ref=o.at[j], send_sem=send_sems.at[j, i],
            recv_sem=recv_sems.at[j, i], device_id=(*chip, c), device_id_type=MESH)
            for j, chip in enumerate(_other_chips(x, y)) for i, (s, o) in enumerate(zip(srcs, outs))]

    return _each_copy(copies, parts, [pltpu.HBM((3,) + p.shape[1:], BF) for p in parts],
                      [pltpu.SemaphoreType.DMA((3, n)), pltpu.SemaphoreType.DMA((3, n))])


def _add_chips_group(tag, kc_idx, parts, arrived):
    n = len(parts)

    def body(kc_ref, *refs):
        for p, rx, t in zip(refs[:n], refs[n:2 * n], refs[2 * n:]):
            t[...] = ((p[...] + rx[0].astype(F32)) + rx[1].astype(F32)) + rx[2].astype(F32)

    outs = _pallas_call(
        body, name="add_chips_" + tag,
        grid_spec=pltpu.PrefetchScalarGridSpec(
            num_scalar_prefetch=1, grid=(1,),
            in_specs=([pl.BlockSpec(rx.shape[1:], lambda i, kc_ref: (0, 0)) for rx in arrived]
                      + [pl.BlockSpec(rx.shape, lambda i, kc_ref: (0, 0, 0)) for rx in arrived]),
            out_specs=[pl.BlockSpec((None,) + rx.shape[1:], lambda i, kc_ref: (kc_ref[1], 0, 0)) for rx in arrived]),
        out_shape=[pltpu.HBM((2,) + rx.shape[1:], F32) for rx in arrived],
        compiler_params=_params(48),
    )(kc_idx, *_in_hbm(list(parts) + list(arrived)))
    return list(outs)


def _join_group(halves):
    n = len(halves)

    def copies(bufs, sems):
        send_sems, recv_sems = sems
        x, y, c = _mesh_pos()
        sib = (x, y, 1 - c)
        sends = [pltpu.make_async_remote_copy(src_ref=b.at[c], dst_ref=b.at[c], send_sem=send_sems.at[i],
                                              recv_sem=recv_sems.at[i], device_id=sib, device_id_type=MESH)
                 for i, b in enumerate(bufs)]
        landed = [pltpu.make_async_remote_copy(src_ref=b.at[c], dst_ref=b.at[1 - c], send_sem=send_sems.at[i],
                                               recv_sem=recv_sems.at[i], device_id=sib, device_id_type=MESH)
                  for i, b in enumerate(bufs)]
        return sends, landed

    def start(_, bufs, sems):
        for cp in copies(bufs, sems)[0]:
            cp.start()

    def finish(_, bufs, sems):
        sends, landed = copies(bufs, sems)
        for cp in landed:
            cp.wait_recv()
        for cp in sends:
            cp.wait_send()

    return _Carried(halves, [pltpu.HBM(h.shape, F32) for h in halves],
                    [pltpu.SemaphoreType.DMA((n,)), pltpu.SemaphoreType.DMA((n,))], start, finish,
                    {i: i for i in range(n)})


def _combine(carries):
    operands, out_shapes, sems, aliases, spans = [], [], [], {}, []
    for c in carries:
        aliases.update({len(operands) + i: len(out_shapes) + o for i, o in c.aliases.items()})
        spans.append((len(operands), len(out_shapes), len(sems)))
        operands += list(c.operands)
        out_shapes += list(c.out_shapes)
        sems += list(c.sems)

    def each(phase):
        def run(ins, outs, sem_refs):
            for c, (a, b, s) in zip(carries, spans):
                getattr(c, phase)(ins[a:a + len(c.operands)], outs[b:b + len(c.out_shapes)], sem_refs[s:s + len(c.sems)])
        return run

    return _Carried(operands, out_shapes, sems, each("start"), each("finish"), aliases)


def _allreduce_small(arrays, wire):
    n = len(arrays)
    halves = [(a.shape[0], a.shape[1] // 2) for a in arrays]

    def body(*refs):
        srcs, outs = refs[:n], refs[n:2 * n]
        mine_bufs, sib_bufs, chip_bufs, total_bufs = (refs[k * n:(k + 1) * n] for k in range(2, 6))
        send_sems, recv_sems, local_sems = refs[6 * n:]
        x, y, c = _mesh_pos()
        k0 = 2 * x + y
        sib = (x, y, 1 - c)

        def remote(src, dst, j, i, to):
            return pltpu.make_async_remote_copy(src_ref=src, dst_ref=dst, send_sem=send_sems.at[j, i],
                                                recv_sem=recv_sems.at[j, i], device_id=to, device_id_type=MESH)

        def cols(ref, i, core):
            return ref.at[:, pl.ds(pl.multiple_of(core * halves[i][1], LANE), halves[i][1])]

        swaps = [remote(cols(s, i, 1 - c), b, 0, i, sib) for i, (s, b) in enumerate(zip(srcs, sib_bufs))]
        own = [pltpu.make_async_copy(cols(s, i, c), m, local_sems.at[i]) for i, (s, m) in enumerate(zip(srcs, mine_bufs))]
        for cp in swaps + own:
            cp.start()
        for cp in swaps + own:
            cp.wait()
        for m, b, buf in zip(mine_bufs, sib_bufs, chip_bufs):
            buf[k0] = (m[...] + b[...]).astype(buf.dtype)
        chips = _other_chips(x, y)
        sends = [remote(buf.at[k0], buf.at[k0], 1 + j, i, (*chip, c))
                 for j, chip in enumerate(chips) for i, buf in enumerate(chip_bufs)]
        for cp in sends:
            cp.start()
        for j, chip in enumerate(chips):
            for i, buf in enumerate(chip_bufs):
                remote(buf.at[k0], buf.at[2 * chip[0] + chip[1]], 1 + j, i, (*chip, c)).wait_recv()
        for cp in sends:
            cp.wait_send()
        for t, buf in zip(total_bufs, chip_bufs):
            t[...] = ((buf[0].astype(F32) + buf[1].astype(F32)) + buf[2].astype(F32)) + buf[3].astype(F32)
        joins = [remote(t, cols(o, i, c), 4, i, sib) for i, (t, o) in enumerate(zip(total_bufs, outs))]
        keep = [pltpu.make_async_copy(t, cols(o, i, c), local_sems.at[i]) for i, (t, o) in enumerate(zip(total_bufs, outs))]
        for cp in joins + keep:
            cp.start()
        for i, (t, o) in enumerate(zip(total_bufs, outs)):
            remote(t, cols(o, i, 1 - c), 4, i, sib).wait_recv()
        for cp in joins:
            cp.wait_send()
        for cp in keep:
            cp.wait()

    specs = [_full(a.shape) for a in arrays]
    return _pallas_call(
        body, name="allreduce_small", grid=(1,), in_specs=specs, out_specs=specs,
        out_shape=[_sds(a.shape) for a in arrays],
        scratch_shapes=([pltpu.VMEM(h, F32) for h in halves] + [pltpu.VMEM(h, F32) for h in halves]
                        + [pltpu.VMEM((NCHIP,) + h, dt) for h, dt in zip(halves, wire)] + [pltpu.VMEM(h, F32) for h in halves]
                        + [pltpu.SemaphoreType.DMA((5, n)), pltpu.SemaphoreType.DMA((5, n)), pltpu.SemaphoreType.DMA((n,))]),
        compiler_params=_params(32),
    )(*arrays)


def _adamw_terms(w, g, m, v):
    m = ADAM_B1 * m + (1.0 - ADAM_B1) * g
    v = ADAM_B2 * v + (1.0 - ADAM_B2) * jnp.square(g)
    m_hat = m / (1.0 - ADAM_B1 ** ADAM_STEP)
    v_hat = v / (1.0 - ADAM_B2 ** ADAM_STEP)
    return -ADAM_LR * (m_hat / (jnp.sqrt(v_hat) + ADAM_EPS) + ADAM_WD * w), m, v


ADAM_STEPS = 4


def _adamw_group(tag, ws, gs, ms, vs):
    n = len(ws)

    def body(*refs):
        ins, outs = refs[:4 * n], refs[4 * n:]
        for i in range(n):
            w, g, m, v = (ins[k * n + i][...] for k in range(4))
            outs[i][...] = g
            outs[n + i][...], outs[2 * n + i][...], outs[3 * n + i][...] = _adamw_terms(w, g, m, v)

    specs = [pl.BlockSpec((w.shape[0] // ADAM_STEPS, w.shape[1]), lambda i: (i, 0)) for w in ws]
    outs = _pallas_call(
        body, name="adamw_" + tag, grid=(ADAM_STEPS,), in_specs=specs * 4, out_specs=specs * 4,
        out_shape=[_sds(w.shape) for w in ws] * 4, compiler_params=_params(48),
    )(*_in_hbm(list(ws) + list(gs) + list(ms) + list(vs)))
    return outs[:n], outs[n:2 * n], outs[2 * n:3 * n], outs[3 * n:]


def _adamw_replicated(sums, row_of, direct):
    ns, nr, nd = len(sums), len(row_of), len(direct)

    def body(*refs):
        sum_refs = refs[:ns]
        ins = refs[ns:ns + 3 * nr + 4 * nd]
        outs = refs[ns + 3 * nr + 4 * nd:]
        for i, (_, _, _, si, row) in enumerate(row_of):
            w_ref, m_ref, v_ref = ins[3 * i:3 * i + 3]
            g = sum_refs[si][row:row + 1, :]
            outs[4 * i][...] = g
            outs[4 * i + 1][...], outs[4 * i + 2][...], outs[4 * i + 3][...] = _adamw_terms(w_ref[...], g, m_ref[...], v_ref[...])
        for i in range(nd):
            w_ref, m_ref, v_ref, g_ref = ins[3 * nr + 4 * i:3 * nr + 4 * i + 4]
            o = outs[4 * (nr + i):4 * (nr + i) + 4]
            g = g_ref[...]
            o[0][...] = g
            o[1][...], o[2][...], o[3][...] = _adamw_terms(w_ref[...], g, m_ref[...], v_ref[...])

    operands = list(sums)
    shapes = []
    for w, m, v, _, _ in row_of:
        operands += [w, m, v]
        shapes += [w.shape] * 4
    for w, m, v, g in direct:
        operands += [w, m, v, g]
        shapes += [w.shape] * 4
    flat = _pallas_call(
        body, name="adamw_replicated", grid=(1,), in_specs=[_full(a.shape) for a in operands],
        out_specs=[_full(s) for s in shapes], out_shape=[_sds(s) for s in shapes],
        compiler_params=_params(56),
    )(*operands)
    return [flat[4 * i:4 * i + 4] for i in range(nr + nd)]


class _Exchanges:
    def __init__(self, shards, conv_w, chip, core, apply):
        self.shards, self.conv_w, self.apply = shards, conv_w, apply
        self.active, self.calls = [], 0
        self.chip_core_idx = jnp.stack([chip, core]).astype(jnp.int32)

    def first(self):
        later = [n for n in self.shards if n != "w_in"]
        carry = _gather_group([self.shards["w_in"].astype(BF), self.conv_w], [True, False])
        outs = _to_bf16_group("gather_first", [self.shards[n] for n in later], carry)
        self.shards = dict(zip(later, outs))
        return {"w_in": outs[len(later)], "conv_w": jnp.transpose(outs[len(later) + 1], (1, 0, 2)).reshape(4, LW)}

    def gather(self, names):
        return _gather_group([self.shards[n] for n in names], [True] * len(names))

    def reduce(self, tag, grads):
        self.active.append({"tag": tag, "names": list(grads), "stage": 0, "grads": list(grads.values())})

    def run(self, call, hold=()):
        groups = [g for g in self.active if g["tag"] not in hold]
        carries = [self._exchange_of(g) for g in groups]
        carry = _combine(carries)
        outs = list(call(carry))
        own = len(outs) - len(carry.out_shapes)
        landed = outs[own:]
        for g, c in zip(groups, carries):
            self._sum_after(g, landed[:len(c.out_shapes)])
            landed = landed[len(c.out_shapes):]
        self.active = [g for g in self.active if g["stage"] < 3]
        return outs[:own]

    def _exchange_of(self, g):
        if g["stage"] == 0:
            return _swap_group(g["grads"])
        if g["stage"] == 1:
            return _exchange_group(g["bf16"])
        return _join_group(g["halves"])

    def _sum_after(self, g, landed):
        if g["stage"] == 0:
            g["f32"], g["bf16"] = _add_sibling_group(g["tag"], self.chip_core_idx, g["grads"], landed)
        elif g["stage"] == 1:
            g["halves"] = _add_chips_group(g["tag"], self.chip_core_idx, g["f32"], landed)
        else:
            self.apply(g["tag"], g["names"], [t.reshape(2 * t.shape[1], t.shape[2]) for t in landed])
        g["stage"] += 1

    def drain(self):
        while self.active:
            self.calls += 1
            self.run(lambda carry: _run_now("reduce_%d" % self.calls, carry))


INPUT_NAMES = (["x", "p"] + [n for n in
               ["g_mix", "w_in", "b_in", "lam_re", "lam_im", "log_dt", "s5_b_re", "s5_b_im", "s5_c_re", "s5_c_im", "s5_d",
                "w_glu", "b_glu", "conv_w", "conv_b", "w_r", "b_r", "w_i", "b_i", "lru_lambda", "w_a_out", "w_b_out", "w_o",
                "g_ffn", "w_ffn_gate", "w_ffn_up", "w_ffn_down", "g_ple_gate", "w_ple_gate", "b_ple_gate", "w_ple", "g_ple",
                "g_final"]])
WEIGHT_NAMES = INPUT_NAMES[2:]


def kernel(*args):
    names = INPUT_NAMES + ["loss_target"] + ["m_" + n for n in WEIGHT_NAMES] + ["v_" + n for n in WEIGHT_NAMES]
    assert len(args) == len(names)
    given = dict(zip(names, args))

    def view(name):
        a = given[name]
        return jnp.swapaxes(a, -1, -2) if name.endswith(TRANSPOSED) else a

    def unview(name, a):
        return jnp.swapaxes(a, -1, -2) if name in TRANSPOSED else a

    def local(name):
        return view(name) if name.endswith("g_final") else view(name)[0]

    xi, yi, ci = _mesh_pos()
    k0 = 2 * xi + yi
    x, p, tgt = given["x"][0], given["p"][0, 0], given["loss_target"][0]

    results = {}

    row_halves = {}

    def apply(tag, names, totals):
        totals = dict(zip(names, totals))
        row_halves.update({n: totals.pop(n) for n in names if n in ("w_in_lo", "w_in_hi")})
        if len(row_halves) == 2:
            totals["w_in"] = jnp.concatenate([row_halves.pop("w_in_lo"), row_halves.pop("w_in_hi")])
        names = list(totals)
        if not names:
            return
        new = _adamw_group(tag, [local(n) for n in names], list(totals.values()), [local("m_" + n) for n in names],
                           [local("v_" + n) for n in names])
        for kind, arrays in zip(("grad", "delta", "new_m", "new_v"), new):
            for n, arr in zip(names, arrays):
                results[kind, n] = unview(n, arr[None])

    comm = _Exchanges({n: local(n) for n, _ in SHARDED}, local("conv_w"), k0, ci, apply)
    w = {n: local(n) for n in WEIGHT_NAMES if n != "conv_w" and n not in dict(SHARDED)}
    gx, sums, blocks = _local_step(x, p, tgt, w, comm)

    sum_names, block_names = list(sums), list(blocks)
    red = _allreduce_small([sums[n] for n in sum_names] + [blocks[n] for n in block_names],
                           [F32] * len(sum_names) + [BF] * len(block_names))
    sums = dict(zip(sum_names, red[:len(sum_names)]))
    blocks = dict(zip(block_names, red[len(sum_names):]))
    loss = jnp.sum(sums[LOSS_ROW[0]][LOSS_ROW[1]])
    direct_g = _replicated_grads(w, sums, blocks)
    conv_rows = sums[CONV_W_ROWS[0]][CONV_W_ROWS[1]:CONV_W_ROWS[1] + 4]
    direct_g["conv_w"] = lax.dynamic_slice(conv_rows, (0, k0 * CONV_SHARD[1]), CONV_SHARD)
    as_row = lambda a: a.reshape(1, -1)
    row_names = list(ACC_ROWS)
    row_of = [(as_row(given[n]), as_row(given["m_" + n]), as_row(given["v_" + n]),
               sum_names.index(ACC_ROWS[n][0]), ACC_ROWS[n][1]) for n in row_names]
    direct_names = list(direct_g)
    direct = [(view(n), view("m_" + n), view("v_" + n), direct_g[n].reshape(view(n).shape)) for n in direct_names]
    done = _adamw_replicated([sums[n] for n in sum_names], row_of, direct)
    for n, four in zip(row_names + direct_names, done):
        for kind, arr in zip(("grad", "delta", "new_m", "new_v"), four):
            results[kind, n] = unview(n, arr).reshape(given[n].shape)

    out = [loss, gx[None]]
    for kind in ("grad", "delta", "new_m", "new_v"):
        out += [results[kind, n] for n in WEIGHT_NAMES]
    return tuple(out)
```

```python
import functools
import math

import jax
import jax.numpy as jnp
from jax import lax
from jax.experimental import pallas as pl
from jax.experimental.pallas import tpu as pltpu

F32 = jnp.float32
BF = jnp.bfloat16

D = 1024
S5W = 512
NG, NS, NP = 32, 64, 16
GN = NG * NS
LW = 1024
NH, HD = 16, 64
LRU_C = 8.0
FH = 2816
NCHIP = 4
FC = FH // NCHIP
PLE = 256
INC = S5W + LW + 2 * D
EPS = 1e-6
ADAM_LR, ADAM_B1, ADAM_B2, ADAM_EPS, ADAM_WD, ADAM_STEP = 0.001, 0.9, 0.999, 1e-08, 0.01, 10

TM = 256
TK = 1024
LC = 512
SUB = 8
VMEM_MB = 1024 * 1024
MESH = pl.DeviceIdType.MESH
ANY = pl.BlockSpec(memory_space=pl.ANY)


def _mm(a, b):
    return jnp.dot(a.astype(BF), b.astype(BF), preferred_element_type=F32)


def _mm_nt(a, b):
    return lax.dot_general(a.astype(BF), b.astype(BF), (((1,), (1,)), ((), ())), preferred_element_type=F32)


def _mm_tn(a, b):
    return lax.dot_general(a.astype(BF), b.astype(BF), (((0,), (0,)), ((), ())), preferred_element_type=F32)


def _blockdiag_mm(x, blocks_ref):
    n, rows, _ = blocks_ref.shape
    return jnp.concatenate([jnp.dot(x[:, j * rows:(j + 1) * rows], blocks_ref[j], preferred_element_type=F32)
                            for j in range(n)], axis=1)


def _blockdiag_mm_t(x, blocks_ref):
    n, _, wide = blocks_ref.shape
    return jnp.concatenate([lax.dot_general(x[:, j * wide:(j + 1) * wide], blocks_ref[j], (((1,), (1,)), ((), ())),
                                            preferred_element_type=F32) for j in range(n)], axis=1)


def _rms(x):
    r = lax.rsqrt(jnp.mean(x * x, axis=-1, keepdims=True) + EPS)
    return x * r, r


def _rms_bwd(dy, xh, r, g):
    dxh = dy * g
    return r * (dxh - xh * jnp.mean(dxh * xh, axis=-1, keepdims=True))


def _colsum(x):
    return jnp.sum(x, axis=0, keepdims=True)


def _sig(x):
    return jax.nn.sigmoid(x)


def _gelu_grad(x):
    c = math.sqrt(2.0 / math.pi)
    t = jnp.tanh(c * (x + 0.044715 * x * x * x))
    return 0.5 * (1.0 + t) + 0.5 * x * (1.0 - t * t) * c * (1.0 + 3.0 * 0.044715 * x * x)


def _neg_expm1(x):
    series = -x * (1.0 + x * (0.5 + x * (1.0 / 6.0 + x * (1.0 / 24.0))))
    return jnp.where(x > -0.03, series, 1.0 - jnp.exp(x))


def _tok(width):
    return pl.BlockSpec((TM, width), lambda i: (i, 0))


def _tok_rev(width, nt):
    return pl.BlockSpec((TM, width), lambda i: (nt - 1 - i, 0))


def _full(shape):
    return pl.BlockSpec(shape, lambda i: (0,) * len(shape))


def _params(vmem_mb, **kw):
    return pltpu.CompilerParams(dimension_semantics=("arbitrary",), vmem_limit_bytes=vmem_mb * VMEM_MB, **kw)


def _sds(shape, dtype=F32):
    return jax.ShapeDtypeStruct(shape, dtype)


def _far(shape, dtype=F32):
    return pltpu.HBM(shape, dtype)


class _Carried:
    def __init__(self, operands, out_shapes, sems, start, finish, aliases=None):
        self.operands, self.out_shapes, self.sems = list(operands), list(out_shapes), list(sems)
        self.start, self.finish, self.aliases = start, finish, dict(aliases or {})


def _in_hbm(arrays):
    return [pltpu.with_memory_space_constraint(a, pltpu.HBM) for a in arrays]


def _pallas_call(body, carry=None, **kw):
    if carry is None:
        return pl.pallas_call(body, **kw)

    def at_step(corner):
        hit = [pl.program_id(d) == (size - 1 if corner else 0) for d, size in enumerate(kw["grid"])]
        return functools.reduce(jnp.logical_and, hit)

    name, grid, compiler_params = kw["name"], kw["grid"], kw["compiler_params"]
    in_specs, out_specs, out_shape = list(kw["in_specs"]), list(kw["out_specs"]), list(kw["out_shape"])
    scratch_shapes = list(kw.get("scratch_shapes", ()))
    n_in, n_out, n_scr = len(in_specs), len(out_specs), len(scratch_shapes)
    c_in, c_out = len(carry.operands), len(carry.out_shapes)

    def full_body(*refs):
        ins, refs = refs[:n_in], refs[n_in:]
        c_ins, refs = refs[:c_in], refs[c_in:]
        outs, refs = refs[:n_out], refs[n_out:]
        c_outs, refs = refs[:c_out], refs[c_out:]
        scratch, c_sems = refs[:n_scr], refs[n_scr:]

        @pl.when(at_step(0))
        def _():
            carry.start(c_ins, c_outs, c_sems)

        body(*ins, *outs, *scratch)

        @pl.when(at_step(1))
        def _():
            carry.finish(c_ins, c_outs, c_sems)

    call = pl.pallas_call(
        full_body, name=name, grid=grid, in_specs=in_specs + [ANY] * c_in, out_specs=out_specs + [ANY] * c_out,
        out_shape=out_shape + list(carry.out_shapes), scratch_shapes=scratch_shapes + list(carry.sems),
        input_output_aliases={n_in + i: n_out + o for i, o in carry.aliases.items()},
        compiler_params=compiler_params)
    return lambda *operands: call(*operands, *_in_hbm(carry.operands))


def _resident(pairs, sems):
    first = pl.program_id(0) == 0
    copies = [pltpu.make_async_copy(src, dst, sems.at[j]) for j, (src, dst) in enumerate(pairs)]

    @pl.when(first)
    def _():
        for cp in copies:
            cp.start()

    def wait(j):
        @pl.when(first)
        def _():
            copies[j].wait()

    return wait


def _resident_now(pairs, sems):
    @pl.when(pl.program_id(0) == 0)
    def _():
        copies = [pltpu.make_async_copy(src, dst, sems.at[j]) for j, (src, dst) in enumerate(pairs)]
        for cp in copies:
            cp.start()
        for cp in copies:
            cp.wait()


def _row_iota(width):
    return lax.broadcasted_iota(jnp.int32, (SUB, width), 0)


def _bcast_row(x, row):
    return jnp.broadcast_to(x[row:row + 1, :], x.shape)


def _slab(k):
    return pl.ds(pl.multiple_of(k * SUB, SUB), SUB)


QC = INC // NCHIP
Z_PARTS = ((0, S5W), (S5W, S5W + LW), (S5W + LW, INC))


def _inproj_fwd(x, g_mix, w_in, b_in, carry=None):
    L = x.shape[0]

    def body(x_ref, g_ref, w_hbm, b_ref, h_ref, ua_ref, ub_ref, gp_ref, w_vm, w_sems):
        _resident_now([(w_hbm.at[k], w_vm.at[k]) for k in range(NCHIP)], w_sems)
        xh, _ = _rms(x_ref[...])
        h = (xh * g_ref[...]).astype(BF)
        h_ref[...] = h
        for k in range(NCHIP):
            lo, hi = k * QC, (k + 1) * QC
            z = jnp.dot(h, w_vm[k], preferred_element_type=F32) + b_ref[:, lo:hi]
            for ref, (a, b) in zip((ua_ref, ub_ref, gp_ref), Z_PARTS):
                s, e = max(lo, a), min(hi, b)
                if s < e:
                    ref[:, s - a:e - a] = z[:, s - lo:e - lo]

    return _pallas_call(
        body, carry, name="inproj_fwd", grid=(L // TM,),
        in_specs=[_tok(D), _full((1, D)), ANY, _full((1, INC))],
        out_specs=[_tok(D), _tok(S5W), _tok(LW), _tok(2 * D)],
        out_shape=[_far((L, D), BF), _far((L, S5W)), _far((L, LW)), _sds((L, 2 * D))],
        scratch_shapes=[pltpu.VMEM((NCHIP, D, QC), BF), pltpu.SemaphoreType.DMA((NCHIP,))],
        compiler_params=_params(40),
    )(*_in_hbm([x]), g_mix, *_in_hbm([w_in]), b_in)


def _inproj_bwd(x, dx1, dua, dub, dgp, g_mix, w_in, carry=None):
    L = x.shape[0]

    def body(x_ref, dx1_ref, dua_ref, dub_ref, dgp_ref, g_ref, w_hbm, gx_ref, dz_ref, dg_ref, db_ref, w_vm, w_sems):
        _resident_now([(w_hbm.at[k], w_vm.at[k]) for k in range(NCHIP)], w_sems)

        @pl.when(pl.program_id(0) == 0)
        def _():
            dg_ref[...] = jnp.zeros_like(dg_ref)
            db_ref[...] = jnp.zeros_like(db_ref)

        for src, (a, b) in zip((dua_ref, dub_ref, dgp_ref), Z_PARTS):
            d = src[...]
            dz_ref[:, a:b] = d.astype(BF)
            db_ref[0:1, a:b] += _colsum(d)
        dh = jnp.zeros((TM, D), F32)
        for k in range(NCHIP):
            dh = dh + lax.dot_general(dz_ref[:, k * QC:(k + 1) * QC], w_vm[k], (((1,), (1,)), ((), ())),
                                      preferred_element_type=F32)
        xh, r = _rms(x_ref[...])
        dg_ref[0:1, :] += _colsum(dh * xh)
        gx_ref[...] = dx1_ref[...] + _rms_bwd(dh, xh, r, g_ref[...])

    return _pallas_call(
        body, carry, name="inproj_bwd", grid=(L // TM,),
        in_specs=[_tok(D), _tok(D), _tok(S5W), _tok(LW), _tok(2 * D), _full((1, D)), ANY],
        out_specs=[_tok(D), _tok(INC), _full((SUB, D)), _full((SUB, INC))],
        out_shape=[_sds((L, D)), _sds((L, INC), BF), _sds((SUB, D)), _sds((SUB, INC))],
        scratch_shapes=[pltpu.VMEM((NCHIP, D, QC), BF), pltpu.SemaphoreType.DMA((NCHIP,))],
        compiler_params=_params(40),
    )(x, dx1, *_in_hbm([dua]), dub, dgp, g_mix, *_in_hbm([w_in]))


def _cscan(xr_ref, xi_ref, con_ref, cr_ref, ci_ref, reverse):
    n_slab = xr_ref.shape[0] // SUB
    width = xr_ref.shape[1]
    for lc in range(width // LC):
        cols = slice(lc * LC, (lc + 1) * LC)
        con = [con_ref[SUB * j:SUB * (j + 1), cols] for j in range(8)]

        def step(k, carry, cols=cols, con=con):
            cr, ci = carry
            rows = _slab(n_slab - 1 - k if reverse else k)
            xr, xi = xr_ref[rows, cols], xi_ref[rows, cols]
            for j, sh in enumerate((1, 2, 4)):
                mr, mi = con[2 * j], con[2 * j + 1]
                pr = pltpu.roll(xr, SUB - sh if reverse else sh, 0)
                pi = pltpu.roll(xi, SUB - sh if reverse else sh, 0)
                xr, xi = xr + mr * pr - mi * pi, xi + mr * pi + mi * pr
            xr, xi = xr + con[6] * cr - con[7] * ci, xi + con[6] * ci + con[7] * cr
            xr_ref[rows, cols] = xr
            xi_ref[rows, cols] = xi
            row = 0 if reverse else SUB - 1
            return _bcast_row(xr, row), _bcast_row(xi, row)

        cr, ci = lax.fori_loop(0, n_slab, step, (cr_ref[:, cols], ci_ref[:, cols]))
        cr_ref[:, cols] = cr
        ci_ref[:, cols] = ci


def _s5_fwd(ua, bbr, bbi, ccr, cci, dsk, con, w_glu, b_glu, carry=None):
    L = ua.shape[0]

    def body(ua_ref, bbr_hbm, bbi_hbm, ccr_hbm, cci_hbm, dsk_ref, con_ref, wg_ref, bg_ref,
             sr_ref, si_ref, y_ref, zg_ref, ya_ref, bbr_vm, bbi_vm, ccr_vm, cci_vm, cr_ref, ci_ref, w_sems):
        landed = _resident([(bbr_hbm, bbr_vm), (bbi_hbm, bbi_vm), (ccr_hbm, ccr_vm), (cci_hbm, cci_vm)], w_sems)

        @pl.when(pl.program_id(0) == 0)
        def _():
            cr_ref[...] = jnp.zeros_like(cr_ref)
            ci_ref[...] = jnp.zeros_like(ci_ref)

        u = ua_ref[...]
        ub = u.astype(BF)
        landed(0)
        sr_ref[...] = _blockdiag_mm(ub, bbr_vm)
        landed(1)
        si_ref[...] = _blockdiag_mm(ub, bbi_vm)
        _cscan(sr_ref, si_ref, con_ref, cr_ref, ci_ref, reverse=False)
        landed(2)
        landed(3)
        y = (_blockdiag_mm_t(sr_ref[...].astype(BF), ccr_vm) - _blockdiag_mm_t(si_ref[...].astype(BF), cci_vm)
             + dsk_ref[...] * u)
        y_ref[...] = y
        zg = jax.nn.gelu(y)
        zg_ref[...] = zg.astype(BF)
        q = _mm(zg, wg_ref[...]) + bg_ref[...]
        ya_ref[...] = (zg * _sig(q)).astype(BF)

    return _pallas_call(
        body, carry, name="s5_fwd", grid=(L // TM,),
        in_specs=[_tok(S5W), ANY, ANY, ANY, ANY, _full((1, S5W)), _full((8 * SUB, GN)),
                  _full((S5W, S5W)), _full((1, S5W))],
        out_specs=[_tok(GN), _tok(GN), _tok(S5W), _tok(S5W), _tok(S5W)],
        out_shape=[_sds((L, GN)), _sds((L, GN)), _far((L, S5W)), _far((L, S5W), BF), _far((L, S5W), BF)],
        scratch_shapes=[pltpu.VMEM((S5W // 128, 128, GN // (S5W // 128)), BF)] * 4 + [
                        pltpu.VMEM((SUB, GN), F32), pltpu.VMEM((SUB, GN), F32),
                        pltpu.SemaphoreType.DMA((4,))],
        compiler_params=_params(44),
    )(*_in_hbm([ua, bbr, bbi, ccr, cci]), dsk, con, w_glu, b_glu)


def _s5_bwd(dya, y, ua, sr, si, bbr, bbi, ccr, cci, dsk, con_rev, w_glu, b_glu, carry=None):
    L = ua.shape[0]
    nt = L // TM
    spt = TM // SUB
    n_slab = spt

    def halo_map(i):
        return (jnp.maximum((nt - 1 - i) * spt - 1, 0), 0)

    def body(dya_ref, y_ref, ua_ref, sr_ref, si_ref, hr_ref, hi_ref, bbr_hbm, bbi_hbm, ccr_hbm, cci_hbm,
             dsk_ref, con_ref, wg_ref, bg_ref,
             dua_ref, dq_ref, dy_ref, lr_ref, li_ref, da_ref, dsm_ref,
             bbr_vm, bbi_vm, ccr_vm, cci_vm, cr_ref, ci_ref, w_sems):
        i = pl.program_id(0)
        landed = _resident([(ccr_hbm, ccr_vm), (cci_hbm, cci_vm), (bbr_hbm, bbr_vm), (bbi_hbm, bbi_vm)], w_sems)

        @pl.when(i == 0)
        def _():
            cr_ref[...] = jnp.zeros_like(cr_ref)
            ci_ref[...] = jnp.zeros_like(ci_ref)
            da_ref[...] = jnp.zeros_like(da_ref)
            dsm_ref[...] = jnp.zeros_like(dsm_ref)

        u = ua_ref[...]
        yv = y_ref[...]
        dya = dya_ref[...]
        zg = jax.nn.gelu(yv)
        sg = _sig(_mm(zg, wg_ref[...]) + bg_ref[...])
        dq = dya * zg * sg * (1.0 - sg)
        dq_ref[...] = dq.astype(BF)
        dzg = dya * sg + _mm_nt(dq, wg_ref[...])
        dy = dzg * _gelu_grad(yv)
        dyb = dy.astype(BF)
        dy_ref[...] = dyb
        dsm_ref[0:1, :] += _colsum(dy * u)
        dsm_ref[1:2, :] += _colsum(dq)
        landed(0)
        lr_ref[...] = _blockdiag_mm(dyb, ccr_vm)
        landed(1)
        li_ref[...] = -_blockdiag_mm(dyb, cci_vm)
        _cscan(lr_ref, li_ref, con_ref, cr_ref, ci_ref, reverse=True)

        first_tile = (i == nt - 1)
        row = _row_iota(LC)
        for lc in range(GN // LC):
            cols = slice(lc * LC, (lc + 1) * LC)
            h_r = jnp.where(first_tile, 0.0, hr_ref[:, cols])
            h_i = jnp.where(first_tile, 0.0, hi_ref[:, cols])

            def step(k, acc, cols=cols, h_r=h_r, h_i=h_i):
                ar, ai = acc
                rows = _slab(k)
                prev = _slab(jnp.maximum(k - 1, 0))
                pr = jnp.where(k == 0, h_r, sr_ref[prev, cols])
                pi = jnp.where(k == 0, h_i, si_ref[prev, cols])
                spr = pltpu.roll(jnp.where(row == SUB - 1, pr, sr_ref[rows, cols]), 1, 0)
                spi = pltpu.roll(jnp.where(row == SUB - 1, pi, si_ref[rows, cols]), 1, 0)
                lr, li = lr_ref[rows, cols], li_ref[rows, cols]
                return ar + lr * spr + li * spi, ai + li * spr - lr * spi

            zero = jnp.zeros((SUB, LC), F32)
            ar, ai = lax.fori_loop(0, n_slab, step, (zero, zero))
            da_ref[0:1, cols] += _colsum(ar)
            da_ref[1:2, cols] += _colsum(ai)

        landed(2)
        landed(3)
        dua_ref[...] = (dy * dsk_ref[...] + _blockdiag_mm_t(lr_ref[...].astype(BF), bbr_vm)
                        + _blockdiag_mm_t(li_ref[...].astype(BF), bbi_vm))

    return _pallas_call(
        body, carry, name="s5_bwd", grid=(nt,),
        in_specs=[_tok_rev(S5W, nt), _tok_rev(S5W, nt), _tok_rev(S5W, nt), _tok_rev(GN, nt), _tok_rev(GN, nt),
                  pl.BlockSpec((SUB, GN), halo_map), pl.BlockSpec((SUB, GN), halo_map),
                  ANY, ANY, ANY, ANY, _full((1, S5W)), _full((8 * SUB, GN)), _full((S5W, S5W)), _full((1, S5W))],
        out_specs=[_tok_rev(S5W, nt), _tok_rev(S5W, nt), _tok_rev(S5W, nt), _tok_rev(GN, nt), _tok_rev(GN, nt),
                   _full((SUB, GN)), _full((SUB, S5W))],
        out_shape=[_sds((L, S5W)), _sds((L, S5W), BF), _sds((L, S5W), BF), _sds((L, GN)), _sds((L, GN)),
                   _sds((SUB, GN)), _sds((SUB, S5W))],
        scratch_shapes=[pltpu.VMEM((S5W // 128, 128, GN // (S5W // 128)), BF)] * 4 + [
                        pltpu.VMEM((SUB, GN), F32), pltpu.VMEM((SUB, GN), F32),
                        pltpu.SemaphoreType.DMA((4,))],
        compiler_params=_params(52),
    )(dya, y, ua, sr, si, sr, si, *_in_hbm([bbr, bbi, ccr, cci]), dsk, con_rev, w_glu, b_glu)


def _lru_gate_terms(rg, sp):
    log_a = -LRU_C * rg * sp
    a = jnp.exp(log_a)
    mult = jnp.sqrt(_neg_expm1(2.0 * log_a))
    return a, mult


def _lru_fwd(ub, conv_w, conv_b, wr, wi, b_r, b_i, sp, carry=None):
    L = ub.shape[0]
    n_slab = TM // SUB

    def body(ub_ref, cw_ref, cb_ref, wr_ref, wi_ref, br_ref, bi_ref, sp_ref,
             xc_ref, rg_ref, ig_ref, h_ref, hp_ref, a_ref, halo_ref, carry_ref):
        @pl.when(pl.program_id(0) == 0)
        def _():
            halo_ref[...] = jnp.zeros_like(halo_ref)
            carry_ref[...] = jnp.zeros_like(carry_ref)

        row = _row_iota(LW)
        taps = [cw_ref[k:k + 1, :] for k in range(4)]
        cb = cb_ref[...]

        def conv_step(k, prev):
            rows = _slab(k)
            cur = ub_ref[rows, :]
            acc = taps[3] * cur + cb
            for j in (1, 2, 3):
                acc = acc + taps[3 - j] * pltpu.roll(jnp.where(row >= SUB - j, prev, cur), j, 0)
            xc_ref[rows, :] = acc
            return cur

        halo_ref[...] = lax.fori_loop(0, n_slab, conv_step, halo_ref[...])

        xc = xc_ref[...]
        xcb = xc.astype(BF)
        rg = _sig(_blockdiag_mm(xcb, wr_ref) + br_ref[...])
        ig = _sig(_blockdiag_mm(xcb, wi_ref) + bi_ref[...])
        rg_ref[...] = rg
        ig_ref[...] = ig
        a, mult = _lru_gate_terms(rg, sp_ref[...])
        a_ref[...] = a
        h_ref[...] = mult * ig * xc

        rowc = _row_iota(LC)
        for lc in range(LW // LC):
            cols = slice(lc * LC, (lc + 1) * LC)

            def step(k, c, cols=cols):
                rows = _slab(k)
                av, b = a_ref[rows, cols], h_ref[rows, cols]
                for sh in (1, 2, 4):
                    keep = rowc >= sh
                    b = b + av * jnp.where(keep, pltpu.roll(b, sh, 0), 0.0)
                    av = av * jnp.where(keep, pltpu.roll(av, sh, 0), 1.0)
                h = b + av * c
                h_ref[rows, cols] = h
                hp_ref[rows, cols] = jnp.where(rowc == 0, c, pltpu.roll(h, 1, 0))
                return _bcast_row(h, SUB - 1)

            carry_ref[:, cols] = lax.fori_loop(0, n_slab, step, carry_ref[:, cols])

    return _pallas_call(
        body, carry, name="lru_fwd", grid=(L // TM,),
        in_specs=[_tok(LW), _full((4, LW)), _full((1, LW)), _full((LW // 128, 128, 128)), _full((LW // 128, 128, 128)),
                  _full((1, LW)), _full((1, LW)), _full((1, LW))],
        out_specs=[_tok(LW)] * 5,
        out_shape=[_far((L, LW))] * 5,
        scratch_shapes=[pltpu.VMEM((TM, LW), F32), pltpu.VMEM((SUB, LW), F32), pltpu.VMEM((SUB, LW), F32)],
        compiler_params=_params(40),
    )(*_in_hbm([ub]), conv_w, conv_b, wr, wi, b_r, b_i, sp)


def _lru_bwd(dyb, xc, rg, ig, hp, ub, conv_w, wr, wi, sp, dsp, carry=None):
    L = ub.shape[0]
    nt = L // TM
    spt = TM // SUB
    n_slab = spt

    def halo_map(i):
        return (jnp.maximum((nt - 1 - i) * spt - 1, 0), 0)

    def body(dh_ref, xc_ref, rg_ref, ig_ref, hp_ref, ub_ref, uh_ref, cw_ref, wr_ref, wi_ref, sp_ref, dsp_ref,
             dub_ref, dpr_ref, dpi_ref, acc_ref, a_ref, lam_ref, dxc_ref, carry_ref, next_ref):
        i = pl.program_id(0)

        @pl.when(i == 0)
        def _():
            carry_ref[...] = jnp.zeros_like(carry_ref)
            next_ref[...] = jnp.zeros_like(next_ref)
            acc_ref[...] = jnp.zeros_like(acc_ref)

        sp = sp_ref[...]
        rg, ig, xc = rg_ref[...], ig_ref[...], xc_ref[...]
        a, mult = _lru_gate_terms(rg, sp)
        a_ref[...] = a

        rowc = _row_iota(LC)
        for lc in range(LW // LC):
            cols = slice(lc * LC, (lc + 1) * LC)

            def step(k, c, cols=cols):
                rows = _slab(n_slab - 1 - k)
                av, dh = a_ref[rows, cols], dh_ref[rows, cols]
                b = av * dh
                for sh in (1, 2, 4):
                    keep = rowc < SUB - sh
                    b = b + av * jnp.where(keep, pltpu.roll(b, SUB - sh, 0), 0.0)
                    av = av * jnp.where(keep, pltpu.roll(av, SUB - sh, 0), 1.0)
                mu = b + av * c
                lam_ref[rows, cols] = dh + jnp.where(rowc == SUB - 1, c, pltpu.roll(mu, SUB - 1, 0))
                return _bcast_row(mu, 0)

            carry_ref[:, cols] = lax.fori_loop(0, n_slab, step, carry_ref[:, cols])

        lam = lam_ref[...]
        d_a = lam * hp_ref[...]
        d_mult = lam * ig * xc
        d_ig = lam * mult * xc
        dxc = lam * mult * ig
        d_log_a = d_a * a - d_mult * a * a / mult
        d_rg = (-LRU_C) * sp * d_log_a
        acc_ref[0:1, :] += _colsum((-LRU_C) * rg * d_log_a) * dsp_ref[...]
        dpr = d_rg * rg * (1.0 - rg)
        dpi = d_ig * ig * (1.0 - ig)
        acc_ref[1:2, :] += _colsum(dpr)
        acc_ref[2:3, :] += _colsum(dpi)
        dprb, dpib = dpr.astype(BF), dpi.astype(BF)
        dpr_ref[...] = dprb
        dpi_ref[...] = dpib
        dxc = dxc + _blockdiag_mm_t(dprb, wr_ref) + _blockdiag_mm_t(dpib, wi_ref)
        dxc_ref[...] = dxc
        acc_ref[3:4, :] += _colsum(dxc)

        row = _row_iota(LW)
        taps = [cw_ref[k:k + 1, :] for k in range(4)]
        u_halo = jnp.where(i == nt - 1, 0.0, uh_ref[...])
        nxt_tile = next_ref[...]

        def conv_step(k, accs):
            rows = _slab(k)
            cur = dxc_ref[rows, :]
            nxt = jnp.where(k == n_slab - 1, nxt_tile, dxc_ref[_slab(jnp.minimum(k + 1, n_slab - 1)), :])
            ucur = ub_ref[rows, :]
            uprev = jnp.where(k == 0, u_halo, ub_ref[_slab(jnp.maximum(k - 1, 0)), :])
            du = taps[3] * cur
            new = [accs[3] + cur * ucur]
            for j in (1, 2, 3):
                du = du + taps[3 - j] * pltpu.roll(jnp.where(row < j, nxt, cur), SUB - j, 0)
                new.append(accs[3 - j] + cur * pltpu.roll(jnp.where(row >= SUB - j, uprev, ucur), j, 0))
            dub_ref[rows, :] = du
            return tuple(new[::-1])

        zero = jnp.zeros((SUB, LW), F32)
        accs = lax.fori_loop(0, n_slab, conv_step, (zero, zero, zero, zero))
        for k in range(4):
            acc_ref[4 + k:5 + k, :] += _colsum(accs[k])
        next_ref[...] = dxc_ref[0:SUB, :]

    return _pallas_call(
        body, carry, name="lru_bwd", grid=(nt,),
        in_specs=[_tok_rev(LW, nt)] * 6 + [pl.BlockSpec((SUB, LW), halo_map), _full((4, LW)),
                                           _full((LW // 128, 128, 128)), _full((LW // 128, 128, 128)), _full((1, LW)), _full((1, LW))],
        out_specs=[_tok_rev(LW, nt), _tok_rev(LW, nt), _tok_rev(LW, nt), _full((SUB, LW))],
        out_shape=[_sds((L, LW)), _far((L, LW), BF), _far((L, LW), BF), _sds((SUB, LW))],
        scratch_shapes=[pltpu.VMEM((TM, LW), F32), pltpu.VMEM((TM, LW), F32), pltpu.VMEM((TM, LW), F32),
                        pltpu.VMEM((SUB, LW), F32), pltpu.VMEM((SUB, LW), F32)],
        compiler_params=_params(48),
    )(dyb, xc, rg, ig, hp, ub, ub, conv_w, wr, wi, sp, dsp)


AC = D // NCHIP


def _merge_fwd(x, ya, yb, gp, w_a, w_b, w_o, carry=None):
    L = x.shape[0]

    def body(x_ref, ya_ref, yb_ref, gp_ref, wa_ref, wb_ref, wo_ref, x1_ref, pa_ref, pb_ref, mg_ref):
        ya = ya_ref[...]
        for k in range(NCHIP):
            pa_ref[:, k * AC:(k + 1) * AC] = jnp.dot(ya, wa_ref[k], preferred_element_type=F32)
        pb = _mm(yb_ref[...], wb_ref[...])
        pb_ref[...] = pb
        gp = gp_ref[...]
        merged = (_sig(gp[:, :D]) * pa_ref[...] + _sig(gp[:, D:]) * pb).astype(BF)
        mg_ref[...] = merged
        x1_ref[...] = x_ref[...] + jnp.dot(merged, wo_ref[...], preferred_element_type=F32)

    return _pallas_call(
        body, carry, name="merge_fwd", grid=(L // TM,),
        in_specs=[_tok(D), _tok(S5W), _tok(LW), _tok(2 * D), _full((NCHIP, S5W, AC)), _full((LW, D)), _full((D, D))],
        out_specs=[_tok(D), _tok(D), _tok(D), _tok(D)],
        out_shape=[_sds((L, D)), _sds((L, D)), _sds((L, D)), _far((L, D), BF)],
        compiler_params=_params(40),
    )(x, ya, yb, gp, w_a, w_b, w_o)


def _merge_bwd(dx1, gp, pa, pb, w_a, w_b, w_o, carry=None):
    L = dx1.shape[0]

    def body(dx1_ref, gp_ref, pa_ref, pb_ref, wa_ref, wb_ref, wo_ref, dya_ref, dyb_ref, dgp_ref, dpa_ref, dpb_ref):
        dm = _mm_nt(dx1_ref[...], wo_ref[...])
        gp = gp_ref[...]
        sa, sb = _sig(gp[:, :D]), _sig(gp[:, D:])
        dpa = (dm * sa).astype(BF)
        dpb = (dm * sb).astype(BF)
        dpa_ref[...] = dpa
        dpb_ref[...] = dpb
        dgp_ref[:, :D] = dm * pa_ref[...] * sa * (1.0 - sa)
        dgp_ref[:, D:] = dm * pb_ref[...] * sb * (1.0 - sb)
        dya = jnp.zeros((TM, S5W), F32)
        for k in range(NCHIP):
            dya = dya + _mm_nt(dpa[:, k * AC:(k + 1) * AC], wa_ref[k])
        dya_ref[...] = dya
        dyb_ref[...] = _mm_nt(dpb, wb_ref[...])

    return _pallas_call(
        body, carry, name="merge_bwd", grid=(L // TM,),
        in_specs=[_tok(D), _tok(2 * D), _tok(D), _tok(D), _full((NCHIP, S5W, AC)), _full((LW, D)), _full((D, D))],
        out_specs=[_tok(S5W), _tok(LW), _tok(2 * D), _tok(D), _tok(D)],
        out_shape=[_far((L, S5W)), _far((L, LW)), _sds((L, 2 * D)), _far((L, D), BF), _far((L, D), BF)],
        compiler_params=_params(40),
    )(dx1, gp, pa, pb, w_a, w_b, w_o)


def _chunk_tok(width):
    return pl.BlockSpec((NCHIP, TM, width), lambda i: (0, i, 0))


def _ffn_fwd(x1, g_ffn, wg, wu, wd, carry=None):
    L = x1.shape[0]

    def body(x_ref, g_ref, wg_hbm, wu_hbm, wd_hbm, x2_ref, h2_ref, gg_ref, uu_ref, wg_vm, wu_vm, wd_vm, w_sems):
        _resident_now([(src.at[c], dst.at[c]) for c in range(NCHIP)
                       for src, dst in ((wg_hbm, wg_vm), (wu_hbm, wu_vm), (wd_hbm, wd_vm))], w_sems)
        x = x_ref[...]
        xh, _ = _rms(x)
        h2 = (xh * g_ref[...]).astype(BF)
        h2_ref[...] = h2
        out = x
        for c in range(NCHIP):
            gg = lax.dot_general(h2, wg_vm[c], (((1,), (1,)), ((), ())), preferred_element_type=F32)
            uu = lax.dot_general(h2, wu_vm[c], (((1,), (1,)), ((), ())), preferred_element_type=F32)
            gg_ref[c] = gg.astype(BF)
            uu_ref[c] = uu.astype(BF)
            act = (gg * _sig(gg) * uu).astype(BF)
            out = out + jnp.dot(act, wd_vm[c], preferred_element_type=F32)
        x2_ref[...] = out

    return _pallas_call(
        body, carry, name="ffn_fwd", grid=(L // TM,),
        in_specs=[_tok(D), _full((1, D)), ANY, ANY, ANY],
        out_specs=[_tok(D), _tok(D), _chunk_tok(FC), _chunk_tok(FC)],
        out_shape=[_sds((L, D)), _sds((L, D), BF), _sds((NCHIP, L, FC), BF), _sds((NCHIP, L, FC), BF)],
        scratch_shapes=[pltpu.VMEM((NCHIP, FC, D), BF)] * 3 + [pltpu.SemaphoreType.DMA((3 * NCHIP,))],
        compiler_params=_params(52),
    )(x1, g_ffn, wg, wu, wd)


def _ffn_bwd(x1, dx2, gg, uu, g_ffn, wg, wu, wd, carry=None):
    L = x1.shape[0]

    def body(x_ref, dx2_ref, gg_ref, uu_ref, g_ref, wg_hbm, wu_hbm, wd_hbm,
             dx1_ref, act_ref, dgg_ref, duu_ref, dg_ref, wg_vm, wu_vm, wd_vm, w_sems):
        _resident_now([(src.at[c], dst.at[c]) for c in range(NCHIP)
                       for src, dst in ((wg_hbm, wg_vm), (wu_hbm, wu_vm), (wd_hbm, wd_vm))], w_sems)

        @pl.when(pl.program_id(0) == 0)
        def _():
            dg_ref[...] = jnp.zeros_like(dg_ref)

        dx2 = dx2_ref[...]
        dx2b = dx2.astype(BF)
        dh2 = jnp.zeros((TM, D), F32)
        for c in range(NCHIP):
            g = gg_ref[c].astype(F32)
            u = uu_ref[c].astype(F32)
            s = _sig(g)
            silu = g * s
            act_ref[c] = (silu * u).astype(BF)
            dact = lax.dot_general(dx2b, wd_vm[c], (((1,), (1,)), ((), ())), preferred_element_type=F32)
            dg = (dact * u * s * (1.0 + g * (1.0 - s))).astype(BF)
            du = (dact * silu).astype(BF)
            dgg_ref[c] = dg
            duu_ref[c] = du
            dh2 = dh2 + jnp.dot(dg, wg_vm[c], preferred_element_type=F32)
            dh2 = dh2 + jnp.dot(du, wu_vm[c], preferred_element_type=F32)
        xh, r = _rms(x_ref[...])
        dg_ref[0:1, :] += _colsum(dh2 * xh)
        dx1_ref[...] = dx2 + _rms_bwd(dh2, xh, r, g_ref[...])

    return _pallas_call(
        body, carry, name="ffn_bwd", grid=(L // TM,),
        in_specs=[_tok(D), _tok(D), _chunk_tok(FC), _chunk_tok(FC), _full((1, D)), ANY, ANY, ANY],
        out_specs=[_tok(D), _chunk_tok(FC), _chunk_tok(FC), _chunk_tok(FC), _full((SUB, D))],
        out_shape=[_sds((L, D)), _sds((NCHIP, L, FC), BF), _sds((NCHIP, L, FC), BF), _sds((NCHIP, L, FC), BF),
                   _sds((SUB, D))],
        scratch_shapes=[pltpu.VMEM((NCHIP, FC, D), BF)] * 3 + [pltpu.SemaphoreType.DMA((3 * NCHIP,))],
        compiler_params=_params(56),
    )(x1, dx2, gg, uu, g_ffn, wg, wu, wd)


def _ple_loss(x2, p, tgt, g_pg, w_pg, b_pg, w_ple, g_ple, g_final):
    L = x2.shape[0]

    def body(x2_ref, p_ref, t_ref, gpg_ref, wpg_ref, bpg_ref, wple_ref, gple_ref, gf_ref,
             dx2_ref, n2_ref, dpre_ref, de0_ref, acc_ref):
        @pl.when(pl.program_id(0) == 0)
        def _():
            acc_ref[...] = jnp.zeros_like(acc_ref)

        x2 = x2_ref[...]
        x2h, r2 = _rms(x2)
        n2 = (x2h * gpg_ref[...]).astype(BF)
        n2_ref[...] = n2
        gate = _sig(jnp.dot(n2, wpg_ref[...], preferred_element_type=F32) + bpg_ref[...])
        pb = p_ref[...].astype(BF)
        e0 = jnp.concatenate([jnp.dot(pb, wple_ref[k], preferred_element_type=F32) for k in range(NCHIP)], axis=1)
        e0h, re = _rms(e0)
        e = e0h * gple_ref[...]
        x3 = x2 + gate * e
        x3h, r3 = _rms(x3)
        diff = x3h * gf_ref[...] - t_ref[...]
        acc_ref[4:5, :] += _colsum(diff * diff) * (0.5 / D)
        dy = diff * (1.0 / D)
        acc_ref[3:4, :] += _colsum(dy * x3h)
        dx3 = _rms_bwd(dy, x3h, r3, gf_ref[...])
        de = dx3 * gate
        acc_ref[2:3, :] += _colsum(de * e0h)
        de0_ref[...] = _rms_bwd(de, e0h, re, gple_ref[...]).astype(BF)
        dpre = dx3 * e * gate * (1.0 - gate)
        acc_ref[1:2, :] += _colsum(dpre)
        dpreb = dpre.astype(BF)
        dpre_ref[...] = dpreb
        dn2 = lax.dot_general(dpreb, wpg_ref[...], (((1,), (1,)), ((), ())), preferred_element_type=F32)
        acc_ref[0:1, :] += _colsum(dn2 * x2h)
        dx2_ref[...] = dx3 + _rms_bwd(dn2, x2h, r2, gpg_ref[...])

    return _pallas_call(
        body, name="ple_loss", grid=(L // TM,),
        in_specs=[_tok(D), _tok(PLE), _tok(D), _full((1, D)), _full((D, D)), _full((1, D)), _full((NCHIP, PLE, AC)),
                  _full((1, D)), _full((1, D))],
        out_specs=[_tok(D), _tok(D), _tok(D), _tok(D), _full((SUB, D))],
        out_shape=[_sds((L, D)), _sds((L, D), BF), _sds((L, D), BF), _sds((L, D), BF), _sds((SUB, D))],
        compiler_params=_params(40),
    )(x2, p, tgt, g_pg, *_in_hbm([w_pg]), b_pg, *_in_hbm([w_ple]), g_ple, g_final)


def _tn(name, a, b, col_chunk=None, a_block=None, carry=None):
    L = a.shape[-2]
    m, n = a.shape[-1], b.shape[-1]
    a_col = 0
    if a_block is not None:
        a_col, m = a_block
    tk = L if (a.ndim == 3 or b.ndim == 3 or a_block is not None) else TK
    if a.ndim == 3 or b.ndim == 3:
        nj, bn = (a if a.ndim == 3 else b).shape[0], n
        a_spec = (pl.BlockSpec((None, tk, m), lambda j, t: (j, t, 0)) if a.ndim == 3
                  else pl.BlockSpec((tk, m), lambda j, t: (t, 0)))
        b_spec = (pl.BlockSpec((None, tk, n), lambda j, t: (j, t, 0)) if b.ndim == 3
                  else pl.BlockSpec((tk, n), lambda j, t: (t, 0)))
        out_spec, out_shape = pl.BlockSpec((None, m, n), lambda j, t: (j, 0, 0)), _sds((nj, m, n))
    else:
        bn = col_chunk
        if bn is None:
            bn = next((cand for cand in (1024, 512) if n > cand and n % cand == 0), n)
        nj = n // bn
        a_spec = pl.BlockSpec((tk, m), lambda j, t: (t, a_col))
        b_spec = pl.BlockSpec((tk, bn), lambda j, t: (t, j))
        if col_chunk is None:
            out_spec, out_shape = pl.BlockSpec((m, bn), lambda j, t: (0, j)), _sds((m, n))
        else:
            out_spec, out_shape = pl.BlockSpec((None, m, bn), lambda j, t: (j, 0, 0)), _sds((nj, m, bn))

    def body(a_ref, b_ref, o_ref):
        if tk == L:
            o_ref[...] = _mm_tn(a_ref[...], b_ref[...])
        else:
            @pl.when(pl.program_id(1) == 0)
            def _():
                o_ref[...] = jnp.zeros_like(o_ref)

            o_ref[...] += _mm_tn(a_ref[...], b_ref[...])

    outs = _pallas_call(
        body, carry, name=name, grid=(nj, L // tk), in_specs=[a_spec, b_spec], out_specs=[out_spec],
        out_shape=[pltpu.HBM(out_shape.shape, out_shape.dtype)],
        compiler_params=pltpu.CompilerParams(dimension_semantics=("arbitrary", "arbitrary"),
                                             vmem_limit_bytes=(30 if tk == L else 28) * VMEM_MB),
    )(*(_in_hbm([a, b]) if tk == L else (a, b)))
    return outs[0] if carry is None else outs


LANE = 128


def _tn_blocks(name, a, bs, ga, gb, carry=None):
    L, m, n, nb = a.shape[0], a.shape[1], bs[0].shape[1], len(bs)
    per = LANE // ga
    wb = per * gb
    n_super = m // LANE

    def body(a_ref, *refs):
        b_refs, o_refs, acc_refs = refs[:nb], refs[nb:2 * nb], refs[2 * nb:]
        t = pl.program_id(0)

        @pl.when(t == 0)
        def _():
            for acc in acc_refs:
                acc[...] = jnp.zeros_like(acc)

        lhs = a_ref[...].astype(BF)
        for b_ref, acc in zip(b_refs, acc_refs):
            rhs = b_ref[...].astype(BF)
            for j in range(n_super):
                acc[j] += _mm_tn(lhs[:, j * LANE:(j + 1) * LANE], rhs[:, j * wb:(j + 1) * wb])

        @pl.when(t == L // TK - 1)
        def _():
            own = (lax.broadcasted_iota(jnp.int32, (LANE, wb), 0) // ga) == (lax.broadcasted_iota(jnp.int32, (LANE, wb), 1) // gb)
            for o_ref, acc in zip(o_refs, acc_refs):
                for j in range(n_super):
                    kept = jnp.where(own, acc[j], 0.0)
                    o_ref[:, j * wb:(j + 1) * wb] = jnp.sum(kept.reshape(per, ga, wb), axis=0)

    outs = _pallas_call(
        body, carry, name=name, grid=(L // TK,),
        in_specs=[pl.BlockSpec((TK, m), lambda t: (t, 0))] + [pl.BlockSpec((TK, n), lambda t: (t, 0))] * nb,
        out_specs=[_full((ga, n))] * nb, out_shape=[_sds((ga, n))] * nb,
        scratch_shapes=[pltpu.VMEM((n_super, LANE, wb), F32)] * nb,
        compiler_params=_params(48),
    )(*_in_hbm([a] + list(bs)))
    return list(outs)


def _s5_discretize(lam_re, lam_im, log_dt, b_re, b_im):
    dt = jnp.exp(log_dt)[:, None]
    mag = jnp.exp(lam_re * dt)
    ar = mag * jnp.cos(lam_im * dt)
    ai = mag * jnp.sin(lam_im * dt)
    den = lam_re * lam_re + lam_im * lam_im
    nr = ar - 1.0
    fr = (nr * lam_re + ai * lam_im) / den
    fi = (ai * lam_re - nr * lam_im) / den
    bbr = fr[:, None, :] * b_re - fi[:, None, :] * b_im
    bbi = fr[:, None, :] * b_im + fi[:, None, :] * b_re
    return ar, ai, bbr, bbi


def _prepare(by_rows, block_cols, ar, ai):
    n = len(by_rows)

    def body(*refs):
        srcs, (ar_ref, ai_ref), dense, (con_ref, rev_ref) = refs[:n], refs[n:n + 2], refs[n + 2:2 * n + 2], refs[2 * n + 2:]
        for src, out, c in zip(srcs, dense, block_cols):
            r = src.shape[0]
            per = LANE // r
            wide = per * c
            own = (lax.broadcasted_iota(jnp.int32, (LANE, wide), 0) // r) == (lax.broadcasted_iota(jnp.int32, (LANE, wide), 1) // c)
            for j in range(out.shape[0]):
                tiled = jnp.broadcast_to(src[:, j * wide:(j + 1) * wide][None], (per, r, wide)).reshape(LANE, wide)
                out[j] = jnp.where(own, tiled, 0.0).astype(BF)
        a_r, a_i = ar_ref[...], ai_ref[...]
        pw = [(jnp.ones_like(a_r), jnp.zeros_like(a_i))]
        for _ in range(SUB):
            pr, pi = pw[-1]
            pw.append((pr * a_r - pi * a_i, pr * a_i + pi * a_r))
        row = _row_iota(GN)
        for ref, reverse in ((con_ref, False), (rev_ref, True)):
            sign = -1.0 if reverse else 1.0
            for j, sh in enumerate((1, 2, 4)):
                keep = (row < SUB - sh) if reverse else (row >= sh)
                ref[2 * j * SUB:(2 * j + 1) * SUB, :] = jnp.where(keep, pw[sh][0], 0.0)
                ref[(2 * j + 1) * SUB:(2 * j + 2) * SUB, :] = jnp.where(keep, sign * pw[sh][1], 0.0)
            p_r, p_i = jnp.zeros((SUB, GN), F32), jnp.zeros((SUB, GN), F32)
            for i in range(SUB):
                k = SUB - i if reverse else i + 1
                p_r = jnp.where(row == i, pw[k][0], p_r)
                p_i = jnp.where(row == i, sign * pw[k][1], p_i)
            ref[6 * SUB:7 * SUB, :] = p_r
            ref[7 * SUB:8 * SUB, :] = p_i

    dense_shapes = [(b.shape[1] // (LANE // b.shape[0] * c), LANE, LANE // b.shape[0] * c)
                    for b, c in zip(by_rows, block_cols)]
    outs = _pallas_call(
        body, name="prepare", grid=(1,), in_specs=[_full(b.shape) for b in by_rows] + [_full((1, GN))] * 2,
        out_specs=[_full(s) for s in dense_shapes] + [_full((8 * SUB, GN))] * 2,
        out_shape=[_far(s, BF) for s in dense_shapes] + [_sds((8 * SUB, GN)), _far((8 * SUB, GN))],
        compiler_params=_params(48),
    )(*by_rows, ar, ai)
    return outs[:n], outs[n], outs[n + 1]


def _local_step(x, p, tgt, w, comm):
    rows_of = lambda a: a.reshape(NCHIP * a.shape[1], a.shape[2])
    quarters = lambda a: a.reshape(NCHIP, a.shape[0] // NCHIP, a.shape[1])

    def gathering(names, call):
        carry = comm.gather(names)
        outs = list(call(carry))
        own = len(outs) - len(carry.out_shapes)
        w.update(zip(names, outs[own:]))
        return outs[:own]

    w.update(comm.first())
    ar, ai, bbr, bbi = _s5_discretize(w["lam_re"], w["lam_im"], w["log_dt"], w["s5_b_re"], w["s5_b_im"])
    by_row = lambda b: jnp.transpose(b, (1, 0, 2)).reshape(b.shape[1], -1)
    (bbr_d, bbi_d, ccr_d, cci_d, wr_d, wi_d), con, con_rev = _prepare(
        [by_row(b) for b in (bbr, bbi, w["s5_c_re"], w["s5_c_im"], w["w_r"], w["w_i"])], [NS] * 4 + [HD] * 2,
        ar.reshape(1, GN), ai.reshape(1, GN))
    dsk = w["s5_d"].reshape(1, S5W)
    lam = w["lru_lambda"].reshape(1, LW)
    sp = jax.nn.softplus(-lam)
    b_r, b_i = w["b_r"].reshape(1, LW), w["b_i"].reshape(1, LW)
    row = lambda name: w[name].reshape(1, -1)

    h, ua, ub, gp = gathering(["w_glu", "w_a_out", "w_b_out"], lambda carry: _inproj_fwd(
        x, row("g_mix"), w["w_in"], row("b_in"), carry))
    w_glu = rows_of(w["w_glu"])
    sr, si, y, zg, ya = gathering(["w_o", "w_ffn_gate"], lambda carry: _s5_fwd(
        ua, bbr_d, bbi_d, ccr_d, cci_d, dsk, con, w_glu, row("b_glu"), carry))
    xc, rg, ig, yb, hp = gathering(["w_ffn_up"], lambda carry: _lru_fwd(
        ub, w["conv_w"], row("conv_b"), wr_d, wi_d, b_r, b_i, sp, carry))
    w_b_out, w_o = rows_of(w["w_b_out"]), rows_of(w["w_o"])
    x1, pa, pb, merged = gathering(["w_ffn_down"], lambda carry: _merge_fwd(
        x, ya, yb, gp, w["w_a_out"], w_b_out, w_o, carry))
    x2, h2, gg, uu = gathering(["w_ple_gate", "w_ple"], lambda carry: _ffn_fwd(
        x1, row("g_ffn"), w["w_ffn_gate"], w["w_ffn_up"], w["w_ffn_down"], carry))
    w_pg = rows_of(w["w_ple_gate"])
    dx2, n2, dpre, de0, acc_p = _ple_loss(x2, p, tgt, row("g_ple_gate"), w_pg, row("b_ple_gate"),
                                          w["w_ple"], row("g_ple"), row("g_final"))
    comm.reduce("ple", {"w_ple_gate": quarters(_tn("dw_ple_gate", n2, dpre)),
                        "w_ple": _tn("dw_ple", p, de0, col_chunk=AC)})
    dx1, act, dgg, duu, acc_f = comm.run(lambda carry: _ffn_bwd(
        x1, dx2, gg, uu, row("g_ffn"), w["w_ffn_gate"], w["w_ffn_up"], w["w_ffn_down"], carry))
    comm.reduce("ffn_gate", {"w_ffn_gate": _tn("dw_ffn_gate", dgg, h2)})
    comm.reduce("w_o", {"w_o": quarters(_tn("dw_o", *_in_hbm([merged, dx1])))})
    comm.reduce("ffn_up", {"w_ffn_up": comm.run(lambda carry: _tn("dw_ffn_up", duu, h2, carry=carry))[0]})
    comm.reduce("ffn_down", {"w_ffn_down": comm.run(lambda carry: _tn("dw_ffn_down", act, dx2, carry=carry),
                                                    hold=("ffn_gate", "w_o"))[0]})
    dya, dyb, dgp, dpa, dpb = comm.run(lambda carry: _merge_bwd(
        dx1, gp, pa, pb, w["w_a_out"], w_b_out, w_o, carry), hold=("ffn_gate", "ffn_up"))
    comm.reduce("merge", {"w_a_out": _tn("dw_a_out", ya, dpa, col_chunk=AC), "w_b_out": quarters(_tn("dw_b_out", yb, dpb))})
    dua, dq, dy, lr, li, acc_a, acc_s = comm.run(lambda carry: _s5_bwd(
        dya, y, ua, sr, si, bbr_d, bbi_d, ccr_d, cci_d, dsk, con_rev, w_glu, row("b_glu"), carry), hold=("ffn_down",))
    dub, dpr, dpi, acc_l = comm.run(lambda carry: _lru_bwd(
        dyb, xc, rg, ig, hp, ub, w["conv_w"], wr_d, wi_d, sp, -_sig(-lam), carry))
    gx, dz, acc_g, acc_b = _inproj_bwd(x, dx1, dua, dub, dgp, row("g_mix"), w["w_in"])
    half = (D // 2,)
    comm.reduce("in_lo", {"w_in_lo": comm.run(lambda carry: _tn(
        "dw_in_lo", h, dz, col_chunk=QC, a_block=(0,) + half, carry=carry))[0]}, swap_now=True)
    comm.reduce("in_hi", {"w_in_hi": comm.run(lambda carry: _tn(
        "dw_in_hi", h, dz, col_chunk=QC, a_block=(1,) + half, carry=carry))[0], "w_glu": quarters(_tn("dw_glu", zg, dq))},
        swap_now=True)
    d_wr, d_wi = comm.run(lambda carry: _tn_blocks("dw_r_i", xc, [dpr, dpi], HD, HD, carry))
    d_bbr, d_bbi = comm.run(lambda carry: _tn_blocks("d_bb", ua, [lr, li], NP, NS, carry))
    d_ccr, d_cci = comm.run(lambda carry: _tn_blocks("d_cc", dy, [sr, si], NP, NS, carry))
    comm.drain()
    sums = {"ple": acc_p, "ffn": acc_f, "mix": acc_g, "b_in": acc_b, "lru": acc_l, "s5": acc_s, "s5_a": acc_a}
    blocks = {"bb_re": d_bbr, "bb_im": d_bbi,
              "cc_re": d_ccr, "cc_im": d_cci,
              "w_r": d_wr, "w_i": d_wi}
    return gx, sums, blocks


def _replicated_grads(w, sums, blocks):
    grouped = lambda e, groups: jnp.transpose(e.reshape(e.shape[0], groups, -1), (1, 0, 2))
    d_ar, d_ai = sums["s5_a"][0].reshape(NG, NS), sums["s5_a"][1].reshape(NG, NS)
    d_bbr, d_bbi = grouped(blocks["bb_re"], NG), grouped(blocks["bb_im"], NG)
    _, vjp = jax.vjp(_s5_discretize, w["lam_re"], w["lam_im"], w["log_dt"], w["s5_b_re"], w["s5_b_im"])
    g = dict(zip(("lam_re", "lam_im", "log_dt", "s5_b_re", "s5_b_im"), vjp((d_ar, d_ai, d_bbr, d_bbi))))
    g["s5_c_re"] = grouped(blocks["cc_re"], NG)
    g["s5_c_im"] = -grouped(blocks["cc_im"], NG)
    g["w_r"], g["w_i"] = grouped(blocks["w_r"], NH), grouped(blocks["w_i"], NH)
    g["s5_d"] = sums["s5"][0].reshape(NG, NP)
    g["b_r"] = sums["lru"][1].reshape(NH, HD)
    g["b_i"] = sums["lru"][2].reshape(NH, HD)
    return g


ACC_ROWS = {"g_mix": ("mix", 0), "b_in": ("b_in", 0), "g_ffn": ("ffn", 0), "g_ple_gate": ("ple", 0),
            "b_ple_gate": ("ple", 1), "g_ple": ("ple", 2), "g_final": ("ple", 3), "b_glu": ("s5", 1),
            "lru_lambda": ("lru", 0), "conv_b": ("lru", 3)}
LOSS_ROW = ("ple", 4)
CONV_W_ROWS = ("lru", 4)


SHARDED = [("w_in", (D, QC)), ("w_glu", (S5W // NCHIP, S5W)), ("w_a_out", (S5W, AC)), ("w_b_out", (LW // NCHIP, D)),
           ("w_o", (D // NCHIP, D)), ("w_ffn_gate", (FC, D)), ("w_ffn_up", (FC, D)), ("w_ffn_down", (FC, D)),
           ("w_ple_gate", (D // NCHIP, D)), ("w_ple", (PLE, AC))]
TRANSPOSED = ("w_ffn_gate", "w_ffn_up", "s5_b_re", "s5_b_im")
CONV_SHARD = (4, LW // NCHIP)


def _mesh_pos():
    return lax.axis_index("x"), lax.axis_index("y"), lax.axis_index("c")


def _other_chips(x, y):
    return [(1 - x, y), (x, 1 - y), (1 - x, 1 - y)]


def _half_rows(c, rows, align):
    return pl.ds(pl.multiple_of(c * (rows // 2), align), rows // 2)


def _run_now(name, carry):
    c_in, c_out = len(carry.operands), len(carry.out_shapes)

    def body(*refs):
        ins, outs, sems = refs[:c_in], refs[c_in:c_in + c_out], refs[c_in + c_out:]
        carry.start(ins, outs, sems)
        carry.finish(ins, outs, sems)

    return pl.pallas_call(body, name=name, in_specs=[ANY] * c_in, out_specs=[ANY] * c_out,
                          out_shape=list(carry.out_shapes), scratch_shapes=list(carry.sems),
                          input_output_aliases=dict(carry.aliases))(*_in_hbm(carry.operands))


def _gather_group(shards, split):
    n = len(shards)

    def copies(srcs, outs, sems):
        send_sems, recv_sems = sems
        x, y, c = _mesh_pos()
        k0 = 2 * x + y
        sib = (x, y, 1 - c)
        chips = _other_chips(x, y)

        def remote(src, dst, j, i, to):
            return pltpu.make_async_remote_copy(src_ref=src, dst_ref=dst, send_sem=send_sems.at[j, i],
                                                recv_sem=recv_sems.at[j, i], device_id=to, device_id_type=MESH)

        def rows(ref, i, core, *lead):
            if not split[i]:
                return ref.at[lead] if lead else ref
            return ref.at[(*lead, _half_rows(core, shards[i].shape[0], 16))]

        own = [remote(s, o.at[k0], 6, i, sib) for i, (s, o) in enumerate(zip(srcs, outs))]
        ici, landed, fwd, fwd_landed = [], [], [], []
        for j, chip in enumerate(chips):
            kj = 2 * chip[0] + chip[1]
            pairs = list(enumerate(zip(srcs, outs)))
            ici.append([remote(rows(s, i, c), rows(o, i, c, k0), j, i, (*chip, c)) for i, (s, o) in pairs])
            landed.append([remote(rows(s, i, c), rows(o, i, c, kj), j, i, (*chip, c)) for i, (s, o) in pairs])
            fwd.append([remote(rows(o, i, c, kj), rows(o, i, c, kj), 3 + j, i, sib) for i, (s, o) in pairs if split[i]])
            fwd_landed.append([remote(rows(o, i, 1 - c, kj), rows(o, i, 1 - c, kj), 3 + j, i, sib)
                               for i, (s, o) in pairs if split[i]])
        return own, ici, landed, fwd, fwd_landed

    def start(srcs, outs, sems):
        own, ici, _, _, _ = copies(srcs, outs, sems)
        for cp in own + [cp for per_chip in ici for cp in per_chip]:
            cp.start()

    def finish(srcs, outs, sems):
        own, ici, landed, fwd, fwd_landed = copies(srcs, outs, sems)
        passed = [i for i in range(n) if split[i]]
        for j in range(3):
            for i, cp in enumerate(landed[j]):
                cp.wait_recv()
                if split[i]:
                    fwd[j][passed.index(i)].start()
        for j in range(3):
            for cp in fwd_landed[j]:
                cp.wait_recv()
        for cp in own:
            cp.wait_recv()
        for cp in own + [cp for per_chip in ici + fwd for cp in per_chip]:
            cp.wait_send()

    return _Carried(shards, [_far((NCHIP,) + s.shape, s.dtype) for s in shards],
                    [pltpu.SemaphoreType.DMA((7, n)), pltpu.SemaphoreType.DMA((7, n))], start, finish)


def _to_bf16_group(name, arrays, carry):
    n = len(arrays)

    def body(*refs):
        for src, dst in zip(refs[:n], refs[n:]):
            dst[...] = src[...].astype(BF)

    specs = [pl.BlockSpec((a.shape[0] // 2, a.shape[1]), lambda i: (i, 0)) for a in arrays]
    return _pallas_call(body, carry, name=name, grid=(2,), in_specs=specs, out_specs=specs,
                        out_shape=[_far(a.shape, BF) for a in arrays], compiler_params=_params(48))(*arrays)


def _each_copy(copies, carried, out_shapes, sems, aliases=None):
    def start(ins, outs, sem_refs):
        for cp in copies(ins, outs, sem_refs):
            cp.start()

    def finish(ins, outs, sem_refs):
        for cp in copies(ins, outs, sem_refs):
            cp.wait()

    return _Carried(carried, out_shapes, sems, start, finish, aliases)


def _swap_group(grads):
    n = len(grads)

    def copies(srcs, outs, sems):
        send_sems, recv_sems = sems
        x, y, c = _mesh_pos()
        return [pltpu.make_async_remote_copy(src_ref=s.at[:, _half_rows(1 - c, s.shape[1], 8)], dst_ref=o,
                                             send_sem=send_sems.at[i], recv_sem=recv_sems.at[i], device_id=(x, y, 1 - c),
                                             device_id_type=MESH) for i, (s, o) in enumerate(zip(srcs, outs))]

    return _each_copy(copies, grads, [pltpu.HBM((NCHIP, g.shape[1] // 2, g.shape[2]), F32) for g in grads],
                      [pltpu.SemaphoreType.DMA((n,)), pltpu.SemaphoreType.DMA((n,))])


def _add_sibling_group(tag, kc_idx, grads, gots):
    n = len(grads)

    def body(kc_ref, *refs):
        for g, rx, p, pb in zip(refs[:n], refs[n:2 * n], refs[2 * n:3 * n], refs[3 * n:]):
            s = g[...] + rx[...]
            pb[...] = s.astype(BF)

            @pl.when(pl.program_id(0) == kc_ref[0])
            def _():
                p[...] = s

    halves = [pl.BlockSpec((None,) + rx.shape[1:], lambda k, kc_ref: (k, 0, 0)) for rx in gots]
    mine = [pl.BlockSpec((None,) + rx.shape[1:], lambda k, kc_ref: (k, kc_ref[1], 0)) for rx in gots]
    own = [pl.BlockSpec(rx.shape[1:], lambda k, kc_ref: (0, 0)) for rx in gots]
    outs = _pallas_call(
        body, name="add_sibling_" + tag,
        grid_spec=pltpu.PrefetchScalarGridSpec(num_scalar_prefetch=1, grid=(NCHIP,), in_specs=mine + halves,
                                               out_specs=own + halves),
        out_shape=[pltpu.HBM(rx.shape[1:], F32) for rx in gots] + [pltpu.HBM(rx.shape, BF) for rx in gots],
        compiler_params=_params(48),
    )(kc_idx, *_in_hbm(list(grads) + list(gots)))
    return outs[:n], outs[n:]


def _exchange_group(parts):
    n = len(parts)

    def copies(srcs, outs, sems):
        send_sems, recv_sems = sems
        x, y, c = _mesh_pos()
        return [pltpu.make_async_remote_copy(
            src_ref=s.at[2 * chip[0] + chip[1]], dst_ref=o.at[j], send_sem=send_sems.at[j, i],
            recv_sem=recv_sems.at[j, i], device_id=(*chip, c), device_id_type=MESH)
            for j, chip in enumerate(_other_chips(x, y)) for i, (s, o) in enumerate(zip(srcs, outs))]

    return _each_copy(copies, parts, [pltpu.HBM((3,) + p.shape[1:], BF) for p in parts],
                      [pltpu.SemaphoreType.DMA((3, n)), pltpu.SemaphoreType.DMA((3, n))])


def _add_chips_group(tag, kc_idx, parts, arrived):
    n = len(parts)

    def body(kc_ref, *refs):
        for p, rx, t in zip(refs[:n], refs[n:2 * n], refs[2 * n:]):
            t[...] = ((p[...] + rx[0].astype(F32)) + rx[1].astype(F32)) + rx[2].astype(F32)

    outs = _pallas_call(
        body, name="add_chips_" + tag,
        grid_spec=pltpu.PrefetchScalarGridSpec(
            num_scalar_prefetch=1, grid=(1,),
            in_specs=([pl.BlockSpec(rx.shape[1:], lambda i, kc_ref: (0, 0)) for rx in arrived]
                      + [pl.BlockSpec(rx.shape, lambda i, kc_ref: (0, 0, 0)) for rx in arrived]),
            out_specs=[pl.BlockSpec((None,) + rx.shape[1:], lambda i, kc_ref: (kc_ref[1], 0, 0)) for rx in arrived]),
        out_shape=[pltpu.HBM((2,) + rx.shape[1:], F32) for rx in arrived],
        compiler_params=_params(48),
    )(kc_idx, *_in_hbm(list(parts) + list(arrived)))
    return list(outs)


def _join_group(halves):
    n = len(halves)

    def copies(bufs, sems):
        send_sems, recv_sems = sems
        x, y, c = _mesh_pos()
        sib = (x, y, 1 - c)
        sends = [pltpu.make_async_remote_copy(src_ref=b.at[c], dst_ref=b.at[c], send_sem=send_sems.at[i],
                                              recv_sem=recv_sems.at[i], device_id=sib, device_id_type=MESH)
                 for i, b in enumerate(bufs)]
        landed = [pltpu.make_async_remote_copy(src_ref=b.at[c], dst_ref=b.at[1 - c], send_sem=send_sems.at[i],
                                               recv_sem=recv_sems.at[i], device_id=sib, device_id_type=MESH)
                  for i, b in enumerate(bufs)]
        return sends, landed

    def start(_, bufs, sems):
        for cp in copies(bufs, sems)[0]:
            cp.start()

    def finish(_, bufs, sems):
        sends, landed = copies(bufs, sems)
        for cp in landed:
            cp.wait_recv()
        for cp in sends:
            cp.wait_send()

    return _Carried(halves, [pltpu.HBM(h.shape, F32) for h in halves],
                    [pltpu.SemaphoreType.DMA((n,)), pltpu.SemaphoreType.DMA((n,))], start, finish,
                    {i: i for i in range(n)})


def _combine(carries):
    operands, out_shapes, sems, aliases, spans = [], [], [], {}, []
    for c in carries:
        aliases.update({len(operands) + i: len(out_shapes) + o for i, o in c.aliases.items()})
        spans.append((len(operands), len(out_shapes), len(sems)))
        operands += list(c.operands)
        out_shapes += list(c.out_shapes)
        sems += list(c.sems)

    def each(phase):
        def run(ins, outs, sem_refs):
            for c, (a, b, s) in zip(carries, spans):
                getattr(c, phase)(ins[a:a + len(c.operands)], outs[b:b + len(c.out_shapes)], sem_refs[s:s + len(c.sems)])
        return run

    return _Carried(operands, out_shapes, sems, each("start"), each("finish"), aliases)


def _allreduce_small(arrays, wire):
    n = len(arrays)
    halves = [(a.shape[0], a.shape[1] // 2) for a in arrays]

    def body(*refs):
        srcs, outs = refs[:n], refs[n:2 * n]
        mine_bufs, sib_bufs, chip_bufs, total_bufs = (refs[k * n:(k + 1) * n] for k in range(2, 6))
        send_sems, recv_sems, local_sems = refs[6 * n:]
        x, y, c = _mesh_pos()
        k0 = 2 * x + y
        sib = (x, y, 1 - c)

        def remote(src, dst, j, i, to):
            return pltpu.make_async_remote_copy(src_ref=src, dst_ref=dst, send_sem=send_sems.at[j, i],
                                                recv_sem=recv_sems.at[j, i], device_id=to, device_id_type=MESH)

        def cols(ref, i, core):
            return ref.at[:, pl.ds(pl.multiple_of(core * halves[i][1], LANE), halves[i][1])]

        swaps = [remote(cols(s, i, 1 - c), b, 0, i, sib) for i, (s, b) in enumerate(zip(srcs, sib_bufs))]
        own = [pltpu.make_async_copy(cols(s, i, c), m, local_sems.at[i]) for i, (s, m) in enumerate(zip(srcs, mine_bufs))]
        for cp in swaps + own:
            cp.start()
        for cp in swaps + own:
            cp.wait()
        for m, b, buf in zip(mine_bufs, sib_bufs, chip_bufs):
            buf[k0] = (m[...] + b[...]).astype(buf.dtype)
        chips = _other_chips(x, y)
        sends = [remote(buf.at[k0], buf.at[k0], 1 + j, i, (*chip, c))
                 for j, chip in enumerate(chips) for i, buf in enumerate(chip_bufs)]
        for cp in sends:
            cp.start()
        for j, chip in enumerate(chips):
            for i, buf in enumerate(chip_bufs):
                remote(buf.at[k0], buf.at[2 * chip[0] + chip[1]], 1 + j, i, (*chip, c)).wait_recv()
        for cp in sends:
            cp.wait_send()
        for t, buf in zip(total_bufs, chip_bufs):
            t[...] = ((buf[0].astype(F32) + buf[1].astype(F32)) + buf[2].astype(F32)) + buf[3].astype(F32)
        joins = [remote(t, cols(o, i, c), 4, i, sib) for i, (t, o) in enumerate(zip(total_bufs, outs))]
        keep = [pltpu.make_async_copy(t, cols(o, i, c), local_sems.at[i]) for i, (t, o) in enumerate(zip(total_bufs, outs))]
        for cp in joins + keep:
            cp.start()
        for i, (t, o) in enumerate(zip(total_bufs, outs)):
            remote(t, cols(o, i, 1 - c), 4, i, sib).wait_recv()
        for cp in joins:
            cp.wait_send()
        for cp in keep:
            cp.wait()

    specs = [_full(a.shape) for a in arrays]
    return _pallas_call(
        body, name="allreduce_small", grid=(1,), in_specs=specs, out_specs=specs,
        out_shape=[_sds(a.shape) for a in arrays],
        scratch_shapes=([pltpu.VMEM(h, F32) for h in halves] + [pltpu.VMEM(h, F32) for h in halves]
                        + [pltpu.VMEM((NCHIP,) + h, dt) for h, dt in zip(halves, wire)] + [pltpu.VMEM(h, F32) for h in halves]
                        + [pltpu.SemaphoreType.DMA((5, n)), pltpu.SemaphoreType.DMA((5, n)), pltpu.SemaphoreType.DMA((n,))]),
        compiler_params=_params(32),
    )(*arrays)


def _adamw_terms(w, g, m, v):
    m = ADAM_B1 * m + (1.0 - ADAM_B1) * g
    v = ADAM_B2 * v + (1.0 - ADAM_B2) * jnp.square(g)
    m_hat = m / (1.0 - ADAM_B1 ** ADAM_STEP)
    v_hat = v / (1.0 - ADAM_B2 ** ADAM_STEP)
    return -ADAM_LR * (m_hat / (jnp.sqrt(v_hat) + ADAM_EPS) + ADAM_WD * w), m, v


ADAM_STEPS = 4


def _adamw_group(tag, ws, gs, ms, vs):
    n = len(ws)

    def body(*refs):
        ins, outs = refs[:4 * n], refs[4 * n:]
        for i in range(n):
            w, g, m, v = (ins[k * n + i][...] for k in range(4))
            outs[i][...] = g
            outs[n + i][...], outs[2 * n + i][...], outs[3 * n + i][...] = _adamw_terms(w, g, m, v)

    specs = [pl.BlockSpec((w.shape[0] // ADAM_STEPS, w.shape[1]), lambda i: (i, 0)) for w in ws]
    outs = _pallas_call(
        body, name="adamw_" + tag, grid=(ADAM_STEPS,), in_specs=specs * 4, out_specs=specs * 4,
        out_shape=[_sds(w.shape) for w in ws] * 4, compiler_params=_params(48),
    )(*_in_hbm(list(ws) + list(gs) + list(ms) + list(vs)))
    return outs[:n], outs[n:2 * n], outs[2 * n:3 * n], outs[3 * n:]


def _adamw_replicated(sums, row_of, direct):
    ns, nr, nd = len(sums), len(row_of), len(direct)

    def body(*refs):
        sum_refs = refs[:ns]
        ins = refs[ns:ns + 3 * nr + 4 * nd]
        outs = refs[ns + 3 * nr + 4 * nd:]
        for i, (_, _, _, si, row) in enumerate(row_of):
            w_ref, m_ref, v_ref = ins[3 * i:3 * i + 3]
            g = sum_refs[si][row:row + 1, :]
            outs[4 * i][...] = g
            outs[4 * i + 1][...], outs[4 * i + 2][...], outs[4 * i + 3][...] = _adamw_terms(w_ref[...], g, m_ref[...], v_ref[...])
        for i in range(nd):
            w_ref, m_ref, v_ref, g_ref = ins[3 * nr + 4 * i:3 * nr + 4 * i + 4]
            o = outs[4 * (nr + i):4 * (nr + i) + 4]
            g = g_ref[...]
            o[0][...] = g
            o[1][...], o[2][...], o[3][...] = _adamw_terms(w_ref[...], g, m_ref[...], v_ref[...])

    operands = list(sums)
    shapes = []
    for w, m, v, _, _ in row_of:
        operands += [w, m, v]
        shapes += [w.shape] * 4
    for w, m, v, g in direct:
        operands += [w, m, v, g]
        shapes += [w.shape] * 4
    flat = _pallas_call(
        body, name="adamw_replicated", grid=(1,), in_specs=[_full(a.shape) for a in operands],
        out_specs=[_full(s) for s in shapes], out_shape=[_sds(s) for s in shapes],
        compiler_params=_params(56),
    )(*operands)
    return [flat[4 * i:4 * i + 4] for i in range(nr + nd)]


class _Exchanges:
    def __init__(self, shards, conv_w, chip, core, apply):
        self.shards, self.conv_w, self.apply = shards, conv_w, apply
        self.active, self.calls = [], 0
        self.chip_core_idx = jnp.stack([chip, core]).astype(jnp.int32)

    def first(self):
        later = [n for n in self.shards if n != "w_in"]
        carry = _gather_group([self.shards["w_in"].astype(BF), self.conv_w], [True, False])
        outs = _to_bf16_group("gather_first", [self.shards[n] for n in later], carry)
        self.shards = dict(zip(later, outs))
        return {"w_in": outs[len(later)], "conv_w": jnp.transpose(outs[len(later) + 1], (1, 0, 2)).reshape(4, LW)}

    def gather(self, names):
        return _gather_group([self.shards[n] for n in names], [True] * len(names))

    def reduce(self, tag, grads, swap_now=False):
        group = {"tag": tag, "names": list(grads), "stage": 0, "grads": list(grads.values())}
        self.active.append(group)
        if swap_now:
            self._sum_after(group, _run_now("swap_" + tag, self._exchange_of(group)))

    def run(self, call, hold=()):
        groups = [g for g in self.active if g["tag"] not in hold]
        carries = [self._exchange_of(g) for g in groups]
        carry = _combine(carries)
        outs = list(call(carry))
        own = len(outs) - len(carry.out_shapes)
        landed = outs[own:]
        for g, c in zip(groups, carries):
            self._sum_after(g, landed[:len(c.out_shapes)])
            landed = landed[len(c.out_shapes):]
        self.active = [g for g in self.active if g["stage"] < 3]
        return outs[:own]

    def _exchange_of(self, g):
        if g["stage"] == 0:
            return _swap_group(g["grads"])
        if g["stage"] == 1:
            return _exchange_group(g["bf16"])
        return _join_group(g["halves"])

    def _sum_after(self, g, landed):
        if g["stage"] == 0:
            g["f32"], g["bf16"] = _add_sibling_group(g["tag"], self.chip_core_idx, g["grads"], landed)
        elif g["stage"] == 1:
            g["halves"] = _add_chips_group(g["tag"], self.chip_core_idx, g["f32"], landed)
        else:
            self.apply(g["tag"], g["names"], [t.reshape(2 * t.shape[1], t.shape[2]) for t in landed])
        g["stage"] += 1

    def drain(self):
        while self.active:
            self.calls += 1
            self.run(lambda carry: _run_now("reduce_%d" % self.calls, carry))


INPUT_NAMES = (["x", "p"] + [n for n in
               ["g_mix", "w_in", "b_in", "lam_re", "lam_im", "log_dt", "s5_b_re", "s5_b_im", "s5_c_re", "s5_c_im", "s5_d",
                "w_glu", "b_glu", "conv_w", "conv_b", "w_r", "b_r", "w_i", "b_i", "lru_lambda", "w_a_out", "w_b_out", "w_o",
                "g_ffn", "w_ffn_gate", "w_ffn_up", "w_ffn_down", "g_ple_gate", "w_ple_gate", "b_ple_gate", "w_ple", "g_ple",
                "g_final"]])
WEIGHT_NAMES = INPUT_NAMES[2:]


def kernel(*args):
    names = INPUT_NAMES + ["loss_target"] + ["m_" + n for n in WEIGHT_NAMES] + ["v_" + n for n in WEIGHT_NAMES]
    assert len(args) == len(names)
    given = dict(zip(names, args))

    def view(name):
        a = given[name]
        return jnp.swapaxes(a, -1, -2) if name.endswith(TRANSPOSED) else a

    def unview(name, a):
        return jnp.swapaxes(a, -1, -2) if name in TRANSPOSED else a

    def local(name):
        return view(name) if name.endswith("g_final") else view(name)[0]

    xi, yi, ci = _mesh_pos()
    k0 = 2 * xi + yi
    x, p, tgt = given["x"][0], given["p"][0, 0], given["loss_target"][0]

    results = {}

    row_halves = {}

    def apply(tag, names, totals):
        totals = dict(zip(names, totals))
        row_halves.update({n: totals.pop(n) for n in names if n in ("w_in_lo", "w_in_hi")})
        if len(row_halves) == 2:
            totals["w_in"] = jnp.concatenate([row_halves.pop("w_in_lo"), row_halves.pop("w_in_hi")])
        names = list(totals)
        if not names:
            return
        new = _adamw_group(tag, [local(n) for n in names], list(totals.values()), [local("m_" + n) for n in names],
                           [local("v_" + n) for n in names])
        for kind, arrays in zip(("grad", "delta", "new_m", "new_v"), new):
            for n, arr in zip(names, arrays):
                results[kind, n] = unview(n, arr[None])

    comm = _Exchanges({n: local(n) for n, _ in SHARDED}, local("conv_w"), k0, ci, apply)
    w = {n: local(n) for n in WEIGHT_NAMES if n != "conv_w" and n not in dict(SHARDED)}
    gx, sums, blocks = _local_step(x, p, tgt, w, comm)

    sum_names, block_names = list(sums), list(blocks)
    red = _allreduce_small([sums[n] for n in sum_names] + [blocks[n] for n in block_names],
                           [F32] * len(sum_names) + [BF] * len(block_names))
    sums = dict(zip(sum_names, red[:len(sum_names)]))
    blocks = dict(zip(block_names, red[len(sum_names):]))
    loss = jnp.sum(sums[LOSS_ROW[0]][LOSS_ROW[1]])
    direct_g = _replicated_grads(w, sums, blocks)
    conv_rows = sums[CONV_W_ROWS[0]][CONV_W_ROWS[1]:CONV_W_ROWS[1] + 4]
    direct_g["conv_w"] = lax.dynamic_slice(conv_rows, (0, k0 * CONV_SHARD[1]), CONV_SHARD)
    as_row = lambda a: a.reshape(1, -1)
    row_names = list(ACC_ROWS)
    row_of = [(as_row(given[n]), as_row(given["m_" + n]), as_row(given["v_" + n]),
               sum_names.index(ACC_ROWS[n][0]), ACC_ROWS[n][1]) for n in row_names]
    direct_names = list(direct_g)
    direct = [(view(n), view("m_" + n), view("v_" + n), direct_g[n].reshape(view(n).shape)) for n in direct_names]
    done = _adamw_replicated([sums[n] for n in sum_names], row_of, direct)
    for n, four in zip(row_names + direct_names, done):
        for kind, arr in zip(("grad", "delta", "new_m", "new_v"), four):
            results[kind, n] = unview(n, arr).reshape(given[n].shape)

    out = [loss, gx[None]]
    for kind in ("grad", "delta", "new_m", "new_v"):
        out += [results[kind, n] for n in WEIGHT_NAMES]
    return tuple(out)
```

```python
import functools
import math

import jax
import jax.numpy as jnp
from jax import lax
from jax.experimental import pallas as pl
from jax.experimental.pallas import tpu as pltpu

F32 = jnp.float32
BF = jnp.bfloat16

D = 1024
S5W = 512
NG, NS, NP = 32, 64, 16
GN = NG * NS
LW = 1024
NH, HD = 16, 64
LRU_C = 8.0
FH = 2816
NCHIP = 4
FC = FH // NCHIP
PLE = 256
INC = S5W + LW + 2 * D
EPS = 1e-6
ADAM_LR, ADAM_B1, ADAM_B2, ADAM_EPS, ADAM_WD, ADAM_STEP = 0.001, 0.9, 0.999, 1e-08, 0.01, 10

TM = 256
TK = 1024
LC = 512
SUB = 8
VMEM_MB = 1024 * 1024
MESH = pl.DeviceIdType.MESH
ANY = pl.BlockSpec(memory_space=pl.ANY)


def _mm(a, b):
    return jnp.dot(a.astype(BF), b.astype(BF), preferred_element_type=F32)


def _mm_nt(a, b):
    return lax.dot_general(a.astype(BF), b.astype(BF), (((1,), (1,)), ((), ())), preferred_element_type=F32)


def _mm_tn(a, b):
    return lax.dot_general(a.astype(BF), b.astype(BF), (((0,), (0,)), ((), ())), preferred_element_type=F32)


def _blockdiag_mm(x, blocks_ref):
    n, rows, _ = blocks_ref.shape
    return jnp.concatenate([jnp.dot(x[:, j * rows:(j + 1) * rows], blocks_ref[j], preferred_element_type=F32)
                            for j in range(n)], axis=1)


def _blockdiag_mm_t(x, blocks_ref):
    n, _, wide = blocks_ref.shape
    return jnp.concatenate([lax.dot_general(x[:, j * wide:(j + 1) * wide], blocks_ref[j], (((1,), (1,)), ((), ())),
                                            preferred_element_type=F32) for j in range(n)], axis=1)


def _rms(x):
    r = lax.rsqrt(jnp.mean(x * x, axis=-1, keepdims=True) + EPS)
    return x * r, r


def _rms_bwd(dy, xh, r, g):
    dxh = dy * g
    return r * (dxh - xh * jnp.mean(dxh * xh, axis=-1, keepdims=True))


def _colsum(x):
    return jnp.sum(x, axis=0, keepdims=True)


def _sig(x):
    return jax.nn.sigmoid(x)


def _gelu_grad(x):
    c = math.sqrt(2.0 / math.pi)
    t = jnp.tanh(c * (x + 0.044715 * x * x * x))
    return 0.5 * (1.0 + t) + 0.5 * x * (1.0 - t * t) * c * (1.0 + 3.0 * 0.044715 * x * x)


def _neg_expm1(x):
    series = -x * (1.0 + x * (0.5 + x * (1.0 / 6.0 + x * (1.0 / 24.0))))
    return jnp.where(x > -0.03, series, 1.0 - jnp.exp(x))


def _tok(width):
    return pl.BlockSpec((TM, width), lambda i: (i, 0))


def _tok_rev(width, nt):
    return pl.BlockSpec((TM, width), lambda i: (nt - 1 - i, 0))


def _full(shape):
    return pl.BlockSpec(shape, lambda i: (0,) * len(shape))


def _params(vmem_mb, **kw):
    return pltpu.CompilerParams(dimension_semantics=("arbitrary",), vmem_limit_bytes=vmem_mb * VMEM_MB, **kw)


def _sds(shape, dtype=F32):
    return jax.ShapeDtypeStruct(shape, dtype)


def _far(shape, dtype=F32):
    return pltpu.HBM(shape, dtype)


class _Carried:
    def __init__(self, operands, out_shapes, sems, start, finish, aliases=None):
        self.operands, self.out_shapes, self.sems = list(operands), list(out_shapes), list(sems)
        self.start, self.finish, self.aliases = start, finish, dict(aliases or {})


def _in_hbm(arrays):
    return [pltpu.with_memory_space_constraint(a, pltpu.HBM) for a in arrays]


def _pallas_call(body, carry=None, **kw):
    if carry is None:
        return pl.pallas_call(body, **kw)

    def at_step(corner):
        hit = [pl.program_id(d) == (size - 1 if corner else 0) for d, size in enumerate(kw["grid"])]
        return functools.reduce(jnp.logical_and, hit)

    name, grid, compiler_params = kw["name"], kw["grid"], kw["compiler_params"]
    in_specs, out_specs, out_shape = list(kw["in_specs"]), list(kw["out_specs"]), list(kw["out_shape"])
    scratch_shapes = list(kw.get("scratch_shapes", ()))
    n_in, n_out, n_scr = len(in_specs), len(out_specs), len(scratch_shapes)
    c_in, c_out = len(carry.operands), len(carry.out_shapes)

    def full_body(*refs):
        ins, refs = refs[:n_in], refs[n_in:]
        c_ins, refs = refs[:c_in], refs[c_in:]
        outs, refs = refs[:n_out], refs[n_out:]
        c_outs, refs = refs[:c_out], refs[c_out:]
        scratch, c_sems = refs[:n_scr], refs[n_scr:]

        @pl.when(at_step(0))
        def _():
            carry.start(c_ins, c_outs, c_sems)

        body(*ins, *outs, *scratch)

        @pl.when(at_step(1))
        def _():
            carry.finish(c_ins, c_outs, c_sems)

    call = pl.pallas_call(
        full_body, name=name, grid=grid, in_specs=in_specs + [ANY] * c_in, out_specs=out_specs + [ANY] * c_out,
        out_shape=out_shape + list(carry.out_shapes), scratch_shapes=scratch_shapes + list(carry.sems),
        input_output_aliases={n_in + i: n_out + o for i, o in carry.aliases.items()},
        compiler_params=compiler_params)
    return lambda *operands: call(*operands, *_in_hbm(carry.operands))


def _resident(pairs, sems):
    first = pl.program_id(0) == 0
    copies = [pltpu.make_async_copy(src, dst, sems.at[j]) for j, (src, dst) in enumerate(pairs)]

    @pl.when(first)
    def _():
        for cp in copies:
            cp.start()

    def wait(j):
        @pl.when(first)
        def _():
            copies[j].wait()

    return wait


def _resident_now(pairs, sems):
    @pl.when(pl.program_id(0) == 0)
    def _():
        copies = [pltpu.make_async_copy(src, dst, sems.at[j]) for j, (src, dst) in enumerate(pairs)]
        for cp in copies:
            cp.start()
        for cp in copies:
            cp.wait()


def _row_iota(width):
    return lax.broadcasted_iota(jnp.int32, (SUB, width), 0)


def _bcast_row(x, row):
    return jnp.broadcast_to(x[row:row + 1, :], x.shape)


def _slab(k):
    return pl.ds(pl.multiple_of(k * SUB, SUB), SUB)


QC = INC // NCHIP
Z_PARTS = ((0, S5W), (S5W, S5W + LW), (S5W + LW, INC))


def _inproj_fwd(x, g_mix, w_in, b_in, carry=None):
    L = x.shape[0]

    def body(x_ref, g_ref, w_hbm, b_ref, h_ref, ua_ref, ub_ref, gp_ref, w_vm, w_sems):
        _resident_now([(w_hbm.at[k], w_vm.at[k]) for k in range(NCHIP)], w_sems)
        xh, _ = _rms(x_ref[...])
        h = (xh * g_ref[...]).astype(BF)
        h_ref[...] = h
        for k in range(NCHIP):
            lo, hi = k * QC, (k + 1) * QC
            z = jnp.dot(h, w_vm[k], preferred_element_type=F32) + b_ref[:, lo:hi]
            for ref, (a, b) in zip((ua_ref, ub_ref, gp_ref), Z_PARTS):
                s, e = max(lo, a), min(hi, b)
                if s < e:
                    ref[:, s - a:e - a] = z[:, s - lo:e - lo]

    return _pallas_call(
        body, carry, name="inproj_fwd", grid=(L // TM,),
        in_specs=[_tok(D), _full((1, D)), ANY, _full((1, INC))],
        out_specs=[_tok(D), _tok(S5W), _tok(LW), _tok(2 * D)],
        out_shape=[_far((L, D), BF), _far((L, S5W)), _far((L, LW)), _sds((L, 2 * D))],
        scratch_shapes=[pltpu.VMEM((NCHIP, D, QC), BF), pltpu.SemaphoreType.DMA((NCHIP,))],
        compiler_params=_params(40),
    )(*_in_hbm([x]), g_mix, *_in_hbm([w_in]), b_in)


def _inproj_bwd(x, dx1, dua, dub, dgp, g_mix, w_in, carry=None):
    L = x.shape[0]

    def body(x_ref, dx1_ref, dua_ref, dub_ref, dgp_ref, g_ref, w_hbm, gx_ref, dz_ref, dg_ref, db_ref, w_vm, w_sems):
        _resident_now([(w_hbm.at[k], w_vm.at[k]) for k in range(NCHIP)], w_sems)

        @pl.when(pl.program_id(0) == 0)
        def _():
            dg_ref[...] = jnp.zeros_like(dg_ref)
            db_ref[...] = jnp.zeros_like(db_ref)

        for src, (a, b) in zip((dua_ref, dub_ref, dgp_ref), Z_PARTS):
            d = src[...]
            dz_ref[:, a:b] = d.astype(BF)
            db_ref[0:1, a:b] += _colsum(d)
        dh = jnp.zeros((TM, D), F32)
        for k in range(NCHIP):
            dh = dh + lax.dot_general(dz_ref[:, k * QC:(k + 1) * QC], w_vm[k], (((1,), (1,)), ((), ())),
                                      preferred_element_type=F32)
        xh, r = _rms(x_ref[...])
        dg_ref[0:1, :] += _colsum(dh * xh)
        gx_ref[...] = dx1_ref[...] + _rms_bwd(dh, xh, r, g_ref[...])

    return _pallas_call(
        body, carry, name="inproj_bwd", grid=(L // TM,),
        in_specs=[_tok(D), _tok(D), _tok(S5W), _tok(LW), _tok(2 * D), _full((1, D)), ANY],
        out_specs=[_tok(D), _tok(INC), _full((SUB, D)), _full((SUB, INC))],
        out_shape=[_sds((L, D)), _sds((L, INC), BF), _sds((SUB, D)), _sds((SUB, INC))],
        scratch_shapes=[pltpu.VMEM((NCHIP, D, QC), BF), pltpu.SemaphoreType.DMA((NCHIP,))],
        compiler_params=_params(40),
    )(x, dx1, *_in_hbm([dua]), dub, dgp, g_mix, *_in_hbm([w_in]))


def _cscan(xr_ref, xi_ref, con_ref, cr_ref, ci_ref, reverse):
    n_slab = xr_ref.shape[0] // SUB
    width = xr_ref.shape[1]
    for lc in range(width // LC):
        cols = slice(lc * LC, (lc + 1) * LC)
        con = [con_ref[SUB * j:SUB * (j + 1), cols] for j in range(8)]

        def step(k, carry, cols=cols, con=con):
            cr, ci = carry
            rows = _slab(n_slab - 1 - k if reverse else k)
            xr, xi = xr_ref[rows, cols], xi_ref[rows, cols]
            for j, sh in enumerate((1, 2, 4)):
                mr, mi = con[2 * j], con[2 * j + 1]
                pr = pltpu.roll(xr, SUB - sh if reverse else sh, 0)
                pi = pltpu.roll(xi, SUB - sh if reverse else sh, 0)
                xr, xi = xr + mr * pr - mi * pi, xi + mr * pi + mi * pr
            xr, xi = xr + con[6] * cr - con[7] * ci, xi + con[6] * ci + con[7] * cr
            xr_ref[rows, cols] = xr
            xi_ref[rows, cols] = xi
            row = 0 if reverse else SUB - 1
            return _bcast_row(xr, row), _bcast_row(xi, row)

        cr, ci = lax.fori_loop(0, n_slab, step, (cr_ref[:, cols], ci_ref[:, cols]))
        cr_ref[:, cols] = cr
        ci_ref[:, cols] = ci


def _s5_fwd(ua, bbr, bbi, ccr, cci, dsk, con, w_glu, b_glu, carry=None):
    L = ua.shape[0]

    def body(ua_ref, bbr_hbm, bbi_hbm, ccr_hbm, cci_hbm, dsk_ref, con_ref, wg_ref, bg_ref,
             sr_ref, si_ref, y_ref, zg_ref, ya_ref, bbr_vm, bbi_vm, ccr_vm, cci_vm, cr_ref, ci_ref, w_sems):
        landed = _resident([(bbr_hbm, bbr_vm), (bbi_hbm, bbi_vm), (ccr_hbm, ccr_vm), (cci_hbm, cci_vm)], w_sems)

        @pl.when(pl.program_id(0) == 0)
        def _():
            cr_ref[...] = jnp.zeros_like(cr_ref)
            ci_ref[...] = jnp.zeros_like(ci_ref)

        u = ua_ref[...]
        ub = u.astype(BF)
        landed(0)
        sr_ref[...] = _blockdiag_mm(ub, bbr_vm)
        landed(1)
        si_ref[...] = _blockdiag_mm(ub, bbi_vm)
        _cscan(sr_ref, si_ref, con_ref, cr_ref, ci_ref, reverse=False)
        landed(2)
        landed(3)
        y = (_blockdiag_mm_t(sr_ref[...].astype(BF), ccr_vm) - _blockdiag_mm_t(si_ref[...].astype(BF), cci_vm)
             + dsk_ref[...] * u)
        y_ref[...] = y
        zg = jax.nn.gelu(y)
        zg_ref[...] = zg.astype(BF)
        q = _mm(zg, wg_ref[...]) + bg_ref[...]
        ya_ref[...] = (zg * _sig(q)).astype(BF)

    return _pallas_call(
        body, carry, name="s5_fwd", grid=(L // TM,),
        in_specs=[_tok(S5W), ANY, ANY, ANY, ANY, _full((1, S5W)), _full((8 * SUB, GN)),
                  _full((S5W, S5W)), _full((1, S5W))],
        out_specs=[_tok(GN), _tok(GN), _tok(S5W), _tok(S5W), _tok(S5W)],
        out_shape=[_sds((L, GN)), _sds((L, GN)), _far((L, S5W)), _far((L, S5W), BF), _far((L, S5W), BF)],
        scratch_shapes=[pltpu.VMEM((S5W // 128, 128, GN // (S5W // 128)), BF)] * 4 + [
                        pltpu.VMEM((SUB, GN), F32), pltpu.VMEM((SUB, GN), F32),
                        pltpu.SemaphoreType.DMA((4,))],
        compiler_params=_params(44),
    )(*_in_hbm([ua, bbr, bbi, ccr, cci]), dsk, con, w_glu, b_glu)


def _s5_bwd(dya, y, ua, sr, si, bbr, bbi, ccr, cci, dsk, con_rev, w_glu, b_glu, carry=None):
    L = ua.shape[0]
    nt = L // TM
    spt = TM // SUB
    n_slab = spt

    def halo_map(i):
        return (jnp.maximum((nt - 1 - i) * spt - 1, 0), 0)

    def body(dya_ref, y_ref, ua_ref, sr_ref, si_ref, hr_ref, hi_ref, bbr_hbm, bbi_hbm, ccr_hbm, cci_hbm,
             dsk_ref, con_ref, wg_ref, bg_ref,
             dua_ref, dq_ref, dy_ref, lr_ref, li_ref, da_ref, dsm_ref,
             bbr_vm, bbi_vm, ccr_vm, cci_vm, cr_ref, ci_ref, w_sems):
        i = pl.program_id(0)
        landed = _resident([(ccr_hbm, ccr_vm), (cci_hbm, cci_vm), (bbr_hbm, bbr_vm), (bbi_hbm, bbi_vm)], w_sems)

        @pl.when(i == 0)
        def _():
            cr_ref[...] = jnp.zeros_like(cr_ref)
            ci_ref[...] = jnp.zeros_like(ci_ref)
            da_ref[...] = jnp.zeros_like(da_ref)
            dsm_ref[...] = jnp.zeros_like(dsm_ref)

        u = ua_ref[...]
        yv = y_ref[...]
        dya = dya_ref[...]
        zg = jax.nn.gelu(yv)
        sg = _sig(_mm(zg, wg_ref[...]) + bg_ref[...])
        dq = dya * zg * sg * (1.0 - sg)
        dq_ref[...] = dq.astype(BF)
        dzg = dya * sg + _mm_nt(dq, wg_ref[...])
        dy = dzg * _gelu_grad(yv)
        dyb = dy.astype(BF)
        dy_ref[...] = dyb
        dsm_ref[0:1, :] += _colsum(dy * u)
        dsm_ref[1:2, :] += _colsum(dq)
        landed(0)
        lr_ref[...] = _blockdiag_mm(dyb, ccr_vm)
        landed(1)
        li_ref[...] = -_blockdiag_mm(dyb, cci_vm)
        _cscan(lr_ref, li_ref, con_ref, cr_ref, ci_ref, reverse=True)

        first_tile = (i == nt - 1)
        row = _row_iota(LC)
        for lc in range(GN // LC):
            cols = slice(lc * LC, (lc + 1) * LC)
            h_r = jnp.where(first_tile, 0.0, hr_ref[:, cols])
            h_i = jnp.where(first_tile, 0.0, hi_ref[:, cols])

            def step(k, acc, cols=cols, h_r=h_r, h_i=h_i):
                ar, ai = acc
                rows = _slab(k)
                prev = _slab(jnp.maximum(k - 1, 0))
                pr = jnp.where(k == 0, h_r, sr_ref[prev, cols])
                pi = jnp.where(k == 0, h_i, si_ref[prev, cols])
                spr = pltpu.roll(jnp.where(row == SUB - 1, pr, sr_ref[rows, cols]), 1, 0)
                spi = pltpu.roll(jnp.where(row == SUB - 1, pi, si_ref[rows, cols]), 1, 0)
                lr, li = lr_ref[rows, cols], li_ref[rows, cols]
                return ar + lr * spr + li * spi, ai + li * spr - lr * spi

            zero = jnp.zeros((SUB, LC), F32)
            ar, ai = lax.fori_loop(0, n_slab, step, (zero, zero))
            da_ref[0:1, cols] += _colsum(ar)
            da_ref[1:2, cols] += _colsum(ai)

        landed(2)
        landed(3)
        dua_ref[...] = (dy * dsk_ref[...] + _blockdiag_mm_t(lr_ref[...].astype(BF), bbr_vm)
                        + _blockdiag_mm_t(li_ref[...].astype(BF), bbi_vm))

    return _pallas_call(
        body, carry, name="s5_bwd", grid=(nt,),
        in_specs=[_tok_rev(S5W, nt), _tok_rev(S5W, nt), _tok_rev(S5W, nt), _tok_rev(GN, nt), _tok_rev(GN, nt),
                  pl.BlockSpec((SUB, GN), halo_map), pl.BlockSpec((SUB, GN), halo_map),
                  ANY, ANY, ANY, ANY, _full((1, S5W)), _full((8 * SUB, GN)), _full((S5W, S5W)), _full((1, S5W))],
        out_specs=[_tok_rev(S5W, nt), _tok_rev(S5W, nt), _tok_rev(S5W, nt), _tok_rev(GN, nt), _tok_rev(GN, nt),
                   _full((SUB, GN)), _full((SUB, S5W))],
        out_shape=[_sds((L, S5W)), _sds((L, S5W), BF), _sds((L, S5W), BF), _sds((L, GN)), _sds((L, GN)),
                   _sds((SUB, GN)), _sds((SUB, S5W))],
        scratch_shapes=[pltpu.VMEM((S5W // 128, 128, GN // (S5W // 128)), BF)] * 4 + [
                        pltpu.VMEM((SUB, GN), F32), pltpu.VMEM((SUB, GN), F32),
                        pltpu.SemaphoreType.DMA((4,))],
        compiler_params=_params(52),
    )(dya, y, ua, sr, si, sr, si, *_in_hbm([bbr, bbi, ccr, cci]), dsk, con_rev, w_glu, b_glu)


def _lru_gate_terms(rg, sp):
    log_a = -LRU_C * rg * sp
    a = jnp.exp(log_a)
    mult = jnp.sqrt(_neg_expm1(2.0 * log_a))
    return a, mult


def _lru_fwd(ub, conv_w, conv_b, wr, wi, b_r, b_i, sp, carry=None):
    L = ub.shape[0]
    n_slab = TM // SUB

    def body(ub_ref, cw_ref, cb_ref, wr_ref, wi_ref, br_ref, bi_ref, sp_ref,
             xc_ref, rg_ref, ig_ref, h_ref, hp_ref, a_ref, halo_ref, carry_ref):
        @pl.when(pl.program_id(0) == 0)
        def _():
            halo_ref[...] = jnp.zeros_like(halo_ref)
            carry_ref[...] = jnp.zeros_like(carry_ref)

        row = _row_iota(LW)
        taps = [cw_ref[k:k + 1, :] for k in range(4)]
        cb = cb_ref[...]

        def conv_step(k, prev):
            rows = _slab(k)
            cur = ub_ref[rows, :]
            acc = taps[3] * cur + cb
            for j in (1, 2, 3):
                acc = acc + taps[3 - j] * pltpu.roll(jnp.where(row >= SUB - j, prev, cur), j, 0)
            xc_ref[rows, :] = acc
            return cur

        halo_ref[...] = lax.fori_loop(0, n_slab, conv_step, halo_ref[...])

        xc = xc_ref[...]
        xcb = xc.astype(BF)
        rg = _sig(_blockdiag_mm(xcb, wr_ref) + br_ref[...])
        ig = _sig(_blockdiag_mm(xcb, wi_ref) + bi_ref[...])
        rg_ref[...] = rg
        ig_ref[...] = ig
        a, mult = _lru_gate_terms(rg, sp_ref[...])
        a_ref[...] = a
        h_ref[...] = mult * ig * xc

        rowc = _row_iota(LC)
        for lc in range(LW // LC):
            cols = slice(lc * LC, (lc + 1) * LC)

            def step(k, c, cols=cols):
                rows = _slab(k)
                av, b = a_ref[rows, cols], h_ref[rows, cols]
                for sh in (1, 2, 4):
                    keep = rowc >= sh
                    b = b + av * jnp.where(keep, pltpu.roll(b, sh, 0), 0.0)
                    av = av * jnp.where(keep, pltpu.roll(av, sh, 0), 1.0)
                h = b + av * c
                h_ref[rows, cols] = h
                hp_ref[rows, cols] = jnp.where(rowc == 0, c, pltpu.roll(h, 1, 0))
                return _bcast_row(h, SUB - 1)

            carry_ref[:, cols] = lax.fori_loop(0, n_slab, step, carry_ref[:, cols])

    return _pallas_call(
        body, carry, name="lru_fwd", grid=(L // TM,),
        in_specs=[_tok(LW), _full((4, LW)), _full((1, LW)), _full((LW // 128, 128, 128)), _full((LW // 128, 128, 128)),
                  _full((1, LW)), _full((1, LW)), _full((1, LW))],
        out_specs=[_tok(LW)] * 5,
        out_shape=[_far((L, LW))] * 5,
        scratch_shapes=[pltpu.VMEM((TM, LW), F32), pltpu.VMEM((SUB, LW), F32), pltpu.VMEM((SUB, LW), F32)],
        compiler_params=_params(40),
    )(*_in_hbm([ub]), conv_w, conv_b, wr, wi, b_r, b_i, sp)


def _lru_bwd(dyb, xc, rg, ig, hp, ub, conv_w, wr, wi, sp, dsp, carry=None):
    L = ub.shape[0]
    nt = L // TM
    spt = TM // SUB
    n_slab = spt

    def halo_map(i):
        return (jnp.maximum((nt - 1 - i) * spt - 1, 0), 0)

    def body(dh_ref, xc_ref, rg_ref, ig_ref, hp_ref, ub_ref, uh_ref, cw_ref, wr_ref, wi_ref, sp_ref, dsp_ref,
             dub_ref, dpr_ref, dpi_ref, acc_ref, a_ref, lam_ref, dxc_ref, carry_ref, next_ref):
        i = pl.program_id(0)

        @pl.when(i == 0)
        def _():
            carry_ref[...] = jnp.zeros_like(carry_ref)
            next_ref[...] = jnp.zeros_like(next_ref)
            acc_ref[...] = jnp.zeros_like(acc_ref)

        sp = sp_ref[...]
        rg, ig, xc = rg_ref[...], ig_ref[...], xc_ref[...]
        a, mult = _lru_gate_terms(rg, sp)
        a_ref[...] = a

        rowc = _row_iota(LC)
        for lc in range(LW // LC):
            cols = slice(lc * LC, (lc + 1) * LC)

            def step(k, c, cols=cols):
                rows = _slab(n_slab - 1 - k)
                av, dh = a_ref[rows, cols], dh_ref[rows, cols]
                b = av * dh
                for sh in (1, 2, 4):
                    keep = rowc < SUB - sh
                    b = b + av * jnp.where(keep, pltpu.roll(b, SUB - sh, 0), 0.0)
                    av = av * jnp.where(keep, pltpu.roll(av, SUB - sh, 0), 1.0)
                mu = b + av * c
                lam_ref[rows, cols] = dh + jnp.where(rowc == SUB - 1, c, pltpu.roll(mu, SUB - 1, 0))
                return _bcast_row(mu, 0)

            carry_ref[:, cols] = lax.fori_loop(0, n_slab, step, carry_ref[:, cols])

        lam = lam_ref[...]
        d_a = lam * hp_ref[...]
        d_mult = lam * ig * xc
        d_ig = lam * mult * xc
        dxc = lam * mult * ig
        d_log_a = d_a * a - d_mult * a * a / mult
        d_rg = (-LRU_C) * sp * d_log_a
        acc_ref[0:1, :] += _colsum((-LRU_C) * rg * d_log_a) * dsp_ref[...]
        dpr = d_rg * rg * (1.0 - rg)
        dpi = d_ig * ig * (1.0 - ig)
        acc_ref[1:2, :] += _colsum(dpr)
        acc_ref[2:3, :] += _colsum(dpi)
        dprb, dpib = dpr.astype(BF), dpi.astype(BF)
        dpr_ref[...] = dprb
        dpi_ref[...] = dpib
        dxc = dxc + _blockdiag_mm_t(dprb, wr_ref) + _blockdiag_mm_t(dpib, wi_ref)
        dxc_ref[...] = dxc
        acc_ref[3:4, :] += _colsum(dxc)

        row = _row_iota(LW)
        taps = [cw_ref[k:k + 1, :] for k in range(4)]
        u_halo = jnp.where(i == nt - 1, 0.0, uh_ref[...])
        nxt_tile = next_ref[...]

        def conv_step(k, accs):
            rows = _slab(k)
            cur = dxc_ref[rows, :]
            nxt = jnp.where(k == n_slab - 1, nxt_tile, dxc_ref[_slab(jnp.minimum(k + 1, n_slab - 1)), :])
            ucur = ub_ref[rows, :]
            uprev = jnp.where(k == 0, u_halo, ub_ref[_slab(jnp.maximum(k - 1, 0)), :])
            du = taps[3] * cur
            new = [accs[3] + cur * ucur]
            for j in (1, 2, 3):
                du = du + taps[3 - j] * pltpu.roll(jnp.where(row < j, nxt, cur), SUB - j, 0)
                new.append(accs[3 - j] + cur * pltpu.roll(jnp.where(row >= SUB - j, uprev, ucur), j, 0))
            dub_ref[rows, :] = du
            return tuple(new[::-1])

        zero = jnp.zeros((SUB, LW), F32)
        accs = lax.fori_loop(0, n_slab, conv_step, (zero, zero, zero, zero))
        for k in range(4):
            acc_ref[4 + k:5 + k, :] += _colsum(accs[k])
        next_ref[...] = dxc_ref[0:SUB, :]

    return _pallas_call(
        body, carry, name="lru_bwd", grid=(nt,),
        in_specs=[_tok_rev(LW, nt)] * 6 + [pl.BlockSpec((SUB, LW), halo_map), _full((4, LW)),
                                           _full((LW // 128, 128, 128)), _full((LW // 128, 128, 128)), _full((1, LW)), _full((1, LW))],
        out_specs=[_tok_rev(LW, nt), _tok_rev(LW, nt), _tok_rev(LW, nt), _full((SUB, LW))],
        out_shape=[_sds((L, LW)), _far((L, LW), BF), _far((L, LW), BF), _sds((SUB, LW))],
        scratch_shapes=[pltpu.VMEM((TM, LW), F32), pltpu.VMEM((TM, LW), F32), pltpu.VMEM((TM, LW), F32),
                        pltpu.VMEM((SUB, LW), F32), pltpu.VMEM((SUB, LW), F32)],
        compiler_params=_params(48),
    )(dyb, xc, rg, ig, hp, ub, ub, conv_w, wr, wi, sp, dsp)


AC = D // NCHIP


def _merge_fwd(x, ya, yb, gp, w_a, w_b, w_o, carry=None):
    L = x.shape[0]

    def body(x_ref, ya_ref, yb_ref, gp_ref, wa_ref, wb_ref, wo_ref, x1_ref, pa_ref, pb_ref, mg_ref):
        ya = ya_ref[...]
        for k in range(NCHIP):
            pa_ref[:, k * AC:(k + 1) * AC] = jnp.dot(ya, wa_ref[k], preferred_element_type=F32)
        pb = _mm(yb_ref[...], wb_ref[...])
        pb_ref[...] = pb
        gp = gp_ref[...]
        merged = (_sig(gp[:, :D]) * pa_ref[...] + _sig(gp[:, D:]) * pb).astype(BF)
        mg_ref[...] = merged
        x1_ref[...] = x_ref[...] + jnp.dot(merged, wo_ref[...], preferred_element_type=F32)

    return _pallas_call(
        body, carry, name="merge_fwd", grid=(L // TM,),
        in_specs=[_tok(D), _tok(S5W), _tok(LW), _tok(2 * D), _full((NCHIP, S5W, AC)), _full((LW, D)), _full((D, D))],
        out_specs=[_tok(D), _tok(D), _tok(D), _tok(D)],
        out_shape=[_sds((L, D)), _sds((L, D)), _sds((L, D)), _far((L, D), BF)],
        compiler_params=_params(40),
    )(x, ya, yb, gp, w_a, w_b, w_o)


def _merge_bwd(dx1, gp, pa, pb, w_a, w_b, w_o, carry=None):
    L = dx1.shape[0]

    def body(dx1_ref, gp_ref, pa_ref, pb_ref, wa_ref, wb_ref, wo_ref, dya_ref, dyb_ref, dgp_ref, dpa_ref, dpb_ref):
        dm = _mm_nt(dx1_ref[...], wo_ref[...])
        gp = gp_ref[...]
        sa, sb = _sig(gp[:, :D]), _sig(gp[:, D:])
        dpa = (dm * sa).astype(BF)
        dpb = (dm * sb).astype(BF)
        dpa_ref[...] = dpa
        dpb_ref[...] = dpb
        dgp_ref[:, :D] = dm * pa_ref[...] * sa * (1.0 - sa)
        dgp_ref[:, D:] = dm * pb_ref[...] * sb * (1.0 - sb)
        dya = jnp.zeros((TM, S5W), F32)
        for k in range(NCHIP):
            dya = dya + _mm_nt(dpa[:, k * AC:(k + 1) * AC], wa_ref[k])
        dya_ref[...] = dya
        dyb_ref[...] = _mm_nt(dpb, wb_ref[...])

    return _pallas_call(
        body, carry, name="merge_bwd", grid=(L // TM,),
        in_specs=[_tok(D), _tok(2 * D), _tok(D), _tok(D), _full((NCHIP, S5W, AC)), _full((LW, D)), _full((D, D))],
        out_specs=[_tok(S5W), _tok(LW), _tok(2 * D), _tok(D), _tok(D)],
        out_shape=[_far((L, S5W)), _far((L, LW)), _sds((L, 2 * D)), _far((L, D), BF), _far((L, D), BF)],
        compiler_params=_params(40),
    )(dx1, gp, pa, pb, w_a, w_b, w_o)


def _chunk_tok(width):
    return pl.BlockSpec((NCHIP, TM, width), lambda i: (0, i, 0))


def _ffn_weight_copies(wg_hbm, wu_hbm, wd_halves, wg_vm, wu_vm, wd_vm):
    half = FC // 2
    pairs = []
    for c in range(NCHIP):
        pairs += [(wg_hbm.at[c], wg_vm.at[c]), (wu_hbm.at[c], wu_vm.at[c])]
        pairs += [(src.at[c], wd_vm.at[c, pl.ds(k * half, half)]) for k, src in enumerate(wd_halves)]
    return pairs


def _ffn_fwd(x1, g_ffn, wg, wu, wd, carry=None):
    L = x1.shape[0]

    def body(x_ref, g_ref, wg_hbm, wu_hbm, wd_lo_hbm, wd_hi_hbm, x2_ref, h2_ref, gg_ref, uu_ref, wg_vm, wu_vm, wd_vm, w_sems):
        _resident_now(_ffn_weight_copies(wg_hbm, wu_hbm, (wd_lo_hbm, wd_hi_hbm), wg_vm, wu_vm, wd_vm), w_sems)
        x = x_ref[...]
        xh, _ = _rms(x)
        h2 = (xh * g_ref[...]).astype(BF)
        h2_ref[...] = h2
        out = x
        for c in range(NCHIP):
            gg = lax.dot_general(h2, wg_vm[c], (((1,), (1,)), ((), ())), preferred_element_type=F32)
            uu = lax.dot_general(h2, wu_vm[c], (((1,), (1,)), ((), ())), preferred_element_type=F32)
            gg_ref[c] = gg.astype(BF)
            uu_ref[c] = uu.astype(BF)
            act = (gg * _sig(gg) * uu).astype(BF)
            out = out + jnp.dot(act, wd_vm[c], preferred_element_type=F32)
        x2_ref[...] = out

    return _pallas_call(
        body, carry, name="ffn_fwd", grid=(L // TM,),
        in_specs=[_tok(D), _full((1, D)), ANY, ANY, ANY, ANY],
        out_specs=[_tok(D), _tok(D), _chunk_tok(FC), _chunk_tok(FC)],
        out_shape=[_sds((L, D)), _sds((L, D), BF), _sds((NCHIP, L, FC), BF), _sds((NCHIP, L, FC), BF)],
        scratch_shapes=[pltpu.VMEM((NCHIP, FC, D), BF)] * 3 + [pltpu.SemaphoreType.DMA((4 * NCHIP,))],
        compiler_params=_params(52),
    )(x1, g_ffn, wg, wu, *wd)


def _ffn_bwd(x1, dx2, gg, uu, g_ffn, wg, wu, wd, carry=None):
    L = x1.shape[0]

    def body(x_ref, dx2_ref, gg_ref, uu_ref, g_ref, wg_hbm, wu_hbm, wd_lo_hbm, wd_hi_hbm,
             dx1_ref, act_ref, dgg_ref, duu_ref, dg_ref, wg_vm, wu_vm, wd_vm, w_sems):
        _resident_now(_ffn_weight_copies(wg_hbm, wu_hbm, (wd_lo_hbm, wd_hi_hbm), wg_vm, wu_vm, wd_vm), w_sems)

        @pl.when(pl.program_id(0) == 0)
        def _():
            dg_ref[...] = jnp.zeros_like(dg_ref)

        dx2 = dx2_ref[...]
        dx2b = dx2.astype(BF)
        dh2 = jnp.zeros((TM, D), F32)
        for c in range(NCHIP):
            g = gg_ref[c].astype(F32)
            u = uu_ref[c].astype(F32)
            s = _sig(g)
            silu = g * s
            act_ref[c] = (silu * u).astype(BF)
            dact = lax.dot_general(dx2b, wd_vm[c], (((1,), (1,)), ((), ())), preferred_element_type=F32)
            dg = (dact * u * s * (1.0 + g * (1.0 - s))).astype(BF)
            du = (dact * silu).astype(BF)
            dgg_ref[c] = dg
            duu_ref[c] = du
            dh2 = dh2 + jnp.dot(dg, wg_vm[c], preferred_element_type=F32)
            dh2 = dh2 + jnp.dot(du, wu_vm[c], preferred_element_type=F32)
        xh, r = _rms(x_ref[...])
        dg_ref[0:1, :] += _colsum(dh2 * xh)
        dx1_ref[...] = dx2 + _rms_bwd(dh2, xh, r, g_ref[...])

    return _pallas_call(
        body, carry, name="ffn_bwd", grid=(L // TM,),
        in_specs=[_tok(D), _tok(D), _chunk_tok(FC), _chunk_tok(FC), _full((1, D)), ANY, ANY, ANY, ANY],
        out_specs=[_tok(D), _chunk_tok(FC), _chunk_tok(FC), _chunk_tok(FC), _full((SUB, D))],
        out_shape=[_sds((L, D)), _sds((NCHIP, L, FC), BF), _sds((NCHIP, L, FC), BF), _sds((NCHIP, L, FC), BF),
                   _sds((SUB, D))],
        scratch_shapes=[pltpu.VMEM((NCHIP, FC, D), BF)] * 3 + [pltpu.SemaphoreType.DMA((4 * NCHIP,))],
        compiler_params=_params(56),
    )(x1, dx2, gg, uu, g_ffn, wg, wu, *wd)


def _ple_loss(x2, p, tgt, g_pg, w_pg, b_pg, w_ple, g_ple, g_final):
    L = x2.shape[0]

    def body(x2_ref, p_ref, t_ref, gpg_ref, wpg_ref, bpg_ref, wple_ref, gple_ref, gf_ref,
             dx2_ref, n2_ref, dpre_ref, de0_ref, acc_ref):
        @pl.when(pl.program_id(0) == 0)
        def _():
            acc_ref[...] = jnp.zeros_like(acc_ref)

        x2 = x2_ref[...]
        x2h, r2 = _rms(x2)
        n2 = (x2h * gpg_ref[...]).astype(BF)
        n2_ref[...] = n2
        gate = _sig(jnp.dot(n2, wpg_ref[...], preferred_element_type=F32) + bpg_ref[...])
        pb = p_ref[...].astype(BF)
        e0 = jnp.concatenate([jnp.dot(pb, wple_ref[k], preferred_element_type=F32) for k in range(NCHIP)], axis=1)
        e0h, re = _rms(e0)
        e = e0h * gple_ref[...]
        x3 = x2 + gate * e
        x3h, r3 = _rms(x3)
        diff = x3h * gf_ref[...] - t_ref[...]
        acc_ref[4:5, :] += _colsum(diff * diff) * (0.5 / D)
        dy = diff * (1.0 / D)
        acc_ref[3:4, :] += _colsum(dy * x3h)
        dx3 = _rms_bwd(dy, x3h, r3, gf_ref[...])
        de = dx3 * gate
        acc_ref[2:3, :] += _colsum(de * e0h)
        de0_ref[...] = _rms_bwd(de, e0h, re, gple_ref[...]).astype(BF)
        dpre = dx3 * e * gate * (1.0 - gate)
        acc_ref[1:2, :] += _colsum(dpre)
        dpreb = dpre.astype(BF)
        dpre_ref[...] = dpreb
        dn2 = lax.dot_general(dpreb, wpg_ref[...], (((1,), (1,)), ((), ())), preferred_element_type=F32)
        acc_ref[0:1, :] += _colsum(dn2 * x2h)
        dx2_ref[...] = dx3 + _rms_bwd(dn2, x2h, r2, gpg_ref[...])

    return _pallas_call(
        body, name="ple_loss", grid=(L // TM,),
        in_specs=[_tok(D), _tok(PLE), _tok(D), _full((1, D)), _full((D, D)), _full((1, D)), _full((NCHIP, PLE, AC)),
                  _full((1, D)), _full((1, D))],
        out_specs=[_tok(D), _tok(D), _tok(D), _tok(D), _full((SUB, D))],
        out_shape=[_sds((L, D)), _sds((L, D), BF), _sds((L, D), BF), _sds((L, D), BF), _sds((SUB, D))],
        compiler_params=_params(40),
    )(x2, p, tgt, g_pg, *_in_hbm([w_pg]), b_pg, *_in_hbm([w_ple]), g_ple, g_final)


def _tn(name, a, b, col_chunk=None, a_block=None, carry=None):
    L = a.shape[-2]
    m, n = a.shape[-1], b.shape[-1]
    a_col = 0
    if a_block is not None:
        a_col, m = a_block
    tk = L if (a.ndim == 3 or b.ndim == 3 or a_block is not None) else TK
    if a.ndim == 3 or b.ndim == 3:
        nj, bn = (a if a.ndim == 3 else b).shape[0], n
        a_spec = (pl.BlockSpec((None, tk, m), lambda j, t: (j, t, 0)) if a.ndim == 3
                  else pl.BlockSpec((tk, m), lambda j, t: (t, 0)))
        b_spec = (pl.BlockSpec((None, tk, n), lambda j, t: (j, t, 0)) if b.ndim == 3
                  else pl.BlockSpec((tk, n), lambda j, t: (t, 0)))
        out_spec, out_shape = pl.BlockSpec((None, m, n), lambda j, t: (j, 0, 0)), _sds((nj, m, n))
    else:
        bn = col_chunk
        if bn is None:
            bn = next((cand for cand in (1024, 512) if n > cand and n % cand == 0), n)
        nj = n // bn
        a_spec = pl.BlockSpec((tk, m), lambda j, t: (t, a_col))
        b_spec = pl.BlockSpec((tk, bn), lambda j, t: (t, j))
        if col_chunk is None:
            out_spec, out_shape = pl.BlockSpec((m, bn), lambda j, t: (0, j)), _sds((m, n))
        else:
            out_spec, out_shape = pl.BlockSpec((None, m, bn), lambda j, t: (j, 0, 0)), _sds((nj, m, bn))

    def body(a_ref, b_ref, o_ref):
        if tk == L:
            o_ref[...] = _mm_tn(a_ref[...], b_ref[...])
        else:
            @pl.when(pl.program_id(1) == 0)
            def _():
                o_ref[...] = jnp.zeros_like(o_ref)

            o_ref[...] += _mm_tn(a_ref[...], b_ref[...])

    outs = _pallas_call(
        body, carry, name=name, grid=(nj, L // tk), in_specs=[a_spec, b_spec], out_specs=[out_spec],
        out_shape=[pltpu.HBM(out_shape.shape, out_shape.dtype)],
        compiler_params=pltpu.CompilerParams(dimension_semantics=("arbitrary", "arbitrary"),
                                             vmem_limit_bytes=(30 if tk == L else 28) * VMEM_MB),
    )(*(_in_hbm([a, b]) if tk == L else (a, b)))
    return outs[0] if carry is None else outs


LANE = 128


def _tn_blocks(name, a, bs, ga, gb, carry=None):
    L, m, n, nb = a.shape[0], a.shape[1], bs[0].shape[1], len(bs)
    per = LANE // ga
    wb = per * gb
    n_super = m // LANE

    def body(a_ref, *refs):
        b_refs, o_refs, acc_refs = refs[:nb], refs[nb:2 * nb], refs[2 * nb:]
        t = pl.program_id(0)

        @pl.when(t == 0)
        def _():
            for acc in acc_refs:
                acc[...] = jnp.zeros_like(acc)

        lhs = a_ref[...].astype(BF)
        for b_ref, acc in zip(b_refs, acc_refs):
            rhs = b_ref[...].astype(BF)
            for j in range(n_super):
                acc[j] += _mm_tn(lhs[:, j * LANE:(j + 1) * LANE], rhs[:, j * wb:(j + 1) * wb])

        @pl.when(t == L // TK - 1)
        def _():
            own = (lax.broadcasted_iota(jnp.int32, (LANE, wb), 0) // ga) == (lax.broadcasted_iota(jnp.int32, (LANE, wb), 1) // gb)
            for o_ref, acc in zip(o_refs, acc_refs):
                for j in range(n_super):
                    kept = jnp.where(own, acc[j], 0.0)
                    o_ref[:, j * wb:(j + 1) * wb] = jnp.sum(kept.reshape(per, ga, wb), axis=0)

    outs = _pallas_call(
        body, carry, name=name, grid=(L // TK,),
        in_specs=[pl.BlockSpec((TK, m), lambda t: (t, 0))] + [pl.BlockSpec((TK, n), lambda t: (t, 0))] * nb,
        out_specs=[_full((ga, n))] * nb, out_shape=[_sds((ga, n))] * nb,
        scratch_shapes=[pltpu.VMEM((n_super, LANE, wb), F32)] * nb,
        compiler_params=_params(48),
    )(*_in_hbm([a] + list(bs)))
    return list(outs)


def _s5_discretize(lam_re, lam_im, log_dt, b_re, b_im):
    dt = jnp.exp(log_dt)[:, None]
    mag = jnp.exp(lam_re * dt)
    ar = mag * jnp.cos(lam_im * dt)
    ai = mag * jnp.sin(lam_im * dt)
    den = lam_re * lam_re + lam_im * lam_im
    nr = ar - 1.0
    fr = (nr * lam_re + ai * lam_im) / den
    fi = (ai * lam_re - nr * lam_im) / den
    bbr = fr[:, None, :] * b_re - fi[:, None, :] * b_im
    bbi = fr[:, None, :] * b_im + fi[:, None, :] * b_re
    return ar, ai, bbr, bbi


def _prepare(by_rows, block_cols, ar, ai):
    n = len(by_rows)

    def body(*refs):
        srcs, (ar_ref, ai_ref), dense, (con_ref, rev_ref) = refs[:n], refs[n:n + 2], refs[n + 2:2 * n + 2], refs[2 * n + 2:]
        for src, out, c in zip(srcs, dense, block_cols):
            r = src.shape[0]
            per = LANE // r
            wide = per * c
            own = (lax.broadcasted_iota(jnp.int32, (LANE, wide), 0) // r) == (lax.broadcasted_iota(jnp.int32, (LANE, wide), 1) // c)
            for j in range(out.shape[0]):
                tiled = jnp.broadcast_to(src[:, j * wide:(j + 1) * wide][None], (per, r, wide)).reshape(LANE, wide)
                out[j] = jnp.where(own, tiled, 0.0).astype(BF)
        a_r, a_i = ar_ref[...], ai_ref[...]
        pw = [(jnp.ones_like(a_r), jnp.zeros_like(a_i))]
        for _ in range(SUB):
            pr, pi = pw[-1]
            pw.append((pr * a_r - pi * a_i, pr * a_i + pi * a_r))
        row = _row_iota(GN)
        for ref, reverse in ((con_ref, False), (rev_ref, True)):
            sign = -1.0 if reverse else 1.0
            for j, sh in enumerate((1, 2, 4)):
                keep = (row < SUB - sh) if reverse else (row >= sh)
                ref[2 * j * SUB:(2 * j + 1) * SUB, :] = jnp.where(keep, pw[sh][0], 0.0)
                ref[(2 * j + 1) * SUB:(2 * j + 2) * SUB, :] = jnp.where(keep, sign * pw[sh][1], 0.0)
            p_r, p_i = jnp.zeros((SUB, GN), F32), jnp.zeros((SUB, GN), F32)
            for i in range(SUB):
                k = SUB - i if reverse else i + 1
                p_r = jnp.where(row == i, pw[k][0], p_r)
                p_i = jnp.where(row == i, sign * pw[k][1], p_i)
            ref[6 * SUB:7 * SUB, :] = p_r
            ref[7 * SUB:8 * SUB, :] = p_i

    dense_shapes = [(b.shape[1] // (LANE // b.shape[0] * c), LANE, LANE // b.shape[0] * c)
                    for b, c in zip(by_rows, block_cols)]
    outs = _pallas_call(
        body, name="prepare", grid=(1,), in_specs=[_full(b.shape) for b in by_rows] + [_full((1, GN))] * 2,
        out_specs=[_full(s) for s in dense_shapes] + [_full((8 * SUB, GN))] * 2,
        out_shape=[_far(s, BF) for s in dense_shapes] + [_sds((8 * SUB, GN)), _far((8 * SUB, GN))],
        compiler_params=_params(48),
    )(*by_rows, ar, ai)
    return outs[:n], outs[n], outs[n + 1]


def _local_step(x, p, tgt, w, comm):
    rows_of = lambda a: a.reshape(NCHIP * a.shape[1], a.shape[2])
    quarters = lambda a: a.reshape(NCHIP, a.shape[0] // NCHIP, a.shape[1])

    def gathering(names, call):
        carry = comm.gather(names)
        outs = list(call(carry))
        own = len(outs) - len(carry.out_shapes)
        w.update(zip(names, outs[own:]))
        return outs[:own]

    w.update(comm.first())
    ar, ai, bbr, bbi = _s5_discretize(w["lam_re"], w["lam_im"], w["log_dt"], w["s5_b_re"], w["s5_b_im"])
    by_row = lambda b: jnp.transpose(b, (1, 0, 2)).reshape(b.shape[1], -1)
    (bbr_d, bbi_d, ccr_d, cci_d, wr_d, wi_d), con, con_rev = _prepare(
        [by_row(b) for b in (bbr, bbi, w["s5_c_re"], w["s5_c_im"], w["w_r"], w["w_i"])], [NS] * 4 + [HD] * 2,
        ar.reshape(1, GN), ai.reshape(1, GN))
    dsk = w["s5_d"].reshape(1, S5W)
    lam = w["lru_lambda"].reshape(1, LW)
    sp = jax.nn.softplus(-lam)
    b_r, b_i = w["b_r"].reshape(1, LW), w["b_i"].reshape(1, LW)
    row = lambda name: w[name].reshape(1, -1)

    h, ua, ub, gp = gathering(["w_glu", "w_a_out", "w_b_out", "w_o"], lambda carry: _inproj_fwd(
        x, row("g_mix"), w["w_in"], row("b_in"), carry))
    w_glu = rows_of(w["w_glu"])
    sr, si, y, zg, ya = gathering(["w_ffn_gate", "w_ffn_down_lo"], lambda carry: _s5_fwd(
        ua, bbr_d, bbi_d, ccr_d, cci_d, dsk, con, w_glu, row("b_glu"), carry))
    xc, rg, ig, yb, hp = gathering(["w_ffn_up"], lambda carry: _lru_fwd(
        ub, w["conv_w"], row("conv_b"), wr_d, wi_d, b_r, b_i, sp, carry))
    w_b_out, w_o = rows_of(w["w_b_out"]), rows_of(w["w_o"])
    x1, pa, pb, merged = gathering(["w_ffn_down_hi"], lambda carry: _merge_fwd(
        x, ya, yb, gp, w["w_a_out"], w_b_out, w_o, carry))
    w_ffn_down = (w["w_ffn_down_lo"], w["w_ffn_down_hi"])
    x2, h2, gg, uu = gathering(["w_ple_gate", "w_ple"], lambda carry: _ffn_fwd(
        x1, row("g_ffn"), w["w_ffn_gate"], w["w_ffn_up"], w_ffn_down, carry))
    w_pg = rows_of(w["w_ple_gate"])
    dx2, n2, dpre, de0, acc_p = _ple_loss(x2, p, tgt, row("g_ple_gate"), w_pg, row("b_ple_gate"),
                                          w["w_ple"], row("g_ple"), row("g_final"))
    comm.reduce("ple", {"w_ple_gate": quarters(_tn("dw_ple_gate", n2, dpre)),
                        "w_ple": _tn("dw_ple", p, de0, col_chunk=AC)})
    dx1, act, dgg, duu, acc_f = comm.run(lambda carry: _ffn_bwd(
        x1, dx2, gg, uu, row("g_ffn"), w["w_ffn_gate"], w["w_ffn_up"], w_ffn_down, carry))
    comm.reduce("ffn_gate", {"w_ffn_gate": _tn("dw_ffn_gate", dgg, h2)})
    comm.reduce("w_o", {"w_o": quarters(_tn("dw_o", *_in_hbm([merged, dx1])))})
    comm.reduce("ffn_up", {"w_ffn_up": comm.run(lambda carry: _tn("dw_ffn_up", duu, h2, carry=carry))[0]})
    comm.reduce("ffn_down", {"w_ffn_down": comm.run(lambda carry: _tn("dw_ffn_down", act, dx2, carry=carry),
                                                    hold=("ffn_gate", "w_o"))[0]})
    dya, dyb, dgp, dpa, dpb = comm.run(lambda carry: _merge_bwd(
        dx1, gp, pa, pb, w["w_a_out"], w_b_out, w_o, carry), hold=("ffn_gate", "ffn_up"))
    comm.reduce("merge", {"w_a_out": _tn("dw_a_out", ya, dpa, col_chunk=AC), "w_b_out": quarters(_tn("dw_b_out", yb, dpb))})
    dua, dq, dy, lr, li, acc_a, acc_s = comm.run(lambda carry: _s5_bwd(
        dya, y, ua, sr, si, bbr_d, bbi_d, ccr_d, cci_d, dsk, con_rev, w_glu, row("b_glu"), carry), hold=("ffn_down",))
    dub, dpr, dpi, acc_l = comm.run(lambda carry: _lru_bwd(
        dyb, xc, rg, ig, hp, ub, w["conv_w"], wr_d, wi_d, sp, -_sig(-lam), carry))
    gx, dz, acc_g, acc_b = _inproj_bwd(x, dx1, dua, dub, dgp, row("g_mix"), w["w_in"])
    half = (D // 2,)
    comm.reduce("in_lo", {"w_in_lo": comm.run(lambda carry: _tn(
        "dw_in_lo", h, dz, col_chunk=QC, a_block=(0,) + half, carry=carry))[0]})
    comm.reduce("in_hi", {"w_in_hi": comm.run(lambda carry: _tn(
        "dw_in_hi", h, dz, col_chunk=QC, a_block=(1,) + half, carry=carry))[0], "w_glu": quarters(_tn("dw_glu", zg, dq))})
    d_wr, d_wi = comm.run(lambda carry: _tn_blocks("dw_r_i", xc, [dpr, dpi], HD, HD, carry))
    d_bbr, d_bbi = comm.run(lambda carry: _tn_blocks("d_bb", ua, [lr, li], NP, NS, carry))
    d_ccr, d_cci = comm.run(lambda carry: _tn_blocks("d_cc", dy, [sr, si], NP, NS, carry))
    comm.drain()
    sums = {"ple": acc_p, "ffn": acc_f, "mix": acc_g, "b_in": acc_b, "lru": acc_l, "s5": acc_s, "s5_a": acc_a}
    blocks = {"bb_re": d_bbr, "bb_im": d_bbi,
              "cc_re": d_ccr, "cc_im": d_cci,
              "w_r": d_wr, "w_i": d_wi}
    return gx, sums, blocks


def _replicated_grads(w, sums, blocks):
    grouped = lambda e, groups: jnp.transpose(e.reshape(e.shape[0], groups, -1), (1, 0, 2))
    d_ar, d_ai = sums["s5_a"][0].reshape(NG, NS), sums["s5_a"][1].reshape(NG, NS)
    d_bbr, d_bbi = grouped(blocks["bb_re"], NG), grouped(blocks["bb_im"], NG)
    _, vjp = jax.vjp(_s5_discretize, w["lam_re"], w["lam_im"], w["log_dt"], w["s5_b_re"], w["s5_b_im"])
    g = dict(zip(("lam_re", "lam_im", "log_dt", "s5_b_re", "s5_b_im"), vjp((d_ar, d_ai, d_bbr, d_bbi))))
    g["s5_c_re"] = grouped(blocks["cc_re"], NG)
    g["s5_c_im"] = -grouped(blocks["cc_im"], NG)
    g["w_r"], g["w_i"] = grouped(blocks["w_r"], NH), grouped(blocks["w_i"], NH)
    g["s5_d"] = sums["s5"][0].reshape(NG, NP)
    g["b_r"] = sums["lru"][1].reshape(NH, HD)
    g["b_i"] = sums["lru"][2].reshape(NH, HD)
    return g


ACC_ROWS = {"g_mix": ("mix", 0), "b_in": ("b_in", 0), "g_ffn": ("ffn", 0), "g_ple_gate": ("ple", 0),
            "b_ple_gate": ("ple", 1), "g_ple": ("ple", 2), "g_final": ("ple", 3), "b_glu": ("s5", 1),
            "lru_lambda": ("lru", 0), "conv_b": ("lru", 3)}
LOSS_ROW = ("ple", 4)
CONV_W_ROWS = ("lru", 4)


SHARDED = [("w_in", (D, QC)), ("w_glu", (S5W // NCHIP, S5W)), ("w_a_out", (S5W, AC)), ("w_b_out", (LW // NCHIP, D)),
           ("w_o", (D // NCHIP, D)), ("w_ffn_gate", (FC, D)), ("w_ffn_up", (FC, D)), ("w_ffn_down", (FC, D)),
           ("w_ple_gate", (D // NCHIP, D)), ("w_ple", (PLE, AC))]
TRANSPOSED = ("w_ffn_gate", "w_ffn_up", "s5_b_re", "s5_b_im")
CONV_SHARD = (4, LW // NCHIP)


def _mesh_pos():
    return lax.axis_index("x"), lax.axis_index("y"), lax.axis_index("c")


def _other_chips(x, y):
    return [(1 - x, y), (x, 1 - y), (1 - x, 1 - y)]


def _half_rows(c, rows, align):
    return pl.ds(pl.multiple_of(c * (rows // 2), align), rows // 2)


def _run_now(name, carry):
    c_in, c_out = len(carry.operands), len(carry.out_shapes)

    def body(*refs):
        ins, outs, sems = refs[:c_in], refs[c_in:c_in + c_out], refs[c_in + c_out:]
        carry.start(ins, outs, sems)
        carry.finish(ins, outs, sems)

    return pl.pallas_call(body, name=name, in_specs=[ANY] * c_in, out_specs=[ANY] * c_out,
                          out_shape=list(carry.out_shapes), scratch_shapes=list(carry.sems),
                          input_output_aliases=dict(carry.aliases))(*_in_hbm(carry.operands))


def _gather_group(shards, split):
    n = len(shards)

    def copies(srcs, outs, sems):
        send_sems, recv_sems = sems
        x, y, c = _mesh_pos()
        k0 = 2 * x + y
        sib = (x, y, 1 - c)
        chips = _other_chips(x, y)

        def remote(src, dst, j, i, to):
            return pltpu.make_async_remote_copy(src_ref=src, dst_ref=dst, send_sem=send_sems.at[j, i],
                                                recv_sem=recv_sems.at[j, i], device_id=to, device_id_type=MESH)

        def rows(ref, i, core, *lead):
            if not split[i]:
                return ref.at[lead] if lead else ref
            return ref.at[(*lead, _half_rows(core, shards[i].shape[0], 16))]

        own = [remote(s, o.at[k0], 6, i, sib) for i, (s, o) in enumerate(zip(srcs, outs))]
        ici, landed, fwd, fwd_landed = [], [], [], []
        for j, chip in enumerate(chips):
            kj = 2 * chip[0] + chip[1]
            pairs = list(enumerate(zip(srcs, outs)))
            ici.append([remote(rows(s, i, c), rows(o, i, c, k0), j, i, (*chip, c)) for i, (s, o) in pairs])
            landed.append([remote(rows(s, i, c), rows(o, i, c, kj), j, i, (*chip, c)) for i, (s, o) in pairs])
            fwd.append([remote(rows(o, i, c, kj), rows(o, i, c, kj), 3 + j, i, sib) for i, (s, o) in pairs if split[i]])
            fwd_landed.append([remote(rows(o, i, 1 - c, kj), rows(o, i, 1 - c, kj), 3 + j, i, sib)
                               for i, (s, o) in pairs if split[i]])
        return own, ici, landed, fwd, fwd_landed

    def start(srcs, outs, sems):
        own, ici, _, _, _ = copies(srcs, outs, sems)
        for cp in own + [cp for per_chip in ici for cp in per_chip]:
            cp.start()

    def finish(srcs, outs, sems):
        own, ici, landed, fwd, fwd_landed = copies(srcs, outs, sems)
        passed = [i for i in range(n) if split[i]]
        for j in range(3):
            for i, cp in enumerate(landed[j]):
                cp.wait_recv()
                if split[i]:
                    fwd[j][passed.index(i)].start()
        for j in range(3):
            for cp in fwd_landed[j]:
                cp.wait_recv()
        for cp in own:
            cp.wait_recv()
        for cp in own + [cp for per_chip in ici + fwd for cp in per_chip]:
            cp.wait_send()

    return _Carried(shards, [_far((NCHIP,) + s.shape, s.dtype) for s in shards],
                    [pltpu.SemaphoreType.DMA((7, n)), pltpu.SemaphoreType.DMA((7, n))], start, finish)


def _to_bf16_group(name, arrays, carry):
    n = len(arrays)

    def body(*refs):
        for src, dst in zip(refs[:n], refs[n:]):
            dst[...] = src[...].astype(BF)

    specs = [pl.BlockSpec((a.shape[0] // 2, a.shape[1]), lambda i: (i, 0)) for a in arrays]
    return _pallas_call(body, carry, name=name, grid=(2,), in_specs=specs, out_specs=specs,
                        out_shape=[_far(a.shape, BF) for a in arrays], compiler_params=_params(48))(*arrays)


def _each_copy(copies, carried, out_shapes, sems, aliases=None):
    def start(ins, outs, sem_refs):
        for cp in copies(ins, outs, sem_refs):
            cp.start()

    def finish(ins, outs, sem_refs):
        for cp in copies(ins, outs, sem_refs):
            cp.wait()

    return _Carried(carried, out_shapes, sems, start, finish, aliases)


def _swap_group(grads):
    n = len(grads)

    def copies(srcs, outs, sems):
        send_sems, recv_sems = sems
        x, y, c = _mesh_pos()
        return [pltpu.make_async_remote_copy(src_ref=s.at[:, _half_rows(1 - c, s.shape[1], 8)], dst_ref=o,
                                             send_sem=send_sems.at[i], recv_sem=recv_sems.at[i], device_id=(x, y, 1 - c),
                                             device_id_type=MESH) for i, (s, o) in enumerate(zip(srcs, outs))]

    return _each_copy(copies, grads, [pltpu.HBM((NCHIP, g.shape[1] // 2, g.shape[2]), F32) for g in grads],
                      [pltpu.SemaphoreType.DMA((n,)), pltpu.SemaphoreType.DMA((n,))])


def _add_sibling_group(tag, kc_idx, grads, gots):
    n = len(grads)

    def body(kc_ref, *refs):
        for g, rx, p, pb in zip(refs[:n], refs[n:2 * n], refs[2 * n:3 * n], refs[3 * n:]):
            s = g[...] + rx[...]
            pb[...] = s.astype(BF)

            @pl.when(pl.program_id(0) == kc_ref[0])
            def _():
                p[...] = s

    halves = [pl.BlockSpec((None,) + rx.shape[1:], lambda k, kc_ref: (k, 0, 0)) for rx in gots]
    mine = [pl.BlockSpec((None,) + rx.shape[1:], lambda k, kc_ref: (k, kc_ref[1], 0)) for rx in gots]
    own = [pl.BlockSpec(rx.shape[1:], lambda k, kc_ref: (0, 0)) for rx in gots]
    outs = _pallas_call(
        body, name="add_sibling_" + tag,
        grid_spec=pltpu.PrefetchScalarGridSpec(num_scalar_prefetch=1, grid=(NCHIP,), in_specs=mine + halves,
                                               out_specs=own + halves),
        out_shape=[pltpu.HBM(rx.shape[1:], F32) for rx in gots] + [pltpu.HBM(rx.shape, BF) for rx in gots],
        compiler_params=_params(48),
    )(kc_idx, *_in_hbm(list(grads) + list(gots)))
    return outs[:n], outs[n:]


def _exchange_group(parts):
    n = len(parts)

    def copies(srcs, outs, sems):
        send_sems, recv_sems = sems
        x, y, c = _mesh_pos()
        return [pltpu.make_async_remote_copy(
            src_ref=s.at[2 * chip[0] + chip[1]], dst_ref=o.at[j], send_sem=send_sems.at[j, i],
            recv_sem=recv_sems.at[j, i], device_id=(*chip, c), device_id_type=MESH)
            for j, chip in enumerate(_other_chips(x, y)) for i, (s, o) in enumerate(zip(srcs, outs))]

    return _each_copy(copies, parts, [pltpu.HBM((3,) + p.shape[1:], BF) for p in parts],
                      [pltpu.SemaphoreType.DMA((3, n)), pltpu.SemaphoreType.DMA((3, n))])


def _add_chips_group(tag, kc_idx, parts, arrived):
    n = len(parts)

    def body(kc_ref, *refs):
        for p, rx, t in zip(refs[:n], refs[n:2 * n], refs[2 * n:]):
            t[...] = ((p[...] + rx[0].astype(F32)) + rx[1].astype(F32)) + rx[2].astype(F32)

    outs = _pallas_call(
        body, name="add_chips_" + tag,
        grid_spec=pltpu.PrefetchScalarGridSpec(
            num_scalar_prefetch=1, grid=(1,),
            in_specs=([pl.BlockSpec(rx.shape[1:], lambda i, kc_ref: (0, 0)) for rx in arrived]
                      + [pl.BlockSpec(rx.shape, lambda i, kc_ref: (0, 0, 0)) for rx in arrived]),
            out_specs=[pl.BlockSpec((None,) + rx.shape[1:], lambda i, kc_ref: (kc_ref[1], 0, 0)) for rx in arrived]),
        out_shape=[pltpu.HBM((2,) + rx.shape[1:], F32) for rx in arrived],
        compiler_params=_params(48),
    )(kc_idx, *_in_hbm(list(parts) + list(arrived)))
    return list(outs)


def _join_group(halves):
    n = len(halves)

    def copies(bufs, sems):
        send_sems, recv_sems = sems
        x, y, c = _mesh_pos()
        sib = (x, y, 1 - c)
        sends = [pltpu.make_async_remote_copy(src_ref=b.at[c], dst_ref=b.at[c], send_sem=send_sems.at[i],
                                              recv_sem=recv_sems.at[i], device_id=sib, device_id_type=MESH)
                 for i, b in enumerate(bufs)]
        landed = [pltpu.make_async_remote_copy(src_ref=b.at[c], dst_ref=b.at[1 - c], send_sem=send_sems.at[i],
                                               recv_sem=recv_sems.at[i], device_id=sib, device_id_type=MESH)
                  for i, b in enumerate(bufs)]
        return sends, landed

    def start(_, bufs, sems):
        for cp in copies(bufs, sems)[0]:
            cp.start()

    def finish(_, bufs, sems):
        sends, landed = copies(bufs, sems)
        for cp in landed:
            cp.wait_recv()
        for cp in sends:
            cp.wait_send()

    return _Carried(halves, [pltpu.HBM(h.shape, F32) for h in halves],
                    [pltpu.SemaphoreType.DMA((n,)), pltpu.SemaphoreType.DMA((n,))], start, finish,
                    {i: i for i in range(n)})


def _combine(carries):
    operands, out_shapes, sems, aliases, spans = [], [], [], {}, []
    for c in carries:
        aliases.update({len(operands) + i: len(out_shapes) + o for i, o in c.aliases.items()})
        spans.append((len(operands), len(out_shapes), len(sems)))
        operands += list(c.operands)
        out_shapes += list(c.out_shapes)
        sems += list(c.sems)

    def each(phase):
        def run(ins, outs, sem_refs):
            for c, (a, b, s) in zip(carries, spans):
                getattr(c, phase)(ins[a:a + len(c.operands)], outs[b:b + len(c.out_shapes)], sem_refs[s:s + len(c.sems)])
        return run

    return _Carried(operands, out_shapes, sems, each("start"), each("finish"), aliases)


def _allreduce_small(arrays, wire):
    n = len(arrays)
    halves = [(a.shape[0], a.shape[1] // 2) for a in arrays]

    def body(*refs):
        srcs, outs = refs[:n], refs[n:2 * n]
        mine_bufs, sib_bufs, chip_bufs, total_bufs = (refs[k * n:(k + 1) * n] for k in range(2, 6))
        send_sems, recv_sems, local_sems = refs[6 * n:]
        x, y, c = _mesh_pos()
        k0 = 2 * x + y
        sib = (x, y, 1 - c)

        def remote(src, dst, j, i, to):
            return pltpu.make_async_remote_copy(src_ref=src, dst_ref=dst, send_sem=send_sems.at[j, i],
                                                recv_sem=recv_sems.at[j, i], device_id=to, device_id_type=MESH)

        def cols(ref, i, core):
            return ref.at[:, pl.ds(pl.multiple_of(core * halves[i][1], LANE), halves[i][1])]

        swaps = [remote(cols(s, i, 1 - c), b, 0, i, sib) for i, (s, b) in enumerate(zip(srcs, sib_bufs))]
        own = [pltpu.make_async_copy(cols(s, i, c), m, local_sems.at[i]) for i, (s, m) in enumerate(zip(srcs, mine_bufs))]
        for cp in swaps + own:
            cp.start()
        for cp in swaps + own:
            cp.wait()
        for m, b, buf in zip(mine_bufs, sib_bufs, chip_bufs):
            buf[k0] = (m[...] + b[...]).astype(buf.dtype)
        chips = _other_chips(x, y)
        sends = [remote(buf.at[k0], buf.at[k0], 1 + j, i, (*chip, c))
                 for j, chip in enumerate(chips) for i, buf in enumerate(chip_bufs)]
        for cp in sends:
            cp.start()
        for j, chip in enumerate(chips):
            for i, buf in enumerate(chip_bufs):
                remote(buf.at[k0], buf.at[2 * chip[0] + chip[1]], 1 + j, i, (*chip, c)).wait_recv()
        for cp in sends:
            cp.wait_send()
        for t, buf in zip(total_bufs, chip_bufs):
            t[...] = ((buf[0].astype(F32) + buf[1].astype(F32)) + buf[2].astype(F32)) + buf[3].astype(F32)
        joins = [remote(t, cols(o, i, c), 4, i, sib) for i, (t, o) in enumerate(zip(total_bufs, outs))]
        keep = [pltpu.make_async_copy(t, cols(o, i, c), local_sems.at[i]) for i, (t, o) in enumerate(zip(total_bufs, outs))]
        for cp in joins + keep:
            cp.start()
        for i, (t, o) in enumerate(zip(total_bufs, outs)):
            remote(t, cols(o, i, 1 - c), 4, i, sib).wait_recv()
        for cp in joins:
            cp.wait_send()
        for cp in keep:
            cp.wait()

    specs = [_full(a.shape) for a in arrays]
    return _pallas_call(
        body, name="allreduce_small", grid=(1,), in_specs=specs, out_specs=specs,
        out_shape=[_sds(a.shape) for a in arrays],
        scratch_shapes=([pltpu.VMEM(h, F32) for h in halves] + [pltpu.VMEM(h, F32) for h in halves]
                        + [pltpu.VMEM((NCHIP,) + h, dt) for h, dt in zip(halves, wire)] + [pltpu.VMEM(h, F32) for h in halves]
                        + [pltpu.SemaphoreType.DMA((5, n)), pltpu.SemaphoreType.DMA((5, n)), pltpu.SemaphoreType.DMA((n,))]),
        compiler_params=_params(32),
    )(*arrays)


def _adamw_terms(w, g, m, v):
    m = ADAM_B1 * m + (1.0 - ADAM_B1) * g
    v = ADAM_B2 * v + (1.0 - ADAM_B2) * jnp.square(g)
    m_hat = m / (1.0 - ADAM_B1 ** ADAM_STEP)
    v_hat = v / (1.0 - ADAM_B2 ** ADAM_STEP)
    return -ADAM_LR * (m_hat / (jnp.sqrt(v_hat) + ADAM_EPS) + ADAM_WD * w), m, v


ADAM_STEPS = 4


def _adamw_group(tag, ws, gs, ms, vs):
    n = len(ws)

    def body(*refs):
        ins, outs = refs[:4 * n], refs[4 * n:]
        for i in range(n):
            w, g, m, v = (ins[k * n + i][...] for k in range(4))
            outs[i][...] = g
            outs[n + i][...], outs[2 * n + i][...], outs[3 * n + i][...] = _adamw_terms(w, g, m, v)

    specs = [pl.BlockSpec((w.shape[0] // ADAM_STEPS, w.shape[1]), lambda i: (i, 0)) for w in ws]
    outs = _pallas_call(
        body, name="adamw_" + tag, grid=(ADAM_STEPS,), in_specs=specs * 4, out_specs=specs * 4,
        out_shape=[_sds(w.shape) for w in ws] * 4, compiler_params=_params(48),
    )(*_in_hbm(list(ws) + list(gs) + list(ms) + list(vs)))
    return outs[:n], outs[n:2 * n], outs[2 * n:3 * n], outs[3 * n:]


def _adamw_replicated(sums, row_of, direct):
    ns, nr, nd = len(sums), len(row_of), len(direct)

    def body(*refs):
        sum_refs = refs[:ns]
        ins = refs[ns:ns + 3 * nr + 4 * nd]
        outs = refs[ns + 3 * nr + 4 * nd:]
        for i, (_, _, _, si, row) in enumerate(row_of):
            w_ref, m_ref, v_ref = ins[3 * i:3 * i + 3]
            g = sum_refs[si][row:row + 1, :]
            outs[4 * i][...] = g
            outs[4 * i + 1][...], outs[4 * i + 2][...], outs[4 * i + 3][...] = _adamw_terms(w_ref[...], g, m_ref[...], v_ref[...])
        for i in range(nd):
            w_ref, m_ref, v_ref, g_ref = ins[3 * nr + 4 * i:3 * nr + 4 * i + 4]
            o = outs[4 * (nr + i):4 * (nr + i) + 4]
            g = g_ref[...]
            o[0][...] = g
            o[1][...], o[2][...], o[3][...] = _adamw_terms(w_ref[...], g, m_ref[...], v_ref[...])

    operands = list(sums)
    shapes = []
    for w, m, v, _, _ in row_of:
        operands += [w, m, v]
        shapes += [w.shape] * 4
    for w, m, v, g in direct:
        operands += [w, m, v, g]
        shapes += [w.shape] * 4
    flat = _pallas_call(
        body, name="adamw_replicated", grid=(1,), in_specs=[_full(a.shape) for a in operands],
        out_specs=[_full(s) for s in shapes], out_shape=[_sds(s) for s in shapes],
        compiler_params=_params(56),
    )(*operands)
    return [flat[4 * i:4 * i + 4] for i in range(nr + nd)]


class _Exchanges:
    def __init__(self, shards, conv_w, chip, core, apply):
        self.shards, self.conv_w, self.apply = shards, conv_w, apply
        self.active, self.calls = [], 0
        self.chip_core_idx = jnp.stack([chip, core]).astype(jnp.int32)

    def first(self):
        later = [n for n in self.shards if n != "w_in"]
        carry = _gather_group([self.shards["w_in"].astype(BF), self.conv_w], [True, False])
        outs = _to_bf16_group("gather_first", [self.shards[n] for n in later], carry)
        self.shards = dict(zip(later, outs))
        down = self.shards.pop("w_ffn_down")
        half = down.shape[0] // 2
        self.shards.update(w_ffn_down_lo=down[:half], w_ffn_down_hi=down[half:])
        return {"w_in": outs[len(later)], "conv_w": jnp.transpose(outs[len(later) + 1], (1, 0, 2)).reshape(4, LW)}

    def gather(self, names):
        return _gather_group([self.shards[n] for n in names], [True] * len(names))

    def reduce(self, tag, grads):
        self.active.append({"tag": tag, "names": list(grads), "stage": 0, "grads": list(grads.values())})

    def run(self, call, hold=()):
        groups = [g for g in self.active if g["tag"] not in hold]
        carries = [self._exchange_of(g) for g in groups]
        carry = _combine(carries)
        outs = list(call(carry))
        own = len(outs) - len(carry.out_shapes)
        landed = outs[own:]
        for g, c in zip(groups, carries):
            self._sum_after(g, landed[:len(c.out_shapes)])
            landed = landed[len(c.out_shapes):]
        self.active = [g for g in self.active if g["stage"] < 3]
        return outs[:own]

    def _exchange_of(self, g):
        if g["stage"] == 0:
            return _swap_group(g["grads"])
        if g["stage"] == 1:
            return _exchange_group(g["bf16"])
        return _join_group(g["halves"])

    def _sum_after(self, g, landed):
        if g["stage"] == 0:
            g["f32"], g["bf16"] = _add_sibling_group(g["tag"], self.chip_core_idx, g["grads"], landed)
        elif g["stage"] == 1:
            g["halves"] = _add_chips_group(g["tag"], self.chip_core_idx, g["f32"], landed)
        else:
            self.apply(g["tag"], g["names"], [t.reshape(2 * t.shape[1], t.shape[2]) for t in landed])
        g["stage"] += 1

    def drain(self):
        while self.active:
            self.calls += 1
            self.run(lambda carry: _run_now("reduce_%d" % self.calls, carry))


INPUT_NAMES = (["x", "p"] + [n for n in
               ["g_mix", "w_in", "b_in", "lam_re", "lam_im", "log_dt", "s5_b_re", "s5_b_im", "s5_c_re", "s5_c_im", "s5_d",
                "w_glu", "b_glu", "conv_w", "conv_b", "w_r", "b_r", "w_i", "b_i", "lru_lambda", "w_a_out", "w_b_out", "w_o",
                "g_ffn", "w_ffn_gate", "w_ffn_up", "w_ffn_down", "g_ple_gate", "w_ple_gate", "b_ple_gate", "w_ple", "g_ple",
                "g_final"]])
WEIGHT_NAMES = INPUT_NAMES[2:]


def kernel(*args):
    names = INPUT_NAMES + ["loss_target"] + ["m_" + n for n in WEIGHT_NAMES] + ["v_" + n for n in WEIGHT_NAMES]
    assert len(args) == len(names)
    given = dict(zip(names, args))

    def view(name):
        a = given[name]
        return jnp.swapaxes(a, -1, -2) if name.endswith(TRANSPOSED) else a

    def unview(name, a):
        return jnp.swapaxes(a, -1, -2) if name in TRANSPOSED else a

    def local(name):
        return view(name) if name.endswith("g_final") else view(name)[0]

    xi, yi, ci = _mesh_pos()
    k0 = 2 * xi + yi
    x, p, tgt = given["x"][0], given["p"][0, 0], given["loss_target"][0]

    results = {}

    row_halves = {}

    def apply(tag, names, totals):
        totals = dict(zip(names, totals))
        row_halves.update({n: totals.pop(n) for n in names if n in ("w_in_lo", "w_in_hi")})
        if len(row_halves) == 2:
            totals["w_in"] = jnp.concatenate([row_halves.pop("w_in_lo"), row_halves.pop("w_in_hi")])
        names = list(totals)
        if not names:
            return
        new = _adamw_group(tag, [local(n) for n in names], list(totals.values()), [local("m_" + n) for n in names],
                           [local("v_" + n) for n in names])
        for kind, arrays in zip(("grad", "delta", "new_m", "new_v"), new):
            for n, arr in zip(names, arrays):
                results[kind, n] = unview(n, arr[None])

    comm = _Exchanges({n: local(n) for n, _ in SHARDED}, local("conv_w"), k0, ci, apply)
    w = {n: local(n) for n in WEIGHT_NAMES if n != "conv_w" and n not in dict(SHARDED)}
    gx, sums, blocks = _local_step(x, p, tgt, w, comm)

    sum_names, block_names = list(sums), list(blocks)
    red = _allreduce_small([sums[n] for n in sum_names] + [blocks[n] for n in block_names],
                           [F32] * len(sum_names) + [BF] * len(block_names))
    sums = dict(zip(sum_names, red[:len(sum_names)]))
    blocks = dict(zip(block_names, red[len(sum_names):]))
    loss = jnp.sum(sums[LOSS_ROW[0]][LOSS_ROW[1]])
    direct_g = _replicated_grads(w, sums, blocks)
    conv_rows = sums[CONV_W_ROWS[0]][CONV_W_ROWS[1]:CONV_W_ROWS[1] + 4]
    direct_g["conv_w"] = lax.dynamic_slice(conv_rows, (0, k0 * CONV_SHARD[1]), CONV_SHARD)
    as_row = lambda a: a.reshape(1, -1)
    row_names = list(ACC_ROWS)
    row_of = [(as_row(given[n]), as_row(given["m_" + n]), as_row(given["v_" + n]),
               sum_names.index(ACC_ROWS[n][0]), ACC_ROWS[n][1]) for n in row_names]
    direct_names = list(direct_g)
    direct = [(view(n), view("m_" + n), view("v_" + n), direct_g[n].reshape(view(n).shape)) for n in direct_names]
    done = _adamw_replicated([sums[n] for n in sum_names], row_of, direct)
    for n, four in zip(row_names + direct_names, done):
        for kind, arr in zip(("grad", "delta", "new_m", "new_v"), four):
            results[kind, n] = unview(n, arr).reshape(given[n].shape)

    out = [loss, gx[None]]
    for kind in ("grad", "delta", "new_m", "new_v"):
        out += [results[kind, n] for n in WEIGHT_NAMES]
    return tuple(out)
```

```python
import functools
import math

import jax
import jax.numpy as jnp
from jax import lax
from jax.experimental import pallas as pl
from jax.experimental.pallas import tpu as pltpu

F32 = jnp.float32
BF = jnp.bfloat16

D = 1024
S5W = 512
NG, NS, NP = 32, 64, 16
GN = NG * NS
LW = 1024
NH, HD = 16, 64
LRU_C = 8.0
FH = 2816
NCHIP = 4
FC = FH // NCHIP
PLE = 256
INC = S5W + LW + 2 * D
EPS = 1e-6
ADAM_LR, ADAM_B1, ADAM_B2, ADAM_EPS, ADAM_WD, ADAM_STEP = 0.001, 0.9, 0.999, 1e-08, 0.01, 10

TM = 256
TK = 1024
LC = 512
SUB = 8
VMEM_MB = 1024 * 1024
MESH = pl.DeviceIdType.MESH
ANY = pl.BlockSpec(memory_space=pl.ANY)


def _mm(a, b):
    return jnp.dot(a.astype(BF), b.astype(BF), preferred_element_type=F32)


def _mm_nt(a, b):
    return lax.dot_general(a.astype(BF), b.astype(BF), (((1,), (1,)), ((), ())), preferred_element_type=F32)


def _mm_tn(a, b):
    return lax.dot_general(a.astype(BF), b.astype(BF), (((0,), (0,)), ((), ())), preferred_element_type=F32)


def _blockdiag_mm(x, blocks_ref):
    n, rows, _ = blocks_ref.shape
    return jnp.concatenate([jnp.dot(x[:, j * rows:(j + 1) * rows], blocks_ref[j], preferred_element_type=F32)
                            for j in range(n)], axis=1)


def _blockdiag_mm_t(x, blocks_ref):
    n, _, wide = blocks_ref.shape
    return jnp.concatenate([lax.dot_general(x[:, j * wide:(j + 1) * wide], blocks_ref[j], (((1,), (1,)), ((), ())),
                                            preferred_element_type=F32) for j in range(n)], axis=1)


def _rms(x):
    r = lax.rsqrt(jnp.mean(x * x, axis=-1, keepdims=True) + EPS)
    return x * r, r


def _rms_bwd(dy, xh, r, g):
    dxh = dy * g
    return r * (dxh - xh * jnp.mean(dxh * xh, axis=-1, keepdims=True))


def _colsum(x):
    return jnp.sum(x, axis=0, keepdims=True)


def _sig(x):
    return jax.nn.sigmoid(x)


def _gelu_grad(x):
    c = math.sqrt(2.0 / math.pi)
    t = jnp.tanh(c * (x + 0.044715 * x * x * x))
    return 0.5 * (1.0 + t) + 0.5 * x * (1.0 - t * t) * c * (1.0 + 3.0 * 0.044715 * x * x)


def _neg_expm1(x):
    series = -x * (1.0 + x * (0.5 + x * (1.0 / 6.0 + x * (1.0 / 24.0))))
    return jnp.where(x > -0.03, series, 1.0 - jnp.exp(x))


def _tok(width):
    return pl.BlockSpec((TM, width), lambda i: (i, 0))


def _tok_rev(width, nt):
    return pl.BlockSpec((TM, width), lambda i: (nt - 1 - i, 0))


def _full(shape):
    return pl.BlockSpec(shape, lambda i: (0,) * len(shape))


def _params(vmem_mb, **kw):
    return pltpu.CompilerParams(dimension_semantics=("arbitrary",), vmem_limit_bytes=vmem_mb * VMEM_MB, **kw)


def _sds(shape, dtype=F32):
    return jax.ShapeDtypeStruct(shape, dtype)


def _far(shape, dtype=F32):
    return pltpu.HBM(shape, dtype)


class _Carried:
    def __init__(self, operands, out_shapes, sems, start, finish, aliases=None):
        self.operands, self.out_shapes, self.sems = list(operands), list(out_shapes), list(sems)
        self.start, self.finish, self.aliases = start, finish, dict(aliases or {})


def _in_hbm(arrays):
    return [pltpu.with_memory_space_constraint(a, pltpu.HBM) for a in arrays]


def _pallas_call(body, carry=None, **kw):
    if carry is None:
        return pl.pallas_call(body, **kw)

    def at_step(corner):
        hit = [pl.program_id(d) == (size - 1 if corner else 0) for d, size in enumerate(kw["grid"])]
        return functools.reduce(jnp.logical_and, hit)

    name, grid, compiler_params = kw["name"], kw["grid"], kw["compiler_params"]
    in_specs, out_specs, out_shape = list(kw["in_specs"]), list(kw["out_specs"]), list(kw["out_shape"])
    scratch_shapes = list(kw.get("scratch_shapes", ()))
    n_in, n_out, n_scr = len(in_specs), len(out_specs), len(scratch_shapes)
    c_in, c_out = len(carry.operands), len(carry.out_shapes)

    def full_body(*refs):
        ins, refs = refs[:n_in], refs[n_in:]
        c_ins, refs = refs[:c_in], refs[c_in:]
        outs, refs = refs[:n_out], refs[n_out:]
        c_outs, refs = refs[:c_out], refs[c_out:]
        scratch, c_sems = refs[:n_scr], refs[n_scr:]

        @pl.when(at_step(0))
        def _():
            carry.start(c_ins, c_outs, c_sems)

        body(*ins, *outs, *scratch)

        @pl.when(at_step(1))
        def _():
            carry.finish(c_ins, c_outs, c_sems)

    call = pl.pallas_call(
        full_body, name=name, grid=grid, in_specs=in_specs + [ANY] * c_in, out_specs=out_specs + [ANY] * c_out,
        out_shape=out_shape + list(carry.out_shapes), scratch_shapes=scratch_shapes + list(carry.sems),
        input_output_aliases={n_in + i: n_out + o for i, o in carry.aliases.items()},
        compiler_params=compiler_params)
    return lambda *operands: call(*operands, *_in_hbm(carry.operands))


def _resident(pairs, sems):
    first = pl.program_id(0) == 0
    copies = [pltpu.make_async_copy(src, dst, sems.at[j]) for j, (src, dst) in enumerate(pairs)]

    @pl.when(first)
    def _():
        for cp in copies:
            cp.start()

    def wait(j):
        @pl.when(first)
        def _():
            copies[j].wait()

    return wait


def _resident_now(pairs, sems):
    @pl.when(pl.program_id(0) == 0)
    def _():
        copies = [pltpu.make_async_copy(src, dst, sems.at[j]) for j, (src, dst) in enumerate(pairs)]
        for cp in copies:
            cp.start()
        for cp in copies:
            cp.wait()


def _row_iota(width):
    return lax.broadcasted_iota(jnp.int32, (SUB, width), 0)


def _bcast_row(x, row):
    return jnp.broadcast_to(x[row:row + 1, :], x.shape)


def _slab(k):
    return pl.ds(pl.multiple_of(k * SUB, SUB), SUB)


QC = INC // NCHIP
Z_PARTS = ((0, S5W), (S5W, S5W + LW), (S5W + LW, INC))


def _inproj_fwd(x, g_mix, w_in, b_in, carry=None):
    L = x.shape[0]

    def body(x_ref, g_ref, w_hbm, b_ref, h_ref, ua_ref, ub_ref, gp_ref, w_vm, w_sems):
        _resident_now([(w_hbm.at[k], w_vm.at[k]) for k in range(NCHIP)], w_sems)
        xh, _ = _rms(x_ref[...])
        h = (xh * g_ref[...]).astype(BF)
        h_ref[...] = h
        for k in range(NCHIP):
            lo, hi = k * QC, (k + 1) * QC
            z = jnp.dot(h, w_vm[k], preferred_element_type=F32) + b_ref[:, lo:hi]
            for ref, (a, b) in zip((ua_ref, ub_ref, gp_ref), Z_PARTS):
                s, e = max(lo, a), min(hi, b)
                if s < e:
                    ref[:, s - a:e - a] = z[:, s - lo:e - lo]

    return _pallas_call(
        body, carry, name="inproj_fwd", grid=(L // TM,),
        in_specs=[_tok(D), _full((1, D)), ANY, _full((1, INC))],
        out_specs=[_tok(D), _tok(S5W), _tok(LW), _tok(2 * D)],
        out_shape=[_far((L, D), BF), _far((L, S5W)), _far((L, LW)), _sds((L, 2 * D))],
        scratch_shapes=[pltpu.VMEM((NCHIP, D, QC), BF), pltpu.SemaphoreType.DMA((NCHIP,))],
        compiler_params=_params(40),
    )(*_in_hbm([x]), g_mix, *_in_hbm([w_in]), b_in)


def _inproj_bwd(x, dx1, dua, dub, dgp, g_mix, w_in, carry=None):
    L = x.shape[0]

    def body(x_ref, dx1_ref, dua_ref, dub_ref, dgp_ref, g_ref, w_hbm, gx_ref, dz_ref, dg_ref, db_ref, w_vm, w_sems):
        _resident_now([(w_hbm.at[k], w_vm.at[k]) for k in range(NCHIP)], w_sems)

        @pl.when(pl.program_id(0) == 0)
        def _():
            dg_ref[...] = jnp.zeros_like(dg_ref)
            db_ref[...] = jnp.zeros_like(db_ref)

        for src, (a, b) in zip((dua_ref, dub_ref, dgp_ref), Z_PARTS):
            d = src[...]
            dz_ref[:, a:b] = d.astype(BF)
            db_ref[0:1, a:b] += _colsum(d)
        dh = jnp.zeros((TM, D), F32)
        for k in range(NCHIP):
            dh = dh + lax.dot_general(dz_ref[:, k * QC:(k + 1) * QC], w_vm[k], (((1,), (1,)), ((), ())),
                                      preferred_element_type=F32)
        xh, r = _rms(x_ref[...])
        dg_ref[0:1, :] += _colsum(dh * xh)
        gx_ref[...] = dx1_ref[...] + _rms_bwd(dh, xh, r, g_ref[...])

    return _pallas_call(
        body, carry, name="inproj_bwd", grid=(L // TM,),
        in_specs=[_tok(D), _tok(D), _tok(S5W), _tok(LW), _tok(2 * D), _full((1, D)), ANY],
        out_specs=[_tok(D), _tok(INC), _full((SUB, D)), _full((SUB, INC))],
        out_shape=[_sds((L, D)), _sds((L, INC), BF), _sds((SUB, D)), _sds((SUB, INC))],
        scratch_shapes=[pltpu.VMEM((NCHIP, D, QC), BF), pltpu.SemaphoreType.DMA((NCHIP,))],
        compiler_params=_params(40),
    )(x, dx1, *_in_hbm([dua]), dub, dgp, g_mix, *_in_hbm([w_in]))


def _cscan(xr_ref, xi_ref, con_ref, cr_ref, ci_ref, reverse):
    n_slab = xr_ref.shape[0] // SUB
    width = xr_ref.shape[1]
    for lc in range(width // LC):
        cols = slice(lc * LC, (lc + 1) * LC)
        con = [con_ref[SUB * j:SUB * (j + 1), cols] for j in range(8)]

        def step(k, carry, cols=cols, con=con):
            cr, ci = carry
            rows = _slab(n_slab - 1 - k if reverse else k)
            xr, xi = xr_ref[rows, cols], xi_ref[rows, cols]
            for j, sh in enumerate((1, 2, 4)):
                mr, mi = con[2 * j], con[2 * j + 1]
                pr = pltpu.roll(xr, SUB - sh if reverse else sh, 0)
                pi = pltpu.roll(xi, SUB - sh if reverse else sh, 0)
                xr, xi = xr + mr * pr - mi * pi, xi + mr * pi + mi * pr
            xr, xi = xr + con[6] * cr - con[7] * ci, xi + con[6] * ci + con[7] * cr
            xr_ref[rows, cols] = xr
            xi_ref[rows, cols] = xi
            row = 0 if reverse else SUB - 1
            return _bcast_row(xr, row), _bcast_row(xi, row)

        cr, ci = lax.fori_loop(0, n_slab, step, (cr_ref[:, cols], ci_ref[:, cols]))
        cr_ref[:, cols] = cr
        ci_ref[:, cols] = ci


def _s5_fwd(ua, bbr, bbi, ccr, cci, dsk, con, w_glu, b_glu, carry=None):
    L = ua.shape[0]

    def body(ua_ref, bbr_hbm, bbi_hbm, ccr_hbm, cci_hbm, dsk_ref, con_ref, wg_ref, bg_ref,
             sr_ref, si_ref, y_ref, zg_ref, ya_ref, bbr_vm, bbi_vm, ccr_vm, cci_vm, cr_ref, ci_ref, w_sems):
        landed = _resident([(bbr_hbm, bbr_vm), (bbi_hbm, bbi_vm), (ccr_hbm, ccr_vm), (cci_hbm, cci_vm)], w_sems)

        @pl.when(pl.program_id(0) == 0)
        def _():
            cr_ref[...] = jnp.zeros_like(cr_ref)
            ci_ref[...] = jnp.zeros_like(ci_ref)

        u = ua_ref[...]
        ub = u.astype(BF)
        landed(0)
        sr_ref[...] = _blockdiag_mm(ub, bbr_vm)
        landed(1)
        si_ref[...] = _blockdiag_mm(ub, bbi_vm)
        _cscan(sr_ref, si_ref, con_ref, cr_ref, ci_ref, reverse=False)
        landed(2)
        landed(3)
        y = (_blockdiag_mm_t(sr_ref[...].astype(BF), ccr_vm) - _blockdiag_mm_t(si_ref[...].astype(BF), cci_vm)
             + dsk_ref[...] * u)
        y_ref[...] = y
        zg = jax.nn.gelu(y)
        zg_ref[...] = zg.astype(BF)
        q = _mm(zg, wg_ref[...]) + bg_ref[...]
        ya_ref[...] = (zg * _sig(q)).astype(BF)

    return _pallas_call(
        body, carry, name="s5_fwd", grid=(L // TM,),
        in_specs=[_tok(S5W), ANY, ANY, ANY, ANY, _full((1, S5W)), _full((8 * SUB, GN)),
                  _full((S5W, S5W)), _full((1, S5W))],
        out_specs=[_tok(GN), _tok(GN), _tok(S5W), _tok(S5W), _tok(S5W)],
        out_shape=[_sds((L, GN)), _sds((L, GN)), _far((L, S5W)), _far((L, S5W), BF), _far((L, S5W), BF)],
        scratch_shapes=[pltpu.VMEM((S5W // 128, 128, GN // (S5W // 128)), BF)] * 4 + [
                        pltpu.VMEM((SUB, GN), F32), pltpu.VMEM((SUB, GN), F32),
                        pltpu.SemaphoreType.DMA((4,))],
        compiler_params=_params(44),
    )(*_in_hbm([ua, bbr, bbi, ccr, cci]), dsk, con, w_glu, b_glu)


def _s5_bwd(dya, y, ua, sr, si, bbr, bbi, ccr, cci, dsk, con_rev, w_glu, b_glu, carry=None):
    L = ua.shape[0]
    nt = L // TM
    spt = TM // SUB
    n_slab = spt

    def halo_map(i):
        return (jnp.maximum((nt - 1 - i) * spt - 1, 0), 0)

    def body(dya_ref, y_ref, ua_ref, sr_ref, si_ref, hr_ref, hi_ref, bbr_hbm, bbi_hbm, ccr_hbm, cci_hbm,
             dsk_ref, con_ref, wg_ref, bg_ref,
             dua_ref, dq_ref, dy_ref, lr_ref, li_ref, da_ref, dsm_ref,
             bbr_vm, bbi_vm, ccr_vm, cci_vm, cr_ref, ci_ref, w_sems):
        i = pl.program_id(0)
        landed = _resident([(ccr_hbm, ccr_vm), (cci_hbm, cci_vm), (bbr_hbm, bbr_vm), (bbi_hbm, bbi_vm)], w_sems)

        @pl.when(i == 0)
        def _():
            cr_ref[...] = jnp.zeros_like(cr_ref)
            ci_ref[...] = jnp.zeros_like(ci_ref)
            da_ref[...] = jnp.zeros_like(da_ref)
            dsm_ref[...] = jnp.zeros_like(dsm_ref)

        u = ua_ref[...]
        yv = y_ref[...]
        dya = dya_ref[...]
        zg = jax.nn.gelu(yv)
        sg = _sig(_mm(zg, wg_ref[...]) + bg_ref[...])
        dq = dya * zg * sg * (1.0 - sg)
        dq_ref[...] = dq.astype(BF)
        dzg = dya * sg + _mm_nt(dq, wg_ref[...])
        dy = dzg * _gelu_grad(yv)
        dyb = dy.astype(BF)
        dy_ref[...] = dyb
        dsm_ref[0:1, :] += _colsum(dy * u)
        dsm_ref[1:2, :] += _colsum(dq)
        landed(0)
        lr_ref[...] = _blockdiag_mm(dyb, ccr_vm)
        landed(1)
        li_ref[...] = -_blockdiag_mm(dyb, cci_vm)
        _cscan(lr_ref, li_ref, con_ref, cr_ref, ci_ref, reverse=True)

        first_tile = (i == nt - 1)
        row = _row_iota(LC)
        for lc in range(GN // LC):
            cols = slice(lc * LC, (lc + 1) * LC)
            h_r = jnp.where(first_tile, 0.0, hr_ref[:, cols])
            h_i = jnp.where(first_tile, 0.0, hi_ref[:, cols])

            def step(k, acc, cols=cols, h_r=h_r, h_i=h_i):
                ar, ai = acc
                rows = _slab(k)
                prev = _slab(jnp.maximum(k - 1, 0))
                pr = jnp.where(k == 0, h_r, sr_ref[prev, cols])
                pi = jnp.where(k == 0, h_i, si_ref[prev, cols])
                spr = pltpu.roll(jnp.where(row == SUB - 1, pr, sr_ref[rows, cols]), 1, 0)
                spi = pltpu.roll(jnp.where(row == SUB - 1, pi, si_ref[rows, cols]), 1, 0)
                lr, li = lr_ref[rows, cols], li_ref[rows, cols]
                return ar + lr * spr + li * spi, ai + li * spr - lr * spi

            zero = jnp.zeros((SUB, LC), F32)
            ar, ai = lax.fori_loop(0, n_slab, step, (zero, zero))
            da_ref[0:1, cols] += _colsum(ar)
            da_ref[1:2, cols] += _colsum(ai)

        landed(2)
        landed(3)
        dua_ref[...] = (dy * dsk_ref[...] + _blockdiag_mm_t(lr_ref[...].astype(BF), bbr_vm)
                        + _blockdiag_mm_t(li_ref[...].astype(BF), bbi_vm))

    return _pallas_call(
        body, carry, name="s5_bwd", grid=(nt,),
        in_specs=[_tok_rev(S5W, nt), _tok_rev(S5W, nt), _tok_rev(S5W, nt), _tok_rev(GN, nt), _tok_rev(GN, nt),
                  pl.BlockSpec((SUB, GN), halo_map), pl.BlockSpec((SUB, GN), halo_map),
                  ANY, ANY, ANY, ANY, _full((1, S5W)), _full((8 * SUB, GN)), _full((S5W, S5W)), _full((1, S5W))],
        out_specs=[_tok_rev(S5W, nt), _tok_rev(S5W, nt), _tok_rev(S5W, nt), _tok_rev(GN, nt), _tok_rev(GN, nt),
                   _full((SUB, GN)), _full((SUB, S5W))],
        out_shape=[_sds((L, S5W)), _sds((L, S5W), BF), _sds((L, S5W), BF), _sds((L, GN)), _sds((L, GN)),
                   _sds((SUB, GN)), _sds((SUB, S5W))],
        scratch_shapes=[pltpu.VMEM((S5W // 128, 128, GN // (S5W // 128)), BF)] * 4 + [
                        pltpu.VMEM((SUB, GN), F32), pltpu.VMEM((SUB, GN), F32),
                        pltpu.SemaphoreType.DMA((4,))],
        compiler_params=_params(52),
    )(dya, y, ua, sr, si, sr, si, *_in_hbm([bbr, bbi, ccr, cci]), dsk, con_rev, w_glu, b_glu)


def _lru_gate_terms(rg, sp):
    log_a = -LRU_C * rg * sp
    a = jnp.exp(log_a)
    mult = jnp.sqrt(_neg_expm1(2.0 * log_a))
    return a, mult


def _lru_fwd(ub, conv_w, conv_b, wr, wi, b_r, b_i, sp, carry=None):
    L = ub.shape[0]
    n_slab = TM // SUB

    def body(ub_ref, cw_ref, cb_ref, wr_ref, wi_ref, br_ref, bi_ref, sp_ref,
             xc_ref, rg_ref, ig_ref, h_ref, hp_ref, a_ref, halo_ref, carry_ref):
        @pl.when(pl.program_id(0) == 0)
        def _():
            halo_ref[...] = jnp.zeros_like(halo_ref)
            carry_ref[...] = jnp.zeros_like(carry_ref)

        row = _row_iota(LW)
        taps = [cw_ref[k:k + 1, :] for k in range(4)]
        cb = cb_ref[...]

        def conv_step(k, prev):
            rows = _slab(k)
            cur = ub_ref[rows, :]
            acc = taps[3] * cur + cb
            for j in (1, 2, 3):
                acc = acc + taps[3 - j] * pltpu.roll(jnp.where(row >= SUB - j, prev, cur), j, 0)
            xc_ref[rows, :] = acc
            return cur

        halo_ref[...] = lax.fori_loop(0, n_slab, conv_step, halo_ref[...])

        xc = xc_ref[...]
        xcb = xc.astype(BF)
        rg = _sig(_blockdiag_mm(xcb, wr_ref) + br_ref[...])
        ig = _sig(_blockdiag_mm(xcb, wi_ref) + bi_ref[...])
        rg_ref[...] = rg
        ig_ref[...] = ig
        a, mult = _lru_gate_terms(rg, sp_ref[...])
        a_ref[...] = a
        h_ref[...] = mult * ig * xc

        rowc = _row_iota(LC)
        for lc in range(LW // LC):
            cols = slice(lc * LC, (lc + 1) * LC)

            def step(k, c, cols=cols):
                rows = _slab(k)
                av, b = a_ref[rows, cols], h_ref[rows, cols]
                for sh in (1, 2, 4):
                    keep = rowc >= sh
                    b = b + av * jnp.where(keep, pltpu.roll(b, sh, 0), 0.0)
                    av = av * jnp.where(keep, pltpu.roll(av, sh, 0), 1.0)
                h = b + av * c
                h_ref[rows, cols] = h
                hp_ref[rows, cols] = jnp.where(rowc == 0, c, pltpu.roll(h, 1, 0))
                return _bcast_row(h, SUB - 1)

            carry_ref[:, cols] = lax.fori_loop(0, n_slab, step, carry_ref[:, cols])

    return _pallas_call(
        body, carry, name="lru_fwd", grid=(L // TM,),
        in_specs=[_tok(LW), _full((4, LW)), _full((1, LW)), _full((LW // 128, 128, 128)), _full((LW // 128, 128, 128)),
                  _full((1, LW)), _full((1, LW)), _full((1, LW))],
        out_specs=[_tok(LW)] * 5,
        out_shape=[_far((L, LW))] * 5,
        scratch_shapes=[pltpu.VMEM((TM, LW), F32), pltpu.VMEM((SUB, LW), F32), pltpu.VMEM((SUB, LW), F32)],
        compiler_params=_params(40),
    )(*_in_hbm([ub]), conv_w, conv_b, wr, wi, b_r, b_i, sp)


def _lru_bwd(dyb, xc, rg, ig, hp, ub, conv_w, wr, wi, sp, dsp, carry=None):
    L = ub.shape[0]
    nt = L // TM
    spt = TM // SUB
    n_slab = spt

    def halo_map(i):
        return (jnp.maximum((nt - 1 - i) * spt - 1, 0), 0)

    def body(dh_ref, xc_ref, rg_ref, ig_ref, hp_ref, ub_ref, uh_ref, cw_ref, wr_ref, wi_ref, sp_ref, dsp_ref,
             dub_ref, dpr_ref, dpi_ref, acc_ref, a_ref, lam_ref, dxc_ref, carry_ref, next_ref):
        i = pl.program_id(0)

        @pl.when(i == 0)
        def _():
            carry_ref[...] = jnp.zeros_like(carry_ref)
            next_ref[...] = jnp.zeros_like(next_ref)
            acc_ref[...] = jnp.zeros_like(acc_ref)

        sp = sp_ref[...]
        rg, ig, xc = rg_ref[...], ig_ref[...], xc_ref[...]
        a, mult = _lru_gate_terms(rg, sp)
        a_ref[...] = a

        rowc = _row_iota(LC)
        for lc in range(LW // LC):
            cols = slice(lc * LC, (lc + 1) * LC)

            def step(k, c, cols=cols):
                rows = _slab(n_slab - 1 - k)
                av, dh = a_ref[rows, cols], dh_ref[rows, cols]
                b = av * dh
                for sh in (1, 2, 4):
                    keep = rowc < SUB - sh
                    b = b + av * jnp.where(keep, pltpu.roll(b, SUB - sh, 0), 0.0)
                    av = av * jnp.where(keep, pltpu.roll(av, SUB - sh, 0), 1.0)
                mu = b + av * c
                lam_ref[rows, cols] = dh + jnp.where(rowc == SUB - 1, c, pltpu.roll(mu, SUB - 1, 0))
                return _bcast_row(mu, 0)

            carry_ref[:, cols] = lax.fori_loop(0, n_slab, step, carry_ref[:, cols])

        lam = lam_ref[...]
        d_a = lam * hp_ref[...]
        d_mult = lam * ig * xc
        d_ig = lam * mult * xc
        dxc = lam * mult * ig
        d_log_a = d_a * a - d_mult * a * a / mult
        d_rg = (-LRU_C) * sp * d_log_a
        acc_ref[0:1, :] += _colsum((-LRU_C) * rg * d_log_a) * dsp_ref[...]
        dpr = d_rg * rg * (1.0 - rg)
        dpi = d_ig * ig * (1.0 - ig)
        acc_ref[1:2, :] += _colsum(dpr)
        acc_ref[2:3, :] += _colsum(dpi)
        dprb, dpib = dpr.astype(BF), dpi.astype(BF)
        dpr_ref[...] = dprb
        dpi_ref[...] = dpib
        dxc = dxc + _blockdiag_mm_t(dprb, wr_ref) + _blockdiag_mm_t(dpib, wi_ref)
        dxc_ref[...] = dxc
        acc_ref[3:4, :] += _colsum(dxc)

        row = _row_iota(LW)
        taps = [cw_ref[k:k + 1, :] for k in range(4)]
        u_halo = jnp.where(i == nt - 1, 0.0, uh_ref[...])
        nxt_tile = next_ref[...]

        def conv_step(k, accs):
            rows = _slab(k)
            cur = dxc_ref[rows, :]
            nxt = jnp.where(k == n_slab - 1, nxt_tile, dxc_ref[_slab(jnp.minimum(k + 1, n_slab - 1)), :])
            ucur = ub_ref[rows, :]
            uprev = jnp.where(k == 0, u_halo, ub_ref[_slab(jnp.maximum(k - 1, 0)), :])
            du = taps[3] * cur
            new = [accs[3] + cur * ucur]
            for j in (1, 2, 3):
                du = du + taps[3 - j] * pltpu.roll(jnp.where(row < j, nxt, cur), SUB - j, 0)
                new.append(accs[3 - j] + cur * pltpu.roll(jnp.where(row >= SUB - j, uprev, ucur), j, 0))
            dub_ref[rows, :] = du
            return tuple(new[::-1])

        zero = jnp.zeros((SUB, LW), F32)
        accs = lax.fori_loop(0, n_slab, conv_step, (zero, zero, zero, zero))
        for k in range(4):
            acc_ref[4 + k:5 + k, :] += _colsum(accs[k])
        next_ref[...] = dxc_ref[0:SUB, :]

    return _pallas_call(
        body, carry, name="lru_bwd", grid=(nt,),
        in_specs=[_tok_rev(LW, nt)] * 6 + [pl.BlockSpec((SUB, LW), halo_map), _full((4, LW)),
                                           _full((LW // 128, 128, 128)), _full((LW // 128, 128, 128)), _full((1, LW)), _full((1, LW))],
        out_specs=[_tok_rev(LW, nt), _tok_rev(LW, nt), _tok_rev(LW, nt), _full((SUB, LW))],
        out_shape=[_sds((L, LW)), _far((L, LW), BF), _far((L, LW), BF), _sds((SUB, LW))],
        scratch_shapes=[pltpu.VMEM((TM, LW), F32), pltpu.VMEM((TM, LW), F32), pltpu.VMEM((TM, LW), F32),
                        pltpu.VMEM((SUB, LW), F32), pltpu.VMEM((SUB, LW), F32)],
        compiler_params=_params(48),
    )(dyb, xc, rg, ig, hp, ub, ub, conv_w, wr, wi, sp, dsp)


AC = D // NCHIP


def _merge_fwd(x, ya, yb, gp, w_a, w_b, w_o, carry=None):
    L = x.shape[0]

    def body(x_ref, ya_ref, yb_ref, gp_ref, wa_ref, wb_ref, wo_ref, x1_ref, pa_ref, pb_ref, mg_ref):
        ya = ya_ref[...]
        for k in range(NCHIP):
            pa_ref[:, k * AC:(k + 1) * AC] = jnp.dot(ya, wa_ref[k], preferred_element_type=F32)
        pb = _mm(yb_ref[...], wb_ref[...])
        pb_ref[...] = pb
        gp = gp_ref[...]
        merged = (_sig(gp[:, :D]) * pa_ref[...] + _sig(gp[:, D:]) * pb).astype(BF)
        mg_ref[...] = merged
        x1_ref[...] = x_ref[...] + jnp.dot(merged, wo_ref[...], preferred_element_type=F32)

    return _pallas_call(
        body, carry, name="merge_fwd", grid=(L // TM,),
        in_specs=[_tok(D), _tok(S5W), _tok(LW), _tok(2 * D), _full((NCHIP, S5W, AC)), _full((LW, D)), _full((D, D))],
        out_specs=[_tok(D), _tok(D), _tok(D), _tok(D)],
        out_shape=[_sds((L, D)), _sds((L, D)), _sds((L, D)), _far((L, D), BF)],
        compiler_params=_params(40),
    )(x, ya, yb, gp, w_a, w_b, w_o)


def _merge_bwd(dx1, gp, pa, pb, w_a, w_b, w_o, carry=None):
    L = dx1.shape[0]

    def body(dx1_ref, gp_ref, pa_ref, pb_ref, wa_ref, wb_ref, wo_ref, dya_ref, dyb_ref, dgp_ref, dpa_ref, dpb_ref):
        dm = _mm_nt(dx1_ref[...], wo_ref[...])
        gp = gp_ref[...]
        sa, sb = _sig(gp[:, :D]), _sig(gp[:, D:])
        dpa = (dm * sa).astype(BF)
        dpb = (dm * sb).astype(BF)
        dpa_ref[...] = dpa
        dpb_ref[...] = dpb
        dgp_ref[:, :D] = dm * pa_ref[...] * sa * (1.0 - sa)
        dgp_ref[:, D:] = dm * pb_ref[...] * sb * (1.0 - sb)
        dya = jnp.zeros((TM, S5W), F32)
        for k in range(NCHIP):
            dya = dya + _mm_nt(dpa[:, k * AC:(k + 1) * AC], wa_ref[k])
        dya_ref[...] = dya
        dyb_ref[...] = _mm_nt(dpb, wb_ref[...])

    return _pallas_call(
        body, carry, name="merge_bwd", grid=(L // TM,),
        in_specs=[_tok(D), _tok(2 * D), _tok(D), _tok(D), _full((NCHIP, S5W, AC)), _full((LW, D)), _full((D, D))],
        out_specs=[_tok(S5W), _tok(LW), _tok(2 * D), _tok(D), _tok(D)],
        out_shape=[_far((L, S5W)), _far((L, LW)), _sds((L, 2 * D)), _far((L, D), BF), _far((L, D), BF)],
        compiler_params=_params(40),
    )(dx1, gp, pa, pb, w_a, w_b, w_o)


def _chunk_tok(width):
    return pl.BlockSpec((NCHIP, TM, width), lambda i: (0, i, 0))


def _ffn_up_fwd(x1, g_ffn, wg, wu, carry=None):
    L = x1.shape[0]

    def body(x_ref, g_ref, wg_hbm, wu_hbm, h2_ref, gg_ref, uu_ref, wg_vm, wu_vm, w_sems):
        _resident_now([(src.at[c], dst.at[c]) for c in range(NCHIP) for src, dst in ((wg_hbm, wg_vm), (wu_hbm, wu_vm))],
                      w_sems)
        xh, _ = _rms(x_ref[...])
        h2 = (xh * g_ref[...]).astype(BF)
        h2_ref[...] = h2
        for c in range(NCHIP):
            gg_ref[c] = lax.dot_general(h2, wg_vm[c], (((1,), (1,)), ((), ())), preferred_element_type=F32).astype(BF)
            uu_ref[c] = lax.dot_general(h2, wu_vm[c], (((1,), (1,)), ((), ())), preferred_element_type=F32).astype(BF)

    return _pallas_call(
        body, carry, name="ffn_up_fwd", grid=(L // TM,),
        in_specs=[_tok(D), _full((1, D)), ANY, ANY],
        out_specs=[_tok(D), _chunk_tok(FC), _chunk_tok(FC)],
        out_shape=[_far((L, D), BF), _sds((NCHIP, L, FC), BF), _sds((NCHIP, L, FC), BF)],
        scratch_shapes=[pltpu.VMEM((NCHIP, FC, D), BF)] * 2 + [pltpu.SemaphoreType.DMA((2 * NCHIP,))],
        compiler_params=_params(44),
    )(x1, g_ffn, wg, wu)


def _ffn_down_fwd(x1, gg, uu, wd):
    L = x1.shape[0]

    def body(x_ref, gg_ref, uu_ref, wd_hbm, x2_ref, wd_vm, w_sems):
        _resident_now([(wd_hbm.at[c], wd_vm.at[c]) for c in range(NCHIP)], w_sems)
        out = x_ref[...]
        for c in range(NCHIP):
            g = gg_ref[c].astype(F32)
            act = (g * _sig(g) * uu_ref[c].astype(F32)).astype(BF)
            out = out + jnp.dot(act, wd_vm[c], preferred_element_type=F32)
        x2_ref[...] = out

    return _pallas_call(
        body, name="ffn_down_fwd", grid=(L // TM,),
        in_specs=[_tok(D), _chunk_tok(FC), _chunk_tok(FC), ANY], out_specs=[_tok(D)],
        out_shape=[_far((L, D))],
        scratch_shapes=[pltpu.VMEM((NCHIP, FC, D), BF), pltpu.SemaphoreType.DMA((NCHIP,))],
        compiler_params=_params(40),
    )(x1, gg, uu, *_in_hbm([wd]))[0]


def _ffn_bwd(x1, dx2, gg, uu, g_ffn, wg, wu, wd, carry=None):
    L = x1.shape[0]

    def body(x_ref, dx2_ref, gg_ref, uu_ref, g_ref, wg_hbm, wu_hbm, wd_hbm,
             dx1_ref, act_ref, dgg_ref, duu_ref, dg_ref, wg_vm, wu_vm, wd_vm, w_sems):
        _resident_now([(src.at[c], dst.at[c]) for c in range(NCHIP)
                       for src, dst in ((wg_hbm, wg_vm), (wu_hbm, wu_vm), (wd_hbm, wd_vm))], w_sems)

        @pl.when(pl.program_id(0) == 0)
        def _():
            dg_ref[...] = jnp.zeros_like(dg_ref)

        dx2 = dx2_ref[...]
        dx2b = dx2.astype(BF)
        dh2 = jnp.zeros((TM, D), F32)
        for c in range(NCHIP):
            g = gg_ref[c].astype(F32)
            u = uu_ref[c].astype(F32)
            s = _sig(g)
            silu = g * s
            act_ref[c] = (silu * u).astype(BF)
            dact = lax.dot_general(dx2b, wd_vm[c], (((1,), (1,)), ((), ())), preferred_element_type=F32)
            dg = (dact * u * s * (1.0 + g * (1.0 - s))).astype(BF)
            du = (dact * silu).astype(BF)
            dgg_ref[c] = dg
            duu_ref[c] = du
            dh2 = dh2 + jnp.dot(dg, wg_vm[c], preferred_element_type=F32)
            dh2 = dh2 + jnp.dot(du, wu_vm[c], preferred_element_type=F32)
        xh, r = _rms(x_ref[...])
        dg_ref[0:1, :] += _colsum(dh2 * xh)
        dx1_ref[...] = dx2 + _rms_bwd(dh2, xh, r, g_ref[...])

    return _pallas_call(
        body, carry, name="ffn_bwd", grid=(L // TM,),
        in_specs=[_tok(D), _tok(D), _chunk_tok(FC), _chunk_tok(FC), _full((1, D)), ANY, ANY, ANY],
        out_specs=[_tok(D), _chunk_tok(FC), _chunk_tok(FC), _chunk_tok(FC), _full((SUB, D))],
        out_shape=[_sds((L, D)), _sds((NCHIP, L, FC), BF), _sds((NCHIP, L, FC), BF), _sds((NCHIP, L, FC), BF),
                   _sds((SUB, D))],
        scratch_shapes=[pltpu.VMEM((NCHIP, FC, D), BF)] * 3 + [pltpu.SemaphoreType.DMA((3 * NCHIP,))],
        compiler_params=_params(56),
    )(x1, dx2, gg, uu, g_ffn, wg, wu, wd)


def _ple_loss(x2, p, tgt, g_pg, w_pg, b_pg, w_ple, g_ple, g_final):
    L = x2.shape[0]

    def body(x2_ref, p_ref, t_ref, gpg_ref, wpg_ref, bpg_ref, wple_ref, gple_ref, gf_ref,
             dx2_ref, n2_ref, dpre_ref, de0_ref, acc_ref):
        @pl.when(pl.program_id(0) == 0)
        def _():
            acc_ref[...] = jnp.zeros_like(acc_ref)

        x2 = x2_ref[...]
        x2h, r2 = _rms(x2)
        n2 = (x2h * gpg_ref[...]).astype(BF)
        n2_ref[...] = n2
        gate = _sig(jnp.dot(n2, wpg_ref[...], preferred_element_type=F32) + bpg_ref[...])
        pb = p_ref[...].astype(BF)
        e0 = jnp.concatenate([jnp.dot(pb, wple_ref[k], preferred_element_type=F32) for k in range(NCHIP)], axis=1)
        e0h, re = _rms(e0)
        e = e0h * gple_ref[...]
        x3 = x2 + gate * e
        x3h, r3 = _rms(x3)
        diff = x3h * gf_ref[...] - t_ref[...]
        acc_ref[4:5, :] += _colsum(diff * diff) * (0.5 / D)
        dy = diff * (1.0 / D)
        acc_ref[3:4, :] += _colsum(dy * x3h)
        dx3 = _rms_bwd(dy, x3h, r3, gf_ref[...])
        de = dx3 * gate
        acc_ref[2:3, :] += _colsum(de * e0h)
        de0_ref[...] = _rms_bwd(de, e0h, re, gple_ref[...]).astype(BF)
        dpre = dx3 * e * gate * (1.0 - gate)
        acc_ref[1:2, :] += _colsum(dpre)
        dpreb = dpre.astype(BF)
        dpre_ref[...] = dpreb
        dn2 = lax.dot_general(dpreb, wpg_ref[...], (((1,), (1,)), ((), ())), preferred_element_type=F32)
        acc_ref[0:1, :] += _colsum(dn2 * x2h)
        dx2_ref[...] = dx3 + _rms_bwd(dn2, x2h, r2, gpg_ref[...])

    return _pallas_call(
        body, name="ple_loss", grid=(L // TM,),
        in_specs=[_tok(D), _tok(PLE), _tok(D), _full((1, D)), _full((D, D)), _full((1, D)), _full((NCHIP, PLE, AC)),
                  _full((1, D)), _full((1, D))],
        out_specs=[_tok(D), _tok(D), _tok(D), _tok(D), _full((SUB, D))],
        out_shape=[_sds((L, D)), _sds((L, D), BF), _sds((L, D), BF), _sds((L, D), BF), _sds((SUB, D))],
        compiler_params=_params(40),
    )(x2, p, tgt, g_pg, *_in_hbm([w_pg]), b_pg, *_in_hbm([w_ple]), g_ple, g_final)


def _tn(name, a, b, col_chunk=None, a_block=None, carry=None):
    L = a.shape[-2]
    m, n = a.shape[-1], b.shape[-1]
    a_col = 0
    if a_block is not None:
        a_col, m = a_block
    tk = L if (a.ndim == 3 or b.ndim == 3 or a_block is not None) else TK
    if a.ndim == 3 or b.ndim == 3:
        nj, bn = (a if a.ndim == 3 else b).shape[0], n
        a_spec = (pl.BlockSpec((None, tk, m), lambda j, t: (j, t, 0)) if a.ndim == 3
                  else pl.BlockSpec((tk, m), lambda j, t: (t, 0)))
        b_spec = (pl.BlockSpec((None, tk, n), lambda j, t: (j, t, 0)) if b.ndim == 3
                  else pl.BlockSpec((tk, n), lambda j, t: (t, 0)))
        out_spec, out_shape = pl.BlockSpec((None, m, n), lambda j, t: (j, 0, 0)), _sds((nj, m, n))
    else:
        bn = col_chunk
        if bn is None:
            bn = next((cand for cand in (1024, 512) if n > cand and n % cand == 0), n)
        nj = n // bn
        a_spec = pl.BlockSpec((tk, m), lambda j, t: (t, a_col))
        b_spec = pl.BlockSpec((tk, bn), lambda j, t: (t, j))
        if col_chunk is None:
            out_spec, out_shape = pl.BlockSpec((m, bn), lambda j, t: (0, j)), _sds((m, n))
        else:
            out_spec, out_shape = pl.BlockSpec((None, m, bn), lambda j, t: (j, 0, 0)), _sds((nj, m, bn))

    def body(a_ref, b_ref, o_ref):
        if tk == L:
            o_ref[...] = _mm_tn(a_ref[...], b_ref[...])
        else:
            @pl.when(pl.program_id(1) == 0)
            def _():
                o_ref[...] = jnp.zeros_like(o_ref)

            o_ref[...] += _mm_tn(a_ref[...], b_ref[...])

    outs = _pallas_call(
        body, carry, name=name, grid=(nj, L // tk), in_specs=[a_spec, b_spec], out_specs=[out_spec],
        out_shape=[pltpu.HBM(out_shape.shape, out_shape.dtype)],
        compiler_params=pltpu.CompilerParams(dimension_semantics=("arbitrary", "arbitrary"),
                                             vmem_limit_bytes=(30 if tk == L else 28) * VMEM_MB),
    )(*(_in_hbm([a, b]) if tk == L else (a, b)))
    return outs[0] if carry is None else outs


LANE = 128


def _tn_blocks(name, a, bs, ga, gb, carry=None):
    L, m, n, nb = a.shape[0], a.shape[1], bs[0].shape[1], len(bs)
    per = LANE // ga
    wb = per * gb
    n_super = m // LANE

    def body(a_ref, *refs):
        b_refs, o_refs, acc_refs = refs[:nb], refs[nb:2 * nb], refs[2 * nb:]
        t = pl.program_id(0)

        @pl.when(t == 0)
        def _():
            for acc in acc_refs:
                acc[...] = jnp.zeros_like(acc)

        lhs = a_ref[...].astype(BF)
        for b_ref, acc in zip(b_refs, acc_refs):
            rhs = b_ref[...].astype(BF)
            for j in range(n_super):
                acc[j] += _mm_tn(lhs[:, j * LANE:(j + 1) * LANE], rhs[:, j * wb:(j + 1) * wb])

        @pl.when(t == L // TK - 1)
        def _():
            own = (lax.broadcasted_iota(jnp.int32, (LANE, wb), 0) // ga) == (lax.broadcasted_iota(jnp.int32, (LANE, wb), 1) // gb)
            for o_ref, acc in zip(o_refs, acc_refs):
                for j in range(n_super):
                    kept = jnp.where(own, acc[j], 0.0)
                    o_ref[:, j * wb:(j + 1) * wb] = jnp.sum(kept.reshape(per, ga, wb), axis=0)

    outs = _pallas_call(
        body, carry, name=name, grid=(L // TK,),
        in_specs=[pl.BlockSpec((TK, m), lambda t: (t, 0))] + [pl.BlockSpec((TK, n), lambda t: (t, 0))] * nb,
        out_specs=[_full((ga, n))] * nb, out_shape=[_sds((ga, n))] * nb,
        scratch_shapes=[pltpu.VMEM((n_super, LANE, wb), F32)] * nb,
        compiler_params=_params(48),
    )(*_in_hbm([a] + list(bs)))
    return list(outs)


def _s5_discretize(lam_re, lam_im, log_dt, b_re, b_im):
    dt = jnp.exp(log_dt)[:, None]
    mag = jnp.exp(lam_re * dt)
    ar = mag * jnp.cos(lam_im * dt)
    ai = mag * jnp.sin(lam_im * dt)
    den = lam_re * lam_re + lam_im * lam_im
    nr = ar - 1.0
    fr = (nr * lam_re + ai * lam_im) / den
    fi = (ai * lam_re - nr * lam_im) / den
    bbr = fr[:, None, :] * b_re - fi[:, None, :] * b_im
    bbi = fr[:, None, :] * b_im + fi[:, None, :] * b_re
    return ar, ai, bbr, bbi


def _prepare(by_rows, block_cols, ar, ai):
    n = len(by_rows)

    def body(*refs):
        srcs, (ar_ref, ai_ref), dense, (con_ref, rev_ref) = refs[:n], refs[n:n + 2], refs[n + 2:2 * n + 2], refs[2 * n + 2:]
        for src, out, c in zip(srcs, dense, block_cols):
            r = src.shape[0]
            per = LANE // r
            wide = per * c
            own = (lax.broadcasted_iota(jnp.int32, (LANE, wide), 0) // r) == (lax.broadcasted_iota(jnp.int32, (LANE, wide), 1) // c)
            for j in range(out.shape[0]):
                tiled = jnp.broadcast_to(src[:, j * wide:(j + 1) * wide][None], (per, r, wide)).reshape(LANE, wide)
                out[j] = jnp.where(own, tiled, 0.0).astype(BF)
        a_r, a_i = ar_ref[...], ai_ref[...]
        pw = [(jnp.ones_like(a_r), jnp.zeros_like(a_i))]
        for _ in range(SUB):
            pr, pi = pw[-1]
            pw.append((pr * a_r - pi * a_i, pr * a_i + pi * a_r))
        row = _row_iota(GN)
        for ref, reverse in ((con_ref, False), (rev_ref, True)):
            sign = -1.0 if reverse else 1.0
            for j, sh in enumerate((1, 2, 4)):
                keep = (row < SUB - sh) if reverse else (row >= sh)
                ref[2 * j * SUB:(2 * j + 1) * SUB, :] = jnp.where(keep, pw[sh][0], 0.0)
                ref[(2 * j + 1) * SUB:(2 * j + 2) * SUB, :] = jnp.where(keep, sign * pw[sh][1], 0.0)
            p_r, p_i = jnp.zeros((SUB, GN), F32), jnp.zeros((SUB, GN), F32)
            for i in range(SUB):
                k = SUB - i if reverse else i + 1
                p_r = jnp.where(row == i, pw[k][0], p_r)
                p_i = jnp.where(row == i, sign * pw[k][1], p_i)
            ref[6 * SUB:7 * SUB, :] = p_r
            ref[7 * SUB:8 * SUB, :] = p_i

    dense_shapes = [(b.shape[1] // (LANE // b.shape[0] * c), LANE, LANE // b.shape[0] * c)
                    for b, c in zip(by_rows, block_cols)]
    outs = _pallas_call(
        body, name="prepare", grid=(1,), in_specs=[_full(b.shape) for b in by_rows] + [_full((1, GN))] * 2,
        out_specs=[_full(s) for s in dense_shapes] + [_full((8 * SUB, GN))] * 2,
        out_shape=[_far(s, BF) for s in dense_shapes] + [_sds((8 * SUB, GN)), _far((8 * SUB, GN))],
        compiler_params=_params(48),
    )(*by_rows, ar, ai)
    return outs[:n], outs[n], outs[n + 1]


def _local_step(x, p, tgt, w, comm):
    rows_of = lambda a: a.reshape(NCHIP * a.shape[1], a.shape[2])
    quarters = lambda a: a.reshape(NCHIP, a.shape[0] // NCHIP, a.shape[1])

    def gathering(names, call):
        carry = comm.gather(names)
        outs = list(call(carry))
        own = len(outs) - len(carry.out_shapes)
        w.update(zip(names, outs[own:]))
        return outs[:own]

    w.update(comm.first())
    ar, ai, bbr, bbi = _s5_discretize(w["lam_re"], w["lam_im"], w["log_dt"], w["s5_b_re"], w["s5_b_im"])
    by_row = lambda b: jnp.transpose(b, (1, 0, 2)).reshape(b.shape[1], -1)
    (bbr_d, bbi_d, ccr_d, cci_d, wr_d, wi_d), con, con_rev = _prepare(
        [by_row(b) for b in (bbr, bbi, w["s5_c_re"], w["s5_c_im"], w["w_r"], w["w_i"])], [NS] * 4 + [HD] * 2,
        ar.reshape(1, GN), ai.reshape(1, GN))
    dsk = w["s5_d"].reshape(1, S5W)
    lam = w["lru_lambda"].reshape(1, LW)
    sp = jax.nn.softplus(-lam)
    b_r, b_i = w["b_r"].reshape(1, LW), w["b_i"].reshape(1, LW)
    row = lambda name: w[name].reshape(1, -1)

    h, ua, ub, gp = gathering(["w_glu", "w_a_out", "w_b_out"], lambda carry: _inproj_fwd(
        x, row("g_mix"), w["w_in"], row("b_in"), carry))
    w_glu = rows_of(w["w_glu"])
    sr, si, y, zg, ya = gathering(["w_o", "w_ffn_gate"], lambda carry: _s5_fwd(
        ua, bbr_d, bbi_d, ccr_d, cci_d, dsk, con, w_glu, row("b_glu"), carry))
    xc, rg, ig, yb, hp = gathering(["w_ffn_up"], lambda carry: _lru_fwd(
        ub, w["conv_w"], row("conv_b"), wr_d, wi_d, b_r, b_i, sp, carry))
    w_b_out, w_o = rows_of(w["w_b_out"]), rows_of(w["w_o"])
    x1, pa, pb, merged = gathering(["w_ple_gate", "w_ple"], lambda carry: _merge_fwd(
        x, ya, yb, gp, w["w_a_out"], w_b_out, w_o, carry))
    h2, gg, uu = gathering(["w_ffn_down"], lambda carry: _ffn_up_fwd(
        x1, row("g_ffn"), w["w_ffn_gate"], w["w_ffn_up"], carry))
    x2 = _ffn_down_fwd(x1, gg, uu, w["w_ffn_down"])
    w_pg = rows_of(w["w_ple_gate"])
    dx2, n2, dpre, de0, acc_p = _ple_loss(x2, p, tgt, row("g_ple_gate"), w_pg, row("b_ple_gate"),
                                          w["w_ple"], row("g_ple"), row("g_final"))
    comm.reduce("ple", {"w_ple_gate": quarters(_tn("dw_ple_gate", n2, dpre)),
                        "w_ple": _tn("dw_ple", p, de0, col_chunk=AC)})
    dx1, act, dgg, duu, acc_f = comm.run(lambda carry: _ffn_bwd(
        x1, dx2, gg, uu, row("g_ffn"), w["w_ffn_gate"], w["w_ffn_up"], w["w_ffn_down"], carry))
    comm.reduce("ffn_gate", {"w_ffn_gate": _tn("dw_ffn_gate", dgg, h2)})
    comm.reduce("w_o", {"w_o": quarters(_tn("dw_o", *_in_hbm([merged, dx1])))})
    comm.reduce("ffn_up", {"w_ffn_up": comm.run(lambda carry: _tn("dw_ffn_up", duu, h2, carry=carry))[0]})
    comm.reduce("ffn_down", {"w_ffn_down": comm.run(lambda carry: _tn("dw_ffn_down", act, dx2, carry=carry),
                                                    hold=("ffn_gate", "w_o"))[0]})
    dya, dyb, dgp, dpa, dpb = comm.run(lambda carry: _merge_bwd(
        dx1, gp, pa, pb, w["w_a_out"], w_b_out, w_o, carry), hold=("ffn_gate", "ffn_up"))
    comm.reduce("merge", {"w_a_out": _tn("dw_a_out", ya, dpa, col_chunk=AC), "w_b_out": quarters(_tn("dw_b_out", yb, dpb))})
    dua, dq, dy, lr, li, acc_a, acc_s = comm.run(lambda carry: _s5_bwd(
        dya, y, ua, sr, si, bbr_d, bbi_d, ccr_d, cci_d, dsk, con_rev, w_glu, row("b_glu"), carry), hold=("ffn_down",))
    dub, dpr, dpi, acc_l = comm.run(lambda carry: _lru_bwd(
        dyb, xc, rg, ig, hp, ub, w["conv_w"], wr_d, wi_d, sp, -_sig(-lam), carry))
    gx, dz, acc_g, acc_b = _inproj_bwd(x, dx1, dua, dub, dgp, row("g_mix"), w["w_in"])
    half = (D // 2,)
    comm.reduce("in_lo", {"w_in_lo": comm.run(lambda carry: _tn(
        "dw_in_lo", h, dz, col_chunk=QC, a_block=(0,) + half, carry=carry))[0]})
    comm.reduce("in_hi", {"w_in_hi": comm.run(lambda carry: _tn(
        "dw_in_hi", h, dz, col_chunk=QC, a_block=(1,) + half, carry=carry))[0], "w_glu": quarters(_tn("dw_glu", zg, dq))})
    d_wr, d_wi = comm.run(lambda carry: _tn_blocks("dw_r_i", xc, [dpr, dpi], HD, HD, carry))
    d_bbr, d_bbi = comm.run(lambda carry: _tn_blocks("d_bb", ua, [lr, li], NP, NS, carry))
    d_ccr, d_cci = comm.run(lambda carry: _tn_blocks("d_cc", dy, [sr, si], NP, NS, carry))
    comm.drain()
    sums = {"ple": acc_p, "ffn": acc_f, "mix": acc_g, "b_in": acc_b, "lru": acc_l, "s5": acc_s, "s5_a": acc_a}
    blocks = {"bb_re": d_bbr, "bb_im": d_bbi,
              "cc_re": d_ccr, "cc_im": d_cci,
              "w_r": d_wr, "w_i": d_wi}
    return gx, sums, blocks


def _replicated_grads(w, sums, blocks):
    grouped = lambda e, groups: jnp.transpose(e.reshape(e.shape[0], groups, -1), (1, 0, 2))
    d_ar, d_ai = sums["s5_a"][0].reshape(NG, NS), sums["s5_a"][1].reshape(NG, NS)
    d_bbr, d_bbi = grouped(blocks["bb_re"], NG), grouped(blocks["bb_im"], NG)
    _, vjp = jax.vjp(_s5_discretize, w["lam_re"], w["lam_im"], w["log_dt"], w["s5_b_re"], w["s5_b_im"])
    g = dict(zip(("lam_re", "lam_im", "log_dt", "s5_b_re", "s5_b_im"), vjp((d_ar, d_ai, d_bbr, d_bbi))))
    g["s5_c_re"] = grouped(blocks["cc_re"], NG)
    g["s5_c_im"] = -grouped(blocks["cc_im"], NG)
    g["w_r"], g["w_i"] = grouped(blocks["w_r"], NH), grouped(blocks["w_i"], NH)
    g["s5_d"] = sums["s5"][0].reshape(NG, NP)
    g["b_r"] = sums["lru"][1].reshape(NH, HD)
    g["b_i"] = sums["lru"][2].reshape(NH, HD)
    return g


ACC_ROWS = {"g_mix": ("mix", 0), "b_in": ("b_in", 0), "g_ffn": ("ffn", 0), "g_ple_gate": ("ple", 0),
            "b_ple_gate": ("ple", 1), "g_ple": ("ple", 2), "g_final": ("ple", 3), "b_glu": ("s5", 1),
            "lru_lambda": ("lru", 0), "conv_b": ("lru", 3)}
LOSS_ROW = ("ple", 4)
CONV_W_ROWS = ("lru", 4)


SHARDED = [("w_in", (D, QC)), ("w_glu", (S5W // NCHIP, S5W)), ("w_a_out", (S5W, AC)), ("w_b_out", (LW // NCHIP, D)),
           ("w_o", (D // NCHIP, D)), ("w_ffn_gate", (FC, D)), ("w_ffn_up", (FC, D)), ("w_ffn_down", (FC, D)),
           ("w_ple_gate", (D // NCHIP, D)), ("w_ple", (PLE, AC))]
TRANSPOSED = ("w_ffn_gate", "w_ffn_up", "s5_b_re", "s5_b_im")
CONV_SHARD = (4, LW // NCHIP)


def _mesh_pos():
    return lax.axis_index("x"), lax.axis_index("y"), lax.axis_index("c")


def _other_chips(x, y):
    return [(1 - x, y), (x, 1 - y), (1 - x, 1 - y)]


def _half_rows(c, rows, align):
    return pl.ds(pl.multiple_of(c * (rows // 2), align), rows // 2)


def _run_now(name, carry):
    c_in, c_out = len(carry.operands), len(carry.out_shapes)

    def body(*refs):
        ins, outs, sems = refs[:c_in], refs[c_in:c_in + c_out], refs[c_in + c_out:]
        carry.start(ins, outs, sems)
        carry.finish(ins, outs, sems)

    return pl.pallas_call(body, name=name, in_specs=[ANY] * c_in, out_specs=[ANY] * c_out,
                          out_shape=list(carry.out_shapes), scratch_shapes=list(carry.sems),
                          input_output_aliases=dict(carry.aliases))(*_in_hbm(carry.operands))


def _gather_group(shards, split):
    n = len(shards)

    def copies(srcs, outs, sems):
        send_sems, recv_sems = sems
        x, y, c = _mesh_pos()
        k0 = 2 * x + y
        sib = (x, y, 1 - c)
        chips = _other_chips(x, y)

        def remote(src, dst, j, i, to):
            return pltpu.make_async_remote_copy(src_ref=src, dst_ref=dst, send_sem=send_sems.at[j, i],
                                                recv_sem=recv_sems.at[j, i], device_id=to, device_id_type=MESH)

        def rows(ref, i, core, *lead):
            if not split[i]:
                return ref.at[lead] if lead else ref
            return ref.at[(*lead, _half_rows(core, shards[i].shape[0], 16))]

        own = [remote(s, o.at[k0], 6, i, sib) for i, (s, o) in enumerate(zip(srcs, outs))]
        ici, landed, fwd, fwd_landed = [], [], [], []
        for j, chip in enumerate(chips):
            kj = 2 * chip[0] + chip[1]
            pairs = list(enumerate(zip(srcs, outs)))
            ici.append([remote(rows(s, i, c), rows(o, i, c, k0), j, i, (*chip, c)) for i, (s, o) in pairs])
            landed.append([remote(rows(s, i, c), rows(o, i, c, kj), j, i, (*chip, c)) for i, (s, o) in pairs])
            fwd.append([remote(rows(o, i, c, kj), rows(o, i, c, kj), 3 + j, i, sib) for i, (s, o) in pairs if split[i]])
            fwd_landed.append([remote(rows(o, i, 1 - c, kj), rows(o, i, 1 - c, kj), 3 + j, i, sib)
                               for i, (s, o) in pairs if split[i]])
        return own, ici, landed, fwd, fwd_landed

    def start(srcs, outs, sems):
        own, ici, _, _, _ = copies(srcs, outs, sems)
        for cp in own + [cp for per_chip in ici for cp in per_chip]:
            cp.start()

    def finish(srcs, outs, sems):
        own, ici, landed, fwd, fwd_landed = copies(srcs, outs, sems)
        passed = [i for i in range(n) if split[i]]
        for j in range(3):
            for i, cp in enumerate(landed[j]):
                cp.wait_recv()
                if split[i]:
                    fwd[j][passed.index(i)].start()
        for j in range(3):
            for cp in fwd_landed[j]:
                cp.wait_recv()
        for cp in own:
            cp.wait_recv()
        for cp in own + [cp for per_chip in ici + fwd for cp in per_chip]:
            cp.wait_send()

    return _Carried(shards, [_far((NCHIP,) + s.shape, s.dtype) for s in shards],
                    [pltpu.SemaphoreType.DMA((7, n)), pltpu.SemaphoreType.DMA((7, n))], start, finish)


def _to_bf16_group(name, arrays, carry):
    n = len(arrays)

    def body(*refs):
        for src, dst in zip(refs[:n], refs[n:]):
            dst[...] = src[...].astype(BF)

    specs = [pl.BlockSpec((a.shape[0] // 2, a.shape[1]), lambda i: (i, 0)) for a in arrays]
    return _pallas_call(body, carry, name=name, grid=(2,), in_specs=specs, out_specs=specs,
                        out_shape=[_far(a.shape, BF) for a in arrays], compiler_params=_params(48))(*arrays)


def _each_copy(copies, carried, out_shapes, sems, aliases=None):
    def start(ins, outs, sem_refs):
        for cp in copies(ins, outs, sem_refs):
            cp.start()

    def finish(ins, outs, sem_refs):
        for cp in copies(ins, outs, sem_refs):
            cp.wait()

    return _Carried(carried, out_shapes, sems, start, finish, aliases)


def _swap_group(grads):
    n = len(grads)

    def copies(srcs, outs, sems):
        send_sems, recv_sems = sems
        x, y, c = _mesh_pos()
        return [pltpu.make_async_remote_copy(src_ref=s.at[:, _half_rows(1 - c, s.shape[1], 8)], dst_ref=o,
                                             send_sem=send_sems.at[i], recv_sem=recv_sems.at[i], device_id=(x, y, 1 - c),
                                             device_id_type=MESH) for i, (s, o) in enumerate(zip(srcs, outs))]

    return _each_copy(copies, grads, [pltpu.HBM((NCHIP, g.shape[1] // 2, g.shape[2]), F32) for g in grads],
                      [pltpu.SemaphoreType.DMA((n,)), pltpu.SemaphoreType.DMA((n,))])


def _add_sibling_group(tag, kc_idx, grads, gots):
    n = len(grads)

    def body(kc_ref, *refs):
        for g, rx, p, pb in zip(refs[:n], refs[n:2 * n], refs[2 * n:3 * n], refs[3 * n:]):
            s = g[...] + rx[...]
            pb[...] = s.astype(BF)

            @pl.when(pl.program_id(0) == kc_ref[0])
            def _():
                p[...] = s

    halves = [pl.BlockSpec((None,) + rx.shape[1:], lambda k, kc_ref: (k, 0, 0)) for rx in gots]
    mine = [pl.BlockSpec((None,) + rx.shape[1:], lambda k, kc_ref: (k, kc_ref[1], 0)) for rx in gots]
    own = [pl.BlockSpec(rx.shape[1:], lambda k, kc_ref: (0, 0)) for rx in gots]
    outs = _pallas_call(
        body, name="add_sibling_" + tag,
        grid_spec=pltpu.PrefetchScalarGridSpec(num_scalar_prefetch=1, grid=(NCHIP,), in_specs=mine + halves,
                                               out_specs=own + halves),
        out_shape=[pltpu.HBM(rx.shape[1:], F32) for rx in gots] + [pltpu.HBM(rx.shape, BF) for rx in gots],
        compiler_params=_params(48),
    )(kc_idx, *_in_hbm(list(grads) + list(gots)))
    return outs[:n], outs[n:]


def _exchange_group(parts):
    n = len(parts)

    def copies(srcs, outs, sems):
        send_sems, recv_sems = sems
        x, y, c = _mesh_pos()
        return [pltpu.make_async_remote_copy(
            src_ref=s.at[2 * chip[0] + chip[1]], dst_ref=o.at[j], send_sem=send_sems.at[j, i],
            recv_sem=recv_sems.at[j, i], device_id=(*chip, c), device_id_type=MESH)
            for j, chip in enumerate(_other_chips(x, y)) for i, (s, o) in enumerate(zip(srcs, outs))]

    return _each_copy(copies, parts, [pltpu.HBM((3,) + p.shape[1:], BF) for p in parts],
                      [pltpu.SemaphoreType.DMA((3, n)), pltpu.SemaphoreType.DMA((3, n))])


def _add_chips_group(tag, kc_idx, parts, arrived):
    n = len(parts)

    def body(kc_ref, *refs):
        for p, rx, t in zip(refs[:n], refs[n:2 * n], refs[2 * n:]):
            t[...] = ((p[...] + rx[0].astype(F32)) + rx[1].astype(F32)) + rx[2].astype(F32)

    outs = _pallas_call(
        body, name="add_chips_" + tag,
        grid_spec=pltpu.PrefetchScalarGridSpec(
            num_scalar_prefetch=1, grid=(1,),
            in_specs=([pl.BlockSpec(rx.shape[1:], lambda i, kc_ref: (0, 0)) for rx in arrived]
                      + [pl.BlockSpec(rx.shape, lambda i, kc_ref: (0, 0, 0)) for rx in arrived]),
            out_specs=[pl.BlockSpec((None,) + rx.shape[1:], lambda i, kc_ref: (kc_ref[1], 0, 0)) for rx in arrived]),
        out_shape=[pltpu.HBM((2,) + rx.shape[1:], F32) for rx in arrived],
        compiler_params=_params(48),
    )(kc_idx, *_in_hbm(list(parts) + list(arrived)))
    return list(outs)


def _join_group(halves):
    n = len(halves)

    def copies(bufs, sems):
        send_sems, recv_sems = sems
        x, y, c = _mesh_pos()
        sib = (x, y, 1 - c)
        sends = [pltpu.make_async_remote_copy(src_ref=b.at[c], dst_ref=b.at[c], send_sem=send_sems.at[i],
                                              recv_sem=recv_sems.at[i], device_id=sib, device_id_type=MESH)
                 for i, b in enumerate(bufs)]
        landed = [pltpu.make_async_remote_copy(src_ref=b.at[c], dst_ref=b.at[1 - c], send_sem=send_sems.at[i],
                                               recv_sem=recv_sems.at[i], device_id=sib, device_id_type=MESH)
                  for i, b in enumerate(bufs)]
        return sends, landed

    def start(_, bufs, sems):
        for cp in copies(bufs, sems)[0]:
            cp.start()

    def finish(_, bufs, sems):
        sends, landed = copies(bufs, sems)
        for cp in landed:
            cp.wait_recv()
        for cp in sends:
            cp.wait_send()

    return _Carried(halves, [pltpu.HBM(h.shape, F32) for h in halves],
                    [pltpu.SemaphoreType.DMA((n,)), pltpu.SemaphoreType.DMA((n,))], start, finish,
                    {i: i for i in range(n)})


def _combine(carries):
    operands, out_shapes, sems, aliases, spans = [], [], [], {}, []
    for c in carries:
        aliases.update({len(operands) + i: len(out_shapes) + o for i, o in c.aliases.items()})
        spans.append((len(operands), len(out_shapes), len(sems)))
        operands += list(c.operands)
        out_shapes += list(c.out_shapes)
        sems += list(c.sems)

    def each(phase):
        def run(ins, outs, sem_refs):
            for c, (a, b, s) in zip(carries, spans):
                getattr(c, phase)(ins[a:a + len(c.operands)], outs[b:b + len(c.out_shapes)], sem_refs[s:s + len(c.sems)])
        return run

    return _Carried(operands, out_shapes, sems, each("start"), each("finish"), aliases)


def _allreduce_small(arrays, wire):
    n = len(arrays)
    halves = [(a.shape[0], a.shape[1] // 2) for a in arrays]

    def body(*refs):
        srcs, outs = refs[:n], refs[n:2 * n]
        mine_bufs, sib_bufs, chip_bufs, total_bufs = (refs[k * n:(k + 1) * n] for k in range(2, 6))
        send_sems, recv_sems, local_sems = refs[6 * n:]
        x, y, c = _mesh_pos()
        k0 = 2 * x + y
        sib = (x, y, 1 - c)

        def remote(src, dst, j, i, to):
            return pltpu.make_async_remote_copy(src_ref=src, dst_ref=dst, send_sem=send_sems.at[j, i],
                                                recv_sem=recv_sems.at[j, i], device_id=to, device_id_type=MESH)

        def cols(ref, i, core):
            return ref.at[:, pl.ds(pl.multiple_of(core * halves[i][1], LANE), halves[i][1])]

        swaps = [remote(cols(s, i, 1 - c), b, 0, i, sib) for i, (s, b) in enumerate(zip(srcs, sib_bufs))]
        own = [pltpu.make_async_copy(cols(s, i, c), m, local_sems.at[i]) for i, (s, m) in enumerate(zip(srcs, mine_bufs))]
        for cp in swaps + own:
            cp.start()
        for cp in swaps + own:
            cp.wait()
        for m, b, buf in zip(mine_bufs, sib_bufs, chip_bufs):
            buf[k0] = (m[...] + b[...]).astype(buf.dtype)
        chips = _other_chips(x, y)
        sends = [remote(buf.at[k0], buf.at[k0], 1 + j, i, (*chip, c))
                 for j, chip in enumerate(chips) for i, buf in enumerate(chip_bufs)]
        for cp in sends:
            cp.start()
        for j, chip in enumerate(chips):
            for i, buf in enumerate(chip_bufs):
                remote(buf.at[k0], buf.at[2 * chip[0] + chip[1]], 1 + j, i, (*chip, c)).wait_recv()
        for cp in sends:
            cp.wait_send()
        for t, buf in zip(total_bufs, chip_bufs):
            t[...] = ((buf[0].astype(F32) + buf[1].astype(F32)) + buf[2].astype(F32)) + buf[3].astype(F32)
        joins = [remote(t, cols(o, i, c), 4, i, sib) for i, (t, o) in enumerate(zip(total_bufs, outs))]
        keep = [pltpu.make_async_copy(t, cols(o, i, c), local_sems.at[i]) for i, (t, o) in enumerate(zip(total_bufs, outs))]
        for cp in joins + keep:
            cp.start()
        for i, (t, o) in enumerate(zip(total_bufs, outs)):
            remote(t, cols(o, i, 1 - c), 4, i, sib).wait_recv()
        for cp in joins:
            cp.wait_send()
        for cp in keep:
            cp.wait()

    specs = [_full(a.shape) for a in arrays]
    return _pallas_call(
        body, name="allreduce_small", grid=(1,), in_specs=specs, out_specs=specs,
        out_shape=[_sds(a.shape) for a in arrays],
        scratch_shapes=([pltpu.VMEM(h, F32) for h in halves] + [pltpu.VMEM(h, F32) for h in halves]
                        + [pltpu.VMEM((NCHIP,) + h, dt) for h, dt in zip(halves, wire)] + [pltpu.VMEM(h, F32) for h in halves]
                        + [pltpu.SemaphoreType.DMA((5, n)), pltpu.SemaphoreType.DMA((5, n)), pltpu.SemaphoreType.DMA((n,))]),
        compiler_params=_params(32),
    )(*arrays)


def _adamw_terms(w, g, m, v):
    m = ADAM_B1 * m + (1.0 - ADAM_B1) * g
    v = ADAM_B2 * v + (1.0 - ADAM_B2) * jnp.square(g)
    m_hat = m / (1.0 - ADAM_B1 ** ADAM_STEP)
    v_hat = v / (1.0 - ADAM_B2 ** ADAM_STEP)
    return -ADAM_LR * (m_hat / (jnp.sqrt(v_hat) + ADAM_EPS) + ADAM_WD * w), m, v


ADAM_STEPS = 4


def _adamw_group(tag, ws, gs, ms, vs):
    n = len(ws)

    def body(*refs):
        ins, outs = refs[:4 * n], refs[4 * n:]
        for i in range(n):
            w, g, m, v = (ins[k * n + i][...] for k in range(4))
            outs[i][...] = g
            outs[n + i][...], outs[2 * n + i][...], outs[3 * n + i][...] = _adamw_terms(w, g, m, v)

    specs = [pl.BlockSpec((w.shape[0] // ADAM_STEPS, w.shape[1]), lambda i: (i, 0)) for w in ws]
    outs = _pallas_call(
        body, name="adamw_" + tag, grid=(ADAM_STEPS,), in_specs=specs * 4, out_specs=specs * 4,
        out_shape=[_sds(w.shape) for w in ws] * 4, compiler_params=_params(48),
    )(*_in_hbm(list(ws) + list(gs) + list(ms) + list(vs)))
    return outs[:n], outs[n:2 * n], outs[2 * n:3 * n], outs[3 * n:]


def _adamw_replicated(sums, row_of, direct):
    ns, nr, nd = len(sums), len(row_of), len(direct)

    def body(*refs):
        sum_refs = refs[:ns]
        ins = refs[ns:ns + 3 * nr + 4 * nd]
        outs = refs[ns + 3 * nr + 4 * nd:]
        for i, (_, _, _, si, row) in enumerate(row_of):
            w_ref, m_ref, v_ref = ins[3 * i:3 * i + 3]
            g = sum_refs[si][row:row + 1, :]
            outs[4 * i][...] = g
            outs[4 * i + 1][...], outs[4 * i + 2][...], outs[4 * i + 3][...] = _adamw_terms(w_ref[...], g, m_ref[...], v_ref[...])
        for i in range(nd):
            w_ref, m_ref, v_ref, g_ref = ins[3 * nr + 4 * i:3 * nr + 4 * i + 4]
            o = outs[4 * (nr + i):4 * (nr + i) + 4]
            g = g_ref[...]
            o[0][...] = g
            o[1][...], o[2][...], o[3][...] = _adamw_terms(w_ref[...], g, m_ref[...], v_ref[...])

    operands = list(sums)
    shapes = []
    for w, m, v, _, _ in row_of:
        operands += [w, m, v]
        shapes += [w.shape] * 4
    for w, m, v, g in direct:
        operands += [w, m, v, g]
        shapes += [w.shape] * 4
    flat = _pallas_call(
        body, name="adamw_replicated", grid=(1,), in_specs=[_full(a.shape) for a in operands],
        out_specs=[_full(s) for s in shapes], out_shape=[_sds(s) for s in shapes],
        compiler_params=_params(56),
    )(*operands)
    return [flat[4 * i:4 * i + 4] for i in range(nr + nd)]


class _Exchanges:
    def __init__(self, shards, conv_w, chip, core, apply):
        self.shards, self.conv_w, self.apply = shards, conv_w, apply
        self.active, self.calls = [], 0
        self.chip_core_idx = jnp.stack([chip, core]).astype(jnp.int32)

    def first(self):
        later = [n for n in self.shards if n != "w_in"]
        carry = _gather_group([self.shards["w_in"].astype(BF), self.conv_w], [True, False])
        outs = _to_bf16_group("gather_first", [self.shards[n] for n in later], carry)
        self.shards = dict(zip(later, outs))
        return {"w_in": outs[len(later)], "conv_w": jnp.transpose(outs[len(later) + 1], (1, 0, 2)).reshape(4, LW)}

    def gather(self, names):
        return _gather_group([self.shards[n] for n in names], [True] * len(names))

    def reduce(self, tag, grads):
        self.active.append({"tag": tag, "names": list(grads), "stage": 0, "grads": list(grads.values())})

    def run(self, call, hold=()):
        groups = [g for g in self.active if g["tag"] not in hold]
        carries = [self._exchange_of(g) for g in groups]
        carry = _combine(carries)
        outs = list(call(carry))
        own = len(outs) - len(carry.out_shapes)
        landed = outs[own:]
        for g, c in zip(groups, carries):
            self._sum_after(g, landed[:len(c.out_shapes)])
            landed = landed[len(c.out_shapes):]
        self.active = [g for g in self.active if g["stage"] < 3]
        return outs[:own]

    def _exchange_of(self, g):
        if g["stage"] == 0:
            return _swap_group(g["grads"])
        if g["stage"] == 1:
            return _exchange_group(g["bf16"])
        return _join_group(g["halves"])

    def _sum_after(self, g, landed):
        if g["stage"] == 0:
            g["f32"], g["bf16"] = _add_sibling_group(g["tag"], self.chip_core_idx, g["grads"], landed)
        elif g["stage"] == 1:
            g["halves"] = _add_chips_group(g["tag"], self.chip_core_idx, g["f32"], landed)
        else:
            self.apply(g["tag"], g["names"], [t.reshape(2 * t.shape[1], t.shape[2]) for t in landed])
        g["stage"] += 1

    def drain(self):
        while self.active:
            self.calls += 1
            self.run(lambda carry: _run_now("reduce_%d" % self.calls, carry))


INPUT_NAMES = (["x", "p"] + [n for n in
               ["g_mix", "w_in", "b_in", "lam_re", "lam_im", "log_dt", "s5_b_re", "s5_b_im", "s5_c_re", "s5_c_im", "s5_d",
                "w_glu", "b_glu", "conv_w", "conv_b", "w_r", "b_r", "w_i", "b_i", "lru_lambda", "w_a_out", "w_b_out", "w_o",
                "g_ffn", "w_ffn_gate", "w_ffn_up", "w_ffn_down", "g_ple_gate", "w_ple_gate", "b_ple_gate", "w_ple", "g_ple",
                "g_final"]])
WEIGHT_NAMES = INPUT_NAMES[2:]


def kernel(*args):
    names = INPUT_NAMES + ["loss_target"] + ["m_" + n for n in WEIGHT_NAMES] + ["v_" + n for n in WEIGHT_NAMES]
    assert len(args) == len(names)
    given = dict(zip(names, args))

    def view(name):
        a = given[name]
        return jnp.swapaxes(a, -1, -2) if name.endswith(TRANSPOSED) else a

    def unview(name, a):
        return jnp.swapaxes(a, -1, -2) if name in TRANSPOSED else a

    def local(name):
        return view(name) if name.endswith("g_final") else view(name)[0]

    xi, yi, ci = _mesh_pos()
    k0 = 2 * xi + yi
    x, p, tgt = given["x"][0], given["p"][0, 0], given["loss_target"][0]

    results = {}

    row_halves = {}

    def apply(tag, names, totals):
        totals = dict(zip(names, totals))
        row_halves.update({n: totals.pop(n) for n in names if n in ("w_in_lo", "w_in_hi")})
        if len(row_halves) == 2:
            totals["w_in"] = jnp.concatenate([row_halves.pop("w_in_lo"), row_halves.pop("w_in_hi")])
        names = list(totals)
        if not names:
            return
        new = _adamw_group(tag, [local(n) for n in names], list(totals.values()), [local("m_" + n) for n in names],
                           [local("v_" + n) for n in names])
        for kind, arrays in zip(("grad", "delta", "new_m", "new_v"), new):
            for n, arr in zip(names, arrays):
                results[kind, n] = unview(n, arr[None])

    comm = _Exchanges({n: local(n) for n, _ in SHARDED}, local("conv_w"), k0, ci, apply)
    w = {n: local(n) for n in WEIGHT_NAMES if n != "conv_w" and n not in dict(SHARDED)}
    gx, sums, blocks = _local_step(x, p, tgt, w, comm)

    sum_names, block_names = list(sums), list(blocks)
    red = _allreduce_small([sums[n] for n in sum_names] + [blocks[n] for n in block_names],
                           [F32] * len(sum_names) + [BF] * len(block_names))
    sums = dict(zip(sum_names, red[:len(sum_names)]))
    blocks = dict(zip(block_names, red[len(sum_names):]))
    loss = jnp.sum(sums[LOSS_ROW[0]][LOSS_ROW[1]])
    direct_g = _replicated_grads(w, sums, blocks)
    conv_rows = sums[CONV_W_ROWS[0]][CONV_W_ROWS[1]:CONV_W_ROWS[1] + 4]
    direct_g["conv_w"] = lax.dynamic_slice(conv_rows, (0, k0 * CONV_SHARD[1]), CONV_SHARD)
    as_row = lambda a: a.reshape(1, -1)
    row_names = list(ACC_ROWS)
    row_of = [(as_row(given[n]), as_row(given["m_" + n]), as_row(given["v_" + n]),
               sum_names.index(ACC_ROWS[n][0]), ACC_ROWS[n][1]) for n in row_names]
    direct_names = list(direct_g)
    direct = [(view(n), view("m_" + n), view("v_" + n), direct_g[n].reshape(view(n).shape)) for n in direct_names]
    done = _adamw_replicated([sums[n] for n in sum_names], row_of, direct)
    for n, four in zip(row_names + direct_names, done):
        for kind, arr in zip(("grad", "delta", "new_m", "new_v"), four):
            results[kind, n] = unview(n, arr).reshape(given[n].shape)

    out = [loss, gx[None]]
    for kind in ("grad", "delta", "new_m", "new_v"):
        out += [results[kind, n] for n in WEIGHT_NAMES]
    return tuple(out)
```

```python
import functools
import math

import jax
import jax.numpy as jnp
from jax import lax
from jax.experimental import pallas as pl
from jax.experimental.pallas import tpu as pltpu

F32 = jnp.float32
BF = jnp.bfloat16

D = 1024
S5W = 512
NG, NS, NP = 32, 64, 16
GN = NG * NS
LW = 1024
NH, HD = 16, 64
LRU_C = 8.0
FH = 2816
NCHIP = 4
FC = FH // NCHIP
PLE = 256
INC = S5W + LW + 2 * D
EPS = 1e-6
ADAM_LR, ADAM_B1, ADAM_B2, ADAM_EPS, ADAM_WD, ADAM_STEP = 0.001, 0.9, 0.999, 1e-08, 0.01, 10

TM = 256
TK = 1024
LC = 512
SUB = 8
VMEM_MB = 1024 * 1024
MESH = pl.DeviceIdType.MESH
ANY = pl.BlockSpec(memory_space=pl.ANY)


def _mm(a, b):
    return jnp.dot(a.astype(BF), b.astype(BF), preferred_element_type=F32)


def _mm_nt(a, b):
    return lax.dot_general(a.astype(BF), b.astype(BF), (((1,), (1,)), ((), ())), preferred_element_type=F32)


def _mm_tn(a, b):
    return lax.dot_general(a.astype(BF), b.astype(BF), (((0,), (0,)), ((), ())), preferred_element_type=F32)


def _blockdiag_mm(x, blocks_ref):
    n, rows, _ = blocks_ref.shape
    return jnp.concatenate([jnp.dot(x[:, j * rows:(j + 1) * rows], blocks_ref[j], preferred_element_type=F32)
                            for j in range(n)], axis=1)


def _blockdiag_mm_t(x, blocks_ref):
    n, _, wide = blocks_ref.shape
    return jnp.concatenate([lax.dot_general(x[:, j * wide:(j + 1) * wide], blocks_ref[j], (((1,), (1,)), ((), ())),
                                            preferred_element_type=F32) for j in range(n)], axis=1)


def _rms(x):
    r = lax.rsqrt(jnp.mean(x * x, axis=-1, keepdims=True) + EPS)
    return x * r, r


def _rms_bwd(dy, xh, r, g):
    dxh = dy * g
    return r * (dxh - xh * jnp.mean(dxh * xh, axis=-1, keepdims=True))


def _colsum(x):
    return jnp.sum(x, axis=0, keepdims=True)


def _sig(x):
    return jax.nn.sigmoid(x)


def _gelu_grad(x):
    c = math.sqrt(2.0 / math.pi)
    t = jnp.tanh(c * (x + 0.044715 * x * x * x))
    return 0.5 * (1.0 + t) + 0.5 * x * (1.0 - t * t) * c * (1.0 + 3.0 * 0.044715 * x * x)


def _neg_expm1(x):
    series = -x * (1.0 + x * (0.5 + x * (1.0 / 6.0 + x * (1.0 / 24.0))))
    return jnp.where(x > -0.03, series, 1.0 - jnp.exp(x))


def _tok(width):
    return pl.BlockSpec((TM, width), lambda i: (i, 0))


def _tok_rev(width, nt):
    return pl.BlockSpec((TM, width), lambda i: (nt - 1 - i, 0))


def _full(shape):
    return pl.BlockSpec(shape, lambda i: (0,) * len(shape))


def _params(vmem_mb, **kw):
    return pltpu.CompilerParams(dimension_semantics=("arbitrary",), vmem_limit_bytes=vmem_mb * VMEM_MB, **kw)


def _sds(shape, dtype=F32):
    return jax.ShapeDtypeStruct(shape, dtype)


def _far(shape, dtype=F32):
    return pltpu.HBM(shape, dtype)


class _Carried:
    def __init__(self, operands, out_shapes, sems, start, finish, aliases=None):
        self.operands, self.out_shapes, self.sems = list(operands), list(out_shapes), list(sems)
        self.start, self.finish, self.aliases = start, finish, dict(aliases or {})


def _in_hbm(arrays):
    return [pltpu.with_memory_space_constraint(a, pltpu.HBM) for a in arrays]


def _pallas_call(body, carry=None, **kw):
    if carry is None:
        return pl.pallas_call(body, **kw)

    def at_step(corner):
        hit = [pl.program_id(d) == (size - 1 if corner else 0) for d, size in enumerate(kw["grid"])]
        return functools.reduce(jnp.logical_and, hit)

    name, grid, compiler_params = kw["name"], kw["grid"], kw["compiler_params"]
    in_specs, out_specs, out_shape = list(kw["in_specs"]), list(kw["out_specs"]), list(kw["out_shape"])
    scratch_shapes = list(kw.get("scratch_shapes", ()))
    n_in, n_out, n_scr = len(in_specs), len(out_specs), len(scratch_shapes)
    c_in, c_out = len(carry.operands), len(carry.out_shapes)

    def full_body(*refs):
        ins, refs = refs[:n_in], refs[n_in:]
        c_ins, refs = refs[:c_in], refs[c_in:]
        outs, refs = refs[:n_out], refs[n_out:]
        c_outs, refs = refs[:c_out], refs[c_out:]
        scratch, c_sems = refs[:n_scr], refs[n_scr:]

        @pl.when(at_step(0))
        def _():
            carry.start(c_ins, c_outs, c_sems)

        body(*ins, *outs, *scratch)

        @pl.when(at_step(1))
        def _():
            carry.finish(c_ins, c_outs, c_sems)

    call = pl.pallas_call(
        full_body, name=name, grid=grid, in_specs=in_specs + [ANY] * c_in, out_specs=out_specs + [ANY] * c_out,
        out_shape=out_shape + list(carry.out_shapes), scratch_shapes=scratch_shapes + list(carry.sems),
        input_output_aliases={n_in + i: n_out + o for i, o in carry.aliases.items()},
        compiler_params=compiler_params)
    return lambda *operands: call(*operands, *_in_hbm(carry.operands))


def _resident(pairs, sems):
    first = pl.program_id(0) == 0
    copies = [pltpu.make_async_copy(src, dst, sems.at[j]) for j, (src, dst) in enumerate(pairs)]

    @pl.when(first)
    def _():
        for cp in copies:
            cp.start()

    def wait(j):
        @pl.when(first)
        def _():
            copies[j].wait()

    return wait


def _resident_now(pairs, sems):
    @pl.when(pl.program_id(0) == 0)
    def _():
        copies = [pltpu.make_async_copy(src, dst, sems.at[j]) for j, (src, dst) in enumerate(pairs)]
        for cp in copies:
            cp.start()
        for cp in copies:
            cp.wait()


def _row_iota(width):
    return lax.broadcasted_iota(jnp.int32, (SUB, width), 0)


def _bcast_row(x, row):
    return jnp.broadcast_to(x[row:row + 1, :], x.shape)


def _slab(k):
    return pl.ds(pl.multiple_of(k * SUB, SUB), SUB)


QC = INC // NCHIP
Z_PARTS = ((0, S5W), (S5W, S5W + LW), (S5W + LW, INC))


def _inproj_fwd(x, g_mix, w_in, b_in, carry=None):
    L = x.shape[0]

    def body(x_ref, g_ref, w_hbm, b_ref, h_ref, ua_ref, ub_ref, gp_ref, w_vm, w_sems):
        _resident_now([(w_hbm.at[k], w_vm.at[k]) for k in range(NCHIP)], w_sems)
        xh, _ = _rms(x_ref[...])
        h = (xh * g_ref[...]).astype(BF)
        h_ref[...] = h
        for k in range(NCHIP):
            lo, hi = k * QC, (k + 1) * QC
            z = jnp.dot(h, w_vm[k], preferred_element_type=F32) + b_ref[:, lo:hi]
            for ref, (a, b) in zip((ua_ref, ub_ref, gp_ref), Z_PARTS):
                s, e = max(lo, a), min(hi, b)
                if s < e:
                    ref[:, s - a:e - a] = z[:, s - lo:e - lo]

    return _pallas_call(
        body, carry, name="inproj_fwd", grid=(L // TM,),
        in_specs=[_tok(D), _full((1, D)), ANY, _full((1, INC))],
        out_specs=[_tok(D), _tok(S5W), _tok(LW), _tok(2 * D)],
        out_shape=[_far((L, D), BF), _far((L, S5W)), _far((L, LW)), _sds((L, 2 * D))],
        scratch_shapes=[pltpu.VMEM((NCHIP, D, QC), BF), pltpu.SemaphoreType.DMA((NCHIP,))],
        compiler_params=_params(40),
    )(*_in_hbm([x]), g_mix, *_in_hbm([w_in]), b_in)


def _inproj_bwd(x, dx1, dua, dub, dgp, g_mix, w_in, carry=None):
    L = x.shape[0]

    def body(x_ref, dx1_ref, dua_ref, dub_ref, dgp_ref, g_ref, w_hbm, gx_ref, dz_ref, dg_ref, db_ref, w_vm, w_sems):
        _resident_now([(w_hbm.at[k], w_vm.at[k]) for k in range(NCHIP)], w_sems)

        @pl.when(pl.program_id(0) == 0)
        def _():
            dg_ref[...] = jnp.zeros_like(dg_ref)
            db_ref[...] = jnp.zeros_like(db_ref)

        for src, (a, b) in zip((dua_ref, dub_ref, dgp_ref), Z_PARTS):
            d = src[...]
            dz_ref[:, a:b] = d.astype(BF)
            db_ref[0:1, a:b] += _colsum(d)
        dh = jnp.zeros((TM, D), F32)
        for k in range(NCHIP):
            dh = dh + lax.dot_general(dz_ref[:, k * QC:(k + 1) * QC], w_vm[k], (((1,), (1,)), ((), ())),
                                      preferred_element_type=F32)
        xh, r = _rms(x_ref[...])
        dg_ref[0:1, :] += _colsum(dh * xh)
        gx_ref[...] = dx1_ref[...] + _rms_bwd(dh, xh, r, g_ref[...])

    return _pallas_call(
        body, carry, name="inproj_bwd", grid=(L // TM,),
        in_specs=[_tok(D), _tok(D), _tok(S5W), _tok(LW), _tok(2 * D), _full((1, D)), ANY],
        out_specs=[_tok(D), _tok(INC), _full((SUB, D)), _full((SUB, INC))],
        out_shape=[_sds((L, D)), _sds((L, INC), BF), _sds((SUB, D)), _sds((SUB, INC))],
        scratch_shapes=[pltpu.VMEM((NCHIP, D, QC), BF), pltpu.SemaphoreType.DMA((NCHIP,))],
        compiler_params=_params(40),
    )(x, dx1, *_in_hbm([dua]), dub, dgp, g_mix, *_in_hbm([w_in]))


def _cscan(xr_ref, xi_ref, con_ref, cr_ref, ci_ref, reverse):
    n_slab = xr_ref.shape[0] // SUB
    width = xr_ref.shape[1]
    for lc in range(width // LC):
        cols = slice(lc * LC, (lc + 1) * LC)
        con = [con_ref[SUB * j:SUB * (j + 1), cols] for j in range(8)]

        def step(k, carry, cols=cols, con=con):
            cr, ci = carry
            rows = _slab(n_slab - 1 - k if reverse else k)
            xr, xi = xr_ref[rows, cols], xi_ref[rows, cols]
            for j, sh in enumerate((1, 2, 4)):
                mr, mi = con[2 * j], con[2 * j + 1]
                pr = pltpu.roll(xr, SUB - sh if reverse else sh, 0)
                pi = pltpu.roll(xi, SUB - sh if reverse else sh, 0)
                xr, xi = xr + mr * pr - mi * pi, xi + mr * pi + mi * pr
            xr, xi = xr + con[6] * cr - con[7] * ci, xi + con[6] * ci + con[7] * cr
            xr_ref[rows, cols] = xr
            xi_ref[rows, cols] = xi
            row = 0 if reverse else SUB - 1
            return _bcast_row(xr, row), _bcast_row(xi, row)

        cr, ci = lax.fori_loop(0, n_slab, step, (cr_ref[:, cols], ci_ref[:, cols]))
        cr_ref[:, cols] = cr
        ci_ref[:, cols] = ci


def _s5_fwd(ua, bbr, bbi, ccr, cci, dsk, con, w_glu, b_glu, carry=None):
    L = ua.shape[0]

    def body(ua_ref, bbr_hbm, bbi_hbm, ccr_hbm, cci_hbm, dsk_ref, con_ref, wg_ref, bg_ref,
             sr_ref, si_ref, y_ref, zg_ref, ya_ref, bbr_vm, bbi_vm, ccr_vm, cci_vm, cr_ref, ci_ref, w_sems):
        landed = _resident([(bbr_hbm, bbr_vm), (bbi_hbm, bbi_vm), (ccr_hbm, ccr_vm), (cci_hbm, cci_vm)], w_sems)

        @pl.when(pl.program_id(0) == 0)
        def _():
            cr_ref[...] = jnp.zeros_like(cr_ref)
            ci_ref[...] = jnp.zeros_like(ci_ref)

        u = ua_ref[...]
        ub = u.astype(BF)
        landed(0)
        sr_ref[...] = _blockdiag_mm(ub, bbr_vm)
        landed(1)
        si_ref[...] = _blockdiag_mm(ub, bbi_vm)
        _cscan(sr_ref, si_ref, con_ref, cr_ref, ci_ref, reverse=False)
        landed(2)
        landed(3)
        y = (_blockdiag_mm_t(sr_ref[...].astype(BF), ccr_vm) - _blockdiag_mm_t(si_ref[...].astype(BF), cci_vm)
             + dsk_ref[...] * u)
        y_ref[...] = y
        zg = jax.nn.gelu(y)
        zg_ref[...] = zg.astype(BF)
        q = _mm(zg, wg_ref[...]) + bg_ref[...]
        ya_ref[...] = (zg * _sig(q)).astype(BF)

    return _pallas_call(
        body, carry, name="s5_fwd", grid=(L // TM,),
        in_specs=[_tok(S5W), ANY, ANY, ANY, ANY, _full((1, S5W)), _full((8 * SUB, GN)),
                  _full((S5W, S5W)), _full((1, S5W))],
        out_specs=[_tok(GN), _tok(GN), _tok(S5W), _tok(S5W), _tok(S5W)],
        out_shape=[_sds((L, GN)), _sds((L, GN)), _far((L, S5W)), _far((L, S5W), BF), _far((L, S5W), BF)],
        scratch_shapes=[pltpu.VMEM((S5W // 128, 128, GN // (S5W // 128)), BF)] * 4 + [
                        pltpu.VMEM((SUB, GN), F32), pltpu.VMEM((SUB, GN), F32),
                        pltpu.SemaphoreType.DMA((4,))],
        compiler_params=_params(44),
    )(*_in_hbm([ua, bbr, bbi, ccr, cci]), dsk, con, w_glu, b_glu)


def _s5_bwd(dya, y, ua, sr, si, bbr, bbi, ccr, cci, dsk, con_rev, w_glu, b_glu, carry=None):
    L = ua.shape[0]
    nt = L // TM
    spt = TM // SUB
    n_slab = spt

    def halo_map(i):
        return (jnp.maximum((nt - 1 - i) * spt - 1, 0), 0)

    def body(dya_ref, y_ref, ua_ref, sr_ref, si_ref, hr_ref, hi_ref, bbr_hbm, bbi_hbm, ccr_hbm, cci_hbm,
             dsk_ref, con_ref, wg_ref, bg_ref,
             dua_ref, dq_ref, dy_ref, lr_ref, li_ref, da_ref, dsm_ref,
             bbr_vm, bbi_vm, ccr_vm, cci_vm, cr_ref, ci_ref, w_sems):
        i = pl.program_id(0)
        landed = _resident([(ccr_hbm, ccr_vm), (cci_hbm, cci_vm), (bbr_hbm, bbr_vm), (bbi_hbm, bbi_vm)], w_sems)

        @pl.when(i == 0)
        def _():
            cr_ref[...] = jnp.zeros_like(cr_ref)
            ci_ref[...] = jnp.zeros_like(ci_ref)
            da_ref[...] = jnp.zeros_like(da_ref)
            dsm_ref[...] = jnp.zeros_like(dsm_ref)

        u = ua_ref[...]
        yv = y_ref[...]
        dya = dya_ref[...]
        zg = jax.nn.gelu(yv)
        sg = _sig(_mm(zg, wg_ref[...]) + bg_ref[...])
        dq = dya * zg * sg * (1.0 - sg)
        dq_ref[...] = dq.astype(BF)
        dzg = dya * sg + _mm_nt(dq, wg_ref[...])
        dy = dzg * _gelu_grad(yv)
        dyb = dy.astype(BF)
        dy_ref[...] = dyb
        dsm_ref[0:1, :] += _colsum(dy * u)
        dsm_ref[1:2, :] += _colsum(dq)
        landed(0)
        lr_ref[...] = _blockdiag_mm(dyb, ccr_vm)
        landed(1)
        li_ref[...] = -_blockdiag_mm(dyb, cci_vm)
        _cscan(lr_ref, li_ref, con_ref, cr_ref, ci_ref, reverse=True)

        first_tile = (i == nt - 1)
        row = _row_iota(LC)
        for lc in range(GN // LC):
            cols = slice(lc * LC, (lc + 1) * LC)
            h_r = jnp.where(first_tile, 0.0, hr_ref[:, cols])
            h_i = jnp.where(first_tile, 0.0, hi_ref[:, cols])

            def step(k, acc, cols=cols, h_r=h_r, h_i=h_i):
                ar, ai = acc
                rows = _slab(k)
                prev = _slab(jnp.maximum(k - 1, 0))
                pr = jnp.where(k == 0, h_r, sr_ref[prev, cols])
                pi = jnp.where(k == 0, h_i, si_ref[prev, cols])
                spr = pltpu.roll(jnp.where(row == SUB - 1, pr, sr_ref[rows, cols]), 1, 0)
                spi = pltpu.roll(jnp.where(row == SUB - 1, pi, si_ref[rows, cols]), 1, 0)
                lr, li = lr_ref[rows, cols], li_ref[rows, cols]
                return ar + lr * spr + li * spi, ai + li * spr - lr * spi

            zero = jnp.zeros((SUB, LC), F32)
            ar, ai = lax.fori_loop(0, n_slab, step, (zero, zero))
            da_ref[0:1, cols] += _colsum(ar)
            da_ref[1:2, cols] += _colsum(ai)

        landed(2)
        landed(3)
        dua_ref[...] = (dy * dsk_ref[...] + _blockdiag_mm_t(lr_ref[...].astype(BF), bbr_vm)
                        + _blockdiag_mm_t(li_ref[...].astype(BF), bbi_vm))

    return _pallas_call(
        body, carry, name="s5_bwd", grid=(nt,),
        in_specs=[_tok_rev(S5W, nt), _tok_rev(S5W, nt), _tok_rev(S5W, nt), _tok_rev(GN, nt), _tok_rev(GN, nt),
                  pl.BlockSpec((SUB, GN), halo_map), pl.BlockSpec((SUB, GN), halo_map),
                  ANY, ANY, ANY, ANY, _full((1, S5W)), _full((8 * SUB, GN)), _full((S5W, S5W)), _full((1, S5W))],
        out_specs=[_tok_rev(S5W, nt), _tok_rev(S5W, nt), _tok_rev(S5W, nt), _tok_rev(GN, nt), _tok_rev(GN, nt),
                   _full((SUB, GN)), _full((SUB, S5W))],
        out_shape=[_sds((L, S5W)), _sds((L, S5W), BF), _sds((L, S5W), BF), _sds((L, GN)), _sds((L, GN)),
                   _sds((SUB, GN)), _sds((SUB, S5W))],
        scratch_shapes=[pltpu.VMEM((S5W // 128, 128, GN // (S5W // 128)), BF)] * 4 + [
                        pltpu.VMEM((SUB, GN), F32), pltpu.VMEM((SUB, GN), F32),
                        pltpu.SemaphoreType.DMA((4,))],
        compiler_params=_params(52),
    )(dya, y, ua, sr, si, sr, si, *_in_hbm([bbr, bbi, ccr, cci]), dsk, con_rev, w_glu, b_glu)


def _lru_gate_terms(rg, sp):
    log_a = -LRU_C * rg * sp
    a = jnp.exp(log_a)
    mult = jnp.sqrt(_neg_expm1(2.0 * log_a))
    return a, mult


def _lru_fwd(ub, conv_w, conv_b, wr, wi, b_r, b_i, sp, carry=None):
    L = ub.shape[0]
    n_slab = TM // SUB

    def body(ub_ref, cw_ref, cb_ref, wr_ref, wi_ref, br_ref, bi_ref, sp_ref,
             xc_ref, rg_ref, ig_ref, h_ref, hp_ref, a_ref, halo_ref, carry_ref):
        @pl.when(pl.program_id(0) == 0)
        def _():
            halo_ref[...] = jnp.zeros_like(halo_ref)
            carry_ref[...] = jnp.zeros_like(carry_ref)

        row = _row_iota(LW)
        taps = [cw_ref[k:k + 1, :] for k in range(4)]
        cb = cb_ref[...]

        def conv_step(k, prev):
            rows = _slab(k)
            cur = ub_ref[rows, :]
            acc = taps[3] * cur + cb
            for j in (1, 2, 3):
                acc = acc + taps[3 - j] * pltpu.roll(jnp.where(row >= SUB - j, prev, cur), j, 0)
            xc_ref[rows, :] = acc
            return cur

        halo_ref[...] = lax.fori_loop(0, n_slab, conv_step, halo_ref[...])

        xc = xc_ref[...]
        xcb = xc.astype(BF)
        rg = _sig(_blockdiag_mm(xcb, wr_ref) + br_ref[...])
        ig = _sig(_blockdiag_mm(xcb, wi_ref) + bi_ref[...])
        rg_ref[...] = rg
        ig_ref[...] = ig
        a, mult = _lru_gate_terms(rg, sp_ref[...])
        a_ref[...] = a
        h_ref[...] = mult * ig * xc

        rowc = _row_iota(LC)
        for lc in range(LW // LC):
            cols = slice(lc * LC, (lc + 1) * LC)

            def step(k, c, cols=cols):
                rows = _slab(k)
                av, b = a_ref[rows, cols], h_ref[rows, cols]
                for sh in (1, 2, 4):
                    keep = rowc >= sh
                    b = b + av * jnp.where(keep, pltpu.roll(b, sh, 0), 0.0)
                    av = av * jnp.where(keep, pltpu.roll(av, sh, 0), 1.0)
                h = b + av * c
                h_ref[rows, cols] = h
                hp_ref[rows, cols] = jnp.where(rowc == 0, c, pltpu.roll(h, 1, 0))
                return _bcast_row(h, SUB - 1)

            carry_ref[:, cols] = lax.fori_loop(0, n_slab, step, carry_ref[:, cols])

    return _pallas_call(
        body, carry, name="lru_fwd", grid=(L // TM,),
        in_specs=[_tok(LW), _full((4, LW)), _full((1, LW)), _full((LW // 128, 128, 128)), _full((LW // 128, 128, 128)),
                  _full((1, LW)), _full((1, LW)), _full((1, LW))],
        out_specs=[_tok(LW)] * 5,
        out_shape=[_far((L, LW))] * 5,
        scratch_shapes=[pltpu.VMEM((TM, LW), F32), pltpu.VMEM((SUB, LW), F32), pltpu.VMEM((SUB, LW), F32)],
        compiler_params=_params(40),
    )(*_in_hbm([ub]), conv_w, conv_b, wr, wi, b_r, b_i, sp)


def _lru_bwd(dyb, xc, rg, ig, hp, ub, conv_w, wr, wi, sp, dsp, carry=None):
    L = ub.shape[0]
    nt = L // TM
    spt = TM // SUB
    n_slab = spt

    def halo_map(i):
        return (jnp.maximum((nt - 1 - i) * spt - 1, 0), 0)

    def body(dh_ref, xc_ref, rg_ref, ig_ref, hp_ref, ub_ref, uh_ref, cw_ref, wr_ref, wi_ref, sp_ref, dsp_ref,
             dub_ref, dpr_ref, dpi_ref, acc_ref, a_ref, lam_ref, dxc_ref, carry_ref, next_ref):
        i = pl.program_id(0)

        @pl.when(i == 0)
        def _():
            carry_ref[...] = jnp.zeros_like(carry_ref)
            next_ref[...] = jnp.zeros_like(next_ref)
            acc_ref[...] = jnp.zeros_like(acc_ref)

        sp = sp_ref[...]
        rg, ig, xc = rg_ref[...], ig_ref[...], xc_ref[...]
        a, mult = _lru_gate_terms(rg, sp)
        a_ref[...] = a

        rowc = _row_iota(LC)
        for lc in range(LW // LC):
            cols = slice(lc * LC, (lc + 1) * LC)

            def step(k, c, cols=cols):
                rows = _slab(n_slab - 1 - k)
                av, dh = a_ref[rows, cols], dh_ref[rows, cols]
                b = av * dh
                for sh in (1, 2, 4):
                    keep = rowc < SUB - sh
                    b = b + av * jnp.where(keep, pltpu.roll(b, SUB - sh, 0), 0.0)
                    av = av * jnp.where(keep, pltpu.roll(av, SUB - sh, 0), 1.0)
                mu = b + av * c
                lam_ref[rows, cols] = dh + jnp.where(rowc == SUB - 1, c, pltpu.roll(mu, SUB - 1, 0))
                return _bcast_row(mu, 0)

            carry_ref[:, cols] = lax.fori_loop(0, n_slab, step, carry_ref[:, cols])

        lam = lam_ref[...]
        d_a = lam * hp_ref[...]
        d_mult = lam * ig * xc
        d_ig = lam * mult * xc
        dxc = lam * mult * ig
        d_log_a = d_a * a - d_mult * a * a / mult
        d_rg = (-LRU_C) * sp * d_log_a
        acc_ref[0:1, :] += _colsum((-LRU_C) * rg * d_log_a) * dsp_ref[...]
        dpr = d_rg * rg * (1.0 - rg)
        dpi = d_ig * ig * (1.0 - ig)
        acc_ref[1:2, :] += _colsum(dpr)
        acc_ref[2:3, :] += _colsum(dpi)
        dprb, dpib = dpr.astype(BF), dpi.astype(BF)
        dpr_ref[...] = dprb
        dpi_ref[...] = dpib
        dxc = dxc + _blockdiag_mm_t(dprb, wr_ref) + _blockdiag_mm_t(dpib, wi_ref)
        dxc_ref[...] = dxc
        acc_ref[3:4, :] += _colsum(dxc)

        row = _row_iota(LW)
        taps = [cw_ref[k:k + 1, :] for k in range(4)]
        u_halo = jnp.where(i == nt - 1, 0.0, uh_ref[...])
        nxt_tile = next_ref[...]

        def conv_step(k, accs):
            rows = _slab(k)
            cur = dxc_ref[rows, :]
            nxt = jnp.where(k == n_slab - 1, nxt_tile, dxc_ref[_slab(jnp.minimum(k + 1, n_slab - 1)), :])
            ucur = ub_ref[rows, :]
            uprev = jnp.where(k == 0, u_halo, ub_ref[_slab(jnp.maximum(k - 1, 0)), :])
            du = taps[3] * cur
            new = [accs[3] + cur * ucur]
            for j in (1, 2, 3):
                du = du + taps[3 - j] * pltpu.roll(jnp.where(row < j, nxt, cur), SUB - j, 0)
                new.append(accs[3 - j] + cur * pltpu.roll(jnp.where(row >= SUB - j, uprev, ucur), j, 0))
            dub_ref[rows, :] = du
            return tuple(new[::-1])

        zero = jnp.zeros((SUB, LW), F32)
        accs = lax.fori_loop(0, n_slab, conv_step, (zero, zero, zero, zero))
        for k in range(4):
            acc_ref[4 + k:5 + k, :] += _colsum(accs[k])
        next_ref[...] = dxc_ref[0:SUB, :]

    return _pallas_call(
        body, carry, name="lru_bwd", grid=(nt,),
        in_specs=[_tok_rev(LW, nt)] * 6 + [pl.BlockSpec((SUB, LW), halo_map), _full((4, LW)),
                                           _full((LW // 128, 128, 128)), _full((LW // 128, 128, 128)), _full((1, LW)), _full((1, LW))],
        out_specs=[_tok_rev(LW, nt), _tok_rev(LW, nt), _tok_rev(LW, nt), _full((SUB, LW))],
        out_shape=[_sds((L, LW)), _far((L, LW), BF), _far((L, LW), BF), _sds((SUB, LW))],
        scratch_shapes=[pltpu.VMEM((TM, LW), F32), pltpu.VMEM((TM, LW), F32), pltpu.VMEM((TM, LW), F32),
                        pltpu.VMEM((SUB, LW), F32), pltpu.VMEM((SUB, LW), F32)],
        compiler_params=_params(48),
    )(dyb, xc, rg, ig, hp, ub, ub, conv_w, wr, wi, sp, dsp)


AC = D // NCHIP


def _merge_fwd(x, ya, yb, gp, w_a, w_b, w_o, carry=None):
    L = x.shape[0]

    def body(x_ref, ya_ref, yb_ref, gp_ref, wa_ref, wb_ref, wo_ref, x1_ref, pa_ref, pb_ref, mg_ref):
        ya = ya_ref[...]
        for k in range(NCHIP):
            pa_ref[:, k * AC:(k + 1) * AC] = jnp.dot(ya, wa_ref[k], preferred_element_type=F32)
        pb = _mm(yb_ref[...], wb_ref[...])
        pb_ref[...] = pb
        gp = gp_ref[...]
        merged = (_sig(gp[:, :D]) * pa_ref[...] + _sig(gp[:, D:]) * pb).astype(BF)
        mg_ref[...] = merged
        x1_ref[...] = x_ref[...] + jnp.dot(merged, wo_ref[...], preferred_element_type=F32)

    return _pallas_call(
        body, carry, name="merge_fwd", grid=(L // TM,),
        in_specs=[_tok(D), _tok(S5W), _tok(LW), _tok(2 * D), _full((NCHIP, S5W, AC)), _full((LW, D)), _full((D, D))],
        out_specs=[_tok(D), _tok(D), _tok(D), _tok(D)],
        out_shape=[_sds((L, D)), _sds((L, D)), _sds((L, D)), _far((L, D), BF)],
        compiler_params=_params(40),
    )(x, ya, yb, gp, w_a, w_b, w_o)


def _merge_bwd(dx1, gp, pa, pb, w_a, w_b, w_o, carry=None):
    L = dx1.shape[0]

    def body(dx1_ref, gp_ref, pa_ref, pb_ref, wa_ref, wb_ref, wo_ref, dya_ref, dyb_ref, dgp_ref, dpa_ref, dpb_ref):
        dm = _mm_nt(dx1_ref[...], wo_ref[...])
        gp = gp_ref[...]
        sa, sb = _sig(gp[:, :D]), _sig(gp[:, D:])
        dpa = (dm * sa).astype(BF)
        dpb = (dm * sb).astype(BF)
        dpa_ref[...] = dpa
        dpb_ref[...] = dpb
        dgp_ref[:, :D] = dm * pa_ref[...] * sa * (1.0 - sa)
        dgp_ref[:, D:] = dm * pb_ref[...] * sb * (1.0 - sb)
        dya = jnp.zeros((TM, S5W), F32)
        for k in range(NCHIP):
            dya = dya + _mm_nt(dpa[:, k * AC:(k + 1) * AC], wa_ref[k])
        dya_ref[...] = dya
        dyb_ref[...] = _mm_nt(dpb, wb_ref[...])

    return _pallas_call(
        body, carry, name="merge_bwd", grid=(L // TM,),
        in_specs=[_tok(D), _tok(2 * D), _tok(D), _tok(D), _full((NCHIP, S5W, AC)), _full((LW, D)), _full((D, D))],
        out_specs=[_tok(S5W), _tok(LW), _tok(2 * D), _tok(D), _tok(D)],
        out_shape=[_far((L, S5W)), _far((L, LW)), _sds((L, 2 * D)), _far((L, D), BF), _far((L, D), BF)],
        compiler_params=_params(40),
    )(dx1, gp, pa, pb, w_a, w_b, w_o)


def _chunk_tok(width):
    return pl.BlockSpec((NCHIP, TM, width), lambda i: (0, i, 0))


def _row_halves_copies(halves, whole_vm):
    half = whole_vm.shape[1] // 2
    return [(src.at[c], whole_vm.at[c, pl.ds(k * half, half)]) for c in range(NCHIP) for k, src in enumerate(halves)]


def _ffn_up_fwd(x1, g_ffn, wg, wu, carry=None):
    L = x1.shape[0]

    def body(x_ref, g_ref, wg_hbm, wu_lo_hbm, wu_hi_hbm, h2_ref, gg_ref, uu_ref, wg_vm, wu_vm, w_sems):
        _resident_now([(wg_hbm.at[c], wg_vm.at[c]) for c in range(NCHIP)]
                      + _row_halves_copies((wu_lo_hbm, wu_hi_hbm), wu_vm), w_sems)
        xh, _ = _rms(x_ref[...])
        h2 = (xh * g_ref[...]).astype(BF)
        h2_ref[...] = h2
        for c in range(NCHIP):
            gg_ref[c] = lax.dot_general(h2, wg_vm[c], (((1,), (1,)), ((), ())), preferred_element_type=F32).astype(BF)
            uu_ref[c] = lax.dot_general(h2, wu_vm[c], (((1,), (1,)), ((), ())), preferred_element_type=F32).astype(BF)

    return _pallas_call(
        body, carry, name="ffn_up_fwd", grid=(L // TM,),
        in_specs=[_tok(D), _full((1, D)), ANY, ANY, ANY],
        out_specs=[_tok(D), _chunk_tok(FC), _chunk_tok(FC)],
        out_shape=[_far((L, D), BF), _sds((NCHIP, L, FC), BF), _sds((NCHIP, L, FC), BF)],
        scratch_shapes=[pltpu.VMEM((NCHIP, FC, D), BF)] * 2 + [pltpu.SemaphoreType.DMA((3 * NCHIP,))],
        compiler_params=_params(44),
    )(x1, g_ffn, wg, *wu)


def _ffn_down_fwd(x1, gg, uu, wd):
    L = x1.shape[0]

    def body(x_ref, gg_ref, uu_ref, wd_hbm, x2_ref, wd_vm, w_sems):
        _resident_now([(wd_hbm.at[c], wd_vm.at[c]) for c in range(NCHIP)], w_sems)
        out = x_ref[...]
        for c in range(NCHIP):
            g = gg_ref[c].astype(F32)
            act = (g * _sig(g) * uu_ref[c].astype(F32)).astype(BF)
            out = out + jnp.dot(act, wd_vm[c], preferred_element_type=F32)
        x2_ref[...] = out

    return _pallas_call(
        body, name="ffn_down_fwd", grid=(L // TM,),
        in_specs=[_tok(D), _chunk_tok(FC), _chunk_tok(FC), ANY], out_specs=[_tok(D)],
        out_shape=[_far((L, D))],
        scratch_shapes=[pltpu.VMEM((NCHIP, FC, D), BF), pltpu.SemaphoreType.DMA((NCHIP,))],
        compiler_params=_params(40),
    )(x1, gg, uu, *_in_hbm([wd]))[0]


def _ffn_bwd(x1, dx2, gg, uu, g_ffn, wg, wu, wd, carry=None):
    L = x1.shape[0]

    def body(x_ref, dx2_ref, gg_ref, uu_ref, g_ref, wg_hbm, wu_lo_hbm, wu_hi_hbm, wd_hbm,
             dx1_ref, act_ref, dgg_ref, duu_ref, dg_ref, wg_vm, wu_vm, wd_vm, w_sems):
        _resident_now([(src.at[c], dst.at[c]) for c in range(NCHIP) for src, dst in ((wg_hbm, wg_vm), (wd_hbm, wd_vm))]
                      + _row_halves_copies((wu_lo_hbm, wu_hi_hbm), wu_vm), w_sems)

        @pl.when(pl.program_id(0) == 0)
        def _():
            dg_ref[...] = jnp.zeros_like(dg_ref)

        dx2 = dx2_ref[...]
        dx2b = dx2.astype(BF)
        dh2 = jnp.zeros((TM, D), F32)
        for c in range(NCHIP):
            g = gg_ref[c].astype(F32)
            u = uu_ref[c].astype(F32)
            s = _sig(g)
            silu = g * s
            act_ref[c] = (silu * u).astype(BF)
            dact = lax.dot_general(dx2b, wd_vm[c], (((1,), (1,)), ((), ())), preferred_element_type=F32)
            dg = (dact * u * s * (1.0 + g * (1.0 - s))).astype(BF)
            du = (dact * silu).astype(BF)
            dgg_ref[c] = dg
            duu_ref[c] = du
            dh2 = dh2 + jnp.dot(dg, wg_vm[c], preferred_element_type=F32)
            dh2 = dh2 + jnp.dot(du, wu_vm[c], preferred_element_type=F32)
        xh, r = _rms(x_ref[...])
        dg_ref[0:1, :] += _colsum(dh2 * xh)
        dx1_ref[...] = dx2 + _rms_bwd(dh2, xh, r, g_ref[...])

    return _pallas_call(
        body, carry, name="ffn_bwd", grid=(L // TM,),
        in_specs=[_tok(D), _tok(D), _chunk_tok(FC), _chunk_tok(FC), _full((1, D)), ANY, ANY, ANY, ANY],
        out_specs=[_tok(D), _chunk_tok(FC), _chunk_tok(FC), _chunk_tok(FC), _full((SUB, D))],
        out_shape=[_sds((L, D)), _sds((NCHIP, L, FC), BF), _sds((NCHIP, L, FC), BF), _sds((NCHIP, L, FC), BF),
                   _sds((SUB, D))],
        scratch_shapes=[pltpu.VMEM((NCHIP, FC, D), BF)] * 3 + [pltpu.SemaphoreType.DMA((4 * NCHIP,))],
        compiler_params=_params(56),
    )(x1, dx2, gg, uu, g_ffn, wg, *wu, wd)


def _ple_loss(x2, p, tgt, g_pg, w_pg, b_pg, w_ple, g_ple, g_final):
    L = x2.shape[0]

    def body(x2_ref, p_ref, t_ref, gpg_ref, wpg_ref, bpg_ref, wple_ref, gple_ref, gf_ref,
             dx2_ref, n2_ref, dpre_ref, de0_ref, acc_ref):
        @pl.when(pl.program_id(0) == 0)
        def _():
            acc_ref[...] = jnp.zeros_like(acc_ref)

        x2 = x2_ref[...]
        x2h, r2 = _rms(x2)
        n2 = (x2h * gpg_ref[...]).astype(BF)
        n2_ref[...] = n2
        gate = _sig(jnp.dot(n2, wpg_ref[...], preferred_element_type=F32) + bpg_ref[...])
        pb = p_ref[...].astype(BF)
        e0 = jnp.concatenate([jnp.dot(pb, wple_ref[k], preferred_element_type=F32) for k in range(NCHIP)], axis=1)
        e0h, re = _rms(e0)
        e = e0h * gple_ref[...]
        x3 = x2 + gate * e
        x3h, r3 = _rms(x3)
        diff = x3h * gf_ref[...] - t_ref[...]
        acc_ref[4:5, :] += _colsum(diff * diff) * (0.5 / D)
        dy = diff * (1.0 / D)
        acc_ref[3:4, :] += _colsum(dy * x3h)
        dx3 = _rms_bwd(dy, x3h, r3, gf_ref[...])
        de = dx3 * gate
        acc_ref[2:3, :] += _colsum(de * e0h)
        de0_ref[...] = _rms_bwd(de, e0h, re, gple_ref[...]).astype(BF)
        dpre = dx3 * e * gate * (1.0 - gate)
        acc_ref[1:2, :] += _colsum(dpre)
        dpreb = dpre.astype(BF)
        dpre_ref[...] = dpreb
        dn2 = lax.dot_general(dpreb, wpg_ref[...], (((1,), (1,)), ((), ())), preferred_element_type=F32)
        acc_ref[0:1, :] += _colsum(dn2 * x2h)
        dx2_ref[...] = dx3 + _rms_bwd(dn2, x2h, r2, gpg_ref[...])

    return _pallas_call(
        body, name="ple_loss", grid=(L // TM,),
        in_specs=[_tok(D), _tok(PLE), _tok(D), _full((1, D)), _full((D, D)), _full((1, D)), _full((NCHIP, PLE, AC)),
                  _full((1, D)), _full((1, D))],
        out_specs=[_tok(D), _tok(D), _tok(D), _tok(D), _full((SUB, D))],
        out_shape=[_sds((L, D)), _sds((L, D), BF), _sds((L, D), BF), _sds((L, D), BF), _sds((SUB, D))],
        compiler_params=_params(40),
    )(x2, p, tgt, g_pg, *_in_hbm([w_pg]), b_pg, *_in_hbm([w_ple]), g_ple, g_final)


def _tn(name, a, b, col_chunk=None, a_block=None, carry=None):
    L = a.shape[-2]
    m, n = a.shape[-1], b.shape[-1]
    a_col = 0
    if a_block is not None:
        a_col, m = a_block
    tk = L if (a.ndim == 3 or b.ndim == 3 or a_block is not None) else TK
    if a.ndim == 3 or b.ndim == 3:
        nj, bn = (a if a.ndim == 3 else b).shape[0], n
        a_spec = (pl.BlockSpec((None, tk, m), lambda j, t: (j, t, 0)) if a.ndim == 3
                  else pl.BlockSpec((tk, m), lambda j, t: (t, 0)))
        b_spec = (pl.BlockSpec((None, tk, n), lambda j, t: (j, t, 0)) if b.ndim == 3
                  else pl.BlockSpec((tk, n), lambda j, t: (t, 0)))
        out_spec, out_shape = pl.BlockSpec((None, m, n), lambda j, t: (j, 0, 0)), _sds((nj, m, n))
    else:
        bn = col_chunk
        if bn is None:
            bn = next((cand for cand in (1024, 512) if n > cand and n % cand == 0), n)
        nj = n // bn
        a_spec = pl.BlockSpec((tk, m), lambda j, t: (t, a_col))
        b_spec = pl.BlockSpec((tk, bn), lambda j, t: (t, j))
        if col_chunk is None:
            out_spec, out_shape = pl.BlockSpec((m, bn), lambda j, t: (0, j)), _sds((m, n))
        else:
            out_spec, out_shape = pl.BlockSpec((None, m, bn), lambda j, t: (j, 0, 0)), _sds((nj, m, bn))

    def body(a_ref, b_ref, o_ref):
        if tk == L:
            o_ref[...] = _mm_tn(a_ref[...], b_ref[...])
        else:
            @pl.when(pl.program_id(1) == 0)
            def _():
                o_ref[...] = jnp.zeros_like(o_ref)

            o_ref[...] += _mm_tn(a_ref[...], b_ref[...])

    outs = _pallas_call(
        body, carry, name=name, grid=(nj, L // tk), in_specs=[a_spec, b_spec], out_specs=[out_spec],
        out_shape=[pltpu.HBM(out_shape.shape, out_shape.dtype)],
        compiler_params=pltpu.CompilerParams(dimension_semantics=("arbitrary", "arbitrary"),
                                             vmem_limit_bytes=(30 if tk == L else 28) * VMEM_MB),
    )(*(_in_hbm([a, b]) if tk == L else (a, b)))
    return outs[0] if carry is None else outs


LANE = 128


def _tn_blocks(name, a, bs, ga, gb, carry=None):
    L, m, n, nb = a.shape[0], a.shape[1], bs[0].shape[1], len(bs)
    per = LANE // ga
    wb = per * gb
    n_super = m // LANE

    def body(a_ref, *refs):
        b_refs, o_refs, acc_refs = refs[:nb], refs[nb:2 * nb], refs[2 * nb:]
        t = pl.program_id(0)

        @pl.when(t == 0)
        def _():
            for acc in acc_refs:
                acc[...] = jnp.zeros_like(acc)

        lhs = a_ref[...].astype(BF)
        for b_ref, acc in zip(b_refs, acc_refs):
            rhs = b_ref[...].astype(BF)
            for j in range(n_super):
                acc[j] += _mm_tn(lhs[:, j * LANE:(j + 1) * LANE], rhs[:, j * wb:(j + 1) * wb])

        @pl.when(t == L // TK - 1)
        def _():
            own = (lax.broadcasted_iota(jnp.int32, (LANE, wb), 0) // ga) == (lax.broadcasted_iota(jnp.int32, (LANE, wb), 1) // gb)
            for o_ref, acc in zip(o_refs, acc_refs):
                for j in range(n_super):
                    kept = jnp.where(own, acc[j], 0.0)
                    o_ref[:, j * wb:(j + 1) * wb] = jnp.sum(kept.reshape(per, ga, wb), axis=0)

    outs = _pallas_call(
        body, carry, name=name, grid=(L // TK,),
        in_specs=[pl.BlockSpec((TK, m), lambda t: (t, 0))] + [pl.BlockSpec((TK, n), lambda t: (t, 0))] * nb,
        out_specs=[_full((ga, n))] * nb, out_shape=[_sds((ga, n))] * nb,
        scratch_shapes=[pltpu.VMEM((n_super, LANE, wb), F32)] * nb,
        compiler_params=_params(48),
    )(*_in_hbm([a] + list(bs)))
    return list(outs)


def _s5_discretize(lam_re, lam_im, log_dt, b_re, b_im):
    dt = jnp.exp(log_dt)[:, None]
    mag = jnp.exp(lam_re * dt)
    ar = mag * jnp.cos(lam_im * dt)
    ai = mag * jnp.sin(lam_im * dt)
    den = lam_re * lam_re + lam_im * lam_im
    nr = ar - 1.0
    fr = (nr * lam_re + ai * lam_im) / den
    fi = (ai * lam_re - nr * lam_im) / den
    bbr = fr[:, None, :] * b_re - fi[:, None, :] * b_im
    bbi = fr[:, None, :] * b_im + fi[:, None, :] * b_re
    return ar, ai, bbr, bbi


def _prepare(by_rows, block_cols, ar, ai):
    n = len(by_rows)

    def body(*refs):
        srcs, (ar_ref, ai_ref), dense, (con_ref, rev_ref) = refs[:n], refs[n:n + 2], refs[n + 2:2 * n + 2], refs[2 * n + 2:]
        for src, out, c in zip(srcs, dense, block_cols):
            r = src.shape[0]
            per = LANE // r
            wide = per * c
            own = (lax.broadcasted_iota(jnp.int32, (LANE, wide), 0) // r) == (lax.broadcasted_iota(jnp.int32, (LANE, wide), 1) // c)
            for j in range(out.shape[0]):
                tiled = jnp.broadcast_to(src[:, j * wide:(j + 1) * wide][None], (per, r, wide)).reshape(LANE, wide)
                out[j] = jnp.where(own, tiled, 0.0).astype(BF)
        a_r, a_i = ar_ref[...], ai_ref[...]
        pw = [(jnp.ones_like(a_r), jnp.zeros_like(a_i))]
        for _ in range(SUB):
            pr, pi = pw[-1]
            pw.append((pr * a_r - pi * a_i, pr * a_i + pi * a_r))
        row = _row_iota(GN)
        for ref, reverse in ((con_ref, False), (rev_ref, True)):
            sign = -1.0 if reverse else 1.0
            for j, sh in enumerate((1, 2, 4)):
                keep = (row < SUB - sh) if reverse else (row >= sh)
                ref[2 * j * SUB:(2 * j + 1) * SUB, :] = jnp.where(keep, pw[sh][0], 0.0)
                ref[(2 * j + 1) * SUB:(2 * j + 2) * SUB, :] = jnp.where(keep, sign * pw[sh][1], 0.0)
            p_r, p_i = jnp.zeros((SUB, GN), F32), jnp.zeros((SUB, GN), F32)
            for i in range(SUB):
                k = SUB - i if reverse else i + 1
                p_r = jnp.where(row == i, pw[k][0], p_r)
                p_i = jnp.where(row == i, sign * pw[k][1], p_i)
            ref[6 * SUB:7 * SUB, :] = p_r
            ref[7 * SUB:8 * SUB, :] = p_i

    dense_shapes = [(b.shape[1] // (LANE // b.shape[0] * c), LANE, LANE // b.shape[0] * c)
                    for b, c in zip(by_rows, block_cols)]
    outs = _pallas_call(
        body, name="prepare", grid=(1,), in_specs=[_full(b.shape) for b in by_rows] + [_full((1, GN))] * 2,
        out_specs=[_full(s) for s in dense_shapes] + [_full((8 * SUB, GN))] * 2,
        out_shape=[_far(s, BF) for s in dense_shapes] + [_sds((8 * SUB, GN)), _far((8 * SUB, GN))],
        compiler_params=_params(48),
    )(*by_rows, ar, ai)
    return outs[:n], outs[n], outs[n + 1]


def _local_step(x, p, tgt, w, comm):
    rows_of = lambda a: a.reshape(NCHIP * a.shape[1], a.shape[2])
    quarters = lambda a: a.reshape(NCHIP, a.shape[0] // NCHIP, a.shape[1])

    def gathering(names, call):
        carry = comm.gather(names)
        outs = list(call(carry))
        own = len(outs) - len(carry.out_shapes)
        w.update(zip(names, outs[own:]))
        return outs[:own]

    w.update(comm.first())
    ar, ai, bbr, bbi = _s5_discretize(w["lam_re"], w["lam_im"], w["log_dt"], w["s5_b_re"], w["s5_b_im"])
    by_row = lambda b: jnp.transpose(b, (1, 0, 2)).reshape(b.shape[1], -1)
    (bbr_d, bbi_d, ccr_d, cci_d, wr_d, wi_d), con, con_rev = _prepare(
        [by_row(b) for b in (bbr, bbi, w["s5_c_re"], w["s5_c_im"], w["w_r"], w["w_i"])], [NS] * 4 + [HD] * 2,
        ar.reshape(1, GN), ai.reshape(1, GN))
    dsk = w["s5_d"].reshape(1, S5W)
    lam = w["lru_lambda"].reshape(1, LW)
    sp = jax.nn.softplus(-lam)
    b_r, b_i = w["b_r"].reshape(1, LW), w["b_i"].reshape(1, LW)
    row = lambda name: w[name].reshape(1, -1)

    h, ua, ub, gp = gathering(["w_glu", "w_a_out", "w_b_out"], lambda carry: _inproj_fwd(
        x, row("g_mix"), w["w_in"], row("b_in"), carry))
    w_glu = rows_of(w["w_glu"])
    sr, si, y, zg, ya = gathering(["w_o", "w_ffn_gate"], lambda carry: _s5_fwd(
        ua, bbr_d, bbi_d, ccr_d, cci_d, dsk, con, w_glu, row("b_glu"), carry))
    xc, rg, ig, yb, hp = gathering(["w_ffn_up_lo"], lambda carry: _lru_fwd(
        ub, w["conv_w"], row("conv_b"), wr_d, wi_d, b_r, b_i, sp, carry))
    w_b_out, w_o = rows_of(w["w_b_out"]), rows_of(w["w_o"])
    x1, pa, pb, merged = gathering(["w_ffn_up_hi", "w_ple_gate", "w_ple"], lambda carry: _merge_fwd(
        x, ya, yb, gp, w["w_a_out"], w_b_out, w_o, carry))
    w_ffn_up = (w["w_ffn_up_lo"], w["w_ffn_up_hi"])
    h2, gg, uu = gathering(["w_ffn_down"], lambda carry: _ffn_up_fwd(
        x1, row("g_ffn"), w["w_ffn_gate"], w_ffn_up, carry))
    x2 = _ffn_down_fwd(x1, gg, uu, w["w_ffn_down"])
    w_pg = rows_of(w["w_ple_gate"])
    dx2, n2, dpre, de0, acc_p = _ple_loss(x2, p, tgt, row("g_ple_gate"), w_pg, row("b_ple_gate"),
                                          w["w_ple"], row("g_ple"), row("g_final"))
    comm.reduce("ple", {"w_ple_gate": quarters(_tn("dw_ple_gate", n2, dpre)),
                        "w_ple": _tn("dw_ple", p, de0, col_chunk=AC)})
    dx1, act, dgg, duu, acc_f = comm.run(lambda carry: _ffn_bwd(
        x1, dx2, gg, uu, row("g_ffn"), w["w_ffn_gate"], w_ffn_up, w["w_ffn_down"], carry))
    comm.reduce("ffn_gate", {"w_ffn_gate": _tn("dw_ffn_gate", dgg, h2)})
    comm.reduce("w_o", {"w_o": quarters(_tn("dw_o", *_in_hbm([merged, dx1])))})
    comm.reduce("ffn_up", {"w_ffn_up": comm.run(lambda carry: _tn("dw_ffn_up", duu, h2, carry=carry))[0]})
    comm.reduce("ffn_down", {"w_ffn_down": comm.run(lambda carry: _tn("dw_ffn_down", act, dx2, carry=carry),
                                                    hold=("ffn_gate", "w_o"))[0]})
    dya, dyb, dgp, dpa, dpb = comm.run(lambda carry: _merge_bwd(
        dx1, gp, pa, pb, w["w_a_out"], w_b_out, w_o, carry), hold=("ffn_gate", "ffn_up"))
    comm.reduce("merge", {"w_a_out": _tn("dw_a_out", ya, dpa, col_chunk=AC), "w_b_out": quarters(_tn("dw_b_out", yb, dpb))})
    dua, dq, dy, lr, li, acc_a, acc_s = comm.run(lambda carry: _s5_bwd(
        dya, y, ua, sr, si, bbr_d, bbi_d, ccr_d, cci_d, dsk, con_rev, w_glu, row("b_glu"), carry), hold=("ffn_down",))
    dub, dpr, dpi, acc_l = comm.run(lambda carry: _lru_bwd(
        dyb, xc, rg, ig, hp, ub, w["conv_w"], wr_d, wi_d, sp, -_sig(-lam), carry))
    gx, dz, acc_g, acc_b = _inproj_bwd(x, dx1, dua, dub, dgp, row("g_mix"), w["w_in"])
    half = (D // 2,)
    comm.reduce("in_lo", {"w_in_lo": comm.run(lambda carry: _tn(
        "dw_in_lo", h, dz, col_chunk=QC, a_block=(0,) + half, carry=carry))[0]})
    comm.reduce("in_hi", {"w_in_hi": comm.run(lambda carry: _tn(
        "dw_in_hi", h, dz, col_chunk=QC, a_block=(1,) + half, carry=carry))[0], "w_glu": quarters(_tn("dw_glu", zg, dq))})
    d_wr, d_wi = comm.run(lambda carry: _tn_blocks("dw_r_i", xc, [dpr, dpi], HD, HD, carry))
    d_bbr, d_bbi = comm.run(lambda carry: _tn_blocks("d_bb", ua, [lr, li], NP, NS, carry))
    d_ccr, d_cci = comm.run(lambda carry: _tn_blocks("d_cc", dy, [sr, si], NP, NS, carry))
    comm.drain()
    sums = {"ple": acc_p, "ffn": acc_f, "mix": acc_g, "b_in": acc_b, "lru": acc_l, "s5": acc_s, "s5_a": acc_a}
    blocks = {"bb_re": d_bbr, "bb_im": d_bbi,
              "cc_re": d_ccr, "cc_im": d_cci,
              "w_r": d_wr, "w_i": d_wi}
    return gx, sums, blocks


def _replicated_grads(w, sums, blocks):
    grouped = lambda e, groups: jnp.transpose(e.reshape(e.shape[0], groups, -1), (1, 0, 2))
    d_ar, d_ai = sums["s5_a"][0].reshape(NG, NS), sums["s5_a"][1].reshape(NG, NS)
    d_bbr, d_bbi = grouped(blocks["bb_re"], NG), grouped(blocks["bb_im"], NG)
    _, vjp = jax.vjp(_s5_discretize, w["lam_re"], w["lam_im"], w["log_dt"], w["s5_b_re"], w["s5_b_im"])
    g = dict(zip(("lam_re", "lam_im", "log_dt", "s5_b_re", "s5_b_im"), vjp((d_ar, d_ai, d_bbr, d_bbi))))
    g["s5_c_re"] = grouped(blocks["cc_re"], NG)
    g["s5_c_im"] = -grouped(blocks["cc_im"], NG)
    g["w_r"], g["w_i"] = grouped(blocks["w_r"], NH), grouped(blocks["w_i"], NH)
    g["s5_d"] = sums["s5"][0].reshape(NG, NP)
    g["b_r"] = sums["lru"][1].reshape(NH, HD)
    g["b_i"] = sums["lru"][2].reshape(NH, HD)
    return g


ACC_ROWS = {"g_mix": ("mix", 0), "b_in": ("b_in", 0), "g_ffn": ("ffn", 0), "g_ple_gate": ("ple", 0),
            "b_ple_gate": ("ple", 1), "g_ple": ("ple", 2), "g_final": ("ple", 3), "b_glu": ("s5", 1),
            "lru_lambda": ("lru", 0), "conv_b": ("lru", 3)}
LOSS_ROW = ("ple", 4)
CONV_W_ROWS = ("lru", 4)


SHARDED = [("w_in", (D, QC)), ("w_glu", (S5W // NCHIP, S5W)), ("w_a_out", (S5W, AC)), ("w_b_out", (LW // NCHIP, D)),
           ("w_o", (D // NCHIP, D)), ("w_ffn_gate", (FC, D)), ("w_ffn_up", (FC, D)), ("w_ffn_down", (FC, D)),
           ("w_ple_gate", (D // NCHIP, D)), ("w_ple", (PLE, AC))]
TRANSPOSED = ("w_ffn_gate", "w_ffn_up", "s5_b_re", "s5_b_im")
CONV_SHARD = (4, LW // NCHIP)


def _mesh_pos():
    return lax.axis_index("x"), lax.axis_index("y"), lax.axis_index("c")


def _other_chips(x, y):
    return [(1 - x, y), (x, 1 - y), (1 - x, 1 - y)]


def _half_rows(c, rows, align):
    return pl.ds(pl.multiple_of(c * (rows // 2), align), rows // 2)


def _run_now(name, carry):
    c_in, c_out = len(carry.operands), len(carry.out_shapes)

    def body(*refs):
        ins, outs, sems = refs[:c_in], refs[c_in:c_in + c_out], refs[c_in + c_out:]
        carry.start(ins, outs, sems)
        carry.finish(ins, outs, sems)

    return pl.pallas_call(body, name=name, in_specs=[ANY] * c_in, out_specs=[ANY] * c_out,
                          out_shape=list(carry.out_shapes), scratch_shapes=list(carry.sems),
                          input_output_aliases=dict(carry.aliases))(*_in_hbm(carry.operands))


def _gather_group(shards, split):
    n = len(shards)

    def copies(srcs, outs, sems):
        send_sems, recv_sems = sems
        x, y, c = _mesh_pos()
        k0 = 2 * x + y
        sib = (x, y, 1 - c)
        chips = _other_chips(x, y)

        def remote(src, dst, j, i, to):
            return pltpu.make_async_remote_copy(src_ref=src, dst_ref=dst, send_sem=send_sems.at[j, i],
                                                recv_sem=recv_sems.at[j, i], device_id=to, device_id_type=MESH)

        def rows(ref, i, core, *lead):
            if not split[i]:
                return ref.at[lead] if lead else ref
            return ref.at[(*lead, _half_rows(core, shards[i].shape[0], 16))]

        own = [remote(s, o.at[k0], 6, i, sib) for i, (s, o) in enumerate(zip(srcs, outs))]
        ici, landed, fwd, fwd_landed = [], [], [], []
        for j, chip in enumerate(chips):
            kj = 2 * chip[0] + chip[1]
            pairs = list(enumerate(zip(srcs, outs)))
            ici.append([remote(rows(s, i, c), rows(o, i, c, k0), j, i, (*chip, c)) for i, (s, o) in pairs])
            landed.append([remote(rows(s, i, c), rows(o, i, c, kj), j, i, (*chip, c)) for i, (s, o) in pairs])
            fwd.append([remote(rows(o, i, c, kj), rows(o, i, c, kj), 3 + j, i, sib) for i, (s, o) in pairs if split[i]])
            fwd_landed.append([remote(rows(o, i, 1 - c, kj), rows(o, i, 1 - c, kj), 3 + j, i, sib)
                               for i, (s, o) in pairs if split[i]])
        return own, ici, landed, fwd, fwd_landed

    def start(srcs, outs, sems):
        own, ici, _, _, _ = copies(srcs, outs, sems)
        for cp in own + [cp for per_chip in ici for cp in per_chip]:
            cp.start()

    def finish(srcs, outs, sems):
        own, ici, landed, fwd, fwd_landed = copies(srcs, outs, sems)
        passed = [i for i in range(n) if split[i]]
        for j in range(3):
            for i, cp in enumerate(landed[j]):
                cp.wait_recv()
                if split[i]:
                    fwd[j][passed.index(i)].start()
        for j in range(3):
            for cp in fwd_landed[j]:
                cp.wait_recv()
        for cp in own:
            cp.wait_recv()
        for cp in own + [cp for per_chip in ici + fwd for cp in per_chip]:
            cp.wait_send()

    return _Carried(shards, [_far((NCHIP,) + s.shape, s.dtype) for s in shards],
                    [pltpu.SemaphoreType.DMA((7, n)), pltpu.SemaphoreType.DMA((7, n))], start, finish)


def _to_bf16_group(name, arrays, carry):
    n = len(arrays)

    def body(*refs):
        for src, dst in zip(refs[:n], refs[n:]):
            dst[...] = src[...].astype(BF)

    specs = [pl.BlockSpec((a.shape[0] // 2, a.shape[1]), lambda i: (i, 0)) for a in arrays]
    return _pallas_call(body, carry, name=name, grid=(2,), in_specs=specs, out_specs=specs,
                        out_shape=[_far(a.shape, BF) for a in arrays], compiler_params=_params(48))(*arrays)


def _each_copy(copies, carried, out_shapes, sems, aliases=None):
    def start(ins, outs, sem_refs):
        for cp in copies(ins, outs, sem_refs):
            cp.start()

    def finish(ins, outs, sem_refs):
        for cp in copies(ins, outs, sem_refs):
            cp.wait()

    return _Carried(carried, out_shapes, sems, start, finish, aliases)


def _swap_group(grads):
    n = len(grads)

    def copies(srcs, outs, sems):
        send_sems, recv_sems = sems
        x, y, c = _mesh_pos()
        return [pltpu.make_async_remote_copy(src_ref=s.at[:, _half_rows(1 - c, s.shape[1], 8)], dst_ref=o,
                                             send_sem=send_sems.at[i], recv_sem=recv_sems.at[i], device_id=(x, y, 1 - c),
                                             device_id_type=MESH) for i, (s, o) in enumerate(zip(srcs, outs))]

    return _each_copy(copies, grads, [pltpu.HBM((NCHIP, g.shape[1] // 2, g.shape[2]), F32) for g in grads],
                      [pltpu.SemaphoreType.DMA((n,)), pltpu.SemaphoreType.DMA((n,))])


def _add_sibling_group(tag, kc_idx, grads, gots):
    n = len(grads)

    def body(kc_ref, *refs):
        for g, rx, p, pb in zip(refs[:n], refs[n:2 * n], refs[2 * n:3 * n], refs[3 * n:]):
            s = g[...] + rx[...]
            pb[...] = s.astype(BF)

            @pl.when(pl.program_id(0) == kc_ref[0])
            def _():
                p[...] = s

    halves = [pl.BlockSpec((None,) + rx.shape[1:], lambda k, kc_ref: (k, 0, 0)) for rx in gots]
    mine = [pl.BlockSpec((None,) + rx.shape[1:], lambda k, kc_ref: (k, kc_ref[1], 0)) for rx in gots]
    own = [pl.BlockSpec(rx.shape[1:], lambda k, kc_ref: (0, 0)) for rx in gots]
    outs = _pallas_call(
        body, name="add_sibling_" + tag,
        grid_spec=pltpu.PrefetchScalarGridSpec(num_scalar_prefetch=1, grid=(NCHIP,), in_specs=mine + halves,
                                               out_specs=own + halves),
        out_shape=[pltpu.HBM(rx.shape[1:], F32) for rx in gots] + [pltpu.HBM(rx.shape, BF) for rx in gots],
        compiler_params=_params(48),
    )(kc_idx, *_in_hbm(list(grads) + list(gots)))
    return outs[:n], outs[n:]


def _exchange_group(parts):
    n = len(parts)

    def copies(srcs, outs, sems):
        send_sems, recv_sems = sems
        x, y, c = _mesh_pos()
        return [pltpu.make_async_remote_copy(
            src_ref=s.at[2 * chip[0] + chip[1]], dst_ref=o.at[j], send_sem=send_sems.at[j, i],
            recv_sem=recv_sems.at[j, i], device_id=(*chip, c), device_id_type=MESH)
            for j, chip in enumerate(_other_chips(x, y)) for i, (s, o) in enumerate(zip(srcs, outs))]

    return _each_copy(copies, parts, [pltpu.HBM((3,) + p.shape[1:], BF) for p in parts],
                      [pltpu.SemaphoreType.DMA((3, n)), pltpu.SemaphoreType.DMA((3, n))])


def _add_chips_group(tag, kc_idx, parts, arrived):
    n = len(parts)

    def body(kc_ref, *refs):
        for p, rx, t in zip(refs[:n], refs[n:2 * n], refs[2 * n:]):
            t[...] = ((p[...] + rx[0].astype(F32)) + rx[1].astype(F32)) + rx[2].astype(F32)

    outs = _pallas_call(
        body, name="add_chips_" + tag,
        grid_spec=pltpu.PrefetchScalarGridSpec(
            num_scalar_prefetch=1, grid=(1,),
            in_specs=([pl.BlockSpec(rx.shape[1:], lambda i, kc_ref: (0, 0)) for rx in arrived]
                      + [pl.BlockSpec(rx.shape, lambda i, kc_ref: (0, 0, 0)) for rx in arrived]),
            out_specs=[pl.BlockSpec((None,) + rx.shape[1:], lambda i, kc_ref: (kc_ref[1], 0, 0)) for rx in arrived]),
        out_shape=[pltpu.HBM((2,) + rx.shape[1:], F32) for rx in arrived],
        compiler_params=_params(48),
    )(kc_idx, *_in_hbm(list(parts) + list(arrived)))
    return list(outs)


def _join_group(halves):
    n = len(halves)

    def copies(bufs, sems):
        send_sems, recv_sems = sems
        x, y, c = _mesh_pos()
        sib = (x, y, 1 - c)
        sends = [pltpu.make_async_remote_copy(src_ref=b.at[c], dst_ref=b.at[c], send_sem=send_sems.at[i],
                                              recv_sem=recv_sems.at[i], device_id=sib, device_id_type=MESH)
                 for i, b in enumerate(bufs)]
        landed = [pltpu.make_async_remote_copy(src_ref=b.at[c], dst_ref=b.at[1 - c], send_sem=send_sems.at[i],
                                               recv_sem=recv_sems.at[i], device_id=sib, device_id_type=MESH)
                  for i, b in enumerate(bufs)]
        return sends, landed

    def start(_, bufs, sems):
        for cp in copies(bufs, sems)[0]:
            cp.start()

    def finish(_, bufs, sems):
        sends, landed = copies(bufs, sems)
        for cp in landed:
            cp.wait_recv()
        for cp in sends:
            cp.wait_send()

    return _Carried(halves, [pltpu.HBM(h.shape, F32) for h in halves],
                    [pltpu.SemaphoreType.DMA((n,)), pltpu.SemaphoreType.DMA((n,))], start, finish,
                    {i: i for i in range(n)})


def _combine(carries):
    operands, out_shapes, sems, aliases, spans = [], [], [], {}, []
    for c in carries:
        aliases.update({len(operands) + i: len(out_shapes) + o for i, o in c.aliases.items()})
        spans.append((len(operands), len(out_shapes), len(sems)))
        operands += list(c.operands)
        out_shapes += list(c.out_shapes)
        sems += list(c.sems)

    def each(phase):
        def run(ins, outs, sem_refs):
            for c, (a, b, s) in zip(carries, spans):
                getattr(c, phase)(ins[a:a + len(c.operands)], outs[b:b + len(c.out_shapes)], sem_refs[s:s + len(c.sems)])
        return run

    return _Carried(operands, out_shapes, sems, each("start"), each("finish"), aliases)


def _allreduce_small(arrays, wire):
    n = len(arrays)
    halves = [(a.shape[0], a.shape[1] // 2) for a in arrays]

    def body(*refs):
        srcs, outs = refs[:n], refs[n:2 * n]
        mine_bufs, sib_bufs, chip_bufs, total_bufs = (refs[k * n:(k + 1) * n] for k in range(2, 6))
        send_sems, recv_sems, local_sems = refs[6 * n:]
        x, y, c = _mesh_pos()
        k0 = 2 * x + y
        sib = (x, y, 1 - c)

        def remote(src, dst, j, i, to):
            return pltpu.make_async_remote_copy(src_ref=src, dst_ref=dst, send_sem=send_sems.at[j, i],
                                                recv_sem=recv_sems.at[j, i], device_id=to, device_id_type=MESH)

        def cols(ref, i, core):
            return ref.at[:, pl.ds(pl.multiple_of(core * halves[i][1], LANE), halves[i][1])]

        swaps = [remote(cols(s, i, 1 - c), b, 0, i, sib) for i, (s, b) in enumerate(zip(srcs, sib_bufs))]
        own = [pltpu.make_async_copy(cols(s, i, c), m, local_sems.at[i]) for i, (s, m) in enumerate(zip(srcs, mine_bufs))]
        for cp in swaps + own:
            cp.start()
        for cp in swaps + own:
            cp.wait()
        for m, b, buf in zip(mine_bufs, sib_bufs, chip_bufs):
            buf[k0] = (m[...] + b[...]).astype(buf.dtype)
        chips = _other_chips(x, y)
        sends = [remote(buf.at[k0], buf.at[k0], 1 + j, i, (*chip, c))
                 for j, chip in enumerate(chips) for i, buf in enumerate(chip_bufs)]
        for cp in sends:
            cp.start()
        for j, chip in enumerate(chips):
            for i, buf in enumerate(chip_bufs):
                remote(buf.at[k0], buf.at[2 * chip[0] + chip[1]], 1 + j, i, (*chip, c)).wait_recv()
        for cp in sends:
            cp.wait_send()
        for t, buf in zip(total_bufs, chip_bufs):
            t[...] = ((buf[0].astype(F32) + buf[1].astype(F32)) + buf[2].astype(F32)) + buf[3].astype(F32)
        joins = [remote(t, cols(o, i, c), 4, i, sib) for i, (t, o) in enumerate(zip(total_bufs, outs))]
        keep = [pltpu.make_async_copy(t, cols(o, i, c), local_sems.at[i]) for i, (t, o) in enumerate(zip(total_bufs, outs))]
        for cp in joins + keep:
            cp.start()
        for i, (t, o) in enumerate(zip(total_bufs, outs)):
            remote(t, cols(o, i, 1 - c), 4, i, sib).wait_recv()
        for cp in joins:
            cp.wait_send()
        for cp in keep:
            cp.wait()

    specs = [_full(a.shape) for a in arrays]
    return _pallas_call(
        body, name="allreduce_small", grid=(1,), in_specs=specs, out_specs=specs,
        out_shape=[_sds(a.shape) for a in arrays],
        scratch_shapes=([pltpu.VMEM(h, F32) for h in halves] + [pltpu.VMEM(h, F32) for h in halves]
                        + [pltpu.VMEM((NCHIP,) + h, dt) for h, dt in zip(halves, wire)] + [pltpu.VMEM(h, F32) for h in halves]
                        + [pltpu.SemaphoreType.DMA((5, n)), pltpu.SemaphoreType.DMA((5, n)), pltpu.SemaphoreType.DMA((n,))]),
        compiler_params=_params(32),
    )(*arrays)


def _adamw_terms(w, g, m, v):
    m = ADAM_B1 * m + (1.0 - ADAM_B1) * g
    v = ADAM_B2 * v + (1.0 - ADAM_B2) * jnp.square(g)
    m_hat = m / (1.0 - ADAM_B1 ** ADAM_STEP)
    v_hat = v / (1.0 - ADAM_B2 ** ADAM_STEP)
    return -ADAM_LR * (m_hat / (jnp.sqrt(v_hat) + ADAM_EPS) + ADAM_WD * w), m, v


ADAM_STEPS = 4


def _adamw_group(tag, ws, gs, ms, vs):
    n = len(ws)

    def body(*refs):
        ins, outs = refs[:4 * n], refs[4 * n:]
        for i in range(n):
            w, g, m, v = (ins[k * n + i][...] for k in range(4))
            outs[i][...] = g
            outs[n + i][...], outs[2 * n + i][...], outs[3 * n + i][...] = _adamw_terms(w, g, m, v)

    specs = [pl.BlockSpec((w.shape[0] // ADAM_STEPS, w.shape[1]), lambda i: (i, 0)) for w in ws]
    outs = _pallas_call(
        body, name="adamw_" + tag, grid=(ADAM_STEPS,), in_specs=specs * 4, out_specs=specs * 4,
        out_shape=[_sds(w.shape) for w in ws] * 4, compiler_params=_params(48),
    )(*_in_hbm(list(ws) + list(gs) + list(ms) + list(vs)))
    return outs[:n], outs[n:2 * n], outs[2 * n:3 * n], outs[3 * n:]


def _adamw_replicated(sums, row_of, direct):
    ns, nr, nd = len(sums), len(row_of), len(direct)

    def body(*refs):
        sum_refs = refs[:ns]
        ins = refs[ns:ns + 3 * nr + 4 * nd]
        outs = refs[ns + 3 * nr + 4 * nd:]
        for i, (_, _, _, si, row) in enumerate(row_of):
            w_ref, m_ref, v_ref = ins[3 * i:3 * i + 3]
            g = sum_refs[si][row:row + 1, :]
            outs[4 * i][...] = g
            outs[4 * i + 1][...], outs[4 * i + 2][...], outs[4 * i + 3][...] = _adamw_terms(w_ref[...], g, m_ref[...], v_ref[...])
        for i in range(nd):
            w_ref, m_ref, v_ref, g_ref = ins[3 * nr + 4 * i:3 * nr + 4 * i + 4]
            o = outs[4 * (nr + i):4 * (nr + i) + 4]
            g = g_ref[...]
            o[0][...] = g
            o[1][...], o[2][...], o[3][...] = _adamw_terms(w_ref[...], g, m_ref[...], v_ref[...])

    operands = list(sums)
    shapes = []
    for w, m, v, _, _ in row_of:
        operands += [w, m, v]
        shapes += [w.shape] * 4
    for w, m, v, g in direct:
        operands += [w, m, v, g]
        shapes += [w.shape] * 4
    flat = _pallas_call(
        body, name="adamw_replicated", grid=(1,), in_specs=[_full(a.shape) for a in operands],
        out_specs=[_full(s) for s in shapes], out_shape=[_sds(s) for s in shapes],
        compiler_params=_params(56),
    )(*operands)
    return [flat[4 * i:4 * i + 4] for i in range(nr + nd)]


class _Exchanges:
    def __init__(self, shards, conv_w, chip, core, apply):
        self.shards, self.conv_w, self.apply = shards, conv_w, apply
        self.active, self.calls = [], 0
        self.chip_core_idx = jnp.stack([chip, core]).astype(jnp.int32)

    def first(self):
        later = [n for n in self.shards if n != "w_in"]
        carry = _gather_group([self.shards["w_in"].astype(BF), self.conv_w], [True, False])
        outs = _to_bf16_group("gather_first", [self.shards[n] for n in later], carry)
        self.shards = dict(zip(later, outs))
        up = self.shards.pop("w_ffn_up")
        half = up.shape[0] // 2
        self.shards.update(w_ffn_up_lo=up[:half], w_ffn_up_hi=up[half:])
        return {"w_in": outs[len(later)], "conv_w": jnp.transpose(outs[len(later) + 1], (1, 0, 2)).reshape(4, LW)}

    def gather(self, names):
        return _gather_group([self.shards[n] for n in names], [True] * len(names))

    def reduce(self, tag, grads):
        self.active.append({"tag": tag, "names": list(grads), "stage": 0, "grads": list(grads.values())})

    def run(self, call, hold=()):
        groups = [g for g in self.active if g["tag"] not in hold]
        carries = [self._exchange_of(g) for g in groups]
        carry = _combine(carries)
        outs = list(call(carry))
        own = len(outs) - len(carry.out_shapes)
        landed = outs[own:]
        for g, c in zip(groups, carries):
            self._sum_after(g, landed[:len(c.out_shapes)])
            landed = landed[len(c.out_shapes):]
        self.active = [g for g in self.active if g["stage"] < 3]
        return outs[:own]

    def _exchange_of(self, g):
        if g["stage"] == 0:
            return _swap_group(g["grads"])
        if g["stage"] == 1:
            return _exchange_group(g["bf16"])
        return _join_group(g["halves"])

    def _sum_after(self, g, landed):
        if g["stage"] == 0:
            g["f32"], g["bf16"] = _add_sibling_group(g["tag"], self.chip_core_idx, g["grads"], landed)
        elif g["stage"] == 1:
            g["halves"] = _add_chips_group(g["tag"], self.chip_core_idx, g["f32"], landed)
        else:
            self.apply(g["tag"], g["names"], [t.reshape(2 * t.shape[1], t.shape[2]) for t in landed])
        g["stage"] += 1

    def drain(self):
        while self.active:
            self.calls += 1
            self.run(lambda carry: _run_now("reduce_%d" % self.calls, carry))


INPUT_NAMES = (["x", "p"] + [n for n in
               ["g_mix", "w_in", "b_in", "lam_re", "lam_im", "log_dt", "s5_b_re", "s5_b_im", "s5_c_re", "s5_c_im", "s5_d",
                "w_glu", "b_glu", "conv_w", "conv_b", "w_r", "b_r", "w_i", "b_i", "lru_lambda", "w_a_out", "w_b_out", "w_o",
                "g_ffn", "w_ffn_gate", "w_ffn_up", "w_ffn_down", "g_ple_gate", "w_ple_gate", "b_ple_gate", "w_ple", "g_ple",
                "g_final"]])
WEIGHT_NAMES = INPUT_NAMES[2:]


def kernel(*args):
    names = INPUT_NAMES + ["loss_target"] + ["m_" + n for n in WEIGHT_NAMES] + ["v_" + n for n in WEIGHT_NAMES]
    assert len(args) == len(names)
    given = dict(zip(names, args))

    def view(name):
        a = given[name]
        return jnp.swapaxes(a, -1, -2) if name.endswith(TRANSPOSED) else a

    def unview(name, a):
        return jnp.swapaxes(a, -1, -2) if name in TRANSPOSED else a

    def local(name):
        return view(name) if name.endswith("g_final") else view(name)[0]

    xi, yi, ci = _mesh_pos()
    k0 = 2 * xi + yi
    x, p, tgt = given["x"][0], given["p"][0, 0], given["loss_target"][0]

    results = {}

    row_halves = {}

    def apply(tag, names, totals):
        totals = dict(zip(names, totals))
        row_halves.update({n: totals.pop(n) for n in names if n in ("w_in_lo", "w_in_hi")})
        if len(row_halves) == 2:
            totals["w_in"] = jnp.concatenate([row_halves.pop("w_in_lo"), row_halves.pop("w_in_hi")])
        names = list(totals)
        if not names:
            return
        new = _adamw_group(tag, [local(n) for n in names], list(totals.values()), [local("m_" + n) for n in names],
                           [local("v_" + n) for n in names])
        for kind, arrays in zip(("grad", "delta", "new_m", "new_v"), new):
            for n, arr in zip(names, arrays):
                results[kind, n] = unview(n, arr[None])

    comm = _Exchanges({n: local(n) for n, _ in SHARDED}, local("conv_w"), k0, ci, apply)
    w = {n: local(n) for n in WEIGHT_NAMES if n != "conv_w" and n not in dict(SHARDED)}
    gx, sums, blocks = _local_step(x, p, tgt, w, comm)

    sum_names, block_names = list(sums), list(blocks)
    red = _allreduce_small([sums[n] for n in sum_names] + [blocks[n] for n in block_names],
                           [F32] * len(sum_names) + [BF] * len(block_names))
    sums = dict(zip(sum_names, red[:len(sum_names)]))
    blocks = dict(zip(block_names, red[len(sum_names):]))
    loss = jnp.sum(sums[LOSS_ROW[0]][LOSS_ROW[1]])
    direct_g = _replicated_grads(w, sums, blocks)
    conv_rows = sums[CONV_W_ROWS[0]][CONV_W_ROWS[1]:CONV_W_ROWS[1] + 4]
    direct_g["conv_w"] = lax.dynamic_slice(conv_rows, (0, k0 * CONV_SHARD[1]), CONV_SHARD)
    as_row = lambda a: a.reshape(1, -1)
    row_names = list(ACC_ROWS)
    row_of = [(as_row(given[n]), as_row(given["m_" + n]), as_row(given["v_" + n]),
               sum_names.index(ACC_ROWS[n][0]), ACC_ROWS[n][1]) for n in row_names]
    direct_names = list(direct_g)
    direct = [(view(n), view("m_" + n), view("v_" + n), direct_g[n].reshape(view(n).shape)) for n in direct_names]
    done = _adamw_replicated([sums[n] for n in sum_names], row_of, direct)
    for n, four in zip(row_names + direct_names, done):
        for kind, arr in zip(("grad", "delta", "new_m", "new_v"), four):
            results[kind, n] = unview(n, arr).reshape(given[n].shape)

    out = [loss, gx[None]]
    for kind in ("grad", "delta", "new_m", "new_v"):
        out += [results[kind, n] for n in WEIGHT_NAMES]
    return tuple(out)
```

```python
import functools
import math

import jax
import jax.numpy as jnp
from jax import lax
from jax.experimental import pallas as pl
from jax.experimental.pallas import tpu as pltpu

F32 = jnp.float32
BF = jnp.bfloat16

D = 1024
S5W = 512
NG, NS, NP = 32, 64, 16
GN = NG * NS
LW = 1024
NH, HD = 16, 64
LRU_C = 8.0
FH = 2816
NCHIP = 4
FC = FH // NCHIP
PLE = 256
INC = S5W + LW + 2 * D
EPS = 1e-6
ADAM_LR, ADAM_B1, ADAM_B2, ADAM_EPS, ADAM_WD, ADAM_STEP = 0.001, 0.9, 0.999, 1e-08, 0.01, 10

TM = 256
TK = 1024
LC = 512
SUB = 8
VMEM_MB = 1024 * 1024
MESH = pl.DeviceIdType.MESH
ANY = pl.BlockSpec(memory_space=pl.ANY)


def _mm(a, b):
    return jnp.dot(a.astype(BF), b.astype(BF), preferred_element_type=F32)


def _mm_nt(a, b):
    return lax.dot_general(a.astype(BF), b.astype(BF), (((1,), (1,)), ((), ())), preferred_element_type=F32)


def _mm_tn(a, b):
    return lax.dot_general(a.astype(BF), b.astype(BF), (((0,), (0,)), ((), ())), preferred_element_type=F32)


def _blockdiag_mm(x, blocks_ref):
    n, rows, _ = blocks_ref.shape
    return jnp.concatenate([jnp.dot(x[:, j * rows:(j + 1) * rows], blocks_ref[j], preferred_element_type=F32)
                            for j in range(n)], axis=1)


def _blockdiag_mm_t(x, blocks_ref):
    n, _, wide = blocks_ref.shape
    return jnp.concatenate([lax.dot_general(x[:, j * wide:(j + 1) * wide], blocks_ref[j], (((1,), (1,)), ((), ())),
                                            preferred_element_type=F32) for j in range(n)], axis=1)


def _rms(x):
    r = lax.rsqrt(jnp.mean(x * x, axis=-1, keepdims=True) + EPS)
    return x * r, r


def _rms_bwd(dy, xh, r, g):
    dxh = dy * g
    return r * (dxh - xh * jnp.mean(dxh * xh, axis=-1, keepdims=True))


def _colsum(x):
    return jnp.sum(x, axis=0, keepdims=True)


def _sig(x):
    return jax.nn.sigmoid(x)


def _gelu_grad(x):
    c = math.sqrt(2.0 / math.pi)
    t = jnp.tanh(c * (x + 0.044715 * x * x * x))
    return 0.5 * (1.0 + t) + 0.5 * x * (1.0 - t * t) * c * (1.0 + 3.0 * 0.044715 * x * x)


def _neg_expm1(x):
    series = -x * (1.0 + x * (0.5 + x * (1.0 / 6.0 + x * (1.0 / 24.0))))
    return jnp.where(x > -0.03, series, 1.0 - jnp.exp(x))


def _tok(width):
    return pl.BlockSpec((TM, width), lambda i: (i, 0))


def _tok_rev(width, nt):
    return pl.BlockSpec((TM, width), lambda i: (nt - 1 - i, 0))


def _full(shape):
    return pl.BlockSpec(shape, lambda i: (0,) * len(shape))


def _params(vmem_mb, **kw):
    return pltpu.CompilerParams(dimension_semantics=("arbitrary",), vmem_limit_bytes=vmem_mb * VMEM_MB, **kw)


def _sds(shape, dtype=F32):
    return jax.ShapeDtypeStruct(shape, dtype)


def _far(shape, dtype=F32):
    return pltpu.HBM(shape, dtype)


class _Carried:
    def __init__(self, operands, out_shapes, sems, start, finish, aliases=None):
        self.operands, self.out_shapes, self.sems = list(operands), list(out_shapes), list(sems)
        self.start, self.finish, self.aliases = start, finish, dict(aliases or {})


def _in_hbm(arrays):
    return [pltpu.with_memory_space_constraint(a, pltpu.HBM) for a in arrays]


def _pallas_call(body, carry=None, **kw):
    if carry is None:
        return pl.pallas_call(body, **kw)

    def at_step(corner):
        hit = [pl.program_id(d) == (size - 1 if corner else 0) for d, size in enumerate(kw["grid"])]
        return functools.reduce(jnp.logical_and, hit)

    name, grid, compiler_params = kw["name"], kw["grid"], kw["compiler_params"]
    in_specs, out_specs, out_shape = list(kw["in_specs"]), list(kw["out_specs"]), list(kw["out_shape"])
    scratch_shapes = list(kw.get("scratch_shapes", ()))
    n_in, n_out, n_scr = len(in_specs), len(out_specs), len(scratch_shapes)
    c_in, c_out = len(carry.operands), len(carry.out_shapes)

    def full_body(*refs):
        ins, refs = refs[:n_in], refs[n_in:]
        c_ins, refs = refs[:c_in], refs[c_in:]
        outs, refs = refs[:n_out], refs[n_out:]
        c_outs, refs = refs[:c_out], refs[c_out:]
        scratch, c_sems = refs[:n_scr], refs[n_scr:]

        @pl.when(at_step(0))
        def _():
            carry.start(c_ins, c_outs, c_sems)

        body(*ins, *outs, *scratch)

        @pl.when(at_step(1))
        def _():
            carry.finish(c_ins, c_outs, c_sems)

    call = pl.pallas_call(
        full_body, name=name, grid=grid, in_specs=in_specs + [ANY] * c_in, out_specs=out_specs + [ANY] * c_out,
        out_shape=out_shape + list(carry.out_shapes), scratch_shapes=scratch_shapes + list(carry.sems),
        input_output_aliases={n_in + i: n_out + o for i, o in carry.aliases.items()},
        compiler_params=compiler_params)
    return lambda *operands: call(*operands, *_in_hbm(carry.operands))


def _resident(pairs, sems):
    first = pl.program_id(0) == 0
    copies = [pltpu.make_async_copy(src, dst, sems.at[j]) for j, (src, dst) in enumerate(pairs)]

    @pl.when(first)
    def _():
        for cp in copies:
            cp.start()

    def wait(j):
        @pl.when(first)
        def _():
            copies[j].wait()

    return wait


def _resident_now(pairs, sems):
    @pl.when(pl.program_id(0) == 0)
    def _():
        copies = [pltpu.make_async_copy(src, dst, sems.at[j]) for j, (src, dst) in enumerate(pairs)]
        for cp in copies:
            cp.start()
        for cp in copies:
            cp.wait()


def _row_iota(width):
    return lax.broadcasted_iota(jnp.int32, (SUB, width), 0)


def _bcast_row(x, row):
    return jnp.broadcast_to(x[row:row + 1, :], x.shape)


def _slab(k):
    return pl.ds(pl.multiple_of(k * SUB, SUB), SUB)


QC = INC // NCHIP
Z_PARTS = ((0, S5W), (S5W, S5W + LW), (S5W + LW, INC))


def _inproj_fwd(x, g_mix, w_in, b_in, carry=None):
    L = x.shape[0]

    def body(x_ref, g_ref, w_hbm, b_ref, h_ref, ua_ref, ub_ref, gp_ref, w_vm, w_sems):
        _resident_now([(w_hbm.at[k], w_vm.at[k]) for k in range(NCHIP)], w_sems)
        xh, _ = _rms(x_ref[...])
        h = (xh * g_ref[...]).astype(BF)
        h_ref[...] = h
        for k in range(NCHIP):
            lo, hi = k * QC, (k + 1) * QC
            z = jnp.dot(h, w_vm[k], preferred_element_type=F32) + b_ref[:, lo:hi]
            for ref, (a, b) in zip((ua_ref, ub_ref, gp_ref), Z_PARTS):
                s, e = max(lo, a), min(hi, b)
                if s < e:
                    ref[:, s - a:e - a] = z[:, s - lo:e - lo]

    return _pallas_call(
        body, carry, name="inproj_fwd", grid=(L // TM,),
        in_specs=[_tok(D), _full((1, D)), ANY, _full((1, INC))],
        out_specs=[_tok(D), _tok(S5W), _tok(LW), _tok(2 * D)],
        out_shape=[_far((L, D), BF), _far((L, S5W)), _far((L, LW)), _sds((L, 2 * D))],
        scratch_shapes=[pltpu.VMEM((NCHIP, D, QC), BF), pltpu.SemaphoreType.DMA((NCHIP,))],
        compiler_params=_params(40),
    )(*_in_hbm([x]), g_mix, *_in_hbm([w_in]), b_in)


def _inproj_bwd(x, dx1, dua, dub, dgp, g_mix, w_in, carry=None):
    L = x.shape[0]

    def body(x_ref, dx1_ref, dua_ref, dub_ref, dgp_ref, g_ref, w_hbm, gx_ref, dz_ref, dg_ref, db_ref, w_vm, w_sems):
        _resident_now([(w_hbm.at[k], w_vm.at[k]) for k in range(NCHIP)], w_sems)

        @pl.when(pl.program_id(0) == 0)
        def _():
            dg_ref[...] = jnp.zeros_like(dg_ref)
            db_ref[...] = jnp.zeros_like(db_ref)

        for src, (a, b) in zip((dua_ref, dub_ref, dgp_ref), Z_PARTS):
            d = src[...]
            dz_ref[:, a:b] = d.astype(BF)
            db_ref[0:1, a:b] += _colsum(d)
        dh = jnp.zeros((TM, D), F32)
        for k in range(NCHIP):
            dh = dh + lax.dot_general(dz_ref[:, k * QC:(k + 1) * QC], w_vm[k], (((1,), (1,)), ((), ())),
                                      preferred_element_type=F32)
        xh, r = _rms(x_ref[...])
        dg_ref[0:1, :] += _colsum(dh * xh)
        gx_ref[...] = dx1_ref[...] + _rms_bwd(dh, xh, r, g_ref[...])

    return _pallas_call(
        body, carry, name="inproj_bwd", grid=(L // TM,),
        in_specs=[_tok(D), _tok(D), _tok(S5W), _tok(LW), _tok(2 * D), _full((1, D)), ANY],
        out_specs=[_tok(D), _tok(INC), _full((SUB, D)), _full((SUB, INC))],
        out_shape=[_sds((L, D)), _sds((L, INC), BF), _sds((SUB, D)), _sds((SUB, INC))],
        scratch_shapes=[pltpu.VMEM((NCHIP, D, QC), BF), pltpu.SemaphoreType.DMA((NCHIP,))],
        compiler_params=_params(40),
    )(x, dx1, *_in_hbm([dua]), dub, dgp, g_mix, *_in_hbm([w_in]))


def _cscan(xr_ref, xi_ref, con_ref, cr_ref, ci_ref, reverse):
    n_slab = xr_ref.shape[0] // SUB
    width = xr_ref.shape[1]
    for lc in range(width // LC):
        cols = slice(lc * LC, (lc + 1) * LC)
        con = [con_ref[SUB * j:SUB * (j + 1), cols] for j in range(8)]

        def step(k, carry, cols=cols, con=con):
            cr, ci = carry
            rows = _slab(n_slab - 1 - k if reverse else k)
            xr, xi = xr_ref[rows, cols], xi_ref[rows, cols]
            for j, sh in enumerate((1, 2, 4)):
                mr, mi = con[2 * j], con[2 * j + 1]
                pr = pltpu.roll(xr, SUB - sh if reverse else sh, 0)
                pi = pltpu.roll(xi, SUB - sh if reverse else sh, 0)
                xr, xi = xr + mr * pr - mi * pi, xi + mr * pi + mi * pr
            xr, xi = xr + con[6] * cr - con[7] * ci, xi + con[6] * ci + con[7] * cr
            xr_ref[rows, cols] = xr
            xi_ref[rows, cols] = xi
            row = 0 if reverse else SUB - 1
            return _bcast_row(xr, row), _bcast_row(xi, row)

        cr, ci = lax.fori_loop(0, n_slab, step, (cr_ref[:, cols], ci_ref[:, cols]))
        cr_ref[:, cols] = cr
        ci_ref[:, cols] = ci


def _s5_fwd(ua, bbr, bbi, ccr, cci, dsk, con, w_glu, b_glu, carry=None):
    L = ua.shape[0]

    def body(ua_ref, bbr_hbm, bbi_hbm, ccr_hbm, cci_hbm, dsk_ref, con_ref, wg_ref, bg_ref,
             sr_ref, si_ref, y_ref, zg_ref, ya_ref, bbr_vm, bbi_vm, ccr_vm, cci_vm, cr_ref, ci_ref, w_sems):
        landed = _resident([(bbr_hbm, bbr_vm), (bbi_hbm, bbi_vm), (ccr_hbm, ccr_vm), (cci_hbm, cci_vm)], w_sems)

        @pl.when(pl.program_id(0) == 0)
        def _():
            cr_ref[...] = jnp.zeros_like(cr_ref)
            ci_ref[...] = jnp.zeros_like(ci_ref)

        u = ua_ref[...]
        ub = u.astype(BF)
        landed(0)
        sr_ref[...] = _blockdiag_mm(ub, bbr_vm)
        landed(1)
        si_ref[...] = _blockdiag_mm(ub, bbi_vm)
        _cscan(sr_ref, si_ref, con_ref, cr_ref, ci_ref, reverse=False)
        landed(2)
        landed(3)
        y = (_blockdiag_mm_t(sr_ref[...].astype(BF), ccr_vm) - _blockdiag_mm_t(si_ref[...].astype(BF), cci_vm)
             + dsk_ref[...] * u)
        y_ref[...] = y
        zg = jax.nn.gelu(y)
        zg_ref[...] = zg.astype(BF)
        q = _mm(zg, wg_ref[...]) + bg_ref[...]
        ya_ref[...] = (zg * _sig(q)).astype(BF)

    return _pallas_call(
        body, carry, name="s5_fwd", grid=(L // TM,),
        in_specs=[_tok(S5W), ANY, ANY, ANY, ANY, _full((1, S5W)), _full((8 * SUB, GN)),
                  _full((S5W, S5W)), _full((1, S5W))],
        out_specs=[_tok(GN), _tok(GN), _tok(S5W), _tok(S5W), _tok(S5W)],
        out_shape=[_sds((L, GN)), _sds((L, GN)), _far((L, S5W)), _far((L, S5W), BF), _far((L, S5W), BF)],
        scratch_shapes=[pltpu.VMEM((S5W // 128, 128, GN // (S5W // 128)), BF)] * 4 + [
                        pltpu.VMEM((SUB, GN), F32), pltpu.VMEM((SUB, GN), F32),
                        pltpu.SemaphoreType.DMA((4,))],
        compiler_params=_params(44),
    )(*_in_hbm([ua, bbr, bbi, ccr, cci]), dsk, con, w_glu, b_glu)


def _s5_bwd(dya, y, ua, sr, si, bbr, bbi, ccr, cci, dsk, con_rev, w_glu, b_glu, carry=None):
    L = ua.shape[0]
    nt = L // TM
    spt = TM // SUB
    n_slab = spt

    def halo_map(i):
        return (jnp.maximum((nt - 1 - i) * spt - 1, 0), 0)

    def body(dya_ref, y_ref, ua_ref, sr_ref, si_ref, hr_ref, hi_ref, bbr_hbm, bbi_hbm, ccr_hbm, cci_hbm,
             dsk_ref, con_ref, wg_ref, bg_ref,
             dua_ref, dq_ref, dy_ref, lr_ref, li_ref, da_ref, dsm_ref,
             bbr_vm, bbi_vm, ccr_vm, cci_vm, cr_ref, ci_ref, w_sems):
        i = pl.program_id(0)
        landed = _resident([(ccr_hbm, ccr_vm), (cci_hbm, cci_vm), (bbr_hbm, bbr_vm), (bbi_hbm, bbi_vm)], w_sems)

        @pl.when(i == 0)
        def _():
            cr_ref[...] = jnp.zeros_like(cr_ref)
            ci_ref[...] = jnp.zeros_like(ci_ref)
            da_ref[...] = jnp.zeros_like(da_ref)
            dsm_ref[...] = jnp.zeros_like(dsm_ref)

        u = ua_ref[...]
        yv = y_ref[...]
        dya = dya_ref[...]
        zg = jax.nn.gelu(yv)
        sg = _sig(_mm(zg, wg_ref[...]) + bg_ref[...])
        dq = dya * zg * sg * (1.0 - sg)
        dq_ref[...] = dq.astype(BF)
        dzg = dya * sg + _mm_nt(dq, wg_ref[...])
        dy = dzg * _gelu_grad(yv)
        dyb = dy.astype(BF)
        dy_ref[...] = dyb
        dsm_ref[0:1, :] += _colsum(dy * u)
        dsm_ref[1:2, :] += _colsum(dq)
        landed(0)
        lr_ref[...] = _blockdiag_mm(dyb, ccr_vm)
        landed(1)
        li_ref[...] = -_blockdiag_mm(dyb, cci_vm)
        _cscan(lr_ref, li_ref, con_ref, cr_ref, ci_ref, reverse=True)

        first_tile = (i == nt - 1)
        row = _row_iota(LC)
        for lc in range(GN // LC):
            cols = slice(lc * LC, (lc + 1) * LC)
            h_r = jnp.where(first_tile, 0.0, hr_ref[:, cols])
            h_i = jnp.where(first_tile, 0.0, hi_ref[:, cols])

            def step(k, acc, cols=cols, h_r=h_r, h_i=h_i):
                ar, ai = acc
                rows = _slab(k)
                prev = _slab(jnp.maximum(k - 1, 0))
                pr = jnp.where(k == 0, h_r, sr_ref[prev, cols])
                pi = jnp.where(k == 0, h_i, si_ref[prev, cols])
                spr = pltpu.roll(jnp.where(row == SUB - 1, pr, sr_ref[rows, cols]), 1, 0)
                spi = pltpu.roll(jnp.where(row == SUB - 1, pi, si_ref[rows, cols]), 1, 0)
                lr, li = lr_ref[rows, cols], li_ref[rows, cols]
                return ar + lr * spr + li * spi, ai + li * spr - lr * spi

            zero = jnp.zeros((SUB, LC), F32)
            ar, ai = lax.fori_loop(0, n_slab, step, (zero, zero))
            da_ref[0:1, cols] += _colsum(ar)
            da_ref[1:2, cols] += _colsum(ai)

        landed(2)
        landed(3)
        dua_ref[...] = (dy * dsk_ref[...] + _blockdiag_mm_t(lr_ref[...].astype(BF), bbr_vm)
                        + _blockdiag_mm_t(li_ref[...].astype(BF), bbi_vm))

    return _pallas_call(
        body, carry, name="s5_bwd", grid=(nt,),
        in_specs=[_tok_rev(S5W, nt), _tok_rev(S5W, nt), _tok_rev(S5W, nt), _tok_rev(GN, nt), _tok_rev(GN, nt),
                  pl.BlockSpec((SUB, GN), halo_map), pl.BlockSpec((SUB, GN), halo_map),
                  ANY, ANY, ANY, ANY, _full((1, S5W)), _full((8 * SUB, GN)), _full((S5W, S5W)), _full((1, S5W))],
        out_specs=[_tok_rev(S5W, nt), _tok_rev(S5W, nt), _tok_rev(S5W, nt), _tok_rev(GN, nt), _tok_rev(GN, nt),
                   _full((SUB, GN)), _full((SUB, S5W))],
        out_shape=[_sds((L, S5W)), _sds((L, S5W), BF), _sds((L, S5W), BF), _sds((L, GN)), _sds((L, GN)),
                   _sds((SUB, GN)), _sds((SUB, S5W))],
        scratch_shapes=[pltpu.VMEM((S5W // 128, 128, GN // (S5W // 128)), BF)] * 4 + [
                        pltpu.VMEM((SUB, GN), F32), pltpu.VMEM((SUB, GN), F32),
                        pltpu.SemaphoreType.DMA((4,))],
        compiler_params=_params(52),
    )(dya, y, ua, sr, si, sr, si, *_in_hbm([bbr, bbi, ccr, cci]), dsk, con_rev, w_glu, b_glu)


def _lru_gate_terms(rg, sp):
    log_a = -LRU_C * rg * sp
    a = jnp.exp(log_a)
    mult = jnp.sqrt(_neg_expm1(2.0 * log_a))
    return a, mult


def _lru_fwd(ub, conv_w, conv_b, wr, wi, b_r, b_i, sp, carry=None):
    L = ub.shape[0]
    n_slab = TM // SUB

    def body(ub_ref, cw_ref, cb_ref, wr_ref, wi_ref, br_ref, bi_ref, sp_ref,
             xc_ref, rg_ref, ig_ref, h_ref, hp_ref, a_ref, halo_ref, carry_ref):
        @pl.when(pl.program_id(0) == 0)
        def _():
            halo_ref[...] = jnp.zeros_like(halo_ref)
            carry_ref[...] = jnp.zeros_like(carry_ref)

        row = _row_iota(LW)
        taps = [cw_ref[k:k + 1, :] for k in range(4)]
        cb = cb_ref[...]

        def conv_step(k, prev):
            rows = _slab(k)
            cur = ub_ref[rows, :]
            acc = taps[3] * cur + cb
            for j in (1, 2, 3):
                acc = acc + taps[3 - j] * pltpu.roll(jnp.where(row >= SUB - j, prev, cur), j, 0)
            xc_ref[rows, :] = acc
            return cur

        halo_ref[...] = lax.fori_loop(0, n_slab, conv_step, halo_ref[...])

        xc = xc_ref[...]
        xcb = xc.astype(BF)
        rg = _sig(_blockdiag_mm(xcb, wr_ref) + br_ref[...])
        ig = _sig(_blockdiag_mm(xcb, wi_ref) + bi_ref[...])
        rg_ref[...] = rg
        ig_ref[...] = ig
        a, mult = _lru_gate_terms(rg, sp_ref[...])
        a_ref[...] = a
        h_ref[...] = mult * ig * xc

        rowc = _row_iota(LC)
        for lc in range(LW // LC):
            cols = slice(lc * LC, (lc + 1) * LC)

            def step(k, c, cols=cols):
                rows = _slab(k)
                av, b = a_ref[rows, cols], h_ref[rows, cols]
                for sh in (1, 2, 4):
                    keep = rowc >= sh
                    b = b + av * jnp.where(keep, pltpu.roll(b, sh, 0), 0.0)
                    av = av * jnp.where(keep, pltpu.roll(av, sh, 0), 1.0)
                h = b + av * c
                h_ref[rows, cols] = h
                hp_ref[rows, cols] = jnp.where(rowc == 0, c, pltpu.roll(h, 1, 0))
                return _bcast_row(h, SUB - 1)

            carry_ref[:, cols] = lax.fori_loop(0, n_slab, step, carry_ref[:, cols])

    return _pallas_call(
        body, carry, name="lru_fwd", grid=(L // TM,),
        in_specs=[_tok(LW), _full((4, LW)), _full((1, LW)), _full((LW // 128, 128, 128)), _full((LW // 128, 128, 128)),
                  _full((1, LW)), _full((1, LW)), _full((1, LW))],
        out_specs=[_tok(LW)] * 5,
        out_shape=[_far((L, LW))] * 5,
        scratch_shapes=[pltpu.VMEM((TM, LW), F32), pltpu.VMEM((SUB, LW), F32), pltpu.VMEM((SUB, LW), F32)],
        compiler_params=_params(40),
    )(*_in_hbm([ub]), conv_w, conv_b, wr, wi, b_r, b_i, sp)


def _lru_bwd(dyb, xc, rg, ig, hp, ub, conv_w, wr, wi, sp, dsp, carry=None):
    L = ub.shape[0]
    nt = L // TM
    spt = TM // SUB
    n_slab = spt

    def halo_map(i):
        return (jnp.maximum((nt - 1 - i) * spt - 1, 0), 0)

    def body(dh_ref, xc_ref, rg_ref, ig_ref, hp_ref, ub_ref, uh_ref, cw_ref, wr_ref, wi_ref, sp_ref, dsp_ref,
             dub_ref, dpr_ref, dpi_ref, acc_ref, a_ref, lam_ref, dxc_ref, carry_ref, next_ref):
        i = pl.program_id(0)

        @pl.when(i == 0)
        def _():
            carry_ref[...] = jnp.zeros_like(carry_ref)
            next_ref[...] = jnp.zeros_like(next_ref)
            acc_ref[...] = jnp.zeros_like(acc_ref)

        sp = sp_ref[...]
        rg, ig, xc = rg_ref[...], ig_ref[...], xc_ref[...]
        a, mult = _lru_gate_terms(rg, sp)
        a_ref[...] = a

        rowc = _row_iota(LC)
        for lc in range(LW // LC):
            cols = slice(lc * LC, (lc + 1) * LC)

            def step(k, c, cols=cols):
                rows = _slab(n_slab - 1 - k)
                av, dh = a_ref[rows, cols], dh_ref[rows, cols]
                b = av * dh
                for sh in (1, 2, 4):
                    keep = rowc < SUB - sh
                    b = b + av * jnp.where(keep, pltpu.roll(b, SUB - sh, 0), 0.0)
                    av = av * jnp.where(keep, pltpu.roll(av, SUB - sh, 0), 1.0)
                mu = b + av * c
                lam_ref[rows, cols] = dh + jnp.where(rowc == SUB - 1, c, pltpu.roll(mu, SUB - 1, 0))
                return _bcast_row(mu, 0)

            carry_ref[:, cols] = lax.fori_loop(0, n_slab, step, carry_ref[:, cols])

        lam = lam_ref[...]
        d_a = lam * hp_ref[...]
        d_mult = lam * ig * xc
        d_ig = lam * mult * xc
        dxc = lam * mult * ig
        d_log_a = d_a * a - d_mult * a * a / mult
        d_rg = (-LRU_C) * sp * d_log_a
        acc_ref[0:1, :] += _colsum((-LRU_C) * rg * d_log_a) * dsp_ref[...]
        dpr = d_rg * rg * (1.0 - rg)
        dpi = d_ig * ig * (1.0 - ig)
        acc_ref[1:2, :] += _colsum(dpr)
        acc_ref[2:3, :] += _colsum(dpi)
        dprb, dpib = dpr.astype(BF), dpi.astype(BF)
        dpr_ref[...] = dprb
        dpi_ref[...] = dpib
        dxc = dxc + _blockdiag_mm_t(dprb, wr_ref) + _blockdiag_mm_t(dpib, wi_ref)
        dxc_ref[...] = dxc
        acc_ref[3:4, :] += _colsum(dxc)

        row = _row_iota(LW)
        taps = [cw_ref[k:k + 1, :] for k in range(4)]
        u_halo = jnp.where(i == nt - 1, 0.0, uh_ref[...])
        nxt_tile = next_ref[...]

        def conv_step(k, accs):
            rows = _slab(k)
            cur = dxc_ref[rows, :]
            nxt = jnp.where(k == n_slab - 1, nxt_tile, dxc_ref[_slab(jnp.minimum(k + 1, n_slab - 1)), :])
            ucur = ub_ref[rows, :]
            uprev = jnp.where(k == 0, u_halo, ub_ref[_slab(jnp.maximum(k - 1, 0)), :])
            du = taps[3] * cur
            new = [accs[3] + cur * ucur]
            for j in (1, 2, 3):
                du = du + taps[3 - j] * pltpu.roll(jnp.where(row < j, nxt, cur), SUB - j, 0)
                new.append(accs[3 - j] + cur * pltpu.roll(jnp.where(row >= SUB - j, uprev, ucur), j, 0))
            dub_ref[rows, :] = du
            return tuple(new[::-1])

        zero = jnp.zeros((SUB, LW), F32)
        accs = lax.fori_loop(0, n_slab, conv_step, (zero, zero, zero, zero))
        for k in range(4):
            acc_ref[4 + k:5 + k, :] += _colsum(accs[k])
        next_ref[...] = dxc_ref[0:SUB, :]

    return _pallas_call(
        body, carry, name="lru_bwd", grid=(nt,),
        in_specs=[_tok_rev(LW, nt)] * 6 + [pl.BlockSpec((SUB, LW), halo_map), _full((4, LW)),
                                           _full((LW // 128, 128, 128)), _full((LW // 128, 128, 128)), _full((1, LW)), _full((1, LW))],
        out_specs=[_tok_rev(LW, nt), _tok_rev(LW, nt), _tok_rev(LW, nt), _full((SUB, LW))],
        out_shape=[_sds((L, LW)), _far((L, LW), BF), _far((L, LW), BF), _sds((SUB, LW))],
        scratch_shapes=[pltpu.VMEM((TM, LW), F32), pltpu.VMEM((TM, LW), F32), pltpu.VMEM((TM, LW), F32),
                        pltpu.VMEM((SUB, LW), F32), pltpu.VMEM((SUB, LW), F32)],
        compiler_params=_params(48),
    )(dyb, xc, rg, ig, hp, ub, ub, conv_w, wr, wi, sp, dsp)


AC = D // NCHIP


def _merge_fwd(x, ya, yb, gp, w_a, w_b, w_o, carry=None):
    L = x.shape[0]

    def body(x_ref, ya_ref, yb_ref, gp_ref, wa_ref, wb_ref, wo_ref, x1_ref, pa_ref, pb_ref, mg_ref):
        ya = ya_ref[...]
        for k in range(NCHIP):
            pa_ref[:, k * AC:(k + 1) * AC] = jnp.dot(ya, wa_ref[k], preferred_element_type=F32)
        pb = _mm(yb_ref[...], wb_ref[...])
        pb_ref[...] = pb
        gp = gp_ref[...]
        merged = (_sig(gp[:, :D]) * pa_ref[...] + _sig(gp[:, D:]) * pb).astype(BF)
        mg_ref[...] = merged
        x1_ref[...] = x_ref[...] + jnp.dot(merged, wo_ref[...], preferred_element_type=F32)

    return _pallas_call(
        body, carry, name="merge_fwd", grid=(L // TM,),
        in_specs=[_tok(D), _tok(S5W), _tok(LW), _tok(2 * D), _full((NCHIP, S5W, AC)), _full((LW, D)), _full((D, D))],
        out_specs=[_tok(D), _tok(D), _tok(D), _tok(D)],
        out_shape=[_sds((L, D)), _sds((L, D)), _sds((L, D)), _far((L, D), BF)],
        compiler_params=_params(40),
    )(x, ya, yb, gp, w_a, w_b, w_o)


def _merge_bwd(dx1, gp, pa, pb, w_a, w_b, w_o, carry=None):
    L = dx1.shape[0]

    def body(dx1_ref, gp_ref, pa_ref, pb_ref, wa_ref, wb_ref, wo_ref, dya_ref, dyb_ref, dgp_ref, dpa_ref, dpb_ref):
        dm = _mm_nt(dx1_ref[...], wo_ref[...])
        gp = gp_ref[...]
        sa, sb = _sig(gp[:, :D]), _sig(gp[:, D:])
        dpa = (dm * sa).astype(BF)
        dpb = (dm * sb).astype(BF)
        dpa_ref[...] = dpa
        dpb_ref[...] = dpb
        dgp_ref[:, :D] = dm * pa_ref[...] * sa * (1.0 - sa)
        dgp_ref[:, D:] = dm * pb_ref[...] * sb * (1.0 - sb)
        dya = jnp.zeros((TM, S5W), F32)
        for k in range(NCHIP):
            dya = dya + _mm_nt(dpa[:, k * AC:(k + 1) * AC], wa_ref[k])
        dya_ref[...] = dya
        dyb_ref[...] = _mm_nt(dpb, wb_ref[...])

    return _pallas_call(
        body, carry, name="merge_bwd", grid=(L // TM,),
        in_specs=[_tok(D), _tok(2 * D), _tok(D), _tok(D), _full((NCHIP, S5W, AC)), _full((LW, D)), _full((D, D))],
        out_specs=[_tok(S5W), _tok(LW), _tok(2 * D), _tok(D), _tok(D)],
        out_shape=[_far((L, S5W)), _far((L, LW)), _sds((L, 2 * D)), _far((L, D), BF), _far((L, D), BF)],
        compiler_params=_params(40),
    )(dx1, gp, pa, pb, w_a, w_b, w_o)


def _chunk_tok(width):
    return pl.BlockSpec((NCHIP, TM, width), lambda i: (0, i, 0))


def _ffn_up_fwd(x1, g_ffn, wg, wu, carry=None):
    L = x1.shape[0]

    def body(x_ref, g_ref, wg_hbm, wu_hbm, h2_ref, gg_ref, uu_ref, wg_vm, wu_vm, w_sems):
        _resident_now([(src.at[c], dst.at[c]) for c in range(NCHIP) for src, dst in ((wg_hbm, wg_vm), (wu_hbm, wu_vm))],
                      w_sems)
        xh, _ = _rms(x_ref[...])
        h2 = (xh * g_ref[...]).astype(BF)
        h2_ref[...] = h2
        for c in range(NCHIP):
            gg_ref[c] = lax.dot_general(h2, wg_vm[c], (((1,), (1,)), ((), ())), preferred_element_type=F32).astype(BF)
            uu_ref[c] = lax.dot_general(h2, wu_vm[c], (((1,), (1,)), ((), ())), preferred_element_type=F32).astype(BF)

    return _pallas_call(
        body, carry, name="ffn_up_fwd", grid=(L // TM,),
        in_specs=[_tok(D), _full((1, D)), ANY, ANY],
        out_specs=[_tok(D), _chunk_tok(FC), _chunk_tok(FC)],
        out_shape=[_far((L, D), BF), _sds((NCHIP, L, FC), BF), _sds((NCHIP, L, FC), BF)],
        scratch_shapes=[pltpu.VMEM((NCHIP, FC, D), BF)] * 2 + [pltpu.SemaphoreType.DMA((2 * NCHIP,))],
        compiler_params=_params(44),
    )(x1, g_ffn, wg, wu)


def _ffn_down_fwd(x1, gg, uu, wd):
    L = x1.shape[0]

    def body(x_ref, gg_ref, uu_ref, wd_hbm, x2_ref, wd_vm, w_sems):
        _resident_now([(wd_hbm.at[c], wd_vm.at[c]) for c in range(NCHIP)], w_sems)
        out = x_ref[...]
        for c in range(NCHIP):
            g = gg_ref[c].astype(F32)
            act = (g * _sig(g) * uu_ref[c].astype(F32)).astype(BF)
            out = out + jnp.dot(act, wd_vm[c], preferred_element_type=F32)
        x2_ref[...] = out

    rows = 2 * TM
    chunks = pl.BlockSpec((NCHIP, rows, FC), lambda i: (0, i, 0))
    return _pallas_call(
        body, name="ffn_down_fwd", grid=(L // rows,),
        in_specs=[pl.BlockSpec((rows, D), lambda i: (i, 0)), chunks, chunks, ANY],
        out_specs=[pl.BlockSpec((rows, D), lambda i: (i, 0))],
        out_shape=[_far((L, D))],
        scratch_shapes=[pltpu.VMEM((NCHIP, FC, D), BF), pltpu.SemaphoreType.DMA((NCHIP,))],
        compiler_params=_params(40),
    )(x1, gg, uu, *_in_hbm([wd]))[0]


def _ffn_bwd(x1, dx2, gg, uu, g_ffn, wg, wu, wd, carry=None):
    L = x1.shape[0]

    def body(x_ref, dx2_ref, gg_ref, uu_ref, g_ref, wg_hbm, wu_hbm, wd_hbm,
             dx1_ref, act_ref, dgg_ref, duu_ref, dg_ref, wg_vm, wu_vm, wd_vm, w_sems):
        _resident_now([(src.at[c], dst.at[c]) for c in range(NCHIP)
                       for src, dst in ((wg_hbm, wg_vm), (wu_hbm, wu_vm), (wd_hbm, wd_vm))], w_sems)

        @pl.when(pl.program_id(0) == 0)
        def _():
            dg_ref[...] = jnp.zeros_like(dg_ref)

        dx2 = dx2_ref[...]
        dx2b = dx2.astype(BF)
        dh2 = jnp.zeros((TM, D), F32)
        for c in range(NCHIP):
            g = gg_ref[c].astype(F32)
            u = uu_ref[c].astype(F32)
            s = _sig(g)
            silu = g * s
            act_ref[c] = (silu * u).astype(BF)
            dact = lax.dot_general(dx2b, wd_vm[c], (((1,), (1,)), ((), ())), preferred_element_type=F32)
            dg = (dact * u * s * (1.0 + g * (1.0 - s))).astype(BF)
            du = (dact * silu).astype(BF)
            dgg_ref[c] = dg
            duu_ref[c] = du
            dh2 = dh2 + jnp.dot(dg, wg_vm[c], preferred_element_type=F32)
            dh2 = dh2 + jnp.dot(du, wu_vm[c], preferred_element_type=F32)
        xh, r = _rms(x_ref[...])
        dg_ref[0:1, :] += _colsum(dh2 * xh)
        dx1_ref[...] = dx2 + _rms_bwd(dh2, xh, r, g_ref[...])

    return _pallas_call(
        body, carry, name="ffn_bwd", grid=(L // TM,),
        in_specs=[_tok(D), _tok(D), _chunk_tok(FC), _chunk_tok(FC), _full((1, D)), ANY, ANY, ANY],
        out_specs=[_tok(D), _chunk_tok(FC), _chunk_tok(FC), _chunk_tok(FC), _full((SUB, D))],
        out_shape=[_sds((L, D)), _sds((NCHIP, L, FC), BF), _sds((NCHIP, L, FC), BF), _sds((NCHIP, L, FC), BF),
                   _sds((SUB, D))],
        scratch_shapes=[pltpu.VMEM((NCHIP, FC, D), BF)] * 3 + [pltpu.SemaphoreType.DMA((3 * NCHIP,))],
        compiler_params=_params(56),
    )(x1, dx2, gg, uu, g_ffn, wg, wu, wd)


def _ple_loss(x2, p, tgt, g_pg, w_pg, b_pg, w_ple, g_ple, g_final):
    L = x2.shape[0]

    def body(x2_ref, p_ref, t_ref, gpg_ref, wpg_ref, bpg_ref, wple_ref, gple_ref, gf_ref,
             dx2_ref, n2_ref, dpre_ref, de0_ref, acc_ref):
        @pl.when(pl.program_id(0) == 0)
        def _():
            acc_ref[...] = jnp.zeros_like(acc_ref)

        x2 = x2_ref[...]
        x2h, r2 = _rms(x2)
        n2 = (x2h * gpg_ref[...]).astype(BF)
        n2_ref[...] = n2
        gate = _sig(jnp.dot(n2, wpg_ref[...], preferred_element_type=F32) + bpg_ref[...])
        pb = p_ref[...].astype(BF)
        e0 = jnp.concatenate([jnp.dot(pb, wple_ref[k], preferred_element_type=F32) for k in range(NCHIP)], axis=1)
        e0h, re = _rms(e0)
        e = e0h * gple_ref[...]
        x3 = x2 + gate * e
        x3h, r3 = _rms(x3)
        diff = x3h * gf_ref[...] - t_ref[...]
        acc_ref[4:5, :] += _colsum(diff * diff) * (0.5 / D)
        dy = diff * (1.0 / D)
        acc_ref[3:4, :] += _colsum(dy * x3h)
        dx3 = _rms_bwd(dy, x3h, r3, gf_ref[...])
        de = dx3 * gate
        acc_ref[2:3, :] += _colsum(de * e0h)
        de0_ref[...] = _rms_bwd(de, e0h, re, gple_ref[...]).astype(BF)
        dpre = dx3 * e * gate * (1.0 - gate)
        acc_ref[1:2, :] += _colsum(dpre)
        dpreb = dpre.astype(BF)
        dpre_ref[...] = dpreb
        dn2 = lax.dot_general(dpreb, wpg_ref[...], (((1,), (1,)), ((), ())), preferred_element_type=F32)
        acc_ref[0:1, :] += _colsum(dn2 * x2h)
        dx2_ref[...] = dx3 + _rms_bwd(dn2, x2h, r2, gpg_ref[...])

    return _pallas_call(
        body, name="ple_loss", grid=(L // TM,),
        in_specs=[_tok(D), _tok(PLE), _tok(D), _full((1, D)), _full((D, D)), _full((1, D)), _full((NCHIP, PLE, AC)),
                  _full((1, D)), _full((1, D))],
        out_specs=[_tok(D), _tok(D), _tok(D), _tok(D), _full((SUB, D))],
        out_shape=[_sds((L, D)), _sds((L, D), BF), _sds((L, D), BF), _sds((L, D), BF), _sds((SUB, D))],
        compiler_params=_params(40),
    )(x2, p, tgt, g_pg, *_in_hbm([w_pg]), b_pg, *_in_hbm([w_ple]), g_ple, g_final)


def _tn(name, a, b, col_chunk=None, a_block=None, carry=None):
    L = a.shape[-2]
    m, n = a.shape[-1], b.shape[-1]
    a_col = 0
    if a_block is not None:
        a_col, m = a_block
    tk = L if (a.ndim == 3 or b.ndim == 3 or a_block is not None) else TK
    if a.ndim == 3 or b.ndim == 3:
        nj, bn = (a if a.ndim == 3 else b).shape[0], n
        a_spec = (pl.BlockSpec((None, tk, m), lambda j, t: (j, t, 0)) if a.ndim == 3
                  else pl.BlockSpec((tk, m), lambda j, t: (t, 0)))
        b_spec = (pl.BlockSpec((None, tk, n), lambda j, t: (j, t, 0)) if b.ndim == 3
                  else pl.BlockSpec((tk, n), lambda j, t: (t, 0)))
        out_spec, out_shape = pl.BlockSpec((None, m, n), lambda j, t: (j, 0, 0)), _sds((nj, m, n))
    else:
        bn = col_chunk
        if bn is None:
            bn = next((cand for cand in (1024, 512) if n > cand and n % cand == 0), n)
        nj = n // bn
        a_spec = pl.BlockSpec((tk, m), lambda j, t: (t, a_col))
        b_spec = pl.BlockSpec((tk, bn), lambda j, t: (t, j))
        if col_chunk is None:
            out_spec, out_shape = pl.BlockSpec((m, bn), lambda j, t: (0, j)), _sds((m, n))
        else:
            out_spec, out_shape = pl.BlockSpec((None, m, bn), lambda j, t: (j, 0, 0)), _sds((nj, m, bn))

    def body(a_ref, b_ref, o_ref):
        if tk == L:
            o_ref[...] = _mm_tn(a_ref[...], b_ref[...])
        else:
            @pl.when(pl.program_id(1) == 0)
            def _():
                o_ref[...] = jnp.zeros_like(o_ref)

            o_ref[...] += _mm_tn(a_ref[...], b_ref[...])

    outs = _pallas_call(
        body, carry, name=name, grid=(nj, L // tk), in_specs=[a_spec, b_spec], out_specs=[out_spec],
        out_shape=[pltpu.HBM(out_shape.shape, out_shape.dtype)],
        compiler_params=pltpu.CompilerParams(dimension_semantics=("arbitrary", "arbitrary"),
                                             vmem_limit_bytes=(30 if tk == L else 28) * VMEM_MB),
    )(*(_in_hbm([a, b]) if tk == L else (a, b)))
    return outs[0] if carry is None else outs


LANE = 128


def _tn_blocks(name, a, bs, ga, gb, carry=None):
    L, m, n, nb = a.shape[0], a.shape[1], bs[0].shape[1], len(bs)
    per = LANE // ga
    wb = per * gb
    n_super = m // LANE

    def body(a_ref, *refs):
        b_refs, o_refs, acc_refs = refs[:nb], refs[nb:2 * nb], refs[2 * nb:]
        t = pl.program_id(0)

        @pl.when(t == 0)
        def _():
            for acc in acc_refs:
                acc[...] = jnp.zeros_like(acc)

        lhs = a_ref[...].astype(BF)
        for b_ref, acc in zip(b_refs, acc_refs):
            rhs = b_ref[...].astype(BF)
            for j in range(n_super):
                acc[j] += _mm_tn(lhs[:, j * LANE:(j + 1) * LANE], rhs[:, j * wb:(j + 1) * wb])

        @pl.when(t == L // TK - 1)
        def _():
            own = (lax.broadcasted_iota(jnp.int32, (LANE, wb), 0) // ga) == (lax.broadcasted_iota(jnp.int32, (LANE, wb), 1) // gb)
            for o_ref, acc in zip(o_refs, acc_refs):
                for j in range(n_super):
                    kept = jnp.where(own, acc[j], 0.0)
                    o_ref[:, j * wb:(j + 1) * wb] = jnp.sum(kept.reshape(per, ga, wb), axis=0)

    outs = _pallas_call(
        body, carry, name=name, grid=(L // TK,),
        in_specs=[pl.BlockSpec((TK, m), lambda t: (t, 0))] + [pl.BlockSpec((TK, n), lambda t: (t, 0))] * nb,
        out_specs=[_full((ga, n))] * nb, out_shape=[_sds((ga, n))] * nb,
        scratch_shapes=[pltpu.VMEM((n_super, LANE, wb), F32)] * nb,
        compiler_params=_params(48),
    )(*_in_hbm([a] + list(bs)))
    return list(outs)


def _s5_discretize(lam_re, lam_im, log_dt, b_re, b_im):
    dt = jnp.exp(log_dt)[:, None]
    mag = jnp.exp(lam_re * dt)
    ar = mag * jnp.cos(lam_im * dt)
    ai = mag * jnp.sin(lam_im * dt)
    den = lam_re * lam_re + lam_im * lam_im
    nr = ar - 1.0
    fr = (nr * lam_re + ai * lam_im) / den
    fi = (ai * lam_re - nr * lam_im) / den
    bbr = fr[:, None, :] * b_re - fi[:, None, :] * b_im
    bbi = fr[:, None, :] * b_im + fi[:, None, :] * b_re
    return ar, ai, bbr, bbi


def _prepare(by_rows, block_cols, ar, ai):
    n = len(by_rows)

    def body(*refs):
        srcs, (ar_ref, ai_ref), dense, (con_ref, rev_ref) = refs[:n], refs[n:n + 2], refs[n + 2:2 * n + 2], refs[2 * n + 2:]
        for src, out, c in zip(srcs, dense, block_cols):
            r = src.shape[0]
            per = LANE // r
            wide = per * c
            own = (lax.broadcasted_iota(jnp.int32, (LANE, wide), 0) // r) == (lax.broadcasted_iota(jnp.int32, (LANE, wide), 1) // c)
            for j in range(out.shape[0]):
                tiled = jnp.broadcast_to(src[:, j * wide:(j + 1) * wide][None], (per, r, wide)).reshape(LANE, wide)
                out[j] = jnp.where(own, tiled, 0.0).astype(BF)
        a_r, a_i = ar_ref[...], ai_ref[...]
        pw = [(jnp.ones_like(a_r), jnp.zeros_like(a_i))]
        for _ in range(SUB):
            pr, pi = pw[-1]
            pw.append((pr * a_r - pi * a_i, pr * a_i + pi * a_r))
        row = _row_iota(GN)
        for ref, reverse in ((con_ref, False), (rev_ref, True)):
            sign = -1.0 if reverse else 1.0
            for j, sh in enumerate((1, 2, 4)):
                keep = (row < SUB - sh) if reverse else (row >= sh)
                ref[2 * j * SUB:(2 * j + 1) * SUB, :] = jnp.where(keep, pw[sh][0], 0.0)
                ref[(2 * j + 1) * SUB:(2 * j + 2) * SUB, :] = jnp.where(keep, sign * pw[sh][1], 0.0)
            p_r, p_i = jnp.zeros((SUB, GN), F32), jnp.zeros((SUB, GN), F32)
            for i in range(SUB):
                k = SUB - i if reverse else i + 1
                p_r = jnp.where(row == i, pw[k][0], p_r)
                p_i = jnp.where(row == i, sign * pw[k][1], p_i)
            ref[6 * SUB:7 * SUB, :] = p_r
            ref[7 * SUB:8 * SUB, :] = p_i

    dense_shapes = [(b.shape[1] // (LANE // b.shape[0] * c), LANE, LANE // b.shape[0] * c)
                    for b, c in zip(by_rows, block_cols)]
    outs = _pallas_call(
        body, name="prepare", grid=(1,), in_specs=[_full(b.shape) for b in by_rows] + [_full((1, GN))] * 2,
        out_specs=[_full(s) for s in dense_shapes] + [_full((8 * SUB, GN))] * 2,
        out_shape=[_far(s, BF) for s in dense_shapes] + [_sds((8 * SUB, GN)), _far((8 * SUB, GN))],
        compiler_params=_params(48),
    )(*by_rows, ar, ai)
    return outs[:n], outs[n], outs[n + 1]


def _local_step(x, p, tgt, w, comm):
    rows_of = lambda a: a.reshape(NCHIP * a.shape[1], a.shape[2])
    quarters = lambda a: a.reshape(NCHIP, a.shape[0] // NCHIP, a.shape[1])

    def gathering(names, call):
        carry = comm.gather(names)
        outs = list(call(carry))
        own = len(outs) - len(carry.out_shapes)
        w.update(zip(names, outs[own:]))
        return outs[:own]

    w.update(comm.first())
    ar, ai, bbr, bbi = _s5_discretize(w["lam_re"], w["lam_im"], w["log_dt"], w["s5_b_re"], w["s5_b_im"])
    by_row = lambda b: jnp.transpose(b, (1, 0, 2)).reshape(b.shape[1], -1)
    (bbr_d, bbi_d, ccr_d, cci_d, wr_d, wi_d), con, con_rev = _prepare(
        [by_row(b) for b in (bbr, bbi, w["s5_c_re"], w["s5_c_im"], w["w_r"], w["w_i"])], [NS] * 4 + [HD] * 2,
        ar.reshape(1, GN), ai.reshape(1, GN))
    dsk = w["s5_d"].reshape(1, S5W)
    lam = w["lru_lambda"].reshape(1, LW)
    sp = jax.nn.softplus(-lam)
    b_r, b_i = w["b_r"].reshape(1, LW), w["b_i"].reshape(1, LW)
    row = lambda name: w[name].reshape(1, -1)

    h, ua, ub, gp = gathering(["w_glu", "w_a_out", "w_b_out"], lambda carry: _inproj_fwd(
        x, row("g_mix"), w["w_in"], row("b_in"), carry))
    w_glu = rows_of(w["w_glu"])
    sr, si, y, zg, ya = gathering(["w_o", "w_ffn_gate"], lambda carry: _s5_fwd(
        ua, bbr_d, bbi_d, ccr_d, cci_d, dsk, con, w_glu, row("b_glu"), carry))
    xc, rg, ig, yb, hp = gathering(["w_ffn_up"], lambda carry: _lru_fwd(
        ub, w["conv_w"], row("conv_b"), wr_d, wi_d, b_r, b_i, sp, carry))
    w_b_out, w_o = rows_of(w["w_b_out"]), rows_of(w["w_o"])
    x1, pa, pb, merged = gathering(["w_ple_gate", "w_ple"], lambda carry: _merge_fwd(
        x, ya, yb, gp, w["w_a_out"], w_b_out, w_o, carry))
    h2, gg, uu = gathering(["w_ffn_down"], lambda carry: _ffn_up_fwd(
        x1, row("g_ffn"), w["w_ffn_gate"], w["w_ffn_up"], carry))
    x2 = _ffn_down_fwd(x1, gg, uu, w["w_ffn_down"])
    w_pg = rows_of(w["w_ple_gate"])
    dx2, n2, dpre, de0, acc_p = _ple_loss(x2, p, tgt, row("g_ple_gate"), w_pg, row("b_ple_gate"),
                                          w["w_ple"], row("g_ple"), row("g_final"))
    comm.reduce("ple", {"w_ple_gate": quarters(_tn("dw_ple_gate", n2, dpre)),
                        "w_ple": _tn("dw_ple", p, de0, col_chunk=AC)})
    dx1, act, dgg, duu, acc_f = comm.run(lambda carry: _ffn_bwd(
        x1, dx2, gg, uu, row("g_ffn"), w["w_ffn_gate"], w["w_ffn_up"], w["w_ffn_down"], carry))
    comm.reduce("ffn_gate", {"w_ffn_gate": _tn("dw_ffn_gate", dgg, h2)})
    comm.reduce("w_o", {"w_o": quarters(_tn("dw_o", *_in_hbm([merged, dx1])))})
    comm.reduce("ffn_up", {"w_ffn_up": comm.run(lambda carry: _tn("dw_ffn_up", duu, h2, carry=carry))[0]})
    comm.reduce("ffn_down", {"w_ffn_down": comm.run(lambda carry: _tn("dw_ffn_down", act, dx2, carry=carry),
                                                    hold=("ffn_gate", "w_o"))[0]})
    dya, dyb, dgp, dpa, dpb = comm.run(lambda carry: _merge_bwd(
        dx1, gp, pa, pb, w["w_a_out"], w_b_out, w_o, carry), hold=("ffn_gate", "ffn_up"))
    comm.reduce("merge", {"w_a_out": _tn("dw_a_out", ya, dpa, col_chunk=AC), "w_b_out": quarters(_tn("dw_b_out", yb, dpb))})
    dua, dq, dy, lr, li, acc_a, acc_s = comm.run(lambda carry: _s5_bwd(
        dya, y, ua, sr, si, bbr_d, bbi_d, ccr_d, cci_d, dsk, con_rev, w_glu, row("b_glu"), carry), hold=("ffn_down",))
    dub, dpr, dpi, acc_l = comm.run(lambda carry: _lru_bwd(
        dyb, xc, rg, ig, hp, ub, w["conv_w"], wr_d, wi_d, sp, -_sig(-lam), carry))
    gx, dz, acc_g, acc_b = _inproj_bwd(x, dx1, dua, dub, dgp, row("g_mix"), w["w_in"])
    half = (D // 2,)
    comm.reduce("in_lo", {"w_in_lo": comm.run(lambda carry: _tn(
        "dw_in_lo", h, dz, col_chunk=QC, a_block=(0,) + half, carry=carry))[0]})
    comm.reduce("in_hi", {"w_in_hi": comm.run(lambda carry: _tn(
        "dw_in_hi", h, dz, col_chunk=QC, a_block=(1,) + half, carry=carry))[0], "w_glu": quarters(_tn("dw_glu", zg, dq))})
    d_wr, d_wi = comm.run(lambda carry: _tn_blocks("dw_r_i", xc, [dpr, dpi], HD, HD, carry))
    d_bbr, d_bbi = comm.run(lambda carry: _tn_blocks("d_bb", ua, [lr, li], NP, NS, carry))
    d_ccr, d_cci = comm.run(lambda carry: _tn_blocks("d_cc", dy, [sr, si], NP, NS, carry))
    comm.drain()
    sums = {"ple": acc_p, "ffn": acc_f, "mix": acc_g, "b_in": acc_b, "lru": acc_l, "s5": acc_s, "s5_a": acc_a}
    blocks = {"bb_re": d_bbr, "bb_im": d_bbi,
              "cc_re": d_ccr, "cc_im": d_cci,
              "w_r": d_wr, "w_i": d_wi}
    return gx, sums, blocks


def _replicated_grads(w, sums, blocks):
    grouped = lambda e, groups: jnp.transpose(e.reshape(e.shape[0], groups, -1), (1, 0, 2))
    d_ar, d_ai = sums["s5_a"][0].reshape(NG, NS), sums["s5_a"][1].reshape(NG, NS)
    d_bbr, d_bbi = grouped(blocks["bb_re"], NG), grouped(blocks["bb_im"], NG)
    _, vjp = jax.vjp(_s5_discretize, w["lam_re"], w["lam_im"], w["log_dt"], w["s5_b_re"], w["s5_b_im"])
    g = dict(zip(("lam_re", "lam_im", "log_dt", "s5_b_re", "s5_b_im"), vjp((d_ar, d_ai, d_bbr, d_bbi))))
    g["s5_c_re"] = grouped(blocks["cc_re"], NG)
    g["s5_c_im"] = -grouped(blocks["cc_im"], NG)
    g["w_r"], g["w_i"] = grouped(blocks["w_r"], NH), grouped(blocks["w_i"], NH)
    g["s5_d"] = sums["s5"][0].reshape(NG, NP)
    g["b_r"] = sums["lru"][1].reshape(NH, HD)
    g["b_i"] = sums["lru"][2].reshape(NH, HD)
    return g


ACC_ROWS = {"g_mix": ("mix", 0), "b_in": ("b_in", 0), "g_ffn": ("ffn", 0), "g_ple_gate": ("ple", 0),
            "b_ple_gate": ("ple", 1), "g_ple": ("ple", 2), "g_final": ("ple", 3), "b_glu": ("s5", 1),
            "lru_lambda": ("lru", 0), "conv_b": ("lru", 3)}
LOSS_ROW = ("ple", 4)
CONV_W_ROWS = ("lru", 4)


SHARDED = [("w_in", (D, QC)), ("w_glu", (S5W // NCHIP, S5W)), ("w_a_out", (S5W, AC)), ("w_b_out", (LW // NCHIP, D)),
           ("w_o", (D // NCHIP, D)), ("w_ffn_gate", (FC, D)), ("w_ffn_up", (FC, D)), ("w_ffn_down", (FC, D)),
           ("w_ple_gate", (D // NCHIP, D)), ("w_ple", (PLE, AC))]
TRANSPOSED = ("w_ffn_gate", "w_ffn_up", "s5_b_re", "s5_b_im")
CONV_SHARD = (4, LW // NCHIP)


def _mesh_pos():
    return lax.axis_index("x"), lax.axis_index("y"), lax.axis_index("c")


def _other_chips(x, y):
    return [(1 - x, y), (x, 1 - y), (1 - x, 1 - y)]


def _half_rows(c, rows, align):
    return pl.ds(pl.multiple_of(c * (rows // 2), align), rows // 2)


def _run_now(name, carry):
    c_in, c_out = len(carry.operands), len(carry.out_shapes)

    def body(*refs):
        ins, outs, sems = refs[:c_in], refs[c_in:c_in + c_out], refs[c_in + c_out:]
        carry.start(ins, outs, sems)
        carry.finish(ins, outs, sems)

    return pl.pallas_call(body, name=name, in_specs=[ANY] * c_in, out_specs=[ANY] * c_out,
                          out_shape=list(carry.out_shapes), scratch_shapes=list(carry.sems),
                          input_output_aliases=dict(carry.aliases))(*_in_hbm(carry.operands))


def _gather_group(shards, split):
    n = len(shards)

    def copies(srcs, outs, sems):
        send_sems, recv_sems = sems
        x, y, c = _mesh_pos()
        k0 = 2 * x + y
        sib = (x, y, 1 - c)
        chips = _other_chips(x, y)

        def remote(src, dst, j, i, to):
            return pltpu.make_async_remote_copy(src_ref=src, dst_ref=dst, send_sem=send_sems.at[j, i],
                                                recv_sem=recv_sems.at[j, i], device_id=to, device_id_type=MESH)

        def rows(ref, i, core, *lead):
            if not split[i]:
                return ref.at[lead] if lead else ref
            return ref.at[(*lead, _half_rows(core, shards[i].shape[0], 16))]

        own = [remote(s, o.at[k0], 6, i, sib) for i, (s, o) in enumerate(zip(srcs, outs))]
        ici, landed, fwd, fwd_landed = [], [], [], []
        for j, chip in enumerate(chips):
            kj = 2 * chip[0] + chip[1]
            pairs = list(enumerate(zip(srcs, outs)))
            ici.append([remote(rows(s, i, c), rows(o, i, c, k0), j, i, (*chip, c)) for i, (s, o) in pairs])
            landed.append([remote(rows(s, i, c), rows(o, i, c, kj), j, i, (*chip, c)) for i, (s, o) in pairs])
            fwd.append([remote(rows(o, i, c, kj), rows(o, i, c, kj), 3 + j, i, sib) for i, (s, o) in pairs if split[i]])
            fwd_landed.append([remote(rows(o, i, 1 - c, kj), rows(o, i, 1 - c, kj), 3 + j, i, sib)
                               for i, (s, o) in pairs if split[i]])
        return own, ici, landed, fwd, fwd_landed

    def start(srcs, outs, sems):
        own, ici, _, _, _ = copies(srcs, outs, sems)
        for cp in own + [cp for per_chip in ici for cp in per_chip]:
            cp.start()

    def finish(srcs, outs, sems):
        own, ici, landed, fwd, fwd_landed = copies(srcs, outs, sems)
        passed = [i for i in range(n) if split[i]]
        for j in range(3):
            for i, cp in enumerate(landed[j]):
                cp.wait_recv()
                if split[i]:
                    fwd[j][passed.index(i)].start()
        for j in range(3):
            for cp in fwd_landed[j]:
                cp.wait_recv()
        for cp in own:
            cp.wait_recv()
        for cp in own + [cp for per_chip in ici + fwd for cp in per_chip]:
            cp.wait_send()

    return _Carried(shards, [_far((NCHIP,) + s.shape, s.dtype) for s in shards],
                    [pltpu.SemaphoreType.DMA((7, n)), pltpu.SemaphoreType.DMA((7, n))], start, finish)


def _to_bf16_group(name, arrays, carry):
    n = len(arrays)

    def body(*refs):
        for src, dst in zip(refs[:n], refs[n:]):
            dst[...] = src[...].astype(BF)

    specs = [pl.BlockSpec((a.shape[0] // 2, a.shape[1]), lambda i: (i, 0)) for a in arrays]
    return _pallas_call(body, carry, name=name, grid=(2,), in_specs=specs, out_specs=specs,
                        out_shape=[_far(a.shape, BF) for a in arrays], compiler_params=_params(48))(*arrays)


def _each_copy(copies, carried, out_shapes, sems, aliases=None):
    def start(ins, outs, sem_refs):
        for cp in copies(ins, outs, sem_refs):
            cp.start()

    def finish(ins, outs, sem_refs):
        for cp in copies(ins, outs, sem_refs):
            cp.wait()

    return _Carried(carried, out_shapes, sems, start, finish, aliases)


def _swap_group(grads):
    n = len(grads)

    def copies(srcs, outs, sems):
        send_sems, recv_sems = sems
        x, y, c = _mesh_pos()
        return [pltpu.make_async_remote_copy(src_ref=s.at[:, _half_rows(1 - c, s.shape[1], 8)], dst_ref=o,
                                             send_sem=send_sems.at[i], recv_sem=recv_sems.at[i], device_id=(x, y, 1 - c),
                                             device_id_type=MESH) for i, (s, o) in enumerate(zip(srcs, outs))]

    return _each_copy(copies, grads, [pltpu.HBM((NCHIP, g.shape[1] // 2, g.shape[2]), F32) for g in grads],
                      [pltpu.SemaphoreType.DMA((n,)), pltpu.SemaphoreType.DMA((n,))])


def _add_sibling_group(tag, kc_idx, grads, gots):
    n = len(grads)

    def body(kc_ref, *refs):
        for g, rx, p, pb in zip(refs[:n], refs[n:2 * n], refs[2 * n:3 * n], refs[3 * n:]):
            s = g[...] + rx[...]
            pb[...] = s.astype(BF)

            @pl.when(pl.program_id(0) == kc_ref[0])
            def _():
                p[...] = s

    halves = [pl.BlockSpec((None,) + rx.shape[1:], lambda k, kc_ref: (k, 0, 0)) for rx in gots]
    mine = [pl.BlockSpec((None,) + rx.shape[1:], lambda k, kc_ref: (k, kc_ref[1], 0)) for rx in gots]
    own = [pl.BlockSpec(rx.shape[1:], lambda k, kc_ref: (0, 0)) for rx in gots]
    outs = _pallas_call(
        body, name="add_sibling_" + tag,
        grid_spec=pltpu.PrefetchScalarGridSpec(num_scalar_prefetch=1, grid=(NCHIP,), in_specs=mine + halves,
                                               out_specs=own + halves),
        out_shape=[pltpu.HBM(rx.shape[1:], F32) for rx in gots] + [pltpu.HBM(rx.shape, BF) for rx in gots],
        compiler_params=_params(48),
    )(kc_idx, *_in_hbm(list(grads) + list(gots)))
    return outs[:n], outs[n:]


def _exchange_group(parts):
    n = len(parts)

    def copies(srcs, outs, sems):
        send_sems, recv_sems = sems
        x, y, c = _mesh_pos()
        return [pltpu.make_async_remote_copy(
            src_ref=s.at[2 * chip[0] + chip[1]], dst_ref=o.at[j], send_sem=send_sems.at[j, i],
            recv_sem=recv_sems.at[j, i], device_id=(*chip, c), device_id_type=MESH)
            for j, chip in enumerate(_other_chips(x, y)) for i, (s, o) in enumerate(zip(srcs, outs))]

    return _each_copy(copies, parts, [pltpu.HBM((3,) + p.shape[1:], BF) for p in parts],
                      [pltpu.SemaphoreType.DMA((3, n)), pltpu.SemaphoreType.DMA((3, n))])


def _add_chips_group(tag, kc_idx, parts, arrived):
    n = len(parts)

    def body(kc_ref, *refs):
        for p, rx, t in zip(refs[:n], refs[n:2 * n], refs[2 * n:]):
            t[...] = ((p[...] + rx[0].astype(F32)) + rx[1].astype(F32)) + rx[2].astype(F32)

    outs = _pallas_call(
        body, name="add_chips_" + tag,
        grid_spec=pltpu.PrefetchScalarGridSpec(
            num_scalar_prefetch=1, grid=(1,),
            in_specs=([pl.BlockSpec(rx.shape[1:], lambda i, kc_ref: (0, 0)) for rx in arrived]
                      + [pl.BlockSpec(rx.shape, lambda i, kc_ref: (0, 0, 0)) for rx in arrived]),
            out_specs=[pl.BlockSpec((None,) + rx.shape[1:], lambda i, kc_ref: (kc_ref[1], 0, 0)) for rx in arrived]),
        out_shape=[pltpu.HBM((2,) + rx.shape[1:], F32) for rx in arrived],
        compiler_params=_params(48),
    )(kc_idx, *_in_hbm(list(parts) + list(arrived)))
    return list(outs)


def _join_group(halves):
    n = len(halves)

    def copies(bufs, sems):
        send_sems, recv_sems = sems
        x, y, c = _mesh_pos()
        sib = (x, y, 1 - c)
        sends = [pltpu.make_async_remote_copy(src_ref=b.at[c], dst_ref=b.at[c], send_sem=send_sems.at[i],
                                              recv_sem=recv_sems.at[i], device_id=sib, device_id_type=MESH)
                 for i, b in enumerate(bufs)]
        landed = [pltpu.make_async_remote_copy(src_ref=b.at[c], dst_ref=b.at[1 - c], send_sem=send_sems.at[i],
                                               recv_sem=recv_sems.at[i], device_id=sib, device_id_type=MESH)
                  for i, b in enumerate(bufs)]
        return sends, landed

    def start(_, bufs, sems):
        for cp in copies(bufs, sems)[0]:
            cp.start()

    def finish(_, bufs, sems):
        sends, landed = copies(bufs, sems)
        for cp in landed:
            cp.wait_recv()
        for cp in sends:
            cp.wait_send()

    return _Carried(halves, [pltpu.HBM(h.shape, F32) for h in halves],
                    [pltpu.SemaphoreType.DMA((n,)), pltpu.SemaphoreType.DMA((n,))], start, finish,
                    {i: i for i in range(n)})


def _combine(carries):
    operands, out_shapes, sems, aliases, spans = [], [], [], {}, []
    for c in carries:
        aliases.update({len(operands) + i: len(out_shapes) + o for i, o in c.aliases.items()})
        spans.append((len(operands), len(out_shapes), len(sems)))
        operands += list(c.operands)
        out_shapes += list(c.out_shapes)
        sems += list(c.sems)

    def each(phase):
        def run(ins, outs, sem_refs):
            for c, (a, b, s) in zip(carries, spans):
                getattr(c, phase)(ins[a:a + len(c.operands)], outs[b:b + len(c.out_shapes)], sem_refs[s:s + len(c.sems)])
        return run

    return _Carried(operands, out_shapes, sems, each("start"), each("finish"), aliases)


def _allreduce_small(arrays, wire):
    n = len(arrays)
    halves = [(a.shape[0], a.shape[1] // 2) for a in arrays]

    def body(*refs):
        srcs, outs = refs[:n], refs[n:2 * n]
        mine_bufs, sib_bufs, chip_bufs, total_bufs = (refs[k * n:(k + 1) * n] for k in range(2, 6))
        send_sems, recv_sems, local_sems = refs[6 * n:]
        x, y, c = _mesh_pos()
        k0 = 2 * x + y
        sib = (x, y, 1 - c)

        def remote(src, dst, j, i, to):
            return pltpu.make_async_remote_copy(src_ref=src, dst_ref=dst, send_sem=send_sems.at[j, i],
                                                recv_sem=recv_sems.at[j, i], device_id=to, device_id_type=MESH)

        def cols(ref, i, core):
            return ref.at[:, pl.ds(pl.multiple_of(core * halves[i][1], LANE), halves[i][1])]

        swaps = [remote(cols(s, i, 1 - c), b, 0, i, sib) for i, (s, b) in enumerate(zip(srcs, sib_bufs))]
        own = [pltpu.make_async_copy(cols(s, i, c), m, local_sems.at[i]) for i, (s, m) in enumerate(zip(srcs, mine_bufs))]
        for cp in swaps + own:
            cp.start()
        for cp in swaps + own:
            cp.wait()
        for m, b, buf in zip(mine_bufs, sib_bufs, chip_bufs):
            buf[k0] = (m[...] + b[...]).astype(buf.dtype)
        chips = _other_chips(x, y)
        sends = [remote(buf.at[k0], buf.at[k0], 1 + j, i, (*chip, c))
                 for j, chip in enumerate(chips) for i, buf in enumerate(chip_bufs)]
        for cp in sends:
            cp.start()
        for j, chip in enumerate(chips):
            for i, buf in enumerate(chip_bufs):
                remote(buf.at[k0], buf.at[2 * chip[0] + chip[1]], 1 + j, i, (*chip, c)).wait_recv()
        for cp in sends:
            cp.wait_send()
        for t, buf in zip(total_bufs, chip_bufs):
            t[...] = ((buf[0].astype(F32) + buf[1].astype(F32)) + buf[2].astype(F32)) + buf[3].astype(F32)
        joins = [remote(t, cols(o, i, c), 4, i, sib) for i, (t, o) in enumerate(zip(total_bufs, outs))]
        keep = [pltpu.make_async_copy(t, cols(o, i, c), local_sems.at[i]) for i, (t, o) in enumerate(zip(total_bufs, outs))]
        for cp in joins + keep:
            cp.start()
        for i, (t, o) in enumerate(zip(total_bufs, outs)):
            remote(t, cols(o, i, 1 - c), 4, i, sib).wait_recv()
        for cp in joins:
            cp.wait_send()
        for cp in keep:
            cp.wait()

    specs = [_full(a.shape) for a in arrays]
    return _pallas_call(
        body, name="allreduce_small", grid=(1,), in_specs=specs, out_specs=specs,
        out_shape=[_sds(a.shape) for a in arrays],
        scratch_shapes=([pltpu.VMEM(h, F32) for h in halves] + [pltpu.VMEM(h, F32) for h in halves]
                        + [pltpu.VMEM((NCHIP,) + h, dt) for h, dt in zip(halves, wire)] + [pltpu.VMEM(h, F32) for h in halves]
                        + [pltpu.SemaphoreType.DMA((5, n)), pltpu.SemaphoreType.DMA((5, n)), pltpu.SemaphoreType.DMA((n,))]),
        compiler_params=_params(32),
    )(*arrays)


def _adamw_terms(w, g, m, v):
    m = ADAM_B1 * m + (1.0 - ADAM_B1) * g
    v = ADAM_B2 * v + (1.0 - ADAM_B2) * jnp.square(g)
    m_hat = m / (1.0 - ADAM_B1 ** ADAM_STEP)
    v_hat = v / (1.0 - ADAM_B2 ** ADAM_STEP)
    return -ADAM_LR * (m_hat / (jnp.sqrt(v_hat) + ADAM_EPS) + ADAM_WD * w), m, v


ADAM_STEPS = 4


def _adamw_group(tag, ws, gs, ms, vs):
    n = len(ws)

    def body(*refs):
        ins, outs = refs[:4 * n], refs[4 * n:]
        for i in range(n):
            w, g, m, v = (ins[k * n + i][...] for k in range(4))
            outs[i][...] = g
            outs[n + i][...], outs[2 * n + i][...], outs[3 * n + i][...] = _adamw_terms(w, g, m, v)

    specs = [pl.BlockSpec((w.shape[0] // ADAM_STEPS, w.shape[1]), lambda i: (i, 0)) for w in ws]
    outs = _pallas_call(
        body, name="adamw_" + tag, grid=(ADAM_STEPS,), in_specs=specs * 4, out_specs=specs * 4,
        out_shape=[_sds(w.shape) for w in ws] * 4, compiler_params=_params(48),
    )(*_in_hbm(list(ws) + list(gs) + list(ms) + list(vs)))
    return outs[:n], outs[n:2 * n], outs[2 * n:3 * n], outs[3 * n:]


def _adamw_replicated(sums, row_of, direct):
    ns, nr, nd = len(sums), len(row_of), len(direct)

    def body(*refs):
        sum_refs = refs[:ns]
        ins = refs[ns:ns + 3 * nr + 4 * nd]
        outs = refs[ns + 3 * nr + 4 * nd:]
        for i, (_, _, _, si, row) in enumerate(row_of):
            w_ref, m_ref, v_ref = ins[3 * i:3 * i + 3]
            g = sum_refs[si][row:row + 1, :]
            outs[4 * i][...] = g
            outs[4 * i + 1][...], outs[4 * i + 2][...], outs[4 * i + 3][...] = _adamw_terms(w_ref[...], g, m_ref[...], v_ref[...])
        for i in range(nd):
            w_ref, m_ref, v_ref, g_ref = ins[3 * nr + 4 * i:3 * nr + 4 * i + 4]
            o = outs[4 * (nr + i):4 * (nr + i) + 4]
            g = g_ref[...]
            o[0][...] = g
            o[1][...], o[2][...], o[3][...] = _adamw_terms(w_ref[...], g, m_ref[...], v_ref[...])

    operands = list(sums)
    shapes = []
    for w, m, v, _, _ in row_of:
        operands += [w, m, v]
        shapes += [w.shape] * 4
    for w, m, v, g in direct:
        operands += [w, m, v, g]
        shapes += [w.shape] * 4
    flat = _pallas_call(
        body, name="adamw_replicated", grid=(1,), in_specs=[_full(a.shape) for a in operands],
        out_specs=[_full(s) for s in shapes], out_shape=[_sds(s) for s in shapes],
        compiler_params=_params(56),
    )(*operands)
    return [flat[4 * i:4 * i + 4] for i in range(nr + nd)]


class _Exchanges:
    def __init__(self, shards, conv_w, chip, core, apply):
        self.shards, self.conv_w, self.apply = shards, conv_w, apply
        self.active, self.calls = [], 0
        self.chip_core_idx = jnp.stack([chip, core]).astype(jnp.int32)

    def first(self):
        later = [n for n in self.shards if n != "w_in"]
        carry = _gather_group([self.shards["w_in"].astype(BF), self.conv_w], [True, False])
        outs = _to_bf16_group("gather_first", [self.shards[n] for n in later], carry)
        self.shards = dict(zip(later, outs))
        return {"w_in": outs[len(later)], "conv_w": jnp.transpose(outs[len(later) + 1], (1, 0, 2)).reshape(4, LW)}

    def gather(self, names):
        return _gather_group([self.shards[n] for n in names], [True] * len(names))

    def reduce(self, tag, grads):
        self.active.append({"tag": tag, "names": list(grads), "stage": 0, "grads": list(grads.values())})

    def run(self, call, hold=()):
        groups = [g for g in self.active if g["tag"] not in hold]
        carries = [self._exchange_of(g) for g in groups]
        carry = _combine(carries)
        outs = list(call(carry))
        own = len(outs) - len(carry.out_shapes)
        landed = outs[own:]
        for g, c in zip(groups, carries):
            self._sum_after(g, landed[:len(c.out_shapes)])
            landed = landed[len(c.out_shapes):]
        self.active = [g for g in self.active if g["stage"] < 3]
        return outs[:own]

    def _exchange_of(self, g):
        if g["stage"] == 0:
            return _swap_group(g["grads"])
        if g["stage"] == 1:
            return _exchange_group(g["bf16"])
        return _join_group(g["halves"])

    def _sum_after(self, g, landed):
        if g["stage"] == 0:
            g["f32"], g["bf16"] = _add_sibling_group(g["tag"], self.chip_core_idx, g["grads"], landed)
        elif g["stage"] == 1:
            g["halves"] = _add_chips_group(g["tag"], self.chip_core_idx, g["f32"], landed)
        else:
            self.apply(g["tag"], g["names"], [t.reshape(2 * t.shape[1], t.shape[2]) for t in landed])
        g["stage"] += 1

    def drain(self):
        while self.active:
            self.calls += 1
            self.run(lambda carry: _run_now("reduce_%d" % self.calls, carry))


INPUT_NAMES = (["x", "p"] + [n for n in
               ["g_mix", "w_in", "b_in", "lam_re", "lam_im", "log_dt", "s5_b_re", "s5_b_im", "s5_c_re", "s5_c_im", "s5_d",
                "w_glu", "b_glu", "conv_w", "conv_b", "w_r", "b_r", "w_i", "b_i", "lru_lambda", "w_a_out", "w_b_out", "w_o",
                "g_ffn", "w_ffn_gate", "w_ffn_up", "w_ffn_down", "g_ple_gate", "w_ple_gate", "b_ple_gate", "w_ple", "g_ple",
                "g_final"]])
WEIGHT_NAMES = INPUT_NAMES[2:]


def kernel(*args):
    names = INPUT_NAMES + ["loss_target"] + ["m_" + n for n in WEIGHT_NAMES] + ["v_" + n for n in WEIGHT_NAMES]
    assert len(args) == len(names)
    given = dict(zip(names, args))

    def view(name):
        a = given[name]
        return jnp.swapaxes(a, -1, -2) if name.endswith(TRANSPOSED) else a

    def unview(name, a):
        return jnp.swapaxes(a, -1, -2) if name in TRANSPOSED else a

    def local(name):
        return view(name) if name.endswith("g_final") else view(name)[0]

    xi, yi, ci = _mesh_pos()
    k0 = 2 * xi + yi
    x, p, tgt = given["x"][0], given["p"][0, 0], given["loss_target"][0]

    results = {}

    row_halves = {}

    def apply(tag, names, totals):
        totals = dict(zip(names, totals))
        row_halves.update({n: totals.pop(n) for n in names if n in ("w_in_lo", "w_in_hi")})
        if len(row_halves) == 2:
            totals["w_in"] = jnp.concatenate([row_halves.pop("w_in_lo"), row_halves.pop("w_in_hi")])
        names = list(totals)
        if not names:
            return
        new = _adamw_group(tag, [local(n) for n in names], list(totals.values()), [local("m_" + n) for n in names],
                           [local("v_" + n) for n in names])
        for kind, arrays in zip(("grad", "delta", "new_m", "new_v"), new):
            for n, arr in zip(names, arrays):
                results[kind, n] = unview(n, arr[None])

    comm = _Exchanges({n: local(n) for n, _ in SHARDED}, local("conv_w"), k0, ci, apply)
    w = {n: local(n) for n in WEIGHT_NAMES if n != "conv_w" and n not in dict(SHARDED)}
    gx, sums, blocks = _local_step(x, p, tgt, w, comm)

    sum_names, block_names = list(sums), list(blocks)
    red = _allreduce_small([sums[n] for n in sum_names] + [blocks[n] for n in block_names],
                           [F32] * len(sum_names) + [BF] * len(block_names))
    sums = dict(zip(sum_names, red[:len(sum_names)]))
    blocks = dict(zip(block_names, red[len(sum_names):]))
    loss = jnp.sum(sums[LOSS_ROW[0]][LOSS_ROW[1]])
    direct_g = _replicated_grads(w, sums, blocks)
    conv_rows = sums[CONV_W_ROWS[0]][CONV_W_ROWS[1]:CONV_W_ROWS[1] + 4]
    direct_g["conv_w"] = lax.dynamic_slice(conv_rows, (0, k0 * CONV_SHARD[1]), CONV_SHARD)
    as_row = lambda a: a.reshape(1, -1)
    row_names = list(ACC_ROWS)
    row_of = [(as_row(given[n]), as_row(given["m_" + n]), as_row(given["v_" + n]),
               sum_names.index(ACC_ROWS[n][0]), ACC_ROWS[n][1]) for n in row_names]
    direct_names = list(direct_g)
    direct = [(view(n), view("m_" + n), view("v_" + n), direct_g[n].reshape(view(n).shape)) for n in direct_names]
    done = _adamw_replicated([sums[n] for n in sum_names], row_of, direct)
    for n, four in zip(row_names + direct_names, done):
        for kind, arr in zip(("grad", "delta", "new_m", "new_v"), four):
            results[kind, n] = unview(n, arr).reshape(given[n].shape)

    out = [loss, gx[None]]
    for kind in ("grad", "delta", "new_m", "new_v"):
        out += [results[kind, n] for n in WEIGHT_NAMES]
    return tuple(out)
```
